```python
import math
import jax, jax.numpy as jnp
from jax import lax
import numpy as np

D_MODEL = 1024
BATCH = 8
SEQ = 2048
DEPTH = 1

FOX_HEADS = 8
FOX_HEAD_DIM = 64
FOX_WIDTH = FOX_HEADS * FOX_HEAD_DIM
MLA_HEADS = 8
MLA_NOPE_DIM = 64
MLA_ROPE_DIM = 32
MLA_V_DIM = 64
MLA_Q_LORA = 768
MLA_KV_LORA = 256
MLA_WIDTH = MLA_HEADS * MLA_V_DIM
ROPE_THETA = 10000.0
D_FF = ((8 * D_MODEL + 3 * 256 - 1) // (3 * 256)) * 256
Q_BLOCK = 128
NORM_EPS = 1e-6

IN_WIDTHS = (
    FOX_WIDTH,
    FOX_WIDTH,
    FOX_WIDTH,
    FOX_HEADS,
    MLA_Q_LORA,
    MLA_KV_LORA,
    MLA_ROPE_DIM,
    D_MODEL,
    D_MODEL,
)
D_IN = sum(IN_WIDTHS)

kernel_name = "hybrid_fox_mla_sandwich_adaln_block"


def rmsnorm(x, g):
    xf = x.astype(jnp.float32)
    y = xf * lax.rsqrt(jnp.mean(xf * xf, axis=-1, keepdims=True) + NORM_EPS)
    return (y * g.astype(jnp.float32)).astype(x.dtype)


def rope(x, cos, sin):
    x1, x2 = jnp.split(x, 2, axis=-1)
    return jnp.concatenate([x1 * cos - x2 * sin, x2 * cos + x1 * sin], axis=-1).astype(x.dtype)


def blocked_causal_attention(q, k, v, scale, log_decay=None):
    B, H, S, dk = q.shape
    nb = S // Q_BLOCK
    q_blocks = q.reshape(B, H, nb, Q_BLOCK, dk).transpose(2, 0, 1, 3, 4)
    d_blocks = None if log_decay is None else log_decay.reshape(B, H, nb, Q_BLOCK).transpose(2, 0, 1, 3)
    key_pos = jnp.arange(S)

    def one_block(args):
        blk, q_blk, d_blk = args
        s = jnp.einsum("bhqd,bhkd->bhqk", q_blk, k, preferred_element_type=jnp.float32) * scale
        if d_blk is not None:
            s = s + (d_blk[..., :, None] - log_decay[:, :, None, :])
        query_pos = blk * Q_BLOCK + jnp.arange(Q_BLOCK)
        s = jnp.where(key_pos[None, :] <= query_pos[:, None], s, -jnp.inf)
        p = jax.nn.softmax(s, axis=-1)
        return jnp.einsum("bhqk,bhkd->bhqd", p.astype(v.dtype), v)

    out = lax.map(one_block, (jnp.arange(nb), q_blocks, d_blocks))
    return out.transpose(1, 2, 0, 3, 4).reshape(B, H, S, v.shape[-1])


def _fwd_setup_inputs(seed: int = 0) -> dict:
    key = jax.random.key(seed)
    ks = jax.random.split(key, 24)
    f32 = jnp.float32

    def normal(k, shape, fan_in):
        return jax.random.normal(k, shape, f32) * (fan_in ** -0.5)

    def gain(k, shape):
        return 1.0 + 0.05 * jax.random.normal(k, shape, f32)

    x = jax.random.normal(ks[0], (BATCH, SEQ, D_MODEL), f32)
    c = jax.random.normal(ks[1], (BATCH, D_MODEL), f32)
    offsets = jax.random.randint(ks[2], (BATCH, 1), 0, 1024, dtype=jnp.int32)
    positions = (offsets + jnp.arange(SEQ, dtype=jnp.int32)[None, :]).astype(jnp.int32)

    return {
        "x": x,
        "c": c,
        "positions": positions,
        "w_ada": normal(ks[3], (DEPTH, D_MODEL, 6 * D_MODEL), D_MODEL),
        "b_ada": 0.02 * jax.random.normal(ks[4], (DEPTH, 6 * D_MODEL), f32),
        "g_pre_mix": gain(ks[5], (DEPTH, D_MODEL)),
        "g_post_mix": gain(ks[6], (DEPTH, D_MODEL)),
        "g_pre_ffn": gain(ks[7], (DEPTH, D_MODEL)),
        "g_post_ffn": gain(ks[8], (DEPTH, D_MODEL)),
        "w_in": normal(ks[9], (DEPTH, D_MODEL, D_IN), D_MODEL),
        "b_forget": 3.0 + 0.5 * jax.random.normal(ks[10], (DEPTH, FOX_HEADS), f32),
        "g_q_lora": gain(ks[11], (DEPTH, MLA_Q_LORA)),
        "w_uq": normal(ks[12], (DEPTH, MLA_Q_LORA, MLA_HEADS * (MLA_NOPE_DIM + MLA_ROPE_DIM)), MLA_Q_LORA),
        "g_kv_lora": gain(ks[13], (DEPTH, MLA_KV_LORA)),
        "w_ukv": normal(ks[14], (DEPTH, MLA_KV_LORA, MLA_HEADS * (MLA_NOPE_DIM + MLA_V_DIM)), MLA_KV_LORA),
        "w_proj_fox": normal(ks[15], (DEPTH, FOX_WIDTH, D_MODEL), FOX_WIDTH),
        "w_proj_mla": normal(ks[16], (DEPTH, MLA_WIDTH, D_MODEL), MLA_WIDTH),
        "w_out": normal(ks[17], (DEPTH, D_MODEL, D_MODEL), D_MODEL),
        "w_ffn_in": normal(ks[18], (DEPTH, D_MODEL, 2 * D_FF), D_MODEL),
        "w_ffn_out": normal(ks[19], (DEPTH, D_FF, D_MODEL), D_FF),
    }


def _fwd_reference(x, c, positions, w_ada, b_ada, g_pre_mix, g_post_mix, g_pre_ffn, g_post_ffn,
              w_in, b_forget, g_q_lora, w_uq, g_kv_lora, w_ukv, w_proj_fox, w_proj_mla,
              w_out, w_ffn_in, w_ffn_out):
    B, S, D = x.shape
    inv_freq = 1.0 / (ROPE_THETA ** (jnp.arange(0, MLA_ROPE_DIM, 2, dtype=jnp.float32) / MLA_ROPE_DIM))
    angles = positions.astype(jnp.float32)[..., None] * inv_freq
    cos, sin = jnp.cos(angles), jnp.sin(angles)
    split_points = [int(v) for v in np.cumsum(IN_WIDTHS)[:-1]]
    silu_c = jax.nn.silu(c)

    for l in range(DEPTH):
        mod = (silu_c @ w_ada[l] + b_ada[l])[:, None, :]
        shift_mix, scale_mix, gate_mix, shift_ffn, scale_ffn, gate_ffn = jnp.split(mod, 6, axis=-1)

        h = rmsnorm(x, g_pre_mix[l]) * (1.0 + scale_mix) + shift_mix
        proj = h @ w_in[l]
        (fq, fk, fv, f_logit, cq, ckv, k_rope_in, gate_fox, gate_mla) = jnp.split(proj, split_points, axis=-1)

        q_a = fq.reshape(B, S, FOX_HEADS, FOX_HEAD_DIM).transpose(0, 2, 1, 3)
        k_a = fk.reshape(B, S, FOX_HEADS, FOX_HEAD_DIM).transpose(0, 2, 1, 3)
        v_a = fv.reshape(B, S, FOX_HEADS, FOX_HEAD_DIM).transpose(0, 2, 1, 3)
        log_f = jax.nn.log_sigmoid((f_logit + b_forget[l]).astype(jnp.float32))
        cum_log_f = jnp.cumsum(log_f, axis=1).transpose(0, 2, 1)
        o_a = blocked_causal_attention(q_a, k_a, v_a, 1.0 / math.sqrt(FOX_HEAD_DIM), cum_log_f)
        o_a = o_a.transpose(0, 2, 1, 3).reshape(B, S, FOX_WIDTH)

        q_b = (rmsnorm(cq, g_q_lora[l]) @ w_uq[l]).reshape(B, S, MLA_HEADS, MLA_NOPE_DIM + MLA_ROPE_DIM)
        q_nope, q_pe = jnp.split(q_b, [MLA_NOPE_DIM], axis=-1)
        q_pe = rope(q_pe, cos[:, :, None, :], sin[:, :, None, :])
        q_b = jnp.concatenate([q_nope, q_pe], axis=-1).transpose(0, 2, 1, 3)
        kv_b = (rmsnorm(ckv, g_kv_lora[l]) @ w_ukv[l]).reshape(B, S, MLA_HEADS, MLA_NOPE_DIM + MLA_V_DIM)
        k_nope, v_b = jnp.split(kv_b, [MLA_NOPE_DIM], axis=-1)
        k_pe = rope(k_rope_in, cos, sin)
        k_pe = jnp.broadcast_to(k_pe[:, :, None, :], (B, S, MLA_HEADS, MLA_ROPE_DIM))
        k_b = jnp.concatenate([k_nope, k_pe], axis=-1).transpose(0, 2, 1, 3)
        v_b = v_b.transpose(0, 2, 1, 3)
        o_b = blocked_causal_attention(q_b, k_b, v_b, 1.0 / math.sqrt(MLA_NOPE_DIM + MLA_ROPE_DIM))
        o_b = o_b.transpose(0, 2, 1, 3).reshape(B, S, MLA_WIDTH)

        merged = (jax.nn.sigmoid(gate_fox) * (o_a @ w_proj_fox[l])
                  + jax.nn.sigmoid(gate_mla) * (o_b @ w_proj_mla[l]))
        y = merged @ w_out[l]
        x = x + gate_mix * rmsnorm(y, g_post_mix[l])

        h = rmsnorm(x, g_pre_ffn[l]) * (1.0 + scale_ffn) + shift_ffn
        g, u = jnp.split(h @ w_ffn_in[l], 2, axis=-1)
        y = (jax.nn.silu(g) * u) @ w_ffn_out[l]
        x = x + gate_ffn * rmsnorm(y, g_post_ffn[l])

    return x


import jax as _jax
import jax.numpy as _jnp

TWIN_FORMAT = 'train_step'
FWD_PARAMS = ['x', 'c', 'positions', 'w_ada', 'b_ada', 'g_pre_mix', 'g_post_mix', 'g_pre_ffn', 'g_post_ffn', 'w_in', 'b_forget', 'g_q_lora', 'w_uq', 'g_kv_lora', 'w_ukv', 'w_proj_fox', 'w_proj_mla', 'w_out', 'w_ffn_in', 'w_ffn_out']
TWIN_WEIGHTS = ['w_ada', 'b_ada', 'g_pre_mix', 'g_post_mix', 'g_pre_ffn', 'g_post_ffn', 'w_in', 'b_forget', 'g_q_lora', 'w_uq', 'g_kv_lora', 'w_ukv', 'w_proj_fox', 'w_proj_mla', 'w_out', 'w_ffn_in', 'w_ffn_out']
TWIN_DIFF_INPUT = 'x'
TWIN_INPUTS = ['x', 'c', 'positions', 'w_ada', 'b_ada', 'g_pre_mix', 'g_post_mix', 'g_pre_ffn', 'g_post_ffn', 'w_in', 'b_forget', 'g_q_lora', 'w_uq', 'g_kv_lora', 'w_ukv', 'w_proj_fox', 'w_proj_mla', 'w_out', 'w_ffn_in', 'w_ffn_out', 'loss_target', 'm_w_ada', 'm_b_ada', 'm_g_pre_mix', 'm_g_post_mix', 'm_g_pre_ffn', 'm_g_post_ffn', 'm_w_in', 'm_b_forget', 'm_g_q_lora', 'm_w_uq', 'm_g_kv_lora', 'm_w_ukv', 'm_w_proj_fox', 'm_w_proj_mla', 'm_w_out', 'm_w_ffn_in', 'm_w_ffn_out', 'v_w_ada', 'v_b_ada', 'v_g_pre_mix', 'v_g_post_mix', 'v_g_pre_ffn', 'v_g_post_ffn', 'v_w_in', 'v_b_forget', 'v_g_q_lora', 'v_w_uq', 'v_g_kv_lora', 'v_w_ukv', 'v_w_proj_fox', 'v_w_proj_mla', 'v_w_out', 'v_w_ffn_in', 'v_w_ffn_out']
TWIN_OUTPUTS = ['loss', 'grad_x', 'grad_w_ada', 'grad_b_ada', 'grad_g_pre_mix', 'grad_g_post_mix', 'grad_g_pre_ffn', 'grad_g_post_ffn', 'grad_w_in', 'grad_b_forget', 'grad_g_q_lora', 'grad_w_uq', 'grad_g_kv_lora', 'grad_w_ukv', 'grad_w_proj_fox', 'grad_w_proj_mla', 'grad_w_out', 'grad_w_ffn_in', 'grad_w_ffn_out', 'delta_w_ada', 'delta_b_ada', 'delta_g_pre_mix', 'delta_g_post_mix', 'delta_g_pre_ffn', 'delta_g_post_ffn', 'delta_w_in', 'delta_b_forget', 'delta_g_q_lora', 'delta_w_uq', 'delta_g_kv_lora', 'delta_w_ukv', 'delta_w_proj_fox', 'delta_w_proj_mla', 'delta_w_out', 'delta_w_ffn_in', 'delta_w_ffn_out', 'new_m_w_ada', 'new_m_b_ada', 'new_m_g_pre_mix', 'new_m_g_post_mix', 'new_m_g_pre_ffn', 'new_m_g_post_ffn', 'new_m_w_in', 'new_m_b_forget', 'new_m_g_q_lora', 'new_m_w_uq', 'new_m_g_kv_lora', 'new_m_w_ukv', 'new_m_w_proj_fox', 'new_m_w_proj_mla', 'new_m_w_out', 'new_m_w_ffn_in', 'new_m_w_ffn_out', 'new_v_w_ada', 'new_v_b_ada', 'new_v_g_pre_mix', 'new_v_g_post_mix', 'new_v_g_pre_ffn', 'new_v_g_post_ffn', 'new_v_w_in', 'new_v_b_forget', 'new_v_g_q_lora', 'new_v_w_uq', 'new_v_g_kv_lora', 'new_v_w_ukv', 'new_v_w_proj_fox', 'new_v_w_proj_mla', 'new_v_w_out', 'new_v_w_ffn_in', 'new_v_w_ffn_out']
TWIN_LEAF_KINDS = {'loss': 'loss', 'grad_x': 'grad_x', 'grad_w_ada': 'grad_w', 'grad_b_ada': 'grad_w', 'grad_g_pre_mix': 'grad_w', 'grad_g_post_mix': 'grad_w', 'grad_g_pre_ffn': 'grad_w', 'grad_g_post_ffn': 'grad_w', 'grad_w_in': 'grad_w', 'grad_b_forget': 'grad_w', 'grad_g_q_lora': 'grad_w', 'grad_w_uq': 'grad_w', 'grad_g_kv_lora': 'grad_w', 'grad_w_ukv': 'grad_w', 'grad_w_proj_fox': 'grad_w', 'grad_w_proj_mla': 'grad_w', 'grad_w_out': 'grad_w', 'grad_w_ffn_in': 'grad_w', 'grad_w_ffn_out': 'grad_w', 'delta_w_ada': 'delta_w', 'delta_b_ada': 'delta_w', 'delta_g_pre_mix': 'delta_w', 'delta_g_post_mix': 'delta_w', 'delta_g_pre_ffn': 'delta_w', 'delta_g_post_ffn': 'delta_w', 'delta_w_in': 'delta_w', 'delta_b_forget': 'delta_w', 'delta_g_q_lora': 'delta_w', 'delta_w_uq': 'delta_w', 'delta_g_kv_lora': 'delta_w', 'delta_w_ukv': 'delta_w', 'delta_w_proj_fox': 'delta_w', 'delta_w_proj_mla': 'delta_w', 'delta_w_out': 'delta_w', 'delta_w_ffn_in': 'delta_w', 'delta_w_ffn_out': 'delta_w', 'new_m_w_ada': 'new_m', 'new_m_b_ada': 'new_m', 'new_m_g_pre_mix': 'new_m', 'new_m_g_post_mix': 'new_m', 'new_m_g_pre_ffn': 'new_m', 'new_m_g_post_ffn': 'new_m', 'new_m_w_in': 'new_m', 'new_m_b_forget': 'new_m', 'new_m_g_q_lora': 'new_m', 'new_m_w_uq': 'new_m', 'new_m_g_kv_lora': 'new_m', 'new_m_w_ukv': 'new_m', 'new_m_w_proj_fox': 'new_m', 'new_m_w_proj_mla': 'new_m', 'new_m_w_out': 'new_m', 'new_m_w_ffn_in': 'new_m', 'new_m_w_ffn_out': 'new_m', 'new_v_w_ada': 'new_v', 'new_v_b_ada': 'new_v', 'new_v_g_pre_mix': 'new_v', 'new_v_g_post_mix': 'new_v', 'new_v_g_pre_ffn': 'new_v', 'new_v_g_post_ffn': 'new_v', 'new_v_w_in': 'new_v', 'new_v_b_forget': 'new_v', 'new_v_g_q_lora': 'new_v', 'new_v_w_uq': 'new_v', 'new_v_g_kv_lora': 'new_v', 'new_v_w_ukv': 'new_v', 'new_v_w_proj_fox': 'new_v', 'new_v_w_proj_mla': 'new_v', 'new_v_w_out': 'new_v', 'new_v_w_ffn_in': 'new_v', 'new_v_w_ffn_out': 'new_v'}


def _forward(args):
    return _fwd_reference(*[args[k] for k in FWD_PARAMS])


def _output_shape():
    out = _jax.eval_shape(lambda: _forward(_fwd_setup_inputs(0)))
    return out.shape, out.dtype

N_MICROBATCH = 1
ADAM_LR = 0.001
ADAM_B1 = 0.9
ADAM_B2 = 0.999
ADAM_EPS = 1e-08
ADAM_WD = 0.01
ADAM_STEP = 10
PER_EXAMPLE_BATCH_AXIS = {'x': 0, 'c': 0, 'positions': 0, 'loss_target': 0}
SHARED_INPUTS = []
_WEIGHT_DTYPES = {'w_ada': _jnp.float32, 'b_ada': _jnp.float32, 'g_pre_mix': _jnp.float32, 'g_post_mix': _jnp.float32, 'g_pre_ffn': _jnp.float32, 'g_post_ffn': _jnp.float32, 'w_in': _jnp.float32, 'b_forget': _jnp.float32, 'g_q_lora': _jnp.float32, 'w_uq': _jnp.float32, 'g_kv_lora': _jnp.float32, 'w_ukv': _jnp.float32, 'w_proj_fox': _jnp.float32, 'w_proj_mla': _jnp.float32, 'w_out': _jnp.float32, 'w_ffn_in': _jnp.float32, 'w_ffn_out': _jnp.float32}
MOMENT_SCALE = {'w_ada': 2.517589e+00, 'b_ada': 4.429803e+00, 'g_pre_mix': 3.159601e-01, 'g_post_mix': 8.528846e+00, 'g_pre_ffn': 7.625470e-01, 'g_post_ffn': 7.880382e+00, 'w_in': 1.265683e+00, 'b_forget': 1.666495e+00, 'g_q_lora': 8.143494e-02, 'w_uq': 7.640498e-02, 'g_kv_lora': 2.904901e+00, 'w_ukv': 1.530031e+00, 'w_proj_fox': 2.052565e+00, 'w_proj_mla': 1.518971e+00, 'w_out': 2.585037e+00, 'w_ffn_in': 5.646535e-01, 'w_ffn_out': 1.041201e+00}


def _to_microbatches(a, axis):
    t = _jnp.moveaxis(a, axis, 0)
    t = t.reshape((N_MICROBATCH, t.shape[0] // N_MICROBATCH) + t.shape[1:])
    return _jnp.moveaxis(t, 1, axis + 1)


def setup_inputs(seed: int = 0) -> dict:
    inp = _fwd_setup_inputs(seed)
    key = _jax.random.fold_in(_jax.random.key(seed), 7919)
    shape, _ = _output_shape()
    out = dict(inp)
    out["loss_target"] = _jax.random.normal(_jax.random.fold_in(key, 0), shape, _jnp.float32)
    for i, name in enumerate(TWIN_WEIGHTS):
        w = inp[name].astype(_jnp.float32)
        if MOMENT_SCALE is None:
            s = _jnp.sqrt(_jnp.mean(_jnp.square(w)) + 1e-30)
        else:
            s = MOMENT_SCALE[name]
        km, kv = _jax.random.split(_jax.random.fold_in(key, i + 1))
        out[name] = w
        out["m_" + name] = s * _jax.random.normal(km, w.shape, _jnp.float32)
        out["v_" + name] = (s * s) * _jax.random.uniform(kv, w.shape, _jnp.float32, 0.5, 1.5)
    if N_MICROBATCH > 1:
        for name, axis in PER_EXAMPLE_BATCH_AXIS.items():
            out[name] = _to_microbatches(out[name], axis)
    return {'x': out['x'], 'c': out['c'], 'positions': out['positions'], 'w_ada': out['w_ada'], 'b_ada': out['b_ada'], 'g_pre_mix': out['g_pre_mix'], 'g_post_mix': out['g_post_mix'], 'g_pre_ffn': out['g_pre_ffn'], 'g_post_ffn': out['g_post_ffn'], 'w_in': out['w_in'], 'b_forget': out['b_forget'], 'g_q_lora': out['g_q_lora'], 'w_uq': out['w_uq'], 'g_kv_lora': out['g_kv_lora'], 'w_ukv': out['w_ukv'], 'w_proj_fox': out['w_proj_fox'], 'w_proj_mla': out['w_proj_mla'], 'w_out': out['w_out'], 'w_ffn_in': out['w_ffn_in'], 'w_ffn_out': out['w_ffn_out'], 'loss_target': out['loss_target'], 'm_w_ada': out['m_w_ada'], 'm_b_ada': out['m_b_ada'], 'm_g_pre_mix': out['m_g_pre_mix'], 'm_g_post_mix': out['m_g_post_mix'], 'm_g_pre_ffn': out['m_g_pre_ffn'], 'm_g_post_ffn': out['m_g_post_ffn'], 'm_w_in': out['m_w_in'], 'm_b_forget': out['m_b_forget'], 'm_g_q_lora': out['m_g_q_lora'], 'm_w_uq': out['m_w_uq'], 'm_g_kv_lora': out['m_g_kv_lora'], 'm_w_ukv': out['m_w_ukv'], 'm_w_proj_fox': out['m_w_proj_fox'], 'm_w_proj_mla': out['m_w_proj_mla'], 'm_w_out': out['m_w_out'], 'm_w_ffn_in': out['m_w_ffn_in'], 'm_w_ffn_out': out['m_w_ffn_out'], 'v_w_ada': out['v_w_ada'], 'v_b_ada': out['v_b_ada'], 'v_g_pre_mix': out['v_g_pre_mix'], 'v_g_post_mix': out['v_g_post_mix'], 'v_g_pre_ffn': out['v_g_pre_ffn'], 'v_g_post_ffn': out['v_g_post_ffn'], 'v_w_in': out['v_w_in'], 'v_b_forget': out['v_b_forget'], 'v_g_q_lora': out['v_g_q_lora'], 'v_w_uq': out['v_w_uq'], 'v_g_kv_lora': out['v_g_kv_lora'], 'v_w_ukv': out['v_w_ukv'], 'v_w_proj_fox': out['v_w_proj_fox'], 'v_w_proj_mla': out['v_w_proj_mla'], 'v_w_out': out['v_w_out'], 'v_w_ffn_in': out['v_w_ffn_in'], 'v_w_ffn_out': out['v_w_ffn_out']}


def _loss(weights, diff, rest, loss_target):
    with _jax.named_scope("forward"):
        args = {**rest, TWIN_DIFF_INPUT: diff, **{k: w.astype(_WEIGHT_DTYPES[k]) for k, w in weights.items()}}
        y = _forward(args)
    with _jax.named_scope("loss_head"):
        err = _jnp.square(y.astype(_jnp.float32) - loss_target)
        return 0.5 * _jnp.sum(_jnp.mean(err, axis=-1)) if err.ndim else 0.5 * err


def _adamw(w, g, m, v):
    m = ADAM_B1 * m + (1.0 - ADAM_B1) * g
    v = ADAM_B2 * v + (1.0 - ADAM_B2) * _jnp.square(g)
    m_hat = m / (1.0 - ADAM_B1 ** ADAM_STEP)
    v_hat = v / (1.0 - ADAM_B2 ** ADAM_STEP)
    delta = -ADAM_LR * (m_hat / (_jnp.sqrt(v_hat) + ADAM_EPS) + ADAM_WD * w)
    return delta, m, v


def reference(x, c, positions, w_ada, b_ada, g_pre_mix, g_post_mix, g_pre_ffn, g_post_ffn, w_in, b_forget, g_q_lora, w_uq, g_kv_lora, w_ukv, w_proj_fox, w_proj_mla, w_out, w_ffn_in, w_ffn_out, loss_target, m_w_ada, m_b_ada, m_g_pre_mix, m_g_post_mix, m_g_pre_ffn, m_g_post_ffn, m_w_in, m_b_forget, m_g_q_lora, m_w_uq, m_g_kv_lora, m_w_ukv, m_w_proj_fox, m_w_proj_mla, m_w_out, m_w_ffn_in, m_w_ffn_out, v_w_ada, v_b_ada, v_g_pre_mix, v_g_post_mix, v_g_pre_ffn, v_g_post_ffn, v_w_in, v_b_forget, v_g_q_lora, v_w_uq, v_g_kv_lora, v_w_ukv, v_w_proj_fox, v_w_proj_mla, v_w_out, v_w_ffn_in, v_w_ffn_out):
    given = dict(x=x, c=c, positions=positions, w_ada=w_ada, b_ada=b_ada, g_pre_mix=g_pre_mix, g_post_mix=g_post_mix, g_pre_ffn=g_pre_ffn, g_post_ffn=g_post_ffn, w_in=w_in, b_forget=b_forget, g_q_lora=g_q_lora, w_uq=w_uq, g_kv_lora=g_kv_lora, w_ukv=w_ukv, w_proj_fox=w_proj_fox, w_proj_mla=w_proj_mla, w_out=w_out, w_ffn_in=w_ffn_in, w_ffn_out=w_ffn_out, loss_target=loss_target, m_w_ada=m_w_ada, m_b_ada=m_b_ada, m_g_pre_mix=m_g_pre_mix, m_g_post_mix=m_g_post_mix, m_g_pre_ffn=m_g_pre_ffn, m_g_post_ffn=m_g_post_ffn, m_w_in=m_w_in, m_b_forget=m_b_forget, m_g_q_lora=m_g_q_lora, m_w_uq=m_w_uq, m_g_kv_lora=m_g_kv_lora, m_w_ukv=m_w_ukv, m_w_proj_fox=m_w_proj_fox, m_w_proj_mla=m_w_proj_mla, m_w_out=m_w_out, m_w_ffn_in=m_w_ffn_in, m_w_ffn_out=m_w_ffn_out, v_w_ada=v_w_ada, v_b_ada=v_b_ada, v_g_pre_mix=v_g_pre_mix, v_g_post_mix=v_g_post_mix, v_g_pre_ffn=v_g_pre_ffn, v_g_post_ffn=v_g_post_ffn, v_w_in=v_w_in, v_b_forget=v_b_forget, v_g_q_lora=v_g_q_lora, v_w_uq=v_w_uq, v_g_kv_lora=v_g_kv_lora, v_w_ukv=v_w_ukv, v_w_proj_fox=v_w_proj_fox, v_w_proj_mla=v_w_proj_mla, v_w_out=v_w_out, v_w_ffn_in=v_w_ffn_in, v_w_ffn_out=v_w_ffn_out)
    weights = {n: given[n] for n in TWIN_WEIGHTS}
    shared = {n: given[n] for n in SHARED_INPUTS}
    per_example = {n: given[n] for n in ['x', 'c', 'positions']}
    grad_fn = _jax.value_and_grad(_loss, argnums=(0, 1))

    def one_microbatch(ex, loss_target):
        ex = dict(ex)
        diff = ex.pop(TWIN_DIFF_INPUT)
        return grad_fn(weights, diff, {**shared, **ex}, loss_target)

    if N_MICROBATCH == 1:
        loss, (grad_w, grad_x) = one_microbatch(per_example, given["loss_target"])
    else:
        def body(carry, xs):
            loss_sum, grad_sum = carry
            l_k, (gw_k, gx_k) = one_microbatch(xs[0], xs[1])
            with _jax.named_scope("update"):
                return (loss_sum + l_k, _jax.tree.map(_jnp.add, grad_sum, gw_k)), gx_k

        init = (_jnp.zeros((), _jnp.float32), _jax.tree.map(_jnp.zeros_like, weights))
        (loss, grad_w), grad_x = _jax.lax.scan(body, init, (per_example, given["loss_target"]))
    with _jax.named_scope("update"):
        delta_w, new_m, new_v = {}, {}, {}
        for n in TWIN_WEIGHTS:
            delta_w[n], new_m[n], new_v[n] = _adamw(weights[n], grad_w[n], given["m_" + n], given["v_" + n])
    return (loss, grad_x, *[grad_w[n] for n in TWIN_WEIGHTS], *[delta_w[n] for n in TWIN_WEIGHTS],
            *[new_m[n] for n in TWIN_WEIGHTS], *[new_v[n] for n in TWIN_WEIGHTS])
```

```python
import functools
import math

import jax
import jax.numpy as jnp
from jax import lax
from jax.experimental import pallas as pl
from jax.experimental.pallas import tpu as pltpu

F32 = jnp.float32
BF16 = jnp.bfloat16

N_DEV = 8
S = 2048
D = 1024
D_FF = 2816
HEADS = 8
HEAD_DIM = 64
Q_LORA = 768
KV_LORA = 256
ROPE_DIM = 32
ROPE_THETA = 10000.0
NORM_EPS = 1e-6
LANES = 128
VMEM_LIMIT = 56 * 1024 * 1024

ADAM_LR = 0.001
ADAM_B1 = 0.9
ADAM_B2 = 0.999
ADAM_EPS = 1e-08
ADAM_WD = 0.01
ADAM_STEP = 10

ATT_T = 256
N_ATT = S // ATT_T

NN = (((1,), (0,)), ((), ()))
NT = (((1,), (1,)), ((), ()))
TN = (((0,), (0,)), ((), ()))
MESH = pl.DeviceIdType.MESH


def _params(sem=None):
    return pltpu.CompilerParams(dimension_semantics=sem, vmem_limit_bytes=VMEM_LIMIT)


def _pick(n, cap):
    best = None
    for t in range(LANES, cap + 1, LANES):
        if n % t == 0:
            best = t
    return best if best is not None else n


def _mm(a, b, mode, out_dtype, name, acc=None):
    if mode == "nn":
        (m, k), (k2, n), dn = a.shape, b.shape, NN
    elif mode == "nt":
        (m, k), (n, k2), dn = a.shape, b.shape, NT
    else:
        (k, m), (k2, n), dn = a.shape, b.shape, TN
    assert k == k2, (a.shape, b.shape, mode)
    tn = _pick(n, 640)
    tm = _pick(m, 1024)
    osz = jnp.dtype(out_dtype).itemsize

    def need(tm_):
        blk = tm_ * k * 2 + tn * k * 2 + tm_ * tn * osz + (tm_ * tn * 4 if acc is not None else 0)
        return 2 * blk + tm_ * tn * 4
    while need(tm) > 36 * 1024 * 1024 and tm % 256 == 0:
        tm //= 2

    def body(*refs):
        if acc is not None:
            a_ref, b_ref, c_ref, o_ref = refs
        else:
            a_ref, b_ref, o_ref = refs
        r = lax.dot_general(a_ref[...], b_ref[...], dn, preferred_element_type=F32)
        if acc is not None:
            r = r + c_ref[...]
        o_ref[...] = r.astype(o_ref.dtype)

    if mode == "tn":
        a_spec = pl.BlockSpec((k, tm), lambda i, j: (0, i))
    else:
        a_spec = pl.BlockSpec((tm, k), lambda i, j: (i, 0))
    if mode == "nt":
        b_spec = pl.BlockSpec((tn, k), lambda i, j: (j, 0))
    else:
        b_spec = pl.BlockSpec((k, tn), lambda i, j: (0, j))
    o_spec = pl.BlockSpec((tm, tn), lambda i, j: (i, j))
    in_specs = [a_spec, b_spec] + ([o_spec] if acc is not None else [])
    args = (a, b) + ((acc,) if acc is not None else ())
    return pl.pallas_call(
        body, name=name, grid=(m // tm, n // tn),
        in_specs=in_specs, out_specs=o_spec,
        out_shape=jax.ShapeDtypeStruct((m, n), out_dtype),
        compiler_params=_params(("parallel", "parallel")),
    )(*args)


def _rowwise(fn, row_ins, vec_ins, row_outs, sum_outs, name, tm=256):
    n_in = len(row_ins) + len(vec_ins)
    n_o = len(row_outs)
    rows = row_ins[0][0].shape[0]

    def body(*refs):
        vals = [r[...] for r in refs[:n_in]]
        outs = refs[n_in:]
        ro, so = fn(*vals)
        assert len(ro) == n_o and len(so) == len(sum_outs)
        for r, v in zip(outs[:n_o], ro):
            r[...] = v.astype(r.dtype)
        if sum_outs:
            @pl.when(pl.program_id(0) == 0)
            def _():
                for r in outs[n_o:]:
                    r[...] = jnp.zeros(r.shape, F32)
            for r, v in zip(outs[n_o:], so):
                r[...] += v

    in_specs = [pl.BlockSpec((tm, w), functools.partial(lambda i, b: (i, b), b=b)) for _, w, b in row_ins]
    in_specs += [pl.BlockSpec(v.shape, lambda i: (0, 0)) for v in vec_ins]
    out_specs = [pl.BlockSpec((tm, w), lambda i: (i, 0)) for w, _ in row_outs]
    out_specs += [pl.BlockSpec((1, w), lambda i: (0, 0)) for w in sum_outs]
    out_shape = [jax.ShapeDtypeStruct((rows, w), dt) for w, dt in row_outs]
    out_shape += [jax.ShapeDtypeStruct((1, w), F32) for w in sum_outs]
    return pl.pallas_call(
        body, name=name, grid=(rows // tm,),
        in_specs=in_specs, out_specs=out_specs, out_shape=out_shape,
        compiler_params=_params(("arbitrary",)),
    )(*[a for a, _, _ in row_ins], *vec_ins)


def _sigmoid(x):
    return 1.0 / (1.0 + jnp.exp(-x))


def _rstd(x):
    return lax.rsqrt(jnp.mean(x * x, axis=-1, keepdims=True) + NORM_EPS)


def _norm_bwd(dyn, xn, r):
    return r * (dyn - xn * jnp.mean(dyn * xn, axis=-1, keepdims=True))


def _colsum(x):
    return jnp.sum(x, axis=0, keepdims=True)


def _rope_tables(pos, invf):
    def fn(p, f):
        lane = lax.broadcasted_iota(jnp.int32, (1, LANES), 1)
        ang = p * f
        cs, sn = jnp.cos(ang), jnp.sin(ang)
        rot = (lane >= 64) & (lane < 96)
        ct = jnp.where(lane < 64, 1.0, jnp.where(rot, cs, 0.0))
        sa = jnp.where((lane >= 64) & (lane < 80), -sn, 0.0)
        sb = jnp.where((lane >= 80) & (lane < 96), sn, 0.0)
        return (ct, sa, sb), ()
    return _rowwise(fn, [(pos, 1, 0)], [invf], [(LANES, F32)] * 3, [], "rope_tables")


def _rope(x, ct, sa, sb):
    return x * ct + pltpu.roll(x, LANES - 16, 1) * sa + pltpu.roll(x, 16, 1) * sb


def _rope_t(x, ct, sa, sb):
    return x * ct - pltpu.roll(x, LANES - 16, 1) * sa - pltpu.roll(x, 16, 1) * sb


def _head_mask(width, hh):
    lane = lax.broadcasted_iota(jnp.int32, (1, width), 1)
    half = width // 2
    return (lane >= hh * half) & (lane < (hh + 1) * half)


def _attn_fwd(q, qo, k, ko, v, vo, dkp, scale, bias, name):
    T = ATT_T

    def body(*refs):
        if bias is not None:
            q_ref, k_ref, v_ref, b_ref, o_ref, lse_ref = refs
        else:
            q_ref, k_ref, v_ref, o_ref, lse_ref = refs
        i = pl.program_id(1)
        qb = q_ref[...]
        row = lax.broadcasted_iota(jnp.int32, (T, T), 0)
        col = lax.broadcasted_iota(jnp.int32, (T, T), 1)
        outs = []
        for hh in range(2):
            qm = jnp.where(_head_mask(dkp, hh), qb, jnp.zeros_like(qb))

            def step(j, carry, masked):
                m, l, acc = carry
                ks = pl.ds(pl.multiple_of(j * T, T), T)
                s = lax.dot_general(qm, k_ref[ks, :], NT, preferred_element_type=F32) * scale
                if bias is not None:
                    s = s + b_ref[hh, j]
                if masked:
                    s = jnp.where(row >= col, s, -jnp.inf)
                m_new = jnp.maximum(m, jnp.max(s, axis=1, keepdims=True))
                alpha = jnp.exp(m - m_new)
                p = jnp.exp(s - m_new)
                l = alpha * l + jnp.sum(p, axis=1, keepdims=True)
                acc = alpha * acc + lax.dot_general(p.astype(BF16), v_ref[ks, :], NN,
                                                    preferred_element_type=F32)
                return m_new, l, acc

            init = (jnp.full((T, 1), -jnp.inf, F32), jnp.zeros((T, 1), F32), jnp.zeros((T, LANES), F32))
            carry = lax.fori_loop(0, i, functools.partial(step, masked=False), init)
            m, l, acc = step(i, carry, True)
            outs.append(acc / l)
            lse_ref[hh] = m + jnp.log(l)
        o_ref[...] = jnp.where(_head_mask(LANES, 0), outs[0], outs[1]).astype(o_ref.dtype)

    in_specs = [
        pl.BlockSpec((T, dkp), lambda p, i: (i, qo + p)),
        pl.BlockSpec((S, dkp), lambda p, i: (0, ko + p)),
        pl.BlockSpec((S, LANES), lambda p, i: (0, vo + p)),
    ]
    args = [q, k, v]
    if bias is not None:
        in_specs.append(pl.BlockSpec((2, N_ATT, 1, T), lambda p, i: (p, 0, 0, 0)))
        args.append(bias)
    return pl.pallas_call(
        body, name=name, grid=(HEADS // 2, N_ATT),
        in_specs=in_specs,
        out_specs=[pl.BlockSpec((T, LANES), lambda p, i: (i, p)),
                   pl.BlockSpec((2, T, 1), lambda p, i: (p, i, 0))],
        out_shape=[jax.ShapeDtypeStruct((S, HEADS * HEAD_DIM), BF16),
                   jax.ShapeDtypeStruct((HEADS, S, 1), F32)],
        compiler_params=_params(("parallel", "arbitrary")),
    )(*args)


def _attn_delta(q, qo, k, ko, v, vo, do, lse, dkp, scale, bias, name):
    T = ATT_T

    def body(*refs):
        q_ref, k_ref, v_ref, do_ref, lse_ref = refs[:5]
        b_ref = refs[5] if bias is not None else None
        out_ref = refs[-1]
        i = pl.program_id(1)
        qb = q_ref[...]
        dob = do_ref[...]
        row = lax.broadcasted_iota(jnp.int32, (T, T), 0)
        col = lax.broadcasted_iota(jnp.int32, (T, T), 1)
        for hh in range(2):
            qm = jnp.where(_head_mask(dkp, hh), qb, jnp.zeros_like(qb))
            dom = jnp.where(_head_mask(LANES, hh), dob, jnp.zeros_like(dob))
            lse = lse_ref[hh]

            def step(j, acc, masked):
                ks = pl.ds(pl.multiple_of(j * T, T), T)
                s = lax.dot_general(qm, k_ref[ks, :], NT, preferred_element_type=F32) * scale
                if bias is not None:
                    s = s + b_ref[hh, j]
                s = s - lse
                if masked:
                    s = jnp.where(row >= col, s, -jnp.inf)
                dp = lax.dot_general(dom, v_ref[ks, :], NT, preferred_element_type=F32)
                return acc + jnp.sum(jnp.exp(s) * dp, axis=1, keepdims=True)

            acc = lax.fori_loop(0, i, functools.partial(step, masked=False), jnp.zeros((T, 1), F32))
            out_ref[hh] = step(i, acc, True)

    in_specs = [
        pl.BlockSpec((T, dkp), lambda p, i: (i, qo + p)),
        pl.BlockSpec((S, dkp), lambda p, i: (0, ko + p)),
        pl.BlockSpec((S, LANES), lambda p, i: (0, vo + p)),
        pl.BlockSpec((T, LANES), lambda p, i: (i, p)),
        pl.BlockSpec((2, T, 1), lambda p, i: (p, i, 0)),
    ]
    args = [q, k, v, do, lse]
    if bias is not None:
        in_specs.append(pl.BlockSpec((2, N_ATT, 1, T), lambda p, i: (p, 0, 0, 0)))
        args.append(bias)
    return pl.pallas_call(
        body, name=name, grid=(HEADS // 2, N_ATT),
        in_specs=in_specs, out_specs=pl.BlockSpec((2, T, 1), lambda p, i: (p, i, 0)),
        out_shape=jax.ShapeDtypeStruct((HEADS, S, 1), F32),
        compiler_params=_params(("parallel", "arbitrary")),
    )(*args)


def _attn_bwd(q, qo, k, ko, v, vo, do, lse, delta_in, dkp, scale, bias, qk_dtype, name):
    T = ATT_T
    has_b = bias is not None

    def body(*refs):
        q_ref, k_ref, v_ref, do_ref, lse_ref, delta = refs[:6]
        refs = refs[6:]
        if has_b:
            b_ref, refs = refs[0], refs[1:]
        dq_ref, dk_ref, dv_ref = refs[:3]
        refs = refs[3:]
        if has_b:
            db_ref, refs = refs[0], refs[1:]
        dq_acc, dk_acc, dv_acc = refs
        j = pl.program_id(1)

        @pl.when(j == 0)
        def _():
            dq_acc[...] = jnp.zeros(dq_acc.shape, F32)

        dk_acc[...] = jnp.zeros(dk_acc.shape, F32)
        dv_acc[...] = jnp.zeros(dv_acc.shape, F32)
        kb = k_ref[...]
        vb = v_ref[...]
        row = lax.broadcasted_iota(jnp.int32, (T, T), 0)
        col = lax.broadcasted_iota(jnp.int32, (T, T), 1)
        for hh in range(2):
            mk = _head_mask(dkp, hh)
            mv = _head_mask(LANES, hh)
            km = jnp.where(mk, kb, jnp.zeros_like(kb))

            def step(i, db, masked):
                rs = pl.ds(pl.multiple_of(i * T, T), T)
                qb = q_ref[rs, :]
                qm = jnp.where(mk, qb, jnp.zeros_like(qb))
                dob = do_ref[rs, :]
                dom = jnp.where(mv, dob, jnp.zeros_like(dob))
                s = lax.dot_general(qm, kb, NT, preferred_element_type=F32) * scale
                if has_b:
                    s = s + b_ref[hh, 0]
                s = s - lse_ref[hh, rs, :]
                if masked:
                    s = jnp.where(row >= col, s, -jnp.inf)
                p = jnp.exp(s)
                dp = lax.dot_general(dom, vb, NT, preferred_element_type=F32)
                ds = p * (dp - delta[hh, rs, :])
                dv_acc[...] += lax.dot_general(p.astype(BF16), dom, TN, preferred_element_type=F32)
                dsb = (ds * scale).astype(BF16)
                dk_acc[...] += lax.dot_general(dsb, qm, TN, preferred_element_type=F32)
                dq_acc[rs, :] += lax.dot_general(dsb, km, NN, preferred_element_type=F32)
                if has_b:
                    db = db + jnp.sum(ds, axis=0, keepdims=True)
                return db

            db = step(j, jnp.zeros((1, T), F32), True)
            db = lax.fori_loop(j + 1, N_ATT, functools.partial(step, masked=False), db)
            if has_b:
                db_ref[hh, 0] = db
        dk_ref[...] = dk_acc[...].astype(dk_ref.dtype)
        dv_ref[...] = dv_acc[...].astype(dv_ref.dtype)

        @pl.when(j == N_ATT - 1)
        def _():
            dq_ref[...] = dq_acc[...].astype(dq_ref.dtype)

    in_specs = [
        pl.BlockSpec((S, dkp), lambda p, j: (0, qo + p)),
        pl.BlockSpec((T, dkp), lambda p, j: (j, ko + p)),
        pl.BlockSpec((T, LANES), lambda p, j: (j, vo + p)),
        pl.BlockSpec((S, LANES), lambda p, j: (0, p)),
        pl.BlockSpec((2, S, 1), lambda p, j: (p, 0, 0)),
        pl.BlockSpec((2, S, 1), lambda p, j: (p, 0, 0)),
    ]
    args = [q, k, v, do, lse, delta_in]
    out_specs = [
        pl.BlockSpec((S, dkp), lambda p, j: (0, p)),
        pl.BlockSpec((T, dkp), lambda p, j: (j, p)),
        pl.BlockSpec((T, LANES), lambda p, j: (j, p)),
    ]
    width = (HEADS // 2) * dkp
    out_shape = [
        jax.ShapeDtypeStruct((S, width), qk_dtype),
        jax.ShapeDtypeStruct((S, width), qk_dtype),
        jax.ShapeDtypeStruct((S, HEADS * HEAD_DIM), BF16),
    ]
    if has_b:
        in_specs.append(pl.BlockSpec((2, 1, 1, T), lambda p, j: (p, j, 0, 0)))
        args.append(bias)
        out_specs.append(pl.BlockSpec((2, 1, 1, T), lambda p, j: (p, j, 0, 0)))
        out_shape.append(jax.ShapeDtypeStruct((HEADS, N_ATT, 1, T), F32))
    return pl.pallas_call(
        body, name=name, grid=(HEADS // 2, N_ATT),
        in_specs=in_specs, out_specs=out_specs, out_shape=out_shape,
        scratch_shapes=[pltpu.VMEM((S, dkp), F32), pltpu.VMEM((T, dkp), F32), pltpu.VMEM((T, LANES), F32)],
        compiler_params=_params(("parallel", "arbitrary")),
    )(*args)


def _tri(upper):
    a = lax.broadcasted_iota(jnp.int32, (LANES, LANES), 0)
    b = lax.broadcasted_iota(jnp.int32, (LANES, LANES), 1)
    return jnp.where(a <= b if upper else a >= b, 1.0, 0.0).astype(F32)


def _fox_gates(zt, bf):
    def body(z_ref, b_ref, o_ref):
        tri = _tri(True)
        carry = jnp.zeros((HEADS, 1), F32)
        for t in range(S // LANES):
            sl = slice(t * LANES, (t + 1) * LANES)
            z = z_ref[:, sl] + b_ref[...]
            logf = jnp.minimum(z, 0.0) - jnp.log(1.0 + jnp.exp(-jnp.abs(z)))
            c = lax.dot_general(logf, tri, NN, preferred_element_type=F32,
                                precision=lax.Precision.HIGHEST) + carry
            o_ref[:, sl] = -c
            carry = c[:, LANES - 1:LANES]

    return pl.pallas_call(
        body, name="fox_gates", out_shape=jax.ShapeDtypeStruct((HEADS, S), F32),
        compiler_params=_params(),
    )(zt, bf)


def _fox_gates_bwd(dbias, zt, bf):
    def body(d_ref, z_ref, b_ref, dz_ref, dbf_ref):
        tri = _tri(False)
        carry = jnp.zeros((HEADS, 1), F32)
        tot = jnp.zeros((HEADS, 1), F32)
        for t in reversed(range(S // LANES)):
            sl = slice(t * LANES, (t + 1) * LANES)
            df = -d_ref[:, sl]
            c = lax.dot_general(df, tri, NN, preferred_element_type=F32,
                                precision=lax.Precision.HIGHEST) + carry
            carry = c[:, 0:1]
            z = z_ref[:, sl] + b_ref[...]
            dz = c * _sigmoid(-z)
            dz_ref[:, sl] = dz
            tot = tot + jnp.sum(dz, axis=1, keepdims=True)
        dbf_ref[...] = tot

    return pl.pallas_call(
        body, name="fox_gates_bwd",
        out_shape=[jax.ShapeDtypeStruct((HEADS, S), F32), jax.ShapeDtypeStruct((HEADS, 1), F32)],
        compiler_params=_params(),
    )(dbias, zt, bf)


def _mod_part(c_all, w_ada, b_cols):
    def body(c_ref, w_ref, b_ref, o_ref, s_ref):
        c = c_ref[...]
        sc = c * _sigmoid(c)
        s_ref[...] = sc
        o_ref[...] = lax.dot_general(sc, w_ref[...], NN, preferred_element_type=F32,
                                     precision=lax.Precision.HIGHEST) + b_ref[...]

    return pl.pallas_call(
        body, name="mod_part",
        out_shape=[jax.ShapeDtypeStruct((N_DEV, w_ada.shape[1]), F32), jax.ShapeDtypeStruct(c_all.shape, F32)],
        compiler_params=_params(),
    )(c_all, w_ada, b_cols)


def _w_ada_grad(sc_t, dm):
    def body(s_ref, d_ref, o_ref):
        acc = jnp.zeros(o_ref.shape, F32)
        for b in range(N_DEV):
            acc = acc + s_ref[:, b:b + 1] * d_ref[b:b + 1, :]
        o_ref[...] = acc

    return pl.pallas_call(
        body, name="w_ada_grad", out_shape=jax.ShapeDtypeStruct((sc_t.shape[0], dm.shape[1]), F32),
        compiler_params=_params(),
    )(sc_t, dm)


def _adamw(w, m, v, parts, name):
    rows, cols = w.shape
    n = parts.shape[0]
    tr = rows if rows <= 512 else 256

    def body(w_ref, m_ref, v_ref, p_ref, g_out, d_out, m_out, v_out):
        g = p_ref[0].astype(F32)
        for kk in range(1, n):
            g = g + p_ref[kk].astype(F32)
        mm = ADAM_B1 * m_ref[...] + (1.0 - ADAM_B1) * g
        vv = ADAM_B2 * v_ref[...] + (1.0 - ADAM_B2) * (g * g)
        m_hat = mm / (1.0 - ADAM_B1 ** ADAM_STEP)
        v_hat = vv / (1.0 - ADAM_B2 ** ADAM_STEP)
        g_out[...] = g
        d_out[...] = -ADAM_LR * (m_hat / (jnp.sqrt(v_hat) + ADAM_EPS) + ADAM_WD * w_ref[...])
        m_out[...] = mm
        v_out[...] = vv

    spec = pl.BlockSpec((tr, cols), lambda i: (i, 0))
    return pl.pallas_call(
        body, name=name, grid=(rows // tr,),
        in_specs=[spec, spec, spec, pl.BlockSpec((n, tr, cols), lambda i: (0, i, 0))],
        out_specs=[spec] * 4, out_shape=[jax.ShapeDtypeStruct((rows, cols), F32)] * 4,
        compiler_params=_params(("parallel",)),
    )(w, m, v, parts)


def _coords():
    return lax.axis_index("x"), lax.axis_index("y"), lax.axis_index("c")


def _flat(px, py, pc):
    return 4 * px + 2 * py + pc


def _all_gather(arrs, name):
    n = len(arrs)

    def body(*refs):
        ins, outs = refs[:n], refs[n:2 * n]
        send, recv, lsem = refs[2 * n:]
        x, y, c = _coords()
        me, sibling = (x, y, c), (x, y, 1 - c)
        chips = [(1 - x, y), (x, 1 - y), (1 - x, 1 - y)]

        def copy(a, kk, block, to, src=None):
            slot = outs[a].at[_flat(*block)]
            return pltpu.make_async_remote_copy(
                src_ref=slot if src is None else src, dst_ref=slot,
                send_sem=send.at[a, kk], recv_sem=recv.at[a, kk],
                device_id=to, device_id_type=MESH)

        mine = [pltpu.make_async_copy(ins[a], outs[a].at[_flat(*me)], lsem.at[a]) for a in range(n)]
        for cp in mine:
            cp.start()
        first = []
        for a in range(n):
            first.append(copy(a, 0, me, sibling, src=ins[a]))
            first += [copy(a, 1 + j, me, (*chip, c), src=ins[a]) for j, chip in enumerate(chips)]
        for cp in first:
            cp.start()
        passed = []
        for j, chip in enumerate(chips):
            for a in range(n):
                copy(a, 1 + j, (*chip, c), me).wait_recv()
                cp = copy(a, 4 + j, (*chip, c), sibling)
                cp.start()
                passed.append(cp)
        for a in range(n):
            copy(a, 0, sibling, me).wait_recv()
        for j, chip in enumerate(chips):
            for a in range(n):
                copy(a, 4 + j, (*chip, 1 - c), me).wait_recv()
        for cp in first + passed:
            cp.wait_send()
        for cp in mine:
            cp.wait()

    any_spec = pl.BlockSpec(memory_space=pl.ANY)
    return pl.pallas_call(
        body, name=name,
        in_specs=[any_spec] * n, out_specs=[any_spec] * n,
        out_shape=[jax.ShapeDtypeStruct((N_DEV,) + a.shape, a.dtype) for a in arrs],
        scratch_shapes=[pltpu.SemaphoreType.DMA((n, 7)), pltpu.SemaphoreType.DMA((n, 7)),
                        pltpu.SemaphoreType.DMA((n,))],
    )(*arrs)


def _exchange(arrs, name):
    n = len(arrs)

    def body(*refs):
        ins, outs = refs[:n], refs[n:2 * n]
        send, recv, lsem = refs[2 * n:]
        x, y, c = _coords()
        me = _flat(x, y, c)
        peers = []
        for r in range(1, N_DEV):
            px = 1 - x if r & 4 else x
            py = 1 - y if r & 2 else y
            pc = 1 - c if r & 1 else c
            peers.append((px, py, pc))

        def copy(a, kk):
            peer = peers[kk]
            return pltpu.make_async_remote_copy(
                src_ref=ins[a].at[_flat(*peer)], dst_ref=outs[a].at[me],
                send_sem=send.at[a, kk], recv_sem=recv.at[a, kk],
                device_id=peer, device_id_type=MESH)

        def arrival(a, kk):
            slot = outs[a].at[_flat(*peers[kk])]
            return pltpu.make_async_remote_copy(
                src_ref=slot, dst_ref=slot, send_sem=send.at[a, kk], recv_sem=recv.at[a, kk],
                device_id=peers[kk], device_id_type=MESH)

        mine = [pltpu.make_async_copy(ins[a].at[me], outs[a].at[me], lsem.at[a]) for a in range(n)]
        for cp in mine:
            cp.start()
        sends = [copy(a, kk) for a in range(n) for kk in range(N_DEV - 1)]
        for cp in sends:
            cp.start()
        for a in range(n):
            for kk in range(N_DEV - 1):
                arrival(a, kk).wait_recv()
        for cp in sends:
            cp.wait_send()
        for cp in mine:
            cp.wait()

    any_spec = pl.BlockSpec(memory_space=pl.ANY)
    return pl.pallas_call(
        body, name=name,
        in_specs=[any_spec] * n, out_specs=[any_spec] * n,
        out_shape=[jax.ShapeDtypeStruct(a.shape, a.dtype) for a in arrs],
        scratch_shapes=[pltpu.SemaphoreType.DMA((n, 7)), pltpu.SemaphoreType.DMA((n, 7)),
                        pltpu.SemaphoreType.DMA((n,))],
    )(*arrs)


IN_SPLITS = (512, 512, 512, 8, 768, 256, 32, 1024, 1024)


def _unshard_cols(g):
    return jnp.transpose(g, (1, 0, 2)).reshape(g.shape[1], -1)


def _shard_cols(w):
    k = w.shape[0]
    return jnp.transpose(w.reshape(k, N_DEV, -1), (1, 0, 2))


def _prepare_weights(g):
    w_in = _unshard_cols(g["w_in"])
    offs = [0]
    for wd in IN_SPLITS:
        offs.append(offs[-1] + wd)
    fq, fk, fv, wf, cq, ckv, kr, gf, gm = [w_in[:, offs[i]:offs[i + 1]] for i in range(9)]
    z = functools.partial(jnp.zeros, dtype=BF16)
    misc = jnp.concatenate([wf, z((D, 56)), kr, z((D, 32))], axis=1)
    w_uq = g["w_uq"].reshape(Q_LORA, HEADS, 96)
    w_uq = jnp.pad(w_uq, ((0, 0), (0, 0), (0, 32))).reshape(Q_LORA, HEADS * LANES)
    ukv = g["w_ukv"]
    w_k = jnp.transpose(jnp.pad(ukv[:, :, :64], ((0, 0), (0, 0), (0, 64))), (1, 0, 2)).reshape(KV_LORA, HEADS * LANES)
    w_v = jnp.transpose(ukv[:, :, 64:], (1, 0, 2)).reshape(KV_LORA, HEADS * HEAD_DIM)
    return dict(
        w_a=jnp.concatenate([cq, ckv, gf, gm, misc], axis=1),
        w_b=jnp.concatenate([fq, fk, fv], axis=1),
        w_uq=w_uq, w_kv=jnp.concatenate([w_k, w_v], axis=1), w_k=w_k, w_v=w_v,
        w_pf=_unshard_cols(g["w_proj_fox"]), w_pm=_unshard_cols(g["w_proj_mla"]),
        w_out=g["w_out"].reshape(D, D),
        w_ffn_in=_unshard_cols(g["w_ffn_in"]),
        w_ffn_out=g["w_ffn_out"].reshape(D_FF, D),
    )


def _shard_grads(dw):
    da, db_ = dw["w_a"], dw["w_b"]
    misc = da[:, 3072:3200]
    w_in = jnp.concatenate([db_, misc[:, 0:8], da[:, 0:768], da[:, 768:1024], misc[:, 64:96],
                            da[:, 1024:3072]], axis=1)
    w_uq = dw["w_uq"].reshape(Q_LORA, HEADS, LANES)[:, :, :96].reshape(Q_LORA, Q_LORA)
    k_part = dw["w_k"].reshape(KV_LORA, HEADS, LANES)[:, :, :64]
    v_part = dw["w_v"].reshape(KV_LORA, HEADS, HEAD_DIM)
    w_ukv = jnp.transpose(jnp.concatenate([k_part, v_part], axis=2), (1, 0, 2))
    return dict(
        w_in=_shard_cols(w_in), w_uq=w_uq.reshape(N_DEV, Q_LORA // N_DEV, Q_LORA), w_ukv=w_ukv,
        w_proj_fox=_shard_cols(dw["w_pf"]), w_proj_mla=_shard_cols(dw["w_pm"]),
        w_out=dw["w_out"].reshape(N_DEV, D // N_DEV, D),
        w_ffn_in=_shard_cols(dw["w_ffn_in"]),
        w_ffn_out=dw["w_ffn_out"].reshape(N_DEV, D_FF // N_DEV, D),
    )


def _fwd_bwd(x, pos, mod, target, w, vec):
    shift_mix, scale_mix, gate_mix, shift_ffn, scale_ffn, gate_ffn = [mod[:, i * D:(i + 1) * D] for i in range(6)]
    g_pre_mix, g_post_mix, g_pre_ffn, g_post_ffn = vec["g_pre_mix"], vec["g_post_mix"], vec["g_pre_ffn"], vec["g_post_ffn"]
    g_q, g_kv = vec["g_q_lora"], vec["g_kv_lora"]

    inv_freq = 1.0 / (ROPE_THETA ** (jnp.arange(0, ROPE_DIM, 2, dtype=F32) / ROPE_DIM))
    invf = jnp.concatenate([jnp.zeros((64,), F32), inv_freq, inv_freq, jnp.zeros((32,), F32)]).reshape(1, LANES)
    ct, sa, sb = _rope_tables(pos, invf)

    def pre1(xv, g, sc, sh):
        return ((xv * _rstd(xv) * g) * (1.0 + sc) + sh,), ()
    (h,) = _rowwise(pre1, [(x, D, 0)], [g_pre_mix, scale_mix, shift_mix], [(D, BF16)], [], "pre_mix")
    proj_a = _mm(h, w["w_a"], "nn", F32, "in_proj_a")
    qkv = _mm(h, w["w_b"], "nn", BF16, "in_proj_b")

    def lora_norm(cq, ckv, gq, gkv):
        return (cq * _rstd(cq) * gq, ckv * _rstd(ckv) * gkv), ()
    cqn, ckvn = _rowwise(lora_norm, [(proj_a, Q_LORA, 0), (proj_a, KV_LORA, 3)], [g_q, g_kv],
                         [(Q_LORA, BF16), (KV_LORA, BF16)], [], "lora_norm")
    qb = _mm(cqn, w["w_uq"], "nn", F32, "mla_uq")
    kvb = _mm(ckvn, w["w_kv"], "nn", F32, "mla_ukv")

    def mla_rope(qv, kv, vv, misc, c_, a_, b_):
        lane = lax.broadcasted_iota(jnp.int32, (1, LANES), 1)
        kpe = jnp.where((lane >= 64) & (lane < 96), _rope(misc, c_, a_, b_), 0.0)
        qs = [_rope(qv[:, hd * LANES:(hd + 1) * LANES], c_, a_, b_) for hd in range(HEADS)]
        ks = [kv[:, hd * LANES:(hd + 1) * LANES] + kpe for hd in range(HEADS)]
        return (jnp.concatenate(qs, axis=1), jnp.concatenate(ks, axis=1), vv), ()
    q_m, k_m, v_m = _rowwise(
        mla_rope, [(qb, D, 0), (kvb, D, 0), (kvb, 512, 2), (proj_a, LANES, 24), (ct, LANES, 0), (sa, LANES, 0), (sb, LANES, 0)],
        [], [(D, BF16), (D, BF16), (512, BF16)], [], "mla_rope")

    zt = jnp.transpose(proj_a[:, 3072:3080])
    bf = jnp.transpose(vec["b_forget"])
    neg_f = _fox_gates(zt, bf)
    bias = neg_f.reshape(HEADS, N_ATT, 1, ATT_T)
    o_a, lse_a = _attn_fwd(qkv, 0, qkv, 4, qkv, 8, LANES, 1.0 / math.sqrt(HEAD_DIM), bias, "fox_attn")
    o_b, lse_b = _attn_fwd(q_m, 0, k_m, 0, v_m, 0, 2 * LANES, 1.0 / math.sqrt(64 + ROPE_DIM), None, "mla_attn")

    pa = _mm(o_a, w["w_pf"], "nn", F32, "proj_fox")
    pb = _mm(o_b, w["w_pm"], "nn", F32, "proj_mla")

    def merge(gf, gm, pa_, pb_):
        return (_sigmoid(gf) * pa_ + _sigmoid(gm) * pb_,), ()
    (merged,) = _rowwise(merge, [(proj_a, D, 1), (proj_a, D, 2), (pa, D, 0), (pb, D, 0)], [], [(D, BF16)], [], "merge")
    y = _mm(merged, w["w_out"], "nn", F32, "out_proj")

    def post1(xv, yv, gate, gpost, gpre, sc, sh):
        x1 = xv + gate * (yv * _rstd(yv) * gpost)
        return (x1, (x1 * _rstd(x1) * gpre) * (1.0 + sc) + sh), ()
    x1, h2 = _rowwise(post1, [(x, D, 0), (y, D, 0)], [gate_mix, g_post_mix, g_pre_ffn, scale_ffn, shift_ffn],
                      [(D, F32), (D, BF16)], [], "post_mix")
    gu = _mm(h2, w["w_ffn_in"], "nn", F32, "ffn_in")

    def swiglu(g, u):
        return (g * _sigmoid(g) * u,), ()
    (act,) = _rowwise(swiglu, [(gu, D_FF, 0), (gu, D_FF, 1)], [], [(D_FF, BF16)], [], "swiglu")
    y2 = _mm(act, w["w_ffn_out"], "nn", F32, "ffn_out")

    def head(x1v, y2v, tv, gate, gpost):
        r = _rstd(y2v)
        yn = y2v * r
        n2 = yn * gpost
        err = (x1v + gate * n2) - tv
        dx2 = err * (1.0 / D)
        dn2 = dx2 * gate
        dy2 = _norm_bwd(dn2 * gpost, yn, r)
        return (dx2, dy2), (_colsum(err * err), _colsum(dx2 * n2), _colsum(dn2 * yn))
    dx2, dy2, err_cols, d_gate_ffn, d_g_post_ffn = _rowwise(
        head, [(x1, D, 0), (y2, D, 0), (target, D, 0)], [gate_ffn, g_post_ffn], [(D, F32), (D, BF16)], [D, D, D], "loss_head")

    dact = _mm(dy2, w["w_ffn_out"], "nt", F32, "ffn_out_dx")
    dw = {"w_ffn_out": _mm(act, dy2, "tn", BF16, "ffn_out_dw")}

    def swiglu_bwd(g, u, da):
        sg = _sigmoid(g)
        return (jnp.concatenate([da * u * (sg * (1.0 + g * (1.0 - sg))), da * (g * sg)], axis=1),), ()
    (dgu,) = _rowwise(swiglu_bwd, [(gu, D_FF, 0), (gu, D_FF, 1), (dact, D_FF, 0)], [], [(2 * D_FF, BF16)], [], "swiglu_bwd")
    dh2 = _mm(dgu, w["w_ffn_in"], "nt", F32, "ffn_in_dx")
    dw["w_ffn_in"] = _mm(h2, dgu, "tn", BF16, "ffn_in_dw")

    def mid(dh, x1v, dx2v, yv, gpre, sc, gate, gpost):
        r2 = _rstd(x1v)
        x1n = x1v * r2
        t = dh * x1n
        dx1 = dx2v + _norm_bwd(dh * (gpre * (1.0 + sc)), x1n, r2)
        ry = _rstd(yv)
        yn = yv * ry
        dn1 = dx1 * gate
        dy = _norm_bwd(dn1 * gpost, yn, ry)
        sums = (_colsum(dh), _colsum(t) * gpre, _colsum(t) * (1.0 + sc), _colsum(dx1 * (yn * gpost)), _colsum(dn1 * yn))
        return (dx1, dy), sums
    dx1, dy, d_shift_ffn, d_scale_ffn, d_g_pre_ffn, d_gate_mix, d_g_post_mix = _rowwise(
        mid, [(dh2, D, 0), (x1, D, 0), (dx2, D, 0), (y, D, 0)], [g_pre_ffn, scale_ffn, gate_mix, g_post_mix],
        [(D, F32), (D, BF16)], [D] * 5, "mid_bwd")

    dmerged = _mm(dy, w["w_out"], "nt", F32, "out_proj_dx")
    dw["w_out"] = _mm(merged, dy, "tn", BF16, "out_proj_dw")

    def merge_bwd(dm, gf, gm, pa_, pb_):
        sf, sm = _sigmoid(gf), _sigmoid(gm)
        dgates = jnp.concatenate([dm * pa_ * (sf * (1.0 - sf)), dm * pb_ * (sm * (1.0 - sm))], axis=1)
        return (dm * sf, dm * sm, dgates), ()
    dpa, dpb, dgates = _rowwise(
        merge_bwd, [(dmerged, D, 0), (proj_a, D, 1), (proj_a, D, 2), (pa, D, 0), (pb, D, 0)], [],
        [(D, BF16), (D, BF16), (2 * D, BF16)], [], "merge_bwd")
    do_a = _mm(dpa, w["w_pf"], "nt", BF16, "proj_fox_dx")
    do_b = _mm(dpb, w["w_pm"], "nt", BF16, "proj_mla_dx")
    dw["w_pf"] = _mm(o_a, dpa, "tn", BF16, "proj_fox_dw")
    dw["w_pm"] = _mm(o_b, dpb, "tn", BF16, "proj_mla_dw")

    sc_a, sc_b = 1.0 / math.sqrt(HEAD_DIM), 1.0 / math.sqrt(64 + ROPE_DIM)
    delta_a = _attn_delta(qkv, 0, qkv, 4, qkv, 8, do_a, lse_a, LANES, sc_a, bias, "fox_attn_delta")
    delta_b = _attn_delta(q_m, 0, k_m, 0, v_m, 0, do_b, lse_b, 2 * LANES, sc_b, None, "mla_attn_delta")
    dq_a, dk_a, dv_a, dbias = _attn_bwd(qkv, 0, qkv, 4, qkv, 8, do_a, lse_a, delta_a, LANES, sc_a, bias, BF16, "fox_attn_bwd")
    dq_m, dk_m, dv_m = _attn_bwd(q_m, 0, k_m, 0, v_m, 0, do_b, lse_b, delta_b, 2 * LANES, sc_b, None, F32, "mla_attn_bwd")

    def mla_rope_bwd(dq, dk, c_, a_, b_):
        lane = lax.broadcasted_iota(jnp.int32, (1, LANES), 1)
        dqs = [_rope_t(dq[:, hd * LANES:(hd + 1) * LANES], c_, a_, b_) for hd in range(HEADS)]
        dkpe = dk[:, 0:LANES]
        for hd in range(1, HEADS):
            dkpe = dkpe + dk[:, hd * LANES:(hd + 1) * LANES]
        dkpe = jnp.where((lane >= 64) & (lane < 96), dkpe, 0.0)
        dkr = jnp.where((lane >= 64) & (lane < 96), _rope_t(dkpe, c_, a_, b_), 0.0)
        return (jnp.concatenate(dqs, axis=1), dk, dkr), ()
    dqb, dkb, dkr = _rowwise(mla_rope_bwd, [(dq_m, D, 0), (dk_m, D, 0), (ct, LANES, 0), (sa, LANES, 0), (sb, LANES, 0)],
                             [], [(D, BF16), (D, BF16), (LANES, F32)], [], "mla_rope_bwd")
    dcqn = _mm(dqb, w["w_uq"], "nt", F32, "mla_uq_dx")
    dw["w_uq"] = _mm(cqn, dqb, "tn", BF16, "mla_uq_dw")
    dckvn = _mm(dv_m, w["w_v"], "nt", F32, "mla_uv_dx", acc=_mm(dkb, w["w_k"], "nt", F32, "mla_uk_dx"))
    dw["w_k"] = _mm(ckvn, dkb, "tn", BF16, "mla_uk_dw")
    dw["w_v"] = _mm(ckvn, dv_m, "tn", BF16, "mla_uv_dw")

    def lora_norm_bwd(cq, ckv, dq, dkv, gq, gkv):
        rq, rk = _rstd(cq), _rstd(ckv)
        cqh, ckh = cq * rq, ckv * rk
        return (_norm_bwd(dq * gq, cqh, rq), _norm_bwd(dkv * gkv, ckh, rk)), (_colsum(dq * cqh), _colsum(dkv * ckh))
    dcq, dckv, d_g_q, d_g_kv = _rowwise(
        lora_norm_bwd, [(proj_a, Q_LORA, 0), (proj_a, KV_LORA, 3), (dcqn, Q_LORA, 0), (dckvn, KV_LORA, 0)], [g_q, g_kv],
        [(Q_LORA, BF16), (KV_LORA, BF16)], [Q_LORA, KV_LORA], "lora_norm_bwd")

    dzt, d_bf = _fox_gates_bwd(dbias.reshape(HEADS, S), zt, bf)
    dmisc = (dkr + jnp.pad(jnp.transpose(dzt), ((0, 0), (0, LANES - HEADS)))).astype(BF16)
    dproj_a = jnp.concatenate([dcq, dckv, dgates, dmisc], axis=1)
    dqkv = jnp.concatenate([dq_a, dk_a, dv_a], axis=1)
    dh = _mm(dqkv, w["w_b"], "nt", F32, "in_proj_b_dx", acc=_mm(dproj_a, w["w_a"], "nt", F32, "in_proj_a_dx"))
    dw["w_a"] = _mm(h, dproj_a, "tn", BF16, "in_proj_a_dw")
    dw["w_b"] = _mm(h, dqkv, "tn", BF16, "in_proj_b_dw")

    def first(dhv, xv, dx1v, gpre, sc):
        r = _rstd(xv)
        xn = xv * r
        t = dhv * xn
        dx = dx1v + _norm_bwd(dhv * (gpre * (1.0 + sc)), xn, r)
        return (dx,), (_colsum(dhv), _colsum(t) * gpre, _colsum(t) * (1.0 + sc))
    grad_x, d_shift_mix, d_scale_mix, d_g_pre_mix = _rowwise(
        first, [(dh, D, 0), (x, D, 0), (dx1, D, 0)], [g_pre_mix, scale_mix], [(D, F32)], [D] * 3, "pre_mix_bwd")

    dmod = jnp.concatenate([d_shift_mix, d_scale_mix, d_gate_mix, d_shift_ffn, d_scale_ffn, d_gate_ffn], axis=1)
    small = dict(dmod=dmod, g_pre_mix=d_g_pre_mix, g_post_mix=d_g_post_mix, g_pre_ffn=d_g_pre_ffn,
                 g_post_ffn=d_g_post_ffn, g_q_lora=d_g_q, g_kv_lora=d_g_kv,
                 b_forget=jnp.pad(jnp.transpose(d_bf), ((0, 0), (0, LANES - HEADS))), err=err_cols)
    return grad_x, dw, small


SMALL_ORDER = ("dmod", "g_pre_mix", "g_post_mix", "g_pre_ffn", "g_post_ffn", "g_q_lora", "g_kv_lora", "b_forget", "err")
SMALL_PARAM = {"dmod": "b_ada"}
MATRICES = ("w_in", "w_uq", "w_ukv", "w_proj_fox", "w_proj_mla", "w_out", "w_ffn_in", "w_ffn_out")
WEIGHTS = ("w_ada", "b_ada", "g_pre_mix", "g_post_mix", "g_pre_ffn", "g_post_ffn", "w_in", "b_forget", "g_q_lora",
           "w_uq", "g_kv_lora", "w_ukv", "w_proj_fox", "w_proj_mla", "w_out", "w_ffn_in", "w_ffn_out")


def _pad_lanes(v):
    return jnp.pad(v, ((0, 0), (0, (-v.shape[1]) % LANES)))


def kernel(x, c, positions, w_ada, b_ada, g_pre_mix, g_post_mix, g_pre_ffn, g_post_ffn, w_in, b_forget, g_q_lora, w_uq, g_kv_lora, w_ukv, w_proj_fox, w_proj_mla, w_out, w_ffn_in, w_ffn_out, loss_target, m_w_ada, m_b_ada, m_g_pre_mix, m_g_post_mix, m_g_pre_ffn, m_g_post_ffn, m_w_in, m_b_forget, m_g_q_lora, m_w_uq, m_g_kv_lora, m_w_ukv, m_w_proj_fox, m_w_proj_mla, m_w_out, m_w_ffn_in, m_w_ffn_out, v_w_ada, v_b_ada, v_g_pre_mix, v_g_post_mix, v_g_pre_ffn, v_g_post_ffn, v_w_in, v_b_forget, v_g_q_lora, v_w_uq, v_g_kv_lora, v_w_ukv, v_w_proj_fox, v_w_proj_mla, v_w_out, v_w_ffn_in, v_w_ffn_out):
    prm = dict(w_ada=w_ada, b_ada=b_ada, g_pre_mix=g_pre_mix, g_post_mix=g_post_mix, g_pre_ffn=g_pre_ffn,
               g_post_ffn=g_post_ffn, w_in=w_in, b_forget=b_forget, g_q_lora=g_q_lora, w_uq=w_uq, g_kv_lora=g_kv_lora,
               w_ukv=w_ukv, w_proj_fox=w_proj_fox, w_proj_mla=w_proj_mla, w_out=w_out, w_ffn_in=w_ffn_in, w_ffn_out=w_ffn_out)
    mom = dict(w_ada=m_w_ada, b_ada=m_b_ada, g_pre_mix=m_g_pre_mix, g_post_mix=m_g_post_mix, g_pre_ffn=m_g_pre_ffn,
               g_post_ffn=m_g_post_ffn, w_in=m_w_in, b_forget=m_b_forget, g_q_lora=m_g_q_lora, w_uq=m_w_uq,
               g_kv_lora=m_g_kv_lora, w_ukv=m_w_ukv, w_proj_fox=m_w_proj_fox, w_proj_mla=m_w_proj_mla, w_out=m_w_out,
               w_ffn_in=m_w_ffn_in, w_ffn_out=m_w_ffn_out)
    var = dict(w_ada=v_w_ada, b_ada=v_b_ada, g_pre_mix=v_g_pre_mix, g_post_mix=v_g_post_mix, g_pre_ffn=v_g_pre_ffn,
               g_post_ffn=v_g_post_ffn, w_in=v_w_in, b_forget=v_b_forget, g_q_lora=v_g_q_lora, w_uq=v_w_uq,
               g_kv_lora=v_g_kv_lora, w_ukv=v_w_ukv, w_proj_fox=v_w_proj_fox, w_proj_mla=v_w_proj_mla, w_out=v_w_out,
               w_ffn_in=v_w_ffn_in, w_ffn_out=v_w_ffn_out)
    me = _flat(*_coords())

    locals_bf16 = [prm[n][0].astype(BF16) for n in MATRICES]
    gathered = _all_gather(locals_bf16 + [c], "gather_weights")
    w = _prepare_weights(dict(zip(MATRICES, gathered[:-1])))
    c_all = gathered[-1].reshape(N_DEV, D)

    ada_cols = w_ada.shape[2]
    b_cols = lax.dynamic_slice(b_ada, (0, me * ada_cols), (1, ada_cols))
    mod_cols, silu_c = _mod_part(c_all, w_ada[0], b_cols)
    (mod_all,) = _all_gather([mod_cols], "gather_mod")
    mod = lax.dynamic_index_in_dim(mod_all, me, axis=1, keepdims=False).reshape(1, 6 * D)

    vec = dict(g_pre_mix=g_pre_mix, g_post_mix=g_post_mix, g_pre_ffn=g_pre_ffn, g_post_ffn=g_post_ffn,
               g_q_lora=g_q_lora, g_kv_lora=g_kv_lora, b_forget=b_forget)
    pos = positions.astype(F32).reshape(S, 1)
    grad_x, dw, small = _fwd_bwd(x[0], pos, mod, loss_target[0], w, vec)

    bundle = jnp.concatenate([small[n] for n in SMALL_ORDER], axis=1)
    (bundle_all,) = _all_gather([bundle], "gather_small")
    bundle_all = bundle_all.reshape(N_DEV, 1, -1)
    shards = _shard_grads(dw)
    parts = _exchange([shards[n] for n in MATRICES], "exchange_grads")

    out = {}
    for n, p in zip(MATRICES, parts):
        out[n] = _adamw(prm[n][0], mom[n][0], var[n][0], p, "adamw_" + n)

    dmod_all = bundle_all[:, 0, :6 * D]
    dm_cols = lax.dynamic_slice(dmod_all, (0, me * ada_cols), (N_DEV, ada_cols))
    g_ada = _w_ada_grad(jnp.transpose(silu_c), dm_cols)
    out["w_ada"] = _adamw(w_ada[0], m_w_ada[0], v_w_ada[0], g_ada[None], "adamw_w_ada")

    def row(src, n):
        if n == "err":
            return jnp.zeros((1, D), F32)
        return _pad_lanes(src[SMALL_PARAM.get(n, n)])
    wv, mv, vv = [jnp.concatenate([row(src, n) for n in SMALL_ORDER], axis=1) for src in (prm, mom, var)]
    gs, ds, ms, vs = _adamw(wv, mv, vv, bundle_all, "adamw_small")
    off = 0
    for n in SMALL_ORDER:
        width = small[n].shape[1]
        if n != "err":
            pn = SMALL_PARAM.get(n, n)
            real = prm[pn].shape[1]
            out[pn] = tuple(t[:, off:off + real] for t in (gs, ds, ms, vs))
        else:
            loss = 0.5 * jnp.sum(gs[0, off:off + width]) / D
        off += width

    res = [loss, grad_x[None]]
    for kind in range(4):
        for n in WEIGHTS:
            t = out[n][kind]
            res.append(t[None] if prm[n].ndim == 3 else t)
    return tuple(res)
```

```python
import functools
import math

import jax
import jax.numpy as jnp
from jax import lax
from jax.experimental import pallas as pl
from jax.experimental.pallas import tpu as pltpu

F32 = jnp.float32
BF16 = jnp.bfloat16

N_DEV = 8
S = 2048
D = 1024
D_FF = 2816
HEADS = 8
HEAD_DIM = 64
Q_LORA = 768
KV_LORA = 256
ROPE_DIM = 32
ROPE_THETA = 10000.0
NORM_EPS = 1e-6
LANES = 128
VMEM_LIMIT = 56 * 1024 * 1024

ADAM_LR = 0.001
ADAM_B1 = 0.9
ADAM_B2 = 0.999
ADAM_EPS = 1e-08
ADAM_WD = 0.01
ADAM_STEP = 10

ATT_T = 256
N_ATT = S // ATT_T

NN = (((1,), (0,)), ((), ()))
NT = (((1,), (1,)), ((), ()))
TN = (((0,), (0,)), ((), ()))
MESH = pl.DeviceIdType.MESH


def _params(sem=None):
    return pltpu.CompilerParams(dimension_semantics=sem, vmem_limit_bytes=VMEM_LIMIT)


def _pick(n, cap):
    best = None
    for t in range(LANES, cap + 1, LANES):
        if n % t == 0:
            best = t
    return best if best is not None else n


def _mm(a, b, mode, out_dtype, name, acc=None):
    if mode == "nn":
        (m, k), (k2, n), dn = a.shape, b.shape, NN
    elif mode == "nt":
        (m, k), (n, k2), dn = a.shape, b.shape, NT
    else:
        (k, m), (k2, n), dn = a.shape, b.shape, TN
    assert k == k2, (a.shape, b.shape, mode)
    tn = _pick(n, 640)
    tm = _pick(m, 1024)
    osz = jnp.dtype(out_dtype).itemsize

    def need(tm_):
        blk = tm_ * k * 2 + tn * k * 2 + tm_ * tn * osz + (tm_ * tn * 4 if acc is not None else 0)
        return 2 * blk + tm_ * tn * 4
    while need(tm) > 36 * 1024 * 1024 and tm % 256 == 0:
        tm //= 2

    def body(*refs):
        if acc is not None:
            a_ref, b_ref, c_ref, o_ref = refs
        else:
            a_ref, b_ref, o_ref = refs
        r = lax.dot_general(a_ref[...], b_ref[...], dn, preferred_element_type=F32)
        if acc is not None:
            r = r + c_ref[...]
        o_ref[...] = r.astype(o_ref.dtype)

    if mode == "tn":
        a_spec = pl.BlockSpec((k, tm), lambda i, j: (0, i))
    else:
        a_spec = pl.BlockSpec((tm, k), lambda i, j: (i, 0))
    if mode == "nt":
        b_spec = pl.BlockSpec((tn, k), lambda i, j: (j, 0))
    else:
        b_spec = pl.BlockSpec((k, tn), lambda i, j: (0, j))
    o_spec = pl.BlockSpec((tm, tn), lambda i, j: (i, j))
    in_specs = [a_spec, b_spec] + ([o_spec] if acc is not None else [])
    args = (a, b) + ((acc,) if acc is not None else ())
    return pl.pallas_call(
        body, name=name, grid=(m // tm, n // tn),
        in_specs=in_specs, out_specs=o_spec,
        out_shape=jax.ShapeDtypeStruct((m, n), out_dtype),
        compiler_params=_params(("parallel", "parallel")),
    )(*args)


def _rowwise(fn, row_ins, vec_ins, row_outs, sum_outs, name, tm=256):
    n_in = len(row_ins) + len(vec_ins)
    n_o = len(row_outs)
    rows = row_ins[0][0].shape[0]

    def body(*refs):
        vals = [r[...] for r in refs[:n_in]]
        outs = refs[n_in:]
        ro, so = fn(*vals)
        assert len(ro) == n_o and len(so) == len(sum_outs)
        for r, v in zip(outs[:n_o], ro):
            r[...] = v.astype(r.dtype)
        if sum_outs:
            @pl.when(pl.program_id(0) == 0)
            def _():
                for r in outs[n_o:]:
                    r[...] = jnp.zeros(r.shape, F32)
            for r, v in zip(outs[n_o:], so):
                r[...] += v

    in_specs = [pl.BlockSpec((tm, w), functools.partial(lambda i, b: (i, b), b=b)) for _, w, b in row_ins]
    in_specs += [pl.BlockSpec(v.shape, lambda i: (0, 0)) for v in vec_ins]
    out_specs = [pl.BlockSpec((tm, w), lambda i: (i, 0)) for w, _ in row_outs]
    out_specs += [pl.BlockSpec((1, w), lambda i: (0, 0)) for w in sum_outs]
    out_shape = [jax.ShapeDtypeStruct((rows, w), dt) for w, dt in row_outs]
    out_shape += [jax.ShapeDtypeStruct((1, w), F32) for w in sum_outs]
    return pl.pallas_call(
        body, name=name, grid=(rows // tm,),
        in_specs=in_specs, out_specs=out_specs, out_shape=out_shape,
        compiler_params=_params(("arbitrary",)),
    )(*[a for a, _, _ in row_ins], *vec_ins)


def _sigmoid(x):
    return 1.0 / (1.0 + jnp.exp(-x))


def _rstd(x):
    return lax.rsqrt(jnp.mean(x * x, axis=-1, keepdims=True) + NORM_EPS)


def _norm_bwd(dyn, xn, r):
    return r * (dyn - xn * jnp.mean(dyn * xn, axis=-1, keepdims=True))


def _colsum(x):
    return jnp.sum(x, axis=0, keepdims=True)


def _rope_tables(pos, invf):
    def fn(p, f):
        lane = lax.broadcasted_iota(jnp.int32, (1, LANES), 1)
        ang = p * f
        cs, sn = jnp.cos(ang), jnp.sin(ang)
        rot = (lane >= 64) & (lane < 96)
        ct = jnp.where(lane < 64, 1.0, jnp.where(rot, cs, 0.0))
        sa = jnp.where((lane >= 64) & (lane < 80), -sn, 0.0)
        sb = jnp.where((lane >= 80) & (lane < 96), sn, 0.0)
        return (ct, sa, sb), ()
    return _rowwise(fn, [(pos, 1, 0)], [invf], [(LANES, F32)] * 3, [], "rope_tables")


def _rope(x, ct, sa, sb):
    return x * ct + pltpu.roll(x, LANES - 16, 1) * sa + pltpu.roll(x, 16, 1) * sb


def _rope_t(x, ct, sa, sb):
    return x * ct - pltpu.roll(x, LANES - 16, 1) * sa - pltpu.roll(x, 16, 1) * sb


def _head_mask(width, hh):
    lane = lax.broadcasted_iota(jnp.int32, (1, width), 1)
    half = width // 2
    return (lane >= hh * half) & (lane < (hh + 1) * half)


ATT_PP = 2
ATT_CHAINS = [(a, hh) for a in range(ATT_PP) for hh in range(2)]
ATT_G = HEADS // (2 * ATT_PP)


def _pair(ref_or_val, a, width, rows=slice(None)):
    return ref_or_val[rows, a * width:(a + 1) * width]


def _attn_fwd(q, qo, k, ko, v, vo, dkp, scale, bias, name):
    T = ATT_T
    assert qo % ATT_PP == 0 and ko % ATT_PP == 0 and vo % ATT_PP == 0
    qo, ko, vo = qo // ATT_PP, ko // ATT_PP, vo // ATT_PP

    def body(*refs):
        if bias is not None:
            q_ref, k_ref, v_ref, b_ref, o_ref, lse_ref = refs
        else:
            q_ref, k_ref, v_ref, o_ref, lse_ref = refs
        i = pl.program_id(1)
        row = lax.broadcasted_iota(jnp.int32, (T, T), 0)
        col = lax.broadcasted_iota(jnp.int32, (T, T), 1)
        qms = []
        for a, hh in ATT_CHAINS:
            qb = _pair(q_ref, a, dkp)
            qms.append(jnp.where(_head_mask(dkp, hh), qb, jnp.zeros_like(qb)))

        def step(j, carry, masked):
            ks = pl.ds(pl.multiple_of(j * T, T), T)
            new = []
            for ci, (a, hh) in enumerate(ATT_CHAINS):
                m, l, acc = carry[ci]
                s = lax.dot_general(qms[ci], _pair(k_ref, a, dkp, ks), NT, preferred_element_type=F32) * scale
                if bias is not None:
                    s = s + b_ref[2 * a + hh, j]
                if masked:
                    s = jnp.where(row >= col, s, -jnp.inf)
                m_new = jnp.maximum(m, jnp.max(s, axis=1, keepdims=True))
                alpha = jnp.exp(m - m_new)
                p = jnp.exp(s - m_new)
                l = alpha * l + jnp.sum(p, axis=1, keepdims=True)
                acc = alpha * acc + lax.dot_general(p.astype(BF16), _pair(v_ref, a, LANES, ks), NN,
                                                    preferred_element_type=F32)
                new.append((m_new, l, acc))
            return tuple(new)

        init = tuple((jnp.full((T, 1), -jnp.inf, F32), jnp.zeros((T, 1), F32), jnp.zeros((T, LANES), F32))
                     for _ in ATT_CHAINS)
        carry = lax.fori_loop(0, i, functools.partial(step, masked=False), init)
        carry = step(i, carry, True)
        for a in range(ATT_PP):
            (m0, l0, acc0), (m1, l1, acc1) = carry[2 * a], carry[2 * a + 1]
            lse_ref[2 * a] = m0 + jnp.log(l0)
            lse_ref[2 * a + 1] = m1 + jnp.log(l1)
            o_ref[:, a * LANES:(a + 1) * LANES] = jnp.where(_head_mask(LANES, 0), acc0 / l0, acc1 / l1).astype(o_ref.dtype)

    in_specs = [
        pl.BlockSpec((T, ATT_PP * dkp), lambda g, i: (i, qo + g)),
        pl.BlockSpec((S, ATT_PP * dkp), lambda g, i: (0, ko + g)),
        pl.BlockSpec((S, ATT_PP * LANES), lambda g, i: (0, vo + g)),
    ]
    args = [q, k, v]
    if bias is not None:
        in_specs.append(pl.BlockSpec((2 * ATT_PP, N_ATT, 1, T), lambda g, i: (g, 0, 0, 0)))
        args.append(bias)
    return pl.pallas_call(
        body, name=name, grid=(ATT_G, N_ATT),
        in_specs=in_specs,
        out_specs=[pl.BlockSpec((T, ATT_PP * LANES), lambda g, i: (i, g)),
                   pl.BlockSpec((2 * ATT_PP, T, 1), lambda g, i: (g, i, 0))],
        out_shape=[jax.ShapeDtypeStruct((S, HEADS * HEAD_DIM), BF16),
                   jax.ShapeDtypeStruct((HEADS, S, 1), F32)],
        compiler_params=_params(("parallel", "arbitrary")),
    )(*args)


def _attn_delta(q, qo, k, ko, v, vo, do, lse, dkp, scale, bias, name):
    T = ATT_T
    qo, ko, vo = qo // ATT_PP, ko // ATT_PP, vo // ATT_PP

    def body(*refs):
        q_ref, k_ref, v_ref, do_ref, lse_ref = refs[:5]
        b_ref = refs[5] if bias is not None else None
        out_ref = refs[-1]
        i = pl.program_id(1)
        row = lax.broadcasted_iota(jnp.int32, (T, T), 0)
        col = lax.broadcasted_iota(jnp.int32, (T, T), 1)
        qms, doms, lses = [], [], []
        for a, hh in ATT_CHAINS:
            qb, dob = _pair(q_ref, a, dkp), _pair(do_ref, a, LANES)
            qms.append(jnp.where(_head_mask(dkp, hh), qb, jnp.zeros_like(qb)))
            doms.append(jnp.where(_head_mask(LANES, hh), dob, jnp.zeros_like(dob)))
            lses.append(lse_ref[2 * a + hh])

        def step(j, carry, masked):
            ks = pl.ds(pl.multiple_of(j * T, T), T)
            new = []
            for ci, (a, hh) in enumerate(ATT_CHAINS):
                s = lax.dot_general(qms[ci], _pair(k_ref, a, dkp, ks), NT, preferred_element_type=F32) * scale
                if bias is not None:
                    s = s + b_ref[2 * a + hh, j]
                s = s - lses[ci]
                if masked:
                    s = jnp.where(row >= col, s, -jnp.inf)
                dp = lax.dot_general(doms[ci], _pair(v_ref, a, LANES, ks), NT, preferred_element_type=F32)
                new.append(carry[ci] + jnp.sum(jnp.exp(s) * dp, axis=1, keepdims=True))
            return tuple(new)

        init = tuple(jnp.zeros((T, 1), F32) for _ in ATT_CHAINS)
        acc = step(i, lax.fori_loop(0, i, functools.partial(step, masked=False), init), True)
        for ci, (a, hh) in enumerate(ATT_CHAINS):
            out_ref[2 * a + hh] = acc[ci]

    in_specs = [
        pl.BlockSpec((T, ATT_PP * dkp), lambda g, i: (i, qo + g)),
        pl.BlockSpec((S, ATT_PP * dkp), lambda g, i: (0, ko + g)),
        pl.BlockSpec((S, ATT_PP * LANES), lambda g, i: (0, vo + g)),
        pl.BlockSpec((T, ATT_PP * LANES), lambda g, i: (i, g)),
        pl.BlockSpec((2 * ATT_PP, T, 1), lambda g, i: (g, i, 0)),
    ]
    args = [q, k, v, do, lse]
    if bias is not None:
        in_specs.append(pl.BlockSpec((2 * ATT_PP, N_ATT, 1, T), lambda g, i: (g, 0, 0, 0)))
        args.append(bias)
    return pl.pallas_call(
        body, name=name, grid=(ATT_G, N_ATT),
        in_specs=in_specs, out_specs=pl.BlockSpec((2 * ATT_PP, T, 1), lambda g, i: (g, i, 0)),
        out_shape=jax.ShapeDtypeStruct((HEADS, S, 1), F32),
        compiler_params=_params(("parallel", "arbitrary")),
    )(*args)


def _attn_bwd(q, qo, k, ko, v, vo, do, lse, delta_in, dkp, scale, bias, qk_dtype, name):
    T = ATT_T
    has_b = bias is not None
    qo, ko, vo = qo // ATT_PP, ko // ATT_PP, vo // ATT_PP

    def body(*refs):
        q_ref, k_ref, v_ref, do_ref, lse_ref, delta = refs[:6]
        refs = refs[6:]
        if has_b:
            b_ref, refs = refs[0], refs[1:]
        dq_ref, dk_ref, dv_ref = refs[:3]
        refs = refs[3:]
        if has_b:
            db_ref, refs = refs[0], refs[1:]
        dq_acc, dk_acc, dv_acc = refs
        j = pl.program_id(1)

        @pl.when(j == 0)
        def _():
            dq_acc[...] = jnp.zeros(dq_acc.shape, F32)

        dk_acc[...] = jnp.zeros(dk_acc.shape, F32)
        dv_acc[...] = jnp.zeros(dv_acc.shape, F32)
        row = lax.broadcasted_iota(jnp.int32, (T, T), 0)
        col = lax.broadcasted_iota(jnp.int32, (T, T), 1)
        kbs, vbs, kms = [], [], []
        for a in range(ATT_PP):
            kb = _pair(k_ref, a, dkp)
            kbs.append(kb)
            vbs.append(_pair(v_ref, a, LANES))
            kms.append(jnp.concatenate([jnp.where(_head_mask(dkp, hh), kb, jnp.zeros_like(kb)) for hh in range(2)], axis=0))

        def step(i, dbs, masked):
            rs = pl.ds(pl.multiple_of(i * T, T), T)
            new = list(dbs)
            for a in range(ATT_PP):
                qb = _pair(q_ref, a, dkp, rs)
                dob = _pair(do_ref, a, LANES, rs)
                qm2, dom2, p2, ds2 = [], [], [], []
                for hh in range(2):
                    h = 2 * a + hh
                    qm = jnp.where(_head_mask(dkp, hh), qb, jnp.zeros_like(qb))
                    dom = jnp.where(_head_mask(LANES, hh), dob, jnp.zeros_like(dob))
                    s = lax.dot_general(qm, kbs[a], NT, preferred_element_type=F32) * scale
                    if has_b:
                        s = s + b_ref[h, 0]
                    s = s - lse_ref[h, rs, :]
                    if masked:
                        s = jnp.where(row >= col, s, -jnp.inf)
                    p = jnp.exp(s)
                    dp = lax.dot_general(dom, vbs[a], NT, preferred_element_type=F32)
                    ds = p * (dp - delta[h, rs, :])
                    if has_b:
                        new[h] = dbs[h] + jnp.sum(ds, axis=0, keepdims=True)
                    qm2.append(qm)
                    dom2.append(dom)
                    p2.append(p.astype(BF16))
                    ds2.append((ds * scale).astype(BF16))
                dv_acc[:, a * LANES:(a + 1) * LANES] += lax.dot_general(
                    jnp.concatenate(p2, axis=0), jnp.concatenate(dom2, axis=0), TN, preferred_element_type=F32)
                dk_acc[:, a * dkp:(a + 1) * dkp] += lax.dot_general(
                    jnp.concatenate(ds2, axis=0), jnp.concatenate(qm2, axis=0), TN, preferred_element_type=F32)
                dq_acc[rs, a * dkp:(a + 1) * dkp] += lax.dot_general(
                    jnp.concatenate(ds2, axis=1), kms[a], NN, preferred_element_type=F32)
            return tuple(new)

        dbs = step(j, tuple(jnp.zeros((1, T), F32) for _ in ATT_CHAINS), True)
        dbs = lax.fori_loop(j + 1, N_ATT, functools.partial(step, masked=False), dbs)
        if has_b:
            for ci, (a, hh) in enumerate(ATT_CHAINS):
                db_ref[2 * a + hh, 0] = dbs[ci]
        dk_ref[...] = dk_acc[...].astype(dk_ref.dtype)
        dv_ref[...] = dv_acc[...].astype(dv_ref.dtype)

        @pl.when(j == N_ATT - 1)
        def _():
            dq_ref[...] = dq_acc[...].astype(dq_ref.dtype)

    in_specs = [
        pl.BlockSpec((S, ATT_PP * dkp), lambda g, j: (0, qo + g)),
        pl.BlockSpec((T, ATT_PP * dkp), lambda g, j: (j, ko + g)),
        pl.BlockSpec((T, ATT_PP * LANES), lambda g, j: (j, vo + g)),
        pl.BlockSpec((S, ATT_PP * LANES), lambda g, j: (0, g)),
        pl.BlockSpec((2 * ATT_PP, S, 1), lambda g, j: (g, 0, 0)),
        pl.BlockSpec((2 * ATT_PP, S, 1), lambda g, j: (g, 0, 0)),
    ]
    args = [q, k, v, do, lse, delta_in]
    out_specs = [
        pl.BlockSpec((S, ATT_PP * dkp), lambda g, j: (0, g)),
        pl.BlockSpec((T, ATT_PP * dkp), lambda g, j: (j, g)),
        pl.BlockSpec((T, ATT_PP * LANES), lambda g, j: (j, g)),
    ]
    width = (HEADS // 2) * dkp
    out_shape = [
        jax.ShapeDtypeStruct((S, width), qk_dtype),
        jax.ShapeDtypeStruct((S, width), qk_dtype),
        jax.ShapeDtypeStruct((S, HEADS * HEAD_DIM), BF16),
    ]
    if has_b:
        in_specs.append(pl.BlockSpec((2 * ATT_PP, 1, 1, T), lambda g, j: (g, j, 0, 0)))
        args.append(bias)
        out_specs.append(pl.BlockSpec((2 * ATT_PP, 1, 1, T), lambda g, j: (g, j, 0, 0)))
        out_shape.append(jax.ShapeDtypeStruct((HEADS, N_ATT, 1, T), F32))
    return pl.pallas_call(
        body, name=name, grid=(ATT_G, N_ATT),
        in_specs=in_specs, out_specs=out_specs, out_shape=out_shape,
        scratch_shapes=[pltpu.VMEM((S, ATT_PP * dkp), F32), pltpu.VMEM((T, ATT_PP * dkp), F32),
                        pltpu.VMEM((T, ATT_PP * LANES), F32)],
        compiler_params=_params(("parallel", "arbitrary")),
    )(*args)


def _tri(upper):
    a = lax.broadcasted_iota(jnp.int32, (LANES, LANES), 0)
    b = lax.broadcasted_iota(jnp.int32, (LANES, LANES), 1)
    return jnp.where(a <= b if upper else a >= b, 1.0, 0.0).astype(F32)


def _fox_gates(zt, bf):
    def body(z_ref, b_ref, o_ref):
        tri = _tri(True)
        carry = jnp.zeros((HEADS, 1), F32)
        for t in range(S // LANES):
            sl = slice(t * LANES, (t + 1) * LANES)
            z = z_ref[:, sl] + b_ref[...]
            logf = jnp.minimum(z, 0.0) - jnp.log(1.0 + jnp.exp(-jnp.abs(z)))
            c = lax.dot_general(logf, tri, NN, preferred_element_type=F32,
                                precision=lax.Precision.HIGHEST) + carry
            o_ref[:, sl] = -c
            carry = c[:, LANES - 1:LANES]

    return pl.pallas_call(
        body, name="fox_gates", out_shape=jax.ShapeDtypeStruct((HEADS, S), F32),
        compiler_params=_params(),
    )(zt, bf)


def _fox_gates_bwd(dbias, zt, bf):
    def body(d_ref, z_ref, b_ref, dz_ref, dbf_ref):
        tri = _tri(False)
        carry = jnp.zeros((HEADS, 1), F32)
        tot = jnp.zeros((HEADS, 1), F32)
        for t in reversed(range(S // LANES)):
            sl = slice(t * LANES, (t + 1) * LANES)
            df = -d_ref[:, sl]
            c = lax.dot_general(df, tri, NN, preferred_element_type=F32,
                                precision=lax.Precision.HIGHEST) + carry
            carry = c[:, 0:1]
            z = z_ref[:, sl] + b_ref[...]
            dz = c * _sigmoid(-z)
            dz_ref[:, sl] = dz
            tot = tot + jnp.sum(dz, axis=1, keepdims=True)
        dbf_ref[...] = tot

    return pl.pallas_call(
        body, name="fox_gates_bwd",
        out_shape=[jax.ShapeDtypeStruct((HEADS, S), F32), jax.ShapeDtypeStruct((HEADS, 1), F32)],
        compiler_params=_params(),
    )(dbias, zt, bf)


def _mod_part(c_all, w_ada, b_cols):
    def body(c_ref, w_ref, b_ref, o_ref, s_ref):
        c = c_ref[...]
        sc = c * _sigmoid(c)
        s_ref[...] = sc
        o_ref[...] = lax.dot_general(sc, w_ref[...], NN, preferred_element_type=F32,
                                     precision=lax.Precision.HIGHEST) + b_ref[...]

    return pl.pallas_call(
        body, name="mod_part",
        out_shape=[jax.ShapeDtypeStruct((N_DEV, w_ada.shape[1]), F32), jax.ShapeDtypeStruct(c_all.shape, F32)],
        compiler_params=_params(),
    )(c_all, w_ada, b_cols)


def _w_ada_grad(sc_t, dm):
    def body(s_ref, d_ref, o_ref):
        acc = jnp.zeros(o_ref.shape, F32)
        for b in range(N_DEV):
            acc = acc + s_ref[:, b:b + 1] * d_ref[b:b + 1, :]
        o_ref[...] = acc

    return pl.pallas_call(
        body, name="w_ada_grad", out_shape=jax.ShapeDtypeStruct((sc_t.shape[0], dm.shape[1]), F32),
        compiler_params=_params(),
    )(sc_t, dm)


def _adamw(w, m, v, parts, name):
    rows, cols = w.shape
    n = parts.shape[0]
    tr = rows if rows <= 512 else 256

    def body(w_ref, m_ref, v_ref, p_ref, g_out, d_out, m_out, v_out):
        g = p_ref[0].astype(F32)
        for kk in range(1, n):
            g = g + p_ref[kk].astype(F32)
        mm = ADAM_B1 * m_ref[...] + (1.0 - ADAM_B1) * g
        vv = ADAM_B2 * v_ref[...] + (1.0 - ADAM_B2) * (g * g)
        m_hat = mm / (1.0 - ADAM_B1 ** ADAM_STEP)
        v_hat = vv / (1.0 - ADAM_B2 ** ADAM_STEP)
        g_out[...] = g
        d_out[...] = -ADAM_LR * (m_hat / (jnp.sqrt(v_hat) + ADAM_EPS) + ADAM_WD * w_ref[...])
        m_out[...] = mm
        v_out[...] = vv

    spec = pl.BlockSpec((tr, cols), lambda i: (i, 0))
    return pl.pallas_call(
        body, name=name, grid=(rows // tr,),
        in_specs=[spec, spec, spec, pl.BlockSpec((n, tr, cols), lambda i: (0, i, 0))],
        out_specs=[spec] * 4, out_shape=[jax.ShapeDtypeStruct((rows, cols), F32)] * 4,
        compiler_params=_params(("parallel",)),
    )(w, m, v, parts)


def _coords():
    return lax.axis_index("x"), lax.axis_index("y"), lax.axis_index("c")


def _flat(px, py, pc):
    return 4 * px + 2 * py + pc


def _all_gather(arrs, name):
    n = len(arrs)

    def body(*refs):
        ins, outs = refs[:n], refs[n:2 * n]
        send, recv, lsem = refs[2 * n:]
        x, y, c = _coords()
        me, sibling = (x, y, c), (x, y, 1 - c)
        chips = [(1 - x, y), (x, 1 - y), (1 - x, 1 - y)]

        def copy(a, kk, block, to, src=None):
            slot = outs[a].at[_flat(*block)]
            return pltpu.make_async_remote_copy(
                src_ref=slot if src is None else src, dst_ref=slot,
                send_sem=send.at[a, kk], recv_sem=recv.at[a, kk],
                device_id=to, device_id_type=MESH)

        mine = [pltpu.make_async_copy(ins[a], outs[a].at[_flat(*me)], lsem.at[a]) for a in range(n)]
        for cp in mine:
            cp.start()
        first = []
        for a in range(n):
            first.append(copy(a, 0, me, sibling, src=ins[a]))
            first += [copy(a, 1 + j, me, (*chip, c), src=ins[a]) for j, chip in enumerate(chips)]
        for cp in first:
            cp.start()
        passed = []
        for j, chip in enumerate(chips):
            for a in range(n):
                copy(a, 1 + j, (*chip, c), me).wait_recv()
                cp = copy(a, 4 + j, (*chip, c), sibling)
                cp.start()
                passed.append(cp)
        for a in range(n):
            copy(a, 0, sibling, me).wait_recv()
        for j, chip in enumerate(chips):
            for a in range(n):
                copy(a, 4 + j, (*chip, 1 - c), me).wait_recv()
        for cp in first + passed:
            cp.wait_send()
        for cp in mine:
            cp.wait()

    any_spec = pl.BlockSpec(memory_space=pl.ANY)
    return pl.pallas_call(
        body, name=name,
        in_specs=[any_spec] * n, out_specs=[any_spec] * n,
        out_shape=[jax.ShapeDtypeStruct((N_DEV,) + a.shape, a.dtype) for a in arrs],
        scratch_shapes=[pltpu.SemaphoreType.DMA((n, 7)), pltpu.SemaphoreType.DMA((n, 7)),
                        pltpu.SemaphoreType.DMA((n,))],
    )(*arrs)


def _exchange(arrs, name):
    n = len(arrs)

    def body(*refs):
        ins, outs = refs[:n], refs[n:2 * n]
        send, recv, lsem = refs[2 * n:]
        x, y, c = _coords()
        me = _flat(x, y, c)
        peers = []
        for r in range(1, N_DEV):
            px = 1 - x if r & 4 else x
            py = 1 - y if r & 2 else y
            pc = 1 - c if r & 1 else c
            peers.append((px, py, pc))

        def copy(a, kk):
            peer = peers[kk]
            return pltpu.make_async_remote_copy(
                src_ref=ins[a].at[_flat(*peer)], dst_ref=outs[a].at[me],
                send_sem=send.at[a, kk], recv_sem=recv.at[a, kk],
                device_id=peer, device_id_type=MESH)

        def arrival(a, kk):
            slot = outs[a].at[_flat(*peers[kk])]
            return pltpu.make_async_remote_copy(
                src_ref=slot, dst_ref=slot, send_sem=send.at[a, kk], recv_sem=recv.at[a, kk],
                device_id=peers[kk], device_id_type=MESH)

        mine = [pltpu.make_async_copy(ins[a].at[me], outs[a].at[me], lsem.at[a]) for a in range(n)]
        for cp in mine:
            cp.start()
        sends = [copy(a, kk) for a in range(n) for kk in range(N_DEV - 1)]
        for cp in sends:
            cp.start()
        for a in range(n):
            for kk in range(N_DEV - 1):
                arrival(a, kk).wait_recv()
        for cp in sends:
            cp.wait_send()
        for cp in mine:
            cp.wait()

    any_spec = pl.BlockSpec(memory_space=pl.ANY)
    return pl.pallas_call(
        body, name=name,
        in_specs=[any_spec] * n, out_specs=[any_spec] * n,
        out_shape=[jax.ShapeDtypeStruct(a.shape, a.dtype) for a in arrs],
        scratch_shapes=[pltpu.SemaphoreType.DMA((n, 7)), pltpu.SemaphoreType.DMA((n, 7)),
                        pltpu.SemaphoreType.DMA((n,))],
    )(*arrs)


IN_SPLITS = (512, 512, 512, 8, 768, 256, 32, 1024, 1024)


def _unshard_cols(g):
    return jnp.transpose(g, (1, 0, 2)).reshape(g.shape[1], -1)


def _shard_cols(w):
    k = w.shape[0]
    return jnp.transpose(w.reshape(k, N_DEV, -1), (1, 0, 2))


def _prepare_weights(g):
    w_in = _unshard_cols(g["w_in"])
    offs = [0]
    for wd in IN_SPLITS:
        offs.append(offs[-1] + wd)
    fq, fk, fv, wf, cq, ckv, kr, gf, gm = [w_in[:, offs[i]:offs[i + 1]] for i in range(9)]
    z = functools.partial(jnp.zeros, dtype=BF16)
    misc = jnp.concatenate([wf, z((D, 56)), kr, z((D, 32))], axis=1)
    w_uq = g["w_uq"].reshape(Q_LORA, HEADS, 96)
    w_uq = jnp.pad(w_uq, ((0, 0), (0, 0), (0, 32))).reshape(Q_LORA, HEADS * LANES)
    ukv = g["w_ukv"]
    w_k = jnp.transpose(jnp.pad(ukv[:, :, :64], ((0, 0), (0, 0), (0, 64))), (1, 0, 2)).reshape(KV_LORA, HEADS * LANES)
    w_v = jnp.transpose(ukv[:, :, 64:], (1, 0, 2)).reshape(KV_LORA, HEADS * HEAD_DIM)
    return dict(
        w_a=jnp.concatenate([cq, ckv, gf, gm, misc], axis=1),
        w_b=jnp.concatenate([fq, fk, fv], axis=1),
        w_uq=w_uq, w_kv=jnp.concatenate([w_k, w_v], axis=1), w_k=w_k, w_v=w_v,
        w_pf=_unshard_cols(g["w_proj_fox"]), w_pm=_unshard_cols(g["w_proj_mla"]),
        w_out=g["w_out"].reshape(D, D),
        w_ffn_in=_unshard_cols(g["w_ffn_in"]),
        w_ffn_out=g["w_ffn_out"].reshape(D_FF, D),
    )


def _shard_grads(dw):
    da, db_ = dw["w_a"], dw["w_b"]
    misc = da[:, 3072:3200]
    w_in = jnp.concatenate([db_, misc[:, 0:8], da[:, 0:768], da[:, 768:1024], misc[:, 64:96],
                            da[:, 1024:3072]], axis=1)
    w_uq = dw["w_uq"].reshape(Q_LORA, HEADS, LANES)[:, :, :96].reshape(Q_LORA, Q_LORA)
    k_part = dw["w_k"].reshape(KV_LORA, HEADS, LANES)[:, :, :64]
    v_part = dw["w_v"].reshape(KV_LORA, HEADS, HEAD_DIM)
    w_ukv = jnp.transpose(jnp.concatenate([k_part, v_part], axis=2), (1, 0, 2))
    return dict(
        w_in=_shard_cols(w_in), w_uq=w_uq.reshape(N_DEV, Q_LORA // N_DEV, Q_LORA), w_ukv=w_ukv,
        w_proj_fox=_shard_cols(dw["w_pf"]), w_proj_mla=_shard_cols(dw["w_pm"]),
        w_out=dw["w_out"].reshape(N_DEV, D // N_DEV, D),
        w_ffn_in=_shard_cols(dw["w_ffn_in"]),
        w_ffn_out=dw["w_ffn_out"].reshape(N_DEV, D_FF // N_DEV, D),
    )


def _fwd_bwd(x, pos, mod, target, w, vec):
    shift_mix, scale_mix, gate_mix, shift_ffn, scale_ffn, gate_ffn = [mod[:, i * D:(i + 1) * D] for i in range(6)]
    g_pre_mix, g_post_mix, g_pre_ffn, g_post_ffn = vec["g_pre_mix"], vec["g_post_mix"], vec["g_pre_ffn"], vec["g_post_ffn"]
    g_q, g_kv = vec["g_q_lora"], vec["g_kv_lora"]

    inv_freq = 1.0 / (ROPE_THETA ** (jnp.arange(0, ROPE_DIM, 2, dtype=F32) / ROPE_DIM))
    invf = jnp.concatenate([jnp.zeros((64,), F32), inv_freq, inv_freq, jnp.zeros((32,), F32)]).reshape(1, LANES)
    ct, sa, sb = _rope_tables(pos, invf)

    def pre1(xv, g, sc, sh):
        return ((xv * _rstd(xv) * g) * (1.0 + sc) + sh,), ()
    (h,) = _rowwise(pre1, [(x, D, 0)], [g_pre_mix, scale_mix, shift_mix], [(D, BF16)], [], "pre_mix")
    proj_a = _mm(h, w["w_a"], "nn", F32, "in_proj_a")
    qkv = _mm(h, w["w_b"], "nn", BF16, "in_proj_b")

    def lora_norm(cq, ckv, gq, gkv):
        return (cq * _rstd(cq) * gq, ckv * _rstd(ckv) * gkv), ()
    cqn, ckvn = _rowwise(lora_norm, [(proj_a, Q_LORA, 0), (proj_a, KV_LORA, 3)], [g_q, g_kv],
                         [(Q_LORA, BF16), (KV_LORA, BF16)], [], "lora_norm")
    qb = _mm(cqn, w["w_uq"], "nn", F32, "mla_uq")
    kvb = _mm(ckvn, w["w_kv"], "nn", F32, "mla_ukv")

    def mla_rope(qv, kv, vv, misc, c_, a_, b_):
        lane = lax.broadcasted_iota(jnp.int32, (1, LANES), 1)
        kpe = jnp.where((lane >= 64) & (lane < 96), _rope(misc, c_, a_, b_), 0.0)
        qs = [_rope(qv[:, hd * LANES:(hd + 1) * LANES], c_, a_, b_) for hd in range(HEADS)]
        ks = [kv[:, hd * LANES:(hd + 1) * LANES] + kpe for hd in range(HEADS)]
        return (jnp.concatenate(qs, axis=1), jnp.concatenate(ks, axis=1), vv), ()
    q_m, k_m, v_m = _rowwise(
        mla_rope, [(qb, D, 0), (kvb, D, 0), (kvb, 512, 2), (proj_a, LANES, 24), (ct, LANES, 0), (sa, LANES, 0), (sb, LANES, 0)],
        [], [(D, BF16), (D, BF16), (512, BF16)], [], "mla_rope")

    zt = jnp.transpose(proj_a[:, 3072:3080])
    bf = jnp.transpose(vec["b_forget"])
    neg_f = _fox_gates(zt, bf)
    bias = neg_f.reshape(HEADS, N_ATT, 1, ATT_T)
    o_a, lse_a = _attn_fwd(qkv, 0, qkv, 4, qkv, 8, LANES, 1.0 / math.sqrt(HEAD_DIM), bias, "fox_attn")
    o_b, lse_b = _attn_fwd(q_m, 0, k_m, 0, v_m, 0, 2 * LANES, 1.0 / math.sqrt(64 + ROPE_DIM), None, "mla_attn")

    pa = _mm(o_a, w["w_pf"], "nn", F32, "proj_fox")
    pb = _mm(o_b, w["w_pm"], "nn", F32, "proj_mla")

    def merge(gf, gm, pa_, pb_):
        return (_sigmoid(gf) * pa_ + _sigmoid(gm) * pb_,), ()
    (merged,) = _rowwise(merge, [(proj_a, D, 1), (proj_a, D, 2), (pa, D, 0), (pb, D, 0)], [], [(D, BF16)], [], "merge")
    y = _mm(merged, w["w_out"], "nn", F32, "out_proj")

    def post1(xv, yv, gate, gpost, gpre, sc, sh):
        x1 = xv + gate * (yv * _rstd(yv) * gpost)
        return (x1, (x1 * _rstd(x1) * gpre) * (1.0 + sc) + sh), ()
    x1, h2 = _rowwise(post1, [(x, D, 0), (y, D, 0)], [gate_mix, g_post_mix, g_pre_ffn, scale_ffn, shift_ffn],
                      [(D, F32), (D, BF16)], [], "post_mix")
    gu = _mm(h2, w["w_ffn_in"], "nn", F32, "ffn_in")

    def swiglu(g, u):
        return (g * _sigmoid(g) * u,), ()
    (act,) = _rowwise(swiglu, [(gu, D_FF, 0), (gu, D_FF, 1)], [], [(D_FF, BF16)], [], "swiglu")
    y2 = _mm(act, w["w_ffn_out"], "nn", F32, "ffn_out")

    def head(x1v, y2v, tv, gate, gpost):
        r = _rstd(y2v)
        yn = y2v * r
        n2 = yn * gpost
        err = (x1v + gate * n2) - tv
        dx2 = err * (1.0 / D)
        dn2 = dx2 * gate
        dy2 = _norm_bwd(dn2 * gpost, yn, r)
        return (dx2, dy2), (_colsum(err * err), _colsum(dx2 * n2), _colsum(dn2 * yn))
    dx2, dy2, err_cols, d_gate_ffn, d_g_post_ffn = _rowwise(
        head, [(x1, D, 0), (y2, D, 0), (target, D, 0)], [gate_ffn, g_post_ffn], [(D, F32), (D, BF16)], [D, D, D], "loss_head")

    dact = _mm(dy2, w["w_ffn_out"], "nt", F32, "ffn_out_dx")
    dw = {"w_ffn_out": _mm(act, dy2, "tn", BF16, "ffn_out_dw")}

    def swiglu_bwd(g, u, da):
        sg = _sigmoid(g)
        return (jnp.concatenate([da * u * (sg * (1.0 + g * (1.0 - sg))), da * (g * sg)], axis=1),), ()
    (dgu,) = _rowwise(swiglu_bwd, [(gu, D_FF, 0), (gu, D_FF, 1), (dact, D_FF, 0)], [], [(2 * D_FF, BF16)], [], "swiglu_bwd")
    dh2 = _mm(dgu, w["w_ffn_in"], "nt", F32, "ffn_in_dx")
    dw["w_ffn_in"] = _mm(h2, dgu, "tn", BF16, "ffn_in_dw")

    def mid(dh, x1v, dx2v, yv, gpre, sc, gate, gpost):
        r2 = _rstd(x1v)
        x1n = x1v * r2
        t = dh * x1n
        dx1 = dx2v + _norm_bwd(dh * (gpre * (1.0 + sc)), x1n, r2)
        ry = _rstd(yv)
        yn = yv * ry
        dn1 = dx1 * gate
        dy = _norm_bwd(dn1 * gpost, yn, ry)
        sums = (_colsum(dh), _colsum(t) * gpre, _colsum(t) * (1.0 + sc), _colsum(dx1 * (yn * gpost)), _colsum(dn1 * yn))
        return (dx1, dy), sums
    dx1, dy, d_shift_ffn, d_scale_ffn, d_g_pre_ffn, d_gate_mix, d_g_post_mix = _rowwise(
        mid, [(dh2, D, 0), (x1, D, 0), (dx2, D, 0), (y, D, 0)], [g_pre_ffn, scale_ffn, gate_mix, g_post_mix],
        [(D, F32), (D, BF16)], [D] * 5, "mid_bwd")

    dmerged = _mm(dy, w["w_out"], "nt", F32, "out_proj_dx")
    dw["w_out"] = _mm(merged, dy, "tn", BF16, "out_proj_dw")

    def merge_bwd(dm, gf, gm, pa_, pb_):
        sf, sm = _sigmoid(gf), _sigmoid(gm)
        dgates = jnp.concatenate([dm * pa_ * (sf * (1.0 - sf)), dm * pb_ * (sm * (1.0 - sm))], axis=1)
        return (dm * sf, dm * sm, dgates), ()
    dpa, dpb, dgates = _rowwise(
        merge_bwd, [(dmerged, D, 0), (proj_a, D, 1), (proj_a, D, 2), (pa, D, 0), (pb, D, 0)], [],
        [(D, BF16), (D, BF16), (2 * D, BF16)], [], "merge_bwd")
    do_a = _mm(dpa, w["w_pf"], "nt", BF16, "proj_fox_dx")
    do_b = _mm(dpb, w["w_pm"], "nt", BF16, "proj_mla_dx")
    dw["w_pf"] = _mm(o_a, dpa, "tn", BF16, "proj_fox_dw")
    dw["w_pm"] = _mm(o_b, dpb, "tn", BF16, "proj_mla_dw")

    sc_a, sc_b = 1.0 / math.sqrt(HEAD_DIM), 1.0 / math.sqrt(64 + ROPE_DIM)
    delta_a = _attn_delta(qkv, 0, qkv, 4, qkv, 8, do_a, lse_a, LANES, sc_a, bias, "fox_attn_delta")
    delta_b = _attn_delta(q_m, 0, k_m, 0, v_m, 0, do_b, lse_b, 2 * LANES, sc_b, None, "mla_attn_delta")
    dq_a, dk_a, dv_a, dbias = _attn_bwd(qkv, 0, qkv, 4, qkv, 8, do_a, lse_a, delta_a, LANES, sc_a, bias, BF16, "fox_attn_bwd")
    dq_m, dk_m, dv_m = _attn_bwd(q_m, 0, k_m, 0, v_m, 0, do_b, lse_b, delta_b, 2 * LANES, sc_b, None, F32, "mla_attn_bwd")

    def mla_rope_bwd(dq, dk, c_, a_, b_):
        lane = lax.broadcasted_iota(jnp.int32, (1, LANES), 1)
        dqs = [_rope_t(dq[:, hd * LANES:(hd + 1) * LANES], c_, a_, b_) for hd in range(HEADS)]
        dkpe = dk[:, 0:LANES]
        for hd in range(1, HEADS):
            dkpe = dkpe + dk[:, hd * LANES:(hd + 1) * LANES]
        dkpe = jnp.where((lane >= 64) & (lane < 96), dkpe, 0.0)
        dkr = jnp.where((lane >= 64) & (lane < 96), _rope_t(dkpe, c_, a_, b_), 0.0)
        return (jnp.concatenate(dqs, axis=1), dk, dkr), ()
    dqb, dkb, dkr = _rowwise(mla_rope_bwd, [(dq_m, D, 0), (dk_m, D, 0), (ct, LANES, 0), (sa, LANES, 0), (sb, LANES, 0)],
                             [], [(D, BF16), (D, BF16), (LANES, F32)], [], "mla_rope_bwd")
    dcqn = _mm(dqb, w["w_uq"], "nt", F32, "mla_uq_dx")
    dw["w_uq"] = _mm(cqn, dqb, "tn", BF16, "mla_uq_dw")
    dckvn = _mm(dv_m, w["w_v"], "nt", F32, "mla_uv_dx", acc=_mm(dkb, w["w_k"], "nt", F32, "mla_uk_dx"))
    dw["w_k"] = _mm(ckvn, dkb, "tn", BF16, "mla_uk_dw")
    dw["w_v"] = _mm(ckvn, dv_m, "tn", BF16, "mla_uv_dw")

    def lora_norm_bwd(cq, ckv, dq, dkv, gq, gkv):
        rq, rk = _rstd(cq), _rstd(ckv)
        cqh, ckh = cq * rq, ckv * rk
        return (_norm_bwd(dq * gq, cqh, rq), _norm_bwd(dkv * gkv, ckh, rk)), (_colsum(dq * cqh), _colsum(dkv * ckh))
    dcq, dckv, d_g_q, d_g_kv = _rowwise(
        lora_norm_bwd, [(proj_a, Q_LORA, 0), (proj_a, KV_LORA, 3), (dcqn, Q_LORA, 0), (dckvn, KV_LORA, 0)], [g_q, g_kv],
        [(Q_LORA, BF16), (KV_LORA, BF16)], [Q_LORA, KV_LORA], "lora_norm_bwd")

    dzt, d_bf = _fox_gates_bwd(dbias.reshape(HEADS, S), zt, bf)
    dmisc = (dkr + jnp.pad(jnp.transpose(dzt), ((0, 0), (0, LANES - HEADS)))).astype(BF16)
    dproj_a = jnp.concatenate([dcq, dckv, dgates, dmisc], axis=1)
    dqkv = jnp.concatenate([dq_a, dk_a, dv_a], axis=1)
    dh = _mm(dqkv, w["w_b"], "nt", F32, "in_proj_b_dx", acc=_mm(dproj_a, w["w_a"], "nt", F32, "in_proj_a_dx"))
    dw["w_a"] = _mm(h, dproj_a, "tn", BF16, "in_proj_a_dw")
    dw["w_b"] = _mm(h, dqkv, "tn", BF16, "in_proj_b_dw")

    def first(dhv, xv, dx1v, gpre, sc):
        r = _rstd(xv)
        xn = xv * r
        t = dhv * xn
        dx = dx1v + _norm_bwd(dhv * (gpre * (1.0 + sc)), xn, r)
        return (dx,), (_colsum(dhv), _colsum(t) * gpre, _colsum(t) * (1.0 + sc))
    grad_x, d_shift_mix, d_scale_mix, d_g_pre_mix = _rowwise(
        first, [(dh, D, 0), (x, D, 0), (dx1, D, 0)], [g_pre_mix, scale_mix], [(D, F32)], [D] * 3, "pre_mix_bwd")

    dmod = jnp.concatenate([d_shift_mix, d_scale_mix, d_gate_mix, d_shift_ffn, d_scale_ffn, d_gate_ffn], axis=1)
    small = dict(dmod=dmod, g_pre_mix=d_g_pre_mix, g_post_mix=d_g_post_mix, g_pre_ffn=d_g_pre_ffn,
                 g_post_ffn=d_g_post_ffn, g_q_lora=d_g_q, g_kv_lora=d_g_kv,
                 b_forget=jnp.pad(jnp.transpose(d_bf), ((0, 0), (0, LANES - HEADS))), err=err_cols)
    return grad_x, dw, small


SMALL_ORDER = ("dmod", "g_pre_mix", "g_post_mix", "g_pre_ffn", "g_post_ffn", "g_q_lora", "g_kv_lora", "b_forget", "err")
SMALL_PARAM = {"dmod": "b_ada"}
MATRICES = ("w_in", "w_uq", "w_ukv", "w_proj_fox", "w_proj_mla", "w_out", "w_ffn_in", "w_ffn_out")
WEIGHTS = ("w_ada", "b_ada", "g_pre_mix", "g_post_mix", "g_pre_ffn", "g_post_ffn", "w_in", "b_forget", "g_q_lora",
           "w_uq", "g_kv_lora", "w_ukv", "w_proj_fox", "w_proj_mla", "w_out", "w_ffn_in", "w_ffn_out")


def _pad_lanes(v):
    return jnp.pad(v, ((0, 0), (0, (-v.shape[1]) % LANES)))


def kernel(x, c, positions, w_ada, b_ada, g_pre_mix, g_post_mix, g_pre_ffn, g_post_ffn, w_in, b_forget, g_q_lora, w_uq, g_kv_lora, w_ukv, w_proj_fox, w_proj_mla, w_out, w_ffn_in, w_ffn_out, loss_target, m_w_ada, m_b_ada, m_g_pre_mix, m_g_post_mix, m_g_pre_ffn, m_g_post_ffn, m_w_in, m_b_forget, m_g_q_lora, m_w_uq, m_g_kv_lora, m_w_ukv, m_w_proj_fox, m_w_proj_mla, m_w_out, m_w_ffn_in, m_w_ffn_out, v_w_ada, v_b_ada, v_g_pre_mix, v_g_post_mix, v_g_pre_ffn, v_g_post_ffn, v_w_in, v_b_forget, v_g_q_lora, v_w_uq, v_g_kv_lora, v_w_ukv, v_w_proj_fox, v_w_proj_mla, v_w_out, v_w_ffn_in, v_w_ffn_out):
    prm = dict(w_ada=w_ada, b_ada=b_ada, g_pre_mix=g_pre_mix, g_post_mix=g_post_mix, g_pre_ffn=g_pre_ffn,
               g_post_ffn=g_post_ffn, w_in=w_in, b_forget=b_forget, g_q_lora=g_q_lora, w_uq=w_uq, g_kv_lora=g_kv_lora,
               w_ukv=w_ukv, w_proj_fox=w_proj_fox, w_proj_mla=w_proj_mla, w_out=w_out, w_ffn_in=w_ffn_in, w_ffn_out=w_ffn_out)
    mom = dict(w_ada=m_w_ada, b_ada=m_b_ada, g_pre_mix=m_g_pre_mix, g_post_mix=m_g_post_mix, g_pre_ffn=m_g_pre_ffn,
               g_post_ffn=m_g_post_ffn, w_in=m_w_in, b_forget=m_b_forget, g_q_lora=m_g_q_lora, w_uq=m_w_uq,
               g_kv_lora=m_g_kv_lora, w_ukv=m_w_ukv, w_proj_fox=m_w_proj_fox, w_proj_mla=m_w_proj_mla, w_out=m_w_out,
               w_ffn_in=m_w_ffn_in, w_ffn_out=m_w_ffn_out)
    var = dict(w_ada=v_w_ada, b_ada=v_b_ada, g_pre_mix=v_g_pre_mix, g_post_mix=v_g_post_mix, g_pre_ffn=v_g_pre_ffn,
               g_post_ffn=v_g_post_ffn, w_in=v_w_in, b_forget=v_b_forget, g_q_lora=v_g_q_lora, w_uq=v_w_uq,
               g_kv_lora=v_g_kv_lora, w_ukv=v_w_ukv, w_proj_fox=v_w_proj_fox, w_proj_mla=v_w_proj_mla, w_out=v_w_out,
               w_ffn_in=v_w_ffn_in, w_ffn_out=v_w_ffn_out)
    me = _flat(*_coords())

    locals_bf16 = [prm[n][0].astype(BF16) for n in MATRICES]
    gathered = _all_gather(locals_bf16 + [c], "gather_weights")
    w = _prepare_weights(dict(zip(MATRICES, gathered[:-1])))
    c_all = gathered[-1].reshape(N_DEV, D)

    ada_cols = w_ada.shape[2]
    b_cols = lax.dynamic_slice(b_ada, (0, me * ada_cols), (1, ada_cols))
    mod_cols, silu_c = _mod_part(c_all, w_ada[0], b_cols)
    (mod_all,) = _all_gather([mod_cols], "gather_mod")
    mod = lax.dynamic_index_in_dim(mod_all, me, axis=1, keepdims=False).reshape(1, 6 * D)

    vec = dict(g_pre_mix=g_pre_mix, g_post_mix=g_post_mix, g_pre_ffn=g_pre_ffn, g_post_ffn=g_post_ffn,
               g_q_lora=g_q_lora, g_kv_lora=g_kv_lora, b_forget=b_forget)
    pos = positions.astype(F32).reshape(S, 1)
    grad_x, dw, small = _fwd_bwd(x[0], pos, mod, loss_target[0], w, vec)

    bundle = jnp.concatenate([small[n] for n in SMALL_ORDER], axis=1)
    (bundle_all,) = _all_gather([bundle], "gather_small")
    bundle_all = bundle_all.reshape(N_DEV, 1, -1)
    shards = _shard_grads(dw)
    parts = _exchange([shards[n] for n in MATRICES], "exchange_grads")

    out = {}
    for n, p in zip(MATRICES, parts):
        out[n] = _adamw(prm[n][0], mom[n][0], var[n][0], p, "adamw_" + n)

    dmod_all = bundle_all[:, 0, :6 * D]
    dm_cols = lax.dynamic_slice(dmod_all, (0, me * ada_cols), (N_DEV, ada_cols))
    g_ada = _w_ada_grad(jnp.transpose(silu_c), dm_cols)
    out["w_ada"] = _adamw(w_ada[0], m_w_ada[0], v_w_ada[0], g_ada[None], "adamw_w_ada")

    def row(src, n):
        if n == "err":
            return jnp.zeros((1, D), F32)
        return _pad_lanes(src[SMALL_PARAM.get(n, n)])
    wv, mv, vv = [jnp.concatenate([row(src, n) for n in SMALL_ORDER], axis=1) for src in (prm, mom, var)]
    gs, ds, ms, vs = _adamw(wv, mv, vv, bundle_all, "adamw_small")
    off = 0
    for n in SMALL_ORDER:
        width = small[n].shape[1]
        if n != "err":
            pn = SMALL_PARAM.get(n, n)
            real = prm[pn].shape[1]
            out[pn] = tuple(t[:, off:off + real] for t in (gs, ds, ms, vs))
        else:
            loss = 0.5 * jnp.sum(gs[0, off:off + width]) / D
        off += width

    res = [loss, grad_x[None]]
    for kind in range(4):
        for n in WEIGHTS:
            t = out[n][kind]
            res.append(t[None] if prm[n].ndim == 3 else t)
    return tuple(res)
```

```python
import functools
import math

import jax
import jax.numpy as jnp
from jax import lax
from jax.experimental import pallas as pl
from jax.experimental.pallas import tpu as pltpu

F32 = jnp.float32
BF16 = jnp.bfloat16

N_DEV = 8
S = 2048
D = 1024
D_FF = 2816
HEADS = 8
HEAD_DIM = 64
Q_LORA = 768
KV_LORA = 256
ROPE_DIM = 32
ROPE_THETA = 10000.0
NORM_EPS = 1e-6
LANES = 128
VMEM_LIMIT = 56 * 1024 * 1024

ADAM_LR = 0.001
ADAM_B1 = 0.9
ADAM_B2 = 0.999
ADAM_EPS = 1e-08
ADAM_WD = 0.01
ADAM_STEP = 10

ATT_T = 256
N_ATT = S // ATT_T

NN = (((1,), (0,)), ((), ()))
NT = (((1,), (1,)), ((), ()))
TN = (((0,), (0,)), ((), ()))
MESH = pl.DeviceIdType.MESH


def _params(sem=None):
    return pltpu.CompilerParams(dimension_semantics=sem, vmem_limit_bytes=VMEM_LIMIT)


def _pick(n, cap):
    best = None
    for t in range(LANES, cap + 1, LANES):
        if n % t == 0:
            best = t
    return best if best is not None else n


def _mm(a, b, mode, out_dtype, name, acc=None):
    if mode == "nn":
        (m, k), (k2, n), dn = a.shape, b.shape, NN
    elif mode == "nt":
        (m, k), (n, k2), dn = a.shape, b.shape, NT
    else:
        (k, m), (k2, n), dn = a.shape, b.shape, TN
    assert k == k2, (a.shape, b.shape, mode)
    tn = _pick(n, 640)
    tm = _pick(m, 1024)
    osz = jnp.dtype(out_dtype).itemsize

    def need(tm_):
        blk = tm_ * k * 2 + tn * k * 2 + tm_ * tn * osz + (tm_ * tn * 4 if acc is not None else 0)
        return 2 * blk + tm_ * tn * 4
    while need(tm) > 36 * 1024 * 1024 and tm % 256 == 0:
        tm //= 2

    def body(*refs):
        if acc is not None:
            a_ref, b_ref, c_ref, o_ref = refs
        else:
            a_ref, b_ref, o_ref = refs
        r = lax.dot_general(a_ref[...], b_ref[...], dn, preferred_element_type=F32)
        if acc is not None:
            r = r + c_ref[...]
        o_ref[...] = r.astype(o_ref.dtype)

    if mode == "tn":
        a_spec = pl.BlockSpec((k, tm), lambda i, j: (0, i))
    else:
        a_spec = pl.BlockSpec((tm, k), lambda i, j: (i, 0))
    if mode == "nt":
        b_spec = pl.BlockSpec((tn, k), lambda i, j: (j, 0))
    else:
        b_spec = pl.BlockSpec((k, tn), lambda i, j: (0, j))
    o_spec = pl.BlockSpec((tm, tn), lambda i, j: (i, j))
    in_specs = [a_spec, b_spec] + ([o_spec] if acc is not None else [])
    args = (a, b) + ((acc,) if acc is not None else ())
    return pl.pallas_call(
        body, name=name, grid=(m // tm, n // tn),
        in_specs=in_specs, out_specs=o_spec,
        out_shape=jax.ShapeDtypeStruct((m, n), out_dtype),
        compiler_params=_params(("parallel", "parallel")),
    )(*args)


def _rowwise(fn, row_ins, vec_ins, row_outs, sum_outs, name, tm=256):
    n_in = len(row_ins) + len(vec_ins)
    n_o = len(row_outs)
    rows = row_ins[0][0].shape[0]

    def body(*refs):
        vals = [r[...] for r in refs[:n_in]]
        outs = refs[n_in:]
        ro, so = fn(*vals)
        assert len(ro) == n_o and len(so) == len(sum_outs)
        for r, v in zip(outs[:n_o], ro):
            r[...] = v.astype(r.dtype)
        if sum_outs:
            @pl.when(pl.program_id(0) == 0)
            def _():
                for r in outs[n_o:]:
                    r[...] = jnp.zeros(r.shape, F32)
            for r, v in zip(outs[n_o:], so):
                r[...] += v

    in_specs = [pl.BlockSpec((tm, w), functools.partial(lambda i, b: (i, b), b=b)) for _, w, b in row_ins]
    in_specs += [pl.BlockSpec(v.shape, lambda i: (0, 0)) for v in vec_ins]
    out_specs = [pl.BlockSpec((tm, w), lambda i: (i, 0)) for w, _ in row_outs]
    out_specs += [pl.BlockSpec((1, w), lambda i: (0, 0)) for w in sum_outs]
    out_shape = [jax.ShapeDtypeStruct((rows, w), dt) for w, dt in row_outs]
    out_shape += [jax.ShapeDtypeStruct((1, w), F32) for w in sum_outs]
    return pl.pallas_call(
        body, name=name, grid=(rows // tm,),
        in_specs=in_specs, out_specs=out_specs, out_shape=out_shape,
        compiler_params=_params(("arbitrary",)),
    )(*[a for a, _, _ in row_ins], *vec_ins)


def _sigmoid(x):
    return 1.0 / (1.0 + jnp.exp(-x))


def _rstd(x):
    return lax.rsqrt(jnp.mean(x * x, axis=-1, keepdims=True) + NORM_EPS)


def _norm_bwd(dyn, xn, r):
    return r * (dyn - xn * jnp.mean(dyn * xn, axis=-1, keepdims=True))


def _colsum(x):
    return jnp.sum(x, axis=0, keepdims=True)


def _rope_tables(pos, invf):
    def fn(p, f):
        lane = lax.broadcasted_iota(jnp.int32, (1, LANES), 1)
        ang = p * f
        cs, sn = jnp.cos(ang), jnp.sin(ang)
        rot = (lane >= 64) & (lane < 96)
        ct = jnp.where(lane < 64, 1.0, jnp.where(rot, cs, 0.0))
        sa = jnp.where((lane >= 64) & (lane < 80), -sn, 0.0)
        sb = jnp.where((lane >= 80) & (lane < 96), sn, 0.0)
        return (ct, sa, sb), ()
    return _rowwise(fn, [(pos, 1, 0)], [invf], [(LANES, F32)] * 3, [], "rope_tables")


def _rope(x, ct, sa, sb):
    return x * ct + pltpu.roll(x, LANES - 16, 1) * sa + pltpu.roll(x, 16, 1) * sb


def _rope_t(x, ct, sa, sb):
    return x * ct - pltpu.roll(x, LANES - 16, 1) * sa - pltpu.roll(x, 16, 1) * sb


def _head_mask(width, hh):
    lane = lax.broadcasted_iota(jnp.int32, (1, width), 1)
    half = width // 2
    return (lane >= hh * half) & (lane < (hh + 1) * half)


ATT_PP = 2
ATT_CHAINS = [(a, hh) for a in range(ATT_PP) for hh in range(2)]
ATT_G = HEADS // (2 * ATT_PP)


def _pair(ref_or_val, a, width, rows=slice(None)):
    return ref_or_val[rows, a * width:(a + 1) * width]


def _attn_fwd(q, qo, k, ko, v, vo, dkp, scale, bias, name):
    T = ATT_T
    assert qo % ATT_PP == 0 and ko % ATT_PP == 0 and vo % ATT_PP == 0
    qo, ko, vo = qo // ATT_PP, ko // ATT_PP, vo // ATT_PP

    def body(*refs):
        if bias is not None:
            q_ref, k_ref, v_ref, b_ref, o_ref, lse_ref = refs
        else:
            q_ref, k_ref, v_ref, o_ref, lse_ref = refs
        i = pl.program_id(1)
        row = lax.broadcasted_iota(jnp.int32, (T, T), 0)
        col = lax.broadcasted_iota(jnp.int32, (T, T), 1)
        qms = []
        for a, hh in ATT_CHAINS:
            qb = _pair(q_ref, a, dkp)
            qms.append(jnp.where(_head_mask(dkp, hh), qb, jnp.zeros_like(qb)))

        def step(j, carry, masked):
            ks = pl.ds(pl.multiple_of(j * T, T), T)
            new = []
            for ci, (a, hh) in enumerate(ATT_CHAINS):
                m, l, acc = carry[ci]
                s = lax.dot_general(qms[ci], _pair(k_ref, a, dkp, ks), NT, preferred_element_type=F32) * scale
                if bias is not None:
                    s = s + b_ref[2 * a + hh, j]
                if masked:
                    s = jnp.where(row >= col, s, -jnp.inf)
                m_new = jnp.maximum(m, jnp.max(s, axis=1, keepdims=True))
                alpha = jnp.exp(m - m_new)
                p = jnp.exp(s - m_new)
                l = alpha * l + jnp.sum(p, axis=1, keepdims=True)
                acc = alpha * acc + lax.dot_general(p.astype(BF16), _pair(v_ref, a, LANES, ks), NN,
                                                    preferred_element_type=F32)
                new.append((m_new, l, acc))
            return tuple(new)

        init = tuple((jnp.full((T, 1), -jnp.inf, F32), jnp.zeros((T, 1), F32), jnp.zeros((T, LANES), F32))
                     for _ in ATT_CHAINS)
        carry = lax.fori_loop(0, i, functools.partial(step, masked=False), init)
        carry = step(i, carry, True)
        for a in range(ATT_PP):
            (m0, l0, acc0), (m1, l1, acc1) = carry[2 * a], carry[2 * a + 1]
            lse_ref[2 * a] = m0 + jnp.log(l0)
            lse_ref[2 * a + 1] = m1 + jnp.log(l1)
            o_ref[:, a * LANES:(a + 1) * LANES] = jnp.where(_head_mask(LANES, 0), acc0 / l0, acc1 / l1).astype(o_ref.dtype)

    in_specs = [
        pl.BlockSpec((T, ATT_PP * dkp), lambda g, i: (i, qo + g)),
        pl.BlockSpec((S, ATT_PP * dkp), lambda g, i: (0, ko + g)),
        pl.BlockSpec((S, ATT_PP * LANES), lambda g, i: (0, vo + g)),
    ]
    args = [q, k, v]
    if bias is not None:
        in_specs.append(pl.BlockSpec((2 * ATT_PP, N_ATT, 1, T), lambda g, i: (g, 0, 0, 0)))
        args.append(bias)
    return pl.pallas_call(
        body, name=name, grid=(ATT_G, N_ATT),
        in_specs=in_specs,
        out_specs=[pl.BlockSpec((T, ATT_PP * LANES), lambda g, i: (i, g)),
                   pl.BlockSpec((2 * ATT_PP, T, 1), lambda g, i: (g, i, 0))],
        out_shape=[jax.ShapeDtypeStruct((S, HEADS * HEAD_DIM), BF16),
                   jax.ShapeDtypeStruct((HEADS, S, 1), F32)],
        compiler_params=_params(("parallel", "arbitrary")),
    )(*args)


def _attn_delta(q, qo, k, ko, v, vo, do, lse, dkp, scale, bias, name):
    T = ATT_T
    qo, ko, vo = qo // ATT_PP, ko // ATT_PP, vo // ATT_PP

    def body(*refs):
        q_ref, k_ref, v_ref, do_ref, lse_ref = refs[:5]
        b_ref = refs[5] if bias is not None else None
        out_ref = refs[-1]
        i = pl.program_id(1)
        row = lax.broadcasted_iota(jnp.int32, (T, T), 0)
        col = lax.broadcasted_iota(jnp.int32, (T, T), 1)
        qms, doms, lses = [], [], []
        for a, hh in ATT_CHAINS:
            qb, dob = _pair(q_ref, a, dkp), _pair(do_ref, a, LANES)
            qms.append(jnp.where(_head_mask(dkp, hh), qb, jnp.zeros_like(qb)))
            doms.append(jnp.where(_head_mask(LANES, hh), dob, jnp.zeros_like(dob)))
            lses.append(lse_ref[2 * a + hh])

        def step(j, carry, masked):
            ks = pl.ds(pl.multiple_of(j * T, T), T)
            new = []
            for ci, (a, hh) in enumerate(ATT_CHAINS):
                s = lax.dot_general(qms[ci], _pair(k_ref, a, dkp, ks), NT, preferred_element_type=F32) * scale
                if bias is not None:
                    s = s + b_ref[2 * a + hh, j]
                s = s - lses[ci]
                if masked:
                    s = jnp.where(row >= col, s, -jnp.inf)
                dp = lax.dot_general(doms[ci], _pair(v_ref, a, LANES, ks), NT, preferred_element_type=F32)
                new.append(carry[ci] + jnp.sum(jnp.exp(s) * dp, axis=1, keepdims=True))
            return tuple(new)

        init = tuple(jnp.zeros((T, 1), F32) for _ in ATT_CHAINS)
        acc = step(i, lax.fori_loop(0, i, functools.partial(step, masked=False), init), True)
        for ci, (a, hh) in enumerate(ATT_CHAINS):
            out_ref[2 * a + hh] = acc[ci]

    in_specs = [
        pl.BlockSpec((T, ATT_PP * dkp), lambda g, i: (i, qo + g)),
        pl.BlockSpec((S, ATT_PP * dkp), lambda g, i: (0, ko + g)),
        pl.BlockSpec((S, ATT_PP * LANES), lambda g, i: (0, vo + g)),
        pl.BlockSpec((T, ATT_PP * LANES), lambda g, i: (i, g)),
        pl.BlockSpec((2 * ATT_PP, T, 1), lambda g, i: (g, i, 0)),
    ]
    args = [q, k, v, do, lse]
    if bias is not None:
        in_specs.append(pl.BlockSpec((2 * ATT_PP, N_ATT, 1, T), lambda g, i: (g, 0, 0, 0)))
        args.append(bias)
    return pl.pallas_call(
        body, name=name, grid=(ATT_G, N_ATT),
        in_specs=in_specs, out_specs=pl.BlockSpec((2 * ATT_PP, T, 1), lambda g, i: (g, i, 0)),
        out_shape=jax.ShapeDtypeStruct((HEADS, S, 1), F32),
        compiler_params=_params(("parallel", "arbitrary")),
    )(*args)


def _attn_bwd(q, qo, k, ko, v, vo, do, lse, delta_in, dkp, scale, bias, qk_dtype, name):
    T = ATT_T
    has_b = bias is not None
    qo, ko, vo = qo // ATT_PP, ko // ATT_PP, vo // ATT_PP

    def body(*refs):
        q_ref, k_ref, v_ref, do_ref, lse_ref, delta = refs[:6]
        refs = refs[6:]
        if has_b:
            b_ref, refs = refs[0], refs[1:]
        dq_ref, dk_ref, dv_ref = refs[:3]
        refs = refs[3:]
        if has_b:
            db_ref, refs = refs[0], refs[1:]
        dq_acc, dk_acc, dv_acc = refs
        j = pl.program_id(1)

        @pl.when(j == 0)
        def _():
            dq_acc[...] = jnp.zeros(dq_acc.shape, F32)

        dk_acc[...] = jnp.zeros(dk_acc.shape, F32)
        dv_acc[...] = jnp.zeros(dv_acc.shape, F32)
        row = lax.broadcasted_iota(jnp.int32, (T, T), 0)
        col = lax.broadcasted_iota(jnp.int32, (T, T), 1)
        kbs, vbs, kms = [], [], []
        for a in range(ATT_PP):
            kb = _pair(k_ref, a, dkp)
            kbs.append(kb)
            vbs.append(_pair(v_ref, a, LANES))
            kms.append(jnp.concatenate([jnp.where(_head_mask(dkp, hh), kb, jnp.zeros_like(kb)) for hh in range(2)], axis=0))

        def step(i, dbs, masked):
            rs = pl.ds(pl.multiple_of(i * T, T), T)
            new = list(dbs)
            for a in range(ATT_PP):
                qb = _pair(q_ref, a, dkp, rs)
                dob = _pair(do_ref, a, LANES, rs)
                qm2, dom2, p2, ds2 = [], [], [], []
                for hh in range(2):
                    h = 2 * a + hh
                    qm = jnp.where(_head_mask(dkp, hh), qb, jnp.zeros_like(qb))
                    dom = jnp.where(_head_mask(LANES, hh), dob, jnp.zeros_like(dob))
                    s = lax.dot_general(qm, kbs[a], NT, preferred_element_type=F32) * scale
                    if has_b:
                        s = s + b_ref[h, 0]
                    s = s - lse_ref[h, rs, :]
                    if masked:
                        s = jnp.where(row >= col, s, -jnp.inf)
                    p = jnp.exp(s)
                    dp = lax.dot_general(dom, vbs[a], NT, preferred_element_type=F32)
                    ds = p * (dp - delta[h, rs, :])
                    if has_b:
                        new[h] = dbs[h] + jnp.sum(ds, axis=0, keepdims=True)
                    qm2.append(qm)
                    dom2.append(dom)
                    p2.append(p.astype(BF16))
                    ds2.append((ds * scale).astype(BF16))
                dv_acc[:, a * LANES:(a + 1) * LANES] += lax.dot_general(
                    jnp.concatenate(p2, axis=0), jnp.concatenate(dom2, axis=0), TN, preferred_element_type=F32)
                dk_acc[:, a * dkp:(a + 1) * dkp] += lax.dot_general(
                    jnp.concatenate(ds2, axis=0), jnp.concatenate(qm2, axis=0), TN, preferred_element_type=F32)
                dq_acc[rs, a * dkp:(a + 1) * dkp] += lax.dot_general(
                    jnp.concatenate(ds2, axis=1), kms[a], NN, preferred_element_type=F32)
            return tuple(new)

        dbs = step(j, tuple(jnp.zeros((1, T), F32) for _ in ATT_CHAINS), True)
        dbs = lax.fori_loop(j + 1, N_ATT, functools.partial(step, masked=False), dbs)
        if has_b:
            for ci, (a, hh) in enumerate(ATT_CHAINS):
                db_ref[2 * a + hh, 0] = dbs[ci]
        dk_ref[...] = dk_acc[...].astype(dk_ref.dtype)
        dv_ref[...] = dv_acc[...].astype(dv_ref.dtype)

        @pl.when(j == N_ATT - 1)
        def _():
            dq_ref[...] = dq_acc[...].astype(dq_ref.dtype)

    in_specs = [
        pl.BlockSpec((S, ATT_PP * dkp), lambda g, j: (0, qo + g)),
        pl.BlockSpec((T, ATT_PP * dkp), lambda g, j: (j, ko + g)),
        pl.BlockSpec((T, ATT_PP * LANES), lambda g, j: (j, vo + g)),
        pl.BlockSpec((S, ATT_PP * LANES), lambda g, j: (0, g)),
        pl.BlockSpec((2 * ATT_PP, S, 1), lambda g, j: (g, 0, 0)),
        pl.BlockSpec((2 * ATT_PP, S, 1), lambda g, j: (g, 0, 0)),
    ]
    args = [q, k, v, do, lse, delta_in]
    out_specs = [
        pl.BlockSpec((S, ATT_PP * dkp), lambda g, j: (0, g)),
        pl.BlockSpec((T, ATT_PP * dkp), lambda g, j: (j, g)),
        pl.BlockSpec((T, ATT_PP * LANES), lambda g, j: (j, g)),
    ]
    width = (HEADS // 2) * dkp
    out_shape = [
        jax.ShapeDtypeStruct((S, width), qk_dtype),
        jax.ShapeDtypeStruct((S, width), qk_dtype),
        jax.ShapeDtypeStruct((S, HEADS * HEAD_DIM), BF16),
    ]
    if has_b:
        in_specs.append(pl.BlockSpec((2 * ATT_PP, 1, 1, T), lambda g, j: (g, j, 0, 0)))
        args.append(bias)
        out_specs.append(pl.BlockSpec((2 * ATT_PP, 1, 1, T), lambda g, j: (g, j, 0, 0)))
        out_shape.append(jax.ShapeDtypeStruct((HEADS, N_ATT, 1, T), F32))
    return pl.pallas_call(
        body, name=name, grid=(ATT_G, N_ATT),
        in_specs=in_specs, out_specs=out_specs, out_shape=out_shape,
        scratch_shapes=[pltpu.VMEM((S, ATT_PP * dkp), F32), pltpu.VMEM((T, ATT_PP * dkp), F32),
                        pltpu.VMEM((T, ATT_PP * LANES), F32)],
        compiler_params=_params(("parallel", "arbitrary")),
    )(*args)


def _tri(upper):
    a = lax.broadcasted_iota(jnp.int32, (LANES, LANES), 0)
    b = lax.broadcasted_iota(jnp.int32, (LANES, LANES), 1)
    return jnp.where(a <= b if upper else a >= b, 1.0, 0.0).astype(F32)


def _fox_gates(zt, bf):
    def body(z_ref, b_ref, o_ref):
        tri = _tri(True)
        carry = jnp.zeros((HEADS, 1), F32)
        for t in range(S // LANES):
            sl = slice(t * LANES, (t + 1) * LANES)
            z = z_ref[:, sl] + b_ref[...]
            logf = jnp.minimum(z, 0.0) - jnp.log(1.0 + jnp.exp(-jnp.abs(z)))
            c = lax.dot_general(logf, tri, NN, preferred_element_type=F32,
                                precision=lax.Precision.HIGHEST) + carry
            o_ref[:, sl] = -c
            carry = c[:, LANES - 1:LANES]

    return pl.pallas_call(
        body, name="fox_gates", out_shape=jax.ShapeDtypeStruct((HEADS, S), F32),
        compiler_params=_params(),
    )(zt, bf)


def _fox_gates_bwd(dbias, zt, bf):
    def body(d_ref, z_ref, b_ref, dz_ref, dbf_ref):
        tri = _tri(False)
        carry = jnp.zeros((HEADS, 1), F32)
        tot = jnp.zeros((HEADS, 1), F32)
        for t in reversed(range(S // LANES)):
            sl = slice(t * LANES, (t + 1) * LANES)
            df = -d_ref[:, sl]
            c = lax.dot_general(df, tri, NN, preferred_element_type=F32,
                                precision=lax.Precision.HIGHEST) + carry
            carry = c[:, 0:1]
            z = z_ref[:, sl] + b_ref[...]
            dz = c * _sigmoid(-z)
            dz_ref[:, sl] = dz
            tot = tot + jnp.sum(dz, axis=1, keepdims=True)
        dbf_ref[...] = tot

    return pl.pallas_call(
        body, name="fox_gates_bwd",
        out_shape=[jax.ShapeDtypeStruct((HEADS, S), F32), jax.ShapeDtypeStruct((HEADS, 1), F32)],
        compiler_params=_params(),
    )(dbias, zt, bf)


def _mod_part(c_all, w_ada, b_cols):
    def body(c_ref, w_ref, b_ref, o_ref, s_ref):
        c = c_ref[...]
        sc = c * _sigmoid(c)
        s_ref[...] = sc
        o_ref[...] = lax.dot_general(sc, w_ref[...], NN, preferred_element_type=F32,
                                     precision=lax.Precision.HIGHEST) + b_ref[...]

    return pl.pallas_call(
        body, name="mod_part",
        out_shape=[jax.ShapeDtypeStruct((N_DEV, w_ada.shape[1]), F32), jax.ShapeDtypeStruct(c_all.shape, F32)],
        compiler_params=_params(),
    )(c_all, w_ada, b_cols)


def _w_ada_grad(sc_t, dm):
    def body(s_ref, d_ref, o_ref):
        acc = jnp.zeros(o_ref.shape, F32)
        for b in range(N_DEV):
            acc = acc + s_ref[:, b:b + 1] * d_ref[b:b + 1, :]
        o_ref[...] = acc

    return pl.pallas_call(
        body, name="w_ada_grad", out_shape=jax.ShapeDtypeStruct((sc_t.shape[0], dm.shape[1]), F32),
        compiler_params=_params(),
    )(sc_t, dm)


def _adamw(w, m, v, parts, name):
    rows, cols = w.shape
    n = parts.shape[0]
    tr = rows if rows <= 512 else 256

    def body(w_ref, m_ref, v_ref, p_ref, g_out, d_out, m_out, v_out):
        g = p_ref[0].astype(F32)
        for kk in range(1, n):
            g = g + p_ref[kk].astype(F32)
        mm = ADAM_B1 * m_ref[...] + (1.0 - ADAM_B1) * g
        vv = ADAM_B2 * v_ref[...] + (1.0 - ADAM_B2) * (g * g)
        m_hat = mm / (1.0 - ADAM_B1 ** ADAM_STEP)
        v_hat = vv / (1.0 - ADAM_B2 ** ADAM_STEP)
        g_out[...] = g
        d_out[...] = -ADAM_LR * (m_hat / (jnp.sqrt(v_hat) + ADAM_EPS) + ADAM_WD * w_ref[...])
        m_out[...] = mm
        v_out[...] = vv

    spec = pl.BlockSpec((tr, cols), lambda i: (i, 0))
    return pl.pallas_call(
        body, name=name, grid=(rows // tr,),
        in_specs=[spec, spec, spec, pl.BlockSpec((n, tr, cols), lambda i: (0, i, 0))],
        out_specs=[spec] * 4, out_shape=[jax.ShapeDtypeStruct((rows, cols), F32)] * 4,
        compiler_params=_params(("parallel",)),
    )(w, m, v, parts)


def _coords():
    return lax.axis_index("x"), lax.axis_index("y"), lax.axis_index("c")


def _flat(px, py, pc):
    return 4 * px + 2 * py + pc


def _all_gather(arrs, name):
    n = len(arrs)

    def body(*refs):
        ins, outs = refs[:n], refs[n:2 * n]
        send, recv, lsem = refs[2 * n:]
        x, y, c = _coords()
        me, sibling = (x, y, c), (x, y, 1 - c)
        chips = [(1 - x, y), (x, 1 - y), (1 - x, 1 - y)]

        def copy(a, kk, block, to, src=None):
            slot = outs[a].at[_flat(*block)]
            return pltpu.make_async_remote_copy(
                src_ref=slot if src is None else src, dst_ref=slot,
                send_sem=send.at[a, kk], recv_sem=recv.at[a, kk],
                device_id=to, device_id_type=MESH)

        mine = [pltpu.make_async_copy(ins[a], outs[a].at[_flat(*me)], lsem.at[a]) for a in range(n)]
        for cp in mine:
            cp.start()
        first = []
        for a in range(n):
            first.append(copy(a, 0, me, sibling, src=ins[a]))
            first += [copy(a, 1 + j, me, (*chip, c), src=ins[a]) for j, chip in enumerate(chips)]
        for cp in first:
            cp.start()
        passed = []
        for j, chip in enumerate(chips):
            for a in range(n):
                copy(a, 1 + j, (*chip, c), me).wait_recv()
                cp = copy(a, 4 + j, (*chip, c), sibling)
                cp.start()
                passed.append(cp)
        for a in range(n):
            copy(a, 0, sibling, me).wait_recv()
        for j, chip in enumerate(chips):
            for a in range(n):
                copy(a, 4 + j, (*chip, 1 - c), me).wait_recv()
        for cp in first + passed:
            cp.wait_send()
        for cp in mine:
            cp.wait()

    any_spec = pl.BlockSpec(memory_space=pl.ANY)
    return pl.pallas_call(
        body, name=name,
        in_specs=[any_spec] * n, out_specs=[any_spec] * n,
        out_shape=[jax.ShapeDtypeStruct((N_DEV,) + a.shape, a.dtype) for a in arrs],
        scratch_shapes=[pltpu.SemaphoreType.DMA((n, 7)), pltpu.SemaphoreType.DMA((n, 7)),
                        pltpu.SemaphoreType.DMA((n,))],
    )(*arrs)


def _exchange(arrs, name):
    n = len(arrs)

    def body(*refs):
        ins, outs = refs[:n], refs[n:2 * n]
        send, recv, lsem = refs[2 * n:]
        x, y, c = _coords()
        me = _flat(x, y, c)
        peers = []
        for r in range(1, N_DEV):
            px = 1 - x if r & 4 else x
            py = 1 - y if r & 2 else y
            pc = 1 - c if r & 1 else c
            peers.append((px, py, pc))

        def copy(a, kk):
            peer = peers[kk]
            return pltpu.make_async_remote_copy(
                src_ref=ins[a].at[_flat(*peer)], dst_ref=outs[a].at[me],
                send_sem=send.at[a, kk], recv_sem=recv.at[a, kk],
                device_id=peer, device_id_type=MESH)

        def arrival(a, kk):
            slot = outs[a].at[_flat(*peers[kk])]
            return pltpu.make_async_remote_copy(
                src_ref=slot, dst_ref=slot, send_sem=send.at[a, kk], recv_sem=recv.at[a, kk],
                device_id=peers[kk], device_id_type=MESH)

        mine = [pltpu.make_async_copy(ins[a].at[me], outs[a].at[me], lsem.at[a]) for a in range(n)]
        for cp in mine:
            cp.start()
        sends = [copy(a, kk) for a in range(n) for kk in range(N_DEV - 1)]
        for cp in sends:
            cp.start()
        for a in range(n):
            for kk in range(N_DEV - 1):
                arrival(a, kk).wait_recv()
        for cp in sends:
            cp.wait_send()
        for cp in mine:
            cp.wait()

    any_spec = pl.BlockSpec(memory_space=pl.ANY)
    return pl.pallas_call(
        body, name=name,
        in_specs=[any_spec] * n, out_specs=[any_spec] * n,
        out_shape=[jax.ShapeDtypeStruct(a.shape, a.dtype) for a in arrs],
        scratch_shapes=[pltpu.SemaphoreType.DMA((n, 7)), pltpu.SemaphoreType.DMA((n, 7)),
                        pltpu.SemaphoreType.DMA((n,))],
    )(*arrs)


def _peer_list():
    x, y, c = _coords()
    return [((1 - x if r & 4 else x), (1 - y if r & 2 else y), (1 - c if r & 1 else c)) for r in range(1, N_DEV)]


def _copy_for(mode, src, land, send, recv, peer, me):
    src_ref = src if mode == "gather" else src.at[_flat(*peer)]
    return pltpu.make_async_remote_copy(src_ref=src_ref, dst_ref=land.at[me], send_sem=send, recv_sem=recv,
                                        device_id=peer, device_id_type=MESH)


HBM_SPEC = pl.BlockSpec(memory_space=pltpu.HBM)
SEM_SPEC = pl.BlockSpec(memory_space=pltpu.SEMAPHORE)
ANY_SPEC = pl.BlockSpec(memory_space=pl.ANY)
SIDE_EFFECT = pltpu.SideEffectType.DATAFLOW_SIDE_EFFECTING


def _async_start(groups, mode, after, name):
    flat_arrs = [a for g in groups for a in g]
    n = len(flat_arrs)

    def body(*refs):
        srcs, lands = refs[:n], refs[n:2 * n]
        outs = refs[2 * n + 1:]
        token = outs[-1]
        me = _flat(*_coords())
        for ai in range(n):
            for peer in _peer_list():
                _copy_for(mode, srcs[ai], lands[ai], outs[2 * ai], outs[2 * ai + 1], peer, me).start()
        token[...] = jnp.zeros(token.shape, F32)

    land_shapes = [((N_DEV,) + a.shape if mode == "gather" else a.shape) for a in flat_arrs]
    out_shape = [pltpu.SemaphoreType.DMA(())] * (2 * n)
    out_shape += [pltpu.HBM(a.shape, a.dtype) for a in flat_arrs]
    out_shape += [pltpu.HBM(s, a.dtype) for s, a in zip(land_shapes, flat_arrs)]
    out_shape.append(jax.ShapeDtypeStruct((8, LANES), F32))
    res = pl.pallas_call(
        body, name=name, out_shape=tuple(out_shape),
        in_specs=[HBM_SPEC] * (2 * n) + [ANY_SPEC],
        out_specs=tuple([SEM_SPEC] * (2 * n) + [HBM_SPEC] * (2 * n) + [pl.BlockSpec(memory_space=pltpu.VMEM)]),
        input_output_aliases={i: 2 * n + i for i in range(2 * n)},
        compiler_params=pltpu.CompilerParams(has_side_effects=SIDE_EFFECT),
    )(*[pltpu.with_memory_space_constraint(a, pltpu.HBM) for a in flat_arrs],
      *[pltpu.with_memory_space_constraint(lax.empty(s, a.dtype), pltpu.HBM) for s, a in zip(land_shapes, flat_arrs)],
      after)
    sems, thru = res[:2 * n], res[2 * n:-1]
    states, idx = [], 0
    for g in groups:
        k = len(g)
        states.append((list(sems[2 * idx:2 * (idx + k):2]), list(sems[2 * idx + 1:2 * (idx + k):2]),
                       list(thru[idx:idx + k]), list(thru[n + idx:n + idx + k])))
        idx += k
    return states, res[-1][0, 0]


def _async_wait(state, after, name):
    sends, recvs, srcs, lands = state
    g = len(srcs)

    def body(*refs):
        l_refs, sems = refs[g:2 * g], refs[2 * g:4 * g]
        for ai in range(g):
            seven = l_refs[ai].at[pl.ds(0, N_DEV - 1)]
            cp = pltpu.make_async_remote_copy(src_ref=seven, dst_ref=seven, send_sem=sems[ai], recv_sem=sems[g + ai],
                                              device_id=_coords(), device_id_type=MESH)
            cp.wait_send()
            cp.wait_recv()

    res = pl.pallas_call(
        body, name=name,
        out_shape=tuple([pltpu.HBM(a.shape, a.dtype) for a in srcs] + [pltpu.HBM(a.shape, a.dtype) for a in lands]),
        in_specs=[HBM_SPEC] * (2 * g) + [SEM_SPEC] * (2 * g) + [ANY_SPEC],
        out_specs=tuple([HBM_SPEC] * (2 * g)),
        input_output_aliases={i: i for i in range(2 * g)},
        compiler_params=pltpu.CompilerParams(has_side_effects=SIDE_EFFECT),
    )(*srcs, *lands, *sends, *recvs, after)
    return list(res[:g]), list(res[g:])


def _with_own(land, own, me):
    return lax.dynamic_update_index_in_dim(land, own, me, 0)


IN_SPLITS = (512, 512, 512, 8, 768, 256, 32, 1024, 1024)


def _unshard_cols(g):
    return jnp.transpose(g, (1, 0, 2)).reshape(g.shape[1], -1)


def _shard_cols(w):
    k = w.shape[0]
    return jnp.transpose(w.reshape(k, N_DEV, -1), (1, 0, 2))


def _prepare_weights(g):
    w = {}
    if "w_in" in g:
        w_in = _unshard_cols(g["w_in"])
        offs = [0]
        for wd in IN_SPLITS:
            offs.append(offs[-1] + wd)
        fq, fk, fv, wf, cq, ckv, kr, gf, gm = [w_in[:, offs[i]:offs[i + 1]] for i in range(9)]
        z = functools.partial(jnp.zeros, dtype=BF16)
        misc = jnp.concatenate([wf, z((D, 56)), kr, z((D, 32))], axis=1)
        w["w_a"] = jnp.concatenate([cq, ckv, gf, gm, misc], axis=1)
        w["w_b"] = jnp.concatenate([fq, fk, fv], axis=1)
    if "w_uq" in g:
        w_uq = g["w_uq"].reshape(Q_LORA, HEADS, 96)
        w["w_uq"] = jnp.pad(w_uq, ((0, 0), (0, 0), (0, 32))).reshape(Q_LORA, HEADS * LANES)
        ukv = g["w_ukv"]
        w["w_k"] = jnp.transpose(jnp.pad(ukv[:, :, :64], ((0, 0), (0, 0), (0, 64))), (1, 0, 2)).reshape(KV_LORA, HEADS * LANES)
        w["w_v"] = jnp.transpose(ukv[:, :, 64:], (1, 0, 2)).reshape(KV_LORA, HEADS * HEAD_DIM)
        w["w_kv"] = jnp.concatenate([w["w_k"], w["w_v"]], axis=1)
    if "w_out" in g:
        w["w_pf"] = _unshard_cols(g["w_proj_fox"])
        w["w_pm"] = _unshard_cols(g["w_proj_mla"])
        w["w_out"] = g["w_out"].reshape(D, D)
    if "w_ffn_in" in g:
        w["w_ffn_in"] = _unshard_cols(g["w_ffn_in"])
        w["w_ffn_out"] = g["w_ffn_out"].reshape(D_FF, D)
    return w


def _shard_grads(dw):
    out = {}
    if "w_a" in dw:
        da, db_ = dw["w_a"], dw["w_b"]
        misc = da[:, 3072:3200]
        w_in = jnp.concatenate([db_, misc[:, 0:8], da[:, 0:768], da[:, 768:1024], misc[:, 64:96],
                                da[:, 1024:3072]], axis=1)
        out["w_in"] = _shard_cols(w_in)
    if "w_uq" in dw:
        w_uq = dw["w_uq"].reshape(Q_LORA, HEADS, LANES)[:, :, :96].reshape(Q_LORA, Q_LORA)
        out["w_uq"] = w_uq.reshape(N_DEV, Q_LORA // N_DEV, Q_LORA)
        k_part = dw["w_k"].reshape(KV_LORA, HEADS, LANES)[:, :, :64]
        v_part = dw["w_v"].reshape(KV_LORA, HEADS, HEAD_DIM)
        out["w_ukv"] = jnp.transpose(jnp.concatenate([k_part, v_part], axis=2), (1, 0, 2))
    if "w_out" in dw:
        out["w_proj_fox"] = _shard_cols(dw["w_pf"])
        out["w_proj_mla"] = _shard_cols(dw["w_pm"])
        out["w_out"] = dw["w_out"].reshape(N_DEV, D // N_DEV, D)
    if "w_ffn_in" in dw:
        out["w_ffn_in"] = _shard_cols(dw["w_ffn_in"])
        out["w_ffn_out"] = dw["w_ffn_out"].reshape(N_DEV, D_FF // N_DEV, D)
    return out


def _fwd_bwd(x, pos, mod, target, w, vec, wts, send):
    shift_mix, scale_mix, gate_mix, shift_ffn, scale_ffn, gate_ffn = [mod[:, i * D:(i + 1) * D] for i in range(6)]
    g_pre_mix, g_post_mix, g_pre_ffn, g_post_ffn = vec["g_pre_mix"], vec["g_post_mix"], vec["g_pre_ffn"], vec["g_post_ffn"]
    g_q, g_kv = vec["g_q_lora"], vec["g_kv_lora"]

    inv_freq = 1.0 / (ROPE_THETA ** (jnp.arange(0, ROPE_DIM, 2, dtype=F32) / ROPE_DIM))
    invf = jnp.concatenate([jnp.zeros((64,), F32), inv_freq, inv_freq, jnp.zeros((32,), F32)]).reshape(1, LANES)
    ct, sa, sb = _rope_tables(pos, invf)

    def pre1(xv, g, sc, sh):
        return ((xv * _rstd(xv) * g) * (1.0 + sc) + sh,), ()
    (h,) = _rowwise(pre1, [(x, D, 0)], [g_pre_mix, scale_mix, shift_mix], [(D, BF16)], [], "pre_mix")
    proj_a = _mm(h, w["w_a"], "nn", F32, "in_proj_a")
    qkv = _mm(h, w["w_b"], "nn", BF16, "in_proj_b")

    def lora_norm(cq, ckv, gq, gkv):
        return (cq * _rstd(cq) * gq, ckv * _rstd(ckv) * gkv), ()
    cqn, ckvn = _rowwise(lora_norm, [(proj_a, Q_LORA, 0), (proj_a, KV_LORA, 3)], [g_q, g_kv],
                         [(Q_LORA, BF16), (KV_LORA, BF16)], [], "lora_norm")
    w = {**w, **wts("lora", cqn)}
    qb = _mm(cqn, w["w_uq"], "nn", F32, "mla_uq")
    kvb = _mm(ckvn, w["w_kv"], "nn", F32, "mla_ukv")

    def mla_rope(qv, kv, vv, misc, c_, a_, b_):
        lane = lax.broadcasted_iota(jnp.int32, (1, LANES), 1)
        kpe = jnp.where((lane >= 64) & (lane < 96), _rope(misc, c_, a_, b_), 0.0)
        qs = [_rope(qv[:, hd * LANES:(hd + 1) * LANES], c_, a_, b_) for hd in range(HEADS)]
        ks = [kv[:, hd * LANES:(hd + 1) * LANES] + kpe for hd in range(HEADS)]
        return (jnp.concatenate(qs, axis=1), jnp.concatenate(ks, axis=1), vv), ()
    q_m, k_m, v_m = _rowwise(
        mla_rope, [(qb, D, 0), (kvb, D, 0), (kvb, 512, 2), (proj_a, LANES, 24), (ct, LANES, 0), (sa, LANES, 0), (sb, LANES, 0)],
        [], [(D, BF16), (D, BF16), (512, BF16)], [], "mla_rope")

    zt = jnp.transpose(proj_a[:, 3072:3080])
    bf = jnp.transpose(vec["b_forget"])
    neg_f = _fox_gates(zt, bf)
    bias = neg_f.reshape(HEADS, N_ATT, 1, ATT_T)
    o_a, lse_a = _attn_fwd(qkv, 0, qkv, 4, qkv, 8, LANES, 1.0 / math.sqrt(HEAD_DIM), bias, "fox_attn")
    o_b, lse_b = _attn_fwd(q_m, 0, k_m, 0, v_m, 0, 2 * LANES, 1.0 / math.sqrt(64 + ROPE_DIM), None, "mla_attn")

    w = {**w, **wts("proj", o_b)}
    pa = _mm(o_a, w["w_pf"], "nn", F32, "proj_fox")
    pb = _mm(o_b, w["w_pm"], "nn", F32, "proj_mla")

    def merge(gf, gm, pa_, pb_):
        return (_sigmoid(gf) * pa_ + _sigmoid(gm) * pb_,), ()
    (merged,) = _rowwise(merge, [(proj_a, D, 1), (proj_a, D, 2), (pa, D, 0), (pb, D, 0)], [], [(D, BF16)], [], "merge")
    y = _mm(merged, w["w_out"], "nn", F32, "out_proj")

    def post1(xv, yv, gate, gpost, gpre, sc, sh):
        x1 = xv + gate * (yv * _rstd(yv) * gpost)
        return (x1, (x1 * _rstd(x1) * gpre) * (1.0 + sc) + sh), ()
    x1, h2 = _rowwise(post1, [(x, D, 0), (y, D, 0)], [gate_mix, g_post_mix, g_pre_ffn, scale_ffn, shift_ffn],
                      [(D, F32), (D, BF16)], [], "post_mix")
    w = {**w, **wts("ffn", h2)}
    gu = _mm(h2, w["w_ffn_in"], "nn", F32, "ffn_in")

    def swiglu(g, u):
        return (g * _sigmoid(g) * u,), ()
    (act,) = _rowwise(swiglu, [(gu, D_FF, 0), (gu, D_FF, 1)], [], [(D_FF, BF16)], [], "swiglu")
    y2 = _mm(act, w["w_ffn_out"], "nn", F32, "ffn_out")

    def head(x1v, y2v, tv, gate, gpost):
        r = _rstd(y2v)
        yn = y2v * r
        n2 = yn * gpost
        err = (x1v + gate * n2) - tv
        dx2 = err * (1.0 / D)
        dn2 = dx2 * gate
        dy2 = _norm_bwd(dn2 * gpost, yn, r)
        return (dx2, dy2), (_colsum(err * err), _colsum(dx2 * n2), _colsum(dn2 * yn))
    dx2, dy2, err_cols, d_gate_ffn, d_g_post_ffn = _rowwise(
        head, [(x1, D, 0), (y2, D, 0), (target, D, 0)], [gate_ffn, g_post_ffn], [(D, F32), (D, BF16)], [D, D, D], "loss_head")

    dact = _mm(dy2, w["w_ffn_out"], "nt", F32, "ffn_out_dx")
    dw = {"w_ffn_out": _mm(act, dy2, "tn", BF16, "ffn_out_dw")}

    def swiglu_bwd(g, u, da):
        sg = _sigmoid(g)
        return (jnp.concatenate([da * u * (sg * (1.0 + g * (1.0 - sg))), da * (g * sg)], axis=1),), ()
    (dgu,) = _rowwise(swiglu_bwd, [(gu, D_FF, 0), (gu, D_FF, 1), (dact, D_FF, 0)], [], [(2 * D_FF, BF16)], [], "swiglu_bwd")
    dh2 = _mm(dgu, w["w_ffn_in"], "nt", F32, "ffn_in_dx")
    dw["w_ffn_in"] = _mm(h2, dgu, "tn", BF16, "ffn_in_dw")
    gate_mix = gate_mix + send({n: dw.pop(n) for n in ("w_ffn_in", "w_ffn_out")})

    def mid(dh, x1v, dx2v, yv, gpre, sc, gate, gpost):
        r2 = _rstd(x1v)
        x1n = x1v * r2
        t = dh * x1n
        dx1 = dx2v + _norm_bwd(dh * (gpre * (1.0 + sc)), x1n, r2)
        ry = _rstd(yv)
        yn = yv * ry
        dn1 = dx1 * gate
        dy = _norm_bwd(dn1 * gpost, yn, ry)
        sums = (_colsum(dh), _colsum(t) * gpre, _colsum(t) * (1.0 + sc), _colsum(dx1 * (yn * gpost)), _colsum(dn1 * yn))
        return (dx1, dy), sums
    dx1, dy, d_shift_ffn, d_scale_ffn, d_g_pre_ffn, d_gate_mix, d_g_post_mix = _rowwise(
        mid, [(dh2, D, 0), (x1, D, 0), (dx2, D, 0), (y, D, 0)], [g_pre_ffn, scale_ffn, gate_mix, g_post_mix],
        [(D, F32), (D, BF16)], [D] * 5, "mid_bwd")

    dmerged = _mm(dy, w["w_out"], "nt", F32, "out_proj_dx")
    dw["w_out"] = _mm(merged, dy, "tn", BF16, "out_proj_dw")

    def merge_bwd(dm, gf, gm, pa_, pb_):
        sf, sm = _sigmoid(gf), _sigmoid(gm)
        dgates = jnp.concatenate([dm * pa_ * (sf * (1.0 - sf)), dm * pb_ * (sm * (1.0 - sm))], axis=1)
        return (dm * sf, dm * sm, dgates), ()
    dpa, dpb, dgates = _rowwise(
        merge_bwd, [(dmerged, D, 0), (proj_a, D, 1), (proj_a, D, 2), (pa, D, 0), (pb, D, 0)], [],
        [(D, BF16), (D, BF16), (2 * D, BF16)], [], "merge_bwd")
    do_a = _mm(dpa, w["w_pf"], "nt", BF16, "proj_fox_dx")
    do_b = _mm(dpb, w["w_pm"], "nt", BF16, "proj_mla_dx")
    dw["w_pf"] = _mm(o_a, dpa, "tn", BF16, "proj_fox_dw")
    dw["w_pm"] = _mm(o_b, dpb, "tn", BF16, "proj_mla_dw")
    bias = bias + send({n: dw.pop(n) for n in ("w_out", "w_pf", "w_pm")})

    sc_a, sc_b = 1.0 / math.sqrt(HEAD_DIM), 1.0 / math.sqrt(64 + ROPE_DIM)
    delta_a = _attn_delta(qkv, 0, qkv, 4, qkv, 8, do_a, lse_a, LANES, sc_a, bias, "fox_attn_delta")
    delta_b = _attn_delta(q_m, 0, k_m, 0, v_m, 0, do_b, lse_b, 2 * LANES, sc_b, None, "mla_attn_delta")
    dq_a, dk_a, dv_a, dbias = _attn_bwd(qkv, 0, qkv, 4, qkv, 8, do_a, lse_a, delta_a, LANES, sc_a, bias, BF16, "fox_attn_bwd")
    dq_m, dk_m, dv_m = _attn_bwd(q_m, 0, k_m, 0, v_m, 0, do_b, lse_b, delta_b, 2 * LANES, sc_b, None, F32, "mla_attn_bwd")

    def mla_rope_bwd(dq, dk, c_, a_, b_):
        lane = lax.broadcasted_iota(jnp.int32, (1, LANES), 1)
        dqs = [_rope_t(dq[:, hd * LANES:(hd + 1) * LANES], c_, a_, b_) for hd in range(HEADS)]
        dkpe = dk[:, 0:LANES]
        for hd in range(1, HEADS):
            dkpe = dkpe + dk[:, hd * LANES:(hd + 1) * LANES]
        dkpe = jnp.where((lane >= 64) & (lane < 96), dkpe, 0.0)
        dkr = jnp.where((lane >= 64) & (lane < 96), _rope_t(dkpe, c_, a_, b_), 0.0)
        return (jnp.concatenate(dqs, axis=1), dk, dkr), ()
    dqb, dkb, dkr = _rowwise(mla_rope_bwd, [(dq_m, D, 0), (dk_m, D, 0), (ct, LANES, 0), (sa, LANES, 0), (sb, LANES, 0)],
                             [], [(D, BF16), (D, BF16), (LANES, F32)], [], "mla_rope_bwd")
    dcqn = _mm(dqb, w["w_uq"], "nt", F32, "mla_uq_dx")
    dw["w_uq"] = _mm(cqn, dqb, "tn", BF16, "mla_uq_dw")
    dckvn = _mm(dv_m, w["w_v"], "nt", F32, "mla_uv_dx", acc=_mm(dkb, w["w_k"], "nt", F32, "mla_uk_dx"))
    dw["w_k"] = _mm(ckvn, dkb, "tn", BF16, "mla_uk_dw")
    dw["w_v"] = _mm(ckvn, dv_m, "tn", BF16, "mla_uv_dw")

    def lora_norm_bwd(cq, ckv, dq, dkv, gq, gkv):
        rq, rk = _rstd(cq), _rstd(ckv)
        cqh, ckh = cq * rq, ckv * rk
        return (_norm_bwd(dq * gq, cqh, rq), _norm_bwd(dkv * gkv, ckh, rk)), (_colsum(dq * cqh), _colsum(dkv * ckh))
    dcq, dckv, d_g_q, d_g_kv = _rowwise(
        lora_norm_bwd, [(proj_a, Q_LORA, 0), (proj_a, KV_LORA, 3), (dcqn, Q_LORA, 0), (dckvn, KV_LORA, 0)], [g_q, g_kv],
        [(Q_LORA, BF16), (KV_LORA, BF16)], [Q_LORA, KV_LORA], "lora_norm_bwd")

    dzt, d_bf = _fox_gates_bwd(dbias.reshape(HEADS, S), zt, bf)
    dmisc = (dkr + jnp.pad(jnp.transpose(dzt), ((0, 0), (0, LANES - HEADS)))).astype(BF16)
    dproj_a = jnp.concatenate([dcq, dckv, dgates, dmisc], axis=1)
    dqkv = jnp.concatenate([dq_a, dk_a, dv_a], axis=1)
    dh = _mm(dqkv, w["w_b"], "nt", F32, "in_proj_b_dx", acc=_mm(dproj_a, w["w_a"], "nt", F32, "in_proj_a_dx"))
    dw["w_a"] = _mm(h, dproj_a, "tn", BF16, "in_proj_a_dw")
    dw["w_b"] = _mm(h, dqkv, "tn", BF16, "in_proj_b_dw")

    def first(dhv, xv, dx1v, gpre, sc):
        r = _rstd(xv)
        xn = xv * r
        t = dhv * xn
        dx = dx1v + _norm_bwd(dhv * (gpre * (1.0 + sc)), xn, r)
        return (dx,), (_colsum(dhv), _colsum(t) * gpre, _colsum(t) * (1.0 + sc))
    grad_x, d_shift_mix, d_scale_mix, d_g_pre_mix = _rowwise(
        first, [(dh, D, 0), (x, D, 0), (dx1, D, 0)], [g_pre_mix, scale_mix], [(D, F32)], [D] * 3, "pre_mix_bwd")

    dmod = jnp.concatenate([d_shift_mix, d_scale_mix, d_gate_mix, d_shift_ffn, d_scale_ffn, d_gate_ffn], axis=1)
    small = dict(dmod=dmod, g_pre_mix=d_g_pre_mix, g_post_mix=d_g_post_mix, g_pre_ffn=d_g_pre_ffn,
                 g_post_ffn=d_g_post_ffn, g_q_lora=d_g_q, g_kv_lora=d_g_kv,
                 b_forget=jnp.pad(jnp.transpose(d_bf), ((0, 0), (0, LANES - HEADS))), err=err_cols)
    return grad_x, dw, small


SMALL_ORDER = ("dmod", "g_pre_mix", "g_post_mix", "g_pre_ffn", "g_post_ffn", "g_q_lora", "g_kv_lora", "b_forget", "err")
SMALL_PARAM = {"dmod": "b_ada"}
MATRICES = ("w_in", "w_uq", "w_ukv", "w_proj_fox", "w_proj_mla", "w_out", "w_ffn_in", "w_ffn_out")
WEIGHTS = ("w_ada", "b_ada", "g_pre_mix", "g_post_mix", "g_pre_ffn", "g_post_ffn", "w_in", "b_forget", "g_q_lora",
           "w_uq", "g_kv_lora", "w_ukv", "w_proj_fox", "w_proj_mla", "w_out", "w_ffn_in", "w_ffn_out")


def _pad_lanes(v):
    return jnp.pad(v, ((0, 0), (0, (-v.shape[1]) % LANES)))


def kernel(x, c, positions, w_ada, b_ada, g_pre_mix, g_post_mix, g_pre_ffn, g_post_ffn, w_in, b_forget, g_q_lora, w_uq, g_kv_lora, w_ukv, w_proj_fox, w_proj_mla, w_out, w_ffn_in, w_ffn_out, loss_target, m_w_ada, m_b_ada, m_g_pre_mix, m_g_post_mix, m_g_pre_ffn, m_g_post_ffn, m_w_in, m_b_forget, m_g_q_lora, m_w_uq, m_g_kv_lora, m_w_ukv, m_w_proj_fox, m_w_proj_mla, m_w_out, m_w_ffn_in, m_w_ffn_out, v_w_ada, v_b_ada, v_g_pre_mix, v_g_post_mix, v_g_pre_ffn, v_g_post_ffn, v_w_in, v_b_forget, v_g_q_lora, v_w_uq, v_g_kv_lora, v_w_ukv, v_w_proj_fox, v_w_proj_mla, v_w_out, v_w_ffn_in, v_w_ffn_out):
    prm = dict(w_ada=w_ada, b_ada=b_ada, g_pre_mix=g_pre_mix, g_post_mix=g_post_mix, g_pre_ffn=g_pre_ffn,
               g_post_ffn=g_post_ffn, w_in=w_in, b_forget=b_forget, g_q_lora=g_q_lora, w_uq=w_uq, g_kv_lora=g_kv_lora,
               w_ukv=w_ukv, w_proj_fox=w_proj_fox, w_proj_mla=w_proj_mla, w_out=w_out, w_ffn_in=w_ffn_in, w_ffn_out=w_ffn_out)
    mom = dict(w_ada=m_w_ada, b_ada=m_b_ada, g_pre_mix=m_g_pre_mix, g_post_mix=m_g_post_mix, g_pre_ffn=m_g_pre_ffn,
               g_post_ffn=m_g_post_ffn, w_in=m_w_in, b_forget=m_b_forget, g_q_lora=m_g_q_lora, w_uq=m_w_uq,
               g_kv_lora=m_g_kv_lora, w_ukv=m_w_ukv, w_proj_fox=m_w_proj_fox, w_proj_mla=m_w_proj_mla, w_out=m_w_out,
               w_ffn_in=m_w_ffn_in, w_ffn_out=m_w_ffn_out)
    var = dict(w_ada=v_w_ada, b_ada=v_b_ada, g_pre_mix=v_g_pre_mix, g_post_mix=v_g_post_mix, g_pre_ffn=v_g_pre_ffn,
               g_post_ffn=v_g_post_ffn, w_in=v_w_in, b_forget=v_b_forget, g_q_lora=v_g_q_lora, w_uq=v_w_uq,
               g_kv_lora=v_g_kv_lora, w_ukv=v_w_ukv, w_proj_fox=v_w_proj_fox, w_proj_mla=v_w_proj_mla, w_out=v_w_out,
               w_ffn_in=v_w_ffn_in, w_ffn_out=v_w_ffn_out)
    me = _flat(*_coords())

    own = {n: prm[n][0].astype(BF16) for n in MATRICES}
    w_in_all, c_all = _all_gather([own["w_in"], c], "gather_in")
    w = _prepare_weights({"w_in": w_in_all})
    c_all = c_all.reshape(N_DEV, D)
    later = dict(lora=("w_uq", "w_ukv"), proj=("w_proj_fox", "w_proj_mla", "w_out"), ffn=("w_ffn_in", "w_ffn_out"))
    states, tok = _async_start([[own[n] for n in names] for names in later.values()], "gather", w_in_all, "gather_rest_start")
    gather_state = dict(zip(later, states))

    def wts(group, after):
        srcs, lands = _async_wait(gather_state[group], after,"gather_" + group + "_wait")
        return _prepare_weights({n: _with_own(land, src, me) for n, src, land in zip(later[group], srcs, lands)})

    sent = []

    def send(grads):
        shards = _shard_grads(grads)
        names = list(shards)
        (state,), t = _async_start([[shards[n] for n in names]], "exchange", jnp.zeros((8, LANES), F32),
                                   "exchange_" + names[0] + "_start")
        sent.append((names, state))
        return t

    ada_cols = w_ada.shape[2]
    b_cols = lax.dynamic_slice(b_ada, (0, me * ada_cols), (1, ada_cols))
    mod_cols, silu_c = _mod_part(c_all, w_ada[0], b_cols)
    (mod_all,) = _all_gather([mod_cols], "gather_mod")
    mod = lax.dynamic_index_in_dim(mod_all, me, axis=1, keepdims=False).reshape(1, 6 * D) + tok

    vec = dict(g_pre_mix=g_pre_mix, g_post_mix=g_post_mix, g_pre_ffn=g_pre_ffn, g_post_ffn=g_post_ffn,
               g_q_lora=g_q_lora, g_kv_lora=g_kv_lora, b_forget=b_forget)
    pos = positions.astype(F32).reshape(S, 1)
    grad_x, dw, small = _fwd_bwd(x[0], pos, mod, loss_target[0], w, vec, wts, send)

    bundle = jnp.concatenate([small[n] for n in SMALL_ORDER], axis=1)
    (bundle_all,) = _all_gather([bundle], "gather_small")
    bundle_all = bundle_all.reshape(N_DEV, 1, -1)
    shards = _shard_grads(dw)
    last = list(shards)
    parts = dict(zip(last, _exchange([shards[n] for n in last], "exchange_last")))
    for names, state in sent:
        srcs, lands = _async_wait(state, parts[last[0]],"exchange_" + names[0] + "_wait")
        for n, src, land in zip(names, srcs, lands):
            parts[n] = _with_own(land, lax.dynamic_index_in_dim(src, me, 0, keepdims=False), me)

    out = {}
    for n in MATRICES:
        out[n] = _adamw(prm[n][0], mom[n][0], var[n][0], parts[n], "adamw_" + n)

    dmod_all = bundle_all[:, 0, :6 * D]
    dm_cols = lax.dynamic_slice(dmod_all, (0, me * ada_cols), (N_DEV, ada_cols))
    g_ada = _w_ada_grad(jnp.transpose(silu_c), dm_cols)
    out["w_ada"] = _adamw(w_ada[0], m_w_ada[0], v_w_ada[0], g_ada[None], "adamw_w_ada")

    def row(src, n):
        if n == "err":
            return jnp.zeros((1, D), F32)
        return _pad_lanes(src[SMALL_PARAM.get(n, n)])
    wv, mv, vv = [jnp.concatenate([row(src, n) for n in SMALL_ORDER], axis=1) for src in (prm, mom, var)]
    gs, ds, ms, vs = _adamw(wv, mv, vv, bundle_all, "adamw_small")
    off = 0
    for n in SMALL_ORDER:
        width = small[n].shape[1]
        if n != "err":
            pn = SMALL_PARAM.get(n, n)
            real = prm[pn].shape[1]
            out[pn] = tuple(t[:, off:off + real] for t in (gs, ds, ms, vs))
        else:
            loss = 0.5 * jnp.sum(gs[0, off:off + width]) / D
        off += width

    res = [loss, grad_x[None]]
    for kind in range(4):
        for n in WEIGHTS:
            t = out[n][kind]
            res.append(t[None] if prm[n].ndim == 3 else t)
    return tuple(res)
```

```python
import functools
import math

import jax
import jax.numpy as jnp
from jax import lax
from jax.experimental import pallas as pl
from jax.experimental.pallas import tpu as pltpu

F32 = jnp.float32
BF16 = jnp.bfloat16

N_DEV = 8
S = 2048
D = 1024
D_FF = 2816
HEADS = 8
HEAD_DIM = 64
Q_LORA = 768
KV_LORA = 256
ROPE_DIM = 32
ROPE_THETA = 10000.0
NORM_EPS = 1e-6
LANES = 128
VMEM_LIMIT = 56 * 1024 * 1024

ADAM_LR = 0.001
ADAM_B1 = 0.9
ADAM_B2 = 0.999
ADAM_EPS = 1e-08
ADAM_WD = 0.01
ADAM_STEP = 10

ATT_T = 256
N_ATT = S // ATT_T

NN = (((1,), (0,)), ((), ()))
NT = (((1,), (1,)), ((), ()))
TN = (((0,), (0,)), ((), ()))
MESH = pl.DeviceIdType.MESH


def _params(sem=None):
    return pltpu.CompilerParams(dimension_semantics=sem, vmem_limit_bytes=VMEM_LIMIT)


def _pick(n, cap):
    best = None
    for t in range(LANES, cap + 1, LANES):
        if n % t == 0:
            best = t
    return best if best is not None else n


def _mm(a, b, mode, out_dtype, name, acc=None, dep=None):
    if mode == "nn":
        (m, k), (k2, n), dn = a.shape, b.shape, NN
    elif mode == "nt":
        (m, k), (n, k2), dn = a.shape, b.shape, NT
    else:
        (k, m), (k2, n), dn = a.shape, b.shape, TN
    assert k == k2, (a.shape, b.shape, mode)
    tn = _pick(n, 640)
    tm = _pick(m, 1024)
    osz = jnp.dtype(out_dtype).itemsize

    def need(tm_):
        blk = tm_ * k * 2 + tn * k * 2 + tm_ * tn * osz + (tm_ * tn * 4 if acc is not None else 0)
        return 2 * blk + tm_ * tn * 4
    while need(tm) > 36 * 1024 * 1024 and tm % 256 == 0:
        tm //= 2

    def body(*refs):
        a_ref, b_ref, o_ref = refs[0], refs[1], refs[-1]
        r = lax.dot_general(a_ref[...], b_ref[...], dn, preferred_element_type=F32)
        if acc is not None:
            r = r + refs[2][...]
        o_ref[...] = r.astype(o_ref.dtype)

    if mode == "tn":
        a_spec = pl.BlockSpec((k, tm), lambda i, j: (0, i))
    else:
        a_spec = pl.BlockSpec((tm, k), lambda i, j: (i, 0))
    if mode == "nt":
        b_spec = pl.BlockSpec((tn, k), lambda i, j: (j, 0))
    else:
        b_spec = pl.BlockSpec((k, tn), lambda i, j: (0, j))
    o_spec = pl.BlockSpec((tm, tn), lambda i, j: (i, j))
    in_specs = [a_spec, b_spec] + ([o_spec] if acc is not None else [])
    in_specs += [pl.BlockSpec(memory_space=pl.ANY)] if dep is not None else []
    args = (a, b) + ((acc,) if acc is not None else ()) + ((dep,) if dep is not None else ())
    return pl.pallas_call(
        body, name=name, grid=(m // tm, n // tn),
        in_specs=in_specs, out_specs=o_spec,
        out_shape=jax.ShapeDtypeStruct((m, n), out_dtype),
        compiler_params=_params(("parallel", "parallel")),
    )(*args)


def _rowwise(fn, row_ins, vec_ins, row_outs, sum_outs, name, tm=256):
    n_in = len(row_ins) + len(vec_ins)
    n_o = len(row_outs)
    rows = row_ins[0][0].shape[0]

    def body(*refs):
        vals = [r[...] for r in refs[:n_in]]
        outs = refs[n_in:]
        ro, so = fn(*vals)
        assert len(ro) == n_o and len(so) == len(sum_outs)
        for r, v in zip(outs[:n_o], ro):
            r[...] = v.astype(r.dtype)
        if sum_outs:
            @pl.when(pl.program_id(0) == 0)
            def _():
                for r in outs[n_o:]:
                    r[...] = jnp.zeros(r.shape, F32)
            for r, v in zip(outs[n_o:], so):
                r[...] += v

    in_specs = [pl.BlockSpec((tm, w), functools.partial(lambda i, b: (i, b), b=b)) for _, w, b in row_ins]
    in_specs += [pl.BlockSpec(v.shape, lambda i: (0, 0)) for v in vec_ins]
    out_specs = [pl.BlockSpec((tm, w), lambda i: (i, 0)) for w, _ in row_outs]
    out_specs += [pl.BlockSpec((1, w), lambda i: (0, 0)) for w in sum_outs]
    out_shape = [jax.ShapeDtypeStruct((rows, w), dt) for w, dt in row_outs]
    out_shape += [jax.ShapeDtypeStruct((1, w), F32) for w in sum_outs]
    return pl.pallas_call(
        body, name=name, grid=(rows // tm,),
        in_specs=in_specs, out_specs=out_specs, out_shape=out_shape,
        compiler_params=_params(("arbitrary",)),
    )(*[a for a, _, _ in row_ins], *vec_ins)


def _sigmoid(x):
    return 1.0 / (1.0 + jnp.exp(-x))


def _rstd(x):
    return lax.rsqrt(jnp.mean(x * x, axis=-1, keepdims=True) + NORM_EPS)


def _norm_bwd(dyn, xn, r):
    return r * (dyn - xn * jnp.mean(dyn * xn, axis=-1, keepdims=True))


def _colsum(x):
    return jnp.sum(x, axis=0, keepdims=True)


def _rope_tables(pos, invf):
    def fn(p, f):
        lane = lax.broadcasted_iota(jnp.int32, (1, LANES), 1)
        ang = p * f
        cs, sn = jnp.cos(ang), jnp.sin(ang)
        rot = (lane >= 64) & (lane < 96)
        ct = jnp.where(lane < 64, 1.0, jnp.where(rot, cs, 0.0))
        sa = jnp.where((lane >= 64) & (lane < 80), -sn, 0.0)
        sb = jnp.where((lane >= 80) & (lane < 96), sn, 0.0)
        return (ct, sa, sb), ()
    return _rowwise(fn, [(pos, 1, 0)], [invf], [(LANES, F32)] * 3, [], "rope_tables")


def _rope(x, ct, sa, sb):
    return x * ct + pltpu.roll(x, LANES - 16, 1) * sa + pltpu.roll(x, 16, 1) * sb


def _rope_t(x, ct, sa, sb):
    return x * ct - pltpu.roll(x, LANES - 16, 1) * sa - pltpu.roll(x, 16, 1) * sb


def _head_mask(width, hh):
    lane = lax.broadcasted_iota(jnp.int32, (1, width), 1)
    half = width // 2
    return (lane >= hh * half) & (lane < (hh + 1) * half)


ATT_PP = 2
ATT_CHAINS = [(a, hh) for a in range(ATT_PP) for hh in range(2)]
ATT_G = HEADS // (2 * ATT_PP)


def _pair(ref_or_val, a, width, rows=slice(None)):
    return ref_or_val[rows, a * width:(a + 1) * width]


def _attn_fwd(q, qo, k, ko, v, vo, dkp, scale, bias, name):
    T = ATT_T
    assert qo % ATT_PP == 0 and ko % ATT_PP == 0 and vo % ATT_PP == 0
    qo, ko, vo = qo // ATT_PP, ko // ATT_PP, vo // ATT_PP

    def body(*refs):
        if bias is not None:
            q_ref, k_ref, v_ref, b_ref, o_ref, lse_ref = refs
        else:
            q_ref, k_ref, v_ref, o_ref, lse_ref = refs
        i = pl.program_id(1)
        row = lax.broadcasted_iota(jnp.int32, (T, T), 0)
        col = lax.broadcasted_iota(jnp.int32, (T, T), 1)
        qms = []
        for a, hh in ATT_CHAINS:
            qb = _pair(q_ref, a, dkp)
            qms.append(jnp.where(_head_mask(dkp, hh), qb, jnp.zeros_like(qb)))

        def step(j, carry, masked):
            ks = pl.ds(pl.multiple_of(j * T, T), T)
            new = []
            for ci, (a, hh) in enumerate(ATT_CHAINS):
                m, l, acc = carry[ci]
                s = lax.dot_general(qms[ci], _pair(k_ref, a, dkp, ks), NT, preferred_element_type=F32) * scale
                if bias is not None:
                    s = s + b_ref[2 * a + hh, j]
                if masked:
                    s = jnp.where(row >= col, s, -jnp.inf)
                m_new = jnp.maximum(m, jnp.max(s, axis=1, keepdims=True))
                alpha = jnp.exp(m - m_new)
                p = jnp.exp(s - m_new)
                l = alpha * l + jnp.sum(p, axis=1, keepdims=True)
                acc = alpha * acc + lax.dot_general(p.astype(BF16), _pair(v_ref, a, LANES, ks), NN,
                                                    preferred_element_type=F32)
                new.append((m_new, l, acc))
            return tuple(new)

        init = tuple((jnp.full((T, 1), -jnp.inf, F32), jnp.zeros((T, 1), F32), jnp.zeros((T, LANES), F32))
                     for _ in ATT_CHAINS)
        carry = lax.fori_loop(0, i, functools.partial(step, masked=False), init)
        carry = step(i, carry, True)
        for a in range(ATT_PP):
            (m0, l0, acc0), (m1, l1, acc1) = carry[2 * a], carry[2 * a + 1]
            lse_ref[2 * a] = m0 + jnp.log(l0)
            lse_ref[2 * a + 1] = m1 + jnp.log(l1)
            o_ref[:, a * LANES:(a + 1) * LANES] = jnp.where(_head_mask(LANES, 0), acc0 / l0, acc1 / l1).astype(o_ref.dtype)

    in_specs = [
        pl.BlockSpec((T, ATT_PP * dkp), lambda g, i: (i, qo + g)),
        pl.BlockSpec((S, ATT_PP * dkp), lambda g, i: (0, ko + g)),
        pl.BlockSpec((S, ATT_PP * LANES), lambda g, i: (0, vo + g)),
    ]
    args = [q, k, v]
    if bias is not None:
        in_specs.append(pl.BlockSpec((2 * ATT_PP, N_ATT, 1, T), lambda g, i: (g, 0, 0, 0)))
        args.append(bias)
    return pl.pallas_call(
        body, name=name, grid=(ATT_G, N_ATT),
        in_specs=in_specs,
        out_specs=[pl.BlockSpec((T, ATT_PP * LANES), lambda g, i: (i, g)),
                   pl.BlockSpec((2 * ATT_PP, T, 1), lambda g, i: (g, i, 0))],
        out_shape=[jax.ShapeDtypeStruct((S, HEADS * HEAD_DIM), BF16),
                   jax.ShapeDtypeStruct((HEADS, S, 1), F32)],
        compiler_params=_params(("parallel", "arbitrary")),
    )(*args)


def _attn_delta(q, qo, k, ko, v, vo, do, lse, dkp, scale, bias, name):
    T = ATT_T
    qo, ko, vo = qo // ATT_PP, ko // ATT_PP, vo // ATT_PP

    def body(*refs):
        q_ref, k_ref, v_ref, do_ref, lse_ref = refs[:5]
        b_ref = refs[5] if bias is not None else None
        out_ref = refs[-1]
        i = pl.program_id(1)
        row = lax.broadcasted_iota(jnp.int32, (T, T), 0)
        col = lax.broadcasted_iota(jnp.int32, (T, T), 1)
        qms, doms, lses = [], [], []
        for a, hh in ATT_CHAINS:
            qb, dob = _pair(q_ref, a, dkp), _pair(do_ref, a, LANES)
            qms.append(jnp.where(_head_mask(dkp, hh), qb, jnp.zeros_like(qb)))
            doms.append(jnp.where(_head_mask(LANES, hh), dob, jnp.zeros_like(dob)))
            lses.append(lse_ref[2 * a + hh])

        def step(j, carry, masked):
            ks = pl.ds(pl.multiple_of(j * T, T), T)
            new = []
            for ci, (a, hh) in enumerate(ATT_CHAINS):
                s = lax.dot_general(qms[ci], _pair(k_ref, a, dkp, ks), NT, preferred_element_type=F32) * scale
                if bias is not None:
                    s = s + b_ref[2 * a + hh, j]
                s = s - lses[ci]
                if masked:
                    s = jnp.where(row >= col, s, -jnp.inf)
                dp = lax.dot_general(doms[ci], _pair(v_ref, a, LANES, ks), NT, preferred_element_type=F32)
                new.append(carry[ci] + jnp.sum(jnp.exp(s) * dp, axis=1, keepdims=True))
            return tuple(new)

        init = tuple(jnp.zeros((T, 1), F32) for _ in ATT_CHAINS)
        acc = step(i, lax.fori_loop(0, i, functools.partial(step, masked=False), init), True)
        for ci, (a, hh) in enumerate(ATT_CHAINS):
            out_ref[2 * a + hh] = acc[ci]

    in_specs = [
        pl.BlockSpec((T, ATT_PP * dkp), lambda g, i: (i, qo + g)),
        pl.BlockSpec((S, ATT_PP * dkp), lambda g, i: (0, ko + g)),
        pl.BlockSpec((S, ATT_PP * LANES), lambda g, i: (0, vo + g)),
        pl.BlockSpec((T, ATT_PP * LANES), lambda g, i: (i, g)),
        pl.BlockSpec((2 * ATT_PP, T, 1), lambda g, i: (g, i, 0)),
    ]
    args = [q, k, v, do, lse]
    if bias is not None:
        in_specs.append(pl.BlockSpec((2 * ATT_PP, N_ATT, 1, T), lambda g, i: (g, 0, 0, 0)))
        args.append(bias)
    return pl.pallas_call(
        body, name=name, grid=(ATT_G, N_ATT),
        in_specs=in_specs, out_specs=pl.BlockSpec((2 * ATT_PP, T, 1), lambda g, i: (g, i, 0)),
        out_shape=jax.ShapeDtypeStruct((HEADS, S, 1), F32),
        compiler_params=_params(("parallel", "arbitrary")),
    )(*args)


def _attn_bwd(q, qo, k, ko, v, vo, do, lse, delta_in, dkp, scale, bias, qk_dtype, name):
    T = ATT_T
    has_b = bias is not None
    qo, ko, vo = qo // ATT_PP, ko // ATT_PP, vo // ATT_PP

    def body(*refs):
        q_ref, k_ref, v_ref, do_ref, lse_ref, delta = refs[:6]
        refs = refs[6:]
        if has_b:
            b_ref, refs = refs[0], refs[1:]
        dq_ref, dk_ref, dv_ref = refs[:3]
        refs = refs[3:]
        if has_b:
            db_ref, refs = refs[0], refs[1:]
        dq_acc, dk_acc, dv_acc = refs
        j = pl.program_id(1)

        @pl.when(j == 0)
        def _():
            dq_acc[...] = jnp.zeros(dq_acc.shape, F32)

        dk_acc[...] = jnp.zeros(dk_acc.shape, F32)
        dv_acc[...] = jnp.zeros(dv_acc.shape, F32)
        row = lax.broadcasted_iota(jnp.int32, (T, T), 0)
        col = lax.broadcasted_iota(jnp.int32, (T, T), 1)
        kbs, vbs, kms = [], [], []
        for a in range(ATT_PP):
            kb = _pair(k_ref, a, dkp)
            kbs.append(kb)
            vbs.append(_pair(v_ref, a, LANES))
            kms.append(jnp.concatenate([jnp.where(_head_mask(dkp, hh), kb, jnp.zeros_like(kb)) for hh in range(2)], axis=0))

        def step(i, dbs, masked):
            rs = pl.ds(pl.multiple_of(i * T, T), T)
            new = list(dbs)
            for a in range(ATT_PP):
                qb = _pair(q_ref, a, dkp, rs)
                dob = _pair(do_ref, a, LANES, rs)
                qm2, dom2, p2, ds2 = [], [], [], []
                for hh in range(2):
                    h = 2 * a + hh
                    qm = jnp.where(_head_mask(dkp, hh), qb, jnp.zeros_like(qb))
                    dom = jnp.where(_head_mask(LANES, hh), dob, jnp.zeros_like(dob))
                    s = lax.dot_general(qm, kbs[a], NT, preferred_element_type=F32) * scale
                    if has_b:
                        s = s + b_ref[h, 0]
                    s = s - lse_ref[h, rs, :]
                    if masked:
                        s = jnp.where(row >= col, s, -jnp.inf)
                    p = jnp.exp(s)
                    dp = lax.dot_general(dom, vbs[a], NT, preferred_element_type=F32)
                    ds = p * (dp - delta[h, rs, :])
                    if has_b:
                        new[h] = dbs[h] + jnp.sum(ds, axis=0, keepdims=True)
                    qm2.append(qm)
                    dom2.append(dom)
                    p2.append(p.astype(BF16))
                    ds2.append((ds * scale).astype(BF16))
                dv_acc[:, a * LANES:(a + 1) * LANES] += lax.dot_general(
                    jnp.concatenate(p2, axis=0), jnp.concatenate(dom2, axis=0), TN, preferred_element_type=F32)
                dk_acc[:, a * dkp:(a + 1) * dkp] += lax.dot_general(
                    jnp.concatenate(ds2, axis=0), jnp.concatenate(qm2, axis=0), TN, preferred_element_type=F32)
                dq_acc[rs, a * dkp:(a + 1) * dkp] += lax.dot_general(
                    jnp.concatenate(ds2, axis=1), kms[a], NN, preferred_element_type=F32)
            return tuple(new)

        dbs = step(j, tuple(jnp.zeros((1, T), F32) for _ in ATT_CHAINS), True)
        dbs = lax.fori_loop(j + 1, N_ATT, functools.partial(step, masked=False), dbs)
        if has_b:
            for ci, (a, hh) in enumerate(ATT_CHAINS):
                db_ref[2 * a + hh, 0] = dbs[ci]
        dk_ref[...] = dk_acc[...].astype(dk_ref.dtype)
        dv_ref[...] = dv_acc[...].astype(dv_ref.dtype)

        @pl.when(j == N_ATT - 1)
        def _():
            dq_ref[...] = dq_acc[...].astype(dq_ref.dtype)

    in_specs = [
        pl.BlockSpec((S, ATT_PP * dkp), lambda g, j: (0, qo + g)),
        pl.BlockSpec((T, ATT_PP * dkp), lambda g, j: (j, ko + g)),
        pl.BlockSpec((T, ATT_PP * LANES), lambda g, j: (j, vo + g)),
        pl.BlockSpec((S, ATT_PP * LANES), lambda g, j: (0, g)),
        pl.BlockSpec((2 * ATT_PP, S, 1), lambda g, j: (g, 0, 0)),
        pl.BlockSpec((2 * ATT_PP, S, 1), lambda g, j: (g, 0, 0)),
    ]
    args = [q, k, v, do, lse, delta_in]
    out_specs = [
        pl.BlockSpec((S, ATT_PP * dkp), lambda g, j: (0, g)),
        pl.BlockSpec((T, ATT_PP * dkp), lambda g, j: (j, g)),
        pl.BlockSpec((T, ATT_PP * LANES), lambda g, j: (j, g)),
    ]
    width = (HEADS // 2) * dkp
    out_shape = [
        jax.ShapeDtypeStruct((S, width), qk_dtype),
        jax.ShapeDtypeStruct((S, width), qk_dtype),
        jax.ShapeDtypeStruct((S, HEADS * HEAD_DIM), BF16),
    ]
    if has_b:
        in_specs.append(pl.BlockSpec((2 * ATT_PP, 1, 1, T), lambda g, j: (g, j, 0, 0)))
        args.append(bias)
        out_specs.append(pl.BlockSpec((2 * ATT_PP, 1, 1, T), lambda g, j: (g, j, 0, 0)))
        out_shape.append(jax.ShapeDtypeStruct((HEADS, N_ATT, 1, T), F32))
    return pl.pallas_call(
        body, name=name, grid=(ATT_G, N_ATT),
        in_specs=in_specs, out_specs=out_specs, out_shape=out_shape,
        scratch_shapes=[pltpu.VMEM((S, ATT_PP * dkp), F32), pltpu.VMEM((T, ATT_PP * dkp), F32),
                        pltpu.VMEM((T, ATT_PP * LANES), F32)],
        compiler_params=_params(("parallel", "arbitrary")),
    )(*args)


def _tri(upper):
    a = lax.broadcasted_iota(jnp.int32, (LANES, LANES), 0)
    b = lax.broadcasted_iota(jnp.int32, (LANES, LANES), 1)
    return jnp.where(a <= b if upper else a >= b, 1.0, 0.0).astype(F32)


def _fox_gates(zt, bf):
    def body(z_ref, b_ref, o_ref):
        tri = _tri(True)
        carry = jnp.zeros((HEADS, 1), F32)
        for t in range(S // LANES):
            sl = slice(t * LANES, (t + 1) * LANES)
            z = z_ref[:, sl] + b_ref[...]
            logf = jnp.minimum(z, 0.0) - jnp.log(1.0 + jnp.exp(-jnp.abs(z)))
            c = lax.dot_general(logf, tri, NN, preferred_element_type=F32,
                                precision=lax.Precision.HIGHEST) + carry
            o_ref[:, sl] = -c
            carry = c[:, LANES - 1:LANES]

    return pl.pallas_call(
        body, name="fox_gates", out_shape=jax.ShapeDtypeStruct((HEADS, S), F32),
        compiler_params=_params(),
    )(zt, bf)


def _fox_gates_bwd(dbias, zt, bf):
    def body(d_ref, z_ref, b_ref, dz_ref, dbf_ref):
        tri = _tri(False)
        carry = jnp.zeros((HEADS, 1), F32)
        tot = jnp.zeros((HEADS, 1), F32)
        for t in reversed(range(S // LANES)):
            sl = slice(t * LANES, (t + 1) * LANES)
            df = -d_ref[:, sl]
            c = lax.dot_general(df, tri, NN, preferred_element_type=F32,
                                precision=lax.Precision.HIGHEST) + carry
            carry = c[:, 0:1]
            z = z_ref[:, sl] + b_ref[...]
            dz = c * _sigmoid(-z)
            dz_ref[:, sl] = dz
            tot = tot + jnp.sum(dz, axis=1, keepdims=True)
        dbf_ref[...] = tot

    return pl.pallas_call(
        body, name="fox_gates_bwd",
        out_shape=[jax.ShapeDtypeStruct((HEADS, S), F32), jax.ShapeDtypeStruct((HEADS, 1), F32)],
        compiler_params=_params(),
    )(dbias, zt, bf)


def _mod_part(c_all, w_ada, b_cols):
    def body(c_ref, w_ref, b_ref, o_ref, s_ref):
        c = c_ref[...]
        sc = c * _sigmoid(c)
        s_ref[...] = sc
        o_ref[...] = lax.dot_general(sc, w_ref[...], NN, preferred_element_type=F32,
                                     precision=lax.Precision.HIGHEST) + b_ref[...]

    return pl.pallas_call(
        body, name="mod_part",
        out_shape=[jax.ShapeDtypeStruct((N_DEV, w_ada.shape[1]), F32), jax.ShapeDtypeStruct(c_all.shape, F32)],
        compiler_params=_params(),
    )(c_all, w_ada, b_cols)


def _w_ada_grad(sc_t, dm):
    def body(s_ref, d_ref, o_ref):
        acc = jnp.zeros(o_ref.shape, F32)
        for b in range(N_DEV):
            acc = acc + s_ref[:, b:b + 1] * d_ref[b:b + 1, :]
        o_ref[...] = acc

    return pl.pallas_call(
        body, name="w_ada_grad", out_shape=jax.ShapeDtypeStruct((sc_t.shape[0], dm.shape[1]), F32),
        compiler_params=_params(),
    )(sc_t, dm)


def _adamw(w, m, v, parts, name):
    rows, cols = w.shape
    n = parts.shape[0]
    tr = rows if rows <= 512 else 256

    def body(w_ref, m_ref, v_ref, p_ref, g_out, d_out, m_out, v_out):
        g = p_ref[0].astype(F32)
        for kk in range(1, n):
            g = g + p_ref[kk].astype(F32)
        mm = ADAM_B1 * m_ref[...] + (1.0 - ADAM_B1) * g
        vv = ADAM_B2 * v_ref[...] + (1.0 - ADAM_B2) * (g * g)
        m_hat = mm / (1.0 - ADAM_B1 ** ADAM_STEP)
        v_hat = vv / (1.0 - ADAM_B2 ** ADAM_STEP)
        g_out[...] = g
        d_out[...] = -ADAM_LR * (m_hat / (jnp.sqrt(v_hat) + ADAM_EPS) + ADAM_WD * w_ref[...])
        m_out[...] = mm
        v_out[...] = vv

    spec = pl.BlockSpec((tr, cols), lambda i: (i, 0))
    return pl.pallas_call(
        body, name=name, grid=(rows // tr,),
        in_specs=[spec, spec, spec, pl.BlockSpec((n, tr, cols), lambda i: (0, i, 0))],
        out_specs=[spec] * 4, out_shape=[jax.ShapeDtypeStruct((rows, cols), F32)] * 4,
        compiler_params=_params(("parallel",)),
    )(w, m, v, parts)


def _coords():
    return lax.axis_index("x"), lax.axis_index("y"), lax.axis_index("c")


def _flat(px, py, pc):
    return 4 * px + 2 * py + pc


def _all_gather(arrs, name):
    n = len(arrs)

    def body(*refs):
        ins, outs = refs[:n], refs[n:2 * n]
        send, recv, lsem = refs[2 * n:]
        x, y, c = _coords()
        me, sibling = (x, y, c), (x, y, 1 - c)
        chips = [(1 - x, y), (x, 1 - y), (1 - x, 1 - y)]

        def copy(a, kk, block, to, src=None):
            slot = outs[a].at[_flat(*block)]
            return pltpu.make_async_remote_copy(
                src_ref=slot if src is None else src, dst_ref=slot,
                send_sem=send.at[a, kk], recv_sem=recv.at[a, kk],
                device_id=to, device_id_type=MESH)

        mine = [pltpu.make_async_copy(ins[a], outs[a].at[_flat(*me)], lsem.at[a]) for a in range(n)]
        for cp in mine:
            cp.start()
        first = []
        for a in range(n):
            first.append(copy(a, 0, me, sibling, src=ins[a]))
            first += [copy(a, 1 + j, me, (*chip, c), src=ins[a]) for j, chip in enumerate(chips)]
        for cp in first:
            cp.start()
        passed = []
        for j, chip in enumerate(chips):
            for a in range(n):
                copy(a, 1 + j, (*chip, c), me).wait_recv()
                cp = copy(a, 4 + j, (*chip, c), sibling)
                cp.start()
                passed.append(cp)
        for a in range(n):
            copy(a, 0, sibling, me).wait_recv()
        for j, chip in enumerate(chips):
            for a in range(n):
                copy(a, 4 + j, (*chip, 1 - c), me).wait_recv()
        for cp in first + passed:
            cp.wait_send()
        for cp in mine:
            cp.wait()

    any_spec = pl.BlockSpec(memory_space=pl.ANY)
    return pl.pallas_call(
        body, name=name,
        in_specs=[any_spec] * n, out_specs=[any_spec] * n,
        out_shape=[jax.ShapeDtypeStruct((N_DEV,) + a.shape, a.dtype) for a in arrs],
        scratch_shapes=[pltpu.SemaphoreType.DMA((n, 7)), pltpu.SemaphoreType.DMA((n, 7)),
                        pltpu.SemaphoreType.DMA((n,))],
    )(*arrs)


def _exchange(arrs, name):
    n = len(arrs)

    def body(*refs):
        ins, outs = refs[:n], refs[n:2 * n]
        send, recv, lsem = refs[2 * n:]
        x, y, c = _coords()
        me = _flat(x, y, c)
        peers = []
        for r in range(1, N_DEV):
            px = 1 - x if r & 4 else x
            py = 1 - y if r & 2 else y
            pc = 1 - c if r & 1 else c
            peers.append((px, py, pc))

        def copy(a, kk):
            peer = peers[kk]
            return pltpu.make_async_remote_copy(
                src_ref=ins[a].at[_flat(*peer)], dst_ref=outs[a].at[me],
                send_sem=send.at[a, kk], recv_sem=recv.at[a, kk],
                device_id=peer, device_id_type=MESH)

        def arrival(a, kk):
            slot = outs[a].at[_flat(*peers[kk])]
            return pltpu.make_async_remote_copy(
                src_ref=slot, dst_ref=slot, send_sem=send.at[a, kk], recv_sem=recv.at[a, kk],
                device_id=peers[kk], device_id_type=MESH)

        mine = [pltpu.make_async_copy(ins[a].at[me], outs[a].at[me], lsem.at[a]) for a in range(n)]
        for cp in mine:
            cp.start()
        sends = [copy(a, kk) for a in range(n) for kk in range(N_DEV - 1)]
        for cp in sends:
            cp.start()
        for a in range(n):
            for kk in range(N_DEV - 1):
                arrival(a, kk).wait_recv()
        for cp in sends:
            cp.wait_send()
        for cp in mine:
            cp.wait()

    any_spec = pl.BlockSpec(memory_space=pl.ANY)
    return pl.pallas_call(
        body, name=name,
        in_specs=[any_spec] * n, out_specs=[any_spec] * n,
        out_shape=[jax.ShapeDtypeStruct(a.shape, a.dtype) for a in arrs],
        scratch_shapes=[pltpu.SemaphoreType.DMA((n, 7)), pltpu.SemaphoreType.DMA((n, 7)),
                        pltpu.SemaphoreType.DMA((n,))],
    )(*arrs)


def _peer_list():
    x, y, c = _coords()
    return [((1 - x if r & 4 else x), (1 - y if r & 2 else y), (1 - c if r & 1 else c)) for r in range(1, N_DEV)]


def _copy_for(mode, src, land, send, recv, peer, me):
    src_ref = src if mode == "gather" else src.at[_flat(*peer)]
    return pltpu.make_async_remote_copy(src_ref=src_ref, dst_ref=land.at[me], send_sem=send, recv_sem=recv,
                                        device_id=peer, device_id_type=MESH)


HBM_SPEC = pl.BlockSpec(memory_space=pltpu.HBM)
SEM_SPEC = pl.BlockSpec(memory_space=pltpu.SEMAPHORE)
ANY_SPEC = pl.BlockSpec(memory_space=pl.ANY)
SIDE_EFFECT = pltpu.SideEffectType.DATAFLOW_SIDE_EFFECTING


def _async_start(groups, mode, after, name):
    flat_arrs = [a for g in groups for a in g]
    n = len(flat_arrs)

    def body(*refs):
        srcs, lands = refs[:n], refs[n:2 * n]
        outs = refs[2 * n + 1:]
        token = outs[-1]
        me = _flat(*_coords())
        for ai in range(n):
            for peer in _peer_list():
                _copy_for(mode, srcs[ai], lands[ai], outs[2 * ai], outs[2 * ai + 1], peer, me).start()
        token[...] = jnp.zeros(token.shape, F32)

    land_shapes = [((N_DEV,) + a.shape if mode == "gather" else a.shape) for a in flat_arrs]
    out_shape = [pltpu.SemaphoreType.DMA(())] * (2 * n)
    out_shape += [pltpu.HBM(a.shape, a.dtype) for a in flat_arrs]
    out_shape += [pltpu.HBM(s, a.dtype) for s, a in zip(land_shapes, flat_arrs)]
    out_shape.append(jax.ShapeDtypeStruct((8, LANES), F32))
    res = pl.pallas_call(
        body, name=name, out_shape=tuple(out_shape),
        in_specs=[HBM_SPEC] * (2 * n) + [ANY_SPEC],
        out_specs=tuple([SEM_SPEC] * (2 * n) + [HBM_SPEC] * (2 * n) + [pl.BlockSpec(memory_space=pltpu.VMEM)]),
        input_output_aliases={i: 2 * n + i for i in range(2 * n)},
        compiler_params=pltpu.CompilerParams(has_side_effects=SIDE_EFFECT),
    )(*[pltpu.with_memory_space_constraint(a, pltpu.HBM) for a in flat_arrs],
      *[pltpu.with_memory_space_constraint(lax.empty(s, a.dtype), pltpu.HBM) for s, a in zip(land_shapes, flat_arrs)],
      after)
    sems, thru = res[:2 * n], res[2 * n:-1]
    states, idx = [], 0
    for g in groups:
        k = len(g)
        states.append((list(sems[2 * idx:2 * (idx + k):2]), list(sems[2 * idx + 1:2 * (idx + k):2]),
                       list(thru[idx:idx + k]), list(thru[n + idx:n + idx + k])))
        idx += k
    return states, res[-1]


def _async_wait(state, after, name):
    sends, recvs, srcs, lands = state
    g = len(srcs)

    def body(*refs):
        l_refs, sems = refs[g:2 * g], refs[2 * g:4 * g]
        for ai in range(g):
            seven = l_refs[ai].at[pl.ds(0, N_DEV - 1)]
            cp = pltpu.make_async_remote_copy(src_ref=seven, dst_ref=seven, send_sem=sems[ai], recv_sem=sems[g + ai],
                                              device_id=_coords(), device_id_type=MESH)
            cp.wait_send()
            cp.wait_recv()

    res = pl.pallas_call(
        body, name=name,
        out_shape=tuple([pltpu.HBM(a.shape, a.dtype) for a in srcs] + [pltpu.HBM(a.shape, a.dtype) for a in lands]),
        in_specs=[HBM_SPEC] * (2 * g) + [SEM_SPEC] * (2 * g) + [ANY_SPEC],
        out_specs=tuple([HBM_SPEC] * (2 * g)),
        input_output_aliases={i: i for i in range(2 * g)},
        compiler_params=pltpu.CompilerParams(has_side_effects=SIDE_EFFECT),
    )(*srcs, *lands, *sends, *recvs, after)
    return list(res[:g]), list(res[g:])


def _with_own(land, own, me):
    return lax.dynamic_update_index_in_dim(land, own, me, 0)


IN_SPLITS = (512, 512, 512, 8, 768, 256, 32, 1024, 1024)


def _unshard_cols(g):
    return jnp.transpose(g, (1, 0, 2)).reshape(g.shape[1], -1)


def _shard_cols(w):
    k = w.shape[0]
    return jnp.transpose(w.reshape(k, N_DEV, -1), (1, 0, 2))


def _prepare_weights(g):
    w = {}
    if "w_in" in g:
        w_in = _unshard_cols(g["w_in"])
        offs = [0]
        for wd in IN_SPLITS:
            offs.append(offs[-1] + wd)
        fq, fk, fv, wf, cq, ckv, kr, gf, gm = [w_in[:, offs[i]:offs[i + 1]] for i in range(9)]
        z = functools.partial(jnp.zeros, dtype=BF16)
        misc = jnp.concatenate([wf, z((D, 56)), kr, z((D, 32))], axis=1)
        w["w_a"] = jnp.concatenate([cq, ckv, gf, gm, misc], axis=1)
        w["w_b"] = jnp.concatenate([fq, fk, fv], axis=1)
    if "w_uq" in g:
        w_uq = g["w_uq"].reshape(Q_LORA, HEADS, 96)
        w["w_uq"] = jnp.pad(w_uq, ((0, 0), (0, 0), (0, 32))).reshape(Q_LORA, HEADS * LANES)
        ukv = g["w_ukv"]
        w["w_k"] = jnp.transpose(jnp.pad(ukv[:, :, :64], ((0, 0), (0, 0), (0, 64))), (1, 0, 2)).reshape(KV_LORA, HEADS * LANES)
        w["w_v"] = jnp.transpose(ukv[:, :, 64:], (1, 0, 2)).reshape(KV_LORA, HEADS * HEAD_DIM)
        w["w_kv"] = jnp.concatenate([w["w_k"], w["w_v"]], axis=1)
    if "w_out" in g:
        w["w_pf"] = _unshard_cols(g["w_proj_fox"])
        w["w_pm"] = _unshard_cols(g["w_proj_mla"])
        w["w_out"] = g["w_out"].reshape(D, D)
    if "w_ffn_in" in g:
        w["w_ffn_in"] = _unshard_cols(g["w_ffn_in"])
        w["w_ffn_out"] = g["w_ffn_out"].reshape(D_FF, D)
    return w


def _shard_grads(dw):
    out = {}
    if "w_a" in dw:
        da, db_ = dw["w_a"], dw["w_b"]
        misc = da[:, 3072:3200]
        w_in = jnp.concatenate([db_, misc[:, 0:8], da[:, 0:768], da[:, 768:1024], misc[:, 64:96],
                                da[:, 1024:3072]], axis=1)
        out["w_in"] = _shard_cols(w_in)
    if "w_uq" in dw:
        w_uq = dw["w_uq"].reshape(Q_LORA, HEADS, LANES)[:, :, :96].reshape(Q_LORA, Q_LORA)
        out["w_uq"] = w_uq.reshape(N_DEV, Q_LORA // N_DEV, Q_LORA)
        k_part = dw["w_k"].reshape(KV_LORA, HEADS, LANES)[:, :, :64]
        v_part = dw["w_v"].reshape(KV_LORA, HEADS, HEAD_DIM)
        out["w_ukv"] = jnp.transpose(jnp.concatenate([k_part, v_part], axis=2), (1, 0, 2))
    if "w_out" in dw:
        out["w_proj_fox"] = _shard_cols(dw["w_pf"])
        out["w_proj_mla"] = _shard_cols(dw["w_pm"])
        out["w_out"] = dw["w_out"].reshape(N_DEV, D // N_DEV, D)
    if "w_ffn_in" in dw:
        out["w_ffn_in"] = _shard_cols(dw["w_ffn_in"])
        out["w_ffn_out"] = dw["w_ffn_out"].reshape(N_DEV, D_FF // N_DEV, D)
    return out


def _fwd_bwd(x, pos, mod, target, w, vec, wts, send):
    shift_mix, scale_mix, gate_mix, shift_ffn, scale_ffn, gate_ffn = [mod[:, i * D:(i + 1) * D] for i in range(6)]
    g_pre_mix, g_post_mix, g_pre_ffn, g_post_ffn = vec["g_pre_mix"], vec["g_post_mix"], vec["g_pre_ffn"], vec["g_post_ffn"]
    g_q, g_kv = vec["g_q_lora"], vec["g_kv_lora"]

    inv_freq = 1.0 / (ROPE_THETA ** (jnp.arange(0, ROPE_DIM, 2, dtype=F32) / ROPE_DIM))
    invf = jnp.concatenate([jnp.zeros((64,), F32), inv_freq, inv_freq, jnp.zeros((32,), F32)]).reshape(1, LANES)
    ct, sa, sb = _rope_tables(pos, invf)

    def pre1(xv, g, sc, sh):
        return ((xv * _rstd(xv) * g) * (1.0 + sc) + sh,), ()
    (h,) = _rowwise(pre1, [(x, D, 0)], [g_pre_mix, scale_mix, shift_mix], [(D, BF16)], [], "pre_mix")
    proj_a = _mm(h, w["w_a"], "nn", F32, "in_proj_a")
    qkv = _mm(h, w["w_b"], "nn", BF16, "in_proj_b")

    def lora_norm(cq, ckv, gq, gkv):
        return (cq * _rstd(cq) * gq, ckv * _rstd(ckv) * gkv), ()
    cqn, ckvn = _rowwise(lora_norm, [(proj_a, Q_LORA, 0), (proj_a, KV_LORA, 3)], [g_q, g_kv],
                         [(Q_LORA, BF16), (KV_LORA, BF16)], [], "lora_norm")
    w = {**w, **wts("lora", cqn)}
    qb = _mm(cqn, w["w_uq"], "nn", F32, "mla_uq")
    kvb = _mm(ckvn, w["w_kv"], "nn", F32, "mla_ukv")

    def mla_rope(qv, kv, vv, misc, c_, a_, b_):
        lane = lax.broadcasted_iota(jnp.int32, (1, LANES), 1)
        kpe = jnp.where((lane >= 64) & (lane < 96), _rope(misc, c_, a_, b_), 0.0)
        qs = [_rope(qv[:, hd * LANES:(hd + 1) * LANES], c_, a_, b_) for hd in range(HEADS)]
        ks = [kv[:, hd * LANES:(hd + 1) * LANES] + kpe for hd in range(HEADS)]
        return (jnp.concatenate(qs, axis=1), jnp.concatenate(ks, axis=1), vv), ()
    q_m, k_m, v_m = _rowwise(
        mla_rope, [(qb, D, 0), (kvb, D, 0), (kvb, 512, 2), (proj_a, LANES, 24), (ct, LANES, 0), (sa, LANES, 0), (sb, LANES, 0)],
        [], [(D, BF16), (D, BF16), (512, BF16)], [], "mla_rope")

    zt = jnp.transpose(proj_a[:, 3072:3080])
    bf = jnp.transpose(vec["b_forget"])
    neg_f = _fox_gates(zt, bf)
    bias = neg_f.reshape(HEADS, N_ATT, 1, ATT_T)
    o_a, lse_a = _attn_fwd(qkv, 0, qkv, 4, qkv, 8, LANES, 1.0 / math.sqrt(HEAD_DIM), bias, "fox_attn")
    o_b, lse_b = _attn_fwd(q_m, 0, k_m, 0, v_m, 0, 2 * LANES, 1.0 / math.sqrt(64 + ROPE_DIM), None, "mla_attn")

    w = {**w, **wts("proj", o_b)}
    pa = _mm(o_a, w["w_pf"], "nn", F32, "proj_fox")
    pb = _mm(o_b, w["w_pm"], "nn", F32, "proj_mla")

    def merge(gf, gm, pa_, pb_):
        return (_sigmoid(gf) * pa_ + _sigmoid(gm) * pb_,), ()
    (merged,) = _rowwise(merge, [(proj_a, D, 1), (proj_a, D, 2), (pa, D, 0), (pb, D, 0)], [], [(D, BF16)], [], "merge")
    y = _mm(merged, w["w_out"], "nn", F32, "out_proj")

    def post1(xv, yv, gate, gpost, gpre, sc, sh):
        x1 = xv + gate * (yv * _rstd(yv) * gpost)
        return (x1, (x1 * _rstd(x1) * gpre) * (1.0 + sc) + sh), ()
    x1, h2 = _rowwise(post1, [(x, D, 0), (y, D, 0)], [gate_mix, g_post_mix, g_pre_ffn, scale_ffn, shift_ffn],
                      [(D, F32), (D, BF16)], [], "post_mix")
    w = {**w, **wts("ffn", h2)}
    gu = _mm(h2, w["w_ffn_in"], "nn", F32, "ffn_in")

    def swiglu(g, u):
        return (g * _sigmoid(g) * u,), ()
    (act,) = _rowwise(swiglu, [(gu, D_FF, 0), (gu, D_FF, 1)], [], [(D_FF, BF16)], [], "swiglu")
    y2 = _mm(act, w["w_ffn_out"], "nn", F32, "ffn_out")

    def head(x1v, y2v, tv, gate, gpost):
        r = _rstd(y2v)
        yn = y2v * r
        n2 = yn * gpost
        err = (x1v + gate * n2) - tv
        dx2 = err * (1.0 / D)
        dn2 = dx2 * gate
        dy2 = _norm_bwd(dn2 * gpost, yn, r)
        return (dx2, dy2), (_colsum(err * err), _colsum(dx2 * n2), _colsum(dn2 * yn))
    dx2, dy2, err_cols, d_gate_ffn, d_g_post_ffn = _rowwise(
        head, [(x1, D, 0), (y2, D, 0), (target, D, 0)], [gate_ffn, g_post_ffn], [(D, F32), (D, BF16)], [D, D, D], "loss_head")

    dact = _mm(dy2, w["w_ffn_out"], "nt", F32, "ffn_out_dx")
    dw = {"w_ffn_out": _mm(act, dy2, "tn", BF16, "ffn_out_dw")}

    def swiglu_bwd(g, u, da):
        sg = _sigmoid(g)
        return (jnp.concatenate([da * u * (sg * (1.0 + g * (1.0 - sg))), da * (g * sg)], axis=1),), ()
    (dgu,) = _rowwise(swiglu_bwd, [(gu, D_FF, 0), (gu, D_FF, 1), (dact, D_FF, 0)], [], [(2 * D_FF, BF16)], [], "swiglu_bwd")
    dh2 = _mm(dgu, w["w_ffn_in"], "nt", F32, "ffn_in_dx")
    dw["w_ffn_in"] = _mm(h2, dgu, "tn", BF16, "ffn_in_dw")
    gate_mix = gate_mix + send({n: dw.pop(n) for n in ("w_ffn_in", "w_ffn_out")})[0, 0]

    def mid(dh, x1v, dx2v, yv, gpre, sc, gate, gpost):
        r2 = _rstd(x1v)
        x1n = x1v * r2
        t = dh * x1n
        dx1 = dx2v + _norm_bwd(dh * (gpre * (1.0 + sc)), x1n, r2)
        ry = _rstd(yv)
        yn = yv * ry
        dn1 = dx1 * gate
        dy = _norm_bwd(dn1 * gpost, yn, ry)
        sums = (_colsum(dh), _colsum(t) * gpre, _colsum(t) * (1.0 + sc), _colsum(dx1 * (yn * gpost)), _colsum(dn1 * yn))
        return (dx1, dy), sums
    dx1, dy, d_shift_ffn, d_scale_ffn, d_g_pre_ffn, d_gate_mix, d_g_post_mix = _rowwise(
        mid, [(dh2, D, 0), (x1, D, 0), (dx2, D, 0), (y, D, 0)], [g_pre_ffn, scale_ffn, gate_mix, g_post_mix],
        [(D, F32), (D, BF16)], [D] * 5, "mid_bwd")

    dmerged = _mm(dy, w["w_out"], "nt", F32, "out_proj_dx")
    dw["w_out"] = _mm(merged, dy, "tn", BF16, "out_proj_dw")

    def merge_bwd(dm, gf, gm, pa_, pb_):
        sf, sm = _sigmoid(gf), _sigmoid(gm)
        dgates = jnp.concatenate([dm * pa_ * (sf * (1.0 - sf)), dm * pb_ * (sm * (1.0 - sm))], axis=1)
        return (dm * sf, dm * sm, dgates), ()
    dpa, dpb, dgates = _rowwise(
        merge_bwd, [(dmerged, D, 0), (proj_a, D, 1), (proj_a, D, 2), (pa, D, 0), (pb, D, 0)], [],
        [(D, BF16), (D, BF16), (2 * D, BF16)], [], "merge_bwd")
    do_a = _mm(dpa, w["w_pf"], "nt", BF16, "proj_fox_dx")
    do_b = _mm(dpb, w["w_pm"], "nt", BF16, "proj_mla_dx")
    dw["w_pf"] = _mm(o_a, dpa, "tn", BF16, "proj_fox_dw")
    dw["w_pm"] = _mm(o_b, dpb, "tn", BF16, "proj_mla_dw")
    bias = bias + send({n: dw.pop(n) for n in ("w_out", "w_pf", "w_pm")})[0, 0]

    sc_a, sc_b = 1.0 / math.sqrt(HEAD_DIM), 1.0 / math.sqrt(64 + ROPE_DIM)
    delta_a = _attn_delta(qkv, 0, qkv, 4, qkv, 8, do_a, lse_a, LANES, sc_a, bias, "fox_attn_delta")
    delta_b = _attn_delta(q_m, 0, k_m, 0, v_m, 0, do_b, lse_b, 2 * LANES, sc_b, None, "mla_attn_delta")
    dq_a, dk_a, dv_a, dbias = _attn_bwd(qkv, 0, qkv, 4, qkv, 8, do_a, lse_a, delta_a, LANES, sc_a, bias, BF16, "fox_attn_bwd")
    dq_m, dk_m, dv_m = _attn_bwd(q_m, 0, k_m, 0, v_m, 0, do_b, lse_b, delta_b, 2 * LANES, sc_b, None, F32, "mla_attn_bwd")

    def mla_rope_bwd(dq, dk, c_, a_, b_):
        lane = lax.broadcasted_iota(jnp.int32, (1, LANES), 1)
        dqs = [_rope_t(dq[:, hd * LANES:(hd + 1) * LANES], c_, a_, b_) for hd in range(HEADS)]
        dkpe = dk[:, 0:LANES]
        for hd in range(1, HEADS):
            dkpe = dkpe + dk[:, hd * LANES:(hd + 1) * LANES]
        dkpe = jnp.where((lane >= 64) & (lane < 96), dkpe, 0.0)
        dkr = jnp.where((lane >= 64) & (lane < 96), _rope_t(dkpe, c_, a_, b_), 0.0)
        return (jnp.concatenate(dqs, axis=1), dk, dkr), ()
    dqb, dkb, dkr = _rowwise(mla_rope_bwd, [(dq_m, D, 0), (dk_m, D, 0), (ct, LANES, 0), (sa, LANES, 0), (sb, LANES, 0)],
                             [], [(D, BF16), (D, BF16), (LANES, F32)], [], "mla_rope_bwd")
    dcqn = _mm(dqb, w["w_uq"], "nt", F32, "mla_uq_dx")
    dw["w_uq"] = _mm(cqn, dqb, "tn", BF16, "mla_uq_dw")
    dckvn = _mm(dv_m, w["w_v"], "nt", F32, "mla_uv_dx", acc=_mm(dkb, w["w_k"], "nt", F32, "mla_uk_dx"))
    dw["w_k"] = _mm(ckvn, dkb, "tn", BF16, "mla_uk_dw")
    dw["w_v"] = _mm(ckvn, dv_m, "tn", BF16, "mla_uv_dw")

    def lora_norm_bwd(cq, ckv, dq, dkv, gq, gkv):
        rq, rk = _rstd(cq), _rstd(ckv)
        cqh, ckh = cq * rq, ckv * rk
        return (_norm_bwd(dq * gq, cqh, rq), _norm_bwd(dkv * gkv, ckh, rk)), (_colsum(dq * cqh), _colsum(dkv * ckh))
    dcq, dckv, d_g_q, d_g_kv = _rowwise(
        lora_norm_bwd, [(proj_a, Q_LORA, 0), (proj_a, KV_LORA, 3), (dcqn, Q_LORA, 0), (dckvn, KV_LORA, 0)], [g_q, g_kv],
        [(Q_LORA, BF16), (KV_LORA, BF16)], [Q_LORA, KV_LORA], "lora_norm_bwd")

    dzt, d_bf = _fox_gates_bwd(dbias.reshape(HEADS, S), zt, bf)
    dmisc = (dkr + jnp.pad(jnp.transpose(dzt), ((0, 0), (0, LANES - HEADS)))).astype(BF16)
    dproj_a = jnp.concatenate([dcq, dckv, dgates, dmisc], axis=1)
    dqkv = jnp.concatenate([dq_a, dk_a, dv_a], axis=1)
    dw["w_a"] = _mm(h, dproj_a, "tn", BF16, "in_proj_a_dw")
    dw["w_b"] = _mm(h, dqkv, "tn", BF16, "in_proj_b_dw")
    tok = send(dw)
    dh = _mm(dqkv, w["w_b"], "nt", F32, "in_proj_b_dx", acc=_mm(dproj_a, w["w_a"], "nt", F32, "in_proj_a_dx", dep=tok))

    def first(dhv, xv, dx1v, gpre, sc):
        r = _rstd(xv)
        xn = xv * r
        t = dhv * xn
        dx = dx1v + _norm_bwd(dhv * (gpre * (1.0 + sc)), xn, r)
        return (dx,), (_colsum(dhv), _colsum(t) * gpre, _colsum(t) * (1.0 + sc))
    grad_x, d_shift_mix, d_scale_mix, d_g_pre_mix = _rowwise(
        first, [(dh, D, 0), (x, D, 0), (dx1, D, 0)], [g_pre_mix, scale_mix], [(D, F32)], [D] * 3, "pre_mix_bwd")

    dmod = jnp.concatenate([d_shift_mix, d_scale_mix, d_gate_mix, d_shift_ffn, d_scale_ffn, d_gate_ffn], axis=1)
    small = dict(dmod=dmod, g_pre_mix=d_g_pre_mix, g_post_mix=d_g_post_mix, g_pre_ffn=d_g_pre_ffn,
                 g_post_ffn=d_g_post_ffn, g_q_lora=d_g_q, g_kv_lora=d_g_kv,
                 b_forget=jnp.pad(jnp.transpose(d_bf), ((0, 0), (0, LANES - HEADS))), err=err_cols)
    return grad_x, small


SMALL_ORDER = ("dmod", "g_pre_mix", "g_post_mix", "g_pre_ffn", "g_post_ffn", "g_q_lora", "g_kv_lora", "b_forget", "err")
SMALL_PARAM = {"dmod": "b_ada"}
MATRICES = ("w_in", "w_uq", "w_ukv", "w_proj_fox", "w_proj_mla", "w_out", "w_ffn_in", "w_ffn_out")
WEIGHTS = ("w_ada", "b_ada", "g_pre_mix", "g_post_mix", "g_pre_ffn", "g_post_ffn", "w_in", "b_forget", "g_q_lora",
           "w_uq", "g_kv_lora", "w_ukv", "w_proj_fox", "w_proj_mla", "w_out", "w_ffn_in", "w_ffn_out")


def _pad_lanes(v):
    return jnp.pad(v, ((0, 0), (0, (-v.shape[1]) % LANES)))


def kernel(x, c, positions, w_ada, b_ada, g_pre_mix, g_post_mix, g_pre_ffn, g_post_ffn, w_in, b_forget, g_q_lora, w_uq, g_kv_lora, w_ukv, w_proj_fox, w_proj_mla, w_out, w_ffn_in, w_ffn_out, loss_target, m_w_ada, m_b_ada, m_g_pre_mix, m_g_post_mix, m_g_pre_ffn, m_g_post_ffn, m_w_in, m_b_forget, m_g_q_lora, m_w_uq, m_g_kv_lora, m_w_ukv, m_w_proj_fox, m_w_proj_mla, m_w_out, m_w_ffn_in, m_w_ffn_out, v_w_ada, v_b_ada, v_g_pre_mix, v_g_post_mix, v_g_pre_ffn, v_g_post_ffn, v_w_in, v_b_forget, v_g_q_lora, v_w_uq, v_g_kv_lora, v_w_ukv, v_w_proj_fox, v_w_proj_mla, v_w_out, v_w_ffn_in, v_w_ffn_out):
    prm = dict(w_ada=w_ada, b_ada=b_ada, g_pre_mix=g_pre_mix, g_post_mix=g_post_mix, g_pre_ffn=g_pre_ffn,
               g_post_ffn=g_post_ffn, w_in=w_in, b_forget=b_forget, g_q_lora=g_q_lora, w_uq=w_uq, g_kv_lora=g_kv_lora,
               w_ukv=w_ukv, w_proj_fox=w_proj_fox, w_proj_mla=w_proj_mla, w_out=w_out, w_ffn_in=w_ffn_in, w_ffn_out=w_ffn_out)
    mom = dict(w_ada=m_w_ada, b_ada=m_b_ada, g_pre_mix=m_g_pre_mix, g_post_mix=m_g_post_mix, g_pre_ffn=m_g_pre_ffn,
               g_post_ffn=m_g_post_ffn, w_in=m_w_in, b_forget=m_b_forget, g_q_lora=m_g_q_lora, w_uq=m_w_uq,
               g_kv_lora=m_g_kv_lora, w_ukv=m_w_ukv, w_proj_fox=m_w_proj_fox, w_proj_mla=m_w_proj_mla, w_out=m_w_out,
               w_ffn_in=m_w_ffn_in, w_ffn_out=m_w_ffn_out)
    var = dict(w_ada=v_w_ada, b_ada=v_b_ada, g_pre_mix=v_g_pre_mix, g_post_mix=v_g_post_mix, g_pre_ffn=v_g_pre_ffn,
               g_post_ffn=v_g_post_ffn, w_in=v_w_in, b_forget=v_b_forget, g_q_lora=v_g_q_lora, w_uq=v_w_uq,
               g_kv_lora=v_g_kv_lora, w_ukv=v_w_ukv, w_proj_fox=v_w_proj_fox, w_proj_mla=v_w_proj_mla, w_out=v_w_out,
               w_ffn_in=v_w_ffn_in, w_ffn_out=v_w_ffn_out)
    me = _flat(*_coords())

    own = {n: prm[n][0].astype(BF16) for n in MATRICES}
    w_in_all, c_all = _all_gather([own["w_in"], c], "gather_in")
    w = _prepare_weights({"w_in": w_in_all})
    c_all = c_all.reshape(N_DEV, D)
    later = dict(lora=("w_uq", "w_ukv"), proj=("w_proj_fox", "w_proj_mla", "w_out"), ffn=("w_ffn_in", "w_ffn_out"))
    states, tok = _async_start([[own[n] for n in names] for names in later.values()], "gather", w_in_all, "gather_rest_start")
    gather_state = dict(zip(later, states))

    def wts(group, after):
        srcs, lands = _async_wait(gather_state[group], after, "gather_" + group + "_wait")
        return _prepare_weights({n: _with_own(land, src, me) for n, src, land in zip(later[group], srcs, lands)})

    sent = []

    def send(grads):
        shards = _shard_grads(grads)
        names = list(shards)
        (state,), t = _async_start([[shards[n] for n in names]], "exchange", jnp.zeros((8, LANES), F32),
                                   "exchange_" + names[0] + "_start")
        sent.append((names, state))
        return t

    ada_cols = w_ada.shape[2]
    b_cols = lax.dynamic_slice(b_ada, (0, me * ada_cols), (1, ada_cols))
    mod_cols, silu_c = _mod_part(c_all, w_ada[0], b_cols)
    (mod_all,) = _all_gather([mod_cols], "gather_mod")
    mod = lax.dynamic_index_in_dim(mod_all, me, axis=1, keepdims=False).reshape(1, 6 * D) + tok[0, 0]

    vec = dict(g_pre_mix=g_pre_mix, g_post_mix=g_post_mix, g_pre_ffn=g_pre_ffn, g_post_ffn=g_post_ffn,
               g_q_lora=g_q_lora, g_kv_lora=g_kv_lora, b_forget=b_forget)
    pos = positions.astype(F32).reshape(S, 1)
    grad_x, small = _fwd_bwd(x[0], pos, mod, loss_target[0], w, vec, wts, send)

    bundle = jnp.concatenate([small[n] for n in SMALL_ORDER], axis=1)
    (bundle_all,) = _all_gather([bundle], "gather_small")
    bundle_all = bundle_all.reshape(N_DEV, 1, -1)
    out = {}
    dmod_all = bundle_all[:, 0, :6 * D]
    dm_cols = lax.dynamic_slice(dmod_all, (0, me * ada_cols), (N_DEV, ada_cols))
    g_ada = _w_ada_grad(jnp.transpose(silu_c), dm_cols)
    out["w_ada"] = _adamw(w_ada[0], m_w_ada[0], v_w_ada[0], g_ada[None], "adamw_w_ada")

    after = out["w_ada"][0]
    for names, state in sent:
        srcs, lands = _async_wait(state, after, "exchange_" + names[0] + "_wait")
        for n, src, land in zip(names, srcs, lands):
            parts = _with_own(land, lax.dynamic_index_in_dim(src, me, 0, keepdims=False), me)
            out[n] = _adamw(prm[n][0], mom[n][0], var[n][0], parts, "adamw_" + n)
            after = out[n][0]

    def row(src, n):
        if n == "err":
            return jnp.zeros((1, D), F32)
        return _pad_lanes(src[SMALL_PARAM.get(n, n)])
    wv, mv, vv = [jnp.concatenate([row(src, n) for n in SMALL_ORDER], axis=1) for src in (prm, mom, var)]
    gs, ds, ms, vs = _adamw(wv, mv, vv, bundle_all, "adamw_small")
    off = 0
    for n in SMALL_ORDER:
        width = small[n].shape[1]
        if n != "err":
            pn = SMALL_PARAM.get(n, n)
            real = prm[pn].shape[1]
            out[pn] = tuple(t[:, off:off + real] for t in (gs, ds, ms, vs))
        else:
            loss = 0.5 * jnp.sum(gs[0, off:off + width]) / D
        off += width

    res = [loss, grad_x[None]]
    for kind in range(4):
        for n in WEIGHTS:
            t = out[n][kind]
            res.append(t[None] if prm[n].ndim == 3 else t)
    return tuple(res)
```

```python
import functools
import math

import jax
import jax.numpy as jnp
from jax import lax
from jax.experimental import pallas as pl
from jax.experimental.pallas import tpu as pltpu

F32 = jnp.float32
BF16 = jnp.bfloat16

N_DEV = 8
S = 2048
D = 1024
D_FF = 2816
HEADS = 8
HEAD_DIM = 64
Q_LORA = 768
KV_LORA = 256
ROPE_DIM = 32
ROPE_THETA = 10000.0
NORM_EPS = 1e-6
LANES = 128
VMEM_LIMIT = 56 * 1024 * 1024

ADAM_LR = 0.001
ADAM_B1 = 0.9
ADAM_B2 = 0.999
ADAM_EPS = 1e-08
ADAM_WD = 0.01
ADAM_STEP = 10

ATT_T = 256
N_ATT = S // ATT_T

NN = (((1,), (0,)), ((), ()))
NT = (((1,), (1,)), ((), ()))
TN = (((0,), (0,)), ((), ()))
MESH = pl.DeviceIdType.MESH


def _params(sem=None):
    return pltpu.CompilerParams(dimension_semantics=sem, vmem_limit_bytes=VMEM_LIMIT)


def _pick(n, cap):
    best = None
    for t in range(LANES, cap + 1, LANES):
        if n % t == 0:
            best = t
    return best if best is not None else n


def _mm(a, b, mode, out_dtype, name, acc=None, dep=None):
    if mode == "nn":
        (m, k), (k2, n), dn = a.shape, b.shape, NN
    elif mode == "nt":
        (m, k), (n, k2), dn = a.shape, b.shape, NT
    else:
        (k, m), (k2, n), dn = a.shape, b.shape, TN
    assert k == k2, (a.shape, b.shape, mode)
    tn = _pick(n, 640)
    tm = _pick(m, 1024)
    osz = jnp.dtype(out_dtype).itemsize

    def need(tm_):
        blk = tm_ * k * 2 + tn * k * 2 + tm_ * tn * osz + (tm_ * tn * 4 if acc is not None else 0)
        return 2 * blk + tm_ * tn * 4
    while need(tm) > 36 * 1024 * 1024 and tm % 256 == 0:
        tm //= 2

    def body(*refs):
        a_ref, b_ref, o_ref = refs[0], refs[1], refs[-1]
        r = lax.dot_general(a_ref[...], b_ref[...], dn, preferred_element_type=F32)
        if acc is not None:
            r = r + refs[2][...]
        o_ref[...] = r.astype(o_ref.dtype)

    if mode == "tn":
        a_spec = pl.BlockSpec((k, tm), lambda i, j: (0, i))
    else:
        a_spec = pl.BlockSpec((tm, k), lambda i, j: (i, 0))
    if mode == "nt":
        b_spec = pl.BlockSpec((tn, k), lambda i, j: (j, 0))
    else:
        b_spec = pl.BlockSpec((k, tn), lambda i, j: (0, j))
    o_spec = pl.BlockSpec((tm, tn), lambda i, j: (i, j))
    in_specs = [a_spec, b_spec] + ([o_spec] if acc is not None else [])
    in_specs += [pl.BlockSpec(memory_space=pl.ANY)] if dep is not None else []
    args = (a, b) + ((acc,) if acc is not None else ()) + ((dep,) if dep is not None else ())
    return pl.pallas_call(
        body, name=name, grid=(m // tm, n // tn),
        in_specs=in_specs, out_specs=o_spec,
        out_shape=jax.ShapeDtypeStruct((m, n), out_dtype),
        compiler_params=_params(("parallel", "parallel")),
    )(*args)


def _rowwise(fn, row_ins, vec_ins, row_outs, sum_outs, name, tm=256):
    n_in = len(row_ins) + len(vec_ins)
    n_o = len(row_outs)
    rows = row_ins[0][0].shape[0]

    def body(*refs):
        vals = [r[...] for r in refs[:n_in]]
        outs = refs[n_in:]
        ro, so = fn(*vals)
        assert len(ro) == n_o and len(so) == len(sum_outs)
        for r, v in zip(outs[:n_o], ro):
            r[...] = v.astype(r.dtype)
        if sum_outs:
            @pl.when(pl.program_id(0) == 0)
            def _():
                for r in outs[n_o:]:
                    r[...] = jnp.zeros(r.shape, F32)
            for r, v in zip(outs[n_o:], so):
                r[...] += v

    in_specs = [pl.BlockSpec((tm, w), functools.partial(lambda i, b: (i, b), b=b)) for _, w, b in row_ins]
    in_specs += [pl.BlockSpec(v.shape, lambda i: (0, 0)) for v in vec_ins]
    out_specs = [pl.BlockSpec((tm, w), lambda i: (i, 0)) for w, _ in row_outs]
    out_specs += [pl.BlockSpec((1, w), lambda i: (0, 0)) for w in sum_outs]
    out_shape = [jax.ShapeDtypeStruct((rows, w), dt) for w, dt in row_outs]
    out_shape += [jax.ShapeDtypeStruct((1, w), F32) for w in sum_outs]
    return pl.pallas_call(
        body, name=name, grid=(rows // tm,),
        in_specs=in_specs, out_specs=out_specs, out_shape=out_shape,
        compiler_params=_params(("arbitrary",)),
    )(*[a for a, _, _ in row_ins], *vec_ins)


def _sigmoid(x):
    return 1.0 / (1.0 + jnp.exp(-x))


def _rstd(x):
    return lax.rsqrt(jnp.mean(x * x, axis=-1, keepdims=True) + NORM_EPS)


def _norm_bwd(dyn, xn, r):
    return r * (dyn - xn * jnp.mean(dyn * xn, axis=-1, keepdims=True))


def _colsum(x):
    return jnp.sum(x, axis=0, keepdims=True)


def _rope_tables(pos, invf):
    def fn(p, f):
        lane = lax.broadcasted_iota(jnp.int32, (1, LANES), 1)
        ang = p * f
        cs, sn = jnp.cos(ang), jnp.sin(ang)
        rot = (lane >= 64) & (lane < 96)
        ct = jnp.where(lane < 64, 1.0, jnp.where(rot, cs, 0.0))
        sa = jnp.where((lane >= 64) & (lane < 80), -sn, 0.0)
        sb = jnp.where((lane >= 80) & (lane < 96), sn, 0.0)
        return (ct, sa, sb), ()
    return _rowwise(fn, [(pos, 1, 0)], [invf], [(LANES, F32)] * 3, [], "rope_tables")


def _rope(x, ct, sa, sb):
    return x * ct + pltpu.roll(x, LANES - 16, 1) * sa + pltpu.roll(x, 16, 1) * sb


def _rope_t(x, ct, sa, sb):
    return x * ct - pltpu.roll(x, LANES - 16, 1) * sa - pltpu.roll(x, 16, 1) * sb


def _head_mask(width, hh):
    lane = lax.broadcasted_iota(jnp.int32, (1, width), 1)
    half = width // 2
    return (lane >= hh * half) & (lane < (hh + 1) * half)


ATT_PP = 2
ATT_CHAINS = [(a, hh) for a in range(ATT_PP) for hh in range(2)]
ATT_G = HEADS // (2 * ATT_PP)


def _pair(ref_or_val, a, width, rows=slice(None)):
    return ref_or_val[rows, a * width:(a + 1) * width]


def _attn_fwd(q, qo, k, ko, v, vo, dkp, scale, bias, name):
    T = ATT_T
    assert qo % ATT_PP == 0 and ko % ATT_PP == 0 and vo % ATT_PP == 0
    qo, ko, vo = qo // ATT_PP, ko // ATT_PP, vo // ATT_PP

    def body(*refs):
        if bias is not None:
            q_ref, k_ref, v_ref, b_ref, o_ref, lse_ref = refs
        else:
            q_ref, k_ref, v_ref, o_ref, lse_ref = refs
        i = pl.program_id(1)
        row = lax.broadcasted_iota(jnp.int32, (T, T), 0)
        col = lax.broadcasted_iota(jnp.int32, (T, T), 1)
        qms = []
        for a, hh in ATT_CHAINS:
            qb = _pair(q_ref, a, dkp)
            qms.append(jnp.where(_head_mask(dkp, hh), qb, jnp.zeros_like(qb)))

        def step(j, carry, masked):
            ks = pl.ds(pl.multiple_of(j * T, T), T)
            new = []
            for ci, (a, hh) in enumerate(ATT_CHAINS):
                m, l, acc = carry[ci]
                s = lax.dot_general(qms[ci], _pair(k_ref, a, dkp, ks), NT, preferred_element_type=F32) * scale
                if bias is not None:
                    s = s + b_ref[2 * a + hh, j]
                if masked:
                    s = jnp.where(row >= col, s, -jnp.inf)
                m_new = jnp.maximum(m, jnp.max(s, axis=1, keepdims=True))
                alpha = jnp.exp(m - m_new)
                p = jnp.exp(s - m_new)
                l = alpha * l + jnp.sum(p, axis=1, keepdims=True)
                acc = alpha * acc + lax.dot_general(p.astype(BF16), _pair(v_ref, a, LANES, ks), NN,
                                                    preferred_element_type=F32)
                new.append((m_new, l, acc))
            return tuple(new)

        init = tuple((jnp.full((T, 1), -jnp.inf, F32), jnp.zeros((T, 1), F32), jnp.zeros((T, LANES), F32))
                     for _ in ATT_CHAINS)
        carry = lax.fori_loop(0, i, functools.partial(step, masked=False), init)
        carry = step(i, carry, True)
        for a in range(ATT_PP):
            (m0, l0, acc0), (m1, l1, acc1) = carry[2 * a], carry[2 * a + 1]
            lse_ref[2 * a] = m0 + jnp.log(l0)
            lse_ref[2 * a + 1] = m1 + jnp.log(l1)
            o_ref[:, a * LANES:(a + 1) * LANES] = jnp.where(_head_mask(LANES, 0), acc0 / l0, acc1 / l1).astype(o_ref.dtype)

    in_specs = [
        pl.BlockSpec((T, ATT_PP * dkp), lambda g, i: (i, qo + g)),
        pl.BlockSpec((S, ATT_PP * dkp), lambda g, i: (0, ko + g)),
        pl.BlockSpec((S, ATT_PP * LANES), lambda g, i: (0, vo + g)),
    ]
    args = [q, k, v]
    if bias is not None:
        in_specs.append(pl.BlockSpec((2 * ATT_PP, N_ATT, 1, T), lambda g, i: (g, 0, 0, 0)))
        args.append(bias)
    return pl.pallas_call(
        body, name=name, grid=(ATT_G, N_ATT),
        in_specs=in_specs,
        out_specs=[pl.BlockSpec((T, ATT_PP * LANES), lambda g, i: (i, g)),
                   pl.BlockSpec((2 * ATT_PP, T, 1), lambda g, i: (g, i, 0))],
        out_shape=[jax.ShapeDtypeStruct((S, HEADS * HEAD_DIM), BF16),
                   jax.ShapeDtypeStruct((HEADS, S, 1), F32)],
        compiler_params=_params(("parallel", "arbitrary")),
    )(*args)


def _attn_delta(q, qo, k, ko, v, vo, do, lse, dkp, scale, bias, name):
    T = ATT_T
    qo, ko, vo = qo // ATT_PP, ko // ATT_PP, vo // ATT_PP

    def body(*refs):
        q_ref, k_ref, v_ref, do_ref, lse_ref = refs[:5]
        b_ref = refs[5] if bias is not None else None
        out_ref = refs[-1]
        i = pl.program_id(1)
        row = lax.broadcasted_iota(jnp.int32, (T, T), 0)
        col = lax.broadcasted_iota(jnp.int32, (T, T), 1)
        qms, doms, lses = [], [], []
        for a, hh in ATT_CHAINS:
            qb, dob = _pair(q_ref, a, dkp), _pair(do_ref, a, LANES)
            qms.append(jnp.where(_head_mask(dkp, hh), qb, jnp.zeros_like(qb)))
            doms.append(jnp.where(_head_mask(LANES, hh), dob, jnp.zeros_like(dob)))
            lses.append(lse_ref[2 * a + hh])

        def step(j, carry, masked):
            ks = pl.ds(pl.multiple_of(j * T, T), T)
            new = []
            for ci, (a, hh) in enumerate(ATT_CHAINS):
                s = lax.dot_general(qms[ci], _pair(k_ref, a, dkp, ks), NT, preferred_element_type=F32) * scale
                if bias is not None:
                    s = s + b_ref[2 * a + hh, j]
                s = s - lses[ci]
                if masked:
                    s = jnp.where(row >= col, s, -jnp.inf)
                dp = lax.dot_general(doms[ci], _pair(v_ref, a, LANES, ks), NT, preferred_element_type=F32)
                new.append(carry[ci] + jnp.sum(jnp.exp(s) * dp, axis=1, keepdims=True))
            return tuple(new)

        init = tuple(jnp.zeros((T, 1), F32) for _ in ATT_CHAINS)
        acc = step(i, lax.fori_loop(0, i, functools.partial(step, masked=False), init), True)
        for ci, (a, hh) in enumerate(ATT_CHAINS):
            out_ref[2 * a + hh] = acc[ci]

    in_specs = [
        pl.BlockSpec((T, ATT_PP * dkp), lambda g, i: (i, qo + g)),
        pl.BlockSpec((S, ATT_PP * dkp), lambda g, i: (0, ko + g)),
        pl.BlockSpec((S, ATT_PP * LANES), lambda g, i: (0, vo + g)),
        pl.BlockSpec((T, ATT_PP * LANES), lambda g, i: (i, g)),
        pl.BlockSpec((2 * ATT_PP, T, 1), lambda g, i: (g, i, 0)),
    ]
    args = [q, k, v, do, lse]
    if bias is not None:
        in_specs.append(pl.BlockSpec((2 * ATT_PP, N_ATT, 1, T), lambda g, i: (g, 0, 0, 0)))
        args.append(bias)
    return pl.pallas_call(
        body, name=name, grid=(ATT_G, N_ATT),
        in_specs=in_specs, out_specs=pl.BlockSpec((2 * ATT_PP, T, 1), lambda g, i: (g, i, 0)),
        out_shape=jax.ShapeDtypeStruct((HEADS, S, 1), F32),
        compiler_params=_params(("parallel", "arbitrary")),
    )(*args)


def _attn_bwd(q, qo, k, ko, v, vo, do, lse, delta_in, dkp, scale, bias, qk_dtype, name):
    T = ATT_T
    has_b = bias is not None
    qo, ko, vo = qo // ATT_PP, ko // ATT_PP, vo // ATT_PP

    def body(*refs):
        q_ref, k_ref, v_ref, do_ref, lse_ref, delta = refs[:6]
        refs = refs[6:]
        if has_b:
            b_ref, refs = refs[0], refs[1:]
        dq_ref, dk_ref, dv_ref = refs[:3]
        refs = refs[3:]
        if has_b:
            db_ref, refs = refs[0], refs[1:]
        dq_acc, dk_acc, dv_acc = refs
        j = pl.program_id(1)

        @pl.when(j == 0)
        def _():
            dq_acc[...] = jnp.zeros(dq_acc.shape, F32)

        dk_acc[...] = jnp.zeros(dk_acc.shape, F32)
        dv_acc[...] = jnp.zeros(dv_acc.shape, F32)
        row = lax.broadcasted_iota(jnp.int32, (T, T), 0)
        col = lax.broadcasted_iota(jnp.int32, (T, T), 1)
        kbs, vbs, kms = [], [], []
        for a in range(ATT_PP):
            kb = _pair(k_ref, a, dkp)
            kbs.append(kb)
            vbs.append(_pair(v_ref, a, LANES))
            kms.append(jnp.concatenate([jnp.where(_head_mask(dkp, hh), kb, jnp.zeros_like(kb)) for hh in range(2)], axis=0))

        def step(i, dbs, masked):
            rs = pl.ds(pl.multiple_of(i * T, T), T)
            new = list(dbs)
            for a in range(ATT_PP):
                qb = _pair(q_ref, a, dkp, rs)
                dob = _pair(do_ref, a, LANES, rs)
                qm2, dom2, p2, ds2 = [], [], [], []
                for hh in range(2):
                    h = 2 * a + hh
                    qm = jnp.where(_head_mask(dkp, hh), qb, jnp.zeros_like(qb))
                    dom = jnp.where(_head_mask(LANES, hh), dob, jnp.zeros_like(dob))
                    s = lax.dot_general(qm, kbs[a], NT, preferred_element_type=F32) * scale
                    if has_b:
                        s = s + b_ref[h, 0]
                    s = s - lse_ref[h, rs, :]
                    if masked:
                        s = jnp.where(row >= col, s, -jnp.inf)
                    p = jnp.exp(s)
                    dp = lax.dot_general(dom, vbs[a], NT, preferred_element_type=F32)
                    ds = p * (dp - delta[h, rs, :])
                    if has_b:
                        new[h] = dbs[h] + jnp.sum(ds, axis=0, keepdims=True)
                    qm2.append(qm)
                    dom2.append(dom)
                    p2.append(p.astype(BF16))
                    ds2.append((ds * scale).astype(BF16))
                dv_acc[:, a * LANES:(a + 1) * LANES] += lax.dot_general(
                    jnp.concatenate(p2, axis=0), jnp.concatenate(dom2, axis=0), TN, preferred_element_type=F32)
                dk_acc[:, a * dkp:(a + 1) * dkp] += lax.dot_general(
                    jnp.concatenate(ds2, axis=0), jnp.concatenate(qm2, axis=0), TN, preferred_element_type=F32)
                dq_acc[rs, a * dkp:(a + 1) * dkp] += lax.dot_general(
                    jnp.concatenate(ds2, axis=1), kms[a], NN, preferred_element_type=F32)
            return tuple(new)

        dbs = step(j, tuple(jnp.zeros((1, T), F32) for _ in ATT_CHAINS), True)
        dbs = lax.fori_loop(j + 1, N_ATT, functools.partial(step, masked=False), dbs)
        if has_b:
            for ci, (a, hh) in enumerate(ATT_CHAINS):
                db_ref[2 * a + hh, 0] = dbs[ci]
        dk_ref[...] = dk_acc[...].astype(dk_ref.dtype)
        dv_ref[...] = dv_acc[...].astype(dv_ref.dtype)

        @pl.when(j == N_ATT - 1)
        def _():
            dq_ref[...] = dq_acc[...].astype(dq_ref.dtype)

    in_specs = [
        pl.BlockSpec((S, ATT_PP * dkp), lambda g, j: (0, qo + g)),
        pl.BlockSpec((T, ATT_PP * dkp), lambda g, j: (j, ko + g)),
        pl.BlockSpec((T, ATT_PP * LANES), lambda g, j: (j, vo + g)),
        pl.BlockSpec((S, ATT_PP * LANES), lambda g, j: (0, g)),
        pl.BlockSpec((2 * ATT_PP, S, 1), lambda g, j: (g, 0, 0)),
        pl.BlockSpec((2 * ATT_PP, S, 1), lambda g, j: (g, 0, 0)),
    ]
    args = [q, k, v, do, lse, delta_in]
    out_specs = [
        pl.BlockSpec((S, ATT_PP * dkp), lambda g, j: (0, g)),
        pl.BlockSpec((T, ATT_PP * dkp), lambda g, j: (j, g)),
        pl.BlockSpec((T, ATT_PP * LANES), lambda g, j: (j, g)),
    ]
    width = (HEADS // 2) * dkp
    out_shape = [
        jax.ShapeDtypeStruct((S, width), qk_dtype),
        jax.ShapeDtypeStruct((S, width), qk_dtype),
        jax.ShapeDtypeStruct((S, HEADS * HEAD_DIM), BF16),
    ]
    if has_b:
        in_specs.append(pl.BlockSpec((2 * ATT_PP, 1, 1, T), lambda g, j: (g, j, 0, 0)))
        args.append(bias)
        out_specs.append(pl.BlockSpec((2 * ATT_PP, 1, 1, T), lambda g, j: (g, j, 0, 0)))
        out_shape.append(jax.ShapeDtypeStruct((HEADS, N_ATT, 1, T), F32))
    return pl.pallas_call(
        body, name=name, grid=(ATT_G, N_ATT),
        in_specs=in_specs, out_specs=out_specs, out_shape=out_shape,
        scratch_shapes=[pltpu.VMEM((S, ATT_PP * dkp), F32), pltpu.VMEM((T, ATT_PP * dkp), F32),
                        pltpu.VMEM((T, ATT_PP * LANES), F32)],
        compiler_params=_params(("parallel", "arbitrary")),
    )(*args)


def _tri(upper):
    a = lax.broadcasted_iota(jnp.int32, (LANES, LANES), 0)
    b = lax.broadcasted_iota(jnp.int32, (LANES, LANES), 1)
    return jnp.where(a <= b if upper else a >= b, 1.0, 0.0).astype(F32)


def _fox_gates(zt, bf):
    def body(z_ref, b_ref, o_ref):
        tri = _tri(True)
        carry = jnp.zeros((HEADS, 1), F32)
        for t in range(S // LANES):
            sl = slice(t * LANES, (t + 1) * LANES)
            z = z_ref[:, sl] + b_ref[...]
            logf = jnp.minimum(z, 0.0) - jnp.log(1.0 + jnp.exp(-jnp.abs(z)))
            c = lax.dot_general(logf, tri, NN, preferred_element_type=F32,
                                precision=lax.Precision.HIGHEST) + carry
            o_ref[:, sl] = -c
            carry = c[:, LANES - 1:LANES]

    return pl.pallas_call(
        body, name="fox_gates", out_shape=jax.ShapeDtypeStruct((HEADS, S), F32),
        compiler_params=_params(),
    )(zt, bf)


def _fox_gates_bwd(dbias, zt, bf):
    def body(d_ref, z_ref, b_ref, dz_ref, dbf_ref):
        tri = _tri(False)
        carry = jnp.zeros((HEADS, 1), F32)
        tot = jnp.zeros((HEADS, 1), F32)
        for t in reversed(range(S // LANES)):
            sl = slice(t * LANES, (t + 1) * LANES)
            df = -d_ref[:, sl]
            c = lax.dot_general(df, tri, NN, preferred_element_type=F32,
                                precision=lax.Precision.HIGHEST) + carry
            carry = c[:, 0:1]
            z = z_ref[:, sl] + b_ref[...]
            dz = c * _sigmoid(-z)
            dz_ref[:, sl] = dz
            tot = tot + jnp.sum(dz, axis=1, keepdims=True)
        dbf_ref[...] = tot

    return pl.pallas_call(
        body, name="fox_gates_bwd",
        out_shape=[jax.ShapeDtypeStruct((HEADS, S), F32), jax.ShapeDtypeStruct((HEADS, 1), F32)],
        compiler_params=_params(),
    )(dbias, zt, bf)


def _mod_part(c_all, w_ada, b_cols):
    def body(c_ref, w_ref, b_ref, o_ref, s_ref):
        c = c_ref[...]
        sc = c * _sigmoid(c)
        s_ref[...] = sc
        o_ref[...] = lax.dot_general(sc, w_ref[...], NN, preferred_element_type=F32,
                                     precision=lax.Precision.HIGHEST) + b_ref[...]

    return pl.pallas_call(
        body, name="mod_part",
        out_shape=[jax.ShapeDtypeStruct((N_DEV, w_ada.shape[1]), F32), jax.ShapeDtypeStruct(c_all.shape, F32)],
        compiler_params=_params(),
    )(c_all, w_ada, b_cols)


def _w_ada_grad(sc_t, dm):
    def body(s_ref, d_ref, o_ref):
        acc = jnp.zeros(o_ref.shape, F32)
        for b in range(N_DEV):
            acc = acc + s_ref[:, b:b + 1] * d_ref[b:b + 1, :]
        o_ref[...] = acc

    return pl.pallas_call(
        body, name="w_ada_grad", out_shape=jax.ShapeDtypeStruct((sc_t.shape[0], dm.shape[1]), F32),
        compiler_params=_params(),
    )(sc_t, dm)


def _adamw(w, m, v, parts, name):
    rows, cols = w.shape
    n = parts.shape[0]
    tr = rows if rows <= 512 else 256

    def body(w_ref, m_ref, v_ref, p_ref, g_out, d_out, m_out, v_out):
        g = p_ref[0].astype(F32)
        for kk in range(1, n):
            g = g + p_ref[kk].astype(F32)
        g_out[...] = g
        d_out[...], m_out[...], v_out[...] = _adamw_math(w_ref[...], g, m_ref[...], v_ref[...])

    spec = pl.BlockSpec((tr, cols), lambda i: (i, 0))
    return pl.pallas_call(
        body, name=name, grid=(rows // tr,),
        in_specs=[spec, spec, spec, pl.BlockSpec((n, tr, cols), lambda i: (0, i, 0))],
        out_specs=[spec] * 4, out_shape=[jax.ShapeDtypeStruct((rows, cols), F32)] * 4,
        compiler_params=_params(("parallel",)),
    )(w, m, v, parts)


def _adamw_math(w, g, m, v):
    mm = ADAM_B1 * m + (1.0 - ADAM_B1) * g
    vv = ADAM_B2 * v + (1.0 - ADAM_B2) * (g * g)
    m_hat = mm / (1.0 - ADAM_B1 ** ADAM_STEP)
    v_hat = vv / (1.0 - ADAM_B2 ** ADAM_STEP)
    return -ADAM_LR * (m_hat / (jnp.sqrt(v_hat) + ADAM_EPS) + ADAM_WD * w), mm, vv


def _adamw_rows(bundles, offsets, ws, ms, vs, err_off, err_width):
    k = len(ws)

    def body(*refs):
        b_ref = refs[0]
        w_refs, m_refs, v_refs = refs[1:1 + k], refs[1 + k:1 + 2 * k], refs[1 + 2 * k:1 + 3 * k]
        outs = refs[1 + 3 * k:]
        g_all = b_ref[0]
        for kk in range(1, N_DEV):
            g_all = g_all + b_ref[kk]
        for i in range(k):
            width = w_refs[i].shape[1]
            g = g_all[:, offsets[i]:offsets[i] + width]
            outs[4 * i][...] = g
            outs[4 * i + 1][...], outs[4 * i + 2][...], outs[4 * i + 3][...] = _adamw_math(
                w_refs[i][...], g, m_refs[i][...], v_refs[i][...])
        outs[4 * k][...] = g_all[:, err_off:err_off + err_width]

    out_shape = []
    for w_ in ws:
        out_shape += [jax.ShapeDtypeStruct(w_.shape, F32)] * 4
    out_shape.append(jax.ShapeDtypeStruct((1, err_width), F32))
    res = pl.pallas_call(body, name="adamw_rows", out_shape=out_shape, compiler_params=_params())(bundles, *ws, *ms, *vs)
    return [tuple(res[4 * i:4 * i + 4]) for i in range(k)], res[-1]


def _coords():
    return lax.axis_index("x"), lax.axis_index("y"), lax.axis_index("c")


def _flat(px, py, pc):
    return 4 * px + 2 * py + pc


def _all_gather(arrs, name):
    n = len(arrs)

    def body(*refs):
        ins, outs = refs[:n], refs[n:2 * n]
        send, recv, lsem = refs[2 * n:]
        x, y, c = _coords()
        me, sibling = (x, y, c), (x, y, 1 - c)
        chips = [(1 - x, y), (x, 1 - y), (1 - x, 1 - y)]

        def copy(a, kk, block, to, src=None):
            slot = outs[a].at[_flat(*block)]
            return pltpu.make_async_remote_copy(
                src_ref=slot if src is None else src, dst_ref=slot,
                send_sem=send.at[a, kk], recv_sem=recv.at[a, kk],
                device_id=to, device_id_type=MESH)

        mine = [pltpu.make_async_copy(ins[a], outs[a].at[_flat(*me)], lsem.at[a]) for a in range(n)]
        for cp in mine:
            cp.start()
        first = []
        for a in range(n):
            first.append(copy(a, 0, me, sibling, src=ins[a]))
            first += [copy(a, 1 + j, me, (*chip, c), src=ins[a]) for j, chip in enumerate(chips)]
        for cp in first:
            cp.start()
        passed = []
        for j, chip in enumerate(chips):
            for a in range(n):
                copy(a, 1 + j, (*chip, c), me).wait_recv()
                cp = copy(a, 4 + j, (*chip, c), sibling)
                cp.start()
                passed.append(cp)
        for a in range(n):
            copy(a, 0, sibling, me).wait_recv()
        for j, chip in enumerate(chips):
            for a in range(n):
                copy(a, 4 + j, (*chip, 1 - c), me).wait_recv()
        for cp in first + passed:
            cp.wait_send()
        for cp in mine:
            cp.wait()

    any_spec = pl.BlockSpec(memory_space=pl.ANY)
    return pl.pallas_call(
        body, name=name,
        in_specs=[any_spec] * n, out_specs=[any_spec] * n,
        out_shape=[jax.ShapeDtypeStruct((N_DEV,) + a.shape, a.dtype) for a in arrs],
        scratch_shapes=[pltpu.SemaphoreType.DMA((n, 7)), pltpu.SemaphoreType.DMA((n, 7)),
                        pltpu.SemaphoreType.DMA((n,))],
    )(*arrs)


def _exchange(arrs, name):
    n = len(arrs)

    def body(*refs):
        ins, outs = refs[:n], refs[n:2 * n]
        send, recv, lsem = refs[2 * n:]
        x, y, c = _coords()
        me = _flat(x, y, c)
        peers = []
        for r in range(1, N_DEV):
            px = 1 - x if r & 4 else x
            py = 1 - y if r & 2 else y
            pc = 1 - c if r & 1 else c
            peers.append((px, py, pc))

        def copy(a, kk):
            peer = peers[kk]
            return pltpu.make_async_remote_copy(
                src_ref=ins[a].at[_flat(*peer)], dst_ref=outs[a].at[me],
                send_sem=send.at[a, kk], recv_sem=recv.at[a, kk],
                device_id=peer, device_id_type=MESH)

        def arrival(a, kk):
            slot = outs[a].at[_flat(*peers[kk])]
            return pltpu.make_async_remote_copy(
                src_ref=slot, dst_ref=slot, send_sem=send.at[a, kk], recv_sem=recv.at[a, kk],
                device_id=peers[kk], device_id_type=MESH)

        mine = [pltpu.make_async_copy(ins[a].at[me], outs[a].at[me], lsem.at[a]) for a in range(n)]
        for cp in mine:
            cp.start()
        sends = [copy(a, kk) for a in range(n) for kk in range(N_DEV - 1)]
        for cp in sends:
            cp.start()
        for a in range(n):
            for kk in range(N_DEV - 1):
                arrival(a, kk).wait_recv()
        for cp in sends:
            cp.wait_send()
        for cp in mine:
            cp.wait()

    any_spec = pl.BlockSpec(memory_space=pl.ANY)
    return pl.pallas_call(
        body, name=name,
        in_specs=[any_spec] * n, out_specs=[any_spec] * n,
        out_shape=[jax.ShapeDtypeStruct(a.shape, a.dtype) for a in arrs],
        scratch_shapes=[pltpu.SemaphoreType.DMA((n, 7)), pltpu.SemaphoreType.DMA((n, 7)),
                        pltpu.SemaphoreType.DMA((n,))],
    )(*arrs)


def _peer_list():
    x, y, c = _coords()
    return [((1 - x if r & 4 else x), (1 - y if r & 2 else y), (1 - c if r & 1 else c)) for r in range(1, N_DEV)]


def _copy_for(mode, src, land, send, recv, peer, me):
    src_ref = src if mode == "gather" else src.at[_flat(*peer)]
    return pltpu.make_async_remote_copy(src_ref=src_ref, dst_ref=land.at[me], send_sem=send, recv_sem=recv,
                                        device_id=peer, device_id_type=MESH)


HBM_SPEC = pl.BlockSpec(memory_space=pltpu.HBM)
SEM_SPEC = pl.BlockSpec(memory_space=pltpu.SEMAPHORE)
ANY_SPEC = pl.BlockSpec(memory_space=pl.ANY)
SIDE_EFFECT = pltpu.SideEffectType.DATAFLOW_SIDE_EFFECTING


def _async_start(groups, mode, after, name):
    flat_arrs = [a for g in groups for a in g]
    n = len(flat_arrs)

    def body(*refs):
        srcs, lands = refs[:n], refs[n:2 * n]
        outs = refs[2 * n + 1:]
        token = outs[-1]
        me = _flat(*_coords())
        for ai in range(n):
            for peer in _peer_list():
                _copy_for(mode, srcs[ai], lands[ai], outs[2 * ai], outs[2 * ai + 1], peer, me).start()
        token[...] = jnp.zeros(token.shape, F32)

    land_shapes = [((N_DEV,) + a.shape if mode == "gather" else a.shape) for a in flat_arrs]
    out_shape = [pltpu.SemaphoreType.DMA(())] * (2 * n)
    out_shape += [pltpu.HBM(a.shape, a.dtype) for a in flat_arrs]
    out_shape += [pltpu.HBM(s, a.dtype) for s, a in zip(land_shapes, flat_arrs)]
    out_shape.append(jax.ShapeDtypeStruct((8, LANES), F32))
    res = pl.pallas_call(
        body, name=name, out_shape=tuple(out_shape),
        in_specs=[HBM_SPEC] * (2 * n) + [ANY_SPEC],
        out_specs=tuple([SEM_SPEC] * (2 * n) + [HBM_SPEC] * (2 * n) + [pl.BlockSpec(memory_space=pltpu.VMEM)]),
        input_output_aliases={i: 2 * n + i for i in range(2 * n)},
        compiler_params=pltpu.CompilerParams(has_side_effects=SIDE_EFFECT),
    )(*[pltpu.with_memory_space_constraint(a, pltpu.HBM) for a in flat_arrs],
      *[pltpu.with_memory_space_constraint(lax.empty(s, a.dtype), pltpu.HBM) for s, a in zip(land_shapes, flat_arrs)],
      after)
    sems, thru = res[:2 * n], res[2 * n:-1]
    states, idx = [], 0
    for g in groups:
        k = len(g)
        states.append((list(sems[2 * idx:2 * (idx + k):2]), list(sems[2 * idx + 1:2 * (idx + k):2]),
                       list(thru[idx:idx + k]), list(thru[n + idx:n + idx + k])))
        idx += k
    return states, res[-1]


def _async_wait(state, after, name):
    sends, recvs, srcs, lands = state
    g = len(srcs)

    def body(*refs):
        l_refs, sems = refs[g:2 * g], refs[2 * g:4 * g]
        for ai in range(g):
            seven = l_refs[ai].at[pl.ds(0, N_DEV - 1)]
            cp = pltpu.make_async_remote_copy(src_ref=seven, dst_ref=seven, send_sem=sems[ai], recv_sem=sems[g + ai],
                                              device_id=_coords(), device_id_type=MESH)
            cp.wait_send()
            cp.wait_recv()

    res = pl.pallas_call(
        body, name=name,
        out_shape=tuple([pltpu.HBM(a.shape, a.dtype) for a in srcs] + [pltpu.HBM(a.shape, a.dtype) for a in lands]),
        in_specs=[HBM_SPEC] * (2 * g) + [SEM_SPEC] * (2 * g) + [ANY_SPEC],
        out_specs=tuple([HBM_SPEC] * (2 * g)),
        input_output_aliases={i: i for i in range(2 * g)},
        compiler_params=pltpu.CompilerParams(has_side_effects=SIDE_EFFECT),
    )(*srcs, *lands, *sends, *recvs, after)
    return list(res[:g]), list(res[g:])


def _with_own(land, own, me):
    return lax.dynamic_update_index_in_dim(land, own, me, 0)


IN_SPLITS = (512, 512, 512, 8, 768, 256, 32, 1024, 1024)


def _unshard_cols(g):
    _, k, n = g.shape
    tr = min(k, 256)

    def body(g_ref, o_ref):
        o_ref[...] = jnp.concatenate([g_ref[j] for j in range(N_DEV)], axis=1)

    return pl.pallas_call(
        body, name="unshard_cols_%d" % n, grid=(k // tr,),
        in_specs=[pl.BlockSpec((N_DEV, tr, n), lambda i: (0, i, 0))],
        out_specs=pl.BlockSpec((tr, N_DEV * n), lambda i: (i, 0)),
        out_shape=jax.ShapeDtypeStruct((k, N_DEV * n), g.dtype),
        compiler_params=_params(("parallel",)),
    )(g)


def _shard_cols(w):
    k, n = w.shape[0], w.shape[1] // N_DEV
    tr = min(k, 256)

    def body(w_ref, o_ref):
        full = w_ref[...]
        for j in range(N_DEV):
            o_ref[j] = full[:, j * n:(j + 1) * n]

    return pl.pallas_call(
        body, name="shard_cols_%d" % n, grid=(k // tr,),
        in_specs=[pl.BlockSpec((tr, N_DEV * n), lambda i: (i, 0))],
        out_specs=pl.BlockSpec((N_DEV, tr, n), lambda i: (0, i, 0)),
        out_shape=jax.ShapeDtypeStruct((N_DEV, k, n), w.dtype),
        compiler_params=_params(("parallel",)),
    )(w)


IN_OFFS = tuple(sum(IN_SPLITS[:i]) for i in range(len(IN_SPLITS) + 1))
IN_SHARD = IN_OFFS[-1] // N_DEV
REGROUP_ROWS = 128


def _w_in_regroup(g):
    def body(g_ref, a_ref, b_ref):
        full = jnp.concatenate([g_ref[j] for j in range(N_DEV)], axis=1)
        fq, fk, fv, wf, cq, ckv, kr, gf, gm = [full[:, IN_OFFS[i]:IN_OFFS[i + 1]] for i in range(9)]
        rows = full.shape[0]
        a_ref[...] = jnp.concatenate([cq, ckv, gf, gm, wf, jnp.zeros((rows, 56), BF16), kr, jnp.zeros((rows, 32), BF16)], axis=1)
        b_ref[...] = jnp.concatenate([fq, fk, fv], axis=1)

    tr = REGROUP_ROWS
    return pl.pallas_call(
        body, name="w_in_regroup", grid=(D // tr,),
        in_specs=[pl.BlockSpec((N_DEV, tr, IN_SHARD), lambda i: (0, i, 0))],
        out_specs=[pl.BlockSpec((tr, 3200), lambda i: (i, 0)), pl.BlockSpec((tr, 1536), lambda i: (i, 0))],
        out_shape=[jax.ShapeDtypeStruct((D, 3200), BF16), jax.ShapeDtypeStruct((D, 1536), BF16)],
        compiler_params=_params(("parallel",)),
    )(g)


def _w_in_ungroup(da, db_):
    def body(a_ref, b_ref, o_ref):
        a = a_ref[...]
        full = jnp.concatenate([b_ref[...], a[:, 3072:3080], a[:, 0:768], a[:, 768:1024], a[:, 3136:3168],
                                a[:, 1024:3072]], axis=1)
        for j in range(N_DEV):
            o_ref[j] = full[:, j * IN_SHARD:(j + 1) * IN_SHARD]

    tr = REGROUP_ROWS
    return pl.pallas_call(
        body, name="w_in_ungroup", grid=(D // tr,),
        in_specs=[pl.BlockSpec((tr, 3200), lambda i: (i, 0)), pl.BlockSpec((tr, 1536), lambda i: (i, 0))],
        out_specs=pl.BlockSpec((N_DEV, tr, IN_SHARD), lambda i: (0, i, 0)),
        out_shape=jax.ShapeDtypeStruct((N_DEV, D, IN_SHARD), BF16),
        compiler_params=_params(("parallel",)),
    )(da, db_)


def _prepare_weights(g):
    w = {}
    if "w_in" in g:
        w["w_a"], w["w_b"] = _w_in_regroup(g["w_in"])
    if "w_uq" in g:
        w_uq = g["w_uq"].reshape(Q_LORA, HEADS, 96)
        w["w_uq"] = jnp.pad(w_uq, ((0, 0), (0, 0), (0, 32))).reshape(Q_LORA, HEADS * LANES)
        ukv = g["w_ukv"]
        w["w_k"] = jnp.transpose(jnp.pad(ukv[:, :, :64], ((0, 0), (0, 0), (0, 64))), (1, 0, 2)).reshape(KV_LORA, HEADS * LANES)
        w["w_v"] = jnp.transpose(ukv[:, :, 64:], (1, 0, 2)).reshape(KV_LORA, HEADS * HEAD_DIM)
        w["w_kv"] = jnp.concatenate([w["w_k"], w["w_v"]], axis=1)
    if "w_out" in g:
        w["w_pf"] = _unshard_cols(g["w_proj_fox"])
        w["w_pm"] = _unshard_cols(g["w_proj_mla"])
        w["w_out"] = g["w_out"].reshape(D, D)
    if "w_ffn_in" in g:
        w["w_ffn_in"] = _unshard_cols(g["w_ffn_in"])
        w["w_ffn_out"] = g["w_ffn_out"].reshape(D_FF, D)
    return w


def _shard_grads(dw):
    out = {}
    if "w_a" in dw:
        out["w_in"] = _w_in_ungroup(dw["w_a"], dw["w_b"])
    if "w_uq" in dw:
        w_uq = dw["w_uq"].reshape(Q_LORA, HEADS, LANES)[:, :, :96].reshape(Q_LORA, Q_LORA)
        out["w_uq"] = w_uq.reshape(N_DEV, Q_LORA // N_DEV, Q_LORA)
        k_part = dw["w_k"].reshape(KV_LORA, HEADS, LANES)[:, :, :64]
        v_part = dw["w_v"].reshape(KV_LORA, HEADS, HEAD_DIM)
        out["w_ukv"] = jnp.transpose(jnp.concatenate([k_part, v_part], axis=2), (1, 0, 2))
    if "w_out" in dw:
        out["w_proj_fox"] = _shard_cols(dw["w_pf"])
        out["w_proj_mla"] = _shard_cols(dw["w_pm"])
        out["w_out"] = dw["w_out"].reshape(N_DEV, D // N_DEV, D)
    if "w_ffn_in" in dw:
        out["w_ffn_in"] = _shard_cols(dw["w_ffn_in"])
        out["w_ffn_out"] = dw["w_ffn_out"].reshape(N_DEV, D_FF // N_DEV, D)
    return out


def _fwd_bwd(x, pos, mod, target, w, vec, wts, send):
    shift_mix, scale_mix, gate_mix, shift_ffn, scale_ffn, gate_ffn = [mod[:, i * D:(i + 1) * D] for i in range(6)]
    g_pre_mix, g_post_mix, g_pre_ffn, g_post_ffn = vec["g_pre_mix"], vec["g_post_mix"], vec["g_pre_ffn"], vec["g_post_ffn"]
    g_q, g_kv = vec["g_q_lora"], vec["g_kv_lora"]

    inv_freq = 1.0 / (ROPE_THETA ** (jnp.arange(0, ROPE_DIM, 2, dtype=F32) / ROPE_DIM))
    invf = jnp.concatenate([jnp.zeros((64,), F32), inv_freq, inv_freq, jnp.zeros((32,), F32)]).reshape(1, LANES)
    ct, sa, sb = _rope_tables(pos, invf)

    def pre1(xv, g, sc, sh):
        return ((xv * _rstd(xv) * g) * (1.0 + sc) + sh,), ()
    (h,) = _rowwise(pre1, [(x, D, 0)], [g_pre_mix, scale_mix, shift_mix], [(D, BF16)], [], "pre_mix")
    proj_a = _mm(h, w["w_a"], "nn", F32, "in_proj_a")
    qkv = _mm(h, w["w_b"], "nn", BF16, "in_proj_b")

    def lora_norm(cq, ckv, gq, gkv):
        return (cq * _rstd(cq) * gq, ckv * _rstd(ckv) * gkv), ()
    cqn, ckvn = _rowwise(lora_norm, [(proj_a, Q_LORA, 0), (proj_a, KV_LORA, 3)], [g_q, g_kv],
                         [(Q_LORA, BF16), (KV_LORA, BF16)], [], "lora_norm")
    w = {**w, **wts("lora", cqn)}
    qb = _mm(cqn, w["w_uq"], "nn", F32, "mla_uq")
    kvb = _mm(ckvn, w["w_kv"], "nn", F32, "mla_ukv")

    def mla_rope(qv, kv, vv, misc, c_, a_, b_):
        lane = lax.broadcasted_iota(jnp.int32, (1, LANES), 1)
        kpe = jnp.where((lane >= 64) & (lane < 96), _rope(misc, c_, a_, b_), 0.0)
        qs = [_rope(qv[:, hd * LANES:(hd + 1) * LANES], c_, a_, b_) for hd in range(HEADS)]
        ks = [kv[:, hd * LANES:(hd + 1) * LANES] + kpe for hd in range(HEADS)]
        return (jnp.concatenate(qs, axis=1), jnp.concatenate(ks, axis=1), vv), ()
    q_m, k_m, v_m = _rowwise(
        mla_rope, [(qb, D, 0), (kvb, D, 0), (kvb, 512, 2), (proj_a, LANES, 24), (ct, LANES, 0), (sa, LANES, 0), (sb, LANES, 0)],
        [], [(D, BF16), (D, BF16), (512, BF16)], [], "mla_rope")

    zt = jnp.transpose(proj_a[:, 3072:3080])
    bf = jnp.transpose(vec["b_forget"])
    neg_f = _fox_gates(zt, bf)
    bias = neg_f.reshape(HEADS, N_ATT, 1, ATT_T)
    o_a, lse_a = _attn_fwd(qkv, 0, qkv, 4, qkv, 8, LANES, 1.0 / math.sqrt(HEAD_DIM), bias, "fox_attn")
    o_b, lse_b = _attn_fwd(q_m, 0, k_m, 0, v_m, 0, 2 * LANES, 1.0 / math.sqrt(64 + ROPE_DIM), None, "mla_attn")

    w = {**w, **wts("proj", o_b)}
    pa = _mm(o_a, w["w_pf"], "nn", F32, "proj_fox")
    pb = _mm(o_b, w["w_pm"], "nn", F32, "proj_mla")

    def merge(gf, gm, pa_, pb_):
        return (_sigmoid(gf) * pa_ + _sigmoid(gm) * pb_,), ()
    (merged,) = _rowwise(merge, [(proj_a, D, 1), (proj_a, D, 2), (pa, D, 0), (pb, D, 0)], [], [(D, BF16)], [], "merge")
    y = _mm(merged, w["w_out"], "nn", F32, "out_proj")

    def post1(xv, yv, gate, gpost, gpre, sc, sh):
        x1 = xv + gate * (yv * _rstd(yv) * gpost)
        return (x1, (x1 * _rstd(x1) * gpre) * (1.0 + sc) + sh), ()
    x1, h2 = _rowwise(post1, [(x, D, 0), (y, D, 0)], [gate_mix, g_post_mix, g_pre_ffn, scale_ffn, shift_ffn],
                      [(D, F32), (D, BF16)], [], "post_mix")
    w = {**w, **wts("ffn", h2)}
    gu = _mm(h2, w["w_ffn_in"], "nn", F32, "ffn_in")

    def swiglu(g, u):
        return (g * _sigmoid(g) * u,), ()
    (act,) = _rowwise(swiglu, [(gu, D_FF, 0), (gu, D_FF, 1)], [], [(D_FF, BF16)], [], "swiglu")
    y2 = _mm(act, w["w_ffn_out"], "nn", F32, "ffn_out")

    def head(x1v, y2v, tv, gate, gpost):
        r = _rstd(y2v)
        yn = y2v * r
        n2 = yn * gpost
        err = (x1v + gate * n2) - tv
        dx2 = err * (1.0 / D)
        dn2 = dx2 * gate
        dy2 = _norm_bwd(dn2 * gpost, yn, r)
        return (dx2, dy2), (_colsum(err * err), _colsum(dx2 * n2), _colsum(dn2 * yn))
    dx2, dy2, err_cols, d_gate_ffn, d_g_post_ffn = _rowwise(
        head, [(x1, D, 0), (y2, D, 0), (target, D, 0)], [gate_ffn, g_post_ffn], [(D, F32), (D, BF16)], [D, D, D], "loss_head")

    dact = _mm(dy2, w["w_ffn_out"], "nt", F32, "ffn_out_dx")
    dw = {"w_ffn_out": _mm(act, dy2, "tn", BF16, "ffn_out_dw")}

    def swiglu_bwd(g, u, da):
        sg = _sigmoid(g)
        return (jnp.concatenate([da * u * (sg * (1.0 + g * (1.0 - sg))), da * (g * sg)], axis=1),), ()
    (dgu,) = _rowwise(swiglu_bwd, [(gu, D_FF, 0), (gu, D_FF, 1), (dact, D_FF, 0)], [], [(2 * D_FF, BF16)], [], "swiglu_bwd")
    dh2 = _mm(dgu, w["w_ffn_in"], "nt", F32, "ffn_in_dx")
    dw["w_ffn_in"] = _mm(h2, dgu, "tn", BF16, "ffn_in_dw")
    gate_mix = gate_mix + send({n: dw.pop(n) for n in ("w_ffn_in", "w_ffn_out")})[0, 0]

    def mid(dh, x1v, dx2v, yv, gpre, sc, gate, gpost):
        r2 = _rstd(x1v)
        x1n = x1v * r2
        t = dh * x1n
        dx1 = dx2v + _norm_bwd(dh * (gpre * (1.0 + sc)), x1n, r2)
        ry = _rstd(yv)
        yn = yv * ry
        dn1 = dx1 * gate
        dy = _norm_bwd(dn1 * gpost, yn, ry)
        sums = (_colsum(dh), _colsum(t) * gpre, _colsum(t) * (1.0 + sc), _colsum(dx1 * (yn * gpost)), _colsum(dn1 * yn))
        return (dx1, dy), sums
    dx1, dy, d_shift_ffn, d_scale_ffn, d_g_pre_ffn, d_gate_mix, d_g_post_mix = _rowwise(
        mid, [(dh2, D, 0), (x1, D, 0), (dx2, D, 0), (y, D, 0)], [g_pre_ffn, scale_ffn, gate_mix, g_post_mix],
        [(D, F32), (D, BF16)], [D] * 5, "mid_bwd")

    dmerged = _mm(dy, w["w_out"], "nt", F32, "out_proj_dx")
    dw["w_out"] = _mm(merged, dy, "tn", BF16, "out_proj_dw")

    def merge_bwd(dm, gf, gm, pa_, pb_):
        sf, sm = _sigmoid(gf), _sigmoid(gm)
        dgates = jnp.concatenate([dm * pa_ * (sf * (1.0 - sf)), dm * pb_ * (sm * (1.0 - sm))], axis=1)
        return (dm * sf, dm * sm, dgates), ()
    dpa, dpb, dgates = _rowwise(
        merge_bwd, [(dmerged, D, 0), (proj_a, D, 1), (proj_a, D, 2), (pa, D, 0), (pb, D, 0)], [],
        [(D, BF16), (D, BF16), (2 * D, BF16)], [], "merge_bwd")
    do_a = _mm(dpa, w["w_pf"], "nt", BF16, "proj_fox_dx")
    do_b = _mm(dpb, w["w_pm"], "nt", BF16, "proj_mla_dx")
    dw["w_pf"] = _mm(o_a, dpa, "tn", BF16, "proj_fox_dw")
    dw["w_pm"] = _mm(o_b, dpb, "tn", BF16, "proj_mla_dw")
    bias = bias + send({n: dw.pop(n) for n in ("w_out", "w_pf", "w_pm")})[0, 0]

    sc_a, sc_b = 1.0 / math.sqrt(HEAD_DIM), 1.0 / math.sqrt(64 + ROPE_DIM)
    delta_a = _attn_delta(qkv, 0, qkv, 4, qkv, 8, do_a, lse_a, LANES, sc_a, bias, "fox_attn_delta")
    delta_b = _attn_delta(q_m, 0, k_m, 0, v_m, 0, do_b, lse_b, 2 * LANES, sc_b, None, "mla_attn_delta")
    dq_a, dk_a, dv_a, dbias = _attn_bwd(qkv, 0, qkv, 4, qkv, 8, do_a, lse_a, delta_a, LANES, sc_a, bias, BF16, "fox_attn_bwd")
    dq_m, dk_m, dv_m = _attn_bwd(q_m, 0, k_m, 0, v_m, 0, do_b, lse_b, delta_b, 2 * LANES, sc_b, None, F32, "mla_attn_bwd")

    def mla_rope_bwd(dq, dk, c_, a_, b_):
        lane = lax.broadcasted_iota(jnp.int32, (1, LANES), 1)
        dqs = [_rope_t(dq[:, hd * LANES:(hd + 1) * LANES], c_, a_, b_) for hd in range(HEADS)]
        dkpe = dk[:, 0:LANES]
        for hd in range(1, HEADS):
            dkpe = dkpe + dk[:, hd * LANES:(hd + 1) * LANES]
        dkpe = jnp.where((lane >= 64) & (lane < 96), dkpe, 0.0)
        dkr = jnp.where((lane >= 64) & (lane < 96), _rope_t(dkpe, c_, a_, b_), 0.0)
        return (jnp.concatenate(dqs, axis=1), dk, dkr), ()
    dqb, dkb, dkr = _rowwise(mla_rope_bwd, [(dq_m, D, 0), (dk_m, D, 0), (ct, LANES, 0), (sa, LANES, 0), (sb, LANES, 0)],
                             [], [(D, BF16), (D, BF16), (LANES, F32)], [], "mla_rope_bwd")
    dcqn = _mm(dqb, w["w_uq"], "nt", F32, "mla_uq_dx")
    dw["w_uq"] = _mm(cqn, dqb, "tn", BF16, "mla_uq_dw")
    dckvn = _mm(dv_m, w["w_v"], "nt", F32, "mla_uv_dx", acc=_mm(dkb, w["w_k"], "nt", F32, "mla_uk_dx"))
    dw["w_k"] = _mm(ckvn, dkb, "tn", BF16, "mla_uk_dw")
    dw["w_v"] = _mm(ckvn, dv_m, "tn", BF16, "mla_uv_dw")

    def lora_norm_bwd(cq, ckv, dq, dkv, gq, gkv):
        rq, rk = _rstd(cq), _rstd(ckv)
        cqh, ckh = cq * rq, ckv * rk
        return (_norm_bwd(dq * gq, cqh, rq), _norm_bwd(dkv * gkv, ckh, rk)), (_colsum(dq * cqh), _colsum(dkv * ckh))
    dcq, dckv, d_g_q, d_g_kv = _rowwise(
        lora_norm_bwd, [(proj_a, Q_LORA, 0), (proj_a, KV_LORA, 3), (dcqn, Q_LORA, 0), (dckvn, KV_LORA, 0)], [g_q, g_kv],
        [(Q_LORA, BF16), (KV_LORA, BF16)], [Q_LORA, KV_LORA], "lora_norm_bwd")

    dzt, d_bf = _fox_gates_bwd(dbias.reshape(HEADS, S), zt, bf)
    dmisc = (dkr + jnp.pad(jnp.transpose(dzt), ((0, 0), (0, LANES - HEADS)))).astype(BF16)
    dproj_a = jnp.concatenate([dcq, dckv, dgates, dmisc], axis=1)
    dqkv = jnp.concatenate([dq_a, dk_a, dv_a], axis=1)
    dw["w_a"] = _mm(h, dproj_a, "tn", BF16, "in_proj_a_dw")
    dw["w_b"] = _mm(h, dqkv, "tn", BF16, "in_proj_b_dw")
    tok = send(dw)
    dh = _mm(dqkv, w["w_b"], "nt", F32, "in_proj_b_dx", acc=_mm(dproj_a, w["w_a"], "nt", F32, "in_proj_a_dx", dep=tok))

    def first(dhv, xv, dx1v, gpre, sc):
        r = _rstd(xv)
        xn = xv * r
        t = dhv * xn
        dx = dx1v + _norm_bwd(dhv * (gpre * (1.0 + sc)), xn, r)
        return (dx,), (_colsum(dhv), _colsum(t) * gpre, _colsum(t) * (1.0 + sc))
    grad_x, d_shift_mix, d_scale_mix, d_g_pre_mix = _rowwise(
        first, [(dh, D, 0), (x, D, 0), (dx1, D, 0)], [g_pre_mix, scale_mix], [(D, F32)], [D] * 3, "pre_mix_bwd")

    dmod = jnp.concatenate([d_shift_mix, d_scale_mix, d_gate_mix, d_shift_ffn, d_scale_ffn, d_gate_ffn], axis=1)
    small = dict(dmod=dmod, g_pre_mix=d_g_pre_mix, g_post_mix=d_g_post_mix, g_pre_ffn=d_g_pre_ffn,
                 g_post_ffn=d_g_post_ffn, g_q_lora=d_g_q, g_kv_lora=d_g_kv,
                 b_forget=jnp.pad(jnp.transpose(d_bf), ((0, 0), (0, LANES - HEADS))), err=err_cols)
    return grad_x, small


SMALL_ORDER = ("dmod", "g_pre_mix", "g_post_mix", "g_pre_ffn", "g_post_ffn", "g_q_lora", "g_kv_lora", "b_forget", "err")
SMALL_PARAM = {"dmod": "b_ada"}
MATRICES = ("w_in", "w_uq", "w_ukv", "w_proj_fox", "w_proj_mla", "w_out", "w_ffn_in", "w_ffn_out")
WEIGHTS = ("w_ada", "b_ada", "g_pre_mix", "g_post_mix", "g_pre_ffn", "g_post_ffn", "w_in", "b_forget", "g_q_lora",
           "w_uq", "g_kv_lora", "w_ukv", "w_proj_fox", "w_proj_mla", "w_out", "w_ffn_in", "w_ffn_out")


def _pad_lanes(v):
    return jnp.pad(v, ((0, 0), (0, (-v.shape[1]) % LANES)))


def kernel(x, c, positions, w_ada, b_ada, g_pre_mix, g_post_mix, g_pre_ffn, g_post_ffn, w_in, b_forget, g_q_lora, w_uq, g_kv_lora, w_ukv, w_proj_fox, w_proj_mla, w_out, w_ffn_in, w_ffn_out, loss_target, m_w_ada, m_b_ada, m_g_pre_mix, m_g_post_mix, m_g_pre_ffn, m_g_post_ffn, m_w_in, m_b_forget, m_g_q_lora, m_w_uq, m_g_kv_lora, m_w_ukv, m_w_proj_fox, m_w_proj_mla, m_w_out, m_w_ffn_in, m_w_ffn_out, v_w_ada, v_b_ada, v_g_pre_mix, v_g_post_mix, v_g_pre_ffn, v_g_post_ffn, v_w_in, v_b_forget, v_g_q_lora, v_w_uq, v_g_kv_lora, v_w_ukv, v_w_proj_fox, v_w_proj_mla, v_w_out, v_w_ffn_in, v_w_ffn_out):
    prm = dict(w_ada=w_ada, b_ada=b_ada, g_pre_mix=g_pre_mix, g_post_mix=g_post_mix, g_pre_ffn=g_pre_ffn,
               g_post_ffn=g_post_ffn, w_in=w_in, b_forget=b_forget, g_q_lora=g_q_lora, w_uq=w_uq, g_kv_lora=g_kv_lora,
               w_ukv=w_ukv, w_proj_fox=w_proj_fox, w_proj_mla=w_proj_mla, w_out=w_out, w_ffn_in=w_ffn_in, w_ffn_out=w_ffn_out)
    mom = dict(w_ada=m_w_ada, b_ada=m_b_ada, g_pre_mix=m_g_pre_mix, g_post_mix=m_g_post_mix, g_pre_ffn=m_g_pre_ffn,
               g_post_ffn=m_g_post_ffn, w_in=m_w_in, b_forget=m_b_forget, g_q_lora=m_g_q_lora, w_uq=m_w_uq,
               g_kv_lora=m_g_kv_lora, w_ukv=m_w_ukv, w_proj_fox=m_w_proj_fox, w_proj_mla=m_w_proj_mla, w_out=m_w_out,
               w_ffn_in=m_w_ffn_in, w_ffn_out=m_w_ffn_out)
    var = dict(w_ada=v_w_ada, b_ada=v_b_ada, g_pre_mix=v_g_pre_mix, g_post_mix=v_g_post_mix, g_pre_ffn=v_g_pre_ffn,
               g_post_ffn=v_g_post_ffn, w_in=v_w_in, b_forget=v_b_forget, g_q_lora=v_g_q_lora, w_uq=v_w_uq,
               g_kv_lora=v_g_kv_lora, w_ukv=v_w_ukv, w_proj_fox=v_w_proj_fox, w_proj_mla=v_w_proj_mla, w_out=v_w_out,
               w_ffn_in=v_w_ffn_in, w_ffn_out=v_w_ffn_out)
    me = _flat(*_coords())

    own = {n: prm[n][0].astype(BF16) for n in MATRICES}
    w_in_all, c_all = _all_gather([own["w_in"], c], "gather_in")
    w = _prepare_weights({"w_in": w_in_all})
    c_all = c_all.reshape(N_DEV, D)
    later = dict(lora=("w_uq", "w_ukv"), proj=("w_proj_fox", "w_proj_mla", "w_out"), ffn=("w_ffn_in", "w_ffn_out"))
    states, tok = _async_start([[own[n] for n in names] for names in later.values()], "gather", w_in_all, "gather_rest_start")
    gather_state = dict(zip(later, states))

    def wts(group, after):
        srcs, lands = _async_wait(gather_state[group], after, "gather_" + group + "_wait")
        return _prepare_weights({n: _with_own(land, src, me) for n, src, land in zip(later[group], srcs, lands)})

    sent = []

    def send(grads):
        shards = _shard_grads(grads)
        names = list(shards)
        (state,), t = _async_start([[shards[n] for n in names]], "exchange", jnp.zeros((8, LANES), F32),
                                   "exchange_" + names[0] + "_start")
        sent.append((names, state))
        return t

    ada_cols = w_ada.shape[2]
    b_cols = lax.dynamic_slice(b_ada, (0, me * ada_cols), (1, ada_cols))
    mod_cols, silu_c = _mod_part(c_all, w_ada[0], b_cols)
    (mod_all,) = _all_gather([mod_cols], "gather_mod")
    mod = lax.dynamic_index_in_dim(mod_all, me, axis=1, keepdims=False).reshape(1, 6 * D) + tok[0, 0]

    vec = dict(g_pre_mix=g_pre_mix, g_post_mix=g_post_mix, g_pre_ffn=g_pre_ffn, g_post_ffn=g_post_ffn,
               g_q_lora=g_q_lora, g_kv_lora=g_kv_lora, b_forget=b_forget)
    pos = positions.astype(F32).reshape(S, 1)
    grad_x, small = _fwd_bwd(x[0], pos, mod, loss_target[0], w, vec, wts, send)

    bundle = jnp.concatenate([small[n] for n in SMALL_ORDER], axis=1)
    (small_state,), tok = _async_start([[bundle]], "gather", jnp.zeros((8, LANES), F32), "gather_small_start")

    out = {}
    after = tok
    for names, state in sent:
        srcs, lands = _async_wait(state, after, "exchange_" + names[0] + "_wait")
        for n, src, land in zip(names, srcs, lands):
            parts = _with_own(land, lax.dynamic_index_in_dim(src, me, 0, keepdims=False), me)
            out[n] = _adamw(prm[n][0], mom[n][0], var[n][0], parts, "adamw_" + n)
            after = out[n][0]

    (own_bundle,), (bundle_all,) = _async_wait(small_state, after, "gather_small_wait")
    bundle_all = _with_own(bundle_all, own_bundle, me)
    dmod_all = bundle_all[:, 0, :6 * D]
    dm_cols = lax.dynamic_slice(dmod_all, (0, me * ada_cols), (N_DEV, ada_cols))
    g_ada = _w_ada_grad(jnp.transpose(silu_c), dm_cols)
    out["w_ada"] = _adamw(w_ada[0], m_w_ada[0], v_w_ada[0], g_ada[None], "adamw_w_ada")

    offsets, off = {}, 0
    for n in SMALL_ORDER:
        offsets[n] = off
        off += small[n].shape[1]
    names = [SMALL_PARAM.get(n, n) for n in SMALL_ORDER if n != "err"]
    results, err = _adamw_rows(bundle_all, [offsets[n] for n in SMALL_ORDER if n != "err"],
                               [prm[n] for n in names], [mom[n] for n in names], [var[n] for n in names],
                               offsets["err"], D)
    out.update(zip(names, results))
    loss = 0.5 * jnp.sum(err) / D

    res = [loss, grad_x[None]]
    for kind in range(4):
        for n in WEIGHTS:
            t = out[n][kind]
            res.append(t[None] if prm[n].ndim == 3 else t)
    return tuple(res)
```

```python
import functools
import math

import jax
import jax.numpy as jnp
from jax import lax
from jax.experimental import pallas as pl
from jax.experimental.pallas import tpu as pltpu

F32 = jnp.float32
BF16 = jnp.bfloat16

N_DEV = 8
S = 2048
D = 1024
D_FF = 2816
HEADS = 8
HEAD_DIM = 64
Q_LORA = 768
KV_LORA = 256
ROPE_DIM = 32
ROPE_THETA = 10000.0
NORM_EPS = 1e-6
LANES = 128
VMEM_LIMIT = 56 * 1024 * 1024

ADAM_LR = 0.001
ADAM_B1 = 0.9
ADAM_B2 = 0.999
ADAM_EPS = 1e-08
ADAM_WD = 0.01
ADAM_STEP = 10

ATT_T = 256
N_ATT = S // ATT_T

NN = (((1,), (0,)), ((), ()))
NT = (((1,), (1,)), ((), ()))
TN = (((0,), (0,)), ((), ()))
MESH = pl.DeviceIdType.MESH


def _params(sem=None):
    return pltpu.CompilerParams(dimension_semantics=sem, vmem_limit_bytes=VMEM_LIMIT)


def _pick(n, cap):
    best = None
    for t in range(LANES, cap + 1, LANES):
        if n % t == 0:
            best = t
    return best if best is not None else n


def _mm(a, b, mode, out_dtype, name, acc=None, dep=None):
    if mode == "nn":
        (m, k), (k2, n), dn = a.shape, b.shape, NN
    elif mode == "nt":
        (m, k), (n, k2), dn = a.shape, b.shape, NT
    else:
        (k, m), (k2, n), dn = a.shape, b.shape, TN
    assert k == k2, (a.shape, b.shape, mode)
    tn = _pick(n, 640)
    tm = _pick(m, 1024)
    osz = jnp.dtype(out_dtype).itemsize

    def need(tm_):
        blk = tm_ * k * 2 + tn * k * 2 + tm_ * tn * osz + (tm_ * tn * 4 if acc is not None else 0)
        return 2 * blk + tm_ * tn * 4
    while need(tm) > 36 * 1024 * 1024 and tm % 256 == 0:
        tm //= 2

    def body(*refs):
        a_ref, b_ref, o_ref = refs[0], refs[1], refs[-1]
        r = lax.dot_general(a_ref[...], b_ref[...], dn, preferred_element_type=F32)
        if acc is not None:
            r = r + refs[2][...]
        o_ref[...] = r.astype(o_ref.dtype)

    if mode == "tn":
        a_spec = pl.BlockSpec((k, tm), lambda i, j: (0, i))
    else:
        a_spec = pl.BlockSpec((tm, k), lambda i, j: (i, 0))
    if mode == "nt":
        b_spec = pl.BlockSpec((tn, k), lambda i, j: (j, 0))
    else:
        b_spec = pl.BlockSpec((k, tn), lambda i, j: (0, j))
    o_spec = pl.BlockSpec((tm, tn), lambda i, j: (i, j))
    in_specs = [a_spec, b_spec] + ([o_spec] if acc is not None else [])
    in_specs += [pl.BlockSpec(memory_space=pl.ANY)] if dep is not None else []
    args = (a, b) + ((acc,) if acc is not None else ()) + ((dep,) if dep is not None else ())
    return pl.pallas_call(
        body, name=name, grid=(m // tm, n // tn),
        in_specs=in_specs, out_specs=o_spec,
        out_shape=jax.ShapeDtypeStruct((m, n), out_dtype),
        compiler_params=_params(("parallel", "parallel")),
    )(*args)


def _rowwise(fn, row_ins, vec_ins, row_outs, sum_outs, name, tm=256):
    n_in = len(row_ins) + len(vec_ins)
    n_o = len(row_outs)
    rows = row_ins[0][0].shape[0]

    def body(*refs):
        vals = [r[...] for r in refs[:n_in]]
        outs = refs[n_in:]
        ro, so = fn(*vals)
        assert len(ro) == n_o and len(so) == len(sum_outs)
        for r, v in zip(outs[:n_o], ro):
            r[...] = v.astype(r.dtype)
        if sum_outs:
            @pl.when(pl.program_id(0) == 0)
            def _():
                for r in outs[n_o:]:
                    r[...] = jnp.zeros(r.shape, F32)
            for r, v in zip(outs[n_o:], so):
                r[...] += v

    in_specs = [pl.BlockSpec((tm, w), functools.partial(lambda i, b: (i, b), b=b)) for _, w, b in row_ins]
    in_specs += [pl.BlockSpec(v.shape, lambda i: (0, 0)) for v in vec_ins]
    out_specs = [pl.BlockSpec((tm, w), lambda i: (i, 0)) for w, _ in row_outs]
    out_specs += [pl.BlockSpec((1, w), lambda i: (0, 0)) for w in sum_outs]
    out_shape = [jax.ShapeDtypeStruct((rows, w), dt) for w, dt in row_outs]
    out_shape += [jax.ShapeDtypeStruct((1, w), F32) for w in sum_outs]
    return pl.pallas_call(
        body, name=name, grid=(rows // tm,),
        in_specs=in_specs, out_specs=out_specs, out_shape=out_shape,
        compiler_params=_params(("arbitrary",)),
    )(*[a for a, _, _ in row_ins], *vec_ins)


def _sigmoid(x):
    return 1.0 / (1.0 + jnp.exp(-x))


def _rstd(x):
    return lax.rsqrt(jnp.mean(x * x, axis=-1, keepdims=True) + NORM_EPS)


def _norm_bwd(dyn, xn, r):
    return r * (dyn - xn * jnp.mean(dyn * xn, axis=-1, keepdims=True))


def _colsum(x):
    return jnp.sum(x, axis=0, keepdims=True)


def _rope_tables(pos, invf):
    def fn(p, f):
        lane = lax.broadcasted_iota(jnp.int32, (1, LANES), 1)
        ang = p * f
        cs, sn = jnp.cos(ang), jnp.sin(ang)
        rot = (lane >= 64) & (lane < 96)
        ct = jnp.where(lane < 64, 1.0, jnp.where(rot, cs, 0.0))
        sa = jnp.where((lane >= 64) & (lane < 80), -sn, 0.0)
        sb = jnp.where((lane >= 80) & (lane < 96), sn, 0.0)
        return (ct, sa, sb), ()
    return _rowwise(fn, [(pos, 1, 0)], [invf], [(LANES, F32)] * 3, [], "rope_tables")


def _rope(x, ct, sa, sb):
    return x * ct + pltpu.roll(x, LANES - 16, 1) * sa + pltpu.roll(x, 16, 1) * sb


def _rope_t(x, ct, sa, sb):
    return x * ct - pltpu.roll(x, LANES - 16, 1) * sa - pltpu.roll(x, 16, 1) * sb


def _head_mask(width, hh):
    lane = lax.broadcasted_iota(jnp.int32, (1, width), 1)
    half = width // 2
    return (lane >= hh * half) & (lane < (hh + 1) * half)


ATT_PP = 2
ATT_CHAINS = [(a, hh) for a in range(ATT_PP) for hh in range(2)]
ATT_G = HEADS // (2 * ATT_PP)


def _pair(ref_or_val, a, width, rows=slice(None)):
    return ref_or_val[rows, a * width:(a + 1) * width]


def _attn_fwd(q, qo, k, ko, v, vo, dkp, scale, bias, name):
    T = ATT_T
    assert qo % ATT_PP == 0 and ko % ATT_PP == 0 and vo % ATT_PP == 0
    qo, ko, vo = qo // ATT_PP, ko // ATT_PP, vo // ATT_PP

    def body(*refs):
        if bias is not None:
            q_ref, k_ref, v_ref, b_ref, o_ref, lse_ref, s_scr = refs
        else:
            q_ref, k_ref, v_ref, o_ref, lse_ref, s_scr = refs
        i = pl.program_id(1)
        row = lax.broadcasted_iota(jnp.int32, (T, T), 0)
        col = lax.broadcasted_iota(jnp.int32, (T, T), 1)
        qms = []
        for a, hh in ATT_CHAINS:
            qb = _pair(q_ref, a, dkp)
            qms.append(jnp.where(_head_mask(dkp, hh), qb, jnp.zeros_like(qb)))

        def fold(t):
            return [t[:, c * LANES:(c + 1) * LANES] for c in range(T // LANES)]

        def run(nt):
            mls = [jnp.full((T, LANES), -jnp.inf, F32) for _ in ATT_CHAINS]
            for j in range(nt):
                ks = slice(j * T, (j + 1) * T)
                for ci, (a, hh) in enumerate(ATT_CHAINS):
                    s = lax.dot_general(qms[ci], _pair(k_ref, a, dkp, ks), NT, preferred_element_type=F32) * scale
                    if bias is not None:
                        s = s + b_ref[2 * a + hh, j]
                    if j == nt - 1:
                        s = jnp.where(row >= col, s, -jnp.inf)
                    s_scr[ci, j] = s
                    for part in fold(s):
                        mls[ci] = jnp.maximum(mls[ci], part)
            ms = [jnp.max(ml, axis=1, keepdims=True) for ml in mls]
            mbs = [jnp.broadcast_to(m, (T, LANES)) for m in ms]
            for a in range(ATT_PP):
                ls = [jnp.zeros((T, LANES), F32) for _ in range(2)]
                ps, vms = [], []
                for j in range(nt):
                    vb = _pair(v_ref, a, LANES, slice(j * T, (j + 1) * T))
                    for hh in range(2):
                        parts = [jnp.exp(part - mbs[2 * a + hh]) for part in fold(s_scr[2 * a + hh, j])]
                        for part in parts:
                            ls[hh] = ls[hh] + part
                        ps.append(jnp.concatenate(parts, axis=1).astype(BF16))
                        vms.append(jnp.where(_head_mask(LANES, hh), vb, jnp.zeros_like(vb)))
                acc = lax.dot_general(jnp.concatenate(ps, axis=1), jnp.concatenate(vms, axis=0), NN,
                                      preferred_element_type=F32)
                l0, l1 = [jnp.sum(l, axis=1, keepdims=True) for l in ls]
                lse_ref[2 * a] = ms[2 * a] + jnp.log(l0)
                lse_ref[2 * a + 1] = ms[2 * a + 1] + jnp.log(l1)
                inv = jnp.where(_head_mask(LANES, 0), 1.0 / l0, 1.0 / l1)
                o_ref[:, a * LANES:(a + 1) * LANES] = (acc * inv).astype(o_ref.dtype)

        for nt in range(1, N_ATT + 1):
            pl.when(i == nt - 1)(functools.partial(run, nt))

    in_specs = [
        pl.BlockSpec((T, ATT_PP * dkp), lambda g, i: (i, qo + g)),
        pl.BlockSpec((S, ATT_PP * dkp), lambda g, i: (0, ko + g)),
        pl.BlockSpec((S, ATT_PP * LANES), lambda g, i: (0, vo + g)),
    ]
    args = [q, k, v]
    if bias is not None:
        in_specs.append(pl.BlockSpec((2 * ATT_PP, N_ATT, 1, T), lambda g, i: (g, 0, 0, 0)))
        args.append(bias)
    return pl.pallas_call(
        body, name=name, grid=(ATT_G, N_ATT),
        in_specs=in_specs,
        out_specs=[pl.BlockSpec((T, ATT_PP * LANES), lambda g, i: (i, g)),
                   pl.BlockSpec((2 * ATT_PP, T, 1), lambda g, i: (g, i, 0))],
        out_shape=[jax.ShapeDtypeStruct((S, HEADS * HEAD_DIM), BF16),
                   jax.ShapeDtypeStruct((HEADS, S, 1), F32)],
        scratch_shapes=[pltpu.VMEM((len(ATT_CHAINS), N_ATT, T, T), F32)],
        compiler_params=_params(("parallel", "arbitrary")),
    )(*args)


def _attn_grad(q, qo, k, ko, v, vo, do, lse, dkp, scale, bias, qk_dtype, name):
    T = ATT_T
    has_b = bias is not None
    qo, ko, vo = qo // ATT_PP, ko // ATT_PP, vo // ATT_PP
    n_ch = len(ATT_CHAINS)

    def body(*refs):
        q_ref, k_ref, v_ref, do_ref, lse_ref = refs[:5]
        refs = refs[5:]
        if has_b:
            b_ref, refs = refs[0], refs[1:]
        dq_ref, dk_ref, dv_ref = refs[:3]
        refs = refs[3:]
        if has_b:
            db_ref, refs = refs[0], refs[1:]
        p_scr, dp_scr, dk_acc, dv_acc = refs[:4]
        db_acc = refs[4] if has_b else None
        i = pl.program_id(1)

        @pl.when(i == 0)
        def _():
            dk_acc[...] = jnp.zeros(dk_acc.shape, F32)
            dv_acc[...] = jnp.zeros(dv_acc.shape, F32)
            if has_b:
                db_acc[...] = jnp.zeros(db_acc.shape, F32)

        row = lax.broadcasted_iota(jnp.int32, (T, T), 0)
        col = lax.broadcasted_iota(jnp.int32, (T, T), 1)

        def fold(t):
            return [t[:, c * LANES:(c + 1) * LANES] for c in range(T // LANES)]

        qms, doms, lses = [], [], []
        for a, hh in ATT_CHAINS:
            qb, dob = _pair(q_ref, a, dkp), _pair(do_ref, a, LANES)
            qms.append(jnp.where(_head_mask(dkp, hh), qb, jnp.zeros_like(qb)))
            doms.append(jnp.where(_head_mask(LANES, hh), dob, jnp.zeros_like(dob)))
            lses.append(lse_ref[2 * a + hh])

        def run(nt):
            dls = [jnp.zeros((T, LANES), F32) for _ in ATT_CHAINS]
            for j in range(nt):
                ks = slice(j * T, (j + 1) * T)
                for ci, (a, hh) in enumerate(ATT_CHAINS):
                    s = lax.dot_general(qms[ci], _pair(k_ref, a, dkp, ks), NT, preferred_element_type=F32) * scale
                    if has_b:
                        s = s + b_ref[ci, j]
                    s = s - lses[ci]
                    if j == nt - 1:
                        s = jnp.where(row >= col, s, -jnp.inf)
                    p = jnp.exp(s)
                    dp = lax.dot_general(doms[ci], _pair(v_ref, a, LANES, ks), NT, preferred_element_type=F32)
                    p_scr[ci, j] = p
                    dp_scr[ci, j] = dp
                    for part in fold(p * dp):
                        dls[ci] = dls[ci] + part
            deltas = [jnp.broadcast_to(jnp.sum(dl, axis=1, keepdims=True), (T, LANES)) for dl in dls]
            for a in range(ATT_PP):
                ds_all, km_all = [], []
                qm2t = jnp.transpose(jnp.concatenate([qms[2 * a], qms[2 * a + 1]], axis=0))
                dom2t = jnp.transpose(jnp.concatenate([doms[2 * a], doms[2 * a + 1]], axis=0))
                for j in range(nt):
                    ks = slice(j * T, (j + 1) * T)
                    kb = _pair(k_ref, a, dkp, ks)
                    p2, ds2 = [], []
                    for hh in range(2):
                        ci = 2 * a + hh
                        p = p_scr[ci, j]
                        ds = jnp.concatenate([pp * (dd - deltas[ci]) for pp, dd in zip(fold(p), fold(dp_scr[ci, j]))], axis=1)
                        if has_b:
                            db_acc[ci, j] += jnp.sum(ds, axis=0, keepdims=True)
                        p2.append(p.astype(BF16))
                        ds2.append((ds * scale).astype(BF16))
                        km_all.append(jnp.where(_head_mask(dkp, hh), kb, jnp.zeros_like(kb)))
                    dv_acc[a * LANES:(a + 1) * LANES, ks] += lax.dot_general(
                        dom2t, jnp.concatenate(p2, axis=0), NN, preferred_element_type=F32)
                    dk_acc[a * dkp:(a + 1) * dkp, ks] += lax.dot_general(
                        qm2t, jnp.concatenate(ds2, axis=0), NN, preferred_element_type=F32)
                    ds_all += ds2
                dq = lax.dot_general(jnp.concatenate(ds_all, axis=1), jnp.concatenate(km_all, axis=0), NN,
                                     preferred_element_type=F32)
                dq_ref[:, a * dkp:(a + 1) * dkp] = dq.astype(dq_ref.dtype)

        for nt in range(1, N_ATT + 1):
            pl.when(i == nt - 1)(functools.partial(run, nt))

        @pl.when(i == N_ATT - 1)
        def _():
            dk_ref[...] = jnp.transpose(dk_acc[...]).astype(dk_ref.dtype)
            dv_ref[...] = jnp.transpose(dv_acc[...]).astype(dv_ref.dtype)
            if has_b:
                db_ref[...] = db_acc[...]

    in_specs = [
        pl.BlockSpec((T, ATT_PP * dkp), lambda g, i: (i, qo + g)),
        pl.BlockSpec((S, ATT_PP * dkp), lambda g, i: (0, ko + g)),
        pl.BlockSpec((S, ATT_PP * LANES), lambda g, i: (0, vo + g)),
        pl.BlockSpec((T, ATT_PP * LANES), lambda g, i: (i, g)),
        pl.BlockSpec((2 * ATT_PP, T, 1), lambda g, i: (g, i, 0)),
    ]
    args = [q, k, v, do, lse]
    out_specs = [
        pl.BlockSpec((T, ATT_PP * dkp), lambda g, i: (i, g)),
        pl.BlockSpec((S, ATT_PP * dkp), lambda g, i: (0, g)),
        pl.BlockSpec((S, ATT_PP * LANES), lambda g, i: (0, g)),
    ]
    width = (HEADS // 2) * dkp
    out_shape = [
        jax.ShapeDtypeStruct((S, width), qk_dtype),
        jax.ShapeDtypeStruct((S, width), qk_dtype),
        jax.ShapeDtypeStruct((S, HEADS * HEAD_DIM), BF16),
    ]
    scratch = [pltpu.VMEM((n_ch, N_ATT, T, T), F32), pltpu.VMEM((n_ch, N_ATT, T, T), F32),
               pltpu.VMEM((ATT_PP * dkp, S), F32), pltpu.VMEM((ATT_PP * LANES, S), F32)]
    if has_b:
        bspec = pl.BlockSpec((2 * ATT_PP, N_ATT, 1, T), lambda g, i: (g, 0, 0, 0))
        in_specs.append(bspec)
        args.append(bias)
        out_specs.append(bspec)
        out_shape.append(jax.ShapeDtypeStruct((HEADS, N_ATT, 1, T), F32))
        scratch.append(pltpu.VMEM((2 * ATT_PP, N_ATT, 1, T), F32))
    return pl.pallas_call(
        body, name=name, grid=(ATT_G, N_ATT),
        in_specs=in_specs, out_specs=out_specs, out_shape=out_shape, scratch_shapes=scratch,
        compiler_params=_params(("parallel", "arbitrary")),
    )(*args)


def _tri(upper):
    a = lax.broadcasted_iota(jnp.int32, (LANES, LANES), 0)
    b = lax.broadcasted_iota(jnp.int32, (LANES, LANES), 1)
    return jnp.where(a <= b if upper else a >= b, 1.0, 0.0).astype(F32)


def _fox_gates(zt, bf):
    def body(z_ref, b_ref, o_ref):
        tri = _tri(True)
        carry = jnp.zeros((HEADS, 1), F32)
        for t in range(S // LANES):
            sl = slice(t * LANES, (t + 1) * LANES)
            z = z_ref[:, sl] + b_ref[...]
            logf = jnp.minimum(z, 0.0) - jnp.log(1.0 + jnp.exp(-jnp.abs(z)))
            c = lax.dot_general(logf, tri, NN, preferred_element_type=F32,
                                precision=lax.Precision.HIGHEST) + carry
            o_ref[:, sl] = -c
            carry = c[:, LANES - 1:LANES]

    return pl.pallas_call(
        body, name="fox_gates", out_shape=jax.ShapeDtypeStruct((HEADS, S), F32),
        compiler_params=_params(),
    )(zt, bf)


def _fox_gates_bwd(dbias, zt, bf):
    def body(d_ref, z_ref, b_ref, dz_ref, dbf_ref):
        tri = _tri(False)
        carry = jnp.zeros((HEADS, 1), F32)
        tot = jnp.zeros((HEADS, 1), F32)
        for t in reversed(range(S // LANES)):
            sl = slice(t * LANES, (t + 1) * LANES)
            df = -d_ref[:, sl]
            c = lax.dot_general(df, tri, NN, preferred_element_type=F32,
                                precision=lax.Precision.HIGHEST) + carry
            carry = c[:, 0:1]
            z = z_ref[:, sl] + b_ref[...]
            dz = c * _sigmoid(-z)
            dz_ref[:, sl] = dz
            tot = tot + jnp.sum(dz, axis=1, keepdims=True)
        dbf_ref[...] = tot

    return pl.pallas_call(
        body, name="fox_gates_bwd",
        out_shape=[jax.ShapeDtypeStruct((HEADS, S), F32), jax.ShapeDtypeStruct((HEADS, 1), F32)],
        compiler_params=_params(),
    )(dbias, zt, bf)


def _mod_part(c_all, w_ada, b_cols):
    def body(c_ref, w_ref, b_ref, o_ref, s_ref):
        c = c_ref[...]
        sc = c * _sigmoid(c)
        s_ref[...] = sc
        o_ref[...] = lax.dot_general(sc, w_ref[...], NN, preferred_element_type=F32,
                                     precision=lax.Precision.HIGHEST) + b_ref[...]

    return pl.pallas_call(
        body, name="mod_part",
        out_shape=[jax.ShapeDtypeStruct((N_DEV, w_ada.shape[1]), F32), jax.ShapeDtypeStruct(c_all.shape, F32)],
        compiler_params=_params(),
    )(c_all, w_ada, b_cols)


def _w_ada_grad(sc_t, dm):
    def body(s_ref, d_ref, o_ref):
        acc = jnp.zeros(o_ref.shape, F32)
        for b in range(N_DEV):
            acc = acc + s_ref[:, b:b + 1] * d_ref[b:b + 1, :]
        o_ref[...] = acc

    return pl.pallas_call(
        body, name="w_ada_grad", out_shape=jax.ShapeDtypeStruct((sc_t.shape[0], dm.shape[1]), F32),
        compiler_params=_params(),
    )(sc_t, dm)


def _adamw(w, m, v, parts, name):
    rows, cols = w.shape
    n = parts.shape[0]
    tr = rows if rows <= 512 else 256

    def body(w_ref, m_ref, v_ref, p_ref, g_out, d_out, m_out, v_out):
        g = p_ref[0].astype(F32)
        for kk in range(1, n):
            g = g + p_ref[kk].astype(F32)
        g_out[...] = g
        d_out[...], m_out[...], v_out[...] = _adamw_math(w_ref[...], g, m_ref[...], v_ref[...])

    spec = pl.BlockSpec((tr, cols), lambda i: (i, 0))
    return pl.pallas_call(
        body, name=name, grid=(rows // tr,),
        in_specs=[spec, spec, spec, pl.BlockSpec((n, tr, cols), lambda i: (0, i, 0))],
        out_specs=[spec] * 4, out_shape=[jax.ShapeDtypeStruct((rows, cols), F32)] * 4,
        compiler_params=_params(("parallel",)),
    )(w, m, v, parts)


def _adamw_math(w, g, m, v):
    mm = ADAM_B1 * m + (1.0 - ADAM_B1) * g
    vv = ADAM_B2 * v + (1.0 - ADAM_B2) * (g * g)
    m_hat = mm / (1.0 - ADAM_B1 ** ADAM_STEP)
    v_hat = vv / (1.0 - ADAM_B2 ** ADAM_STEP)
    return -ADAM_LR * (m_hat / (jnp.sqrt(v_hat) + ADAM_EPS) + ADAM_WD * w), mm, vv


def _adamw_rows(bundles, offsets, ws, ms, vs, err_off, err_width):
    k = len(ws)

    def body(*refs):
        b_ref = refs[0]
        w_refs, m_refs, v_refs = refs[1:1 + k], refs[1 + k:1 + 2 * k], refs[1 + 2 * k:1 + 3 * k]
        outs = refs[1 + 3 * k:]
        g_all = b_ref[0]
        for kk in range(1, N_DEV):
            g_all = g_all + b_ref[kk]
        for i in range(k):
            width = w_refs[i].shape[1]
            g = g_all[:, offsets[i]:offsets[i] + width]
            outs[4 * i][...] = g
            outs[4 * i + 1][...], outs[4 * i + 2][...], outs[4 * i + 3][...] = _adamw_math(
                w_refs[i][...], g, m_refs[i][...], v_refs[i][...])
        outs[4 * k][...] = g_all[:, err_off:err_off + err_width]

    out_shape = []
    for w_ in ws:
        out_shape += [jax.ShapeDtypeStruct(w_.shape, F32)] * 4
    out_shape.append(jax.ShapeDtypeStruct((1, err_width), F32))
    res = pl.pallas_call(body, name="adamw_rows", out_shape=out_shape, compiler_params=_params())(bundles, *ws, *ms, *vs)
    return [tuple(res[4 * i:4 * i + 4]) for i in range(k)], res[-1]


def _coords():
    return lax.axis_index("x"), lax.axis_index("y"), lax.axis_index("c")


def _flat(px, py, pc):
    return 4 * px + 2 * py + pc


def _all_gather(arrs, name):
    n = len(arrs)

    def body(*refs):
        ins, outs = refs[:n], refs[n:2 * n]
        send, recv, lsem = refs[2 * n:]
        x, y, c = _coords()
        me, sibling = (x, y, c), (x, y, 1 - c)
        chips = [(1 - x, y), (x, 1 - y), (1 - x, 1 - y)]

        def copy(a, kk, block, to, src=None):
            slot = outs[a].at[_flat(*block)]
            return pltpu.make_async_remote_copy(
                src_ref=slot if src is None else src, dst_ref=slot,
                send_sem=send.at[a, kk], recv_sem=recv.at[a, kk],
                device_id=to, device_id_type=MESH)

        mine = [pltpu.make_async_copy(ins[a], outs[a].at[_flat(*me)], lsem.at[a]) for a in range(n)]
        for cp in mine:
            cp.start()
        first = []
        for a in range(n):
            first.append(copy(a, 0, me, sibling, src=ins[a]))
            first += [copy(a, 1 + j, me, (*chip, c), src=ins[a]) for j, chip in enumerate(chips)]
        for cp in first:
            cp.start()
        passed = []
        for j, chip in enumerate(chips):
            for a in range(n):
                copy(a, 1 + j, (*chip, c), me).wait_recv()
                cp = copy(a, 4 + j, (*chip, c), sibling)
                cp.start()
                passed.append(cp)
        for a in range(n):
            copy(a, 0, sibling, me).wait_recv()
        for j, chip in enumerate(chips):
            for a in range(n):
                copy(a, 4 + j, (*chip, 1 - c), me).wait_recv()
        for cp in first + passed:
            cp.wait_send()
        for cp in mine:
            cp.wait()

    any_spec = pl.BlockSpec(memory_space=pl.ANY)
    return pl.pallas_call(
        body, name=name,
        in_specs=[any_spec] * n, out_specs=[any_spec] * n,
        out_shape=[jax.ShapeDtypeStruct((N_DEV,) + a.shape, a.dtype) for a in arrs],
        scratch_shapes=[pltpu.SemaphoreType.DMA((n, 7)), pltpu.SemaphoreType.DMA((n, 7)),
                        pltpu.SemaphoreType.DMA((n,))],
    )(*arrs)


def _peer_list():
    x, y, c = _coords()
    return [((1 - x if r & 4 else x), (1 - y if r & 2 else y), (1 - c if r & 1 else c)) for r in range(1, N_DEV)]


def _copy_for(mode, src, land, send, recv, peer, me):
    src_ref = src if mode == "gather" else src.at[_flat(*peer)]
    return pltpu.make_async_remote_copy(src_ref=src_ref, dst_ref=land.at[me], send_sem=send, recv_sem=recv,
                                        device_id=peer, device_id_type=MESH)


HBM_SPEC = pl.BlockSpec(memory_space=pltpu.HBM)
SEM_SPEC = pl.BlockSpec(memory_space=pltpu.SEMAPHORE)
ANY_SPEC = pl.BlockSpec(memory_space=pl.ANY)
SIDE_EFFECT = pltpu.SideEffectType.DATAFLOW_SIDE_EFFECTING


def _async_start(groups, mode, after, name):
    flat_arrs = [a for g in groups for a in g]
    n = len(flat_arrs)

    def body(*refs):
        srcs, lands = refs[:n], refs[n:2 * n]
        outs = refs[2 * n + 1:]
        token = outs[-1]
        me = _flat(*_coords())
        for ai in range(n):
            for peer in _peer_list():
                _copy_for(mode, srcs[ai], lands[ai], outs[2 * ai], outs[2 * ai + 1], peer, me).start()
        token[...] = jnp.zeros(token.shape, F32)

    land_shapes = [((N_DEV,) + a.shape if mode == "gather" else a.shape) for a in flat_arrs]
    out_shape = [pltpu.SemaphoreType.DMA(())] * (2 * n)
    out_shape += [pltpu.HBM(a.shape, a.dtype) for a in flat_arrs]
    out_shape += [pltpu.HBM(s, a.dtype) for s, a in zip(land_shapes, flat_arrs)]
    out_shape.append(jax.ShapeDtypeStruct((8, LANES), F32))
    res = pl.pallas_call(
        body, name=name, out_shape=tuple(out_shape),
        in_specs=[HBM_SPEC] * (2 * n) + [ANY_SPEC],
        out_specs=tuple([SEM_SPEC] * (2 * n) + [HBM_SPEC] * (2 * n) + [pl.BlockSpec(memory_space=pltpu.VMEM)]),
        input_output_aliases={i: 2 * n + i for i in range(2 * n)},
        compiler_params=pltpu.CompilerParams(has_side_effects=SIDE_EFFECT),
    )(*[pltpu.with_memory_space_constraint(a, pltpu.HBM) for a in flat_arrs],
      *[pltpu.with_memory_space_constraint(lax.empty(s, a.dtype), pltpu.HBM) for s, a in zip(land_shapes, flat_arrs)],
      after)
    sems, thru = res[:2 * n], res[2 * n:-1]
    states, idx = [], 0
    for g in groups:
        k = len(g)
        states.append((list(sems[2 * idx:2 * (idx + k):2]), list(sems[2 * idx + 1:2 * (idx + k):2]),
                       list(thru[idx:idx + k]), list(thru[n + idx:n + idx + k])))
        idx += k
    return states, res[-1]


def _async_wait(state, after, name):
    sends, recvs, srcs, lands = state
    g = len(srcs)

    def body(*refs):
        l_refs, sems = refs[g:2 * g], refs[2 * g:4 * g]
        for ai in range(g):
            seven = l_refs[ai].at[pl.ds(0, N_DEV - 1)]
            cp = pltpu.make_async_remote_copy(src_ref=seven, dst_ref=seven, send_sem=sems[ai], recv_sem=sems[g + ai],
                                              device_id=_coords(), device_id_type=MESH)
            cp.wait_send()
            cp.wait_recv()

    res = pl.pallas_call(
        body, name=name,
        out_shape=tuple([pltpu.HBM(a.shape, a.dtype) for a in srcs] + [pltpu.HBM(a.shape, a.dtype) for a in lands]),
        in_specs=[HBM_SPEC] * (2 * g) + [SEM_SPEC] * (2 * g) + [ANY_SPEC],
        out_specs=tuple([HBM_SPEC] * (2 * g)),
        input_output_aliases={i: i for i in range(2 * g)},
        compiler_params=pltpu.CompilerParams(has_side_effects=SIDE_EFFECT),
    )(*srcs, *lands, *sends, *recvs, after)
    return list(res[:g]), list(res[g:])


def _with_own(land, own, me):
    return lax.dynamic_update_index_in_dim(land, own, me, 0)


IN_SPLITS = (512, 512, 512, 8, 768, 256, 32, 1024, 1024)


def _unshard_cols(g):
    _, k, n = g.shape
    tr = min(k, 256)

    def body(g_ref, o_ref):
        o_ref[...] = jnp.concatenate([g_ref[j] for j in range(N_DEV)], axis=1)

    return pl.pallas_call(
        body, name="unshard_cols_%d" % n, grid=(k // tr,),
        in_specs=[pl.BlockSpec((N_DEV, tr, n), lambda i: (0, i, 0))],
        out_specs=pl.BlockSpec((tr, N_DEV * n), lambda i: (i, 0)),
        out_shape=jax.ShapeDtypeStruct((k, N_DEV * n), g.dtype),
        compiler_params=_params(("parallel",)),
    )(g)


def _shard_cols(w):
    k, n = w.shape[0], w.shape[1] // N_DEV
    tr = min(k, 256)

    def body(w_ref, o_ref):
        full = w_ref[...]
        for j in range(N_DEV):
            o_ref[j] = full[:, j * n:(j + 1) * n]

    return pl.pallas_call(
        body, name="shard_cols_%d" % n, grid=(k // tr,),
        in_specs=[pl.BlockSpec((tr, N_DEV * n), lambda i: (i, 0))],
        out_specs=pl.BlockSpec((N_DEV, tr, n), lambda i: (0, i, 0)),
        out_shape=jax.ShapeDtypeStruct((N_DEV, k, n), w.dtype),
        compiler_params=_params(("parallel",)),
    )(w)


IN_OFFS = tuple(sum(IN_SPLITS[:i]) for i in range(len(IN_SPLITS) + 1))
IN_SHARD = IN_OFFS[-1] // N_DEV
REGROUP_ROWS = 128


def _w_in_regroup(g):
    def body(g_ref, a_ref, b_ref):
        full = jnp.concatenate([g_ref[j] for j in range(N_DEV)], axis=1)
        fq, fk, fv, wf, cq, ckv, kr, gf, gm = [full[:, IN_OFFS[i]:IN_OFFS[i + 1]] for i in range(9)]
        rows = full.shape[0]
        a_ref[...] = jnp.concatenate([cq, ckv, gf, gm, wf, jnp.zeros((rows, 56), BF16), kr, jnp.zeros((rows, 32), BF16)], axis=1)
        b_ref[...] = jnp.concatenate([fq, fk, fv], axis=1)

    tr = REGROUP_ROWS
    return pl.pallas_call(
        body, name="w_in_regroup", grid=(D // tr,),
        in_specs=[pl.BlockSpec((N_DEV, tr, IN_SHARD), lambda i: (0, i, 0))],
        out_specs=[pl.BlockSpec((tr, 3200), lambda i: (i, 0)), pl.BlockSpec((tr, 1536), lambda i: (i, 0))],
        out_shape=[jax.ShapeDtypeStruct((D, 3200), BF16), jax.ShapeDtypeStruct((D, 1536), BF16)],
        compiler_params=_params(("parallel",)),
    )(g)


def _w_in_ungroup(da, db_):
    def body(a_ref, b_ref, o_ref):
        a = a_ref[...]
        full = jnp.concatenate([b_ref[...], a[:, 3072:3080], a[:, 0:768], a[:, 768:1024], a[:, 3136:3168],
                                a[:, 1024:3072]], axis=1)
        for j in range(N_DEV):
            o_ref[j] = full[:, j * IN_SHARD:(j + 1) * IN_SHARD]

    tr = REGROUP_ROWS
    return pl.pallas_call(
        body, name="w_in_ungroup", grid=(D // tr,),
        in_specs=[pl.BlockSpec((tr, 3200), lambda i: (i, 0)), pl.BlockSpec((tr, 1536), lambda i: (i, 0))],
        out_specs=pl.BlockSpec((N_DEV, tr, IN_SHARD), lambda i: (0, i, 0)),
        out_shape=jax.ShapeDtypeStruct((N_DEV, D, IN_SHARD), BF16),
        compiler_params=_params(("parallel",)),
    )(da, db_)


def _prepare_weights(g):
    w = {}
    if "w_in" in g:
        w["w_a"], w["w_b"] = _w_in_regroup(g["w_in"])
    if "w_uq" in g:
        w_uq = g["w_uq"].reshape(Q_LORA, HEADS, 96)
        w["w_uq"] = jnp.pad(w_uq, ((0, 0), (0, 0), (0, 32))).reshape(Q_LORA, HEADS * LANES)
        ukv = g["w_ukv"]
        w["w_k"] = jnp.transpose(jnp.pad(ukv[:, :, :64], ((0, 0), (0, 0), (0, 64))), (1, 0, 2)).reshape(KV_LORA, HEADS * LANES)
        w["w_v"] = jnp.transpose(ukv[:, :, 64:], (1, 0, 2)).reshape(KV_LORA, HEADS * HEAD_DIM)
        w["w_kv"] = jnp.concatenate([w["w_k"], w["w_v"]], axis=1)
    if "w_out" in g:
        w["w_pf"] = _unshard_cols(g["w_proj_fox"])
        w["w_pm"] = _unshard_cols(g["w_proj_mla"])
        w["w_out"] = g["w_out"].reshape(D, D)
    if "w_ffn_in" in g:
        w["w_ffn_in"] = _unshard_cols(g["w_ffn_in"])
        w["w_ffn_out"] = g["w_ffn_out"].reshape(D_FF, D)
    return w


def _shard_grads(dw):
    out = {}
    if "w_a" in dw:
        out["w_in"] = _w_in_ungroup(dw["w_a"], dw["w_b"])
    if "w_uq" in dw:
        w_uq = dw["w_uq"].reshape(Q_LORA, HEADS, LANES)[:, :, :96].reshape(Q_LORA, Q_LORA)
        out["w_uq"] = w_uq.reshape(N_DEV, Q_LORA // N_DEV, Q_LORA)
        k_part = dw["w_k"].reshape(KV_LORA, HEADS, LANES)[:, :, :64]
        v_part = dw["w_v"].reshape(KV_LORA, HEADS, HEAD_DIM)
        out["w_ukv"] = jnp.transpose(jnp.concatenate([k_part, v_part], axis=2), (1, 0, 2))
    if "w_out" in dw:
        out["w_proj_fox"] = _shard_cols(dw["w_pf"])
        out["w_proj_mla"] = _shard_cols(dw["w_pm"])
        out["w_out"] = dw["w_out"].reshape(N_DEV, D // N_DEV, D)
    if "w_ffn_in" in dw:
        out["w_ffn_in"] = _shard_cols(dw["w_ffn_in"])
        out["w_ffn_out"] = dw["w_ffn_out"].reshape(N_DEV, D_FF // N_DEV, D)
    return out


def _fwd_bwd(x, pos, mod, target, w, vec, wts, send):
    shift_mix, scale_mix, gate_mix, shift_ffn, scale_ffn, gate_ffn = [mod[:, i * D:(i + 1) * D] for i in range(6)]
    g_pre_mix, g_post_mix, g_pre_ffn, g_post_ffn = vec["g_pre_mix"], vec["g_post_mix"], vec["g_pre_ffn"], vec["g_post_ffn"]
    g_q, g_kv = vec["g_q_lora"], vec["g_kv_lora"]

    inv_freq = 1.0 / (ROPE_THETA ** (jnp.arange(0, ROPE_DIM, 2, dtype=F32) / ROPE_DIM))
    invf = jnp.concatenate([jnp.zeros((64,), F32), inv_freq, inv_freq, jnp.zeros((32,), F32)]).reshape(1, LANES)
    ct, sa, sb = _rope_tables(pos, invf)

    def pre1(xv, g, sc, sh):
        return ((xv * _rstd(xv) * g) * (1.0 + sc) + sh,), ()
    (h,) = _rowwise(pre1, [(x, D, 0)], [g_pre_mix, scale_mix, shift_mix], [(D, BF16)], [], "pre_mix")
    proj_a = _mm(h, w["w_a"], "nn", F32, "in_proj_a")
    qkv = _mm(h, w["w_b"], "nn", BF16, "in_proj_b")

    def lora_norm(cq, ckv, gq, gkv):
        return (cq * _rstd(cq) * gq, ckv * _rstd(ckv) * gkv), ()
    cqn, ckvn = _rowwise(lora_norm, [(proj_a, Q_LORA, 0), (proj_a, KV_LORA, 3)], [g_q, g_kv],
                         [(Q_LORA, BF16), (KV_LORA, BF16)], [], "lora_norm")
    w = {**w, **wts("lora", cqn)}
    qb = _mm(cqn, w["w_uq"], "nn", F32, "mla_uq")
    kvb = _mm(ckvn, w["w_kv"], "nn", F32, "mla_ukv")

    def mla_rope(qv, kv, vv, misc, c_, a_, b_):
        lane = lax.broadcasted_iota(jnp.int32, (1, LANES), 1)
        kpe = jnp.where((lane >= 64) & (lane < 96), _rope(misc, c_, a_, b_), 0.0)
        qs = [_rope(qv[:, hd * LANES:(hd + 1) * LANES], c_, a_, b_) for hd in range(HEADS)]
        ks = [kv[:, hd * LANES:(hd + 1) * LANES] + kpe for hd in range(HEADS)]
        return (jnp.concatenate(qs, axis=1), jnp.concatenate(ks, axis=1), vv), ()
    q_m, k_m, v_m = _rowwise(
        mla_rope, [(qb, D, 0), (kvb, D, 0), (kvb, 512, 2), (proj_a, LANES, 24), (ct, LANES, 0), (sa, LANES, 0), (sb, LANES, 0)],
        [], [(D, BF16), (D, BF16), (512, BF16)], [], "mla_rope")

    zt = jnp.transpose(proj_a[:, 3072:3080])
    bf = jnp.transpose(vec["b_forget"])
    neg_f = _fox_gates(zt, bf)
    bias = neg_f.reshape(HEADS, N_ATT, 1, ATT_T)
    o_a, lse_a = _attn_fwd(qkv, 0, qkv, 4, qkv, 8, LANES, 1.0 / math.sqrt(HEAD_DIM), bias, "fox_attn")
    o_b, lse_b = _attn_fwd(q_m, 0, k_m, 0, v_m, 0, 2 * LANES, 1.0 / math.sqrt(64 + ROPE_DIM), None, "mla_attn")

    w = {**w, **wts("proj", o_b)}
    pa = _mm(o_a, w["w_pf"], "nn", F32, "proj_fox")
    pb = _mm(o_b, w["w_pm"], "nn", F32, "proj_mla")

    def merge(gf, gm, pa_, pb_):
        return (_sigmoid(gf) * pa_ + _sigmoid(gm) * pb_,), ()
    (merged,) = _rowwise(merge, [(proj_a, D, 1), (proj_a, D, 2), (pa, D, 0), (pb, D, 0)], [], [(D, BF16)], [], "merge")
    y = _mm(merged, w["w_out"], "nn", F32, "out_proj")

    def post1(xv, yv, gate, gpost, gpre, sc, sh):
        x1 = xv + gate * (yv * _rstd(yv) * gpost)
        return (x1, (x1 * _rstd(x1) * gpre) * (1.0 + sc) + sh), ()
    x1, h2 = _rowwise(post1, [(x, D, 0), (y, D, 0)], [gate_mix, g_post_mix, g_pre_ffn, scale_ffn, shift_ffn],
                      [(D, F32), (D, BF16)], [], "post_mix")
    w = {**w, **wts("ffn", h2)}
    gu = _mm(h2, w["w_ffn_in"], "nn", F32, "ffn_in")

    def swiglu(g, u):
        return (g * _sigmoid(g) * u,), ()
    (act,) = _rowwise(swiglu, [(gu, D_FF, 0), (gu, D_FF, 1)], [], [(D_FF, BF16)], [], "swiglu")
    y2 = _mm(act, w["w_ffn_out"], "nn", F32, "ffn_out")

    def head(x1v, y2v, tv, gate, gpost):
        r = _rstd(y2v)
        yn = y2v * r
        n2 = yn * gpost
        err = (x1v + gate * n2) - tv
        dx2 = err * (1.0 / D)
        dn2 = dx2 * gate
        dy2 = _norm_bwd(dn2 * gpost, yn, r)
        return (dx2, dy2), (_colsum(err * err), _colsum(dx2 * n2), _colsum(dn2 * yn))
    dx2, dy2, err_cols, d_gate_ffn, d_g_post_ffn = _rowwise(
        head, [(x1, D, 0), (y2, D, 0), (target, D, 0)], [gate_ffn, g_post_ffn], [(D, F32), (D, BF16)], [D, D, D], "loss_head")

    dact = _mm(dy2, w["w_ffn_out"], "nt", F32, "ffn_out_dx")
    dw = {"w_ffn_out": _mm(act, dy2, "tn", BF16, "ffn_out_dw")}

    def swiglu_bwd(g, u, da):
        sg = _sigmoid(g)
        return (jnp.concatenate([da * u * (sg * (1.0 + g * (1.0 - sg))), da * (g * sg)], axis=1),), ()
    (dgu,) = _rowwise(swiglu_bwd, [(gu, D_FF, 0), (gu, D_FF, 1), (dact, D_FF, 0)], [], [(2 * D_FF, BF16)], [], "swiglu_bwd")
    dh2 = _mm(dgu, w["w_ffn_in"], "nt", F32, "ffn_in_dx")
    dw["w_ffn_in"] = _mm(h2, dgu, "tn", BF16, "ffn_in_dw")
    gate_mix = gate_mix + send({n: dw.pop(n) for n in ("w_ffn_in", "w_ffn_out")})[0, 0]

    def mid(dh, x1v, dx2v, yv, gpre, sc, gate, gpost):
        r2 = _rstd(x1v)
        x1n = x1v * r2
        t = dh * x1n
        dx1 = dx2v + _norm_bwd(dh * (gpre * (1.0 + sc)), x1n, r2)
        ry = _rstd(yv)
        yn = yv * ry
        dn1 = dx1 * gate
        dy = _norm_bwd(dn1 * gpost, yn, ry)
        sums = (_colsum(dh), _colsum(t) * gpre, _colsum(t) * (1.0 + sc), _colsum(dx1 * (yn * gpost)), _colsum(dn1 * yn))
        return (dx1, dy), sums
    dx1, dy, d_shift_ffn, d_scale_ffn, d_g_pre_ffn, d_gate_mix, d_g_post_mix = _rowwise(
        mid, [(dh2, D, 0), (x1, D, 0), (dx2, D, 0), (y, D, 0)], [g_pre_ffn, scale_ffn, gate_mix, g_post_mix],
        [(D, F32), (D, BF16)], [D] * 5, "mid_bwd")

    dmerged = _mm(dy, w["w_out"], "nt", F32, "out_proj_dx")
    dw["w_out"] = _mm(merged, dy, "tn", BF16, "out_proj_dw")

    def merge_bwd(dm, gf, gm, pa_, pb_):
        sf, sm = _sigmoid(gf), _sigmoid(gm)
        dgates = jnp.concatenate([dm * pa_ * (sf * (1.0 - sf)), dm * pb_ * (sm * (1.0 - sm))], axis=1)
        return (dm * sf, dm * sm, dgates), ()
    dpa, dpb, dgates = _rowwise(
        merge_bwd, [(dmerged, D, 0), (proj_a, D, 1), (proj_a, D, 2), (pa, D, 0), (pb, D, 0)], [],
        [(D, BF16), (D, BF16), (2 * D, BF16)], [], "merge_bwd")
    do_a = _mm(dpa, w["w_pf"], "nt", BF16, "proj_fox_dx")
    do_b = _mm(dpb, w["w_pm"], "nt", BF16, "proj_mla_dx")
    dw["w_pf"] = _mm(o_a, dpa, "tn", BF16, "proj_fox_dw")
    dw["w_pm"] = _mm(o_b, dpb, "tn", BF16, "proj_mla_dw")
    bias = bias + send({n: dw.pop(n) for n in ("w_out", "w_pf", "w_pm")})[0, 0]

    sc_a, sc_b = 1.0 / math.sqrt(HEAD_DIM), 1.0 / math.sqrt(64 + ROPE_DIM)
    dq_a, dk_a, dv_a, dbias = _attn_grad(qkv, 0, qkv, 4, qkv, 8, do_a, lse_a, LANES, sc_a, bias, BF16, "fox_attn_bwd")
    dq_m, dk_m, dv_m = _attn_grad(q_m, 0, k_m, 0, v_m, 0, do_b, lse_b, 2 * LANES, sc_b, None, F32, "mla_attn_bwd")

    def mla_rope_bwd(dq, dk, c_, a_, b_):
        lane = lax.broadcasted_iota(jnp.int32, (1, LANES), 1)
        dqs = [_rope_t(dq[:, hd * LANES:(hd + 1) * LANES], c_, a_, b_) for hd in range(HEADS)]
        dkpe = dk[:, 0:LANES]
        for hd in range(1, HEADS):
            dkpe = dkpe + dk[:, hd * LANES:(hd + 1) * LANES]
        dkpe = jnp.where((lane >= 64) & (lane < 96), dkpe, 0.0)
        dkr = jnp.where((lane >= 64) & (lane < 96), _rope_t(dkpe, c_, a_, b_), 0.0)
        return (jnp.concatenate(dqs, axis=1), dk, dkr), ()
    dqb, dkb, dkr = _rowwise(mla_rope_bwd, [(dq_m, D, 0), (dk_m, D, 0), (ct, LANES, 0), (sa, LANES, 0), (sb, LANES, 0)],
                             [], [(D, BF16), (D, BF16), (LANES, F32)], [], "mla_rope_bwd")
    dcqn = _mm(dqb, w["w_uq"], "nt", F32, "mla_uq_dx")
    dw["w_uq"] = _mm(cqn, dqb, "tn", BF16, "mla_uq_dw")
    dckvn = _mm(dv_m, w["w_v"], "nt", F32, "mla_uv_dx", acc=_mm(dkb, w["w_k"], "nt", F32, "mla_uk_dx"))
    dw["w_k"] = _mm(ckvn, dkb, "tn", BF16, "mla_uk_dw")
    dw["w_v"] = _mm(ckvn, dv_m, "tn", BF16, "mla_uv_dw")

    def lora_norm_bwd(cq, ckv, dq, dkv, gq, gkv):
        rq, rk = _rstd(cq), _rstd(ckv)
        cqh, ckh = cq * rq, ckv * rk
        return (_norm_bwd(dq * gq, cqh, rq), _norm_bwd(dkv * gkv, ckh, rk)), (_colsum(dq * cqh), _colsum(dkv * ckh))
    dcq, dckv, d_g_q, d_g_kv = _rowwise(
        lora_norm_bwd, [(proj_a, Q_LORA, 0), (proj_a, KV_LORA, 3), (dcqn, Q_LORA, 0), (dckvn, KV_LORA, 0)], [g_q, g_kv],
        [(Q_LORA, BF16), (KV_LORA, BF16)], [Q_LORA, KV_LORA], "lora_norm_bwd")

    dzt, d_bf = _fox_gates_bwd(dbias.reshape(HEADS, S), zt, bf)
    dmisc = (dkr + jnp.pad(jnp.transpose(dzt), ((0, 0), (0, LANES - HEADS)))).astype(BF16)
    dproj_a = jnp.concatenate([dcq, dckv, dgates, dmisc], axis=1)
    dqkv = jnp.concatenate([dq_a, dk_a, dv_a], axis=1)
    dw["w_a"] = _mm(h, dproj_a, "tn", BF16, "in_proj_a_dw")
    dw["w_b"] = _mm(h, dqkv, "tn", BF16, "in_proj_b_dw")
    tok = send(dw)
    dh = _mm(dqkv, w["w_b"], "nt", F32, "in_proj_b_dx", acc=_mm(dproj_a, w["w_a"], "nt", F32, "in_proj_a_dx", dep=tok))

    def first(dhv, xv, dx1v, gpre, sc):
        r = _rstd(xv)
        xn = xv * r
        t = dhv * xn
        dx = dx1v + _norm_bwd(dhv * (gpre * (1.0 + sc)), xn, r)
        return (dx,), (_colsum(dhv), _colsum(t) * gpre, _colsum(t) * (1.0 + sc))
    grad_x, d_shift_mix, d_scale_mix, d_g_pre_mix = _rowwise(
        first, [(dh, D, 0), (x, D, 0), (dx1, D, 0)], [g_pre_mix, scale_mix], [(D, F32)], [D] * 3, "pre_mix_bwd")

    dmod = jnp.concatenate([d_shift_mix, d_scale_mix, d_gate_mix, d_shift_ffn, d_scale_ffn, d_gate_ffn], axis=1)
    small = dict(dmod=dmod, g_pre_mix=d_g_pre_mix, g_post_mix=d_g_post_mix, g_pre_ffn=d_g_pre_ffn,
                 g_post_ffn=d_g_post_ffn, g_q_lora=d_g_q, g_kv_lora=d_g_kv,
                 b_forget=jnp.pad(jnp.transpose(d_bf), ((0, 0), (0, LANES - HEADS))), err=err_cols)
    return grad_x, small


SMALL_ORDER = ("dmod", "g_pre_mix", "g_post_mix", "g_pre_ffn", "g_post_ffn", "g_q_lora", "g_kv_lora", "b_forget", "err")
SMALL_PARAM = {"dmod": "b_ada"}
MATRICES = ("w_in", "w_uq", "w_ukv", "w_proj_fox", "w_proj_mla", "w_out", "w_ffn_in", "w_ffn_out")
WEIGHTS = ("w_ada", "b_ada", "g_pre_mix", "g_post_mix", "g_pre_ffn", "g_post_ffn", "w_in", "b_forget", "g_q_lora",
           "w_uq", "g_kv_lora", "w_ukv", "w_proj_fox", "w_proj_mla", "w_out", "w_ffn_in", "w_ffn_out")


def kernel(x, c, positions, w_ada, b_ada, g_pre_mix, g_post_mix, g_pre_ffn, g_post_ffn, w_in, b_forget, g_q_lora, w_uq, g_kv_lora, w_ukv, w_proj_fox, w_proj_mla, w_out, w_ffn_in, w_ffn_out, loss_target, m_w_ada, m_b_ada, m_g_pre_mix, m_g_post_mix, m_g_pre_ffn, m_g_post_ffn, m_w_in, m_b_forget, m_g_q_lora, m_w_uq, m_g_kv_lora, m_w_ukv, m_w_proj_fox, m_w_proj_mla, m_w_out, m_w_ffn_in, m_w_ffn_out, v_w_ada, v_b_ada, v_g_pre_mix, v_g_post_mix, v_g_pre_ffn, v_g_post_ffn, v_w_in, v_b_forget, v_g_q_lora, v_w_uq, v_g_kv_lora, v_w_ukv, v_w_proj_fox, v_w_proj_mla, v_w_out, v_w_ffn_in, v_w_ffn_out):
    prm = dict(w_ada=w_ada, b_ada=b_ada, g_pre_mix=g_pre_mix, g_post_mix=g_post_mix, g_pre_ffn=g_pre_ffn,
               g_post_ffn=g_post_ffn, w_in=w_in, b_forget=b_forget, g_q_lora=g_q_lora, w_uq=w_uq, g_kv_lora=g_kv_lora,
               w_ukv=w_ukv, w_proj_fox=w_proj_fox, w_proj_mla=w_proj_mla, w_out=w_out, w_ffn_in=w_ffn_in, w_ffn_out=w_ffn_out)
    mom = dict(w_ada=m_w_ada, b_ada=m_b_ada, g_pre_mix=m_g_pre_mix, g_post_mix=m_g_post_mix, g_pre_ffn=m_g_pre_ffn,
               g_post_ffn=m_g_post_ffn, w_in=m_w_in, b_forget=m_b_forget, g_q_lora=m_g_q_lora, w_uq=m_w_uq,
               g_kv_lora=m_g_kv_lora, w_ukv=m_w_ukv, w_proj_fox=m_w_proj_fox, w_proj_mla=m_w_proj_mla, w_out=m_w_out,
               w_ffn_in=m_w_ffn_in, w_ffn_out=m_w_ffn_out)
    var = dict(w_ada=v_w_ada, b_ada=v_b_ada, g_pre_mix=v_g_pre_mix, g_post_mix=v_g_post_mix, g_pre_ffn=v_g_pre_ffn,
               g_post_ffn=v_g_post_ffn, w_in=v_w_in, b_forget=v_b_forget, g_q_lora=v_g_q_lora, w_uq=v_w_uq,
               g_kv_lora=v_g_kv_lora, w_ukv=v_w_ukv, w_proj_fox=v_w_proj_fox, w_proj_mla=v_w_proj_mla, w_out=v_w_out,
               w_ffn_in=v_w_ffn_in, w_ffn_out=v_w_ffn_out)
    me = _flat(*_coords())

    own = {n: prm[n][0].astype(BF16) for n in MATRICES}
    w_in_all, c_all = _all_gather([own["w_in"], c], "gather_in")
    w = _prepare_weights({"w_in": w_in_all})
    c_all = c_all.reshape(N_DEV, D)
    later = dict(lora=("w_uq", "w_ukv"), proj=("w_proj_fox", "w_proj_mla", "w_out"), ffn=("w_ffn_in", "w_ffn_out"))
    states, tok = _async_start([[own[n] for n in names] for names in later.values()], "gather", w_in_all, "gather_rest_start")
    gather_state = dict(zip(later, states))

    def wts(group, after):
        srcs, lands = _async_wait(gather_state[group], after, "gather_" + group + "_wait")
        return _prepare_weights({n: _with_own(land, src, me) for n, src, land in zip(later[group], srcs, lands)})

    sent = []

    def send(grads):
        shards = _shard_grads(grads)
        names = list(shards)
        (state,), t = _async_start([[shards[n] for n in names]], "exchange", jnp.zeros((8, LANES), F32),
                                   "exchange_" + names[0] + "_start")
        sent.append((names, state))
        return t

    ada_cols = w_ada.shape[2]
    b_cols = lax.dynamic_slice(b_ada, (0, me * ada_cols), (1, ada_cols))
    mod_cols, silu_c = _mod_part(c_all, w_ada[0], b_cols)
    (mod_all,) = _all_gather([mod_cols], "gather_mod")
    mod = lax.dynamic_index_in_dim(mod_all, me, axis=1, keepdims=False).reshape(1, 6 * D) + tok[0, 0]

    vec = dict(g_pre_mix=g_pre_mix, g_post_mix=g_post_mix, g_pre_ffn=g_pre_ffn, g_post_ffn=g_post_ffn,
               g_q_lora=g_q_lora, g_kv_lora=g_kv_lora, b_forget=b_forget)
    pos = positions.astype(F32).reshape(S, 1)
    grad_x, small = _fwd_bwd(x[0], pos, mod, loss_target[0], w, vec, wts, send)

    bundle = jnp.concatenate([small[n] for n in SMALL_ORDER], axis=1)
    (small_state,), tok = _async_start([[bundle]], "gather", jnp.zeros((8, LANES), F32), "gather_small_start")

    out = {}
    after = tok
    for names, state in sent:
        srcs, lands = _async_wait(state, after, "exchange_" + names[0] + "_wait")
        for n, src, land in zip(names, srcs, lands):
            parts = _with_own(land, lax.dynamic_index_in_dim(src, me, 0, keepdims=False), me)
            out[n] = _adamw(prm[n][0], mom[n][0], var[n][0], parts, "adamw_" + n)
            after = out[n][0]

    (own_bundle,), (bundle_all,) = _async_wait(small_state, after, "gather_small_wait")
    bundle_all = _with_own(bundle_all, own_bundle, me)
    dmod_all = bundle_all[:, 0, :6 * D]
    dm_cols = lax.dynamic_slice(dmod_all, (0, me * ada_cols), (N_DEV, ada_cols))
    g_ada = _w_ada_grad(jnp.transpose(silu_c), dm_cols)
    out["w_ada"] = _adamw(w_ada[0], m_w_ada[0], v_w_ada[0], g_ada[None], "adamw_w_ada")

    offsets, off = {}, 0
    for n in SMALL_ORDER:
        offsets[n] = off
        off += small[n].shape[1]
    names = [SMALL_PARAM.get(n, n) for n in SMALL_ORDER if n != "err"]
    results, err = _adamw_rows(bundle_all, [offsets[n] for n in SMALL_ORDER if n != "err"],
                               [prm[n] for n in names], [mom[n] for n in names], [var[n] for n in names],
                               offsets["err"], D)
    out.update(zip(names, results))
    loss = 0.5 * jnp.sum(err) / D

    res = [loss, grad_x[None]]
    for kind in range(4):
        for n in WEIGHTS:
            t = out[n][kind]
            res.append(t[None] if prm[n].ndim == 3 else t)
    return tuple(res)
```

```python
import functools
import math

import jax
import jax.numpy as jnp
from jax import lax
from jax.experimental import pallas as pl
from jax.experimental.pallas import tpu as pltpu

F32 = jnp.float32
BF16 = jnp.bfloat16

N_DEV = 8
S = 2048
D = 1024
D_FF = 2816
HEADS = 8
HEAD_DIM = 64
Q_LORA = 768
KV_LORA = 256
ROPE_DIM = 32
ROPE_THETA = 10000.0
NORM_EPS = 1e-6
LANES = 128
VMEM_LIMIT = 56 * 1024 * 1024

ADAM_LR = 0.001
ADAM_B1 = 0.9
ADAM_B2 = 0.999
ADAM_EPS = 1e-08
ADAM_WD = 0.01
ADAM_STEP = 10

ATT_T = 256
N_ATT = S // ATT_T

NN = (((1,), (0,)), ((), ()))
NT = (((1,), (1,)), ((), ()))
TN = (((0,), (0,)), ((), ()))
MESH = pl.DeviceIdType.MESH


def _params(sem=None):
    return pltpu.CompilerParams(dimension_semantics=sem, vmem_limit_bytes=VMEM_LIMIT)


def _pick(n, cap):
    best = None
    for t in range(LANES, cap + 1, LANES):
        if n % t == 0:
            best = t
    return best if best is not None else n


def _mm(a, b, mode, out_dtype, name, acc=None, dep=None):
    if mode == "nn":
        (m, k), (k2, n), dn = a.shape, b.shape, NN
    elif mode == "nt":
        (m, k), (n, k2), dn = a.shape, b.shape, NT
    else:
        (k, m), (k2, n), dn = a.shape, b.shape, TN
    assert k == k2, (a.shape, b.shape, mode)
    tn = _pick(n, 640)
    tm = _pick(m, 1024)
    osz = jnp.dtype(out_dtype).itemsize

    def need(tm_):
        blk = tm_ * k * 2 + tn * k * 2 + tm_ * tn * osz + (tm_ * tn * 4 if acc is not None else 0)
        return 2 * blk + tm_ * tn * 4
    while need(tm) > 36 * 1024 * 1024 and tm % 256 == 0:
        tm //= 2

    def body(*refs):
        a_ref, b_ref, o_ref = refs[0], refs[1], refs[-1]
        r = lax.dot_general(a_ref[...], b_ref[...], dn, preferred_element_type=F32)
        if acc is not None:
            r = r + refs[2][...]
        o_ref[...] = r.astype(o_ref.dtype)

    if mode == "tn":
        a_spec = pl.BlockSpec((k, tm), lambda i, j: (0, i))
    else:
        a_spec = pl.BlockSpec((tm, k), lambda i, j: (i, 0))
    if mode == "nt":
        b_spec = pl.BlockSpec((tn, k), lambda i, j: (j, 0))
    else:
        b_spec = pl.BlockSpec((k, tn), lambda i, j: (0, j))
    o_spec = pl.BlockSpec((tm, tn), lambda i, j: (i, j))
    in_specs = [a_spec, b_spec] + ([o_spec] if acc is not None else [])
    in_specs += [pl.BlockSpec(memory_space=pl.ANY)] if dep is not None else []
    args = (a, b) + ((acc,) if acc is not None else ()) + ((dep,) if dep is not None else ())
    return pl.pallas_call(
        body, name=name, grid=(m // tm, n // tn),
        in_specs=in_specs, out_specs=o_spec,
        out_shape=jax.ShapeDtypeStruct((m, n), out_dtype),
        compiler_params=_params(("parallel", "parallel")),
    )(*args)


def _mm_epi(a, b, mode, tnb, epi, name, tm, rows=(), vecs=(), outs=(), sums=()):
    m, k = a.shape
    nb = b.shape[1] if mode == "nn" else b.shape[0]
    dn = NN if mode == "nn" else NT
    n_in = 2 + len(rows) + len(vecs)

    def body(*refs):
        r = lax.dot_general(refs[0][...], refs[1][...], dn, preferred_element_type=F32)
        o_vals, s_vals = epi(r, *[x[...] for x in refs[2:n_in]])
        o_refs = refs[n_in:n_in + len(outs)]
        s_refs = refs[n_in + len(outs):]
        assert len(o_vals) == len(o_refs) and len(s_vals) == len(s_refs)
        for o_ref, val in zip(o_refs, o_vals):
            o_ref[...] = val.astype(o_ref.dtype)
        if sums:
            @pl.when((pl.program_id(0) == 0) & (pl.program_id(1) == 0))
            def _():
                for s_ref in s_refs:
                    s_ref[...] = jnp.zeros(s_ref.shape, F32)
            for s_ref, val in zip(s_refs, s_vals):
                s_ref[...] += val

    b_spec = pl.BlockSpec((k, tnb), lambda i, j: (0, j)) if mode == "nn" else pl.BlockSpec((tnb, k), lambda i, j: (j, 0))
    in_specs = [pl.BlockSpec((tm, k), lambda i, j: (i, 0)), b_spec]
    rows = [tuple(r) + (0,) * (3 - len(r)) for r in rows]
    in_specs += [pl.BlockSpec((tm, w), functools.partial(lambda i, j, off: (i, j + off), off=off)) for _, w, off in rows]
    in_specs += [pl.BlockSpec(v.shape, lambda i, j: (0, 0)) for v in vecs]
    out_specs = [pl.BlockSpec((tm, w), lambda i, j: (i, j)) for _, w, _ in outs]
    out_specs += [pl.BlockSpec((1, w), lambda i, j: (0, 0)) for w in sums]
    out_shape = [jax.ShapeDtypeStruct((m, full), dt) for full, _, dt in outs]
    out_shape += [jax.ShapeDtypeStruct((1, w), F32) for w in sums]
    return pl.pallas_call(
        body, name=name, grid=(m // tm, nb // tnb),
        in_specs=in_specs, out_specs=out_specs, out_shape=out_shape,
        compiler_params=_params(("arbitrary", "arbitrary") if sums else ("parallel", "parallel")),
    )(a, b, *[r[0] for r in rows], *vecs)


def _rowwise(fn, row_ins, vec_ins, row_outs, sum_outs, name, tm=256):
    n_in = len(row_ins) + len(vec_ins)
    n_o = len(row_outs)
    rows = row_ins[0][0].shape[0]

    def body(*refs):
        vals = [r[...] for r in refs[:n_in]]
        outs = refs[n_in:]
        ro, so = fn(*vals)
        assert len(ro) == n_o and len(so) == len(sum_outs)
        for r, v in zip(outs[:n_o], ro):
            r[...] = v.astype(r.dtype)
        if sum_outs:
            @pl.when(pl.program_id(0) == 0)
            def _():
                for r in outs[n_o:]:
                    r[...] = jnp.zeros(r.shape, F32)
            for r, v in zip(outs[n_o:], so):
                r[...] += v

    in_specs = [pl.BlockSpec((tm, w), functools.partial(lambda i, b: (i, b), b=b)) for _, w, b in row_ins]
    in_specs += [pl.BlockSpec(v.shape, lambda i: (0, 0)) for v in vec_ins]
    out_specs = [pl.BlockSpec((tm, w), lambda i: (i, 0)) for w, _ in row_outs]
    out_specs += [pl.BlockSpec((1, w), lambda i: (0, 0)) for w in sum_outs]
    out_shape = [jax.ShapeDtypeStruct((rows, w), dt) for w, dt in row_outs]
    out_shape += [jax.ShapeDtypeStruct((1, w), F32) for w in sum_outs]
    return pl.pallas_call(
        body, name=name, grid=(rows // tm,),
        in_specs=in_specs, out_specs=out_specs, out_shape=out_shape,
        compiler_params=_params(("arbitrary",)),
    )(*[a for a, _, _ in row_ins], *vec_ins)


def _sigmoid(x):
    return 1.0 / (1.0 + jnp.exp(-x))


def _rstd(x):
    return lax.rsqrt(jnp.mean(x * x, axis=-1, keepdims=True) + NORM_EPS)


def _norm_bwd(dyn, xn, r):
    return r * (dyn - xn * jnp.mean(dyn * xn, axis=-1, keepdims=True))


def _colsum(x):
    return jnp.sum(x, axis=0, keepdims=True)


def _rope_tables(pos, invf):
    def fn(p, f):
        lane = lax.broadcasted_iota(jnp.int32, (1, LANES), 1)
        ang = p * f
        cs, sn = jnp.cos(ang), jnp.sin(ang)
        rot = (lane >= 64) & (lane < 96)
        ct = jnp.where(lane < 64, 1.0, jnp.where(rot, cs, 0.0))
        sa = jnp.where((lane >= 64) & (lane < 80), -sn, 0.0)
        sb = jnp.where((lane >= 80) & (lane < 96), sn, 0.0)
        return (ct, sa, sb), ()
    return _rowwise(fn, [(pos, 1, 0)], [invf], [(LANES, F32)] * 3, [], "rope_tables")


def _rope(x, ct, sa, sb):
    return x * ct + pltpu.roll(x, LANES - 16, 1) * sa + pltpu.roll(x, 16, 1) * sb


def _rope_t(x, ct, sa, sb):
    return x * ct - pltpu.roll(x, LANES - 16, 1) * sa - pltpu.roll(x, 16, 1) * sb


def _head_mask(width, hh):
    lane = lax.broadcasted_iota(jnp.int32, (1, width), 1)
    half = width // 2
    return (lane >= hh * half) & (lane < (hh + 1) * half)


ATT_PP = 2
ATT_CHAINS = [(a, hh) for a in range(ATT_PP) for hh in range(2)]
ATT_G = HEADS // (2 * ATT_PP)


def _pair(ref_or_val, a, width, rows=slice(None)):
    return ref_or_val[rows, a * width:(a + 1) * width]


def _attn_fwd(q, qo, k, ko, v, vo, dkp, scale, bias, name):
    T = ATT_T
    assert qo % ATT_PP == 0 and ko % ATT_PP == 0 and vo % ATT_PP == 0
    qo, ko, vo = qo // ATT_PP, ko // ATT_PP, vo // ATT_PP

    def body(*refs):
        if bias is not None:
            q_ref, k_ref, v_ref, b_ref, o_ref, lse_ref, s_scr = refs
        else:
            q_ref, k_ref, v_ref, o_ref, lse_ref, s_scr = refs
        i = pl.program_id(1)
        row = lax.broadcasted_iota(jnp.int32, (T, T), 0)
        col = lax.broadcasted_iota(jnp.int32, (T, T), 1)
        qms = []
        for a, hh in ATT_CHAINS:
            qb = _pair(q_ref, a, dkp)
            qms.append(jnp.where(_head_mask(dkp, hh), qb, jnp.zeros_like(qb)))

        def fold(t):
            return [t[:, c * LANES:(c + 1) * LANES] for c in range(T // LANES)]

        def run(nt):
            mls = [jnp.full((T, LANES), -jnp.inf, F32) for _ in ATT_CHAINS]
            for j in range(nt):
                ks = slice(j * T, (j + 1) * T)
                for ci, (a, hh) in enumerate(ATT_CHAINS):
                    s = lax.dot_general(qms[ci], _pair(k_ref, a, dkp, ks), NT, preferred_element_type=F32) * scale
                    if bias is not None:
                        s = s + b_ref[2 * a + hh, j]
                    if j == nt - 1:
                        s = jnp.where(row >= col, s, -jnp.inf)
                    s_scr[ci, j] = s
                    for part in fold(s):
                        mls[ci] = jnp.maximum(mls[ci], part)
            ms = [jnp.max(ml, axis=1, keepdims=True) for ml in mls]
            mbs = [jnp.broadcast_to(m, (T, LANES)) for m in ms]
            for a in range(ATT_PP):
                ls = [jnp.zeros((T, LANES), F32) for _ in range(2)]
                ps, vms = [], []
                for j in range(nt):
                    vb = _pair(v_ref, a, LANES, slice(j * T, (j + 1) * T))
                    for hh in range(2):
                        parts = [jnp.exp(part - mbs[2 * a + hh]) for part in fold(s_scr[2 * a + hh, j])]
                        for part in parts:
                            ls[hh] = ls[hh] + part
                        ps.append(jnp.concatenate(parts, axis=1).astype(BF16))
                        vms.append(jnp.where(_head_mask(LANES, hh), vb, jnp.zeros_like(vb)))
                acc = lax.dot_general(jnp.concatenate(ps, axis=1), jnp.concatenate(vms, axis=0), NN,
                                      preferred_element_type=F32)
                l0, l1 = [jnp.sum(l, axis=1, keepdims=True) for l in ls]
                lse_ref[2 * a] = ms[2 * a] + jnp.log(l0)
                lse_ref[2 * a + 1] = ms[2 * a + 1] + jnp.log(l1)
                inv = jnp.where(_head_mask(LANES, 0), 1.0 / l0, 1.0 / l1)
                o_ref[:, a * LANES:(a + 1) * LANES] = (acc * inv).astype(o_ref.dtype)

        for nt in range(1, N_ATT + 1):
            pl.when(i == nt - 1)(functools.partial(run, nt))

    in_specs = [
        pl.BlockSpec((T, ATT_PP * dkp), lambda g, i: (i, qo + g)),
        pl.BlockSpec((S, ATT_PP * dkp), lambda g, i: (0, ko + g)),
        pl.BlockSpec((S, ATT_PP * LANES), lambda g, i: (0, vo + g)),
    ]
    args = [q, k, v]
    if bias is not None:
        in_specs.append(pl.BlockSpec((2 * ATT_PP, N_ATT, 1, T), lambda g, i: (g, 0, 0, 0)))
        args.append(bias)
    return pl.pallas_call(
        body, name=name, grid=(ATT_G, N_ATT),
        in_specs=in_specs,
        out_specs=[pl.BlockSpec((T, ATT_PP * LANES), lambda g, i: (i, g)),
                   pl.BlockSpec((2 * ATT_PP, T, 1), lambda g, i: (g, i, 0))],
        out_shape=[jax.ShapeDtypeStruct((S, HEADS * HEAD_DIM), BF16),
                   jax.ShapeDtypeStruct((HEADS, S, 1), F32)],
        scratch_shapes=[pltpu.VMEM((len(ATT_CHAINS), N_ATT, T, T), F32)],
        compiler_params=_params(("parallel", "arbitrary")),
    )(*args)


def _attn_grad(q, qo, k, ko, v, vo, do, lse, dkp, scale, bias, qk_dtype, name):
    T = ATT_T
    has_b = bias is not None
    qo, ko, vo = qo // ATT_PP, ko // ATT_PP, vo // ATT_PP
    n_ch = len(ATT_CHAINS)

    def body(*refs):
        q_ref, k_ref, v_ref, do_ref, lse_ref = refs[:5]
        refs = refs[5:]
        if has_b:
            b_ref, refs = refs[0], refs[1:]
        dq_ref, dk_ref, dv_ref = refs[:3]
        refs = refs[3:]
        if has_b:
            db_ref, refs = refs[0], refs[1:]
        p_scr, dp_scr, dk_acc, dv_acc = refs[:4]
        db_acc = refs[4] if has_b else None
        i = pl.program_id(1)

        @pl.when(i == 0)
        def _():
            dk_acc[...] = jnp.zeros(dk_acc.shape, F32)
            dv_acc[...] = jnp.zeros(dv_acc.shape, F32)
            if has_b:
                db_acc[...] = jnp.zeros(db_acc.shape, F32)

        row = lax.broadcasted_iota(jnp.int32, (T, T), 0)
        col = lax.broadcasted_iota(jnp.int32, (T, T), 1)

        def fold(t):
            return [t[:, c * LANES:(c + 1) * LANES] for c in range(T // LANES)]

        qms, doms, lses = [], [], []
        for a, hh in ATT_CHAINS:
            qb, dob = _pair(q_ref, a, dkp), _pair(do_ref, a, LANES)
            qms.append(jnp.where(_head_mask(dkp, hh), qb, jnp.zeros_like(qb)))
            doms.append(jnp.where(_head_mask(LANES, hh), dob, jnp.zeros_like(dob)))
            lses.append(lse_ref[2 * a + hh])

        def run(nt):
            dls = [jnp.zeros((T, LANES), F32) for _ in ATT_CHAINS]
            for j in range(nt):
                ks = slice(j * T, (j + 1) * T)
                for ci, (a, hh) in enumerate(ATT_CHAINS):
                    s = lax.dot_general(qms[ci], _pair(k_ref, a, dkp, ks), NT, preferred_element_type=F32) * scale
                    if has_b:
                        s = s + b_ref[ci, j]
                    s = s - lses[ci]
                    if j == nt - 1:
                        s = jnp.where(row >= col, s, -jnp.inf)
                    p = jnp.exp(s)
                    dp = lax.dot_general(doms[ci], _pair(v_ref, a, LANES, ks), NT, preferred_element_type=F32)
                    p_scr[ci, j] = p
                    dp_scr[ci, j] = dp
                    for part in fold(p * dp):
                        dls[ci] = dls[ci] + part
            deltas = [jnp.broadcast_to(jnp.sum(dl, axis=1, keepdims=True), (T, LANES)) for dl in dls]
            for a in range(ATT_PP):
                ds_all, km_all = [], []
                qm2t = jnp.transpose(jnp.concatenate([qms[2 * a], qms[2 * a + 1]], axis=0))
                dom2t = jnp.transpose(jnp.concatenate([doms[2 * a], doms[2 * a + 1]], axis=0))
                for j in range(nt):
                    ks = slice(j * T, (j + 1) * T)
                    kb = _pair(k_ref, a, dkp, ks)
                    p2, ds2 = [], []
                    for hh in range(2):
                        ci = 2 * a + hh
                        p = p_scr[ci, j]
                        ds = jnp.concatenate([pp * (dd - deltas[ci]) for pp, dd in zip(fold(p), fold(dp_scr[ci, j]))], axis=1)
                        if has_b:
                            db_acc[ci, j] += jnp.sum(ds, axis=0, keepdims=True)
                        p2.append(p.astype(BF16))
                        ds2.append((ds * scale).astype(BF16))
                        km_all.append(jnp.where(_head_mask(dkp, hh), kb, jnp.zeros_like(kb)))
                    dv_acc[a * LANES:(a + 1) * LANES, ks] += lax.dot_general(
                        dom2t, jnp.concatenate(p2, axis=0), NN, preferred_element_type=F32)
                    dk_acc[a * dkp:(a + 1) * dkp, ks] += lax.dot_general(
                        qm2t, jnp.concatenate(ds2, axis=0), NN, preferred_element_type=F32)
                    ds_all += ds2
                dq = lax.dot_general(jnp.concatenate(ds_all, axis=1), jnp.concatenate(km_all, axis=0), NN,
                                     preferred_element_type=F32)
                dq_ref[:, a * dkp:(a + 1) * dkp] = dq.astype(dq_ref.dtype)

        for nt in range(1, N_ATT + 1):
            pl.when(i == nt - 1)(functools.partial(run, nt))

        @pl.when(i == N_ATT - 1)
        def _():
            dk_ref[...] = jnp.transpose(dk_acc[...]).astype(dk_ref.dtype)
            dv_ref[...] = jnp.transpose(dv_acc[...]).astype(dv_ref.dtype)
            if has_b:
                db_ref[...] = db_acc[...]

    in_specs = [
        pl.BlockSpec((T, ATT_PP * dkp), lambda g, i: (i, qo + g)),
        pl.BlockSpec((S, ATT_PP * dkp), lambda g, i: (0, ko + g)),
        pl.BlockSpec((S, ATT_PP * LANES), lambda g, i: (0, vo + g)),
        pl.BlockSpec((T, ATT_PP * LANES), lambda g, i: (i, g)),
        pl.BlockSpec((2 * ATT_PP, T, 1), lambda g, i: (g, i, 0)),
    ]
    args = [q, k, v, do, lse]
    out_specs = [
        pl.BlockSpec((T, ATT_PP * dkp), lambda g, i: (i, g)),
        pl.BlockSpec((S, ATT_PP * dkp), lambda g, i: (0, g)),
        pl.BlockSpec((S, ATT_PP * LANES), lambda g, i: (0, g)),
    ]
    width = (HEADS // 2) * dkp
    out_shape = [
        jax.ShapeDtypeStruct((S, width), qk_dtype),
        jax.ShapeDtypeStruct((S, width), qk_dtype),
        jax.ShapeDtypeStruct((S, HEADS * HEAD_DIM), BF16),
    ]
    scratch = [pltpu.VMEM((n_ch, N_ATT, T, T), F32), pltpu.VMEM((n_ch, N_ATT, T, T), F32),
               pltpu.VMEM((ATT_PP * dkp, S), F32), pltpu.VMEM((ATT_PP * LANES, S), F32)]
    if has_b:
        bspec = pl.BlockSpec((2 * ATT_PP, N_ATT, 1, T), lambda g, i: (g, 0, 0, 0))
        in_specs.append(bspec)
        args.append(bias)
        out_specs.append(bspec)
        out_shape.append(jax.ShapeDtypeStruct((HEADS, N_ATT, 1, T), F32))
        scratch.append(pltpu.VMEM((2 * ATT_PP, N_ATT, 1, T), F32))
    return pl.pallas_call(
        body, name=name, grid=(ATT_G, N_ATT),
        in_specs=in_specs, out_specs=out_specs, out_shape=out_shape, scratch_shapes=scratch,
        compiler_params=_params(("parallel", "arbitrary")),
    )(*args)


def _tri(upper):
    a = lax.broadcasted_iota(jnp.int32, (LANES, LANES), 0)
    b = lax.broadcasted_iota(jnp.int32, (LANES, LANES), 1)
    return jnp.where(a <= b if upper else a >= b, 1.0, 0.0).astype(F32)


def _fox_gates(zt, bf):
    def body(z_ref, b_ref, o_ref):
        tri = _tri(True)
        carry = jnp.zeros((HEADS, 1), F32)
        for t in range(S // LANES):
            sl = slice(t * LANES, (t + 1) * LANES)
            z = z_ref[:, sl] + b_ref[...]
            logf = jnp.minimum(z, 0.0) - jnp.log(1.0 + jnp.exp(-jnp.abs(z)))
            c = lax.dot_general(logf, tri, NN, preferred_element_type=F32,
                                precision=lax.Precision.HIGHEST) + carry
            o_ref[:, sl] = -c
            carry = c[:, LANES - 1:LANES]

    return pl.pallas_call(
        body, name="fox_gates", out_shape=jax.ShapeDtypeStruct((HEADS, S), F32),
        compiler_params=_params(),
    )(zt, bf)


def _fox_gates_bwd(dbias, zt, bf):
    def body(d_ref, z_ref, b_ref, dz_ref, dbf_ref):
        tri = _tri(False)
        carry = jnp.zeros((HEADS, 1), F32)
        tot = jnp.zeros((HEADS, 1), F32)
        for t in reversed(range(S // LANES)):
            sl = slice(t * LANES, (t + 1) * LANES)
            df = -d_ref[:, sl]
            c = lax.dot_general(df, tri, NN, preferred_element_type=F32,
                                precision=lax.Precision.HIGHEST) + carry
            carry = c[:, 0:1]
            z = z_ref[:, sl] + b_ref[...]
            dz = c * _sigmoid(-z)
            dz_ref[:, sl] = dz
            tot = tot + jnp.sum(dz, axis=1, keepdims=True)
        dbf_ref[...] = tot

    return pl.pallas_call(
        body, name="fox_gates_bwd",
        out_shape=[jax.ShapeDtypeStruct((HEADS, S), F32), jax.ShapeDtypeStruct((HEADS, 1), F32)],
        compiler_params=_params(),
    )(dbias, zt, bf)


def _mod_part(c_all, w_ada, b_cols):
    def body(c_ref, w_ref, b_ref, o_ref, s_ref):
        c = c_ref[...]
        sc = c * _sigmoid(c)
        s_ref[...] = sc
        o_ref[...] = lax.dot_general(sc, w_ref[...], NN, preferred_element_type=F32,
                                     precision=lax.Precision.HIGHEST) + b_ref[...]

    return pl.pallas_call(
        body, name="mod_part",
        out_shape=[jax.ShapeDtypeStruct((N_DEV, w_ada.shape[1]), F32), jax.ShapeDtypeStruct(c_all.shape, F32)],
        compiler_params=_params(),
    )(c_all, w_ada, b_cols)


def _w_ada_grad(sc_t, dm):
    def body(s_ref, d_ref, o_ref):
        acc = jnp.zeros(o_ref.shape, F32)
        for b in range(N_DEV):
            acc = acc + s_ref[:, b:b + 1] * d_ref[b:b + 1, :]
        o_ref[...] = acc

    return pl.pallas_call(
        body, name="w_ada_grad", out_shape=jax.ShapeDtypeStruct((sc_t.shape[0], dm.shape[1]), F32),
        compiler_params=_params(),
    )(sc_t, dm)


def _adamw(w, m, v, parts, name):
    rows, cols = w.shape
    n = parts.shape[0]
    tr = rows if rows <= 512 else 256

    def body(w_ref, m_ref, v_ref, p_ref, g_out, d_out, m_out, v_out):
        g = p_ref[0].astype(F32)
        for kk in range(1, n):
            g = g + p_ref[kk].astype(F32)
        g_out[...] = g
        d_out[...], m_out[...], v_out[...] = _adamw_math(w_ref[...], g, m_ref[...], v_ref[...])

    spec = pl.BlockSpec((tr, cols), lambda i: (i, 0))
    return pl.pallas_call(
        body, name=name, grid=(rows // tr,),
        in_specs=[spec, spec, spec, pl.BlockSpec((n, tr, cols), lambda i: (0, i, 0))],
        out_specs=[spec] * 4, out_shape=[jax.ShapeDtypeStruct((rows, cols), F32)] * 4,
        compiler_params=_params(("parallel",)),
    )(w, m, v, parts)


def _adamw_math(w, g, m, v):
    mm = ADAM_B1 * m + (1.0 - ADAM_B1) * g
    vv = ADAM_B2 * v + (1.0 - ADAM_B2) * (g * g)
    m_hat = mm / (1.0 - ADAM_B1 ** ADAM_STEP)
    v_hat = vv / (1.0 - ADAM_B2 ** ADAM_STEP)
    return -ADAM_LR * (m_hat / (jnp.sqrt(v_hat) + ADAM_EPS) + ADAM_WD * w), mm, vv


def _adamw_rows(bundles, offsets, ws, ms, vs, err_off, err_width):
    k = len(ws)

    def body(*refs):
        b_ref = refs[0]
        w_refs, m_refs, v_refs = refs[1:1 + k], refs[1 + k:1 + 2 * k], refs[1 + 2 * k:1 + 3 * k]
        outs = refs[1 + 3 * k:]
        g_all = b_ref[0]
        for kk in range(1, N_DEV):
            g_all = g_all + b_ref[kk]
        for i in range(k):
            width = w_refs[i].shape[1]
            g = g_all[:, offsets[i]:offsets[i] + width]
            outs[4 * i][...] = g
            outs[4 * i + 1][...], outs[4 * i + 2][...], outs[4 * i + 3][...] = _adamw_math(
                w_refs[i][...], g, m_refs[i][...], v_refs[i][...])
        outs[4 * k][...] = g_all[:, err_off:err_off + err_width]

    out_shape = []
    for w_ in ws:
        out_shape += [jax.ShapeDtypeStruct(w_.shape, F32)] * 4
    out_shape.append(jax.ShapeDtypeStruct((1, err_width), F32))
    res = pl.pallas_call(body, name="adamw_rows", out_shape=out_shape, compiler_params=_params())(bundles, *ws, *ms, *vs)
    return [tuple(res[4 * i:4 * i + 4]) for i in range(k)], res[-1]


def _coords():
    return lax.axis_index("x"), lax.axis_index("y"), lax.axis_index("c")


def _flat(px, py, pc):
    return 4 * px + 2 * py + pc


def _all_gather(arrs, name):
    n = len(arrs)

    def body(*refs):
        ins, outs = refs[:n], refs[n:2 * n]
        send, recv, lsem = refs[2 * n:]
        x, y, c = _coords()
        me, sibling = (x, y, c), (x, y, 1 - c)
        chips = [(1 - x, y), (x, 1 - y), (1 - x, 1 - y)]

        def copy(a, kk, block, to, src=None):
            slot = outs[a].at[_flat(*block)]
            return pltpu.make_async_remote_copy(
                src_ref=slot if src is None else src, dst_ref=slot,
                send_sem=send.at[a, kk], recv_sem=recv.at[a, kk],
                device_id=to, device_id_type=MESH)

        mine = [pltpu.make_async_copy(ins[a], outs[a].at[_flat(*me)], lsem.at[a]) for a in range(n)]
        for cp in mine:
            cp.start()
        first = []
        for a in range(n):
            first.append(copy(a, 0, me, sibling, src=ins[a]))
            first += [copy(a, 1 + j, me, (*chip, c), src=ins[a]) for j, chip in enumerate(chips)]
        for cp in first:
            cp.start()
        passed = []
        for j, chip in enumerate(chips):
            for a in range(n):
                copy(a, 1 + j, (*chip, c), me).wait_recv()
                cp = copy(a, 4 + j, (*chip, c), sibling)
                cp.start()
                passed.append(cp)
        for a in range(n):
            copy(a, 0, sibling, me).wait_recv()
        for j, chip in enumerate(chips):
            for a in range(n):
                copy(a, 4 + j, (*chip, 1 - c), me).wait_recv()
        for cp in first + passed:
            cp.wait_send()
        for cp in mine:
            cp.wait()

    any_spec = pl.BlockSpec(memory_space=pl.ANY)
    return pl.pallas_call(
        body, name=name,
        in_specs=[any_spec] * n, out_specs=[any_spec] * n,
        out_shape=[jax.ShapeDtypeStruct((N_DEV,) + a.shape, a.dtype) for a in arrs],
        scratch_shapes=[pltpu.SemaphoreType.DMA((n, 7)), pltpu.SemaphoreType.DMA((n, 7)),
                        pltpu.SemaphoreType.DMA((n,))],
    )(*arrs)


def _peer_list():
    x, y, c = _coords()
    return [((1 - x if r & 4 else x), (1 - y if r & 2 else y), (1 - c if r & 1 else c)) for r in range(1, N_DEV)]


def _copy_for(mode, src, land, send, recv, peer, me):
    src_ref = src if mode == "gather" else src.at[_flat(*peer)]
    return pltpu.make_async_remote_copy(src_ref=src_ref, dst_ref=land.at[me], send_sem=send, recv_sem=recv,
                                        device_id=peer, device_id_type=MESH)


HBM_SPEC = pl.BlockSpec(memory_space=pltpu.HBM)
SEM_SPEC = pl.BlockSpec(memory_space=pltpu.SEMAPHORE)
ANY_SPEC = pl.BlockSpec(memory_space=pl.ANY)
SIDE_EFFECT = pltpu.SideEffectType.DATAFLOW_SIDE_EFFECTING


def _async_start(groups, mode, after, name):
    flat_arrs = [a for g in groups for a in g]
    n = len(flat_arrs)

    def body(*refs):
        srcs, lands = refs[:n], refs[n:2 * n]
        outs = refs[2 * n + 1:]
        token = outs[-1]
        me = _flat(*_coords())
        for ai in range(n):
            for peer in _peer_list():
                _copy_for(mode, srcs[ai], lands[ai], outs[2 * ai], outs[2 * ai + 1], peer, me).start()
        token[...] = jnp.zeros(token.shape, F32)

    land_shapes = [((N_DEV,) + a.shape if mode == "gather" else a.shape) for a in flat_arrs]
    out_shape = [pltpu.SemaphoreType.DMA(())] * (2 * n)
    out_shape += [pltpu.HBM(a.shape, a.dtype) for a in flat_arrs]
    out_shape += [pltpu.HBM(s, a.dtype) for s, a in zip(land_shapes, flat_arrs)]
    out_shape.append(jax.ShapeDtypeStruct((8, LANES), F32))
    res = pl.pallas_call(
        body, name=name, out_shape=tuple(out_shape),
        in_specs=[HBM_SPEC] * (2 * n) + [ANY_SPEC],
        out_specs=tuple([SEM_SPEC] * (2 * n) + [HBM_SPEC] * (2 * n) + [pl.BlockSpec(memory_space=pltpu.VMEM)]),
        input_output_aliases={i: 2 * n + i for i in range(2 * n)},
        compiler_params=pltpu.CompilerParams(has_side_effects=SIDE_EFFECT),
    )(*[pltpu.with_memory_space_constraint(a, pltpu.HBM) for a in flat_arrs],
      *[pltpu.with_memory_space_constraint(lax.empty(s, a.dtype), pltpu.HBM) for s, a in zip(land_shapes, flat_arrs)],
      after)
    sems, thru = res[:2 * n], res[2 * n:-1]
    states, idx = [], 0
    for g in groups:
        k = len(g)
        states.append((list(sems[2 * idx:2 * (idx + k):2]), list(sems[2 * idx + 1:2 * (idx + k):2]),
                       list(thru[idx:idx + k]), list(thru[n + idx:n + idx + k])))
        idx += k
    return states, res[-1]


def _async_wait(state, after, name):
    sends, recvs, srcs, lands = state
    g = len(srcs)

    def body(*refs):
        l_refs, sems = refs[g:2 * g], refs[2 * g:4 * g]
        for ai in range(g):
            seven = l_refs[ai].at[pl.ds(0, N_DEV - 1)]
            cp = pltpu.make_async_remote_copy(src_ref=seven, dst_ref=seven, send_sem=sems[ai], recv_sem=sems[g + ai],
                                              device_id=_coords(), device_id_type=MESH)
            cp.wait_send()
            cp.wait_recv()

    res = pl.pallas_call(
        body, name=name,
        out_shape=tuple([pltpu.HBM(a.shape, a.dtype) for a in srcs] + [pltpu.HBM(a.shape, a.dtype) for a in lands]),
        in_specs=[HBM_SPEC] * (2 * g) + [SEM_SPEC] * (2 * g) + [ANY_SPEC],
        out_specs=tuple([HBM_SPEC] * (2 * g)),
        input_output_aliases={i: i for i in range(2 * g)},
        compiler_params=pltpu.CompilerParams(has_side_effects=SIDE_EFFECT),
    )(*srcs, *lands, *sends, *recvs, after)
    return list(res[:g]), list(res[g:])


def _with_own(land, own, me):
    return lax.dynamic_update_index_in_dim(land, own, me, 0)


IN_SPLITS = (512, 512, 512, 8, 768, 256, 32, 1024, 1024)


def _unshard_cols(g):
    _, k, n = g.shape
    tr = min(k, 256)

    def body(g_ref, o_ref):
        o_ref[...] = jnp.concatenate([g_ref[j] for j in range(N_DEV)], axis=1)

    return pl.pallas_call(
        body, name="unshard_cols_%d" % n, grid=(k // tr,),
        in_specs=[pl.BlockSpec((N_DEV, tr, n), lambda i: (0, i, 0))],
        out_specs=pl.BlockSpec((tr, N_DEV * n), lambda i: (i, 0)),
        out_shape=jax.ShapeDtypeStruct((k, N_DEV * n), g.dtype),
        compiler_params=_params(("parallel",)),
    )(g)


FFN_T = 256
FFN_SHARD = 2 * D_FF // N_DEV


def _unshard_ffn_in(g):
    tr = 256

    def body(g_ref, o_ref):
        full = jnp.concatenate([g_ref[j] for j in range(N_DEV)], axis=1)
        parts = []
        for j in range(D_FF // FFN_T):
            parts += [full[:, j * FFN_T:(j + 1) * FFN_T], full[:, D_FF + j * FFN_T:D_FF + (j + 1) * FFN_T]]
        o_ref[...] = jnp.concatenate(parts, axis=1)

    return pl.pallas_call(
        body, name="unshard_ffn_in", grid=(D // tr,),
        in_specs=[pl.BlockSpec((N_DEV, tr, FFN_SHARD), lambda i: (0, i, 0))],
        out_specs=pl.BlockSpec((tr, 2 * D_FF), lambda i: (i, 0)),
        out_shape=jax.ShapeDtypeStruct((D, 2 * D_FF), g.dtype),
        compiler_params=_params(("parallel",)),
    )(g)


def _shard_ffn_in(w):
    tr = 256

    def body(w_ref, o_ref):
        x = w_ref[...]
        nb = D_FF // FFN_T
        full = jnp.concatenate([x[:, (2 * j + half) * FFN_T:(2 * j + half + 1) * FFN_T]
                                for half in range(2) for j in range(nb)], axis=1)
        for j in range(N_DEV):
            o_ref[j] = full[:, j * FFN_SHARD:(j + 1) * FFN_SHARD]

    return pl.pallas_call(
        body, name="shard_ffn_in", grid=(D // tr,),
        in_specs=[pl.BlockSpec((tr, 2 * D_FF), lambda i: (i, 0))],
        out_specs=pl.BlockSpec((N_DEV, tr, FFN_SHARD), lambda i: (0, i, 0)),
        out_shape=jax.ShapeDtypeStruct((N_DEV, D, FFN_SHARD), w.dtype),
        compiler_params=_params(("parallel",)),
    )(w)


def _shard_cols(w):
    k, n = w.shape[0], w.shape[1] // N_DEV
    tr = min(k, 256)

    def body(w_ref, o_ref):
        full = w_ref[...]
        for j in range(N_DEV):
            o_ref[j] = full[:, j * n:(j + 1) * n]

    return pl.pallas_call(
        body, name="shard_cols_%d" % n, grid=(k // tr,),
        in_specs=[pl.BlockSpec((tr, N_DEV * n), lambda i: (i, 0))],
        out_specs=pl.BlockSpec((N_DEV, tr, n), lambda i: (0, i, 0)),
        out_shape=jax.ShapeDtypeStruct((N_DEV, k, n), w.dtype),
        compiler_params=_params(("parallel",)),
    )(w)


IN_OFFS = tuple(sum(IN_SPLITS[:i]) for i in range(len(IN_SPLITS) + 1))
IN_SHARD = IN_OFFS[-1] // N_DEV
REGROUP_ROWS = 128


def _w_in_regroup(g):
    def body(g_ref, a_ref, b_ref):
        full = jnp.concatenate([g_ref[j] for j in range(N_DEV)], axis=1)
        fq, fk, fv, wf, cq, ckv, kr, gf, gm = [full[:, IN_OFFS[i]:IN_OFFS[i + 1]] for i in range(9)]
        rows = full.shape[0]
        a_ref[...] = jnp.concatenate([cq, ckv, gf, gm, wf, jnp.zeros((rows, 56), BF16), kr, jnp.zeros((rows, 32), BF16)], axis=1)
        b_ref[...] = jnp.concatenate([fq, fk, fv], axis=1)

    tr = REGROUP_ROWS
    return pl.pallas_call(
        body, name="w_in_regroup", grid=(D // tr,),
        in_specs=[pl.BlockSpec((N_DEV, tr, IN_SHARD), lambda i: (0, i, 0))],
        out_specs=[pl.BlockSpec((tr, 3200), lambda i: (i, 0)), pl.BlockSpec((tr, 1536), lambda i: (i, 0))],
        out_shape=[jax.ShapeDtypeStruct((D, 3200), BF16), jax.ShapeDtypeStruct((D, 1536), BF16)],
        compiler_params=_params(("parallel",)),
    )(g)


def _w_in_ungroup(da, db_):
    def body(a_ref, b_ref, o_ref):
        a = a_ref[...]
        full = jnp.concatenate([b_ref[...], a[:, 3072:3080], a[:, 0:768], a[:, 768:1024], a[:, 3136:3168],
                                a[:, 1024:3072]], axis=1)
        for j in range(N_DEV):
            o_ref[j] = full[:, j * IN_SHARD:(j + 1) * IN_SHARD]

    tr = REGROUP_ROWS
    return pl.pallas_call(
        body, name="w_in_ungroup", grid=(D // tr,),
        in_specs=[pl.BlockSpec((tr, 3200), lambda i: (i, 0)), pl.BlockSpec((tr, 1536), lambda i: (i, 0))],
        out_specs=pl.BlockSpec((N_DEV, tr, IN_SHARD), lambda i: (0, i, 0)),
        out_shape=jax.ShapeDtypeStruct((N_DEV, D, IN_SHARD), BF16),
        compiler_params=_params(("parallel",)),
    )(da, db_)


def _prepare_weights(g):
    w = {}
    if "w_in" in g:
        w["w_a"], w["w_b"] = _w_in_regroup(g["w_in"])
    if "w_uq" in g:
        w_uq = g["w_uq"].reshape(Q_LORA, HEADS, 96)
        w["w_uq"] = jnp.pad(w_uq, ((0, 0), (0, 0), (0, 32))).reshape(Q_LORA, HEADS * LANES)
        ukv = g["w_ukv"]
        w["w_k"] = jnp.transpose(jnp.pad(ukv[:, :, :64], ((0, 0), (0, 0), (0, 64))), (1, 0, 2)).reshape(KV_LORA, HEADS * LANES)
        w["w_v"] = jnp.transpose(ukv[:, :, 64:], (1, 0, 2)).reshape(KV_LORA, HEADS * HEAD_DIM)
        w["w_kv"] = jnp.concatenate([w["w_k"], w["w_v"]], axis=1)
    if "w_out" in g:
        w["w_pf"] = _unshard_cols(g["w_proj_fox"])
        w["w_pm"] = _unshard_cols(g["w_proj_mla"])
        w["w_out"] = g["w_out"].reshape(D, D)
    if "w_ffn_in" in g:
        w["w_ffn_in"] = _unshard_ffn_in(g["w_ffn_in"])
        w["w_ffn_out"] = g["w_ffn_out"].reshape(D_FF, D)
    return w


def _shard_grads(dw):
    out = {}
    if "w_a" in dw:
        out["w_in"] = _w_in_ungroup(dw["w_a"], dw["w_b"])
    if "w_uq" in dw:
        w_uq = dw["w_uq"].reshape(Q_LORA, HEADS, LANES)[:, :, :96].reshape(Q_LORA, Q_LORA)
        out["w_uq"] = w_uq.reshape(N_DEV, Q_LORA // N_DEV, Q_LORA)
        k_part = dw["w_k"].reshape(KV_LORA, HEADS, LANES)[:, :, :64]
        v_part = dw["w_v"].reshape(KV_LORA, HEADS, HEAD_DIM)
        out["w_ukv"] = jnp.transpose(jnp.concatenate([k_part, v_part], axis=2), (1, 0, 2))
    if "w_out" in dw:
        out["w_proj_fox"] = _shard_cols(dw["w_pf"])
        out["w_proj_mla"] = _shard_cols(dw["w_pm"])
        out["w_out"] = dw["w_out"].reshape(N_DEV, D // N_DEV, D)
    if "w_ffn_in" in dw:
        out["w_ffn_in"] = _shard_ffn_in(dw["w_ffn_in"])
        out["w_ffn_out"] = dw["w_ffn_out"].reshape(N_DEV, D_FF // N_DEV, D)
    return out


def _fwd_bwd(x, pos, mod, target, w, vec, wts, send):
    shift_mix, scale_mix, gate_mix, shift_ffn, scale_ffn, gate_ffn = [mod[:, i * D:(i + 1) * D] for i in range(6)]
    g_pre_mix, g_post_mix, g_pre_ffn, g_post_ffn = vec["g_pre_mix"], vec["g_post_mix"], vec["g_pre_ffn"], vec["g_post_ffn"]
    g_q, g_kv = vec["g_q_lora"], vec["g_kv_lora"]

    inv_freq = 1.0 / (ROPE_THETA ** (jnp.arange(0, ROPE_DIM, 2, dtype=F32) / ROPE_DIM))
    invf = jnp.concatenate([jnp.zeros((64,), F32), inv_freq, inv_freq, jnp.zeros((32,), F32)]).reshape(1, LANES)
    ct, sa, sb = _rope_tables(pos, invf)

    def pre1(xv, g, sc, sh):
        return ((xv * _rstd(xv) * g) * (1.0 + sc) + sh,), ()
    (h,) = _rowwise(pre1, [(x, D, 0)], [g_pre_mix, scale_mix, shift_mix], [(D, BF16)], [], "pre_mix")
    proj_a = _mm(h, w["w_a"], "nn", F32, "in_proj_a")
    qkv = _mm(h, w["w_b"], "nn", BF16, "in_proj_b")

    def lora_norm(cq, ckv, gq, gkv):
        return (cq * _rstd(cq) * gq, ckv * _rstd(ckv) * gkv), ()
    cqn, ckvn = _rowwise(lora_norm, [(proj_a, Q_LORA, 0), (proj_a, KV_LORA, 3)], [g_q, g_kv],
                         [(Q_LORA, BF16), (KV_LORA, BF16)], [], "lora_norm")
    w = {**w, **wts("lora", cqn)}
    qb = _mm(cqn, w["w_uq"], "nn", F32, "mla_uq")
    kvb = _mm(ckvn, w["w_kv"], "nn", F32, "mla_ukv")

    def mla_rope(qv, kv, vv, misc, c_, a_, b_):
        lane = lax.broadcasted_iota(jnp.int32, (1, LANES), 1)
        kpe = jnp.where((lane >= 64) & (lane < 96), _rope(misc, c_, a_, b_), 0.0)
        qs = [_rope(qv[:, hd * LANES:(hd + 1) * LANES], c_, a_, b_) for hd in range(HEADS)]
        ks = [kv[:, hd * LANES:(hd + 1) * LANES] + kpe for hd in range(HEADS)]
        return (jnp.concatenate(qs, axis=1), jnp.concatenate(ks, axis=1), vv), ()
    q_m, k_m, v_m = _rowwise(
        mla_rope, [(qb, D, 0), (kvb, D, 0), (kvb, 512, 2), (proj_a, LANES, 24), (ct, LANES, 0), (sa, LANES, 0), (sb, LANES, 0)],
        [], [(D, BF16), (D, BF16), (512, BF16)], [], "mla_rope")

    zt = jnp.transpose(proj_a[:, 3072:3080])
    bf = jnp.transpose(vec["b_forget"])
    neg_f = _fox_gates(zt, bf)
    bias = neg_f.reshape(HEADS, N_ATT, 1, ATT_T)
    o_a, lse_a = _attn_fwd(qkv, 0, qkv, 4, qkv, 8, LANES, 1.0 / math.sqrt(HEAD_DIM), bias, "fox_attn")
    o_b, lse_b = _attn_fwd(q_m, 0, k_m, 0, v_m, 0, 2 * LANES, 1.0 / math.sqrt(64 + ROPE_DIM), None, "mla_attn")

    w = {**w, **wts("proj", o_b)}
    pa = _mm(o_a, w["w_pf"], "nn", F32, "proj_fox")

    def merge(pb_, gf, gm, pa_):
        return (_sigmoid(gf) * pa_ + _sigmoid(gm) * pb_, pb_), ()
    merged, pb = _mm_epi(o_b, w["w_pm"], "nn", 512, merge, "proj_mla", 1024,
                         rows=[(proj_a, 512, 2), (proj_a, 512, 4), (pa, 512)], outs=[(D, 512, BF16), (D, 512, F32)])
    def post1(yv, xv, gate, gpost, gpre, sc, sh):
        x1 = xv + gate * (yv * _rstd(yv) * gpost)
        return (x1, (x1 * _rstd(x1) * gpre) * (1.0 + sc) + sh, yv), ()
    x1, h2, y = _mm_epi(merged, w["w_out"], "nn", D, post1, "out_proj", 256, rows=[(x, D)],
                        vecs=[gate_mix, g_post_mix, g_pre_ffn, scale_ffn, shift_ffn],
                        outs=[(D, D, F32), (D, D, BF16), (D, D, F32)])
    w = {**w, **wts("ffn", h2)}

    def swiglu(r):
        g, u = r[:, :FFN_T], r[:, FFN_T:]
        return (g * _sigmoid(g) * u, r), ()
    act, gu = _mm_epi(h2, w["w_ffn_in"], "nn", 2 * FFN_T, swiglu, "ffn_in", 1024,
                      outs=[(D_FF, FFN_T, BF16), (2 * D_FF, 2 * FFN_T, BF16)])

    def head(y2v, x1v, tv, gate, gpost):
        r = _rstd(y2v)
        yn = y2v * r
        n2 = yn * gpost
        err = (x1v + gate * n2) - tv
        dx2 = err * (1.0 / D)
        dn2 = dx2 * gate
        dy2 = _norm_bwd(dn2 * gpost, yn, r)
        return (dx2, dy2), (_colsum(err * err), _colsum(dx2 * n2), _colsum(dn2 * yn))
    dx2, dy2, err_cols, d_gate_ffn, d_g_post_ffn = _mm_epi(
        act, w["w_ffn_out"], "nn", D, head, "ffn_out", 256, rows=[(x1, D), (target, D)], vecs=[gate_ffn, g_post_ffn],
        outs=[(D, D, F32), (D, D, BF16)], sums=[D, D, D])

    def swiglu_bwd(da, guv):
        g, u = guv[:, :FFN_T].astype(F32), guv[:, FFN_T:].astype(F32)
        sg = _sigmoid(g)
        return (jnp.concatenate([da * u * (sg * (1.0 + g * (1.0 - sg))), da * (g * sg)], axis=1),), ()
    (dgu,) = _mm_epi(dy2, w["w_ffn_out"], "nt", FFN_T, swiglu_bwd, "ffn_out_dx", 1024, rows=[(gu, 2 * FFN_T)],
                     outs=[(2 * D_FF, 2 * FFN_T, BF16)])
    dw = {"w_ffn_out": _mm(act, dy2, "tn", BF16, "ffn_out_dw")}
    dw["w_ffn_in"] = _mm(h2, dgu, "tn", BF16, "ffn_in_dw")
    gate_mix = gate_mix + send({n: dw.pop(n) for n in ("w_ffn_in", "w_ffn_out")})[0, 0]

    def mid(dh, x1v, dx2v, yv, gpre, sc, gate, gpost):
        r2 = _rstd(x1v)
        x1n = x1v * r2
        t = dh * x1n
        dx1 = dx2v + _norm_bwd(dh * (gpre * (1.0 + sc)), x1n, r2)
        ry = _rstd(yv)
        yn = yv * ry
        dn1 = dx1 * gate
        dy = _norm_bwd(dn1 * gpost, yn, ry)
        sums = (_colsum(dh), _colsum(t) * gpre, _colsum(t) * (1.0 + sc), _colsum(dx1 * (yn * gpost)), _colsum(dn1 * yn))
        return (dx1, dy), sums
    dx1, dy, d_shift_ffn, d_scale_ffn, d_g_pre_ffn, d_gate_mix, d_g_post_mix = _mm_epi(
        dgu, w["w_ffn_in"], "nt", D, mid, "ffn_in_dx", 256, rows=[(x1, D), (dx2, D), (y, D)],
        vecs=[g_pre_ffn, scale_ffn, gate_mix, g_post_mix], outs=[(D, D, F32), (D, D, BF16)], sums=[D] * 5)

    dw["w_out"] = _mm(merged, dy, "tn", BF16, "out_proj_dw")

    def merge_bwd(dm, gf, gm, pa_, pb_):
        sf, sm = _sigmoid(gf), _sigmoid(gm)
        return (dm * sf, dm * sm, dm * pa_ * (sf * (1.0 - sf)), dm * pb_ * (sm * (1.0 - sm))), ()
    dpa, dpb, dgf, dgm = _mm_epi(dy, w["w_out"], "nt", 512, merge_bwd, "out_proj_dx", 1024,
                                 rows=[(proj_a, 512, 2), (proj_a, 512, 4), (pa, 512), (pb, 512)],
                                 outs=[(D, 512, BF16)] * 4)
    do_a = _mm(dpa, w["w_pf"], "nt", BF16, "proj_fox_dx")
    do_b = _mm(dpb, w["w_pm"], "nt", BF16, "proj_mla_dx")
    dw["w_pf"] = _mm(o_a, dpa, "tn", BF16, "proj_fox_dw")
    dw["w_pm"] = _mm(o_b, dpb, "tn", BF16, "proj_mla_dw")
    bias = bias + send({n: dw.pop(n) for n in ("w_out", "w_pf", "w_pm")})[0, 0]

    sc_a, sc_b = 1.0 / math.sqrt(HEAD_DIM), 1.0 / math.sqrt(64 + ROPE_DIM)
    dq_a, dk_a, dv_a, dbias = _attn_grad(qkv, 0, qkv, 4, qkv, 8, do_a, lse_a, LANES, sc_a, bias, BF16, "fox_attn_bwd")
    dq_m, dk_m, dv_m = _attn_grad(q_m, 0, k_m, 0, v_m, 0, do_b, lse_b, 2 * LANES, sc_b, None, F32, "mla_attn_bwd")

    def mla_rope_bwd(dq, dk, c_, a_, b_):
        lane = lax.broadcasted_iota(jnp.int32, (1, LANES), 1)
        dqs = [_rope_t(dq[:, hd * LANES:(hd + 1) * LANES], c_, a_, b_) for hd in range(HEADS)]
        dkpe = dk[:, 0:LANES]
        for hd in range(1, HEADS):
            dkpe = dkpe + dk[:, hd * LANES:(hd + 1) * LANES]
        dkpe = jnp.where((lane >= 64) & (lane < 96), dkpe, 0.0)
        dkr = jnp.where((lane >= 64) & (lane < 96), _rope_t(dkpe, c_, a_, b_), 0.0)
        return (jnp.concatenate(dqs, axis=1), dk, dkr), ()
    dqb, dkb, dkr = _rowwise(mla_rope_bwd, [(dq_m, D, 0), (dk_m, D, 0), (ct, LANES, 0), (sa, LANES, 0), (sb, LANES, 0)],
                             [], [(D, BF16), (D, BF16), (LANES, F32)], [], "mla_rope_bwd")
    def lora_q_bwd(dq, cq, gq):
        rq = _rstd(cq)
        cqh = cq * rq
        return (_norm_bwd(dq * gq, cqh, rq),), (_colsum(dq * cqh),)
    dcq, d_g_q = _mm_epi(dqb, w["w_uq"], "nt", Q_LORA, lora_q_bwd, "mla_uq_dx", 512, rows=[(proj_a, Q_LORA, 0)],
                         vecs=[g_q], outs=[(Q_LORA, Q_LORA, BF16)], sums=[Q_LORA])
    dw["w_uq"] = _mm(cqn, dqb, "tn", BF16, "mla_uq_dw")

    def lora_kv_bwd(dv_part, dk_part, ckv, gkv):
        dkv = dv_part + dk_part
        rk = _rstd(ckv)
        ckh = ckv * rk
        return (_norm_bwd(dkv * gkv, ckh, rk),), (_colsum(dkv * ckh),)
    dckv, d_g_kv = _mm_epi(dv_m, w["w_v"], "nt", KV_LORA, lora_kv_bwd, "mla_uv_dx", 1024,
                           rows=[(_mm(dkb, w["w_k"], "nt", F32, "mla_uk_dx"), KV_LORA), (proj_a, KV_LORA, 3)],
                           vecs=[g_kv], outs=[(KV_LORA, KV_LORA, BF16)], sums=[KV_LORA])
    dw["w_k"] = _mm(ckvn, dkb, "tn", BF16, "mla_uk_dw")
    dw["w_v"] = _mm(ckvn, dv_m, "tn", BF16, "mla_uv_dw")

    dzt, d_bf = _fox_gates_bwd(dbias.reshape(HEADS, S), zt, bf)
    dmisc = (dkr + jnp.pad(jnp.transpose(dzt), ((0, 0), (0, LANES - HEADS)))).astype(BF16)
    dproj_a = jnp.concatenate([dcq, dckv, dgf, dgm, dmisc], axis=1)
    dqkv = jnp.concatenate([dq_a, dk_a, dv_a], axis=1)
    dw["w_a"] = _mm(h, dproj_a, "tn", BF16, "in_proj_a_dw")
    dw["w_b"] = _mm(h, dqkv, "tn", BF16, "in_proj_b_dw")
    tok = send(dw)

    def first(dh_b, dh_a, xv, dx1v, gpre, sc):
        dhv = dh_b + dh_a
        r = _rstd(xv)
        xn = xv * r
        t = dhv * xn
        dx = dx1v + _norm_bwd(dhv * (gpre * (1.0 + sc)), xn, r)
        return (dx,), (_colsum(dhv), _colsum(t) * gpre, _colsum(t) * (1.0 + sc))
    grad_x, d_shift_mix, d_scale_mix, d_g_pre_mix = _mm_epi(
        dqkv, w["w_b"], "nt", D, first, "in_proj_b_dx", 256,
        rows=[(_mm(dproj_a, w["w_a"], "nt", F32, "in_proj_a_dx", dep=tok), D), (x, D), (dx1, D)],
        vecs=[g_pre_mix, scale_mix], outs=[(D, D, F32)], sums=[D] * 3)

    dmod = jnp.concatenate([d_shift_mix, d_scale_mix, d_gate_mix, d_shift_ffn, d_scale_ffn, d_gate_ffn], axis=1)
    small = dict(dmod=dmod, g_pre_mix=d_g_pre_mix, g_post_mix=d_g_post_mix, g_pre_ffn=d_g_pre_ffn,
                 g_post_ffn=d_g_post_ffn, g_q_lora=d_g_q, g_kv_lora=d_g_kv,
                 b_forget=jnp.pad(jnp.transpose(d_bf), ((0, 0), (0, LANES - HEADS))), err=err_cols)
    return grad_x, small


SMALL_ORDER = ("dmod", "g_pre_mix", "g_post_mix", "g_pre_ffn", "g_post_ffn", "g_q_lora", "g_kv_lora", "b_forget", "err")
SMALL_PARAM = {"dmod": "b_ada"}
MATRICES = ("w_in", "w_uq", "w_ukv", "w_proj_fox", "w_proj_mla", "w_out", "w_ffn_in", "w_ffn_out")
WEIGHTS = ("w_ada", "b_ada", "g_pre_mix", "g_post_mix", "g_pre_ffn", "g_post_ffn", "w_in", "b_forget", "g_q_lora",
           "w_uq", "g_kv_lora", "w_ukv", "w_proj_fox", "w_proj_mla", "w_out", "w_ffn_in", "w_ffn_out")


def kernel(x, c, positions, w_ada, b_ada, g_pre_mix, g_post_mix, g_pre_ffn, g_post_ffn, w_in, b_forget, g_q_lora, w_uq, g_kv_lora, w_ukv, w_proj_fox, w_proj_mla, w_out, w_ffn_in, w_ffn_out, loss_target, m_w_ada, m_b_ada, m_g_pre_mix, m_g_post_mix, m_g_pre_ffn, m_g_post_ffn, m_w_in, m_b_forget, m_g_q_lora, m_w_uq, m_g_kv_lora, m_w_ukv, m_w_proj_fox, m_w_proj_mla, m_w_out, m_w_ffn_in, m_w_ffn_out, v_w_ada, v_b_ada, v_g_pre_mix, v_g_post_mix, v_g_pre_ffn, v_g_post_ffn, v_w_in, v_b_forget, v_g_q_lora, v_w_uq, v_g_kv_lora, v_w_ukv, v_w_proj_fox, v_w_proj_mla, v_w_out, v_w_ffn_in, v_w_ffn_out):
    prm = dict(w_ada=w_ada, b_ada=b_ada, g_pre_mix=g_pre_mix, g_post_mix=g_post_mix, g_pre_ffn=g_pre_ffn,
               g_post_ffn=g_post_ffn, w_in=w_in, b_forget=b_forget, g_q_lora=g_q_lora, w_uq=w_uq, g_kv_lora=g_kv_lora,
               w_ukv=w_ukv, w_proj_fox=w_proj_fox, w_proj_mla=w_proj_mla, w_out=w_out, w_ffn_in=w_ffn_in, w_ffn_out=w_ffn_out)
    mom = dict(w_ada=m_w_ada, b_ada=m_b_ada, g_pre_mix=m_g_pre_mix, g_post_mix=m_g_post_mix, g_pre_ffn=m_g_pre_ffn,
               g_post_ffn=m_g_post_ffn, w_in=m_w_in, b_forget=m_b_forget, g_q_lora=m_g_q_lora, w_uq=m_w_uq,
               g_kv_lora=m_g_kv_lora, w_ukv=m_w_ukv, w_proj_fox=m_w_proj_fox, w_proj_mla=m_w_proj_mla, w_out=m_w_out,
               w_ffn_in=m_w_ffn_in, w_ffn_out=m_w_ffn_out)
    var = dict(w_ada=v_w_ada, b_ada=v_b_ada, g_pre_mix=v_g_pre_mix, g_post_mix=v_g_post_mix, g_pre_ffn=v_g_pre_ffn,
               g_post_ffn=v_g_post_ffn, w_in=v_w_in, b_forget=v_b_forget, g_q_lora=v_g_q_lora, w_uq=v_w_uq,
               g_kv_lora=v_g_kv_lora, w_ukv=v_w_ukv, w_proj_fox=v_w_proj_fox, w_proj_mla=v_w_proj_mla, w_out=v_w_out,
               w_ffn_in=v_w_ffn_in, w_ffn_out=v_w_ffn_out)
    me = _flat(*_coords())

    own = {n: prm[n][0].astype(BF16) for n in MATRICES}
    w_in_all, c_all = _all_gather([own["w_in"], c], "gather_in")
    w = _prepare_weights({"w_in": w_in_all})
    c_all = c_all.reshape(N_DEV, D)
    later = dict(lora=("w_uq", "w_ukv"), proj=("w_proj_fox", "w_proj_mla", "w_out"), ffn=("w_ffn_in", "w_ffn_out"))
    states, tok = _async_start([[own[n] for n in names] for names in later.values()], "gather", w_in_all, "gather_rest_start")
    gather_state = dict(zip(later, states))

    def wts(group, after):
        srcs, lands = _async_wait(gather_state[group], after, "gather_" + group + "_wait")
        return _prepare_weights({n: _with_own(land, src, me) for n, src, land in zip(later[group], srcs, lands)})

    sent = []

    def send(grads):
        shards = _shard_grads(grads)
        names = list(shards)
        (state,), t = _async_start([[shards[n] for n in names]], "exchange", jnp.zeros((8, LANES), F32),
                                   "exchange_" + names[0] + "_start")
        sent.append((names, state))
        return t

    ada_cols = w_ada.shape[2]
    b_cols = lax.dynamic_slice(b_ada, (0, me * ada_cols), (1, ada_cols))
    mod_cols, silu_c = _mod_part(c_all, w_ada[0], b_cols)
    (mod_all,) = _all_gather([mod_cols], "gather_mod")
    mod = lax.dynamic_index_in_dim(mod_all, me, axis=1, keepdims=False).reshape(1, 6 * D) + tok[0, 0]

    vec = dict(g_pre_mix=g_pre_mix, g_post_mix=g_post_mix, g_pre_ffn=g_pre_ffn, g_post_ffn=g_post_ffn,
               g_q_lora=g_q_lora, g_kv_lora=g_kv_lora, b_forget=b_forget)
    pos = positions.astype(F32).reshape(S, 1)
    grad_x, small = _fwd_bwd(x[0], pos, mod, loss_target[0], w, vec, wts, send)

    bundle = jnp.concatenate([small[n] for n in SMALL_ORDER], axis=1)
    (small_state,), tok = _async_start([[bundle]], "gather", jnp.zeros((8, LANES), F32), "gather_small_start")

    out = {}
    after = tok
    for names, state in sent:
        srcs, lands = _async_wait(state, after, "exchange_" + names[0] + "_wait")
        for n, src, land in zip(names, srcs, lands):
            parts = _with_own(land, lax.dynamic_index_in_dim(src, me, 0, keepdims=False), me)
            out[n] = _adamw(prm[n][0], mom[n][0], var[n][0], parts, "adamw_" + n)
            after = out[n][0]

    (own_bundle,), (bundle_all,) = _async_wait(small_state, after, "gather_small_wait")
    bundle_all = _with_own(bundle_all, own_bundle, me)
    dmod_all = bundle_all[:, 0, :6 * D]
    dm_cols = lax.dynamic_slice(dmod_all, (0, me * ada_cols), (N_DEV, ada_cols))
    g_ada = _w_ada_grad(jnp.transpose(silu_c), dm_cols)
    out["w_ada"] = _adamw(w_ada[0], m_w_ada[0], v_w_ada[0], g_ada[None], "adamw_w_ada")

    offsets, off = {}, 0
    for n in SMALL_ORDER:
        offsets[n] = off
        off += small[n].shape[1]
    names = [SMALL_PARAM.get(n, n) for n in SMALL_ORDER if n != "err"]
    results, err = _adamw_rows(bundle_all, [offsets[n] for n in SMALL_ORDER if n != "err"],
                               [prm[n] for n in names], [mom[n] for n in names], [var[n] for n in names],
                               offsets["err"], D)
    out.update(zip(names, results))
    loss = 0.5 * jnp.sum(err) / D

    res = [loss, grad_x[None]]
    for kind in range(4):
        for n in WEIGHTS:
            t = out[n][kind]
            res.append(t[None] if prm[n].ndim == 3 else t)
    return tuple(res)
```

```python
import functools
import math

import jax
import jax.numpy as jnp
from jax import lax
from jax.experimental import pallas as pl
from jax.experimental.pallas import tpu as pltpu

F32 = jnp.float32
BF16 = jnp.bfloat16

N_DEV = 8
S = 2048
D = 1024
D_FF = 2816
HEADS = 8
HEAD_DIM = 64
Q_LORA = 768
KV_LORA = 256
ROPE_DIM = 32
ROPE_THETA = 10000.0
NORM_EPS = 1e-6
LANES = 128
VMEM_LIMIT = 56 * 1024 * 1024

ADAM_LR = 0.001
ADAM_B1 = 0.9
ADAM_B2 = 0.999
ADAM_EPS = 1e-08
ADAM_WD = 0.01
ADAM_STEP = 10

ATT_T = 256
N_ATT = S // ATT_T

NN = (((1,), (0,)), ((), ()))
NT = (((1,), (1,)), ((), ()))
TN = (((0,), (0,)), ((), ()))
MESH = pl.DeviceIdType.MESH


def _params(sem=None):
    return pltpu.CompilerParams(dimension_semantics=sem, vmem_limit_bytes=VMEM_LIMIT)


def _pick(n, cap):
    best = None
    for t in range(LANES, cap + 1, LANES):
        if n % t == 0:
            best = t
    return best if best is not None else n


def _mm(a, b, mode, out_dtype, name, acc=None, dep=None):
    if mode == "nn":
        (m, k), (k2, n), dn = a.shape, b.shape, NN
    elif mode == "nt":
        (m, k), (n, k2), dn = a.shape, b.shape, NT
    else:
        (k, m), (k2, n), dn = a.shape, b.shape, TN
    assert k == k2, (a.shape, b.shape, mode)
    tn = _pick(n, 640)
    tm = _pick(m, 1536)
    osz = jnp.dtype(out_dtype).itemsize

    def need(tm_):
        blk = tm_ * k * 2 + tn * k * 2 + tm_ * tn * osz + (tm_ * tn * 4 if acc is not None else 0)
        return 2 * blk + tm_ * tn * 4
    while need(tm) > 36 * 1024 * 1024 and tm % 256 == 0:
        tm //= 2

    def body(*refs):
        a_ref, b_ref, o_ref = refs[0], refs[1], refs[-1]
        r = lax.dot_general(a_ref[...], b_ref[...], dn, preferred_element_type=F32)
        if acc is not None:
            r = r + refs[2][...]
        o_ref[...] = r.astype(o_ref.dtype)

    if mode == "tn":
        a_spec = pl.BlockSpec((k, tm), lambda i, j: (0, i))
    else:
        a_spec = pl.BlockSpec((tm, k), lambda i, j: (i, 0))
    if mode == "nt":
        b_spec = pl.BlockSpec((tn, k), lambda i, j: (j, 0))
    else:
        b_spec = pl.BlockSpec((k, tn), lambda i, j: (0, j))
    o_spec = pl.BlockSpec((tm, tn), lambda i, j: (i, j))
    in_specs = [a_spec, b_spec] + ([o_spec] if acc is not None else [])
    in_specs += [pl.BlockSpec(memory_space=pl.ANY)] if dep is not None else []
    args = (a, b) + ((acc,) if acc is not None else ()) + ((dep,) if dep is not None else ())
    return pl.pallas_call(
        body, name=name, grid=(m // tm, n // tn),
        in_specs=in_specs, out_specs=o_spec,
        out_shape=jax.ShapeDtypeStruct((m, n), out_dtype),
        compiler_params=_params(("parallel", "parallel")),
    )(*args)


def _mm_epi(a, b, mode, tnb, epi, name, tm, rows=(), vecs=(), outs=(), sums=()):
    m, k = a.shape
    nb = b.shape[1] if mode == "nn" else b.shape[0]
    dn = NN if mode == "nn" else NT
    n_in = 2 + len(rows) + len(vecs)

    def body(*refs):
        r = lax.dot_general(refs[0][...], refs[1][...], dn, preferred_element_type=F32)
        o_vals, s_vals = epi(r, *[x[...] for x in refs[2:n_in]])
        o_refs = refs[n_in:n_in + len(outs)]
        s_refs = refs[n_in + len(outs):]
        assert len(o_vals) == len(o_refs) and len(s_vals) == len(s_refs)
        for o_ref, val in zip(o_refs, o_vals):
            o_ref[...] = val.astype(o_ref.dtype)
        if sums:
            @pl.when((pl.program_id(0) == 0) & (pl.program_id(1) == 0))
            def _():
                for s_ref in s_refs:
                    s_ref[...] = jnp.zeros(s_ref.shape, F32)
            for s_ref, val in zip(s_refs, s_vals):
                s_ref[...] += val

    b_spec = pl.BlockSpec((k, tnb), lambda i, j: (0, j)) if mode == "nn" else pl.BlockSpec((tnb, k), lambda i, j: (j, 0))
    in_specs = [pl.BlockSpec((tm, k), lambda i, j: (i, 0)), b_spec]
    rows = [tuple(r) + (0,) * (3 - len(r)) for r in rows]
    in_specs += [pl.BlockSpec((tm, w), functools.partial(lambda i, j, off: (i, j + off), off=off)) for _, w, off in rows]
    in_specs += [pl.BlockSpec(v.shape, lambda i, j: (0, 0)) for v in vecs]
    out_specs = [pl.BlockSpec((tm, w), lambda i, j: (i, j)) for _, w, _ in outs]
    out_specs += [pl.BlockSpec((1, w), lambda i, j: (0, 0)) for w in sums]
    out_shape = [jax.ShapeDtypeStruct((m, full), dt) for full, _, dt in outs]
    out_shape += [jax.ShapeDtypeStruct((1, w), F32) for w in sums]
    return pl.pallas_call(
        body, name=name, grid=(m // tm, nb // tnb),
        in_specs=in_specs, out_specs=out_specs, out_shape=out_shape,
        compiler_params=_params(("arbitrary", "arbitrary") if sums else ("parallel", "parallel")),
    )(a, b, *[r[0] for r in rows], *vecs)


def _rowwise(fn, row_ins, vec_ins, row_outs, sum_outs, name, tm=256):
    n_in = len(row_ins) + len(vec_ins)
    n_o = len(row_outs)
    rows = row_ins[0][0].shape[0]

    def body(*refs):
        vals = [r[...] for r in refs[:n_in]]
        outs = refs[n_in:]
        ro, so = fn(*vals)
        assert len(ro) == n_o and len(so) == len(sum_outs)
        for r, v in zip(outs[:n_o], ro):
            r[...] = v.astype(r.dtype)
        if sum_outs:
            @pl.when(pl.program_id(0) == 0)
            def _():
                for r in outs[n_o:]:
                    r[...] = jnp.zeros(r.shape, F32)
            for r, v in zip(outs[n_o:], so):
                r[...] += v

    in_specs = [pl.BlockSpec((tm, w), functools.partial(lambda i, b: (i, b), b=b)) for _, w, b in row_ins]
    in_specs += [pl.BlockSpec(v.shape, lambda i: (0, 0)) for v in vec_ins]
    out_specs = [pl.BlockSpec((tm, w), lambda i: (i, 0)) for w, _ in row_outs]
    out_specs += [pl.BlockSpec((1, w), lambda i: (0, 0)) for w in sum_outs]
    out_shape = [jax.ShapeDtypeStruct((rows, w), dt) for w, dt in row_outs]
    out_shape += [jax.ShapeDtypeStruct((1, w), F32) for w in sum_outs]
    return pl.pallas_call(
        body, name=name, grid=(rows // tm,),
        in_specs=in_specs, out_specs=out_specs, out_shape=out_shape,
        compiler_params=_params(("arbitrary",)),
    )(*[a for a, _, _ in row_ins], *vec_ins)


def _sigmoid(x):
    return 1.0 / (1.0 + jnp.exp(-x))


def _rstd(x):
    return lax.rsqrt(jnp.mean(x * x, axis=-1, keepdims=True) + NORM_EPS)


def _norm_bwd(dyn, xn, r):
    return r * (dyn - xn * jnp.mean(dyn * xn, axis=-1, keepdims=True))


def _colsum(x):
    return jnp.sum(x, axis=0, keepdims=True)


def _rope_tables(pos, invf):
    def fn(p, f):
        lane = lax.broadcasted_iota(jnp.int32, (1, LANES), 1)
        ang = p * f
        cs, sn = jnp.cos(ang), jnp.sin(ang)
        rot = (lane >= 64) & (lane < 96)
        ct = jnp.where(lane < 64, 1.0, jnp.where(rot, cs, 0.0))
        sa = jnp.where((lane >= 64) & (lane < 80), -sn, 0.0)
        sb = jnp.where((lane >= 80) & (lane < 96), sn, 0.0)
        return (ct, sa, sb), ()
    return _rowwise(fn, [(pos, 1, 0)], [invf], [(LANES, F32)] * 3, [], "rope_tables")


def _rope(x, ct, sa, sb):
    return x * ct + pltpu.roll(x, LANES - 16, 1) * sa + pltpu.roll(x, 16, 1) * sb


def _rope_t(x, ct, sa, sb):
    return x * ct - pltpu.roll(x, LANES - 16, 1) * sa - pltpu.roll(x, 16, 1) * sb


def _head_mask(width, hh):
    lane = lax.broadcasted_iota(jnp.int32, (1, width), 1)
    half = width // 2
    return (lane >= hh * half) & (lane < (hh + 1) * half)


ATT_PP = 2
ATT_CHAINS = [(a, hh) for a in range(ATT_PP) for hh in range(2)]
ATT_G = HEADS // (2 * ATT_PP)


def _pair(ref_or_val, a, width, rows=slice(None)):
    return ref_or_val[rows, a * width:(a + 1) * width]


def _attn_fwd(q, qo, k, ko, v, vo, dkp, scale, bias, name):
    T = ATT_T
    assert qo % ATT_PP == 0 and ko % ATT_PP == 0 and vo % ATT_PP == 0
    qo, ko, vo = qo // ATT_PP, ko // ATT_PP, vo // ATT_PP

    def body(*refs):
        if bias is not None:
            q_ref, k_ref, v_ref, b_ref, o_ref, lse_ref, s_scr = refs
        else:
            q_ref, k_ref, v_ref, o_ref, lse_ref, s_scr = refs
        i = pl.program_id(1)
        row = lax.broadcasted_iota(jnp.int32, (T, T), 0)
        col = lax.broadcasted_iota(jnp.int32, (T, T), 1)
        qms = []
        for a, hh in ATT_CHAINS:
            qb = _pair(q_ref, a, dkp)
            qms.append(jnp.where(_head_mask(dkp, hh), qb, jnp.zeros_like(qb)))

        def fold(t):
            return [t[:, c * LANES:(c + 1) * LANES] for c in range(T // LANES)]

        def run(nt):
            mls = [jnp.full((T, LANES), -jnp.inf, F32) for _ in ATT_CHAINS]
            for j in range(nt):
                ks = slice(j * T, (j + 1) * T)
                for ci, (a, hh) in enumerate(ATT_CHAINS):
                    s = lax.dot_general(qms[ci], _pair(k_ref, a, dkp, ks), NT, preferred_element_type=F32) * scale
                    if bias is not None:
                        s = s + b_ref[2 * a + hh, j]
                    if j == nt - 1:
                        s = jnp.where(row >= col, s, -jnp.inf)
                    s_scr[ci, j] = s
                    for part in fold(s):
                        mls[ci] = jnp.maximum(mls[ci], part)
            ms = [jnp.max(ml, axis=1, keepdims=True) for ml in mls]
            mbs = [jnp.broadcast_to(m, (T, LANES)) for m in ms]
            for a in range(ATT_PP):
                ls = [jnp.zeros((T, LANES), F32) for _ in range(2)]
                ps, vms = [], []
                for j in range(nt):
                    vb = _pair(v_ref, a, LANES, slice(j * T, (j + 1) * T))
                    for hh in range(2):
                        parts = [jnp.exp(part - mbs[2 * a + hh]) for part in fold(s_scr[2 * a + hh, j])]
                        for part in parts:
                            ls[hh] = ls[hh] + part
                        ps.append(jnp.concatenate(parts, axis=1).astype(BF16))
                        vms.append(jnp.where(_head_mask(LANES, hh), vb, jnp.zeros_like(vb)))
                acc = lax.dot_general(jnp.concatenate(ps, axis=1), jnp.concatenate(vms, axis=0), NN,
                                      preferred_element_type=F32)
                l0, l1 = [jnp.sum(l, axis=1, keepdims=True) for l in ls]
                lse_ref[2 * a] = ms[2 * a] + jnp.log(l0)
                lse_ref[2 * a + 1] = ms[2 * a + 1] + jnp.log(l1)
                inv = jnp.where(_head_mask(LANES, 0), 1.0 / l0, 1.0 / l1)
                o_ref[:, a * LANES:(a + 1) * LANES] = (acc * inv).astype(o_ref.dtype)

        for nt in range(1, N_ATT + 1):
            pl.when(i == nt - 1)(functools.partial(run, nt))

    in_specs = [
        pl.BlockSpec((T, ATT_PP * dkp), lambda g, i: (i, qo + g)),
        pl.BlockSpec((S, ATT_PP * dkp), lambda g, i: (0, ko + g)),
        pl.BlockSpec((S, ATT_PP * LANES), lambda g, i: (0, vo + g)),
    ]
    args = [q, k, v]
    if bias is not None:
        in_specs.append(pl.BlockSpec((2 * ATT_PP, N_ATT, 1, T), lambda g, i: (g, 0, 0, 0)))
        args.append(bias)
    return pl.pallas_call(
        body, name=name, grid=(ATT_G, N_ATT),
        in_specs=in_specs,
        out_specs=[pl.BlockSpec((T, ATT_PP * LANES), lambda g, i: (i, g)),
                   pl.BlockSpec((2 * ATT_PP, T, 1), lambda g, i: (g, i, 0))],
        out_shape=[jax.ShapeDtypeStruct((S, HEADS * HEAD_DIM), BF16),
                   jax.ShapeDtypeStruct((HEADS, S, 1), F32)],
        scratch_shapes=[pltpu.VMEM((len(ATT_CHAINS), N_ATT, T, T), F32)],
        compiler_params=_params(("parallel", "arbitrary")),
    )(*args)


def _attn_grad(q, qo, k, ko, v, vo, do, lse, dkp, scale, bias, qk_dtype, name):
    T = ATT_T
    has_b = bias is not None
    qo, ko, vo = qo // ATT_PP, ko // ATT_PP, vo // ATT_PP
    n_ch = len(ATT_CHAINS)

    def body(*refs):
        q_ref, k_ref, v_ref, do_ref, lse_ref = refs[:5]
        refs = refs[5:]
        if has_b:
            b_ref, refs = refs[0], refs[1:]
        dq_ref, dk_ref, dv_ref = refs[:3]
        refs = refs[3:]
        if has_b:
            db_ref, refs = refs[0], refs[1:]
        p_scr, dp_scr, dk_acc, dv_acc = refs[:4]
        db_acc = refs[4] if has_b else None
        i = pl.program_id(1)

        @pl.when(i == 0)
        def _():
            dk_acc[...] = jnp.zeros(dk_acc.shape, F32)
            dv_acc[...] = jnp.zeros(dv_acc.shape, F32)
            if has_b:
                db_acc[...] = jnp.zeros(db_acc.shape, F32)

        row = lax.broadcasted_iota(jnp.int32, (T, T), 0)
        col = lax.broadcasted_iota(jnp.int32, (T, T), 1)

        def fold(t):
            return [t[:, c * LANES:(c + 1) * LANES] for c in range(T // LANES)]

        qms, doms, lses = [], [], []
        for a, hh in ATT_CHAINS:
            qb, dob = _pair(q_ref, a, dkp), _pair(do_ref, a, LANES)
            qms.append(jnp.where(_head_mask(dkp, hh), qb, jnp.zeros_like(qb)))
            doms.append(jnp.where(_head_mask(LANES, hh), dob, jnp.zeros_like(dob)))
            lses.append(lse_ref[2 * a + hh])

        def run(nt):
            dls = [jnp.zeros((T, LANES), F32) for _ in ATT_CHAINS]
            for j in range(nt):
                ks = slice(j * T, (j + 1) * T)
                for ci, (a, hh) in enumerate(ATT_CHAINS):
                    s = lax.dot_general(qms[ci], _pair(k_ref, a, dkp, ks), NT, preferred_element_type=F32) * scale
                    if has_b:
                        s = s + b_ref[ci, j]
                    s = s - lses[ci]
                    if j == nt - 1:
                        s = jnp.where(row >= col, s, -jnp.inf)
                    p = jnp.exp(s)
                    dp = lax.dot_general(doms[ci], _pair(v_ref, a, LANES, ks), NT, preferred_element_type=F32)
                    p_scr[ci, j] = p
                    dp_scr[ci, j] = dp
                    for part in fold(p * dp):
                        dls[ci] = dls[ci] + part
            deltas = [jnp.broadcast_to(jnp.sum(dl, axis=1, keepdims=True), (T, LANES)) for dl in dls]
            for a in range(ATT_PP):
                ds_all, km_all = [], []
                qm2t = jnp.transpose(jnp.concatenate([qms[2 * a], qms[2 * a + 1]], axis=0))
                dom2t = jnp.transpose(jnp.concatenate([doms[2 * a], doms[2 * a + 1]], axis=0))
                for j in range(nt):
                    ks = slice(j * T, (j + 1) * T)
                    kb = _pair(k_ref, a, dkp, ks)
                    p2, ds2 = [], []
                    for hh in range(2):
                        ci = 2 * a + hh
                        p = p_scr[ci, j]
                        ds = jnp.concatenate([pp * (dd - deltas[ci]) for pp, dd in zip(fold(p), fold(dp_scr[ci, j]))], axis=1)
                        if has_b:
                            db_acc[ci, j] += jnp.sum(ds, axis=0, keepdims=True)
                        p2.append(p.astype(BF16))
                        ds2.append((ds * scale).astype(BF16))
                        km_all.append(jnp.where(_head_mask(dkp, hh), kb, jnp.zeros_like(kb)))
                    dv_acc[a * LANES:(a + 1) * LANES, ks] += lax.dot_general(
                        dom2t, jnp.concatenate(p2, axis=0), NN, preferred_element_type=F32)
                    dk_acc[a * dkp:(a + 1) * dkp, ks] += lax.dot_general(
                        qm2t, jnp.concatenate(ds2, axis=0), NN, preferred_element_type=F32)
                    ds_all += ds2
                dq = lax.dot_general(jnp.concatenate(ds_all, axis=1), jnp.concatenate(km_all, axis=0), NN,
                                     preferred_element_type=F32)
                dq_ref[:, a * dkp:(a + 1) * dkp] = dq.astype(dq_ref.dtype)

        for nt in range(1, N_ATT + 1):
            pl.when(i == nt - 1)(functools.partial(run, nt))

        @pl.when(i == N_ATT - 1)
        def _():
            dk_ref[...] = jnp.transpose(dk_acc[...]).astype(dk_ref.dtype)
            dv_ref[...] = jnp.transpose(dv_acc[...]).astype(dv_ref.dtype)
            if has_b:
                db_ref[...] = db_acc[...]

    in_specs = [
        pl.BlockSpec((T, ATT_PP * dkp), lambda g, i: (i, qo + g)),
        pl.BlockSpec((S, ATT_PP * dkp), lambda g, i: (0, ko + g)),
        pl.BlockSpec((S, ATT_PP * LANES), lambda g, i: (0, vo + g)),
        pl.BlockSpec((T, ATT_PP * LANES), lambda g, i: (i, g)),
        pl.BlockSpec((2 * ATT_PP, T, 1), lambda g, i: (g, i, 0)),
    ]
    args = [q, k, v, do, lse]
    out_specs = [
        pl.BlockSpec((T, ATT_PP * dkp), lambda g, i: (i, g)),
        pl.BlockSpec((S, ATT_PP * dkp), lambda g, i: (0, g)),
        pl.BlockSpec((S, ATT_PP * LANES), lambda g, i: (0, g)),
    ]
    width = (HEADS // 2) * dkp
    out_shape = [
        jax.ShapeDtypeStruct((S, width), qk_dtype),
        jax.ShapeDtypeStruct((S, width), qk_dtype),
        jax.ShapeDtypeStruct((S, HEADS * HEAD_DIM), BF16),
    ]
    scratch = [pltpu.VMEM((n_ch, N_ATT, T, T), F32), pltpu.VMEM((n_ch, N_ATT, T, T), F32),
               pltpu.VMEM((ATT_PP * dkp, S), F32), pltpu.VMEM((ATT_PP * LANES, S), F32)]
    if has_b:
        bspec = pl.BlockSpec((2 * ATT_PP, N_ATT, 1, T), lambda g, i: (g, 0, 0, 0))
        in_specs.append(bspec)
        args.append(bias)
        out_specs.append(bspec)
        out_shape.append(jax.ShapeDtypeStruct((HEADS, N_ATT, 1, T), F32))
        scratch.append(pltpu.VMEM((2 * ATT_PP, N_ATT, 1, T), F32))
    return pl.pallas_call(
        body, name=name, grid=(ATT_G, N_ATT),
        in_specs=in_specs, out_specs=out_specs, out_shape=out_shape, scratch_shapes=scratch,
        compiler_params=_params(("parallel", "arbitrary")),
    )(*args)


def _tri(upper):
    a = lax.broadcasted_iota(jnp.int32, (LANES, LANES), 0)
    b = lax.broadcasted_iota(jnp.int32, (LANES, LANES), 1)
    return jnp.where(a <= b if upper else a >= b, 1.0, 0.0).astype(F32)


def _fox_gates(zt, bf):
    def body(z_ref, b_ref, o_ref):
        tri = _tri(True)
        carry = jnp.zeros((HEADS, 1), F32)
        for t in range(S // LANES):
            sl = slice(t * LANES, (t + 1) * LANES)
            z = z_ref[:, sl] + b_ref[...]
            logf = jnp.minimum(z, 0.0) - jnp.log(1.0 + jnp.exp(-jnp.abs(z)))
            c = lax.dot_general(logf, tri, NN, preferred_element_type=F32,
                                precision=lax.Precision.HIGHEST) + carry
            o_ref[:, sl] = -c
            carry = c[:, LANES - 1:LANES]

    return pl.pallas_call(
        body, name="fox_gates", out_shape=jax.ShapeDtypeStruct((HEADS, S), F32),
        compiler_params=_params(),
    )(zt, bf)


def _fox_gates_bwd(dbias, zt, bf):
    def body(d_ref, z_ref, b_ref, dz_ref, dbf_ref):
        tri = _tri(False)
        carry = jnp.zeros((HEADS, 1), F32)
        tot = jnp.zeros((HEADS, 1), F32)
        for t in reversed(range(S // LANES)):
            sl = slice(t * LANES, (t + 1) * LANES)
            df = -d_ref[:, sl]
            c = lax.dot_general(df, tri, NN, preferred_element_type=F32,
                                precision=lax.Precision.HIGHEST) + carry
            carry = c[:, 0:1]
            z = z_ref[:, sl] + b_ref[...]
            dz = c * _sigmoid(-z)
            dz_ref[:, sl] = dz
            tot = tot + jnp.sum(dz, axis=1, keepdims=True)
        dbf_ref[...] = tot

    return pl.pallas_call(
        body, name="fox_gates_bwd",
        out_shape=[jax.ShapeDtypeStruct((HEADS, S), F32), jax.ShapeDtypeStruct((HEADS, 1), F32)],
        compiler_params=_params(),
    )(dbias, zt, bf)


def _mod_part(c_all, w_ada, b_cols):
    def body(c_ref, w_ref, b_ref, o_ref, s_ref):
        c = c_ref[...]
        sc = c * _sigmoid(c)
        s_ref[...] = sc
        o_ref[...] = lax.dot_general(sc, w_ref[...], NN, preferred_element_type=F32,
                                     precision=lax.Precision.HIGHEST) + b_ref[...]

    return pl.pallas_call(
        body, name="mod_part",
        out_shape=[jax.ShapeDtypeStruct((N_DEV, w_ada.shape[1]), F32), jax.ShapeDtypeStruct(c_all.shape, F32)],
        compiler_params=_params(),
    )(c_all, w_ada, b_cols)


def _w_ada_grad(sc_t, dm):
    def body(s_ref, d_ref, o_ref):
        acc = jnp.zeros(o_ref.shape, F32)
        for b in range(N_DEV):
            acc = acc + s_ref[:, b:b + 1] * d_ref[b:b + 1, :]
        o_ref[...] = acc

    return pl.pallas_call(
        body, name="w_ada_grad", out_shape=jax.ShapeDtypeStruct((sc_t.shape[0], dm.shape[1]), F32),
        compiler_params=_params(),
    )(sc_t, dm)


def _adamw(w, m, v, parts, name):
    rows, cols = w.shape
    n = parts.shape[0]
    tr = rows if rows <= 512 else 256

    def body(w_ref, m_ref, v_ref, p_ref, g_out, d_out, m_out, v_out):
        g = p_ref[0].astype(F32)
        for kk in range(1, n):
            g = g + p_ref[kk].astype(F32)
        g_out[...] = g
        d_out[...], m_out[...], v_out[...] = _adamw_math(w_ref[...], g, m_ref[...], v_ref[...])

    spec = pl.BlockSpec((tr, cols), lambda i: (i, 0))
    return pl.pallas_call(
        body, name=name, grid=(rows // tr,),
        in_specs=[spec, spec, spec, pl.BlockSpec((n, tr, cols), lambda i: (0, i, 0))],
        out_specs=[spec] * 4, out_shape=[jax.ShapeDtypeStruct((rows, cols), F32)] * 4,
        compiler_params=_params(("parallel",)),
    )(w, m, v, parts)


def _adamw_math(w, g, m, v):
    mm = ADAM_B1 * m + (1.0 - ADAM_B1) * g
    vv = ADAM_B2 * v + (1.0 - ADAM_B2) * (g * g)
    m_hat = mm / (1.0 - ADAM_B1 ** ADAM_STEP)
    v_hat = vv / (1.0 - ADAM_B2 ** ADAM_STEP)
    return -ADAM_LR * (m_hat / (jnp.sqrt(v_hat) + ADAM_EPS) + ADAM_WD * w), mm, vv


def _adamw_rows(bundles, offsets, ws, ms, vs, err_off, err_width):
    k = len(ws)

    def body(*refs):
        b_ref = refs[0]
        w_refs, m_refs, v_refs = refs[1:1 + k], refs[1 + k:1 + 2 * k], refs[1 + 2 * k:1 + 3 * k]
        outs = refs[1 + 3 * k:]
        g_all = b_ref[0]
        for kk in range(1, N_DEV):
            g_all = g_all + b_ref[kk]
        for i in range(k):
            width = w_refs[i].shape[1]
            g = g_all[:, offsets[i]:offsets[i] + width]
            outs[4 * i][...] = g
            outs[4 * i + 1][...], outs[4 * i + 2][...], outs[4 * i + 3][...] = _adamw_math(
                w_refs[i][...], g, m_refs[i][...], v_refs[i][...])
        outs[4 * k][...] = g_all[:, err_off:err_off + err_width]

    out_shape = []
    for w_ in ws:
        out_shape += [jax.ShapeDtypeStruct(w_.shape, F32)] * 4
    out_shape.append(jax.ShapeDtypeStruct((1, err_width), F32))
    res = pl.pallas_call(body, name="adamw_rows", out_shape=out_shape, compiler_params=_params())(bundles, *ws, *ms, *vs)
    return [tuple(res[4 * i:4 * i + 4]) for i in range(k)], res[-1]


def _coords():
    return lax.axis_index("x"), lax.axis_index("y"), lax.axis_index("c")


def _flat(px, py, pc):
    return 4 * px + 2 * py + pc


def _all_gather(arrs, name):
    n = len(arrs)

    def body(*refs):
        ins, outs = refs[:n], refs[n:2 * n]
        send, recv, lsem = refs[2 * n:]
        x, y, c = _coords()
        me, sibling = (x, y, c), (x, y, 1 - c)
        chips = [(1 - x, y), (x, 1 - y), (1 - x, 1 - y)]

        def copy(a, kk, block, to, src=None):
            slot = outs[a].at[_flat(*block)]
            return pltpu.make_async_remote_copy(
                src_ref=slot if src is None else src, dst_ref=slot,
                send_sem=send.at[a, kk], recv_sem=recv.at[a, kk],
                device_id=to, device_id_type=MESH)

        mine = [pltpu.make_async_copy(ins[a], outs[a].at[_flat(*me)], lsem.at[a]) for a in range(n)]
        for cp in mine:
            cp.start()
        first = []
        for a in range(n):
            first.append(copy(a, 0, me, sibling, src=ins[a]))
            first += [copy(a, 1 + j, me, (*chip, c), src=ins[a]) for j, chip in enumerate(chips)]
        for cp in first:
            cp.start()
        passed = []
        for j, chip in enumerate(chips):
            for a in range(n):
                copy(a, 1 + j, (*chip, c), me).wait_recv()
                cp = copy(a, 4 + j, (*chip, c), sibling)
                cp.start()
                passed.append(cp)
        for a in range(n):
            copy(a, 0, sibling, me).wait_recv()
        for j, chip in enumerate(chips):
            for a in range(n):
                copy(a, 4 + j, (*chip, 1 - c), me).wait_recv()
        for cp in first + passed:
            cp.wait_send()
        for cp in mine:
            cp.wait()

    any_spec = pl.BlockSpec(memory_space=pl.ANY)
    return pl.pallas_call(
        body, name=name,
        in_specs=[any_spec] * n, out_specs=[any_spec] * n,
        out_shape=[jax.ShapeDtypeStruct((N_DEV,) + a.shape, a.dtype) for a in arrs],
        scratch_shapes=[pltpu.SemaphoreType.DMA((n, 7)), pltpu.SemaphoreType.DMA((n, 7)),
                        pltpu.SemaphoreType.DMA((n,))],
    )(*arrs)


def _peer_list():
    x, y, c = _coords()
    return [((1 - x if r & 4 else x), (1 - y if r & 2 else y), (1 - c if r & 1 else c)) for r in range(1, N_DEV)]


def _copy_plan(mode, src, land):
    x, y, c = _coords()
    me = _flat(x, y, c)
    if mode == "gather":
        return [(src, land.at[me], peer) for peer in _peer_list()]
    if mode == "exchange":
        return [(src.at[_flat(*peer)], land.at[me], peer) for peer in _peer_list()]
    if mode == "pair":
        return [(src.at[_flat(q // 2, q % 2, 1 - c)], land.at[q], (x, y, 1 - c)) for q in range(N_DEV // 2)]
    assert mode == "chips"
    plan = []
    for r in range(1, N_DEV // 2):
        qx, qy = (1 - x if r & 2 else x), (1 - y if r & 1 else y)
        plan.append((src.at[2 * qx + qy], land.at[2 * x + y], (qx, qy, c)))
    return plan


N_COPIES = dict(gather=N_DEV - 1, exchange=N_DEV - 1, pair=N_DEV // 2, chips=N_DEV // 2 - 1)


def _land_shape(mode, shape):
    return {"gather": (N_DEV,) + shape, "exchange": shape, "pair": (N_DEV // 2,) + shape[1:], "chips": shape}[mode]


HBM_SPEC = pl.BlockSpec(memory_space=pltpu.HBM)
SEM_SPEC = pl.BlockSpec(memory_space=pltpu.SEMAPHORE)
ANY_SPEC = pl.BlockSpec(memory_space=pl.ANY)
SIDE_EFFECT = pltpu.SideEffectType.DATAFLOW_SIDE_EFFECTING


def _async_start(groups, mode, after, name):
    flat_arrs = [a for g in groups for a in g]
    n = len(flat_arrs)

    def body(*refs):
        srcs, lands = refs[:n], refs[n:2 * n]
        outs = refs[2 * n + 1:]
        token = outs[-1]
        for ai in range(n):
            for src_ref, dst_ref, peer in _copy_plan(mode, srcs[ai], lands[ai]):
                pltpu.make_async_remote_copy(src_ref=src_ref, dst_ref=dst_ref, send_sem=outs[2 * ai],
                                             recv_sem=outs[2 * ai + 1], device_id=peer, device_id_type=MESH).start()
        token[...] = jnp.zeros(token.shape, F32)

    land_shapes = [_land_shape(mode, a.shape) for a in flat_arrs]
    out_shape = [pltpu.SemaphoreType.DMA(())] * (2 * n)
    out_shape += [pltpu.HBM(a.shape, a.dtype) for a in flat_arrs]
    out_shape += [pltpu.HBM(s, a.dtype) for s, a in zip(land_shapes, flat_arrs)]
    out_shape.append(jax.ShapeDtypeStruct((8, LANES), F32))
    res = pl.pallas_call(
        body, name=name, out_shape=tuple(out_shape),
        in_specs=[HBM_SPEC] * (2 * n) + [ANY_SPEC],
        out_specs=tuple([SEM_SPEC] * (2 * n) + [HBM_SPEC] * (2 * n) + [pl.BlockSpec(memory_space=pltpu.VMEM)]),
        input_output_aliases={i: 2 * n + i for i in range(2 * n)},
        compiler_params=pltpu.CompilerParams(has_side_effects=SIDE_EFFECT),
    )(*[pltpu.with_memory_space_constraint(a, pltpu.HBM) for a in flat_arrs],
      *[pltpu.with_memory_space_constraint(lax.empty(s, a.dtype), pltpu.HBM) for s, a in zip(land_shapes, flat_arrs)],
      after)
    sems, thru = res[:2 * n], res[2 * n:-1]
    states, idx = [], 0
    for g in groups:
        k = len(g)
        states.append((list(sems[2 * idx:2 * (idx + k):2]), list(sems[2 * idx + 1:2 * (idx + k):2]),
                       list(thru[idx:idx + k]), list(thru[n + idx:n + idx + k]), mode))
        idx += k
    return states, res[-1]


def _async_wait(state, after, name):
    sends, recvs, srcs, lands, mode = state
    g = len(srcs)

    def body(*refs):
        l_refs, sems = refs[g:2 * g], refs[2 * g:4 * g]
        for ai in range(g):
            moved = l_refs[ai].at[pl.ds(0, N_COPIES[mode])]
            cp = pltpu.make_async_remote_copy(src_ref=moved, dst_ref=moved, send_sem=sems[ai], recv_sem=sems[g + ai],
                                              device_id=_coords(), device_id_type=MESH)
            cp.wait_send()
            cp.wait_recv()

    res = pl.pallas_call(
        body, name=name,
        out_shape=tuple([pltpu.HBM(a.shape, a.dtype) for a in srcs] + [pltpu.HBM(a.shape, a.dtype) for a in lands]),
        in_specs=[HBM_SPEC] * (2 * g) + [SEM_SPEC] * (2 * g) + [ANY_SPEC],
        out_specs=tuple([HBM_SPEC] * (2 * g)),
        input_output_aliases={i: i for i in range(2 * g)},
        compiler_params=pltpu.CompilerParams(has_side_effects=SIDE_EFFECT),
    )(*srcs, *lands, *sends, *recvs, after)
    return list(res[:g]), list(res[g:])


def _add_blocks(a, b):
    def body(a_ref, b_ref, o_ref):
        o_ref[...] = (a_ref[...].astype(F32) + b_ref[...].astype(F32)).astype(o_ref.dtype)

    spec = pl.BlockSpec((1,) + a.shape[1:], lambda i: (i, 0, 0))
    return pl.pallas_call(
        body, name="add_blocks", grid=(a.shape[0],), in_specs=[spec, spec], out_specs=spec,
        out_shape=jax.ShapeDtypeStruct(a.shape, a.dtype), compiler_params=_params(("parallel",)),
    )(a, b)


def _with_own(land, own, me):
    return lax.dynamic_update_index_in_dim(land, own, me, 0)


IN_SPLITS = (512, 512, 512, 8, 768, 256, 32, 1024, 1024)


def _unshard_cols(g):
    _, k, n = g.shape
    tr = min(k, 256)

    def body(g_ref, o_ref):
        o_ref[...] = jnp.concatenate([g_ref[j] for j in range(N_DEV)], axis=1)

    return pl.pallas_call(
        body, name="unshard_cols_%d" % n, grid=(k // tr,),
        in_specs=[pl.BlockSpec((N_DEV, tr, n), lambda i: (0, i, 0))],
        out_specs=pl.BlockSpec((tr, N_DEV * n), lambda i: (i, 0)),
        out_shape=jax.ShapeDtypeStruct((k, N_DEV * n), g.dtype),
        compiler_params=_params(("parallel",)),
    )(g)


FFN_T = 256
FFN_SHARD = 2 * D_FF // N_DEV


def _unshard_ffn_in(g):
    tr = 256

    def body(g_ref, o_ref):
        full = jnp.concatenate([g_ref[j] for j in range(N_DEV)], axis=1)
        parts = []
        for j in range(D_FF // FFN_T):
            parts += [full[:, j * FFN_T:(j + 1) * FFN_T], full[:, D_FF + j * FFN_T:D_FF + (j + 1) * FFN_T]]
        o_ref[...] = jnp.concatenate(parts, axis=1)

    return pl.pallas_call(
        body, name="unshard_ffn_in", grid=(D // tr,),
        in_specs=[pl.BlockSpec((N_DEV, tr, FFN_SHARD), lambda i: (0, i, 0))],
        out_specs=pl.BlockSpec((tr, 2 * D_FF), lambda i: (i, 0)),
        out_shape=jax.ShapeDtypeStruct((D, 2 * D_FF), g.dtype),
        compiler_params=_params(("parallel",)),
    )(g)


def _shard_ffn_in(w):
    tr = 256

    def body(w_ref, o_ref):
        x = w_ref[...]
        nb = D_FF // FFN_T
        full = jnp.concatenate([x[:, (2 * j + half) * FFN_T:(2 * j + half + 1) * FFN_T]
                                for half in range(2) for j in range(nb)], axis=1)
        for j in range(N_DEV):
            o_ref[j] = full[:, j * FFN_SHARD:(j + 1) * FFN_SHARD]

    return pl.pallas_call(
        body, name="shard_ffn_in", grid=(D // tr,),
        in_specs=[pl.BlockSpec((tr, 2 * D_FF), lambda i: (i, 0))],
        out_specs=pl.BlockSpec((N_DEV, tr, FFN_SHARD), lambda i: (0, i, 0)),
        out_shape=jax.ShapeDtypeStruct((N_DEV, D, FFN_SHARD), w.dtype),
        compiler_params=_params(("parallel",)),
    )(w)


def _shard_cols(w):
    k, n = w.shape[0], w.shape[1] // N_DEV
    tr = min(k, 256)

    def body(w_ref, o_ref):
        full = w_ref[...]
        for j in range(N_DEV):
            o_ref[j] = full[:, j * n:(j + 1) * n]

    return pl.pallas_call(
        body, name="shard_cols_%d" % n, grid=(k // tr,),
        in_specs=[pl.BlockSpec((tr, N_DEV * n), lambda i: (i, 0))],
        out_specs=pl.BlockSpec((N_DEV, tr, n), lambda i: (0, i, 0)),
        out_shape=jax.ShapeDtypeStruct((N_DEV, k, n), w.dtype),
        compiler_params=_params(("parallel",)),
    )(w)


IN_OFFS = tuple(sum(IN_SPLITS[:i]) for i in range(len(IN_SPLITS) + 1))
IN_SHARD = IN_OFFS[-1] // N_DEV
REGROUP_ROWS = 128


def _w_in_regroup(g):
    def body(g_ref, a_ref, b_ref):
        full = jnp.concatenate([g_ref[j] for j in range(N_DEV)], axis=1)
        fq, fk, fv, wf, cq, ckv, kr, gf, gm = [full[:, IN_OFFS[i]:IN_OFFS[i + 1]] for i in range(9)]
        rows = full.shape[0]
        a_ref[...] = jnp.concatenate([cq, ckv, gf, gm, wf, jnp.zeros((rows, 56), BF16), kr, jnp.zeros((rows, 32), BF16)], axis=1)
        b_ref[...] = jnp.concatenate([fq, fk, fv], axis=1)

    tr = REGROUP_ROWS
    return pl.pallas_call(
        body, name="w_in_regroup", grid=(D // tr,),
        in_specs=[pl.BlockSpec((N_DEV, tr, IN_SHARD), lambda i: (0, i, 0))],
        out_specs=[pl.BlockSpec((tr, 3200), lambda i: (i, 0)), pl.BlockSpec((tr, 1536), lambda i: (i, 0))],
        out_shape=[jax.ShapeDtypeStruct((D, 3200), BF16), jax.ShapeDtypeStruct((D, 1536), BF16)],
        compiler_params=_params(("parallel",)),
    )(g)


def _w_in_ungroup(da, db_):
    def body(a_ref, b_ref, o_ref):
        a = a_ref[...]
        full = jnp.concatenate([b_ref[...], a[:, 3072:3080], a[:, 0:768], a[:, 768:1024], a[:, 3136:3168],
                                a[:, 1024:3072]], axis=1)
        for j in range(N_DEV):
            o_ref[j] = full[:, j * IN_SHARD:(j + 1) * IN_SHARD]

    tr = REGROUP_ROWS
    return pl.pallas_call(
        body, name="w_in_ungroup", grid=(D // tr,),
        in_specs=[pl.BlockSpec((tr, 3200), lambda i: (i, 0)), pl.BlockSpec((tr, 1536), lambda i: (i, 0))],
        out_specs=pl.BlockSpec((N_DEV, tr, IN_SHARD), lambda i: (0, i, 0)),
        out_shape=jax.ShapeDtypeStruct((N_DEV, D, IN_SHARD), BF16),
        compiler_params=_params(("parallel",)),
    )(da, db_)


def _prepare_weights(g):
    w = {}
    if "w_in" in g:
        w["w_a"], w["w_b"] = _w_in_regroup(g["w_in"])
    if "w_uq" in g:
        w_uq = g["w_uq"].reshape(Q_LORA, HEADS, 96)
        w["w_uq"] = jnp.pad(w_uq, ((0, 0), (0, 0), (0, 32))).reshape(Q_LORA, HEADS * LANES)
        ukv = g["w_ukv"]
        w["w_k"] = jnp.transpose(jnp.pad(ukv[:, :, :64], ((0, 0), (0, 0), (0, 64))), (1, 0, 2)).reshape(KV_LORA, HEADS * LANES)
        w["w_v"] = jnp.transpose(ukv[:, :, 64:], (1, 0, 2)).reshape(KV_LORA, HEADS * HEAD_DIM)
        w["w_kv"] = jnp.concatenate([w["w_k"], w["w_v"]], axis=1)
    if "w_out" in g:
        w["w_pf"] = _unshard_cols(g["w_proj_fox"])
        w["w_pm"] = _unshard_cols(g["w_proj_mla"])
        w["w_out"] = g["w_out"].reshape(D, D)
    if "w_ffn_in" in g:
        w["w_ffn_in"] = _unshard_ffn_in(g["w_ffn_in"])
        w["w_ffn_out"] = g["w_ffn_out"].reshape(D_FF, D)
    return w


def _shard_grads(dw):
    out = {}
    if "w_a" in dw:
        out["w_in"] = _w_in_ungroup(dw["w_a"], dw["w_b"])
    if "w_uq" in dw:
        w_uq = dw["w_uq"].reshape(Q_LORA, HEADS, LANES)[:, :, :96].reshape(Q_LORA, Q_LORA)
        out["w_uq"] = w_uq.reshape(N_DEV, Q_LORA // N_DEV, Q_LORA)
        k_part = dw["w_k"].reshape(KV_LORA, HEADS, LANES)[:, :, :64]
        v_part = dw["w_v"].reshape(KV_LORA, HEADS, HEAD_DIM)
        out["w_ukv"] = jnp.transpose(jnp.concatenate([k_part, v_part], axis=2), (1, 0, 2))
    if "w_out" in dw:
        out["w_proj_fox"] = _shard_cols(dw["w_pf"])
        out["w_proj_mla"] = _shard_cols(dw["w_pm"])
        out["w_out"] = dw["w_out"].reshape(N_DEV, D // N_DEV, D)
    if "w_ffn_in" in dw:
        out["w_ffn_in"] = _shard_ffn_in(dw["w_ffn_in"])
        out["w_ffn_out"] = dw["w_ffn_out"].reshape(N_DEV, D_FF // N_DEV, D)
    return out


def _fwd_bwd(x, pos, mod, target, w, vec, wts, send, relay):
    shift_mix, scale_mix, gate_mix, shift_ffn, scale_ffn, gate_ffn = [mod[:, i * D:(i + 1) * D] for i in range(6)]
    g_pre_mix, g_post_mix, g_pre_ffn, g_post_ffn = vec["g_pre_mix"], vec["g_post_mix"], vec["g_pre_ffn"], vec["g_post_ffn"]
    g_q, g_kv = vec["g_q_lora"], vec["g_kv_lora"]

    inv_freq = 1.0 / (ROPE_THETA ** (jnp.arange(0, ROPE_DIM, 2, dtype=F32) / ROPE_DIM))
    invf = jnp.concatenate([jnp.zeros((64,), F32), inv_freq, inv_freq, jnp.zeros((32,), F32)]).reshape(1, LANES)
    ct, sa, sb = _rope_tables(pos, invf)

    def pre1(xv, g, sc, sh):
        return ((xv * _rstd(xv) * g) * (1.0 + sc) + sh,), ()
    (h,) = _rowwise(pre1, [(x, D, 0)], [g_pre_mix, scale_mix, shift_mix], [(D, BF16)], [], "pre_mix")
    proj_a = _mm(h, w["w_a"], "nn", F32, "in_proj_a")
    qkv = _mm(h, w["w_b"], "nn", BF16, "in_proj_b")

    def lora_norm(cq, ckv, gq, gkv):
        return (cq * _rstd(cq) * gq, ckv * _rstd(ckv) * gkv), ()
    cqn, ckvn = _rowwise(lora_norm, [(proj_a, Q_LORA, 0), (proj_a, KV_LORA, 3)], [g_q, g_kv],
                         [(Q_LORA, BF16), (KV_LORA, BF16)], [], "lora_norm")
    w = {**w, **wts("lora", cqn)}
    qb = _mm(cqn, w["w_uq"], "nn", F32, "mla_uq")
    kvb = _mm(ckvn, w["w_kv"], "nn", F32, "mla_ukv")

    def mla_rope(qv, kv, vv, misc, c_, a_, b_):
        lane = lax.broadcasted_iota(jnp.int32, (1, LANES), 1)
        kpe = jnp.where((lane >= 64) & (lane < 96), _rope(misc, c_, a_, b_), 0.0)
        qs = [_rope(qv[:, hd * LANES:(hd + 1) * LANES], c_, a_, b_) for hd in range(HEADS)]
        ks = [kv[:, hd * LANES:(hd + 1) * LANES] + kpe for hd in range(HEADS)]
        return (jnp.concatenate(qs, axis=1), jnp.concatenate(ks, axis=1), vv), ()
    q_m, k_m, v_m = _rowwise(
        mla_rope, [(qb, D, 0), (kvb, D, 0), (kvb, 512, 2), (proj_a, LANES, 24), (ct, LANES, 0), (sa, LANES, 0), (sb, LANES, 0)],
        [], [(D, BF16), (D, BF16), (512, BF16)], [], "mla_rope")

    zt = jnp.transpose(proj_a[:, 3072:3080])
    bf = jnp.transpose(vec["b_forget"])
    neg_f = _fox_gates(zt, bf)
    bias = neg_f.reshape(HEADS, N_ATT, 1, ATT_T)
    o_a, lse_a = _attn_fwd(qkv, 0, qkv, 4, qkv, 8, LANES, 1.0 / math.sqrt(HEAD_DIM), bias, "fox_attn")
    o_b, lse_b = _attn_fwd(q_m, 0, k_m, 0, v_m, 0, 2 * LANES, 1.0 / math.sqrt(64 + ROPE_DIM), None, "mla_attn")

    w = {**w, **wts("proj", o_b)}
    pa = _mm(o_a, w["w_pf"], "nn", BF16, "proj_fox")

    def merge(pb_, gf, gm, pa_):
        return (_sigmoid(gf) * pa_ + _sigmoid(gm) * pb_, pb_), ()
    merged, pb = _mm_epi(o_b, w["w_pm"], "nn", 512, merge, "proj_mla", 1024,
                         rows=[(proj_a, 512, 2), (proj_a, 512, 4), (pa, 512)], outs=[(D, 512, BF16), (D, 512, BF16)])
    def post1(yv, xv, gate, gpost, gpre, sc, sh):
        x1 = xv + gate * (yv * _rstd(yv) * gpost)
        return (x1, (x1 * _rstd(x1) * gpre) * (1.0 + sc) + sh, yv), ()
    x1, h2, y = _mm_epi(merged, w["w_out"], "nn", D, post1, "out_proj", 256, rows=[(x, D)],
                        vecs=[gate_mix, g_post_mix, g_pre_ffn, scale_ffn, shift_ffn],
                        outs=[(D, D, F32), (D, D, BF16), (D, D, F32)])
    w = {**w, **wts("ffn", h2)}

    def swiglu(r):
        g, u = r[:, :FFN_T], r[:, FFN_T:]
        return (g * _sigmoid(g) * u, r), ()
    act, gu = _mm_epi(h2, w["w_ffn_in"], "nn", 2 * FFN_T, swiglu, "ffn_in", 1024,
                      outs=[(D_FF, FFN_T, BF16), (2 * D_FF, 2 * FFN_T, BF16)])

    def head(y2v, x1v, tv, gate, gpost):
        r = _rstd(y2v)
        yn = y2v * r
        n2 = yn * gpost
        err = (x1v + gate * n2) - tv
        dx2 = err * (1.0 / D)
        dn2 = dx2 * gate
        dy2 = _norm_bwd(dn2 * gpost, yn, r)
        return (dx2, dy2), (_colsum(err * err), _colsum(dx2 * n2), _colsum(dn2 * yn))
    dx2, dy2, err_cols, d_gate_ffn, d_g_post_ffn = _mm_epi(
        act, w["w_ffn_out"], "nn", D, head, "ffn_out", 256, rows=[(x1, D), (target, D)], vecs=[gate_ffn, g_post_ffn],
        outs=[(D, D, F32), (D, D, BF16)], sums=[D, D, D])

    def swiglu_bwd(da, guv):
        g, u = guv[:, :FFN_T].astype(F32), guv[:, FFN_T:].astype(F32)
        sg = _sigmoid(g)
        return (jnp.concatenate([da * u * (sg * (1.0 + g * (1.0 - sg))), da * (g * sg)], axis=1),), ()
    (dgu,) = _mm_epi(dy2, w["w_ffn_out"], "nt", FFN_T, swiglu_bwd, "ffn_out_dx", 1024, rows=[(gu, 2 * FFN_T)],
                     outs=[(2 * D_FF, 2 * FFN_T, BF16)])
    dw = {"w_ffn_out": _mm(act, dy2, "tn", BF16, "ffn_out_dw")}
    dw["w_ffn_in"] = _mm(h2, dgu, "tn", BF16, "ffn_in_dw")
    gate_mix = gate_mix + send({n: dw.pop(n) for n in ("w_ffn_in", "w_ffn_out")})[0, 0]

    def mid(dh, x1v, dx2v, yv, gpre, sc, gate, gpost):
        r2 = _rstd(x1v)
        x1n = x1v * r2
        t = dh * x1n
        dx1 = dx2v + _norm_bwd(dh * (gpre * (1.0 + sc)), x1n, r2)
        ry = _rstd(yv)
        yn = yv * ry
        dn1 = dx1 * gate
        dy = _norm_bwd(dn1 * gpost, yn, ry)
        sums = (_colsum(dh), _colsum(t) * gpre, _colsum(t) * (1.0 + sc), _colsum(dx1 * (yn * gpost)), _colsum(dn1 * yn))
        return (dx1, dy), sums
    dx1, dy, d_shift_ffn, d_scale_ffn, d_g_pre_ffn, d_gate_mix, d_g_post_mix = _mm_epi(
        dgu, w["w_ffn_in"], "nt", D, mid, "ffn_in_dx", 256, rows=[(x1, D), (dx2, D), (y, D)],
        vecs=[g_pre_ffn, scale_ffn, gate_mix, g_post_mix], outs=[(D, D, F32), (D, D, BF16)], sums=[D] * 5)

    dw["w_out"] = _mm(merged, dy, "tn", BF16, "out_proj_dw")

    def merge_bwd(dm, gf, gm, pa_, pb_):
        sf, sm = _sigmoid(gf), _sigmoid(gm)
        return (dm * sf, dm * sm, dm * pa_ * (sf * (1.0 - sf)), dm * pb_ * (sm * (1.0 - sm))), ()
    dpa, dpb, dgf, dgm = _mm_epi(dy, w["w_out"], "nt", 512, merge_bwd, "out_proj_dx", 1024,
                                 rows=[(proj_a, 512, 2), (proj_a, 512, 4), (pa, 512), (pb, 512)],
                                 outs=[(D, 512, BF16)] * 4)
    do_a = _mm(dpa, w["w_pf"], "nt", BF16, "proj_fox_dx")
    do_b = _mm(dpb, w["w_pm"], "nt", BF16, "proj_mla_dx")
    dw["w_pf"] = _mm(o_a, dpa, "tn", BF16, "proj_fox_dw")
    dw["w_pm"] = _mm(o_b, dpb, "tn", BF16, "proj_mla_dw")
    bias = bias + send({n: dw.pop(n) for n in ("w_out", "w_pf", "w_pm")})[0, 0]

    sc_a, sc_b = 1.0 / math.sqrt(HEAD_DIM), 1.0 / math.sqrt(64 + ROPE_DIM)
    dq_a, dk_a, dv_a, dbias = _attn_grad(qkv, 0, qkv, 4, qkv, 8, do_a, lse_a, LANES, sc_a, bias, BF16, "fox_attn_bwd")
    dq_m, dk_m, dv_m = _attn_grad(q_m, 0, k_m, 0, v_m, 0, do_b, lse_b, 2 * LANES, sc_b, None, F32, "mla_attn_bwd")

    def mla_rope_bwd(dq, dk, c_, a_, b_):
        lane = lax.broadcasted_iota(jnp.int32, (1, LANES), 1)
        dqs = [_rope_t(dq[:, hd * LANES:(hd + 1) * LANES], c_, a_, b_) for hd in range(HEADS)]
        dkpe = dk[:, 0:LANES]
        for hd in range(1, HEADS):
            dkpe = dkpe + dk[:, hd * LANES:(hd + 1) * LANES]
        dkpe = jnp.where((lane >= 64) & (lane < 96), dkpe, 0.0)
        dkr = jnp.where((lane >= 64) & (lane < 96), _rope_t(dkpe, c_, a_, b_), 0.0)
        return (jnp.concatenate(dqs, axis=1), dk, dkr), ()
    dqb, dkb, dkr = _rowwise(mla_rope_bwd, [(dq_m, D, 0), (dk_m, D, 0), (ct, LANES, 0), (sa, LANES, 0), (sb, LANES, 0)],
                             [], [(D, BF16), (D, BF16), (LANES, F32)], [], "mla_rope_bwd")
    def lora_q_bwd(dq, cq, gq):
        rq = _rstd(cq)
        cqh = cq * rq
        return (_norm_bwd(dq * gq, cqh, rq),), (_colsum(dq * cqh),)
    dcq, d_g_q = _mm_epi(dqb, w["w_uq"], "nt", Q_LORA, lora_q_bwd, "mla_uq_dx", 512, rows=[(proj_a, Q_LORA, 0)],
                         vecs=[g_q], outs=[(Q_LORA, Q_LORA, BF16)], sums=[Q_LORA])
    dw["w_uq"] = _mm(cqn, dqb, "tn", BF16, "mla_uq_dw")

    def lora_kv_bwd(dv_part, dk_part, ckv, gkv):
        dkv = dv_part + dk_part
        rk = _rstd(ckv)
        ckh = ckv * rk
        return (_norm_bwd(dkv * gkv, ckh, rk),), (_colsum(dkv * ckh),)
    dckv, d_g_kv = _mm_epi(dv_m, w["w_v"], "nt", KV_LORA, lora_kv_bwd, "mla_uv_dx", 1024,
                           rows=[(_mm(dkb, w["w_k"], "nt", F32, "mla_uk_dx"), KV_LORA), (proj_a, KV_LORA, 3)],
                           vecs=[g_kv], outs=[(KV_LORA, KV_LORA, BF16)], sums=[KV_LORA])
    dw["w_k"] = _mm(ckvn, dkb, "tn", BF16, "mla_uk_dw")
    dw["w_v"] = _mm(ckvn, dv_m, "tn", BF16, "mla_uv_dw")

    dzt, d_bf = _fox_gates_bwd(dbias.reshape(HEADS, S), zt, bf)
    dmisc = (dkr + jnp.pad(jnp.transpose(dzt), ((0, 0), (0, LANES - HEADS)))).astype(BF16)
    dproj_a = jnp.concatenate([dcq, dckv, dgf, dgm, dmisc], axis=1)
    dqkv = jnp.concatenate([dq_a, dk_a, dv_a], axis=1)
    dw["w_a"] = _mm(h, dproj_a, "tn", BF16, "in_proj_a_dw")
    dw["w_b"] = _mm(h, dqkv, "tn", BF16, "in_proj_b_dw")
    tok = send(dw, True)
    dh_a = _mm(dproj_a, w["w_a"], "nt", F32, "in_proj_a_dx", dep=tok)
    g_pre_mix = g_pre_mix + relay(dh_a)[0, 0]

    def first(dh_b, dh_a, xv, dx1v, gpre, sc):
        dhv = dh_b + dh_a
        r = _rstd(xv)
        xn = xv * r
        t = dhv * xn
        dx = dx1v + _norm_bwd(dhv * (gpre * (1.0 + sc)), xn, r)
        return (dx,), (_colsum(dhv), _colsum(t) * gpre, _colsum(t) * (1.0 + sc))
    grad_x, d_shift_mix, d_scale_mix, d_g_pre_mix = _mm_epi(
        dqkv, w["w_b"], "nt", D, first, "in_proj_b_dx", 256,
        rows=[(dh_a, D), (x, D), (dx1, D)],
        vecs=[g_pre_mix, scale_mix], outs=[(D, D, F32)], sums=[D] * 3)

    dmod = jnp.concatenate([d_shift_mix, d_scale_mix, d_gate_mix, d_shift_ffn, d_scale_ffn, d_gate_ffn], axis=1)
    small = dict(dmod=dmod, g_pre_mix=d_g_pre_mix, g_post_mix=d_g_post_mix, g_pre_ffn=d_g_pre_ffn,
                 g_post_ffn=d_g_post_ffn, g_q_lora=d_g_q, g_kv_lora=d_g_kv,
                 b_forget=jnp.pad(jnp.transpose(d_bf), ((0, 0), (0, LANES - HEADS))), err=err_cols)
    return grad_x, small


SMALL_ORDER = ("dmod", "g_pre_mix", "g_post_mix", "g_pre_ffn", "g_post_ffn", "g_q_lora", "g_kv_lora", "b_forget", "err")
SMALL_PARAM = {"dmod": "b_ada"}
MATRICES = ("w_in", "w_uq", "w_ukv", "w_proj_fox", "w_proj_mla", "w_out", "w_ffn_in", "w_ffn_out")
WEIGHTS = ("w_ada", "b_ada", "g_pre_mix", "g_post_mix", "g_pre_ffn", "g_post_ffn", "w_in", "b_forget", "g_q_lora",
           "w_uq", "g_kv_lora", "w_ukv", "w_proj_fox", "w_proj_mla", "w_out", "w_ffn_in", "w_ffn_out")


def kernel(x, c, positions, w_ada, b_ada, g_pre_mix, g_post_mix, g_pre_ffn, g_post_ffn, w_in, b_forget, g_q_lora, w_uq, g_kv_lora, w_ukv, w_proj_fox, w_proj_mla, w_out, w_ffn_in, w_ffn_out, loss_target, m_w_ada, m_b_ada, m_g_pre_mix, m_g_post_mix, m_g_pre_ffn, m_g_post_ffn, m_w_in, m_b_forget, m_g_q_lora, m_w_uq, m_g_kv_lora, m_w_ukv, m_w_proj_fox, m_w_proj_mla, m_w_out, m_w_ffn_in, m_w_ffn_out, v_w_ada, v_b_ada, v_g_pre_mix, v_g_post_mix, v_g_pre_ffn, v_g_post_ffn, v_w_in, v_b_forget, v_g_q_lora, v_w_uq, v_g_kv_lora, v_w_ukv, v_w_proj_fox, v_w_proj_mla, v_w_out, v_w_ffn_in, v_w_ffn_out):
    prm = dict(w_ada=w_ada, b_ada=b_ada, g_pre_mix=g_pre_mix, g_post_mix=g_post_mix, g_pre_ffn=g_pre_ffn,
               g_post_ffn=g_post_ffn, w_in=w_in, b_forget=b_forget, g_q_lora=g_q_lora, w_uq=w_uq, g_kv_lora=g_kv_lora,
               w_ukv=w_ukv, w_proj_fox=w_proj_fox, w_proj_mla=w_proj_mla, w_out=w_out, w_ffn_in=w_ffn_in, w_ffn_out=w_ffn_out)
    mom = dict(w_ada=m_w_ada, b_ada=m_b_ada, g_pre_mix=m_g_pre_mix, g_post_mix=m_g_post_mix, g_pre_ffn=m_g_pre_ffn,
               g_post_ffn=m_g_post_ffn, w_in=m_w_in, b_forget=m_b_forget, g_q_lora=m_g_q_lora, w_uq=m_w_uq,
               g_kv_lora=m_g_kv_lora, w_ukv=m_w_ukv, w_proj_fox=m_w_proj_fox, w_proj_mla=m_w_proj_mla, w_out=m_w_out,
               w_ffn_in=m_w_ffn_in, w_ffn_out=m_w_ffn_out)
    var = dict(w_ada=v_w_ada, b_ada=v_b_ada, g_pre_mix=v_g_pre_mix, g_post_mix=v_g_post_mix, g_pre_ffn=v_g_pre_ffn,
               g_post_ffn=v_g_post_ffn, w_in=v_w_in, b_forget=v_b_forget, g_q_lora=v_g_q_lora, w_uq=v_w_uq,
               g_kv_lora=v_g_kv_lora, w_ukv=v_w_ukv, w_proj_fox=v_w_proj_fox, w_proj_mla=v_w_proj_mla, w_out=v_w_out,
               w_ffn_in=v_w_ffn_in, w_ffn_out=v_w_ffn_out)
    me = _flat(*_coords())

    own = {n: prm[n][0].astype(BF16) for n in MATRICES}
    w_in_all, c_all = _all_gather([own["w_in"], c], "gather_in")
    w = _prepare_weights({"w_in": w_in_all})
    c_all = c_all.reshape(N_DEV, D)
    later = dict(lora=("w_uq", "w_ukv"), proj=("w_proj_fox", "w_proj_mla", "w_out"), ffn=("w_ffn_in", "w_ffn_out"))
    states, tok = _async_start([[own[n] for n in names] for names in later.values()], "gather", w_in_all, "gather_rest_start")
    gather_state = dict(zip(later, states))

    def wts(group, after):
        srcs, lands = _async_wait(gather_state[group], after, "gather_" + group + "_wait")
        return _prepare_weights({n: _with_own(land, src, me) for n, src, land in zip(later[group], srcs, lands)})

    sent, last = [], {}
    no_dep = jnp.zeros((8, LANES), F32)

    def send(grads, final=False):
        shards = _shard_grads(grads)
        names = list(shards)
        (state,), t = _async_start([[shards[n] for n in names]], "pair" if final else "exchange", no_dep,
                                   "exchange_" + names[0] + "_start")
        if final:
            last.update(names=names, state=state)
        else:
            sent.append((names, state))
        return t

    def relay(after):
        srcs, lands = _async_wait(last["state"], after, "exchange_pair_wait")
        sums = []
        for src, land in zip(srcs, lands):
            by_chip = src.reshape((N_DEV // 2, 2) + src.shape[1:])
            sums.append(_add_blocks(lax.dynamic_index_in_dim(by_chip, lax.axis_index("c"), 1, keepdims=False), land))
        (last["state"],), t = _async_start([sums], "chips", no_dep, "exchange_chips_start")
        return t

    ada_cols = w_ada.shape[2]
    b_cols = lax.dynamic_slice(b_ada, (0, me * ada_cols), (1, ada_cols))
    mod_cols, silu_c = _mod_part(c_all, w_ada[0], b_cols)
    (mod_all,) = _all_gather([mod_cols], "gather_mod")
    mod = lax.dynamic_index_in_dim(mod_all, me, axis=1, keepdims=False).reshape(1, 6 * D) + tok[0, 0]

    vec = dict(g_pre_mix=g_pre_mix, g_post_mix=g_post_mix, g_pre_ffn=g_pre_ffn, g_post_ffn=g_post_ffn,
               g_q_lora=g_q_lora, g_kv_lora=g_kv_lora, b_forget=b_forget)
    pos = positions.astype(F32).reshape(S, 1)
    grad_x, small = _fwd_bwd(x[0], pos, mod, loss_target[0], w, vec, wts, send, relay)

    bundle = jnp.concatenate([small[n] for n in SMALL_ORDER], axis=1)
    (small_state,), tok = _async_start([[bundle]], "gather", jnp.zeros((8, LANES), F32), "gather_small_start")

    out = {}
    after = tok
    for names, state in sent:
        srcs, lands = _async_wait(state, after, "exchange_" + names[0] + "_wait")
        for n, src, land in zip(names, srcs, lands):
            parts = _with_own(land, lax.dynamic_index_in_dim(src, me, 0, keepdims=False), me)
            out[n] = _adamw(prm[n][0], mom[n][0], var[n][0], parts, "adamw_" + n)
            after = out[n][0]
    srcs, lands = _async_wait(last["state"], after, "exchange_chips_wait")
    chip = me // 2
    for n, src, land in zip(last["names"], srcs, lands):
        parts = _with_own(land, lax.dynamic_index_in_dim(src, chip, 0, keepdims=False), chip)
        out[n] = _adamw(prm[n][0], mom[n][0], var[n][0], parts, "adamw_" + n)
        after = out[n][0]

    (own_bundle,), (bundle_all,) = _async_wait(small_state, after, "gather_small_wait")
    bundle_all = _with_own(bundle_all, own_bundle, me)
    dmod_all = bundle_all[:, 0, :6 * D]
    dm_cols = lax.dynamic_slice(dmod_all, (0, me * ada_cols), (N_DEV, ada_cols))
    g_ada = _w_ada_grad(jnp.transpose(silu_c), dm_cols)
    out["w_ada"] = _adamw(w_ada[0], m_w_ada[0], v_w_ada[0], g_ada[None], "adamw_w_ada")

    offsets, off = {}, 0
    for n in SMALL_ORDER:
        offsets[n] = off
        off += small[n].shape[1]
    names = [SMALL_PARAM.get(n, n) for n in SMALL_ORDER if n != "err"]
    results, err = _adamw_rows(bundle_all, [offsets[n] for n in SMALL_ORDER if n != "err"],
                               [prm[n] for n in names], [mom[n] for n in names], [var[n] for n in names],
                               offsets["err"], D)
    out.update(zip(names, results))
    loss = 0.5 * jnp.sum(err) / D

    res = [loss, grad_x[None]]
    for kind in range(4):
        for n in WEIGHTS:
            t = out[n][kind]
            res.append(t[None] if prm[n].ndim == 3 else t)
    return tuple(res)
```

```python
import functools
import math

import jax
import jax.numpy as jnp
from jax import lax
from jax.experimental import pallas as pl
from jax.experimental.pallas import tpu as pltpu

F32 = jnp.float32
BF16 = jnp.bfloat16

N_DEV = 8
S = 2048
D = 1024
D_FF = 2816
HEADS = 8
HEAD_DIM = 64
Q_LORA = 768
KV_LORA = 256
ROPE_DIM = 32
ROPE_THETA = 10000.0
NORM_EPS = 1e-6
LANES = 128
VMEM_LIMIT = 56 * 1024 * 1024

ADAM_LR = 0.001
ADAM_B1 = 0.9
ADAM_B2 = 0.999
ADAM_EPS = 1e-08
ADAM_WD = 0.01
ADAM_STEP = 10

ATT_T = 256
N_ATT = S // ATT_T

NN = (((1,), (0,)), ((), ()))
NT = (((1,), (1,)), ((), ()))
TN = (((0,), (0,)), ((), ()))
MESH = pl.DeviceIdType.MESH


def _params(sem=None):
    return pltpu.CompilerParams(dimension_semantics=sem, vmem_limit_bytes=VMEM_LIMIT)


def _pick(n, cap):
    best = None
    for t in range(LANES, cap + 1, LANES):
        if n % t == 0:
            best = t
    return best if best is not None else n


def _mm(a, b, mode, out_dtype, name, acc=None, dep=None):
    if mode == "nn":
        (m, k), (k2, n), dn = a.shape, b.shape, NN
    elif mode == "nt":
        (m, k), (n, k2), dn = a.shape, b.shape, NT
    else:
        (k, m), (k2, n), dn = a.shape, b.shape, TN
    assert k == k2, (a.shape, b.shape, mode)
    tn = _pick(n, 640)
    tm = _pick(m, 1536)
    osz = jnp.dtype(out_dtype).itemsize

    def need(tm_):
        blk = tm_ * k * 2 + tn * k * 2 + tm_ * tn * osz + (tm_ * tn * 4 if acc is not None else 0)
        return 2 * blk + tm_ * tn * 4
    while need(tm) > 36 * 1024 * 1024 and tm % 256 == 0:
        tm //= 2

    def body(*refs):
        a_ref, b_ref, o_ref = refs[0], refs[1], refs[-1]
        r = lax.dot_general(a_ref[...], b_ref[...], dn, preferred_element_type=F32)
        if acc is not None:
            r = r + refs[2][...]
        o_ref[...] = r.astype(o_ref.dtype)

    if mode == "tn":
        a_spec = pl.BlockSpec((k, tm), lambda i, j: (0, i))
    else:
        a_spec = pl.BlockSpec((tm, k), lambda i, j: (i, 0))
    if mode == "nt":
        b_spec = pl.BlockSpec((tn, k), lambda i, j: (j, 0))
    else:
        b_spec = pl.BlockSpec((k, tn), lambda i, j: (0, j))
    o_spec = pl.BlockSpec((tm, tn), lambda i, j: (i, j))
    in_specs = [a_spec, b_spec] + ([o_spec] if acc is not None else [])
    in_specs += [pl.BlockSpec(memory_space=pl.ANY)] if dep is not None else []
    args = (a, b) + ((acc,) if acc is not None else ()) + ((dep,) if dep is not None else ())
    return pl.pallas_call(
        body, name=name, grid=(m // tm, n // tn),
        in_specs=in_specs, out_specs=o_spec,
        out_shape=jax.ShapeDtypeStruct((m, n), out_dtype),
        compiler_params=_params(("parallel", "parallel")),
    )(*args)


def _mm_epi(a, b, mode, tnb, epi, name, tm, rows=(), vecs=(), outs=(), sums=()):
    m, k = a.shape
    nb = b.shape[1] if mode == "nn" else b.shape[0]
    dn = NN if mode == "nn" else NT
    n_in = 2 + len(rows) + len(vecs)

    def body(*refs):
        r = lax.dot_general(refs[0][...], refs[1][...], dn, preferred_element_type=F32)
        o_vals, s_vals = epi(r, *[x[...] for x in refs[2:n_in]])
        o_refs = refs[n_in:n_in + len(outs)]
        s_refs = refs[n_in + len(outs):]
        assert len(o_vals) == len(o_refs) and len(s_vals) == len(s_refs)
        for o_ref, val in zip(o_refs, o_vals):
            o_ref[...] = val.astype(o_ref.dtype)
        if sums:
            @pl.when((pl.program_id(0) == 0) & (pl.program_id(1) == 0))
            def _():
                for s_ref in s_refs:
                    s_ref[...] = jnp.zeros(s_ref.shape, F32)
            for s_ref, val in zip(s_refs, s_vals):
                s_ref[...] += val

    b_spec = pl.BlockSpec((k, tnb), lambda i, j: (0, j)) if mode == "nn" else pl.BlockSpec((tnb, k), lambda i, j: (j, 0))
    in_specs = [pl.BlockSpec((tm, k), lambda i, j: (i, 0)), b_spec]
    rows = [tuple(r) + (0,) * (3 - len(r)) for r in rows]
    in_specs += [pl.BlockSpec((tm, w), functools.partial(lambda i, j, off: (i, j + off), off=off)) for _, w, off in rows]
    in_specs += [pl.BlockSpec(v.shape, lambda i, j: (0, 0)) for v in vecs]
    out_specs = [pl.BlockSpec((tm, w), lambda i, j: (i, j)) for _, w, _ in outs]
    out_specs += [pl.BlockSpec((1, w), lambda i, j: (0, 0)) for w in sums]
    out_shape = [jax.ShapeDtypeStruct((m, full), dt) for full, _, dt in outs]
    out_shape += [jax.ShapeDtypeStruct((1, w), F32) for w in sums]
    return pl.pallas_call(
        body, name=name, grid=(m // tm, nb // tnb),
        in_specs=in_specs, out_specs=out_specs, out_shape=out_shape,
        compiler_params=_params(("arbitrary", "arbitrary") if sums else ("parallel", "parallel")),
    )(a, b, *[r[0] for r in rows], *vecs)


def _rowwise(fn, row_ins, vec_ins, row_outs, sum_outs, name, tm=256):
    n_in = len(row_ins) + len(vec_ins)
    n_o = len(row_outs)
    rows = row_ins[0][0].shape[0]

    def body(*refs):
        vals = [r[...] for r in refs[:n_in]]
        outs = refs[n_in:]
        ro, so = fn(*vals)
        assert len(ro) == n_o and len(so) == len(sum_outs)
        for r, v in zip(outs[:n_o], ro):
            r[...] = v.astype(r.dtype)
        if sum_outs:
            @pl.when(pl.program_id(0) == 0)
            def _():
                for r in outs[n_o:]:
                    r[...] = jnp.zeros(r.shape, F32)
            for r, v in zip(outs[n_o:], so):
                r[...] += v

    in_specs = [pl.BlockSpec((tm, w), functools.partial(lambda i, b: (i, b), b=b)) for _, w, b in row_ins]
    in_specs += [pl.BlockSpec(v.shape, lambda i: (0, 0)) for v in vec_ins]
    out_specs = [pl.BlockSpec((tm, w), lambda i: (i, 0)) for w, _ in row_outs]
    out_specs += [pl.BlockSpec((1, w), lambda i: (0, 0)) for w in sum_outs]
    out_shape = [jax.ShapeDtypeStruct((rows, w), dt) for w, dt in row_outs]
    out_shape += [jax.ShapeDtypeStruct((1, w), F32) for w in sum_outs]
    return pl.pallas_call(
        body, name=name, grid=(rows // tm,),
        in_specs=in_specs, out_specs=out_specs, out_shape=out_shape,
        compiler_params=_params(("arbitrary",)),
    )(*[a for a, _, _ in row_ins], *vec_ins)


def _sigmoid(x):
    return 1.0 / (1.0 + jnp.exp(-x))


def _rstd(x):
    return lax.rsqrt(jnp.mean(x * x, axis=-1, keepdims=True) + NORM_EPS)


def _norm_bwd(dyn, xn, r):
    return r * (dyn - xn * jnp.mean(dyn * xn, axis=-1, keepdims=True))


def _colsum(x):
    return jnp.sum(x, axis=0, keepdims=True)


def _rope_tables(pos, invf):
    def fn(p, f):
        lane = lax.broadcasted_iota(jnp.int32, (1, LANES), 1)
        ang = p * f
        cs, sn = jnp.cos(ang), jnp.sin(ang)
        rot = (lane >= 64) & (lane < 96)
        ct = jnp.where(lane < 64, 1.0, jnp.where(rot, cs, 0.0))
        sa = jnp.where((lane >= 64) & (lane < 80), -sn, 0.0)
        sb = jnp.where((lane >= 80) & (lane < 96), sn, 0.0)
        return (ct, sa, sb), ()
    return _rowwise(fn, [(pos, 1, 0)], [invf], [(LANES, F32)] * 3, [], "rope_tables")


def _rope(x, ct, sa, sb):
    return x * ct + pltpu.roll(x, LANES - 16, 1) * sa + pltpu.roll(x, 16, 1) * sb


def _rope_t(x, ct, sa, sb):
    return x * ct - pltpu.roll(x, LANES - 16, 1) * sa - pltpu.roll(x, 16, 1) * sb


def _head_mask(width, hh):
    lane = lax.broadcasted_iota(jnp.int32, (1, width), 1)
    half = width // 2
    return (lane >= hh * half) & (lane < (hh + 1) * half)


ATT_PP = 2
ATT_CHAINS = [(a, hh) for a in range(ATT_PP) for hh in range(2)]
ATT_G = HEADS // (2 * ATT_PP)


def _pair(ref_or_val, a, width, rows=slice(None)):
    return ref_or_val[rows, a * width:(a + 1) * width]


def _attn_fwd(q, qo, k, ko, v, vo, dkp, scale, bias, name):
    T = ATT_T
    assert qo % ATT_PP == 0 and ko % ATT_PP == 0 and vo % ATT_PP == 0
    qo, ko, vo = qo // ATT_PP, ko // ATT_PP, vo // ATT_PP

    def body(*refs):
        if bias is not None:
            q_ref, k_ref, v_ref, b_ref, o_ref, lse_ref, s_scr = refs
        else:
            q_ref, k_ref, v_ref, o_ref, lse_ref, s_scr = refs
        i = pl.program_id(1)
        row = lax.broadcasted_iota(jnp.int32, (T, T), 0)
        col = lax.broadcasted_iota(jnp.int32, (T, T), 1)
        qms = []
        for a, hh in ATT_CHAINS:
            qb = _pair(q_ref, a, dkp)
            qms.append(jnp.where(_head_mask(dkp, hh), qb, jnp.zeros_like(qb)))

        def fold(t):
            return [t[:, c * LANES:(c + 1) * LANES] for c in range(T // LANES)]

        def run(nt):
            mls = [jnp.full((T, LANES), -jnp.inf, F32) for _ in ATT_CHAINS]
            for j in range(nt):
                ks = slice(j * T, (j + 1) * T)
                for ci, (a, hh) in enumerate(ATT_CHAINS):
                    s = lax.dot_general(qms[ci], _pair(k_ref, a, dkp, ks), NT, preferred_element_type=F32) * scale
                    if bias is not None:
                        s = s + b_ref[2 * a + hh, j]
                    if j == nt - 1:
                        s = jnp.where(row >= col, s, -jnp.inf)
                    s_scr[ci, j] = s
                    for part in fold(s):
                        mls[ci] = jnp.maximum(mls[ci], part)
            ms = [jnp.max(ml, axis=1, keepdims=True) for ml in mls]
            mbs = [jnp.broadcast_to(m, (T, LANES)) for m in ms]
            for a in range(ATT_PP):
                ls = [jnp.zeros((T, LANES), F32) for _ in range(2)]
                ps, vms = [], []
                for j in range(nt):
                    vb = _pair(v_ref, a, LANES, slice(j * T, (j + 1) * T))
                    for hh in range(2):
                        parts = [jnp.exp(part - mbs[2 * a + hh]) for part in fold(s_scr[2 * a + hh, j])]
                        for part in parts:
                            ls[hh] = ls[hh] + part
                        ps.append(jnp.concatenate(parts, axis=1).astype(BF16))
                        vms.append(jnp.where(_head_mask(LANES, hh), vb, jnp.zeros_like(vb)))
                acc = lax.dot_general(jnp.concatenate(ps, axis=1), jnp.concatenate(vms, axis=0), NN,
                                      preferred_element_type=F32)
                l0, l1 = [jnp.sum(l, axis=1, keepdims=True) for l in ls]
                lse_ref[2 * a] = ms[2 * a] + jnp.log(l0)
                lse_ref[2 * a + 1] = ms[2 * a + 1] + jnp.log(l1)
                inv = jnp.where(_head_mask(LANES, 0), 1.0 / l0, 1.0 / l1)
                o_ref[:, a * LANES:(a + 1) * LANES] = (acc * inv).astype(o_ref.dtype)

        for nt in range(1, N_ATT + 1):
            pl.when(i == nt - 1)(functools.partial(run, nt))

    in_specs = [
        pl.BlockSpec((T, ATT_PP * dkp), lambda g, i: (i, qo + g)),
        pl.BlockSpec((S, ATT_PP * dkp), lambda g, i: (0, ko + g)),
        pl.BlockSpec((S, ATT_PP * LANES), lambda g, i: (0, vo + g)),
    ]
    args = [q, k, v]
    if bias is not None:
        in_specs.append(pl.BlockSpec((2 * ATT_PP, N_ATT, 1, T), lambda g, i: (g, 0, 0, 0)))
        args.append(bias)
    return pl.pallas_call(
        body, name=name, grid=(ATT_G, N_ATT),
        in_specs=in_specs,
        out_specs=[pl.BlockSpec((T, ATT_PP * LANES), lambda g, i: (i, g)),
                   pl.BlockSpec((2 * ATT_PP, T, 1), lambda g, i: (g, i, 0))],
        out_shape=[jax.ShapeDtypeStruct((S, HEADS * HEAD_DIM), BF16),
                   jax.ShapeDtypeStruct((HEADS, S, 1), F32)],
        scratch_shapes=[pltpu.VMEM((len(ATT_CHAINS), N_ATT, T, T), F32)],
        compiler_params=_params(("parallel", "arbitrary")),
    )(*args)


def _attn_grad(q, qo, k, ko, v, vo, do, lse, dkp, scale, bias, qk_dtype, name):
    T = ATT_T
    has_b = bias is not None
    qo, ko, vo = qo // ATT_PP, ko // ATT_PP, vo // ATT_PP
    n_ch = len(ATT_CHAINS)

    def body(*refs):
        q_ref, k_ref, v_ref, do_ref, lse_ref = refs[:5]
        refs = refs[5:]
        if has_b:
            b_ref, refs = refs[0], refs[1:]
        dq_ref, dk_ref, dv_ref = refs[:3]
        refs = refs[3:]
        if has_b:
            db_ref, refs = refs[0], refs[1:]
        p_scr, dp_scr, dk_acc, dv_acc = refs[:4]
        db_acc = refs[4] if has_b else None
        i = pl.program_id(1)

        @pl.when(i == 0)
        def _():
            dk_acc[...] = jnp.zeros(dk_acc.shape, F32)
            dv_acc[...] = jnp.zeros(dv_acc.shape, F32)
            if has_b:
                db_acc[...] = jnp.zeros(db_acc.shape, F32)

        row = lax.broadcasted_iota(jnp.int32, (T, T), 0)
        col = lax.broadcasted_iota(jnp.int32, (T, T), 1)

        def fold(t):
            return [t[:, c * LANES:(c + 1) * LANES] for c in range(T // LANES)]

        qms, doms, lses = [], [], []
        for a, hh in ATT_CHAINS:
            qb, dob = _pair(q_ref, a, dkp), _pair(do_ref, a, LANES)
            qms.append(jnp.where(_head_mask(dkp, hh), qb, jnp.zeros_like(qb)))
            doms.append(jnp.where(_head_mask(LANES, hh), dob, jnp.zeros_like(dob)))
            lses.append(lse_ref[2 * a + hh])

        def run(nt):
            dls = [jnp.zeros((T, LANES), F32) for _ in ATT_CHAINS]
            for j in range(nt):
                ks = slice(j * T, (j + 1) * T)
                for ci, (a, hh) in enumerate(ATT_CHAINS):
                    s = lax.dot_general(qms[ci], _pair(k_ref, a, dkp, ks), NT, preferred_element_type=F32) * scale
                    if has_b:
                        s = s + b_ref[ci, j]
                    s = s - lses[ci]
                    if j == nt - 1:
                        s = jnp.where(row >= col, s, -jnp.inf)
                    p = jnp.exp(s)
                    dp = lax.dot_general(doms[ci], _pair(v_ref, a, LANES, ks), NT, preferred_element_type=F32)
                    p_scr[ci, j] = p
                    dp_scr[ci, j] = dp
                    for part in fold(p * dp):
                        dls[ci] = dls[ci] + part
            deltas = [jnp.broadcast_to(jnp.sum(dl, axis=1, keepdims=True), (T, LANES)) for dl in dls]
            for a in range(ATT_PP):
                ds_all, km_all = [], []
                qm2t = jnp.transpose(jnp.concatenate([qms[2 * a], qms[2 * a + 1]], axis=0))
                dom2t = jnp.transpose(jnp.concatenate([doms[2 * a], doms[2 * a + 1]], axis=0))
                for j in range(nt):
                    ks = slice(j * T, (j + 1) * T)
                    kb = _pair(k_ref, a, dkp, ks)
                    p2, ds2 = [], []
                    for hh in range(2):
                        ci = 2 * a + hh
                        p = p_scr[ci, j]
                        ds = jnp.concatenate([pp * (dd - deltas[ci]) for pp, dd in zip(fold(p), fold(dp_scr[ci, j]))], axis=1)
                        if has_b:
                            db_acc[ci, j] += jnp.sum(ds, axis=0, keepdims=True)
                        p2.append(p.astype(BF16))
                        ds2.append((ds * scale).astype(BF16))
                        km_all.append(jnp.where(_head_mask(dkp, hh), kb, jnp.zeros_like(kb)))
                    dv_acc[a * LANES:(a + 1) * LANES, ks] += lax.dot_general(
                        dom2t, jnp.concatenate(p2, axis=0), NN, preferred_element_type=F32)
                    dk_acc[a * dkp:(a + 1) * dkp, ks] += lax.dot_general(
                        qm2t, jnp.concatenate(ds2, axis=0), NN, preferred_element_type=F32)
                    ds_all += ds2
                dq = lax.dot_general(jnp.concatenate(ds_all, axis=1), jnp.concatenate(km_all, axis=0), NN,
                                     preferred_element_type=F32)
                dq_ref[:, a * dkp:(a + 1) * dkp] = dq.astype(dq_ref.dtype)

        for nt in range(1, N_ATT + 1):
            pl.when(i == nt - 1)(functools.partial(run, nt))

        @pl.when(i == N_ATT - 1)
        def _():
            dk_ref[...] = jnp.transpose(dk_acc[...]).astype(dk_ref.dtype)
            dv_ref[...] = jnp.transpose(dv_acc[...]).astype(dv_ref.dtype)
            if has_b:
                db_ref[...] = db_acc[...]

    in_specs = [
        pl.BlockSpec((T, ATT_PP * dkp), lambda g, i: (i, qo + g)),
        pl.BlockSpec((S, ATT_PP * dkp), lambda g, i: (0, ko + g)),
        pl.BlockSpec((S, ATT_PP * LANES), lambda g, i: (0, vo + g)),
        pl.BlockSpec((T, ATT_PP * LANES), lambda g, i: (i, g)),
        pl.BlockSpec((2 * ATT_PP, T, 1), lambda g, i: (g, i, 0)),
    ]
    args = [q, k, v, do, lse]
    out_specs = [
        pl.BlockSpec((T, ATT_PP * dkp), lambda g, i: (i, g)),
        pl.BlockSpec((S, ATT_PP * dkp), lambda g, i: (0, g)),
        pl.BlockSpec((S, ATT_PP * LANES), lambda g, i: (0, g)),
    ]
    width = (HEADS // 2) * dkp
    out_shape = [
        jax.ShapeDtypeStruct((S, width), qk_dtype),
        jax.ShapeDtypeStruct((S, width), qk_dtype),
        jax.ShapeDtypeStruct((S, HEADS * HEAD_DIM), BF16),
    ]
    scratch = [pltpu.VMEM((n_ch, N_ATT, T, T), F32), pltpu.VMEM((n_ch, N_ATT, T, T), F32),
               pltpu.VMEM((ATT_PP * dkp, S), F32), pltpu.VMEM((ATT_PP * LANES, S), F32)]
    if has_b:
        bspec = pl.BlockSpec((2 * ATT_PP, N_ATT, 1, T), lambda g, i: (g, 0, 0, 0))
        in_specs.append(bspec)
        args.append(bias)
        out_specs.append(bspec)
        out_shape.append(jax.ShapeDtypeStruct((HEADS, N_ATT, 1, T), F32))
        scratch.append(pltpu.VMEM((2 * ATT_PP, N_ATT, 1, T), F32))
    return pl.pallas_call(
        body, name=name, grid=(ATT_G, N_ATT),
        in_specs=in_specs, out_specs=out_specs, out_shape=out_shape, scratch_shapes=scratch,
        compiler_params=_params(("parallel", "arbitrary")),
    )(*args)


def _tri(upper):
    a = lax.broadcasted_iota(jnp.int32, (LANES, LANES), 0)
    b = lax.broadcasted_iota(jnp.int32, (LANES, LANES), 1)
    return jnp.where(a <= b if upper else a >= b, 1.0, 0.0).astype(F32)


def _fox_gates(zt, bf):
    def body(z_ref, b_ref, o_ref):
        tri = _tri(True)
        carry = jnp.zeros((HEADS, 1), F32)
        for t in range(S // LANES):
            sl = slice(t * LANES, (t + 1) * LANES)
            z = z_ref[:, sl] + b_ref[...]
            logf = jnp.minimum(z, 0.0) - jnp.log(1.0 + jnp.exp(-jnp.abs(z)))
            c = lax.dot_general(logf, tri, NN, preferred_element_type=F32,
                                precision=lax.Precision.HIGHEST) + carry
            o_ref[:, sl] = -c
            carry = c[:, LANES - 1:LANES]

    return pl.pallas_call(
        body, name="fox_gates", out_shape=jax.ShapeDtypeStruct((HEADS, S), F32),
        compiler_params=_params(),
    )(zt, bf)


def _fox_gates_bwd(dbias, zt, bf):
    def body(d_ref, z_ref, b_ref, dz_ref, dbf_ref):
        tri = _tri(False)
        carry = jnp.zeros((HEADS, 1), F32)
        tot = jnp.zeros((HEADS, 1), F32)
        for t in reversed(range(S // LANES)):
            sl = slice(t * LANES, (t + 1) * LANES)
            df = -d_ref[:, sl]
            c = lax.dot_general(df, tri, NN, preferred_element_type=F32,
                                precision=lax.Precision.HIGHEST) + carry
            carry = c[:, 0:1]
            z = z_ref[:, sl] + b_ref[...]
            dz = c * _sigmoid(-z)
            dz_ref[:, sl] = dz
            tot = tot + jnp.sum(dz, axis=1, keepdims=True)
        dbf_ref[...] = tot

    return pl.pallas_call(
        body, name="fox_gates_bwd",
        out_shape=[jax.ShapeDtypeStruct((HEADS, S), F32), jax.ShapeDtypeStruct((HEADS, 1), F32)],
        compiler_params=_params(),
    )(dbias, zt, bf)


def _mod_part(c_all, w_ada, b_cols):
    def body(c_ref, w_ref, b_ref, o_ref, s_ref):
        c = c_ref[...]
        sc = c * _sigmoid(c)
        s_ref[...] = sc
        o_ref[...] = lax.dot_general(sc, w_ref[...], NN, preferred_element_type=F32,
                                     precision=lax.Precision.HIGHEST) + b_ref[...]

    return pl.pallas_call(
        body, name="mod_part",
        out_shape=[jax.ShapeDtypeStruct((N_DEV, w_ada.shape[1]), F32), jax.ShapeDtypeStruct(c_all.shape, F32)],
        compiler_params=_params(),
    )(c_all, w_ada, b_cols)


def _w_ada_grad(sc_t, dm):
    def body(s_ref, d_ref, o_ref):
        acc = jnp.zeros(o_ref.shape, F32)
        for b in range(N_DEV):
            acc = acc + s_ref[:, b:b + 1] * d_ref[b:b + 1, :]
        o_ref[...] = acc

    return pl.pallas_call(
        body, name="w_ada_grad", out_shape=jax.ShapeDtypeStruct((sc_t.shape[0], dm.shape[1]), F32),
        compiler_params=_params(),
    )(sc_t, dm)


def _adamw(w, m, v, parts, name, own=None, slot=None):
    rows, cols = w.shape
    n = parts.shape[0]
    tr = rows if rows <= 512 else 256

    def body(*refs):
        if own is not None:
            s_ref, refs = refs[0], refs[1:]
            w_ref, m_ref, v_ref, p_ref, o_ref, g_out, d_out, m_out, v_out = refs
            terms = [jnp.where(s_ref[0] == kk, o_ref[0], p_ref[kk]) for kk in range(n)]
        else:
            w_ref, m_ref, v_ref, p_ref, g_out, d_out, m_out, v_out = refs
            terms = [p_ref[kk] for kk in range(n)]
        g = terms[0].astype(F32)
        for term in terms[1:]:
            g = g + term.astype(F32)
        g_out[...] = g
        d_out[...], m_out[...], v_out[...] = _adamw_math(w_ref[...], g, m_ref[...], v_ref[...])

    spec = pl.BlockSpec((tr, cols), lambda i, *_: (i, 0))
    in_specs = [spec, spec, spec, pl.BlockSpec((n, tr, cols), lambda i, *_: (0, i, 0))]
    out_shape = [jax.ShapeDtypeStruct((rows, cols), F32)] * 4
    if own is None:
        return pl.pallas_call(
            body, name=name, grid=(rows // tr,), in_specs=in_specs, out_specs=[spec] * 4, out_shape=out_shape,
            compiler_params=_params(("parallel",)),
        )(w, m, v, parts)
    in_specs.append(pl.BlockSpec((1, tr, cols), lambda i, s: (s[0], i, 0)))
    return pl.pallas_call(
        body, name=name, out_shape=out_shape, compiler_params=_params(("parallel",)),
        grid_spec=pltpu.PrefetchScalarGridSpec(num_scalar_prefetch=1, grid=(rows // tr,), in_specs=in_specs,
                                               out_specs=[spec] * 4),
    )(slot, w, m, v, parts, own)


def _adamw_math(w, g, m, v):
    mm = ADAM_B1 * m + (1.0 - ADAM_B1) * g
    vv = ADAM_B2 * v + (1.0 - ADAM_B2) * (g * g)
    m_hat = mm / (1.0 - ADAM_B1 ** ADAM_STEP)
    v_hat = vv / (1.0 - ADAM_B2 ** ADAM_STEP)
    return -ADAM_LR * (m_hat / (jnp.sqrt(v_hat) + ADAM_EPS) + ADAM_WD * w), mm, vv


def _adamw_rows(bundles, offsets, ws, ms, vs, err_off, err_width):
    k = len(ws)

    def body(*refs):
        b_ref = refs[0]
        w_refs, m_refs, v_refs = refs[1:1 + k], refs[1 + k:1 + 2 * k], refs[1 + 2 * k:1 + 3 * k]
        outs = refs[1 + 3 * k:]
        g_all = b_ref[0]
        for kk in range(1, N_DEV):
            g_all = g_all + b_ref[kk]
        for i in range(k):
            width = w_refs[i].shape[1]
            g = g_all[:, offsets[i]:offsets[i] + width]
            outs[4 * i][...] = g
            outs[4 * i + 1][...], outs[4 * i + 2][...], outs[4 * i + 3][...] = _adamw_math(
                w_refs[i][...], g, m_refs[i][...], v_refs[i][...])
        outs[4 * k][...] = g_all[:, err_off:err_off + err_width]

    out_shape = []
    for w_ in ws:
        out_shape += [jax.ShapeDtypeStruct(w_.shape, F32)] * 4
    out_shape.append(jax.ShapeDtypeStruct((1, err_width), F32))
    res = pl.pallas_call(body, name="adamw_rows", out_shape=out_shape, compiler_params=_params())(bundles, *ws, *ms, *vs)
    return [tuple(res[4 * i:4 * i + 4]) for i in range(k)], res[-1]


def _coords():
    return lax.axis_index("x"), lax.axis_index("y"), lax.axis_index("c")


def _flat(px, py, pc):
    return 4 * px + 2 * py + pc


def _all_gather(arrs, name):
    n = len(arrs)

    def body(*refs):
        ins, outs = refs[:n], refs[n:2 * n]
        send, recv, lsem = refs[2 * n:]
        x, y, c = _coords()
        me, sibling = (x, y, c), (x, y, 1 - c)
        chips = [(1 - x, y), (x, 1 - y), (1 - x, 1 - y)]

        def copy(a, kk, block, to, src=None):
            slot = outs[a].at[_flat(*block)]
            return pltpu.make_async_remote_copy(
                src_ref=slot if src is None else src, dst_ref=slot,
                send_sem=send.at[a, kk], recv_sem=recv.at[a, kk],
                device_id=to, device_id_type=MESH)

        mine = [pltpu.make_async_copy(ins[a], outs[a].at[_flat(*me)], lsem.at[a]) for a in range(n)]
        for cp in mine:
            cp.start()
        first = []
        for a in range(n):
            first.append(copy(a, 0, me, sibling, src=ins[a]))
            first += [copy(a, 1 + j, me, (*chip, c), src=ins[a]) for j, chip in enumerate(chips)]
        for cp in first:
            cp.start()
        passed = []
        for j, chip in enumerate(chips):
            for a in range(n):
                copy(a, 1 + j, (*chip, c), me).wait_recv()
                cp = copy(a, 4 + j, (*chip, c), sibling)
                cp.start()
                passed.append(cp)
        for a in range(n):
            copy(a, 0, sibling, me).wait_recv()
        for j, chip in enumerate(chips):
            for a in range(n):
                copy(a, 4 + j, (*chip, 1 - c), me).wait_recv()
        for cp in first + passed:
            cp.wait_send()
        for cp in mine:
            cp.wait()

    any_spec = pl.BlockSpec(memory_space=pl.ANY)
    return pl.pallas_call(
        body, name=name,
        in_specs=[any_spec] * n, out_specs=[any_spec] * n,
        out_shape=[jax.ShapeDtypeStruct((N_DEV,) + a.shape, a.dtype) for a in arrs],
        scratch_shapes=[pltpu.SemaphoreType.DMA((n, 7)), pltpu.SemaphoreType.DMA((n, 7)),
                        pltpu.SemaphoreType.DMA((n,))],
    )(*arrs)


def _peer_list():
    x, y, c = _coords()
    return [((1 - x if r & 4 else x), (1 - y if r & 2 else y), (1 - c if r & 1 else c)) for r in range(1, N_DEV)]


def _copy_plan(mode, src, land):
    x, y, c = _coords()
    me = _flat(x, y, c)
    if mode == "gather":
        return [(src, land.at[me], peer) for peer in _peer_list()]
    if mode == "exchange":
        return [(src.at[_flat(*peer)], land.at[me], peer) for peer in _peer_list()]
    if mode == "pair":
        return [(src.at[_flat(q // 2, q % 2, 1 - c)], land.at[q], (x, y, 1 - c)) for q in range(N_DEV // 2)]
    assert mode == "chips"
    plan = []
    for r in range(1, N_DEV // 2):
        qx, qy = (1 - x if r & 2 else x), (1 - y if r & 1 else y)
        plan.append((src.at[2 * qx + qy], land.at[2 * x + y], (qx, qy, c)))
    return plan


N_COPIES = dict(gather=N_DEV - 1, exchange=N_DEV - 1, pair=N_DEV // 2, chips=N_DEV // 2 - 1)


def _land_shape(mode, shape):
    return {"gather": (N_DEV,) + shape, "exchange": shape, "pair": (N_DEV // 2,) + shape[1:], "chips": shape}[mode]


HBM_SPEC = pl.BlockSpec(memory_space=pltpu.HBM)
SEM_SPEC = pl.BlockSpec(memory_space=pltpu.SEMAPHORE)
ANY_SPEC = pl.BlockSpec(memory_space=pl.ANY)
SIDE_EFFECT = pltpu.SideEffectType.DATAFLOW_SIDE_EFFECTING


def _async_start(groups, mode, after, name):
    flat_arrs = [a for g in groups for a in g]
    n = len(flat_arrs)

    def body(*refs):
        srcs, lands = refs[:n], refs[n:2 * n]
        outs = refs[2 * n + 1:]
        token = outs[-1]
        for ai in range(n):
            for src_ref, dst_ref, peer in _copy_plan(mode, srcs[ai], lands[ai]):
                pltpu.make_async_remote_copy(src_ref=src_ref, dst_ref=dst_ref, send_sem=outs[2 * ai],
                                             recv_sem=outs[2 * ai + 1], device_id=peer, device_id_type=MESH).start()
        token[...] = jnp.zeros(token.shape, F32)

    land_shapes = [_land_shape(mode, a.shape) for a in flat_arrs]
    out_shape = [pltpu.SemaphoreType.DMA(())] * (2 * n)
    out_shape += [pltpu.HBM(a.shape, a.dtype) for a in flat_arrs]
    out_shape += [pltpu.HBM(s, a.dtype) for s, a in zip(land_shapes, flat_arrs)]
    out_shape.append(jax.ShapeDtypeStruct((8, LANES), F32))
    res = pl.pallas_call(
        body, name=name, out_shape=tuple(out_shape),
        in_specs=[HBM_SPEC] * (2 * n) + [ANY_SPEC],
        out_specs=tuple([SEM_SPEC] * (2 * n) + [HBM_SPEC] * (2 * n) + [pl.BlockSpec(memory_space=pltpu.VMEM)]),
        input_output_aliases={i: 2 * n + i for i in range(2 * n)},
        compiler_params=pltpu.CompilerParams(has_side_effects=SIDE_EFFECT),
    )(*[pltpu.with_memory_space_constraint(a, pltpu.HBM) for a in flat_arrs],
      *[pltpu.with_memory_space_constraint(lax.empty(s, a.dtype), pltpu.HBM) for s, a in zip(land_shapes, flat_arrs)],
      after)
    sems, thru = res[:2 * n], res[2 * n:-1]
    states, idx = [], 0
    for g in groups:
        k = len(g)
        states.append((list(sems[2 * idx:2 * (idx + k):2]), list(sems[2 * idx + 1:2 * (idx + k):2]),
                       list(thru[idx:idx + k]), list(thru[n + idx:n + idx + k]), mode))
        idx += k
    return states, res[-1]


def _async_wait(state, after, name):
    sends, recvs, srcs, lands, mode = state
    g = len(srcs)

    def body(*refs):
        l_refs, sems = refs[g:2 * g], refs[2 * g:4 * g]
        for ai in range(g):
            moved = l_refs[ai].at[pl.ds(0, N_COPIES[mode])]
            cp = pltpu.make_async_remote_copy(src_ref=moved, dst_ref=moved, send_sem=sems[ai], recv_sem=sems[g + ai],
                                              device_id=_coords(), device_id_type=MESH)
            cp.wait_send()
            cp.wait_recv()

    res = pl.pallas_call(
        body, name=name,
        out_shape=tuple([pltpu.HBM(a.shape, a.dtype) for a in srcs] + [pltpu.HBM(a.shape, a.dtype) for a in lands]),
        in_specs=[HBM_SPEC] * (2 * g) + [SEM_SPEC] * (2 * g) + [ANY_SPEC],
        out_specs=tuple([HBM_SPEC] * (2 * g)),
        input_output_aliases={i: i for i in range(2 * g)},
        compiler_params=pltpu.CompilerParams(has_side_effects=SIDE_EFFECT),
    )(*srcs, *lands, *sends, *recvs, after)
    return list(res[:g]), list(res[g:])


def _add_blocks(a, b):
    def body(a_ref, b_ref, o_ref):
        o_ref[...] = (a_ref[...].astype(F32) + b_ref[...].astype(F32)).astype(o_ref.dtype)

    spec = pl.BlockSpec((1,) + a.shape[1:], lambda i: (i, 0, 0))
    return pl.pallas_call(
        body, name="add_blocks", grid=(a.shape[0],), in_specs=[spec, spec], out_specs=spec,
        out_shape=jax.ShapeDtypeStruct(a.shape, a.dtype), compiler_params=_params(("parallel",)),
    )(a, b)


def _with_own(land, own, me):
    return lax.dynamic_update_index_in_dim(land, own, me, 0)


IN_SPLITS = (512, 512, 512, 8, 768, 256, 32, 1024, 1024)


def _from_shards(g, fn, out_width, name, own=None, slot=None):
    _, k, n = g.shape
    tr = min(k, 256)

    def body(*refs):
        if own is not None:
            s_ref, g_ref, own_ref, o_ref = refs
            cols = [jnp.where(s_ref[0] == j, own_ref[...], g_ref[j]) for j in range(N_DEV)]
        else:
            g_ref, o_ref = refs
            cols = [g_ref[j] for j in range(N_DEV)]
        o_ref[...] = fn(jnp.concatenate(cols, axis=1))

    in_specs = [pl.BlockSpec((N_DEV, tr, n), lambda i, *_: (0, i, 0))]
    out_spec = pl.BlockSpec((tr, out_width), lambda i, *_: (i, 0))
    out_shape = jax.ShapeDtypeStruct((k, out_width), g.dtype)
    if own is None:
        return pl.pallas_call(body, name=name, grid=(k // tr,), in_specs=in_specs, out_specs=out_spec,
                              out_shape=out_shape, compiler_params=_params(("parallel",)))(g)
    in_specs.append(pl.BlockSpec((tr, n), lambda i, *_: (i, 0)))
    return pl.pallas_call(
        body, name=name, out_shape=out_shape, compiler_params=_params(("parallel",)),
        grid_spec=pltpu.PrefetchScalarGridSpec(num_scalar_prefetch=1, grid=(k // tr,), in_specs=in_specs, out_specs=out_spec),
    )(slot, g, own)


def _unshard_cols(g, own=None, slot=None):
    return _from_shards(g, lambda full: full, N_DEV * g.shape[2], "unshard_cols_%d" % g.shape[2], own, slot)


FFN_T = 256
FFN_SHARD = 2 * D_FF // N_DEV


def _unshard_ffn_in(g, own=None, slot=None):
    def pairs(full):
        parts = []
        for j in range(D_FF // FFN_T):
            parts += [full[:, j * FFN_T:(j + 1) * FFN_T], full[:, D_FF + j * FFN_T:D_FF + (j + 1) * FFN_T]]
        return jnp.concatenate(parts, axis=1)

    return _from_shards(g, pairs, 2 * D_FF, "unshard_ffn_in", own, slot)


def _shard_ffn_in(w):
    tr = 256

    def body(w_ref, o_ref):
        x = w_ref[...]
        nb = D_FF // FFN_T
        full = jnp.concatenate([x[:, (2 * j + half) * FFN_T:(2 * j + half + 1) * FFN_T]
                                for half in range(2) for j in range(nb)], axis=1)
        for j in range(N_DEV):
            o_ref[j] = full[:, j * FFN_SHARD:(j + 1) * FFN_SHARD]

    return pl.pallas_call(
        body, name="shard_ffn_in", grid=(D // tr,),
        in_specs=[pl.BlockSpec((tr, 2 * D_FF), lambda i: (i, 0))],
        out_specs=pl.BlockSpec((N_DEV, tr, FFN_SHARD), lambda i: (0, i, 0)),
        out_shape=jax.ShapeDtypeStruct((N_DEV, D, FFN_SHARD), w.dtype),
        compiler_params=_params(("parallel",)),
    )(w)


def _shard_cols(w):
    k, n = w.shape[0], w.shape[1] // N_DEV
    tr = min(k, 256)

    def body(w_ref, o_ref):
        full = w_ref[...]
        for j in range(N_DEV):
            o_ref[j] = full[:, j * n:(j + 1) * n]

    return pl.pallas_call(
        body, name="shard_cols_%d" % n, grid=(k // tr,),
        in_specs=[pl.BlockSpec((tr, N_DEV * n), lambda i: (i, 0))],
        out_specs=pl.BlockSpec((N_DEV, tr, n), lambda i: (0, i, 0)),
        out_shape=jax.ShapeDtypeStruct((N_DEV, k, n), w.dtype),
        compiler_params=_params(("parallel",)),
    )(w)


IN_OFFS = tuple(sum(IN_SPLITS[:i]) for i in range(len(IN_SPLITS) + 1))
IN_SHARD = IN_OFFS[-1] // N_DEV
REGROUP_ROWS = 128


def _w_in_regroup(g):
    def body(g_ref, a_ref, b_ref):
        full = jnp.concatenate([g_ref[j] for j in range(N_DEV)], axis=1)
        fq, fk, fv, wf, cq, ckv, kr, gf, gm = [full[:, IN_OFFS[i]:IN_OFFS[i + 1]] for i in range(9)]
        rows = full.shape[0]
        a_ref[...] = jnp.concatenate([cq, ckv, gf, gm, wf, jnp.zeros((rows, 56), BF16), kr, jnp.zeros((rows, 32), BF16)], axis=1)
        b_ref[...] = jnp.concatenate([fq, fk, fv], axis=1)

    tr = REGROUP_ROWS
    return pl.pallas_call(
        body, name="w_in_regroup", grid=(D // tr,),
        in_specs=[pl.BlockSpec((N_DEV, tr, IN_SHARD), lambda i: (0, i, 0))],
        out_specs=[pl.BlockSpec((tr, 3200), lambda i: (i, 0)), pl.BlockSpec((tr, 1536), lambda i: (i, 0))],
        out_shape=[jax.ShapeDtypeStruct((D, 3200), BF16), jax.ShapeDtypeStruct((D, 1536), BF16)],
        compiler_params=_params(("parallel",)),
    )(g)


def _w_in_ungroup(da, db_):
    def body(a_ref, b_ref, o_ref):
        a = a_ref[...]
        full = jnp.concatenate([b_ref[...], a[:, 3072:3080], a[:, 0:768], a[:, 768:1024], a[:, 3136:3168],
                                a[:, 1024:3072]], axis=1)
        for j in range(N_DEV):
            o_ref[j] = full[:, j * IN_SHARD:(j + 1) * IN_SHARD]

    tr = REGROUP_ROWS
    return pl.pallas_call(
        body, name="w_in_ungroup", grid=(D // tr,),
        in_specs=[pl.BlockSpec((tr, 3200), lambda i: (i, 0)), pl.BlockSpec((tr, 1536), lambda i: (i, 0))],
        out_specs=pl.BlockSpec((N_DEV, tr, IN_SHARD), lambda i: (0, i, 0)),
        out_shape=jax.ShapeDtypeStruct((N_DEV, D, IN_SHARD), BF16),
        compiler_params=_params(("parallel",)),
    )(da, db_)


def _prepare_weights(g, own=None, slot=None):
    w = {}
    if own is not None:
        small = ("w_uq", "w_ukv", "w_out", "w_ffn_out")
        g = {n: (_with_own(a, own[n], slot[0]) if n in small else a) for n, a in g.items()}
    pick = (lambda n: (own[n], slot)) if own is not None else (lambda n: (None, None))
    if "w_in" in g:
        w["w_a"], w["w_b"] = _w_in_regroup(g["w_in"])
    if "w_uq" in g:
        w_uq = g["w_uq"].reshape(Q_LORA, HEADS, 96)
        w["w_uq"] = jnp.pad(w_uq, ((0, 0), (0, 0), (0, 32))).reshape(Q_LORA, HEADS * LANES)
        ukv = g["w_ukv"]
        w["w_k"] = jnp.transpose(jnp.pad(ukv[:, :, :64], ((0, 0), (0, 0), (0, 64))), (1, 0, 2)).reshape(KV_LORA, HEADS * LANES)
        w["w_v"] = jnp.transpose(ukv[:, :, 64:], (1, 0, 2)).reshape(KV_LORA, HEADS * HEAD_DIM)
        w["w_kv"] = jnp.concatenate([w["w_k"], w["w_v"]], axis=1)
    if "w_out" in g:
        w["w_pf"] = _unshard_cols(g["w_proj_fox"], *pick("w_proj_fox"))
        w["w_pm"] = _unshard_cols(g["w_proj_mla"], *pick("w_proj_mla"))
        w["w_out"] = g["w_out"].reshape(D, D)
    if "w_ffn_in" in g:
        w["w_ffn_in"] = _unshard_ffn_in(g["w_ffn_in"], *pick("w_ffn_in"))
        w["w_ffn_out"] = g["w_ffn_out"].reshape(D_FF, D)
    return w


def _shard_grads(dw):
    out = {}
    if "w_a" in dw:
        out["w_in"] = _w_in_ungroup(dw["w_a"], dw["w_b"])
    if "w_uq" in dw:
        w_uq = dw["w_uq"].reshape(Q_LORA, HEADS, LANES)[:, :, :96].reshape(Q_LORA, Q_LORA)
        out["w_uq"] = w_uq.reshape(N_DEV, Q_LORA // N_DEV, Q_LORA)
        k_part = dw["w_k"].reshape(KV_LORA, HEADS, LANES)[:, :, :64]
        v_part = dw["w_v"].reshape(KV_LORA, HEADS, HEAD_DIM)
        out["w_ukv"] = jnp.transpose(jnp.concatenate([k_part, v_part], axis=2), (1, 0, 2))
    if "w_out" in dw:
        out["w_proj_fox"] = _shard_cols(dw["w_pf"])
        out["w_proj_mla"] = _shard_cols(dw["w_pm"])
        out["w_out"] = dw["w_out"].reshape(N_DEV, D // N_DEV, D)
    if "w_ffn_in" in dw:
        out["w_ffn_in"] = _shard_ffn_in(dw["w_ffn_in"])
        out["w_ffn_out"] = dw["w_ffn_out"].reshape(N_DEV, D_FF // N_DEV, D)
    return out


def _fwd_bwd(x, pos, mod, target, w, vec, wts, send, relay):
    shift_mix, scale_mix, gate_mix, shift_ffn, scale_ffn, gate_ffn = [mod[:, i * D:(i + 1) * D] for i in range(6)]
    g_pre_mix, g_post_mix, g_pre_ffn, g_post_ffn = vec["g_pre_mix"], vec["g_post_mix"], vec["g_pre_ffn"], vec["g_post_ffn"]
    g_q, g_kv = vec["g_q_lora"], vec["g_kv_lora"]

    inv_freq = 1.0 / (ROPE_THETA ** (jnp.arange(0, ROPE_DIM, 2, dtype=F32) / ROPE_DIM))
    invf = jnp.concatenate([jnp.zeros((64,), F32), inv_freq, inv_freq, jnp.zeros((32,), F32)]).reshape(1, LANES)
    ct, sa, sb = _rope_tables(pos, invf)

    def pre1(xv, g, sc, sh):
        return ((xv * _rstd(xv) * g) * (1.0 + sc) + sh,), ()
    (h,) = _rowwise(pre1, [(x, D, 0)], [g_pre_mix, scale_mix, shift_mix], [(D, BF16)], [], "pre_mix")
    proj_a = _mm(h, w["w_a"], "nn", F32, "in_proj_a")
    qkv = _mm(h, w["w_b"], "nn", BF16, "in_proj_b")

    def lora_norm(cq, ckv, gq, gkv):
        return (cq * _rstd(cq) * gq, ckv * _rstd(ckv) * gkv), ()
    cqn, ckvn = _rowwise(lora_norm, [(proj_a, Q_LORA, 0), (proj_a, KV_LORA, 3)], [g_q, g_kv],
                         [(Q_LORA, BF16), (KV_LORA, BF16)], [], "lora_norm")
    w = {**w, **wts("lora", cqn)}
    qb = _mm(cqn, w["w_uq"], "nn", F32, "mla_uq")
    kvb = _mm(ckvn, w["w_kv"], "nn", F32, "mla_ukv")

    def mla_rope(qv, kv, vv, misc, c_, a_, b_):
        lane = lax.broadcasted_iota(jnp.int32, (1, LANES), 1)
        kpe = jnp.where((lane >= 64) & (lane < 96), _rope(misc, c_, a_, b_), 0.0)
        qs = [_rope(qv[:, hd * LANES:(hd + 1) * LANES], c_, a_, b_) for hd in range(HEADS)]
        ks = [kv[:, hd * LANES:(hd + 1) * LANES] + kpe for hd in range(HEADS)]
        return (jnp.concatenate(qs, axis=1), jnp.concatenate(ks, axis=1), vv), ()
    q_m, k_m, v_m = _rowwise(
        mla_rope, [(qb, D, 0), (kvb, D, 0), (kvb, 512, 2), (proj_a, LANES, 24), (ct, LANES, 0), (sa, LANES, 0), (sb, LANES, 0)],
        [], [(D, BF16), (D, BF16), (512, BF16)], [], "mla_rope")

    zt = jnp.transpose(proj_a[:, 3072:3080])
    bf = jnp.transpose(vec["b_forget"])
    neg_f = _fox_gates(zt, bf)
    bias = neg_f.reshape(HEADS, N_ATT, 1, ATT_T)
    o_a, lse_a = _attn_fwd(qkv, 0, qkv, 4, qkv, 8, LANES, 1.0 / math.sqrt(HEAD_DIM), bias, "fox_attn")
    o_b, lse_b = _attn_fwd(q_m, 0, k_m, 0, v_m, 0, 2 * LANES, 1.0 / math.sqrt(64 + ROPE_DIM), None, "mla_attn")

    w = {**w, **wts("proj", o_b)}
    pa = _mm(o_a, w["w_pf"], "nn", BF16, "proj_fox")

    def merge(pb_, gf, gm, pa_):
        return (_sigmoid(gf) * pa_ + _sigmoid(gm) * pb_, pb_), ()
    merged, pb = _mm_epi(o_b, w["w_pm"], "nn", 512, merge, "proj_mla", 1024,
                         rows=[(proj_a, 512, 2), (proj_a, 512, 4), (pa, 512)], outs=[(D, 512, BF16), (D, 512, BF16)])
    def post1(yv, xv, gate, gpost, gpre, sc, sh):
        x1 = xv + gate * (yv * _rstd(yv) * gpost)
        return (x1, (x1 * _rstd(x1) * gpre) * (1.0 + sc) + sh, yv), ()
    x1, h2, y = _mm_epi(merged, w["w_out"], "nn", D, post1, "out_proj", 256, rows=[(x, D)],
                        vecs=[gate_mix, g_post_mix, g_pre_ffn, scale_ffn, shift_ffn],
                        outs=[(D, D, F32), (D, D, BF16), (D, D, F32)])
    w = {**w, **wts("ffn", h2)}

    def swiglu(r):
        g, u = r[:, :FFN_T], r[:, FFN_T:]
        return (g * _sigmoid(g) * u, r), ()
    act, gu = _mm_epi(h2, w["w_ffn_in"], "nn", 2 * FFN_T, swiglu, "ffn_in", 1024,
                      outs=[(D_FF, FFN_T, BF16), (2 * D_FF, 2 * FFN_T, BF16)])

    def head(y2v, x1v, tv, gate, gpost):
        r = _rstd(y2v)
        yn = y2v * r
        n2 = yn * gpost
        err = (x1v + gate * n2) - tv
        dx2 = err * (1.0 / D)
        dn2 = dx2 * gate
        dy2 = _norm_bwd(dn2 * gpost, yn, r)
        return (dx2, dy2), (_colsum(err * err), _colsum(dx2 * n2), _colsum(dn2 * yn))
    dx2, dy2, err_cols, d_gate_ffn, d_g_post_ffn = _mm_epi(
        act, w["w_ffn_out"], "nn", D, head, "ffn_out", 256, rows=[(x1, D), (target, D)], vecs=[gate_ffn, g_post_ffn],
        outs=[(D, D, F32), (D, D, BF16)], sums=[D, D, D])

    def swiglu_bwd(da, guv):
        g, u = guv[:, :FFN_T].astype(F32), guv[:, FFN_T:].astype(F32)
        sg = _sigmoid(g)
        return (jnp.concatenate([da * u * (sg * (1.0 + g * (1.0 - sg))), da * (g * sg)], axis=1),), ()
    (dgu,) = _mm_epi(dy2, w["w_ffn_out"], "nt", FFN_T, swiglu_bwd, "ffn_out_dx", 1024, rows=[(gu, 2 * FFN_T)],
                     outs=[(2 * D_FF, 2 * FFN_T, BF16)])
    dw = {"w_ffn_out": _mm(act, dy2, "tn", BF16, "ffn_out_dw")}
    dw["w_ffn_in"] = _mm(h2, dgu, "tn", BF16, "ffn_in_dw")
    gate_mix = gate_mix + send({n: dw.pop(n) for n in ("w_ffn_in", "w_ffn_out")})[0, 0]

    def mid(dh, x1v, dx2v, yv, gpre, sc, gate, gpost):
        r2 = _rstd(x1v)
        x1n = x1v * r2
        t = dh * x1n
        dx1 = dx2v + _norm_bwd(dh * (gpre * (1.0 + sc)), x1n, r2)
        ry = _rstd(yv)
        yn = yv * ry
        dn1 = dx1 * gate
        dy = _norm_bwd(dn1 * gpost, yn, ry)
        sums = (_colsum(dh), _colsum(t) * gpre, _colsum(t) * (1.0 + sc), _colsum(dx1 * (yn * gpost)), _colsum(dn1 * yn))
        return (dx1, dy), sums
    dx1, dy, d_shift_ffn, d_scale_ffn, d_g_pre_ffn, d_gate_mix, d_g_post_mix = _mm_epi(
        dgu, w["w_ffn_in"], "nt", D, mid, "ffn_in_dx", 256, rows=[(x1, D), (dx2, D), (y, D)],
        vecs=[g_pre_ffn, scale_ffn, gate_mix, g_post_mix], outs=[(D, D, F32), (D, D, BF16)], sums=[D] * 5)

    dw["w_out"] = _mm(merged, dy, "tn", BF16, "out_proj_dw")

    def merge_bwd(dm, gf, gm, pa_, pb_):
        sf, sm = _sigmoid(gf), _sigmoid(gm)
        return (dm * sf, dm * sm, dm * pa_ * (sf * (1.0 - sf)), dm * pb_ * (sm * (1.0 - sm))), ()
    dpa, dpb, dgf, dgm = _mm_epi(dy, w["w_out"], "nt", 512, merge_bwd, "out_proj_dx", 1024,
                                 rows=[(proj_a, 512, 2), (proj_a, 512, 4), (pa, 512), (pb, 512)],
                                 outs=[(D, 512, BF16)] * 4)
    do_a = _mm(dpa, w["w_pf"], "nt", BF16, "proj_fox_dx")
    do_b = _mm(dpb, w["w_pm"], "nt", BF16, "proj_mla_dx")
    dw["w_pf"] = _mm(o_a, dpa, "tn", BF16, "proj_fox_dw")
    dw["w_pm"] = _mm(o_b, dpb, "tn", BF16, "proj_mla_dw")
    bias = bias + send({n: dw.pop(n) for n in ("w_out", "w_pf", "w_pm")})[0, 0]

    sc_a, sc_b = 1.0 / math.sqrt(HEAD_DIM), 1.0 / math.sqrt(64 + ROPE_DIM)
    dq_a, dk_a, dv_a, dbias = _attn_grad(qkv, 0, qkv, 4, qkv, 8, do_a, lse_a, LANES, sc_a, bias, BF16, "fox_attn_bwd")
    dq_m, dk_m, dv_m = _attn_grad(q_m, 0, k_m, 0, v_m, 0, do_b, lse_b, 2 * LANES, sc_b, None, F32, "mla_attn_bwd")

    def mla_rope_bwd(dq, dk, c_, a_, b_):
        lane = lax.broadcasted_iota(jnp.int32, (1, LANES), 1)
        dqs = [_rope_t(dq[:, hd * LANES:(hd + 1) * LANES], c_, a_, b_) for hd in range(HEADS)]
        dkpe = dk[:, 0:LANES]
        for hd in range(1, HEADS):
            dkpe = dkpe + dk[:, hd * LANES:(hd + 1) * LANES]
        dkpe = jnp.where((lane >= 64) & (lane < 96), dkpe, 0.0)
        dkr = jnp.where((lane >= 64) & (lane < 96), _rope_t(dkpe, c_, a_, b_), 0.0)
        return (jnp.concatenate(dqs, axis=1), dk, dkr), ()
    dqb, dkb, dkr = _rowwise(mla_rope_bwd, [(dq_m, D, 0), (dk_m, D, 0), (ct, LANES, 0), (sa, LANES, 0), (sb, LANES, 0)],
                             [], [(D, BF16), (D, BF16), (LANES, F32)], [], "mla_rope_bwd")
    def lora_q_bwd(dq, cq, gq):
        rq = _rstd(cq)
        cqh = cq * rq
        return (_norm_bwd(dq * gq, cqh, rq),), (_colsum(dq * cqh),)
    dcq, d_g_q = _mm_epi(dqb, w["w_uq"], "nt", Q_LORA, lora_q_bwd, "mla_uq_dx", 512, rows=[(proj_a, Q_LORA, 0)],
                         vecs=[g_q], outs=[(Q_LORA, Q_LORA, BF16)], sums=[Q_LORA])
    dw["w_uq"] = _mm(cqn, dqb, "tn", BF16, "mla_uq_dw")

    def lora_kv_bwd(dv_part, dk_part, ckv, gkv):
        dkv = dv_part + dk_part
        rk = _rstd(ckv)
        ckh = ckv * rk
        return (_norm_bwd(dkv * gkv, ckh, rk),), (_colsum(dkv * ckh),)
    dckv, d_g_kv = _mm_epi(dv_m, w["w_v"], "nt", KV_LORA, lora_kv_bwd, "mla_uv_dx", 1024,
                           rows=[(_mm(dkb, w["w_k"], "nt", F32, "mla_uk_dx"), KV_LORA), (proj_a, KV_LORA, 3)],
                           vecs=[g_kv], outs=[(KV_LORA, KV_LORA, BF16)], sums=[KV_LORA])
    dw["w_k"] = _mm(ckvn, dkb, "tn", BF16, "mla_uk_dw")
    dw["w_v"] = _mm(ckvn, dv_m, "tn", BF16, "mla_uv_dw")

    dzt, d_bf = _fox_gates_bwd(dbias.reshape(HEADS, S), zt, bf)
    dmisc = (dkr + jnp.pad(jnp.transpose(dzt), ((0, 0), (0, LANES - HEADS)))).astype(BF16)
    dproj_a = jnp.concatenate([dcq, dckv, dgf, dgm, dmisc], axis=1)
    dqkv = jnp.concatenate([dq_a, dk_a, dv_a], axis=1)
    dw["w_a"] = _mm(h, dproj_a, "tn", BF16, "in_proj_a_dw")
    dw["w_b"] = _mm(h, dqkv, "tn", BF16, "in_proj_b_dw")
    tok = send(dw, True)
    dh_a = _mm(dproj_a, w["w_a"], "nt", F32, "in_proj_a_dx", dep=tok)
    g_pre_mix = g_pre_mix + relay(dh_a)[0, 0]

    def first(dh_b, dh_a, xv, dx1v, gpre, sc):
        dhv = dh_b + dh_a
        r = _rstd(xv)
        xn = xv * r
        t = dhv * xn
        dx = dx1v + _norm_bwd(dhv * (gpre * (1.0 + sc)), xn, r)
        return (dx,), (_colsum(dhv), _colsum(t) * gpre, _colsum(t) * (1.0 + sc))
    grad_x, d_shift_mix, d_scale_mix, d_g_pre_mix = _mm_epi(
        dqkv, w["w_b"], "nt", D, first, "in_proj_b_dx", 256,
        rows=[(dh_a, D), (x, D), (dx1, D)],
        vecs=[g_pre_mix, scale_mix], outs=[(D, D, F32)], sums=[D] * 3)

    dmod = jnp.concatenate([d_shift_mix, d_scale_mix, d_gate_mix, d_shift_ffn, d_scale_ffn, d_gate_ffn], axis=1)
    small = dict(dmod=dmod, g_pre_mix=d_g_pre_mix, g_post_mix=d_g_post_mix, g_pre_ffn=d_g_pre_ffn,
                 g_post_ffn=d_g_post_ffn, g_q_lora=d_g_q, g_kv_lora=d_g_kv,
                 b_forget=jnp.pad(jnp.transpose(d_bf), ((0, 0), (0, LANES - HEADS))), err=err_cols)
    return grad_x, small


SMALL_ORDER = ("dmod", "g_pre_mix", "g_post_mix", "g_pre_ffn", "g_post_ffn", "g_q_lora", "g_kv_lora", "b_forget", "err")
SMALL_PARAM = {"dmod": "b_ada"}
MATRICES = ("w_in", "w_uq", "w_ukv", "w_proj_fox", "w_proj_mla", "w_out", "w_ffn_in", "w_ffn_out")
WEIGHTS = ("w_ada", "b_ada", "g_pre_mix", "g_post_mix", "g_pre_ffn", "g_post_ffn", "w_in", "b_forget", "g_q_lora",
           "w_uq", "g_kv_lora", "w_ukv", "w_proj_fox", "w_proj_mla", "w_out", "w_ffn_in", "w_ffn_out")


def kernel(x, c, positions, w_ada, b_ada, g_pre_mix, g_post_mix, g_pre_ffn, g_post_ffn, w_in, b_forget, g_q_lora, w_uq, g_kv_lora, w_ukv, w_proj_fox, w_proj_mla, w_out, w_ffn_in, w_ffn_out, loss_target, m_w_ada, m_b_ada, m_g_pre_mix, m_g_post_mix, m_g_pre_ffn, m_g_post_ffn, m_w_in, m_b_forget, m_g_q_lora, m_w_uq, m_g_kv_lora, m_w_ukv, m_w_proj_fox, m_w_proj_mla, m_w_out, m_w_ffn_in, m_w_ffn_out, v_w_ada, v_b_ada, v_g_pre_mix, v_g_post_mix, v_g_pre_ffn, v_g_post_ffn, v_w_in, v_b_forget, v_g_q_lora, v_w_uq, v_g_kv_lora, v_w_ukv, v_w_proj_fox, v_w_proj_mla, v_w_out, v_w_ffn_in, v_w_ffn_out):
    prm = dict(w_ada=w_ada, b_ada=b_ada, g_pre_mix=g_pre_mix, g_post_mix=g_post_mix, g_pre_ffn=g_pre_ffn,
               g_post_ffn=g_post_ffn, w_in=w_in, b_forget=b_forget, g_q_lora=g_q_lora, w_uq=w_uq, g_kv_lora=g_kv_lora,
               w_ukv=w_ukv, w_proj_fox=w_proj_fox, w_proj_mla=w_proj_mla, w_out=w_out, w_ffn_in=w_ffn_in, w_ffn_out=w_ffn_out)
    mom = dict(w_ada=m_w_ada, b_ada=m_b_ada, g_pre_mix=m_g_pre_mix, g_post_mix=m_g_post_mix, g_pre_ffn=m_g_pre_ffn,
               g_post_ffn=m_g_post_ffn, w_in=m_w_in, b_forget=m_b_forget, g_q_lora=m_g_q_lora, w_uq=m_w_uq,
               g_kv_lora=m_g_kv_lora, w_ukv=m_w_ukv, w_proj_fox=m_w_proj_fox, w_proj_mla=m_w_proj_mla, w_out=m_w_out,
               w_ffn_in=m_w_ffn_in, w_ffn_out=m_w_ffn_out)
    var = dict(w_ada=v_w_ada, b_ada=v_b_ada, g_pre_mix=v_g_pre_mix, g_post_mix=v_g_post_mix, g_pre_ffn=v_g_pre_ffn,
               g_post_ffn=v_g_post_ffn, w_in=v_w_in, b_forget=v_b_forget, g_q_lora=v_g_q_lora, w_uq=v_w_uq,
               g_kv_lora=v_g_kv_lora, w_ukv=v_w_ukv, w_proj_fox=v_w_proj_fox, w_proj_mla=v_w_proj_mla, w_out=v_w_out,
               w_ffn_in=v_w_ffn_in, w_ffn_out=v_w_ffn_out)
    me = _flat(*_coords())
    slot = jnp.reshape(me, (1,)).astype(jnp.int32)

    own = {n: prm[n][0].astype(BF16) for n in MATRICES}
    w_in_all, c_all = _all_gather([own["w_in"], c], "gather_in")
    w = _prepare_weights({"w_in": w_in_all})
    c_all = c_all.reshape(N_DEV, D)
    later = dict(lora=("w_uq", "w_ukv"), proj=("w_proj_fox", "w_proj_mla", "w_out"), ffn=("w_ffn_in", "w_ffn_out"))
    states, tok = _async_start([[own[n] for n in names] for names in later.values()], "gather", w_in_all, "gather_rest_start")
    gather_state = dict(zip(later, states))

    def wts(group, after):
        srcs, lands = _async_wait(gather_state[group], after, "gather_" + group + "_wait")
        return _prepare_weights(dict(zip(later[group], lands)), dict(zip(later[group], srcs)), slot)

    sent, last = [], {}
    no_dep = jnp.zeros((8, LANES), F32)

    def send(grads, final=False):
        shards = _shard_grads(grads)
        names = list(shards)
        (state,), t = _async_start([[shards[n] for n in names]], "pair" if final else "exchange", no_dep,
                                   "exchange_" + names[0] + "_start")
        if final:
            last.update(names=names, state=state)
        else:
            sent.append((names, state))
        return t

    def relay(after):
        srcs, lands = _async_wait(last["state"], after, "exchange_pair_wait")
        sums = []
        for src, land in zip(srcs, lands):
            by_chip = src.reshape((N_DEV // 2, 2) + src.shape[1:])
            sums.append(_add_blocks(lax.dynamic_index_in_dim(by_chip, lax.axis_index("c"), 1, keepdims=False), land))
        (last["state"],), t = _async_start([sums], "chips", no_dep, "exchange_chips_start")
        return t

    ada_cols = w_ada.shape[2]
    b_cols = lax.dynamic_slice(b_ada, (0, me * ada_cols), (1, ada_cols))
    mod_cols, silu_c = _mod_part(c_all, w_ada[0], b_cols)
    (mod_all,) = _all_gather([mod_cols], "gather_mod")
    mod = lax.dynamic_index_in_dim(mod_all, me, axis=1, keepdims=False).reshape(1, 6 * D) + tok[0, 0]

    vec = dict(g_pre_mix=g_pre_mix, g_post_mix=g_post_mix, g_pre_ffn=g_pre_ffn, g_post_ffn=g_post_ffn,
               g_q_lora=g_q_lora, g_kv_lora=g_kv_lora, b_forget=b_forget)
    pos = positions.astype(F32).reshape(S, 1)
    grad_x, small = _fwd_bwd(x[0], pos, mod, loss_target[0], w, vec, wts, send, relay)

    bundle = jnp.concatenate([small[n] for n in SMALL_ORDER], axis=1)
    (small_state,), tok = _async_start([[bundle]], "gather", jnp.zeros((8, LANES), F32), "gather_small_start")

    out = {}
    after = tok
    for names, state in sent:
        srcs, lands = _async_wait(state, after, "exchange_" + names[0] + "_wait")
        for n, src, land in zip(names, srcs, lands):
            out[n] = _adamw(prm[n][0], mom[n][0], var[n][0], land, "adamw_" + n, src, slot)
            after = out[n][0]
    srcs, lands = _async_wait(last["state"], after, "exchange_chips_wait")
    for n, src, land in zip(last["names"], srcs, lands):
        out[n] = _adamw(prm[n][0], mom[n][0], var[n][0], land, "adamw_" + n, src, slot // 2)
        after = out[n][0]

    (own_bundle,), (bundle_all,) = _async_wait(small_state, after, "gather_small_wait")
    bundle_all = _with_own(bundle_all, own_bundle, me)
    dmod_all = bundle_all[:, 0, :6 * D]
    dm_cols = lax.dynamic_slice(dmod_all, (0, me * ada_cols), (N_DEV, ada_cols))
    g_ada = _w_ada_grad(jnp.transpose(silu_c), dm_cols)
    out["w_ada"] = _adamw(w_ada[0], m_w_ada[0], v_w_ada[0], g_ada[None], "adamw_w_ada")

    offsets, off = {}, 0
    for n in SMALL_ORDER:
        offsets[n] = off
        off += small[n].shape[1]
    names = [SMALL_PARAM.get(n, n) for n in SMALL_ORDER if n != "err"]
    results, err = _adamw_rows(bundle_all, [offsets[n] for n in SMALL_ORDER if n != "err"],
                               [prm[n] for n in names], [mom[n] for n in names], [var[n] for n in names],
                               offsets["err"], D)
    out.update(zip(names, results))
    loss = 0.5 * jnp.sum(err) / D

    res = [loss, grad_x[None]]
    for kind in range(4):
        for n in WEIGHTS:
            t = out[n][kind]
            res.append(t[None] if prm[n].ndim == 3 else t)
    return tuple(res)
```

```python
import functools
import math

import jax
import jax.numpy as jnp
from jax import lax
from jax.experimental import pallas as pl
from jax.experimental.pallas import tpu as pltpu

F32 = jnp.float32
BF16 = jnp.bfloat16

N_DEV = 8
S = 2048
D = 1024
D_FF = 2816
HEADS = 8
HEAD_DIM = 64
Q_LORA = 768
KV_LORA = 256
ROPE_DIM = 32
ROPE_THETA = 10000.0
NORM_EPS = 1e-6
LANES = 128
VMEM_LIMIT = 56 * 1024 * 1024

ADAM_LR = 0.001
ADAM_B1 = 0.9
ADAM_B2 = 0.999
ADAM_EPS = 1e-08
ADAM_WD = 0.01
ADAM_STEP = 10

ATT_T = 256
LOG2E = 1.4426950408889634
N_ATT = S // ATT_T

NN = (((1,), (0,)), ((), ()))
NT = (((1,), (1,)), ((), ()))
TN = (((0,), (0,)), ((), ()))
MESH = pl.DeviceIdType.MESH


def _params(sem=None):
    return pltpu.CompilerParams(dimension_semantics=sem, vmem_limit_bytes=VMEM_LIMIT)


def _pick(n, cap):
    best = None
    for t in range(LANES, cap + 1, LANES):
        if n % t == 0:
            best = t
    return best if best is not None else n


def _mm(a, b, mode, out_dtype, name, acc=None, dep=None):
    if mode == "nn":
        (m, k), (k2, n), dn = a.shape, b.shape, NN
    elif mode == "nt":
        (m, k), (n, k2), dn = a.shape, b.shape, NT
    else:
        (k, m), (k2, n), dn = a.shape, b.shape, TN
    assert k == k2, (a.shape, b.shape, mode)
    tn = _pick(n, 640)
    tm = _pick(m, 1536)
    osz = jnp.dtype(out_dtype).itemsize

    def need(tm_):
        blk = tm_ * k * 2 + tn * k * 2 + tm_ * tn * osz + (tm_ * tn * 4 if acc is not None else 0)
        return 2 * blk + tm_ * tn * 4
    while need(tm) > 36 * 1024 * 1024 and tm % 256 == 0:
        tm //= 2

    def body(*refs):
        a_ref, b_ref, o_ref = refs[0], refs[1], refs[-1]
        r = lax.dot_general(a_ref[...], b_ref[...], dn, preferred_element_type=F32)
        if acc is not None:
            r = r + refs[2][...]
        o_ref[...] = r.astype(o_ref.dtype)

    if mode == "tn":
        a_spec = pl.BlockSpec((k, tm), lambda i, j: (0, i))
    else:
        a_spec = pl.BlockSpec((tm, k), lambda i, j: (i, 0))
    if mode == "nt":
        b_spec = pl.BlockSpec((tn, k), lambda i, j: (j, 0))
    else:
        b_spec = pl.BlockSpec((k, tn), lambda i, j: (0, j))
    o_spec = pl.BlockSpec((tm, tn), lambda i, j: (i, j))
    in_specs = [a_spec, b_spec] + ([o_spec] if acc is not None else [])
    in_specs += [pl.BlockSpec(memory_space=pl.ANY)] if dep is not None else []
    args = (a, b) + ((acc,) if acc is not None else ()) + ((dep,) if dep is not None else ())
    return pl.pallas_call(
        body, name=name, grid=(m // tm, n // tn),
        in_specs=in_specs, out_specs=o_spec,
        out_shape=jax.ShapeDtypeStruct((m, n), out_dtype),
        compiler_params=_params(("parallel", "parallel")),
    )(*args)


def _mm_epi(a, b, mode, tnb, epi, name, tm, rows=(), vecs=(), outs=(), sums=()):
    m, k = a.shape
    nb = b.shape[1] if mode == "nn" else b.shape[0]
    dn = NN if mode == "nn" else NT
    n_in = 2 + len(rows) + len(vecs)

    def body(*refs):
        r = lax.dot_general(refs[0][...], refs[1][...], dn, preferred_element_type=F32)
        o_vals, s_vals = epi(r, *[x[...] for x in refs[2:n_in]])
        o_refs = refs[n_in:n_in + len(outs)]
        s_refs = refs[n_in + len(outs):]
        assert len(o_vals) == len(o_refs) and len(s_vals) == len(s_refs)
        for o_ref, val in zip(o_refs, o_vals):
            o_ref[...] = val.astype(o_ref.dtype)
        if sums:
            @pl.when((pl.program_id(0) == 0) & (pl.program_id(1) == 0))
            def _():
                for s_ref in s_refs:
                    s_ref[...] = jnp.zeros(s_ref.shape, F32)
            for s_ref, val in zip(s_refs, s_vals):
                s_ref[...] += val

    b_spec = pl.BlockSpec((k, tnb), lambda i, j: (0, j)) if mode == "nn" else pl.BlockSpec((tnb, k), lambda i, j: (j, 0))
    in_specs = [pl.BlockSpec((tm, k), lambda i, j: (i, 0)), b_spec]
    rows = [tuple(r) + (0,) * (3 - len(r)) for r in rows]
    in_specs += [pl.BlockSpec((tm, w), functools.partial(lambda i, j, off: (i, j + off), off=off)) for _, w, off in rows]
    in_specs += [pl.BlockSpec(v.shape, lambda i, j: (0, 0)) for v in vecs]
    out_specs = [pl.BlockSpec((tm, w), lambda i, j: (i, j)) for _, w, _ in outs]
    out_specs += [pl.BlockSpec((1, w), lambda i, j: (0, 0)) for w in sums]
    out_shape = [jax.ShapeDtypeStruct((m, full), dt) for full, _, dt in outs]
    out_shape += [jax.ShapeDtypeStruct((1, w), F32) for w in sums]
    return pl.pallas_call(
        body, name=name, grid=(m // tm, nb // tnb),
        in_specs=in_specs, out_specs=out_specs, out_shape=out_shape,
        compiler_params=_params(("arbitrary", "arbitrary") if sums else ("parallel", "parallel")),
    )(a, b, *[r[0] for r in rows], *vecs)


def _rowwise(fn, row_ins, vec_ins, row_outs, sum_outs, name, tm=256):
    n_in = len(row_ins) + len(vec_ins)
    n_o = len(row_outs)
    rows = row_ins[0][0].shape[0]

    def body(*refs):
        vals = [r[...] for r in refs[:n_in]]
        outs = refs[n_in:]
        ro, so = fn(*vals)
        assert len(ro) == n_o and len(so) == len(sum_outs)
        for r, v in zip(outs[:n_o], ro):
            r[...] = v.astype(r.dtype)
        if sum_outs:
            @pl.when(pl.program_id(0) == 0)
            def _():
                for r in outs[n_o:]:
                    r[...] = jnp.zeros(r.shape, F32)
            for r, v in zip(outs[n_o:], so):
                r[...] += v

    in_specs = [pl.BlockSpec((tm, w), functools.partial(lambda i, b: (i, b), b=b)) for _, w, b in row_ins]
    in_specs += [pl.BlockSpec(v.shape, lambda i: (0, 0)) for v in vec_ins]
    out_specs = [pl.BlockSpec((tm, w), lambda i: (i, 0)) for w, _ in row_outs]
    out_specs += [pl.BlockSpec((1, w), lambda i: (0, 0)) for w in sum_outs]
    out_shape = [jax.ShapeDtypeStruct((rows, w), dt) for w, dt in row_outs]
    out_shape += [jax.ShapeDtypeStruct((1, w), F32) for w in sum_outs]
    return pl.pallas_call(
        body, name=name, grid=(rows // tm,),
        in_specs=in_specs, out_specs=out_specs, out_shape=out_shape,
        compiler_params=_params(("arbitrary",)),
    )(*[a for a, _, _ in row_ins], *vec_ins)


def _sigmoid(x):
    return 1.0 / (1.0 + jnp.exp(-x))


def _rstd(x):
    return lax.rsqrt(jnp.mean(x * x, axis=-1, keepdims=True) + NORM_EPS)


def _norm_bwd(dyn, xn, r):
    return r * (dyn - xn * jnp.mean(dyn * xn, axis=-1, keepdims=True))


def _colsum(x):
    return jnp.sum(x, axis=0, keepdims=True)


def _rope_tables(pos, invf):
    def fn(p, f):
        lane = lax.broadcasted_iota(jnp.int32, (1, LANES), 1)
        ang = p * f
        cs, sn = jnp.cos(ang), jnp.sin(ang)
        rot = (lane >= 64) & (lane < 96)
        ct = jnp.where(lane < 64, 1.0, jnp.where(rot, cs, 0.0))
        sa = jnp.where((lane >= 64) & (lane < 80), -sn, 0.0)
        sb = jnp.where((lane >= 80) & (lane < 96), sn, 0.0)
        return (ct, sa, sb), ()
    return _rowwise(fn, [(pos, 1, 0)], [invf], [(LANES, F32)] * 3, [], "rope_tables")


def _rope(x, ct, sa, sb):
    return x * ct + pltpu.roll(x, LANES - 16, 1) * sa + pltpu.roll(x, 16, 1) * sb


def _rope_t(x, ct, sa, sb):
    return x * ct - pltpu.roll(x, LANES - 16, 1) * sa - pltpu.roll(x, 16, 1) * sb


def _head_mask(width, hh):
    lane = lax.broadcasted_iota(jnp.int32, (1, width), 1)
    half = width // 2
    return (lane >= hh * half) & (lane < (hh + 1) * half)


ATT_PP = 2
ATT_CHAINS = [(a, hh) for a in range(ATT_PP) for hh in range(2)]
ATT_G = HEADS // (2 * ATT_PP)


def _pair(ref_or_val, a, width, rows=slice(None)):
    return ref_or_val[rows, a * width:(a + 1) * width]


def _attn_fwd(q, qo, k, ko, v, vo, dkp, scale, bias, name):
    T = ATT_T
    assert qo % ATT_PP == 0 and ko % ATT_PP == 0 and vo % ATT_PP == 0
    qo, ko, vo = qo // ATT_PP, ko // ATT_PP, vo // ATT_PP

    def body(*refs):
        if bias is not None:
            q_ref, k_ref, v_ref, b_ref, o_ref, lse_ref, s_scr = refs
        else:
            q_ref, k_ref, v_ref, o_ref, lse_ref, s_scr = refs
        i = pl.program_id(1)
        row = lax.broadcasted_iota(jnp.int32, (T, T), 0)
        col = lax.broadcasted_iota(jnp.int32, (T, T), 1)
        qms = []
        for a, hh in ATT_CHAINS:
            qb = _pair(q_ref, a, dkp)
            qms.append(jnp.where(_head_mask(dkp, hh), qb, jnp.zeros_like(qb)))

        def fold(t):
            return [t[:, c * LANES:(c + 1) * LANES] for c in range(T // LANES)]

        def run(nt):
            mls = [jnp.full((T, LANES), -jnp.inf, F32) for _ in ATT_CHAINS]
            for j in range(nt):
                ks = slice(j * T, (j + 1) * T)
                for ci, (a, hh) in enumerate(ATT_CHAINS):
                    s = lax.dot_general(qms[ci], _pair(k_ref, a, dkp, ks), NT, preferred_element_type=F32) * (scale * LOG2E)
                    if bias is not None:
                        s = s + b_ref[2 * a + hh, j] * LOG2E
                    if j == nt - 1:
                        s = jnp.where(row >= col, s, -jnp.inf)
                    s_scr[ci, j] = s
                    for part in fold(s):
                        mls[ci] = jnp.maximum(mls[ci], part)
            ms = [jnp.max(ml, axis=1, keepdims=True) for ml in mls]
            mbs = [jnp.broadcast_to(m, (T, LANES)) for m in ms]
            for a in range(ATT_PP):
                ls = [jnp.zeros((T, LANES), F32) for _ in range(2)]
                ps, vms = [], []
                for j in range(nt):
                    vb = _pair(v_ref, a, LANES, slice(j * T, (j + 1) * T))
                    for hh in range(2):
                        parts = [jnp.exp2(part - mbs[2 * a + hh]) for part in fold(s_scr[2 * a + hh, j])]
                        for part in parts:
                            ls[hh] = ls[hh] + part
                        ps.append(jnp.concatenate(parts, axis=1).astype(BF16))
                        vms.append(jnp.where(_head_mask(LANES, hh), vb, jnp.zeros_like(vb)))
                acc = lax.dot_general(jnp.concatenate(ps, axis=1), jnp.concatenate(vms, axis=0), NN,
                                      preferred_element_type=F32)
                l0, l1 = [jnp.sum(l, axis=1, keepdims=True) for l in ls]
                lse_ref[2 * a] = ms[2 * a] + jnp.log2(l0)
                lse_ref[2 * a + 1] = ms[2 * a + 1] + jnp.log2(l1)
                inv = jnp.where(_head_mask(LANES, 0), 1.0 / l0, 1.0 / l1)
                o_ref[:, a * LANES:(a + 1) * LANES] = (acc * inv).astype(o_ref.dtype)

        for nt in range(1, N_ATT + 1):
            pl.when(i == nt - 1)(functools.partial(run, nt))

    in_specs = [
        pl.BlockSpec((T, ATT_PP * dkp), lambda g, i: (i, qo + g)),
        pl.BlockSpec((S, ATT_PP * dkp), lambda g, i: (0, ko + g)),
        pl.BlockSpec((S, ATT_PP * LANES), lambda g, i: (0, vo + g)),
    ]
    args = [q, k, v]
    if bias is not None:
        in_specs.append(pl.BlockSpec((2 * ATT_PP, N_ATT, 1, T), lambda g, i: (g, 0, 0, 0)))
        args.append(bias)
    return pl.pallas_call(
        body, name=name, grid=(ATT_G, N_ATT),
        in_specs=in_specs,
        out_specs=[pl.BlockSpec((T, ATT_PP * LANES), lambda g, i: (i, g)),
                   pl.BlockSpec((2 * ATT_PP, T, 1), lambda g, i: (g, i, 0))],
        out_shape=[jax.ShapeDtypeStruct((S, HEADS * HEAD_DIM), BF16),
                   jax.ShapeDtypeStruct((HEADS, S, 1), F32)],
        scratch_shapes=[pltpu.VMEM((len(ATT_CHAINS), N_ATT, T, T), F32)],
        compiler_params=_params(("parallel", "arbitrary")),
    )(*args)


def _attn_grad(q, qo, k, ko, v, vo, do, lse, dkp, scale, bias, qk_dtype, name):
    T = ATT_T
    has_b = bias is not None
    qo, ko, vo = qo // ATT_PP, ko // ATT_PP, vo // ATT_PP
    n_ch = len(ATT_CHAINS)

    def body(*refs):
        q_ref, k_ref, v_ref, do_ref, lse_ref = refs[:5]
        refs = refs[5:]
        if has_b:
            b_ref, refs = refs[0], refs[1:]
        dq_ref, dk_ref, dv_ref = refs[:3]
        refs = refs[3:]
        if has_b:
            db_ref, refs = refs[0], refs[1:]
        p_scr, dp_scr, dk_acc, dv_acc = refs[:4]
        db_acc = refs[4] if has_b else None
        i = pl.program_id(1)

        @pl.when(i == 0)
        def _():
            dk_acc[...] = jnp.zeros(dk_acc.shape, F32)
            dv_acc[...] = jnp.zeros(dv_acc.shape, F32)
            if has_b:
                db_acc[...] = jnp.zeros(db_acc.shape, F32)

        row = lax.broadcasted_iota(jnp.int32, (T, T), 0)
        col = lax.broadcasted_iota(jnp.int32, (T, T), 1)

        def fold(t):
            return [t[:, c * LANES:(c + 1) * LANES] for c in range(T // LANES)]

        qms, doms, lses = [], [], []
        for a, hh in ATT_CHAINS:
            qb, dob = _pair(q_ref, a, dkp), _pair(do_ref, a, LANES)
            qms.append(jnp.where(_head_mask(dkp, hh), qb, jnp.zeros_like(qb)))
            doms.append(jnp.where(_head_mask(LANES, hh), dob, jnp.zeros_like(dob)))
            lses.append(lse_ref[2 * a + hh])

        def run(nt):
            dls = [jnp.zeros((T, LANES), F32) for _ in ATT_CHAINS]
            for j in range(nt):
                ks = slice(j * T, (j + 1) * T)
                for ci, (a, hh) in enumerate(ATT_CHAINS):
                    s = lax.dot_general(qms[ci], _pair(k_ref, a, dkp, ks), NT, preferred_element_type=F32) * (scale * LOG2E)
                    if has_b:
                        s = s + b_ref[ci, j] * LOG2E
                    s = s - lses[ci]
                    if j == nt - 1:
                        s = jnp.where(row >= col, s, -jnp.inf)
                    p = jnp.exp2(s)
                    dp = lax.dot_general(doms[ci], _pair(v_ref, a, LANES, ks), NT, preferred_element_type=F32)
                    p_scr[ci, j] = p
                    dp_scr[ci, j] = dp
                    for part in fold(p * dp):
                        dls[ci] = dls[ci] + part
            deltas = [jnp.broadcast_to(jnp.sum(dl, axis=1, keepdims=True), (T, LANES)) for dl in dls]
            for a in range(ATT_PP):
                ds_all, km_all = [], []
                qm2t = jnp.transpose(jnp.concatenate([qms[2 * a], qms[2 * a + 1]], axis=0))
                dom2t = jnp.transpose(jnp.concatenate([doms[2 * a], doms[2 * a + 1]], axis=0))
                for j in range(nt):
                    ks = slice(j * T, (j + 1) * T)
                    kb = _pair(k_ref, a, dkp, ks)
                    p2, ds2 = [], []
                    for hh in range(2):
                        ci = 2 * a + hh
                        p = p_scr[ci, j]
                        ds = jnp.concatenate([pp * (dd - deltas[ci]) for pp, dd in zip(fold(p), fold(dp_scr[ci, j]))], axis=1)
                        if has_b:
                            db_acc[ci, j] += jnp.sum(ds, axis=0, keepdims=True)
                        p2.append(p.astype(BF16))
                        ds2.append((ds * scale).astype(BF16))
                        km_all.append(jnp.where(_head_mask(dkp, hh), kb, jnp.zeros_like(kb)))
                    dv_acc[a * LANES:(a + 1) * LANES, ks] += lax.dot_general(
                        dom2t, jnp.concatenate(p2, axis=0), NN, preferred_element_type=F32)
                    dk_acc[a * dkp:(a + 1) * dkp, ks] += lax.dot_general(
                        qm2t, jnp.concatenate(ds2, axis=0), NN, preferred_element_type=F32)
                    ds_all += ds2
                dq = lax.dot_general(jnp.concatenate(ds_all, axis=1), jnp.concatenate(km_all, axis=0), NN,
                                     preferred_element_type=F32)
                dq_ref[:, a * dkp:(a + 1) * dkp] = dq.astype(dq_ref.dtype)

        for nt in range(1, N_ATT + 1):
            pl.when(i == nt - 1)(functools.partial(run, nt))

        @pl.when(i == N_ATT - 1)
        def _():
            dk_ref[...] = jnp.transpose(dk_acc[...]).astype(dk_ref.dtype)
            dv_ref[...] = jnp.transpose(dv_acc[...]).astype(dv_ref.dtype)
            if has_b:
                db_ref[...] = db_acc[...]

    in_specs = [
        pl.BlockSpec((T, ATT_PP * dkp), lambda g, i: (i, qo + g)),
        pl.BlockSpec((S, ATT_PP * dkp), lambda g, i: (0, ko + g)),
        pl.BlockSpec((S, ATT_PP * LANES), lambda g, i: (0, vo + g)),
        pl.BlockSpec((T, ATT_PP * LANES), lambda g, i: (i, g)),
        pl.BlockSpec((2 * ATT_PP, T, 1), lambda g, i: (g, i, 0)),
    ]
    args = [q, k, v, do, lse]
    out_specs = [
        pl.BlockSpec((T, ATT_PP * dkp), lambda g, i: (i, g)),
        pl.BlockSpec((S, ATT_PP * dkp), lambda g, i: (0, g)),
        pl.BlockSpec((S, ATT_PP * LANES), lambda g, i: (0, g)),
    ]
    width = (HEADS // 2) * dkp
    out_shape = [
        jax.ShapeDtypeStruct((S, width), qk_dtype),
        jax.ShapeDtypeStruct((S, width), qk_dtype),
        jax.ShapeDtypeStruct((S, HEADS * HEAD_DIM), BF16),
    ]
    scratch = [pltpu.VMEM((n_ch, N_ATT, T, T), F32), pltpu.VMEM((n_ch, N_ATT, T, T), F32),
               pltpu.VMEM((ATT_PP * dkp, S), F32), pltpu.VMEM((ATT_PP * LANES, S), F32)]
    if has_b:
        bspec = pl.BlockSpec((2 * ATT_PP, N_ATT, 1, T), lambda g, i: (g, 0, 0, 0))
        in_specs.append(bspec)
        args.append(bias)
        out_specs.append(bspec)
        out_shape.append(jax.ShapeDtypeStruct((HEADS, N_ATT, 1, T), F32))
        scratch.append(pltpu.VMEM((2 * ATT_PP, N_ATT, 1, T), F32))
    return pl.pallas_call(
        body, name=name, grid=(ATT_G, N_ATT),
        in_specs=in_specs, out_specs=out_specs, out_shape=out_shape, scratch_shapes=scratch,
        compiler_params=_params(("parallel", "arbitrary")),
    )(*args)


def _tri(upper):
    a = lax.broadcasted_iota(jnp.int32, (LANES, LANES), 0)
    b = lax.broadcasted_iota(jnp.int32, (LANES, LANES), 1)
    return jnp.where(a <= b if upper else a >= b, 1.0, 0.0).astype(F32)


def _fox_gates(zt, bf):
    def body(z_ref, b_ref, o_ref):
        tri = _tri(True)
        carry = jnp.zeros((HEADS, 1), F32)
        for t in range(S // LANES):
            sl = slice(t * LANES, (t + 1) * LANES)
            z = z_ref[:, sl] + b_ref[...]
            logf = jnp.minimum(z, 0.0) - jnp.log(1.0 + jnp.exp(-jnp.abs(z)))
            c = lax.dot_general(logf, tri, NN, preferred_element_type=F32,
                                precision=lax.Precision.HIGHEST) + carry
            o_ref[:, sl] = -c
            carry = c[:, LANES - 1:LANES]

    return pl.pallas_call(
        body, name="fox_gates", out_shape=jax.ShapeDtypeStruct((HEADS, S), F32),
        compiler_params=_params(),
    )(zt, bf)


def _fox_gates_bwd(dbias, zt, bf):
    def body(d_ref, z_ref, b_ref, dz_ref, dbf_ref):
        tri = _tri(False)
        carry = jnp.zeros((HEADS, 1), F32)
        tot = jnp.zeros((HEADS, 1), F32)
        for t in reversed(range(S // LANES)):
            sl = slice(t * LANES, (t + 1) * LANES)
            df = -d_ref[:, sl]
            c = lax.dot_general(df, tri, NN, preferred_element_type=F32,
                                precision=lax.Precision.HIGHEST) + carry
            carry = c[:, 0:1]
            z = z_ref[:, sl] + b_ref[...]
            dz = c * _sigmoid(-z)
            dz_ref[:, sl] = dz
            tot = tot + jnp.sum(dz, axis=1, keepdims=True)
        dbf_ref[...] = tot

    return pl.pallas_call(
        body, name="fox_gates_bwd",
        out_shape=[jax.ShapeDtypeStruct((HEADS, S), F32), jax.ShapeDtypeStruct((HEADS, 1), F32)],
        compiler_params=_params(),
    )(dbias, zt, bf)


def _mod_part(c_all, w_ada, b_cols):
    def body(c_ref, w_ref, b_ref, o_ref, s_ref):
        c = c_ref[...]
        sc = c * _sigmoid(c)
        s_ref[...] = sc
        o_ref[...] = lax.dot_general(sc, w_ref[...], NN, preferred_element_type=F32,
                                     precision=lax.Precision.HIGHEST) + b_ref[...]

    return pl.pallas_call(
        body, name="mod_part",
        out_shape=[jax.ShapeDtypeStruct((N_DEV, w_ada.shape[1]), F32), jax.ShapeDtypeStruct(c_all.shape, F32)],
        compiler_params=_params(),
    )(c_all, w_ada, b_cols)


def _w_ada_grad(sc_t, dm):
    def body(s_ref, d_ref, o_ref):
        acc = jnp.zeros(o_ref.shape, F32)
        for b in range(N_DEV):
            acc = acc + s_ref[:, b:b + 1] * d_ref[b:b + 1, :]
        o_ref[...] = acc

    return pl.pallas_call(
        body, name="w_ada_grad", out_shape=jax.ShapeDtypeStruct((sc_t.shape[0], dm.shape[1]), F32),
        compiler_params=_params(),
    )(sc_t, dm)


def _adamw(w, m, v, parts, name, own=None, slot=None):
    rows, cols = w.shape
    n = parts.shape[0]
    tr = rows if rows <= 512 else 256

    def body(*refs):
        if own is not None:
            s_ref, refs = refs[0], refs[1:]
            w_ref, m_ref, v_ref, p_ref, o_ref, g_out, d_out, m_out, v_out = refs
            terms = [jnp.where(s_ref[0] == kk, o_ref[0], p_ref[kk]) for kk in range(n)]
        else:
            w_ref, m_ref, v_ref, p_ref, g_out, d_out, m_out, v_out = refs
            terms = [p_ref[kk] for kk in range(n)]
        g = terms[0].astype(F32)
        for term in terms[1:]:
            g = g + term.astype(F32)
        g_out[...] = g
        d_out[...], m_out[...], v_out[...] = _adamw_math(w_ref[...], g, m_ref[...], v_ref[...])

    spec = pl.BlockSpec((tr, cols), lambda i, *_: (i, 0))
    in_specs = [spec, spec, spec, pl.BlockSpec((n, tr, cols), lambda i, *_: (0, i, 0))]
    out_shape = [jax.ShapeDtypeStruct((rows, cols), F32)] * 4
    if own is None:
        return pl.pallas_call(
            body, name=name, grid=(rows // tr,), in_specs=in_specs, out_specs=[spec] * 4, out_shape=out_shape,
            compiler_params=_params(("parallel",)),
        )(w, m, v, parts)
    in_specs.append(pl.BlockSpec((1, tr, cols), lambda i, s: (s[0], i, 0)))
    return pl.pallas_call(
        body, name=name, out_shape=out_shape, compiler_params=_params(("parallel",)),
        grid_spec=pltpu.PrefetchScalarGridSpec(num_scalar_prefetch=1, grid=(rows // tr,), in_specs=in_specs,
                                               out_specs=[spec] * 4),
    )(slot, w, m, v, parts, own)


def _adamw_math(w, g, m, v):
    mm = ADAM_B1 * m + (1.0 - ADAM_B1) * g
    vv = ADAM_B2 * v + (1.0 - ADAM_B2) * (g * g)
    m_hat = mm / (1.0 - ADAM_B1 ** ADAM_STEP)
    v_hat = vv / (1.0 - ADAM_B2 ** ADAM_STEP)
    return -ADAM_LR * (m_hat / (jnp.sqrt(v_hat) + ADAM_EPS) + ADAM_WD * w), mm, vv


def _adamw_rows(bundles, offsets, ws, ms, vs, err_off, err_width):
    k = len(ws)

    def body(*refs):
        b_ref = refs[0]
        w_refs, m_refs, v_refs = refs[1:1 + k], refs[1 + k:1 + 2 * k], refs[1 + 2 * k:1 + 3 * k]
        outs = refs[1 + 3 * k:]
        g_all = b_ref[0]
        for kk in range(1, N_DEV):
            g_all = g_all + b_ref[kk]
        for i in range(k):
            width = w_refs[i].shape[1]
            g = g_all[:, offsets[i]:offsets[i] + width]
            outs[4 * i][...] = g
            outs[4 * i + 1][...], outs[4 * i + 2][...], outs[4 * i + 3][...] = _adamw_math(
                w_refs[i][...], g, m_refs[i][...], v_refs[i][...])
        outs[4 * k][...] = g_all[:, err_off:err_off + err_width]

    out_shape = []
    for w_ in ws:
        out_shape += [jax.ShapeDtypeStruct(w_.shape, F32)] * 4
    out_shape.append(jax.ShapeDtypeStruct((1, err_width), F32))
    res = pl.pallas_call(body, name="adamw_rows", out_shape=out_shape, compiler_params=_params())(bundles, *ws, *ms, *vs)
    return [tuple(res[4 * i:4 * i + 4]) for i in range(k)], res[-1]


def _coords():
    return lax.axis_index("x"), lax.axis_index("y"), lax.axis_index("c")


def _flat(px, py, pc):
    return 4 * px + 2 * py + pc


def _all_gather(arrs, name):
    n = len(arrs)

    def body(*refs):
        ins, outs = refs[:n], refs[n:2 * n]
        send, recv, lsem = refs[2 * n:]
        x, y, c = _coords()
        me, sibling = (x, y, c), (x, y, 1 - c)
        chips = [(1 - x, y), (x, 1 - y), (1 - x, 1 - y)]

        def copy(a, kk, block, to, src=None):
            slot = outs[a].at[_flat(*block)]
            return pltpu.make_async_remote_copy(
                src_ref=slot if src is None else src, dst_ref=slot,
                send_sem=send.at[a, kk], recv_sem=recv.at[a, kk],
                device_id=to, device_id_type=MESH)

        mine = [pltpu.make_async_copy(ins[a], outs[a].at[_flat(*me)], lsem.at[a]) for a in range(n)]
        for cp in mine:
            cp.start()
        first = []
        for a in range(n):
            first.append(copy(a, 0, me, sibling, src=ins[a]))
            first += [copy(a, 1 + j, me, (*chip, c), src=ins[a]) for j, chip in enumerate(chips)]
        for cp in first:
            cp.start()
        passed = []
        for j, chip in enumerate(chips):
            for a in range(n):
                copy(a, 1 + j, (*chip, c), me).wait_recv()
                cp = copy(a, 4 + j, (*chip, c), sibling)
                cp.start()
                passed.append(cp)
        for a in range(n):
            copy(a, 0, sibling, me).wait_recv()
        for j, chip in enumerate(chips):
            for a in range(n):
                copy(a, 4 + j, (*chip, 1 - c), me).wait_recv()
        for cp in first + passed:
            cp.wait_send()
        for cp in mine:
            cp.wait()

    any_spec = pl.BlockSpec(memory_space=pl.ANY)
    return pl.pallas_call(
        body, name=name,
        in_specs=[any_spec] * n, out_specs=[any_spec] * n,
        out_shape=[jax.ShapeDtypeStruct((N_DEV,) + a.shape, a.dtype) for a in arrs],
        scratch_shapes=[pltpu.SemaphoreType.DMA((n, 7)), pltpu.SemaphoreType.DMA((n, 7)),
                        pltpu.SemaphoreType.DMA((n,))],
    )(*arrs)


def _peer_list():
    x, y, c = _coords()
    return [((1 - x if r & 4 else x), (1 - y if r & 2 else y), (1 - c if r & 1 else c)) for r in range(1, N_DEV)]


def _copy_plan(mode, src, land):
    x, y, c = _coords()
    me = _flat(x, y, c)
    if mode == "gather":
        return [(src, land.at[me], peer) for peer in _peer_list()]
    if mode == "exchange":
        return [(src.at[_flat(*peer)], land.at[me], peer) for peer in _peer_list()]
    if mode == "pair":
        return [(src.at[_flat(q // 2, q % 2, 1 - c)], land.at[q], (x, y, 1 - c)) for q in range(N_DEV // 2)]
    assert mode == "chips"
    plan = []
    for r in range(1, N_DEV // 2):
        qx, qy = (1 - x if r & 2 else x), (1 - y if r & 1 else y)
        plan.append((src.at[2 * qx + qy], land.at[2 * x + y], (qx, qy, c)))
    return plan


N_COPIES = dict(gather=N_DEV - 1, exchange=N_DEV - 1, pair=N_DEV // 2, chips=N_DEV // 2 - 1)


def _land_shape(mode, shape):
    return {"gather": (N_DEV,) + shape, "exchange": shape, "pair": (N_DEV // 2,) + shape[1:], "chips": shape}[mode]


HBM_SPEC = pl.BlockSpec(memory_space=pltpu.HBM)
SEM_SPEC = pl.BlockSpec(memory_space=pltpu.SEMAPHORE)
ANY_SPEC = pl.BlockSpec(memory_space=pl.ANY)
SIDE_EFFECT = pltpu.SideEffectType.DATAFLOW_SIDE_EFFECTING


def _async_start(groups, mode, after, name):
    flat_arrs = [a for g in groups for a in g]
    n = len(flat_arrs)

    def body(*refs):
        srcs, lands = refs[:n], refs[n:2 * n]
        outs = refs[2 * n + 1:]
        token = outs[-1]
        for ai in range(n):
            for src_ref, dst_ref, peer in _copy_plan(mode, srcs[ai], lands[ai]):
                pltpu.make_async_remote_copy(src_ref=src_ref, dst_ref=dst_ref, send_sem=outs[2 * ai],
                                             recv_sem=outs[2 * ai + 1], device_id=peer, device_id_type=MESH).start()
        token[...] = jnp.zeros(token.shape, F32)

    land_shapes = [_land_shape(mode, a.shape) for a in flat_arrs]
    out_shape = [pltpu.SemaphoreType.DMA(())] * (2 * n)
    out_shape += [pltpu.HBM(a.shape, a.dtype) for a in flat_arrs]
    out_shape += [pltpu.HBM(s, a.dtype) for s, a in zip(land_shapes, flat_arrs)]
    out_shape.append(jax.ShapeDtypeStruct((8, LANES), F32))
    res = pl.pallas_call(
        body, name=name, out_shape=tuple(out_shape),
        in_specs=[HBM_SPEC] * (2 * n) + [ANY_SPEC],
        out_specs=tuple([SEM_SPEC] * (2 * n) + [HBM_SPEC] * (2 * n) + [pl.BlockSpec(memory_space=pltpu.VMEM)]),
        input_output_aliases={i: 2 * n + i for i in range(2 * n)},
        compiler_params=pltpu.CompilerParams(has_side_effects=SIDE_EFFECT),
    )(*[pltpu.with_memory_space_constraint(a, pltpu.HBM) for a in flat_arrs],
      *[pltpu.with_memory_space_constraint(lax.empty(s, a.dtype), pltpu.HBM) for s, a in zip(land_shapes, flat_arrs)],
      after)
    sems, thru = res[:2 * n], res[2 * n:-1]
    states, idx = [], 0
    for g in groups:
        k = len(g)
        states.append((list(sems[2 * idx:2 * (idx + k):2]), list(sems[2 * idx + 1:2 * (idx + k):2]),
                       list(thru[idx:idx + k]), list(thru[n + idx:n + idx + k]), mode))
        idx += k
    return states, res[-1]


def _async_wait(state, after, name):
    sends, recvs, srcs, lands, mode = state
    g = len(srcs)

    def body(*refs):
        l_refs, sems = refs[g:2 * g], refs[2 * g:4 * g]
        for ai in range(g):
            moved = l_refs[ai].at[pl.ds(0, N_COPIES[mode])]
            cp = pltpu.make_async_remote_copy(src_ref=moved, dst_ref=moved, send_sem=sems[ai], recv_sem=sems[g + ai],
                                              device_id=_coords(), device_id_type=MESH)
            cp.wait_send()
            cp.wait_recv()

    res = pl.pallas_call(
        body, name=name,
        out_shape=tuple([pltpu.HBM(a.shape, a.dtype) for a in srcs] + [pltpu.HBM(a.shape, a.dtype) for a in lands]),
        in_specs=[HBM_SPEC] * (2 * g) + [SEM_SPEC] * (2 * g) + [ANY_SPEC],
        out_specs=tuple([HBM_SPEC] * (2 * g)),
        input_output_aliases={i: i for i in range(2 * g)},
        compiler_params=pltpu.CompilerParams(has_side_effects=SIDE_EFFECT),
    )(*srcs, *lands, *sends, *recvs, after)
    return list(res[:g]), list(res[g:])


def _add_blocks(a, b):
    def body(a_ref, b_ref, o_ref):
        o_ref[...] = (a_ref[...].astype(F32) + b_ref[...].astype(F32)).astype(o_ref.dtype)

    spec = pl.BlockSpec((1,) + a.shape[1:], lambda i: (i, 0, 0))
    return pl.pallas_call(
        body, name="add_blocks", grid=(a.shape[0],), in_specs=[spec, spec], out_specs=spec,
        out_shape=jax.ShapeDtypeStruct(a.shape, a.dtype), compiler_params=_params(("parallel",)),
    )(a, b)


def _with_own(land, own, me):
    return lax.dynamic_update_index_in_dim(land, own, me, 0)


IN_SPLITS = (512, 512, 512, 8, 768, 256, 32, 1024, 1024)


def _from_shards(g, fn, out_width, name, own=None, slot=None):
    _, k, n = g.shape
    tr = min(k, 256)

    def body(*refs):
        if own is not None:
            s_ref, g_ref, own_ref, o_ref = refs
            cols = [jnp.where(s_ref[0] == j, own_ref[...], g_ref[j]) for j in range(N_DEV)]
        else:
            g_ref, o_ref = refs
            cols = [g_ref[j] for j in range(N_DEV)]
        o_ref[...] = fn(jnp.concatenate(cols, axis=1))

    in_specs = [pl.BlockSpec((N_DEV, tr, n), lambda i, *_: (0, i, 0))]
    out_spec = pl.BlockSpec((tr, out_width), lambda i, *_: (i, 0))
    out_shape = jax.ShapeDtypeStruct((k, out_width), g.dtype)
    if own is None:
        return pl.pallas_call(body, name=name, grid=(k // tr,), in_specs=in_specs, out_specs=out_spec,
                              out_shape=out_shape, compiler_params=_params(("parallel",)))(g)
    in_specs.append(pl.BlockSpec((tr, n), lambda i, *_: (i, 0)))
    return pl.pallas_call(
        body, name=name, out_shape=out_shape, compiler_params=_params(("parallel",)),
        grid_spec=pltpu.PrefetchScalarGridSpec(num_scalar_prefetch=1, grid=(k // tr,), in_specs=in_specs, out_specs=out_spec),
    )(slot, g, own)


def _unshard_cols(g, own=None, slot=None):
    return _from_shards(g, lambda full: full, N_DEV * g.shape[2], "unshard_cols_%d" % g.shape[2], own, slot)


FFN_T = 256
FFN_SHARD = 2 * D_FF // N_DEV


def _unshard_ffn_in(g, own=None, slot=None):
    def pairs(full):
        parts = []
        for j in range(D_FF // FFN_T):
            parts += [full[:, j * FFN_T:(j + 1) * FFN_T], full[:, D_FF + j * FFN_T:D_FF + (j + 1) * FFN_T]]
        return jnp.concatenate(parts, axis=1)

    return _from_shards(g, pairs, 2 * D_FF, "unshard_ffn_in", own, slot)


def _shard_ffn_in(w):
    tr = 256

    def body(w_ref, o_ref):
        x = w_ref[...]
        nb = D_FF // FFN_T
        full = jnp.concatenate([x[:, (2 * j + half) * FFN_T:(2 * j + half + 1) * FFN_T]
                                for half in range(2) for j in range(nb)], axis=1)
        for j in range(N_DEV):
            o_ref[j] = full[:, j * FFN_SHARD:(j + 1) * FFN_SHARD]

    return pl.pallas_call(
        body, name="shard_ffn_in", grid=(D // tr,),
        in_specs=[pl.BlockSpec((tr, 2 * D_FF), lambda i: (i, 0))],
        out_specs=pl.BlockSpec((N_DEV, tr, FFN_SHARD), lambda i: (0, i, 0)),
        out_shape=jax.ShapeDtypeStruct((N_DEV, D, FFN_SHARD), w.dtype),
        compiler_params=_params(("parallel",)),
    )(w)


def _shard_cols(w):
    k, n = w.shape[0], w.shape[1] // N_DEV
    tr = min(k, 256)

    def body(w_ref, o_ref):
        full = w_ref[...]
        for j in range(N_DEV):
            o_ref[j] = full[:, j * n:(j + 1) * n]

    return pl.pallas_call(
        body, name="shard_cols_%d" % n, grid=(k // tr,),
        in_specs=[pl.BlockSpec((tr, N_DEV * n), lambda i: (i, 0))],
        out_specs=pl.BlockSpec((N_DEV, tr, n), lambda i: (0, i, 0)),
        out_shape=jax.ShapeDtypeStruct((N_DEV, k, n), w.dtype),
        compiler_params=_params(("parallel",)),
    )(w)


IN_OFFS = tuple(sum(IN_SPLITS[:i]) for i in range(len(IN_SPLITS) + 1))
IN_SHARD = IN_OFFS[-1] // N_DEV
REGROUP_ROWS = 128


def _w_in_regroup(g):
    def body(g_ref, a_ref, b_ref):
        full = jnp.concatenate([g_ref[j] for j in range(N_DEV)], axis=1)
        fq, fk, fv, wf, cq, ckv, kr, gf, gm = [full[:, IN_OFFS[i]:IN_OFFS[i + 1]] for i in range(9)]
        rows = full.shape[0]
        a_ref[...] = jnp.concatenate([cq, ckv, gf, gm, wf, jnp.zeros((rows, 56), BF16), kr, jnp.zeros((rows, 32), BF16)], axis=1)
        b_ref[...] = jnp.concatenate([fq, fk, fv], axis=1)

    tr = REGROUP_ROWS
    return pl.pallas_call(
        body, name="w_in_regroup", grid=(D // tr,),
        in_specs=[pl.BlockSpec((N_DEV, tr, IN_SHARD), lambda i: (0, i, 0))],
        out_specs=[pl.BlockSpec((tr, 3200), lambda i: (i, 0)), pl.BlockSpec((tr, 1536), lambda i: (i, 0))],
        out_shape=[jax.ShapeDtypeStruct((D, 3200), BF16), jax.ShapeDtypeStruct((D, 1536), BF16)],
        compiler_params=_params(("parallel",)),
    )(g)


def _w_in_ungroup(da, db_):
    def body(a_ref, b_ref, o_ref):
        a = a_ref[...]
        full = jnp.concatenate([b_ref[...], a[:, 3072:3080], a[:, 0:768], a[:, 768:1024], a[:, 3136:3168],
                                a[:, 1024:3072]], axis=1)
        for j in range(N_DEV):
            o_ref[j] = full[:, j * IN_SHARD:(j + 1) * IN_SHARD]

    tr = REGROUP_ROWS
    return pl.pallas_call(
        body, name="w_in_ungroup", grid=(D // tr,),
        in_specs=[pl.BlockSpec((tr, 3200), lambda i: (i, 0)), pl.BlockSpec((tr, 1536), lambda i: (i, 0))],
        out_specs=pl.BlockSpec((N_DEV, tr, IN_SHARD), lambda i: (0, i, 0)),
        out_shape=jax.ShapeDtypeStruct((N_DEV, D, IN_SHARD), BF16),
        compiler_params=_params(("parallel",)),
    )(da, db_)


def _prepare_weights(g, own=None, slot=None):
    w = {}
    if own is not None:
        small = ("w_uq", "w_ukv", "w_out", "w_ffn_out")
        g = {n: (_with_own(a, own[n], slot[0]) if n in small else a) for n, a in g.items()}
    pick = (lambda n: (own[n], slot)) if own is not None else (lambda n: (None, None))
    if "w_in" in g:
        w["w_a"], w["w_b"] = _w_in_regroup(g["w_in"])
    if "w_uq" in g:
        w_uq = g["w_uq"].reshape(Q_LORA, HEADS, 96)
        w["w_uq"] = jnp.pad(w_uq, ((0, 0), (0, 0), (0, 32))).reshape(Q_LORA, HEADS * LANES)
        ukv = g["w_ukv"]
        w["w_k"] = jnp.transpose(jnp.pad(ukv[:, :, :64], ((0, 0), (0, 0), (0, 64))), (1, 0, 2)).reshape(KV_LORA, HEADS * LANES)
        w["w_v"] = jnp.transpose(ukv[:, :, 64:], (1, 0, 2)).reshape(KV_LORA, HEADS * HEAD_DIM)
    if "w_out" in g:
        w["w_pf"] = _unshard_cols(g["w_proj_fox"], *pick("w_proj_fox"))
        w["w_pm"] = _unshard_cols(g["w_proj_mla"], *pick("w_proj_mla"))
        w["w_out"] = g["w_out"].reshape(D, D)
    if "w_ffn_in" in g:
        w["w_ffn_in"] = _unshard_ffn_in(g["w_ffn_in"], *pick("w_ffn_in"))
        w["w_ffn_out"] = g["w_ffn_out"].reshape(D_FF, D)
    return w


def _shard_grads(dw):
    out = {}
    if "w_a" in dw:
        out["w_in"] = _w_in_ungroup(dw["w_a"], dw["w_b"])
    if "w_uq" in dw:
        w_uq = dw["w_uq"].reshape(Q_LORA, HEADS, LANES)[:, :, :96].reshape(Q_LORA, Q_LORA)
        out["w_uq"] = w_uq.reshape(N_DEV, Q_LORA // N_DEV, Q_LORA)
        k_part = dw["w_k"].reshape(KV_LORA, HEADS, LANES)[:, :, :64]
        v_part = dw["w_v"].reshape(KV_LORA, HEADS, HEAD_DIM)
        out["w_ukv"] = jnp.transpose(jnp.concatenate([k_part, v_part], axis=2), (1, 0, 2))
    if "w_out" in dw:
        out["w_proj_fox"] = _shard_cols(dw["w_pf"])
        out["w_proj_mla"] = _shard_cols(dw["w_pm"])
        out["w_out"] = dw["w_out"].reshape(N_DEV, D // N_DEV, D)
    if "w_ffn_in" in dw:
        out["w_ffn_in"] = _shard_ffn_in(dw["w_ffn_in"])
        out["w_ffn_out"] = dw["w_ffn_out"].reshape(N_DEV, D_FF // N_DEV, D)
    return out


def _fwd_bwd(x, pos, mod, target, w, vec, wts, send, relay):
    shift_mix, scale_mix, gate_mix, shift_ffn, scale_ffn, gate_ffn = [mod[:, i * D:(i + 1) * D] for i in range(6)]
    g_pre_mix, g_post_mix, g_pre_ffn, g_post_ffn = vec["g_pre_mix"], vec["g_post_mix"], vec["g_pre_ffn"], vec["g_post_ffn"]
    g_q, g_kv = vec["g_q_lora"], vec["g_kv_lora"]

    inv_freq = 1.0 / (ROPE_THETA ** (jnp.arange(0, ROPE_DIM, 2, dtype=F32) / ROPE_DIM))
    invf = jnp.concatenate([jnp.zeros((64,), F32), inv_freq, inv_freq, jnp.zeros((32,), F32)]).reshape(1, LANES)
    ct, sa, sb = _rope_tables(pos, invf)

    def pre1(xv, g, sc, sh):
        return ((xv * _rstd(xv) * g) * (1.0 + sc) + sh,), ()
    (h,) = _rowwise(pre1, [(x, D, 0)], [g_pre_mix, scale_mix, shift_mix], [(D, BF16)], [], "pre_mix")
    proj_a = _mm(h, w["w_a"], "nn", F32, "in_proj_a")
    qkv = _mm(h, w["w_b"], "nn", BF16, "in_proj_b")

    def lora_norm(cq, ckv, gq, gkv):
        return (cq * _rstd(cq) * gq, ckv * _rstd(ckv) * gkv), ()
    cqn, ckvn = _rowwise(lora_norm, [(proj_a, Q_LORA, 0), (proj_a, KV_LORA, 3)], [g_q, g_kv],
                         [(Q_LORA, BF16), (KV_LORA, BF16)], [], "lora_norm")
    w = {**w, **wts("lora", cqn)}
    tables = [(ct, LANES), (sa, LANES), (sb, LANES)]

    def rope_q(qv, c_, a_, b_):
        return (jnp.concatenate([_rope(qv[:, hd * LANES:(hd + 1) * LANES], c_, a_, b_) for hd in range(HEADS)], axis=1),), ()
    (q_m,) = _mm_epi(cqn, w["w_uq"], "nn", D, rope_q, "mla_uq", 512, rows=tables, outs=[(D, D, BF16)])

    def rope_k(kv, misc, c_, a_, b_):
        lane = lax.broadcasted_iota(jnp.int32, (1, LANES), 1)
        kpe = jnp.where((lane >= 64) & (lane < 96), _rope(misc, c_, a_, b_), 0.0)
        return (jnp.concatenate([kv[:, hd * LANES:(hd + 1) * LANES] + kpe for hd in range(HEADS)], axis=1),), ()
    (k_m,) = _mm_epi(ckvn, w["w_k"], "nn", D, rope_k, "mla_uk", 512, rows=[(proj_a, LANES, 24)] + tables,
                     outs=[(D, D, BF16)])
    v_m = _mm(ckvn, w["w_v"], "nn", BF16, "mla_uv")

    zt = jnp.transpose(proj_a[:, 3072:3080])
    bf = jnp.transpose(vec["b_forget"])
    neg_f = _fox_gates(zt, bf)
    bias = neg_f.reshape(HEADS, N_ATT, 1, ATT_T)
    o_a, lse_a = _attn_fwd(qkv, 0, qkv, 4, qkv, 8, LANES, 1.0 / math.sqrt(HEAD_DIM), bias, "fox_attn")
    o_b, lse_b = _attn_fwd(q_m, 0, k_m, 0, v_m, 0, 2 * LANES, 1.0 / math.sqrt(64 + ROPE_DIM), None, "mla_attn")

    w = {**w, **wts("proj", o_b)}
    pa = _mm(o_a, w["w_pf"], "nn", BF16, "proj_fox")

    def merge(pb_, gf, gm, pa_):
        return (_sigmoid(gf) * pa_ + _sigmoid(gm) * pb_, pb_), ()
    merged, pb = _mm_epi(o_b, w["w_pm"], "nn", 512, merge, "proj_mla", 1024,
                         rows=[(proj_a, 512, 2), (proj_a, 512, 4), (pa, 512)], outs=[(D, 512, BF16), (D, 512, BF16)])
    def post1(yv, xv, gate, gpost, gpre, sc, sh):
        x1 = xv + gate * (yv * _rstd(yv) * gpost)
        return (x1, (x1 * _rstd(x1) * gpre) * (1.0 + sc) + sh, yv), ()
    x1, h2, y = _mm_epi(merged, w["w_out"], "nn", D, post1, "out_proj", 256, rows=[(x, D)],
                        vecs=[gate_mix, g_post_mix, g_pre_ffn, scale_ffn, shift_ffn],
                        outs=[(D, D, F32), (D, D, BF16), (D, D, F32)])
    w = {**w, **wts("ffn", h2)}

    def swiglu(r):
        g, u = r[:, :FFN_T], r[:, FFN_T:]
        return (g * _sigmoid(g) * u, r), ()
    act, gu = _mm_epi(h2, w["w_ffn_in"], "nn", 2 * FFN_T, swiglu, "ffn_in", 1024,
                      outs=[(D_FF, FFN_T, BF16), (2 * D_FF, 2 * FFN_T, BF16)])

    def head(y2v, x1v, tv, gate, gpost):
        r = _rstd(y2v)
        yn = y2v * r
        n2 = yn * gpost
        err = (x1v + gate * n2) - tv
        dx2 = err * (1.0 / D)
        dn2 = dx2 * gate
        dy2 = _norm_bwd(dn2 * gpost, yn, r)
        return (dx2, dy2), (_colsum(err * err), _colsum(dx2 * n2), _colsum(dn2 * yn))
    dx2, dy2, err_cols, d_gate_ffn, d_g_post_ffn = _mm_epi(
        act, w["w_ffn_out"], "nn", D, head, "ffn_out", 256, rows=[(x1, D), (target, D)], vecs=[gate_ffn, g_post_ffn],
        outs=[(D, D, F32), (D, D, BF16)], sums=[D, D, D])

    def swiglu_bwd(da, guv):
        g, u = guv[:, :FFN_T].astype(F32), guv[:, FFN_T:].astype(F32)
        sg = _sigmoid(g)
        return (jnp.concatenate([da * u * (sg * (1.0 + g * (1.0 - sg))), da * (g * sg)], axis=1),), ()
    (dgu,) = _mm_epi(dy2, w["w_ffn_out"], "nt", FFN_T, swiglu_bwd, "ffn_out_dx", 1024, rows=[(gu, 2 * FFN_T)],
                     outs=[(2 * D_FF, 2 * FFN_T, BF16)])
    dw = {"w_ffn_out": _mm(act, dy2, "tn", BF16, "ffn_out_dw")}
    dw["w_ffn_in"] = _mm(h2, dgu, "tn", BF16, "ffn_in_dw")
    gate_mix = gate_mix + send({n: dw.pop(n) for n in ("w_ffn_in", "w_ffn_out")})[0, 0]

    def mid(dh, x1v, dx2v, yv, gpre, sc, gate, gpost):
        r2 = _rstd(x1v)
        x1n = x1v * r2
        t = dh * x1n
        dx1 = dx2v + _norm_bwd(dh * (gpre * (1.0 + sc)), x1n, r2)
        ry = _rstd(yv)
        yn = yv * ry
        dn1 = dx1 * gate
        dy = _norm_bwd(dn1 * gpost, yn, ry)
        sums = (_colsum(dh), _colsum(t) * gpre, _colsum(t) * (1.0 + sc), _colsum(dx1 * (yn * gpost)), _colsum(dn1 * yn))
        return (dx1, dy), sums
    dx1, dy, d_shift_ffn, d_scale_ffn, d_g_pre_ffn, d_gate_mix, d_g_post_mix = _mm_epi(
        dgu, w["w_ffn_in"], "nt", D, mid, "ffn_in_dx", 256, rows=[(x1, D), (dx2, D), (y, D)],
        vecs=[g_pre_ffn, scale_ffn, gate_mix, g_post_mix], outs=[(D, D, F32), (D, D, BF16)], sums=[D] * 5)

    dw["w_out"] = _mm(merged, dy, "tn", BF16, "out_proj_dw")

    def merge_bwd(dm, gf, gm, pa_, pb_):
        sf, sm = _sigmoid(gf), _sigmoid(gm)
        return (dm * sf, dm * sm, dm * pa_ * (sf * (1.0 - sf)), dm * pb_ * (sm * (1.0 - sm))), ()
    dpa, dpb, dgf, dgm = _mm_epi(dy, w["w_out"], "nt", 512, merge_bwd, "out_proj_dx", 1024,
                                 rows=[(proj_a, 512, 2), (proj_a, 512, 4), (pa, 512), (pb, 512)],
                                 outs=[(D, 512, BF16)] * 4)
    do_a = _mm(dpa, w["w_pf"], "nt", BF16, "proj_fox_dx")
    do_b = _mm(dpb, w["w_pm"], "nt", BF16, "proj_mla_dx")
    dw["w_pf"] = _mm(o_a, dpa, "tn", BF16, "proj_fox_dw")
    dw["w_pm"] = _mm(o_b, dpb, "tn", BF16, "proj_mla_dw")
    bias = bias + send({n: dw.pop(n) for n in ("w_out", "w_pf", "w_pm")})[0, 0]

    sc_a, sc_b = 1.0 / math.sqrt(HEAD_DIM), 1.0 / math.sqrt(64 + ROPE_DIM)
    dq_a, dk_a, dv_a, dbias = _attn_grad(qkv, 0, qkv, 4, qkv, 8, do_a, lse_a, LANES, sc_a, bias, BF16, "fox_attn_bwd")
    dq_m, dk_m, dv_m = _attn_grad(q_m, 0, k_m, 0, v_m, 0, do_b, lse_b, 2 * LANES, sc_b, None, F32, "mla_attn_bwd")

    def mla_rope_bwd(dq, dk, c_, a_, b_):
        lane = lax.broadcasted_iota(jnp.int32, (1, LANES), 1)
        dqs = [_rope_t(dq[:, hd * LANES:(hd + 1) * LANES], c_, a_, b_) for hd in range(HEADS)]
        dkpe = dk[:, 0:LANES]
        for hd in range(1, HEADS):
            dkpe = dkpe + dk[:, hd * LANES:(hd + 1) * LANES]
        dkpe = jnp.where((lane >= 64) & (lane < 96), dkpe, 0.0)
        dkr = jnp.where((lane >= 64) & (lane < 96), _rope_t(dkpe, c_, a_, b_), 0.0)
        return (jnp.concatenate(dqs, axis=1), dk, dkr), ()
    dqb, dkb, dkr = _rowwise(mla_rope_bwd, [(dq_m, D, 0), (dk_m, D, 0), (ct, LANES, 0), (sa, LANES, 0), (sb, LANES, 0)],
                             [], [(D, BF16), (D, BF16), (LANES, F32)], [], "mla_rope_bwd")
    def lora_q_bwd(dq, cq, gq):
        rq = _rstd(cq)
        cqh = cq * rq
        return (_norm_bwd(dq * gq, cqh, rq),), (_colsum(dq * cqh),)
    dcq, d_g_q = _mm_epi(dqb, w["w_uq"], "nt", Q_LORA, lora_q_bwd, "mla_uq_dx", 512, rows=[(proj_a, Q_LORA, 0)],
                         vecs=[g_q], outs=[(Q_LORA, Q_LORA, BF16)], sums=[Q_LORA])
    dw["w_uq"] = _mm(cqn, dqb, "tn", BF16, "mla_uq_dw")

    def lora_kv_bwd(dv_part, dk_part, ckv, gkv):
        dkv = dv_part + dk_part
        rk = _rstd(ckv)
        ckh = ckv * rk
        return (_norm_bwd(dkv * gkv, ckh, rk),), (_colsum(dkv * ckh),)
    dckv, d_g_kv = _mm_epi(dv_m, w["w_v"], "nt", KV_LORA, lora_kv_bwd, "mla_uv_dx", 1024,
                           rows=[(_mm(dkb, w["w_k"], "nt", F32, "mla_uk_dx"), KV_LORA), (proj_a, KV_LORA, 3)],
                           vecs=[g_kv], outs=[(KV_LORA, KV_LORA, BF16)], sums=[KV_LORA])
    dw["w_k"] = _mm(ckvn, dkb, "tn", BF16, "mla_uk_dw")
    dw["w_v"] = _mm(ckvn, dv_m, "tn", BF16, "mla_uv_dw")

    dzt, d_bf = _fox_gates_bwd(dbias.reshape(HEADS, S), zt, bf)
    dmisc = (dkr + jnp.pad(jnp.transpose(dzt), ((0, 0), (0, LANES - HEADS)))).astype(BF16)
    dproj_a = jnp.concatenate([dcq, dckv, dgf, dgm, dmisc], axis=1)
    dqkv = jnp.concatenate([dq_a, dk_a, dv_a], axis=1)
    dw["w_a"] = _mm(h, dproj_a, "tn", BF16, "in_proj_a_dw")
    dw["w_b"] = _mm(h, dqkv, "tn", BF16, "in_proj_b_dw")
    tok = send(dw, True)
    dh_a = _mm(dproj_a, w["w_a"], "nt", F32, "in_proj_a_dx", dep=tok)
    g_pre_mix = g_pre_mix + relay(dh_a)[0, 0]

    def first(dh_b, dh_a, xv, dx1v, gpre, sc):
        dhv = dh_b + dh_a
        r = _rstd(xv)
        xn = xv * r
        t = dhv * xn
        dx = dx1v + _norm_bwd(dhv * (gpre * (1.0 + sc)), xn, r)
        return (dx,), (_colsum(dhv), _colsum(t) * gpre, _colsum(t) * (1.0 + sc))
    grad_x, d_shift_mix, d_scale_mix, d_g_pre_mix = _mm_epi(
        dqkv, w["w_b"], "nt", D, first, "in_proj_b_dx", 256,
        rows=[(dh_a, D), (x, D), (dx1, D)],
        vecs=[g_pre_mix, scale_mix], outs=[(D, D, F32)], sums=[D] * 3)

    dmod = jnp.concatenate([d_shift_mix, d_scale_mix, d_gate_mix, d_shift_ffn, d_scale_ffn, d_gate_ffn], axis=1)
    small = dict(dmod=dmod, g_pre_mix=d_g_pre_mix, g_post_mix=d_g_post_mix, g_pre_ffn=d_g_pre_ffn,
                 g_post_ffn=d_g_post_ffn, g_q_lora=d_g_q, g_kv_lora=d_g_kv,
                 b_forget=jnp.pad(jnp.transpose(d_bf), ((0, 0), (0, LANES - HEADS))), err=err_cols)
    return grad_x, small


SMALL_ORDER = ("dmod", "g_pre_mix", "g_post_mix", "g_pre_ffn", "g_post_ffn", "g_q_lora", "g_kv_lora", "b_forget", "err")
SMALL_PARAM = {"dmod": "b_ada"}
MATRICES = ("w_in", "w_uq", "w_ukv", "w_proj_fox", "w_proj_mla", "w_out", "w_ffn_in", "w_ffn_out")
WEIGHTS = ("w_ada", "b_ada", "g_pre_mix", "g_post_mix", "g_pre_ffn", "g_post_ffn", "w_in", "b_forget", "g_q_lora",
           "w_uq", "g_kv_lora", "w_ukv", "w_proj_fox", "w_proj_mla", "w_out", "w_ffn_in", "w_ffn_out")


def kernel(x, c, positions, w_ada, b_ada, g_pre_mix, g_post_mix, g_pre_ffn, g_post_ffn, w_in, b_forget, g_q_lora, w_uq, g_kv_lora, w_ukv, w_proj_fox, w_proj_mla, w_out, w_ffn_in, w_ffn_out, loss_target, m_w_ada, m_b_ada, m_g_pre_mix, m_g_post_mix, m_g_pre_ffn, m_g_post_ffn, m_w_in, m_b_forget, m_g_q_lora, m_w_uq, m_g_kv_lora, m_w_ukv, m_w_proj_fox, m_w_proj_mla, m_w_out, m_w_ffn_in, m_w_ffn_out, v_w_ada, v_b_ada, v_g_pre_mix, v_g_post_mix, v_g_pre_ffn, v_g_post_ffn, v_w_in, v_b_forget, v_g_q_lora, v_w_uq, v_g_kv_lora, v_w_ukv, v_w_proj_fox, v_w_proj_mla, v_w_out, v_w_ffn_in, v_w_ffn_out):
    prm = dict(w_ada=w_ada, b_ada=b_ada, g_pre_mix=g_pre_mix, g_post_mix=g_post_mix, g_pre_ffn=g_pre_ffn,
               g_post_ffn=g_post_ffn, w_in=w_in, b_forget=b_forget, g_q_lora=g_q_lora, w_uq=w_uq, g_kv_lora=g_kv_lora,
               w_ukv=w_ukv, w_proj_fox=w_proj_fox, w_proj_mla=w_proj_mla, w_out=w_out, w_ffn_in=w_ffn_in, w_ffn_out=w_ffn_out)
    mom = dict(w_ada=m_w_ada, b_ada=m_b_ada, g_pre_mix=m_g_pre_mix, g_post_mix=m_g_post_mix, g_pre_ffn=m_g_pre_ffn,
               g_post_ffn=m_g_post_ffn, w_in=m_w_in, b_forget=m_b_forget, g_q_lora=m_g_q_lora, w_uq=m_w_uq,
               g_kv_lora=m_g_kv_lora, w_ukv=m_w_ukv, w_proj_fox=m_w_proj_fox, w_proj_mla=m_w_proj_mla, w_out=m_w_out,
               w_ffn_in=m_w_ffn_in, w_ffn_out=m_w_ffn_out)
    var = dict(w_ada=v_w_ada, b_ada=v_b_ada, g_pre_mix=v_g_pre_mix, g_post_mix=v_g_post_mix, g_pre_ffn=v_g_pre_ffn,
               g_post_ffn=v_g_post_ffn, w_in=v_w_in, b_forget=v_b_forget, g_q_lora=v_g_q_lora, w_uq=v_w_uq,
               g_kv_lora=v_g_kv_lora, w_ukv=v_w_ukv, w_proj_fox=v_w_proj_fox, w_proj_mla=v_w_proj_mla, w_out=v_w_out,
               w_ffn_in=v_w_ffn_in, w_ffn_out=v_w_ffn_out)
    me = _flat(*_coords())
    slot = jnp.reshape(me, (1,)).astype(jnp.int32)

    own = {n: prm[n][0].astype(BF16) for n in MATRICES}
    w_in_all, c_all = _all_gather([own["w_in"], c], "gather_in")
    w = _prepare_weights({"w_in": w_in_all})
    c_all = c_all.reshape(N_DEV, D)
    later = dict(lora=("w_uq", "w_ukv"), proj=("w_proj_fox", "w_proj_mla", "w_out"), ffn=("w_ffn_in", "w_ffn_out"))
    states, tok = _async_start([[own[n] for n in names] for names in later.values()], "gather", w_in_all, "gather_rest_start")
    gather_state = dict(zip(later, states))

    def wts(group, after):
        srcs, lands = _async_wait(gather_state[group], after, "gather_" + group + "_wait")
        return _prepare_weights(dict(zip(later[group], lands)), dict(zip(later[group], srcs)), slot)

    sent, last = [], {}
    no_dep = jnp.zeros((8, LANES), F32)

    def send(grads, final=False):
        shards = _shard_grads(grads)
        names = list(shards)
        (state,), t = _async_start([[shards[n] for n in names]], "pair" if final else "exchange", no_dep,
                                   "exchange_" + names[0] + "_start")
        if final:
            last.update(names=names, state=state)
        else:
            sent.append((names, state))
        return t

    def relay(after):
        srcs, lands = _async_wait(last["state"], after, "exchange_pair_wait")
        sums = []
        for src, land in zip(srcs, lands):
            by_chip = src.reshape((N_DEV // 2, 2) + src.shape[1:])
            sums.append(_add_blocks(lax.dynamic_index_in_dim(by_chip, lax.axis_index("c"), 1, keepdims=False), land))
        (last["state"],), t = _async_start([sums], "chips", no_dep, "exchange_chips_start")
        return t

    ada_cols = w_ada.shape[2]
    b_cols = lax.dynamic_slice(b_ada, (0, me * ada_cols), (1, ada_cols))
    mod_cols, silu_c = _mod_part(c_all, w_ada[0], b_cols)
    (mod_all,) = _all_gather([mod_cols], "gather_mod")
    mod = lax.dynamic_index_in_dim(mod_all, me, axis=1, keepdims=False).reshape(1, 6 * D) + tok[0, 0]

    vec = dict(g_pre_mix=g_pre_mix, g_post_mix=g_post_mix, g_pre_ffn=g_pre_ffn, g_post_ffn=g_post_ffn,
               g_q_lora=g_q_lora, g_kv_lora=g_kv_lora, b_forget=b_forget)
    pos = positions.astype(F32).reshape(S, 1)
    grad_x, small = _fwd_bwd(x[0], pos, mod, loss_target[0], w, vec, wts, send, relay)

    bundle = jnp.concatenate([small[n] for n in SMALL_ORDER], axis=1)
    (small_state,), tok = _async_start([[bundle]], "gather", jnp.zeros((8, LANES), F32), "gather_small_start")

    out = {}
    after = tok
    for names, state in sent:
        srcs, lands = _async_wait(state, after, "exchange_" + names[0] + "_wait")
        for n, src, land in zip(names, srcs, lands):
            out[n] = _adamw(prm[n][0], mom[n][0], var[n][0], land, "adamw_" + n, src, slot)
            after = out[n][0]
    srcs, lands = _async_wait(last["state"], after, "exchange_chips_wait")
    for n, src, land in zip(last["names"], srcs, lands):
        out[n] = _adamw(prm[n][0], mom[n][0], var[n][0], land, "adamw_" + n, src, slot // 2)
        after = out[n][0]

    (own_bundle,), (bundle_all,) = _async_wait(small_state, after, "gather_small_wait")
    bundle_all = _with_own(bundle_all, own_bundle, me)
    dmod_all = bundle_all[:, 0, :6 * D]
    dm_cols = lax.dynamic_slice(dmod_all, (0, me * ada_cols), (N_DEV, ada_cols))
    g_ada = _w_ada_grad(jnp.transpose(silu_c), dm_cols)
    out["w_ada"] = _adamw(w_ada[0], m_w_ada[0], v_w_ada[0], g_ada[None], "adamw_w_ada")

    offsets, off = {}, 0
    for n in SMALL_ORDER:
        offsets[n] = off
        off += small[n].shape[1]
    names = [SMALL_PARAM.get(n, n) for n in SMALL_ORDER if n != "err"]
    results, err = _adamw_rows(bundle_all, [offsets[n] for n in SMALL_ORDER if n != "err"],
                               [prm[n] for n in names], [mom[n] for n in names], [var[n] for n in names],
                               offsets["err"], D)
    out.update(zip(names, results))
    loss = 0.5 * jnp.sum(err) / D

    res = [loss, grad_x[None]]
    for kind in range(4):
        for n in WEIGHTS:
            t = out[n][kind]
            res.append(t[None] if prm[n].ndim == 3 else t)
    return tuple(res)
```

```python
import functools
import math

import jax
import jax.numpy as jnp
from jax import lax
from jax.experimental import pallas as pl
from jax.experimental.pallas import tpu as pltpu

F32 = jnp.float32
BF16 = jnp.bfloat16

N_DEV = 8
S = 2048
D = 1024
D_FF = 2816
HEADS = 8
HEAD_DIM = 64
Q_LORA = 768
KV_LORA = 256
ROPE_DIM = 32
ROPE_THETA = 10000.0
NORM_EPS = 1e-6
LANES = 128
VMEM_LIMIT = 56 * 1024 * 1024

ADAM_LR = 0.001
ADAM_B1 = 0.9
ADAM_B2 = 0.999
ADAM_EPS = 1e-08
ADAM_WD = 0.01
ADAM_STEP = 10

ATT_T = 256
LOG2E = 1.4426950408889634
N_ATT = S // ATT_T

NN = (((1,), (0,)), ((), ()))
NT = (((1,), (1,)), ((), ()))
TN = (((0,), (0,)), ((), ()))
MESH = pl.DeviceIdType.MESH


def _params(sem=None):
    return pltpu.CompilerParams(dimension_semantics=sem, vmem_limit_bytes=VMEM_LIMIT)


def _pick(n, cap):
    best = None
    for t in range(LANES, cap + 1, LANES):
        if n % t == 0:
            best = t
    return best if best is not None else n


def _mm(a, b, mode, out_dtype, name, acc=None, dep=None):
    if mode == "nn":
        (m, k), (k2, n), dn = a.shape, b.shape, NN
    elif mode == "nt":
        (m, k), (n, k2), dn = a.shape, b.shape, NT
    else:
        (k, m), (k2, n), dn = a.shape, b.shape, TN
    assert k == k2, (a.shape, b.shape, mode)
    tn = _pick(n, 640)
    tm = _pick(m, 1536)
    osz = jnp.dtype(out_dtype).itemsize

    def need(tm_):
        blk = tm_ * k * 2 + tn * k * 2 + tm_ * tn * osz + (tm_ * tn * 4 if acc is not None else 0)
        return 2 * blk + tm_ * tn * 4
    while need(tm) > 36 * 1024 * 1024 and tm % 256 == 0:
        tm //= 2

    def body(*refs):
        a_ref, b_ref, o_ref = refs[0], refs[1], refs[-1]
        r = lax.dot_general(a_ref[...], b_ref[...], dn, preferred_element_type=F32)
        if acc is not None:
            r = r + refs[2][...]
        o_ref[...] = r.astype(o_ref.dtype)

    if mode == "tn":
        a_spec = pl.BlockSpec((k, tm), lambda i, j: (0, i))
    else:
        a_spec = pl.BlockSpec((tm, k), lambda i, j: (i, 0))
    if mode == "nt":
        b_spec = pl.BlockSpec((tn, k), lambda i, j: (j, 0))
    else:
        b_spec = pl.BlockSpec((k, tn), lambda i, j: (0, j))
    o_spec = pl.BlockSpec((tm, tn), lambda i, j: (i, j))
    in_specs = [a_spec, b_spec] + ([o_spec] if acc is not None else [])
    in_specs += [pl.BlockSpec(memory_space=pl.ANY)] if dep is not None else []
    args = (a, b) + ((acc,) if acc is not None else ()) + ((dep,) if dep is not None else ())
    return pl.pallas_call(
        body, name=name, grid=(m // tm, n // tn),
        in_specs=in_specs, out_specs=o_spec,
        out_shape=jax.ShapeDtypeStruct((m, n), out_dtype),
        compiler_params=_params(("parallel", "parallel")),
    )(*args)


def _mm_epi(a, b, mode, tnb, epi, name, tm, rows=(), vecs=(), outs=(), sums=()):
    m, k = a.shape
    nb = b.shape[1] if mode == "nn" else b.shape[0]
    dn = NN if mode == "nn" else NT
    n_in = 2 + len(rows) + len(vecs)

    def body(*refs):
        r = lax.dot_general(refs[0][...], refs[1][...], dn, preferred_element_type=F32)
        o_vals, s_vals = epi(r, *[x[...] for x in refs[2:n_in]])
        o_refs = refs[n_in:n_in + len(outs)]
        s_refs = refs[n_in + len(outs):]
        assert len(o_vals) == len(o_refs) and len(s_vals) == len(s_refs)
        for o_ref, val in zip(o_refs, o_vals):
            o_ref[...] = val.astype(o_ref.dtype)
        if sums:
            @pl.when((pl.program_id(0) == 0) & (pl.program_id(1) == 0))
            def _():
                for s_ref in s_refs:
                    s_ref[...] = jnp.zeros(s_ref.shape, F32)
            for s_ref, val in zip(s_refs, s_vals):
                s_ref[...] += val

    b_spec = pl.BlockSpec((k, tnb), lambda i, j: (0, j)) if mode == "nn" else pl.BlockSpec((tnb, k), lambda i, j: (j, 0))
    in_specs = [pl.BlockSpec((tm, k), lambda i, j: (i, 0)), b_spec]
    rows = [tuple(r) + (0,) * (3 - len(r)) for r in rows]
    in_specs += [pl.BlockSpec((tm, w), functools.partial(lambda i, j, off: (i, j + off), off=off)) for _, w, off in rows]
    in_specs += [pl.BlockSpec(v.shape, lambda i, j: (0, 0)) for v in vecs]
    out_specs = [pl.BlockSpec((tm, w), lambda i, j: (i, j)) for _, w, _ in outs]
    out_specs += [pl.BlockSpec((1, w), lambda i, j: (0, 0)) for w in sums]
    out_shape = [jax.ShapeDtypeStruct((m, full), dt) for full, _, dt in outs]
    out_shape += [jax.ShapeDtypeStruct((1, w), F32) for w in sums]
    return pl.pallas_call(
        body, name=name, grid=(m // tm, nb // tnb),
        in_specs=in_specs, out_specs=out_specs, out_shape=out_shape,
        compiler_params=_params(("arbitrary", "arbitrary") if sums else ("parallel", "parallel")),
    )(a, b, *[r[0] for r in rows], *vecs)


def _rowwise(fn, row_ins, vec_ins, row_outs, sum_outs, name, tm=256):
    n_in = len(row_ins) + len(vec_ins)
    n_o = len(row_outs)
    rows = row_ins[0][0].shape[0]

    def body(*refs):
        vals = [r[...] for r in refs[:n_in]]
        outs = refs[n_in:]
        ro, so = fn(*vals)
        assert len(ro) == n_o and len(so) == len(sum_outs)
        for r, v in zip(outs[:n_o], ro):
            r[...] = v.astype(r.dtype)
        if sum_outs:
            @pl.when(pl.program_id(0) == 0)
            def _():
                for r in outs[n_o:]:
                    r[...] = jnp.zeros(r.shape, F32)
            for r, v in zip(outs[n_o:], so):
                r[...] += v

    in_specs = [pl.BlockSpec((tm, w), functools.partial(lambda i, b: (i, b), b=b)) for _, w, b in row_ins]
    in_specs += [pl.BlockSpec(v.shape, lambda i: (0, 0)) for v in vec_ins]
    out_specs = [pl.BlockSpec((tm, w), lambda i: (i, 0)) for w, _ in row_outs]
    out_specs += [pl.BlockSpec((1, w), lambda i: (0, 0)) for w in sum_outs]
    out_shape = [jax.ShapeDtypeStruct((rows, w), dt) for w, dt in row_outs]
    out_shape += [jax.ShapeDtypeStruct((1, w), F32) for w in sum_outs]
    return pl.pallas_call(
        body, name=name, grid=(rows // tm,),
        in_specs=in_specs, out_specs=out_specs, out_shape=out_shape,
        compiler_params=_params(("arbitrary",)),
    )(*[a for a, _, _ in row_ins], *vec_ins)


def _sigmoid(x):
    return 1.0 / (1.0 + jnp.exp(-x))


def _rstd(x):
    return lax.rsqrt(jnp.mean(x * x, axis=-1, keepdims=True) + NORM_EPS)


def _norm_bwd(dyn, xn, r):
    return r * (dyn - xn * jnp.mean(dyn * xn, axis=-1, keepdims=True))


def _colsum(x):
    return jnp.sum(x, axis=0, keepdims=True)


def _rope_tables(pos, invf):
    def fn(p, f):
        lane = lax.broadcasted_iota(jnp.int32, (1, LANES), 1)
        ang = p * f
        cs, sn = jnp.cos(ang), jnp.sin(ang)
        rot = (lane >= 64) & (lane < 96)
        ct = jnp.where(lane < 64, 1.0, jnp.where(rot, cs, 0.0))
        sa = jnp.where((lane >= 64) & (lane < 80), -sn, 0.0)
        sb = jnp.where((lane >= 80) & (lane < 96), sn, 0.0)
        return (ct, sa, sb), ()
    return _rowwise(fn, [(pos, 1, 0)], [invf], [(LANES, F32)] * 3, [], "rope_tables")


def _rope(x, ct, sa, sb):
    return x * ct + pltpu.roll(x, LANES - 16, 1) * sa + pltpu.roll(x, 16, 1) * sb


def _rope_t(x, ct, sa, sb):
    return x * ct - pltpu.roll(x, LANES - 16, 1) * sa - pltpu.roll(x, 16, 1) * sb


def _head_mask(width, hh):
    lane = lax.broadcasted_iota(jnp.int32, (1, width), 1)
    half = width // 2
    return (lane >= hh * half) & (lane < (hh + 1) * half)


ATT_PP = 2
ATT_CHAINS = [(a, hh) for a in range(ATT_PP) for hh in range(2)]
ATT_G = HEADS // (2 * ATT_PP)


def _pair(ref_or_val, a, width, rows=slice(None)):
    return ref_or_val[rows, a * width:(a + 1) * width]


def _attn_fwd(q, qo, k, ko, v, vo, dkp, scale, bias, name):
    T = ATT_T
    assert qo % ATT_PP == 0 and ko % ATT_PP == 0 and vo % ATT_PP == 0
    qo, ko, vo = qo // ATT_PP, ko // ATT_PP, vo // ATT_PP

    def body(*refs):
        if bias is not None:
            q_ref, k_ref, v_ref, b_ref, o_ref, lse_ref, s_scr = refs
        else:
            q_ref, k_ref, v_ref, o_ref, lse_ref, s_scr = refs
        i = pl.program_id(1)
        row = lax.broadcasted_iota(jnp.int32, (T, T), 0)
        col = lax.broadcasted_iota(jnp.int32, (T, T), 1)
        qms = []
        for a, hh in ATT_CHAINS:
            qb = _pair(q_ref, a, dkp)
            qms.append(jnp.where(_head_mask(dkp, hh), qb, jnp.zeros_like(qb)))

        def fold(t):
            return [t[:, c * LANES:(c + 1) * LANES] for c in range(T // LANES)]

        def run(nt):
            mls = [jnp.full((T, LANES), -jnp.inf, F32) for _ in ATT_CHAINS]
            for j in range(nt):
                ks = slice(j * T, (j + 1) * T)
                for ci, (a, hh) in enumerate(ATT_CHAINS):
                    s = lax.dot_general(qms[ci], _pair(k_ref, a, dkp, ks), NT, preferred_element_type=F32) * (scale * LOG2E)
                    if bias is not None:
                        s = s + b_ref[2 * a + hh, j] * LOG2E
                    if j == nt - 1:
                        s = jnp.where(row >= col, s, -jnp.inf)
                    s_scr[ci, j] = s
                    for part in fold(s):
                        mls[ci] = jnp.maximum(mls[ci], part)
            ms = [jnp.max(ml, axis=1, keepdims=True) for ml in mls]
            mbs = [jnp.broadcast_to(m, (T, LANES)) for m in ms]
            for a in range(ATT_PP):
                ls = [jnp.zeros((T, LANES), F32) for _ in range(2)]
                ps, vms = [], []
                for j in range(nt):
                    vb = _pair(v_ref, a, LANES, slice(j * T, (j + 1) * T))
                    for hh in range(2):
                        parts = [jnp.exp2(part - mbs[2 * a + hh]) for part in fold(s_scr[2 * a + hh, j])]
                        for part in parts:
                            ls[hh] = ls[hh] + part
                        ps.append(jnp.concatenate(parts, axis=1).astype(BF16))
                        vms.append(jnp.where(_head_mask(LANES, hh), vb, jnp.zeros_like(vb)))
                acc = lax.dot_general(jnp.concatenate(ps, axis=1), jnp.concatenate(vms, axis=0), NN,
                                      preferred_element_type=F32)
                l0, l1 = [jnp.sum(l, axis=1, keepdims=True) for l in ls]
                lse_ref[2 * a] = ms[2 * a] + jnp.log2(l0)
                lse_ref[2 * a + 1] = ms[2 * a + 1] + jnp.log2(l1)
                inv = jnp.where(_head_mask(LANES, 0), 1.0 / l0, 1.0 / l1)
                o_ref[:, a * LANES:(a + 1) * LANES] = (acc * inv).astype(o_ref.dtype)

        for nt in range(1, N_ATT + 1):
            pl.when(i == nt - 1)(functools.partial(run, nt))

    in_specs = [
        pl.BlockSpec((T, ATT_PP * dkp), lambda g, i: (i, qo + g)),
        pl.BlockSpec((S, ATT_PP * dkp), lambda g, i: (0, ko + g)),
        pl.BlockSpec((S, ATT_PP * LANES), lambda g, i: (0, vo + g)),
    ]
    args = [q, k, v]
    if bias is not None:
        in_specs.append(pl.BlockSpec((2 * ATT_PP, N_ATT, 1, T), lambda g, i: (g, 0, 0, 0)))
        args.append(bias)
    return pl.pallas_call(
        body, name=name, grid=(ATT_G, N_ATT),
        in_specs=in_specs,
        out_specs=[pl.BlockSpec((T, ATT_PP * LANES), lambda g, i: (i, g)),
                   pl.BlockSpec((2 * ATT_PP, T, 1), lambda g, i: (g, i, 0))],
        out_shape=[jax.ShapeDtypeStruct((S, HEADS * HEAD_DIM), BF16),
                   jax.ShapeDtypeStruct((HEADS, S, 1), F32)],
        scratch_shapes=[pltpu.VMEM((len(ATT_CHAINS), N_ATT, T, T), F32)],
        compiler_params=_params(("parallel", "arbitrary")),
    )(*args)


def _attn_grad(q, qo, k, ko, v, vo, do, lse, dkp, scale, bias, qk_dtype, name):
    T = ATT_T
    has_b = bias is not None
    qo, ko, vo = qo // ATT_PP, ko // ATT_PP, vo // ATT_PP
    n_ch = len(ATT_CHAINS)

    def body(*refs):
        q_ref, k_ref, v_ref, do_ref, lse_ref = refs[:5]
        refs = refs[5:]
        if has_b:
            b_ref, refs = refs[0], refs[1:]
        dq_ref, dk_ref, dv_ref = refs[:3]
        refs = refs[3:]
        if has_b:
            db_ref, refs = refs[0], refs[1:]
        p_scr, dp_scr, dk_acc, dv_acc = refs[:4]
        db_acc = refs[4] if has_b else None
        i = pl.program_id(1)

        @pl.when(i == 0)
        def _():
            dk_acc[...] = jnp.zeros(dk_acc.shape, F32)
            dv_acc[...] = jnp.zeros(dv_acc.shape, F32)
            if has_b:
                db_acc[...] = jnp.zeros(db_acc.shape, F32)

        row = lax.broadcasted_iota(jnp.int32, (T, T), 0)
        col = lax.broadcasted_iota(jnp.int32, (T, T), 1)

        def fold(t):
            return [t[:, c * LANES:(c + 1) * LANES] for c in range(T // LANES)]

        qms, doms, lses = [], [], []
        for a, hh in ATT_CHAINS:
            qb, dob = _pair(q_ref, a, dkp), _pair(do_ref, a, LANES)
            qms.append(jnp.where(_head_mask(dkp, hh), qb, jnp.zeros_like(qb)))
            doms.append(jnp.where(_head_mask(LANES, hh), dob, jnp.zeros_like(dob)))
            lses.append(lse_ref[2 * a + hh])

        def run(nt):
            dls = [jnp.zeros((T, LANES), F32) for _ in ATT_CHAINS]
            for j in range(nt):
                ks = slice(j * T, (j + 1) * T)
                for ci, (a, hh) in enumerate(ATT_CHAINS):
                    s = lax.dot_general(qms[ci], _pair(k_ref, a, dkp, ks), NT, preferred_element_type=F32) * (scale * LOG2E)
                    if has_b:
                        s = s + b_ref[ci, j] * LOG2E
                    s = s - lses[ci]
                    if j == nt - 1:
                        s = jnp.where(row >= col, s, -jnp.inf)
                    p = jnp.exp2(s)
                    dp = lax.dot_general(doms[ci], _pair(v_ref, a, LANES, ks), NT, preferred_element_type=F32)
                    p_scr[ci, j] = p
                    dp_scr[ci, j] = dp
                    for part in fold(p * dp):
                        dls[ci] = dls[ci] + part
            deltas = [jnp.broadcast_to(jnp.sum(dl, axis=1, keepdims=True), (T, LANES)) for dl in dls]
            for a in range(ATT_PP):
                ds_all, km_all = [], []
                qm2t = jnp.transpose(jnp.concatenate([qms[2 * a], qms[2 * a + 1]], axis=0))
                dom2t = jnp.transpose(jnp.concatenate([doms[2 * a], doms[2 * a + 1]], axis=0))
                for j in range(nt):
                    ks = slice(j * T, (j + 1) * T)
                    kb = _pair(k_ref, a, dkp, ks)
                    p2, ds2 = [], []
                    for hh in range(2):
                        ci = 2 * a + hh
                        p = p_scr[ci, j]
                        ds = jnp.concatenate([pp * (dd - deltas[ci]) for pp, dd in zip(fold(p), fold(dp_scr[ci, j]))], axis=1)
                        if has_b:
                            db_acc[ci, j] += jnp.sum(ds, axis=0, keepdims=True)
                        p2.append(p.astype(BF16))
                        ds2.append((ds * scale).astype(BF16))
                        km_all.append(jnp.where(_head_mask(dkp, hh), kb, jnp.zeros_like(kb)))
                    dv_acc[a * LANES:(a + 1) * LANES, ks] += lax.dot_general(
                        dom2t, jnp.concatenate(p2, axis=0), NN, preferred_element_type=F32)
                    dk_acc[a * dkp:(a + 1) * dkp, ks] += lax.dot_general(
                        qm2t, jnp.concatenate(ds2, axis=0), NN, preferred_element_type=F32)
                    ds_all += ds2
                dq = lax.dot_general(jnp.concatenate(ds_all, axis=1), jnp.concatenate(km_all, axis=0), NN,
                                     preferred_element_type=F32)
                dq_ref[:, a * dkp:(a + 1) * dkp] = dq.astype(dq_ref.dtype)

        for nt in range(1, N_ATT + 1):
            pl.when(i == nt - 1)(functools.partial(run, nt))

        @pl.when(i == N_ATT - 1)
        def _():
            dk_ref[...] = jnp.transpose(dk_acc[...]).astype(dk_ref.dtype)
            dv_ref[...] = jnp.transpose(dv_acc[...]).astype(dv_ref.dtype)
            if has_b:
                db_ref[...] = db_acc[...]

    in_specs = [
        pl.BlockSpec((T, ATT_PP * dkp), lambda g, i: (i, qo + g)),
        pl.BlockSpec((S, ATT_PP * dkp), lambda g, i: (0, ko + g)),
        pl.BlockSpec((S, ATT_PP * LANES), lambda g, i: (0, vo + g)),
        pl.BlockSpec((T, ATT_PP * LANES), lambda g, i: (i, g)),
        pl.BlockSpec((2 * ATT_PP, T, 1), lambda g, i: (g, i, 0)),
    ]
    args = [q, k, v, do, lse]
    out_specs = [
        pl.BlockSpec((T, ATT_PP * dkp), lambda g, i: (i, g)),
        pl.BlockSpec((S, ATT_PP * dkp), lambda g, i: (0, g)),
        pl.BlockSpec((S, ATT_PP * LANES), lambda g, i: (0, g)),
    ]
    width = (HEADS // 2) * dkp
    out_shape = [
        jax.ShapeDtypeStruct((S, width), qk_dtype),
        jax.ShapeDtypeStruct((S, width), qk_dtype),
        jax.ShapeDtypeStruct((S, HEADS * HEAD_DIM), BF16),
    ]
    scratch = [pltpu.VMEM((n_ch, N_ATT, T, T), F32), pltpu.VMEM((n_ch, N_ATT, T, T), F32),
               pltpu.VMEM((ATT_PP * dkp, S), F32), pltpu.VMEM((ATT_PP * LANES, S), F32)]
    if has_b:
        bspec = pl.BlockSpec((2 * ATT_PP, N_ATT, 1, T), lambda g, i: (g, 0, 0, 0))
        in_specs.append(bspec)
        args.append(bias)
        out_specs.append(bspec)
        out_shape.append(jax.ShapeDtypeStruct((HEADS, N_ATT, 1, T), F32))
        scratch.append(pltpu.VMEM((2 * ATT_PP, N_ATT, 1, T), F32))
    return pl.pallas_call(
        body, name=name, grid=(ATT_G, N_ATT),
        in_specs=in_specs, out_specs=out_specs, out_shape=out_shape, scratch_shapes=scratch,
        compiler_params=_params(("parallel", "arbitrary")),
    )(*args)


def _tri(upper):
    a = lax.broadcasted_iota(jnp.int32, (LANES, LANES), 0)
    b = lax.broadcasted_iota(jnp.int32, (LANES, LANES), 1)
    return jnp.where(a <= b if upper else a >= b, 1.0, 0.0).astype(F32)


def _fox_gates(zt, bf):
    def body(z_ref, b_ref, o_ref):
        tri = _tri(True)
        carry = jnp.zeros((HEADS, 1), F32)
        for t in range(S // LANES):
            sl = slice(t * LANES, (t + 1) * LANES)
            z = z_ref[:, sl] + b_ref[...]
            logf = jnp.minimum(z, 0.0) - jnp.log(1.0 + jnp.exp(-jnp.abs(z)))
            c = lax.dot_general(logf, tri, NN, preferred_element_type=F32,
                                precision=lax.Precision.HIGHEST) + carry
            o_ref[:, sl] = -c
            carry = c[:, LANES - 1:LANES]

    return pl.pallas_call(
        body, name="fox_gates", out_shape=jax.ShapeDtypeStruct((HEADS, S), F32),
        compiler_params=_params(),
    )(zt, bf)


def _fox_gates_bwd(dbias, zt, bf):
    def body(d_ref, z_ref, b_ref, dz_ref, dbf_ref):
        tri = _tri(False)
        carry = jnp.zeros((HEADS, 1), F32)
        tot = jnp.zeros((HEADS, 1), F32)
        for t in reversed(range(S // LANES)):
            sl = slice(t * LANES, (t + 1) * LANES)
            df = -d_ref[:, sl]
            c = lax.dot_general(df, tri, NN, preferred_element_type=F32,
                                precision=lax.Precision.HIGHEST) + carry
            carry = c[:, 0:1]
            z = z_ref[:, sl] + b_ref[...]
            dz = c * _sigmoid(-z)
            dz_ref[:, sl] = dz
            tot = tot + jnp.sum(dz, axis=1, keepdims=True)
        dbf_ref[...] = tot

    return pl.pallas_call(
        body, name="fox_gates_bwd",
        out_shape=[jax.ShapeDtypeStruct((HEADS, S), F32), jax.ShapeDtypeStruct((HEADS, 1), F32)],
        compiler_params=_params(),
    )(dbias, zt, bf)


def _mod_part(c_all, w_ada, b_cols):
    def body(c_ref, w_ref, b_ref, o_ref, s_ref):
        c = c_ref[...]
        sc = c * _sigmoid(c)
        s_ref[...] = sc
        o_ref[...] = lax.dot_general(sc, w_ref[...], NN, preferred_element_type=F32,
                                     precision=lax.Precision.HIGHEST) + b_ref[...]

    return pl.pallas_call(
        body, name="mod_part",
        out_shape=[jax.ShapeDtypeStruct((N_DEV, w_ada.shape[1]), F32), jax.ShapeDtypeStruct(c_all.shape, F32)],
        compiler_params=_params(),
    )(c_all, w_ada, b_cols)


def _w_ada_grad(sc_t, dm):
    def body(s_ref, d_ref, o_ref):
        acc = jnp.zeros(o_ref.shape, F32)
        for b in range(N_DEV):
            acc = acc + s_ref[:, b:b + 1] * d_ref[b:b + 1, :]
        o_ref[...] = acc

    return pl.pallas_call(
        body, name="w_ada_grad", out_shape=jax.ShapeDtypeStruct((sc_t.shape[0], dm.shape[1]), F32),
        compiler_params=_params(),
    )(sc_t, dm)


def _adamw(w, m, v, parts, name, own=None, slot=None):
    rows, cols = w.shape
    n = parts.shape[0]
    tr = rows if rows <= 512 else 256

    def body(*refs):
        if own is not None:
            s_ref, refs = refs[0], refs[1:]
            w_ref, m_ref, v_ref, p_ref, o_ref, g_out, d_out, m_out, v_out = refs
            terms = [jnp.where(s_ref[0] == kk, o_ref[0], p_ref[kk]) for kk in range(n)]
        else:
            w_ref, m_ref, v_ref, p_ref, g_out, d_out, m_out, v_out = refs
            terms = [p_ref[kk] for kk in range(n)]
        g = terms[0].astype(F32)
        for term in terms[1:]:
            g = g + term.astype(F32)
        g_out[...] = g
        d_out[...], m_out[...], v_out[...] = _adamw_math(w_ref[...], g, m_ref[...], v_ref[...])

    spec = pl.BlockSpec((tr, cols), lambda i, *_: (i, 0))
    in_specs = [spec, spec, spec, pl.BlockSpec((n, tr, cols), lambda i, *_: (0, i, 0))]
    out_shape = [jax.ShapeDtypeStruct((rows, cols), F32)] * 4
    if own is None:
        return pl.pallas_call(
            body, name=name, grid=(rows // tr,), in_specs=in_specs, out_specs=[spec] * 4, out_shape=out_shape,
            compiler_params=_params(("parallel",)),
        )(w, m, v, parts)
    in_specs.append(pl.BlockSpec((1, tr, cols), lambda i, s: (s[0], i, 0)))
    return pl.pallas_call(
        body, name=name, out_shape=out_shape, compiler_params=_params(("parallel",)),
        grid_spec=pltpu.PrefetchScalarGridSpec(num_scalar_prefetch=1, grid=(rows // tr,), in_specs=in_specs,
                                               out_specs=[spec] * 4),
    )(slot, w, m, v, parts, own)


def _adamw_math(w, g, m, v):
    mm = ADAM_B1 * m + (1.0 - ADAM_B1) * g
    vv = ADAM_B2 * v + (1.0 - ADAM_B2) * (g * g)
    m_hat = mm / (1.0 - ADAM_B1 ** ADAM_STEP)
    v_hat = vv / (1.0 - ADAM_B2 ** ADAM_STEP)
    return -ADAM_LR * (m_hat / (jnp.sqrt(v_hat) + ADAM_EPS) + ADAM_WD * w), mm, vv


def _adamw_rows(bundles, offsets, ws, ms, vs, err_off, err_width):
    k = len(ws)

    def body(*refs):
        b_ref = refs[0]
        w_refs, m_refs, v_refs = refs[1:1 + k], refs[1 + k:1 + 2 * k], refs[1 + 2 * k:1 + 3 * k]
        outs = refs[1 + 3 * k:]
        g_all = b_ref[0]
        for kk in range(1, N_DEV):
            g_all = g_all + b_ref[kk]
        for i in range(k):
            width = w_refs[i].shape[1]
            g = g_all[:, offsets[i]:offsets[i] + width]
            outs[4 * i][...] = g
            outs[4 * i + 1][...], outs[4 * i + 2][...], outs[4 * i + 3][...] = _adamw_math(
                w_refs[i][...], g, m_refs[i][...], v_refs[i][...])
        outs[4 * k][...] = g_all[:, err_off:err_off + err_width]

    out_shape = []
    for w_ in ws:
        out_shape += [jax.ShapeDtypeStruct(w_.shape, F32)] * 4
    out_shape.append(jax.ShapeDtypeStruct((1, err_width), F32))
    res = pl.pallas_call(body, name="adamw_rows", out_shape=out_shape, compiler_params=_params())(bundles, *ws, *ms, *vs)
    return [tuple(res[4 * i:4 * i + 4]) for i in range(k)], res[-1]


def _coords():
    return lax.axis_index("x"), lax.axis_index("y"), lax.axis_index("c")


def _flat(px, py, pc):
    return 4 * px + 2 * py + pc


def _all_gather(arrs, name):
    n = len(arrs)

    def body(*refs):
        ins, outs = refs[:n], refs[n:2 * n]
        send, recv, lsem = refs[2 * n:]
        x, y, c = _coords()
        me, sibling = (x, y, c), (x, y, 1 - c)
        chips = [(1 - x, y), (x, 1 - y), (1 - x, 1 - y)]

        def copy(a, kk, block, to, src=None):
            slot = outs[a].at[_flat(*block)]
            return pltpu.make_async_remote_copy(
                src_ref=slot if src is None else src, dst_ref=slot,
                send_sem=send.at[a, kk], recv_sem=recv.at[a, kk],
                device_id=to, device_id_type=MESH)

        mine = [pltpu.make_async_copy(ins[a], outs[a].at[_flat(*me)], lsem.at[a]) for a in range(n)]
        for cp in mine:
            cp.start()
        first = []
        for a in range(n):
            first.append(copy(a, 0, me, sibling, src=ins[a]))
            first += [copy(a, 1 + j, me, (*chip, c), src=ins[a]) for j, chip in enumerate(chips)]
        for cp in first:
            cp.start()
        passed = []
        for j, chip in enumerate(chips):
            for a in range(n):
                copy(a, 1 + j, (*chip, c), me).wait_recv()
                cp = copy(a, 4 + j, (*chip, c), sibling)
                cp.start()
                passed.append(cp)
        for a in range(n):
            copy(a, 0, sibling, me).wait_recv()
        for j, chip in enumerate(chips):
            for a in range(n):
                copy(a, 4 + j, (*chip, 1 - c), me).wait_recv()
        for cp in first + passed:
            cp.wait_send()
        for cp in mine:
            cp.wait()

    any_spec = pl.BlockSpec(memory_space=pl.ANY)
    return pl.pallas_call(
        body, name=name,
        in_specs=[any_spec] * n, out_specs=[any_spec] * n,
        out_shape=[jax.ShapeDtypeStruct((N_DEV,) + a.shape, a.dtype) for a in arrs],
        scratch_shapes=[pltpu.SemaphoreType.DMA((n, 7)), pltpu.SemaphoreType.DMA((n, 7)),
                        pltpu.SemaphoreType.DMA((n,))],
    )(*arrs)


def _peer_list():
    x, y, c = _coords()
    return [((1 - x if r & 4 else x), (1 - y if r & 2 else y), (1 - c if r & 1 else c)) for r in range(1, N_DEV)]


def _copy_plan(mode, src, land):
    x, y, c = _coords()
    me = _flat(x, y, c)
    if mode == "gather":
        return [(src, land.at[me], peer) for peer in _peer_list()]
    if mode == "exchange":
        return [(src.at[_flat(*peer)], land.at[me], peer) for peer in _peer_list()]
    if mode == "pair":
        return [(src.at[_flat(q // 2, q % 2, 1 - c)], land.at[q], (x, y, 1 - c)) for q in range(N_DEV // 2)]
    chips = [((1 - x if r & 2 else x), (1 - y if r & 1 else y)) for r in range(1, N_DEV // 2)]
    if mode == "chips":
        return [(src.at[2 * qx + qy], land.at[2 * x + y], (qx, qy, c)) for qx, qy in chips]
    if mode == "spread":
        return [(src, land.at[me], (x, y, 1 - c))] + [(src, land.at[me], (qx, qy, c)) for qx, qy in chips]
    assert mode == "forward"
    return [(land.at[_flat(qx, qy, c)], land.at[_flat(qx, qy, c)], (x, y, 1 - c)) for qx, qy in chips]


N_COPIES = dict(gather=N_DEV - 1, exchange=N_DEV - 1, pair=N_DEV // 2, chips=N_DEV // 2 - 1, spread=N_DEV // 2,
                forward=N_DEV // 2 - 1)


def _land_shape(mode, shape):
    return {"gather": (N_DEV,) + shape, "spread": (N_DEV,) + shape, "exchange": shape,
            "pair": (N_DEV // 2,) + shape[1:], "chips": shape}[mode]


HBM_SPEC = pl.BlockSpec(memory_space=pltpu.HBM)
SEM_SPEC = pl.BlockSpec(memory_space=pltpu.SEMAPHORE)
ANY_SPEC = pl.BlockSpec(memory_space=pl.ANY)
SIDE_EFFECT = pltpu.SideEffectType.DATAFLOW_SIDE_EFFECTING


def _async_start(groups, modes, after, name):
    modes = [modes] * len(groups) if isinstance(modes, str) else list(modes)
    arrs = [(a, m) for g, m in zip(groups, modes) for a in g]
    n = len(arrs)
    fresh = [i for i, (_, m) in enumerate(arrs) if m != "forward"]

    def body(*refs):
        srcs, new_lands = refs[:n], refs[n:n + len(fresh)]
        outs = refs[n + len(fresh) + 1:]
        lands = list(srcs)
        for k, i in enumerate(fresh):
            lands[i] = new_lands[k]
        for ai, (_, mode) in enumerate(arrs):
            for src_ref, dst_ref, peer in _copy_plan(mode, srcs[ai], lands[ai]):
                pltpu.make_async_remote_copy(src_ref=src_ref, dst_ref=dst_ref, send_sem=outs[2 * ai],
                                             recv_sem=outs[2 * ai + 1], device_id=peer, device_id_type=MESH).start()
        outs[-1][...] = jnp.zeros(outs[-1].shape, F32)

    land_shapes = [(_land_shape(arrs[i][1], arrs[i][0].shape), arrs[i][0].dtype) for i in fresh]
    n_buf = n + len(fresh)
    out_shape = [pltpu.SemaphoreType.DMA(())] * (2 * n)
    out_shape += [pltpu.HBM(a.shape, a.dtype) for a, _ in arrs]
    out_shape += [pltpu.HBM(shape, dt) for shape, dt in land_shapes]
    out_shape.append(jax.ShapeDtypeStruct((8, LANES), F32))
    res = pl.pallas_call(
        body, name=name, out_shape=tuple(out_shape),
        in_specs=[HBM_SPEC] * n_buf + [ANY_SPEC],
        out_specs=tuple([SEM_SPEC] * (2 * n) + [HBM_SPEC] * n_buf + [pl.BlockSpec(memory_space=pltpu.VMEM)]),
        input_output_aliases={i: 2 * n + i for i in range(n_buf)},
        compiler_params=pltpu.CompilerParams(has_side_effects=SIDE_EFFECT),
    )(*[pltpu.with_memory_space_constraint(a, pltpu.HBM) for a, _ in arrs],
      *[pltpu.with_memory_space_constraint(lax.empty(shape, dt), pltpu.HBM) for shape, dt in land_shapes],
      after)
    sems, thru = res[:2 * n], res[2 * n:-1]
    land_of = {i: thru[n + k] for k, i in enumerate(fresh)}
    states, idx = [], 0
    for g, mode in zip(groups, modes):
        ids = range(idx, idx + len(g))
        idx += len(g)
        states.append(([sems[2 * i] for i in ids], [sems[2 * i + 1] for i in ids],
                       None if mode == "forward" else [thru[i] for i in ids],
                       [land_of.get(i, thru[i]) for i in ids], mode))
    return states, res[-1]


def _async_wait(state, after, name):
    sends, recvs, srcs, lands, mode = state
    g = len(lands)
    bufs = (list(srcs) if srcs is not None else []) + list(lands)
    nb = len(bufs)

    def body(*refs):
        l_refs, sems = refs[nb - g:nb], refs[nb:nb + 2 * g]
        for ai in range(g):
            moved = l_refs[ai].at[pl.ds(0, N_COPIES[mode])]
            cp = pltpu.make_async_remote_copy(src_ref=moved, dst_ref=moved, send_sem=sems[ai], recv_sem=sems[g + ai],
                                              device_id=_coords(), device_id_type=MESH)
            cp.wait_send()
            cp.wait_recv()

    res = pl.pallas_call(
        body, name=name,
        out_shape=tuple(pltpu.HBM(a.shape, a.dtype) for a in bufs),
        in_specs=[HBM_SPEC] * nb + [SEM_SPEC] * (2 * g) + [ANY_SPEC],
        out_specs=tuple([HBM_SPEC] * nb),
        input_output_aliases={i: i for i in range(nb)},
        compiler_params=pltpu.CompilerParams(has_side_effects=SIDE_EFFECT),
    )(*bufs, *sends, *recvs, after)
    return (list(res[:nb - g]) if srcs is not None else None), list(res[nb - g:])


def _add_blocks(a, b):
    def body(a_ref, b_ref, o_ref):
        o_ref[...] = (a_ref[...].astype(F32) + b_ref[...].astype(F32)).astype(o_ref.dtype)

    spec = pl.BlockSpec((1,) + a.shape[1:], lambda i: (i, 0, 0))
    return pl.pallas_call(
        body, name="add_blocks", grid=(a.shape[0],), in_specs=[spec, spec], out_specs=spec,
        out_shape=jax.ShapeDtypeStruct(a.shape, a.dtype), compiler_params=_params(("parallel",)),
    )(a, b)


def _with_own(land, own, me):
    return lax.dynamic_update_index_in_dim(land, own, me, 0)


IN_SPLITS = (512, 512, 512, 8, 768, 256, 32, 1024, 1024)


def _from_shards(g, fn, out_widths, name, own=None, slot=None):
    _, k, n = g.shape
    tr = min(k, 256)

    def body(*refs):
        if own is not None:
            s_ref, g_ref, own_ref = refs[:3]
            cols = [jnp.where(s_ref[0] == j, own_ref[...], g_ref[j]) for j in range(N_DEV)]
        else:
            g_ref = refs[0]
            cols = [g_ref[j] for j in range(N_DEV)]
        for o_ref, val in zip(refs[-len(out_widths):], fn(jnp.concatenate(cols, axis=1))):
            o_ref[...] = val

    in_specs = [pl.BlockSpec((N_DEV, tr, n), lambda i, *_: (0, i, 0))]
    out_spec = [pl.BlockSpec((tr, wd), lambda i, *_: (i, 0)) for wd in out_widths]
    out_shape = [jax.ShapeDtypeStruct((k, wd), g.dtype) for wd in out_widths]
    if own is None:
        return pl.pallas_call(body, name=name, grid=(k // tr,), in_specs=in_specs, out_specs=out_spec,
                              out_shape=out_shape, compiler_params=_params(("parallel",)))(g)
    in_specs.append(pl.BlockSpec((tr, n), lambda i, *_: (i, 0)))
    return pl.pallas_call(
        body, name=name, out_shape=out_shape, compiler_params=_params(("parallel",)),
        grid_spec=pltpu.PrefetchScalarGridSpec(num_scalar_prefetch=1, grid=(k // tr,), in_specs=in_specs, out_specs=out_spec),
    )(slot, g, own)


def _unshard_cols(g, own=None, slot=None):
    return _from_shards(g, lambda full: (full,), [N_DEV * g.shape[2]], "unshard_cols_%d" % g.shape[2], own, slot)[0]


FFN_T = 256
FFN_SHARD = 2 * D_FF // N_DEV


def _unshard_ffn_in(g, own=None, slot=None):
    def pairs(full):
        parts = []
        for j in range(D_FF // FFN_T):
            parts += [full[:, j * FFN_T:(j + 1) * FFN_T], full[:, D_FF + j * FFN_T:D_FF + (j + 1) * FFN_T]]
        return (jnp.concatenate(parts, axis=1),)

    return _from_shards(g, pairs, [2 * D_FF], "unshard_ffn_in", own, slot)[0]


def _shard_ffn_in(w):
    tr = 256

    def body(w_ref, o_ref):
        x = w_ref[...]
        nb = D_FF // FFN_T
        full = jnp.concatenate([x[:, (2 * j + half) * FFN_T:(2 * j + half + 1) * FFN_T]
                                for half in range(2) for j in range(nb)], axis=1)
        for j in range(N_DEV):
            o_ref[j] = full[:, j * FFN_SHARD:(j + 1) * FFN_SHARD]

    return pl.pallas_call(
        body, name="shard_ffn_in", grid=(D // tr,),
        in_specs=[pl.BlockSpec((tr, 2 * D_FF), lambda i: (i, 0))],
        out_specs=pl.BlockSpec((N_DEV, tr, FFN_SHARD), lambda i: (0, i, 0)),
        out_shape=jax.ShapeDtypeStruct((N_DEV, D, FFN_SHARD), w.dtype),
        compiler_params=_params(("parallel",)),
    )(w)


def _shard_cols(w):
    k, n = w.shape[0], w.shape[1] // N_DEV
    tr = min(k, 256)

    def body(w_ref, o_ref):
        full = w_ref[...]
        for j in range(N_DEV):
            o_ref[j] = full[:, j * n:(j + 1) * n]

    return pl.pallas_call(
        body, name="shard_cols_%d" % n, grid=(k // tr,),
        in_specs=[pl.BlockSpec((tr, N_DEV * n), lambda i: (i, 0))],
        out_specs=pl.BlockSpec((N_DEV, tr, n), lambda i: (0, i, 0)),
        out_shape=jax.ShapeDtypeStruct((N_DEV, k, n), w.dtype),
        compiler_params=_params(("parallel",)),
    )(w)


IN_OFFS = tuple(sum(IN_SPLITS[:i]) for i in range(len(IN_SPLITS) + 1))
IN_SHARD = IN_OFFS[-1] // N_DEV
REGROUP_ROWS = 128


def _w_in_regroup(g, own=None, slot=None):
    def groups(full):
        fq, fk, fv, wf, cq, ckv, kr, gf, gm = [full[:, IN_OFFS[i]:IN_OFFS[i + 1]] for i in range(9)]
        rows = full.shape[0]
        w_a = jnp.concatenate([cq, ckv, gf, gm, wf, jnp.zeros((rows, 56), BF16), kr, jnp.zeros((rows, 32), BF16)], axis=1)
        return w_a, jnp.concatenate([fq, fk, fv], axis=1)

    return _from_shards(g, groups, [3200, 1536], "w_in_regroup", own, slot)


def _w_in_ungroup(da, db_):
    def body(a_ref, b_ref, o_ref):
        a = a_ref[...]
        full = jnp.concatenate([b_ref[...], a[:, 3072:3080], a[:, 0:768], a[:, 768:1024], a[:, 3136:3168],
                                a[:, 1024:3072]], axis=1)
        for j in range(N_DEV):
            o_ref[j] = full[:, j * IN_SHARD:(j + 1) * IN_SHARD]

    tr = REGROUP_ROWS
    return pl.pallas_call(
        body, name="w_in_ungroup", grid=(D // tr,),
        in_specs=[pl.BlockSpec((tr, 3200), lambda i: (i, 0)), pl.BlockSpec((tr, 1536), lambda i: (i, 0))],
        out_specs=pl.BlockSpec((N_DEV, tr, IN_SHARD), lambda i: (0, i, 0)),
        out_shape=jax.ShapeDtypeStruct((N_DEV, D, IN_SHARD), BF16),
        compiler_params=_params(("parallel",)),
    )(da, db_)


def _prepare_weights(g, own=None, slot=None):
    w = {}
    if own is not None:
        small = ("w_uq", "w_ukv", "w_out", "w_ffn_out")
        g = {n: (_with_own(a, own[n], slot[0]) if n in small else a) for n, a in g.items()}
    pick = (lambda n: (own[n], slot)) if own is not None else (lambda n: (None, None))
    if "w_in" in g:
        w["w_a"], w["w_b"] = _w_in_regroup(g["w_in"], *pick("w_in"))
    if "w_uq" in g:
        w_uq = g["w_uq"].reshape(Q_LORA, HEADS, 96)
        w["w_uq"] = jnp.pad(w_uq, ((0, 0), (0, 0), (0, 32))).reshape(Q_LORA, HEADS * LANES)
        ukv = g["w_ukv"]
        w["w_k"] = jnp.transpose(jnp.pad(ukv[:, :, :64], ((0, 0), (0, 0), (0, 64))), (1, 0, 2)).reshape(KV_LORA, HEADS * LANES)
        w["w_v"] = jnp.transpose(ukv[:, :, 64:], (1, 0, 2)).reshape(KV_LORA, HEADS * HEAD_DIM)
    if "w_out" in g:
        w["w_pf"] = _unshard_cols(g["w_proj_fox"], *pick("w_proj_fox"))
        w["w_pm"] = _unshard_cols(g["w_proj_mla"], *pick("w_proj_mla"))
        w["w_out"] = g["w_out"].reshape(D, D)
    if "w_ffn_in" in g:
        w["w_ffn_in"] = _unshard_ffn_in(g["w_ffn_in"], *pick("w_ffn_in"))
        w["w_ffn_out"] = g["w_ffn_out"].reshape(D_FF, D)
    return w


def _shard_grads(dw):
    out = {}
    if "w_a" in dw:
        out["w_in"] = _w_in_ungroup(dw["w_a"], dw["w_b"])
    if "w_uq" in dw:
        w_uq = dw["w_uq"].reshape(Q_LORA, HEADS, LANES)[:, :, :96].reshape(Q_LORA, Q_LORA)
        out["w_uq"] = w_uq.reshape(N_DEV, Q_LORA // N_DEV, Q_LORA)
        k_part = dw["w_k"].reshape(KV_LORA, HEADS, LANES)[:, :, :64]
        v_part = dw["w_v"].reshape(KV_LORA, HEADS, HEAD_DIM)
        out["w_ukv"] = jnp.transpose(jnp.concatenate([k_part, v_part], axis=2), (1, 0, 2))
    if "w_out" in dw:
        out["w_proj_fox"] = _shard_cols(dw["w_pf"])
        out["w_proj_mla"] = _shard_cols(dw["w_pm"])
        out["w_out"] = dw["w_out"].reshape(N_DEV, D // N_DEV, D)
    if "w_ffn_in" in dw:
        out["w_ffn_in"] = _shard_ffn_in(dw["w_ffn_in"])
        out["w_ffn_out"] = dw["w_ffn_out"].reshape(N_DEV, D_FF // N_DEV, D)
    return out


def _fwd_bwd(x, pos, mod, target, w, vec, wts, send, relay):
    shift_mix, scale_mix, gate_mix, shift_ffn, scale_ffn, gate_ffn = [mod[:, i * D:(i + 1) * D] for i in range(6)]
    g_pre_mix, g_post_mix, g_pre_ffn, g_post_ffn = vec["g_pre_mix"], vec["g_post_mix"], vec["g_pre_ffn"], vec["g_post_ffn"]
    g_q, g_kv = vec["g_q_lora"], vec["g_kv_lora"]

    inv_freq = 1.0 / (ROPE_THETA ** (jnp.arange(0, ROPE_DIM, 2, dtype=F32) / ROPE_DIM))
    invf = jnp.concatenate([jnp.zeros((64,), F32), inv_freq, inv_freq, jnp.zeros((32,), F32)]).reshape(1, LANES)
    ct, sa, sb = _rope_tables(pos, invf)

    def pre1(xv, g, sc, sh):
        return ((xv * _rstd(xv) * g) * (1.0 + sc) + sh,), ()
    (h,) = _rowwise(pre1, [(x, D, 0)], [g_pre_mix, scale_mix, shift_mix], [(D, BF16)], [], "pre_mix")
    proj_a = _mm(h, w["w_a"], "nn", F32, "in_proj_a")
    qkv = _mm(h, w["w_b"], "nn", BF16, "in_proj_b")

    def lora_norm(cq, ckv, gq, gkv):
        return (cq * _rstd(cq) * gq, ckv * _rstd(ckv) * gkv), ()
    cqn, ckvn = _rowwise(lora_norm, [(proj_a, Q_LORA, 0), (proj_a, KV_LORA, 3)], [g_q, g_kv],
                         [(Q_LORA, BF16), (KV_LORA, BF16)], [], "lora_norm")
    w = {**w, **wts("lora", cqn)}
    tables = [(ct, LANES), (sa, LANES), (sb, LANES)]

    def rope_q(qv, c_, a_, b_):
        return (jnp.concatenate([_rope(qv[:, hd * LANES:(hd + 1) * LANES], c_, a_, b_) for hd in range(HEADS)], axis=1),), ()
    (q_m,) = _mm_epi(cqn, w["w_uq"], "nn", D, rope_q, "mla_uq", 512, rows=tables, outs=[(D, D, BF16)])

    def rope_k(kv, misc, c_, a_, b_):
        lane = lax.broadcasted_iota(jnp.int32, (1, LANES), 1)
        kpe = jnp.where((lane >= 64) & (lane < 96), _rope(misc, c_, a_, b_), 0.0)
        return (jnp.concatenate([kv[:, hd * LANES:(hd + 1) * LANES] + kpe for hd in range(HEADS)], axis=1),), ()
    (k_m,) = _mm_epi(ckvn, w["w_k"], "nn", D, rope_k, "mla_uk", 512, rows=[(proj_a, LANES, 24)] + tables,
                     outs=[(D, D, BF16)])
    v_m = _mm(ckvn, w["w_v"], "nn", BF16, "mla_uv")

    zt = jnp.transpose(proj_a[:, 3072:3080])
    bf = jnp.transpose(vec["b_forget"])
    neg_f = _fox_gates(zt, bf)
    bias = neg_f.reshape(HEADS, N_ATT, 1, ATT_T)
    o_b, lse_b = _attn_fwd(q_m, 0, k_m, 0, v_m, 0, 2 * LANES, 1.0 / math.sqrt(64 + ROPE_DIM), None, "mla_attn")
    bias = bias + wts("relay", o_b)["tok"][0, 0]
    o_a, lse_a = _attn_fwd(qkv, 0, qkv, 4, qkv, 8, LANES, 1.0 / math.sqrt(HEAD_DIM), bias, "fox_attn")

    w = {**w, **wts("proj", o_a)}
    pa = _mm(o_a, w["w_pf"], "nn", BF16, "proj_fox")

    def merge(pb_, gf, gm, pa_):
        return (_sigmoid(gf) * pa_ + _sigmoid(gm) * pb_, pb_), ()
    merged, pb = _mm_epi(o_b, w["w_pm"], "nn", 512, merge, "proj_mla", 1024,
                         rows=[(proj_a, 512, 2), (proj_a, 512, 4), (pa, 512)], outs=[(D, 512, BF16), (D, 512, BF16)])
    def post1(yv, xv, gate, gpost, gpre, sc, sh):
        x1 = xv + gate * (yv * _rstd(yv) * gpost)
        return (x1, (x1 * _rstd(x1) * gpre) * (1.0 + sc) + sh, yv), ()
    x1, h2, y = _mm_epi(merged, w["w_out"], "nn", D, post1, "out_proj", 256, rows=[(x, D)],
                        vecs=[gate_mix, g_post_mix, g_pre_ffn, scale_ffn, shift_ffn],
                        outs=[(D, D, F32), (D, D, BF16), (D, D, F32)])
    w = {**w, **wts("ffn", h2)}

    def swiglu(r):
        g, u = r[:, :FFN_T], r[:, FFN_T:]
        return (g * _sigmoid(g) * u, r), ()
    act, gu = _mm_epi(h2, w["w_ffn_in"], "nn", 2 * FFN_T, swiglu, "ffn_in", 1024,
                      outs=[(D_FF, FFN_T, BF16), (2 * D_FF, 2 * FFN_T, BF16)])

    def head(y2v, x1v, tv, gate, gpost):
        r = _rstd(y2v)
        yn = y2v * r
        n2 = yn * gpost
        err = (x1v + gate * n2) - tv
        dx2 = err * (1.0 / D)
        dn2 = dx2 * gate
        dy2 = _norm_bwd(dn2 * gpost, yn, r)
        return (dx2, dy2), (_colsum(err * err), _colsum(dx2 * n2), _colsum(dn2 * yn))
    dx2, dy2, err_cols, d_gate_ffn, d_g_post_ffn = _mm_epi(
        act, w["w_ffn_out"], "nn", D, head, "ffn_out", 256, rows=[(x1, D), (target, D)], vecs=[gate_ffn, g_post_ffn],
        outs=[(D, D, F32), (D, D, BF16)], sums=[D, D, D])

    def swiglu_bwd(da, guv):
        g, u = guv[:, :FFN_T].astype(F32), guv[:, FFN_T:].astype(F32)
        sg = _sigmoid(g)
        return (jnp.concatenate([da * u * (sg * (1.0 + g * (1.0 - sg))), da * (g * sg)], axis=1),), ()
    (dgu,) = _mm_epi(dy2, w["w_ffn_out"], "nt", FFN_T, swiglu_bwd, "ffn_out_dx", 1024, rows=[(gu, 2 * FFN_T)],
                     outs=[(2 * D_FF, 2 * FFN_T, BF16)])
    dw = {"w_ffn_out": _mm(act, dy2, "tn", BF16, "ffn_out_dw")}
    dw["w_ffn_in"] = _mm(h2, dgu, "tn", BF16, "ffn_in_dw")
    gate_mix = gate_mix + send({n: dw.pop(n) for n in ("w_ffn_in", "w_ffn_out")})[0, 0]

    def mid(dh, x1v, dx2v, yv, gpre, sc, gate, gpost):
        r2 = _rstd(x1v)
        x1n = x1v * r2
        t = dh * x1n
        dx1 = dx2v + _norm_bwd(dh * (gpre * (1.0 + sc)), x1n, r2)
        ry = _rstd(yv)
        yn = yv * ry
        dn1 = dx1 * gate
        dy = _norm_bwd(dn1 * gpost, yn, ry)
        sums = (_colsum(dh), _colsum(t) * gpre, _colsum(t) * (1.0 + sc), _colsum(dx1 * (yn * gpost)), _colsum(dn1 * yn))
        return (dx1, dy), sums
    dx1, dy, d_shift_ffn, d_scale_ffn, d_g_pre_ffn, d_gate_mix, d_g_post_mix = _mm_epi(
        dgu, w["w_ffn_in"], "nt", D, mid, "ffn_in_dx", 256, rows=[(x1, D), (dx2, D), (y, D)],
        vecs=[g_pre_ffn, scale_ffn, gate_mix, g_post_mix], outs=[(D, D, F32), (D, D, BF16)], sums=[D] * 5)

    dw["w_out"] = _mm(merged, dy, "tn", BF16, "out_proj_dw")

    def merge_bwd(dm, gf, gm, pa_, pb_):
        sf, sm = _sigmoid(gf), _sigmoid(gm)
        return (dm * sf, dm * sm, dm * pa_ * (sf * (1.0 - sf)), dm * pb_ * (sm * (1.0 - sm))), ()
    dpa, dpb, dgf, dgm = _mm_epi(dy, w["w_out"], "nt", 512, merge_bwd, "out_proj_dx", 1024,
                                 rows=[(proj_a, 512, 2), (proj_a, 512, 4), (pa, 512), (pb, 512)],
                                 outs=[(D, 512, BF16)] * 4)
    do_a = _mm(dpa, w["w_pf"], "nt", BF16, "proj_fox_dx")
    do_b = _mm(dpb, w["w_pm"], "nt", BF16, "proj_mla_dx")
    dw["w_pf"] = _mm(o_a, dpa, "tn", BF16, "proj_fox_dw")
    dw["w_pm"] = _mm(o_b, dpb, "tn", BF16, "proj_mla_dw")
    bias = bias + send({n: dw.pop(n) for n in ("w_out", "w_pf", "w_pm")})[0, 0]

    sc_a, sc_b = 1.0 / math.sqrt(HEAD_DIM), 1.0 / math.sqrt(64 + ROPE_DIM)
    dq_a, dk_a, dv_a, dbias = _attn_grad(qkv, 0, qkv, 4, qkv, 8, do_a, lse_a, LANES, sc_a, bias, BF16, "fox_attn_bwd")
    dq_m, dk_m, dv_m = _attn_grad(q_m, 0, k_m, 0, v_m, 0, do_b, lse_b, 2 * LANES, sc_b, None, F32, "mla_attn_bwd")

    def mla_rope_bwd(dq, dk, c_, a_, b_):
        lane = lax.broadcasted_iota(jnp.int32, (1, LANES), 1)
        dqs = [_rope_t(dq[:, hd * LANES:(hd + 1) * LANES], c_, a_, b_) for hd in range(HEADS)]
        dkpe = dk[:, 0:LANES]
        for hd in range(1, HEADS):
            dkpe = dkpe + dk[:, hd * LANES:(hd + 1) * LANES]
        dkpe = jnp.where((lane >= 64) & (lane < 96), dkpe, 0.0)
        dkr = jnp.where((lane >= 64) & (lane < 96), _rope_t(dkpe, c_, a_, b_), 0.0)
        return (jnp.concatenate(dqs, axis=1), dk, dkr), ()
    dqb, dkb, dkr = _rowwise(mla_rope_bwd, [(dq_m, D, 0), (dk_m, D, 0), (ct, LANES, 0), (sa, LANES, 0), (sb, LANES, 0)],
                             [], [(D, BF16), (D, BF16), (LANES, F32)], [], "mla_rope_bwd")
    def lora_q_bwd(dq, cq, gq):
        rq = _rstd(cq)
        cqh = cq * rq
        return (_norm_bwd(dq * gq, cqh, rq),), (_colsum(dq * cqh),)
    dcq, d_g_q = _mm_epi(dqb, w["w_uq"], "nt", Q_LORA, lora_q_bwd, "mla_uq_dx", 512, rows=[(proj_a, Q_LORA, 0)],
                         vecs=[g_q], outs=[(Q_LORA, Q_LORA, BF16)], sums=[Q_LORA])
    dw["w_uq"] = _mm(cqn, dqb, "tn", BF16, "mla_uq_dw")

    def lora_kv_bwd(dv_part, dk_part, ckv, gkv):
        dkv = dv_part + dk_part
        rk = _rstd(ckv)
        ckh = ckv * rk
        return (_norm_bwd(dkv * gkv, ckh, rk),), (_colsum(dkv * ckh),)
    dckv, d_g_kv = _mm_epi(dv_m, w["w_v"], "nt", KV_LORA, lora_kv_bwd, "mla_uv_dx", 1024,
                           rows=[(_mm(dkb, w["w_k"], "nt", F32, "mla_uk_dx"), KV_LORA), (proj_a, KV_LORA, 3)],
                           vecs=[g_kv], outs=[(KV_LORA, KV_LORA, BF16)], sums=[KV_LORA])
    dw["w_k"] = _mm(ckvn, dkb, "tn", BF16, "mla_uk_dw")
    dw["w_v"] = _mm(ckvn, dv_m, "tn", BF16, "mla_uv_dw")

    dzt, d_bf = _fox_gates_bwd(dbias.reshape(HEADS, S), zt, bf)
    dmisc = (dkr + jnp.pad(jnp.transpose(dzt), ((0, 0), (0, LANES - HEADS)))).astype(BF16)
    dproj_a = jnp.concatenate([dcq, dckv, dgf, dgm, dmisc], axis=1)
    dqkv = jnp.concatenate([dq_a, dk_a, dv_a], axis=1)
    dw["w_a"] = _mm(h, dproj_a, "tn", BF16, "in_proj_a_dw")
    dw["w_b"] = _mm(h, dqkv, "tn", BF16, "in_proj_b_dw")
    tok = send(dw, True)
    dh_a = _mm(dproj_a, w["w_a"], "nt", F32, "in_proj_a_dx", dep=tok)
    g_pre_mix = g_pre_mix + relay(dh_a)[0, 0]

    def first(dh_b, dh_a, xv, dx1v, gpre, sc):
        dhv = dh_b + dh_a
        r = _rstd(xv)
        xn = xv * r
        t = dhv * xn
        dx = dx1v + _norm_bwd(dhv * (gpre * (1.0 + sc)), xn, r)
        return (dx,), (_colsum(dhv), _colsum(t) * gpre, _colsum(t) * (1.0 + sc))
    grad_x, d_shift_mix, d_scale_mix, d_g_pre_mix = _mm_epi(
        dqkv, w["w_b"], "nt", D, first, "in_proj_b_dx", 256,
        rows=[(dh_a, D), (x, D), (dx1, D)],
        vecs=[g_pre_mix, scale_mix], outs=[(D, D, F32)], sums=[D] * 3)

    dmod = jnp.concatenate([d_shift_mix, d_scale_mix, d_gate_mix, d_shift_ffn, d_scale_ffn, d_gate_ffn], axis=1)
    small = dict(dmod=dmod, g_pre_mix=d_g_pre_mix, g_post_mix=d_g_post_mix, g_pre_ffn=d_g_pre_ffn,
                 g_post_ffn=d_g_post_ffn, g_q_lora=d_g_q, g_kv_lora=d_g_kv,
                 b_forget=jnp.pad(jnp.transpose(d_bf), ((0, 0), (0, LANES - HEADS))), err=err_cols)
    return grad_x, small


SMALL_ORDER = ("dmod", "g_pre_mix", "g_post_mix", "g_pre_ffn", "g_post_ffn", "g_q_lora", "g_kv_lora", "b_forget", "err")
SMALL_PARAM = {"dmod": "b_ada"}
MATRICES = ("w_in", "w_uq", "w_ukv", "w_proj_fox", "w_proj_mla", "w_out", "w_ffn_in", "w_ffn_out")
WEIGHTS = ("w_ada", "b_ada", "g_pre_mix", "g_post_mix", "g_pre_ffn", "g_post_ffn", "w_in", "b_forget", "g_q_lora",
           "w_uq", "g_kv_lora", "w_ukv", "w_proj_fox", "w_proj_mla", "w_out", "w_ffn_in", "w_ffn_out")


def kernel(x, c, positions, w_ada, b_ada, g_pre_mix, g_post_mix, g_pre_ffn, g_post_ffn, w_in, b_forget, g_q_lora, w_uq, g_kv_lora, w_ukv, w_proj_fox, w_proj_mla, w_out, w_ffn_in, w_ffn_out, loss_target, m_w_ada, m_b_ada, m_g_pre_mix, m_g_post_mix, m_g_pre_ffn, m_g_post_ffn, m_w_in, m_b_forget, m_g_q_lora, m_w_uq, m_g_kv_lora, m_w_ukv, m_w_proj_fox, m_w_proj_mla, m_w_out, m_w_ffn_in, m_w_ffn_out, v_w_ada, v_b_ada, v_g_pre_mix, v_g_post_mix, v_g_pre_ffn, v_g_post_ffn, v_w_in, v_b_forget, v_g_q_lora, v_w_uq, v_g_kv_lora, v_w_ukv, v_w_proj_fox, v_w_proj_mla, v_w_out, v_w_ffn_in, v_w_ffn_out):
    prm = dict(w_ada=w_ada, b_ada=b_ada, g_pre_mix=g_pre_mix, g_post_mix=g_post_mix, g_pre_ffn=g_pre_ffn,
               g_post_ffn=g_post_ffn, w_in=w_in, b_forget=b_forget, g_q_lora=g_q_lora, w_uq=w_uq, g_kv_lora=g_kv_lora,
               w_ukv=w_ukv, w_proj_fox=w_proj_fox, w_proj_mla=w_proj_mla, w_out=w_out, w_ffn_in=w_ffn_in, w_ffn_out=w_ffn_out)
    mom = dict(w_ada=m_w_ada, b_ada=m_b_ada, g_pre_mix=m_g_pre_mix, g_post_mix=m_g_post_mix, g_pre_ffn=m_g_pre_ffn,
               g_post_ffn=m_g_post_ffn, w_in=m_w_in, b_forget=m_b_forget, g_q_lora=m_g_q_lora, w_uq=m_w_uq,
               g_kv_lora=m_g_kv_lora, w_ukv=m_w_ukv, w_proj_fox=m_w_proj_fox, w_proj_mla=m_w_proj_mla, w_out=m_w_out,
               w_ffn_in=m_w_ffn_in, w_ffn_out=m_w_ffn_out)
    var = dict(w_ada=v_w_ada, b_ada=v_b_ada, g_pre_mix=v_g_pre_mix, g_post_mix=v_g_post_mix, g_pre_ffn=v_g_pre_ffn,
               g_post_ffn=v_g_post_ffn, w_in=v_w_in, b_forget=v_b_forget, g_q_lora=v_g_q_lora, w_uq=v_w_uq,
               g_kv_lora=v_g_kv_lora, w_ukv=v_w_ukv, w_proj_fox=v_w_proj_fox, w_proj_mla=v_w_proj_mla, w_out=v_w_out,
               w_ffn_in=v_w_ffn_in, w_ffn_out=v_w_ffn_out)
    me = _flat(*_coords())
    slot = jnp.reshape(me, (1,)).astype(jnp.int32)

    own = {n: prm[n][0].astype(BF16) for n in MATRICES}
    no_dep = jnp.zeros((8, LANES), F32)
    (st_c, st_in), tok = _async_start([[c], [own["w_in"]]], ["gather", "spread"], no_dep, "gather_in_start")
    (c_own,), (c_land,) = _async_wait(st_c, tok, "gather_c_wait")
    c_all = _with_own(c_land, c_own, me).reshape(N_DEV, D)
    ada_cols = w_ada.shape[2]
    b_cols = lax.dynamic_slice(b_ada, (0, me * ada_cols), (1, ada_cols))
    mod_cols, silu_c = _mod_part(c_all, w_ada[0], b_cols)
    (mod_all,) = _all_gather([mod_cols], "gather_mod")

    (w_in_own,), (w_in_land,) = _async_wait(st_in, mod_all, "gather_in_wait")
    (st_in,), tok = _async_start([[w_in_land]], "forward", no_dep, "gather_in_forward")
    _, (w_in_land,) = _async_wait(st_in, tok, "gather_in_forward_wait")
    w = _prepare_weights({"w_in": w_in_land}, {"w_in": w_in_own}, slot)
    later = dict(lora=("w_uq", "w_ukv"), proj=("w_proj_fox", "w_proj_mla", "w_out"), ffn=("w_ffn_in", "w_ffn_out"))
    states, tok = _async_start([[own[n] for n in names] for names in later.values()], ["gather", "spread", "spread"],
                               w["w_b"], "gather_rest_start")
    gather_state = dict(zip(later, states))
    own_thru = {}

    def wts(group, after):
        if group == "relay":
            second = []
            for name in ("proj", "ffn"):
                own_thru[name], lands = _async_wait(gather_state[name], after, "gather_" + name + "_wait")
                second.append(lands)
                after = lands[0]
            (gather_state["proj"], gather_state["ffn"]), t = _async_start(second, "forward", no_dep, "gather_rest_forward")
            return {"tok": t}
        srcs, lands = _async_wait(gather_state[group], after, "gather_" + group + "_landed")
        srcs = own_thru.get(group, srcs)
        return _prepare_weights(dict(zip(later[group], lands)), dict(zip(later[group], srcs)), slot)

    sent, last = [], {}

    def send(grads, final=False):
        shards = _shard_grads(grads)
        names = list(shards)
        (state,), t = _async_start([[shards[n] for n in names]], "pair" if final else "exchange", no_dep,
                                   "exchange_" + names[0] + "_start")
        if final:
            last.update(names=names, state=state)
        else:
            sent.append((names, state))
        return t

    def relay(after):
        srcs, lands = _async_wait(last["state"], after, "exchange_pair_wait")
        sums = []
        for src, land in zip(srcs, lands):
            by_chip = src.reshape((N_DEV // 2, 2) + src.shape[1:])
            sums.append(_add_blocks(lax.dynamic_index_in_dim(by_chip, lax.axis_index("c"), 1, keepdims=False), land))
        (last["state"],), t = _async_start([sums], "chips", no_dep, "exchange_chips_start")
        return t

    mod = lax.dynamic_index_in_dim(mod_all, me, axis=1, keepdims=False).reshape(1, 6 * D) + tok[0, 0]

    vec = dict(g_pre_mix=g_pre_mix, g_post_mix=g_post_mix, g_pre_ffn=g_pre_ffn, g_post_ffn=g_post_ffn,
               g_q_lora=g_q_lora, g_kv_lora=g_kv_lora, b_forget=b_forget)
    pos = positions.astype(F32).reshape(S, 1)
    grad_x, small = _fwd_bwd(x[0], pos, mod, loss_target[0], w, vec, wts, send, relay)

    bundle = jnp.concatenate([small[n] for n in SMALL_ORDER], axis=1)
    (small_state,), tok = _async_start([[bundle]], "gather", jnp.zeros((8, LANES), F32), "gather_small_start")

    out = {}
    after = tok
    for names, state in sent:
        srcs, lands = _async_wait(state, after, "exchange_" + names[0] + "_wait")
        for n, src, land in zip(names, srcs, lands):
            out[n] = _adamw(prm[n][0], mom[n][0], var[n][0], land, "adamw_" + n, src, slot)
            after = out[n][0]
    srcs, lands = _async_wait(last["state"], after, "exchange_chips_wait")
    for n, src, land in zip(last["names"], srcs, lands):
        out[n] = _adamw(prm[n][0], mom[n][0], var[n][0], land, "adamw_" + n, src, slot // 2)
        after = out[n][0]

    (own_bundle,), (bundle_all,) = _async_wait(small_state, after, "gather_small_wait")
    bundle_all = _with_own(bundle_all, own_bundle, me)
    dmod_all = bundle_all[:, 0, :6 * D]
    dm_cols = lax.dynamic_slice(dmod_all, (0, me * ada_cols), (N_DEV, ada_cols))
    g_ada = _w_ada_grad(jnp.transpose(silu_c), dm_cols)
    out["w_ada"] = _adamw(w_ada[0], m_w_ada[0], v_w_ada[0], g_ada[None], "adamw_w_ada")

    offsets, off = {}, 0
    for n in SMALL_ORDER:
        offsets[n] = off
        off += small[n].shape[1]
    names = [SMALL_PARAM.get(n, n) for n in SMALL_ORDER if n != "err"]
    results, err = _adamw_rows(bundle_all, [offsets[n] for n in SMALL_ORDER if n != "err"],
                               [prm[n] for n in names], [mom[n] for n in names], [var[n] for n in names],
                               offsets["err"], D)
    out.update(zip(names, results))
    loss = 0.5 * jnp.sum(err) / D

    res = [loss, grad_x[None]]
    for kind in range(4):
        for n in WEIGHTS:
            t = out[n][kind]
            res.append(t[None] if prm[n].ndim == 3 else t)
    return tuple(res)
```

```python
import functools
import math

import jax
import jax.numpy as jnp
from jax import lax
from jax.experimental import pallas as pl
from jax.experimental.pallas import tpu as pltpu

F32 = jnp.float32
BF16 = jnp.bfloat16

N_DEV = 8
S = 2048
D = 1024
D_FF = 2816
HEADS = 8
HEAD_DIM = 64
Q_LORA = 768
KV_LORA = 256
ROPE_DIM = 32
ROPE_THETA = 10000.0
NORM_EPS = 1e-6
LANES = 128
VMEM_LIMIT = 56 * 1024 * 1024

ADAM_LR = 0.001
ADAM_B1 = 0.9
ADAM_B2 = 0.999
ADAM_EPS = 1e-08
ADAM_WD = 0.01
ADAM_STEP = 10

ATT_T = 256
LOG2E = 1.4426950408889634
N_ATT = S // ATT_T

NN = (((1,), (0,)), ((), ()))
NT = (((1,), (1,)), ((), ()))
TN = (((0,), (0,)), ((), ()))
MESH = pl.DeviceIdType.MESH


def _params(sem=None):
    return pltpu.CompilerParams(dimension_semantics=sem, vmem_limit_bytes=VMEM_LIMIT)


def _pick(n, cap):
    best = None
    for t in range(LANES, cap + 1, LANES):
        if n % t == 0:
            best = t
    return best if best is not None else n


def _mm(a, b, mode, out_dtype, name, acc=None, dep=None):
    if mode == "nn":
        (m, k), (k2, n), dn = a.shape, b.shape, NN
    elif mode == "nt":
        (m, k), (n, k2), dn = a.shape, b.shape, NT
    else:
        (k, m), (k2, n), dn = a.shape, b.shape, TN
    assert k == k2, (a.shape, b.shape, mode)
    tn = _pick(n, 640)
    tm = _pick(m, 1536)
    osz = jnp.dtype(out_dtype).itemsize

    def need(tm_):
        blk = tm_ * k * 2 + tn * k * 2 + tm_ * tn * osz + (tm_ * tn * 4 if acc is not None else 0)
        return 2 * blk + tm_ * tn * 4
    while need(tm) > 36 * 1024 * 1024 and tm % 256 == 0:
        tm //= 2

    def body(*refs):
        a_ref, b_ref, o_ref = refs[0], refs[1], refs[-1]
        r = lax.dot_general(a_ref[...], b_ref[...], dn, preferred_element_type=F32)
        if acc is not None:
            r = r + refs[2][...]
        o_ref[...] = r.astype(o_ref.dtype)

    if mode == "tn":
        a_spec = pl.BlockSpec((k, tm), lambda i, j: (0, i))
    else:
        a_spec = pl.BlockSpec((tm, k), lambda i, j: (i, 0))
    if mode == "nt":
        b_spec = pl.BlockSpec((tn, k), lambda i, j: (j, 0))
    else:
        b_spec = pl.BlockSpec((k, tn), lambda i, j: (0, j))
    o_spec = pl.BlockSpec((tm, tn), lambda i, j: (i, j))
    in_specs = [a_spec, b_spec] + ([o_spec] if acc is not None else [])
    in_specs += [pl.BlockSpec(memory_space=pl.ANY)] if dep is not None else []
    args = (a, b) + ((acc,) if acc is not None else ()) + ((dep,) if dep is not None else ())
    return pl.pallas_call(
        body, name=name, grid=(m // tm, n // tn),
        in_specs=in_specs, out_specs=o_spec,
        out_shape=jax.ShapeDtypeStruct((m, n), out_dtype),
        compiler_params=_params(("parallel", "parallel")),
    )(*args)


def _mm_epi(a, b, mode, tnb, epi, name, tm, rows=(), vecs=(), outs=(), sums=(), pro=None):
    m = a.shape[0]
    k, nb = (b.shape if mode == "nn" else b.shape[::-1])
    dn = NN if mode == "nn" else NT
    pro_fn, pro_vecs, a_off = pro if pro is not None else (None, (), 0)
    n_in = 2 + len(rows) + len(vecs)
    n_all = n_in + len(pro_vecs)

    def body(*refs):
        if pro is not None:
            a_out, a_scr = refs[-2:]
            refs = refs[:-2]

            @pl.when(pl.program_id(1) == 0)
            def _():
                a_scr[...] = pro_fn(refs[0][...], *[x[...] for x in refs[n_in:n_all]]).astype(BF16)
                a_out[...] = a_scr[...]
            lhs = a_scr[...]
        else:
            lhs = refs[0][...]
        r = lax.dot_general(lhs, refs[1][...], dn, preferred_element_type=F32)
        o_vals, s_vals = epi(r, *[x[...] for x in refs[2:n_in]])
        o_refs = refs[n_all:n_all + len(outs)]
        s_refs = refs[n_all + len(outs):]
        assert len(o_vals) == len(o_refs) and len(s_vals) == len(s_refs)
        for o_ref, val in zip(o_refs, o_vals):
            o_ref[...] = val.astype(o_ref.dtype)
        if sums:
            @pl.when((pl.program_id(0) == 0) & (pl.program_id(1) == 0))
            def _():
                for s_ref in s_refs:
                    s_ref[...] = jnp.zeros(s_ref.shape, F32)
            for s_ref, val in zip(s_refs, s_vals):
                s_ref[...] += val

    b_spec = pl.BlockSpec((k, tnb), lambda i, j: (0, j)) if mode == "nn" else pl.BlockSpec((tnb, k), lambda i, j: (j, 0))
    in_specs = [pl.BlockSpec((tm, k), lambda i, j: (i, a_off)), b_spec]
    rows = [tuple(r) + (0,) * (3 - len(r)) for r in rows]
    in_specs += [pl.BlockSpec((tm, w), functools.partial(lambda i, j, off: (i, j + off), off=off)) for _, w, off in rows]
    in_specs += [pl.BlockSpec(v.shape, lambda i, j: (0, 0)) for v in list(vecs) + list(pro_vecs)]
    out_specs = [pl.BlockSpec((tm, w), lambda i, j: (i, j)) for _, w, _ in outs]
    out_specs += [pl.BlockSpec((1, w), lambda i, j: (0, 0)) for w in sums]
    out_shape = [jax.ShapeDtypeStruct((m, full), dt) for full, _, dt in outs]
    out_shape += [jax.ShapeDtypeStruct((1, w), F32) for w in sums]
    if pro is not None:
        out_specs.append(pl.BlockSpec((tm, k), lambda i, j: (i, 0)))
        out_shape.append(jax.ShapeDtypeStruct((m, k), BF16))
    return pl.pallas_call(
        body, name=name, grid=(m // tm, nb // tnb),
        in_specs=in_specs, out_specs=out_specs, out_shape=out_shape,
        scratch_shapes=[pltpu.VMEM((tm, k), BF16)] if pro is not None else [],
        compiler_params=_params(("arbitrary", "arbitrary") if sums else ("parallel", "arbitrary" if pro is not None else "parallel")),
    )(a, b, *[r[0] for r in rows], *vecs, *pro_vecs)


def _rowwise(fn, row_ins, vec_ins, row_outs, sum_outs, name, tm=256):
    n_in = len(row_ins) + len(vec_ins)
    n_o = len(row_outs)
    rows = row_ins[0][0].shape[0]

    def body(*refs):
        vals = [r[...] for r in refs[:n_in]]
        outs = refs[n_in:]
        ro, so = fn(*vals)
        assert len(ro) == n_o and len(so) == len(sum_outs)
        for r, v in zip(outs[:n_o], ro):
            r[...] = v.astype(r.dtype)
        if sum_outs:
            @pl.when(pl.program_id(0) == 0)
            def _():
                for r in outs[n_o:]:
                    r[...] = jnp.zeros(r.shape, F32)
            for r, v in zip(outs[n_o:], so):
                r[...] += v

    in_specs = [pl.BlockSpec((tm, w), functools.partial(lambda i, b: (i, b), b=b)) for _, w, b in row_ins]
    in_specs += [pl.BlockSpec(v.shape, lambda i: (0, 0)) for v in vec_ins]
    out_specs = [pl.BlockSpec((tm, w), lambda i: (i, 0)) for w, _ in row_outs]
    out_specs += [pl.BlockSpec((1, w), lambda i: (0, 0)) for w in sum_outs]
    out_shape = [jax.ShapeDtypeStruct((rows, w), dt) for w, dt in row_outs]
    out_shape += [jax.ShapeDtypeStruct((1, w), F32) for w in sum_outs]
    return pl.pallas_call(
        body, name=name, grid=(rows // tm,),
        in_specs=in_specs, out_specs=out_specs, out_shape=out_shape,
        compiler_params=_params(("arbitrary",)),
    )(*[a for a, _, _ in row_ins], *vec_ins)


def _sigmoid(x):
    return 1.0 / (1.0 + jnp.exp(-x))


def _rstd(x):
    return lax.rsqrt(jnp.mean(x * x, axis=-1, keepdims=True) + NORM_EPS)


def _norm_bwd(dyn, xn, r):
    return r * (dyn - xn * jnp.mean(dyn * xn, axis=-1, keepdims=True))


def _colsum(x):
    return jnp.sum(x, axis=0, keepdims=True)


def _rope_tables(pos, invf):
    def fn(p, f):
        lane = lax.broadcasted_iota(jnp.int32, (1, LANES), 1)
        ang = p * f
        cs, sn = jnp.cos(ang), jnp.sin(ang)
        rot = (lane >= 64) & (lane < 96)
        ct = jnp.where(lane < 64, 1.0, jnp.where(rot, cs, 0.0))
        sa = jnp.where((lane >= 64) & (lane < 80), -sn, 0.0)
        sb = jnp.where((lane >= 80) & (lane < 96), sn, 0.0)
        return (ct, sa, sb), ()
    return _rowwise(fn, [(pos, 1, 0)], [invf], [(LANES, F32)] * 3, [], "rope_tables")


def _rope(x, ct, sa, sb):
    return x * ct + pltpu.roll(x, LANES - 16, 1) * sa + pltpu.roll(x, 16, 1) * sb


def _rope_t(x, ct, sa, sb):
    return x * ct - pltpu.roll(x, LANES - 16, 1) * sa - pltpu.roll(x, 16, 1) * sb


def _head_mask(width, hh):
    lane = lax.broadcasted_iota(jnp.int32, (1, width), 1)
    half = width // 2
    return (lane >= hh * half) & (lane < (hh + 1) * half)


ATT_PP = 2
ATT_CHAINS = [(a, hh) for a in range(ATT_PP) for hh in range(2)]
ATT_G = HEADS // (2 * ATT_PP)


def _pair(ref_or_val, a, width, rows=slice(None)):
    return ref_or_val[rows, a * width:(a + 1) * width]


def _attn_fwd(q, qo, k, ko, v, vo, dkp, scale, bias, name):
    T = ATT_T
    assert qo % ATT_PP == 0 and ko % ATT_PP == 0 and vo % ATT_PP == 0
    qo, ko, vo = qo // ATT_PP, ko // ATT_PP, vo // ATT_PP

    def body(*refs):
        if bias is not None:
            q_ref, k_ref, v_ref, b_ref, o_ref, lse_ref, s_scr = refs
        else:
            q_ref, k_ref, v_ref, o_ref, lse_ref, s_scr = refs
        i = pl.program_id(1)
        row = lax.broadcasted_iota(jnp.int32, (T, T), 0)
        col = lax.broadcasted_iota(jnp.int32, (T, T), 1)
        qms = []
        for a, hh in ATT_CHAINS:
            qb = _pair(q_ref, a, dkp)
            qms.append(jnp.where(_head_mask(dkp, hh), qb, jnp.zeros_like(qb)))

        def fold(t):
            return [t[:, c * LANES:(c + 1) * LANES] for c in range(T // LANES)]

        def run(nt):
            mls = [jnp.full((T, LANES), -jnp.inf, F32) for _ in ATT_CHAINS]
            for j in range(nt):
                ks = slice(j * T, (j + 1) * T)
                for ci, (a, hh) in enumerate(ATT_CHAINS):
                    s = lax.dot_general(qms[ci], _pair(k_ref, a, dkp, ks), NT, preferred_element_type=F32) * (scale * LOG2E)
                    if bias is not None:
                        s = s + b_ref[2 * a + hh, j] * LOG2E
                    if j == nt - 1:
                        s = jnp.where(row >= col, s, -jnp.inf)
                    s_scr[ci, j] = s
                    for part in fold(s):
                        mls[ci] = jnp.maximum(mls[ci], part)
            ms = [jnp.max(ml, axis=1, keepdims=True) for ml in mls]
            mbs = [jnp.broadcast_to(m, (T, LANES)) for m in ms]
            for a in range(ATT_PP):
                ls = [jnp.zeros((T, LANES), F32) for _ in range(2)]
                ps, vms = [], []
                for j in range(nt):
                    vb = _pair(v_ref, a, LANES, slice(j * T, (j + 1) * T))
                    for hh in range(2):
                        parts = [jnp.exp2(part - mbs[2 * a + hh]) for part in fold(s_scr[2 * a + hh, j])]
                        for part in parts:
                            ls[hh] = ls[hh] + part
                        ps.append(jnp.concatenate(parts, axis=1).astype(BF16))
                        vms.append(jnp.where(_head_mask(LANES, hh), vb, jnp.zeros_like(vb)))
                acc = lax.dot_general(jnp.concatenate(ps, axis=1), jnp.concatenate(vms, axis=0), NN,
                                      preferred_element_type=F32)
                l0, l1 = [jnp.sum(l, axis=1, keepdims=True) for l in ls]
                lse_ref[2 * a] = ms[2 * a] + jnp.log2(l0)
                lse_ref[2 * a + 1] = ms[2 * a + 1] + jnp.log2(l1)
                inv = jnp.where(_head_mask(LANES, 0), 1.0 / l0, 1.0 / l1)
                o_ref[:, a * LANES:(a + 1) * LANES] = (acc * inv).astype(o_ref.dtype)

        for nt in range(1, N_ATT + 1):
            pl.when(i == nt - 1)(functools.partial(run, nt))

    in_specs = [
        pl.BlockSpec((T, ATT_PP * dkp), lambda g, i: (i, qo + g)),
        pl.BlockSpec((S, ATT_PP * dkp), lambda g, i: (0, ko + g)),
        pl.BlockSpec((S, ATT_PP * LANES), lambda g, i: (0, vo + g)),
    ]
    args = [q, k, v]
    if bias is not None:
        in_specs.append(pl.BlockSpec((2 * ATT_PP, N_ATT, 1, T), lambda g, i: (g, 0, 0, 0)))
        args.append(bias)
    return pl.pallas_call(
        body, name=name, grid=(ATT_G, N_ATT),
        in_specs=in_specs,
        out_specs=[pl.BlockSpec((T, ATT_PP * LANES), lambda g, i: (i, g)),
                   pl.BlockSpec((2 * ATT_PP, T, 1), lambda g, i: (g, i, 0))],
        out_shape=[jax.ShapeDtypeStruct((S, HEADS * HEAD_DIM), BF16),
                   jax.ShapeDtypeStruct((HEADS, S, 1), F32)],
        scratch_shapes=[pltpu.VMEM((len(ATT_CHAINS), N_ATT, T, T), F32)],
        compiler_params=_params(("parallel", "arbitrary")),
    )(*args)


def _attn_grad(q, qo, k, ko, v, vo, do, lse, dkp, scale, bias, qk_dtype, name):
    T = ATT_T
    has_b = bias is not None
    qo, ko, vo = qo // ATT_PP, ko // ATT_PP, vo // ATT_PP
    n_ch = len(ATT_CHAINS)

    def body(*refs):
        q_ref, k_ref, v_ref, do_ref, lse_ref = refs[:5]
        refs = refs[5:]
        if has_b:
            b_ref, refs = refs[0], refs[1:]
        dq_ref, dk_ref, dv_ref = refs[:3]
        refs = refs[3:]
        if has_b:
            db_ref, refs = refs[0], refs[1:]
        p_scr, dp_scr, dk_acc, dv_acc = refs[:4]
        db_acc = refs[4] if has_b else None
        i = pl.program_id(1)

        @pl.when(i == 0)
        def _():
            dk_acc[...] = jnp.zeros(dk_acc.shape, F32)
            dv_acc[...] = jnp.zeros(dv_acc.shape, F32)
            if has_b:
                db_acc[...] = jnp.zeros(db_acc.shape, F32)

        row = lax.broadcasted_iota(jnp.int32, (T, T), 0)
        col = lax.broadcasted_iota(jnp.int32, (T, T), 1)

        def fold(t):
            return [t[:, c * LANES:(c + 1) * LANES] for c in range(T // LANES)]

        qms, doms, lses = [], [], []
        for a, hh in ATT_CHAINS:
            qb, dob = _pair(q_ref, a, dkp), _pair(do_ref, a, LANES)
            qms.append(jnp.where(_head_mask(dkp, hh), qb, jnp.zeros_like(qb)))
            doms.append(jnp.where(_head_mask(LANES, hh), dob, jnp.zeros_like(dob)))
            lses.append(lse_ref[2 * a + hh])

        def run(nt):
            dls = [jnp.zeros((T, LANES), F32) for _ in ATT_CHAINS]
            for j in range(nt):
                ks = slice(j * T, (j + 1) * T)
                for ci, (a, hh) in enumerate(ATT_CHAINS):
                    s = lax.dot_general(qms[ci], _pair(k_ref, a, dkp, ks), NT, preferred_element_type=F32) * (scale * LOG2E)
                    if has_b:
                        s = s + b_ref[ci, j] * LOG2E
                    s = s - lses[ci]
                    if j == nt - 1:
                        s = jnp.where(row >= col, s, -jnp.inf)
                    p = jnp.exp2(s)
                    dp = lax.dot_general(doms[ci], _pair(v_ref, a, LANES, ks), NT, preferred_element_type=F32)
                    p_scr[ci, j] = p
                    dp_scr[ci, j] = dp
                    for part in fold(p * dp):
                        dls[ci] = dls[ci] + part
            deltas = [jnp.broadcast_to(jnp.sum(dl, axis=1, keepdims=True), (T, LANES)) for dl in dls]
            for a in range(ATT_PP):
                ds_all, km_all = [], []
                qm2t = jnp.transpose(jnp.concatenate([qms[2 * a], qms[2 * a + 1]], axis=0))
                dom2t = jnp.transpose(jnp.concatenate([doms[2 * a], doms[2 * a + 1]], axis=0))
                for j in range(nt):
                    ks = slice(j * T, (j + 1) * T)
                    kb = _pair(k_ref, a, dkp, ks)
                    p2, ds2 = [], []
                    for hh in range(2):
                        ci = 2 * a + hh
                        p = p_scr[ci, j]
                        ds = jnp.concatenate([pp * (dd - deltas[ci]) for pp, dd in zip(fold(p), fold(dp_scr[ci, j]))], axis=1)
                        if has_b:
                            db_acc[ci, j] += jnp.sum(ds, axis=0, keepdims=True)
                        p2.append(p.astype(BF16))
                        ds2.append((ds * scale).astype(BF16))
                        km_all.append(jnp.where(_head_mask(dkp, hh), kb, jnp.zeros_like(kb)))
                    dv_acc[a * LANES:(a + 1) * LANES, ks] += lax.dot_general(
                        dom2t, jnp.concatenate(p2, axis=0), NN, preferred_element_type=F32)
                    dk_acc[a * dkp:(a + 1) * dkp, ks] += lax.dot_general(
                        qm2t, jnp.concatenate(ds2, axis=0), NN, preferred_element_type=F32)
                    ds_all += ds2
                dq = lax.dot_general(jnp.concatenate(ds_all, axis=1), jnp.concatenate(km_all, axis=0), NN,
                                     preferred_element_type=F32)
                dq_ref[:, a * dkp:(a + 1) * dkp] = dq.astype(dq_ref.dtype)

        for nt in range(1, N_ATT + 1):
            pl.when(i == nt - 1)(functools.partial(run, nt))

        @pl.when(i == N_ATT - 1)
        def _():
            dk_ref[...] = jnp.transpose(dk_acc[...]).astype(dk_ref.dtype)
            dv_ref[...] = jnp.transpose(dv_acc[...]).astype(dv_ref.dtype)
            if has_b:
                db_ref[...] = db_acc[...]

    in_specs = [
        pl.BlockSpec((T, ATT_PP * dkp), lambda g, i: (i, qo + g)),
        pl.BlockSpec((S, ATT_PP * dkp), lambda g, i: (0, ko + g)),
        pl.BlockSpec((S, ATT_PP * LANES), lambda g, i: (0, vo + g)),
        pl.BlockSpec((T, ATT_PP * LANES), lambda g, i: (i, g)),
        pl.BlockSpec((2 * ATT_PP, T, 1), lambda g, i: (g, i, 0)),
    ]
    args = [q, k, v, do, lse]
    out_specs = [
        pl.BlockSpec((T, ATT_PP * dkp), lambda g, i: (i, g)),
        pl.BlockSpec((S, ATT_PP * dkp), lambda g, i: (0, g)),
        pl.BlockSpec((S, ATT_PP * LANES), lambda g, i: (0, g)),
    ]
    width = (HEADS // 2) * dkp
    out_shape = [
        jax.ShapeDtypeStruct((S, width), qk_dtype),
        jax.ShapeDtypeStruct((S, width), qk_dtype),
        jax.ShapeDtypeStruct((S, HEADS * HEAD_DIM), BF16),
    ]
    scratch = [pltpu.VMEM((n_ch, N_ATT, T, T), F32), pltpu.VMEM((n_ch, N_ATT, T, T), F32),
               pltpu.VMEM((ATT_PP * dkp, S), F32), pltpu.VMEM((ATT_PP * LANES, S), F32)]
    if has_b:
        bspec = pl.BlockSpec((2 * ATT_PP, N_ATT, 1, T), lambda g, i: (g, 0, 0, 0))
        in_specs.append(bspec)
        args.append(bias)
        out_specs.append(bspec)
        out_shape.append(jax.ShapeDtypeStruct((HEADS, N_ATT, 1, T), F32))
        scratch.append(pltpu.VMEM((2 * ATT_PP, N_ATT, 1, T), F32))
    return pl.pallas_call(
        body, name=name, grid=(ATT_G, N_ATT),
        in_specs=in_specs, out_specs=out_specs, out_shape=out_shape, scratch_shapes=scratch,
        compiler_params=_params(("parallel", "arbitrary")),
    )(*args)


def _tri(upper):
    a = lax.broadcasted_iota(jnp.int32, (LANES, LANES), 0)
    b = lax.broadcasted_iota(jnp.int32, (LANES, LANES), 1)
    return jnp.where(a <= b if upper else a >= b, 1.0, 0.0).astype(F32)


def _fox_gates(zt, bf):
    def body(z_ref, b_ref, o_ref):
        tri = _tri(True)
        carry = jnp.zeros((HEADS, 1), F32)
        for t in range(S // LANES):
            sl = slice(t * LANES, (t + 1) * LANES)
            z = z_ref[:, sl] + b_ref[...]
            logf = jnp.minimum(z, 0.0) - jnp.log(1.0 + jnp.exp(-jnp.abs(z)))
            c = lax.dot_general(logf, tri, NN, preferred_element_type=F32,
                                precision=lax.Precision.HIGHEST) + carry
            o_ref[:, sl] = -c
            carry = c[:, LANES - 1:LANES]

    return pl.pallas_call(
        body, name="fox_gates", out_shape=jax.ShapeDtypeStruct((HEADS, S), F32),
        compiler_params=_params(),
    )(zt, bf)


def _fox_gates_bwd(dbias, zt, bf):
    def body(d_ref, z_ref, b_ref, dz_ref, dbf_ref):
        tri = _tri(False)
        carry = jnp.zeros((HEADS, 1), F32)
        tot = jnp.zeros((HEADS, 1), F32)
        for t in reversed(range(S // LANES)):
            sl = slice(t * LANES, (t + 1) * LANES)
            df = -d_ref[:, sl]
            c = lax.dot_general(df, tri, NN, preferred_element_type=F32,
                                precision=lax.Precision.HIGHEST) + carry
            carry = c[:, 0:1]
            z = z_ref[:, sl] + b_ref[...]
            dz = c * _sigmoid(-z)
            dz_ref[:, sl] = dz
            tot = tot + jnp.sum(dz, axis=1, keepdims=True)
        dbf_ref[...] = tot

    return pl.pallas_call(
        body, name="fox_gates_bwd",
        out_shape=[jax.ShapeDtypeStruct((HEADS, S), F32), jax.ShapeDtypeStruct((HEADS, 1), F32)],
        compiler_params=_params(),
    )(dbias, zt, bf)


def _mod_part(c_all, w_ada, b_cols):
    def body(c_ref, w_ref, b_ref, o_ref, s_ref):
        c = c_ref[...]
        sc = c * _sigmoid(c)
        s_ref[...] = sc
        o_ref[...] = lax.dot_general(sc, w_ref[...], NN, preferred_element_type=F32,
                                     precision=lax.Precision.HIGHEST) + b_ref[...]

    return pl.pallas_call(
        body, name="mod_part",
        out_shape=[jax.ShapeDtypeStruct((N_DEV, w_ada.shape[1]), F32), jax.ShapeDtypeStruct(c_all.shape, F32)],
        compiler_params=_params(),
    )(c_all, w_ada, b_cols)


def _w_ada_grad(sc_t, dm):
    def body(s_ref, d_ref, o_ref):
        acc = jnp.zeros(o_ref.shape, F32)
        for b in range(N_DEV):
            acc = acc + s_ref[:, b:b + 1] * d_ref[b:b + 1, :]
        o_ref[...] = acc

    return pl.pallas_call(
        body, name="w_ada_grad", out_shape=jax.ShapeDtypeStruct((sc_t.shape[0], dm.shape[1]), F32),
        compiler_params=_params(),
    )(sc_t, dm)


def _adamw(w, m, v, parts, name, own=None, slot=None):
    rows, cols = w.shape
    n = parts.shape[0]
    tr = rows if rows <= 512 else 256

    def body(*refs):
        if own is not None:
            s_ref, refs = refs[0], refs[1:]
            w_ref, m_ref, v_ref, p_ref, o_ref, g_out, d_out, m_out, v_out = refs
            terms = [jnp.where(s_ref[0] == kk, o_ref[0], p_ref[kk]) for kk in range(n)]
        else:
            w_ref, m_ref, v_ref, p_ref, g_out, d_out, m_out, v_out = refs
            terms = [p_ref[kk] for kk in range(n)]
        g = terms[0].astype(F32)
        for term in terms[1:]:
            g = g + term.astype(F32)
        g_out[...] = g
        d_out[...], m_out[...], v_out[...] = _adamw_math(w_ref[...], g, m_ref[...], v_ref[...])

    spec = pl.BlockSpec((tr, cols), lambda i, *_: (i, 0))
    in_specs = [spec, spec, spec, pl.BlockSpec((n, tr, cols), lambda i, *_: (0, i, 0))]
    out_shape = [jax.ShapeDtypeStruct((rows, cols), F32)] * 4
    if own is None:
        return pl.pallas_call(
            body, name=name, grid=(rows // tr,), in_specs=in_specs, out_specs=[spec] * 4, out_shape=out_shape,
            compiler_params=_params(("parallel",)),
        )(w, m, v, parts)
    in_specs.append(pl.BlockSpec((1, tr, cols), lambda i, s: (s[0], i, 0)))
    return pl.pallas_call(
        body, name=name, out_shape=out_shape, compiler_params=_params(("parallel",)),
        grid_spec=pltpu.PrefetchScalarGridSpec(num_scalar_prefetch=1, grid=(rows // tr,), in_specs=in_specs,
                                               out_specs=[spec] * 4),
    )(slot, w, m, v, parts, own)


def _adamw_math(w, g, m, v):
    mm = ADAM_B1 * m + (1.0 - ADAM_B1) * g
    vv = ADAM_B2 * v + (1.0 - ADAM_B2) * (g * g)
    m_hat = mm / (1.0 - ADAM_B1 ** ADAM_STEP)
    v_hat = vv / (1.0 - ADAM_B2 ** ADAM_STEP)
    return -ADAM_LR * (m_hat / (jnp.sqrt(v_hat) + ADAM_EPS) + ADAM_WD * w), mm, vv


def _adamw_rows(bundles, offsets, ws, ms, vs, err_off, err_width):
    k = len(ws)

    def body(*refs):
        b_ref = refs[0]
        w_refs, m_refs, v_refs = refs[1:1 + k], refs[1 + k:1 + 2 * k], refs[1 + 2 * k:1 + 3 * k]
        outs = refs[1 + 3 * k:]
        g_all = b_ref[0]
        for kk in range(1, N_DEV):
            g_all = g_all + b_ref[kk]
        for i in range(k):
            width = w_refs[i].shape[1]
            g = g_all[:, offsets[i]:offsets[i] + width]
            outs[4 * i][...] = g
            outs[4 * i + 1][...], outs[4 * i + 2][...], outs[4 * i + 3][...] = _adamw_math(
                w_refs[i][...], g, m_refs[i][...], v_refs[i][...])
        outs[4 * k][...] = g_all[:, err_off:err_off + err_width]

    out_shape = []
    for w_ in ws:
        out_shape += [jax.ShapeDtypeStruct(w_.shape, F32)] * 4
    out_shape.append(jax.ShapeDtypeStruct((1, err_width), F32))
    res = pl.pallas_call(body, name="adamw_rows", out_shape=out_shape, compiler_params=_params())(bundles, *ws, *ms, *vs)
    return [tuple(res[4 * i:4 * i + 4]) for i in range(k)], res[-1]


def _coords():
    return lax.axis_index("x"), lax.axis_index("y"), lax.axis_index("c")


def _flat(px, py, pc):
    return 4 * px + 2 * py + pc


def _all_gather(arrs, name):
    n = len(arrs)

    def body(*refs):
        ins, outs = refs[:n], refs[n:2 * n]
        send, recv, lsem = refs[2 * n:]
        x, y, c = _coords()
        me, sibling = (x, y, c), (x, y, 1 - c)
        chips = [(1 - x, y), (x, 1 - y), (1 - x, 1 - y)]

        def copy(a, kk, block, to, src=None):
            slot = outs[a].at[_flat(*block)]
            return pltpu.make_async_remote_copy(
                src_ref=slot if src is None else src, dst_ref=slot,
                send_sem=send.at[a, kk], recv_sem=recv.at[a, kk],
                device_id=to, device_id_type=MESH)

        mine = [pltpu.make_async_copy(ins[a], outs[a].at[_flat(*me)], lsem.at[a]) for a in range(n)]
        for cp in mine:
            cp.start()
        first = []
        for a in range(n):
            first.append(copy(a, 0, me, sibling, src=ins[a]))
            first += [copy(a, 1 + j, me, (*chip, c), src=ins[a]) for j, chip in enumerate(chips)]
        for cp in first:
            cp.start()
        passed = []
        for j, chip in enumerate(chips):
            for a in range(n):
                copy(a, 1 + j, (*chip, c), me).wait_recv()
                cp = copy(a, 4 + j, (*chip, c), sibling)
                cp.start()
                passed.append(cp)
        for a in range(n):
            copy(a, 0, sibling, me).wait_recv()
        for j, chip in enumerate(chips):
            for a in range(n):
                copy(a, 4 + j, (*chip, 1 - c), me).wait_recv()
        for cp in first + passed:
            cp.wait_send()
        for cp in mine:
            cp.wait()

    any_spec = pl.BlockSpec(memory_space=pl.ANY)
    return pl.pallas_call(
        body, name=name,
        in_specs=[any_spec] * n, out_specs=[any_spec] * n,
        out_shape=[jax.ShapeDtypeStruct((N_DEV,) + a.shape, a.dtype) for a in arrs],
        scratch_shapes=[pltpu.SemaphoreType.DMA((n, 7)), pltpu.SemaphoreType.DMA((n, 7)),
                        pltpu.SemaphoreType.DMA((n,))],
    )(*arrs)


def _peer_list():
    x, y, c = _coords()
    return [((1 - x if r & 4 else x), (1 - y if r & 2 else y), (1 - c if r & 1 else c)) for r in range(1, N_DEV)]


def _copy_plan(mode, src, land):
    x, y, c = _coords()
    me = _flat(x, y, c)
    if mode == "gather":
        return [(src, land.at[me], peer) for peer in _peer_list()]
    if mode == "exchange":
        return [(src.at[_flat(*peer)], land.at[me], peer) for peer in _peer_list()]
    if mode == "pair":
        return [(src.at[_flat(q // 2, q % 2, 1 - c)], land.at[q], (x, y, 1 - c)) for q in range(N_DEV // 2)]
    chips = [((1 - x if r & 2 else x), (1 - y if r & 1 else y)) for r in range(1, N_DEV // 2)]
    if mode == "chips":
        return [(src.at[2 * qx + qy], land.at[2 * x + y], (qx, qy, c)) for qx, qy in chips]
    if mode == "spread":
        return [(src, land.at[me], (x, y, 1 - c))] + [(src, land.at[me], (qx, qy, c)) for qx, qy in chips]
    assert mode == "forward"
    return [(land.at[_flat(qx, qy, c)], land.at[_flat(qx, qy, c)], (x, y, 1 - c)) for qx, qy in chips]


N_COPIES = dict(gather=N_DEV - 1, exchange=N_DEV - 1, pair=N_DEV // 2, chips=N_DEV // 2 - 1, spread=N_DEV // 2,
                forward=N_DEV // 2 - 1)


def _land_shape(mode, shape):
    return {"gather": (N_DEV,) + shape, "spread": (N_DEV,) + shape, "exchange": shape,
            "pair": (N_DEV // 2,) + shape[1:], "chips": shape}[mode]


HBM_SPEC = pl.BlockSpec(memory_space=pltpu.HBM)
SEM_SPEC = pl.BlockSpec(memory_space=pltpu.SEMAPHORE)
ANY_SPEC = pl.BlockSpec(memory_space=pl.ANY)
SIDE_EFFECT = pltpu.SideEffectType.DATAFLOW_SIDE_EFFECTING


def _async_start(groups, modes, after, name):
    modes = [modes] * len(groups) if isinstance(modes, str) else list(modes)
    arrs = [(a, m) for g, m in zip(groups, modes) for a in g]
    n = len(arrs)
    fresh = [i for i, (_, m) in enumerate(arrs) if m != "forward"]

    def body(*refs):
        srcs, new_lands = refs[:n], refs[n:n + len(fresh)]
        outs = refs[n + len(fresh) + 1:]
        lands = list(srcs)
        for k, i in enumerate(fresh):
            lands[i] = new_lands[k]
        for ai, (_, mode) in enumerate(arrs):
            for src_ref, dst_ref, peer in _copy_plan(mode, srcs[ai], lands[ai]):
                pltpu.make_async_remote_copy(src_ref=src_ref, dst_ref=dst_ref, send_sem=outs[2 * ai],
                                             recv_sem=outs[2 * ai + 1], device_id=peer, device_id_type=MESH).start()
        outs[-1][...] = jnp.zeros(outs[-1].shape, F32)

    land_shapes = [(_land_shape(arrs[i][1], arrs[i][0].shape), arrs[i][0].dtype) for i in fresh]
    n_buf = n + len(fresh)
    out_shape = [pltpu.SemaphoreType.DMA(())] * (2 * n)
    out_shape += [pltpu.HBM(a.shape, a.dtype) for a, _ in arrs]
    out_shape += [pltpu.HBM(shape, dt) for shape, dt in land_shapes]
    out_shape.append(jax.ShapeDtypeStruct((8, LANES), F32))
    res = pl.pallas_call(
        body, name=name, out_shape=tuple(out_shape),
        in_specs=[HBM_SPEC] * n_buf + [ANY_SPEC],
        out_specs=tuple([SEM_SPEC] * (2 * n) + [HBM_SPEC] * n_buf + [pl.BlockSpec(memory_space=pltpu.VMEM)]),
        input_output_aliases={i: 2 * n + i for i in range(n_buf)},
        compiler_params=pltpu.CompilerParams(has_side_effects=SIDE_EFFECT),
    )(*[pltpu.with_memory_space_constraint(a, pltpu.HBM) for a, _ in arrs],
      *[pltpu.with_memory_space_constraint(lax.empty(shape, dt), pltpu.HBM) for shape, dt in land_shapes],
      after)
    sems, thru = res[:2 * n], res[2 * n:-1]
    land_of = {i: thru[n + k] for k, i in enumerate(fresh)}
    states, idx = [], 0
    for g, mode in zip(groups, modes):
        ids = range(idx, idx + len(g))
        idx += len(g)
        states.append(([sems[2 * i] for i in ids], [sems[2 * i + 1] for i in ids],
                       None if mode == "forward" else [thru[i] for i in ids],
                       [land_of.get(i, thru[i]) for i in ids], mode))
    return states, res[-1]


def _async_wait(state, after, name):
    sends, recvs, srcs, lands, mode = state
    g = len(lands)
    bufs = (list(srcs) if srcs is not None else []) + list(lands)
    nb = len(bufs)

    def body(*refs):
        l_refs, sems = refs[nb - g:nb], refs[nb:nb + 2 * g]
        for ai in range(g):
            moved = l_refs[ai].at[pl.ds(0, N_COPIES[mode])]
            cp = pltpu.make_async_remote_copy(src_ref=moved, dst_ref=moved, send_sem=sems[ai], recv_sem=sems[g + ai],
                                              device_id=_coords(), device_id_type=MESH)
            cp.wait_send()
            cp.wait_recv()

    res = pl.pallas_call(
        body, name=name,
        out_shape=tuple(pltpu.HBM(a.shape, a.dtype) for a in bufs),
        in_specs=[HBM_SPEC] * nb + [SEM_SPEC] * (2 * g) + [ANY_SPEC],
        out_specs=tuple([HBM_SPEC] * nb),
        input_output_aliases={i: i for i in range(nb)},
        compiler_params=pltpu.CompilerParams(has_side_effects=SIDE_EFFECT),
    )(*bufs, *sends, *recvs, after)
    return (list(res[:nb - g]) if srcs is not None else None), list(res[nb - g:])


def _add_blocks(a, b):
    def body(a_ref, b_ref, o_ref):
        o_ref[...] = (a_ref[...].astype(F32) + b_ref[...].astype(F32)).astype(o_ref.dtype)

    spec = pl.BlockSpec((1,) + a.shape[1:], lambda i: (i, 0, 0))
    return pl.pallas_call(
        body, name="add_blocks", grid=(a.shape[0],), in_specs=[spec, spec], out_specs=spec,
        out_shape=jax.ShapeDtypeStruct(a.shape, a.dtype), compiler_params=_params(("parallel",)),
    )(a, b)


def _with_own(land, own, me):
    return lax.dynamic_update_index_in_dim(land, own, me, 0)


IN_SPLITS = (512, 512, 512, 8, 768, 256, 32, 1024, 1024)


def _from_shards(g, fn, out_widths, name, own=None, slot=None):
    _, k, n = g.shape
    tr = min(k, 256)

    def body(*refs):
        if own is not None:
            s_ref, g_ref, own_ref = refs[:3]
            cols = [jnp.where(s_ref[0] == j, own_ref[...], g_ref[j]) for j in range(N_DEV)]
        else:
            g_ref = refs[0]
            cols = [g_ref[j] for j in range(N_DEV)]
        for o_ref, val in zip(refs[-len(out_widths):], fn(jnp.concatenate(cols, axis=1))):
            o_ref[...] = val

    in_specs = [pl.BlockSpec((N_DEV, tr, n), lambda i, *_: (0, i, 0))]
    out_spec = [pl.BlockSpec((tr, wd), lambda i, *_: (i, 0)) for wd in out_widths]
    out_shape = [jax.ShapeDtypeStruct((k, wd), g.dtype) for wd in out_widths]
    if own is None:
        return pl.pallas_call(body, name=name, grid=(k // tr,), in_specs=in_specs, out_specs=out_spec,
                              out_shape=out_shape, compiler_params=_params(("parallel",)))(g)
    in_specs.append(pl.BlockSpec((tr, n), lambda i, *_: (i, 0)))
    return pl.pallas_call(
        body, name=name, out_shape=out_shape, compiler_params=_params(("parallel",)),
        grid_spec=pltpu.PrefetchScalarGridSpec(num_scalar_prefetch=1, grid=(k // tr,), in_specs=in_specs, out_specs=out_spec),
    )(slot, g, own)


def _unshard_cols(g, own=None, slot=None):
    return _from_shards(g, lambda full: (full,), [N_DEV * g.shape[2]], "unshard_cols_%d" % g.shape[2], own, slot)[0]


FFN_T = 256
FFN_SHARD = 2 * D_FF // N_DEV


def _unshard_ffn_in(g, own=None, slot=None):
    def pairs(full):
        parts = []
        for j in range(D_FF // FFN_T):
            parts += [full[:, j * FFN_T:(j + 1) * FFN_T], full[:, D_FF + j * FFN_T:D_FF + (j + 1) * FFN_T]]
        return (jnp.concatenate(parts, axis=1),)

    return _from_shards(g, pairs, [2 * D_FF], "unshard_ffn_in", own, slot)[0]


def _shard_ffn_in(w):
    tr = 256

    def body(w_ref, o_ref):
        x = w_ref[...]
        nb = D_FF // FFN_T
        full = jnp.concatenate([x[:, (2 * j + half) * FFN_T:(2 * j + half + 1) * FFN_T]
                                for half in range(2) for j in range(nb)], axis=1)
        for j in range(N_DEV):
            o_ref[j] = full[:, j * FFN_SHARD:(j + 1) * FFN_SHARD]

    return pl.pallas_call(
        body, name="shard_ffn_in", grid=(D // tr,),
        in_specs=[pl.BlockSpec((tr, 2 * D_FF), lambda i: (i, 0))],
        out_specs=pl.BlockSpec((N_DEV, tr, FFN_SHARD), lambda i: (0, i, 0)),
        out_shape=jax.ShapeDtypeStruct((N_DEV, D, FFN_SHARD), w.dtype),
        compiler_params=_params(("parallel",)),
    )(w)


def _shard_cols(w):
    k, n = w.shape[0], w.shape[1] // N_DEV
    tr = min(k, 256)

    def body(w_ref, o_ref):
        full = w_ref[...]
        for j in range(N_DEV):
            o_ref[j] = full[:, j * n:(j + 1) * n]

    return pl.pallas_call(
        body, name="shard_cols_%d" % n, grid=(k // tr,),
        in_specs=[pl.BlockSpec((tr, N_DEV * n), lambda i: (i, 0))],
        out_specs=pl.BlockSpec((N_DEV, tr, n), lambda i: (0, i, 0)),
        out_shape=jax.ShapeDtypeStruct((N_DEV, k, n), w.dtype),
        compiler_params=_params(("parallel",)),
    )(w)


IN_OFFS = tuple(sum(IN_SPLITS[:i]) for i in range(len(IN_SPLITS) + 1))
IN_SHARD = IN_OFFS[-1] // N_DEV
REGROUP_ROWS = 128


def _w_in_regroup(g, own=None, slot=None):
    def groups(full):
        fq, fk, fv, wf, cq, ckv, kr, gf, gm = [full[:, IN_OFFS[i]:IN_OFFS[i + 1]] for i in range(9)]
        rows = full.shape[0]
        w_a = jnp.concatenate([cq, ckv, gf, gm, wf, jnp.zeros((rows, 56), BF16), kr, jnp.zeros((rows, 32), BF16)], axis=1)
        return w_a, jnp.concatenate([fq, fk, fv], axis=1)

    return _from_shards(g, groups, [3200, 1536], "w_in_regroup", own, slot)


def _w_in_ungroup(da, db_):
    def body(a_ref, b_ref, o_ref):
        a = a_ref[...]
        full = jnp.concatenate([b_ref[...], a[:, 3072:3080], a[:, 0:768], a[:, 768:1024], a[:, 3136:3168],
                                a[:, 1024:3072]], axis=1)
        for j in range(N_DEV):
            o_ref[j] = full[:, j * IN_SHARD:(j + 1) * IN_SHARD]

    tr = REGROUP_ROWS
    return pl.pallas_call(
        body, name="w_in_ungroup", grid=(D // tr,),
        in_specs=[pl.BlockSpec((tr, 3200), lambda i: (i, 0)), pl.BlockSpec((tr, 1536), lambda i: (i, 0))],
        out_specs=pl.BlockSpec((N_DEV, tr, IN_SHARD), lambda i: (0, i, 0)),
        out_shape=jax.ShapeDtypeStruct((N_DEV, D, IN_SHARD), BF16),
        compiler_params=_params(("parallel",)),
    )(da, db_)


def _prepare_weights(g, own=None, slot=None):
    w = {}
    if own is not None:
        small = ("w_uq", "w_ukv", "w_out", "w_ffn_out")
        g = {n: (_with_own(a, own[n], slot[0]) if n in small else a) for n, a in g.items()}
    pick = (lambda n: (own[n], slot)) if own is not None else (lambda n: (None, None))
    if "w_in" in g:
        w["w_a"], w["w_b"] = _w_in_regroup(g["w_in"], *pick("w_in"))
    if "w_uq" in g:
        w_uq = g["w_uq"].reshape(Q_LORA, HEADS, 96)
        w["w_uq"] = jnp.pad(w_uq, ((0, 0), (0, 0), (0, 32))).reshape(Q_LORA, HEADS * LANES)
        ukv = g["w_ukv"]
        w["w_k"] = jnp.transpose(jnp.pad(ukv[:, :, :64], ((0, 0), (0, 0), (0, 64))), (1, 0, 2)).reshape(KV_LORA, HEADS * LANES)
        w["w_v"] = jnp.transpose(ukv[:, :, 64:], (1, 0, 2)).reshape(KV_LORA, HEADS * HEAD_DIM)
    if "w_out" in g:
        w["w_pf"] = _unshard_cols(g["w_proj_fox"], *pick("w_proj_fox"))
        w["w_pm"] = _unshard_cols(g["w_proj_mla"], *pick("w_proj_mla"))
        w["w_out"] = g["w_out"].reshape(D, D)
    if "w_ffn_in" in g:
        w["w_ffn_in"] = _unshard_ffn_in(g["w_ffn_in"], *pick("w_ffn_in"))
        w["w_ffn_out"] = g["w_ffn_out"].reshape(D_FF, D)
    return w


def _shard_grads(dw):
    out = {}
    if "w_a" in dw:
        out["w_in"] = _w_in_ungroup(dw["w_a"], dw["w_b"])
    if "w_uq" in dw:
        w_uq = dw["w_uq"].reshape(Q_LORA, HEADS, LANES)[:, :, :96].reshape(Q_LORA, Q_LORA)
        out["w_uq"] = w_uq.reshape(N_DEV, Q_LORA // N_DEV, Q_LORA)
        k_part = dw["w_k"].reshape(KV_LORA, HEADS, LANES)[:, :, :64]
        v_part = dw["w_v"].reshape(KV_LORA, HEADS, HEAD_DIM)
        out["w_ukv"] = jnp.transpose(jnp.concatenate([k_part, v_part], axis=2), (1, 0, 2))
    if "w_out" in dw:
        out["w_proj_fox"] = _shard_cols(dw["w_pf"])
        out["w_proj_mla"] = _shard_cols(dw["w_pm"])
        out["w_out"] = dw["w_out"].reshape(N_DEV, D // N_DEV, D)
    if "w_ffn_in" in dw:
        out["w_ffn_in"] = _shard_ffn_in(dw["w_ffn_in"])
        out["w_ffn_out"] = dw["w_ffn_out"].reshape(N_DEV, D_FF // N_DEV, D)
    return out


def _fwd_bwd(x, pos, mod, target, w, vec, wts, send, relay):
    shift_mix, scale_mix, gate_mix, shift_ffn, scale_ffn, gate_ffn = [mod[:, i * D:(i + 1) * D] for i in range(6)]
    g_pre_mix, g_post_mix, g_pre_ffn, g_post_ffn = vec["g_pre_mix"], vec["g_post_mix"], vec["g_pre_ffn"], vec["g_post_ffn"]
    g_q, g_kv = vec["g_q_lora"], vec["g_kv_lora"]

    inv_freq = 1.0 / (ROPE_THETA ** (jnp.arange(0, ROPE_DIM, 2, dtype=F32) / ROPE_DIM))
    invf = jnp.concatenate([jnp.zeros((64,), F32), inv_freq, inv_freq, jnp.zeros((32,), F32)]).reshape(1, LANES)
    ct, sa, sb = _rope_tables(pos, invf)

    def pre1(xv, g, sc, sh):
        return (xv * _rstd(xv) * g) * (1.0 + sc) + sh
    proj_a, h = _mm_epi(x, w["w_a"], "nn", 640, lambda r: ((r,), ()), "in_proj_a", 1024, outs=[(3200, 640, F32)],
                        pro=(pre1, [g_pre_mix, scale_mix, shift_mix], 0))
    qkv = _mm(h, w["w_b"], "nn", BF16, "in_proj_b")

    def lora_norm(cv, g):
        return cv * _rstd(cv) * g
    w = {**w, **wts("lora", qkv)}
    tables = [(ct, LANES), (sa, LANES), (sb, LANES)]

    def rope_q(qv, c_, a_, b_):
        return (jnp.concatenate([_rope(qv[:, hd * LANES:(hd + 1) * LANES], c_, a_, b_) for hd in range(HEADS)], axis=1),), ()
    q_m, cqn = _mm_epi(proj_a, w["w_uq"], "nn", D, rope_q, "mla_uq", 512, rows=tables, outs=[(D, D, BF16)],
                       pro=(lora_norm, [g_q], 0))

    def rope_k(kv, misc, c_, a_, b_):
        lane = lax.broadcasted_iota(jnp.int32, (1, LANES), 1)
        kpe = jnp.where((lane >= 64) & (lane < 96), _rope(misc, c_, a_, b_), 0.0)
        return (jnp.concatenate([kv[:, hd * LANES:(hd + 1) * LANES] + kpe for hd in range(HEADS)], axis=1),), ()
    k_m, ckvn = _mm_epi(proj_a, w["w_k"], "nn", D, rope_k, "mla_uk", 512, rows=[(proj_a, LANES, 24)] + tables,
                        outs=[(D, D, BF16)], pro=(lora_norm, [g_kv], Q_LORA // KV_LORA))
    v_m = _mm(ckvn, w["w_v"], "nn", BF16, "mla_uv")

    zt = jnp.transpose(proj_a[:, 3072:3080])
    bf = jnp.transpose(vec["b_forget"])
    neg_f = _fox_gates(zt, bf)
    bias = neg_f.reshape(HEADS, N_ATT, 1, ATT_T)
    o_b, lse_b = _attn_fwd(q_m, 0, k_m, 0, v_m, 0, 2 * LANES, 1.0 / math.sqrt(64 + ROPE_DIM), None, "mla_attn")
    bias = bias + wts("relay", o_b)["tok"][0, 0]
    o_a, lse_a = _attn_fwd(qkv, 0, qkv, 4, qkv, 8, LANES, 1.0 / math.sqrt(HEAD_DIM), bias, "fox_attn")

    w = {**w, **wts("proj", o_a)}
    pa = _mm(o_a, w["w_pf"], "nn", BF16, "proj_fox")

    def merge(pb_, gf, gm, pa_):
        return (_sigmoid(gf) * pa_ + _sigmoid(gm) * pb_, pb_), ()
    merged, pb = _mm_epi(o_b, w["w_pm"], "nn", 512, merge, "proj_mla", 1024,
                         rows=[(proj_a, 512, 2), (proj_a, 512, 4), (pa, 512)], outs=[(D, 512, BF16), (D, 512, BF16)])
    def post1(yv, xv, gate, gpost, gpre, sc, sh):
        x1 = xv + gate * (yv * _rstd(yv) * gpost)
        return (x1, (x1 * _rstd(x1) * gpre) * (1.0 + sc) + sh, yv), ()
    x1, h2, y = _mm_epi(merged, w["w_out"], "nn", D, post1, "out_proj", 256, rows=[(x, D)],
                        vecs=[gate_mix, g_post_mix, g_pre_ffn, scale_ffn, shift_ffn],
                        outs=[(D, D, F32), (D, D, BF16), (D, D, F32)])
    w = {**w, **wts("ffn", h2)}

    def swiglu(r):
        g, u = r[:, :FFN_T], r[:, FFN_T:]
        return (g * _sigmoid(g) * u, r), ()
    act, gu = _mm_epi(h2, w["w_ffn_in"], "nn", 2 * FFN_T, swiglu, "ffn_in", 1024,
                      outs=[(D_FF, FFN_T, BF16), (2 * D_FF, 2 * FFN_T, BF16)])

    def head(y2v, x1v, tv, gate, gpost):
        r = _rstd(y2v)
        yn = y2v * r
        n2 = yn * gpost
        err = (x1v + gate * n2) - tv
        dx2 = err * (1.0 / D)
        dn2 = dx2 * gate
        dy2 = _norm_bwd(dn2 * gpost, yn, r)
        return (dx2, dy2), (_colsum(err * err), _colsum(dx2 * n2), _colsum(dn2 * yn))
    dx2, dy2, err_cols, d_gate_ffn, d_g_post_ffn = _mm_epi(
        act, w["w_ffn_out"], "nn", D, head, "ffn_out", 256, rows=[(x1, D), (target, D)], vecs=[gate_ffn, g_post_ffn],
        outs=[(D, D, F32), (D, D, BF16)], sums=[D, D, D])

    def swiglu_bwd(da, guv):
        g, u = guv[:, :FFN_T].astype(F32), guv[:, FFN_T:].astype(F32)
        sg = _sigmoid(g)
        return (jnp.concatenate([da * u * (sg * (1.0 + g * (1.0 - sg))), da * (g * sg)], axis=1),), ()
    (dgu,) = _mm_epi(dy2, w["w_ffn_out"], "nt", FFN_T, swiglu_bwd, "ffn_out_dx", 1024, rows=[(gu, 2 * FFN_T)],
                     outs=[(2 * D_FF, 2 * FFN_T, BF16)])
    dw = {"w_ffn_out": _mm(act, dy2, "tn", BF16, "ffn_out_dw")}
    dw["w_ffn_in"] = _mm(h2, dgu, "tn", BF16, "ffn_in_dw")
    gate_mix = gate_mix + send({n: dw.pop(n) for n in ("w_ffn_in", "w_ffn_out")})[0, 0]

    def mid(dh, x1v, dx2v, yv, gpre, sc, gate, gpost):
        r2 = _rstd(x1v)
        x1n = x1v * r2
        t = dh * x1n
        dx1 = dx2v + _norm_bwd(dh * (gpre * (1.0 + sc)), x1n, r2)
        ry = _rstd(yv)
        yn = yv * ry
        dn1 = dx1 * gate
        dy = _norm_bwd(dn1 * gpost, yn, ry)
        sums = (_colsum(dh), _colsum(t) * gpre, _colsum(t) * (1.0 + sc), _colsum(dx1 * (yn * gpost)), _colsum(dn1 * yn))
        return (dx1, dy), sums
    dx1, dy, d_shift_ffn, d_scale_ffn, d_g_pre_ffn, d_gate_mix, d_g_post_mix = _mm_epi(
        dgu, w["w_ffn_in"], "nt", D, mid, "ffn_in_dx", 256, rows=[(x1, D), (dx2, D), (y, D)],
        vecs=[g_pre_ffn, scale_ffn, gate_mix, g_post_mix], outs=[(D, D, F32), (D, D, BF16)], sums=[D] * 5)

    dw["w_out"] = _mm(merged, dy, "tn", BF16, "out_proj_dw")

    def merge_bwd(dm, gf, gm, pa_, pb_):
        sf, sm = _sigmoid(gf), _sigmoid(gm)
        return (dm * sf, dm * sm, dm * pa_ * (sf * (1.0 - sf)), dm * pb_ * (sm * (1.0 - sm))), ()
    dpa, dpb, dgf, dgm = _mm_epi(dy, w["w_out"], "nt", 512, merge_bwd, "out_proj_dx", 1024,
                                 rows=[(proj_a, 512, 2), (proj_a, 512, 4), (pa, 512), (pb, 512)],
                                 outs=[(D, 512, BF16)] * 4)
    do_a = _mm(dpa, w["w_pf"], "nt", BF16, "proj_fox_dx")
    do_b = _mm(dpb, w["w_pm"], "nt", BF16, "proj_mla_dx")
    dw["w_pf"] = _mm(o_a, dpa, "tn", BF16, "proj_fox_dw")
    dw["w_pm"] = _mm(o_b, dpb, "tn", BF16, "proj_mla_dw")
    bias = bias + send({n: dw.pop(n) for n in ("w_out", "w_pf", "w_pm")})[0, 0]

    sc_a, sc_b = 1.0 / math.sqrt(HEAD_DIM), 1.0 / math.sqrt(64 + ROPE_DIM)
    dq_a, dk_a, dv_a, dbias = _attn_grad(qkv, 0, qkv, 4, qkv, 8, do_a, lse_a, LANES, sc_a, bias, BF16, "fox_attn_bwd")
    dq_m, dk_m, dv_m = _attn_grad(q_m, 0, k_m, 0, v_m, 0, do_b, lse_b, 2 * LANES, sc_b, None, F32, "mla_attn_bwd")

    def mla_rope_bwd(dq, dk, c_, a_, b_):
        lane = lax.broadcasted_iota(jnp.int32, (1, LANES), 1)
        dqs = [_rope_t(dq[:, hd * LANES:(hd + 1) * LANES], c_, a_, b_) for hd in range(HEADS)]
        dkpe = dk[:, 0:LANES]
        for hd in range(1, HEADS):
            dkpe = dkpe + dk[:, hd * LANES:(hd + 1) * LANES]
        dkpe = jnp.where((lane >= 64) & (lane < 96), dkpe, 0.0)
        dkr = jnp.where((lane >= 64) & (lane < 96), _rope_t(dkpe, c_, a_, b_), 0.0)
        return (jnp.concatenate(dqs, axis=1), dk, dkr), ()
    dqb, dkb, dkr = _rowwise(mla_rope_bwd, [(dq_m, D, 0), (dk_m, D, 0), (ct, LANES, 0), (sa, LANES, 0), (sb, LANES, 0)],
                             [], [(D, BF16), (D, BF16), (LANES, F32)], [], "mla_rope_bwd")
    def lora_q_bwd(dq, cq, gq):
        rq = _rstd(cq)
        cqh = cq * rq
        return (_norm_bwd(dq * gq, cqh, rq),), (_colsum(dq * cqh),)
    dcq, d_g_q = _mm_epi(dqb, w["w_uq"], "nt", Q_LORA, lora_q_bwd, "mla_uq_dx", 512, rows=[(proj_a, Q_LORA, 0)],
                         vecs=[g_q], outs=[(Q_LORA, Q_LORA, BF16)], sums=[Q_LORA])
    dw["w_uq"] = _mm(cqn, dqb, "tn", BF16, "mla_uq_dw")

    def lora_kv_bwd(dv_part, dk_part, ckv, gkv):
        dkv = dv_part + dk_part
        rk = _rstd(ckv)
        ckh = ckv * rk
        return (_norm_bwd(dkv * gkv, ckh, rk),), (_colsum(dkv * ckh),)
    dckv, d_g_kv = _mm_epi(dv_m, w["w_v"], "nt", KV_LORA, lora_kv_bwd, "mla_uv_dx", 1024,
                           rows=[(_mm(dkb, w["w_k"], "nt", F32, "mla_uk_dx"), KV_LORA), (proj_a, KV_LORA, 3)],
                           vecs=[g_kv], outs=[(KV_LORA, KV_LORA, BF16)], sums=[KV_LORA])
    dw["w_k"] = _mm(ckvn, dkb, "tn", BF16, "mla_uk_dw")
    dw["w_v"] = _mm(ckvn, dv_m, "tn", BF16, "mla_uv_dw")

    dzt, d_bf = _fox_gates_bwd(dbias.reshape(HEADS, S), zt, bf)
    dmisc = (dkr + jnp.pad(jnp.transpose(dzt), ((0, 0), (0, LANES - HEADS)))).astype(BF16)
    dproj_a = jnp.concatenate([dcq, dckv, dgf, dgm, dmisc], axis=1)
    dqkv = jnp.concatenate([dq_a, dk_a, dv_a], axis=1)
    dw["w_a"] = _mm(h, dproj_a, "tn", BF16, "in_proj_a_dw")
    dw["w_b"] = _mm(h, dqkv, "tn", BF16, "in_proj_b_dw")
    tok = send(dw, True)
    dh_a = _mm(dproj_a, w["w_a"], "nt", F32, "in_proj_a_dx", dep=tok)
    g_pre_mix = g_pre_mix + relay(dh_a)[0, 0]

    def first(dh_b, dh_a, xv, dx1v, gpre, sc):
        dhv = dh_b + dh_a
        r = _rstd(xv)
        xn = xv * r
        t = dhv * xn
        dx = dx1v + _norm_bwd(dhv * (gpre * (1.0 + sc)), xn, r)
        return (dx,), (_colsum(dhv), _colsum(t) * gpre, _colsum(t) * (1.0 + sc))
    grad_x, d_shift_mix, d_scale_mix, d_g_pre_mix = _mm_epi(
        dqkv, w["w_b"], "nt", D, first, "in_proj_b_dx", 256,
        rows=[(dh_a, D), (x, D), (dx1, D)],
        vecs=[g_pre_mix, scale_mix], outs=[(D, D, F32)], sums=[D] * 3)

    dmod = jnp.concatenate([d_shift_mix, d_scale_mix, d_gate_mix, d_shift_ffn, d_scale_ffn, d_gate_ffn], axis=1)
    small = dict(dmod=dmod, g_pre_mix=d_g_pre_mix, g_post_mix=d_g_post_mix, g_pre_ffn=d_g_pre_ffn,
                 g_post_ffn=d_g_post_ffn, g_q_lora=d_g_q, g_kv_lora=d_g_kv,
                 b_forget=jnp.pad(jnp.transpose(d_bf), ((0, 0), (0, LANES - HEADS))), err=err_cols)
    return grad_x, small


SMALL_ORDER = ("dmod", "g_pre_mix", "g_post_mix", "g_pre_ffn", "g_post_ffn", "g_q_lora", "g_kv_lora", "b_forget", "err")
SMALL_PARAM = {"dmod": "b_ada"}
MATRICES = ("w_in", "w_uq", "w_ukv", "w_proj_fox", "w_proj_mla", "w_out", "w_ffn_in", "w_ffn_out")
WEIGHTS = ("w_ada", "b_ada", "g_pre_mix", "g_post_mix", "g_pre_ffn", "g_post_ffn", "w_in", "b_forget", "g_q_lora",
           "w_uq", "g_kv_lora", "w_ukv", "w_proj_fox", "w_proj_mla", "w_out", "w_ffn_in", "w_ffn_out")


def kernel(x, c, positions, w_ada, b_ada, g_pre_mix, g_post_mix, g_pre_ffn, g_post_ffn, w_in, b_forget, g_q_lora, w_uq, g_kv_lora, w_ukv, w_proj_fox, w_proj_mla, w_out, w_ffn_in, w_ffn_out, loss_target, m_w_ada, m_b_ada, m_g_pre_mix, m_g_post_mix, m_g_pre_ffn, m_g_post_ffn, m_w_in, m_b_forget, m_g_q_lora, m_w_uq, m_g_kv_lora, m_w_ukv, m_w_proj_fox, m_w_proj_mla, m_w_out, m_w_ffn_in, m_w_ffn_out, v_w_ada, v_b_ada, v_g_pre_mix, v_g_post_mix, v_g_pre_ffn, v_g_post_ffn, v_w_in, v_b_forget, v_g_q_lora, v_w_uq, v_g_kv_lora, v_w_ukv, v_w_proj_fox, v_w_proj_mla, v_w_out, v_w_ffn_in, v_w_ffn_out):
    prm = dict(w_ada=w_ada, b_ada=b_ada, g_pre_mix=g_pre_mix, g_post_mix=g_post_mix, g_pre_ffn=g_pre_ffn,
               g_post_ffn=g_post_ffn, w_in=w_in, b_forget=b_forget, g_q_lora=g_q_lora, w_uq=w_uq, g_kv_lora=g_kv_lora,
               w_ukv=w_ukv, w_proj_fox=w_proj_fox, w_proj_mla=w_proj_mla, w_out=w_out, w_ffn_in=w_ffn_in, w_ffn_out=w_ffn_out)
    mom = dict(w_ada=m_w_ada, b_ada=m_b_ada, g_pre_mix=m_g_pre_mix, g_post_mix=m_g_post_mix, g_pre_ffn=m_g_pre_ffn,
               g_post_ffn=m_g_post_ffn, w_in=m_w_in, b_forget=m_b_forget, g_q_lora=m_g_q_lora, w_uq=m_w_uq,
               g_kv_lora=m_g_kv_lora, w_ukv=m_w_ukv, w_proj_fox=m_w_proj_fox, w_proj_mla=m_w_proj_mla, w_out=m_w_out,
               w_ffn_in=m_w_ffn_in, w_ffn_out=m_w_ffn_out)
    var = dict(w_ada=v_w_ada, b_ada=v_b_ada, g_pre_mix=v_g_pre_mix, g_post_mix=v_g_post_mix, g_pre_ffn=v_g_pre_ffn,
               g_post_ffn=v_g_post_ffn, w_in=v_w_in, b_forget=v_b_forget, g_q_lora=v_g_q_lora, w_uq=v_w_uq,
               g_kv_lora=v_g_kv_lora, w_ukv=v_w_ukv, w_proj_fox=v_w_proj_fox, w_proj_mla=v_w_proj_mla, w_out=v_w_out,
               w_ffn_in=v_w_ffn_in, w_ffn_out=v_w_ffn_out)
    me = _flat(*_coords())
    slot = jnp.reshape(me, (1,)).astype(jnp.int32)

    own = {n: prm[n][0].astype(BF16) for n in MATRICES}
    no_dep = jnp.zeros((8, LANES), F32)
    (st_c, st_in), tok = _async_start([[c], [own["w_in"]]], ["gather", "spread"], no_dep, "gather_in_start")
    (c_own,), (c_land,) = _async_wait(st_c, tok, "gather_c_wait")
    c_all = _with_own(c_land, c_own, me).reshape(N_DEV, D)
    ada_cols = w_ada.shape[2]
    b_cols = lax.dynamic_slice(b_ada, (0, me * ada_cols), (1, ada_cols))
    mod_cols, silu_c = _mod_part(c_all, w_ada[0], b_cols)
    (mod_all,) = _all_gather([mod_cols], "gather_mod")

    (w_in_own,), (w_in_land,) = _async_wait(st_in, mod_all, "gather_in_wait")
    (st_in,), tok = _async_start([[w_in_land]], "forward", no_dep, "gather_in_forward")
    _, (w_in_land,) = _async_wait(st_in, tok, "gather_in_forward_wait")
    w = _prepare_weights({"w_in": w_in_land}, {"w_in": w_in_own}, slot)
    later = dict(lora=("w_uq", "w_ukv"), proj=("w_proj_fox", "w_proj_mla", "w_out"), ffn=("w_ffn_in", "w_ffn_out"))
    states, tok = _async_start([[own[n] for n in names] for names in later.values()], ["gather", "spread", "spread"],
                               w["w_b"], "gather_rest_start")
    gather_state = dict(zip(later, states))
    own_thru = {}

    def wts(group, after):
        if group == "relay":
            second = []
            for name in ("proj", "ffn"):
                own_thru[name], lands = _async_wait(gather_state[name], after, "gather_" + name + "_wait")
                second.append(lands)
                after = lands[0]
            (gather_state["proj"], gather_state["ffn"]), t = _async_start(second, "forward", no_dep, "gather_rest_forward")
            return {"tok": t}
        srcs, lands = _async_wait(gather_state[group], after, "gather_" + group + "_landed")
        srcs = own_thru.get(group, srcs)
        return _prepare_weights(dict(zip(later[group], lands)), dict(zip(later[group], srcs)), slot)

    sent, last = [], {}

    def send(grads, final=False):
        shards = _shard_grads(grads)
        names = list(shards)
        (state,), t = _async_start([[shards[n] for n in names]], "pair" if final else "exchange", no_dep,
                                   "exchange_" + names[0] + "_start")
        if final:
            last.update(names=names, state=state)
        else:
            sent.append((names, state))
        return t

    def relay(after):
        srcs, lands = _async_wait(last["state"], after, "exchange_pair_wait")
        sums = []
        for src, land in zip(srcs, lands):
            by_chip = src.reshape((N_DEV // 2, 2) + src.shape[1:])
            sums.append(_add_blocks(lax.dynamic_index_in_dim(by_chip, lax.axis_index("c"), 1, keepdims=False), land))
        (last["state"],), t = _async_start([sums], "chips", no_dep, "exchange_chips_start")
        return t

    mod = lax.dynamic_index_in_dim(mod_all, me, axis=1, keepdims=False).reshape(1, 6 * D) + tok[0, 0]

    vec = dict(g_pre_mix=g_pre_mix, g_post_mix=g_post_mix, g_pre_ffn=g_pre_ffn, g_post_ffn=g_post_ffn,
               g_q_lora=g_q_lora, g_kv_lora=g_kv_lora, b_forget=b_forget)
    pos = positions.astype(F32).reshape(S, 1)
    grad_x, small = _fwd_bwd(x[0], pos, mod, loss_target[0], w, vec, wts, send, relay)

    bundle = jnp.concatenate([small[n] for n in SMALL_ORDER], axis=1)
    (small_state,), tok = _async_start([[bundle]], "gather", jnp.zeros((8, LANES), F32), "gather_small_start")

    out = {}
    after = tok
    for names, state in sent:
        srcs, lands = _async_wait(state, after, "exchange_" + names[0] + "_wait")
        for n, src, land in zip(names, srcs, lands):
            out[n] = _adamw(prm[n][0], mom[n][0], var[n][0], land, "adamw_" + n, src, slot)
            after = out[n][0]
    srcs, lands = _async_wait(last["state"], after, "exchange_chips_wait")
    for n, src, land in zip(last["names"], srcs, lands):
        out[n] = _adamw(prm[n][0], mom[n][0], var[n][0], land, "adamw_" + n, src, slot // 2)
        after = out[n][0]

    (own_bundle,), (bundle_all,) = _async_wait(small_state, after, "gather_small_wait")
    bundle_all = _with_own(bundle_all, own_bundle, me)
    dmod_all = bundle_all[:, 0, :6 * D]
    dm_cols = lax.dynamic_slice(dmod_all, (0, me * ada_cols), (N_DEV, ada_cols))
    g_ada = _w_ada_grad(jnp.transpose(silu_c), dm_cols)
    out["w_ada"] = _adamw(w_ada[0], m_w_ada[0], v_w_ada[0], g_ada[None], "adamw_w_ada")

    offsets, off = {}, 0
    for n in SMALL_ORDER:
        offsets[n] = off
        off += small[n].shape[1]
    names = [SMALL_PARAM.get(n, n) for n in SMALL_ORDER if n != "err"]
    results, err = _adamw_rows(bundle_all, [offsets[n] for n in SMALL_ORDER if n != "err"],
                               [prm[n] for n in names], [mom[n] for n in names], [var[n] for n in names],
                               offsets["err"], D)
    out.update(zip(names, results))
    loss = 0.5 * jnp.sum(err) / D

    res = [loss, grad_x[None]]
    for kind in range(4):
        for n in WEIGHTS:
            t = out[n][kind]
            res.append(t[None] if prm[n].ndim == 3 else t)
    return tuple(res)
```

```python
import functools
import math

import jax
import jax.numpy as jnp
from jax import lax
from jax.experimental import pallas as pl
from jax.experimental.pallas import tpu as pltpu

F32 = jnp.float32
BF16 = jnp.bfloat16

N_DEV = 8
S = 2048
D = 1024
D_FF = 2816
HEADS = 8
HEAD_DIM = 64
Q_LORA = 768
KV_LORA = 256
ROPE_DIM = 32
ROPE_THETA = 10000.0
NORM_EPS = 1e-6
LANES = 128
VMEM_LIMIT = 56 * 1024 * 1024

ADAM_LR = 0.001
ADAM_B1 = 0.9
ADAM_B2 = 0.999
ADAM_EPS = 1e-08
ADAM_WD = 0.01
ADAM_STEP = 10

ATT_T = 256
LOG2E = 1.4426950408889634
N_ATT = S // ATT_T

NN = (((1,), (0,)), ((), ()))
NT = (((1,), (1,)), ((), ()))
TN = (((0,), (0,)), ((), ()))
MESH = pl.DeviceIdType.MESH


def _params(sem=None):
    return pltpu.CompilerParams(dimension_semantics=sem, vmem_limit_bytes=VMEM_LIMIT)


def _pick(n, cap):
    best = None
    for t in range(LANES, cap + 1, LANES):
        if n % t == 0:
            best = t
    return best if best is not None else n


def _mm(a, b, mode, out_dtype, name, acc=None, dep=None):
    if mode == "nn":
        (m, k), (k2, n), dn = a.shape, b.shape, NN
    elif mode == "nt":
        (m, k), (n, k2), dn = a.shape, b.shape, NT
    else:
        (k, m), (k2, n), dn = a.shape, b.shape, TN
    assert k == k2, (a.shape, b.shape, mode)
    tn = _pick(n, 640)
    tm = _pick(m, 1536)
    osz = jnp.dtype(out_dtype).itemsize

    def need(tm_):
        blk = tm_ * k * 2 + tn * k * 2 + tm_ * tn * osz + (tm_ * tn * 4 if acc is not None else 0)
        return 2 * blk + tm_ * tn * 4
    while need(tm) > 36 * 1024 * 1024 and tm % 256 == 0:
        tm //= 2

    def body(*refs):
        a_ref, b_ref, o_ref = refs[0], refs[1], refs[-1]
        r = lax.dot_general(a_ref[...], b_ref[...], dn, preferred_element_type=F32)
        if acc is not None:
            r = r + refs[2][...]
        o_ref[...] = r.astype(o_ref.dtype)

    if mode == "tn":
        a_spec = pl.BlockSpec((k, tm), lambda i, j: (0, i))
    else:
        a_spec = pl.BlockSpec((tm, k), lambda i, j: (i, 0))
    if mode == "nt":
        b_spec = pl.BlockSpec((tn, k), lambda i, j: (j, 0))
    else:
        b_spec = pl.BlockSpec((k, tn), lambda i, j: (0, j))
    o_spec = pl.BlockSpec((tm, tn), lambda i, j: (i, j))
    in_specs = [a_spec, b_spec] + ([o_spec] if acc is not None else [])
    in_specs += [pl.BlockSpec(memory_space=pl.ANY)] if dep is not None else []
    args = (a, b) + ((acc,) if acc is not None else ()) + ((dep,) if dep is not None else ())
    return pl.pallas_call(
        body, name=name, grid=(m // tm, n // tn),
        in_specs=in_specs, out_specs=o_spec,
        out_shape=jax.ShapeDtypeStruct((m, n), out_dtype),
        compiler_params=_params(("parallel", "parallel")),
    )(*args)


def _mm_epi(a, b, mode, tnb, epi, name, tm, rows=(), vecs=(), outs=(), sums=(), pro=None):
    m = a.shape[0]
    k, nb = (b.shape if mode == "nn" else b.shape[::-1])
    dn = NN if mode == "nn" else NT
    pro_fn, pro_vecs, a_off = pro if pro is not None else (None, (), 0)
    n_in = 2 + len(rows) + len(vecs)
    n_all = n_in + len(pro_vecs)

    def body(*refs):
        if pro is not None:
            a_out, a_scr = refs[-2:]
            refs = refs[:-2]

            @pl.when(pl.program_id(1) == 0)
            def _():
                a_scr[...] = pro_fn(refs[0][...], *[x[...] for x in refs[n_in:n_all]]).astype(BF16)
                a_out[...] = a_scr[...]
            lhs = a_scr[...]
        else:
            lhs = refs[0][...]
        r = lax.dot_general(lhs, refs[1][...], dn, preferred_element_type=F32)
        o_vals, s_vals = epi(r, *[x[...] for x in refs[2:n_in]])
        o_refs = refs[n_all:n_all + len(outs)]
        s_refs = refs[n_all + len(outs):]
        assert len(o_vals) == len(o_refs) and len(s_vals) == len(s_refs)
        for o_ref, val in zip(o_refs, o_vals):
            o_ref[...] = val.astype(o_ref.dtype)
        if sums:
            @pl.when((pl.program_id(0) == 0) & (pl.program_id(1) == 0))
            def _():
                for s_ref in s_refs:
                    s_ref[...] = jnp.zeros(s_ref.shape, F32)
            for s_ref, val in zip(s_refs, s_vals):
                s_ref[...] += val

    b_spec = pl.BlockSpec((k, tnb), lambda i, j: (0, j)) if mode == "nn" else pl.BlockSpec((tnb, k), lambda i, j: (j, 0))
    in_specs = [pl.BlockSpec((tm, k), lambda i, j: (i, a_off)), b_spec]
    rows = [tuple(r) + (0,) * (3 - len(r)) for r in rows]
    in_specs += [pl.BlockSpec((tm, w), functools.partial(lambda i, j, off: (i, j + off), off=off)) for _, w, off in rows]
    in_specs += [pl.BlockSpec(v.shape, lambda i, j: (0, 0)) for v in list(vecs) + list(pro_vecs)]
    out_specs = [pl.BlockSpec((tm, w), lambda i, j: (i, j)) for _, w, _ in outs]
    out_specs += [pl.BlockSpec((1, w), lambda i, j: (0, 0)) for w in sums]
    out_shape = [jax.ShapeDtypeStruct((m, full), dt) for full, _, dt in outs]
    out_shape += [jax.ShapeDtypeStruct((1, w), F32) for w in sums]
    if pro is not None:
        out_specs.append(pl.BlockSpec((tm, k), lambda i, j: (i, 0)))
        out_shape.append(jax.ShapeDtypeStruct((m, k), BF16))
    return pl.pallas_call(
        body, name=name, grid=(m // tm, nb // tnb),
        in_specs=in_specs, out_specs=out_specs, out_shape=out_shape,
        scratch_shapes=[pltpu.VMEM((tm, k), BF16)] if pro is not None else [],
        compiler_params=_params(("arbitrary", "arbitrary") if sums else ("parallel", "arbitrary" if pro is not None else "parallel")),
    )(a, b, *[r[0] for r in rows], *vecs, *pro_vecs)


def _rowwise(fn, row_ins, vec_ins, row_outs, sum_outs, name, tm=256):
    n_in = len(row_ins) + len(vec_ins)
    n_o = len(row_outs)
    rows = row_ins[0][0].shape[0]

    def body(*refs):
        vals = [r[...] for r in refs[:n_in]]
        outs = refs[n_in:]
        ro, so = fn(*vals)
        assert len(ro) == n_o and len(so) == len(sum_outs)
        for r, v in zip(outs[:n_o], ro):
            r[...] = v.astype(r.dtype)
        if sum_outs:
            @pl.when(pl.program_id(0) == 0)
            def _():
                for r in outs[n_o:]:
                    r[...] = jnp.zeros(r.shape, F32)
            for r, v in zip(outs[n_o:], so):
                r[...] += v

    in_specs = [pl.BlockSpec((tm, w), functools.partial(lambda i, b: (i, b), b=b)) for _, w, b in row_ins]
    in_specs += [pl.BlockSpec(v.shape, lambda i: (0, 0)) for v in vec_ins]
    out_specs = [pl.BlockSpec((tm, w), lambda i: (i, 0)) for w, _ in row_outs]
    out_specs += [pl.BlockSpec((1, w), lambda i: (0, 0)) for w in sum_outs]
    out_shape = [jax.ShapeDtypeStruct((rows, w), dt) for w, dt in row_outs]
    out_shape += [jax.ShapeDtypeStruct((1, w), F32) for w in sum_outs]
    return pl.pallas_call(
        body, name=name, grid=(rows // tm,),
        in_specs=in_specs, out_specs=out_specs, out_shape=out_shape,
        compiler_params=_params(("arbitrary",)),
    )(*[a for a, _, _ in row_ins], *vec_ins)


def _sigmoid(x):
    return 1.0 / (1.0 + jnp.exp(-x))


def _rstd(x):
    return lax.rsqrt(jnp.mean(x * x, axis=-1, keepdims=True) + NORM_EPS)


def _norm_bwd(dyn, xn, r):
    return r * (dyn - xn * jnp.mean(dyn * xn, axis=-1, keepdims=True))


def _colsum(x):
    return jnp.sum(x, axis=0, keepdims=True)


def _rope_tables(pos, invf):
    def fn(p, f):
        lane = lax.broadcasted_iota(jnp.int32, (1, LANES), 1)
        ang = p * f
        cs, sn = jnp.cos(ang), jnp.sin(ang)
        rot = (lane >= 64) & (lane < 96)
        ct = jnp.where(lane < 64, 1.0, jnp.where(rot, cs, 0.0))
        sa = jnp.where((lane >= 64) & (lane < 80), -sn, 0.0)
        sb = jnp.where((lane >= 80) & (lane < 96), sn, 0.0)
        return (ct, sa, sb), ()
    return _rowwise(fn, [(pos, 1, 0)], [invf], [(LANES, F32)] * 3, [], "rope_tables")


def _rope(x, ct, sa, sb):
    return x * ct + pltpu.roll(x, LANES - 16, 1) * sa + pltpu.roll(x, 16, 1) * sb


def _rope_t(x, ct, sa, sb):
    return x * ct - pltpu.roll(x, LANES - 16, 1) * sa - pltpu.roll(x, 16, 1) * sb


def _head_mask(width, hh):
    lane = lax.broadcasted_iota(jnp.int32, (1, width), 1)
    half = width // 2
    return (lane >= hh * half) & (lane < (hh + 1) * half)


ATT_PP = 2
ATT_CHAINS = [(a, hh) for a in range(ATT_PP) for hh in range(2)]
ATT_G = HEADS // (2 * ATT_PP)


def _pair(ref_or_val, a, width, rows=slice(None)):
    return ref_or_val[rows, a * width:(a + 1) * width]


def _attn_fwd(q, qo, k, ko, v, vo, dkp, scale, bias, name):
    T = ATT_T
    assert qo % ATT_PP == 0 and ko % ATT_PP == 0 and vo % ATT_PP == 0
    qo, ko, vo = qo // ATT_PP, ko // ATT_PP, vo // ATT_PP

    def body(*refs):
        if bias is not None:
            q_ref, k_ref, v_ref, b_ref, o_ref, lse_ref, s_scr = refs
        else:
            q_ref, k_ref, v_ref, o_ref, lse_ref, s_scr = refs
        i = pl.program_id(1)
        row = lax.broadcasted_iota(jnp.int32, (T, T), 0)
        col = lax.broadcasted_iota(jnp.int32, (T, T), 1)
        qms = []
        for a, hh in ATT_CHAINS:
            qb = _pair(q_ref, a, dkp)
            qms.append(jnp.where(_head_mask(dkp, hh), qb, jnp.zeros_like(qb)))

        def fold(t):
            return [t[:, c * LANES:(c + 1) * LANES] for c in range(T // LANES)]

        def run(nt):
            mls = [jnp.full((T, LANES), -jnp.inf, F32) for _ in ATT_CHAINS]
            for j in range(nt):
                ks = slice(j * T, (j + 1) * T)
                for ci, (a, hh) in enumerate(ATT_CHAINS):
                    s = lax.dot_general(qms[ci], _pair(k_ref, a, dkp, ks), NT, preferred_element_type=F32) * (scale * LOG2E)
                    if bias is not None:
                        s = s + b_ref[2 * a + hh, j] * LOG2E
                    if j == nt - 1:
                        s = jnp.where(row >= col, s, -jnp.inf)
                    s_scr[ci, j] = s
                    for part in fold(s):
                        mls[ci] = jnp.maximum(mls[ci], part)
            ms = [jnp.max(ml, axis=1, keepdims=True) for ml in mls]
            mbs = [jnp.broadcast_to(m, (T, LANES)) for m in ms]
            for a in range(ATT_PP):
                ls = [jnp.zeros((T, LANES), F32) for _ in range(2)]
                ps, vms = [], []
                for j in range(nt):
                    vb = _pair(v_ref, a, LANES, slice(j * T, (j + 1) * T))
                    for hh in range(2):
                        parts = [jnp.exp2(part - mbs[2 * a + hh]) for part in fold(s_scr[2 * a + hh, j])]
                        for part in parts:
                            ls[hh] = ls[hh] + part
                        ps.append(jnp.concatenate(parts, axis=1).astype(BF16))
                        vms.append(jnp.where(_head_mask(LANES, hh), vb, jnp.zeros_like(vb)))
                acc = lax.dot_general(jnp.concatenate(ps, axis=1), jnp.concatenate(vms, axis=0), NN,
                                      preferred_element_type=F32)
                l0, l1 = [jnp.sum(l, axis=1, keepdims=True) for l in ls]
                lse_ref[2 * a] = ms[2 * a] + jnp.log2(l0)
                lse_ref[2 * a + 1] = ms[2 * a + 1] + jnp.log2(l1)
                inv = jnp.where(_head_mask(LANES, 0), 1.0 / l0, 1.0 / l1)
                o_ref[:, a * LANES:(a + 1) * LANES] = (acc * inv).astype(o_ref.dtype)

        for nt in range(1, N_ATT + 1):
            pl.when(i == nt - 1)(functools.partial(run, nt))

    in_specs = [
        pl.BlockSpec((T, ATT_PP * dkp), lambda g, i: (i, qo + g)),
        pl.BlockSpec((S, ATT_PP * dkp), lambda g, i: (0, ko + g)),
        pl.BlockSpec((S, ATT_PP * LANES), lambda g, i: (0, vo + g)),
    ]
    args = [q, k, v]
    if bias is not None:
        in_specs.append(pl.BlockSpec((2 * ATT_PP, N_ATT, 1, T), lambda g, i: (g, 0, 0, 0)))
        args.append(bias)
    return pl.pallas_call(
        body, name=name, grid=(ATT_G, N_ATT),
        in_specs=in_specs,
        out_specs=[pl.BlockSpec((T, ATT_PP * LANES), lambda g, i: (i, g)),
                   pl.BlockSpec((2 * ATT_PP, T, 1), lambda g, i: (g, i, 0))],
        out_shape=[jax.ShapeDtypeStruct((S, HEADS * HEAD_DIM), BF16),
                   jax.ShapeDtypeStruct((HEADS, S, 1), F32)],
        scratch_shapes=[pltpu.VMEM((len(ATT_CHAINS), N_ATT, T, T), F32)],
        compiler_params=_params(("parallel", "arbitrary")),
    )(*args)


def _attn_grad(q, qo, k, ko, v, vo, do, lse, dkp, scale, bias, qk_dtype, name):
    T = ATT_T
    has_b = bias is not None
    qo, ko, vo = qo // ATT_PP, ko // ATT_PP, vo // ATT_PP
    n_ch = len(ATT_CHAINS)

    def body(*refs):
        q_ref, k_ref, v_ref, do_ref, lse_ref = refs[:5]
        refs = refs[5:]
        if has_b:
            b_ref, refs = refs[0], refs[1:]
        dq_ref, dk_ref, dv_ref = refs[:3]
        refs = refs[3:]
        if has_b:
            db_ref, refs = refs[0], refs[1:]
        p_scr, dp_scr, dk_acc, dv_acc = refs[:4]
        db_acc = refs[4] if has_b else None
        i = pl.program_id(1)

        @pl.when(i == 0)
        def _():
            dk_acc[...] = jnp.zeros(dk_acc.shape, F32)
            dv_acc[...] = jnp.zeros(dv_acc.shape, F32)
            if has_b:
                db_acc[...] = jnp.zeros(db_acc.shape, F32)

        row = lax.broadcasted_iota(jnp.int32, (T, T), 0)
        col = lax.broadcasted_iota(jnp.int32, (T, T), 1)

        def fold(t):
            return [t[:, c * LANES:(c + 1) * LANES] for c in range(T // LANES)]

        qms, doms, lses = [], [], []
        for a, hh in ATT_CHAINS:
            qb, dob = _pair(q_ref, a, dkp), _pair(do_ref, a, LANES)
            qms.append(jnp.where(_head_mask(dkp, hh), qb, jnp.zeros_like(qb)))
            doms.append(jnp.where(_head_mask(LANES, hh), dob, jnp.zeros_like(dob)))
            lses.append(lse_ref[2 * a + hh])

        def run(nt):
            dls = [jnp.zeros((T, LANES), F32) for _ in ATT_CHAINS]
            for j in range(nt):
                ks = slice(j * T, (j + 1) * T)
                for ci, (a, hh) in enumerate(ATT_CHAINS):
                    s = lax.dot_general(qms[ci], _pair(k_ref, a, dkp, ks), NT, preferred_element_type=F32) * (scale * LOG2E)
                    if has_b:
                        s = s + b_ref[ci, j] * LOG2E
                    s = s - lses[ci]
                    if j == nt - 1:
                        s = jnp.where(row >= col, s, -jnp.inf)
                    p = jnp.exp2(s)
                    dp = lax.dot_general(doms[ci], _pair(v_ref, a, LANES, ks), NT, preferred_element_type=F32)
                    p_scr[ci, j] = p
                    dp_scr[ci, j] = dp
                    for part in fold(p * dp):
                        dls[ci] = dls[ci] + part
            deltas = [jnp.broadcast_to(jnp.sum(dl, axis=1, keepdims=True), (T, LANES)) for dl in dls]
            for a in range(ATT_PP):
                ds_all, km_all = [], []
                qm2t = jnp.transpose(jnp.concatenate([qms[2 * a], qms[2 * a + 1]], axis=0))
                dom2t = jnp.transpose(jnp.concatenate([doms[2 * a], doms[2 * a + 1]], axis=0))
                for j in range(nt):
                    ks = slice(j * T, (j + 1) * T)
                    kb = _pair(k_ref, a, dkp, ks)
                    p2, ds2 = [], []
                    for hh in range(2):
                        ci = 2 * a + hh
                        p = p_scr[ci, j]
                        ds = jnp.concatenate([pp * (dd - deltas[ci]) for pp, dd in zip(fold(p), fold(dp_scr[ci, j]))], axis=1)
                        if has_b:
                            db_acc[ci, j] += jnp.sum(ds, axis=0, keepdims=True)
                        p2.append(p.astype(BF16))
                        ds2.append((ds * scale).astype(BF16))
                        km_all.append(jnp.where(_head_mask(dkp, hh), kb, jnp.zeros_like(kb)))
                    dv_acc[a * LANES:(a + 1) * LANES, ks] += lax.dot_general(
                        dom2t, jnp.concatenate(p2, axis=0), NN, preferred_element_type=F32)
                    dk_acc[a * dkp:(a + 1) * dkp, ks] += lax.dot_general(
                        qm2t, jnp.concatenate(ds2, axis=0), NN, preferred_element_type=F32)
                    ds_all += ds2
                dq = lax.dot_general(jnp.concatenate(ds_all, axis=1), jnp.concatenate(km_all, axis=0), NN,
                                     preferred_element_type=F32)
                dq_ref[:, a * dkp:(a + 1) * dkp] = dq.astype(dq_ref.dtype)

        for nt in range(1, N_ATT + 1):
            pl.when(i == nt - 1)(functools.partial(run, nt))

        @pl.when(i == N_ATT - 1)
        def _():
            dk_ref[...] = jnp.transpose(dk_acc[...]).astype(dk_ref.dtype)
            dv_ref[...] = jnp.transpose(dv_acc[...]).astype(dv_ref.dtype)
            if has_b:
                db_ref[...] = db_acc[...]

    in_specs = [
        pl.BlockSpec((T, ATT_PP * dkp), lambda g, i: (i, qo + g)),
        pl.BlockSpec((S, ATT_PP * dkp), lambda g, i: (0, ko + g)),
        pl.BlockSpec((S, ATT_PP * LANES), lambda g, i: (0, vo + g)),
        pl.BlockSpec((T, ATT_PP * LANES), lambda g, i: (i, g)),
        pl.BlockSpec((2 * ATT_PP, T, 1), lambda g, i: (g, i, 0)),
    ]
    args = [q, k, v, do, lse]
    out_specs = [
        pl.BlockSpec((T, ATT_PP * dkp), lambda g, i: (i, g)),
        pl.BlockSpec((S, ATT_PP * dkp), lambda g, i: (0, g)),
        pl.BlockSpec((S, ATT_PP * LANES), lambda g, i: (0, g)),
    ]
    width = (HEADS // 2) * dkp
    out_shape = [
        jax.ShapeDtypeStruct((S, width), qk_dtype),
        jax.ShapeDtypeStruct((S, width), qk_dtype),
        jax.ShapeDtypeStruct((S, HEADS * HEAD_DIM), BF16),
    ]
    scratch = [pltpu.VMEM((n_ch, N_ATT, T, T), F32), pltpu.VMEM((n_ch, N_ATT, T, T), F32),
               pltpu.VMEM((ATT_PP * dkp, S), F32), pltpu.VMEM((ATT_PP * LANES, S), F32)]
    if has_b:
        bspec = pl.BlockSpec((2 * ATT_PP, N_ATT, 1, T), lambda g, i: (g, 0, 0, 0))
        in_specs.append(bspec)
        args.append(bias)
        out_specs.append(bspec)
        out_shape.append(jax.ShapeDtypeStruct((HEADS, N_ATT, 1, T), F32))
        scratch.append(pltpu.VMEM((2 * ATT_PP, N_ATT, 1, T), F32))
    return pl.pallas_call(
        body, name=name, grid=(ATT_G, N_ATT),
        in_specs=in_specs, out_specs=out_specs, out_shape=out_shape, scratch_shapes=scratch,
        compiler_params=_params(("parallel", "arbitrary")),
    )(*args)


def _tri(upper):
    a = lax.broadcasted_iota(jnp.int32, (LANES, LANES), 0)
    b = lax.broadcasted_iota(jnp.int32, (LANES, LANES), 1)
    return jnp.where(a <= b if upper else a >= b, 1.0, 0.0).astype(F32)


def _fox_gates(zt, bf):
    def body(z_ref, b_ref, o_ref):
        tri = _tri(True)
        carry = jnp.zeros((HEADS, 1), F32)
        for t in range(S // LANES):
            sl = slice(t * LANES, (t + 1) * LANES)
            z = z_ref[:, sl] + b_ref[...]
            logf = jnp.minimum(z, 0.0) - jnp.log(1.0 + jnp.exp(-jnp.abs(z)))
            c = lax.dot_general(logf, tri, NN, preferred_element_type=F32,
                                precision=lax.Precision.HIGHEST) + carry
            o_ref[:, sl] = -c
            carry = c[:, LANES - 1:LANES]

    return pl.pallas_call(
        body, name="fox_gates", out_shape=jax.ShapeDtypeStruct((HEADS, S), F32),
        compiler_params=_params(),
    )(zt, bf)


def _fox_gates_bwd(dbias, zt, bf):
    def body(d_ref, z_ref, b_ref, dz_ref, dbf_ref):
        tri = _tri(False)
        carry = jnp.zeros((HEADS, 1), F32)
        tot = jnp.zeros((HEADS, 1), F32)
        for t in reversed(range(S // LANES)):
            sl = slice(t * LANES, (t + 1) * LANES)
            df = -d_ref[:, sl]
            c = lax.dot_general(df, tri, NN, preferred_element_type=F32,
                                precision=lax.Precision.HIGHEST) + carry
            carry = c[:, 0:1]
            z = z_ref[:, sl] + b_ref[...]
            dz = c * _sigmoid(-z)
            dz_ref[:, sl] = dz
            tot = tot + jnp.sum(dz, axis=1, keepdims=True)
        dbf_ref[...] = tot

    return pl.pallas_call(
        body, name="fox_gates_bwd",
        out_shape=[jax.ShapeDtypeStruct((HEADS, S), F32), jax.ShapeDtypeStruct((HEADS, 1), F32)],
        compiler_params=_params(),
    )(dbias, zt, bf)


def _mod_part(c_all, w_ada, b_cols):
    def body(c_ref, w_ref, b_ref, o_ref, s_ref):
        c = c_ref[...]
        sc = c * _sigmoid(c)
        s_ref[...] = sc
        o_ref[...] = lax.dot_general(sc, w_ref[...], NN, preferred_element_type=F32,
                                     precision=lax.Precision.HIGHEST) + b_ref[...]

    return pl.pallas_call(
        body, name="mod_part",
        out_shape=[jax.ShapeDtypeStruct((N_DEV, w_ada.shape[1]), F32), jax.ShapeDtypeStruct(c_all.shape, F32)],
        compiler_params=_params(),
    )(c_all, w_ada, b_cols)


def _w_ada_grad(sc_t, dm):
    def body(s_ref, d_ref, o_ref):
        acc = jnp.zeros(o_ref.shape, F32)
        for b in range(N_DEV):
            acc = acc + s_ref[:, b:b + 1] * d_ref[b:b + 1, :]
        o_ref[...] = acc

    return pl.pallas_call(
        body, name="w_ada_grad", out_shape=jax.ShapeDtypeStruct((sc_t.shape[0], dm.shape[1]), F32),
        compiler_params=_params(),
    )(sc_t, dm)


def _adamw(w, m, v, parts, name, own=None, slot=None):
    rows, cols = w.shape
    n = parts.shape[0]
    tr = rows if rows <= 512 else 256

    def body(*refs):
        if own is not None:
            s_ref, refs = refs[0], refs[1:]
            w_ref, m_ref, v_ref, p_ref, o_ref, g_out, d_out, m_out, v_out = refs
            terms = [jnp.where(s_ref[0] == kk, o_ref[0], p_ref[kk]) for kk in range(n)]
        else:
            w_ref, m_ref, v_ref, p_ref, g_out, d_out, m_out, v_out = refs
            terms = [p_ref[kk] for kk in range(n)]
        g = terms[0].astype(F32)
        for term in terms[1:]:
            g = g + term.astype(F32)
        g_out[...] = g
        d_out[...], m_out[...], v_out[...] = _adamw_math(w_ref[...], g, m_ref[...], v_ref[...])

    spec = pl.BlockSpec((tr, cols), lambda i, *_: (i, 0))
    in_specs = [spec, spec, spec, pl.BlockSpec((n, tr, cols), lambda i, *_: (0, i, 0))]
    out_shape = [jax.ShapeDtypeStruct((rows, cols), F32)] * 4
    if own is None:
        return pl.pallas_call(
            body, name=name, grid=(rows // tr,), in_specs=in_specs, out_specs=[spec] * 4, out_shape=out_shape,
            compiler_params=_params(("parallel",)),
        )(w, m, v, parts)
    in_specs.append(pl.BlockSpec((1, tr, cols), lambda i, s: (s[0], i, 0)))
    return pl.pallas_call(
        body, name=name, out_shape=out_shape, compiler_params=_params(("parallel",)),
        grid_spec=pltpu.PrefetchScalarGridSpec(num_scalar_prefetch=1, grid=(rows // tr,), in_specs=in_specs,
                                               out_specs=[spec] * 4),
    )(slot, w, m, v, parts, own)


def _adamw_math(w, g, m, v):
    mm = ADAM_B1 * m + (1.0 - ADAM_B1) * g
    vv = ADAM_B2 * v + (1.0 - ADAM_B2) * (g * g)
    m_hat = mm / (1.0 - ADAM_B1 ** ADAM_STEP)
    v_hat = vv / (1.0 - ADAM_B2 ** ADAM_STEP)
    return -ADAM_LR * (m_hat / (jnp.sqrt(v_hat) + ADAM_EPS) + ADAM_WD * w), mm, vv


def _adamw_rows(bundles, offsets, ws, ms, vs, err_off, err_width):
    k = len(ws)

    def body(*refs):
        b_ref = refs[0]
        w_refs, m_refs, v_refs = refs[1:1 + k], refs[1 + k:1 + 2 * k], refs[1 + 2 * k:1 + 3 * k]
        outs = refs[1 + 3 * k:]
        g_all = b_ref[0]
        for kk in range(1, N_DEV):
            g_all = g_all + b_ref[kk]
        for i in range(k):
            width = w_refs[i].shape[1]
            g = g_all[:, offsets[i]:offsets[i] + width]
            outs[4 * i][...] = g
            outs[4 * i + 1][...], outs[4 * i + 2][...], outs[4 * i + 3][...] = _adamw_math(
                w_refs[i][...], g, m_refs[i][...], v_refs[i][...])
        outs[4 * k][...] = g_all[:, err_off:err_off + err_width]

    out_shape = []
    for w_ in ws:
        out_shape += [jax.ShapeDtypeStruct(w_.shape, F32)] * 4
    out_shape.append(jax.ShapeDtypeStruct((1, err_width), F32))
    res = pl.pallas_call(body, name="adamw_rows", out_shape=out_shape, compiler_params=_params())(bundles, *ws, *ms, *vs)
    return [tuple(res[4 * i:4 * i + 4]) for i in range(k)], res[-1]


def _coords():
    return lax.axis_index("x"), lax.axis_index("y"), lax.axis_index("c")


def _flat(px, py, pc):
    return 4 * px + 2 * py + pc


def _all_gather(arrs, name):
    n = len(arrs)

    def body(*refs):
        ins, outs = refs[:n], refs[n:2 * n]
        send, recv, lsem = refs[2 * n:]
        x, y, c = _coords()
        me, sibling = (x, y, c), (x, y, 1 - c)
        chips = [(1 - x, y), (x, 1 - y), (1 - x, 1 - y)]

        def copy(a, kk, block, to, src=None):
            slot = outs[a].at[_flat(*block)]
            return pltpu.make_async_remote_copy(
                src_ref=slot if src is None else src, dst_ref=slot,
                send_sem=send.at[a, kk], recv_sem=recv.at[a, kk],
                device_id=to, device_id_type=MESH)

        mine = [pltpu.make_async_copy(ins[a], outs[a].at[_flat(*me)], lsem.at[a]) for a in range(n)]
        for cp in mine:
            cp.start()
        first = []
        for a in range(n):
            first.append(copy(a, 0, me, sibling, src=ins[a]))
            first += [copy(a, 1 + j, me, (*chip, c), src=ins[a]) for j, chip in enumerate(chips)]
        for cp in first:
            cp.start()
        passed = []
        for j, chip in enumerate(chips):
            for a in range(n):
                copy(a, 1 + j, (*chip, c), me).wait_recv()
                cp = copy(a, 4 + j, (*chip, c), sibling)
                cp.start()
                passed.append(cp)
        for a in range(n):
            copy(a, 0, sibling, me).wait_recv()
        for j, chip in enumerate(chips):
            for a in range(n):
                copy(a, 4 + j, (*chip, 1 - c), me).wait_recv()
        for cp in first + passed:
            cp.wait_send()
        for cp in mine:
            cp.wait()

    any_spec = pl.BlockSpec(memory_space=pl.ANY)
    return pl.pallas_call(
        body, name=name,
        in_specs=[any_spec] * n, out_specs=[any_spec] * n,
        out_shape=[jax.ShapeDtypeStruct((N_DEV,) + a.shape, a.dtype) for a in arrs],
        scratch_shapes=[pltpu.SemaphoreType.DMA((n, 7)), pltpu.SemaphoreType.DMA((n, 7)),
                        pltpu.SemaphoreType.DMA((n,))],
    )(*arrs)


def _peer_list():
    x, y, c = _coords()
    return [((1 - x if r & 4 else x), (1 - y if r & 2 else y), (1 - c if r & 1 else c)) for r in range(1, N_DEV)]


def _copy_plan(mode, src, land):
    x, y, c = _coords()
    me = _flat(x, y, c)
    if mode == "gather":
        return [(src, land.at[me], peer) for peer in _peer_list()]
    if mode == "exchange":
        return [(src.at[_flat(*peer)], land.at[me], peer) for peer in _peer_list()]
    if mode == "pair":
        return [(src.at[_flat(q // 2, q % 2, 1 - c)], land.at[q], (x, y, 1 - c)) for q in range(N_DEV // 2)]
    chips = [((1 - x if r & 2 else x), (1 - y if r & 1 else y)) for r in range(1, N_DEV // 2)]
    if mode == "chips":
        return [(src.at[2 * qx + qy], land.at[2 * x + y], (qx, qy, c)) for qx, qy in chips]
    if mode == "spread":
        return [(src, land.at[me], (x, y, 1 - c))] + [(src, land.at[me], (qx, qy, c)) for qx, qy in chips]
    assert mode == "forward"
    return [(land.at[_flat(qx, qy, c)], land.at[_flat(qx, qy, c)], (x, y, 1 - c)) for qx, qy in chips]


N_COPIES = dict(gather=N_DEV - 1, exchange=N_DEV - 1, pair=N_DEV // 2, chips=N_DEV // 2 - 1, spread=N_DEV // 2,
                forward=N_DEV // 2 - 1)


def _land_shape(mode, shape):
    return {"gather": (N_DEV,) + shape, "spread": (N_DEV,) + shape, "exchange": shape,
            "pair": (N_DEV // 2,) + shape[1:], "chips": shape}[mode]


HBM_SPEC = pl.BlockSpec(memory_space=pltpu.HBM)
SEM_SPEC = pl.BlockSpec(memory_space=pltpu.SEMAPHORE)
ANY_SPEC = pl.BlockSpec(memory_space=pl.ANY)
SIDE_EFFECT = pltpu.SideEffectType.DATAFLOW_SIDE_EFFECTING


def _async_start(groups, modes, after, name):
    modes = [modes] * len(groups) if isinstance(modes, str) else list(modes)
    arrs = [(a, m) for g, m in zip(groups, modes) for a in g]
    n = len(arrs)
    fresh = [i for i, (_, m) in enumerate(arrs) if m != "forward"]

    def body(*refs):
        srcs, new_lands = refs[:n], refs[n:n + len(fresh)]
        outs = refs[n + len(fresh) + 1:]
        lands = list(srcs)
        for k, i in enumerate(fresh):
            lands[i] = new_lands[k]
        for ai, (_, mode) in enumerate(arrs):
            for src_ref, dst_ref, peer in _copy_plan(mode, srcs[ai], lands[ai]):
                pltpu.make_async_remote_copy(src_ref=src_ref, dst_ref=dst_ref, send_sem=outs[2 * ai],
                                             recv_sem=outs[2 * ai + 1], device_id=peer, device_id_type=MESH).start()
        outs[-1][...] = jnp.zeros(outs[-1].shape, F32)

    land_shapes = [(_land_shape(arrs[i][1], arrs[i][0].shape), arrs[i][0].dtype) for i in fresh]
    n_buf = n + len(fresh)
    out_shape = [pltpu.SemaphoreType.DMA(())] * (2 * n)
    out_shape += [pltpu.HBM(a.shape, a.dtype) for a, _ in arrs]
    out_shape += [pltpu.HBM(shape, dt) for shape, dt in land_shapes]
    out_shape.append(jax.ShapeDtypeStruct((8, LANES), F32))
    res = pl.pallas_call(
        body, name=name, out_shape=tuple(out_shape),
        in_specs=[HBM_SPEC] * n_buf + [ANY_SPEC],
        out_specs=tuple([SEM_SPEC] * (2 * n) + [HBM_SPEC] * n_buf + [pl.BlockSpec(memory_space=pltpu.VMEM)]),
        input_output_aliases={i: 2 * n + i for i in range(n_buf)},
        compiler_params=pltpu.CompilerParams(has_side_effects=SIDE_EFFECT),
    )(*[pltpu.with_memory_space_constraint(a, pltpu.HBM) for a, _ in arrs],
      *[pltpu.with_memory_space_constraint(lax.empty(shape, dt), pltpu.HBM) for shape, dt in land_shapes],
      after)
    sems, thru = res[:2 * n], res[2 * n:-1]
    land_of = {i: thru[n + k] for k, i in enumerate(fresh)}
    states, idx = [], 0
    for g, mode in zip(groups, modes):
        ids = range(idx, idx + len(g))
        idx += len(g)
        states.append(([sems[2 * i] for i in ids], [sems[2 * i + 1] for i in ids],
                       None if mode == "forward" else [thru[i] for i in ids],
                       [land_of.get(i, thru[i]) for i in ids], mode))
    return states, res[-1]


def _async_wait(state, after, name):
    sends, recvs, srcs, lands, mode = state
    g = len(lands)
    bufs = (list(srcs) if srcs is not None else []) + list(lands)
    nb = len(bufs)

    def body(*refs):
        l_refs, sems = refs[nb - g:nb], refs[nb:nb + 2 * g]
        for ai in range(g):
            moved = l_refs[ai].at[pl.ds(0, N_COPIES[mode])]
            cp = pltpu.make_async_remote_copy(src_ref=moved, dst_ref=moved, send_sem=sems[ai], recv_sem=sems[g + ai],
                                              device_id=_coords(), device_id_type=MESH)
            cp.wait_send()
            cp.wait_recv()

    res = pl.pallas_call(
        body, name=name,
        out_shape=tuple(pltpu.HBM(a.shape, a.dtype) for a in bufs),
        in_specs=[HBM_SPEC] * nb + [SEM_SPEC] * (2 * g) + [ANY_SPEC],
        out_specs=tuple([HBM_SPEC] * nb),
        input_output_aliases={i: i for i in range(nb)},
        compiler_params=pltpu.CompilerParams(has_side_effects=SIDE_EFFECT),
    )(*bufs, *sends, *recvs, after)
    return (list(res[:nb - g]) if srcs is not None else None), list(res[nb - g:])


def _add_blocks(a, b):
    def body(a_ref, b_ref, o_ref):
        o_ref[...] = (a_ref[...].astype(F32) + b_ref[...].astype(F32)).astype(o_ref.dtype)

    spec = pl.BlockSpec((1,) + a.shape[1:], lambda i: (i, 0, 0))
    return pl.pallas_call(
        body, name="add_blocks", grid=(a.shape[0],), in_specs=[spec, spec], out_specs=spec,
        out_shape=jax.ShapeDtypeStruct(a.shape, a.dtype), compiler_params=_params(("parallel",)),
    )(a, b)


def _with_own(land, own, me):
    return lax.dynamic_update_index_in_dim(land, own, me, 0)


IN_SPLITS = (512, 512, 512, 8, 768, 256, 32, 1024, 1024)


def _from_shards(g, fn, out_widths, name, own=None, slot=None):
    _, k, n = g.shape
    tr = min(k, 256)

    def body(*refs):
        if own is not None:
            s_ref, g_ref, own_ref = refs[:3]
            cols = [jnp.where(s_ref[0] == j, own_ref[...], g_ref[j]) for j in range(N_DEV)]
        else:
            g_ref = refs[0]
            cols = [g_ref[j] for j in range(N_DEV)]
        for o_ref, val in zip(refs[-len(out_widths):], fn(jnp.concatenate(cols, axis=1))):
            o_ref[...] = val

    in_specs = [pl.BlockSpec((N_DEV, tr, n), lambda i, *_: (0, i, 0))]
    out_spec = [pl.BlockSpec((tr, wd), lambda i, *_: (i, 0)) for wd in out_widths]
    out_shape = [jax.ShapeDtypeStruct((k, wd), g.dtype) for wd in out_widths]
    if own is None:
        return pl.pallas_call(body, name=name, grid=(k // tr,), in_specs=in_specs, out_specs=out_spec,
                              out_shape=out_shape, compiler_params=_params(("parallel",)))(g)
    in_specs.append(pl.BlockSpec((tr, n), lambda i, *_: (i, 0)))
    return pl.pallas_call(
        body, name=name, out_shape=out_shape, compiler_params=_params(("parallel",)),
        grid_spec=pltpu.PrefetchScalarGridSpec(num_scalar_prefetch=1, grid=(k // tr,), in_specs=in_specs, out_specs=out_spec),
    )(slot, g, own)


def _unshard_cols(g, own=None, slot=None):
    return _from_shards(g, lambda full: (full,), [N_DEV * g.shape[2]], "unshard_cols_%d" % g.shape[2], own, slot)[0]


FFN_T = 256
FFN_SHARD = 2 * D_FF // N_DEV


def _unshard_ffn_in(g, own=None, slot=None):
    def pairs(full):
        parts = []
        for j in range(D_FF // FFN_T):
            parts += [full[:, j * FFN_T:(j + 1) * FFN_T], full[:, D_FF + j * FFN_T:D_FF + (j + 1) * FFN_T]]
        return (jnp.concatenate(parts, axis=1),)

    return _from_shards(g, pairs, [2 * D_FF], "unshard_ffn_in", own, slot)[0]


def _shard_ffn_in(w):
    tr = 256

    def body(w_ref, o_ref):
        x = w_ref[...]
        nb = D_FF // FFN_T
        full = jnp.concatenate([x[:, (2 * j + half) * FFN_T:(2 * j + half + 1) * FFN_T]
                                for half in range(2) for j in range(nb)], axis=1)
        for j in range(N_DEV):
            o_ref[j] = full[:, j * FFN_SHARD:(j + 1) * FFN_SHARD]

    return pl.pallas_call(
        body, name="shard_ffn_in", grid=(D // tr,),
        in_specs=[pl.BlockSpec((tr, 2 * D_FF), lambda i: (i, 0))],
        out_specs=pl.BlockSpec((N_DEV, tr, FFN_SHARD), lambda i: (0, i, 0)),
        out_shape=jax.ShapeDtypeStruct((N_DEV, D, FFN_SHARD), w.dtype),
        compiler_params=_params(("parallel",)),
    )(w)


def _shard_cols(w):
    k, n = w.shape[0], w.shape[1] // N_DEV
    tr = min(k, 256)

    def body(w_ref, o_ref):
        full = w_ref[...]
        for j in range(N_DEV):
            o_ref[j] = full[:, j * n:(j + 1) * n]

    return pl.pallas_call(
        body, name="shard_cols_%d" % n, grid=(k // tr,),
        in_specs=[pl.BlockSpec((tr, N_DEV * n), lambda i: (i, 0))],
        out_specs=pl.BlockSpec((N_DEV, tr, n), lambda i: (0, i, 0)),
        out_shape=jax.ShapeDtypeStruct((N_DEV, k, n), w.dtype),
        compiler_params=_params(("parallel",)),
    )(w)


IN_OFFS = tuple(sum(IN_SPLITS[:i]) for i in range(len(IN_SPLITS) + 1))
IN_SHARD = IN_OFFS[-1] // N_DEV
REGROUP_ROWS = 128


def _w_in_regroup(g, own=None, slot=None):
    def groups(full):
        fq, fk, fv, wf, cq, ckv, kr, gf, gm = [full[:, IN_OFFS[i]:IN_OFFS[i + 1]] for i in range(9)]
        rows = full.shape[0]
        w_a = jnp.concatenate([cq, ckv, gf, gm, wf, jnp.zeros((rows, 56), BF16), kr, jnp.zeros((rows, 32), BF16)], axis=1)
        return w_a, jnp.concatenate([fq, fk, fv], axis=1)

    return _from_shards(g, groups, [3200, 1536], "w_in_regroup", own, slot)


def _w_in_ungroup(da, db_):
    def body(a_ref, b_ref, o_ref):
        a = a_ref[...]
        full = jnp.concatenate([b_ref[...], a[:, 3072:3080], a[:, 0:768], a[:, 768:1024], a[:, 3136:3168],
                                a[:, 1024:3072]], axis=1)
        for j in range(N_DEV):
            o_ref[j] = full[:, j * IN_SHARD:(j + 1) * IN_SHARD]

    tr = REGROUP_ROWS
    return pl.pallas_call(
        body, name="w_in_ungroup", grid=(D // tr,),
        in_specs=[pl.BlockSpec((tr, 3200), lambda i: (i, 0)), pl.BlockSpec((tr, 1536), lambda i: (i, 0))],
        out_specs=pl.BlockSpec((N_DEV, tr, IN_SHARD), lambda i: (0, i, 0)),
        out_shape=jax.ShapeDtypeStruct((N_DEV, D, IN_SHARD), BF16),
        compiler_params=_params(("parallel",)),
    )(da, db_)


def _prepare_weights(g, own=None, slot=None):
    w = {}
    if own is not None:
        small = ("w_uq", "w_ukv", "w_out", "w_ffn_out")
        g = {n: (_with_own(a, own[n], slot[0]) if n in small else a) for n, a in g.items()}
    pick = (lambda n: (own[n], slot)) if own is not None else (lambda n: (None, None))
    if "w_in" in g:
        w["w_a"], w["w_b"] = _w_in_regroup(g["w_in"], *pick("w_in"))
    if "w_uq" in g:
        w_uq = g["w_uq"].reshape(Q_LORA, HEADS, 96)
        w["w_uq"] = jnp.pad(w_uq, ((0, 0), (0, 0), (0, 32))).reshape(Q_LORA, HEADS * LANES)
        ukv = g["w_ukv"]
        w["w_k"] = jnp.transpose(jnp.pad(ukv[:, :, :64], ((0, 0), (0, 0), (0, 64))), (1, 0, 2)).reshape(KV_LORA, HEADS * LANES)
        w["w_v"] = jnp.transpose(ukv[:, :, 64:], (1, 0, 2)).reshape(KV_LORA, HEADS * HEAD_DIM)
    if "w_out" in g:
        w["w_pf"] = _unshard_cols(g["w_proj_fox"], *pick("w_proj_fox"))
        w["w_pm"] = _unshard_cols(g["w_proj_mla"], *pick("w_proj_mla"))
        w["w_out"] = g["w_out"].reshape(D, D)
    if "w_ffn_in" in g:
        w["w_ffn_in"] = _unshard_ffn_in(g["w_ffn_in"], *pick("w_ffn_in"))
        w["w_ffn_out"] = g["w_ffn_out"].reshape(D_FF, D)
    return w


def _shard_grads(dw):
    out = {}
    if "w_a" in dw:
        out["w_in"] = _w_in_ungroup(dw["w_a"], dw["w_b"])
    if "w_uq" in dw:
        w_uq = dw["w_uq"].reshape(Q_LORA, HEADS, LANES)[:, :, :96].reshape(Q_LORA, Q_LORA)
        out["w_uq"] = w_uq.reshape(N_DEV, Q_LORA // N_DEV, Q_LORA)
        k_part = dw["w_k"].reshape(KV_LORA, HEADS, LANES)[:, :, :64]
        v_part = dw["w_v"].reshape(KV_LORA, HEADS, HEAD_DIM)
        out["w_ukv"] = jnp.transpose(jnp.concatenate([k_part, v_part], axis=2), (1, 0, 2))
    if "w_out" in dw:
        out["w_proj_fox"] = _shard_cols(dw["w_pf"])
        out["w_proj_mla"] = _shard_cols(dw["w_pm"])
        out["w_out"] = dw["w_out"].reshape(N_DEV, D // N_DEV, D)
    if "w_ffn_in" in dw:
        out["w_ffn_in"] = _shard_ffn_in(dw["w_ffn_in"])
        out["w_ffn_out"] = dw["w_ffn_out"].reshape(N_DEV, D_FF // N_DEV, D)
    return out


def _fwd_bwd(x, pos, mod, target, w, vec, wts, send, relay):
    shift_mix, scale_mix, gate_mix, shift_ffn, scale_ffn, gate_ffn = [mod[:, i * D:(i + 1) * D] for i in range(6)]
    g_pre_mix, g_post_mix, g_pre_ffn, g_post_ffn = vec["g_pre_mix"], vec["g_post_mix"], vec["g_pre_ffn"], vec["g_post_ffn"]
    g_q, g_kv = vec["g_q_lora"], vec["g_kv_lora"]

    inv_freq = 1.0 / (ROPE_THETA ** (jnp.arange(0, ROPE_DIM, 2, dtype=F32) / ROPE_DIM))
    invf = jnp.concatenate([jnp.zeros((64,), F32), inv_freq, inv_freq, jnp.zeros((32,), F32)]).reshape(1, LANES)
    ct, sa, sb = _rope_tables(pos, invf)

    def pre1(xv, g, sc, sh):
        return (xv * _rstd(xv) * g) * (1.0 + sc) + sh
    proj_a, h = _mm_epi(x, w["w_a"], "nn", 640, lambda r: ((r,), ()), "in_proj_a", 1024, outs=[(3200, 640, F32)],
                        pro=(pre1, [g_pre_mix, scale_mix, shift_mix], 0))
    qkv = _mm(h, w["w_b"], "nn", BF16, "in_proj_b")

    def lora_norm(cv, g):
        return cv * _rstd(cv) * g
    w = {**w, **wts("lora", qkv)}
    tables = [(ct, LANES), (sa, LANES), (sb, LANES)]

    def rope_q(qv, c_, a_, b_):
        return (jnp.concatenate([_rope(qv[:, hd * LANES:(hd + 1) * LANES], c_, a_, b_) for hd in range(HEADS)], axis=1),), ()
    q_m, cqn = _mm_epi(proj_a, w["w_uq"], "nn", D, rope_q, "mla_uq", 512, rows=tables, outs=[(D, D, BF16)],
                       pro=(lora_norm, [g_q], 0))

    def rope_k(kv, misc, c_, a_, b_):
        lane = lax.broadcasted_iota(jnp.int32, (1, LANES), 1)
        kpe = jnp.where((lane >= 64) & (lane < 96), _rope(misc, c_, a_, b_), 0.0)
        return (jnp.concatenate([kv[:, hd * LANES:(hd + 1) * LANES] + kpe for hd in range(HEADS)], axis=1),), ()
    k_m, ckvn = _mm_epi(proj_a, w["w_k"], "nn", D, rope_k, "mla_uk", 512, rows=[(proj_a, LANES, 24)] + tables,
                        outs=[(D, D, BF16)], pro=(lora_norm, [g_kv], Q_LORA // KV_LORA))
    v_m = _mm(ckvn, w["w_v"], "nn", BF16, "mla_uv")

    zt = jnp.transpose(lax.optimization_barrier(proj_a[:, 3072:3080]))
    bf = jnp.transpose(vec["b_forget"])
    neg_f = _fox_gates(zt, bf)
    bias = neg_f.reshape(HEADS, N_ATT, 1, ATT_T)
    o_b, lse_b = _attn_fwd(q_m, 0, k_m, 0, v_m, 0, 2 * LANES, 1.0 / math.sqrt(64 + ROPE_DIM), None, "mla_attn")
    bias = bias + wts("relay_proj", o_b)["tok"][0, 0]
    o_a, lse_a = _attn_fwd(qkv, 0, qkv, 4, qkv, 8, LANES, 1.0 / math.sqrt(HEAD_DIM), bias, "fox_attn")

    w = {**w, **wts("proj", o_a)}
    gate_mix = gate_mix + wts("relay_ffn", o_a)["tok"][0, 0]
    pa = _mm(o_a, w["w_pf"], "nn", BF16, "proj_fox")

    def merge(pb_, gf, gm, pa_):
        return (_sigmoid(gf) * pa_ + _sigmoid(gm) * pb_, pb_), ()
    merged, pb = _mm_epi(o_b, w["w_pm"], "nn", 512, merge, "proj_mla", 1024,
                         rows=[(proj_a, 512, 2), (proj_a, 512, 4), (pa, 512)], outs=[(D, 512, BF16), (D, 512, BF16)])
    def post1(yv, xv, gate, gpost, gpre, sc, sh):
        x1 = xv + gate * (yv * _rstd(yv) * gpost)
        return (x1, (x1 * _rstd(x1) * gpre) * (1.0 + sc) + sh, yv), ()
    x1, h2, y = _mm_epi(merged, w["w_out"], "nn", D, post1, "out_proj", 256, rows=[(x, D)],
                        vecs=[gate_mix, g_post_mix, g_pre_ffn, scale_ffn, shift_ffn],
                        outs=[(D, D, F32), (D, D, BF16), (D, D, F32)])
    w = {**w, **wts("ffn", h2)}

    def swiglu(r):
        g, u = r[:, :FFN_T], r[:, FFN_T:]
        return (g * _sigmoid(g) * u, r), ()
    act, gu = _mm_epi(h2, w["w_ffn_in"], "nn", 2 * FFN_T, swiglu, "ffn_in", 1024,
                      outs=[(D_FF, FFN_T, BF16), (2 * D_FF, 2 * FFN_T, BF16)])

    def head(y2v, x1v, tv, gate, gpost):
        r = _rstd(y2v)
        yn = y2v * r
        n2 = yn * gpost
        err = (x1v + gate * n2) - tv
        dx2 = err * (1.0 / D)
        dn2 = dx2 * gate
        dy2 = _norm_bwd(dn2 * gpost, yn, r)
        return (dx2, dy2), (_colsum(err * err), _colsum(dx2 * n2), _colsum(dn2 * yn))
    dx2, dy2, err_cols, d_gate_ffn, d_g_post_ffn = _mm_epi(
        act, w["w_ffn_out"], "nn", D, head, "ffn_out", 256, rows=[(x1, D), (target, D)], vecs=[gate_ffn, g_post_ffn],
        outs=[(D, D, F32), (D, D, BF16)], sums=[D, D, D])

    def swiglu_bwd(da, guv):
        g, u = guv[:, :FFN_T].astype(F32), guv[:, FFN_T:].astype(F32)
        sg = _sigmoid(g)
        return (jnp.concatenate([da * u * (sg * (1.0 + g * (1.0 - sg))), da * (g * sg)], axis=1),), ()
    (dgu,) = _mm_epi(dy2, w["w_ffn_out"], "nt", FFN_T, swiglu_bwd, "ffn_out_dx", 1024, rows=[(gu, 2 * FFN_T)],
                     outs=[(2 * D_FF, 2 * FFN_T, BF16)])
    dw = {"w_ffn_out": _mm(act, dy2, "tn", BF16, "ffn_out_dw")}
    dw["w_ffn_in"] = _mm(h2, dgu, "tn", BF16, "ffn_in_dw")
    gate_mix = gate_mix + send({n: dw.pop(n) for n in ("w_ffn_in", "w_ffn_out")})[0, 0]

    def mid(dh, x1v, dx2v, yv, gpre, sc, gate, gpost):
        r2 = _rstd(x1v)
        x1n = x1v * r2
        t = dh * x1n
        dx1 = dx2v + _norm_bwd(dh * (gpre * (1.0 + sc)), x1n, r2)
        ry = _rstd(yv)
        yn = yv * ry
        dn1 = dx1 * gate
        dy = _norm_bwd(dn1 * gpost, yn, ry)
        sums = (_colsum(dh), _colsum(t) * gpre, _colsum(t) * (1.0 + sc), _colsum(dx1 * (yn * gpost)), _colsum(dn1 * yn))
        return (dx1, dy), sums
    dx1, dy, d_shift_ffn, d_scale_ffn, d_g_pre_ffn, d_gate_mix, d_g_post_mix = _mm_epi(
        dgu, w["w_ffn_in"], "nt", D, mid, "ffn_in_dx", 256, rows=[(x1, D), (dx2, D), (y, D)],
        vecs=[g_pre_ffn, scale_ffn, gate_mix, g_post_mix], outs=[(D, D, F32), (D, D, BF16)], sums=[D] * 5)

    dw["w_out"] = _mm(merged, dy, "tn", BF16, "out_proj_dw")

    def merge_bwd(dm, gf, gm, pa_, pb_):
        sf, sm = _sigmoid(gf), _sigmoid(gm)
        return (dm * sf, dm * sm, dm * pa_ * (sf * (1.0 - sf)), dm * pb_ * (sm * (1.0 - sm))), ()
    dpa, dpb, dgf, dgm = _mm_epi(dy, w["w_out"], "nt", 512, merge_bwd, "out_proj_dx", 1024,
                                 rows=[(proj_a, 512, 2), (proj_a, 512, 4), (pa, 512), (pb, 512)],
                                 outs=[(D, 512, BF16)] * 4)
    do_a = _mm(dpa, w["w_pf"], "nt", BF16, "proj_fox_dx")
    do_b = _mm(dpb, w["w_pm"], "nt", BF16, "proj_mla_dx")
    dw["w_pf"] = _mm(o_a, dpa, "tn", BF16, "proj_fox_dw")
    dw["w_pm"] = _mm(o_b, dpb, "tn", BF16, "proj_mla_dw")
    bias = bias + send({n: dw.pop(n) for n in ("w_out", "w_pf", "w_pm")})[0, 0]

    sc_a, sc_b = 1.0 / math.sqrt(HEAD_DIM), 1.0 / math.sqrt(64 + ROPE_DIM)
    dq_a, dk_a, dv_a, dbias = _attn_grad(qkv, 0, qkv, 4, qkv, 8, do_a, lse_a, LANES, sc_a, bias, BF16, "fox_attn_bwd")
    dq_m, dk_m, dv_m = _attn_grad(q_m, 0, k_m, 0, v_m, 0, do_b, lse_b, 2 * LANES, sc_b, None, F32, "mla_attn_bwd")

    def mla_rope_bwd(dq, dk, c_, a_, b_):
        lane = lax.broadcasted_iota(jnp.int32, (1, LANES), 1)
        dqs = [_rope_t(dq[:, hd * LANES:(hd + 1) * LANES], c_, a_, b_) for hd in range(HEADS)]
        dkpe = dk[:, 0:LANES]
        for hd in range(1, HEADS):
            dkpe = dkpe + dk[:, hd * LANES:(hd + 1) * LANES]
        dkpe = jnp.where((lane >= 64) & (lane < 96), dkpe, 0.0)
        dkr = jnp.where((lane >= 64) & (lane < 96), _rope_t(dkpe, c_, a_, b_), 0.0)
        return (jnp.concatenate(dqs, axis=1), dk, dkr), ()
    dqb, dkb, dkr = _rowwise(mla_rope_bwd, [(dq_m, D, 0), (dk_m, D, 0), (ct, LANES, 0), (sa, LANES, 0), (sb, LANES, 0)],
                             [], [(D, BF16), (D, BF16), (LANES, F32)], [], "mla_rope_bwd")
    def lora_q_bwd(dq, cq, gq):
        rq = _rstd(cq)
        cqh = cq * rq
        return (_norm_bwd(dq * gq, cqh, rq),), (_colsum(dq * cqh),)
    dcq, d_g_q = _mm_epi(dqb, w["w_uq"], "nt", Q_LORA, lora_q_bwd, "mla_uq_dx", 512, rows=[(proj_a, Q_LORA, 0)],
                         vecs=[g_q], outs=[(Q_LORA, Q_LORA, BF16)], sums=[Q_LORA])
    dw["w_uq"] = _mm(cqn, dqb, "tn", BF16, "mla_uq_dw")

    def lora_kv_bwd(dv_part, dk_part, ckv, gkv):
        dkv = dv_part + dk_part
        rk = _rstd(ckv)
        ckh = ckv * rk
        return (_norm_bwd(dkv * gkv, ckh, rk),), (_colsum(dkv * ckh),)
    dckv, d_g_kv = _mm_epi(dv_m, w["w_v"], "nt", KV_LORA, lora_kv_bwd, "mla_uv_dx", 1024,
                           rows=[(_mm(dkb, w["w_k"], "nt", F32, "mla_uk_dx"), KV_LORA), (proj_a, KV_LORA, 3)],
                           vecs=[g_kv], outs=[(KV_LORA, KV_LORA, BF16)], sums=[KV_LORA])
    dw["w_k"] = _mm(ckvn, dkb, "tn", BF16, "mla_uk_dw")
    dw["w_v"] = _mm(ckvn, dv_m, "tn", BF16, "mla_uv_dw")

    dzt, d_bf = _fox_gates_bwd(dbias.reshape(HEADS, S), zt, bf)
    dmisc = (dkr + jnp.pad(jnp.transpose(dzt), ((0, 0), (0, LANES - HEADS)))).astype(BF16)
    dproj_a = jnp.concatenate([dcq, dckv, dgf, dgm, dmisc], axis=1)
    dqkv = jnp.concatenate([dq_a, dk_a, dv_a], axis=1)
    dw["w_a"] = _mm(h, dproj_a, "tn", BF16, "in_proj_a_dw")
    dw["w_b"] = _mm(h, dqkv, "tn", BF16, "in_proj_b_dw")
    tok = send(dw, True)
    dh_a = _mm(dproj_a, w["w_a"], "nt", F32, "in_proj_a_dx", dep=tok)
    g_pre_mix = g_pre_mix + relay(dh_a)[0, 0]

    def first(dh_b, dh_a, xv, dx1v, gpre, sc):
        dhv = dh_b + dh_a
        r = _rstd(xv)
        xn = xv * r
        t = dhv * xn
        dx = dx1v + _norm_bwd(dhv * (gpre * (1.0 + sc)), xn, r)
        return (dx,), (_colsum(dhv), _colsum(t) * gpre, _colsum(t) * (1.0 + sc))
    grad_x, d_shift_mix, d_scale_mix, d_g_pre_mix = _mm_epi(
        dqkv, w["w_b"], "nt", D, first, "in_proj_b_dx", 256,
        rows=[(dh_a, D), (x, D), (dx1, D)],
        vecs=[g_pre_mix, scale_mix], outs=[(D, D, F32)], sums=[D] * 3)

    dmod = jnp.concatenate([d_shift_mix, d_scale_mix, d_gate_mix, d_shift_ffn, d_scale_ffn, d_gate_ffn], axis=1)
    small = dict(dmod=dmod, g_pre_mix=d_g_pre_mix, g_post_mix=d_g_post_mix, g_pre_ffn=d_g_pre_ffn,
                 g_post_ffn=d_g_post_ffn, g_q_lora=d_g_q, g_kv_lora=d_g_kv,
                 b_forget=jnp.pad(jnp.transpose(d_bf), ((0, 0), (0, LANES - HEADS))), err=err_cols)
    return grad_x, small


SMALL_ORDER = ("dmod", "g_pre_mix", "g_post_mix", "g_pre_ffn", "g_post_ffn", "g_q_lora", "g_kv_lora", "b_forget", "err")
SMALL_PARAM = {"dmod": "b_ada"}
MATRICES = ("w_in", "w_uq", "w_ukv", "w_proj_fox", "w_proj_mla", "w_out", "w_ffn_in", "w_ffn_out")
WEIGHTS = ("w_ada", "b_ada", "g_pre_mix", "g_post_mix", "g_pre_ffn", "g_post_ffn", "w_in", "b_forget", "g_q_lora",
           "w_uq", "g_kv_lora", "w_ukv", "w_proj_fox", "w_proj_mla", "w_out", "w_ffn_in", "w_ffn_out")


def kernel(x, c, positions, w_ada, b_ada, g_pre_mix, g_post_mix, g_pre_ffn, g_post_ffn, w_in, b_forget, g_q_lora, w_uq, g_kv_lora, w_ukv, w_proj_fox, w_proj_mla, w_out, w_ffn_in, w_ffn_out, loss_target, m_w_ada, m_b_ada, m_g_pre_mix, m_g_post_mix, m_g_pre_ffn, m_g_post_ffn, m_w_in, m_b_forget, m_g_q_lora, m_w_uq, m_g_kv_lora, m_w_ukv, m_w_proj_fox, m_w_proj_mla, m_w_out, m_w_ffn_in, m_w_ffn_out, v_w_ada, v_b_ada, v_g_pre_mix, v_g_post_mix, v_g_pre_ffn, v_g_post_ffn, v_w_in, v_b_forget, v_g_q_lora, v_w_uq, v_g_kv_lora, v_w_ukv, v_w_proj_fox, v_w_proj_mla, v_w_out, v_w_ffn_in, v_w_ffn_out):
    prm = dict(w_ada=w_ada, b_ada=b_ada, g_pre_mix=g_pre_mix, g_post_mix=g_post_mix, g_pre_ffn=g_pre_ffn,
               g_post_ffn=g_post_ffn, w_in=w_in, b_forget=b_forget, g_q_lora=g_q_lora, w_uq=w_uq, g_kv_lora=g_kv_lora,
               w_ukv=w_ukv, w_proj_fox=w_proj_fox, w_proj_mla=w_proj_mla, w_out=w_out, w_ffn_in=w_ffn_in, w_ffn_out=w_ffn_out)
    mom = dict(w_ada=m_w_ada, b_ada=m_b_ada, g_pre_mix=m_g_pre_mix, g_post_mix=m_g_post_mix, g_pre_ffn=m_g_pre_ffn,
               g_post_ffn=m_g_post_ffn, w_in=m_w_in, b_forget=m_b_forget, g_q_lora=m_g_q_lora, w_uq=m_w_uq,
               g_kv_lora=m_g_kv_lora, w_ukv=m_w_ukv, w_proj_fox=m_w_proj_fox, w_proj_mla=m_w_proj_mla, w_out=m_w_out,
               w_ffn_in=m_w_ffn_in, w_ffn_out=m_w_ffn_out)
    var = dict(w_ada=v_w_ada, b_ada=v_b_ada, g_pre_mix=v_g_pre_mix, g_post_mix=v_g_post_mix, g_pre_ffn=v_g_pre_ffn,
               g_post_ffn=v_g_post_ffn, w_in=v_w_in, b_forget=v_b_forget, g_q_lora=v_g_q_lora, w_uq=v_w_uq,
               g_kv_lora=v_g_kv_lora, w_ukv=v_w_ukv, w_proj_fox=v_w_proj_fox, w_proj_mla=v_w_proj_mla, w_out=v_w_out,
               w_ffn_in=v_w_ffn_in, w_ffn_out=v_w_ffn_out)
    me = _flat(*_coords())
    slot = jnp.reshape(me, (1,)).astype(jnp.int32)

    own = {n: prm[n][0].astype(BF16) for n in MATRICES}
    no_dep = jnp.zeros((8, LANES), F32)
    (st_c, st_in), tok = _async_start([[c], [own["w_in"]]], ["gather", "spread"], no_dep, "gather_in_start")
    (c_own,), (c_land,) = _async_wait(st_c, tok, "gather_c_wait")
    c_all = _with_own(c_land, c_own, me).reshape(N_DEV, D)
    ada_cols = w_ada.shape[2]
    b_cols = lax.dynamic_slice(b_ada, (0, me * ada_cols), (1, ada_cols))
    mod_cols, silu_c = _mod_part(c_all, w_ada[0], b_cols)
    (mod_all,) = _all_gather([mod_cols], "gather_mod")

    (w_in_own,), (w_in_land,) = _async_wait(st_in, mod_all, "gather_in_wait")
    (st_in,), tok = _async_start([[w_in_land]], "forward", no_dep, "gather_in_forward")
    _, (w_in_land,) = _async_wait(st_in, tok, "gather_in_forward_wait")
    w = _prepare_weights({"w_in": w_in_land}, {"w_in": w_in_own}, slot)
    later = dict(lora=("w_uq", "w_ukv"), proj=("w_proj_fox", "w_proj_mla", "w_out"), ffn=("w_ffn_in", "w_ffn_out"))
    states, tok = _async_start([[own[n] for n in names] for names in later.values()], ["gather", "spread", "spread"],
                               w["w_b"], "gather_rest_start")
    gather_state = dict(zip(later, states))
    own_thru = {}

    def wts(group, after):
        if group.startswith("relay_"):
            name = group[len("relay_"):]
            own_thru[name], lands = _async_wait(gather_state[name], after, "gather_" + name + "_wait")
            (gather_state[name],), t = _async_start([lands], "forward", no_dep, "gather_" + name + "_forward")
            return {"tok": t}
        srcs, lands = _async_wait(gather_state[group], after, "gather_" + group + "_landed")
        srcs = own_thru.get(group, srcs)
        return _prepare_weights(dict(zip(later[group], lands)), dict(zip(later[group], srcs)), slot)

    sent, last = [], {}

    def send(grads, final=False):
        shards = _shard_grads(grads)
        names = list(shards)
        (state,), t = _async_start([[shards[n] for n in names]], "pair" if final else "exchange", no_dep,
                                   "exchange_" + names[0] + "_start")
        if final:
            last.update(names=names, state=state)
        else:
            sent.append((names, state))
        return t

    def relay(after):
        srcs, lands = _async_wait(last["state"], after, "exchange_pair_wait")
        sums = []
        for src, land in zip(srcs, lands):
            by_chip = src.reshape((N_DEV // 2, 2) + src.shape[1:])
            sums.append(_add_blocks(lax.dynamic_index_in_dim(by_chip, lax.axis_index("c"), 1, keepdims=False), land))
        (last["state"],), t = _async_start([sums], "chips", no_dep, "exchange_chips_start")
        return t

    mod = lax.dynamic_index_in_dim(mod_all, me, axis=1, keepdims=False).reshape(1, 6 * D) + tok[0, 0]

    vec = dict(g_pre_mix=g_pre_mix, g_post_mix=g_post_mix, g_pre_ffn=g_pre_ffn, g_post_ffn=g_post_ffn,
               g_q_lora=g_q_lora, g_kv_lora=g_kv_lora, b_forget=b_forget)
    pos = positions.astype(F32).reshape(S, 1)
    grad_x, small = _fwd_bwd(x[0], pos, mod, loss_target[0], w, vec, wts, send, relay)

    bundle = jnp.concatenate([small[n] for n in SMALL_ORDER], axis=1)
    (small_state,), tok = _async_start([[bundle]], "gather", jnp.zeros((8, LANES), F32), "gather_small_start")

    out = {}
    after = tok
    for names, state in sent:
        srcs, lands = _async_wait(state, after, "exchange_" + names[0] + "_wait")
        for n, src, land in zip(names, srcs, lands):
            out[n] = _adamw(prm[n][0], mom[n][0], var[n][0], land, "adamw_" + n, src, slot)
            after = out[n][0]
    srcs, lands = _async_wait(last["state"], after, "exchange_chips_wait")
    for n, src, land in zip(last["names"], srcs, lands):
        out[n] = _adamw(prm[n][0], mom[n][0], var[n][0], land, "adamw_" + n, src, slot // 2)
        after = out[n][0]

    (own_bundle,), (bundle_all,) = _async_wait(small_state, after, "gather_small_wait")
    bundle_all = _with_own(bundle_all, own_bundle, me)
    dmod_all = bundle_all[:, 0, :6 * D]
    dm_cols = lax.dynamic_slice(dmod_all, (0, me * ada_cols), (N_DEV, ada_cols))
    g_ada = _w_ada_grad(jnp.transpose(silu_c), dm_cols)
    out["w_ada"] = _adamw(w_ada[0], m_w_ada[0], v_w_ada[0], g_ada[None], "adamw_w_ada")

    offsets, off = {}, 0
    for n in SMALL_ORDER:
        offsets[n] = off
        off += small[n].shape[1]
    names = [SMALL_PARAM.get(n, n) for n in SMALL_ORDER if n != "err"]
    results, err = _adamw_rows(bundle_all, [offsets[n] for n in SMALL_ORDER if n != "err"],
                               [prm[n] for n in names], [mom[n] for n in names], [var[n] for n in names],
                               offsets["err"], D)
    out.update(zip(names, results))
    loss = 0.5 * jnp.sum(err) / D

    res = [loss, grad_x[None]]
    for kind in range(4):
        for n in WEIGHTS:
            t = out[n][kind]
            res.append(t[None] if prm[n].ndim == 3 else t)
    return tuple(res)
```

```python
import functools
import math

import jax
import jax.numpy as jnp
from jax import lax
from jax.experimental import pallas as pl
from jax.experimental.pallas import tpu as pltpu

F32 = jnp.float32
BF16 = jnp.bfloat16

N_DEV = 8
S = 2048
D = 1024
D_FF = 2816
HEADS = 8
HEAD_DIM = 64
Q_LORA = 768
KV_LORA = 256
ROPE_DIM = 32
ROPE_THETA = 10000.0
NORM_EPS = 1e-6
LANES = 128
VMEM_LIMIT = 56 * 1024 * 1024

ADAM_LR = 0.001
ADAM_B1 = 0.9
ADAM_B2 = 0.999
ADAM_EPS = 1e-08
ADAM_WD = 0.01
ADAM_STEP = 10

ATT_T = 256
LOG2E = 1.4426950408889634
N_ATT = S // ATT_T

NN = (((1,), (0,)), ((), ()))
NT = (((1,), (1,)), ((), ()))
TN = (((0,), (0,)), ((), ()))
MESH = pl.DeviceIdType.MESH


def _params(sem=None):
    return pltpu.CompilerParams(dimension_semantics=sem, vmem_limit_bytes=VMEM_LIMIT)


def _pick(n, cap):
    best = None
    for t in range(LANES, cap + 1, LANES):
        if n % t == 0:
            best = t
    return best if best is not None else n


def _mm(a, b, mode, out_dtype, name, acc=None, dep=None):
    if mode == "nn":
        (m, k), (k2, n), dn = a.shape, b.shape, NN
    elif mode == "nt":
        (m, k), (n, k2), dn = a.shape, b.shape, NT
    else:
        (k, m), (k2, n), dn = a.shape, b.shape, TN
    assert k == k2, (a.shape, b.shape, mode)
    tn = _pick(n, 640)
    tm = _pick(m, 1536)
    osz = jnp.dtype(out_dtype).itemsize

    def need(tm_):
        blk = tm_ * k * 2 + tn * k * 2 + tm_ * tn * osz + (tm_ * tn * 4 if acc is not None else 0)
        return 2 * blk + tm_ * tn * 4
    while need(tm) > 36 * 1024 * 1024 and tm % 256 == 0:
        tm //= 2

    def body(*refs):
        a_ref, b_ref, o_ref = refs[0], refs[1], refs[-1]
        r = lax.dot_general(a_ref[...], b_ref[...], dn, preferred_element_type=F32)
        if acc is not None:
            r = r + refs[2][...]
        o_ref[...] = r.astype(o_ref.dtype)

    if mode == "tn":
        a_spec = pl.BlockSpec((k, tm), lambda i, j: (0, i))
    else:
        a_spec = pl.BlockSpec((tm, k), lambda i, j: (i, 0))
    if mode == "nt":
        b_spec = pl.BlockSpec((tn, k), lambda i, j: (j, 0))
    else:
        b_spec = pl.BlockSpec((k, tn), lambda i, j: (0, j))
    o_spec = pl.BlockSpec((tm, tn), lambda i, j: (i, j))
    in_specs = [a_spec, b_spec] + ([o_spec] if acc is not None else [])
    in_specs += [pl.BlockSpec(memory_space=pl.ANY)] if dep is not None else []
    args = (a, b) + ((acc,) if acc is not None else ()) + ((dep,) if dep is not None else ())
    return pl.pallas_call(
        body, name=name, grid=(m // tm, n // tn),
        in_specs=in_specs, out_specs=o_spec,
        out_shape=jax.ShapeDtypeStruct((m, n), out_dtype),
        compiler_params=_params(("parallel", "parallel")),
    )(*args)


def _mm_epi(a, b, mode, tnb, epi, name, tm, rows=(), vecs=(), outs=(), sums=(), pro=None):
    m = a.shape[0]
    k, nb = (b.shape if mode == "nn" else b.shape[::-1])
    dn = NN if mode == "nn" else NT
    pro_fn, pro_vecs, a_off = pro if pro is not None else (None, (), 0)
    n_in = 2 + len(rows) + len(vecs)
    n_all = n_in + len(pro_vecs)

    def body(*refs):
        if pro is not None:
            a_out, a_scr = refs[-2:]
            refs = refs[:-2]

            @pl.when(pl.program_id(1) == 0)
            def _():
                a_scr[...] = pro_fn(refs[0][...], *[x[...] for x in refs[n_in:n_all]]).astype(BF16)
                a_out[...] = a_scr[...]
            lhs = a_scr[...]
        else:
            lhs = refs[0][...]
        r = lax.dot_general(lhs, refs[1][...], dn, preferred_element_type=F32)
        o_vals, s_vals = epi(r, *[x[...] for x in refs[2:n_in]])
        o_refs = refs[n_all:n_all + len(outs)]
        s_refs = refs[n_all + len(outs):]
        assert len(o_vals) == len(o_refs) and len(s_vals) == len(s_refs)
        for o_ref, val in zip(o_refs, o_vals):
            o_ref[...] = val.astype(o_ref.dtype)
        if sums:
            @pl.when((pl.program_id(0) == 0) & (pl.program_id(1) == 0))
            def _():
                for s_ref in s_refs:
                    s_ref[...] = jnp.zeros(s_ref.shape, F32)
            for s_ref, val in zip(s_refs, s_vals):
                s_ref[...] += val

    b_spec = pl.BlockSpec((k, tnb), lambda i, j: (0, j)) if mode == "nn" else pl.BlockSpec((tnb, k), lambda i, j: (j, 0))
    in_specs = [pl.BlockSpec((tm, k), lambda i, j: (i, a_off)), b_spec]
    rows = [tuple(r) + (0,) * (3 - len(r)) for r in rows]
    in_specs += [pl.BlockSpec((tm, w), functools.partial(lambda i, j, off: (i, j + off), off=off)) for _, w, off in rows]
    in_specs += [pl.BlockSpec(v.shape, lambda i, j: (0, 0)) for v in list(vecs) + list(pro_vecs)]
    out_specs = [pl.BlockSpec((tm, w), lambda i, j: (i, j)) for _, w, _ in outs]
    out_specs += [pl.BlockSpec((1, w), lambda i, j: (0, 0)) for w in sums]
    out_shape = [jax.ShapeDtypeStruct((m, full), dt) for full, _, dt in outs]
    out_shape += [jax.ShapeDtypeStruct((1, w), F32) for w in sums]
    if pro is not None:
        out_specs.append(pl.BlockSpec((tm, k), lambda i, j: (i, 0)))
        out_shape.append(jax.ShapeDtypeStruct((m, k), BF16))
    return pl.pallas_call(
        body, name=name, grid=(m // tm, nb // tnb),
        in_specs=in_specs, out_specs=out_specs, out_shape=out_shape,
        scratch_shapes=[pltpu.VMEM((tm, k), BF16)] if pro is not None else [],
        compiler_params=_params(("arbitrary", "arbitrary") if sums else ("parallel", "arbitrary" if pro is not None else "parallel")),
    )(a, b, *[r[0] for r in rows], *vecs, *pro_vecs)


def _rowwise(fn, row_ins, vec_ins, row_outs, sum_outs, name, tm=256):
    n_in = len(row_ins) + len(vec_ins)
    n_o = len(row_outs)
    rows = row_ins[0][0].shape[0]

    def body(*refs):
        vals = [r[...] for r in refs[:n_in]]
        outs = refs[n_in:]
        ro, so = fn(*vals)
        assert len(ro) == n_o and len(so) == len(sum_outs)
        for r, v in zip(outs[:n_o], ro):
            r[...] = v.astype(r.dtype)
        if sum_outs:
            @pl.when(pl.program_id(0) == 0)
            def _():
                for r in outs[n_o:]:
                    r[...] = jnp.zeros(r.shape, F32)
            for r, v in zip(outs[n_o:], so):
                r[...] += v

    in_specs = [pl.BlockSpec((tm, w), functools.partial(lambda i, b: (i, b), b=b)) for _, w, b in row_ins]
    in_specs += [pl.BlockSpec(v.shape, lambda i: (0, 0)) for v in vec_ins]
    out_specs = [pl.BlockSpec((tm, w), lambda i: (i, 0)) for w, _ in row_outs]
    out_specs += [pl.BlockSpec((1, w), lambda i: (0, 0)) for w in sum_outs]
    out_shape = [jax.ShapeDtypeStruct((rows, w), dt) for w, dt in row_outs]
    out_shape += [jax.ShapeDtypeStruct((1, w), F32) for w in sum_outs]
    return pl.pallas_call(
        body, name=name, grid=(rows // tm,),
        in_specs=in_specs, out_specs=out_specs, out_shape=out_shape,
        compiler_params=_params(("arbitrary",)),
    )(*[a for a, _, _ in row_ins], *vec_ins)


def _sigmoid(x):
    return 1.0 / (1.0 + jnp.exp(-x))


def _rstd(x):
    return lax.rsqrt(jnp.mean(x * x, axis=-1, keepdims=True) + NORM_EPS)


def _norm_bwd(dyn, xn, r):
    return r * (dyn - xn * jnp.mean(dyn * xn, axis=-1, keepdims=True))


def _colsum(x):
    return jnp.sum(x, axis=0, keepdims=True)


def _rope_tables(pos, invf):
    def fn(p, f):
        lane = lax.broadcasted_iota(jnp.int32, (1, LANES), 1)
        ang = p * f
        cs, sn = jnp.cos(ang), jnp.sin(ang)
        rot = (lane >= 64) & (lane < 96)
        ct = jnp.where(lane < 64, 1.0, jnp.where(rot, cs, 0.0))
        sa = jnp.where((lane >= 64) & (lane < 80), -sn, 0.0)
        sb = jnp.where((lane >= 80) & (lane < 96), sn, 0.0)
        return (ct, sa, sb), ()
    return _rowwise(fn, [(pos, 1, 0)], [invf], [(LANES, F32)] * 3, [], "rope_tables")


def _rope(x, ct, sa, sb):
    return x * ct + pltpu.roll(x, LANES - 16, 1) * sa + pltpu.roll(x, 16, 1) * sb


def _rope_t(x, ct, sa, sb):
    return x * ct - pltpu.roll(x, LANES - 16, 1) * sa - pltpu.roll(x, 16, 1) * sb


def _head_mask(width, hh):
    lane = lax.broadcasted_iota(jnp.int32, (1, width), 1)
    half = width // 2
    return (lane >= hh * half) & (lane < (hh + 1) * half)


ATT_PP = 2
ATT_CHAINS = [(a, hh) for a in range(ATT_PP) for hh in range(2)]
ATT_G = HEADS // (2 * ATT_PP)


def _pair(ref_or_val, a, width, rows=slice(None)):
    return ref_or_val[rows, a * width:(a + 1) * width]


def _attn_fwd(q, qo, k, ko, v, vo, dkp, scale, bias, name):
    T = ATT_T
    assert qo % ATT_PP == 0 and ko % ATT_PP == 0 and vo % ATT_PP == 0
    qo, ko, vo = qo // ATT_PP, ko // ATT_PP, vo // ATT_PP

    def body(*refs):
        if bias is not None:
            q_ref, k_ref, v_ref, b_ref, o_ref, lse_ref, s_scr = refs
        else:
            q_ref, k_ref, v_ref, o_ref, lse_ref, s_scr = refs
        i = pl.program_id(1)
        row = lax.broadcasted_iota(jnp.int32, (T, T), 0)
        col = lax.broadcasted_iota(jnp.int32, (T, T), 1)
        qms = []
        for a, hh in ATT_CHAINS:
            qb = _pair(q_ref, a, dkp)
            qms.append(jnp.where(_head_mask(dkp, hh), qb, jnp.zeros_like(qb)))

        def fold(t):
            return [t[:, c * LANES:(c + 1) * LANES] for c in range(T // LANES)]

        def run(nt):
            mls = [jnp.full((T, LANES), -jnp.inf, F32) for _ in ATT_CHAINS]
            for j in range(nt):
                ks = slice(j * T, (j + 1) * T)
                for ci, (a, hh) in enumerate(ATT_CHAINS):
                    s = lax.dot_general(qms[ci], _pair(k_ref, a, dkp, ks), NT, preferred_element_type=F32) * (scale * LOG2E)
                    if bias is not None:
                        s = s + b_ref[2 * a + hh, j] * LOG2E
                    if j == nt - 1:
                        s = jnp.where(row >= col, s, -jnp.inf)
                    s_scr[ci, j] = s
                    for part in fold(s):
                        mls[ci] = jnp.maximum(mls[ci], part)
            ms = [jnp.max(ml, axis=1, keepdims=True) for ml in mls]
            mbs = [jnp.broadcast_to(m, (T, LANES)) for m in ms]
            for a in range(ATT_PP):
                ls = [jnp.zeros((T, LANES), F32) for _ in range(2)]
                ps, vms = [], []
                for j in range(nt):
                    vb = _pair(v_ref, a, LANES, slice(j * T, (j + 1) * T))
                    for hh in range(2):
                        parts = [jnp.exp2(part - mbs[2 * a + hh]) for part in fold(s_scr[2 * a + hh, j])]
                        for part in parts:
                            ls[hh] = ls[hh] + part
                        ps.append(jnp.concatenate(parts, axis=1).astype(BF16))
                        vms.append(jnp.where(_head_mask(LANES, hh), vb, jnp.zeros_like(vb)))
                acc = lax.dot_general(jnp.concatenate(ps, axis=1), jnp.concatenate(vms, axis=0), NN,
                                      preferred_element_type=F32)
                l0, l1 = [jnp.sum(l, axis=1, keepdims=True) for l in ls]
                lse_ref[2 * a] = ms[2 * a] + jnp.log2(l0)
                lse_ref[2 * a + 1] = ms[2 * a + 1] + jnp.log2(l1)
                inv = jnp.where(_head_mask(LANES, 0), 1.0 / l0, 1.0 / l1)
                o_ref[:, a * LANES:(a + 1) * LANES] = (acc * inv).astype(o_ref.dtype)

        for nt in range(1, N_ATT + 1):
            pl.when(i == nt - 1)(functools.partial(run, nt))

    in_specs = [
        pl.BlockSpec((T, ATT_PP * dkp), lambda g, i: (i, qo + g)),
        pl.BlockSpec((S, ATT_PP * dkp), lambda g, i: (0, ko + g)),
        pl.BlockSpec((S, ATT_PP * LANES), lambda g, i: (0, vo + g)),
    ]
    args = [q, k, v]
    if bias is not None:
        in_specs.append(pl.BlockSpec((2 * ATT_PP, N_ATT, 1, T), lambda g, i: (g, 0, 0, 0)))
        args.append(bias)
    return pl.pallas_call(
        body, name=name, grid=(ATT_G, N_ATT),
        in_specs=in_specs,
        out_specs=[pl.BlockSpec((T, ATT_PP * LANES), lambda g, i: (i, g)),
                   pl.BlockSpec((2 * ATT_PP, T, 1), lambda g, i: (g, i, 0))],
        out_shape=[jax.ShapeDtypeStruct((S, HEADS * HEAD_DIM), BF16),
                   jax.ShapeDtypeStruct((HEADS, S, 1), F32)],
        scratch_shapes=[pltpu.VMEM((len(ATT_CHAINS), N_ATT, T, T), F32)],
        compiler_params=_params(("parallel", "arbitrary")),
    )(*args)


def _attn_grad(q, qo, k, ko, v, vo, do, lse, dkp, scale, bias, qk_dtype, name):
    T = ATT_T
    has_b = bias is not None
    qo, ko, vo = qo // ATT_PP, ko // ATT_PP, vo // ATT_PP
    n_ch = len(ATT_CHAINS)

    def body(*refs):
        q_ref, k_ref, v_ref, do_ref, lse_ref = refs[:5]
        refs = refs[5:]
        if has_b:
            b_ref, refs = refs[0], refs[1:]
        dq_ref, dk_ref, dv_ref = refs[:3]
        refs = refs[3:]
        if has_b:
            db_ref, refs = refs[0], refs[1:]
        p_scr, dp_scr, dk_acc, dv_acc = refs[:4]
        db_acc = refs[4] if has_b else None
        i = pl.program_id(1)

        @pl.when(i == 0)
        def _():
            dk_acc[...] = jnp.zeros(dk_acc.shape, F32)
            dv_acc[...] = jnp.zeros(dv_acc.shape, F32)
            if has_b:
                db_acc[...] = jnp.zeros(db_acc.shape, F32)

        row = lax.broadcasted_iota(jnp.int32, (T, T), 0)
        col = lax.broadcasted_iota(jnp.int32, (T, T), 1)

        def fold(t):
            return [t[:, c * LANES:(c + 1) * LANES] for c in range(T // LANES)]

        qms, doms, lses = [], [], []
        for a, hh in ATT_CHAINS:
            qb, dob = _pair(q_ref, a, dkp), _pair(do_ref, a, LANES)
            qms.append(jnp.where(_head_mask(dkp, hh), qb, jnp.zeros_like(qb)))
            doms.append(jnp.where(_head_mask(LANES, hh), dob, jnp.zeros_like(dob)))
            lses.append(lse_ref[2 * a + hh])

        def run(nt):
            dls = [jnp.zeros((T, LANES), F32) for _ in ATT_CHAINS]
            for j in range(nt):
                ks = slice(j * T, (j + 1) * T)
                for ci, (a, hh) in enumerate(ATT_CHAINS):
                    s = lax.dot_general(qms[ci], _pair(k_ref, a, dkp, ks), NT, preferred_element_type=F32) * (scale * LOG2E)
                    if has_b:
                        s = s + b_ref[ci, j] * LOG2E
                    s = s - lses[ci]
                    if j == nt - 1:
                        s = jnp.where(row >= col, s, -jnp.inf)
                    p = jnp.exp2(s)
                    dp = lax.dot_general(doms[ci], _pair(v_ref, a, LANES, ks), NT, preferred_element_type=F32)
                    p_scr[ci, j] = p
                    dp_scr[ci, j] = dp
                    for part in fold(p * dp):
                        dls[ci] = dls[ci] + part
            deltas = [jnp.broadcast_to(jnp.sum(dl, axis=1, keepdims=True), (T, LANES)) for dl in dls]
            for a in range(ATT_PP):
                ds_all, km_all = [], []
                qm2t = jnp.transpose(jnp.concatenate([qms[2 * a], qms[2 * a + 1]], axis=0))
                dom2t = jnp.transpose(jnp.concatenate([doms[2 * a], doms[2 * a + 1]], axis=0))
                for j in range(nt):
                    ks = slice(j * T, (j + 1) * T)
                    kb = _pair(k_ref, a, dkp, ks)
                    p2, ds2 = [], []
                    for hh in range(2):
                        ci = 2 * a + hh
                        p = p_scr[ci, j]
                        ds = jnp.concatenate([pp * (dd - deltas[ci]) for pp, dd in zip(fold(p), fold(dp_scr[ci, j]))], axis=1)
                        if has_b:
                            db_acc[ci, j] += jnp.sum(ds, axis=0, keepdims=True)
                        p2.append(p.astype(BF16))
                        ds2.append((ds * scale).astype(BF16))
                        km_all.append(jnp.where(_head_mask(dkp, hh), kb, jnp.zeros_like(kb)))
                    dv_acc[a * LANES:(a + 1) * LANES, ks] += lax.dot_general(
                        dom2t, jnp.concatenate(p2, axis=0), NN, preferred_element_type=F32)
                    dk_acc[a * dkp:(a + 1) * dkp, ks] += lax.dot_general(
                        qm2t, jnp.concatenate(ds2, axis=0), NN, preferred_element_type=F32)
                    ds_all += ds2
                dq = lax.dot_general(jnp.concatenate(ds_all, axis=1), jnp.concatenate(km_all, axis=0), NN,
                                     preferred_element_type=F32)
                dq_ref[:, a * dkp:(a + 1) * dkp] = dq.astype(dq_ref.dtype)

        for nt in range(1, N_ATT + 1):
            pl.when(i == nt - 1)(functools.partial(run, nt))

        @pl.when(i == N_ATT - 1)
        def _():
            dk_ref[...] = jnp.transpose(dk_acc[...]).astype(dk_ref.dtype)
            dv_ref[...] = jnp.transpose(dv_acc[...]).astype(dv_ref.dtype)
            if has_b:
                db_ref[...] = db_acc[...]

    in_specs = [
        pl.BlockSpec((T, ATT_PP * dkp), lambda g, i: (i, qo + g)),
        pl.BlockSpec((S, ATT_PP * dkp), lambda g, i: (0, ko + g)),
        pl.BlockSpec((S, ATT_PP * LANES), lambda g, i: (0, vo + g)),
        pl.BlockSpec((T, ATT_PP * LANES), lambda g, i: (i, g)),
        pl.BlockSpec((2 * ATT_PP, T, 1), lambda g, i: (g, i, 0)),
    ]
    args = [q, k, v, do, lse]
    out_specs = [
        pl.BlockSpec((T, ATT_PP * dkp), lambda g, i: (i, g)),
        pl.BlockSpec((S, ATT_PP * dkp), lambda g, i: (0, g)),
        pl.BlockSpec((S, ATT_PP * LANES), lambda g, i: (0, g)),
    ]
    width = (HEADS // 2) * dkp
    out_shape = [
        jax.ShapeDtypeStruct((S, width), qk_dtype),
        jax.ShapeDtypeStruct((S, width), qk_dtype),
        jax.ShapeDtypeStruct((S, HEADS * HEAD_DIM), BF16),
    ]
    scratch = [pltpu.VMEM((n_ch, N_ATT, T, T), F32), pltpu.VMEM((n_ch, N_ATT, T, T), F32),
               pltpu.VMEM((ATT_PP * dkp, S), F32), pltpu.VMEM((ATT_PP * LANES, S), F32)]
    if has_b:
        bspec = pl.BlockSpec((2 * ATT_PP, N_ATT, 1, T), lambda g, i: (g, 0, 0, 0))
        in_specs.append(bspec)
        args.append(bias)
        out_specs.append(bspec)
        out_shape.append(jax.ShapeDtypeStruct((HEADS, N_ATT, 1, T), F32))
        scratch.append(pltpu.VMEM((2 * ATT_PP, N_ATT, 1, T), F32))
    return pl.pallas_call(
        body, name=name, grid=(ATT_G, N_ATT),
        in_specs=in_specs, out_specs=out_specs, out_shape=out_shape, scratch_shapes=scratch,
        compiler_params=_params(("parallel", "arbitrary")),
    )(*args)


def _tri(upper):
    a = lax.broadcasted_iota(jnp.int32, (LANES, LANES), 0)
    b = lax.broadcasted_iota(jnp.int32, (LANES, LANES), 1)
    return jnp.where(a <= b if upper else a >= b, 1.0, 0.0).astype(F32)


def _fox_gates(proj, blk, bf):
    def body(m_ref, b_ref, z_out, o_ref):
        tri = _tri(True)
        carry = jnp.zeros((HEADS, 1), F32)
        for t in range(S // LANES):
            sl = slice(t * LANES, (t + 1) * LANES)
            zt = jnp.transpose(m_ref[sl, :])[:HEADS]
            z_out[:, sl] = zt
            z = zt + b_ref[...]
            logf = jnp.minimum(z, 0.0) - jnp.log(1.0 + jnp.exp(-jnp.abs(z)))
            c = lax.dot_general(logf, tri, NN, preferred_element_type=F32,
                                precision=lax.Precision.HIGHEST) + carry
            o_ref[:, sl] = -c
            carry = c[:, LANES - 1:LANES]

    return pl.pallas_call(
        body, name="fox_gates", grid=(1,),
        in_specs=[pl.BlockSpec((S, LANES), lambda i: (0, blk)), pl.BlockSpec(bf.shape, lambda i: (0, 0))],
        out_specs=[pl.BlockSpec((HEADS, S), lambda i: (0, 0))] * 2,
        out_shape=[jax.ShapeDtypeStruct((HEADS, S), F32)] * 2,
        compiler_params=_params(("arbitrary",)),
    )(proj, bf)


def _fox_gates_bwd(dbias, zt, bf):
    def body(d_ref, z_ref, b_ref, dz_ref, dbf_ref):
        tri = _tri(False)
        carry = jnp.zeros((HEADS, 1), F32)
        tot = jnp.zeros((HEADS, 1), F32)
        for t in reversed(range(S // LANES)):
            sl = slice(t * LANES, (t + 1) * LANES)
            df = -d_ref[:, sl]
            c = lax.dot_general(df, tri, NN, preferred_element_type=F32,
                                precision=lax.Precision.HIGHEST) + carry
            carry = c[:, 0:1]
            z = z_ref[:, sl] + b_ref[...]
            dz = c * _sigmoid(-z)
            dz_ref[:, sl] = dz
            tot = tot + jnp.sum(dz, axis=1, keepdims=True)
        dbf_ref[...] = tot

    return pl.pallas_call(
        body, name="fox_gates_bwd",
        out_shape=[jax.ShapeDtypeStruct((HEADS, S), F32), jax.ShapeDtypeStruct((HEADS, 1), F32)],
        compiler_params=_params(),
    )(dbias, zt, bf)


def _mod_part(c_all, w_ada, b_cols):
    def body(c_ref, w_ref, b_ref, o_ref, s_ref):
        c = c_ref[...]
        sc = c * _sigmoid(c)
        s_ref[...] = sc
        o_ref[...] = lax.dot_general(sc, w_ref[...], NN, preferred_element_type=F32,
                                     precision=lax.Precision.HIGHEST) + b_ref[...]

    return pl.pallas_call(
        body, name="mod_part",
        out_shape=[jax.ShapeDtypeStruct((N_DEV, w_ada.shape[1]), F32), jax.ShapeDtypeStruct(c_all.shape, F32)],
        compiler_params=_params(),
    )(c_all, w_ada, b_cols)


def _w_ada_grad(sc_t, dm):
    def body(s_ref, d_ref, o_ref):
        acc = jnp.zeros(o_ref.shape, F32)
        for b in range(N_DEV):
            acc = acc + s_ref[:, b:b + 1] * d_ref[b:b + 1, :]
        o_ref[...] = acc

    return pl.pallas_call(
        body, name="w_ada_grad", out_shape=jax.ShapeDtypeStruct((sc_t.shape[0], dm.shape[1]), F32),
        compiler_params=_params(),
    )(sc_t, dm)


def _adamw(w, m, v, parts, name, own=None, slot=None):
    rows, cols = w.shape
    n = parts.shape[0]
    tr = rows if rows <= 512 else 256

    def body(*refs):
        if own is not None:
            s_ref, refs = refs[0], refs[1:]
            w_ref, m_ref, v_ref, p_ref, o_ref, g_out, d_out, m_out, v_out = refs
            terms = [jnp.where(s_ref[0] == kk, o_ref[0], p_ref[kk]) for kk in range(n)]
        else:
            w_ref, m_ref, v_ref, p_ref, g_out, d_out, m_out, v_out = refs
            terms = [p_ref[kk] for kk in range(n)]
        g = terms[0].astype(F32)
        for term in terms[1:]:
            g = g + term.astype(F32)
        g_out[...] = g
        d_out[...], m_out[...], v_out[...] = _adamw_math(w_ref[...], g, m_ref[...], v_ref[...])

    spec = pl.BlockSpec((tr, cols), lambda i, *_: (i, 0))
    in_specs = [spec, spec, spec, pl.BlockSpec((n, tr, cols), lambda i, *_: (0, i, 0))]
    out_shape = [jax.ShapeDtypeStruct((rows, cols), F32)] * 4
    if own is None:
        return pl.pallas_call(
            body, name=name, grid=(rows // tr,), in_specs=in_specs, out_specs=[spec] * 4, out_shape=out_shape,
            compiler_params=_params(("parallel",)),
        )(w, m, v, parts)
    in_specs.append(pl.BlockSpec((1, tr, cols), lambda i, s: (s[0], i, 0)))
    return pl.pallas_call(
        body, name=name, out_shape=out_shape, compiler_params=_params(("parallel",)),
        grid_spec=pltpu.PrefetchScalarGridSpec(num_scalar_prefetch=1, grid=(rows // tr,), in_specs=in_specs,
                                               out_specs=[spec] * 4),
    )(slot, w, m, v, parts, own)


def _adamw_math(w, g, m, v):
    mm = ADAM_B1 * m + (1.0 - ADAM_B1) * g
    vv = ADAM_B2 * v + (1.0 - ADAM_B2) * (g * g)
    m_hat = mm / (1.0 - ADAM_B1 ** ADAM_STEP)
    v_hat = vv / (1.0 - ADAM_B2 ** ADAM_STEP)
    return -ADAM_LR * (m_hat / (jnp.sqrt(v_hat) + ADAM_EPS) + ADAM_WD * w), mm, vv


def _adamw_rows(bundles, offsets, ws, ms, vs, err_off, err_width):
    k = len(ws)

    def body(*refs):
        b_ref = refs[0]
        w_refs, m_refs, v_refs = refs[1:1 + k], refs[1 + k:1 + 2 * k], refs[1 + 2 * k:1 + 3 * k]
        outs = refs[1 + 3 * k:]
        g_all = b_ref[0]
        for kk in range(1, N_DEV):
            g_all = g_all + b_ref[kk]
        for i in range(k):
            width = w_refs[i].shape[1]
            g = g_all[:, offsets[i]:offsets[i] + width]
            outs[4 * i][...] = g
            outs[4 * i + 1][...], outs[4 * i + 2][...], outs[4 * i + 3][...] = _adamw_math(
                w_refs[i][...], g, m_refs[i][...], v_refs[i][...])
        outs[4 * k][...] = g_all[:, err_off:err_off + err_width]

    out_shape = []
    for w_ in ws:
        out_shape += [jax.ShapeDtypeStruct(w_.shape, F32)] * 4
    out_shape.append(jax.ShapeDtypeStruct((1, err_width), F32))
    res = pl.pallas_call(body, name="adamw_rows", out_shape=out_shape, compiler_params=_params())(bundles, *ws, *ms, *vs)
    return [tuple(res[4 * i:4 * i + 4]) for i in range(k)], res[-1]


def _coords():
    return lax.axis_index("x"), lax.axis_index("y"), lax.axis_index("c")


def _flat(px, py, pc):
    return 4 * px + 2 * py + pc


def _all_gather(arrs, name):
    n = len(arrs)

    def body(*refs):
        ins, outs = refs[:n], refs[n:2 * n]
        send, recv, lsem = refs[2 * n:]
        x, y, c = _coords()
        me, sibling = (x, y, c), (x, y, 1 - c)
        chips = [(1 - x, y), (x, 1 - y), (1 - x, 1 - y)]

        def copy(a, kk, block, to, src=None):
            slot = outs[a].at[_flat(*block)]
            return pltpu.make_async_remote_copy(
                src_ref=slot if src is None else src, dst_ref=slot,
                send_sem=send.at[a, kk], recv_sem=recv.at[a, kk],
                device_id=to, device_id_type=MESH)

        mine = [pltpu.make_async_copy(ins[a], outs[a].at[_flat(*me)], lsem.at[a]) for a in range(n)]
        for cp in mine:
            cp.start()
        first = []
        for a in range(n):
            first.append(copy(a, 0, me, sibling, src=ins[a]))
            first += [copy(a, 1 + j, me, (*chip, c), src=ins[a]) for j, chip in enumerate(chips)]
        for cp in first:
            cp.start()
        passed = []
        for j, chip in enumerate(chips):
            for a in range(n):
                copy(a, 1 + j, (*chip, c), me).wait_recv()
                cp = copy(a, 4 + j, (*chip, c), sibling)
                cp.start()
                passed.append(cp)
        for a in range(n):
            copy(a, 0, sibling, me).wait_recv()
        for j, chip in enumerate(chips):
            for a in range(n):
                copy(a, 4 + j, (*chip, 1 - c), me).wait_recv()
        for cp in first + passed:
            cp.wait_send()
        for cp in mine:
            cp.wait()

    any_spec = pl.BlockSpec(memory_space=pl.ANY)
    return pl.pallas_call(
        body, name=name,
        in_specs=[any_spec] * n, out_specs=[any_spec] * n,
        out_shape=[jax.ShapeDtypeStruct((N_DEV,) + a.shape, a.dtype) for a in arrs],
        scratch_shapes=[pltpu.SemaphoreType.DMA((n, 7)), pltpu.SemaphoreType.DMA((n, 7)),
                        pltpu.SemaphoreType.DMA((n,))],
    )(*arrs)


def _peer_list():
    x, y, c = _coords()
    return [((1 - x if r & 4 else x), (1 - y if r & 2 else y), (1 - c if r & 1 else c)) for r in range(1, N_DEV)]


def _copy_plan(mode, src, land):
    x, y, c = _coords()
    me = _flat(x, y, c)
    if mode == "gather":
        return [(src, land.at[me], peer) for peer in _peer_list()]
    if mode == "exchange":
        return [(src.at[_flat(*peer)], land.at[me], peer) for peer in _peer_list()]
    if mode == "pair":
        return [(src.at[_flat(q // 2, q % 2, 1 - c)], land.at[q], (x, y, 1 - c)) for q in range(N_DEV // 2)]
    chips = [((1 - x if r & 2 else x), (1 - y if r & 1 else y)) for r in range(1, N_DEV // 2)]
    if mode == "chips":
        return [(src.at[2 * qx + qy], land.at[2 * x + y], (qx, qy, c)) for qx, qy in chips]
    if mode == "spread":
        return [(src, land.at[me], (x, y, 1 - c))] + [(src, land.at[me], (qx, qy, c)) for qx, qy in chips]
    assert mode == "forward"
    return [(land.at[_flat(qx, qy, c)], land.at[_flat(qx, qy, c)], (x, y, 1 - c)) for qx, qy in chips]


N_COPIES = dict(gather=N_DEV - 1, exchange=N_DEV - 1, pair=N_DEV // 2, chips=N_DEV // 2 - 1, spread=N_DEV // 2,
                forward=N_DEV // 2 - 1)


def _land_shape(mode, shape):
    return {"gather": (N_DEV,) + shape, "spread": (N_DEV,) + shape, "exchange": shape,
            "pair": (N_DEV // 2,) + shape[1:], "chips": shape}[mode]


HBM_SPEC = pl.BlockSpec(memory_space=pltpu.HBM)
SEM_SPEC = pl.BlockSpec(memory_space=pltpu.SEMAPHORE)
ANY_SPEC = pl.BlockSpec(memory_space=pl.ANY)
SIDE_EFFECT = pltpu.SideEffectType.DATAFLOW_SIDE_EFFECTING


def _async_start(groups, modes, after, name):
    modes = [modes] * len(groups) if isinstance(modes, str) else list(modes)
    arrs = [(a, m) for g, m in zip(groups, modes) for a in g]
    n = len(arrs)
    fresh = [i for i, (_, m) in enumerate(arrs) if m != "forward"]

    def body(*refs):
        srcs, new_lands = refs[:n], refs[n:n + len(fresh)]
        outs = refs[n + len(fresh) + 1:]
        lands = list(srcs)
        for k, i in enumerate(fresh):
            lands[i] = new_lands[k]
        for ai, (_, mode) in enumerate(arrs):
            for src_ref, dst_ref, peer in _copy_plan(mode, srcs[ai], lands[ai]):
                pltpu.make_async_remote_copy(src_ref=src_ref, dst_ref=dst_ref, send_sem=outs[2 * ai],
                                             recv_sem=outs[2 * ai + 1], device_id=peer, device_id_type=MESH).start()
        outs[-1][...] = jnp.zeros(outs[-1].shape, F32)

    land_shapes = [(_land_shape(arrs[i][1], arrs[i][0].shape), arrs[i][0].dtype) for i in fresh]
    n_buf = n + len(fresh)
    out_shape = [pltpu.SemaphoreType.DMA(())] * (2 * n)
    out_shape += [pltpu.HBM(a.shape, a.dtype) for a, _ in arrs]
    out_shape += [pltpu.HBM(shape, dt) for shape, dt in land_shapes]
    out_shape.append(jax.ShapeDtypeStruct((8, LANES), F32))
    res = pl.pallas_call(
        body, name=name, out_shape=tuple(out_shape),
        in_specs=[HBM_SPEC] * n_buf + [ANY_SPEC],
        out_specs=tuple([SEM_SPEC] * (2 * n) + [HBM_SPEC] * n_buf + [pl.BlockSpec(memory_space=pltpu.VMEM)]),
        input_output_aliases={i: 2 * n + i for i in range(n_buf)},
        compiler_params=pltpu.CompilerParams(has_side_effects=SIDE_EFFECT),
    )(*[pltpu.with_memory_space_constraint(a, pltpu.HBM) for a, _ in arrs],
      *[pltpu.with_memory_space_constraint(lax.empty(shape, dt), pltpu.HBM) for shape, dt in land_shapes],
      after)
    sems, thru = res[:2 * n], res[2 * n:-1]
    land_of = {i: thru[n + k] for k, i in enumerate(fresh)}
    states, idx = [], 0
    for g, mode in zip(groups, modes):
        ids = range(idx, idx + len(g))
        idx += len(g)
        states.append(([sems[2 * i] for i in ids], [sems[2 * i + 1] for i in ids],
                       None if mode == "forward" else [thru[i] for i in ids],
                       [land_of.get(i, thru[i]) for i in ids], mode))
    return states, res[-1]


def _async_wait(state, after, name):
    sends, recvs, srcs, lands, mode = state
    g = len(lands)
    bufs = (list(srcs) if srcs is not None else []) + list(lands)
    nb = len(bufs)

    def body(*refs):
        l_refs, sems = refs[nb - g:nb], refs[nb:nb + 2 * g]
        for ai in range(g):
            moved = l_refs[ai].at[pl.ds(0, N_COPIES[mode])]
            cp = pltpu.make_async_remote_copy(src_ref=moved, dst_ref=moved, send_sem=sems[ai], recv_sem=sems[g + ai],
                                              device_id=_coords(), device_id_type=MESH)
            cp.wait_send()
            cp.wait_recv()

    res = pl.pallas_call(
        body, name=name,
        out_shape=tuple(pltpu.HBM(a.shape, a.dtype) for a in bufs),
        in_specs=[HBM_SPEC] * nb + [SEM_SPEC] * (2 * g) + [ANY_SPEC],
        out_specs=tuple([HBM_SPEC] * nb),
        input_output_aliases={i: i for i in range(nb)},
        compiler_params=pltpu.CompilerParams(has_side_effects=SIDE_EFFECT),
    )(*bufs, *sends, *recvs, after)
    return (list(res[:nb - g]) if srcs is not None else None), list(res[nb - g:])


def _add_blocks(a, b):
    def body(a_ref, b_ref, o_ref):
        o_ref[...] = (a_ref[...].astype(F32) + b_ref[...].astype(F32)).astype(o_ref.dtype)

    spec = pl.BlockSpec((1,) + a.shape[1:], lambda i: (i, 0, 0))
    return pl.pallas_call(
        body, name="add_blocks", grid=(a.shape[0],), in_specs=[spec, spec], out_specs=spec,
        out_shape=jax.ShapeDtypeStruct(a.shape, a.dtype), compiler_params=_params(("parallel",)),
    )(a, b)


def _with_own(land, own, me):
    return lax.dynamic_update_index_in_dim(land, own, me, 0)


IN_SPLITS = (512, 512, 512, 8, 768, 256, 32, 1024, 1024)


def _from_shards(g, fn, out_widths, name, own=None, slot=None):
    _, k, n = g.shape
    tr = min(k, 256)

    def body(*refs):
        if own is not None:
            s_ref, g_ref, own_ref = refs[:3]
            cols = [jnp.where(s_ref[0] == j, own_ref[...], g_ref[j]) for j in range(N_DEV)]
        else:
            g_ref = refs[0]
            cols = [g_ref[j] for j in range(N_DEV)]
        for o_ref, val in zip(refs[-len(out_widths):], fn(jnp.concatenate(cols, axis=1))):
            o_ref[...] = val

    in_specs = [pl.BlockSpec((N_DEV, tr, n), lambda i, *_: (0, i, 0))]
    out_spec = [pl.BlockSpec((tr, wd), lambda i, *_: (i, 0)) for wd in out_widths]
    out_shape = [jax.ShapeDtypeStruct((k, wd), g.dtype) for wd in out_widths]
    if own is None:
        return pl.pallas_call(body, name=name, grid=(k // tr,), in_specs=in_specs, out_specs=out_spec,
                              out_shape=out_shape, compiler_params=_params(("parallel",)))(g)
    in_specs.append(pl.BlockSpec((tr, n), lambda i, *_: (i, 0)))
    return pl.pallas_call(
        body, name=name, out_shape=out_shape, compiler_params=_params(("parallel",)),
        grid_spec=pltpu.PrefetchScalarGridSpec(num_scalar_prefetch=1, grid=(k // tr,), in_specs=in_specs, out_specs=out_spec),
    )(slot, g, own)


def _unshard_cols(g, own=None, slot=None):
    return _from_shards(g, lambda full: (full,), [N_DEV * g.shape[2]], "unshard_cols_%d" % g.shape[2], own, slot)[0]


FFN_T = 256
FFN_SHARD = 2 * D_FF // N_DEV


def _unshard_ffn_in(g, own=None, slot=None):
    def pairs(full):
        parts = []
        for j in range(D_FF // FFN_T):
            parts += [full[:, j * FFN_T:(j + 1) * FFN_T], full[:, D_FF + j * FFN_T:D_FF + (j + 1) * FFN_T]]
        return (jnp.concatenate(parts, axis=1),)

    return _from_shards(g, pairs, [2 * D_FF], "unshard_ffn_in", own, slot)[0]


def _shard_ffn_in(w):
    tr = 256

    def body(w_ref, o_ref):
        x = w_ref[...]
        nb = D_FF // FFN_T
        full = jnp.concatenate([x[:, (2 * j + half) * FFN_T:(2 * j + half + 1) * FFN_T]
                                for half in range(2) for j in range(nb)], axis=1)
        for j in range(N_DEV):
            o_ref[j] = full[:, j * FFN_SHARD:(j + 1) * FFN_SHARD]

    return pl.pallas_call(
        body, name="shard_ffn_in", grid=(D // tr,),
        in_specs=[pl.BlockSpec((tr, 2 * D_FF), lambda i: (i, 0))],
        out_specs=pl.BlockSpec((N_DEV, tr, FFN_SHARD), lambda i: (0, i, 0)),
        out_shape=jax.ShapeDtypeStruct((N_DEV, D, FFN_SHARD), w.dtype),
        compiler_params=_params(("parallel",)),
    )(w)


def _shard_cols(w):
    k, n = w.shape[0], w.shape[1] // N_DEV
    tr = min(k, 256)

    def body(w_ref, o_ref):
        full = w_ref[...]
        for j in range(N_DEV):
            o_ref[j] = full[:, j * n:(j + 1) * n]

    return pl.pallas_call(
        body, name="shard_cols_%d" % n, grid=(k // tr,),
        in_specs=[pl.BlockSpec((tr, N_DEV * n), lambda i: (i, 0))],
        out_specs=pl.BlockSpec((N_DEV, tr, n), lambda i: (0, i, 0)),
        out_shape=jax.ShapeDtypeStruct((N_DEV, k, n), w.dtype),
        compiler_params=_params(("parallel",)),
    )(w)


IN_OFFS = tuple(sum(IN_SPLITS[:i]) for i in range(len(IN_SPLITS) + 1))
IN_SHARD = IN_OFFS[-1] // N_DEV
REGROUP_ROWS = 128


def _w_in_regroup(g, own=None, slot=None):
    def groups(full):
        fq, fk, fv, wf, cq, ckv, kr, gf, gm = [full[:, IN_OFFS[i]:IN_OFFS[i + 1]] for i in range(9)]
        rows = full.shape[0]
        w_a = jnp.concatenate([cq, ckv, gf, gm, wf, jnp.zeros((rows, 56), BF16), kr, jnp.zeros((rows, 32), BF16)], axis=1)
        return w_a, jnp.concatenate([fq, fk, fv], axis=1)

    return _from_shards(g, groups, [3200, 1536], "w_in_regroup", own, slot)


def _w_in_ungroup(da, db_):
    def body(a_ref, b_ref, o_ref):
        a = a_ref[...]
        full = jnp.concatenate([b_ref[...], a[:, 3072:3080], a[:, 0:768], a[:, 768:1024], a[:, 3136:3168],
                                a[:, 1024:3072]], axis=1)
        for j in range(N_DEV):
            o_ref[j] = full[:, j * IN_SHARD:(j + 1) * IN_SHARD]

    tr = REGROUP_ROWS
    return pl.pallas_call(
        body, name="w_in_ungroup", grid=(D // tr,),
        in_specs=[pl.BlockSpec((tr, 3200), lambda i: (i, 0)), pl.BlockSpec((tr, 1536), lambda i: (i, 0))],
        out_specs=pl.BlockSpec((N_DEV, tr, IN_SHARD), lambda i: (0, i, 0)),
        out_shape=jax.ShapeDtypeStruct((N_DEV, D, IN_SHARD), BF16),
        compiler_params=_params(("parallel",)),
    )(da, db_)


def _prepare_weights(g, own=None, slot=None):
    w = {}
    if own is not None:
        small = ("w_uq", "w_ukv", "w_out", "w_ffn_out")
        g = {n: (_with_own(a, own[n], slot[0]) if n in small else a) for n, a in g.items()}
    pick = (lambda n: (own[n], slot)) if own is not None else (lambda n: (None, None))
    if "w_in" in g:
        w["w_a"], w["w_b"] = _w_in_regroup(g["w_in"], *pick("w_in"))
    if "w_uq" in g:
        w_uq = g["w_uq"].reshape(Q_LORA, HEADS, 96)
        w["w_uq"] = jnp.pad(w_uq, ((0, 0), (0, 0), (0, 32))).reshape(Q_LORA, HEADS * LANES)
        ukv = g["w_ukv"]
        w["w_k"] = jnp.transpose(jnp.pad(ukv[:, :, :64], ((0, 0), (0, 0), (0, 64))), (1, 0, 2)).reshape(KV_LORA, HEADS * LANES)
        w["w_v"] = jnp.transpose(ukv[:, :, 64:], (1, 0, 2)).reshape(KV_LORA, HEADS * HEAD_DIM)
    if "w_out" in g:
        w["w_pf"] = _unshard_cols(g["w_proj_fox"], *pick("w_proj_fox"))
        w["w_pm"] = _unshard_cols(g["w_proj_mla"], *pick("w_proj_mla"))
        w["w_out"] = g["w_out"].reshape(D, D)
    if "w_ffn_in" in g:
        w["w_ffn_in"] = _unshard_ffn_in(g["w_ffn_in"], *pick("w_ffn_in"))
        w["w_ffn_out"] = g["w_ffn_out"].reshape(D_FF, D)
    return w


def _shard_grads(dw):
    out = {}
    if "w_a" in dw:
        out["w_in"] = _w_in_ungroup(dw["w_a"], dw["w_b"])
    if "w_uq" in dw:
        w_uq = dw["w_uq"].reshape(Q_LORA, HEADS, LANES)[:, :, :96].reshape(Q_LORA, Q_LORA)
        out["w_uq"] = w_uq.reshape(N_DEV, Q_LORA // N_DEV, Q_LORA)
        k_part = dw["w_k"].reshape(KV_LORA, HEADS, LANES)[:, :, :64]
        v_part = dw["w_v"].reshape(KV_LORA, HEADS, HEAD_DIM)
        out["w_ukv"] = jnp.transpose(jnp.concatenate([k_part, v_part], axis=2), (1, 0, 2))
    if "w_out" in dw:
        out["w_proj_fox"] = _shard_cols(dw["w_pf"])
        out["w_proj_mla"] = _shard_cols(dw["w_pm"])
        out["w_out"] = dw["w_out"].reshape(N_DEV, D // N_DEV, D)
    if "w_ffn_in" in dw:
        out["w_ffn_in"] = _shard_ffn_in(dw["w_ffn_in"])
        out["w_ffn_out"] = dw["w_ffn_out"].reshape(N_DEV, D_FF // N_DEV, D)
    return out


def _fwd_bwd(x, pos, mod, target, w, vec, wts, send, relay):
    shift_mix, scale_mix, gate_mix, shift_ffn, scale_ffn, gate_ffn = [mod[:, i * D:(i + 1) * D] for i in range(6)]
    g_pre_mix, g_post_mix, g_pre_ffn, g_post_ffn = vec["g_pre_mix"], vec["g_post_mix"], vec["g_pre_ffn"], vec["g_post_ffn"]
    g_q, g_kv = vec["g_q_lora"], vec["g_kv_lora"]

    inv_freq = 1.0 / (ROPE_THETA ** (jnp.arange(0, ROPE_DIM, 2, dtype=F32) / ROPE_DIM))
    invf = jnp.concatenate([jnp.zeros((64,), F32), inv_freq, inv_freq, jnp.zeros((32,), F32)]).reshape(1, LANES)
    ct, sa, sb = _rope_tables(pos, invf)

    def pre1(xv, g, sc, sh):
        return (xv * _rstd(xv) * g) * (1.0 + sc) + sh
    proj_a, h = _mm_epi(x, w["w_a"], "nn", 640, lambda r: ((r,), ()), "in_proj_a", 1024, outs=[(3200, 640, F32)],
                        pro=(pre1, [g_pre_mix, scale_mix, shift_mix], 0))
    qkv = _mm(h, w["w_b"], "nn", BF16, "in_proj_b")

    def lora_norm(cv, g):
        return cv * _rstd(cv) * g
    w = {**w, **wts("lora", qkv)}
    tables = [(ct, LANES), (sa, LANES), (sb, LANES)]

    def rope_q(qv, c_, a_, b_):
        return (jnp.concatenate([_rope(qv[:, hd * LANES:(hd + 1) * LANES], c_, a_, b_) for hd in range(HEADS)], axis=1),), ()
    q_m, cqn = _mm_epi(proj_a, w["w_uq"], "nn", D, rope_q, "mla_uq", 512, rows=tables, outs=[(D, D, BF16)],
                       pro=(lora_norm, [g_q], 0))

    def rope_k(kv, misc, c_, a_, b_):
        lane = lax.broadcasted_iota(jnp.int32, (1, LANES), 1)
        kpe = jnp.where((lane >= 64) & (lane < 96), _rope(misc, c_, a_, b_), 0.0)
        return (jnp.concatenate([kv[:, hd * LANES:(hd + 1) * LANES] + kpe for hd in range(HEADS)], axis=1),), ()
    k_m, ckvn = _mm_epi(proj_a, w["w_k"], "nn", D, rope_k, "mla_uk", 512, rows=[(proj_a, LANES, 24)] + tables,
                        outs=[(D, D, BF16)], pro=(lora_norm, [g_kv], Q_LORA // KV_LORA))
    v_m = _mm(ckvn, w["w_v"], "nn", BF16, "mla_uv")

    bf = jnp.transpose(vec["b_forget"])
    zt, neg_f = _fox_gates(proj_a, 24, bf)
    bias = neg_f.reshape(HEADS, N_ATT, 1, ATT_T)
    o_b, lse_b = _attn_fwd(q_m, 0, k_m, 0, v_m, 0, 2 * LANES, 1.0 / math.sqrt(64 + ROPE_DIM), None, "mla_attn")
    bias = bias + wts("relay_proj", o_b)["tok"][0, 0]
    o_a, lse_a = _attn_fwd(qkv, 0, qkv, 4, qkv, 8, LANES, 1.0 / math.sqrt(HEAD_DIM), bias, "fox_attn")

    w = {**w, **wts("proj", o_a)}
    gate_mix = gate_mix + wts("relay_ffn", o_a)["tok"][0, 0]
    pa = _mm(o_a, w["w_pf"], "nn", BF16, "proj_fox")

    def merge(pb_, gf, gm, pa_):
        return (_sigmoid(gf) * pa_ + _sigmoid(gm) * pb_, pb_), ()
    merged, pb = _mm_epi(o_b, w["w_pm"], "nn", 512, merge, "proj_mla", 1024,
                         rows=[(proj_a, 512, 2), (proj_a, 512, 4), (pa, 512)], outs=[(D, 512, BF16), (D, 512, BF16)])
    def post1(yv, xv, gate, gpost, gpre, sc, sh):
        x1 = xv + gate * (yv * _rstd(yv) * gpost)
        return (x1, (x1 * _rstd(x1) * gpre) * (1.0 + sc) + sh, yv), ()
    x1, h2, y = _mm_epi(merged, w["w_out"], "nn", D, post1, "out_proj", 256, rows=[(x, D)],
                        vecs=[gate_mix, g_post_mix, g_pre_ffn, scale_ffn, shift_ffn],
                        outs=[(D, D, F32), (D, D, BF16), (D, D, F32)])
    w = {**w, **wts("ffn", h2)}

    def swiglu(r):
        g, u = r[:, :FFN_T], r[:, FFN_T:]
        return (g * _sigmoid(g) * u, r), ()
    act, gu = _mm_epi(h2, w["w_ffn_in"], "nn", 2 * FFN_T, swiglu, "ffn_in", 1024,
                      outs=[(D_FF, FFN_T, BF16), (2 * D_FF, 2 * FFN_T, BF16)])

    def head(y2v, x1v, tv, gate, gpost):
        r = _rstd(y2v)
        yn = y2v * r
        n2 = yn * gpost
        err = (x1v + gate * n2) - tv
        dx2 = err * (1.0 / D)
        dn2 = dx2 * gate
        dy2 = _norm_bwd(dn2 * gpost, yn, r)
        return (dx2, dy2), (_colsum(err * err), _colsum(dx2 * n2), _colsum(dn2 * yn))
    dx2, dy2, err_cols, d_gate_ffn, d_g_post_ffn = _mm_epi(
        act, w["w_ffn_out"], "nn", D, head, "ffn_out", 256, rows=[(x1, D), (target, D)], vecs=[gate_ffn, g_post_ffn],
        outs=[(D, D, F32), (D, D, BF16)], sums=[D, D, D])

    def swiglu_bwd(da, guv):
        g, u = guv[:, :FFN_T].astype(F32), guv[:, FFN_T:].astype(F32)
        sg = _sigmoid(g)
        return (jnp.concatenate([da * u * (sg * (1.0 + g * (1.0 - sg))), da * (g * sg)], axis=1),), ()
    (dgu,) = _mm_epi(dy2, w["w_ffn_out"], "nt", FFN_T, swiglu_bwd, "ffn_out_dx", 1024, rows=[(gu, 2 * FFN_T)],
                     outs=[(2 * D_FF, 2 * FFN_T, BF16)])
    dw = {"w_ffn_out": _mm(act, dy2, "tn", BF16, "ffn_out_dw")}
    dw["w_ffn_in"] = _mm(h2, dgu, "tn", BF16, "ffn_in_dw")
    gate_mix = gate_mix + send({n: dw.pop(n) for n in ("w_ffn_in", "w_ffn_out")})[0, 0]

    def mid(dh, x1v, dx2v, yv, gpre, sc, gate, gpost):
        r2 = _rstd(x1v)
        x1n = x1v * r2
        t = dh * x1n
        dx1 = dx2v + _norm_bwd(dh * (gpre * (1.0 + sc)), x1n, r2)
        ry = _rstd(yv)
        yn = yv * ry
        dn1 = dx1 * gate
        dy = _norm_bwd(dn1 * gpost, yn, ry)
        sums = (_colsum(dh), _colsum(t) * gpre, _colsum(t) * (1.0 + sc), _colsum(dx1 * (yn * gpost)), _colsum(dn1 * yn))
        return (dx1, dy), sums
    dx1, dy, d_shift_ffn, d_scale_ffn, d_g_pre_ffn, d_gate_mix, d_g_post_mix = _mm_epi(
        dgu, w["w_ffn_in"], "nt", D, mid, "ffn_in_dx", 256, rows=[(x1, D), (dx2, D), (y, D)],
        vecs=[g_pre_ffn, scale_ffn, gate_mix, g_post_mix], outs=[(D, D, F32), (D, D, BF16)], sums=[D] * 5)

    dw["w_out"] = _mm(merged, dy, "tn", BF16, "out_proj_dw")

    def merge_bwd(dm, gf, gm, pa_, pb_):
        sf, sm = _sigmoid(gf), _sigmoid(gm)
        return (dm * sf, dm * sm, dm * pa_ * (sf * (1.0 - sf)), dm * pb_ * (sm * (1.0 - sm))), ()
    dpa, dpb, dgf, dgm = _mm_epi(dy, w["w_out"], "nt", 512, merge_bwd, "out_proj_dx", 1024,
                                 rows=[(proj_a, 512, 2), (proj_a, 512, 4), (pa, 512), (pb, 512)],
                                 outs=[(D, 512, BF16)] * 4)
    do_a = _mm(dpa, w["w_pf"], "nt", BF16, "proj_fox_dx")
    do_b = _mm(dpb, w["w_pm"], "nt", BF16, "proj_mla_dx")
    dw["w_pf"] = _mm(o_a, dpa, "tn", BF16, "proj_fox_dw")
    dw["w_pm"] = _mm(o_b, dpb, "tn", BF16, "proj_mla_dw")
    bias = bias + send({n: dw.pop(n) for n in ("w_out", "w_pf", "w_pm")})[0, 0]

    sc_a, sc_b = 1.0 / math.sqrt(HEAD_DIM), 1.0 / math.sqrt(64 + ROPE_DIM)
    dq_a, dk_a, dv_a, dbias = _attn_grad(qkv, 0, qkv, 4, qkv, 8, do_a, lse_a, LANES, sc_a, bias, BF16, "fox_attn_bwd")
    dq_m, dk_m, dv_m = _attn_grad(q_m, 0, k_m, 0, v_m, 0, do_b, lse_b, 2 * LANES, sc_b, None, F32, "mla_attn_bwd")

    def mla_rope_bwd(dq, dk, c_, a_, b_):
        lane = lax.broadcasted_iota(jnp.int32, (1, LANES), 1)
        dqs = [_rope_t(dq[:, hd * LANES:(hd + 1) * LANES], c_, a_, b_) for hd in range(HEADS)]
        dkpe = dk[:, 0:LANES]
        for hd in range(1, HEADS):
            dkpe = dkpe + dk[:, hd * LANES:(hd + 1) * LANES]
        dkpe = jnp.where((lane >= 64) & (lane < 96), dkpe, 0.0)
        dkr = jnp.where((lane >= 64) & (lane < 96), _rope_t(dkpe, c_, a_, b_), 0.0)
        return (jnp.concatenate(dqs, axis=1), dk, dkr), ()
    dqb, dkb, dkr = _rowwise(mla_rope_bwd, [(dq_m, D, 0), (dk_m, D, 0), (ct, LANES, 0), (sa, LANES, 0), (sb, LANES, 0)],
                             [], [(D, BF16), (D, BF16), (LANES, F32)], [], "mla_rope_bwd")
    def lora_q_bwd(dq, cq, gq):
        rq = _rstd(cq)
        cqh = cq * rq
        return (_norm_bwd(dq * gq, cqh, rq),), (_colsum(dq * cqh),)
    dcq, d_g_q = _mm_epi(dqb, w["w_uq"], "nt", Q_LORA, lora_q_bwd, "mla_uq_dx", 512, rows=[(proj_a, Q_LORA, 0)],
                         vecs=[g_q], outs=[(Q_LORA, Q_LORA, BF16)], sums=[Q_LORA])
    dw["w_uq"] = _mm(cqn, dqb, "tn", BF16, "mla_uq_dw")

    def lora_kv_bwd(dv_part, dk_part, ckv, gkv):
        dkv = dv_part + dk_part
        rk = _rstd(ckv)
        ckh = ckv * rk
        return (_norm_bwd(dkv * gkv, ckh, rk),), (_colsum(dkv * ckh),)
    dckv, d_g_kv = _mm_epi(dv_m, w["w_v"], "nt", KV_LORA, lora_kv_bwd, "mla_uv_dx", 1024,
                           rows=[(_mm(dkb, w["w_k"], "nt", F32, "mla_uk_dx"), KV_LORA), (proj_a, KV_LORA, 3)],
                           vecs=[g_kv], outs=[(KV_LORA, KV_LORA, BF16)], sums=[KV_LORA])
    dw["w_k"] = _mm(ckvn, dkb, "tn", BF16, "mla_uk_dw")
    dw["w_v"] = _mm(ckvn, dv_m, "tn", BF16, "mla_uv_dw")

    dzt, d_bf = _fox_gates_bwd(dbias.reshape(HEADS, S), zt, bf)
    dmisc = (dkr + jnp.pad(jnp.transpose(dzt), ((0, 0), (0, LANES - HEADS)))).astype(BF16)
    dproj_a = jnp.concatenate([dcq, dckv, dgf, dgm, dmisc], axis=1)
    dqkv = jnp.concatenate([dq_a, dk_a, dv_a], axis=1)
    dw["w_a"] = _mm(h, dproj_a, "tn", BF16, "in_proj_a_dw")
    dw["w_b"] = _mm(h, dqkv, "tn", BF16, "in_proj_b_dw")
    tok = send(dw, True)
    dh_a = _mm(dproj_a, w["w_a"], "nt", F32, "in_proj_a_dx", dep=tok)
    g_pre_mix = g_pre_mix + relay(dh_a)[0, 0]

    def first(dh_b, dh_a, xv, dx1v, gpre, sc):
        dhv = dh_b + dh_a
        r = _rstd(xv)
        xn = xv * r
        t = dhv * xn
        dx = dx1v + _norm_bwd(dhv * (gpre * (1.0 + sc)), xn, r)
        return (dx,), (_colsum(dhv), _colsum(t) * gpre, _colsum(t) * (1.0 + sc))
    grad_x, d_shift_mix, d_scale_mix, d_g_pre_mix = _mm_epi(
        dqkv, w["w_b"], "nt", D, first, "in_proj_b_dx", 256,
        rows=[(dh_a, D), (x, D), (dx1, D)],
        vecs=[g_pre_mix, scale_mix], outs=[(D, D, F32)], sums=[D] * 3)

    dmod = jnp.concatenate([d_shift_mix, d_scale_mix, d_gate_mix, d_shift_ffn, d_scale_ffn, d_gate_ffn], axis=1)
    small = dict(dmod=dmod, g_pre_mix=d_g_pre_mix, g_post_mix=d_g_post_mix, g_pre_ffn=d_g_pre_ffn,
                 g_post_ffn=d_g_post_ffn, g_q_lora=d_g_q, g_kv_lora=d_g_kv,
                 b_forget=jnp.pad(jnp.transpose(d_bf), ((0, 0), (0, LANES - HEADS))), err=err_cols)
    return grad_x, small


SMALL_ORDER = ("dmod", "g_pre_mix", "g_post_mix", "g_pre_ffn", "g_post_ffn", "g_q_lora", "g_kv_lora", "b_forget", "err")
SMALL_PARAM = {"dmod": "b_ada"}
MATRICES = ("w_in", "w_uq", "w_ukv", "w_proj_fox", "w_proj_mla", "w_out", "w_ffn_in", "w_ffn_out")
WEIGHTS = ("w_ada", "b_ada", "g_pre_mix", "g_post_mix", "g_pre_ffn", "g_post_ffn", "w_in", "b_forget", "g_q_lora",
           "w_uq", "g_kv_lora", "w_ukv", "w_proj_fox", "w_proj_mla", "w_out", "w_ffn_in", "w_ffn_out")


def kernel(x, c, positions, w_ada, b_ada, g_pre_mix, g_post_mix, g_pre_ffn, g_post_ffn, w_in, b_forget, g_q_lora, w_uq, g_kv_lora, w_ukv, w_proj_fox, w_proj_mla, w_out, w_ffn_in, w_ffn_out, loss_target, m_w_ada, m_b_ada, m_g_pre_mix, m_g_post_mix, m_g_pre_ffn, m_g_post_ffn, m_w_in, m_b_forget, m_g_q_lora, m_w_uq, m_g_kv_lora, m_w_ukv, m_w_proj_fox, m_w_proj_mla, m_w_out, m_w_ffn_in, m_w_ffn_out, v_w_ada, v_b_ada, v_g_pre_mix, v_g_post_mix, v_g_pre_ffn, v_g_post_ffn, v_w_in, v_b_forget, v_g_q_lora, v_w_uq, v_g_kv_lora, v_w_ukv, v_w_proj_fox, v_w_proj_mla, v_w_out, v_w_ffn_in, v_w_ffn_out):
    prm = dict(w_ada=w_ada, b_ada=b_ada, g_pre_mix=g_pre_mix, g_post_mix=g_post_mix, g_pre_ffn=g_pre_ffn,
               g_post_ffn=g_post_ffn, w_in=w_in, b_forget=b_forget, g_q_lora=g_q_lora, w_uq=w_uq, g_kv_lora=g_kv_lora,
               w_ukv=w_ukv, w_proj_fox=w_proj_fox, w_proj_mla=w_proj_mla, w_out=w_out, w_ffn_in=w_ffn_in, w_ffn_out=w_ffn_out)
    mom = dict(w_ada=m_w_ada, b_ada=m_b_ada, g_pre_mix=m_g_pre_mix, g_post_mix=m_g_post_mix, g_pre_ffn=m_g_pre_ffn,
               g_post_ffn=m_g_post_ffn, w_in=m_w_in, b_forget=m_b_forget, g_q_lora=m_g_q_lora, w_uq=m_w_uq,
               g_kv_lora=m_g_kv_lora, w_ukv=m_w_ukv, w_proj_fox=m_w_proj_fox, w_proj_mla=m_w_proj_mla, w_out=m_w_out,
               w_ffn_in=m_w_ffn_in, w_ffn_out=m_w_ffn_out)
    var = dict(w_ada=v_w_ada, b_ada=v_b_ada, g_pre_mix=v_g_pre_mix, g_post_mix=v_g_post_mix, g_pre_ffn=v_g_pre_ffn,
               g_post_ffn=v_g_post_ffn, w_in=v_w_in, b_forget=v_b_forget, g_q_lora=v_g_q_lora, w_uq=v_w_uq,
               g_kv_lora=v_g_kv_lora, w_ukv=v_w_ukv, w_proj_fox=v_w_proj_fox, w_proj_mla=v_w_proj_mla, w_out=v_w_out,
               w_ffn_in=v_w_ffn_in, w_ffn_out=v_w_ffn_out)
    me = _flat(*_coords())
    slot = jnp.reshape(me, (1,)).astype(jnp.int32)

    own = {n: prm[n][0].astype(BF16) for n in MATRICES}
    no_dep = jnp.zeros((8, LANES), F32)
    (st_c, st_in), tok = _async_start([[c], [own["w_in"]]], ["gather", "spread"], no_dep, "gather_in_start")
    (c_own,), (c_land,) = _async_wait(st_c, tok, "gather_c_wait")
    c_all = _with_own(c_land, c_own, me).reshape(N_DEV, D)
    ada_cols = w_ada.shape[2]
    b_cols = lax.dynamic_slice(b_ada, (0, me * ada_cols), (1, ada_cols))
    mod_cols, silu_c = _mod_part(c_all, w_ada[0], b_cols)
    (mod_all,) = _all_gather([mod_cols], "gather_mod")

    (w_in_own,), (w_in_land,) = _async_wait(st_in, mod_all, "gather_in_wait")
    (st_in,), tok = _async_start([[w_in_land]], "forward", no_dep, "gather_in_forward")
    _, (w_in_land,) = _async_wait(st_in, tok, "gather_in_forward_wait")
    w = _prepare_weights({"w_in": w_in_land}, {"w_in": w_in_own}, slot)
    later = dict(lora=("w_uq", "w_ukv"), proj=("w_proj_fox", "w_proj_mla", "w_out"), ffn=("w_ffn_in", "w_ffn_out"))
    states, tok = _async_start([[own[n] for n in names] for names in later.values()], ["gather", "spread", "spread"],
                               w["w_b"], "gather_rest_start")
    gather_state = dict(zip(later, states))
    own_thru = {}

    def wts(group, after):
        if group.startswith("relay_"):
            name = group[len("relay_"):]
            own_thru[name], lands = _async_wait(gather_state[name], after, "gather_" + name + "_wait")
            (gather_state[name],), t = _async_start([lands], "forward", no_dep, "gather_" + name + "_forward")
            return {"tok": t}
        srcs, lands = _async_wait(gather_state[group], after, "gather_" + group + "_landed")
        srcs = own_thru.get(group, srcs)
        return _prepare_weights(dict(zip(later[group], lands)), dict(zip(later[group], srcs)), slot)

    sent, last = [], {}

    def send(grads, final=False):
        shards = _shard_grads(grads)
        names = list(shards)
        (state,), t = _async_start([[shards[n] for n in names]], "pair" if final else "exchange", no_dep,
                                   "exchange_" + names[0] + "_start")
        if final:
            last.update(names=names, state=state)
        else:
            sent.append((names, state))
        return t

    def relay(after):
        srcs, lands = _async_wait(last["state"], after, "exchange_pair_wait")
        sums = []
        for src, land in zip(srcs, lands):
            by_chip = src.reshape((N_DEV // 2, 2) + src.shape[1:])
            sums.append(_add_blocks(lax.dynamic_index_in_dim(by_chip, lax.axis_index("c"), 1, keepdims=False), land))
        (last["state"],), t = _async_start([sums], "chips", no_dep, "exchange_chips_start")
        return t

    mod = lax.dynamic_index_in_dim(mod_all, me, axis=1, keepdims=False).reshape(1, 6 * D) + tok[0, 0]

    vec = dict(g_pre_mix=g_pre_mix, g_post_mix=g_post_mix, g_pre_ffn=g_pre_ffn, g_post_ffn=g_post_ffn,
               g_q_lora=g_q_lora, g_kv_lora=g_kv_lora, b_forget=b_forget)
    pos = positions.astype(F32).reshape(S, 1)
    grad_x, small = _fwd_bwd(x[0], pos, mod, loss_target[0], w, vec, wts, send, relay)

    bundle = jnp.concatenate([small[n] for n in SMALL_ORDER], axis=1)
    (small_state,), tok = _async_start([[bundle]], "gather", jnp.zeros((8, LANES), F32), "gather_small_start")

    out = {}
    after = tok
    for names, state in sent:
        srcs, lands = _async_wait(state, after, "exchange_" + names[0] + "_wait")
        for n, src, land in zip(names, srcs, lands):
            out[n] = _adamw(prm[n][0], mom[n][0], var[n][0], land, "adamw_" + n, src, slot)
            after = out[n][0]
    srcs, lands = _async_wait(last["state"], after, "exchange_chips_wait")
    for n, src, land in zip(last["names"], srcs, lands):
        out[n] = _adamw(prm[n][0], mom[n][0], var[n][0], land, "adamw_" + n, src, slot // 2)
        after = out[n][0]

    (own_bundle,), (bundle_all,) = _async_wait(small_state, after, "gather_small_wait")
    bundle_all = _with_own(bundle_all, own_bundle, me)
    dmod_all = bundle_all[:, 0, :6 * D]
    dm_cols = lax.dynamic_slice(dmod_all, (0, me * ada_cols), (N_DEV, ada_cols))
    g_ada = _w_ada_grad(jnp.transpose(silu_c), dm_cols)
    out["w_ada"] = _adamw(w_ada[0], m_w_ada[0], v_w_ada[0], g_ada[None], "adamw_w_ada")

    offsets, off = {}, 0
    for n in SMALL_ORDER:
        offsets[n] = off
        off += small[n].shape[1]
    names = [SMALL_PARAM.get(n, n) for n in SMALL_ORDER if n != "err"]
    results, err = _adamw_rows(bundle_all, [offsets[n] for n in SMALL_ORDER if n != "err"],
                               [prm[n] for n in names], [mom[n] for n in names], [var[n] for n in names],
                               offsets["err"], D)
    out.update(zip(names, results))
    loss = 0.5 * jnp.sum(err) / D

    res = [loss, grad_x[None]]
    for kind in range(4):
        for n in WEIGHTS:
            t = out[n][kind]
            res.append(t[None] if prm[n].ndim == 3 else t)
    return tuple(res)
```

```python
import functools
import math

import jax
import jax.numpy as jnp
from jax import lax
from jax.experimental import pallas as pl
from jax.experimental.pallas import tpu as pltpu

F32 = jnp.float32
BF16 = jnp.bfloat16

N_DEV = 8
S = 2048
D = 1024
D_FF = 2816
HEADS = 8
HEAD_DIM = 64
Q_LORA = 768
KV_LORA = 256
ROPE_DIM = 32
ROPE_THETA = 10000.0
NORM_EPS = 1e-6
LANES = 128
VMEM_LIMIT = 56 * 1024 * 1024

ADAM_LR = 0.001
ADAM_B1 = 0.9
ADAM_B2 = 0.999
ADAM_EPS = 1e-08
ADAM_WD = 0.01
ADAM_STEP = 10

ATT_T = 256
LOG2E = 1.4426950408889634
N_ATT = S // ATT_T

NN = (((1,), (0,)), ((), ()))
NT = (((1,), (1,)), ((), ()))
TN = (((0,), (0,)), ((), ()))
MESH = pl.DeviceIdType.MESH


def _params(sem=None):
    return pltpu.CompilerParams(dimension_semantics=sem, vmem_limit_bytes=VMEM_LIMIT)


def _pick(n, cap):
    best = None
    for t in range(LANES, cap + 1, LANES):
        if n % t == 0:
            best = t
    return best if best is not None else n


def _mm(a, b, mode, out_dtype, name, acc=None, dep=None):
    if mode == "nn":
        (m, k), (k2, n), dn = a.shape, b.shape, NN
    elif mode == "nt":
        (m, k), (n, k2), dn = a.shape, b.shape, NT
    else:
        (k, m), (k2, n), dn = a.shape, b.shape, TN
    assert k == k2, (a.shape, b.shape, mode)
    tn = _pick(n, 640)
    tm = _pick(m, 1536)
    osz = jnp.dtype(out_dtype).itemsize

    def need(tm_):
        blk = tm_ * k * 2 + tn * k * 2 + tm_ * tn * osz + (tm_ * tn * 4 if acc is not None else 0)
        return 2 * blk + tm_ * tn * 4
    while need(tm) > 36 * 1024 * 1024 and tm % 256 == 0:
        tm //= 2

    def body(*refs):
        a_ref, b_ref, o_ref = refs[0], refs[1], refs[-1]
        r = lax.dot_general(a_ref[...], b_ref[...], dn, preferred_element_type=F32)
        if acc is not None:
            r = r + refs[2][...]
        o_ref[...] = r.astype(o_ref.dtype)

    if mode == "tn":
        a_spec = pl.BlockSpec((k, tm), lambda i, j: (0, i))
    else:
        a_spec = pl.BlockSpec((tm, k), lambda i, j: (i, 0))
    if mode == "nt":
        b_spec = pl.BlockSpec((tn, k), lambda i, j: (j, 0))
    else:
        b_spec = pl.BlockSpec((k, tn), lambda i, j: (0, j))
    o_spec = pl.BlockSpec((tm, tn), lambda i, j: (i, j))
    in_specs = [a_spec, b_spec] + ([o_spec] if acc is not None else [])
    in_specs += [pl.BlockSpec(memory_space=pl.ANY)] if dep is not None else []
    args = (a, b) + ((acc,) if acc is not None else ()) + ((dep,) if dep is not None else ())
    return pl.pallas_call(
        body, name=name, grid=(m // tm, n // tn),
        in_specs=in_specs, out_specs=o_spec,
        out_shape=jax.ShapeDtypeStruct((m, n), out_dtype),
        compiler_params=_params(("parallel", "parallel")),
    )(*args)


def _mm_epi(a, b, mode, tnb, epi, name, tm, rows=(), vecs=(), outs=(), sums=(), pro=None):
    m = a.shape[0]
    k, nb = (b.shape if mode == "nn" else b.shape[::-1])
    dn = NN if mode == "nn" else NT
    pro_fn, pro_vecs, a_off = pro if pro is not None else (None, (), 0)
    n_in = 2 + len(rows) + len(vecs)
    n_all = n_in + len(pro_vecs)
    sub = min(tm, 256)

    def body(*refs):
        if pro is not None:
            a_out, a_scr = refs[-2:]
            refs = refs[:-2]

            @pl.when(pl.program_id(1) == 0)
            def _():
                a_scr[...] = pro_fn(refs[0][...], *[x[...] for x in refs[n_in:n_all]]).astype(BF16)
                a_out[...] = a_scr[...]
            a_ref = a_scr
        else:
            a_ref = refs[0]
        o_refs = refs[n_all:n_all + len(outs)]
        s_refs = refs[n_all + len(outs):]
        if sums:
            @pl.when((pl.program_id(0) == 0) & (pl.program_id(1) == 0))
            def _():
                for s_ref in s_refs:
                    s_ref[...] = jnp.zeros(s_ref.shape, F32)
        for c in range(tm // sub):
            rs = slice(c * sub, (c + 1) * sub)
            r = lax.dot_general(a_ref[rs, :], refs[1][...], dn, preferred_element_type=F32)
            o_vals, s_vals = epi(r, *[x[rs, :] for x in refs[2:2 + len(rows)]], *[x[...] for x in refs[2 + len(rows):n_in]])
            assert len(o_vals) == len(o_refs) and len(s_vals) == len(s_refs)
            for o_ref, val in zip(o_refs, o_vals):
                o_ref[rs, :] = val.astype(o_ref.dtype)
            for s_ref, val in zip(s_refs, s_vals):
                s_ref[...] += val

    once = dict(pipeline_mode=pl.Buffered(1)) if nb == tnb else {}
    if mode == "nn":
        b_spec = pl.BlockSpec((k, tnb), lambda i, j: (0, j), **once)
    else:
        b_spec = pl.BlockSpec((tnb, k), lambda i, j: (j, 0), **once)
    in_specs = [pl.BlockSpec((tm, k), lambda i, j: (i, a_off)), b_spec]
    rows = [tuple(r) + (0,) * (3 - len(r)) for r in rows]
    in_specs += [pl.BlockSpec((tm, w), functools.partial(lambda i, j, off: (i, j + off), off=off)) for _, w, off in rows]
    in_specs += [pl.BlockSpec(v.shape, lambda i, j: (0, 0)) for v in list(vecs) + list(pro_vecs)]
    out_specs = [pl.BlockSpec((tm, w), lambda i, j: (i, j)) for _, w, _ in outs]
    out_specs += [pl.BlockSpec((1, w), lambda i, j: (0, 0)) for w in sums]
    out_shape = [jax.ShapeDtypeStruct((m, full), dt) for full, _, dt in outs]
    out_shape += [jax.ShapeDtypeStruct((1, w), F32) for w in sums]
    if pro is not None:
        out_specs.append(pl.BlockSpec((tm, k), lambda i, j: (i, 0)))
        out_shape.append(jax.ShapeDtypeStruct((m, k), BF16))
    return pl.pallas_call(
        body, name=name, grid=(m // tm, nb // tnb),
        in_specs=in_specs, out_specs=out_specs, out_shape=out_shape,
        scratch_shapes=[pltpu.VMEM((tm, k), BF16)] if pro is not None else [],
        compiler_params=_params(("arbitrary", "arbitrary") if sums else ("parallel", "arbitrary" if pro is not None else "parallel")),
    )(a, b, *[r[0] for r in rows], *vecs, *pro_vecs)


def _rowwise(fn, row_ins, vec_ins, row_outs, sum_outs, name, tm=256):
    n_in = len(row_ins) + len(vec_ins)
    n_o = len(row_outs)
    rows = row_ins[0][0].shape[0]

    def body(*refs):
        vals = [r[...] for r in refs[:n_in]]
        outs = refs[n_in:]
        ro, so = fn(*vals)
        assert len(ro) == n_o and len(so) == len(sum_outs)
        for r, v in zip(outs[:n_o], ro):
            r[...] = v.astype(r.dtype)
        if sum_outs:
            @pl.when(pl.program_id(0) == 0)
            def _():
                for r in outs[n_o:]:
                    r[...] = jnp.zeros(r.shape, F32)
            for r, v in zip(outs[n_o:], so):
                r[...] += v

    in_specs = [pl.BlockSpec((tm, w), functools.partial(lambda i, b: (i, b), b=b)) for _, w, b in row_ins]
    in_specs += [pl.BlockSpec(v.shape, lambda i: (0, 0)) for v in vec_ins]
    out_specs = [pl.BlockSpec((tm, w), lambda i: (i, 0)) for w, _ in row_outs]
    out_specs += [pl.BlockSpec((1, w), lambda i: (0, 0)) for w in sum_outs]
    out_shape = [jax.ShapeDtypeStruct((rows, w), dt) for w, dt in row_outs]
    out_shape += [jax.ShapeDtypeStruct((1, w), F32) for w in sum_outs]
    return pl.pallas_call(
        body, name=name, grid=(rows // tm,),
        in_specs=in_specs, out_specs=out_specs, out_shape=out_shape,
        compiler_params=_params(("arbitrary",)),
    )(*[a for a, _, _ in row_ins], *vec_ins)


def _sigmoid(x):
    return 1.0 / (1.0 + jnp.exp(-x))


def _rstd(x):
    return lax.rsqrt(jnp.mean(x * x, axis=-1, keepdims=True) + NORM_EPS)


def _norm_bwd(dyn, xn, r):
    return r * (dyn - xn * jnp.mean(dyn * xn, axis=-1, keepdims=True))


def _colsum(x):
    return jnp.sum(x, axis=0, keepdims=True)


def _rope_tables(pos, invf):
    def fn(p, f):
        lane = lax.broadcasted_iota(jnp.int32, (1, LANES), 1)
        ang = p * f
        cs, sn = jnp.cos(ang), jnp.sin(ang)
        rot = (lane >= 64) & (lane < 96)
        ct = jnp.where(lane < 64, 1.0, jnp.where(rot, cs, 0.0))
        sa = jnp.where((lane >= 64) & (lane < 80), -sn, 0.0)
        sb = jnp.where((lane >= 80) & (lane < 96), sn, 0.0)
        return (ct, sa, sb), ()
    return _rowwise(fn, [(pos, 1, 0)], [invf], [(LANES, F32)] * 3, [], "rope_tables")


def _rope(x, ct, sa, sb):
    return x * ct + pltpu.roll(x, LANES - 16, 1) * sa + pltpu.roll(x, 16, 1) * sb


def _rope_t(x, ct, sa, sb):
    return x * ct - pltpu.roll(x, LANES - 16, 1) * sa - pltpu.roll(x, 16, 1) * sb


def _head_mask(width, hh):
    lane = lax.broadcasted_iota(jnp.int32, (1, width), 1)
    half = width // 2
    return (lane >= hh * half) & (lane < (hh + 1) * half)


ATT_PP = 2
ATT_CHAINS = [(a, hh) for a in range(ATT_PP) for hh in range(2)]
ATT_G = HEADS // (2 * ATT_PP)


def _pair(ref_or_val, a, width, rows=slice(None)):
    return ref_or_val[rows, a * width:(a + 1) * width]


def _attn_fwd(q, qo, k, ko, v, vo, dkp, scale, bias, name):
    T = ATT_T
    assert qo % ATT_PP == 0 and ko % ATT_PP == 0 and vo % ATT_PP == 0
    qo, ko, vo = qo // ATT_PP, ko // ATT_PP, vo // ATT_PP

    def body(*refs):
        if bias is not None:
            q_ref, k_ref, v_ref, b_ref, o_ref, lse_ref, s_scr = refs
        else:
            q_ref, k_ref, v_ref, o_ref, lse_ref, s_scr = refs
        i = pl.program_id(1)
        row = lax.broadcasted_iota(jnp.int32, (T, T), 0)
        col = lax.broadcasted_iota(jnp.int32, (T, T), 1)
        qms = []
        for a, hh in ATT_CHAINS:
            qb = _pair(q_ref, a, dkp)
            qms.append(jnp.where(_head_mask(dkp, hh), qb, jnp.zeros_like(qb)))

        def fold(t):
            return [t[:, c * LANES:(c + 1) * LANES] for c in range(T // LANES)]

        def run(nt):
            mls = [jnp.full((T, LANES), -jnp.inf, F32) for _ in ATT_CHAINS]
            for j in range(nt):
                ks = slice(j * T, (j + 1) * T)
                for ci, (a, hh) in enumerate(ATT_CHAINS):
                    s = lax.dot_general(qms[ci], _pair(k_ref, a, dkp, ks), NT, preferred_element_type=F32) * (scale * LOG2E)
                    if bias is not None:
                        s = s + b_ref[2 * a + hh, j] * LOG2E
                    if j == nt - 1:
                        s = jnp.where(row >= col, s, -jnp.inf)
                    s_scr[ci, j] = s
                    for part in fold(s):
                        mls[ci] = jnp.maximum(mls[ci], part)
            ms = [jnp.max(ml, axis=1, keepdims=True) for ml in mls]
            mbs = [jnp.broadcast_to(m, (T, LANES)) for m in ms]
            for a in range(ATT_PP):
                ls = [jnp.zeros((T, LANES), F32) for _ in range(2)]
                ps, vms = [], []
                for j in range(nt):
                    vb = _pair(v_ref, a, LANES, slice(j * T, (j + 1) * T))
                    for hh in range(2):
                        parts = [jnp.exp2(part - mbs[2 * a + hh]) for part in fold(s_scr[2 * a + hh, j])]
                        for part in parts:
                            ls[hh] = ls[hh] + part
                        ps.append(jnp.concatenate(parts, axis=1).astype(BF16))
                        vms.append(jnp.where(_head_mask(LANES, hh), vb, jnp.zeros_like(vb)))
                acc = lax.dot_general(jnp.concatenate(ps, axis=1), jnp.concatenate(vms, axis=0), NN,
                                      preferred_element_type=F32)
                l0, l1 = [jnp.sum(l, axis=1, keepdims=True) for l in ls]
                lse_ref[2 * a] = ms[2 * a] + jnp.log2(l0)
                lse_ref[2 * a + 1] = ms[2 * a + 1] + jnp.log2(l1)
                inv = jnp.where(_head_mask(LANES, 0), 1.0 / l0, 1.0 / l1)
                o_ref[:, a * LANES:(a + 1) * LANES] = (acc * inv).astype(o_ref.dtype)

        for nt in range(1, N_ATT + 1):
            pl.when(i == nt - 1)(functools.partial(run, nt))

    in_specs = [
        pl.BlockSpec((T, ATT_PP * dkp), lambda g, i: (i, qo + g)),
        pl.BlockSpec((S, ATT_PP * dkp), lambda g, i: (0, ko + g)),
        pl.BlockSpec((S, ATT_PP * LANES), lambda g, i: (0, vo + g)),
    ]
    args = [q, k, v]
    if bias is not None:
        in_specs.append(pl.BlockSpec((2 * ATT_PP, N_ATT, 1, T), lambda g, i: (g, 0, 0, 0)))
        args.append(bias)
    return pl.pallas_call(
        body, name=name, grid=(ATT_G, N_ATT),
        in_specs=in_specs,
        out_specs=[pl.BlockSpec((T, ATT_PP * LANES), lambda g, i: (i, g)),
                   pl.BlockSpec((2 * ATT_PP, T, 1), lambda g, i: (g, i, 0))],
        out_shape=[jax.ShapeDtypeStruct((S, HEADS * HEAD_DIM), BF16),
                   jax.ShapeDtypeStruct((HEADS, S, 1), F32)],
        scratch_shapes=[pltpu.VMEM((len(ATT_CHAINS), N_ATT, T, T), F32)],
        compiler_params=_params(("parallel", "arbitrary")),
    )(*args)


def _attn_grad(q, qo, k, ko, v, vo, do, lse, dkp, scale, bias, qk_dtype, name):
    T = ATT_T
    has_b = bias is not None
    qo, ko, vo = qo // ATT_PP, ko // ATT_PP, vo // ATT_PP
    n_ch = len(ATT_CHAINS)

    def body(*refs):
        q_ref, k_ref, v_ref, do_ref, lse_ref = refs[:5]
        refs = refs[5:]
        if has_b:
            b_ref, refs = refs[0], refs[1:]
        dq_ref, dk_ref, dv_ref = refs[:3]
        refs = refs[3:]
        if has_b:
            db_ref, refs = refs[0], refs[1:]
        p_scr, dp_scr, dk_acc, dv_acc = refs[:4]
        db_acc = refs[4] if has_b else None
        i = pl.program_id(1)

        @pl.when(i == 0)
        def _():
            dk_acc[...] = jnp.zeros(dk_acc.shape, F32)
            dv_acc[...] = jnp.zeros(dv_acc.shape, F32)
            if has_b:
                db_acc[...] = jnp.zeros(db_acc.shape, F32)

        row = lax.broadcasted_iota(jnp.int32, (T, T), 0)
        col = lax.broadcasted_iota(jnp.int32, (T, T), 1)

        def fold(t):
            return [t[:, c * LANES:(c + 1) * LANES] for c in range(T // LANES)]

        qms, doms, lses = [], [], []
        for a, hh in ATT_CHAINS:
            qb, dob = _pair(q_ref, a, dkp), _pair(do_ref, a, LANES)
            qms.append(jnp.where(_head_mask(dkp, hh), qb, jnp.zeros_like(qb)))
            doms.append(jnp.where(_head_mask(LANES, hh), dob, jnp.zeros_like(dob)))
            lses.append(lse_ref[2 * a + hh])

        def run(nt):
            dls = [jnp.zeros((T, LANES), F32) for _ in ATT_CHAINS]
            for j in range(nt):
                ks = slice(j * T, (j + 1) * T)
                for ci, (a, hh) in enumerate(ATT_CHAINS):
                    s = lax.dot_general(qms[ci], _pair(k_ref, a, dkp, ks), NT, preferred_element_type=F32) * (scale * LOG2E)
                    if has_b:
                        s = s + b_ref[ci, j] * LOG2E
                    s = s - lses[ci]
                    if j == nt - 1:
                        s = jnp.where(row >= col, s, -jnp.inf)
                    p = jnp.exp2(s)
                    dp = lax.dot_general(doms[ci], _pair(v_ref, a, LANES, ks), NT, preferred_element_type=F32)
                    p_scr[ci, j] = p
                    dp_scr[ci, j] = dp
                    for part in fold(p * dp):
                        dls[ci] = dls[ci] + part
            deltas = [jnp.broadcast_to(jnp.sum(dl, axis=1, keepdims=True), (T, LANES)) for dl in dls]
            for a in range(ATT_PP):
                ds_all, km_all = [], []
                qm2t = jnp.transpose(jnp.concatenate([qms[2 * a], qms[2 * a + 1]], axis=0))
                dom2t = jnp.transpose(jnp.concatenate([doms[2 * a], doms[2 * a + 1]], axis=0))
                for j in range(nt):
                    ks = slice(j * T, (j + 1) * T)
                    kb = _pair(k_ref, a, dkp, ks)
                    p2, ds2 = [], []
                    for hh in range(2):
                        ci = 2 * a + hh
                        p = p_scr[ci, j]
                        ds = jnp.concatenate([pp * (dd - deltas[ci]) for pp, dd in zip(fold(p), fold(dp_scr[ci, j]))], axis=1)
                        if has_b:
                            db_acc[ci, j] += jnp.sum(ds, axis=0, keepdims=True)
                        p2.append(p.astype(BF16))
                        ds2.append((ds * scale).astype(BF16))
                        km_all.append(jnp.where(_head_mask(dkp, hh), kb, jnp.zeros_like(kb)))
                    dv_acc[a * LANES:(a + 1) * LANES, ks] += lax.dot_general(
                        dom2t, jnp.concatenate(p2, axis=0), NN, preferred_element_type=F32)
                    dk_acc[a * dkp:(a + 1) * dkp, ks] += lax.dot_general(
                        qm2t, jnp.concatenate(ds2, axis=0), NN, preferred_element_type=F32)
                    ds_all += ds2
                dq = lax.dot_general(jnp.concatenate(ds_all, axis=1), jnp.concatenate(km_all, axis=0), NN,
                                     preferred_element_type=F32)
                dq_ref[:, a * dkp:(a + 1) * dkp] = dq.astype(dq_ref.dtype)

        for nt in range(1, N_ATT + 1):
            pl.when(i == nt - 1)(functools.partial(run, nt))

        @pl.when(i == N_ATT - 1)
        def _():
            dk_ref[...] = jnp.transpose(dk_acc[...]).astype(dk_ref.dtype)
            dv_ref[...] = jnp.transpose(dv_acc[...]).astype(dv_ref.dtype)
            if has_b:
                db_ref[...] = db_acc[...]

    in_specs = [
        pl.BlockSpec((T, ATT_PP * dkp), lambda g, i: (i, qo + g)),
        pl.BlockSpec((S, ATT_PP * dkp), lambda g, i: (0, ko + g)),
        pl.BlockSpec((S, ATT_PP * LANES), lambda g, i: (0, vo + g)),
        pl.BlockSpec((T, ATT_PP * LANES), lambda g, i: (i, g)),
        pl.BlockSpec((2 * ATT_PP, T, 1), lambda g, i: (g, i, 0)),
    ]
    args = [q, k, v, do, lse]
    out_specs = [
        pl.BlockSpec((T, ATT_PP * dkp), lambda g, i: (i, g)),
        pl.BlockSpec((S, ATT_PP * dkp), lambda g, i: (0, g)),
        pl.BlockSpec((S, ATT_PP * LANES), lambda g, i: (0, g)),
    ]
    width = (HEADS // 2) * dkp
    out_shape = [
        jax.ShapeDtypeStruct((S, width), qk_dtype),
        jax.ShapeDtypeStruct((S, width), qk_dtype),
        jax.ShapeDtypeStruct((S, HEADS * HEAD_DIM), BF16),
    ]
    scratch = [pltpu.VMEM((n_ch, N_ATT, T, T), F32), pltpu.VMEM((n_ch, N_ATT, T, T), F32),
               pltpu.VMEM((ATT_PP * dkp, S), F32), pltpu.VMEM((ATT_PP * LANES, S), F32)]
    if has_b:
        bspec = pl.BlockSpec((2 * ATT_PP, N_ATT, 1, T), lambda g, i: (g, 0, 0, 0))
        in_specs.append(bspec)
        args.append(bias)
        out_specs.append(bspec)
        out_shape.append(jax.ShapeDtypeStruct((HEADS, N_ATT, 1, T), F32))
        scratch.append(pltpu.VMEM((2 * ATT_PP, N_ATT, 1, T), F32))
    return pl.pallas_call(
        body, name=name, grid=(ATT_G, N_ATT),
        in_specs=in_specs, out_specs=out_specs, out_shape=out_shape, scratch_shapes=scratch,
        compiler_params=_params(("parallel", "arbitrary")),
    )(*args)


def _tri(upper):
    a = lax.broadcasted_iota(jnp.int32, (LANES, LANES), 0)
    b = lax.broadcasted_iota(jnp.int32, (LANES, LANES), 1)
    return jnp.where(a <= b if upper else a >= b, 1.0, 0.0).astype(F32)


def _fox_gates(proj, blk, bf):
    def body(m_ref, b_ref, z_out, o_ref):
        tri = _tri(True)
        carry = jnp.zeros((HEADS, 1), F32)
        for t in range(S // LANES):
            sl = slice(t * LANES, (t + 1) * LANES)
            zt = jnp.transpose(m_ref[sl, :])[:HEADS]
            z_out[:, sl] = zt
            z = zt + b_ref[...]
            logf = jnp.minimum(z, 0.0) - jnp.log(1.0 + jnp.exp(-jnp.abs(z)))
            c = lax.dot_general(logf, tri, NN, preferred_element_type=F32,
                                precision=lax.Precision.HIGHEST) + carry
            o_ref[:, sl] = -c
            carry = c[:, LANES - 1:LANES]

    return pl.pallas_call(
        body, name="fox_gates", grid=(1,),
        in_specs=[pl.BlockSpec((S, LANES), lambda i: (0, blk)), pl.BlockSpec(bf.shape, lambda i: (0, 0))],
        out_specs=[pl.BlockSpec((HEADS, S), lambda i: (0, 0))] * 2,
        out_shape=[jax.ShapeDtypeStruct((HEADS, S), F32)] * 2,
        compiler_params=_params(("arbitrary",)),
    )(proj, bf)


def _fox_gates_bwd(dbias, zt, bf):
    def body(d_ref, z_ref, b_ref, dz_ref, dbf_ref):
        tri = _tri(False)
        carry = jnp.zeros((HEADS, 1), F32)
        tot = jnp.zeros((HEADS, 1), F32)
        for t in reversed(range(S // LANES)):
            sl = slice(t * LANES, (t + 1) * LANES)
            df = -d_ref[:, sl]
            c = lax.dot_general(df, tri, NN, preferred_element_type=F32,
                                precision=lax.Precision.HIGHEST) + carry
            carry = c[:, 0:1]
            z = z_ref[:, sl] + b_ref[...]
            dz = c * _sigmoid(-z)
            dz_ref[:, sl] = dz
            tot = tot + jnp.sum(dz, axis=1, keepdims=True)
        dbf_ref[...] = tot

    return pl.pallas_call(
        body, name="fox_gates_bwd",
        out_shape=[jax.ShapeDtypeStruct((HEADS, S), F32), jax.ShapeDtypeStruct((HEADS, 1), F32)],
        compiler_params=_params(),
    )(dbias, zt, bf)


def _mod_part(c_all, w_ada, b_cols):
    def body(c_ref, w_ref, b_ref, o_ref, s_ref):
        c = c_ref[...]
        sc = c * _sigmoid(c)
        s_ref[...] = sc
        o_ref[...] = lax.dot_general(sc, w_ref[...], NN, preferred_element_type=F32,
                                     precision=lax.Precision.HIGHEST) + b_ref[...]

    return pl.pallas_call(
        body, name="mod_part",
        out_shape=[jax.ShapeDtypeStruct((N_DEV, w_ada.shape[1]), F32), jax.ShapeDtypeStruct(c_all.shape, F32)],
        compiler_params=_params(),
    )(c_all, w_ada, b_cols)


def _w_ada_grad(sc_t, dm):
    def body(s_ref, d_ref, o_ref):
        acc = jnp.zeros(o_ref.shape, F32)
        for b in range(N_DEV):
            acc = acc + s_ref[:, b:b + 1] * d_ref[b:b + 1, :]
        o_ref[...] = acc

    return pl.pallas_call(
        body, name="w_ada_grad", out_shape=jax.ShapeDtypeStruct((sc_t.shape[0], dm.shape[1]), F32),
        compiler_params=_params(),
    )(sc_t, dm)


def _adamw(w, m, v, parts, name, own=None, slot=None):
    rows, cols = w.shape
    n = parts.shape[0]
    tr = rows if rows <= 512 else 256

    def body(*refs):
        if own is not None:
            s_ref, refs = refs[0], refs[1:]
            w_ref, m_ref, v_ref, p_ref, o_ref, g_out, d_out, m_out, v_out = refs
            terms = [jnp.where(s_ref[0] == kk, o_ref[0], p_ref[kk]) for kk in range(n)]
        else:
            w_ref, m_ref, v_ref, p_ref, g_out, d_out, m_out, v_out = refs
            terms = [p_ref[kk] for kk in range(n)]
        g = terms[0].astype(F32)
        for term in terms[1:]:
            g = g + term.astype(F32)
        g_out[...] = g
        d_out[...], m_out[...], v_out[...] = _adamw_math(w_ref[...], g, m_ref[...], v_ref[...])

    spec = pl.BlockSpec((tr, cols), lambda i, *_: (i, 0))
    in_specs = [spec, spec, spec, pl.BlockSpec((n, tr, cols), lambda i, *_: (0, i, 0))]
    out_shape = [jax.ShapeDtypeStruct((rows, cols), F32)] * 4
    if own is None:
        return pl.pallas_call(
            body, name=name, grid=(rows // tr,), in_specs=in_specs, out_specs=[spec] * 4, out_shape=out_shape,
            compiler_params=_params(("parallel",)),
        )(w, m, v, parts)
    in_specs.append(pl.BlockSpec((1, tr, cols), lambda i, s: (s[0], i, 0)))
    return pl.pallas_call(
        body, name=name, out_shape=out_shape, compiler_params=_params(("parallel",)),
        grid_spec=pltpu.PrefetchScalarGridSpec(num_scalar_prefetch=1, grid=(rows // tr,), in_specs=in_specs,
                                               out_specs=[spec] * 4),
    )(slot, w, m, v, parts, own)


def _adamw_math(w, g, m, v):
    mm = ADAM_B1 * m + (1.0 - ADAM_B1) * g
    vv = ADAM_B2 * v + (1.0 - ADAM_B2) * (g * g)
    m_hat = mm / (1.0 - ADAM_B1 ** ADAM_STEP)
    v_hat = vv / (1.0 - ADAM_B2 ** ADAM_STEP)
    return -ADAM_LR * (m_hat / (jnp.sqrt(v_hat) + ADAM_EPS) + ADAM_WD * w), mm, vv


def _adamw_rows(bundles, offsets, ws, ms, vs, err_off, err_width):
    k = len(ws)

    def body(*refs):
        b_ref = refs[0]
        w_refs, m_refs, v_refs = refs[1:1 + k], refs[1 + k:1 + 2 * k], refs[1 + 2 * k:1 + 3 * k]
        outs = refs[1 + 3 * k:]
        g_all = b_ref[0]
        for kk in range(1, N_DEV):
            g_all = g_all + b_ref[kk]
        for i in range(k):
            width = w_refs[i].shape[1]
            g = g_all[:, offsets[i]:offsets[i] + width]
            outs[4 * i][...] = g
            outs[4 * i + 1][...], outs[4 * i + 2][...], outs[4 * i + 3][...] = _adamw_math(
                w_refs[i][...], g, m_refs[i][...], v_refs[i][...])
        outs[4 * k][...] = g_all[:, err_off:err_off + err_width]

    out_shape = []
    for w_ in ws:
        out_shape += [jax.ShapeDtypeStruct(w_.shape, F32)] * 4
    out_shape.append(jax.ShapeDtypeStruct((1, err_width), F32))
    res = pl.pallas_call(body, name="adamw_rows", out_shape=out_shape, compiler_params=_params())(bundles, *ws, *ms, *vs)
    return [tuple(res[4 * i:4 * i + 4]) for i in range(k)], res[-1]


def _coords():
    return lax.axis_index("x"), lax.axis_index("y"), lax.axis_index("c")


def _flat(px, py, pc):
    return 4 * px + 2 * py + pc


def _all_gather(arrs, name):
    n = len(arrs)

    def body(*refs):
        ins, outs = refs[:n], refs[n:2 * n]
        send, recv, lsem = refs[2 * n:]
        x, y, c = _coords()
        me, sibling = (x, y, c), (x, y, 1 - c)
        chips = [(1 - x, y), (x, 1 - y), (1 - x, 1 - y)]

        def copy(a, kk, block, to, src=None):
            slot = outs[a].at[_flat(*block)]
            return pltpu.make_async_remote_copy(
                src_ref=slot if src is None else src, dst_ref=slot,
                send_sem=send.at[a, kk], recv_sem=recv.at[a, kk],
                device_id=to, device_id_type=MESH)

        mine = [pltpu.make_async_copy(ins[a], outs[a].at[_flat(*me)], lsem.at[a]) for a in range(n)]
        for cp in mine:
            cp.start()
        first = []
        for a in range(n):
            first.append(copy(a, 0, me, sibling, src=ins[a]))
            first += [copy(a, 1 + j, me, (*chip, c), src=ins[a]) for j, chip in enumerate(chips)]
        for cp in first:
            cp.start()
        passed = []
        for j, chip in enumerate(chips):
            for a in range(n):
                copy(a, 1 + j, (*chip, c), me).wait_recv()
                cp = copy(a, 4 + j, (*chip, c), sibling)
                cp.start()
                passed.append(cp)
        for a in range(n):
            copy(a, 0, sibling, me).wait_recv()
        for j, chip in enumerate(chips):
            for a in range(n):
                copy(a, 4 + j, (*chip, 1 - c), me).wait_recv()
        for cp in first + passed:
            cp.wait_send()
        for cp in mine:
            cp.wait()

    any_spec = pl.BlockSpec(memory_space=pl.ANY)
    return pl.pallas_call(
        body, name=name,
        in_specs=[any_spec] * n, out_specs=[any_spec] * n,
        out_shape=[jax.ShapeDtypeStruct((N_DEV,) + a.shape, a.dtype) for a in arrs],
        scratch_shapes=[pltpu.SemaphoreType.DMA((n, 7)), pltpu.SemaphoreType.DMA((n, 7)),
                        pltpu.SemaphoreType.DMA((n,))],
    )(*arrs)


def _peer_list():
    x, y, c = _coords()
    return [((1 - x if r & 4 else x), (1 - y if r & 2 else y), (1 - c if r & 1 else c)) for r in range(1, N_DEV)]


def _copy_plan(mode, src, land):
    x, y, c = _coords()
    me = _flat(x, y, c)
    if mode == "gather":
        return [(src, land.at[me], peer) for peer in _peer_list()]
    if mode == "exchange":
        return [(src.at[_flat(*peer)], land.at[me], peer) for peer in _peer_list()]
    if mode == "pair":
        return [(src.at[_flat(q // 2, q % 2, 1 - c)], land.at[q], (x, y, 1 - c)) for q in range(N_DEV // 2)]
    chips = [((1 - x if r & 2 else x), (1 - y if r & 1 else y)) for r in range(1, N_DEV // 2)]
    if mode == "chips":
        return [(src.at[2 * qx + qy], land.at[2 * x + y], (qx, qy, c)) for qx, qy in chips]
    if mode == "spread":
        return [(src, land.at[me], (x, y, 1 - c))] + [(src, land.at[me], (qx, qy, c)) for qx, qy in chips]
    assert mode == "forward"
    return [(land.at[_flat(qx, qy, c)], land.at[_flat(qx, qy, c)], (x, y, 1 - c)) for qx, qy in chips]


N_COPIES = dict(gather=N_DEV - 1, exchange=N_DEV - 1, pair=N_DEV // 2, chips=N_DEV // 2 - 1, spread=N_DEV // 2,
                forward=N_DEV // 2 - 1)


def _land_shape(mode, shape):
    return {"gather": (N_DEV,) + shape, "spread": (N_DEV,) + shape, "exchange": shape,
            "pair": (N_DEV // 2,) + shape[1:], "chips": shape}[mode]


HBM_SPEC = pl.BlockSpec(memory_space=pltpu.HBM)
SEM_SPEC = pl.BlockSpec(memory_space=pltpu.SEMAPHORE)
ANY_SPEC = pl.BlockSpec(memory_space=pl.ANY)
SIDE_EFFECT = pltpu.SideEffectType.DATAFLOW_SIDE_EFFECTING


def _async_start(groups, modes, after, name):
    modes = [modes] * len(groups) if isinstance(modes, str) else list(modes)
    arrs = [(a, m) for g, m in zip(groups, modes) for a in g]
    n = len(arrs)
    fresh = [i for i, (_, m) in enumerate(arrs) if m != "forward"]

    def body(*refs):
        srcs, new_lands = refs[:n], refs[n:n + len(fresh)]
        outs = refs[n + len(fresh) + 1:]
        lands = list(srcs)
        for k, i in enumerate(fresh):
            lands[i] = new_lands[k]
        for ai, (_, mode) in enumerate(arrs):
            for src_ref, dst_ref, peer in _copy_plan(mode, srcs[ai], lands[ai]):
                pltpu.make_async_remote_copy(src_ref=src_ref, dst_ref=dst_ref, send_sem=outs[2 * ai],
                                             recv_sem=outs[2 * ai + 1], device_id=peer, device_id_type=MESH).start()
        outs[-1][...] = jnp.zeros(outs[-1].shape, F32)

    land_shapes = [(_land_shape(arrs[i][1], arrs[i][0].shape), arrs[i][0].dtype) for i in fresh]
    n_buf = n + len(fresh)
    out_shape = [pltpu.SemaphoreType.DMA(())] * (2 * n)
    out_shape += [pltpu.HBM(a.shape, a.dtype) for a, _ in arrs]
    out_shape += [pltpu.HBM(shape, dt) for shape, dt in land_shapes]
    out_shape.append(jax.ShapeDtypeStruct((8, LANES), F32))
    res = pl.pallas_call(
        body, name=name, out_shape=tuple(out_shape),
        in_specs=[HBM_SPEC] * n_buf + [ANY_SPEC],
        out_specs=tuple([SEM_SPEC] * (2 * n) + [HBM_SPEC] * n_buf + [pl.BlockSpec(memory_space=pltpu.VMEM)]),
        input_output_aliases={i: 2 * n + i for i in range(n_buf)},
        compiler_params=pltpu.CompilerParams(has_side_effects=SIDE_EFFECT),
    )(*[pltpu.with_memory_space_constraint(a, pltpu.HBM) for a, _ in arrs],
      *[pltpu.with_memory_space_constraint(lax.empty(shape, dt), pltpu.HBM) for shape, dt in land_shapes],
      after)
    sems, thru = res[:2 * n], res[2 * n:-1]
    land_of = {i: thru[n + k] for k, i in enumerate(fresh)}
    states, idx = [], 0
    for g, mode in zip(groups, modes):
        ids = range(idx, idx + len(g))
        idx += len(g)
        states.append(([sems[2 * i] for i in ids], [sems[2 * i + 1] for i in ids],
                       None if mode == "forward" else [thru[i] for i in ids],
                       [land_of.get(i, thru[i]) for i in ids], mode))
    return states, res[-1]


def _async_wait(state, after, name):
    sends, recvs, srcs, lands, mode = state
    g = len(lands)
    bufs = (list(srcs) if srcs is not None else []) + list(lands)
    nb = len(bufs)

    def body(*refs):
        l_refs, sems = refs[nb - g:nb], refs[nb:nb + 2 * g]
        for ai in range(g):
            moved = l_refs[ai].at[pl.ds(0, N_COPIES[mode])]
            cp = pltpu.make_async_remote_copy(src_ref=moved, dst_ref=moved, send_sem=sems[ai], recv_sem=sems[g + ai],
                                              device_id=_coords(), device_id_type=MESH)
            cp.wait_send()
            cp.wait_recv()

    res = pl.pallas_call(
        body, name=name,
        out_shape=tuple(pltpu.HBM(a.shape, a.dtype) for a in bufs),
        in_specs=[HBM_SPEC] * nb + [SEM_SPEC] * (2 * g) + [ANY_SPEC],
        out_specs=tuple([HBM_SPEC] * nb),
        input_output_aliases={i: i for i in range(nb)},
        compiler_params=pltpu.CompilerParams(has_side_effects=SIDE_EFFECT),
    )(*bufs, *sends, *recvs, after)
    return (list(res[:nb - g]) if srcs is not None else None), list(res[nb - g:])


def _add_blocks(a, b):
    def body(a_ref, b_ref, o_ref):
        o_ref[...] = (a_ref[...].astype(F32) + b_ref[...].astype(F32)).astype(o_ref.dtype)

    spec = pl.BlockSpec((1,) + a.shape[1:], lambda i: (i, 0, 0))
    return pl.pallas_call(
        body, name="add_blocks", grid=(a.shape[0],), in_specs=[spec, spec], out_specs=spec,
        out_shape=jax.ShapeDtypeStruct(a.shape, a.dtype), compiler_params=_params(("parallel",)),
    )(a, b)


def _with_own(land, own, me):
    return lax.dynamic_update_index_in_dim(land, own, me, 0)


IN_SPLITS = (512, 512, 512, 8, 768, 256, 32, 1024, 1024)


def _from_shards(g, fn, out_widths, name, own=None, slot=None):
    _, k, n = g.shape
    tr = min(k, 256)

    def body(*refs):
        if own is not None:
            s_ref, g_ref, own_ref = refs[:3]
            cols = [jnp.where(s_ref[0] == j, own_ref[...], g_ref[j]) for j in range(N_DEV)]
        else:
            g_ref = refs[0]
            cols = [g_ref[j] for j in range(N_DEV)]
        for o_ref, val in zip(refs[-len(out_widths):], fn(jnp.concatenate(cols, axis=1))):
            o_ref[...] = val

    in_specs = [pl.BlockSpec((N_DEV, tr, n), lambda i, *_: (0, i, 0))]
    out_spec = [pl.BlockSpec((tr, wd), lambda i, *_: (i, 0)) for wd in out_widths]
    out_shape = [jax.ShapeDtypeStruct((k, wd), g.dtype) for wd in out_widths]
    if own is None:
        return pl.pallas_call(body, name=name, grid=(k // tr,), in_specs=in_specs, out_specs=out_spec,
                              out_shape=out_shape, compiler_params=_params(("parallel",)))(g)
    in_specs.append(pl.BlockSpec((tr, n), lambda i, *_: (i, 0)))
    return pl.pallas_call(
        body, name=name, out_shape=out_shape, compiler_params=_params(("parallel",)),
        grid_spec=pltpu.PrefetchScalarGridSpec(num_scalar_prefetch=1, grid=(k // tr,), in_specs=in_specs, out_specs=out_spec),
    )(slot, g, own)


def _unshard_cols(g, own=None, slot=None):
    return _from_shards(g, lambda full: (full,), [N_DEV * g.shape[2]], "unshard_cols_%d" % g.shape[2], own, slot)[0]


FFN_T = 256
FFN_SHARD = 2 * D_FF // N_DEV


def _unshard_ffn_in(g, own=None, slot=None):
    def pairs(full):
        parts = []
        for j in range(D_FF // FFN_T):
            parts += [full[:, j * FFN_T:(j + 1) * FFN_T], full[:, D_FF + j * FFN_T:D_FF + (j + 1) * FFN_T]]
        return (jnp.concatenate(parts, axis=1),)

    return _from_shards(g, pairs, [2 * D_FF], "unshard_ffn_in", own, slot)[0]


def _shard_ffn_in(w):
    tr = 256

    def body(w_ref, o_ref):
        x = w_ref[...]
        nb = D_FF // FFN_T
        full = jnp.concatenate([x[:, (2 * j + half) * FFN_T:(2 * j + half + 1) * FFN_T]
                                for half in range(2) for j in range(nb)], axis=1)
        for j in range(N_DEV):
            o_ref[j] = full[:, j * FFN_SHARD:(j + 1) * FFN_SHARD]

    return pl.pallas_call(
        body, name="shard_ffn_in", grid=(D // tr,),
        in_specs=[pl.BlockSpec((tr, 2 * D_FF), lambda i: (i, 0))],
        out_specs=pl.BlockSpec((N_DEV, tr, FFN_SHARD), lambda i: (0, i, 0)),
        out_shape=jax.ShapeDtypeStruct((N_DEV, D, FFN_SHARD), w.dtype),
        compiler_params=_params(("parallel",)),
    )(w)


def _shard_cols(w):
    k, n = w.shape[0], w.shape[1] // N_DEV
    tr = min(k, 256)

    def body(w_ref, o_ref):
        full = w_ref[...]
        for j in range(N_DEV):
            o_ref[j] = full[:, j * n:(j + 1) * n]

    return pl.pallas_call(
        body, name="shard_cols_%d" % n, grid=(k // tr,),
        in_specs=[pl.BlockSpec((tr, N_DEV * n), lambda i: (i, 0))],
        out_specs=pl.BlockSpec((N_DEV, tr, n), lambda i: (0, i, 0)),
        out_shape=jax.ShapeDtypeStruct((N_DEV, k, n), w.dtype),
        compiler_params=_params(("parallel",)),
    )(w)


IN_OFFS = tuple(sum(IN_SPLITS[:i]) for i in range(len(IN_SPLITS) + 1))
IN_SHARD = IN_OFFS[-1] // N_DEV
REGROUP_ROWS = 128


def _w_in_regroup(g, own=None, slot=None):
    def groups(full):
        fq, fk, fv, wf, cq, ckv, kr, gf, gm = [full[:, IN_OFFS[i]:IN_OFFS[i + 1]] for i in range(9)]
        rows = full.shape[0]
        w_a = jnp.concatenate([cq, ckv, gf, gm, wf, jnp.zeros((rows, 56), BF16), kr, jnp.zeros((rows, 32), BF16)], axis=1)
        return w_a, jnp.concatenate([fq, fk, fv], axis=1)

    return _from_shards(g, groups, [3200, 1536], "w_in_regroup", own, slot)


def _w_in_ungroup(da, db_):
    def body(a_ref, b_ref, o_ref):
        a = a_ref[...]
        full = jnp.concatenate([b_ref[...], a[:, 3072:3080], a[:, 0:768], a[:, 768:1024], a[:, 3136:3168],
                                a[:, 1024:3072]], axis=1)
        for j in range(N_DEV):
            o_ref[j] = full[:, j * IN_SHARD:(j + 1) * IN_SHARD]

    tr = REGROUP_ROWS
    return pl.pallas_call(
        body, name="w_in_ungroup", grid=(D // tr,),
        in_specs=[pl.BlockSpec((tr, 3200), lambda i: (i, 0)), pl.BlockSpec((tr, 1536), lambda i: (i, 0))],
        out_specs=pl.BlockSpec((N_DEV, tr, IN_SHARD), lambda i: (0, i, 0)),
        out_shape=jax.ShapeDtypeStruct((N_DEV, D, IN_SHARD), BF16),
        compiler_params=_params(("parallel",)),
    )(da, db_)


def _prepare_weights(g, own=None, slot=None):
    w = {}
    if own is not None:
        small = ("w_uq", "w_ukv", "w_out", "w_ffn_out")
        g = {n: (_with_own(a, own[n], slot[0]) if n in small else a) for n, a in g.items()}
    pick = (lambda n: (own[n], slot)) if own is not None else (lambda n: (None, None))
    if "w_in" in g:
        w["w_a"], w["w_b"] = _w_in_regroup(g["w_in"], *pick("w_in"))
    if "w_uq" in g:
        w_uq = g["w_uq"].reshape(Q_LORA, HEADS, 96)
        w["w_uq"] = jnp.pad(w_uq, ((0, 0), (0, 0), (0, 32))).reshape(Q_LORA, HEADS * LANES)
        ukv = g["w_ukv"]
        w["w_k"] = jnp.transpose(jnp.pad(ukv[:, :, :64], ((0, 0), (0, 0), (0, 64))), (1, 0, 2)).reshape(KV_LORA, HEADS * LANES)
        w["w_v"] = jnp.transpose(ukv[:, :, 64:], (1, 0, 2)).reshape(KV_LORA, HEADS * HEAD_DIM)
    if "w_out" in g:
        w["w_pf"] = _unshard_cols(g["w_proj_fox"], *pick("w_proj_fox"))
        w["w_pm"] = _unshard_cols(g["w_proj_mla"], *pick("w_proj_mla"))
        w["w_out"] = g["w_out"].reshape(D, D)
    if "w_ffn_in" in g:
        w["w_ffn_in"] = _unshard_ffn_in(g["w_ffn_in"], *pick("w_ffn_in"))
        w["w_ffn_out"] = g["w_ffn_out"].reshape(D_FF, D)
    return w


def _shard_grads(dw):
    out = {}
    if "w_a" in dw:
        out["w_in"] = _w_in_ungroup(dw["w_a"], dw["w_b"])
    if "w_uq" in dw:
        w_uq = dw["w_uq"].reshape(Q_LORA, HEADS, LANES)[:, :, :96].reshape(Q_LORA, Q_LORA)
        out["w_uq"] = w_uq.reshape(N_DEV, Q_LORA // N_DEV, Q_LORA)
        k_part = dw["w_k"].reshape(KV_LORA, HEADS, LANES)[:, :, :64]
        v_part = dw["w_v"].reshape(KV_LORA, HEADS, HEAD_DIM)
        out["w_ukv"] = jnp.transpose(jnp.concatenate([k_part, v_part], axis=2), (1, 0, 2))
    if "w_out" in dw:
        out["w_proj_fox"] = _shard_cols(dw["w_pf"])
        out["w_proj_mla"] = _shard_cols(dw["w_pm"])
        out["w_out"] = dw["w_out"].reshape(N_DEV, D // N_DEV, D)
    if "w_ffn_in" in dw:
        out["w_ffn_in"] = _shard_ffn_in(dw["w_ffn_in"])
        out["w_ffn_out"] = dw["w_ffn_out"].reshape(N_DEV, D_FF // N_DEV, D)
    return out


def _fwd_bwd(x, pos, mod, target, w, vec, wts, send, relay):
    shift_mix, scale_mix, gate_mix, shift_ffn, scale_ffn, gate_ffn = [mod[:, i * D:(i + 1) * D] for i in range(6)]
    g_pre_mix, g_post_mix, g_pre_ffn, g_post_ffn = vec["g_pre_mix"], vec["g_post_mix"], vec["g_pre_ffn"], vec["g_post_ffn"]
    g_q, g_kv = vec["g_q_lora"], vec["g_kv_lora"]

    inv_freq = 1.0 / (ROPE_THETA ** (jnp.arange(0, ROPE_DIM, 2, dtype=F32) / ROPE_DIM))
    invf = jnp.concatenate([jnp.zeros((64,), F32), inv_freq, inv_freq, jnp.zeros((32,), F32)]).reshape(1, LANES)
    ct, sa, sb = _rope_tables(pos, invf)

    def pre1(xv, g, sc, sh):
        return (xv * _rstd(xv) * g) * (1.0 + sc) + sh
    proj_a, h = _mm_epi(x, w["w_a"], "nn", 640, lambda r: ((r,), ()), "in_proj_a", 1024, outs=[(3200, 640, F32)],
                        pro=(pre1, [g_pre_mix, scale_mix, shift_mix], 0))
    qkv = _mm(h, w["w_b"], "nn", BF16, "in_proj_b")

    def lora_norm(cv, g):
        return cv * _rstd(cv) * g
    w = {**w, **wts("lora", qkv)}
    tables = [(ct, LANES), (sa, LANES), (sb, LANES)]

    def rope_q(qv, c_, a_, b_):
        return (jnp.concatenate([_rope(qv[:, hd * LANES:(hd + 1) * LANES], c_, a_, b_) for hd in range(HEADS)], axis=1),), ()
    q_m, cqn = _mm_epi(proj_a, w["w_uq"], "nn", D, rope_q, "mla_uq", 512, rows=tables, outs=[(D, D, BF16)],
                       pro=(lora_norm, [g_q], 0))

    def rope_k(kv, misc, c_, a_, b_):
        lane = lax.broadcasted_iota(jnp.int32, (1, LANES), 1)
        kpe = jnp.where((lane >= 64) & (lane < 96), _rope(misc, c_, a_, b_), 0.0)
        return (jnp.concatenate([kv[:, hd * LANES:(hd + 1) * LANES] + kpe for hd in range(HEADS)], axis=1),), ()
    k_m, ckvn = _mm_epi(proj_a, w["w_k"], "nn", D, rope_k, "mla_uk", 512, rows=[(proj_a, LANES, 24)] + tables,
                        outs=[(D, D, BF16)], pro=(lora_norm, [g_kv], Q_LORA // KV_LORA))
    v_m = _mm(ckvn, w["w_v"], "nn", BF16, "mla_uv")

    bf = jnp.transpose(vec["b_forget"])
    zt, neg_f = _fox_gates(proj_a, 24, bf)
    bias = neg_f.reshape(HEADS, N_ATT, 1, ATT_T)
    o_b, lse_b = _attn_fwd(q_m, 0, k_m, 0, v_m, 0, 2 * LANES, 1.0 / math.sqrt(64 + ROPE_DIM), None, "mla_attn")
    bias = bias + wts("relay_proj", o_b)["tok"][0, 0]
    o_a, lse_a = _attn_fwd(qkv, 0, qkv, 4, qkv, 8, LANES, 1.0 / math.sqrt(HEAD_DIM), bias, "fox_attn")

    w = {**w, **wts("proj", o_a)}
    gate_mix = gate_mix + wts("relay_ffn", o_a)["tok"][0, 0]
    pa = _mm(o_a, w["w_pf"], "nn", BF16, "proj_fox")

    def merge(pb_, gf, gm, pa_):
        return (_sigmoid(gf) * pa_ + _sigmoid(gm) * pb_, pb_), ()
    merged, pb = _mm_epi(o_b, w["w_pm"], "nn", 512, merge, "proj_mla", 1024,
                         rows=[(proj_a, 512, 2), (proj_a, 512, 4), (pa, 512)], outs=[(D, 512, BF16), (D, 512, BF16)])
    def post1(yv, xv, gate, gpost, gpre, sc, sh):
        x1 = xv + gate * (yv * _rstd(yv) * gpost)
        return (x1, (x1 * _rstd(x1) * gpre) * (1.0 + sc) + sh, yv), ()
    x1, h2, y = _mm_epi(merged, w["w_out"], "nn", D, post1, "out_proj", 512, rows=[(x, D)],
                        vecs=[gate_mix, g_post_mix, g_pre_ffn, scale_ffn, shift_ffn],
                        outs=[(D, D, F32), (D, D, BF16), (D, D, F32)])
    w = {**w, **wts("ffn", h2)}

    def swiglu(r):
        g, u = r[:, :FFN_T], r[:, FFN_T:]
        return (g * _sigmoid(g) * u, r), ()
    act, gu = _mm_epi(h2, w["w_ffn_in"], "nn", 2 * FFN_T, swiglu, "ffn_in", 1024,
                      outs=[(D_FF, FFN_T, BF16), (2 * D_FF, 2 * FFN_T, BF16)])

    def head(y2v, x1v, tv, gate, gpost):
        r = _rstd(y2v)
        yn = y2v * r
        n2 = yn * gpost
        err = (x1v + gate * n2) - tv
        dx2 = err * (1.0 / D)
        dn2 = dx2 * gate
        dy2 = _norm_bwd(dn2 * gpost, yn, r)
        return (dx2, dy2), (_colsum(err * err), _colsum(dx2 * n2), _colsum(dn2 * yn))
    dx2, dy2, err_cols, d_gate_ffn, d_g_post_ffn = _mm_epi(
        act, w["w_ffn_out"], "nn", D, head, "ffn_out", 512, rows=[(x1, D), (target, D)], vecs=[gate_ffn, g_post_ffn],
        outs=[(D, D, F32), (D, D, BF16)], sums=[D, D, D])

    def swiglu_bwd(da, guv):
        g, u = guv[:, :FFN_T].astype(F32), guv[:, FFN_T:].astype(F32)
        sg = _sigmoid(g)
        return (jnp.concatenate([da * u * (sg * (1.0 + g * (1.0 - sg))), da * (g * sg)], axis=1),), ()
    (dgu,) = _mm_epi(dy2, w["w_ffn_out"], "nt", FFN_T, swiglu_bwd, "ffn_out_dx", 1024, rows=[(gu, 2 * FFN_T)],
                     outs=[(2 * D_FF, 2 * FFN_T, BF16)])
    dw = {"w_ffn_out": _mm(act, dy2, "tn", BF16, "ffn_out_dw")}
    dw["w_ffn_in"] = _mm(h2, dgu, "tn", BF16, "ffn_in_dw")
    gate_mix = gate_mix + send({n: dw.pop(n) for n in ("w_ffn_in", "w_ffn_out")})[0, 0]

    def mid(dh, x1v, dx2v, yv, gpre, sc, gate, gpost):
        r2 = _rstd(x1v)
        x1n = x1v * r2
        t = dh * x1n
        dx1 = dx2v + _norm_bwd(dh * (gpre * (1.0 + sc)), x1n, r2)
        ry = _rstd(yv)
        yn = yv * ry
        dn1 = dx1 * gate
        dy = _norm_bwd(dn1 * gpost, yn, ry)
        sums = (_colsum(dh), _colsum(t) * gpre, _colsum(t) * (1.0 + sc), _colsum(dx1 * (yn * gpost)), _colsum(dn1 * yn))
        return (dx1, dy), sums
    dx1, dy, d_shift_ffn, d_scale_ffn, d_g_pre_ffn, d_gate_mix, d_g_post_mix = _mm_epi(
        dgu, w["w_ffn_in"], "nt", D, mid, "ffn_in_dx", 512, rows=[(x1, D), (dx2, D), (y, D)],
        vecs=[g_pre_ffn, scale_ffn, gate_mix, g_post_mix], outs=[(D, D, F32), (D, D, BF16)], sums=[D] * 5)

    dw["w_out"] = _mm(merged, dy, "tn", BF16, "out_proj_dw")

    def merge_bwd(dm, gf, gm, pa_, pb_):
        sf, sm = _sigmoid(gf), _sigmoid(gm)
        return (dm * sf, dm * sm, dm * pa_ * (sf * (1.0 - sf)), dm * pb_ * (sm * (1.0 - sm))), ()
    dpa, dpb, dgf, dgm = _mm_epi(dy, w["w_out"], "nt", 512, merge_bwd, "out_proj_dx", 1024,
                                 rows=[(proj_a, 512, 2), (proj_a, 512, 4), (pa, 512), (pb, 512)],
                                 outs=[(D, 512, BF16)] * 4)
    do_a = _mm(dpa, w["w_pf"], "nt", BF16, "proj_fox_dx")
    do_b = _mm(dpb, w["w_pm"], "nt", BF16, "proj_mla_dx")
    dw["w_pf"] = _mm(o_a, dpa, "tn", BF16, "proj_fox_dw")
    dw["w_pm"] = _mm(o_b, dpb, "tn", BF16, "proj_mla_dw")
    bias = bias + send({n: dw.pop(n) for n in ("w_out", "w_pf", "w_pm")})[0, 0]

    sc_a, sc_b = 1.0 / math.sqrt(HEAD_DIM), 1.0 / math.sqrt(64 + ROPE_DIM)
    dq_a, dk_a, dv_a, dbias = _attn_grad(qkv, 0, qkv, 4, qkv, 8, do_a, lse_a, LANES, sc_a, bias, BF16, "fox_attn_bwd")
    dq_m, dk_m, dv_m = _attn_grad(q_m, 0, k_m, 0, v_m, 0, do_b, lse_b, 2 * LANES, sc_b, None, F32, "mla_attn_bwd")

    def mla_rope_bwd(dq, dk, c_, a_, b_):
        lane = lax.broadcasted_iota(jnp.int32, (1, LANES), 1)
        dqs = [_rope_t(dq[:, hd * LANES:(hd + 1) * LANES], c_, a_, b_) for hd in range(HEADS)]
        dkpe = dk[:, 0:LANES]
        for hd in range(1, HEADS):
            dkpe = dkpe + dk[:, hd * LANES:(hd + 1) * LANES]
        dkpe = jnp.where((lane >= 64) & (lane < 96), dkpe, 0.0)
        dkr = jnp.where((lane >= 64) & (lane < 96), _rope_t(dkpe, c_, a_, b_), 0.0)
        return (jnp.concatenate(dqs, axis=1), dk, dkr), ()
    dqb, dkb, dkr = _rowwise(mla_rope_bwd, [(dq_m, D, 0), (dk_m, D, 0), (ct, LANES, 0), (sa, LANES, 0), (sb, LANES, 0)],
                             [], [(D, BF16), (D, BF16), (LANES, F32)], [], "mla_rope_bwd")
    def lora_q_bwd(dq, cq, gq):
        rq = _rstd(cq)
        cqh = cq * rq
        return (_norm_bwd(dq * gq, cqh, rq),), (_colsum(dq * cqh),)
    dcq, d_g_q = _mm_epi(dqb, w["w_uq"], "nt", Q_LORA, lora_q_bwd, "mla_uq_dx", 512, rows=[(proj_a, Q_LORA, 0)],
                         vecs=[g_q], outs=[(Q_LORA, Q_LORA, BF16)], sums=[Q_LORA])
    dw["w_uq"] = _mm(cqn, dqb, "tn", BF16, "mla_uq_dw")

    def lora_kv_bwd(dv_part, dk_part, ckv, gkv):
        dkv = dv_part + dk_part
        rk = _rstd(ckv)
        ckh = ckv * rk
        return (_norm_bwd(dkv * gkv, ckh, rk),), (_colsum(dkv * ckh),)
    dckv, d_g_kv = _mm_epi(dv_m, w["w_v"], "nt", KV_LORA, lora_kv_bwd, "mla_uv_dx", 1024,
                           rows=[(_mm(dkb, w["w_k"], "nt", F32, "mla_uk_dx"), KV_LORA), (proj_a, KV_LORA, 3)],
                           vecs=[g_kv], outs=[(KV_LORA, KV_LORA, BF16)], sums=[KV_LORA])
    dw["w_k"] = _mm(ckvn, dkb, "tn", BF16, "mla_uk_dw")
    dw["w_v"] = _mm(ckvn, dv_m, "tn", BF16, "mla_uv_dw")

    dzt, d_bf = _fox_gates_bwd(dbias.reshape(HEADS, S), zt, bf)
    dmisc = (dkr + jnp.pad(jnp.transpose(dzt), ((0, 0), (0, LANES - HEADS)))).astype(BF16)
    dproj_a = jnp.concatenate([dcq, dckv, dgf, dgm, dmisc], axis=1)
    dqkv = jnp.concatenate([dq_a, dk_a, dv_a], axis=1)
    dw["w_a"] = _mm(h, dproj_a, "tn", BF16, "in_proj_a_dw")
    dw["w_b"] = _mm(h, dqkv, "tn", BF16, "in_proj_b_dw")
    tok = send(dw, True)
    dh_a = _mm(dproj_a, w["w_a"], "nt", F32, "in_proj_a_dx", dep=tok)
    g_pre_mix = g_pre_mix + relay(dh_a)[0, 0]

    def first(dh_b, dh_a, xv, dx1v, gpre, sc):
        dhv = dh_b + dh_a
        r = _rstd(xv)
        xn = xv * r
        t = dhv * xn
        dx = dx1v + _norm_bwd(dhv * (gpre * (1.0 + sc)), xn, r)
        return (dx,), (_colsum(dhv), _colsum(t) * gpre, _colsum(t) * (1.0 + sc))
    grad_x, d_shift_mix, d_scale_mix, d_g_pre_mix = _mm_epi(
        dqkv, w["w_b"], "nt", D, first, "in_proj_b_dx", 512,
        rows=[(dh_a, D), (x, D), (dx1, D)],
        vecs=[g_pre_mix, scale_mix], outs=[(D, D, F32)], sums=[D] * 3)

    dmod = jnp.concatenate([d_shift_mix, d_scale_mix, d_gate_mix, d_shift_ffn, d_scale_ffn, d_gate_ffn], axis=1)
    small = dict(dmod=dmod, g_pre_mix=d_g_pre_mix, g_post_mix=d_g_post_mix, g_pre_ffn=d_g_pre_ffn,
                 g_post_ffn=d_g_post_ffn, g_q_lora=d_g_q, g_kv_lora=d_g_kv,
                 b_forget=jnp.pad(jnp.transpose(d_bf), ((0, 0), (0, LANES - HEADS))), err=err_cols)
    return grad_x, small


SMALL_ORDER = ("dmod", "g_pre_mix", "g_post_mix", "g_pre_ffn", "g_post_ffn", "g_q_lora", "g_kv_lora", "b_forget", "err")
SMALL_PARAM = {"dmod": "b_ada"}
MATRICES = ("w_in", "w_uq", "w_ukv", "w_proj_fox", "w_proj_mla", "w_out", "w_ffn_in", "w_ffn_out")
WEIGHTS = ("w_ada", "b_ada", "g_pre_mix", "g_post_mix", "g_pre_ffn", "g_post_ffn", "w_in", "b_forget", "g_q_lora",
           "w_uq", "g_kv_lora", "w_ukv", "w_proj_fox", "w_proj_mla", "w_out", "w_ffn_in", "w_ffn_out")


def kernel(x, c, positions, w_ada, b_ada, g_pre_mix, g_post_mix, g_pre_ffn, g_post_ffn, w_in, b_forget, g_q_lora, w_uq, g_kv_lora, w_ukv, w_proj_fox, w_proj_mla, w_out, w_ffn_in, w_ffn_out, loss_target, m_w_ada, m_b_ada, m_g_pre_mix, m_g_post_mix, m_g_pre_ffn, m_g_post_ffn, m_w_in, m_b_forget, m_g_q_lora, m_w_uq, m_g_kv_lora, m_w_ukv, m_w_proj_fox, m_w_proj_mla, m_w_out, m_w_ffn_in, m_w_ffn_out, v_w_ada, v_b_ada, v_g_pre_mix, v_g_post_mix, v_g_pre_ffn, v_g_post_ffn, v_w_in, v_b_forget, v_g_q_lora, v_w_uq, v_g_kv_lora, v_w_ukv, v_w_proj_fox, v_w_proj_mla, v_w_out, v_w_ffn_in, v_w_ffn_out):
    prm = dict(w_ada=w_ada, b_ada=b_ada, g_pre_mix=g_pre_mix, g_post_mix=g_post_mix, g_pre_ffn=g_pre_ffn,
               g_post_ffn=g_post_ffn, w_in=w_in, b_forget=b_forget, g_q_lora=g_q_lora, w_uq=w_uq, g_kv_lora=g_kv_lora,
               w_ukv=w_ukv, w_proj_fox=w_proj_fox, w_proj_mla=w_proj_mla, w_out=w_out, w_ffn_in=w_ffn_in, w_ffn_out=w_ffn_out)
    mom = dict(w_ada=m_w_ada, b_ada=m_b_ada, g_pre_mix=m_g_pre_mix, g_post_mix=m_g_post_mix, g_pre_ffn=m_g_pre_ffn,
               g_post_ffn=m_g_post_ffn, w_in=m_w_in, b_forget=m_b_forget, g_q_lora=m_g_q_lora, w_uq=m_w_uq,
               g_kv_lora=m_g_kv_lora, w_ukv=m_w_ukv, w_proj_fox=m_w_proj_fox, w_proj_mla=m_w_proj_mla, w_out=m_w_out,
               w_ffn_in=m_w_ffn_in, w_ffn_out=m_w_ffn_out)
    var = dict(w_ada=v_w_ada, b_ada=v_b_ada, g_pre_mix=v_g_pre_mix, g_post_mix=v_g_post_mix, g_pre_ffn=v_g_pre_ffn,
               g_post_ffn=v_g_post_ffn, w_in=v_w_in, b_forget=v_b_forget, g_q_lora=v_g_q_lora, w_uq=v_w_uq,
               g_kv_lora=v_g_kv_lora, w_ukv=v_w_ukv, w_proj_fox=v_w_proj_fox, w_proj_mla=v_w_proj_mla, w_out=v_w_out,
               w_ffn_in=v_w_ffn_in, w_ffn_out=v_w_ffn_out)
    me = _flat(*_coords())
    slot = jnp.reshape(me, (1,)).astype(jnp.int32)

    own = {n: prm[n][0].astype(BF16) for n in MATRICES}
    no_dep = jnp.zeros((8, LANES), F32)
    (st_c, st_in), tok = _async_start([[c], [own["w_in"]]], ["gather", "spread"], no_dep, "gather_in_start")
    (c_own,), (c_land,) = _async_wait(st_c, tok, "gather_c_wait")
    c_all = _with_own(c_land, c_own, me).reshape(N_DEV, D)
    ada_cols = w_ada.shape[2]
    b_cols = lax.dynamic_slice(b_ada, (0, me * ada_cols), (1, ada_cols))
    mod_cols, silu_c = _mod_part(c_all, w_ada[0], b_cols)
    (mod_all,) = _all_gather([mod_cols], "gather_mod")

    (w_in_own,), (w_in_land,) = _async_wait(st_in, mod_all, "gather_in_wait")
    (st_in,), tok = _async_start([[w_in_land]], "forward", no_dep, "gather_in_forward")
    _, (w_in_land,) = _async_wait(st_in, tok, "gather_in_forward_wait")
    w = _prepare_weights({"w_in": w_in_land}, {"w_in": w_in_own}, slot)
    later = dict(lora=("w_uq", "w_ukv"), proj=("w_proj_fox", "w_proj_mla", "w_out"), ffn=("w_ffn_in", "w_ffn_out"))
    states, tok = _async_start([[own[n] for n in names] for names in later.values()], ["gather", "spread", "spread"],
                               w["w_b"], "gather_rest_start")
    gather_state = dict(zip(later, states))
    own_thru = {}

    def wts(group, after):
        if group.startswith("relay_"):
            name = group[len("relay_"):]
            own_thru[name], lands = _async_wait(gather_state[name], after, "gather_" + name + "_wait")
            (gather_state[name],), t = _async_start([lands], "forward", no_dep, "gather_" + name + "_forward")
            return {"tok": t}
        srcs, lands = _async_wait(gather_state[group], after, "gather_" + group + "_landed")
        srcs = own_thru.get(group, srcs)
        return _prepare_weights(dict(zip(later[group], lands)), dict(zip(later[group], srcs)), slot)

    sent, last = [], {}

    def send(grads, final=False):
        shards = _shard_grads(grads)
        names = list(shards)
        (state,), t = _async_start([[shards[n] for n in names]], "pair" if final else "exchange", no_dep,
                                   "exchange_" + names[0] + "_start")
        if final:
            last.update(names=names, state=state)
        else:
            sent.append((names, state))
        return t

    def relay(after):
        srcs, lands = _async_wait(last["state"], after, "exchange_pair_wait")
        sums = []
        for src, land in zip(srcs, lands):
            by_chip = src.reshape((N_DEV // 2, 2) + src.shape[1:])
            sums.append(_add_blocks(lax.dynamic_index_in_dim(by_chip, lax.axis_index("c"), 1, keepdims=False), land))
        (last["state"],), t = _async_start([sums], "chips", no_dep, "exchange_chips_start")
        return t

    mod = lax.dynamic_index_in_dim(mod_all, me, axis=1, keepdims=False).reshape(1, 6 * D) + tok[0, 0]

    vec = dict(g_pre_mix=g_pre_mix, g_post_mix=g_post_mix, g_pre_ffn=g_pre_ffn, g_post_ffn=g_post_ffn,
               g_q_lora=g_q_lora, g_kv_lora=g_kv_lora, b_forget=b_forget)
    pos = positions.astype(F32).reshape(S, 1)
    grad_x, small = _fwd_bwd(x[0], pos, mod, loss_target[0], w, vec, wts, send, relay)

    bundle = jnp.concatenate([small[n] for n in SMALL_ORDER], axis=1)
    (small_state,), tok = _async_start([[bundle]], "gather", jnp.zeros((8, LANES), F32), "gather_small_start")

    out = {}
    after = tok
    for names, state in sent:
        srcs, lands = _async_wait(state, after, "exchange_" + names[0] + "_wait")
        for n, src, land in zip(names, srcs, lands):
            out[n] = _adamw(prm[n][0], mom[n][0], var[n][0], land, "adamw_" + n, src, slot)
            after = out[n][0]
    srcs, lands = _async_wait(last["state"], after, "exchange_chips_wait")
    for n, src, land in zip(last["names"], srcs, lands):
        out[n] = _adamw(prm[n][0], mom[n][0], var[n][0], land, "adamw_" + n, src, slot // 2)
        after = out[n][0]

    (own_bundle,), (bundle_all,) = _async_wait(small_state, after, "gather_small_wait")
    bundle_all = _with_own(bundle_all, own_bundle, me)
    dmod_all = bundle_all[:, 0, :6 * D]
    dm_cols = lax.dynamic_slice(dmod_all, (0, me * ada_cols), (N_DEV, ada_cols))
    g_ada = _w_ada_grad(jnp.transpose(silu_c), dm_cols)
    out["w_ada"] = _adamw(w_ada[0], m_w_ada[0], v_w_ada[0], g_ada[None], "adamw_w_ada")

    offsets, off = {}, 0
    for n in SMALL_ORDER:
        offsets[n] = off
        off += small[n].shape[1]
    names = [SMALL_PARAM.get(n, n) for n in SMALL_ORDER if n != "err"]
    results, err = _adamw_rows(bundle_all, [offsets[n] for n in SMALL_ORDER if n != "err"],
                               [prm[n] for n in names], [mom[n] for n in names], [var[n] for n in names],
                               offsets["err"], D)
    out.update(zip(names, results))
    loss = 0.5 * jnp.sum(err) / D

    res = [loss, grad_x[None]]
    for kind in range(4):
        for n in WEIGHTS:
            t = out[n][kind]
            res.append(t[None] if prm[n].ndim == 3 else t)
    return tuple(res)
```

```python
import functools
import math

import jax
import jax.numpy as jnp
from jax import lax
from jax.experimental import pallas as pl
from jax.experimental.pallas import tpu as pltpu

F32 = jnp.float32
BF16 = jnp.bfloat16

N_DEV = 8
S = 2048
D = 1024
D_FF = 2816
HEADS = 8
HEAD_DIM = 64
Q_LORA = 768
KV_LORA = 256
ROPE_DIM = 32
ROPE_THETA = 10000.0
NORM_EPS = 1e-6
LANES = 128
VMEM_LIMIT = 56 * 1024 * 1024

ADAM_LR = 0.001
ADAM_B1 = 0.9
ADAM_B2 = 0.999
ADAM_EPS = 1e-08
ADAM_WD = 0.01
ADAM_STEP = 10

ATT_T = 256
LOG2E = 1.4426950408889634
N_ATT = S // ATT_T

NN = (((1,), (0,)), ((), ()))
NT = (((1,), (1,)), ((), ()))
TN = (((0,), (0,)), ((), ()))
MESH = pl.DeviceIdType.MESH


def _params(sem=None):
    return pltpu.CompilerParams(dimension_semantics=sem, vmem_limit_bytes=VMEM_LIMIT)


def _pick(n, cap):
    best = None
    for t in range(LANES, cap + 1, LANES):
        if n % t == 0:
            best = t
    return best if best is not None else n


def _mm(a, b, mode, out_dtype, name, acc=None, dep=None):
    if mode == "nn":
        (m, k), (k2, n), dn = a.shape, b.shape, NN
    elif mode == "nt":
        (m, k), (n, k2), dn = a.shape, b.shape, NT
    else:
        (k, m), (k2, n), dn = a.shape, b.shape, TN
    assert k == k2, (a.shape, b.shape, mode)
    tn = _pick(n, 640)
    tm = _pick(m, 1536)
    osz = jnp.dtype(out_dtype).itemsize

    def need(tm_):
        blk = tm_ * k * 2 + tn * k * 2 + tm_ * tn * osz + (tm_ * tn * 4 if acc is not None else 0)
        return 2 * blk + tm_ * tn * 4
    while need(tm) > 36 * 1024 * 1024 and tm % 256 == 0:
        tm //= 2

    def body(*refs):
        a_ref, b_ref, o_ref = refs[0], refs[1], refs[-1]
        r = lax.dot_general(a_ref[...], b_ref[...], dn, preferred_element_type=F32)
        if acc is not None:
            r = r + refs[2][...]
        o_ref[...] = r.astype(o_ref.dtype)

    if mode == "tn":
        a_spec = pl.BlockSpec((k, tm), lambda i, j: (0, i))
    else:
        a_spec = pl.BlockSpec((tm, k), lambda i, j: (i, 0))
    if mode == "nt":
        b_spec = pl.BlockSpec((tn, k), lambda i, j: (j, 0))
    else:
        b_spec = pl.BlockSpec((k, tn), lambda i, j: (0, j))
    o_spec = pl.BlockSpec((tm, tn), lambda i, j: (i, j))
    in_specs = [a_spec, b_spec] + ([o_spec] if acc is not None else [])
    in_specs += [pl.BlockSpec(memory_space=pl.ANY)] if dep is not None else []
    args = (a, b) + ((acc,) if acc is not None else ()) + ((dep,) if dep is not None else ())
    return pl.pallas_call(
        body, name=name, grid=(m // tm, n // tn),
        in_specs=in_specs, out_specs=o_spec,
        out_shape=jax.ShapeDtypeStruct((m, n), out_dtype),
        compiler_params=_params(("parallel", "parallel")),
    )(*args)


def _mm_epi(a, b, mode, tnb, epi, name, tm, rows=(), vecs=(), outs=(), sums=(), pro=None):
    m = a.shape[0]
    k, nb = (b.shape if mode == "nn" else b.shape[::-1])
    dn = NN if mode == "nn" else NT
    pro_fn, pro_vecs, a_off = pro if pro is not None else (None, (), 0)
    n_in = 2 + len(rows) + len(vecs)
    n_all = n_in + len(pro_vecs)
    sub = min(tm, 256)

    def body(*refs):
        if pro is not None:
            a_out, a_scr = refs[-2:]
            refs = refs[:-2]

            @pl.when(pl.program_id(1) == 0)
            def _():
                a_scr[...] = pro_fn(refs[0][...], *[x[...] for x in refs[n_in:n_all]]).astype(BF16)
                a_out[...] = a_scr[...]
            a_ref = a_scr
        else:
            a_ref = refs[0]
        o_refs = refs[n_all:n_all + len(outs)]
        s_refs = refs[n_all + len(outs):]
        if sums:
            @pl.when((pl.program_id(0) == 0) & (pl.program_id(1) == 0))
            def _():
                for s_ref in s_refs:
                    s_ref[...] = jnp.zeros(s_ref.shape, F32)
        for c in range(tm // sub):
            rs = slice(c * sub, (c + 1) * sub)
            r = lax.dot_general(a_ref[rs, :], refs[1][...], dn, preferred_element_type=F32)
            o_vals, s_vals = epi(r, *[x[rs, :] for x in refs[2:2 + len(rows)]], *[x[...] for x in refs[2 + len(rows):n_in]])
            assert len(o_vals) == len(o_refs) and len(s_vals) == len(s_refs)
            for o_ref, val in zip(o_refs, o_vals):
                o_ref[rs, :] = val.astype(o_ref.dtype)
            for s_ref, val in zip(s_refs, s_vals):
                s_ref[...] += val

    once = dict(pipeline_mode=pl.Buffered(1)) if nb == tnb else {}
    if mode == "nn":
        b_spec = pl.BlockSpec((k, tnb), lambda i, j: (0, j), **once)
    else:
        b_spec = pl.BlockSpec((tnb, k), lambda i, j: (j, 0), **once)
    in_specs = [pl.BlockSpec((tm, k), lambda i, j: (i, a_off)), b_spec]
    rows = [tuple(r) + (0,) * (3 - len(r)) for r in rows]
    in_specs += [pl.BlockSpec((tm, w), functools.partial(lambda i, j, off: (i, j + off), off=off)) for _, w, off in rows]
    in_specs += [pl.BlockSpec(v.shape, lambda i, j: (0, 0)) for v in list(vecs) + list(pro_vecs)]
    out_specs = [pl.BlockSpec((tm, w), lambda i, j: (i, j)) for _, w, _ in outs]
    out_specs += [pl.BlockSpec((1, w), lambda i, j: (0, 0)) for w in sums]
    out_shape = [jax.ShapeDtypeStruct((m, full), dt) for full, _, dt in outs]
    out_shape += [jax.ShapeDtypeStruct((1, w), F32) for w in sums]
    if pro is not None:
        out_specs.append(pl.BlockSpec((tm, k), lambda i, j: (i, 0)))
        out_shape.append(jax.ShapeDtypeStruct((m, k), BF16))
    return pl.pallas_call(
        body, name=name, grid=(m // tm, nb // tnb),
        in_specs=in_specs, out_specs=out_specs, out_shape=out_shape,
        scratch_shapes=[pltpu.VMEM((tm, k), BF16)] if pro is not None else [],
        compiler_params=_params(("arbitrary", "arbitrary") if sums else ("parallel", "arbitrary" if pro is not None else "parallel")),
    )(a, b, *[r[0] for r in rows], *vecs, *pro_vecs)


def _rowwise(fn, row_ins, vec_ins, row_outs, sum_outs, name, tm=256):
    n_in = len(row_ins) + len(vec_ins)
    n_o = len(row_outs)
    rows = row_ins[0][0].shape[0]

    def body(*refs):
        vals = [r[...] for r in refs[:n_in]]
        outs = refs[n_in:]
        ro, so = fn(*vals)
        assert len(ro) == n_o and len(so) == len(sum_outs)
        for r, v in zip(outs[:n_o], ro):
            r[...] = v.astype(r.dtype)
        if sum_outs:
            @pl.when(pl.program_id(0) == 0)
            def _():
                for r in outs[n_o:]:
                    r[...] = jnp.zeros(r.shape, F32)
            for r, v in zip(outs[n_o:], so):
                r[...] += v

    in_specs = [pl.BlockSpec((tm, w), functools.partial(lambda i, b: (i, b), b=b)) for _, w, b in row_ins]
    in_specs += [pl.BlockSpec(v.shape, lambda i: (0, 0)) for v in vec_ins]
    out_specs = [pl.BlockSpec((tm, w), lambda i: (i, 0)) for w, _ in row_outs]
    out_specs += [pl.BlockSpec((1, w), lambda i: (0, 0)) for w in sum_outs]
    out_shape = [jax.ShapeDtypeStruct((rows, w), dt) for w, dt in row_outs]
    out_shape += [jax.ShapeDtypeStruct((1, w), F32) for w in sum_outs]
    return pl.pallas_call(
        body, name=name, grid=(rows // tm,),
        in_specs=in_specs, out_specs=out_specs, out_shape=out_shape,
        compiler_params=_params(("arbitrary",)),
    )(*[a for a, _, _ in row_ins], *vec_ins)


def _sigmoid(x):
    return 1.0 / (1.0 + jnp.exp(-x))


def _rstd(x):
    return lax.rsqrt(jnp.mean(x * x, axis=-1, keepdims=True) + NORM_EPS)


def _norm_bwd(dyn, xn, r):
    return r * (dyn - xn * jnp.mean(dyn * xn, axis=-1, keepdims=True))


def _colsum(x):
    return jnp.sum(x, axis=0, keepdims=True)


def _rope_tables(pos, invf):
    def fn(p, f):
        lane = lax.broadcasted_iota(jnp.int32, (1, LANES), 1)
        ang = p * f
        cs, sn = jnp.cos(ang), jnp.sin(ang)
        rot = (lane >= 64) & (lane < 96)
        ct = jnp.where(lane < 64, 1.0, jnp.where(rot, cs, 0.0))
        sa = jnp.where((lane >= 64) & (lane < 80), -sn, 0.0)
        sb = jnp.where((lane >= 80) & (lane < 96), sn, 0.0)
        return (ct, sa, sb), ()
    return _rowwise(fn, [(pos, 1, 0)], [invf], [(LANES, F32)] * 3, [], "rope_tables")


def _rope(x, ct, sa, sb):
    return x * ct + pltpu.roll(x, LANES - 16, 1) * sa + pltpu.roll(x, 16, 1) * sb


def _rope_t(x, ct, sa, sb):
    return x * ct - pltpu.roll(x, LANES - 16, 1) * sa - pltpu.roll(x, 16, 1) * sb


def _head_mask(width, hh):
    lane = lax.broadcasted_iota(jnp.int32, (1, width), 1)
    half = width // 2
    return (lane >= hh * half) & (lane < (hh + 1) * half)


ATT_PP = 2
ATT_CHAINS = [(a, hh) for a in range(ATT_PP) for hh in range(2)]
ATT_G = HEADS // (2 * ATT_PP)


def _pair(ref_or_val, a, width, rows=slice(None)):
    return ref_or_val[rows, a * width:(a + 1) * width]


def _attn_fwd(q, qo, k, ko, v, vo, dkp, scale, bias, name):
    T = ATT_T
    assert qo % ATT_PP == 0 and ko % ATT_PP == 0 and vo % ATT_PP == 0
    qo, ko, vo = qo // ATT_PP, ko // ATT_PP, vo // ATT_PP

    def body(*refs):
        if bias is not None:
            q_ref, k_ref, v_ref, b_ref, o_ref, lse_ref, s_scr = refs
        else:
            q_ref, k_ref, v_ref, o_ref, lse_ref, s_scr = refs
        i = pl.program_id(1)
        row = lax.broadcasted_iota(jnp.int32, (T, T), 0)
        col = lax.broadcasted_iota(jnp.int32, (T, T), 1)
        qms = []
        for a, hh in ATT_CHAINS:
            qb = _pair(q_ref, a, dkp)
            qms.append(jnp.where(_head_mask(dkp, hh), qb, jnp.zeros_like(qb)))

        def fold(t):
            return [t[:, c * LANES:(c + 1) * LANES] for c in range(T // LANES)]

        def run(nt):
            mls = [jnp.full((T, LANES), -jnp.inf, F32) for _ in ATT_CHAINS]
            for j in range(nt):
                ks = slice(j * T, (j + 1) * T)
                for ci, (a, hh) in enumerate(ATT_CHAINS):
                    s = lax.dot_general(qms[ci], _pair(k_ref, a, dkp, ks), NT, preferred_element_type=F32) * (scale * LOG2E)
                    if bias is not None:
                        s = s + b_ref[2 * a + hh, j] * LOG2E
                    if j == nt - 1:
                        s = jnp.where(row >= col, s, -jnp.inf)
                    s_scr[ci, j] = s
                    for part in fold(s):
                        mls[ci] = jnp.maximum(mls[ci], part)
            ms = [jnp.max(ml, axis=1, keepdims=True) for ml in mls]
            mbs = [jnp.broadcast_to(m, (T, LANES)) for m in ms]
            for a in range(ATT_PP):
                ls = [jnp.zeros((T, LANES), F32) for _ in range(2)]
                ps, vms = [], []
                for j in range(nt):
                    vb = _pair(v_ref, a, LANES, slice(j * T, (j + 1) * T))
                    for hh in range(2):
                        parts = [jnp.exp2(part - mbs[2 * a + hh]) for part in fold(s_scr[2 * a + hh, j])]
                        for part in parts:
                            ls[hh] = ls[hh] + part
                        ps.append(jnp.concatenate(parts, axis=1).astype(BF16))
                        vms.append(jnp.where(_head_mask(LANES, hh), vb, jnp.zeros_like(vb)))
                acc = lax.dot_general(jnp.concatenate(ps, axis=1), jnp.concatenate(vms, axis=0), NN,
                                      preferred_element_type=F32)
                l0, l1 = [jnp.sum(l, axis=1, keepdims=True) for l in ls]
                lse_ref[2 * a] = ms[2 * a] + jnp.log2(l0)
                lse_ref[2 * a + 1] = ms[2 * a + 1] + jnp.log2(l1)
                inv = jnp.where(_head_mask(LANES, 0), 1.0 / l0, 1.0 / l1)
                o_ref[:, a * LANES:(a + 1) * LANES] = (acc * inv).astype(o_ref.dtype)

        for nt in range(1, N_ATT + 1):
            pl.when(i == nt - 1)(functools.partial(run, nt))

    in_specs = [
        pl.BlockSpec((T, ATT_PP * dkp), lambda g, i: (i, qo + g)),
        pl.BlockSpec((S, ATT_PP * dkp), lambda g, i: (0, ko + g)),
        pl.BlockSpec((S, ATT_PP * LANES), lambda g, i: (0, vo + g)),
    ]
    args = [q, k, v]
    if bias is not None:
        in_specs.append(pl.BlockSpec((2 * ATT_PP, N_ATT, 1, T), lambda g, i: (g, 0, 0, 0)))
        args.append(bias)
    return pl.pallas_call(
        body, name=name, grid=(ATT_G, N_ATT),
        in_specs=in_specs,
        out_specs=[pl.BlockSpec((T, ATT_PP * LANES), lambda g, i: (i, g)),
                   pl.BlockSpec((2 * ATT_PP, T, 1), lambda g, i: (g, i, 0))],
        out_shape=[jax.ShapeDtypeStruct((S, HEADS * HEAD_DIM), BF16),
                   jax.ShapeDtypeStruct((HEADS, S, 1), F32)],
        scratch_shapes=[pltpu.VMEM((len(ATT_CHAINS), N_ATT, T, T), F32)],
        compiler_params=_params(("parallel", "arbitrary")),
    )(*args)


def _attn_grad(q, qo, k, ko, v, vo, do, lse, dkp, scale, bias, qk_dtype, name):
    T = ATT_T
    has_b = bias is not None
    qo, ko, vo = qo // ATT_PP, ko // ATT_PP, vo // ATT_PP
    n_ch = len(ATT_CHAINS)

    def body(*refs):
        q_ref, k_ref, v_ref, do_ref, lse_ref = refs[:5]
        refs = refs[5:]
        if has_b:
            b_ref, refs = refs[0], refs[1:]
        dq_ref, dk_ref, dv_ref = refs[:3]
        refs = refs[3:]
        if has_b:
            db_ref, refs = refs[0], refs[1:]
        p_scr, dp_scr, dk_acc, dv_acc = refs[:4]
        db_acc = refs[4] if has_b else None
        i = pl.program_id(1)

        @pl.when(i == 0)
        def _():
            dk_acc[...] = jnp.zeros(dk_acc.shape, F32)
            dv_acc[...] = jnp.zeros(dv_acc.shape, F32)
            if has_b:
                db_acc[...] = jnp.zeros(db_acc.shape, F32)

        row = lax.broadcasted_iota(jnp.int32, (T, T), 0)
        col = lax.broadcasted_iota(jnp.int32, (T, T), 1)

        def fold(t):
            return [t[:, c * LANES:(c + 1) * LANES] for c in range(T // LANES)]

        qms, doms, lses = [], [], []
        for a, hh in ATT_CHAINS:
            qb, dob = _pair(q_ref, a, dkp), _pair(do_ref, a, LANES)
            qms.append(jnp.where(_head_mask(dkp, hh), qb, jnp.zeros_like(qb)))
            doms.append(jnp.where(_head_mask(LANES, hh), dob, jnp.zeros_like(dob)))
            lses.append(lse_ref[2 * a + hh])

        def run(nt):
            dls = [jnp.zeros((T, LANES), F32) for _ in ATT_CHAINS]
            for j in range(nt):
                ks = slice(j * T, (j + 1) * T)
                for ci, (a, hh) in enumerate(ATT_CHAINS):
                    s = lax.dot_general(qms[ci], _pair(k_ref, a, dkp, ks), NT, preferred_element_type=F32) * (scale * LOG2E)
                    if has_b:
                        s = s + b_ref[ci, j] * LOG2E
                    s = s - lses[ci]
                    if j == nt - 1:
                        s = jnp.where(row >= col, s, -jnp.inf)
                    p = jnp.exp2(s)
                    dp = lax.dot_general(doms[ci], _pair(v_ref, a, LANES, ks), NT, preferred_element_type=F32)
                    p_scr[ci, j] = p
                    dp_scr[ci, j] = dp
                    for part in fold(p * dp):
                        dls[ci] = dls[ci] + part
            deltas = [jnp.broadcast_to(jnp.sum(dl, axis=1, keepdims=True), (T, LANES)) for dl in dls]
            for a in range(ATT_PP):
                ds_all, km_all = [], []
                qm2t = jnp.transpose(jnp.concatenate([qms[2 * a], qms[2 * a + 1]], axis=0))
                dom2t = jnp.transpose(jnp.concatenate([doms[2 * a], doms[2 * a + 1]], axis=0))
                for j in range(nt):
                    ks = slice(j * T, (j + 1) * T)
                    kb = _pair(k_ref, a, dkp, ks)
                    p2, ds2 = [], []
                    for hh in range(2):
                        ci = 2 * a + hh
                        p = p_scr[ci, j]
                        ds = jnp.concatenate([pp * (dd - deltas[ci]) for pp, dd in zip(fold(p), fold(dp_scr[ci, j]))], axis=1)
                        if has_b:
                            db_acc[ci, j] += jnp.sum(ds, axis=0, keepdims=True)
                        p2.append(p.astype(BF16))
                        ds2.append((ds * scale).astype(BF16))
                        km_all.append(jnp.where(_head_mask(dkp, hh), kb, jnp.zeros_like(kb)))
                    dv_acc[a * LANES:(a + 1) * LANES, ks] += lax.dot_general(
                        dom2t, jnp.concatenate(p2, axis=0), NN, preferred_element_type=F32)
                    dk_acc[a * dkp:(a + 1) * dkp, ks] += lax.dot_general(
                        qm2t, jnp.concatenate(ds2, axis=0), NN, preferred_element_type=F32)
                    ds_all += ds2
                dq = lax.dot_general(jnp.concatenate(ds_all, axis=1), jnp.concatenate(km_all, axis=0), NN,
                                     preferred_element_type=F32)
                dq_ref[:, a * dkp:(a + 1) * dkp] = dq.astype(dq_ref.dtype)

        for nt in range(1, N_ATT + 1):
            pl.when(i == nt - 1)(functools.partial(run, nt))

        @pl.when(i == N_ATT - 1)
        def _():
            dk_ref[...] = jnp.transpose(dk_acc[...]).astype(dk_ref.dtype)
            dv_ref[...] = jnp.transpose(dv_acc[...]).astype(dv_ref.dtype)
            if has_b:
                db_ref[...] = db_acc[...]

    in_specs = [
        pl.BlockSpec((T, ATT_PP * dkp), lambda g, i: (i, qo + g)),
        pl.BlockSpec((S, ATT_PP * dkp), lambda g, i: (0, ko + g)),
        pl.BlockSpec((S, ATT_PP * LANES), lambda g, i: (0, vo + g)),
        pl.BlockSpec((T, ATT_PP * LANES), lambda g, i: (i, g)),
        pl.BlockSpec((2 * ATT_PP, T, 1), lambda g, i: (g, i, 0)),
    ]
    args = [q, k, v, do, lse]
    out_specs = [
        pl.BlockSpec((T, ATT_PP * dkp), lambda g, i: (i, g)),
        pl.BlockSpec((S, ATT_PP * dkp), lambda g, i: (0, g)),
        pl.BlockSpec((S, ATT_PP * LANES), lambda g, i: (0, g)),
    ]
    width = (HEADS // 2) * dkp
    out_shape = [
        jax.ShapeDtypeStruct((S, width), qk_dtype),
        jax.ShapeDtypeStruct((S, width), qk_dtype),
        jax.ShapeDtypeStruct((S, HEADS * HEAD_DIM), BF16),
    ]
    scratch = [pltpu.VMEM((n_ch, N_ATT, T, T), F32), pltpu.VMEM((n_ch, N_ATT, T, T), F32),
               pltpu.VMEM((ATT_PP * dkp, S), F32), pltpu.VMEM((ATT_PP * LANES, S), F32)]
    if has_b:
        bspec = pl.BlockSpec((2 * ATT_PP, N_ATT, 1, T), lambda g, i: (g, 0, 0, 0))
        in_specs.append(bspec)
        args.append(bias)
        out_specs.append(bspec)
        out_shape.append(jax.ShapeDtypeStruct((HEADS, N_ATT, 1, T), F32))
        scratch.append(pltpu.VMEM((2 * ATT_PP, N_ATT, 1, T), F32))
    return pl.pallas_call(
        body, name=name, grid=(ATT_G, N_ATT),
        in_specs=in_specs, out_specs=out_specs, out_shape=out_shape, scratch_shapes=scratch,
        compiler_params=_params(("parallel", "arbitrary")),
    )(*args)


def _tri(upper):
    a = lax.broadcasted_iota(jnp.int32, (LANES, LANES), 0)
    b = lax.broadcasted_iota(jnp.int32, (LANES, LANES), 1)
    return jnp.where(a <= b if upper else a >= b, 1.0, 0.0).astype(F32)


def _fox_gates(proj, blk, bf):
    def body(m_ref, b_ref, z_out, o_ref):
        tri = _tri(True)
        carry = jnp.zeros((HEADS, 1), F32)
        for t in range(S // LANES):
            sl = slice(t * LANES, (t + 1) * LANES)
            zt = jnp.transpose(m_ref[sl, :])[:HEADS]
            z_out[:, sl] = zt
            z = zt + b_ref[...]
            logf = jnp.minimum(z, 0.0) - jnp.log(1.0 + jnp.exp(-jnp.abs(z)))
            c = lax.dot_general(logf, tri, NN, preferred_element_type=F32,
                                precision=lax.Precision.HIGHEST) + carry
            o_ref[:, sl] = -c
            carry = c[:, LANES - 1:LANES]

    return pl.pallas_call(
        body, name="fox_gates", grid=(1,),
        in_specs=[pl.BlockSpec((S, LANES), lambda i: (0, blk)), pl.BlockSpec(bf.shape, lambda i: (0, 0))],
        out_specs=[pl.BlockSpec((HEADS, S), lambda i: (0, 0))] * 2,
        out_shape=[jax.ShapeDtypeStruct((HEADS, S), F32)] * 2,
        compiler_params=_params(("arbitrary",)),
    )(proj, bf)


def _fox_gates_bwd(dbias, zt, bf):
    def body(d_ref, z_ref, b_ref, dz_ref, dbf_ref):
        tri = _tri(False)
        carry = jnp.zeros((HEADS, 1), F32)
        tot = jnp.zeros((HEADS, 1), F32)
        for t in reversed(range(S // LANES)):
            sl = slice(t * LANES, (t + 1) * LANES)
            df = -d_ref[:, sl]
            c = lax.dot_general(df, tri, NN, preferred_element_type=F32,
                                precision=lax.Precision.HIGHEST) + carry
            carry = c[:, 0:1]
            z = z_ref[:, sl] + b_ref[...]
            dz = c * _sigmoid(-z)
            dz_ref[:, sl] = dz
            tot = tot + jnp.sum(dz, axis=1, keepdims=True)
        dbf_ref[...] = tot

    return pl.pallas_call(
        body, name="fox_gates_bwd",
        out_shape=[jax.ShapeDtypeStruct((HEADS, S), F32), jax.ShapeDtypeStruct((HEADS, 1), F32)],
        compiler_params=_params(),
    )(dbias, zt, bf)


def _mod_part(c_all, w_ada, b_cols):
    def body(c_ref, w_ref, b_ref, o_ref, s_ref):
        c = c_ref[...]
        sc = c * _sigmoid(c)
        s_ref[...] = sc
        o_ref[...] = lax.dot_general(sc, w_ref[...], NN, preferred_element_type=F32,
                                     precision=lax.Precision.HIGHEST) + b_ref[...]

    return pl.pallas_call(
        body, name="mod_part",
        out_shape=[jax.ShapeDtypeStruct((N_DEV, w_ada.shape[1]), F32), jax.ShapeDtypeStruct(c_all.shape, F32)],
        compiler_params=_params(),
    )(c_all, w_ada, b_cols)


def _w_ada_grad(sc_t, dm):
    def body(s_ref, d_ref, o_ref):
        acc = jnp.zeros(o_ref.shape, F32)
        for b in range(N_DEV):
            acc = acc + s_ref[:, b:b + 1] * d_ref[b:b + 1, :]
        o_ref[...] = acc

    return pl.pallas_call(
        body, name="w_ada_grad", out_shape=jax.ShapeDtypeStruct((sc_t.shape[0], dm.shape[1]), F32),
        compiler_params=_params(),
    )(sc_t, dm)


def _adamw(w, m, v, parts, name, own=None, slot=None):
    rows, cols = w.shape
    n = parts.shape[0]
    tr = rows if rows <= 512 else 256

    def body(*refs):
        if own is not None:
            s_ref, refs = refs[0], refs[1:]
            w_ref, m_ref, v_ref, p_ref, o_ref, g_out, d_out, m_out, v_out = refs
            terms = [jnp.where(s_ref[0] == kk, o_ref[0], p_ref[kk]) for kk in range(n)]
        else:
            w_ref, m_ref, v_ref, p_ref, g_out, d_out, m_out, v_out = refs
            terms = [p_ref[kk] for kk in range(n)]
        g = terms[0].astype(F32)
        for term in terms[1:]:
            g = g + term.astype(F32)
        g_out[...] = g
        d_out[...], m_out[...], v_out[...] = _adamw_math(w_ref[...], g, m_ref[...], v_ref[...])

    spec = pl.BlockSpec((tr, cols), lambda i, *_: (i, 0))
    in_specs = [spec, spec, spec, pl.BlockSpec((n, tr, cols), lambda i, *_: (0, i, 0))]
    out_shape = [jax.ShapeDtypeStruct((rows, cols), F32)] * 4
    if own is None:
        return pl.pallas_call(
            body, name=name, grid=(rows // tr,), in_specs=in_specs, out_specs=[spec] * 4, out_shape=out_shape,
            compiler_params=_params(("parallel",)),
        )(w, m, v, parts)
    in_specs.append(pl.BlockSpec((1, tr, cols), lambda i, s: (s[0], i, 0)))
    return pl.pallas_call(
        body, name=name, out_shape=out_shape, compiler_params=_params(("parallel",)),
        grid_spec=pltpu.PrefetchScalarGridSpec(num_scalar_prefetch=1, grid=(rows // tr,), in_specs=in_specs,
                                               out_specs=[spec] * 4),
    )(slot, w, m, v, parts, own)


def _adamw_math(w, g, m, v):
    mm = ADAM_B1 * m + (1.0 - ADAM_B1) * g
    vv = ADAM_B2 * v + (1.0 - ADAM_B2) * (g * g)
    m_hat = mm / (1.0 - ADAM_B1 ** ADAM_STEP)
    v_hat = vv / (1.0 - ADAM_B2 ** ADAM_STEP)
    return -ADAM_LR * (m_hat / (jnp.sqrt(v_hat) + ADAM_EPS) + ADAM_WD * w), mm, vv


def _adamw_rows(bundles, offsets, ws, ms, vs, err_off, err_width):
    k = len(ws)

    def body(*refs):
        b_ref = refs[0]
        w_refs, m_refs, v_refs = refs[1:1 + k], refs[1 + k:1 + 2 * k], refs[1 + 2 * k:1 + 3 * k]
        outs = refs[1 + 3 * k:]
        g_all = b_ref[0]
        for kk in range(1, N_DEV):
            g_all = g_all + b_ref[kk]
        for i in range(k):
            width = w_refs[i].shape[1]
            g = g_all[:, offsets[i]:offsets[i] + width]
            outs[4 * i][...] = g
            outs[4 * i + 1][...], outs[4 * i + 2][...], outs[4 * i + 3][...] = _adamw_math(
                w_refs[i][...], g, m_refs[i][...], v_refs[i][...])
        outs[4 * k][...] = g_all[:, err_off:err_off + err_width]

    out_shape = []
    for w_ in ws:
        out_shape += [jax.ShapeDtypeStruct(w_.shape, F32)] * 4
    out_shape.append(jax.ShapeDtypeStruct((1, err_width), F32))
    res = pl.pallas_call(body, name="adamw_rows", out_shape=out_shape, compiler_params=_params())(bundles, *ws, *ms, *vs)
    return [tuple(res[4 * i:4 * i + 4]) for i in range(k)], res[-1]


def _coords():
    return lax.axis_index("x"), lax.axis_index("y"), lax.axis_index("c")


def _flat(px, py, pc):
    return 4 * px + 2 * py + pc


def _all_gather(arrs, name):
    n = len(arrs)

    def body(*refs):
        ins, outs = refs[:n], refs[n:2 * n]
        send, recv, lsem = refs[2 * n:]
        x, y, c = _coords()
        me, sibling = (x, y, c), (x, y, 1 - c)
        chips = [(1 - x, y), (x, 1 - y), (1 - x, 1 - y)]

        def copy(a, kk, block, to, src=None):
            slot = outs[a].at[_flat(*block)]
            return pltpu.make_async_remote_copy(
                src_ref=slot if src is None else src, dst_ref=slot,
                send_sem=send.at[a, kk], recv_sem=recv.at[a, kk],
                device_id=to, device_id_type=MESH)

        mine = [pltpu.make_async_copy(ins[a], outs[a].at[_flat(*me)], lsem.at[a]) for a in range(n)]
        for cp in mine:
            cp.start()
        first = []
        for a in range(n):
            first.append(copy(a, 0, me, sibling, src=ins[a]))
            first += [copy(a, 1 + j, me, (*chip, c), src=ins[a]) for j, chip in enumerate(chips)]
        for cp in first:
            cp.start()
        passed = []
        for j, chip in enumerate(chips):
            for a in range(n):
                copy(a, 1 + j, (*chip, c), me).wait_recv()
                cp = copy(a, 4 + j, (*chip, c), sibling)
                cp.start()
                passed.append(cp)
        for a in range(n):
            copy(a, 0, sibling, me).wait_recv()
        for j, chip in enumerate(chips):
            for a in range(n):
                copy(a, 4 + j, (*chip, 1 - c), me).wait_recv()
        for cp in first + passed:
            cp.wait_send()
        for cp in mine:
            cp.wait()

    any_spec = pl.BlockSpec(memory_space=pl.ANY)
    return pl.pallas_call(
        body, name=name,
        in_specs=[any_spec] * n, out_specs=[any_spec] * n,
        out_shape=[jax.ShapeDtypeStruct((N_DEV,) + a.shape, a.dtype) for a in arrs],
        scratch_shapes=[pltpu.SemaphoreType.DMA((n, 7)), pltpu.SemaphoreType.DMA((n, 7)),
                        pltpu.SemaphoreType.DMA((n,))],
    )(*arrs)


def _peer_list():
    x, y, c = _coords()
    return [((1 - x if r & 4 else x), (1 - y if r & 2 else y), (1 - c if r & 1 else c)) for r in range(1, N_DEV)]


def _copy_plan(mode, src, land):
    x, y, c = _coords()
    me = _flat(x, y, c)
    if mode == "gather":
        return [(src, land.at[me], peer) for peer in _peer_list()]
    if mode == "exchange":
        return [(src.at[_flat(*peer)], land.at[me], peer) for peer in _peer_list()]
    if mode == "pair":
        return [(src.at[_flat(q // 2, q % 2, 1 - c)], land.at[q], (x, y, 1 - c)) for q in range(N_DEV // 2)]
    chips = [((1 - x if r & 2 else x), (1 - y if r & 1 else y)) for r in range(1, N_DEV // 2)]
    if mode == "chips":
        return [(src.at[2 * qx + qy], land.at[2 * x + y], (qx, qy, c)) for qx, qy in chips]
    if mode == "spread":
        return [(src, land.at[me], (x, y, 1 - c))] + [(src, land.at[me], (qx, qy, c)) for qx, qy in chips]
    assert mode == "forward"
    return [(land.at[_flat(qx, qy, c)], land.at[_flat(qx, qy, c)], (x, y, 1 - c)) for qx, qy in chips]


N_COPIES = dict(gather=N_DEV - 1, exchange=N_DEV - 1, pair=N_DEV // 2, chips=N_DEV // 2 - 1, spread=N_DEV // 2,
                forward=N_DEV // 2 - 1)


def _land_shape(mode, shape):
    return {"gather": (N_DEV,) + shape, "spread": (N_DEV,) + shape, "exchange": shape,
            "pair": (N_DEV // 2,) + shape[1:], "chips": shape}[mode]


HBM_SPEC = pl.BlockSpec(memory_space=pltpu.HBM)
SEM_SPEC = pl.BlockSpec(memory_space=pltpu.SEMAPHORE)
ANY_SPEC = pl.BlockSpec(memory_space=pl.ANY)
SIDE_EFFECT = pltpu.SideEffectType.DATAFLOW_SIDE_EFFECTING


def _async_start(groups, modes, after, name):
    modes = [modes] * len(groups) if isinstance(modes, str) else list(modes)
    arrs = [(a, m) for g, m in zip(groups, modes) for a in g]
    n = len(arrs)
    fresh = [i for i, (_, m) in enumerate(arrs) if m != "forward"]

    def body(*refs):
        srcs, new_lands = refs[:n], refs[n:n + len(fresh)]
        outs = refs[n + len(fresh) + 1:]
        lands = list(srcs)
        for k, i in enumerate(fresh):
            lands[i] = new_lands[k]
        for ai, (_, mode) in enumerate(arrs):
            for src_ref, dst_ref, peer in _copy_plan(mode, srcs[ai], lands[ai]):
                pltpu.make_async_remote_copy(src_ref=src_ref, dst_ref=dst_ref, send_sem=outs[2 * ai],
                                             recv_sem=outs[2 * ai + 1], device_id=peer, device_id_type=MESH).start()
        outs[-1][...] = jnp.zeros(outs[-1].shape, F32)

    land_shapes = [(_land_shape(arrs[i][1], arrs[i][0].shape), arrs[i][0].dtype) for i in fresh]
    n_buf = n + len(fresh)
    out_shape = [pltpu.SemaphoreType.DMA(())] * (2 * n)
    out_shape += [pltpu.HBM(a.shape, a.dtype) for a, _ in arrs]
    out_shape += [pltpu.HBM(shape, dt) for shape, dt in land_shapes]
    out_shape.append(jax.ShapeDtypeStruct((8, LANES), F32))
    res = pl.pallas_call(
        body, name=name, out_shape=tuple(out_shape),
        in_specs=[HBM_SPEC] * n_buf + [ANY_SPEC],
        out_specs=tuple([SEM_SPEC] * (2 * n) + [HBM_SPEC] * n_buf + [pl.BlockSpec(memory_space=pltpu.VMEM)]),
        input_output_aliases={i: 2 * n + i for i in range(n_buf)},
        compiler_params=pltpu.CompilerParams(has_side_effects=SIDE_EFFECT),
    )(*[pltpu.with_memory_space_constraint(a, pltpu.HBM) for a, _ in arrs],
      *[pltpu.with_memory_space_constraint(lax.empty(shape, dt), pltpu.HBM) for shape, dt in land_shapes],
      after)
    sems, thru = res[:2 * n], res[2 * n:-1]
    land_of = {i: thru[n + k] for k, i in enumerate(fresh)}
    states, idx = [], 0
    for g, mode in zip(groups, modes):
        ids = range(idx, idx + len(g))
        idx += len(g)
        states.append(([sems[2 * i] for i in ids], [sems[2 * i + 1] for i in ids],
                       None if mode == "forward" else [thru[i] for i in ids],
                       [land_of.get(i, thru[i]) for i in ids], mode))
    return states, res[-1]


def _async_wait(state, after, name):
    sends, recvs, srcs, lands, mode = state
    g = len(lands)
    bufs = (list(srcs) if srcs is not None else []) + list(lands)
    nb = len(bufs)

    def body(*refs):
        l_refs, sems = refs[nb - g:nb], refs[nb:nb + 2 * g]
        for ai in range(g):
            moved = l_refs[ai].at[pl.ds(0, N_COPIES[mode])]
            cp = pltpu.make_async_remote_copy(src_ref=moved, dst_ref=moved, send_sem=sems[ai], recv_sem=sems[g + ai],
                                              device_id=_coords(), device_id_type=MESH)
            cp.wait_send()
            cp.wait_recv()

    res = pl.pallas_call(
        body, name=name,
        out_shape=tuple(pltpu.HBM(a.shape, a.dtype) for a in bufs),
        in_specs=[HBM_SPEC] * nb + [SEM_SPEC] * (2 * g) + [ANY_SPEC],
        out_specs=tuple([HBM_SPEC] * nb),
        input_output_aliases={i: i for i in range(nb)},
        compiler_params=pltpu.CompilerParams(has_side_effects=SIDE_EFFECT),
    )(*bufs, *sends, *recvs, after)
    return (list(res[:nb - g]) if srcs is not None else None), list(res[nb - g:])


def _add_blocks(a, b):
    def body(a_ref, b_ref, o_ref):
        o_ref[...] = (a_ref[...].astype(F32) + b_ref[...].astype(F32)).astype(o_ref.dtype)

    spec = pl.BlockSpec((1,) + a.shape[1:], lambda i: (i, 0, 0))
    return pl.pallas_call(
        body, name="add_blocks", grid=(a.shape[0],), in_specs=[spec, spec], out_specs=spec,
        out_shape=jax.ShapeDtypeStruct(a.shape, a.dtype), compiler_params=_params(("parallel",)),
    )(a, b)


def _with_own(land, own, me):
    return lax.dynamic_update_index_in_dim(land, own, me, 0)


IN_SPLITS = (512, 512, 512, 8, 768, 256, 32, 1024, 1024)


def _from_shards(g, fn, out_widths, name, own=None, slot=None):
    _, k, n = g.shape
    tr = min(k, 256)

    def body(*refs):
        if own is not None:
            s_ref, g_ref, own_ref = refs[:3]
            cols = [jnp.where(s_ref[0] == j, own_ref[...], g_ref[j]) for j in range(N_DEV)]
        else:
            g_ref = refs[0]
            cols = [g_ref[j] for j in range(N_DEV)]
        for o_ref, val in zip(refs[-len(out_widths):], fn(jnp.concatenate(cols, axis=1))):
            o_ref[...] = val

    in_specs = [pl.BlockSpec((N_DEV, tr, n), lambda i, *_: (0, i, 0))]
    out_spec = [pl.BlockSpec((tr, wd), lambda i, *_: (i, 0)) for wd in out_widths]
    out_shape = [jax.ShapeDtypeStruct((k, wd), g.dtype) for wd in out_widths]
    if own is None:
        return pl.pallas_call(body, name=name, grid=(k // tr,), in_specs=in_specs, out_specs=out_spec,
                              out_shape=out_shape, compiler_params=_params(("parallel",)))(g)
    in_specs.append(pl.BlockSpec((tr, n), lambda i, *_: (i, 0)))
    return pl.pallas_call(
        body, name=name, out_shape=out_shape, compiler_params=_params(("parallel",)),
        grid_spec=pltpu.PrefetchScalarGridSpec(num_scalar_prefetch=1, grid=(k // tr,), in_specs=in_specs, out_specs=out_spec),
    )(slot, g, own)


def _unshard_cols(g, own=None, slot=None):
    return _from_shards(g, lambda full: (full,), [N_DEV * g.shape[2]], "unshard_cols_%d" % g.shape[2], own, slot)[0]


FFN_T = 256
FFN_SHARD = 2 * D_FF // N_DEV


def _unshard_ffn_in(g, own=None, slot=None):
    def pairs(full):
        parts = []
        for j in range(D_FF // FFN_T):
            parts += [full[:, j * FFN_T:(j + 1) * FFN_T], full[:, D_FF + j * FFN_T:D_FF + (j + 1) * FFN_T]]
        return (jnp.concatenate(parts, axis=1),)

    return _from_shards(g, pairs, [2 * D_FF], "unshard_ffn_in", own, slot)[0]


def _shard_ffn_in(w):
    tr = 256

    def body(w_ref, o_ref):
        x = w_ref[...]
        nb = D_FF // FFN_T
        full = jnp.concatenate([x[:, (2 * j + half) * FFN_T:(2 * j + half + 1) * FFN_T]
                                for half in range(2) for j in range(nb)], axis=1)
        for j in range(N_DEV):
            o_ref[j] = full[:, j * FFN_SHARD:(j + 1) * FFN_SHARD]

    return pl.pallas_call(
        body, name="shard_ffn_in", grid=(D // tr,),
        in_specs=[pl.BlockSpec((tr, 2 * D_FF), lambda i: (i, 0))],
        out_specs=pl.BlockSpec((N_DEV, tr, FFN_SHARD), lambda i: (0, i, 0)),
        out_shape=jax.ShapeDtypeStruct((N_DEV, D, FFN_SHARD), w.dtype),
        compiler_params=_params(("parallel",)),
    )(w)


def _shard_cols(w):
    k, n = w.shape[0], w.shape[1] // N_DEV
    tr = min(k, 256)

    def body(w_ref, o_ref):
        full = w_ref[...]
        for j in range(N_DEV):
            o_ref[j] = full[:, j * n:(j + 1) * n]

    return pl.pallas_call(
        body, name="shard_cols_%d" % n, grid=(k // tr,),
        in_specs=[pl.BlockSpec((tr, N_DEV * n), lambda i: (i, 0))],
        out_specs=pl.BlockSpec((N_DEV, tr, n), lambda i: (0, i, 0)),
        out_shape=jax.ShapeDtypeStruct((N_DEV, k, n), w.dtype),
        compiler_params=_params(("parallel",)),
    )(w)


IN_OFFS = tuple(sum(IN_SPLITS[:i]) for i in range(len(IN_SPLITS) + 1))
IN_SHARD = IN_OFFS[-1] // N_DEV
REGROUP_ROWS = 128


def _w_in_regroup(g, own=None, slot=None):
    def groups(full):
        fq, fk, fv, wf, cq, ckv, kr, gf, gm = [full[:, IN_OFFS[i]:IN_OFFS[i + 1]] for i in range(9)]
        rows = full.shape[0]
        w_a = jnp.concatenate([cq, ckv, gf, gm, wf, jnp.zeros((rows, 56), BF16), kr, jnp.zeros((rows, 32), BF16)], axis=1)
        return w_a, jnp.concatenate([fq, fk, fv], axis=1)

    return _from_shards(g, groups, [3200, 1536], "w_in_regroup", own, slot)


def _w_in_ungroup(da, db_):
    def body(a_ref, b_ref, o_ref):
        a = a_ref[...]
        full = jnp.concatenate([b_ref[...], a[:, 3072:3080], a[:, 0:768], a[:, 768:1024], a[:, 3136:3168],
                                a[:, 1024:3072]], axis=1)
        for j in range(N_DEV):
            o_ref[j] = full[:, j * IN_SHARD:(j + 1) * IN_SHARD]

    tr = REGROUP_ROWS
    return pl.pallas_call(
        body, name="w_in_ungroup", grid=(D // tr,),
        in_specs=[pl.BlockSpec((tr, 3200), lambda i: (i, 0)), pl.BlockSpec((tr, 1536), lambda i: (i, 0))],
        out_specs=pl.BlockSpec((N_DEV, tr, IN_SHARD), lambda i: (0, i, 0)),
        out_shape=jax.ShapeDtypeStruct((N_DEV, D, IN_SHARD), BF16),
        compiler_params=_params(("parallel",)),
    )(da, db_)


def _prepare_weights(g, own=None, slot=None):
    w = {}
    if own is not None:
        small = ("w_uq", "w_ukv", "w_out", "w_ffn_out")
        g = {n: (_with_own(a, own[n], slot[0]) if n in small else a) for n, a in g.items()}
    pick = (lambda n: (own[n], slot)) if own is not None else (lambda n: (None, None))
    if "w_in" in g:
        w["w_a"], w["w_b"] = _w_in_regroup(g["w_in"], *pick("w_in"))
    if "w_uq" in g:
        w_uq = g["w_uq"].reshape(Q_LORA, HEADS, 96)
        w["w_uq"] = jnp.pad(w_uq, ((0, 0), (0, 0), (0, 32))).reshape(Q_LORA, HEADS * LANES)
        ukv = g["w_ukv"]
        w["w_k"] = jnp.transpose(jnp.pad(ukv[:, :, :64], ((0, 0), (0, 0), (0, 64))), (1, 0, 2)).reshape(KV_LORA, HEADS * LANES)
        w["w_v"] = jnp.transpose(ukv[:, :, 64:], (1, 0, 2)).reshape(KV_LORA, HEADS * HEAD_DIM)
    if "w_out" in g:
        w["w_pf"] = _unshard_cols(g["w_proj_fox"], *pick("w_proj_fox"))
        w["w_pm"] = _unshard_cols(g["w_proj_mla"], *pick("w_proj_mla"))
        w["w_out"] = g["w_out"].reshape(D, D)
    if "w_ffn_in" in g:
        w["w_ffn_in"] = _unshard_ffn_in(g["w_ffn_in"], *pick("w_ffn_in"))
        w["w_ffn_out"] = g["w_ffn_out"].reshape(D_FF, D)
    return w


def _shard_grads(dw):
    out = {}
    if "w_a" in dw:
        out["w_in"] = _w_in_ungroup(dw["w_a"], dw["w_b"])
    if "w_uq" in dw:
        w_uq = dw["w_uq"].reshape(Q_LORA, HEADS, LANES)[:, :, :96].reshape(Q_LORA, Q_LORA)
        out["w_uq"] = w_uq.reshape(N_DEV, Q_LORA // N_DEV, Q_LORA)
        k_part = dw["w_k"].reshape(KV_LORA, HEADS, LANES)[:, :, :64]
        v_part = dw["w_v"].reshape(KV_LORA, HEADS, HEAD_DIM)
        out["w_ukv"] = jnp.transpose(jnp.concatenate([k_part, v_part], axis=2), (1, 0, 2))
    if "w_out" in dw:
        out["w_proj_fox"] = _shard_cols(dw["w_pf"])
        out["w_proj_mla"] = _shard_cols(dw["w_pm"])
        out["w_out"] = dw["w_out"].reshape(N_DEV, D // N_DEV, D)
    if "w_ffn_in" in dw:
        out["w_ffn_in"] = _shard_ffn_in(dw["w_ffn_in"])
        out["w_ffn_out"] = dw["w_ffn_out"].reshape(N_DEV, D_FF // N_DEV, D)
    return out


def _fwd_bwd(x, pos, mod, target, w, vec, wts, send, relay):
    shift_mix, scale_mix, gate_mix, shift_ffn, scale_ffn, gate_ffn = [mod[:, i * D:(i + 1) * D] for i in range(6)]
    g_pre_mix, g_post_mix, g_pre_ffn, g_post_ffn = vec["g_pre_mix"], vec["g_post_mix"], vec["g_pre_ffn"], vec["g_post_ffn"]
    g_q, g_kv = vec["g_q_lora"], vec["g_kv_lora"]

    inv_freq = 1.0 / (ROPE_THETA ** (jnp.arange(0, ROPE_DIM, 2, dtype=F32) / ROPE_DIM))
    invf = jnp.concatenate([jnp.zeros((64,), F32), inv_freq, inv_freq, jnp.zeros((32,), F32)]).reshape(1, LANES)
    ct, sa, sb = _rope_tables(pos, invf)

    def pre1(xv, g, sc, sh):
        return (xv * _rstd(xv) * g) * (1.0 + sc) + sh
    proj_a, h = _mm_epi(x, w["w_a"], "nn", 640, lambda r: ((r,), ()), "in_proj_a", 1024, outs=[(3200, 640, F32)],
                        pro=(pre1, [g_pre_mix, scale_mix, shift_mix], 0))
    qkv = _mm(h, w["w_b"], "nn", BF16, "in_proj_b")

    def lora_norm(cv, g):
        return cv * _rstd(cv) * g
    w = {**w, **wts("lora", qkv)}
    tables = [(ct, LANES), (sa, LANES), (sb, LANES)]

    def rope_q(qv, c_, a_, b_):
        return (jnp.concatenate([_rope(qv[:, hd * LANES:(hd + 1) * LANES], c_, a_, b_) for hd in range(HEADS)], axis=1),), ()
    q_m, cqn = _mm_epi(proj_a, w["w_uq"], "nn", D, rope_q, "mla_uq", 512, rows=tables, outs=[(D, D, BF16)],
                       pro=(lora_norm, [g_q], 0))

    def rope_k(kv, misc, c_, a_, b_):
        lane = lax.broadcasted_iota(jnp.int32, (1, LANES), 1)
        kpe = jnp.where((lane >= 64) & (lane < 96), _rope(misc, c_, a_, b_), 0.0)
        return (jnp.concatenate([kv[:, hd * LANES:(hd + 1) * LANES] + kpe for hd in range(HEADS)], axis=1),), ()
    k_m, ckvn = _mm_epi(proj_a, w["w_k"], "nn", D, rope_k, "mla_uk", 512, rows=[(proj_a, LANES, 24)] + tables,
                        outs=[(D, D, BF16)], pro=(lora_norm, [g_kv], Q_LORA // KV_LORA))
    v_m = _mm(ckvn, w["w_v"], "nn", BF16, "mla_uv")

    bf = jnp.transpose(vec["b_forget"])
    zt, neg_f = _fox_gates(proj_a, 24, bf)
    bias = neg_f.reshape(HEADS, N_ATT, 1, ATT_T)
    o_b, lse_b = _attn_fwd(q_m, 0, k_m, 0, v_m, 0, 2 * LANES, 1.0 / math.sqrt(64 + ROPE_DIM), None, "mla_attn")
    bias = bias + wts("relay_proj", o_b)["tok"][0, 0]
    o_a, lse_a = _attn_fwd(qkv, 0, qkv, 4, qkv, 8, LANES, 1.0 / math.sqrt(HEAD_DIM), bias, "fox_attn")

    w = {**w, **wts("proj", o_a)}
    gate_mix = gate_mix + wts("relay_ffn", o_a)["tok"][0, 0]
    pa = _mm(o_a, w["w_pf"], "nn", BF16, "proj_fox")

    def merge(pb_, gf, gm, pa_):
        return (_sigmoid(gf) * pa_ + _sigmoid(gm) * pb_, pb_), ()
    merged, pb = _mm_epi(o_b, w["w_pm"], "nn", 512, merge, "proj_mla", 1024,
                         rows=[(proj_a, 512, 2), (proj_a, 512, 4), (pa, 512)], outs=[(D, 512, BF16), (D, 512, BF16)])
    def post1(yv, xv, gate, gpost, gpre, sc, sh):
        x1 = xv + gate * (yv * _rstd(yv) * gpost)
        return (x1, (x1 * _rstd(x1) * gpre) * (1.0 + sc) + sh, yv), ()
    x1, h2, y = _mm_epi(merged, w["w_out"], "nn", D, post1, "out_proj", 512, rows=[(x, D)],
                        vecs=[gate_mix, g_post_mix, g_pre_ffn, scale_ffn, shift_ffn],
                        outs=[(D, D, F32), (D, D, BF16), (D, D, F32)])
    w = {**w, **wts("ffn", h2)}

    def swiglu(r):
        g, u = r[:, :FFN_T], r[:, FFN_T:]
        return (g * _sigmoid(g) * u, r), ()
    act, gu = _mm_epi(h2, w["w_ffn_in"], "nn", 2 * FFN_T, swiglu, "ffn_in", 1024,
                      outs=[(D_FF, FFN_T, BF16), (2 * D_FF, 2 * FFN_T, BF16)])

    def head(y2v, x1v, tv, gate, gpost):
        r = _rstd(y2v)
        yn = y2v * r
        n2 = yn * gpost
        err = (x1v + gate * n2) - tv
        dx2 = err * (1.0 / D)
        dn2 = dx2 * gate
        dy2 = _norm_bwd(dn2 * gpost, yn, r)
        return (dx2, dy2), (_colsum(err * err), _colsum(dx2 * n2), _colsum(dn2 * yn))
    dx2, dy2, err_cols, d_gate_ffn, d_g_post_ffn = _mm_epi(
        act, w["w_ffn_out"], "nn", D, head, "ffn_out", 512, rows=[(x1, D), (target, D)], vecs=[gate_ffn, g_post_ffn],
        outs=[(D, D, F32), (D, D, BF16)], sums=[D, D, D])

    def swiglu_bwd(da, guv):
        g, u = guv[:, :FFN_T].astype(F32), guv[:, FFN_T:].astype(F32)
        sg = _sigmoid(g)
        return (jnp.concatenate([da * u * (sg * (1.0 + g * (1.0 - sg))), da * (g * sg)], axis=1),), ()
    (dgu,) = _mm_epi(dy2, w["w_ffn_out"], "nt", FFN_T, swiglu_bwd, "ffn_out_dx", 1024, rows=[(gu, 2 * FFN_T)],
                     outs=[(2 * D_FF, 2 * FFN_T, BF16)])
    dw = {"w_ffn_out": _mm(act, dy2, "tn", BF16, "ffn_out_dw")}
    dw["w_ffn_in"] = _mm(h2, dgu, "tn", BF16, "ffn_in_dw")
    gate_mix = gate_mix + send({n: dw.pop(n) for n in ("w_ffn_in", "w_ffn_out")})[0, 0]

    def mid(dh, x1v, dx2v, yv, gpre, sc, gate, gpost):
        r2 = _rstd(x1v)
        x1n = x1v * r2
        t = dh * x1n
        dx1 = dx2v + _norm_bwd(dh * (gpre * (1.0 + sc)), x1n, r2)
        ry = _rstd(yv)
        yn = yv * ry
        dn1 = dx1 * gate
        dy = _norm_bwd(dn1 * gpost, yn, ry)
        sums = (_colsum(dh), _colsum(t) * gpre, _colsum(t) * (1.0 + sc), _colsum(dx1 * (yn * gpost)), _colsum(dn1 * yn))
        return (dx1, dy), sums
    dx1, dy, d_shift_ffn, d_scale_ffn, d_g_pre_ffn, d_gate_mix, d_g_post_mix = _mm_epi(
        dgu, w["w_ffn_in"], "nt", D, mid, "ffn_in_dx", 512, rows=[(x1, D), (dx2, D), (y, D)],
        vecs=[g_pre_ffn, scale_ffn, gate_mix, g_post_mix], outs=[(D, D, F32), (D, D, BF16)], sums=[D] * 5)

    dw["w_out"] = _mm(merged, dy, "tn", BF16, "out_proj_dw")

    def merge_bwd(dm, gf, gm, pa_, pb_):
        sf, sm = _sigmoid(gf), _sigmoid(gm)
        return (dm * sf, dm * sm, dm * pa_ * (sf * (1.0 - sf)), dm * pb_ * (sm * (1.0 - sm))), ()
    dpa, dpb, dgf, dgm = _mm_epi(dy, w["w_out"], "nt", 512, merge_bwd, "out_proj_dx", 1024,
                                 rows=[(proj_a, 512, 2), (proj_a, 512, 4), (pa, 512), (pb, 512)],
                                 outs=[(D, 512, BF16)] * 4)
    do_a = _mm(dpa, w["w_pf"], "nt", BF16, "proj_fox_dx")
    do_b = _mm(dpb, w["w_pm"], "nt", BF16, "proj_mla_dx")
    dw["w_pf"] = _mm(o_a, dpa, "tn", BF16, "proj_fox_dw")
    dw["w_pm"] = _mm(o_b, dpb, "tn", BF16, "proj_mla_dw")
    bias = bias + send({n: dw.pop(n) for n in ("w_out", "w_pf", "w_pm")})[0, 0]

    sc_a, sc_b = 1.0 / math.sqrt(HEAD_DIM), 1.0 / math.sqrt(64 + ROPE_DIM)
    dq_a, dk_a, dv_a, dbias = _attn_grad(qkv, 0, qkv, 4, qkv, 8, do_a, lse_a, LANES, sc_a, bias, BF16, "fox_attn_bwd")
    dq_m, dk_m, dv_m = _attn_grad(q_m, 0, k_m, 0, v_m, 0, do_b, lse_b, 2 * LANES, sc_b, None, F32, "mla_attn_bwd")

    def mla_rope_bwd(dq, dk, c_, a_, b_):
        lane = lax.broadcasted_iota(jnp.int32, (1, LANES), 1)
        dqs = [_rope_t(dq[:, hd * LANES:(hd + 1) * LANES], c_, a_, b_) for hd in range(HEADS)]
        dkpe = dk[:, 0:LANES]
        for hd in range(1, HEADS):
            dkpe = dkpe + dk[:, hd * LANES:(hd + 1) * LANES]
        dkpe = jnp.where((lane >= 64) & (lane < 96), dkpe, 0.0)
        dkr = jnp.where((lane >= 64) & (lane < 96), _rope_t(dkpe, c_, a_, b_), 0.0)
        return (jnp.concatenate(dqs, axis=1), dk, dkr), ()
    dqb, dkb, dkr = _rowwise(mla_rope_bwd, [(dq_m, D, 0), (dk_m, D, 0), (ct, LANES, 0), (sa, LANES, 0), (sb, LANES, 0)],
                             [], [(D, BF16), (D, BF16), (LANES, F32)], [], "mla_rope_bwd")
    def lora_q_bwd(dq, cq, gq):
        rq = _rstd(cq)
        cqh = cq * rq
        return (_norm_bwd(dq * gq, cqh, rq),), (_colsum(dq * cqh),)
    dcq, d_g_q = _mm_epi(dqb, w["w_uq"], "nt", Q_LORA, lora_q_bwd, "mla_uq_dx", 512, rows=[(proj_a, Q_LORA, 0)],
                         vecs=[g_q], outs=[(Q_LORA, Q_LORA, BF16)], sums=[Q_LORA])

    def lora_kv_bwd(dv_part, dk_part, ckv, gkv):
        dkv = dv_part + dk_part
        rk = _rstd(ckv)
        ckh = ckv * rk
        return (_norm_bwd(dkv * gkv, ckh, rk),), (_colsum(dkv * ckh),)
    dckv, d_g_kv = _mm_epi(dv_m, w["w_v"], "nt", KV_LORA, lora_kv_bwd, "mla_uv_dx", 1024,
                           rows=[(_mm(dkb, w["w_k"], "nt", F32, "mla_uk_dx"), KV_LORA), (proj_a, KV_LORA, 3)],
                           vecs=[g_kv], outs=[(KV_LORA, KV_LORA, BF16)], sums=[KV_LORA])

    dzt, d_bf = _fox_gates_bwd(dbias.reshape(HEADS, S), zt, bf)
    dmisc = (dkr + jnp.pad(jnp.transpose(dzt), ((0, 0), (0, LANES - HEADS)))).astype(BF16)
    dproj_a = jnp.concatenate([dcq, dckv, dgf, dgm, dmisc], axis=1)
    dqkv = jnp.concatenate([dq_a, dk_a, dv_a], axis=1)
    dw["w_a"] = _mm(h, dproj_a, "tn", BF16, "in_proj_a_dw")
    dw["w_b"] = _mm(h, dqkv, "tn", BF16, "in_proj_b_dw")
    tok = send(dw, True)
    dh_a = _mm(dproj_a, w["w_a"], "nt", F32, "in_proj_a_dx", dep=tok)
    tok = relay(dh_a)
    tok = send({"w_uq": _mm(cqn, dqb, "tn", BF16, "mla_uq_dw", dep=tok),
                "w_k": _mm(ckvn, dkb, "tn", BF16, "mla_uk_dw", dep=tok),
                "w_v": _mm(ckvn, dv_m, "tn", BF16, "mla_uv_dw", dep=tok)}, late=True)
    g_pre_mix = g_pre_mix + tok[0, 0]

    def first(dh_b, dh_a, xv, dx1v, gpre, sc):
        dhv = dh_b + dh_a
        r = _rstd(xv)
        xn = xv * r
        t = dhv * xn
        dx = dx1v + _norm_bwd(dhv * (gpre * (1.0 + sc)), xn, r)
        return (dx,), (_colsum(dhv), _colsum(t) * gpre, _colsum(t) * (1.0 + sc))
    grad_x, d_shift_mix, d_scale_mix, d_g_pre_mix = _mm_epi(
        dqkv, w["w_b"], "nt", D, first, "in_proj_b_dx", 512,
        rows=[(dh_a, D), (x, D), (dx1, D)],
        vecs=[g_pre_mix, scale_mix], outs=[(D, D, F32)], sums=[D] * 3)

    dmod = jnp.concatenate([d_shift_mix, d_scale_mix, d_gate_mix, d_shift_ffn, d_scale_ffn, d_gate_ffn], axis=1)
    small = dict(dmod=dmod, g_pre_mix=d_g_pre_mix, g_post_mix=d_g_post_mix, g_pre_ffn=d_g_pre_ffn,
                 g_post_ffn=d_g_post_ffn, g_q_lora=d_g_q, g_kv_lora=d_g_kv,
                 b_forget=jnp.pad(jnp.transpose(d_bf), ((0, 0), (0, LANES - HEADS))), err=err_cols)
    return grad_x, small


SMALL_ORDER = ("dmod", "g_pre_mix", "g_post_mix", "g_pre_ffn", "g_post_ffn", "g_q_lora", "g_kv_lora", "b_forget", "err")
SMALL_PARAM = {"dmod": "b_ada"}
MATRICES = ("w_in", "w_uq", "w_ukv", "w_proj_fox", "w_proj_mla", "w_out", "w_ffn_in", "w_ffn_out")
WEIGHTS = ("w_ada", "b_ada", "g_pre_mix", "g_post_mix", "g_pre_ffn", "g_post_ffn", "w_in", "b_forget", "g_q_lora",
           "w_uq", "g_kv_lora", "w_ukv", "w_proj_fox", "w_proj_mla", "w_out", "w_ffn_in", "w_ffn_out")


def kernel(x, c, positions, w_ada, b_ada, g_pre_mix, g_post_mix, g_pre_ffn, g_post_ffn, w_in, b_forget, g_q_lora, w_uq, g_kv_lora, w_ukv, w_proj_fox, w_proj_mla, w_out, w_ffn_in, w_ffn_out, loss_target, m_w_ada, m_b_ada, m_g_pre_mix, m_g_post_mix, m_g_pre_ffn, m_g_post_ffn, m_w_in, m_b_forget, m_g_q_lora, m_w_uq, m_g_kv_lora, m_w_ukv, m_w_proj_fox, m_w_proj_mla, m_w_out, m_w_ffn_in, m_w_ffn_out, v_w_ada, v_b_ada, v_g_pre_mix, v_g_post_mix, v_g_pre_ffn, v_g_post_ffn, v_w_in, v_b_forget, v_g_q_lora, v_w_uq, v_g_kv_lora, v_w_ukv, v_w_proj_fox, v_w_proj_mla, v_w_out, v_w_ffn_in, v_w_ffn_out):
    prm = dict(w_ada=w_ada, b_ada=b_ada, g_pre_mix=g_pre_mix, g_post_mix=g_post_mix, g_pre_ffn=g_pre_ffn,
               g_post_ffn=g_post_ffn, w_in=w_in, b_forget=b_forget, g_q_lora=g_q_lora, w_uq=w_uq, g_kv_lora=g_kv_lora,
               w_ukv=w_ukv, w_proj_fox=w_proj_fox, w_proj_mla=w_proj_mla, w_out=w_out, w_ffn_in=w_ffn_in, w_ffn_out=w_ffn_out)
    mom = dict(w_ada=m_w_ada, b_ada=m_b_ada, g_pre_mix=m_g_pre_mix, g_post_mix=m_g_post_mix, g_pre_ffn=m_g_pre_ffn,
               g_post_ffn=m_g_post_ffn, w_in=m_w_in, b_forget=m_b_forget, g_q_lora=m_g_q_lora, w_uq=m_w_uq,
               g_kv_lora=m_g_kv_lora, w_ukv=m_w_ukv, w_proj_fox=m_w_proj_fox, w_proj_mla=m_w_proj_mla, w_out=m_w_out,
               w_ffn_in=m_w_ffn_in, w_ffn_out=m_w_ffn_out)
    var = dict(w_ada=v_w_ada, b_ada=v_b_ada, g_pre_mix=v_g_pre_mix, g_post_mix=v_g_post_mix, g_pre_ffn=v_g_pre_ffn,
               g_post_ffn=v_g_post_ffn, w_in=v_w_in, b_forget=v_b_forget, g_q_lora=v_g_q_lora, w_uq=v_w_uq,
               g_kv_lora=v_g_kv_lora, w_ukv=v_w_ukv, w_proj_fox=v_w_proj_fox, w_proj_mla=v_w_proj_mla, w_out=v_w_out,
               w_ffn_in=v_w_ffn_in, w_ffn_out=v_w_ffn_out)
    me = _flat(*_coords())
    slot = jnp.reshape(me, (1,)).astype(jnp.int32)

    own = {n: prm[n][0].astype(BF16) for n in MATRICES}
    no_dep = jnp.zeros((8, LANES), F32)
    (st_c, st_in), tok = _async_start([[c], [own["w_in"]]], ["gather", "spread"], no_dep, "gather_in_start")
    (c_own,), (c_land,) = _async_wait(st_c, tok, "gather_c_wait")
    c_all = _with_own(c_land, c_own, me).reshape(N_DEV, D)
    ada_cols = w_ada.shape[2]
    b_cols = lax.dynamic_slice(b_ada, (0, me * ada_cols), (1, ada_cols))
    mod_cols, silu_c = _mod_part(c_all, w_ada[0], b_cols)
    (mod_all,) = _all_gather([mod_cols], "gather_mod")

    (w_in_own,), (w_in_land,) = _async_wait(st_in, mod_all, "gather_in_wait")
    (st_in,), tok = _async_start([[w_in_land]], "forward", no_dep, "gather_in_forward")
    _, (w_in_land,) = _async_wait(st_in, tok, "gather_in_forward_wait")
    w = _prepare_weights({"w_in": w_in_land}, {"w_in": w_in_own}, slot)
    later = dict(lora=("w_uq", "w_ukv"), proj=("w_proj_fox", "w_proj_mla", "w_out"), ffn=("w_ffn_in", "w_ffn_out"))
    states, tok = _async_start([[own[n] for n in names] for names in later.values()], ["gather", "spread", "spread"],
                               w["w_b"], "gather_rest_start")
    gather_state = dict(zip(later, states))
    own_thru = {}

    def wts(group, after):
        if group.startswith("relay_"):
            name = group[len("relay_"):]
            own_thru[name], lands = _async_wait(gather_state[name], after, "gather_" + name + "_wait")
            (gather_state[name],), t = _async_start([lands], "forward", no_dep, "gather_" + name + "_forward")
            return {"tok": t}
        srcs, lands = _async_wait(gather_state[group], after, "gather_" + group + "_landed")
        srcs = own_thru.get(group, srcs)
        return _prepare_weights(dict(zip(later[group], lands)), dict(zip(later[group], srcs)), slot)

    sent, late_sent, last = [], [], {}

    def send(grads, final=False, late=False):
        shards = _shard_grads(grads)
        names = list(shards)
        (state,), t = _async_start([[shards[n] for n in names]], "pair" if final else "exchange", no_dep,
                                   "exchange_" + names[0] + "_start")
        if final:
            last.update(names=names, state=state)
        else:
            (late_sent if late else sent).append((names, state))
        return t

    def relay(after):
        srcs, lands = _async_wait(last["state"], after, "exchange_pair_wait")
        sums = []
        for src, land in zip(srcs, lands):
            by_chip = src.reshape((N_DEV // 2, 2) + src.shape[1:])
            sums.append(_add_blocks(lax.dynamic_index_in_dim(by_chip, lax.axis_index("c"), 1, keepdims=False), land))
        (last["state"],), t = _async_start([sums], "chips", no_dep, "exchange_chips_start")
        return t

    mod = lax.dynamic_index_in_dim(mod_all, me, axis=1, keepdims=False).reshape(1, 6 * D) + tok[0, 0]

    vec = dict(g_pre_mix=g_pre_mix, g_post_mix=g_post_mix, g_pre_ffn=g_pre_ffn, g_post_ffn=g_post_ffn,
               g_q_lora=g_q_lora, g_kv_lora=g_kv_lora, b_forget=b_forget)
    pos = positions.astype(F32).reshape(S, 1)
    grad_x, small = _fwd_bwd(x[0], pos, mod, loss_target[0], w, vec, wts, send, relay)

    bundle = jnp.concatenate([small[n] for n in SMALL_ORDER], axis=1)
    (small_state,), tok = _async_start([[bundle]], "gather", jnp.zeros((8, LANES), F32), "gather_small_start")

    out = {}
    after = tok
    for names, state in sent:
        srcs, lands = _async_wait(state, after, "exchange_" + names[0] + "_wait")
        for n, src, land in zip(names, srcs, lands):
            out[n] = _adamw(prm[n][0], mom[n][0], var[n][0], land, "adamw_" + n, src, slot)
            after = out[n][0]
    srcs, lands = _async_wait(last["state"], after, "exchange_chips_wait")
    for n, src, land in zip(last["names"], srcs, lands):
        out[n] = _adamw(prm[n][0], mom[n][0], var[n][0], land, "adamw_" + n, src, slot // 2)
        after = out[n][0]
    for names, state in late_sent:
        srcs, lands = _async_wait(state, after, "exchange_" + names[0] + "_wait")
        for n, src, land in zip(names, srcs, lands):
            out[n] = _adamw(prm[n][0], mom[n][0], var[n][0], land, "adamw_" + n, src, slot)
            after = out[n][0]

    (own_bundle,), (bundle_all,) = _async_wait(small_state, after, "gather_small_wait")
    bundle_all = _with_own(bundle_all, own_bundle, me)
    dmod_all = bundle_all[:, 0, :6 * D]
    dm_cols = lax.dynamic_slice(dmod_all, (0, me * ada_cols), (N_DEV, ada_cols))
    g_ada = _w_ada_grad(jnp.transpose(silu_c), dm_cols)
    out["w_ada"] = _adamw(w_ada[0], m_w_ada[0], v_w_ada[0], g_ada[None], "adamw_w_ada")

    offsets, off = {}, 0
    for n in SMALL_ORDER:
        offsets[n] = off
        off += small[n].shape[1]
    names = [SMALL_PARAM.get(n, n) for n in SMALL_ORDER if n != "err"]
    results, err = _adamw_rows(bundle_all, [offsets[n] for n in SMALL_ORDER if n != "err"],
                               [prm[n] for n in names], [mom[n] for n in names], [var[n] for n in names],
                               offsets["err"], D)
    out.update(zip(names, results))
    loss = 0.5 * jnp.sum(err) / D

    res = [loss, grad_x[None]]
    for kind in range(4):
        for n in WEIGHTS:
            t = out[n][kind]
            res.append(t[None] if prm[n].ndim == 3 else t)
    return tuple(res)
```

```python
import functools
import math

import jax
import jax.numpy as jnp
from jax import lax
from jax.experimental import pallas as pl
from jax.experimental.pallas import tpu as pltpu

F32 = jnp.float32
BF16 = jnp.bfloat16

N_DEV = 8
S = 2048
D = 1024
D_FF = 2816
HEADS = 8
HEAD_DIM = 64
Q_LORA = 768
KV_LORA = 256
ROPE_DIM = 32
ROPE_THETA = 10000.0
NORM_EPS = 1e-6
LANES = 128
VMEM_LIMIT = 56 * 1024 * 1024

ADAM_LR = 0.001
ADAM_B1 = 0.9
ADAM_B2 = 0.999
ADAM_EPS = 1e-08
ADAM_WD = 0.01
ADAM_STEP = 10

ATT_T = 256
LOG2E = 1.4426950408889634
N_ATT = S // ATT_T

NN = (((1,), (0,)), ((), ()))
NT = (((1,), (1,)), ((), ()))
TN = (((0,), (0,)), ((), ()))
MESH = pl.DeviceIdType.MESH


def _params(sem=None):
    return pltpu.CompilerParams(dimension_semantics=sem, vmem_limit_bytes=VMEM_LIMIT)


def _pick(n, cap):
    best = None
    for t in range(LANES, cap + 1, LANES):
        if n % t == 0:
            best = t
    return best if best is not None else n


def _mm(a, b, mode, out_dtype, name, acc=None, dep=None):
    if mode == "nn":
        (m, k), (k2, n), dn = a.shape, b.shape, NN
    elif mode == "nt":
        (m, k), (n, k2), dn = a.shape, b.shape, NT
    else:
        (k, m), (k2, n), dn = a.shape, b.shape, TN
    assert k == k2, (a.shape, b.shape, mode)
    tn = _pick(n, 640)
    tm = _pick(m, 1536)
    osz = jnp.dtype(out_dtype).itemsize

    def need(tm_):
        blk = tm_ * k * 2 + tn * k * 2 + tm_ * tn * osz + (tm_ * tn * 4 if acc is not None else 0)
        return 2 * blk + tm_ * tn * 4
    while need(tm) > 36 * 1024 * 1024 and tm % 256 == 0:
        tm //= 2

    def body(*refs):
        a_ref, b_ref, o_ref = refs[0], refs[1], refs[-1]
        r = lax.dot_general(a_ref[...], b_ref[...], dn, preferred_element_type=F32)
        if acc is not None:
            r = r + refs[2][...]
        o_ref[...] = r.astype(o_ref.dtype)

    if mode == "tn":
        a_spec = pl.BlockSpec((k, tm), lambda i, j: (0, i))
    else:
        a_spec = pl.BlockSpec((tm, k), lambda i, j: (i, 0))
    if mode == "nt":
        b_spec = pl.BlockSpec((tn, k), lambda i, j: (j, 0))
    else:
        b_spec = pl.BlockSpec((k, tn), lambda i, j: (0, j))
    o_spec = pl.BlockSpec((tm, tn), lambda i, j: (i, j))
    in_specs = [a_spec, b_spec] + ([o_spec] if acc is not None else [])
    in_specs += [pl.BlockSpec(memory_space=pl.ANY)] if dep is not None else []
    args = (a, b) + ((acc,) if acc is not None else ()) + ((dep,) if dep is not None else ())
    return pl.pallas_call(
        body, name=name, grid=(m // tm, n // tn),
        in_specs=in_specs, out_specs=o_spec,
        out_shape=jax.ShapeDtypeStruct((m, n), out_dtype),
        compiler_params=_params(("parallel", "parallel")),
    )(*args)


def _mm_epi(a, b, mode, tnb, epi, name, tm, rows=(), vecs=(), outs=(), sums=(), pro=None):
    m = a.shape[0]
    k, nb = (b.shape if mode == "nn" else b.shape[::-1])
    dn = NN if mode == "nn" else NT
    pro_fn, pro_vecs, a_off = pro if pro is not None else (None, (), 0)
    n_in = 2 + len(rows) + len(vecs)
    n_all = n_in + len(pro_vecs)
    sub = min(tm, 256)

    def body(*refs):
        if pro is not None:
            a_out, a_scr = refs[-2:]
            refs = refs[:-2]

            @pl.when(pl.program_id(1) == 0)
            def _():
                a_scr[...] = pro_fn(refs[0][...], *[x[...] for x in refs[n_in:n_all]]).astype(BF16)
                a_out[...] = a_scr[...]
            a_ref = a_scr
        else:
            a_ref = refs[0]
        o_refs = refs[n_all:n_all + len(outs)]
        s_refs = refs[n_all + len(outs):]
        if sums:
            @pl.when((pl.program_id(0) == 0) & (pl.program_id(1) == 0))
            def _():
                for s_ref in s_refs:
                    s_ref[...] = jnp.zeros(s_ref.shape, F32)
        for c in range(tm // sub):
            rs = slice(c * sub, (c + 1) * sub)
            r = lax.dot_general(a_ref[rs, :], refs[1][...], dn, preferred_element_type=F32)
            o_vals, s_vals = epi(r, *[x[rs, :] for x in refs[2:2 + len(rows)]], *[x[...] for x in refs[2 + len(rows):n_in]])
            assert len(o_vals) == len(o_refs) and len(s_vals) == len(s_refs)
            for o_ref, val in zip(o_refs, o_vals):
                o_ref[rs, :] = val.astype(o_ref.dtype)
            for s_ref, val in zip(s_refs, s_vals):
                s_ref[...] += val

    once = dict(pipeline_mode=pl.Buffered(1)) if nb == tnb else {}
    if mode == "nn":
        b_spec = pl.BlockSpec((k, tnb), lambda i, j: (0, j), **once)
    else:
        b_spec = pl.BlockSpec((tnb, k), lambda i, j: (j, 0), **once)
    in_specs = [pl.BlockSpec((tm, k), lambda i, j: (i, a_off)), b_spec]
    rows = [tuple(r) + (0,) * (3 - len(r)) for r in rows]
    in_specs += [pl.BlockSpec((tm, w), functools.partial(lambda i, j, off: (i, j + off), off=off)) for _, w, off in rows]
    in_specs += [pl.BlockSpec(v.shape, lambda i, j: (0, 0)) for v in list(vecs) + list(pro_vecs)]
    out_specs = [pl.BlockSpec((tm, w), lambda i, j: (i, j)) for _, w, _ in outs]
    out_specs += [pl.BlockSpec((1, w), lambda i, j: (0, 0)) for w in sums]
    out_shape = [jax.ShapeDtypeStruct((m, full), dt) for full, _, dt in outs]
    out_shape += [jax.ShapeDtypeStruct((1, w), F32) for w in sums]
    if pro is not None:
        out_specs.append(pl.BlockSpec((tm, k), lambda i, j: (i, 0)))
        out_shape.append(jax.ShapeDtypeStruct((m, k), BF16))
    return pl.pallas_call(
        body, name=name, grid=(m // tm, nb // tnb),
        in_specs=in_specs, out_specs=out_specs, out_shape=out_shape,
        scratch_shapes=[pltpu.VMEM((tm, k), BF16)] if pro is not None else [],
        compiler_params=_params(("arbitrary", "arbitrary") if sums else ("parallel", "arbitrary" if pro is not None else "parallel")),
    )(a, b, *[r[0] for r in rows], *vecs, *pro_vecs)


def _rowwise(fn, row_ins, vec_ins, row_outs, sum_outs, name, tm=256):
    n_in = len(row_ins) + len(vec_ins)
    n_o = len(row_outs)
    rows = row_ins[0][0].shape[0]

    def body(*refs):
        vals = [r[...] for r in refs[:n_in]]
        outs = refs[n_in:]
        ro, so = fn(*vals)
        assert len(ro) == n_o and len(so) == len(sum_outs)
        for r, v in zip(outs[:n_o], ro):
            r[...] = v.astype(r.dtype)
        if sum_outs:
            @pl.when(pl.program_id(0) == 0)
            def _():
                for r in outs[n_o:]:
                    r[...] = jnp.zeros(r.shape, F32)
            for r, v in zip(outs[n_o:], so):
                r[...] += v

    in_specs = [pl.BlockSpec((tm, w), functools.partial(lambda i, b: (i, b), b=b)) for _, w, b in row_ins]
    in_specs += [pl.BlockSpec(v.shape, lambda i: (0, 0)) for v in vec_ins]
    out_specs = [pl.BlockSpec((tm, w), lambda i: (i, 0)) for w, _ in row_outs]
    out_specs += [pl.BlockSpec((1, w), lambda i: (0, 0)) for w in sum_outs]
    out_shape = [jax.ShapeDtypeStruct((rows, w), dt) for w, dt in row_outs]
    out_shape += [jax.ShapeDtypeStruct((1, w), F32) for w in sum_outs]
    return pl.pallas_call(
        body, name=name, grid=(rows // tm,),
        in_specs=in_specs, out_specs=out_specs, out_shape=out_shape,
        compiler_params=_params(("arbitrary",)),
    )(*[a for a, _, _ in row_ins], *vec_ins)


def _sigmoid(x):
    return 1.0 / (1.0 + jnp.exp(-x))


def _rstd(x):
    return lax.rsqrt(jnp.mean(x * x, axis=-1, keepdims=True) + NORM_EPS)


def _norm_bwd(dyn, xn, r):
    return r * (dyn - xn * jnp.mean(dyn * xn, axis=-1, keepdims=True))


def _colsum(x):
    return jnp.sum(x, axis=0, keepdims=True)


def _rope_tables(pos, invf):
    def fn(p, f):
        lane = lax.broadcasted_iota(jnp.int32, (1, LANES), 1)
        ang = p * f
        cs, sn = jnp.cos(ang), jnp.sin(ang)
        rot = (lane >= 64) & (lane < 96)
        ct = jnp.where(lane < 64, 1.0, jnp.where(rot, cs, 0.0))
        sa = jnp.where((lane >= 64) & (lane < 80), -sn, 0.0)
        sb = jnp.where((lane >= 80) & (lane < 96), sn, 0.0)
        return (ct, sa, sb), ()
    return _rowwise(fn, [(pos, 1, 0)], [invf], [(LANES, F32)] * 3, [], "rope_tables")


def _rope(x, ct, sa, sb):
    return x * ct + pltpu.roll(x, LANES - 16, 1) * sa + pltpu.roll(x, 16, 1) * sb


def _rope_t(x, ct, sa, sb):
    return x * ct - pltpu.roll(x, LANES - 16, 1) * sa - pltpu.roll(x, 16, 1) * sb


def _head_mask(width, hh):
    lane = lax.broadcasted_iota(jnp.int32, (1, width), 1)
    half = width // 2
    return (lane >= hh * half) & (lane < (hh + 1) * half)


ATT_PP = 2
ATT_CHAINS = [(a, hh) for a in range(ATT_PP) for hh in range(2)]
ATT_G = HEADS // (2 * ATT_PP)


def _pair(ref_or_val, a, width, rows=slice(None)):
    return ref_or_val[rows, a * width:(a + 1) * width]


def _attn_fwd(q, qo, k, ko, v, vo, dkp, scale, bias, name):
    T = ATT_T
    assert qo % ATT_PP == 0 and ko % ATT_PP == 0 and vo % ATT_PP == 0
    qo, ko, vo = qo // ATT_PP, ko // ATT_PP, vo // ATT_PP

    def body(*refs):
        if bias is not None:
            q_ref, k_ref, v_ref, b_ref, o_ref, lse_ref, s_scr = refs
        else:
            q_ref, k_ref, v_ref, o_ref, lse_ref, s_scr = refs
        i = pl.program_id(1)
        row = lax.broadcasted_iota(jnp.int32, (T, T), 0)
        col = lax.broadcasted_iota(jnp.int32, (T, T), 1)
        qms = []
        for a, hh in ATT_CHAINS:
            qb = _pair(q_ref, a, dkp)
            qms.append(jnp.where(_head_mask(dkp, hh), qb, jnp.zeros_like(qb)))

        def fold(t):
            return [t[:, c * LANES:(c + 1) * LANES] for c in range(T // LANES)]

        def run(nt):
            mls = [jnp.full((T, LANES), -jnp.inf, F32) for _ in ATT_CHAINS]
            for j in range(nt):
                ks = slice(j * T, (j + 1) * T)
                for ci, (a, hh) in enumerate(ATT_CHAINS):
                    s = lax.dot_general(qms[ci], _pair(k_ref, a, dkp, ks), NT, preferred_element_type=F32) * (scale * LOG2E)
                    if bias is not None:
                        s = s + b_ref[2 * a + hh, j] * LOG2E
                    if j == nt - 1:
                        s = jnp.where(row >= col, s, -jnp.inf)
                    s_scr[ci, j] = s
                    for part in fold(s):
                        mls[ci] = jnp.maximum(mls[ci], part)
            ms = [jnp.max(ml, axis=1, keepdims=True) for ml in mls]
            mbs = [jnp.broadcast_to(m, (T, LANES)) for m in ms]
            for a in range(ATT_PP):
                ls = [jnp.zeros((T, LANES), F32) for _ in range(2)]
                ps, vms = [], []
                for j in range(nt):
                    vb = _pair(v_ref, a, LANES, slice(j * T, (j + 1) * T))
                    for hh in range(2):
                        parts = [jnp.exp2(part - mbs[2 * a + hh]) for part in fold(s_scr[2 * a + hh, j])]
                        for part in parts:
                            ls[hh] = ls[hh] + part
                        ps.append(jnp.concatenate(parts, axis=1).astype(BF16))
                        vms.append(jnp.where(_head_mask(LANES, hh), vb, jnp.zeros_like(vb)))
                acc = lax.dot_general(jnp.concatenate(ps, axis=1), jnp.concatenate(vms, axis=0), NN,
                                      preferred_element_type=F32)
                l0, l1 = [jnp.sum(l, axis=1, keepdims=True) for l in ls]
                lse_ref[2 * a] = ms[2 * a] + jnp.log2(l0)
                lse_ref[2 * a + 1] = ms[2 * a + 1] + jnp.log2(l1)
                inv = jnp.where(_head_mask(LANES, 0), 1.0 / l0, 1.0 / l1)
                o_ref[:, a * LANES:(a + 1) * LANES] = (acc * inv).astype(o_ref.dtype)

        for nt in range(1, N_ATT + 1):
            pl.when(i == nt - 1)(functools.partial(run, nt))

    in_specs = [
        pl.BlockSpec((T, ATT_PP * dkp), lambda g, i: (i, qo + g)),
        pl.BlockSpec((S, ATT_PP * dkp), lambda g, i: (0, ko + g)),
        pl.BlockSpec((S, ATT_PP * LANES), lambda g, i: (0, vo + g)),
    ]
    args = [q, k, v]
    if bias is not None:
        in_specs.append(pl.BlockSpec((2 * ATT_PP, N_ATT, 1, T), lambda g, i: (g, 0, 0, 0)))
        args.append(bias)
    return pl.pallas_call(
        body, name=name, grid=(ATT_G, N_ATT),
        in_specs=in_specs,
        out_specs=[pl.BlockSpec((T, ATT_PP * LANES), lambda g, i: (i, g)),
                   pl.BlockSpec((2 * ATT_PP, T, 1), lambda g, i: (g, i, 0))],
        out_shape=[jax.ShapeDtypeStruct((S, HEADS * HEAD_DIM), BF16),
                   jax.ShapeDtypeStruct((HEADS, S, 1), F32)],
        scratch_shapes=[pltpu.VMEM((len(ATT_CHAINS), N_ATT, T, T), F32)],
        compiler_params=_params(("parallel", "arbitrary")),
    )(*args)


def _attn_grad(q, qo, k, ko, v, vo, do, lse, dkp, scale, bias, qk_dtype, name):
    T = ATT_T
    has_b = bias is not None
    qo, ko, vo = qo // ATT_PP, ko // ATT_PP, vo // ATT_PP
    n_ch = len(ATT_CHAINS)

    def body(*refs):
        q_ref, k_ref, v_ref, do_ref, lse_ref = refs[:5]
        refs = refs[5:]
        if has_b:
            b_ref, refs = refs[0], refs[1:]
        dq_ref, dk_ref, dv_ref = refs[:3]
        refs = refs[3:]
        if has_b:
            db_ref, refs = refs[0], refs[1:]
        p_scr, dp_scr, dk_acc, dv_acc = refs[:4]
        db_acc = refs[4] if has_b else None
        i = pl.program_id(1)

        @pl.when(i == 0)
        def _():
            dk_acc[...] = jnp.zeros(dk_acc.shape, F32)
            dv_acc[...] = jnp.zeros(dv_acc.shape, F32)
            if has_b:
                db_acc[...] = jnp.zeros(db_acc.shape, F32)

        row = lax.broadcasted_iota(jnp.int32, (T, T), 0)
        col = lax.broadcasted_iota(jnp.int32, (T, T), 1)

        def fold(t):
            return [t[:, c * LANES:(c + 1) * LANES] for c in range(T // LANES)]

        qms, doms, lses = [], [], []
        for a, hh in ATT_CHAINS:
            qb, dob = _pair(q_ref, a, dkp), _pair(do_ref, a, LANES)
            qms.append(jnp.where(_head_mask(dkp, hh), qb, jnp.zeros_like(qb)))
            doms.append(jnp.where(_head_mask(LANES, hh), dob, jnp.zeros_like(dob)))
            lses.append(lse_ref[2 * a + hh])

        def run(nt):
            dls = [jnp.zeros((T, LANES), F32) for _ in ATT_CHAINS]
            for j in range(nt):
                ks = slice(j * T, (j + 1) * T)
                for ci, (a, hh) in enumerate(ATT_CHAINS):
                    s = lax.dot_general(qms[ci], _pair(k_ref, a, dkp, ks), NT, preferred_element_type=F32) * (scale * LOG2E)
                    if has_b:
                        s = s + b_ref[ci, j] * LOG2E
                    s = s - lses[ci]
                    if j == nt - 1:
                        s = jnp.where(row >= col, s, -jnp.inf)
                    p = jnp.exp2(s)
                    dp = lax.dot_general(doms[ci], _pair(v_ref, a, LANES, ks), NT, preferred_element_type=F32)
                    p_scr[ci, j] = p
                    dp_scr[ci, j] = dp
                    for part in fold(p * dp):
                        dls[ci] = dls[ci] + part
            deltas = [jnp.broadcast_to(jnp.sum(dl, axis=1, keepdims=True), (T, LANES)) for dl in dls]
            for a in range(ATT_PP):
                ds_all, km_all = [], []
                qm2t = jnp.transpose(jnp.concatenate([qms[2 * a], qms[2 * a + 1]], axis=0))
                dom2t = jnp.transpose(jnp.concatenate([doms[2 * a], doms[2 * a + 1]], axis=0))
                for j in range(nt):
                    ks = slice(j * T, (j + 1) * T)
                    kb = _pair(k_ref, a, dkp, ks)
                    p2, ds2 = [], []
                    for hh in range(2):
                        ci = 2 * a + hh
                        p = p_scr[ci, j]
                        ds = jnp.concatenate([pp * (dd - deltas[ci]) for pp, dd in zip(fold(p), fold(dp_scr[ci, j]))], axis=1)
                        if has_b:
                            db_acc[ci, j] += jnp.sum(ds, axis=0, keepdims=True)
                        p2.append(p.astype(BF16))
                        ds2.append((ds * scale).astype(BF16))
                        km_all.append(jnp.where(_head_mask(dkp, hh), kb, jnp.zeros_like(kb)))
                    dv_acc[a * LANES:(a + 1) * LANES, ks] += lax.dot_general(
                        dom2t, jnp.concatenate(p2, axis=0), NN, preferred_element_type=F32)
                    dk_acc[a * dkp:(a + 1) * dkp, ks] += lax.dot_general(
                        qm2t, jnp.concatenate(ds2, axis=0), NN, preferred_element_type=F32)
                    ds_all += ds2
                dq = lax.dot_general(jnp.concatenate(ds_all, axis=1), jnp.concatenate(km_all, axis=0), NN,
                                     preferred_element_type=F32)
                dq_ref[:, a * dkp:(a + 1) * dkp] = dq.astype(dq_ref.dtype)

        for nt in range(1, N_ATT + 1):
            pl.when(i == nt - 1)(functools.partial(run, nt))

        @pl.when(i == N_ATT - 1)
        def _():
            dk_ref[...] = jnp.transpose(dk_acc[...]).astype(dk_ref.dtype)
            dv_ref[...] = jnp.transpose(dv_acc[...]).astype(dv_ref.dtype)
            if has_b:
                db_ref[...] = db_acc[...]

    in_specs = [
        pl.BlockSpec((T, ATT_PP * dkp), lambda g, i: (i, qo + g)),
        pl.BlockSpec((S, ATT_PP * dkp), lambda g, i: (0, ko + g)),
        pl.BlockSpec((S, ATT_PP * LANES), lambda g, i: (0, vo + g)),
        pl.BlockSpec((T, ATT_PP * LANES), lambda g, i: (i, g)),
        pl.BlockSpec((2 * ATT_PP, T, 1), lambda g, i: (g, i, 0)),
    ]
    args = [q, k, v, do, lse]
    out_specs = [
        pl.BlockSpec((T, ATT_PP * dkp), lambda g, i: (i, g)),
        pl.BlockSpec((S, ATT_PP * dkp), lambda g, i: (0, g)),
        pl.BlockSpec((S, ATT_PP * LANES), lambda g, i: (0, g)),
    ]
    width = (HEADS // 2) * dkp
    out_shape = [
        jax.ShapeDtypeStruct((S, width), qk_dtype),
        jax.ShapeDtypeStruct((S, width), qk_dtype),
        jax.ShapeDtypeStruct((S, HEADS * HEAD_DIM), BF16),
    ]
    scratch = [pltpu.VMEM((n_ch, N_ATT, T, T), F32), pltpu.VMEM((n_ch, N_ATT, T, T), F32),
               pltpu.VMEM((ATT_PP * dkp, S), F32), pltpu.VMEM((ATT_PP * LANES, S), F32)]
    if has_b:
        bspec = pl.BlockSpec((2 * ATT_PP, N_ATT, 1, T), lambda g, i: (g, 0, 0, 0))
        in_specs.append(bspec)
        args.append(bias)
        out_specs.append(bspec)
        out_shape.append(jax.ShapeDtypeStruct((HEADS, N_ATT, 1, T), F32))
        scratch.append(pltpu.VMEM((2 * ATT_PP, N_ATT, 1, T), F32))
    return pl.pallas_call(
        body, name=name, grid=(ATT_G, N_ATT),
        in_specs=in_specs, out_specs=out_specs, out_shape=out_shape, scratch_shapes=scratch,
        compiler_params=_params(("parallel", "arbitrary")),
    )(*args)


def _tri(upper):
    a = lax.broadcasted_iota(jnp.int32, (LANES, LANES), 0)
    b = lax.broadcasted_iota(jnp.int32, (LANES, LANES), 1)
    return jnp.where(a <= b if upper else a >= b, 1.0, 0.0).astype(F32)


def _fox_gates(proj, blk, bf):
    def body(m_ref, b_ref, z_out, o_ref):
        tri = _tri(True)
        carry = jnp.zeros((HEADS, 1), F32)
        for t in range(S // LANES):
            sl = slice(t * LANES, (t + 1) * LANES)
            zt = jnp.transpose(m_ref[sl, :])[:HEADS]
            z_out[:, sl] = zt
            z = zt + b_ref[...]
            logf = jnp.minimum(z, 0.0) - jnp.log(1.0 + jnp.exp(-jnp.abs(z)))
            c = lax.dot_general(logf, tri, NN, preferred_element_type=F32,
                                precision=lax.Precision.HIGHEST) + carry
            o_ref[:, sl] = -c
            carry = c[:, LANES - 1:LANES]

    return pl.pallas_call(
        body, name="fox_gates", grid=(1,),
        in_specs=[pl.BlockSpec((S, LANES), lambda i: (0, blk)), pl.BlockSpec(bf.shape, lambda i: (0, 0))],
        out_specs=[pl.BlockSpec((HEADS, S), lambda i: (0, 0))] * 2,
        out_shape=[jax.ShapeDtypeStruct((HEADS, S), F32)] * 2,
        compiler_params=_params(("arbitrary",)),
    )(proj, bf)


def _fox_gates_bwd(dbias, zt, bf):
    def body(d_ref, z_ref, b_ref, dz_ref, dbf_ref):
        tri = _tri(False)
        carry = jnp.zeros((HEADS, 1), F32)
        tot = jnp.zeros((HEADS, 1), F32)
        for t in reversed(range(S // LANES)):
            sl = slice(t * LANES, (t + 1) * LANES)
            df = -d_ref[:, sl]
            c = lax.dot_general(df, tri, NN, preferred_element_type=F32,
                                precision=lax.Precision.HIGHEST) + carry
            carry = c[:, 0:1]
            z = z_ref[:, sl] + b_ref[...]
            dz = c * _sigmoid(-z)
            dz_ref[:, sl] = dz
            tot = tot + jnp.sum(dz, axis=1, keepdims=True)
        dbf_ref[...] = tot

    return pl.pallas_call(
        body, name="fox_gates_bwd",
        out_shape=[jax.ShapeDtypeStruct((HEADS, S), F32), jax.ShapeDtypeStruct((HEADS, 1), F32)],
        compiler_params=_params(),
    )(dbias, zt, bf)


def _mod_part(c_all, w_ada, b_cols):
    def body(c_ref, w_ref, b_ref, o_ref, s_ref):
        c = c_ref[...]
        sc = c * _sigmoid(c)
        s_ref[...] = sc
        o_ref[...] = lax.dot_general(sc, w_ref[...], NN, preferred_element_type=F32,
                                     precision=lax.Precision.HIGHEST) + b_ref[...]

    return pl.pallas_call(
        body, name="mod_part",
        out_shape=[jax.ShapeDtypeStruct((N_DEV, w_ada.shape[1]), F32), jax.ShapeDtypeStruct(c_all.shape, F32)],
        compiler_params=_params(),
    )(c_all, w_ada, b_cols)


def _w_ada_grad(sc_t, dm):
    def body(s_ref, d_ref, o_ref):
        acc = jnp.zeros(o_ref.shape, F32)
        for b in range(N_DEV):
            acc = acc + s_ref[:, b:b + 1] * d_ref[b:b + 1, :]
        o_ref[...] = acc

    return pl.pallas_call(
        body, name="w_ada_grad", out_shape=jax.ShapeDtypeStruct((sc_t.shape[0], dm.shape[1]), F32),
        compiler_params=_params(),
    )(sc_t, dm)


def _adamw(w, m, v, parts, name, own=None, slot=None):
    rows, cols = w.shape
    n = parts.shape[0]
    tr = rows if rows <= 512 else 256

    def body(*refs):
        if own is not None:
            s_ref, refs = refs[0], refs[1:]
            w_ref, m_ref, v_ref, p_ref, o_ref, g_out, d_out, m_out, v_out = refs
            terms = [jnp.where(s_ref[0] == kk, o_ref[0], p_ref[kk]) for kk in range(n)]
        else:
            w_ref, m_ref, v_ref, p_ref, g_out, d_out, m_out, v_out = refs
            terms = [p_ref[kk] for kk in range(n)]
        g = terms[0].astype(F32)
        for term in terms[1:]:
            g = g + term.astype(F32)
        g_out[...] = g
        d_out[...], m_out[...], v_out[...] = _adamw_math(w_ref[...], g, m_ref[...], v_ref[...])

    spec = pl.BlockSpec((tr, cols), lambda i, *_: (i, 0))
    in_specs = [spec, spec, spec, pl.BlockSpec((n, tr, cols), lambda i, *_: (0, i, 0))]
    out_shape = [jax.ShapeDtypeStruct((rows, cols), F32)] * 4
    if own is None:
        return pl.pallas_call(
            body, name=name, grid=(rows // tr,), in_specs=in_specs, out_specs=[spec] * 4, out_shape=out_shape,
            compiler_params=_params(("parallel",)),
        )(w, m, v, parts)
    in_specs.append(pl.BlockSpec((1, tr, cols), lambda i, s: (s[0], i, 0)))
    return pl.pallas_call(
        body, name=name, out_shape=out_shape, compiler_params=_params(("parallel",)),
        grid_spec=pltpu.PrefetchScalarGridSpec(num_scalar_prefetch=1, grid=(rows // tr,), in_specs=in_specs,
                                               out_specs=[spec] * 4),
    )(slot, w, m, v, parts, own)


def _adamw_math(w, g, m, v):
    mm = ADAM_B1 * m + (1.0 - ADAM_B1) * g
    vv = ADAM_B2 * v + (1.0 - ADAM_B2) * (g * g)
    m_hat = mm / (1.0 - ADAM_B1 ** ADAM_STEP)
    v_hat = vv / (1.0 - ADAM_B2 ** ADAM_STEP)
    return -ADAM_LR * (m_hat / (jnp.sqrt(v_hat) + ADAM_EPS) + ADAM_WD * w), mm, vv


def _adamw_rows(bundles, offsets, ws, ms, vs, err_off, err_width):
    k = len(ws)

    def body(*refs):
        b_ref = refs[0]
        w_refs, m_refs, v_refs = refs[1:1 + k], refs[1 + k:1 + 2 * k], refs[1 + 2 * k:1 + 3 * k]
        outs = refs[1 + 3 * k:]
        g_all = b_ref[0]
        for kk in range(1, N_DEV):
            g_all = g_all + b_ref[kk]
        for i in range(k):
            width = w_refs[i].shape[1]
            g = g_all[:, offsets[i]:offsets[i] + width]
            outs[4 * i][...] = g
            outs[4 * i + 1][...], outs[4 * i + 2][...], outs[4 * i + 3][...] = _adamw_math(
                w_refs[i][...], g, m_refs[i][...], v_refs[i][...])
        outs[4 * k][...] = g_all[:, err_off:err_off + err_width]

    out_shape = []
    for w_ in ws:
        out_shape += [jax.ShapeDtypeStruct(w_.shape, F32)] * 4
    out_shape.append(jax.ShapeDtypeStruct((1, err_width), F32))
    res = pl.pallas_call(body, name="adamw_rows", out_shape=out_shape, compiler_params=_params())(bundles, *ws, *ms, *vs)
    return [tuple(res[4 * i:4 * i + 4]) for i in range(k)], res[-1]


def _coords():
    return lax.axis_index("x"), lax.axis_index("y"), lax.axis_index("c")


def _flat(px, py, pc):
    return 4 * px + 2 * py + pc


def _all_gather(arrs, name):
    n = len(arrs)

    def body(*refs):
        ins, outs = refs[:n], refs[n:2 * n]
        send, recv, lsem = refs[2 * n:]
        x, y, c = _coords()
        me, sibling = (x, y, c), (x, y, 1 - c)
        chips = [(1 - x, y), (x, 1 - y), (1 - x, 1 - y)]

        def copy(a, kk, block, to, src=None):
            slot = outs[a].at[_flat(*block)]
            return pltpu.make_async_remote_copy(
                src_ref=slot if src is None else src, dst_ref=slot,
                send_sem=send.at[a, kk], recv_sem=recv.at[a, kk],
                device_id=to, device_id_type=MESH)

        mine = [pltpu.make_async_copy(ins[a], outs[a].at[_flat(*me)], lsem.at[a]) for a in range(n)]
        for cp in mine:
            cp.start()
        first = []
        for a in range(n):
            first.append(copy(a, 0, me, sibling, src=ins[a]))
            first += [copy(a, 1 + j, me, (*chip, c), src=ins[a]) for j, chip in enumerate(chips)]
        for cp in first:
            cp.start()
        passed = []
        for j, chip in enumerate(chips):
            for a in range(n):
                copy(a, 1 + j, (*chip, c), me).wait_recv()
                cp = copy(a, 4 + j, (*chip, c), sibling)
                cp.start()
                passed.append(cp)
        for a in range(n):
            copy(a, 0, sibling, me).wait_recv()
        for j, chip in enumerate(chips):
            for a in range(n):
                copy(a, 4 + j, (*chip, 1 - c), me).wait_recv()
        for cp in first + passed:
            cp.wait_send()
        for cp in mine:
            cp.wait()

    any_spec = pl.BlockSpec(memory_space=pl.ANY)
    return pl.pallas_call(
        body, name=name,
        in_specs=[any_spec] * n, out_specs=[any_spec] * n,
        out_shape=[jax.ShapeDtypeStruct((N_DEV,) + a.shape, a.dtype) for a in arrs],
        scratch_shapes=[pltpu.SemaphoreType.DMA((n, 7)), pltpu.SemaphoreType.DMA((n, 7)),
                        pltpu.SemaphoreType.DMA((n,))],
    )(*arrs)


def _peer_list():
    x, y, c = _coords()
    return [((1 - x if r & 4 else x), (1 - y if r & 2 else y), (1 - c if r & 1 else c)) for r in range(1, N_DEV)]


def _copy_plan(mode, src, land):
    x, y, c = _coords()
    me = _flat(x, y, c)
    if mode == "gather":
        return [(src, land.at[me], peer) for peer in _peer_list()]
    if mode == "exchange":
        return [(src.at[_flat(*peer)], land.at[me], peer) for peer in _peer_list()]
    if mode == "pair":
        return [(src.at[_flat(q // 2, q % 2, 1 - c)], land.at[q], (x, y, 1 - c)) for q in range(N_DEV // 2)]
    chips = [((1 - x if r & 2 else x), (1 - y if r & 1 else y)) for r in range(1, N_DEV // 2)]
    if mode == "chips":
        return [(src.at[2 * qx + qy], land.at[2 * x + y], (qx, qy, c)) for qx, qy in chips]
    if mode == "spread":
        return [(src, land.at[me], (x, y, 1 - c))] + [(src, land.at[me], (qx, qy, c)) for qx, qy in chips]
    assert mode == "forward"
    return [(land.at[_flat(qx, qy, c)], land.at[_flat(qx, qy, c)], (x, y, 1 - c)) for qx, qy in chips]


N_COPIES = dict(gather=N_DEV - 1, exchange=N_DEV - 1, pair=N_DEV // 2, chips=N_DEV // 2 - 1, spread=N_DEV // 2,
                forward=N_DEV // 2 - 1)


def _land_shape(mode, shape):
    return {"gather": (N_DEV,) + shape, "spread": (N_DEV,) + shape, "exchange": shape,
            "pair": (N_DEV // 2,) + shape[1:], "chips": shape}[mode]


HBM_SPEC = pl.BlockSpec(memory_space=pltpu.HBM)
SEM_SPEC = pl.BlockSpec(memory_space=pltpu.SEMAPHORE)
ANY_SPEC = pl.BlockSpec(memory_space=pl.ANY)
SIDE_EFFECT = pltpu.SideEffectType.DATAFLOW_SIDE_EFFECTING


def _async_start(groups, modes, after, name):
    modes = [modes] * len(groups) if isinstance(modes, str) else list(modes)
    arrs = [(a, m) for g, m in zip(groups, modes) for a in g]
    n = len(arrs)
    fresh = [i for i, (_, m) in enumerate(arrs) if m != "forward"]

    def body(*refs):
        srcs, new_lands = refs[:n], refs[n:n + len(fresh)]
        outs = refs[n + len(fresh) + 1:]
        lands = list(srcs)
        for k, i in enumerate(fresh):
            lands[i] = new_lands[k]
        for ai, (_, mode) in enumerate(arrs):
            for src_ref, dst_ref, peer in _copy_plan(mode, srcs[ai], lands[ai]):
                pltpu.make_async_remote_copy(src_ref=src_ref, dst_ref=dst_ref, send_sem=outs[2 * ai],
                                             recv_sem=outs[2 * ai + 1], device_id=peer, device_id_type=MESH).start()
        outs[-1][...] = jnp.zeros(outs[-1].shape, F32)

    land_shapes = [(_land_shape(arrs[i][1], arrs[i][0].shape), arrs[i][0].dtype) for i in fresh]
    n_buf = n + len(fresh)
    out_shape = [pltpu.SemaphoreType.DMA(())] * (2 * n)
    out_shape += [pltpu.HBM(a.shape, a.dtype) for a, _ in arrs]
    out_shape += [pltpu.HBM(shape, dt) for shape, dt in land_shapes]
    out_shape.append(jax.ShapeDtypeStruct((8, LANES), F32))
    res = pl.pallas_call(
        body, name=name, out_shape=tuple(out_shape),
        in_specs=[HBM_SPEC] * n_buf + [ANY_SPEC],
        out_specs=tuple([SEM_SPEC] * (2 * n) + [HBM_SPEC] * n_buf + [pl.BlockSpec(memory_space=pltpu.VMEM)]),
        input_output_aliases={i: 2 * n + i for i in range(n_buf)},
        compiler_params=pltpu.CompilerParams(has_side_effects=SIDE_EFFECT),
    )(*[pltpu.with_memory_space_constraint(a, pltpu.HBM) for a, _ in arrs],
      *[pltpu.with_memory_space_constraint(lax.empty(shape, dt), pltpu.HBM) for shape, dt in land_shapes],
      after)
    sems, thru = res[:2 * n], res[2 * n:-1]
    land_of = {i: thru[n + k] for k, i in enumerate(fresh)}
    states, idx = [], 0
    for g, mode in zip(groups, modes):
        ids = range(idx, idx + len(g))
        idx += len(g)
        states.append(([sems[2 * i] for i in ids], [sems[2 * i + 1] for i in ids],
                       None if mode == "forward" else [thru[i] for i in ids],
                       [land_of.get(i, thru[i]) for i in ids], mode))
    return states, res[-1]


def _async_wait(state, after, name):
    sends, recvs, srcs, lands, mode = state
    g = len(lands)
    bufs = (list(srcs) if srcs is not None else []) + list(lands)
    nb = len(bufs)

    def body(*refs):
        l_refs, sems = refs[nb - g:nb], refs[nb:nb + 2 * g]
        for ai in range(g):
            moved = l_refs[ai].at[pl.ds(0, N_COPIES[mode])]
            cp = pltpu.make_async_remote_copy(src_ref=moved, dst_ref=moved, send_sem=sems[ai], recv_sem=sems[g + ai],
                                              device_id=_coords(), device_id_type=MESH)
            cp.wait_send()
            cp.wait_recv()

    res = pl.pallas_call(
        body, name=name,
        out_shape=tuple(pltpu.HBM(a.shape, a.dtype) for a in bufs),
        in_specs=[HBM_SPEC] * nb + [SEM_SPEC] * (2 * g) + [ANY_SPEC],
        out_specs=tuple([HBM_SPEC] * nb),
        input_output_aliases={i: i for i in range(nb)},
        compiler_params=pltpu.CompilerParams(has_side_effects=SIDE_EFFECT),
    )(*bufs, *sends, *recvs, after)
    return (list(res[:nb - g]) if srcs is not None else None), list(res[nb - g:])


def _add_sibling(mine, theirs, core):
    def body(c_ref, a_ref, b_ref, o_ref):
        o_ref[...] = (a_ref[...].astype(F32) + b_ref[...].astype(F32)).astype(o_ref.dtype)

    blk = (1,) + mine.shape[1:]
    return pl.pallas_call(
        body, name="add_sibling", out_shape=jax.ShapeDtypeStruct(theirs.shape, mine.dtype),
        grid_spec=pltpu.PrefetchScalarGridSpec(
            num_scalar_prefetch=1, grid=(theirs.shape[0],),
            in_specs=[pl.BlockSpec(blk, lambda q, c: (2 * q + c[0], 0, 0)), pl.BlockSpec(blk, lambda q, c: (q, 0, 0))],
            out_specs=pl.BlockSpec(blk, lambda q, c: (q, 0, 0))),
        compiler_params=_params(("parallel",)),
    )(core, mine, theirs)


def _with_own(land, own, me):
    return lax.dynamic_update_index_in_dim(land, own, me, 0)


IN_SPLITS = (512, 512, 512, 8, 768, 256, 32, 1024, 1024)


def _from_shards(g, fn, out_widths, name, own=None, slot=None):
    _, k, n = g.shape
    tr = min(k, 256)

    def body(*refs):
        if own is not None:
            s_ref, g_ref, own_ref = refs[:3]
            cols = [jnp.where(s_ref[0] == j, own_ref[...], g_ref[j]) for j in range(N_DEV)]
        else:
            g_ref = refs[0]
            cols = [g_ref[j] for j in range(N_DEV)]
        for o_ref, val in zip(refs[-len(out_widths):], fn(jnp.concatenate(cols, axis=1))):
            o_ref[...] = val

    in_specs = [pl.BlockSpec((N_DEV, tr, n), lambda i, *_: (0, i, 0))]
    out_spec = [pl.BlockSpec((tr, wd), lambda i, *_: (i, 0)) for wd in out_widths]
    out_shape = [jax.ShapeDtypeStruct((k, wd), g.dtype) for wd in out_widths]
    if own is None:
        return pl.pallas_call(body, name=name, grid=(k // tr,), in_specs=in_specs, out_specs=out_spec,
                              out_shape=out_shape, compiler_params=_params(("parallel",)))(g)
    in_specs.append(pl.BlockSpec((tr, n), lambda i, *_: (i, 0)))
    return pl.pallas_call(
        body, name=name, out_shape=out_shape, compiler_params=_params(("parallel",)),
        grid_spec=pltpu.PrefetchScalarGridSpec(num_scalar_prefetch=1, grid=(k // tr,), in_specs=in_specs, out_specs=out_spec),
    )(slot, g, own)


def _unshard_cols(g, own=None, slot=None):
    return _from_shards(g, lambda full: (full,), [N_DEV * g.shape[2]], "unshard_cols_%d" % g.shape[2], own, slot)[0]


FFN_T = 256
FFN_SHARD = 2 * D_FF // N_DEV


def _unshard_ffn_in(g, own=None, slot=None):
    def pairs(full):
        parts = []
        for j in range(D_FF // FFN_T):
            parts += [full[:, j * FFN_T:(j + 1) * FFN_T], full[:, D_FF + j * FFN_T:D_FF + (j + 1) * FFN_T]]
        return (jnp.concatenate(parts, axis=1),)

    return _from_shards(g, pairs, [2 * D_FF], "unshard_ffn_in", own, slot)[0]


def _shard_ffn_in(w):
    tr = 256

    def body(w_ref, o_ref):
        x = w_ref[...]
        nb = D_FF // FFN_T
        full = jnp.concatenate([x[:, (2 * j + half) * FFN_T:(2 * j + half + 1) * FFN_T]
                                for half in range(2) for j in range(nb)], axis=1)
        for j in range(N_DEV):
            o_ref[j] = full[:, j * FFN_SHARD:(j + 1) * FFN_SHARD]

    return pl.pallas_call(
        body, name="shard_ffn_in", grid=(D // tr,),
        in_specs=[pl.BlockSpec((tr, 2 * D_FF), lambda i: (i, 0))],
        out_specs=pl.BlockSpec((N_DEV, tr, FFN_SHARD), lambda i: (0, i, 0)),
        out_shape=jax.ShapeDtypeStruct((N_DEV, D, FFN_SHARD), w.dtype),
        compiler_params=_params(("parallel",)),
    )(w)


def _shard_cols(w):
    k, n = w.shape[0], w.shape[1] // N_DEV
    tr = min(k, 256)

    def body(w_ref, o_ref):
        full = w_ref[...]
        for j in range(N_DEV):
            o_ref[j] = full[:, j * n:(j + 1) * n]

    return pl.pallas_call(
        body, name="shard_cols_%d" % n, grid=(k // tr,),
        in_specs=[pl.BlockSpec((tr, N_DEV * n), lambda i: (i, 0))],
        out_specs=pl.BlockSpec((N_DEV, tr, n), lambda i: (0, i, 0)),
        out_shape=jax.ShapeDtypeStruct((N_DEV, k, n), w.dtype),
        compiler_params=_params(("parallel",)),
    )(w)


IN_OFFS = tuple(sum(IN_SPLITS[:i]) for i in range(len(IN_SPLITS) + 1))
IN_SHARD = IN_OFFS[-1] // N_DEV
REGROUP_ROWS = 128


def _w_in_regroup(g, own=None, slot=None):
    def groups(full):
        fq, fk, fv, wf, cq, ckv, kr, gf, gm = [full[:, IN_OFFS[i]:IN_OFFS[i + 1]] for i in range(9)]
        rows = full.shape[0]
        w_a = jnp.concatenate([cq, ckv, gf, gm, wf, jnp.zeros((rows, 56), BF16), kr, jnp.zeros((rows, 32), BF16)], axis=1)
        return w_a, jnp.concatenate([fq, fk, fv], axis=1)

    return _from_shards(g, groups, [3200, 1536], "w_in_regroup", own, slot)


def _w_in_ungroup(da, db_):
    def body(a_ref, b_ref, o_ref):
        a = a_ref[...]
        full = jnp.concatenate([b_ref[...], a[:, 3072:3080], a[:, 0:768], a[:, 768:1024], a[:, 3136:3168],
                                a[:, 1024:3072]], axis=1)
        for j in range(N_DEV):
            o_ref[j] = full[:, j * IN_SHARD:(j + 1) * IN_SHARD]

    tr = REGROUP_ROWS
    return pl.pallas_call(
        body, name="w_in_ungroup", grid=(D // tr,),
        in_specs=[pl.BlockSpec((tr, 3200), lambda i: (i, 0)), pl.BlockSpec((tr, 1536), lambda i: (i, 0))],
        out_specs=pl.BlockSpec((N_DEV, tr, IN_SHARD), lambda i: (0, i, 0)),
        out_shape=jax.ShapeDtypeStruct((N_DEV, D, IN_SHARD), BF16),
        compiler_params=_params(("parallel",)),
    )(da, db_)


def _prepare_weights(g, own=None, slot=None):
    w = {}
    if own is not None:
        small = ("w_uq", "w_ukv", "w_out", "w_ffn_out")
        g = {n: (_with_own(a, own[n], slot[0]) if n in small else a) for n, a in g.items()}
    pick = (lambda n: (own[n], slot)) if own is not None else (lambda n: (None, None))
    if "w_in" in g:
        w["w_a"], w["w_b"] = _w_in_regroup(g["w_in"], *pick("w_in"))
    if "w_uq" in g:
        w_uq = g["w_uq"].reshape(Q_LORA, HEADS, 96)
        w["w_uq"] = jnp.pad(w_uq, ((0, 0), (0, 0), (0, 32))).reshape(Q_LORA, HEADS * LANES)
        ukv = g["w_ukv"]
        w["w_k"] = jnp.transpose(jnp.pad(ukv[:, :, :64], ((0, 0), (0, 0), (0, 64))), (1, 0, 2)).reshape(KV_LORA, HEADS * LANES)
        w["w_v"] = jnp.transpose(ukv[:, :, 64:], (1, 0, 2)).reshape(KV_LORA, HEADS * HEAD_DIM)
    if "w_out" in g:
        w["w_pf"] = _unshard_cols(g["w_proj_fox"], *pick("w_proj_fox"))
        w["w_pm"] = _unshard_cols(g["w_proj_mla"], *pick("w_proj_mla"))
        w["w_out"] = g["w_out"].reshape(D, D)
    if "w_ffn_in" in g:
        w["w_ffn_in"] = _unshard_ffn_in(g["w_ffn_in"], *pick("w_ffn_in"))
        w["w_ffn_out"] = g["w_ffn_out"].reshape(D_FF, D)
    return w


def _shard_grads(dw):
    out = {}
    if "w_a" in dw:
        out["w_in"] = _w_in_ungroup(dw["w_a"], dw["w_b"])
    if "w_uq" in dw:
        w_uq = dw["w_uq"].reshape(Q_LORA, HEADS, LANES)[:, :, :96].reshape(Q_LORA, Q_LORA)
        out["w_uq"] = w_uq.reshape(N_DEV, Q_LORA // N_DEV, Q_LORA)
        k_part = dw["w_k"].reshape(KV_LORA, HEADS, LANES)[:, :, :64]
        v_part = dw["w_v"].reshape(KV_LORA, HEADS, HEAD_DIM)
        out["w_ukv"] = jnp.transpose(jnp.concatenate([k_part, v_part], axis=2), (1, 0, 2))
    if "w_out" in dw:
        out["w_proj_fox"] = _shard_cols(dw["w_pf"])
        out["w_proj_mla"] = _shard_cols(dw["w_pm"])
        out["w_out"] = dw["w_out"].reshape(N_DEV, D // N_DEV, D)
    if "w_ffn_in" in dw:
        out["w_ffn_in"] = _shard_ffn_in(dw["w_ffn_in"])
        out["w_ffn_out"] = dw["w_ffn_out"].reshape(N_DEV, D_FF // N_DEV, D)
    return out


def _fwd_bwd(x, pos, mod, target, w, vec, wts, send, relay):
    shift_mix, scale_mix, gate_mix, shift_ffn, scale_ffn, gate_ffn = [mod[:, i * D:(i + 1) * D] for i in range(6)]
    g_pre_mix, g_post_mix, g_pre_ffn, g_post_ffn = vec["g_pre_mix"], vec["g_post_mix"], vec["g_pre_ffn"], vec["g_post_ffn"]
    g_q, g_kv = vec["g_q_lora"], vec["g_kv_lora"]

    inv_freq = 1.0 / (ROPE_THETA ** (jnp.arange(0, ROPE_DIM, 2, dtype=F32) / ROPE_DIM))
    invf = jnp.concatenate([jnp.zeros((64,), F32), inv_freq, inv_freq, jnp.zeros((32,), F32)]).reshape(1, LANES)
    ct, sa, sb = _rope_tables(pos, invf)

    def pre1(xv, g, sc, sh):
        return (xv * _rstd(xv) * g) * (1.0 + sc) + sh
    proj_a, h = _mm_epi(x, w["w_a"], "nn", 640, lambda r: ((r,), ()), "in_proj_a", 1024, outs=[(3200, 640, F32)],
                        pro=(pre1, [g_pre_mix, scale_mix, shift_mix], 0))
    qkv = _mm(h, w["w_b"], "nn", BF16, "in_proj_b")

    def lora_norm(cv, g):
        return cv * _rstd(cv) * g
    w = {**w, **wts("lora", qkv)}
    tables = [(ct, LANES), (sa, LANES), (sb, LANES)]

    def rope_q(qv, c_, a_, b_):
        return (jnp.concatenate([_rope(qv[:, hd * LANES:(hd + 1) * LANES], c_, a_, b_) for hd in range(HEADS)], axis=1),), ()
    q_m, cqn = _mm_epi(proj_a, w["w_uq"], "nn", D, rope_q, "mla_uq", 512, rows=tables, outs=[(D, D, BF16)],
                       pro=(lora_norm, [g_q], 0))

    def rope_k(kv, misc, c_, a_, b_):
        lane = lax.broadcasted_iota(jnp.int32, (1, LANES), 1)
        kpe = jnp.where((lane >= 64) & (lane < 96), _rope(misc, c_, a_, b_), 0.0)
        return (jnp.concatenate([kv[:, hd * LANES:(hd + 1) * LANES] + kpe for hd in range(HEADS)], axis=1),), ()
    k_m, ckvn = _mm_epi(proj_a, w["w_k"], "nn", D, rope_k, "mla_uk", 512, rows=[(proj_a, LANES, 24)] + tables,
                        outs=[(D, D, BF16)], pro=(lora_norm, [g_kv], Q_LORA // KV_LORA))
    v_m = _mm(ckvn, w["w_v"], "nn", BF16, "mla_uv")

    bf = jnp.transpose(vec["b_forget"])
    zt, neg_f = _fox_gates(proj_a, 24, bf)
    bias = neg_f.reshape(HEADS, N_ATT, 1, ATT_T)
    o_b, lse_b = _attn_fwd(q_m, 0, k_m, 0, v_m, 0, 2 * LANES, 1.0 / math.sqrt(64 + ROPE_DIM), None, "mla_attn")
    bias = bias + wts("relay_proj", o_b)["tok"][0, 0]
    o_a, lse_a = _attn_fwd(qkv, 0, qkv, 4, qkv, 8, LANES, 1.0 / math.sqrt(HEAD_DIM), bias, "fox_attn")

    w = {**w, **wts("proj", o_a)}
    gate_mix = gate_mix + wts("relay_ffn", o_a)["tok"][0, 0]
    pa = _mm(o_a, w["w_pf"], "nn", BF16, "proj_fox")

    def merge(pb_, gf, gm, pa_):
        return (_sigmoid(gf) * pa_ + _sigmoid(gm) * pb_, pb_), ()
    merged, pb = _mm_epi(o_b, w["w_pm"], "nn", 512, merge, "proj_mla", 1024,
                         rows=[(proj_a, 512, 2), (proj_a, 512, 4), (pa, 512)], outs=[(D, 512, BF16), (D, 512, BF16)])
    def post1(yv, xv, gate, gpost, gpre, sc, sh):
        x1 = xv + gate * (yv * _rstd(yv) * gpost)
        return (x1, (x1 * _rstd(x1) * gpre) * (1.0 + sc) + sh, yv), ()
    x1, h2, y = _mm_epi(merged, w["w_out"], "nn", D, post1, "out_proj", 512, rows=[(x, D)],
                        vecs=[gate_mix, g_post_mix, g_pre_ffn, scale_ffn, shift_ffn],
                        outs=[(D, D, F32), (D, D, BF16), (D, D, F32)])
    w = {**w, **wts("ffn", h2)}

    def swiglu(r):
        g, u = r[:, :FFN_T], r[:, FFN_T:]
        return (g * _sigmoid(g) * u, r), ()
    act, gu = _mm_epi(h2, w["w_ffn_in"], "nn", 2 * FFN_T, swiglu, "ffn_in", 1024,
                      outs=[(D_FF, FFN_T, BF16), (2 * D_FF, 2 * FFN_T, BF16)])

    def head(y2v, x1v, tv, gate, gpost):
        r = _rstd(y2v)
        yn = y2v * r
        n2 = yn * gpost
        err = (x1v + gate * n2) - tv
        dx2 = err * (1.0 / D)
        dn2 = dx2 * gate
        dy2 = _norm_bwd(dn2 * gpost, yn, r)
        return (dx2, dy2), (_colsum(err * err), _colsum(dx2 * n2), _colsum(dn2 * yn))
    dx2, dy2, err_cols, d_gate_ffn, d_g_post_ffn = _mm_epi(
        act, w["w_ffn_out"], "nn", D, head, "ffn_out", 512, rows=[(x1, D), (target, D)], vecs=[gate_ffn, g_post_ffn],
        outs=[(D, D, F32), (D, D, BF16)], sums=[D, D, D])

    def swiglu_bwd(da, guv):
        g, u = guv[:, :FFN_T].astype(F32), guv[:, FFN_T:].astype(F32)
        sg = _sigmoid(g)
        return (jnp.concatenate([da * u * (sg * (1.0 + g * (1.0 - sg))), da * (g * sg)], axis=1),), ()
    (dgu,) = _mm_epi(dy2, w["w_ffn_out"], "nt", FFN_T, swiglu_bwd, "ffn_out_dx", 1024, rows=[(gu, 2 * FFN_T)],
                     outs=[(2 * D_FF, 2 * FFN_T, BF16)])
    dw = {"w_ffn_out": _mm(act, dy2, "tn", BF16, "ffn_out_dw")}
    dw["w_ffn_in"] = _mm(h2, dgu, "tn", BF16, "ffn_in_dw")
    gate_mix = gate_mix + send({n: dw.pop(n) for n in ("w_ffn_in", "w_ffn_out")})[0, 0]

    def mid(dh, x1v, dx2v, yv, gpre, sc, gate, gpost):
        r2 = _rstd(x1v)
        x1n = x1v * r2
        t = dh * x1n
        dx1 = dx2v + _norm_bwd(dh * (gpre * (1.0 + sc)), x1n, r2)
        ry = _rstd(yv)
        yn = yv * ry
        dn1 = dx1 * gate
        dy = _norm_bwd(dn1 * gpost, yn, ry)
        sums = (_colsum(dh), _colsum(t) * gpre, _colsum(t) * (1.0 + sc), _colsum(dx1 * (yn * gpost)), _colsum(dn1 * yn))
        return (dx1, dy), sums
    dx1, dy, d_shift_ffn, d_scale_ffn, d_g_pre_ffn, d_gate_mix, d_g_post_mix = _mm_epi(
        dgu, w["w_ffn_in"], "nt", D, mid, "ffn_in_dx", 512, rows=[(x1, D), (dx2, D), (y, D)],
        vecs=[g_pre_ffn, scale_ffn, gate_mix, g_post_mix], outs=[(D, D, F32), (D, D, BF16)], sums=[D] * 5)

    dw["w_out"] = _mm(merged, dy, "tn", BF16, "out_proj_dw")

    def merge_bwd(dm, gf, gm, pa_, pb_):
        sf, sm = _sigmoid(gf), _sigmoid(gm)
        return (dm * sf, dm * sm, dm * pa_ * (sf * (1.0 - sf)), dm * pb_ * (sm * (1.0 - sm))), ()
    dpa, dpb, dgf, dgm = _mm_epi(dy, w["w_out"], "nt", 512, merge_bwd, "out_proj_dx", 1024,
                                 rows=[(proj_a, 512, 2), (proj_a, 512, 4), (pa, 512), (pb, 512)],
                                 outs=[(D, 512, BF16)] * 4)
    do_a = _mm(dpa, w["w_pf"], "nt", BF16, "proj_fox_dx")
    do_b = _mm(dpb, w["w_pm"], "nt", BF16, "proj_mla_dx")
    dw["w_pf"] = _mm(o_a, dpa, "tn", BF16, "proj_fox_dw")
    dw["w_pm"] = _mm(o_b, dpb, "tn", BF16, "proj_mla_dw")
    bias = bias + send({n: dw.pop(n) for n in ("w_out", "w_pf", "w_pm")})[0, 0]

    sc_a, sc_b = 1.0 / math.sqrt(HEAD_DIM), 1.0 / math.sqrt(64 + ROPE_DIM)
    dq_a, dk_a, dv_a, dbias = _attn_grad(qkv, 0, qkv, 4, qkv, 8, do_a, lse_a, LANES, sc_a, bias, BF16, "fox_attn_bwd")
    dq_m, dk_m, dv_m = _attn_grad(q_m, 0, k_m, 0, v_m, 0, do_b, lse_b, 2 * LANES, sc_b, None, F32, "mla_attn_bwd")

    def mla_rope_bwd(dq, dk, c_, a_, b_):
        lane = lax.broadcasted_iota(jnp.int32, (1, LANES), 1)
        dqs = [_rope_t(dq[:, hd * LANES:(hd + 1) * LANES], c_, a_, b_) for hd in range(HEADS)]
        dkpe = dk[:, 0:LANES]
        for hd in range(1, HEADS):
            dkpe = dkpe + dk[:, hd * LANES:(hd + 1) * LANES]
        dkpe = jnp.where((lane >= 64) & (lane < 96), dkpe, 0.0)
        dkr = jnp.where((lane >= 64) & (lane < 96), _rope_t(dkpe, c_, a_, b_), 0.0)
        return (jnp.concatenate(dqs, axis=1), dk, dkr), ()
    dqb, dkb, dkr = _rowwise(mla_rope_bwd, [(dq_m, D, 0), (dk_m, D, 0), (ct, LANES, 0), (sa, LANES, 0), (sb, LANES, 0)],
                             [], [(D, BF16), (D, BF16), (LANES, F32)], [], "mla_rope_bwd")
    def lora_q_bwd(dq, cq, gq):
        rq = _rstd(cq)
        cqh = cq * rq
        return (_norm_bwd(dq * gq, cqh, rq),), (_colsum(dq * cqh),)
    dcq, d_g_q = _mm_epi(dqb, w["w_uq"], "nt", Q_LORA, lora_q_bwd, "mla_uq_dx", 512, rows=[(proj_a, Q_LORA, 0)],
                         vecs=[g_q], outs=[(Q_LORA, Q_LORA, BF16)], sums=[Q_LORA])

    def lora_kv_bwd(dv_part, dk_part, ckv, gkv):
        dkv = dv_part + dk_part
        rk = _rstd(ckv)
        ckh = ckv * rk
        return (_norm_bwd(dkv * gkv, ckh, rk),), (_colsum(dkv * ckh),)
    dckv, d_g_kv = _mm_epi(dv_m, w["w_v"], "nt", KV_LORA, lora_kv_bwd, "mla_uv_dx", 1024,
                           rows=[(_mm(dkb, w["w_k"], "nt", F32, "mla_uk_dx"), KV_LORA), (proj_a, KV_LORA, 3)],
                           vecs=[g_kv], outs=[(KV_LORA, KV_LORA, BF16)], sums=[KV_LORA])

    dzt, d_bf = _fox_gates_bwd(dbias.reshape(HEADS, S), zt, bf)
    dmisc = (dkr + jnp.pad(jnp.transpose(dzt), ((0, 0), (0, LANES - HEADS)))).astype(BF16)
    dproj_a = jnp.concatenate([dcq, dckv, dgf, dgm, dmisc], axis=1)
    dqkv = jnp.concatenate([dq_a, dk_a, dv_a], axis=1)
    dw["w_a"] = _mm(h, dproj_a, "tn", BF16, "in_proj_a_dw")
    dw["w_b"] = _mm(h, dqkv, "tn", BF16, "in_proj_b_dw")
    tok = send(dw, True)
    dh_a = _mm(dproj_a, w["w_a"], "nt", F32, "in_proj_a_dx", dep=tok)
    tok = relay(dh_a)
    tok = send({"w_uq": _mm(cqn, dqb, "tn", BF16, "mla_uq_dw", dep=tok),
                "w_k": _mm(ckvn, dkb, "tn", BF16, "mla_uk_dw", dep=tok),
                "w_v": _mm(ckvn, dv_m, "tn", BF16, "mla_uv_dw", dep=tok)}, late=True)
    g_pre_mix = g_pre_mix + tok[0, 0]

    def first(dh_b, dh_a, xv, dx1v, gpre, sc):
        dhv = dh_b + dh_a
        r = _rstd(xv)
        xn = xv * r
        t = dhv * xn
        dx = dx1v + _norm_bwd(dhv * (gpre * (1.0 + sc)), xn, r)
        return (dx,), (_colsum(dhv), _colsum(t) * gpre, _colsum(t) * (1.0 + sc))
    grad_x, d_shift_mix, d_scale_mix, d_g_pre_mix = _mm_epi(
        dqkv, w["w_b"], "nt", D, first, "in_proj_b_dx", 512,
        rows=[(dh_a, D), (x, D), (dx1, D)],
        vecs=[g_pre_mix, scale_mix], outs=[(D, D, F32)], sums=[D] * 3)

    dmod = jnp.concatenate([d_shift_mix, d_scale_mix, d_gate_mix, d_shift_ffn, d_scale_ffn, d_gate_ffn], axis=1)
    small = dict(dmod=dmod, g_pre_mix=d_g_pre_mix, g_post_mix=d_g_post_mix, g_pre_ffn=d_g_pre_ffn,
                 g_post_ffn=d_g_post_ffn, g_q_lora=d_g_q, g_kv_lora=d_g_kv,
                 b_forget=jnp.pad(jnp.transpose(d_bf), ((0, 0), (0, LANES - HEADS))), err=err_cols)
    return grad_x, small


SMALL_ORDER = ("dmod", "g_pre_mix", "g_post_mix", "g_pre_ffn", "g_post_ffn", "g_q_lora", "g_kv_lora", "b_forget", "err")
SMALL_PARAM = {"dmod": "b_ada"}
MATRICES = ("w_in", "w_uq", "w_ukv", "w_proj_fox", "w_proj_mla", "w_out", "w_ffn_in", "w_ffn_out")
WEIGHTS = ("w_ada", "b_ada", "g_pre_mix", "g_post_mix", "g_pre_ffn", "g_post_ffn", "w_in", "b_forget", "g_q_lora",
           "w_uq", "g_kv_lora", "w_ukv", "w_proj_fox", "w_proj_mla", "w_out", "w_ffn_in", "w_ffn_out")


def kernel(x, c, positions, w_ada, b_ada, g_pre_mix, g_post_mix, g_pre_ffn, g_post_ffn, w_in, b_forget, g_q_lora, w_uq, g_kv_lora, w_ukv, w_proj_fox, w_proj_mla, w_out, w_ffn_in, w_ffn_out, loss_target, m_w_ada, m_b_ada, m_g_pre_mix, m_g_post_mix, m_g_pre_ffn, m_g_post_ffn, m_w_in, m_b_forget, m_g_q_lora, m_w_uq, m_g_kv_lora, m_w_ukv, m_w_proj_fox, m_w_proj_mla, m_w_out, m_w_ffn_in, m_w_ffn_out, v_w_ada, v_b_ada, v_g_pre_mix, v_g_post_mix, v_g_pre_ffn, v_g_post_ffn, v_w_in, v_b_forget, v_g_q_lora, v_w_uq, v_g_kv_lora, v_w_ukv, v_w_proj_fox, v_w_proj_mla, v_w_out, v_w_ffn_in, v_w_ffn_out):
    prm = dict(w_ada=w_ada, b_ada=b_ada, g_pre_mix=g_pre_mix, g_post_mix=g_post_mix, g_pre_ffn=g_pre_ffn,
               g_post_ffn=g_post_ffn, w_in=w_in, b_forget=b_forget, g_q_lora=g_q_lora, w_uq=w_uq, g_kv_lora=g_kv_lora,
               w_ukv=w_ukv, w_proj_fox=w_proj_fox, w_proj_mla=w_proj_mla, w_out=w_out, w_ffn_in=w_ffn_in, w_ffn_out=w_ffn_out)
    mom = dict(w_ada=m_w_ada, b_ada=m_b_ada, g_pre_mix=m_g_pre_mix, g_post_mix=m_g_post_mix, g_pre_ffn=m_g_pre_ffn,
               g_post_ffn=m_g_post_ffn, w_in=m_w_in, b_forget=m_b_forget, g_q_lora=m_g_q_lora, w_uq=m_w_uq,
               g_kv_lora=m_g_kv_lora, w_ukv=m_w_ukv, w_proj_fox=m_w_proj_fox, w_proj_mla=m_w_proj_mla, w_out=m_w_out,
               w_ffn_in=m_w_ffn_in, w_ffn_out=m_w_ffn_out)
    var = dict(w_ada=v_w_ada, b_ada=v_b_ada, g_pre_mix=v_g_pre_mix, g_post_mix=v_g_post_mix, g_pre_ffn=v_g_pre_ffn,
               g_post_ffn=v_g_post_ffn, w_in=v_w_in, b_forget=v_b_forget, g_q_lora=v_g_q_lora, w_uq=v_w_uq,
               g_kv_lora=v_g_kv_lora, w_ukv=v_w_ukv, w_proj_fox=v_w_proj_fox, w_proj_mla=v_w_proj_mla, w_out=v_w_out,
               w_ffn_in=v_w_ffn_in, w_ffn_out=v_w_ffn_out)
    me = _flat(*_coords())
    slot = jnp.reshape(me, (1,)).astype(jnp.int32)

    own = {n: prm[n][0].astype(BF16) for n in MATRICES}
    no_dep = jnp.zeros((8, LANES), F32)
    (st_c, st_in), tok = _async_start([[c], [own["w_in"]]], ["gather", "spread"], no_dep, "gather_in_start")
    (c_own,), (c_land,) = _async_wait(st_c, tok, "gather_c_wait")
    c_all = _with_own(c_land, c_own, me).reshape(N_DEV, D)
    ada_cols = w_ada.shape[2]
    b_cols = lax.dynamic_slice(b_ada, (0, me * ada_cols), (1, ada_cols))
    mod_cols, silu_c = _mod_part(c_all, w_ada[0], b_cols)
    (mod_all,) = _all_gather([mod_cols], "gather_mod")

    (w_in_own,), (w_in_land,) = _async_wait(st_in, mod_all, "gather_in_wait")
    (st_in,), tok = _async_start([[w_in_land]], "forward", no_dep, "gather_in_forward")
    _, (w_in_land,) = _async_wait(st_in, tok, "gather_in_forward_wait")
    w = _prepare_weights({"w_in": w_in_land}, {"w_in": w_in_own}, slot)
    later = dict(lora=("w_uq", "w_ukv"), proj=("w_proj_fox", "w_proj_mla", "w_out"), ffn=("w_ffn_in", "w_ffn_out"))
    states, tok = _async_start([[own[n] for n in names] for names in later.values()], ["gather", "spread", "spread"],
                               w["w_b"], "gather_rest_start")
    gather_state = dict(zip(later, states))
    own_thru = {}

    def wts(group, after):
        if group.startswith("relay_"):
            name = group[len("relay_"):]
            own_thru[name], lands = _async_wait(gather_state[name], after, "gather_" + name + "_wait")
            (gather_state[name],), t = _async_start([lands], "forward", no_dep, "gather_" + name + "_forward")
            return {"tok": t}
        srcs, lands = _async_wait(gather_state[group], after, "gather_" + group + "_landed")
        srcs = own_thru.get(group, srcs)
        return _prepare_weights(dict(zip(later[group], lands)), dict(zip(later[group], srcs)), slot)

    sent, late_sent, last = [], [], {}

    def send(grads, final=False, late=False):
        shards = _shard_grads(grads)
        names = list(shards)
        (state,), t = _async_start([[shards[n] for n in names]], "pair" if final else "exchange", no_dep,
                                   "exchange_" + names[0] + "_start")
        if final:
            last.update(names=names, state=state)
        else:
            (late_sent if late else sent).append((names, state))
        return t

    def relay(after):
        srcs, lands = _async_wait(last["state"], after, "exchange_pair_wait")
        core = jnp.reshape(lax.axis_index("c"), (1,)).astype(jnp.int32)
        sums = [_add_sibling(src, land, core) for src, land in zip(srcs, lands)]
        (last["state"],), t = _async_start([sums], "chips", no_dep, "exchange_chips_start")
        return t

    mod = lax.dynamic_index_in_dim(mod_all, me, axis=1, keepdims=False).reshape(1, 6 * D) + tok[0, 0]

    vec = dict(g_pre_mix=g_pre_mix, g_post_mix=g_post_mix, g_pre_ffn=g_pre_ffn, g_post_ffn=g_post_ffn,
               g_q_lora=g_q_lora, g_kv_lora=g_kv_lora, b_forget=b_forget)
    pos = positions.astype(F32).reshape(S, 1)
    grad_x, small = _fwd_bwd(x[0], pos, mod, loss_target[0], w, vec, wts, send, relay)

    bundle = jnp.concatenate([small[n] for n in SMALL_ORDER], axis=1)
    (small_state,), tok = _async_start([[bundle]], "gather", jnp.zeros((8, LANES), F32), "gather_small_start")

    out = {}
    after = tok
    for names, state in sent:
        srcs, lands = _async_wait(state, after, "exchange_" + names[0] + "_wait")
        for n, src, land in zip(names, srcs, lands):
            out[n] = _adamw(prm[n][0], mom[n][0], var[n][0], land, "adamw_" + n, src, slot)
            after = out[n][0]
    srcs, lands = _async_wait(last["state"], after, "exchange_chips_wait")
    for n, src, land in zip(last["names"], srcs, lands):
        out[n] = _adamw(prm[n][0], mom[n][0], var[n][0], land, "adamw_" + n, src, slot // 2)
        after = out[n][0]
    for names, state in late_sent:
        srcs, lands = _async_wait(state, after, "exchange_" + names[0] + "_wait")
        for n, src, land in zip(names, srcs, lands):
            out[n] = _adamw(prm[n][0], mom[n][0], var[n][0], land, "adamw_" + n, src, slot)
            after = out[n][0]

    (own_bundle,), (bundle_all,) = _async_wait(small_state, after, "gather_small_wait")
    bundle_all = _with_own(bundle_all, own_bundle, me)
    dmod_all = bundle_all[:, 0, :6 * D]
    dm_cols = lax.dynamic_slice(dmod_all, (0, me * ada_cols), (N_DEV, ada_cols))
    g_ada = _w_ada_grad(jnp.transpose(silu_c), dm_cols)
    out["w_ada"] = _adamw(w_ada[0], m_w_ada[0], v_w_ada[0], g_ada[None], "adamw_w_ada")

    offsets, off = {}, 0
    for n in SMALL_ORDER:
        offsets[n] = off
        off += small[n].shape[1]
    names = [SMALL_PARAM.get(n, n) for n in SMALL_ORDER if n != "err"]
    results, err = _adamw_rows(bundle_all, [offsets[n] for n in SMALL_ORDER if n != "err"],
                               [prm[n] for n in names], [mom[n] for n in names], [var[n] for n in names],
                               offsets["err"], D)
    out.update(zip(names, results))
    loss = 0.5 * jnp.sum(err) / D

    res = [loss, grad_x[None]]
    for kind in range(4):
        for n in WEIGHTS:
            t = out[n][kind]
            res.append(t[None] if prm[n].ndim == 3 else t)
    return tuple(res)
```

```python
import functools
import math

import jax
import jax.numpy as jnp
from jax import lax
from jax.experimental import pallas as pl
from jax.experimental.pallas import tpu as pltpu

F32 = jnp.float32
BF16 = jnp.bfloat16

N_DEV = 8
S = 2048
D = 1024
D_FF = 2816
HEADS = 8
HEAD_DIM = 64
Q_LORA = 768
KV_LORA = 256
ROPE_DIM = 32
ROPE_THETA = 10000.0
NORM_EPS = 1e-6
LANES = 128
VMEM_LIMIT = 56 * 1024 * 1024

ADAM_LR = 0.001
ADAM_B1 = 0.9
ADAM_B2 = 0.999
ADAM_EPS = 1e-08
ADAM_WD = 0.01
ADAM_STEP = 10

ATT_T = 256
LOG2E = 1.4426950408889634
N_ATT = S // ATT_T

NN = (((1,), (0,)), ((), ()))
NT = (((1,), (1,)), ((), ()))
TN = (((0,), (0,)), ((), ()))
MESH = pl.DeviceIdType.MESH


def _params(sem=None):
    return pltpu.CompilerParams(dimension_semantics=sem, vmem_limit_bytes=VMEM_LIMIT)


def _pick(n, cap):
    best = None
    for t in range(LANES, cap + 1, LANES):
        if n % t == 0:
            best = t
    return best if best is not None else n


def _mm(a, b, mode, out_dtype, name, acc=None, dep=None):
    if mode == "nn":
        (m, k), (k2, n), dn = a.shape, b.shape, NN
    elif mode == "nt":
        (m, k), (n, k2), dn = a.shape, b.shape, NT
    else:
        (k, m), (k2, n), dn = a.shape, b.shape, TN
    assert k == k2, (a.shape, b.shape, mode)
    tn = _pick(n, 640)
    tm = _pick(m, 1536)
    osz = jnp.dtype(out_dtype).itemsize

    def need(tm_):
        blk = tm_ * k * 2 + tn * k * 2 + tm_ * tn * osz + (tm_ * tn * 4 if acc is not None else 0)
        return 2 * blk + tm_ * tn * 4
    while need(tm) > 36 * 1024 * 1024 and tm % 256 == 0:
        tm //= 2

    def body(*refs):
        a_ref, b_ref, o_ref = refs[0], refs[1], refs[-1]
        r = lax.dot_general(a_ref[...], b_ref[...], dn, preferred_element_type=F32)
        if acc is not None:
            r = r + refs[2][...]
        o_ref[...] = r.astype(o_ref.dtype)

    if mode == "tn":
        a_spec = pl.BlockSpec((k, tm), lambda i, j: (0, i))
    else:
        a_spec = pl.BlockSpec((tm, k), lambda i, j: (i, 0))
    if mode == "nt":
        b_spec = pl.BlockSpec((tn, k), lambda i, j: (j, 0))
    else:
        b_spec = pl.BlockSpec((k, tn), lambda i, j: (0, j))
    o_spec = pl.BlockSpec((tm, tn), lambda i, j: (i, j))
    in_specs = [a_spec, b_spec] + ([o_spec] if acc is not None else [])
    in_specs += [pl.BlockSpec(memory_space=pl.ANY)] if dep is not None else []
    args = (a, b) + ((acc,) if acc is not None else ()) + ((dep,) if dep is not None else ())
    return pl.pallas_call(
        body, name=name, grid=(m // tm, n // tn),
        in_specs=in_specs, out_specs=o_spec,
        out_shape=jax.ShapeDtypeStruct((m, n), out_dtype),
        compiler_params=_params(("parallel", "parallel")),
    )(*args)


def _mm_epi(a, b, mode, tnb, epi, name, tm, rows=(), vecs=(), outs=(), sums=(), pro=None):
    m = a.shape[0]
    k, nb = (b.shape if mode == "nn" else b.shape[::-1])
    dn = NN if mode == "nn" else NT
    pro_fn, pro_vecs, a_off = pro if pro is not None else (None, (), 0)
    n_in = 2 + len(rows) + len(vecs)
    n_all = n_in + len(pro_vecs)
    sub = min(tm, 256)

    def body(*refs):
        if pro is not None:
            a_out, a_scr = refs[-2:]
            refs = refs[:-2]

            @pl.when(pl.program_id(1) == 0)
            def _():
                a_scr[...] = pro_fn(refs[0][...], *[x[...] for x in refs[n_in:n_all]]).astype(BF16)
                a_out[...] = a_scr[...]
            a_ref = a_scr
        else:
            a_ref = refs[0]
        o_refs = refs[n_all:n_all + len(outs)]
        s_refs = refs[n_all + len(outs):]
        if sums:
            @pl.when((pl.program_id(0) == 0) & (pl.program_id(1) == 0))
            def _():
                for s_ref in s_refs:
                    s_ref[...] = jnp.zeros(s_ref.shape, F32)
        for c in range(tm // sub):
            rs = slice(c * sub, (c + 1) * sub)
            r = lax.dot_general(a_ref[rs, :], refs[1][...], dn, preferred_element_type=F32)
            o_vals, s_vals = epi(r, *[x[rs, :] for x in refs[2:2 + len(rows)]], *[x[...] for x in refs[2 + len(rows):n_in]])
            assert len(o_vals) == len(o_refs) and len(s_vals) == len(s_refs)
            for o_ref, val in zip(o_refs, o_vals):
                o_ref[rs, :] = val.astype(o_ref.dtype)
            for s_ref, val in zip(s_refs, s_vals):
                s_ref[...] += val

    once = dict(pipeline_mode=pl.Buffered(1)) if nb == tnb else {}
    if mode == "nn":
        b_spec = pl.BlockSpec((k, tnb), lambda i, j: (0, j), **once)
    else:
        b_spec = pl.BlockSpec((tnb, k), lambda i, j: (j, 0), **once)
    in_specs = [pl.BlockSpec((tm, k), lambda i, j: (i, a_off)), b_spec]
    rows = [tuple(r) + (0,) * (3 - len(r)) for r in rows]
    in_specs += [pl.BlockSpec((tm, w), functools.partial(lambda i, j, off: (i, j + off), off=off)) for _, w, off in rows]
    in_specs += [pl.BlockSpec(v.shape, lambda i, j: (0, 0)) for v in list(vecs) + list(pro_vecs)]
    out_specs = [pl.BlockSpec((tm, w), lambda i, j: (i, j)) for _, w, _ in outs]
    out_specs += [pl.BlockSpec((1, w), lambda i, j: (0, 0)) for w in sums]
    out_shape = [jax.ShapeDtypeStruct((m, full), dt) for full, _, dt in outs]
    out_shape += [jax.ShapeDtypeStruct((1, w), F32) for w in sums]
    if pro is not None:
        out_specs.append(pl.BlockSpec((tm, k), lambda i, j: (i, 0)))
        out_shape.append(jax.ShapeDtypeStruct((m, k), BF16))
    return pl.pallas_call(
        body, name=name, grid=(m // tm, nb // tnb),
        in_specs=in_specs, out_specs=out_specs, out_shape=out_shape,
        scratch_shapes=[pltpu.VMEM((tm, k), BF16)] if pro is not None else [],
        compiler_params=_params(("arbitrary", "arbitrary") if sums else ("parallel", "arbitrary" if pro is not None else "parallel")),
    )(a, b, *[r[0] for r in rows], *vecs, *pro_vecs)


def _rowwise(fn, row_ins, vec_ins, row_outs, sum_outs, name, tm=256):
    n_in = len(row_ins) + len(vec_ins)
    n_o = len(row_outs)
    rows = row_ins[0][0].shape[0]

    def body(*refs):
        vals = [r[...] for r in refs[:n_in]]
        outs = refs[n_in:]
        ro, so = fn(*vals)
        assert len(ro) == n_o and len(so) == len(sum_outs)
        for r, v in zip(outs[:n_o], ro):
            r[...] = v.astype(r.dtype)
        if sum_outs:
            @pl.when(pl.program_id(0) == 0)
            def _():
                for r in outs[n_o:]:
                    r[...] = jnp.zeros(r.shape, F32)
            for r, v in zip(outs[n_o:], so):
                r[...] += v

    in_specs = [pl.BlockSpec((tm, w), functools.partial(lambda i, b: (i, b), b=b)) for _, w, b in row_ins]
    in_specs += [pl.BlockSpec(v.shape, lambda i: (0, 0)) for v in vec_ins]
    out_specs = [pl.BlockSpec((tm, w), lambda i: (i, 0)) for w, _ in row_outs]
    out_specs += [pl.BlockSpec((1, w), lambda i: (0, 0)) for w in sum_outs]
    out_shape = [jax.ShapeDtypeStruct((rows, w), dt) for w, dt in row_outs]
    out_shape += [jax.ShapeDtypeStruct((1, w), F32) for w in sum_outs]
    return pl.pallas_call(
        body, name=name, grid=(rows // tm,),
        in_specs=in_specs, out_specs=out_specs, out_shape=out_shape,
        compiler_params=_params(("arbitrary",)),
    )(*[a for a, _, _ in row_ins], *vec_ins)


def _sigmoid(x):
    return 1.0 / (1.0 + jnp.exp(-x))


def _rstd(x):
    return lax.rsqrt(jnp.mean(x * x, axis=-1, keepdims=True) + NORM_EPS)


def _norm_bwd(dyn, xn, r):
    return r * (dyn - xn * jnp.mean(dyn * xn, axis=-1, keepdims=True))


def _colsum(x):
    return jnp.sum(x, axis=0, keepdims=True)


def _rope_tables(pos, invf):
    def fn(p, f):
        lane = lax.broadcasted_iota(jnp.int32, (1, LANES), 1)
        ang = p * f
        cs, sn = jnp.cos(ang), jnp.sin(ang)
        rot = (lane >= 64) & (lane < 96)
        ct = jnp.where(lane < 64, 1.0, jnp.where(rot, cs, 0.0))
        sa = jnp.where((lane >= 64) & (lane < 80), -sn, 0.0)
        sb = jnp.where((lane >= 80) & (lane < 96), sn, 0.0)
        return (ct, sa, sb), ()
    return _rowwise(fn, [(pos, 1, 0)], [invf], [(LANES, F32)] * 3, [], "rope_tables")


def _rope(x, ct, sa, sb):
    return x * ct + pltpu.roll(x, LANES - 16, 1) * sa + pltpu.roll(x, 16, 1) * sb


def _rope_t(x, ct, sa, sb):
    return x * ct - pltpu.roll(x, LANES - 16, 1) * sa - pltpu.roll(x, 16, 1) * sb


def _head_mask(width, hh):
    lane = lax.broadcasted_iota(jnp.int32, (1, width), 1)
    half = width // 2
    return (lane >= hh * half) & (lane < (hh + 1) * half)


ATT_PP = 2
ATT_CHAINS = [(a, hh) for a in range(ATT_PP) for hh in range(2)]
ATT_G = HEADS // (2 * ATT_PP)


def _pair(ref_or_val, a, width, rows=slice(None)):
    return ref_or_val[rows, a * width:(a + 1) * width]


def _attn_fwd(q, qo, k, ko, v, vo, dkp, scale, bias, name):
    T = ATT_T
    assert qo % ATT_PP == 0 and ko % ATT_PP == 0 and vo % ATT_PP == 0
    qo, ko, vo = qo // ATT_PP, ko // ATT_PP, vo // ATT_PP

    def body(*refs):
        if bias is not None:
            q_ref, k_ref, v_ref, b_ref, o_ref, lse_ref, s_scr = refs
        else:
            q_ref, k_ref, v_ref, o_ref, lse_ref, s_scr = refs
        i = pl.program_id(1)
        row = lax.broadcasted_iota(jnp.int32, (T, T), 0)
        col = lax.broadcasted_iota(jnp.int32, (T, T), 1)
        qms = []
        for a, hh in ATT_CHAINS:
            qb = _pair(q_ref, a, dkp)
            qms.append(jnp.where(_head_mask(dkp, hh), qb, jnp.zeros_like(qb)))

        def fold(t):
            return [t[:, c * LANES:(c + 1) * LANES] for c in range(T // LANES)]

        def run(nt):
            mls = [jnp.full((T, LANES), -jnp.inf, F32) for _ in ATT_CHAINS]
            for j in range(nt):
                ks = slice(j * T, (j + 1) * T)
                for ci, (a, hh) in enumerate(ATT_CHAINS):
                    s = lax.dot_general(qms[ci], _pair(k_ref, a, dkp, ks), NT, preferred_element_type=F32) * (scale * LOG2E)
                    if bias is not None:
                        s = s + b_ref[2 * a + hh, j] * LOG2E
                    if j == nt - 1:
                        s = jnp.where(row >= col, s, -jnp.inf)
                    s_scr[ci, j] = s
                    for part in fold(s):
                        mls[ci] = jnp.maximum(mls[ci], part)
            ms = [jnp.max(ml, axis=1, keepdims=True) for ml in mls]
            mbs = [jnp.broadcast_to(m, (T, LANES)) for m in ms]
            for a in range(ATT_PP):
                ls = [jnp.zeros((T, LANES), F32) for _ in range(2)]
                ps, vms = [], []
                for j in range(nt):
                    vb = _pair(v_ref, a, LANES, slice(j * T, (j + 1) * T))
                    for hh in range(2):
                        parts = [jnp.exp2(part - mbs[2 * a + hh]) for part in fold(s_scr[2 * a + hh, j])]
                        for part in parts:
                            ls[hh] = ls[hh] + part
                        ps.append(jnp.concatenate(parts, axis=1).astype(BF16))
                        vms.append(jnp.where(_head_mask(LANES, hh), vb, jnp.zeros_like(vb)))
                acc = lax.dot_general(jnp.concatenate(ps, axis=1), jnp.concatenate(vms, axis=0), NN,
                                      preferred_element_type=F32)
                l0, l1 = [jnp.sum(l, axis=1, keepdims=True) for l in ls]
                lse_ref[2 * a] = ms[2 * a] + jnp.log2(l0)
                lse_ref[2 * a + 1] = ms[2 * a + 1] + jnp.log2(l1)
                inv = jnp.where(_head_mask(LANES, 0), 1.0 / l0, 1.0 / l1)
                o_ref[:, a * LANES:(a + 1) * LANES] = (acc * inv).astype(o_ref.dtype)

        for nt in range(1, N_ATT + 1):
            pl.when(i == nt - 1)(functools.partial(run, nt))

    in_specs = [
        pl.BlockSpec((T, ATT_PP * dkp), lambda g, i: (i, qo + g)),
        pl.BlockSpec((S, ATT_PP * dkp), lambda g, i: (0, ko + g)),
        pl.BlockSpec((S, ATT_PP * LANES), lambda g, i: (0, vo + g)),
    ]
    args = [q, k, v]
    if bias is not None:
        in_specs.append(pl.BlockSpec((2 * ATT_PP, N_ATT, 1, T), lambda g, i: (g, 0, 0, 0)))
        args.append(bias)
    return pl.pallas_call(
        body, name=name, grid=(ATT_G, N_ATT),
        in_specs=in_specs,
        out_specs=[pl.BlockSpec((T, ATT_PP * LANES), lambda g, i: (i, g)),
                   pl.BlockSpec((2 * ATT_PP, T, 1), lambda g, i: (g, i, 0))],
        out_shape=[jax.ShapeDtypeStruct((S, HEADS * HEAD_DIM), BF16),
                   jax.ShapeDtypeStruct((HEADS, S, 1), F32)],
        scratch_shapes=[pltpu.VMEM((len(ATT_CHAINS), N_ATT, T, T), F32)],
        compiler_params=_params(("parallel", "arbitrary")),
    )(*args)


def _attn_grad(q, qo, k, ko, v, vo, do, lse, dkp, scale, bias, qk_dtype, name):
    T = ATT_T
    has_b = bias is not None
    qo, ko, vo = qo // ATT_PP, ko // ATT_PP, vo // ATT_PP
    n_ch = len(ATT_CHAINS)

    def body(*refs):
        q_ref, k_ref, v_ref, do_ref, lse_ref = refs[:5]
        refs = refs[5:]
        if has_b:
            b_ref, refs = refs[0], refs[1:]
        dq_ref, dk_ref, dv_ref = refs[:3]
        refs = refs[3:]
        if has_b:
            db_ref, refs = refs[0], refs[1:]
        p_scr, dp_scr, dk_acc, dv_acc = refs[:4]
        db_acc = refs[4] if has_b else None
        i = pl.program_id(1)

        @pl.when(i == 0)
        def _():
            dk_acc[...] = jnp.zeros(dk_acc.shape, F32)
            dv_acc[...] = jnp.zeros(dv_acc.shape, F32)
            if has_b:
                db_acc[...] = jnp.zeros(db_acc.shape, F32)

        row = lax.broadcasted_iota(jnp.int32, (T, T), 0)
        col = lax.broadcasted_iota(jnp.int32, (T, T), 1)

        def fold(t):
            return [t[:, c * LANES:(c + 1) * LANES] for c in range(T // LANES)]

        qms, doms, lses = [], [], []
        for a, hh in ATT_CHAINS:
            qb, dob = _pair(q_ref, a, dkp), _pair(do_ref, a, LANES)
            qms.append(jnp.where(_head_mask(dkp, hh), qb, jnp.zeros_like(qb)))
            doms.append(jnp.where(_head_mask(LANES, hh), dob, jnp.zeros_like(dob)))
            lses.append(lse_ref[2 * a + hh])

        def run(nt):
            dls = [jnp.zeros((T, LANES), F32) for _ in ATT_CHAINS]
            for j in range(nt):
                ks = slice(j * T, (j + 1) * T)
                for ci, (a, hh) in enumerate(ATT_CHAINS):
                    s = lax.dot_general(qms[ci], _pair(k_ref, a, dkp, ks), NT, preferred_element_type=F32) * (scale * LOG2E)
                    if has_b:
                        s = s + b_ref[ci, j] * LOG2E
                    s = s - lses[ci]
                    if j == nt - 1:
                        s = jnp.where(row >= col, s, -jnp.inf)
                    p = jnp.exp2(s)
                    dp = lax.dot_general(doms[ci], _pair(v_ref, a, LANES, ks), NT, preferred_element_type=F32)
                    p_scr[ci, j] = p
                    dp_scr[ci, j] = dp
                    for part in fold(p * dp):
                        dls[ci] = dls[ci] + part
            deltas = [jnp.broadcast_to(jnp.sum(dl, axis=1, keepdims=True), (T, LANES)) for dl in dls]
            for a in range(ATT_PP):
                ds_all, km_all = [], []
                qm2t = jnp.transpose(jnp.concatenate([qms[2 * a], qms[2 * a + 1]], axis=0))
                dom2t = jnp.transpose(jnp.concatenate([doms[2 * a], doms[2 * a + 1]], axis=0))
                for j in range(nt):
                    ks = slice(j * T, (j + 1) * T)
                    kb = _pair(k_ref, a, dkp, ks)
                    p2, ds2 = [], []
                    for hh in range(2):
                        ci = 2 * a + hh
                        p = p_scr[ci, j]
                        ds = jnp.concatenate([pp * (dd - deltas[ci]) for pp, dd in zip(fold(p), fold(dp_scr[ci, j]))], axis=1)
                        if has_b:
                            db_acc[ci, j] += jnp.sum(ds, axis=0, keepdims=True)
                        p2.append(p.astype(BF16))
                        ds2.append((ds * scale).astype(BF16))
                        km_all.append(jnp.where(_head_mask(dkp, hh), kb, jnp.zeros_like(kb)))
                    dv_acc[a * LANES:(a + 1) * LANES, ks] += lax.dot_general(
                        dom2t, jnp.concatenate(p2, axis=0), NN, preferred_element_type=F32)
                    dk_acc[a * dkp:(a + 1) * dkp, ks] += lax.dot_general(
                        qm2t, jnp.concatenate(ds2, axis=0), NN, preferred_element_type=F32)
                    ds_all += ds2
                dq = lax.dot_general(jnp.concatenate(ds_all, axis=1), jnp.concatenate(km_all, axis=0), NN,
                                     preferred_element_type=F32)
                dq_ref[:, a * dkp:(a + 1) * dkp] = dq.astype(dq_ref.dtype)

        for nt in range(1, N_ATT + 1):
            pl.when(i == nt - 1)(functools.partial(run, nt))

        @pl.when(i == N_ATT - 1)
        def _():
            dk_ref[...] = jnp.transpose(dk_acc[...]).astype(dk_ref.dtype)
            dv_ref[...] = jnp.transpose(dv_acc[...]).astype(dv_ref.dtype)
            if has_b:
                db_ref[...] = db_acc[...]

    in_specs = [
        pl.BlockSpec((T, ATT_PP * dkp), lambda g, i: (i, qo + g)),
        pl.BlockSpec((S, ATT_PP * dkp), lambda g, i: (0, ko + g)),
        pl.BlockSpec((S, ATT_PP * LANES), lambda g, i: (0, vo + g)),
        pl.BlockSpec((T, ATT_PP * LANES), lambda g, i: (i, g)),
        pl.BlockSpec((2 * ATT_PP, T, 1), lambda g, i: (g, i, 0)),
    ]
    args = [q, k, v, do, lse]
    out_specs = [
        pl.BlockSpec((T, ATT_PP * dkp), lambda g, i: (i, g)),
        pl.BlockSpec((S, ATT_PP * dkp), lambda g, i: (0, g)),
        pl.BlockSpec((S, ATT_PP * LANES), lambda g, i: (0, g)),
    ]
    width = (HEADS // 2) * dkp
    out_shape = [
        jax.ShapeDtypeStruct((S, width), qk_dtype),
        jax.ShapeDtypeStruct((S, width), qk_dtype),
        jax.ShapeDtypeStruct((S, HEADS * HEAD_DIM), BF16),
    ]
    scratch = [pltpu.VMEM((n_ch, N_ATT, T, T), F32), pltpu.VMEM((n_ch, N_ATT, T, T), F32),
               pltpu.VMEM((ATT_PP * dkp, S), F32), pltpu.VMEM((ATT_PP * LANES, S), F32)]
    if has_b:
        bspec = pl.BlockSpec((2 * ATT_PP, N_ATT, 1, T), lambda g, i: (g, 0, 0, 0))
        in_specs.append(bspec)
        args.append(bias)
        out_specs.append(bspec)
        out_shape.append(jax.ShapeDtypeStruct((HEADS, N_ATT, 1, T), F32))
        scratch.append(pltpu.VMEM((2 * ATT_PP, N_ATT, 1, T), F32))
    return pl.pallas_call(
        body, name=name, grid=(ATT_G, N_ATT),
        in_specs=in_specs, out_specs=out_specs, out_shape=out_shape, scratch_shapes=scratch,
        compiler_params=_params(("parallel", "arbitrary")),
    )(*args)


def _tri(upper):
    a = lax.broadcasted_iota(jnp.int32, (LANES, LANES), 0)
    b = lax.broadcasted_iota(jnp.int32, (LANES, LANES), 1)
    return jnp.where(a <= b if upper else a >= b, 1.0, 0.0).astype(F32)


def _fox_gates(proj, blk, bf):
    def body(m_ref, b_ref, z_out, o_ref):
        tri = _tri(True)
        carry = jnp.zeros((HEADS, 1), F32)
        for t in range(S // LANES):
            sl = slice(t * LANES, (t + 1) * LANES)
            zt = jnp.transpose(m_ref[sl, :])[:HEADS]
            z_out[:, sl] = zt
            z = zt + b_ref[...]
            logf = jnp.minimum(z, 0.0) - jnp.log(1.0 + jnp.exp(-jnp.abs(z)))
            c = lax.dot_general(logf, tri, NN, preferred_element_type=F32,
                                precision=lax.Precision.HIGHEST) + carry
            o_ref[:, sl] = -c
            carry = c[:, LANES - 1:LANES]

    return pl.pallas_call(
        body, name="fox_gates", grid=(1,),
        in_specs=[pl.BlockSpec((S, LANES), lambda i: (0, blk)), pl.BlockSpec(bf.shape, lambda i: (0, 0))],
        out_specs=[pl.BlockSpec((HEADS, S), lambda i: (0, 0))] * 2,
        out_shape=[jax.ShapeDtypeStruct((HEADS, S), F32)] * 2,
        compiler_params=_params(("arbitrary",)),
    )(proj, bf)


def _fox_gates_bwd(dbias, zt, bf):
    def body(d_ref, z_ref, b_ref, dz_ref, dbf_ref):
        tri = _tri(False)
        carry = jnp.zeros((HEADS, 1), F32)
        tot = jnp.zeros((HEADS, 1), F32)
        for t in reversed(range(S // LANES)):
            sl = slice(t * LANES, (t + 1) * LANES)
            df = -d_ref[:, sl]
            c = lax.dot_general(df, tri, NN, preferred_element_type=F32,
                                precision=lax.Precision.HIGHEST) + carry
            carry = c[:, 0:1]
            z = z_ref[:, sl] + b_ref[...]
            dz = c * _sigmoid(-z)
            dz_ref[:, sl] = dz
            tot = tot + jnp.sum(dz, axis=1, keepdims=True)
        dbf_ref[...] = tot

    return pl.pallas_call(
        body, name="fox_gates_bwd",
        out_shape=[jax.ShapeDtypeStruct((HEADS, S), F32), jax.ShapeDtypeStruct((HEADS, 1), F32)],
        compiler_params=_params(),
    )(dbias, zt, bf)


def _mod_part(c_all, w_ada, b_cols):
    def body(c_ref, w_ref, b_ref, o_ref, s_ref):
        c = c_ref[...]
        sc = c * _sigmoid(c)
        s_ref[...] = sc
        o_ref[...] = lax.dot_general(sc, w_ref[...], NN, preferred_element_type=F32,
                                     precision=lax.Precision.HIGHEST) + b_ref[...]

    return pl.pallas_call(
        body, name="mod_part",
        out_shape=[jax.ShapeDtypeStruct((N_DEV, w_ada.shape[1]), F32), jax.ShapeDtypeStruct(c_all.shape, F32)],
        compiler_params=_params(),
    )(c_all, w_ada, b_cols)


def _w_ada_grad(sc_t, dm):
    def body(s_ref, d_ref, o_ref):
        acc = jnp.zeros(o_ref.shape, F32)
        for b in range(N_DEV):
            acc = acc + s_ref[:, b:b + 1] * d_ref[b:b + 1, :]
        o_ref[...] = acc

    return pl.pallas_call(
        body, name="w_ada_grad", out_shape=jax.ShapeDtypeStruct((sc_t.shape[0], dm.shape[1]), F32),
        compiler_params=_params(),
    )(sc_t, dm)


def _adamw(w, m, v, parts, name, own=None, slot=None):
    rows, cols = w.shape
    n = parts.shape[0]
    by_cols = rows % 256 != 0 and cols % 256 == 0
    tr, tc = (rows, 256) if by_cols else ((rows if rows <= 512 else 256), cols)
    tile = (lambda i: (0, i)) if by_cols else (lambda i: (i, 0))

    def body(*refs):
        if own is not None:
            s_ref, refs = refs[0], refs[1:]
            w_ref, m_ref, v_ref, p_ref, o_ref, g_out, d_out, m_out, v_out = refs
            terms = [jnp.where(s_ref[0] == kk, o_ref[0], p_ref[kk]) for kk in range(n)]
        else:
            w_ref, m_ref, v_ref, p_ref, g_out, d_out, m_out, v_out = refs
            terms = [p_ref[kk] for kk in range(n)]
        g = terms[0].astype(F32)
        for term in terms[1:]:
            g = g + term.astype(F32)
        g_out[...] = g
        d_out[...], m_out[...], v_out[...] = _adamw_math(w_ref[...], g, m_ref[...], v_ref[...])

    spec = pl.BlockSpec((tr, tc), lambda i, *_: tile(i))
    in_specs = [spec, spec, spec, pl.BlockSpec((n, tr, tc), lambda i, *_: (0,) + tile(i))]
    out_shape = [jax.ShapeDtypeStruct((rows, cols), F32)] * 4
    grid = (rows // tr if not by_cols else cols // tc,)
    if own is None:
        return pl.pallas_call(
            body, name=name, grid=grid, in_specs=in_specs, out_specs=[spec] * 4, out_shape=out_shape,
            compiler_params=_params(("parallel",)),
        )(w, m, v, parts)
    in_specs.append(pl.BlockSpec((1, tr, tc), lambda i, s: (s[0],) + tile(i)))
    return pl.pallas_call(
        body, name=name, out_shape=out_shape, compiler_params=_params(("parallel",)),
        grid_spec=pltpu.PrefetchScalarGridSpec(num_scalar_prefetch=1, grid=grid, in_specs=in_specs,
                                               out_specs=[spec] * 4),
    )(slot, w, m, v, parts, own)


def _adamw_math(w, g, m, v):
    mm = ADAM_B1 * m + (1.0 - ADAM_B1) * g
    vv = ADAM_B2 * v + (1.0 - ADAM_B2) * (g * g)
    m_hat = mm / (1.0 - ADAM_B1 ** ADAM_STEP)
    v_hat = vv / (1.0 - ADAM_B2 ** ADAM_STEP)
    return -ADAM_LR * (m_hat / (jnp.sqrt(v_hat) + ADAM_EPS) + ADAM_WD * w), mm, vv


def _adamw_rows(bundles, offsets, ws, ms, vs, err_off, err_width):
    k = len(ws)

    def body(*refs):
        b_ref = refs[0]
        w_refs, m_refs, v_refs = refs[1:1 + k], refs[1 + k:1 + 2 * k], refs[1 + 2 * k:1 + 3 * k]
        outs = refs[1 + 3 * k:]
        g_all = b_ref[0]
        for kk in range(1, N_DEV):
            g_all = g_all + b_ref[kk]
        for i in range(k):
            width = w_refs[i].shape[1]
            g = g_all[:, offsets[i]:offsets[i] + width]
            outs[4 * i][...] = g
            outs[4 * i + 1][...], outs[4 * i + 2][...], outs[4 * i + 3][...] = _adamw_math(
                w_refs[i][...], g, m_refs[i][...], v_refs[i][...])
        outs[4 * k][...] = g_all[:, err_off:err_off + err_width]

    out_shape = []
    for w_ in ws:
        out_shape += [jax.ShapeDtypeStruct(w_.shape, F32)] * 4
    out_shape.append(jax.ShapeDtypeStruct((1, err_width), F32))
    res = pl.pallas_call(body, name="adamw_rows", out_shape=out_shape, compiler_params=_params())(bundles, *ws, *ms, *vs)
    return [tuple(res[4 * i:4 * i + 4]) for i in range(k)], res[-1]


def _coords():
    return lax.axis_index("x"), lax.axis_index("y"), lax.axis_index("c")


def _flat(px, py, pc):
    return 4 * px + 2 * py + pc


def _all_gather(arrs, name):
    n = len(arrs)

    def body(*refs):
        ins, outs = refs[:n], refs[n:2 * n]
        send, recv, lsem = refs[2 * n:]
        x, y, c = _coords()
        me, sibling = (x, y, c), (x, y, 1 - c)
        chips = [(1 - x, y), (x, 1 - y), (1 - x, 1 - y)]

        def copy(a, kk, block, to, src=None):
            slot = outs[a].at[_flat(*block)]
            return pltpu.make_async_remote_copy(
                src_ref=slot if src is None else src, dst_ref=slot,
                send_sem=send.at[a, kk], recv_sem=recv.at[a, kk],
                device_id=to, device_id_type=MESH)

        mine = [pltpu.make_async_copy(ins[a], outs[a].at[_flat(*me)], lsem.at[a]) for a in range(n)]
        for cp in mine:
            cp.start()
        first = []
        for a in range(n):
            first.append(copy(a, 0, me, sibling, src=ins[a]))
            first += [copy(a, 1 + j, me, (*chip, c), src=ins[a]) for j, chip in enumerate(chips)]
        for cp in first:
            cp.start()
        passed = []
        for j, chip in enumerate(chips):
            for a in range(n):
                copy(a, 1 + j, (*chip, c), me).wait_recv()
                cp = copy(a, 4 + j, (*chip, c), sibling)
                cp.start()
                passed.append(cp)
        for a in range(n):
            copy(a, 0, sibling, me).wait_recv()
        for j, chip in enumerate(chips):
            for a in range(n):
                copy(a, 4 + j, (*chip, 1 - c), me).wait_recv()
        for cp in first + passed:
            cp.wait_send()
        for cp in mine:
            cp.wait()

    any_spec = pl.BlockSpec(memory_space=pl.ANY)
    return pl.pallas_call(
        body, name=name,
        in_specs=[any_spec] * n, out_specs=[any_spec] * n,
        out_shape=[jax.ShapeDtypeStruct((N_DEV,) + a.shape, a.dtype) for a in arrs],
        scratch_shapes=[pltpu.SemaphoreType.DMA((n, 7)), pltpu.SemaphoreType.DMA((n, 7)),
                        pltpu.SemaphoreType.DMA((n,))],
    )(*arrs)


def _peer_list():
    x, y, c = _coords()
    return [((1 - x if r & 4 else x), (1 - y if r & 2 else y), (1 - c if r & 1 else c)) for r in range(1, N_DEV)]


def _copy_plan(mode, src, land):
    x, y, c = _coords()
    me = _flat(x, y, c)
    if mode == "gather":
        return [(src, land.at[me], peer) for peer in _peer_list()]
    if mode == "exchange":
        return [(src.at[_flat(*peer)], land.at[me], peer) for peer in _peer_list()]
    if mode == "pair":
        return [(src.at[_flat(q // 2, q % 2, 1 - c)], land.at[q], (x, y, 1 - c)) for q in range(N_DEV // 2)]
    chips = [((1 - x if r & 2 else x), (1 - y if r & 1 else y)) for r in range(1, N_DEV // 2)]
    if mode == "chips":
        return [(src.at[2 * qx + qy], land.at[2 * x + y], (qx, qy, c)) for qx, qy in chips]
    if mode == "spread":
        return [(src, land.at[me], (x, y, 1 - c))] + [(src, land.at[me], (qx, qy, c)) for qx, qy in chips]
    assert mode == "forward"
    return [(land.at[_flat(qx, qy, c)], land.at[_flat(qx, qy, c)], (x, y, 1 - c)) for qx, qy in chips]


N_COPIES = dict(gather=N_DEV - 1, exchange=N_DEV - 1, pair=N_DEV // 2, chips=N_DEV // 2 - 1, spread=N_DEV // 2,
                forward=N_DEV // 2 - 1)


def _land_shape(mode, shape):
    return {"gather": (N_DEV,) + shape, "spread": (N_DEV,) + shape, "exchange": shape,
            "pair": (N_DEV // 2,) + shape[1:], "chips": shape}[mode]


HBM_SPEC = pl.BlockSpec(memory_space=pltpu.HBM)
SEM_SPEC = pl.BlockSpec(memory_space=pltpu.SEMAPHORE)
ANY_SPEC = pl.BlockSpec(memory_space=pl.ANY)
SIDE_EFFECT = pltpu.SideEffectType.DATAFLOW_SIDE_EFFECTING


def _async_start(groups, modes, after, name):
    modes = [modes] * len(groups) if isinstance(modes, str) else list(modes)
    arrs = [(a, m) for g, m in zip(groups, modes) for a in g]
    n = len(arrs)
    fresh = [i for i, (_, m) in enumerate(arrs) if m != "forward"]

    def body(*refs):
        srcs, new_lands = refs[:n], refs[n:n + len(fresh)]
        outs = refs[n + len(fresh) + 1:]
        lands = list(srcs)
        for k, i in enumerate(fresh):
            lands[i] = new_lands[k]
        for ai, (_, mode) in enumerate(arrs):
            for src_ref, dst_ref, peer in _copy_plan(mode, srcs[ai], lands[ai]):
                pltpu.make_async_remote_copy(src_ref=src_ref, dst_ref=dst_ref, send_sem=outs[2 * ai],
                                             recv_sem=outs[2 * ai + 1], device_id=peer, device_id_type=MESH).start()
        outs[-1][...] = jnp.zeros(outs[-1].shape, F32)

    land_shapes = [(_land_shape(arrs[i][1], arrs[i][0].shape), arrs[i][0].dtype) for i in fresh]
    n_buf = n + len(fresh)
    out_shape = [pltpu.SemaphoreType.DMA(())] * (2 * n)
    out_shape += [pltpu.HBM(a.shape, a.dtype) for a, _ in arrs]
    out_shape += [pltpu.HBM(shape, dt) for shape, dt in land_shapes]
    out_shape.append(jax.ShapeDtypeStruct((8, LANES), F32))
    res = pl.pallas_call(
        body, name=name, out_shape=tuple(out_shape),
        in_specs=[HBM_SPEC] * n_buf + [ANY_SPEC],
        out_specs=tuple([SEM_SPEC] * (2 * n) + [HBM_SPEC] * n_buf + [pl.BlockSpec(memory_space=pltpu.VMEM)]),
        input_output_aliases={i: 2 * n + i for i in range(n_buf)},
        compiler_params=pltpu.CompilerParams(has_side_effects=SIDE_EFFECT),
    )(*[pltpu.with_memory_space_constraint(a, pltpu.HBM) for a, _ in arrs],
      *[pltpu.with_memory_space_constraint(lax.empty(shape, dt), pltpu.HBM) for shape, dt in land_shapes],
      after)
    sems, thru = res[:2 * n], res[2 * n:-1]
    land_of = {i: thru[n + k] for k, i in enumerate(fresh)}
    states, idx = [], 0
    for g, mode in zip(groups, modes):
        ids = range(idx, idx + len(g))
        idx += len(g)
        states.append(([sems[2 * i] for i in ids], [sems[2 * i + 1] for i in ids],
                       None if mode == "forward" else [thru[i] for i in ids],
                       [land_of.get(i, thru[i]) for i in ids], mode))
    return states, res[-1]


def _async_wait(state, after, name):
    sends, recvs, srcs, lands, mode = state
    g = len(lands)
    bufs = (list(srcs) if srcs is not None else []) + list(lands)
    nb = len(bufs)

    def body(*refs):
        l_refs, sems = refs[nb - g:nb], refs[nb:nb + 2 * g]
        for ai in range(g):
            moved = l_refs[ai].at[pl.ds(0, N_COPIES[mode])]
            cp = pltpu.make_async_remote_copy(src_ref=moved, dst_ref=moved, send_sem=sems[ai], recv_sem=sems[g + ai],
                                              device_id=_coords(), device_id_type=MESH)
            cp.wait_send()
            cp.wait_recv()

    res = pl.pallas_call(
        body, name=name,
        out_shape=tuple(pltpu.HBM(a.shape, a.dtype) for a in bufs),
        in_specs=[HBM_SPEC] * nb + [SEM_SPEC] * (2 * g) + [ANY_SPEC],
        out_specs=tuple([HBM_SPEC] * nb),
        input_output_aliases={i: i for i in range(nb)},
        compiler_params=pltpu.CompilerParams(has_side_effects=SIDE_EFFECT),
    )(*bufs, *sends, *recvs, after)
    return (list(res[:nb - g]) if srcs is not None else None), list(res[nb - g:])


def _add_sibling(mine, theirs, core):
    def body(c_ref, a_ref, b_ref, o_ref):
        o_ref[...] = (a_ref[...].astype(F32) + b_ref[...].astype(F32)).astype(o_ref.dtype)

    blk = (1,) + mine.shape[1:]
    return pl.pallas_call(
        body, name="add_sibling", out_shape=jax.ShapeDtypeStruct(theirs.shape, mine.dtype),
        grid_spec=pltpu.PrefetchScalarGridSpec(
            num_scalar_prefetch=1, grid=(theirs.shape[0],),
            in_specs=[pl.BlockSpec(blk, lambda q, c: (2 * q + c[0], 0, 0)), pl.BlockSpec(blk, lambda q, c: (q, 0, 0))],
            out_specs=pl.BlockSpec(blk, lambda q, c: (q, 0, 0))),
        compiler_params=_params(("parallel",)),
    )(core, mine, theirs)


def _with_own(land, own, me):
    return lax.dynamic_update_index_in_dim(land, own, me, 0)


IN_SPLITS = (512, 512, 512, 8, 768, 256, 32, 1024, 1024)


def _from_shards(g, fn, out_widths, name, own=None, slot=None):
    _, k, n = g.shape
    tr = min(k, 256)

    def body(*refs):
        if own is not None:
            s_ref, g_ref, own_ref = refs[:3]
            cols = [jnp.where(s_ref[0] == j, own_ref[...], g_ref[j]) for j in range(N_DEV)]
        else:
            g_ref = refs[0]
            cols = [g_ref[j] for j in range(N_DEV)]
        for o_ref, val in zip(refs[-len(out_widths):], fn(jnp.concatenate(cols, axis=1))):
            o_ref[...] = val

    in_specs = [pl.BlockSpec((N_DEV, tr, n), lambda i, *_: (0, i, 0))]
    out_spec = [pl.BlockSpec((tr, wd), lambda i, *_: (i, 0)) for wd in out_widths]
    out_shape = [jax.ShapeDtypeStruct((k, wd), g.dtype) for wd in out_widths]
    if own is None:
        return pl.pallas_call(body, name=name, grid=(k // tr,), in_specs=in_specs, out_specs=out_spec,
                              out_shape=out_shape, compiler_params=_params(("parallel",)))(g)
    in_specs.append(pl.BlockSpec((tr, n), lambda i, *_: (i, 0)))
    return pl.pallas_call(
        body, name=name, out_shape=out_shape, compiler_params=_params(("parallel",)),
        grid_spec=pltpu.PrefetchScalarGridSpec(num_scalar_prefetch=1, grid=(k // tr,), in_specs=in_specs, out_specs=out_spec),
    )(slot, g, own)


def _unshard_cols(g, own=None, slot=None):
    return _from_shards(g, lambda full: (full,), [N_DEV * g.shape[2]], "unshard_cols_%d" % g.shape[2], own, slot)[0]


FFN_T = 256
FFN_SHARD = 2 * D_FF // N_DEV


def _unshard_ffn_in(g, own=None, slot=None):
    def pairs(full):
        parts = []
        for j in range(D_FF // FFN_T):
            parts += [full[:, j * FFN_T:(j + 1) * FFN_T], full[:, D_FF + j * FFN_T:D_FF + (j + 1) * FFN_T]]
        return (jnp.concatenate(parts, axis=1),)

    return _from_shards(g, pairs, [2 * D_FF], "unshard_ffn_in", own, slot)[0]


def _shard_ffn_in_t(wt):
    tc = 256

    def body(w_ref, o_ref):
        x = w_ref[...]
        nb = D_FF // FFN_T
        full = jnp.concatenate([x[(2 * j + half) * FFN_T:(2 * j + half + 1) * FFN_T]
                                for half in range(2) for j in range(nb)], axis=0)
        for j in range(N_DEV):
            o_ref[j] = full[j * FFN_SHARD:(j + 1) * FFN_SHARD]

    return pl.pallas_call(
        body, name="shard_ffn_in_t", grid=(D // tc,),
        in_specs=[pl.BlockSpec((2 * D_FF, tc), lambda i: (0, i))],
        out_specs=pl.BlockSpec((N_DEV, FFN_SHARD, tc), lambda i: (0, 0, i)),
        out_shape=jax.ShapeDtypeStruct((N_DEV, FFN_SHARD, D), wt.dtype),
        compiler_params=_params(("parallel",)),
    )(wt)


def _shard_cols(w):
    k, n = w.shape[0], w.shape[1] // N_DEV
    tr = min(k, 256)

    def body(w_ref, o_ref):
        full = w_ref[...]
        for j in range(N_DEV):
            o_ref[j] = full[:, j * n:(j + 1) * n]

    return pl.pallas_call(
        body, name="shard_cols_%d" % n, grid=(k // tr,),
        in_specs=[pl.BlockSpec((tr, N_DEV * n), lambda i: (i, 0))],
        out_specs=pl.BlockSpec((N_DEV, tr, n), lambda i: (0, i, 0)),
        out_shape=jax.ShapeDtypeStruct((N_DEV, k, n), w.dtype),
        compiler_params=_params(("parallel",)),
    )(w)


IN_OFFS = tuple(sum(IN_SPLITS[:i]) for i in range(len(IN_SPLITS) + 1))
IN_SHARD = IN_OFFS[-1] // N_DEV
REGROUP_ROWS = 128
MISC_AT = Q_LORA + KV_LORA + 2 * D
A_COLS = MISC_AT + LANES
A_TILE = A_COLS // 5
B_COLS = 3 * HEADS * HEAD_DIM
MISC_BLOCK = MISC_AT // LANES
KR_AT = 64


def _w_in_regroup(g, own=None, slot=None):
    def groups(full):
        fq, fk, fv, wf, cq, ckv, kr, gf, gm = [full[:, IN_OFFS[i]:IN_OFFS[i + 1]] for i in range(9)]
        rows = full.shape[0]
        gap = jnp.zeros((rows, KR_AT - HEADS), BF16)
        tail = jnp.zeros((rows, LANES - KR_AT - ROPE_DIM), BF16)
        return jnp.concatenate([cq, ckv, gf, gm, wf, gap, kr, tail], axis=1), jnp.concatenate([fq, fk, fv], axis=1)

    return _from_shards(g, groups, [A_COLS, B_COLS], "w_in_regroup", own, slot)


def _w_in_ungroup(da, db_):
    def body(a_ref, b_ref, o_ref):
        a = a_ref[...]
        lora = Q_LORA + KV_LORA
        full = jnp.concatenate([b_ref[...], a[:, MISC_AT:MISC_AT + HEADS], a[:, :lora],
                                a[:, MISC_AT + KR_AT:MISC_AT + KR_AT + ROPE_DIM], a[:, lora:MISC_AT]], axis=1)
        for j in range(N_DEV):
            o_ref[j] = full[:, j * IN_SHARD:(j + 1) * IN_SHARD]

    tr = REGROUP_ROWS
    return pl.pallas_call(
        body, name="w_in_ungroup", grid=(D // tr,),
        in_specs=[pl.BlockSpec((tr, A_COLS), lambda i: (i, 0)), pl.BlockSpec((tr, B_COLS), lambda i: (i, 0))],
        out_specs=pl.BlockSpec((N_DEV, tr, IN_SHARD), lambda i: (0, i, 0)),
        out_shape=jax.ShapeDtypeStruct((N_DEV, D, IN_SHARD), BF16),
        compiler_params=_params(("parallel",)),
    )(da, db_)


def _prepare_weights(g, own=None, slot=None):
    w = {}
    if own is not None:
        small = ("w_uq", "w_ukv", "w_out", "w_ffn_out")
        g = {n: (_with_own(a, own[n], slot[0]) if n in small else a) for n, a in g.items()}
    pick = (lambda n: (own[n], slot)) if own is not None else (lambda n: (None, None))
    if "w_in" in g:
        w["w_a"], w["w_b"] = _w_in_regroup(g["w_in"], *pick("w_in"))
    if "w_uq" in g:
        w_uq = g["w_uq"].reshape(Q_LORA, HEADS, 96)
        w["w_uq"] = jnp.pad(w_uq, ((0, 0), (0, 0), (0, 32))).reshape(Q_LORA, HEADS * LANES)
        ukv = g["w_ukv"]
        w["w_k"] = jnp.transpose(jnp.pad(ukv[:, :, :64], ((0, 0), (0, 0), (0, 64))), (1, 0, 2)).reshape(KV_LORA, HEADS * LANES)
        w["w_v"] = jnp.transpose(ukv[:, :, 64:], (1, 0, 2)).reshape(KV_LORA, HEADS * HEAD_DIM)
    if "w_out" in g:
        w["w_pf"] = _unshard_cols(g["w_proj_fox"], *pick("w_proj_fox"))
        w["w_pm"] = _unshard_cols(g["w_proj_mla"], *pick("w_proj_mla"))
        w["w_out"] = g["w_out"].reshape(D, D)
    if "w_ffn_in" in g:
        w["w_ffn_in"] = _unshard_ffn_in(g["w_ffn_in"], *pick("w_ffn_in"))
        w["w_ffn_out"] = g["w_ffn_out"].reshape(D_FF, D)
    return w


def _shard_grads(dw):
    out = {}
    if "w_a" in dw:
        out["w_in"] = _w_in_ungroup(dw["w_a"], dw["w_b"])
    if "w_uq" in dw:
        w_uq = dw["w_uq"].reshape(Q_LORA, HEADS, LANES)[:, :, :96].reshape(Q_LORA, Q_LORA)
        out["w_uq"] = w_uq.reshape(N_DEV, Q_LORA // N_DEV, Q_LORA)
        k_part = dw["w_k"].reshape(KV_LORA, HEADS, LANES)[:, :, :64]
        v_part = dw["w_v"].reshape(KV_LORA, HEADS, HEAD_DIM)
        out["w_ukv"] = jnp.transpose(jnp.concatenate([k_part, v_part], axis=2), (1, 0, 2))
    if "w_out" in dw:
        out["w_proj_fox"] = _shard_cols(dw["w_pf"])
        out["w_proj_mla"] = _shard_cols(dw["w_pm"])
        out["w_out"] = dw["w_out"].reshape(N_DEV, D // N_DEV, D)
    if "w_ffn_in" in dw:
        out["w_ffn_in"] = _shard_ffn_in_t(dw["w_ffn_in"])
        out["w_ffn_out"] = dw["w_ffn_out"].reshape(N_DEV, D_FF // N_DEV, D)
    return out


def _fwd_bwd(x, pos, mod, target, w, vec, wts, send, relay):
    shift_mix, scale_mix, gate_mix, shift_ffn, scale_ffn, gate_ffn = [mod[:, i * D:(i + 1) * D] for i in range(6)]
    g_pre_mix, g_post_mix, g_pre_ffn, g_post_ffn = vec["g_pre_mix"], vec["g_post_mix"], vec["g_pre_ffn"], vec["g_post_ffn"]
    g_q, g_kv = vec["g_q_lora"], vec["g_kv_lora"]

    inv_freq = 1.0 / (ROPE_THETA ** (jnp.arange(0, ROPE_DIM, 2, dtype=F32) / ROPE_DIM))
    invf = jnp.concatenate([jnp.zeros((64,), F32), inv_freq, inv_freq, jnp.zeros((32,), F32)]).reshape(1, LANES)
    ct, sa, sb = _rope_tables(pos, invf)

    def pre1(xv, g, sc, sh):
        return (xv * _rstd(xv) * g) * (1.0 + sc) + sh
    proj_a, h = _mm_epi(x, w["w_a"], "nn", A_TILE, lambda r: ((r,), ()), "in_proj_a", 1024, outs=[(A_COLS, A_TILE, F32)],
                        pro=(pre1, [g_pre_mix, scale_mix, shift_mix], 0))
    qkv = _mm(h, w["w_b"], "nn", BF16, "in_proj_b")

    def lora_norm(cv, g):
        return cv * _rstd(cv) * g
    w = {**w, **wts("lora", qkv)}
    tables = [(ct, LANES), (sa, LANES), (sb, LANES)]

    def rope_q(qv, c_, a_, b_):
        return (jnp.concatenate([_rope(qv[:, hd * LANES:(hd + 1) * LANES], c_, a_, b_) for hd in range(HEADS)], axis=1),), ()
    q_m, cqn = _mm_epi(proj_a, w["w_uq"], "nn", D, rope_q, "mla_uq", 512, rows=tables, outs=[(D, D, BF16)],
                       pro=(lora_norm, [g_q], 0))

    def rope_k(kv, misc, c_, a_, b_):
        lane = lax.broadcasted_iota(jnp.int32, (1, LANES), 1)
        kpe = jnp.where((lane >= 64) & (lane < 96), _rope(misc, c_, a_, b_), 0.0)
        return (jnp.concatenate([kv[:, hd * LANES:(hd + 1) * LANES] + kpe for hd in range(HEADS)], axis=1),), ()
    k_m, ckvn = _mm_epi(proj_a, w["w_k"], "nn", D, rope_k, "mla_uk", 512, rows=[(proj_a, LANES, MISC_BLOCK)] + tables,
                        outs=[(D, D, BF16)], pro=(lora_norm, [g_kv], Q_LORA // KV_LORA))
    v_m = _mm(ckvn, w["w_v"], "nn", BF16, "mla_uv")

    bf = jnp.transpose(vec["b_forget"])
    zt, neg_f = _fox_gates(proj_a, MISC_BLOCK, bf)
    bias = neg_f.reshape(HEADS, N_ATT, 1, ATT_T)
    o_b, lse_b = _attn_fwd(q_m, 0, k_m, 0, v_m, 0, 2 * LANES, 1.0 / math.sqrt(64 + ROPE_DIM), None, "mla_attn")
    bias = bias + wts("relay_proj", o_b)["tok"][0, 0]
    o_a, lse_a = _attn_fwd(qkv, 0, qkv, 4, qkv, 8, LANES, 1.0 / math.sqrt(HEAD_DIM), bias, "fox_attn")

    w = {**w, **wts("proj", o_a)}
    gate_mix = gate_mix + wts("relay_ffn", o_a)["tok"][0, 0]
    pa = _mm(o_a, w["w_pf"], "nn", BF16, "proj_fox")

    def merge(pb_, gf, gm, pa_):
        return (_sigmoid(gf) * pa_ + _sigmoid(gm) * pb_, pb_), ()
    merged, pb = _mm_epi(o_b, w["w_pm"], "nn", 512, merge, "proj_mla", 1024,
                         rows=[(proj_a, 512, 2), (proj_a, 512, 4), (pa, 512)], outs=[(D, 512, BF16), (D, 512, BF16)])
    def post1(yv, xv, gate, gpost, gpre, sc, sh):
        x1 = xv + gate * (yv * _rstd(yv) * gpost)
        return (x1, (x1 * _rstd(x1) * gpre) * (1.0 + sc) + sh, yv), ()
    x1, h2, y = _mm_epi(merged, w["w_out"], "nn", D, post1, "out_proj", 512, rows=[(x, D)],
                        vecs=[gate_mix, g_post_mix, g_pre_ffn, scale_ffn, shift_ffn],
                        outs=[(D, D, F32), (D, D, BF16), (D, D, F32)])
    w = {**w, **wts("ffn", h2)}

    def swiglu(r):
        g, u = r[:, :FFN_T], r[:, FFN_T:]
        return (g * _sigmoid(g) * u, r), ()
    act, gu = _mm_epi(h2, w["w_ffn_in"], "nn", 2 * FFN_T, swiglu, "ffn_in", 1024,
                      outs=[(D_FF, FFN_T, BF16), (2 * D_FF, 2 * FFN_T, BF16)])

    def head(y2v, x1v, tv, gate, gpost):
        r = _rstd(y2v)
        yn = y2v * r
        n2 = yn * gpost
        err = (x1v + gate * n2) - tv
        dx2 = err * (1.0 / D)
        dn2 = dx2 * gate
        dy2 = _norm_bwd(dn2 * gpost, yn, r)
        return (dx2, dy2), (_colsum(err * err), _colsum(dx2 * n2), _colsum(dn2 * yn))
    dx2, dy2, err_cols, d_gate_ffn, d_g_post_ffn = _mm_epi(
        act, w["w_ffn_out"], "nn", D, head, "ffn_out", 512, rows=[(x1, D), (target, D)], vecs=[gate_ffn, g_post_ffn],
        outs=[(D, D, F32), (D, D, BF16)], sums=[D, D, D])

    def swiglu_bwd(da, guv):
        g, u = guv[:, :FFN_T].astype(F32), guv[:, FFN_T:].astype(F32)
        sg = _sigmoid(g)
        return (jnp.concatenate([da * u * (sg * (1.0 + g * (1.0 - sg))), da * (g * sg)], axis=1),), ()
    (dgu,) = _mm_epi(dy2, w["w_ffn_out"], "nt", FFN_T, swiglu_bwd, "ffn_out_dx", 1024, rows=[(gu, 2 * FFN_T)],
                     outs=[(2 * D_FF, 2 * FFN_T, BF16)])
    dw = {"w_ffn_out": _mm(act, dy2, "tn", BF16, "ffn_out_dw")}
    dw["w_ffn_in"] = _mm(dgu, h2, "tn", BF16, "ffn_in_dw")
    gate_mix = gate_mix + send({n: dw.pop(n) for n in ("w_ffn_in", "w_ffn_out")})[0, 0]

    def mid(dh, x1v, dx2v, yv, gpre, sc, gate, gpost):
        r2 = _rstd(x1v)
        x1n = x1v * r2
        t = dh * x1n
        dx1 = dx2v + _norm_bwd(dh * (gpre * (1.0 + sc)), x1n, r2)
        ry = _rstd(yv)
        yn = yv * ry
        dn1 = dx1 * gate
        dy = _norm_bwd(dn1 * gpost, yn, ry)
        sums = (_colsum(dh), _colsum(t) * gpre, _colsum(t) * (1.0 + sc), _colsum(dx1 * (yn * gpost)), _colsum(dn1 * yn))
        return (dx1, dy), sums
    dx1, dy, d_shift_ffn, d_scale_ffn, d_g_pre_ffn, d_gate_mix, d_g_post_mix = _mm_epi(
        dgu, w["w_ffn_in"], "nt", D, mid, "ffn_in_dx", 512, rows=[(x1, D), (dx2, D), (y, D)],
        vecs=[g_pre_ffn, scale_ffn, gate_mix, g_post_mix], outs=[(D, D, F32), (D, D, BF16)], sums=[D] * 5)

    dw["w_out"] = _mm(merged, dy, "tn", BF16, "out_proj_dw")

    def merge_bwd(dm, gf, gm, pa_, pb_):
        sf, sm = _sigmoid(gf), _sigmoid(gm)
        return (dm * sf, dm * sm, dm * pa_ * (sf * (1.0 - sf)), dm * pb_ * (sm * (1.0 - sm))), ()
    dpa, dpb, dgf, dgm = _mm_epi(dy, w["w_out"], "nt", 512, merge_bwd, "out_proj_dx", 1024,
                                 rows=[(proj_a, 512, 2), (proj_a, 512, 4), (pa, 512), (pb, 512)],
                                 outs=[(D, 512, BF16)] * 4)
    do_a = _mm(dpa, w["w_pf"], "nt", BF16, "proj_fox_dx")
    do_b = _mm(dpb, w["w_pm"], "nt", BF16, "proj_mla_dx")
    dw["w_pf"] = _mm(o_a, dpa, "tn", BF16, "proj_fox_dw")
    dw["w_pm"] = _mm(o_b, dpb, "tn", BF16, "proj_mla_dw")
    bias = bias + send({n: dw.pop(n) for n in ("w_out", "w_pf", "w_pm")})[0, 0]

    sc_a, sc_b = 1.0 / math.sqrt(HEAD_DIM), 1.0 / math.sqrt(64 + ROPE_DIM)
    dq_a, dk_a, dv_a, dbias = _attn_grad(qkv, 0, qkv, 4, qkv, 8, do_a, lse_a, LANES, sc_a, bias, BF16, "fox_attn_bwd")
    dq_m, dk_m, dv_m = _attn_grad(q_m, 0, k_m, 0, v_m, 0, do_b, lse_b, 2 * LANES, sc_b, None, F32, "mla_attn_bwd")

    def mla_rope_bwd(dq, dk, c_, a_, b_):
        lane = lax.broadcasted_iota(jnp.int32, (1, LANES), 1)
        dqs = [_rope_t(dq[:, hd * LANES:(hd + 1) * LANES], c_, a_, b_) for hd in range(HEADS)]
        dkpe = dk[:, 0:LANES]
        for hd in range(1, HEADS):
            dkpe = dkpe + dk[:, hd * LANES:(hd + 1) * LANES]
        dkpe = jnp.where((lane >= 64) & (lane < 96), dkpe, 0.0)
        dkr = jnp.where((lane >= 64) & (lane < 96), _rope_t(dkpe, c_, a_, b_), 0.0)
        return (jnp.concatenate(dqs, axis=1), dk, dkr), ()
    dqb, dkb, dkr = _rowwise(mla_rope_bwd, [(dq_m, D, 0), (dk_m, D, 0), (ct, LANES, 0), (sa, LANES, 0), (sb, LANES, 0)],
                             [], [(D, BF16), (D, BF16), (LANES, F32)], [], "mla_rope_bwd")
    def lora_q_bwd(dq, cq, gq):
        rq = _rstd(cq)
        cqh = cq * rq
        return (_norm_bwd(dq * gq, cqh, rq),), (_colsum(dq * cqh),)
    dcq, d_g_q = _mm_epi(dqb, w["w_uq"], "nt", Q_LORA, lora_q_bwd, "mla_uq_dx", 512, rows=[(proj_a, Q_LORA, 0)],
                         vecs=[g_q], outs=[(Q_LORA, Q_LORA, BF16)], sums=[Q_LORA])

    def lora_kv_bwd(dv_part, dk_part, ckv, gkv):
        dkv = dv_part + dk_part
        rk = _rstd(ckv)
        ckh = ckv * rk
        return (_norm_bwd(dkv * gkv, ckh, rk),), (_colsum(dkv * ckh),)
    dckv, d_g_kv = _mm_epi(dv_m, w["w_v"], "nt", KV_LORA, lora_kv_bwd, "mla_uv_dx", 1024,
                           rows=[(_mm(dkb, w["w_k"], "nt", F32, "mla_uk_dx"), KV_LORA), (proj_a, KV_LORA, 3)],
                           vecs=[g_kv], outs=[(KV_LORA, KV_LORA, BF16)], sums=[KV_LORA])

    dzt, d_bf = _fox_gates_bwd(dbias.reshape(HEADS, S), zt, bf)
    dmisc = (dkr + jnp.pad(jnp.transpose(dzt), ((0, 0), (0, LANES - HEADS)))).astype(BF16)
    dproj_a = jnp.concatenate([dcq, dckv, dgf, dgm, dmisc], axis=1)
    dqkv = jnp.concatenate([dq_a, dk_a, dv_a], axis=1)
    dw["w_a"] = _mm(h, dproj_a, "tn", BF16, "in_proj_a_dw")
    dw["w_b"] = _mm(h, dqkv, "tn", BF16, "in_proj_b_dw")
    tok = send(dw, True)
    dh_a = _mm(dproj_a, w["w_a"], "nt", F32, "in_proj_a_dx", dep=tok)
    tok = relay(dh_a)
    tok = send({"w_uq": _mm(cqn, dqb, "tn", BF16, "mla_uq_dw", dep=tok),
                "w_k": _mm(ckvn, dkb, "tn", BF16, "mla_uk_dw", dep=tok),
                "w_v": _mm(ckvn, dv_m, "tn", BF16, "mla_uv_dw", dep=tok)}, late=True)
    g_pre_mix = g_pre_mix + tok[0, 0]

    def first(dh_b, dh_a, xv, dx1v, gpre, sc):
        dhv = dh_b + dh_a
        r = _rstd(xv)
        xn = xv * r
        t = dhv * xn
        dx = dx1v + _norm_bwd(dhv * (gpre * (1.0 + sc)), xn, r)
        return (dx,), (_colsum(dhv), _colsum(t) * gpre, _colsum(t) * (1.0 + sc))
    grad_x, d_shift_mix, d_scale_mix, d_g_pre_mix = _mm_epi(
        dqkv, w["w_b"], "nt", D, first, "in_proj_b_dx", 512,
        rows=[(dh_a, D), (x, D), (dx1, D)],
        vecs=[g_pre_mix, scale_mix], outs=[(D, D, F32)], sums=[D] * 3)

    dmod = jnp.concatenate([d_shift_mix, d_scale_mix, d_gate_mix, d_shift_ffn, d_scale_ffn, d_gate_ffn], axis=1)
    small = dict(dmod=dmod, g_pre_mix=d_g_pre_mix, g_post_mix=d_g_post_mix, g_pre_ffn=d_g_pre_ffn,
                 g_post_ffn=d_g_post_ffn, g_q_lora=d_g_q, g_kv_lora=d_g_kv,
                 b_forget=jnp.pad(jnp.transpose(d_bf), ((0, 0), (0, LANES - HEADS))), err=err_cols)
    return grad_x, small


SMALL_ORDER = ("dmod", "g_pre_mix", "g_post_mix", "g_pre_ffn", "g_post_ffn", "g_q_lora", "g_kv_lora", "b_forget", "err")
SMALL_PARAM = {"dmod": "b_ada"}
MATRICES = ("w_in", "w_uq", "w_ukv", "w_proj_fox", "w_proj_mla", "w_out", "w_ffn_in", "w_ffn_out")
WEIGHTS = ("w_ada", "b_ada", "g_pre_mix", "g_post_mix", "g_pre_ffn", "g_post_ffn", "w_in", "b_forget", "g_q_lora",
           "w_uq", "g_kv_lora", "w_ukv", "w_proj_fox", "w_proj_mla", "w_out", "w_ffn_in", "w_ffn_out")


def kernel(x, c, positions, w_ada, b_ada, g_pre_mix, g_post_mix, g_pre_ffn, g_post_ffn, w_in, b_forget, g_q_lora, w_uq, g_kv_lora, w_ukv, w_proj_fox, w_proj_mla, w_out, w_ffn_in, w_ffn_out, loss_target, m_w_ada, m_b_ada, m_g_pre_mix, m_g_post_mix, m_g_pre_ffn, m_g_post_ffn, m_w_in, m_b_forget, m_g_q_lora, m_w_uq, m_g_kv_lora, m_w_ukv, m_w_proj_fox, m_w_proj_mla, m_w_out, m_w_ffn_in, m_w_ffn_out, v_w_ada, v_b_ada, v_g_pre_mix, v_g_post_mix, v_g_pre_ffn, v_g_post_ffn, v_w_in, v_b_forget, v_g_q_lora, v_w_uq, v_g_kv_lora, v_w_ukv, v_w_proj_fox, v_w_proj_mla, v_w_out, v_w_ffn_in, v_w_ffn_out):
    prm = dict(w_ada=w_ada, b_ada=b_ada, g_pre_mix=g_pre_mix, g_post_mix=g_post_mix, g_pre_ffn=g_pre_ffn,
               g_post_ffn=g_post_ffn, w_in=w_in, b_forget=b_forget, g_q_lora=g_q_lora, w_uq=w_uq, g_kv_lora=g_kv_lora,
               w_ukv=w_ukv, w_proj_fox=w_proj_fox, w_proj_mla=w_proj_mla, w_out=w_out, w_ffn_in=w_ffn_in, w_ffn_out=w_ffn_out)
    mom = dict(w_ada=m_w_ada, b_ada=m_b_ada, g_pre_mix=m_g_pre_mix, g_post_mix=m_g_post_mix, g_pre_ffn=m_g_pre_ffn,
               g_post_ffn=m_g_post_ffn, w_in=m_w_in, b_forget=m_b_forget, g_q_lora=m_g_q_lora, w_uq=m_w_uq,
               g_kv_lora=m_g_kv_lora, w_ukv=m_w_ukv, w_proj_fox=m_w_proj_fox, w_proj_mla=m_w_proj_mla, w_out=m_w_out,
               w_ffn_in=m_w_ffn_in, w_ffn_out=m_w_ffn_out)
    var = dict(w_ada=v_w_ada, b_ada=v_b_ada, g_pre_mix=v_g_pre_mix, g_post_mix=v_g_post_mix, g_pre_ffn=v_g_pre_ffn,
               g_post_ffn=v_g_post_ffn, w_in=v_w_in, b_forget=v_b_forget, g_q_lora=v_g_q_lora, w_uq=v_w_uq,
               g_kv_lora=v_g_kv_lora, w_ukv=v_w_ukv, w_proj_fox=v_w_proj_fox, w_proj_mla=v_w_proj_mla, w_out=v_w_out,
               w_ffn_in=v_w_ffn_in, w_ffn_out=v_w_ffn_out)
    me = _flat(*_coords())
    slot = jnp.reshape(me, (1,)).astype(jnp.int32)

    own = {n: prm[n][0].astype(BF16) for n in MATRICES}
    no_dep = jnp.zeros((8, LANES), F32)
    (st_c, st_in), tok = _async_start([[c], [own["w_in"]]], ["gather", "spread"], no_dep, "gather_in_start")
    (c_own,), (c_land,) = _async_wait(st_c, tok, "gather_c_wait")
    c_all = _with_own(c_land, c_own, me).reshape(N_DEV, D)
    ada_cols = w_ada.shape[2]
    b_cols = lax.dynamic_slice(b_ada, (0, me * ada_cols), (1, ada_cols))
    mod_cols, silu_c = _mod_part(c_all, w_ada[0], b_cols)
    (mod_all,) = _all_gather([mod_cols], "gather_mod")

    (w_in_own,), (w_in_land,) = _async_wait(st_in, mod_all, "gather_in_wait")
    (st_in,), tok = _async_start([[w_in_land]], "forward", no_dep, "gather_in_forward")
    _, (w_in_land,) = _async_wait(st_in, tok, "gather_in_forward_wait")
    w = _prepare_weights({"w_in": w_in_land}, {"w_in": w_in_own}, slot)
    later = dict(lora=("w_uq", "w_ukv"), proj=("w_proj_fox", "w_proj_mla", "w_out"), ffn=("w_ffn_in", "w_ffn_out"))
    states, tok = _async_start([[own[n] for n in names] for names in later.values()], ["gather", "spread", "spread"],
                               w["w_b"], "gather_rest_start")
    gather_state = dict(zip(later, states))
    own_thru = {}

    def wts(group, after):
        if group.startswith("relay_"):
            name = group[len("relay_"):]
            own_thru[name], lands = _async_wait(gather_state[name], after, "gather_" + name + "_wait")
            (gather_state[name],), t = _async_start([lands], "forward", no_dep, "gather_" + name + "_forward")
            return {"tok": t}
        srcs, lands = _async_wait(gather_state[group], after, "gather_" + group + "_landed")
        srcs = own_thru.get(group, srcs)
        return _prepare_weights(dict(zip(later[group], lands)), dict(zip(later[group], srcs)), slot)

    sent, late_sent, last = [], [], {}

    def send(grads, final=False, late=False):
        shards = _shard_grads(grads)
        names = list(shards)
        (state,), t = _async_start([[shards[n] for n in names]], "pair" if final else "exchange", no_dep,
                                   "exchange_" + names[0] + "_start")
        if final:
            last.update(names=names, state=state)
        else:
            (late_sent if late else sent).append((names, state))
        return t

    def relay(after):
        srcs, lands = _async_wait(last["state"], after, "exchange_pair_wait")
        core = jnp.reshape(lax.axis_index("c"), (1,)).astype(jnp.int32)
        sums = [_add_sibling(src, land, core) for src, land in zip(srcs, lands)]
        (last["state"],), t = _async_start([sums], "chips", no_dep, "exchange_chips_start")
        return t

    mod = lax.dynamic_index_in_dim(mod_all, me, axis=1, keepdims=False).reshape(1, 6 * D) + tok[0, 0]

    vec = dict(g_pre_mix=g_pre_mix, g_post_mix=g_post_mix, g_pre_ffn=g_pre_ffn, g_post_ffn=g_post_ffn,
               g_q_lora=g_q_lora, g_kv_lora=g_kv_lora, b_forget=b_forget)
    pos = positions.astype(F32).reshape(S, 1)
    grad_x, small = _fwd_bwd(x[0], pos, mod, loss_target[0], w, vec, wts, send, relay)

    bundle = jnp.concatenate([small[n] for n in SMALL_ORDER], axis=1)
    (small_state,), tok = _async_start([[bundle]], "gather", jnp.zeros((8, LANES), F32), "gather_small_start")

    out = {}
    swap = lambda a: jnp.swapaxes(a, -1, -2)

    def update(n, land, src, sl):
        if n != "w_ffn_in":
            out[n] = _adamw(prm[n][0], mom[n][0], var[n][0], land, "adamw_" + n, src, sl)
            return out[n][0]
        res = _adamw(swap(prm[n][0]), swap(mom[n][0]), swap(var[n][0]), land, "adamw_" + n, src, sl)
        out[n] = tuple(swap(t) for t in res)
        return res[0]

    after = tok
    for names, state in sent:
        srcs, lands = _async_wait(state, after, "exchange_" + names[0] + "_wait")
        for n, src, land in zip(names, srcs, lands):
            after = update(n, land, src, slot)
    srcs, lands = _async_wait(last["state"], after, "exchange_chips_wait")
    for n, src, land in zip(last["names"], srcs, lands):
        after = update(n, land, src, slot // 2)
    for names, state in late_sent:
        srcs, lands = _async_wait(state, after, "exchange_" + names[0] + "_wait")
        for n, src, land in zip(names, srcs, lands):
            after = update(n, land, src, slot)

    (own_bundle,), (bundle_all,) = _async_wait(small_state, after, "gather_small_wait")
    bundle_all = _with_own(bundle_all, own_bundle, me)
    dmod_all = bundle_all[:, 0, :6 * D]
    dm_cols = lax.dynamic_slice(dmod_all, (0, me * ada_cols), (N_DEV, ada_cols))
    g_ada = _w_ada_grad(jnp.transpose(silu_c), dm_cols)
    out["w_ada"] = _adamw(w_ada[0], m_w_ada[0], v_w_ada[0], g_ada[None], "adamw_w_ada")

    offsets, off = {}, 0
    for n in SMALL_ORDER:
        offsets[n] = off
        off += small[n].shape[1]
    names = [SMALL_PARAM.get(n, n) for n in SMALL_ORDER if n != "err"]
    results, err = _adamw_rows(bundle_all, [offsets[n] for n in SMALL_ORDER if n != "err"],
                               [prm[n] for n in names], [mom[n] for n in names], [var[n] for n in names],
                               offsets["err"], D)
    out.update(zip(names, results))
    loss = 0.5 * jnp.sum(err) / D

    res = [loss, grad_x[None]]
    for kind in range(4):
        for n in WEIGHTS:
            t = out[n][kind]
            res.append(t[None] if prm[n].ndim == 3 else t)
    return tuple(res)
```

```python
import functools
import math

import jax
import jax.numpy as jnp
from jax import lax
from jax.experimental import pallas as pl
from jax.experimental.pallas import tpu as pltpu

F32 = jnp.float32
BF16 = jnp.bfloat16

N_DEV = 8
S = 2048
D = 1024
D_FF = 2816
HEADS = 8
HEAD_DIM = 64
Q_LORA = 768
KV_LORA = 256
ROPE_DIM = 32
ROPE_THETA = 10000.0
NORM_EPS = 1e-6
LANES = 128
VMEM_LIMIT = 56 * 1024 * 1024

ADAM_LR = 0.001
ADAM_B1 = 0.9
ADAM_B2 = 0.999
ADAM_EPS = 1e-08
ADAM_WD = 0.01
ADAM_STEP = 10

ATT_T = 256
LOG2E = 1.4426950408889634
N_ATT = S // ATT_T

NN = (((1,), (0,)), ((), ()))
NT = (((1,), (1,)), ((), ()))
TN = (((0,), (0,)), ((), ()))
MESH = pl.DeviceIdType.MESH


def _params(sem=None):
    return pltpu.CompilerParams(dimension_semantics=sem, vmem_limit_bytes=VMEM_LIMIT)


def _pick(n, cap):
    best = None
    for t in range(LANES, cap + 1, LANES):
        if n % t == 0:
            best = t
    return best if best is not None else n


def _mm(a, b, mode, out_dtype, name, acc=None, dep=None):
    if mode == "nn":
        (m, k), (k2, n), dn = a.shape, b.shape, NN
    elif mode == "nt":
        (m, k), (n, k2), dn = a.shape, b.shape, NT
    else:
        (k, m), (k2, n), dn = a.shape, b.shape, TN
    assert k == k2, (a.shape, b.shape, mode)
    tn = _pick(n, 640)
    tm = _pick(m, 1536)
    osz = jnp.dtype(out_dtype).itemsize

    def need(tm_):
        blk = tm_ * k * 2 + tn * k * 2 + tm_ * tn * osz + (tm_ * tn * 4 if acc is not None else 0)
        return 2 * blk + tm_ * tn * 4
    while need(tm) > 36 * 1024 * 1024 and tm % 256 == 0:
        tm //= 2

    def body(*refs):
        a_ref, b_ref, o_ref = refs[0], refs[1], refs[-1]
        r = lax.dot_general(a_ref[...], b_ref[...], dn, preferred_element_type=F32)
        if acc is not None:
            r = r + refs[2][...]
        o_ref[...] = r.astype(o_ref.dtype)

    if mode == "tn":
        a_spec = pl.BlockSpec((k, tm), lambda i, j: (0, i))
    else:
        a_spec = pl.BlockSpec((tm, k), lambda i, j: (i, 0))
    if mode == "nt":
        b_spec = pl.BlockSpec((tn, k), lambda i, j: (j, 0))
    else:
        b_spec = pl.BlockSpec((k, tn), lambda i, j: (0, j))
    o_spec = pl.BlockSpec((tm, tn), lambda i, j: (i, j))
    in_specs = [a_spec, b_spec] + ([o_spec] if acc is not None else [])
    in_specs += [pl.BlockSpec(memory_space=pl.ANY)] if dep is not None else []
    args = (a, b) + ((acc,) if acc is not None else ()) + ((dep,) if dep is not None else ())
    return pl.pallas_call(
        body, name=name, grid=(m // tm, n // tn),
        in_specs=in_specs, out_specs=o_spec,
        out_shape=jax.ShapeDtypeStruct((m, n), out_dtype),
        compiler_params=_params(("parallel", "parallel")),
    )(*args)


def _mm_epi(a, b, mode, tnb, epi, name, tm, rows=(), vecs=(), outs=(), sums=(), pro=None):
    m = a.shape[0]
    k, nb = (b.shape if mode == "nn" else b.shape[::-1])
    dn = NN if mode == "nn" else NT
    pro_fn, pro_vecs, a_off = pro if pro is not None else (None, (), 0)
    n_in = 2 + len(rows) + len(vecs)
    n_all = n_in + len(pro_vecs)
    sub = min(tm, 256)

    def body(*refs):
        if pro is not None:
            a_out, a_scr = refs[-2:]
            refs = refs[:-2]

            @pl.when(pl.program_id(1) == 0)
            def _():
                a_scr[...] = pro_fn(refs[0][...], *[x[...] for x in refs[n_in:n_all]]).astype(BF16)
                a_out[...] = a_scr[...]
            a_ref = a_scr
        else:
            a_ref = refs[0]
        o_refs = refs[n_all:n_all + len(outs)]
        s_refs = refs[n_all + len(outs):]
        if sums:
            @pl.when((pl.program_id(0) == 0) & (pl.program_id(1) == 0))
            def _():
                for s_ref in s_refs:
                    s_ref[...] = jnp.zeros(s_ref.shape, F32)
        for c in range(tm // sub):
            rs = slice(c * sub, (c + 1) * sub)
            r = lax.dot_general(a_ref[rs, :], refs[1][...], dn, preferred_element_type=F32)
            o_vals, s_vals = epi(r, *[x[rs, :] for x in refs[2:2 + len(rows)]], *[x[...] for x in refs[2 + len(rows):n_in]])
            assert len(o_vals) == len(o_refs) and len(s_vals) == len(s_refs)
            for o_ref, val in zip(o_refs, o_vals):
                o_ref[rs, :] = val.astype(o_ref.dtype)
            for s_ref, val in zip(s_refs, s_vals):
                s_ref[...] += val

    once = dict(pipeline_mode=pl.Buffered(1)) if nb == tnb else {}
    if mode == "nn":
        b_spec = pl.BlockSpec((k, tnb), lambda i, j: (0, j), **once)
    else:
        b_spec = pl.BlockSpec((tnb, k), lambda i, j: (j, 0), **once)
    in_specs = [pl.BlockSpec((tm, k), lambda i, j: (i, a_off)), b_spec]
    rows = [tuple(r) + (0,) * (3 - len(r)) for r in rows]
    in_specs += [pl.BlockSpec((tm, w), functools.partial(lambda i, j, off: (i, j + off), off=off)) for _, w, off in rows]
    in_specs += [pl.BlockSpec(v.shape, lambda i, j: (0, 0)) for v in list(vecs) + list(pro_vecs)]
    out_specs = [pl.BlockSpec((tm, w), lambda i, j: (i, j)) for _, w, _ in outs]
    out_specs += [pl.BlockSpec((1, w), lambda i, j: (0, 0)) for w in sums]
    out_shape = [jax.ShapeDtypeStruct((m, full), dt) for full, _, dt in outs]
    out_shape += [jax.ShapeDtypeStruct((1, w), F32) for w in sums]
    if pro is not None:
        out_specs.append(pl.BlockSpec((tm, k), lambda i, j: (i, 0)))
        out_shape.append(jax.ShapeDtypeStruct((m, k), BF16))
    return pl.pallas_call(
        body, name=name, grid=(m // tm, nb // tnb),
        in_specs=in_specs, out_specs=out_specs, out_shape=out_shape,
        scratch_shapes=[pltpu.VMEM((tm, k), BF16)] if pro is not None else [],
        compiler_params=_params(("arbitrary", "arbitrary") if sums else ("parallel", "arbitrary" if pro is not None else "parallel")),
    )(a, b, *[r[0] for r in rows], *vecs, *pro_vecs)


def _rowwise(fn, row_ins, vec_ins, row_outs, sum_outs, name, tm=256):
    n_in = len(row_ins) + len(vec_ins)
    n_o = len(row_outs)
    rows = row_ins[0][0].shape[0]

    def body(*refs):
        vals = [r[...] for r in refs[:n_in]]
        outs = refs[n_in:]
        ro, so = fn(*vals)
        assert len(ro) == n_o and len(so) == len(sum_outs)
        for r, v in zip(outs[:n_o], ro):
            r[...] = v.astype(r.dtype)
        if sum_outs:
            @pl.when(pl.program_id(0) == 0)
            def _():
                for r in outs[n_o:]:
                    r[...] = jnp.zeros(r.shape, F32)
            for r, v in zip(outs[n_o:], so):
                r[...] += v

    in_specs = [pl.BlockSpec((tm, w), functools.partial(lambda i, b: (i, b), b=b)) for _, w, b in row_ins]
    in_specs += [pl.BlockSpec(v.shape, lambda i: (0, 0)) for v in vec_ins]
    out_specs = [pl.BlockSpec((tm, w), lambda i: (i, 0)) for w, _ in row_outs]
    out_specs += [pl.BlockSpec((1, w), lambda i: (0, 0)) for w in sum_outs]
    out_shape = [jax.ShapeDtypeStruct((rows, w), dt) for w, dt in row_outs]
    out_shape += [jax.ShapeDtypeStruct((1, w), F32) for w in sum_outs]
    return pl.pallas_call(
        body, name=name, grid=(rows // tm,),
        in_specs=in_specs, out_specs=out_specs, out_shape=out_shape,
        compiler_params=_params(("arbitrary",)),
    )(*[a for a, _, _ in row_ins], *vec_ins)


def _sigmoid(x):
    return 1.0 / (1.0 + jnp.exp(-x))


def _rstd(x):
    return lax.rsqrt(jnp.mean(x * x, axis=-1, keepdims=True) + NORM_EPS)


def _norm_bwd(dyn, xn, r):
    return r * (dyn - xn * jnp.mean(dyn * xn, axis=-1, keepdims=True))


def _colsum(x):
    return jnp.sum(x, axis=0, keepdims=True)


def _rope_tables(pos, invf):
    def fn(p, f):
        lane = lax.broadcasted_iota(jnp.int32, (1, LANES), 1)
        ang = p * f
        cs, sn = jnp.cos(ang), jnp.sin(ang)
        rot = (lane >= 64) & (lane < 96)
        ct = jnp.where(lane < 64, 1.0, jnp.where(rot, cs, 0.0))
        sa = jnp.where((lane >= 64) & (lane < 80), -sn, 0.0)
        sb = jnp.where((lane >= 80) & (lane < 96), sn, 0.0)
        return (ct, sa, sb), ()
    return _rowwise(fn, [(pos, 1, 0)], [invf], [(LANES, F32)] * 3, [], "rope_tables")


def _rope(x, ct, sa, sb):
    return x * ct + pltpu.roll(x, LANES - 16, 1) * sa + pltpu.roll(x, 16, 1) * sb


def _rope_t(x, ct, sa, sb):
    return x * ct - pltpu.roll(x, LANES - 16, 1) * sa - pltpu.roll(x, 16, 1) * sb


def _head_mask(width, hh):
    lane = lax.broadcasted_iota(jnp.int32, (1, width), 1)
    half = width // 2
    return (lane >= hh * half) & (lane < (hh + 1) * half)


ATT_PP = 2
ATT_CHAINS = [(a, hh) for a in range(ATT_PP) for hh in range(2)]
ATT_G = HEADS // (2 * ATT_PP)


def _pair(ref_or_val, a, width, rows=slice(None)):
    return ref_or_val[rows, a * width:(a + 1) * width]


def _attn_fwd(q, qo, k, ko, v, vo, dkp, scale, bias, name):
    T = ATT_T
    assert qo % ATT_PP == 0 and ko % ATT_PP == 0 and vo % ATT_PP == 0
    qo, ko, vo = qo // ATT_PP, ko // ATT_PP, vo // ATT_PP

    def body(*refs):
        if bias is not None:
            q_ref, k_ref, v_ref, b_ref, o_ref, lse_ref, s_scr = refs
        else:
            q_ref, k_ref, v_ref, o_ref, lse_ref, s_scr = refs
        i = pl.program_id(1)
        row = lax.broadcasted_iota(jnp.int32, (T, T), 0)
        col = lax.broadcasted_iota(jnp.int32, (T, T), 1)
        qms = []
        for a, hh in ATT_CHAINS:
            qb = _pair(q_ref, a, dkp)
            qms.append(jnp.where(_head_mask(dkp, hh), qb, jnp.zeros_like(qb)))

        def fold(t):
            return [t[:, c * LANES:(c + 1) * LANES] for c in range(T // LANES)]

        def run(nt):
            mls = [jnp.full((T, LANES), -jnp.inf, F32) for _ in ATT_CHAINS]
            for j in range(nt):
                ks = slice(j * T, (j + 1) * T)
                for ci, (a, hh) in enumerate(ATT_CHAINS):
                    s = lax.dot_general(qms[ci], _pair(k_ref, a, dkp, ks), NT, preferred_element_type=F32) * (scale * LOG2E)
                    if bias is not None:
                        s = s + b_ref[2 * a + hh, j] * LOG2E
                    if j == nt - 1:
                        s = jnp.where(row >= col, s, -jnp.inf)
                    s_scr[ci, j] = s
                    for part in fold(s):
                        mls[ci] = jnp.maximum(mls[ci], part)
            ms = [jnp.max(ml, axis=1, keepdims=True) for ml in mls]
            mbs = [jnp.broadcast_to(m, (T, LANES)) for m in ms]
            for a in range(ATT_PP):
                ls = [jnp.zeros((T, LANES), F32) for _ in range(2)]
                ps, vms = [], []
                for j in range(nt):
                    vb = _pair(v_ref, a, LANES, slice(j * T, (j + 1) * T))
                    for hh in range(2):
                        parts = [jnp.exp2(part - mbs[2 * a + hh]) for part in fold(s_scr[2 * a + hh, j])]
                        for part in parts:
                            ls[hh] = ls[hh] + part
                        ps.append(jnp.concatenate(parts, axis=1).astype(BF16))
                        vms.append(jnp.where(_head_mask(LANES, hh), vb, jnp.zeros_like(vb)))
                acc = lax.dot_general(jnp.concatenate(ps, axis=1), jnp.concatenate(vms, axis=0), NN,
                                      preferred_element_type=F32)
                l0, l1 = [jnp.sum(l, axis=1, keepdims=True) for l in ls]
                lse_ref[2 * a] = ms[2 * a] + jnp.log2(l0)
                lse_ref[2 * a + 1] = ms[2 * a + 1] + jnp.log2(l1)
                inv = jnp.where(_head_mask(LANES, 0), 1.0 / l0, 1.0 / l1)
                o_ref[:, a * LANES:(a + 1) * LANES] = (acc * inv).astype(o_ref.dtype)

        for nt in range(1, N_ATT + 1):
            pl.when(i == nt - 1)(functools.partial(run, nt))

    in_specs = [
        pl.BlockSpec((T, ATT_PP * dkp), lambda g, i: (i, qo + g)),
        pl.BlockSpec((S, ATT_PP * dkp), lambda g, i: (0, ko + g)),
        pl.BlockSpec((S, ATT_PP * LANES), lambda g, i: (0, vo + g)),
    ]
    args = [q, k, v]
    if bias is not None:
        in_specs.append(pl.BlockSpec((2 * ATT_PP, N_ATT, 1, T), lambda g, i: (g, 0, 0, 0)))
        args.append(bias)
    return pl.pallas_call(
        body, name=name, grid=(ATT_G, N_ATT),
        in_specs=in_specs,
        out_specs=[pl.BlockSpec((T, ATT_PP * LANES), lambda g, i: (i, g)),
                   pl.BlockSpec((2 * ATT_PP, T, 1), lambda g, i: (g, i, 0))],
        out_shape=[jax.ShapeDtypeStruct((S, HEADS * HEAD_DIM), BF16),
                   jax.ShapeDtypeStruct((HEADS, S, 1), F32)],
        scratch_shapes=[pltpu.VMEM((len(ATT_CHAINS), N_ATT, T, T), F32)],
        compiler_params=_params(("parallel", "arbitrary")),
    )(*args)


def _attn_grad(q, qo, k, ko, v, vo, do, lse, dkp, scale, bias, qk_dtype, name):
    T = ATT_T
    has_b = bias is not None
    qo, ko, vo = qo // ATT_PP, ko // ATT_PP, vo // ATT_PP
    n_ch = len(ATT_CHAINS)

    def body(*refs):
        q_ref, k_ref, v_ref, do_ref, lse_ref = refs[:5]
        refs = refs[5:]
        if has_b:
            b_ref, refs = refs[0], refs[1:]
        dq_ref, dk_ref, dv_ref = refs[:3]
        refs = refs[3:]
        if has_b:
            db_ref, refs = refs[0], refs[1:]
        p_scr, dp_scr, dk_acc, dv_acc = refs[:4]
        db_acc = refs[4] if has_b else None
        i = pl.program_id(1)

        @pl.when(i == 0)
        def _():
            dk_acc[...] = jnp.zeros(dk_acc.shape, F32)
            dv_acc[...] = jnp.zeros(dv_acc.shape, F32)
            if has_b:
                db_acc[...] = jnp.zeros(db_acc.shape, F32)

        row = lax.broadcasted_iota(jnp.int32, (T, T), 0)
        col = lax.broadcasted_iota(jnp.int32, (T, T), 1)

        def fold(t):
            return [t[:, c * LANES:(c + 1) * LANES] for c in range(T // LANES)]

        qms, doms, lses = [], [], []
        for a, hh in ATT_CHAINS:
            qb, dob = _pair(q_ref, a, dkp), _pair(do_ref, a, LANES)
            qms.append(jnp.where(_head_mask(dkp, hh), qb, jnp.zeros_like(qb)))
            doms.append(jnp.where(_head_mask(LANES, hh), dob, jnp.zeros_like(dob)))
            lses.append(lse_ref[2 * a + hh])

        def run(nt):
            dls = [jnp.zeros((T, LANES), F32) for _ in ATT_CHAINS]
            for j in range(nt):
                ks = slice(j * T, (j + 1) * T)
                for ci, (a, hh) in enumerate(ATT_CHAINS):
                    s = lax.dot_general(qms[ci], _pair(k_ref, a, dkp, ks), NT, preferred_element_type=F32) * (scale * LOG2E)
                    if has_b:
                        s = s + b_ref[ci, j] * LOG2E
                    s = s - lses[ci]
                    if j == nt - 1:
                        s = jnp.where(row >= col, s, -jnp.inf)
                    p = jnp.exp2(s)
                    dp = lax.dot_general(doms[ci], _pair(v_ref, a, LANES, ks), NT, preferred_element_type=F32)
                    p_scr[ci, j] = p
                    dp_scr[ci, j] = dp
                    for part in fold(p * dp):
                        dls[ci] = dls[ci] + part
            deltas = [jnp.broadcast_to(jnp.sum(dl, axis=1, keepdims=True), (T, LANES)) for dl in dls]
            for a in range(ATT_PP):
                ds_all, km_all = [], []
                qm2t = jnp.transpose(jnp.concatenate([qms[2 * a], qms[2 * a + 1]], axis=0))
                dom2t = jnp.transpose(jnp.concatenate([doms[2 * a], doms[2 * a + 1]], axis=0))
                for j in range(nt):
                    ks = slice(j * T, (j + 1) * T)
                    kb = _pair(k_ref, a, dkp, ks)
                    p2, ds2 = [], []
                    for hh in range(2):
                        ci = 2 * a + hh
                        p = p_scr[ci, j]
                        ds = jnp.concatenate([pp * (dd - deltas[ci]) for pp, dd in zip(fold(p), fold(dp_scr[ci, j]))], axis=1)
                        if has_b:
                            db_acc[ci, j] += jnp.sum(ds, axis=0, keepdims=True)
                        p2.append(p.astype(BF16))
                        ds2.append((ds * scale).astype(BF16))
                        km_all.append(jnp.where(_head_mask(dkp, hh), kb, jnp.zeros_like(kb)))
                    dv_acc[a * LANES:(a + 1) * LANES, ks] += lax.dot_general(
                        dom2t, jnp.concatenate(p2, axis=0), NN, preferred_element_type=F32)
                    dk_acc[a * dkp:(a + 1) * dkp, ks] += lax.dot_general(
                        qm2t, jnp.concatenate(ds2, axis=0), NN, preferred_element_type=F32)
                    ds_all += ds2
                dq = lax.dot_general(jnp.concatenate(ds_all, axis=1), jnp.concatenate(km_all, axis=0), NN,
                                     preferred_element_type=F32)
                dq_ref[:, a * dkp:(a + 1) * dkp] = dq.astype(dq_ref.dtype)

        for nt in range(1, N_ATT + 1):
            pl.when(i == nt - 1)(functools.partial(run, nt))

        @pl.when(i == N_ATT - 1)
        def _():
            dk_ref[...] = jnp.transpose(dk_acc[...]).astype(dk_ref.dtype)
            dv_ref[...] = jnp.transpose(dv_acc[...]).astype(dv_ref.dtype)
            if has_b:
                db_ref[...] = db_acc[...]

    in_specs = [
        pl.BlockSpec((T, ATT_PP * dkp), lambda g, i: (i, qo + g)),
        pl.BlockSpec((S, ATT_PP * dkp), lambda g, i: (0, ko + g)),
        pl.BlockSpec((S, ATT_PP * LANES), lambda g, i: (0, vo + g)),
        pl.BlockSpec((T, ATT_PP * LANES), lambda g, i: (i, g)),
        pl.BlockSpec((2 * ATT_PP, T, 1), lambda g, i: (g, i, 0)),
    ]
    args = [q, k, v, do, lse]
    out_specs = [
        pl.BlockSpec((T, ATT_PP * dkp), lambda g, i: (i, g)),
        pl.BlockSpec((S, ATT_PP * dkp), lambda g, i: (0, g)),
        pl.BlockSpec((S, ATT_PP * LANES), lambda g, i: (0, g)),
    ]
    width = (HEADS // 2) * dkp
    out_shape = [
        jax.ShapeDtypeStruct((S, width), qk_dtype),
        jax.ShapeDtypeStruct((S, width), qk_dtype),
        jax.ShapeDtypeStruct((S, HEADS * HEAD_DIM), BF16),
    ]
    scratch = [pltpu.VMEM((n_ch, N_ATT, T, T), F32), pltpu.VMEM((n_ch, N_ATT, T, T), F32),
               pltpu.VMEM((ATT_PP * dkp, S), F32), pltpu.VMEM((ATT_PP * LANES, S), F32)]
    if has_b:
        bspec = pl.BlockSpec((2 * ATT_PP, N_ATT, 1, T), lambda g, i: (g, 0, 0, 0))
        in_specs.append(bspec)
        args.append(bias)
        out_specs.append(bspec)
        out_shape.append(jax.ShapeDtypeStruct((HEADS, N_ATT, 1, T), F32))
        scratch.append(pltpu.VMEM((2 * ATT_PP, N_ATT, 1, T), F32))
    return pl.pallas_call(
        body, name=name, grid=(ATT_G, N_ATT),
        in_specs=in_specs, out_specs=out_specs, out_shape=out_shape, scratch_shapes=scratch,
        compiler_params=_params(("parallel", "arbitrary")),
    )(*args)


def _tri(upper):
    a = lax.broadcasted_iota(jnp.int32, (LANES, LANES), 0)
    b = lax.broadcasted_iota(jnp.int32, (LANES, LANES), 1)
    return jnp.where(a <= b if upper else a >= b, 1.0, 0.0).astype(F32)


def _fox_gates(proj, blk, bf):
    def body(m_ref, b_ref, z_out, o_ref):
        tri = _tri(True)
        carry = jnp.zeros((HEADS, 1), F32)
        for t in range(S // LANES):
            sl = slice(t * LANES, (t + 1) * LANES)
            zt = jnp.transpose(m_ref[sl, :])[:HEADS]
            z_out[:, sl] = zt
            z = zt + b_ref[...]
            logf = jnp.minimum(z, 0.0) - jnp.log(1.0 + jnp.exp(-jnp.abs(z)))
            c = lax.dot_general(logf, tri, NN, preferred_element_type=F32,
                                precision=lax.Precision.HIGHEST) + carry
            o_ref[:, sl] = -c
            carry = c[:, LANES - 1:LANES]

    return pl.pallas_call(
        body, name="fox_gates", grid=(1,),
        in_specs=[pl.BlockSpec((S, LANES), lambda i: (0, blk)), pl.BlockSpec(bf.shape, lambda i: (0, 0))],
        out_specs=[pl.BlockSpec((HEADS, S), lambda i: (0, 0))] * 2,
        out_shape=[jax.ShapeDtypeStruct((HEADS, S), F32)] * 2,
        compiler_params=_params(("arbitrary",)),
    )(proj, bf)


def _fox_gates_bwd(dbias, zt, bf):
    def body(d_ref, z_ref, b_ref, dz_ref, dbf_ref):
        tri = _tri(False)
        carry = jnp.zeros((HEADS, 1), F32)
        tot = jnp.zeros((HEADS, 1), F32)
        for t in reversed(range(S // LANES)):
            sl = slice(t * LANES, (t + 1) * LANES)
            df = -d_ref[:, sl]
            c = lax.dot_general(df, tri, NN, preferred_element_type=F32,
                                precision=lax.Precision.HIGHEST) + carry
            carry = c[:, 0:1]
            z = z_ref[:, sl] + b_ref[...]
            dz = c * _sigmoid(-z)
            dz_ref[:, sl] = dz
            tot = tot + jnp.sum(dz, axis=1, keepdims=True)
        dbf_ref[...] = tot

    return pl.pallas_call(
        body, name="fox_gates_bwd",
        out_shape=[jax.ShapeDtypeStruct((HEADS, S), F32), jax.ShapeDtypeStruct((HEADS, 1), F32)],
        compiler_params=_params(),
    )(dbias, zt, bf)


def _mod_part(c_all, w_ada, b_cols):
    def body(c_ref, w_ref, b_ref, o_ref, s_ref):
        c = c_ref[...]
        sc = c * _sigmoid(c)
        s_ref[...] = sc
        o_ref[...] = lax.dot_general(sc, w_ref[...], NN, preferred_element_type=F32,
                                     precision=lax.Precision.HIGHEST) + b_ref[...]

    return pl.pallas_call(
        body, name="mod_part",
        out_shape=[jax.ShapeDtypeStruct((N_DEV, w_ada.shape[1]), F32), jax.ShapeDtypeStruct(c_all.shape, F32)],
        compiler_params=_params(),
    )(c_all, w_ada, b_cols)


def _adamw_w_ada(w, m, v, sc_t, dm):
    rows, cols = w.shape
    tr = 256

    def body(w_ref, m_ref, v_ref, s_ref, d_ref, g_out, d_out, m_out, v_out):
        g = s_ref[:, 0:1] * d_ref[0:1, :]
        for b in range(1, N_DEV):
            g = g + s_ref[:, b:b + 1] * d_ref[b:b + 1, :]
        g_out[...] = g
        d_out[...], m_out[...], v_out[...] = _adamw_math(w_ref[...], g, m_ref[...], v_ref[...])

    spec = pl.BlockSpec((tr, cols), lambda i: (i, 0))
    return pl.pallas_call(
        body, name="adamw_w_ada", grid=(rows // tr,),
        in_specs=[spec, spec, spec, pl.BlockSpec((tr, N_DEV), lambda i: (i, 0)), pl.BlockSpec(dm.shape, lambda i: (0, 0))],
        out_specs=[spec] * 4, out_shape=[jax.ShapeDtypeStruct((rows, cols), F32)] * 4,
        compiler_params=_params(("parallel",)),
    )(w, m, v, sc_t, dm)


def _adamw(w, m, v, parts, name, own=None, slot=None):
    rows, cols = w.shape
    n = parts.shape[0]
    by_cols = rows % 256 != 0 and cols % 256 == 0
    tr, tc = (rows, 256) if by_cols else ((rows if rows <= 512 else 256), cols)
    tile = (lambda i: (0, i)) if by_cols else (lambda i: (i, 0))

    def body(*refs):
        if own is not None:
            s_ref, refs = refs[0], refs[1:]
            w_ref, m_ref, v_ref, p_ref, o_ref, g_out, d_out, m_out, v_out = refs
            terms = [jnp.where(s_ref[0] == kk, o_ref[0], p_ref[kk]) for kk in range(n)]
        else:
            w_ref, m_ref, v_ref, p_ref, g_out, d_out, m_out, v_out = refs
            terms = [p_ref[kk] for kk in range(n)]
        g = terms[0].astype(F32)
        for term in terms[1:]:
            g = g + term.astype(F32)
        g_out[...] = g
        d_out[...], m_out[...], v_out[...] = _adamw_math(w_ref[...], g, m_ref[...], v_ref[...])

    spec = pl.BlockSpec((tr, tc), lambda i, *_: tile(i))
    in_specs = [spec, spec, spec, pl.BlockSpec((n, tr, tc), lambda i, *_: (0,) + tile(i))]
    out_shape = [jax.ShapeDtypeStruct((rows, cols), F32)] * 4
    grid = (rows // tr if not by_cols else cols // tc,)
    if own is None:
        return pl.pallas_call(
            body, name=name, grid=grid, in_specs=in_specs, out_specs=[spec] * 4, out_shape=out_shape,
            compiler_params=_params(("parallel",)),
        )(w, m, v, parts)
    in_specs.append(pl.BlockSpec((1, tr, tc), lambda i, s: (s[0],) + tile(i)))
    return pl.pallas_call(
        body, name=name, out_shape=out_shape, compiler_params=_params(("parallel",)),
        grid_spec=pltpu.PrefetchScalarGridSpec(num_scalar_prefetch=1, grid=grid, in_specs=in_specs,
                                               out_specs=[spec] * 4),
    )(slot, w, m, v, parts, own)


def _adamw_math(w, g, m, v):
    mm = ADAM_B1 * m + (1.0 - ADAM_B1) * g
    vv = ADAM_B2 * v + (1.0 - ADAM_B2) * (g * g)
    m_hat = mm / (1.0 - ADAM_B1 ** ADAM_STEP)
    v_hat = vv / (1.0 - ADAM_B2 ** ADAM_STEP)
    return -ADAM_LR * (m_hat / (jnp.sqrt(v_hat) + ADAM_EPS) + ADAM_WD * w), mm, vv


def _adamw_rows(bundles, offsets, ws, ms, vs, err_off, err_width):
    k = len(ws)

    def body(*refs):
        b_ref = refs[0]
        w_refs, m_refs, v_refs = refs[1:1 + k], refs[1 + k:1 + 2 * k], refs[1 + 2 * k:1 + 3 * k]
        outs = refs[1 + 3 * k:]
        g_all = b_ref[0]
        for kk in range(1, N_DEV):
            g_all = g_all + b_ref[kk]
        for i in range(k):
            width = w_refs[i].shape[1]
            g = g_all[:, offsets[i]:offsets[i] + width]
            outs[4 * i][...] = g
            outs[4 * i + 1][...], outs[4 * i + 2][...], outs[4 * i + 3][...] = _adamw_math(
                w_refs[i][...], g, m_refs[i][...], v_refs[i][...])
        outs[4 * k][...] = g_all[:, err_off:err_off + err_width]

    out_shape = []
    for w_ in ws:
        out_shape += [jax.ShapeDtypeStruct(w_.shape, F32)] * 4
    out_shape.append(jax.ShapeDtypeStruct((1, err_width), F32))
    res = pl.pallas_call(body, name="adamw_rows", out_shape=out_shape, compiler_params=_params())(bundles, *ws, *ms, *vs)
    return [tuple(res[4 * i:4 * i + 4]) for i in range(k)], res[-1]


def _coords():
    return lax.axis_index("x"), lax.axis_index("y"), lax.axis_index("c")


def _flat(px, py, pc):
    return 4 * px + 2 * py + pc


def _all_gather(arrs, name):
    n = len(arrs)

    def body(*refs):
        ins, outs = refs[:n], refs[n:2 * n]
        send, recv, lsem = refs[2 * n:]
        x, y, c = _coords()
        me, sibling = (x, y, c), (x, y, 1 - c)
        chips = [(1 - x, y), (x, 1 - y), (1 - x, 1 - y)]

        def copy(a, kk, block, to, src=None):
            slot = outs[a].at[_flat(*block)]
            return pltpu.make_async_remote_copy(
                src_ref=slot if src is None else src, dst_ref=slot,
                send_sem=send.at[a, kk], recv_sem=recv.at[a, kk],
                device_id=to, device_id_type=MESH)

        mine = [pltpu.make_async_copy(ins[a], outs[a].at[_flat(*me)], lsem.at[a]) for a in range(n)]
        for cp in mine:
            cp.start()
        first = []
        for a in range(n):
            first.append(copy(a, 0, me, sibling, src=ins[a]))
            first += [copy(a, 1 + j, me, (*chip, c), src=ins[a]) for j, chip in enumerate(chips)]
        for cp in first:
            cp.start()
        passed = []
        for j, chip in enumerate(chips):
            for a in range(n):
                copy(a, 1 + j, (*chip, c), me).wait_recv()
                cp = copy(a, 4 + j, (*chip, c), sibling)
                cp.start()
                passed.append(cp)
        for a in range(n):
            copy(a, 0, sibling, me).wait_recv()
        for j, chip in enumerate(chips):
            for a in range(n):
                copy(a, 4 + j, (*chip, 1 - c), me).wait_recv()
        for cp in first + passed:
            cp.wait_send()
        for cp in mine:
            cp.wait()

    any_spec = pl.BlockSpec(memory_space=pl.ANY)
    return pl.pallas_call(
        body, name=name,
        in_specs=[any_spec] * n, out_specs=[any_spec] * n,
        out_shape=[jax.ShapeDtypeStruct((N_DEV,) + a.shape, a.dtype) for a in arrs],
        scratch_shapes=[pltpu.SemaphoreType.DMA((n, 7)), pltpu.SemaphoreType.DMA((n, 7)),
                        pltpu.SemaphoreType.DMA((n,))],
    )(*arrs)


def _peer_list():
    x, y, c = _coords()
    return [((1 - x if r & 4 else x), (1 - y if r & 2 else y), (1 - c if r & 1 else c)) for r in range(1, N_DEV)]


def _copy_plan(mode, src, land):
    x, y, c = _coords()
    me = _flat(x, y, c)
    if mode == "gather":
        return [(src, land.at[me], peer) for peer in _peer_list()]
    if mode == "exchange":
        return [(src.at[_flat(*peer)], land.at[me], peer) for peer in _peer_list()]
    if mode == "pair":
        return [(src.at[_flat(q // 2, q % 2, 1 - c)], land.at[q], (x, y, 1 - c)) for q in range(N_DEV // 2)]
    chips = [((1 - x if r & 2 else x), (1 - y if r & 1 else y)) for r in range(1, N_DEV // 2)]
    if mode == "chips":
        return [(src.at[2 * qx + qy], land.at[2 * x + y], (qx, qy, c)) for qx, qy in chips]
    if mode == "spread":
        return [(src, land.at[me], (x, y, 1 - c))] + [(src, land.at[me], (qx, qy, c)) for qx, qy in chips]
    assert mode == "forward"
    return [(land.at[_flat(qx, qy, c)], land.at[_flat(qx, qy, c)], (x, y, 1 - c)) for qx, qy in chips]


N_COPIES = dict(gather=N_DEV - 1, exchange=N_DEV - 1, pair=N_DEV // 2, chips=N_DEV // 2 - 1, spread=N_DEV // 2,
                forward=N_DEV // 2 - 1)


def _land_shape(mode, shape):
    return {"gather": (N_DEV,) + shape, "spread": (N_DEV,) + shape, "exchange": shape,
            "pair": (N_DEV // 2,) + shape[1:], "chips": shape}[mode]


HBM_SPEC = pl.BlockSpec(memory_space=pltpu.HBM)
SEM_SPEC = pl.BlockSpec(memory_space=pltpu.SEMAPHORE)
ANY_SPEC = pl.BlockSpec(memory_space=pl.ANY)
SIDE_EFFECT = pltpu.SideEffectType.DATAFLOW_SIDE_EFFECTING


def _async_start(groups, modes, after, name):
    modes = [modes] * len(groups) if isinstance(modes, str) else list(modes)
    arrs = [(a, m) for g, m in zip(groups, modes) for a in g]
    n = len(arrs)
    fresh = [i for i, (_, m) in enumerate(arrs) if m != "forward"]

    def body(*refs):
        srcs, new_lands = refs[:n], refs[n:n + len(fresh)]
        outs = refs[n + len(fresh) + 1:]
        lands = list(srcs)
        for k, i in enumerate(fresh):
            lands[i] = new_lands[k]
        for ai, (_, mode) in enumerate(arrs):
            for src_ref, dst_ref, peer in _copy_plan(mode, srcs[ai], lands[ai]):
                pltpu.make_async_remote_copy(src_ref=src_ref, dst_ref=dst_ref, send_sem=outs[2 * ai],
                                             recv_sem=outs[2 * ai + 1], device_id=peer, device_id_type=MESH).start()
        outs[-1][...] = jnp.zeros(outs[-1].shape, F32)

    land_shapes = [(_land_shape(arrs[i][1], arrs[i][0].shape), arrs[i][0].dtype) for i in fresh]
    n_buf = n + len(fresh)
    out_shape = [pltpu.SemaphoreType.DMA(())] * (2 * n)
    out_shape += [pltpu.HBM(a.shape, a.dtype) for a, _ in arrs]
    out_shape += [pltpu.HBM(shape, dt) for shape, dt in land_shapes]
    out_shape.append(jax.ShapeDtypeStruct((8, LANES), F32))
    res = pl.pallas_call(
        body, name=name, out_shape=tuple(out_shape),
        in_specs=[HBM_SPEC] * n_buf + [ANY_SPEC],
        out_specs=tuple([SEM_SPEC] * (2 * n) + [HBM_SPEC] * n_buf + [pl.BlockSpec(memory_space=pltpu.VMEM)]),
        input_output_aliases={i: 2 * n + i for i in range(n_buf)},
        compiler_params=pltpu.CompilerParams(has_side_effects=SIDE_EFFECT),
    )(*[pltpu.with_memory_space_constraint(a, pltpu.HBM) for a, _ in arrs],
      *[pltpu.with_memory_space_constraint(lax.empty(shape, dt), pltpu.HBM) for shape, dt in land_shapes],
      after)
    sems, thru = res[:2 * n], res[2 * n:-1]
    land_of = {i: thru[n + k] for k, i in enumerate(fresh)}
    states, idx = [], 0
    for g, mode in zip(groups, modes):
        ids = range(idx, idx + len(g))
        idx += len(g)
        states.append(([sems[2 * i] for i in ids], [sems[2 * i + 1] for i in ids],
                       None if mode == "forward" else [thru[i] for i in ids],
                       [land_of.get(i, thru[i]) for i in ids], mode))
    return states, res[-1]


def _async_wait(state, after, name):
    sends, recvs, srcs, lands, mode = state
    g = len(lands)
    bufs = (list(srcs) if srcs is not None else []) + list(lands)
    nb = len(bufs)

    def body(*refs):
        l_refs, sems = refs[nb - g:nb], refs[nb:nb + 2 * g]
        for ai in range(g):
            moved = l_refs[ai].at[pl.ds(0, N_COPIES[mode])]
            cp = pltpu.make_async_remote_copy(src_ref=moved, dst_ref=moved, send_sem=sems[ai], recv_sem=sems[g + ai],
                                              device_id=_coords(), device_id_type=MESH)
            cp.wait_send()
            cp.wait_recv()

    res = pl.pallas_call(
        body, name=name,
        out_shape=tuple(pltpu.HBM(a.shape, a.dtype) for a in bufs),
        in_specs=[HBM_SPEC] * nb + [SEM_SPEC] * (2 * g) + [ANY_SPEC],
        out_specs=tuple([HBM_SPEC] * nb),
        input_output_aliases={i: i for i in range(nb)},
        compiler_params=pltpu.CompilerParams(has_side_effects=SIDE_EFFECT),
    )(*bufs, *sends, *recvs, after)
    return (list(res[:nb - g]) if srcs is not None else None), list(res[nb - g:])


def _add_sibling(mine, theirs, core):
    def body(c_ref, a_ref, b_ref, o_ref):
        o_ref[...] = (a_ref[...].astype(F32) + b_ref[...].astype(F32)).astype(o_ref.dtype)

    blk = (1,) + mine.shape[1:]
    return pl.pallas_call(
        body, name="add_sibling", out_shape=jax.ShapeDtypeStruct(theirs.shape, mine.dtype),
        grid_spec=pltpu.PrefetchScalarGridSpec(
            num_scalar_prefetch=1, grid=(theirs.shape[0],),
            in_specs=[pl.BlockSpec(blk, lambda q, c: (2 * q + c[0], 0, 0)), pl.BlockSpec(blk, lambda q, c: (q, 0, 0))],
            out_specs=pl.BlockSpec(blk, lambda q, c: (q, 0, 0))),
        compiler_params=_params(("parallel",)),
    )(core, mine, theirs)


def _with_own(land, own, me):
    return lax.dynamic_update_index_in_dim(land, own, me, 0)


IN_SPLITS = (512, 512, 512, 8, 768, 256, 32, 1024, 1024)


def _from_shards(g, fn, out_widths, name, own=None, slot=None):
    _, k, n = g.shape
    tr = min(k, 256)

    def body(*refs):
        if own is not None:
            s_ref, g_ref, own_ref = refs[:3]
            cols = [jnp.where(s_ref[0] == j, own_ref[...], g_ref[j]) for j in range(N_DEV)]
        else:
            g_ref = refs[0]
            cols = [g_ref[j] for j in range(N_DEV)]
        for o_ref, val in zip(refs[-len(out_widths):], fn(jnp.concatenate(cols, axis=1))):
            o_ref[...] = val

    in_specs = [pl.BlockSpec((N_DEV, tr, n), lambda i, *_: (0, i, 0))]
    out_spec = [pl.BlockSpec((tr, wd), lambda i, *_: (i, 0)) for wd in out_widths]
    out_shape = [jax.ShapeDtypeStruct((k, wd), g.dtype) for wd in out_widths]
    if own is None:
        return pl.pallas_call(body, name=name, grid=(k // tr,), in_specs=in_specs, out_specs=out_spec,
                              out_shape=out_shape, compiler_params=_params(("parallel",)))(g)
    in_specs.append(pl.BlockSpec((tr, n), lambda i, *_: (i, 0)))
    return pl.pallas_call(
        body, name=name, out_shape=out_shape, compiler_params=_params(("parallel",)),
        grid_spec=pltpu.PrefetchScalarGridSpec(num_scalar_prefetch=1, grid=(k // tr,), in_specs=in_specs, out_specs=out_spec),
    )(slot, g, own)


def _unshard_cols(g, own=None, slot=None):
    return _from_shards(g, lambda full: (full,), [N_DEV * g.shape[2]], "unshard_cols_%d" % g.shape[2], own, slot)[0]


FFN_T = 256
FFN_SHARD = 2 * D_FF // N_DEV


def _unshard_ffn_in(g, own=None, slot=None):
    def pairs(full):
        parts = []
        for j in range(D_FF // FFN_T):
            parts += [full[:, j * FFN_T:(j + 1) * FFN_T], full[:, D_FF + j * FFN_T:D_FF + (j + 1) * FFN_T]]
        return (jnp.concatenate(parts, axis=1),)

    return _from_shards(g, pairs, [2 * D_FF], "unshard_ffn_in", own, slot)[0]


def _shard_ffn_in_t(wt):
    tc = 256

    def body(w_ref, o_ref):
        x = w_ref[...]
        nb = D_FF // FFN_T
        full = jnp.concatenate([x[(2 * j + half) * FFN_T:(2 * j + half + 1) * FFN_T]
                                for half in range(2) for j in range(nb)], axis=0)
        for j in range(N_DEV):
            o_ref[j] = full[j * FFN_SHARD:(j + 1) * FFN_SHARD]

    return pl.pallas_call(
        body, name="shard_ffn_in_t", grid=(D // tc,),
        in_specs=[pl.BlockSpec((2 * D_FF, tc), lambda i: (0, i))],
        out_specs=pl.BlockSpec((N_DEV, FFN_SHARD, tc), lambda i: (0, 0, i)),
        out_shape=jax.ShapeDtypeStruct((N_DEV, FFN_SHARD, D), wt.dtype),
        compiler_params=_params(("parallel",)),
    )(wt)


def _shard_cols(w):
    k, n = w.shape[0], w.shape[1] // N_DEV
    tr = min(k, 256)

    def body(w_ref, o_ref):
        full = w_ref[...]
        for j in range(N_DEV):
            o_ref[j] = full[:, j * n:(j + 1) * n]

    return pl.pallas_call(
        body, name="shard_cols_%d" % n, grid=(k // tr,),
        in_specs=[pl.BlockSpec((tr, N_DEV * n), lambda i: (i, 0))],
        out_specs=pl.BlockSpec((N_DEV, tr, n), lambda i: (0, i, 0)),
        out_shape=jax.ShapeDtypeStruct((N_DEV, k, n), w.dtype),
        compiler_params=_params(("parallel",)),
    )(w)


IN_OFFS = tuple(sum(IN_SPLITS[:i]) for i in range(len(IN_SPLITS) + 1))
IN_SHARD = IN_OFFS[-1] // N_DEV
REGROUP_ROWS = 128
MISC_AT = Q_LORA + KV_LORA + 2 * D
A_COLS = MISC_AT + LANES
A_TILE = A_COLS // 5
B_COLS = 3 * HEADS * HEAD_DIM
MISC_BLOCK = MISC_AT // LANES
KR_AT = 64


def _w_in_regroup(g, own=None, slot=None):
    def groups(full):
        fq, fk, fv, wf, cq, ckv, kr, gf, gm = [full[:, IN_OFFS[i]:IN_OFFS[i + 1]] for i in range(9)]
        rows = full.shape[0]
        gap = jnp.zeros((rows, KR_AT - HEADS), BF16)
        tail = jnp.zeros((rows, LANES - KR_AT - ROPE_DIM), BF16)
        return jnp.concatenate([cq, ckv, gf, gm, wf, gap, kr, tail], axis=1), jnp.concatenate([fq, fk, fv], axis=1)

    return _from_shards(g, groups, [A_COLS, B_COLS], "w_in_regroup", own, slot)


def _w_in_ungroup(da, db_):
    def body(a_ref, b_ref, o_ref):
        a = a_ref[...]
        lora = Q_LORA + KV_LORA
        full = jnp.concatenate([b_ref[...], a[:, MISC_AT:MISC_AT + HEADS], a[:, :lora],
                                a[:, MISC_AT + KR_AT:MISC_AT + KR_AT + ROPE_DIM], a[:, lora:MISC_AT]], axis=1)
        for j in range(N_DEV):
            o_ref[j] = full[:, j * IN_SHARD:(j + 1) * IN_SHARD]

    tr = REGROUP_ROWS
    return pl.pallas_call(
        body, name="w_in_ungroup", grid=(D // tr,),
        in_specs=[pl.BlockSpec((tr, A_COLS), lambda i: (i, 0)), pl.BlockSpec((tr, B_COLS), lambda i: (i, 0))],
        out_specs=pl.BlockSpec((N_DEV, tr, IN_SHARD), lambda i: (0, i, 0)),
        out_shape=jax.ShapeDtypeStruct((N_DEV, D, IN_SHARD), BF16),
        compiler_params=_params(("parallel",)),
    )(da, db_)


def _prepare_weights(g, own=None, slot=None):
    w = {}
    if own is not None:
        small = ("w_uq", "w_ukv", "w_out", "w_ffn_out")
        g = {n: (_with_own(a, own[n], slot[0]) if n in small else a) for n, a in g.items()}
    pick = (lambda n: (own[n], slot)) if own is not None else (lambda n: (None, None))
    if "w_in" in g:
        w["w_a"], w["w_b"] = _w_in_regroup(g["w_in"], *pick("w_in"))
    if "w_uq" in g:
        w_uq = g["w_uq"].reshape(Q_LORA, HEADS, 96)
        w["w_uq"] = jnp.pad(w_uq, ((0, 0), (0, 0), (0, 32))).reshape(Q_LORA, HEADS * LANES)
        ukv = g["w_ukv"]
        w["w_k"] = jnp.transpose(jnp.pad(ukv[:, :, :64], ((0, 0), (0, 0), (0, 64))), (1, 0, 2)).reshape(KV_LORA, HEADS * LANES)
        w["w_v"] = jnp.transpose(ukv[:, :, 64:], (1, 0, 2)).reshape(KV_LORA, HEADS * HEAD_DIM)
    if "w_out" in g:
        w["w_pf"] = _unshard_cols(g["w_proj_fox"], *pick("w_proj_fox"))
        w["w_pm"] = _unshard_cols(g["w_proj_mla"], *pick("w_proj_mla"))
        w["w_out"] = g["w_out"].reshape(D, D)
    if "w_ffn_in" in g:
        w["w_ffn_in"] = _unshard_ffn_in(g["w_ffn_in"], *pick("w_ffn_in"))
        w["w_ffn_out"] = g["w_ffn_out"].reshape(D_FF, D)
    return w


def _shard_grads(dw):
    out = {}
    if "w_a" in dw:
        out["w_in"] = _w_in_ungroup(dw["w_a"], dw["w_b"])
    if "w_uq" in dw:
        w_uq = dw["w_uq"].reshape(Q_LORA, HEADS, LANES)[:, :, :96].reshape(Q_LORA, Q_LORA)
        out["w_uq"] = w_uq.reshape(N_DEV, Q_LORA // N_DEV, Q_LORA)
        k_part = dw["w_k"].reshape(KV_LORA, HEADS, LANES)[:, :, :64]
        v_part = dw["w_v"].reshape(KV_LORA, HEADS, HEAD_DIM)
        out["w_ukv"] = jnp.transpose(jnp.concatenate([k_part, v_part], axis=2), (1, 0, 2))
    if "w_out" in dw:
        out["w_proj_fox"] = _shard_cols(dw["w_pf"])
        out["w_proj_mla"] = _shard_cols(dw["w_pm"])
        out["w_out"] = dw["w_out"].reshape(N_DEV, D // N_DEV, D)
    if "w_ffn_in" in dw:
        out["w_ffn_in"] = _shard_ffn_in_t(dw["w_ffn_in"])
        out["w_ffn_out"] = dw["w_ffn_out"].reshape(N_DEV, D_FF // N_DEV, D)
    return out


def _fwd_bwd(x, pos, mod, target, w, vec, wts, send, relay):
    shift_mix, scale_mix, gate_mix, shift_ffn, scale_ffn, gate_ffn = [mod[:, i * D:(i + 1) * D] for i in range(6)]
    g_pre_mix, g_post_mix, g_pre_ffn, g_post_ffn = vec["g_pre_mix"], vec["g_post_mix"], vec["g_pre_ffn"], vec["g_post_ffn"]
    g_q, g_kv = vec["g_q_lora"], vec["g_kv_lora"]

    inv_freq = 1.0 / (ROPE_THETA ** (jnp.arange(0, ROPE_DIM, 2, dtype=F32) / ROPE_DIM))
    invf = jnp.concatenate([jnp.zeros((64,), F32), inv_freq, inv_freq, jnp.zeros((32,), F32)]).reshape(1, LANES)
    ct, sa, sb = _rope_tables(pos, invf)

    def pre1(xv, g, sc, sh):
        return (xv * _rstd(xv) * g) * (1.0 + sc) + sh
    proj_a, h = _mm_epi(x, w["w_a"], "nn", A_TILE, lambda r: ((r,), ()), "in_proj_a", 1024, outs=[(A_COLS, A_TILE, F32)],
                        pro=(pre1, [g_pre_mix, scale_mix, shift_mix], 0))
    qkv = _mm(h, w["w_b"], "nn", BF16, "in_proj_b")

    def lora_norm(cv, g):
        return cv * _rstd(cv) * g
    w = {**w, **wts("lora", qkv)}
    tables = [(ct, LANES), (sa, LANES), (sb, LANES)]

    def rope_q(qv, c_, a_, b_):
        return (jnp.concatenate([_rope(qv[:, hd * LANES:(hd + 1) * LANES], c_, a_, b_) for hd in range(HEADS)], axis=1),), ()
    q_m, cqn = _mm_epi(proj_a, w["w_uq"], "nn", D, rope_q, "mla_uq", 512, rows=tables, outs=[(D, D, BF16)],
                       pro=(lora_norm, [g_q], 0))

    def rope_k(kv, misc, c_, a_, b_):
        lane = lax.broadcasted_iota(jnp.int32, (1, LANES), 1)
        kpe = jnp.where((lane >= 64) & (lane < 96), _rope(misc, c_, a_, b_), 0.0)
        return (jnp.concatenate([kv[:, hd * LANES:(hd + 1) * LANES] + kpe for hd in range(HEADS)], axis=1),), ()
    k_m, ckvn = _mm_epi(proj_a, w["w_k"], "nn", D, rope_k, "mla_uk", 512, rows=[(proj_a, LANES, MISC_BLOCK)] + tables,
                        outs=[(D, D, BF16)], pro=(lora_norm, [g_kv], Q_LORA // KV_LORA))
    v_m = _mm(ckvn, w["w_v"], "nn", BF16, "mla_uv")

    bf = jnp.transpose(vec["b_forget"])
    zt, neg_f = _fox_gates(proj_a, MISC_BLOCK, bf)
    bias = neg_f.reshape(HEADS, N_ATT, 1, ATT_T)
    o_b, lse_b = _attn_fwd(q_m, 0, k_m, 0, v_m, 0, 2 * LANES, 1.0 / math.sqrt(64 + ROPE_DIM), None, "mla_attn")
    bias = bias + wts("relay_proj", o_b)["tok"][0, 0]
    o_a, lse_a = _attn_fwd(qkv, 0, qkv, 4, qkv, 8, LANES, 1.0 / math.sqrt(HEAD_DIM), bias, "fox_attn")

    w = {**w, **wts("proj", o_a)}
    gate_mix = gate_mix + wts("relay_ffn", o_a)["tok"][0, 0]
    pa = _mm(o_a, w["w_pf"], "nn", BF16, "proj_fox")

    def merge(pb_, gf, gm, pa_):
        return (_sigmoid(gf) * pa_ + _sigmoid(gm) * pb_, pb_), ()
    merged, pb = _mm_epi(o_b, w["w_pm"], "nn", 512, merge, "proj_mla", 1024,
                         rows=[(proj_a, 512, 2), (proj_a, 512, 4), (pa, 512)], outs=[(D, 512, BF16), (D, 512, BF16)])
    def post1(yv, xv, gate, gpost, gpre, sc, sh):
        x1 = xv + gate * (yv * _rstd(yv) * gpost)
        return (x1, (x1 * _rstd(x1) * gpre) * (1.0 + sc) + sh, yv), ()
    x1, h2, y = _mm_epi(merged, w["w_out"], "nn", D, post1, "out_proj", 512, rows=[(x, D)],
                        vecs=[gate_mix, g_post_mix, g_pre_ffn, scale_ffn, shift_ffn],
                        outs=[(D, D, F32), (D, D, BF16), (D, D, F32)])
    w = {**w, **wts("ffn", h2)}

    def swiglu(r):
        g, u = r[:, :FFN_T], r[:, FFN_T:]
        return (g * _sigmoid(g) * u, r), ()
    act, gu = _mm_epi(h2, w["w_ffn_in"], "nn", 2 * FFN_T, swiglu, "ffn_in", 1024,
                      outs=[(D_FF, FFN_T, BF16), (2 * D_FF, 2 * FFN_T, BF16)])

    def head(y2v, x1v, tv, gate, gpost):
        r = _rstd(y2v)
        yn = y2v * r
        n2 = yn * gpost
        err = (x1v + gate * n2) - tv
        dx2 = err * (1.0 / D)
        dn2 = dx2 * gate
        dy2 = _norm_bwd(dn2 * gpost, yn, r)
        return (dx2, dy2), (_colsum(err * err), _colsum(dx2 * n2), _colsum(dn2 * yn))
    dx2, dy2, err_cols, d_gate_ffn, d_g_post_ffn = _mm_epi(
        act, w["w_ffn_out"], "nn", D, head, "ffn_out", 512, rows=[(x1, D), (target, D)], vecs=[gate_ffn, g_post_ffn],
        outs=[(D, D, F32), (D, D, BF16)], sums=[D, D, D])

    def swiglu_bwd(da, guv):
        g, u = guv[:, :FFN_T].astype(F32), guv[:, FFN_T:].astype(F32)
        sg = _sigmoid(g)
        return (jnp.concatenate([da * u * (sg * (1.0 + g * (1.0 - sg))), da * (g * sg)], axis=1),), ()
    (dgu,) = _mm_epi(dy2, w["w_ffn_out"], "nt", FFN_T, swiglu_bwd, "ffn_out_dx", 1024, rows=[(gu, 2 * FFN_T)],
                     outs=[(2 * D_FF, 2 * FFN_T, BF16)])
    dw = {"w_ffn_out": _mm(act, dy2, "tn", BF16, "ffn_out_dw")}
    dw["w_ffn_in"] = _mm(dgu, h2, "tn", BF16, "ffn_in_dw")
    gate_mix = gate_mix + send({n: dw.pop(n) for n in ("w_ffn_in", "w_ffn_out")})[0, 0]

    def mid(dh, x1v, dx2v, yv, gpre, sc, gate, gpost):
        r2 = _rstd(x1v)
        x1n = x1v * r2
        t = dh * x1n
        dx1 = dx2v + _norm_bwd(dh * (gpre * (1.0 + sc)), x1n, r2)
        ry = _rstd(yv)
        yn = yv * ry
        dn1 = dx1 * gate
        dy = _norm_bwd(dn1 * gpost, yn, ry)
        sums = (_colsum(dh), _colsum(t) * gpre, _colsum(t) * (1.0 + sc), _colsum(dx1 * (yn * gpost)), _colsum(dn1 * yn))
        return (dx1, dy), sums
    dx1, dy, d_shift_ffn, d_scale_ffn, d_g_pre_ffn, d_gate_mix, d_g_post_mix = _mm_epi(
        dgu, w["w_ffn_in"], "nt", D, mid, "ffn_in_dx", 512, rows=[(x1, D), (dx2, D), (y, D)],
        vecs=[g_pre_ffn, scale_ffn, gate_mix, g_post_mix], outs=[(D, D, F32), (D, D, BF16)], sums=[D] * 5)

    dw["w_out"] = _mm(merged, dy, "tn", BF16, "out_proj_dw")

    def merge_bwd(dm, gf, gm, pa_, pb_):
        sf, sm = _sigmoid(gf), _sigmoid(gm)
        return (dm * sf, dm * sm, dm * pa_ * (sf * (1.0 - sf)), dm * pb_ * (sm * (1.0 - sm))), ()
    dpa, dpb, dgf, dgm = _mm_epi(dy, w["w_out"], "nt", 512, merge_bwd, "out_proj_dx", 1024,
                                 rows=[(proj_a, 512, 2), (proj_a, 512, 4), (pa, 512), (pb, 512)],
                                 outs=[(D, 512, BF16)] * 4)
    do_a = _mm(dpa, w["w_pf"], "nt", BF16, "proj_fox_dx")
    do_b = _mm(dpb, w["w_pm"], "nt", BF16, "proj_mla_dx")
    dw["w_pf"] = _mm(o_a, dpa, "tn", BF16, "proj_fox_dw")
    dw["w_pm"] = _mm(o_b, dpb, "tn", BF16, "proj_mla_dw")
    bias = bias + send({n: dw.pop(n) for n in ("w_out", "w_pf", "w_pm")})[0, 0]

    sc_a, sc_b = 1.0 / math.sqrt(HEAD_DIM), 1.0 / math.sqrt(64 + ROPE_DIM)
    dq_a, dk_a, dv_a, dbias = _attn_grad(qkv, 0, qkv, 4, qkv, 8, do_a, lse_a, LANES, sc_a, bias, BF16, "fox_attn_bwd")
    dq_m, dk_m, dv_m = _attn_grad(q_m, 0, k_m, 0, v_m, 0, do_b, lse_b, 2 * LANES, sc_b, None, F32, "mla_attn_bwd")

    def mla_rope_bwd(dq, dk, c_, a_, b_):
        lane = lax.broadcasted_iota(jnp.int32, (1, LANES), 1)
        dqs = [_rope_t(dq[:, hd * LANES:(hd + 1) * LANES], c_, a_, b_) for hd in range(HEADS)]
        dkpe = dk[:, 0:LANES]
        for hd in range(1, HEADS):
            dkpe = dkpe + dk[:, hd * LANES:(hd + 1) * LANES]
        dkpe = jnp.where((lane >= 64) & (lane < 96), dkpe, 0.0)
        dkr = jnp.where((lane >= 64) & (lane < 96), _rope_t(dkpe, c_, a_, b_), 0.0)
        return (jnp.concatenate(dqs, axis=1), dk, dkr), ()
    dqb, dkb, dkr = _rowwise(mla_rope_bwd, [(dq_m, D, 0), (dk_m, D, 0), (ct, LANES, 0), (sa, LANES, 0), (sb, LANES, 0)],
                             [], [(D, BF16), (D, BF16), (LANES, F32)], [], "mla_rope_bwd")
    def lora_q_bwd(dq, cq, gq):
        rq = _rstd(cq)
        cqh = cq * rq
        return (_norm_bwd(dq * gq, cqh, rq),), (_colsum(dq * cqh),)
    dcq, d_g_q = _mm_epi(dqb, w["w_uq"], "nt", Q_LORA, lora_q_bwd, "mla_uq_dx", 512, rows=[(proj_a, Q_LORA, 0)],
                         vecs=[g_q], outs=[(Q_LORA, Q_LORA, BF16)], sums=[Q_LORA])

    def lora_kv_bwd(dv_part, dk_part, ckv, gkv):
        dkv = dv_part + dk_part
        rk = _rstd(ckv)
        ckh = ckv * rk
        return (_norm_bwd(dkv * gkv, ckh, rk),), (_colsum(dkv * ckh),)
    dckv, d_g_kv = _mm_epi(dv_m, w["w_v"], "nt", KV_LORA, lora_kv_bwd, "mla_uv_dx", 1024,
                           rows=[(_mm(dkb, w["w_k"], "nt", F32, "mla_uk_dx"), KV_LORA), (proj_a, KV_LORA, 3)],
                           vecs=[g_kv], outs=[(KV_LORA, KV_LORA, BF16)], sums=[KV_LORA])

    dzt, d_bf = _fox_gates_bwd(dbias.reshape(HEADS, S), zt, bf)
    dmisc = (dkr + jnp.pad(jnp.transpose(dzt), ((0, 0), (0, LANES - HEADS)))).astype(BF16)
    dproj_a = jnp.concatenate([dcq, dckv, dgf, dgm, dmisc], axis=1)
    dqkv = jnp.concatenate([dq_a, dk_a, dv_a], axis=1)
    dw["w_a"] = _mm(h, dproj_a, "tn", BF16, "in_proj_a_dw")
    dw["w_b"] = _mm(h, dqkv, "tn", BF16, "in_proj_b_dw")
    tok = send(dw, True)
    dh_a = _mm(dproj_a, w["w_a"], "nt", F32, "in_proj_a_dx", dep=tok)
    tok = relay(dh_a)
    tok = send({"w_uq": _mm(cqn, dqb, "tn", BF16, "mla_uq_dw", dep=tok),
                "w_k": _mm(ckvn, dkb, "tn", BF16, "mla_uk_dw", dep=tok),
                "w_v": _mm(ckvn, dv_m, "tn", BF16, "mla_uv_dw", dep=tok)}, late=True)
    g_pre_mix = g_pre_mix + tok[0, 0]

    def first(dh_b, dh_a, xv, dx1v, gpre, sc):
        dhv = dh_b + dh_a
        r = _rstd(xv)
        xn = xv * r
        t = dhv * xn
        dx = dx1v + _norm_bwd(dhv * (gpre * (1.0 + sc)), xn, r)
        return (dx,), (_colsum(dhv), _colsum(t) * gpre, _colsum(t) * (1.0 + sc))
    grad_x, d_shift_mix, d_scale_mix, d_g_pre_mix = _mm_epi(
        dqkv, w["w_b"], "nt", D, first, "in_proj_b_dx", 512,
        rows=[(dh_a, D), (x, D), (dx1, D)],
        vecs=[g_pre_mix, scale_mix], outs=[(D, D, F32)], sums=[D] * 3)

    dmod = jnp.concatenate([d_shift_mix, d_scale_mix, d_gate_mix, d_shift_ffn, d_scale_ffn, d_gate_ffn], axis=1)
    small = dict(dmod=dmod, g_pre_mix=d_g_pre_mix, g_post_mix=d_g_post_mix, g_pre_ffn=d_g_pre_ffn,
                 g_post_ffn=d_g_post_ffn, g_q_lora=d_g_q, g_kv_lora=d_g_kv,
                 b_forget=jnp.pad(jnp.transpose(d_bf), ((0, 0), (0, LANES - HEADS))), err=err_cols)
    return grad_x, small


SMALL_ORDER = ("dmod", "g_pre_mix", "g_post_mix", "g_pre_ffn", "g_post_ffn", "g_q_lora", "g_kv_lora", "b_forget", "err")
SMALL_PARAM = {"dmod": "b_ada"}
MATRICES = ("w_in", "w_uq", "w_ukv", "w_proj_fox", "w_proj_mla", "w_out", "w_ffn_in", "w_ffn_out")
WEIGHTS = ("w_ada", "b_ada", "g_pre_mix", "g_post_mix", "g_pre_ffn", "g_post_ffn", "w_in", "b_forget", "g_q_lora",
           "w_uq", "g_kv_lora", "w_ukv", "w_proj_fox", "w_proj_mla", "w_out", "w_ffn_in", "w_ffn_out")


def kernel(x, c, positions, w_ada, b_ada, g_pre_mix, g_post_mix, g_pre_ffn, g_post_ffn, w_in, b_forget, g_q_lora, w_uq, g_kv_lora, w_ukv, w_proj_fox, w_proj_mla, w_out, w_ffn_in, w_ffn_out, loss_target, m_w_ada, m_b_ada, m_g_pre_mix, m_g_post_mix, m_g_pre_ffn, m_g_post_ffn, m_w_in, m_b_forget, m_g_q_lora, m_w_uq, m_g_kv_lora, m_w_ukv, m_w_proj_fox, m_w_proj_mla, m_w_out, m_w_ffn_in, m_w_ffn_out, v_w_ada, v_b_ada, v_g_pre_mix, v_g_post_mix, v_g_pre_ffn, v_g_post_ffn, v_w_in, v_b_forget, v_g_q_lora, v_w_uq, v_g_kv_lora, v_w_ukv, v_w_proj_fox, v_w_proj_mla, v_w_out, v_w_ffn_in, v_w_ffn_out):
    prm = dict(w_ada=w_ada, b_ada=b_ada, g_pre_mix=g_pre_mix, g_post_mix=g_post_mix, g_pre_ffn=g_pre_ffn,
               g_post_ffn=g_post_ffn, w_in=w_in, b_forget=b_forget, g_q_lora=g_q_lora, w_uq=w_uq, g_kv_lora=g_kv_lora,
               w_ukv=w_ukv, w_proj_fox=w_proj_fox, w_proj_mla=w_proj_mla, w_out=w_out, w_ffn_in=w_ffn_in, w_ffn_out=w_ffn_out)
    mom = dict(w_ada=m_w_ada, b_ada=m_b_ada, g_pre_mix=m_g_pre_mix, g_post_mix=m_g_post_mix, g_pre_ffn=m_g_pre_ffn,
               g_post_ffn=m_g_post_ffn, w_in=m_w_in, b_forget=m_b_forget, g_q_lora=m_g_q_lora, w_uq=m_w_uq,
               g_kv_lora=m_g_kv_lora, w_ukv=m_w_ukv, w_proj_fox=m_w_proj_fox, w_proj_mla=m_w_proj_mla, w_out=m_w_out,
               w_ffn_in=m_w_ffn_in, w_ffn_out=m_w_ffn_out)
    var = dict(w_ada=v_w_ada, b_ada=v_b_ada, g_pre_mix=v_g_pre_mix, g_post_mix=v_g_post_mix, g_pre_ffn=v_g_pre_ffn,
               g_post_ffn=v_g_post_ffn, w_in=v_w_in, b_forget=v_b_forget, g_q_lora=v_g_q_lora, w_uq=v_w_uq,
               g_kv_lora=v_g_kv_lora, w_ukv=v_w_ukv, w_proj_fox=v_w_proj_fox, w_proj_mla=v_w_proj_mla, w_out=v_w_out,
               w_ffn_in=v_w_ffn_in, w_ffn_out=v_w_ffn_out)
    me = _flat(*_coords())
    slot = jnp.reshape(me, (1,)).astype(jnp.int32)

    own = {n: prm[n][0].astype(BF16) for n in MATRICES}
    no_dep = jnp.zeros((8, LANES), F32)
    (st_c, st_in), tok = _async_start([[c], [own["w_in"]]], ["gather", "spread"], no_dep, "gather_in_start")
    (c_own,), (c_land,) = _async_wait(st_c, tok, "gather_c_wait")
    c_all = _with_own(c_land, c_own, me).reshape(N_DEV, D)
    ada_cols = w_ada.shape[2]
    b_cols = lax.dynamic_slice(b_ada, (0, me * ada_cols), (1, ada_cols))
    mod_cols, silu_c = _mod_part(c_all, w_ada[0], b_cols)
    (mod_all,) = _all_gather([mod_cols], "gather_mod")

    (w_in_own,), (w_in_land,) = _async_wait(st_in, mod_all, "gather_in_wait")
    (st_in,), tok = _async_start([[w_in_land]], "forward", no_dep, "gather_in_forward")
    _, (w_in_land,) = _async_wait(st_in, tok, "gather_in_forward_wait")
    w = _prepare_weights({"w_in": w_in_land}, {"w_in": w_in_own}, slot)
    later = dict(lora=("w_uq", "w_ukv"), proj=("w_proj_fox", "w_proj_mla", "w_out"), ffn=("w_ffn_in", "w_ffn_out"))
    states, tok = _async_start([[own[n] for n in names] for names in later.values()], ["gather", "spread", "spread"],
                               w["w_b"], "gather_rest_start")
    gather_state = dict(zip(later, states))
    own_thru = {}

    def wts(group, after):
        if group.startswith("relay_"):
            name = group[len("relay_"):]
            own_thru[name], lands = _async_wait(gather_state[name], after, "gather_" + name + "_wait")
            (gather_state[name],), t = _async_start([lands], "forward", no_dep, "gather_" + name + "_forward")
            return {"tok": t}
        srcs, lands = _async_wait(gather_state[group], after, "gather_" + group + "_landed")
        srcs = own_thru.get(group, srcs)
        return _prepare_weights(dict(zip(later[group], lands)), dict(zip(later[group], srcs)), slot)

    sent, late_sent, last = [], [], {}

    def send(grads, final=False, late=False):
        shards = _shard_grads(grads)
        names = list(shards)
        (state,), t = _async_start([[shards[n] for n in names]], "pair" if final else "exchange", no_dep,
                                   "exchange_" + names[0] + "_start")
        if final:
            last.update(names=names, state=state)
        else:
            (late_sent if late else sent).append((names, state))
        return t

    def relay(after):
        srcs, lands = _async_wait(last["state"], after, "exchange_pair_wait")
        core = jnp.reshape(lax.axis_index("c"), (1,)).astype(jnp.int32)
        sums = [_add_sibling(src, land, core) for src, land in zip(srcs, lands)]
        (last["state"],), t = _async_start([sums], "chips", no_dep, "exchange_chips_start")
        return t

    mod = lax.dynamic_index_in_dim(mod_all, me, axis=1, keepdims=False).reshape(1, 6 * D) + tok[0, 0]

    vec = dict(g_pre_mix=g_pre_mix, g_post_mix=g_post_mix, g_pre_ffn=g_pre_ffn, g_post_ffn=g_post_ffn,
               g_q_lora=g_q_lora, g_kv_lora=g_kv_lora, b_forget=b_forget)
    pos = positions.astype(F32).reshape(S, 1)
    grad_x, small = _fwd_bwd(x[0], pos, mod, loss_target[0], w, vec, wts, send, relay)

    bundle = jnp.concatenate([small[n] for n in SMALL_ORDER], axis=1)
    (small_state,), tok = _async_start([[bundle]], "gather", jnp.zeros((8, LANES), F32), "gather_small_start")

    out = {}
    swap = lambda a: jnp.swapaxes(a, -1, -2)

    def update(n, land, src, sl):
        if n != "w_ffn_in":
            out[n] = _adamw(prm[n][0], mom[n][0], var[n][0], land, "adamw_" + n, src, sl)
            return out[n][0]
        res = _adamw(swap(prm[n][0]), swap(mom[n][0]), swap(var[n][0]), land, "adamw_" + n, src, sl)
        out[n] = tuple(swap(t) for t in res)
        return res[0]

    after = tok
    for names, state in sent:
        srcs, lands = _async_wait(state, after, "exchange_" + names[0] + "_wait")
        for n, src, land in zip(names, srcs, lands):
            after = update(n, land, src, slot)
    srcs, lands = _async_wait(last["state"], after, "exchange_chips_wait")
    for n, src, land in zip(last["names"], srcs, lands):
        after = update(n, land, src, slot // 2)
    for names, state in late_sent:
        srcs, lands = _async_wait(state, after, "exchange_" + names[0] + "_wait")
        for n, src, land in zip(names, srcs, lands):
            after = update(n, land, src, slot)

    (own_bundle,), (bundle_all,) = _async_wait(small_state, after, "gather_small_wait")
    bundle_all = _with_own(bundle_all, own_bundle, me)
    dmod_all = bundle_all[:, 0, :6 * D]
    dm_cols = lax.dynamic_slice(dmod_all, (0, me * ada_cols), (N_DEV, ada_cols))
    out["w_ada"] = _adamw_w_ada(w_ada[0], m_w_ada[0], v_w_ada[0], jnp.transpose(silu_c), dm_cols)

    offsets, off = {}, 0
    for n in SMALL_ORDER:
        offsets[n] = off
        off += small[n].shape[1]
    names = [SMALL_PARAM.get(n, n) for n in SMALL_ORDER if n != "err"]
    results, err = _adamw_rows(bundle_all, [offsets[n] for n in SMALL_ORDER if n != "err"],
                               [prm[n] for n in names], [mom[n] for n in names], [var[n] for n in names],
                               offsets["err"], D)
    out.update(zip(names, results))
    loss = 0.5 * jnp.sum(err) / D

    res = [loss, grad_x[None]]
    for kind in range(4):
        for n in WEIGHTS:
            t = out[n][kind]
            res.append(t[None] if prm[n].ndim == 3 else t)
    return tuple(res)
```

```python
import functools
import math

import jax
import jax.numpy as jnp
from jax import lax
from jax.experimental import pallas as pl
from jax.experimental.pallas import tpu as pltpu

F32 = jnp.float32
BF16 = jnp.bfloat16

N_DEV = 8
S = 2048
D = 1024
D_FF = 2816
HEADS = 8
HEAD_DIM = 64
Q_LORA = 768
KV_LORA = 256
ROPE_DIM = 32
ROPE_THETA = 10000.0
NORM_EPS = 1e-6
LANES = 128
VMEM_LIMIT = 56 * 1024 * 1024

ADAM_LR = 0.001
ADAM_B1 = 0.9
ADAM_B2 = 0.999
ADAM_EPS = 1e-08
ADAM_WD = 0.01
ADAM_STEP = 10

ATT_T = 256
LOG2E = 1.4426950408889634
N_ATT = S // ATT_T

NN = (((1,), (0,)), ((), ()))
NT = (((1,), (1,)), ((), ()))
TN = (((0,), (0,)), ((), ()))
MESH = pl.DeviceIdType.MESH


def _params(sem=None):
    return pltpu.CompilerParams(dimension_semantics=sem, vmem_limit_bytes=VMEM_LIMIT)


def _pick(n, cap):
    best = None
    for t in range(LANES, cap + 1, LANES):
        if n % t == 0:
            best = t
    return best if best is not None else n


def _mm(a, b, mode, out_dtype, name, acc=None, dep=None):
    if mode == "nn":
        (m, k), (k2, n), dn = a.shape, b.shape, NN
    elif mode == "nt":
        (m, k), (n, k2), dn = a.shape, b.shape, NT
    else:
        (k, m), (k2, n), dn = a.shape, b.shape, TN
    assert k == k2, (a.shape, b.shape, mode)
    tn = _pick(n, 640)
    tm = _pick(m, 1536)
    osz = jnp.dtype(out_dtype).itemsize

    def need(tm_):
        blk = tm_ * k * 2 + tn * k * 2 + tm_ * tn * osz + (tm_ * tn * 4 if acc is not None else 0)
        return 2 * blk + tm_ * tn * 4
    while need(tm) > 36 * 1024 * 1024 and tm % 256 == 0:
        tm //= 2

    def body(*refs):
        a_ref, b_ref, o_ref = refs[0], refs[1], refs[-1]
        r = lax.dot_general(a_ref[...], b_ref[...], dn, preferred_element_type=F32)
        if acc is not None:
            r = r + refs[2][...]
        o_ref[...] = r.astype(o_ref.dtype)

    if mode == "tn":
        a_spec = pl.BlockSpec((k, tm), lambda i, j: (0, i))
    else:
        a_spec = pl.BlockSpec((tm, k), lambda i, j: (i, 0))
    if mode == "nt":
        b_spec = pl.BlockSpec((tn, k), lambda i, j: (j, 0))
    else:
        b_spec = pl.BlockSpec((k, tn), lambda i, j: (0, j))
    o_spec = pl.BlockSpec((tm, tn), lambda i, j: (i, j))
    in_specs = [a_spec, b_spec] + ([o_spec] if acc is not None else [])
    in_specs += [pl.BlockSpec(memory_space=pl.ANY)] if dep is not None else []
    args = (a, b) + ((acc,) if acc is not None else ()) + ((dep,) if dep is not None else ())
    return pl.pallas_call(
        body, name=name, grid=(m // tm, n // tn),
        in_specs=in_specs, out_specs=o_spec,
        out_shape=jax.ShapeDtypeStruct((m, n), out_dtype),
        compiler_params=_params(("parallel", "parallel")),
    )(*args)


def _mm_epi(a, b, mode, tnb, epi, name, tm, rows=(), vecs=(), outs=(), sums=(), pro=None):
    m = a.shape[0]
    k, nb = (b.shape if mode == "nn" else b.shape[::-1])
    dn = NN if mode == "nn" else NT
    pro_fn, pro_vecs, a_off = pro if pro is not None else (None, (), 0)
    n_in = 2 + len(rows) + len(vecs)
    n_all = n_in + len(pro_vecs)
    sub = min(tm, 256)

    def body(*refs):
        if pro is not None:
            a_out, a_scr = refs[-2:]
            refs = refs[:-2]

            @pl.when(pl.program_id(1) == 0)
            def _():
                a_scr[...] = pro_fn(refs[0][...], *[x[...] for x in refs[n_in:n_all]]).astype(BF16)
                a_out[...] = a_scr[...]
            a_ref = a_scr
        else:
            a_ref = refs[0]
        o_refs = refs[n_all:n_all + len(outs)]
        s_refs = refs[n_all + len(outs):]
        if sums:
            @pl.when((pl.program_id(0) == 0) & (pl.program_id(1) == 0))
            def _():
                for s_ref in s_refs:
                    s_ref[...] = jnp.zeros(s_ref.shape, F32)
        for c in range(tm // sub):
            rs = slice(c * sub, (c + 1) * sub)
            r = lax.dot_general(a_ref[rs, :], refs[1][...], dn, preferred_element_type=F32)
            o_vals, s_vals = epi(r, *[x[rs, :] for x in refs[2:2 + len(rows)]], *[x[...] for x in refs[2 + len(rows):n_in]])
            assert len(o_vals) == len(o_refs) and len(s_vals) == len(s_refs)
            for o_ref, val in zip(o_refs, o_vals):
                o_ref[rs, :] = val.astype(o_ref.dtype)
            for s_ref, val in zip(s_refs, s_vals):
                s_ref[...] += val

    once = dict(pipeline_mode=pl.Buffered(1)) if nb == tnb else {}
    if mode == "nn":
        b_spec = pl.BlockSpec((k, tnb), lambda i, j: (0, j), **once)
    else:
        b_spec = pl.BlockSpec((tnb, k), lambda i, j: (j, 0), **once)
    in_specs = [pl.BlockSpec((tm, k), lambda i, j: (i, a_off)), b_spec]
    rows = [tuple(r) + (0,) * (3 - len(r)) for r in rows]
    in_specs += [pl.BlockSpec((tm, w), functools.partial(lambda i, j, off: (i, j + off), off=off)) for _, w, off in rows]
    in_specs += [pl.BlockSpec(v.shape, lambda i, j: (0, 0)) for v in list(vecs) + list(pro_vecs)]
    out_specs = [pl.BlockSpec((tm, w), lambda i, j: (i, j)) for _, w, _ in outs]
    out_specs += [pl.BlockSpec((1, w), lambda i, j: (0, 0)) for w in sums]
    out_shape = [jax.ShapeDtypeStruct((m, full), dt) for full, _, dt in outs]
    out_shape += [jax.ShapeDtypeStruct((1, w), F32) for w in sums]
    if pro is not None:
        out_specs.append(pl.BlockSpec((tm, k), lambda i, j: (i, 0)))
        out_shape.append(jax.ShapeDtypeStruct((m, k), BF16))
    return pl.pallas_call(
        body, name=name, grid=(m // tm, nb // tnb),
        in_specs=in_specs, out_specs=out_specs, out_shape=out_shape,
        scratch_shapes=[pltpu.VMEM((tm, k), BF16)] if pro is not None else [],
        compiler_params=_params(("arbitrary", "arbitrary") if sums else ("parallel", "arbitrary" if pro is not None else "parallel")),
    )(a, b, *[r[0] for r in rows], *vecs, *pro_vecs)


def _rowwise(fn, row_ins, vec_ins, row_outs, sum_outs, name, tm=256):
    n_in = len(row_ins) + len(vec_ins)
    n_o = len(row_outs)
    rows = row_ins[0][0].shape[0]

    def body(*refs):
        vals = [r[...] for r in refs[:n_in]]
        outs = refs[n_in:]
        ro, so = fn(*vals)
        assert len(ro) == n_o and len(so) == len(sum_outs)
        for r, v in zip(outs[:n_o], ro):
            r[...] = v.astype(r.dtype)
        if sum_outs:
            @pl.when(pl.program_id(0) == 0)
            def _():
                for r in outs[n_o:]:
                    r[...] = jnp.zeros(r.shape, F32)
            for r, v in zip(outs[n_o:], so):
                r[...] += v

    in_specs = [pl.BlockSpec((tm, w), functools.partial(lambda i, b: (i, b), b=b)) for _, w, b in row_ins]
    in_specs += [pl.BlockSpec(v.shape, lambda i: (0, 0)) for v in vec_ins]
    out_specs = [pl.BlockSpec((tm, w), lambda i: (i, 0)) for w, _ in row_outs]
    out_specs += [pl.BlockSpec((1, w), lambda i: (0, 0)) for w in sum_outs]
    out_shape = [jax.ShapeDtypeStruct((rows, w), dt) for w, dt in row_outs]
    out_shape += [jax.ShapeDtypeStruct((1, w), F32) for w in sum_outs]
    return pl.pallas_call(
        body, name=name, grid=(rows // tm,),
        in_specs=in_specs, out_specs=out_specs, out_shape=out_shape,
        compiler_params=_params(("arbitrary",)),
    )(*[a for a, _, _ in row_ins], *vec_ins)


def _sigmoid(x):
    return 1.0 / (1.0 + jnp.exp(-x))


def _gate(x):
    return 0.5 * jnp.tanh(0.5 * x) + 0.5


def _rstd(x):
    return lax.rsqrt(jnp.mean(x * x, axis=-1, keepdims=True) + NORM_EPS)


def _norm_bwd(dyn, xn, r):
    return r * (dyn - xn * jnp.mean(dyn * xn, axis=-1, keepdims=True))


def _colsum(x):
    return jnp.sum(x, axis=0, keepdims=True)


def _rope_tables(pos, invf):
    def fn(p, f):
        lane = lax.broadcasted_iota(jnp.int32, (1, LANES), 1)
        ang = p * f
        cs, sn = jnp.cos(ang), jnp.sin(ang)
        rot = (lane >= 64) & (lane < 96)
        ct = jnp.where(lane < 64, 1.0, jnp.where(rot, cs, 0.0))
        sa = jnp.where((lane >= 64) & (lane < 80), -sn, 0.0)
        sb = jnp.where((lane >= 80) & (lane < 96), sn, 0.0)
        return (ct, sa, sb), ()
    return _rowwise(fn, [(pos, 1, 0)], [invf], [(LANES, F32)] * 3, [], "rope_tables")


def _rope(x, ct, sa, sb):
    return x * ct + pltpu.roll(x, LANES - 16, 1) * sa + pltpu.roll(x, 16, 1) * sb


def _rope_t(x, ct, sa, sb):
    return x * ct - pltpu.roll(x, LANES - 16, 1) * sa - pltpu.roll(x, 16, 1) * sb


def _head_mask(width, hh):
    lane = lax.broadcasted_iota(jnp.int32, (1, width), 1)
    half = width // 2
    return (lane >= hh * half) & (lane < (hh + 1) * half)


ATT_PP = 2
ATT_CHAINS = [(a, hh) for a in range(ATT_PP) for hh in range(2)]
ATT_G = HEADS // (2 * ATT_PP)


def _pair(ref_or_val, a, width, rows=slice(None)):
    return ref_or_val[rows, a * width:(a + 1) * width]


def _attn_fwd(q, qo, k, ko, v, vo, dkp, scale, bias, name):
    T = ATT_T
    assert qo % ATT_PP == 0 and ko % ATT_PP == 0 and vo % ATT_PP == 0
    qo, ko, vo = qo // ATT_PP, ko // ATT_PP, vo // ATT_PP

    def body(*refs):
        if bias is not None:
            q_ref, k_ref, v_ref, b_ref, o_ref, lse_ref, s_scr = refs
        else:
            q_ref, k_ref, v_ref, o_ref, lse_ref, s_scr = refs
        i = pl.program_id(1)
        row = lax.broadcasted_iota(jnp.int32, (T, T), 0)
        col = lax.broadcasted_iota(jnp.int32, (T, T), 1)
        qms = []
        for a, hh in ATT_CHAINS:
            qb = _pair(q_ref, a, dkp)
            qms.append(jnp.where(_head_mask(dkp, hh), qb, jnp.zeros_like(qb)))

        def fold(t):
            return [t[:, c * LANES:(c + 1) * LANES] for c in range(T // LANES)]

        def run(nt):
            mls = [jnp.full((T, LANES), -jnp.inf, F32) for _ in ATT_CHAINS]
            for j in range(nt):
                ks = slice(j * T, (j + 1) * T)
                for ci, (a, hh) in enumerate(ATT_CHAINS):
                    s = lax.dot_general(qms[ci], _pair(k_ref, a, dkp, ks), NT, preferred_element_type=F32) * (scale * LOG2E)
                    if bias is not None:
                        s = s + b_ref[2 * a + hh, j] * LOG2E
                    if j == nt - 1:
                        s = jnp.where(row >= col, s, -jnp.inf)
                    s_scr[ci, j] = s
                    for part in fold(s):
                        mls[ci] = jnp.maximum(mls[ci], part)
            ms = [jnp.max(ml, axis=1, keepdims=True) for ml in mls]
            mbs = [jnp.broadcast_to(m, (T, LANES)) for m in ms]
            for a in range(ATT_PP):
                ls = [jnp.zeros((T, LANES), F32) for _ in range(2)]
                ps, vms = [], []
                for j in range(nt):
                    vb = _pair(v_ref, a, LANES, slice(j * T, (j + 1) * T))
                    for hh in range(2):
                        parts = [jnp.exp2(part - mbs[2 * a + hh]) for part in fold(s_scr[2 * a + hh, j])]
                        for part in parts:
                            ls[hh] = ls[hh] + part
                        ps.append(jnp.concatenate(parts, axis=1).astype(BF16))
                        vms.append(jnp.where(_head_mask(LANES, hh), vb, jnp.zeros_like(vb)))
                acc = lax.dot_general(jnp.concatenate(ps, axis=1), jnp.concatenate(vms, axis=0), NN,
                                      preferred_element_type=F32)
                l0, l1 = [jnp.sum(l, axis=1, keepdims=True) for l in ls]
                lse_ref[2 * a] = ms[2 * a] + jnp.log2(l0)
                lse_ref[2 * a + 1] = ms[2 * a + 1] + jnp.log2(l1)
                inv = jnp.where(_head_mask(LANES, 0), 1.0 / l0, 1.0 / l1)
                o_ref[:, a * LANES:(a + 1) * LANES] = (acc * inv).astype(o_ref.dtype)

        for nt in range(1, N_ATT + 1):
            pl.when(i == nt - 1)(functools.partial(run, nt))

    in_specs = [
        pl.BlockSpec((T, ATT_PP * dkp), lambda g, i: (i, qo + g)),
        pl.BlockSpec((S, ATT_PP * dkp), lambda g, i: (0, ko + g)),
        pl.BlockSpec((S, ATT_PP * LANES), lambda g, i: (0, vo + g)),
    ]
    args = [q, k, v]
    if bias is not None:
        in_specs.append(pl.BlockSpec((2 * ATT_PP, N_ATT, 1, T), lambda g, i: (g, 0, 0, 0)))
        args.append(bias)
    return pl.pallas_call(
        body, name=name, grid=(ATT_G, N_ATT),
        in_specs=in_specs,
        out_specs=[pl.BlockSpec((T, ATT_PP * LANES), lambda g, i: (i, g)),
                   pl.BlockSpec((2 * ATT_PP, T, 1), lambda g, i: (g, i, 0))],
        out_shape=[jax.ShapeDtypeStruct((S, HEADS * HEAD_DIM), BF16),
                   jax.ShapeDtypeStruct((HEADS, S, 1), F32)],
        scratch_shapes=[pltpu.VMEM((len(ATT_CHAINS), N_ATT, T, T), F32)],
        compiler_params=_params(("parallel", "arbitrary")),
    )(*args)


def _attn_grad(q, qo, k, ko, v, vo, do, lse, dkp, scale, bias, qk_dtype, name):
    T = ATT_T
    has_b = bias is not None
    qo, ko, vo = qo // ATT_PP, ko // ATT_PP, vo // ATT_PP
    n_ch = len(ATT_CHAINS)

    def body(*refs):
        q_ref, k_ref, v_ref, do_ref, lse_ref = refs[:5]
        refs = refs[5:]
        if has_b:
            b_ref, refs = refs[0], refs[1:]
        dq_ref, dk_ref, dv_ref = refs[:3]
        refs = refs[3:]
        if has_b:
            db_ref, refs = refs[0], refs[1:]
        p_scr, dp_scr, dk_acc, dv_acc = refs[:4]
        db_acc = refs[4] if has_b else None
        i = pl.program_id(1)

        @pl.when(i == 0)
        def _():
            dk_acc[...] = jnp.zeros(dk_acc.shape, F32)
            dv_acc[...] = jnp.zeros(dv_acc.shape, F32)
            if has_b:
                db_acc[...] = jnp.zeros(db_acc.shape, F32)

        row = lax.broadcasted_iota(jnp.int32, (T, T), 0)
        col = lax.broadcasted_iota(jnp.int32, (T, T), 1)

        def fold(t):
            return [t[:, c * LANES:(c + 1) * LANES] for c in range(T // LANES)]

        qms, doms, lses = [], [], []
        for a, hh in ATT_CHAINS:
            qb, dob = _pair(q_ref, a, dkp), _pair(do_ref, a, LANES)
            qms.append(jnp.where(_head_mask(dkp, hh), qb, jnp.zeros_like(qb)))
            doms.append(jnp.where(_head_mask(LANES, hh), dob, jnp.zeros_like(dob)))
            lses.append(lse_ref[2 * a + hh])

        def run(nt):
            dls = [jnp.zeros((T, LANES), F32) for _ in ATT_CHAINS]
            for j in range(nt):
                ks = slice(j * T, (j + 1) * T)
                for ci, (a, hh) in enumerate(ATT_CHAINS):
                    s = lax.dot_general(qms[ci], _pair(k_ref, a, dkp, ks), NT, preferred_element_type=F32) * (scale * LOG2E)
                    if has_b:
                        s = s + b_ref[ci, j] * LOG2E
                    s = s - lses[ci]
                    if j == nt - 1:
                        s = jnp.where(row >= col, s, -jnp.inf)
                    p = jnp.exp2(s)
                    dp = lax.dot_general(doms[ci], _pair(v_ref, a, LANES, ks), NT, preferred_element_type=F32)
                    p_scr[ci, j] = p
                    dp_scr[ci, j] = dp
                    for part in fold(p * dp):
                        dls[ci] = dls[ci] + part
            deltas = [jnp.broadcast_to(jnp.sum(dl, axis=1, keepdims=True), (T, LANES)) for dl in dls]
            for a in range(ATT_PP):
                ds_all, km_all = [], []
                qm2t = jnp.transpose(jnp.concatenate([qms[2 * a], qms[2 * a + 1]], axis=0))
                dom2t = jnp.transpose(jnp.concatenate([doms[2 * a], doms[2 * a + 1]], axis=0))
                for j in range(nt):
                    ks = slice(j * T, (j + 1) * T)
                    kb = _pair(k_ref, a, dkp, ks)
                    p2, ds2 = [], []
                    for hh in range(2):
                        ci = 2 * a + hh
                        p = p_scr[ci, j]
                        ds = jnp.concatenate([pp * (dd - deltas[ci]) for pp, dd in zip(fold(p), fold(dp_scr[ci, j]))], axis=1)
                        if has_b:
                            db_acc[ci, j] += jnp.sum(ds, axis=0, keepdims=True)
                        p2.append(p.astype(BF16))
                        ds2.append((ds * scale).astype(BF16))
                        km_all.append(jnp.where(_head_mask(dkp, hh), kb, jnp.zeros_like(kb)))
                    dv_acc[a * LANES:(a + 1) * LANES, ks] += lax.dot_general(
                        dom2t, jnp.concatenate(p2, axis=0), NN, preferred_element_type=F32)
                    dk_acc[a * dkp:(a + 1) * dkp, ks] += lax.dot_general(
                        qm2t, jnp.concatenate(ds2, axis=0), NN, preferred_element_type=F32)
                    ds_all += ds2
                dq = lax.dot_general(jnp.concatenate(ds_all, axis=1), jnp.concatenate(km_all, axis=0), NN,
                                     preferred_element_type=F32)
                dq_ref[:, a * dkp:(a + 1) * dkp] = dq.astype(dq_ref.dtype)

        for nt in range(1, N_ATT + 1):
            pl.when(i == nt - 1)(functools.partial(run, nt))

        @pl.when(i == N_ATT - 1)
        def _():
            dk_ref[...] = jnp.transpose(dk_acc[...]).astype(dk_ref.dtype)
            dv_ref[...] = jnp.transpose(dv_acc[...]).astype(dv_ref.dtype)
            if has_b:
                db_ref[...] = db_acc[...]

    in_specs = [
        pl.BlockSpec((T, ATT_PP * dkp), lambda g, i: (i, qo + g)),
        pl.BlockSpec((S, ATT_PP * dkp), lambda g, i: (0, ko + g)),
        pl.BlockSpec((S, ATT_PP * LANES), lambda g, i: (0, vo + g)),
        pl.BlockSpec((T, ATT_PP * LANES), lambda g, i: (i, g)),
        pl.BlockSpec((2 * ATT_PP, T, 1), lambda g, i: (g, i, 0)),
    ]
    args = [q, k, v, do, lse]
    out_specs = [
        pl.BlockSpec((T, ATT_PP * dkp), lambda g, i: (i, g)),
        pl.BlockSpec((S, ATT_PP * dkp), lambda g, i: (0, g)),
        pl.BlockSpec((S, ATT_PP * LANES), lambda g, i: (0, g)),
    ]
    width = (HEADS // 2) * dkp
    out_shape = [
        jax.ShapeDtypeStruct((S, width), qk_dtype),
        jax.ShapeDtypeStruct((S, width), qk_dtype),
        jax.ShapeDtypeStruct((S, HEADS * HEAD_DIM), BF16),
    ]
    scratch = [pltpu.VMEM((n_ch, N_ATT, T, T), F32), pltpu.VMEM((n_ch, N_ATT, T, T), F32),
               pltpu.VMEM((ATT_PP * dkp, S), F32), pltpu.VMEM((ATT_PP * LANES, S), F32)]
    if has_b:
        bspec = pl.BlockSpec((2 * ATT_PP, N_ATT, 1, T), lambda g, i: (g, 0, 0, 0))
        in_specs.append(bspec)
        args.append(bias)
        out_specs.append(bspec)
        out_shape.append(jax.ShapeDtypeStruct((HEADS, N_ATT, 1, T), F32))
        scratch.append(pltpu.VMEM((2 * ATT_PP, N_ATT, 1, T), F32))
    return pl.pallas_call(
        body, name=name, grid=(ATT_G, N_ATT),
        in_specs=in_specs, out_specs=out_specs, out_shape=out_shape, scratch_shapes=scratch,
        compiler_params=_params(("parallel", "arbitrary")),
    )(*args)


def _tri(upper):
    a = lax.broadcasted_iota(jnp.int32, (LANES, LANES), 0)
    b = lax.broadcasted_iota(jnp.int32, (LANES, LANES), 1)
    return jnp.where(a <= b if upper else a >= b, 1.0, 0.0).astype(F32)


def _fox_gates(proj, blk, bf):
    def body(m_ref, b_ref, z_out, o_ref):
        tri = _tri(True)
        carry = jnp.zeros((HEADS, 1), F32)
        for t in range(S // LANES):
            sl = slice(t * LANES, (t + 1) * LANES)
            zt = jnp.transpose(m_ref[sl, :])[:HEADS]
            z_out[:, sl] = zt
            z = zt + b_ref[...]
            logf = jnp.minimum(z, 0.0) - jnp.log(1.0 + jnp.exp(-jnp.abs(z)))
            c = lax.dot_general(logf, tri, NN, preferred_element_type=F32,
                                precision=lax.Precision.HIGHEST) + carry
            o_ref[:, sl] = -c
            carry = c[:, LANES - 1:LANES]

    return pl.pallas_call(
        body, name="fox_gates", grid=(1,),
        in_specs=[pl.BlockSpec((S, LANES), lambda i: (0, blk)), pl.BlockSpec(bf.shape, lambda i: (0, 0))],
        out_specs=[pl.BlockSpec((HEADS, S), lambda i: (0, 0))] * 2,
        out_shape=[jax.ShapeDtypeStruct((HEADS, S), F32)] * 2,
        compiler_params=_params(("arbitrary",)),
    )(proj, bf)


def _fox_gates_bwd(dbias, zt, bf):
    def body(d_ref, z_ref, b_ref, dz_ref, dbf_ref):
        tri = _tri(False)
        carry = jnp.zeros((HEADS, 1), F32)
        tot = jnp.zeros((HEADS, 1), F32)
        for t in reversed(range(S // LANES)):
            sl = slice(t * LANES, (t + 1) * LANES)
            df = -d_ref[:, sl]
            c = lax.dot_general(df, tri, NN, preferred_element_type=F32,
                                precision=lax.Precision.HIGHEST) + carry
            carry = c[:, 0:1]
            z = z_ref[:, sl] + b_ref[...]
            dz = c * _sigmoid(-z)
            dz_ref[:, sl] = dz
            tot = tot + jnp.sum(dz, axis=1, keepdims=True)
        dbf_ref[...] = tot

    return pl.pallas_call(
        body, name="fox_gates_bwd",
        out_shape=[jax.ShapeDtypeStruct((HEADS, S), F32), jax.ShapeDtypeStruct((HEADS, 1), F32)],
        compiler_params=_params(),
    )(dbias, zt, bf)


def _mod_part(c_all, w_ada, b_cols):
    def body(c_ref, w_ref, b_ref, o_ref, s_ref):
        c = c_ref[...]
        sc = c * _sigmoid(c)
        s_ref[...] = sc
        o_ref[...] = lax.dot_general(sc, w_ref[...], NN, preferred_element_type=F32,
                                     precision=lax.Precision.HIGHEST) + b_ref[...]

    return pl.pallas_call(
        body, name="mod_part",
        out_shape=[jax.ShapeDtypeStruct((N_DEV, w_ada.shape[1]), F32), jax.ShapeDtypeStruct(c_all.shape, F32)],
        compiler_params=_params(),
    )(c_all, w_ada, b_cols)


def _adamw_w_ada(w, m, v, sc_t, dm):
    rows, cols = w.shape
    tr = 256

    def body(w_ref, m_ref, v_ref, s_ref, d_ref, g_out, d_out, m_out, v_out):
        g = s_ref[:, 0:1] * d_ref[0:1, :]
        for b in range(1, N_DEV):
            g = g + s_ref[:, b:b + 1] * d_ref[b:b + 1, :]
        g_out[...] = g
        d_out[...], m_out[...], v_out[...] = _adamw_math(w_ref[...], g, m_ref[...], v_ref[...])

    spec = pl.BlockSpec((tr, cols), lambda i: (i, 0))
    return pl.pallas_call(
        body, name="adamw_w_ada", grid=(rows // tr,),
        in_specs=[spec, spec, spec, pl.BlockSpec((tr, N_DEV), lambda i: (i, 0)), pl.BlockSpec(dm.shape, lambda i: (0, 0))],
        out_specs=[spec] * 4, out_shape=[jax.ShapeDtypeStruct((rows, cols), F32)] * 4,
        compiler_params=_params(("parallel",)),
    )(w, m, v, sc_t, dm)


def _adamw(w, m, v, parts, name, own=None, slot=None):
    rows, cols = w.shape
    n = parts.shape[0]
    by_cols = rows % 256 != 0 and cols % 256 == 0
    tr, tc = (rows, 256) if by_cols else ((rows if rows <= 512 else 256), cols)
    tile = (lambda i: (0, i)) if by_cols else (lambda i: (i, 0))

    def body(*refs):
        if own is not None:
            s_ref, refs = refs[0], refs[1:]
            w_ref, m_ref, v_ref, p_ref, o_ref, g_out, d_out, m_out, v_out = refs
            terms = [jnp.where(s_ref[0] == kk, o_ref[0], p_ref[kk]) for kk in range(n)]
        else:
            w_ref, m_ref, v_ref, p_ref, g_out, d_out, m_out, v_out = refs
            terms = [p_ref[kk] for kk in range(n)]
        g = terms[0].astype(F32)
        for term in terms[1:]:
            g = g + term.astype(F32)
        g_out[...] = g
        d_out[...], m_out[...], v_out[...] = _adamw_math(w_ref[...], g, m_ref[...], v_ref[...])

    spec = pl.BlockSpec((tr, tc), lambda i, *_: tile(i))
    in_specs = [spec, spec, spec, pl.BlockSpec((n, tr, tc), lambda i, *_: (0,) + tile(i))]
    out_shape = [jax.ShapeDtypeStruct((rows, cols), F32)] * 4
    grid = (rows // tr if not by_cols else cols // tc,)
    if own is None:
        return pl.pallas_call(
            body, name=name, grid=grid, in_specs=in_specs, out_specs=[spec] * 4, out_shape=out_shape,
            compiler_params=_params(("parallel",)),
        )(w, m, v, parts)
    in_specs.append(pl.BlockSpec((1, tr, tc), lambda i, s: (s[0],) + tile(i)))
    return pl.pallas_call(
        body, name=name, out_shape=out_shape, compiler_params=_params(("parallel",)),
        grid_spec=pltpu.PrefetchScalarGridSpec(num_scalar_prefetch=1, grid=grid, in_specs=in_specs,
                                               out_specs=[spec] * 4),
    )(slot, w, m, v, parts, own)


def _adamw_math(w, g, m, v):
    mm = ADAM_B1 * m + (1.0 - ADAM_B1) * g
    vv = ADAM_B2 * v + (1.0 - ADAM_B2) * (g * g)
    m_hat = mm / (1.0 - ADAM_B1 ** ADAM_STEP)
    v_hat = vv / (1.0 - ADAM_B2 ** ADAM_STEP)
    return -ADAM_LR * (m_hat / (jnp.sqrt(v_hat) + ADAM_EPS) + ADAM_WD * w), mm, vv


def _adamw_rows(bundles, offsets, ws, ms, vs, err_off, err_width):
    k = len(ws)

    def body(*refs):
        b_ref = refs[0]
        w_refs, m_refs, v_refs = refs[1:1 + k], refs[1 + k:1 + 2 * k], refs[1 + 2 * k:1 + 3 * k]
        outs = refs[1 + 3 * k:]
        g_all = b_ref[0]
        for kk in range(1, N_DEV):
            g_all = g_all + b_ref[kk]
        for i in range(k):
            width = w_refs[i].shape[1]
            g = g_all[:, offsets[i]:offsets[i] + width]
            outs[4 * i][...] = g
            outs[4 * i + 1][...], outs[4 * i + 2][...], outs[4 * i + 3][...] = _adamw_math(
                w_refs[i][...], g, m_refs[i][...], v_refs[i][...])
        outs[4 * k][...] = g_all[:, err_off:err_off + err_width]

    out_shape = []
    for w_ in ws:
        out_shape += [jax.ShapeDtypeStruct(w_.shape, F32)] * 4
    out_shape.append(jax.ShapeDtypeStruct((1, err_width), F32))
    res = pl.pallas_call(body, name="adamw_rows", out_shape=out_shape, compiler_params=_params())(bundles, *ws, *ms, *vs)
    return [tuple(res[4 * i:4 * i + 4]) for i in range(k)], res[-1]


def _coords():
    return lax.axis_index("x"), lax.axis_index("y"), lax.axis_index("c")


def _flat(px, py, pc):
    return 4 * px + 2 * py + pc


def _all_gather(arrs, name):
    n = len(arrs)

    def body(*refs):
        ins, outs = refs[:n], refs[n:2 * n]
        send, recv, lsem = refs[2 * n:]
        x, y, c = _coords()
        me, sibling = (x, y, c), (x, y, 1 - c)
        chips = [(1 - x, y), (x, 1 - y), (1 - x, 1 - y)]

        def copy(a, kk, block, to, src=None):
            slot = outs[a].at[_flat(*block)]
            return pltpu.make_async_remote_copy(
                src_ref=slot if src is None else src, dst_ref=slot,
                send_sem=send.at[a, kk], recv_sem=recv.at[a, kk],
                device_id=to, device_id_type=MESH)

        mine = [pltpu.make_async_copy(ins[a], outs[a].at[_flat(*me)], lsem.at[a]) for a in range(n)]
        for cp in mine:
            cp.start()
        first = []
        for a in range(n):
            first.append(copy(a, 0, me, sibling, src=ins[a]))
            first += [copy(a, 1 + j, me, (*chip, c), src=ins[a]) for j, chip in enumerate(chips)]
        for cp in first:
            cp.start()
        passed = []
        for j, chip in enumerate(chips):
            for a in range(n):
                copy(a, 1 + j, (*chip, c), me).wait_recv()
                cp = copy(a, 4 + j, (*chip, c), sibling)
                cp.start()
                passed.append(cp)
        for a in range(n):
            copy(a, 0, sibling, me).wait_recv()
        for j, chip in enumerate(chips):
            for a in range(n):
                copy(a, 4 + j, (*chip, 1 - c), me).wait_recv()
        for cp in first + passed:
            cp.wait_send()
        for cp in mine:
            cp.wait()

    any_spec = pl.BlockSpec(memory_space=pl.ANY)
    return pl.pallas_call(
        body, name=name,
        in_specs=[any_spec] * n, out_specs=[any_spec] * n,
        out_shape=[jax.ShapeDtypeStruct((N_DEV,) + a.shape, a.dtype) for a in arrs],
        scratch_shapes=[pltpu.SemaphoreType.DMA((n, 7)), pltpu.SemaphoreType.DMA((n, 7)),
                        pltpu.SemaphoreType.DMA((n,))],
    )(*arrs)


def _peer_list():
    x, y, c = _coords()
    return [((1 - x if r & 4 else x), (1 - y if r & 2 else y), (1 - c if r & 1 else c)) for r in range(1, N_DEV)]


def _copy_plan(mode, src, land):
    x, y, c = _coords()
    me = _flat(x, y, c)
    if mode == "gather":
        return [(src, land.at[me], peer) for peer in _peer_list()]
    if mode == "exchange":
        return [(src.at[_flat(*peer)], land.at[me], peer) for peer in _peer_list()]
    if mode == "pair":
        return [(src.at[_flat(q // 2, q % 2, 1 - c)], land.at[q], (x, y, 1 - c)) for q in range(N_DEV // 2)]
    chips = [((1 - x if r & 2 else x), (1 - y if r & 1 else y)) for r in range(1, N_DEV // 2)]
    if mode == "chips":
        return [(src.at[2 * qx + qy], land.at[2 * x + y], (qx, qy, c)) for qx, qy in chips]
    if mode == "spread":
        return [(src, land.at[me], (x, y, 1 - c))] + [(src, land.at[me], (qx, qy, c)) for qx, qy in chips]
    assert mode == "forward"
    return [(land.at[_flat(qx, qy, c)], land.at[_flat(qx, qy, c)], (x, y, 1 - c)) for qx, qy in chips]


N_COPIES = dict(gather=N_DEV - 1, exchange=N_DEV - 1, pair=N_DEV // 2, chips=N_DEV // 2 - 1, spread=N_DEV // 2,
                forward=N_DEV // 2 - 1)


def _land_shape(mode, shape):
    return {"gather": (N_DEV,) + shape, "spread": (N_DEV,) + shape, "exchange": shape,
            "pair": (N_DEV // 2,) + shape[1:], "chips": shape}[mode]


HBM_SPEC = pl.BlockSpec(memory_space=pltpu.HBM)
SEM_SPEC = pl.BlockSpec(memory_space=pltpu.SEMAPHORE)
ANY_SPEC = pl.BlockSpec(memory_space=pl.ANY)
SIDE_EFFECT = pltpu.SideEffectType.DATAFLOW_SIDE_EFFECTING


def _async_start(groups, modes, after, name):
    modes = [modes] * len(groups) if isinstance(modes, str) else list(modes)
    arrs = [(a, m) for g, m in zip(groups, modes) for a in g]
    n = len(arrs)
    fresh = [i for i, (_, m) in enumerate(arrs) if m != "forward"]

    def body(*refs):
        srcs, new_lands = refs[:n], refs[n:n + len(fresh)]
        outs = refs[n + len(fresh) + 1:]
        lands = list(srcs)
        for k, i in enumerate(fresh):
            lands[i] = new_lands[k]
        for ai, (_, mode) in enumerate(arrs):
            for src_ref, dst_ref, peer in _copy_plan(mode, srcs[ai], lands[ai]):
                pltpu.make_async_remote_copy(src_ref=src_ref, dst_ref=dst_ref, send_sem=outs[2 * ai],
                                             recv_sem=outs[2 * ai + 1], device_id=peer, device_id_type=MESH).start()
        outs[-1][...] = jnp.zeros(outs[-1].shape, F32)

    land_shapes = [(_land_shape(arrs[i][1], arrs[i][0].shape), arrs[i][0].dtype) for i in fresh]
    n_buf = n + len(fresh)
    out_shape = [pltpu.SemaphoreType.DMA(())] * (2 * n)
    out_shape += [pltpu.HBM(a.shape, a.dtype) for a, _ in arrs]
    out_shape += [pltpu.HBM(shape, dt) for shape, dt in land_shapes]
    out_shape.append(jax.ShapeDtypeStruct((8, LANES), F32))
    res = pl.pallas_call(
        body, name=name, out_shape=tuple(out_shape),
        in_specs=[HBM_SPEC] * n_buf + [ANY_SPEC],
        out_specs=tuple([SEM_SPEC] * (2 * n) + [HBM_SPEC] * n_buf + [pl.BlockSpec(memory_space=pltpu.VMEM)]),
        input_output_aliases={i: 2 * n + i for i in range(n_buf)},
        compiler_params=pltpu.CompilerParams(has_side_effects=SIDE_EFFECT),
    )(*[pltpu.with_memory_space_constraint(a, pltpu.HBM) for a, _ in arrs],
      *[pltpu.with_memory_space_constraint(lax.empty(shape, dt), pltpu.HBM) for shape, dt in land_shapes],
      after)
    sems, thru = res[:2 * n], res[2 * n:-1]
    land_of = {i: thru[n + k] for k, i in enumerate(fresh)}
    states, idx = [], 0
    for g, mode in zip(groups, modes):
        ids = range(idx, idx + len(g))
        idx += len(g)
        states.append(([sems[2 * i] for i in ids], [sems[2 * i + 1] for i in ids],
                       None if mode == "forward" else [thru[i] for i in ids],
                       [land_of.get(i, thru[i]) for i in ids], mode))
    return states, res[-1]


def _async_wait(state, after, name):
    sends, recvs, srcs, lands, mode = state
    g = len(lands)
    bufs = (list(srcs) if srcs is not None else []) + list(lands)
    nb = len(bufs)

    def body(*refs):
        l_refs, sems = refs[nb - g:nb], refs[nb:nb + 2 * g]
        for ai in range(g):
            moved = l_refs[ai].at[pl.ds(0, N_COPIES[mode])]
            cp = pltpu.make_async_remote_copy(src_ref=moved, dst_ref=moved, send_sem=sems[ai], recv_sem=sems[g + ai],
                                              device_id=_coords(), device_id_type=MESH)
            cp.wait_send()
            cp.wait_recv()

    res = pl.pallas_call(
        body, name=name,
        out_shape=tuple(pltpu.HBM(a.shape, a.dtype) for a in bufs),
        in_specs=[HBM_SPEC] * nb + [SEM_SPEC] * (2 * g) + [ANY_SPEC],
        out_specs=tuple([HBM_SPEC] * nb),
        input_output_aliases={i: i for i in range(nb)},
        compiler_params=pltpu.CompilerParams(has_side_effects=SIDE_EFFECT),
    )(*bufs, *sends, *recvs, after)
    return (list(res[:nb - g]) if srcs is not None else None), list(res[nb - g:])


def _add_sibling(mine, theirs, core):
    def body(c_ref, a_ref, b_ref, o_ref):
        o_ref[...] = (a_ref[...].astype(F32) + b_ref[...].astype(F32)).astype(o_ref.dtype)

    blk = (1,) + mine.shape[1:]
    return pl.pallas_call(
        body, name="add_sibling", out_shape=jax.ShapeDtypeStruct(theirs.shape, mine.dtype),
        grid_spec=pltpu.PrefetchScalarGridSpec(
            num_scalar_prefetch=1, grid=(theirs.shape[0],),
            in_specs=[pl.BlockSpec(blk, lambda q, c: (2 * q + c[0], 0, 0)), pl.BlockSpec(blk, lambda q, c: (q, 0, 0))],
            out_specs=pl.BlockSpec(blk, lambda q, c: (q, 0, 0))),
        compiler_params=_params(("parallel",)),
    )(core, mine, theirs)


def _with_own(land, own, me):
    return lax.dynamic_update_index_in_dim(land, own, me, 0)


IN_SPLITS = (512, 512, 512, 8, 768, 256, 32, 1024, 1024)


def _from_shards(g, fn, out_widths, name, own=None, slot=None):
    _, k, n = g.shape
    tr = min(k, 256)

    def body(*refs):
        if own is not None:
            s_ref, g_ref, own_ref = refs[:3]
            cols = [jnp.where(s_ref[0] == j, own_ref[...], g_ref[j]) for j in range(N_DEV)]
        else:
            g_ref = refs[0]
            cols = [g_ref[j] for j in range(N_DEV)]
        for o_ref, val in zip(refs[-len(out_widths):], fn(jnp.concatenate(cols, axis=1))):
            o_ref[...] = val

    in_specs = [pl.BlockSpec((N_DEV, tr, n), lambda i, *_: (0, i, 0))]
    out_spec = [pl.BlockSpec((tr, wd), lambda i, *_: (i, 0)) for wd in out_widths]
    out_shape = [jax.ShapeDtypeStruct((k, wd), g.dtype) for wd in out_widths]
    if own is None:
        return pl.pallas_call(body, name=name, grid=(k // tr,), in_specs=in_specs, out_specs=out_spec,
                              out_shape=out_shape, compiler_params=_params(("parallel",)))(g)
    in_specs.append(pl.BlockSpec((tr, n), lambda i, *_: (i, 0)))
    return pl.pallas_call(
        body, name=name, out_shape=out_shape, compiler_params=_params(("parallel",)),
        grid_spec=pltpu.PrefetchScalarGridSpec(num_scalar_prefetch=1, grid=(k // tr,), in_specs=in_specs, out_specs=out_spec),
    )(slot, g, own)


def _unshard_cols(g, own=None, slot=None):
    return _from_shards(g, lambda full: (full,), [N_DEV * g.shape[2]], "unshard_cols_%d" % g.shape[2], own, slot)[0]


FFN_T = 256
FFN_SHARD = 2 * D_FF // N_DEV


def _unshard_ffn_in(g, own=None, slot=None):
    def pairs(full):
        parts = []
        for j in range(D_FF // FFN_T):
            parts += [full[:, j * FFN_T:(j + 1) * FFN_T], full[:, D_FF + j * FFN_T:D_FF + (j + 1) * FFN_T]]
        return (jnp.concatenate(parts, axis=1),)

    return _from_shards(g, pairs, [2 * D_FF], "unshard_ffn_in", own, slot)[0]


def _shard_ffn_in_t(wt):
    tc = 256

    def body(w_ref, o_ref):
        x = w_ref[...]
        nb = D_FF // FFN_T
        full = jnp.concatenate([x[(2 * j + half) * FFN_T:(2 * j + half + 1) * FFN_T]
                                for half in range(2) for j in range(nb)], axis=0)
        for j in range(N_DEV):
            o_ref[j] = full[j * FFN_SHARD:(j + 1) * FFN_SHARD]

    return pl.pallas_call(
        body, name="shard_ffn_in_t", grid=(D // tc,),
        in_specs=[pl.BlockSpec((2 * D_FF, tc), lambda i: (0, i))],
        out_specs=pl.BlockSpec((N_DEV, FFN_SHARD, tc), lambda i: (0, 0, i)),
        out_shape=jax.ShapeDtypeStruct((N_DEV, FFN_SHARD, D), wt.dtype),
        compiler_params=_params(("parallel",)),
    )(wt)


def _shard_cols(w):
    k, n = w.shape[0], w.shape[1] // N_DEV
    tr = min(k, 256)

    def body(w_ref, o_ref):
        full = w_ref[...]
        for j in range(N_DEV):
            o_ref[j] = full[:, j * n:(j + 1) * n]

    return pl.pallas_call(
        body, name="shard_cols_%d" % n, grid=(k // tr,),
        in_specs=[pl.BlockSpec((tr, N_DEV * n), lambda i: (i, 0))],
        out_specs=pl.BlockSpec((N_DEV, tr, n), lambda i: (0, i, 0)),
        out_shape=jax.ShapeDtypeStruct((N_DEV, k, n), w.dtype),
        compiler_params=_params(("parallel",)),
    )(w)


IN_OFFS = tuple(sum(IN_SPLITS[:i]) for i in range(len(IN_SPLITS) + 1))
IN_SHARD = IN_OFFS[-1] // N_DEV
REGROUP_ROWS = 128
MISC_AT = Q_LORA + KV_LORA + 2 * D
A_COLS = MISC_AT + LANES
A_TILE = A_COLS // 5
B_COLS = 3 * HEADS * HEAD_DIM
MISC_BLOCK = MISC_AT // LANES
KR_AT = 64


def _w_in_regroup(g, own=None, slot=None):
    def groups(full):
        fq, fk, fv, wf, cq, ckv, kr, gf, gm = [full[:, IN_OFFS[i]:IN_OFFS[i + 1]] for i in range(9)]
        rows = full.shape[0]
        gap = jnp.zeros((rows, KR_AT - HEADS), BF16)
        tail = jnp.zeros((rows, LANES - KR_AT - ROPE_DIM), BF16)
        return jnp.concatenate([cq, ckv, gf, gm, wf, gap, kr, tail], axis=1), jnp.concatenate([fq, fk, fv], axis=1)

    return _from_shards(g, groups, [A_COLS, B_COLS], "w_in_regroup", own, slot)


def _w_in_ungroup(da, db_):
    def body(a_ref, b_ref, o_ref):
        a = a_ref[...]
        lora = Q_LORA + KV_LORA
        full = jnp.concatenate([b_ref[...], a[:, MISC_AT:MISC_AT + HEADS], a[:, :lora],
                                a[:, MISC_AT + KR_AT:MISC_AT + KR_AT + ROPE_DIM], a[:, lora:MISC_AT]], axis=1)
        for j in range(N_DEV):
            o_ref[j] = full[:, j * IN_SHARD:(j + 1) * IN_SHARD]

    tr = REGROUP_ROWS
    return pl.pallas_call(
        body, name="w_in_ungroup", grid=(D // tr,),
        in_specs=[pl.BlockSpec((tr, A_COLS), lambda i: (i, 0)), pl.BlockSpec((tr, B_COLS), lambda i: (i, 0))],
        out_specs=pl.BlockSpec((N_DEV, tr, IN_SHARD), lambda i: (0, i, 0)),
        out_shape=jax.ShapeDtypeStruct((N_DEV, D, IN_SHARD), BF16),
        compiler_params=_params(("parallel",)),
    )(da, db_)


def _prepare_weights(g, own=None, slot=None):
    w = {}
    if own is not None:
        small = ("w_uq", "w_ukv", "w_out", "w_ffn_out")
        g = {n: (_with_own(a, own[n], slot[0]) if n in small else a) for n, a in g.items()}
    pick = (lambda n: (own[n], slot)) if own is not None else (lambda n: (None, None))
    if "w_in" in g:
        w["w_a"], w["w_b"] = _w_in_regroup(g["w_in"], *pick("w_in"))
    if "w_uq" in g:
        w_uq = g["w_uq"].reshape(Q_LORA, HEADS, 96)
        w["w_uq"] = jnp.pad(w_uq, ((0, 0), (0, 0), (0, 32))).reshape(Q_LORA, HEADS * LANES)
        ukv = g["w_ukv"]
        w["w_k"] = jnp.transpose(jnp.pad(ukv[:, :, :64], ((0, 0), (0, 0), (0, 64))), (1, 0, 2)).reshape(KV_LORA, HEADS * LANES)
        w["w_v"] = jnp.transpose(ukv[:, :, 64:], (1, 0, 2)).reshape(KV_LORA, HEADS * HEAD_DIM)
    if "w_out" in g:
        w["w_pf"] = _unshard_cols(g["w_proj_fox"], *pick("w_proj_fox"))
        w["w_pm"] = _unshard_cols(g["w_proj_mla"], *pick("w_proj_mla"))
        w["w_out"] = g["w_out"].reshape(D, D)
    if "w_ffn_in" in g:
        w["w_ffn_in"] = _unshard_ffn_in(g["w_ffn_in"], *pick("w_ffn_in"))
        w["w_ffn_out"] = g["w_ffn_out"].reshape(D_FF, D)
    return w


def _shard_grads(dw):
    out = {}
    if "w_a" in dw:
        out["w_in"] = _w_in_ungroup(dw["w_a"], dw["w_b"])
    if "w_uq" in dw:
        w_uq = dw["w_uq"].reshape(Q_LORA, HEADS, LANES)[:, :, :96].reshape(Q_LORA, Q_LORA)
        out["w_uq"] = w_uq.reshape(N_DEV, Q_LORA // N_DEV, Q_LORA)
        k_part = dw["w_k"].reshape(KV_LORA, HEADS, LANES)[:, :, :64]
        v_part = dw["w_v"].reshape(KV_LORA, HEADS, HEAD_DIM)
        out["w_ukv"] = jnp.transpose(jnp.concatenate([k_part, v_part], axis=2), (1, 0, 2))
    if "w_out" in dw:
        out["w_proj_fox"] = _shard_cols(dw["w_pf"])
        out["w_proj_mla"] = _shard_cols(dw["w_pm"])
        out["w_out"] = dw["w_out"].reshape(N_DEV, D // N_DEV, D)
    if "w_ffn_in" in dw:
        out["w_ffn_in"] = _shard_ffn_in_t(dw["w_ffn_in"])
        out["w_ffn_out"] = dw["w_ffn_out"].reshape(N_DEV, D_FF // N_DEV, D)
    return out


def _fwd_bwd(x, pos, mod, target, w, vec, wts, send, relay):
    shift_mix, scale_mix, gate_mix, shift_ffn, scale_ffn, gate_ffn = [mod[:, i * D:(i + 1) * D] for i in range(6)]
    g_pre_mix, g_post_mix, g_pre_ffn, g_post_ffn = vec["g_pre_mix"], vec["g_post_mix"], vec["g_pre_ffn"], vec["g_post_ffn"]
    g_q, g_kv = vec["g_q_lora"], vec["g_kv_lora"]

    inv_freq = 1.0 / (ROPE_THETA ** (jnp.arange(0, ROPE_DIM, 2, dtype=F32) / ROPE_DIM))
    invf = jnp.concatenate([jnp.zeros((64,), F32), inv_freq, inv_freq, jnp.zeros((32,), F32)]).reshape(1, LANES)
    ct, sa, sb = _rope_tables(pos, invf)

    def pre1(xv, g, sc, sh):
        return (xv * _rstd(xv) * g) * (1.0 + sc) + sh
    proj_a, h = _mm_epi(x, w["w_a"], "nn", A_TILE, lambda r: ((r,), ()), "in_proj_a", 1024, outs=[(A_COLS, A_TILE, F32)],
                        pro=(pre1, [g_pre_mix, scale_mix, shift_mix], 0))
    qkv = _mm(h, w["w_b"], "nn", BF16, "in_proj_b")

    def lora_norm(cv, g):
        return cv * _rstd(cv) * g
    w = {**w, **wts("lora", qkv)}
    tables = [(ct, LANES), (sa, LANES), (sb, LANES)]

    def rope_q(qv, c_, a_, b_):
        return (jnp.concatenate([_rope(qv[:, hd * LANES:(hd + 1) * LANES], c_, a_, b_) for hd in range(HEADS)], axis=1),), ()
    q_m, cqn = _mm_epi(proj_a, w["w_uq"], "nn", D, rope_q, "mla_uq", 512, rows=tables, outs=[(D, D, BF16)],
                       pro=(lora_norm, [g_q], 0))

    def rope_k(kv, misc, c_, a_, b_):
        lane = lax.broadcasted_iota(jnp.int32, (1, LANES), 1)
        kpe = jnp.where((lane >= 64) & (lane < 96), _rope(misc, c_, a_, b_), 0.0)
        return (jnp.concatenate([kv[:, hd * LANES:(hd + 1) * LANES] + kpe for hd in range(HEADS)], axis=1),), ()
    k_m, ckvn = _mm_epi(proj_a, w["w_k"], "nn", D, rope_k, "mla_uk", 512, rows=[(proj_a, LANES, MISC_BLOCK)] + tables,
                        outs=[(D, D, BF16)], pro=(lora_norm, [g_kv], Q_LORA // KV_LORA))
    v_m = _mm(ckvn, w["w_v"], "nn", BF16, "mla_uv")

    bf = jnp.transpose(vec["b_forget"])
    zt, neg_f = _fox_gates(proj_a, MISC_BLOCK, bf)
    bias = neg_f.reshape(HEADS, N_ATT, 1, ATT_T)
    o_b, lse_b = _attn_fwd(q_m, 0, k_m, 0, v_m, 0, 2 * LANES, 1.0 / math.sqrt(64 + ROPE_DIM), None, "mla_attn")
    bias = bias + wts("relay_proj", o_b)["tok"][0, 0]
    o_a, lse_a = _attn_fwd(qkv, 0, qkv, 4, qkv, 8, LANES, 1.0 / math.sqrt(HEAD_DIM), bias, "fox_attn")

    w = {**w, **wts("proj", o_a)}
    gate_mix = gate_mix + wts("relay_ffn", o_a)["tok"][0, 0]
    pa = _mm(o_a, w["w_pf"], "nn", BF16, "proj_fox")

    def merge(pb_, gf, gm, pa_):
        return (_gate(gf) * pa_ + _gate(gm) * pb_, pb_), ()
    merged, pb = _mm_epi(o_b, w["w_pm"], "nn", 512, merge, "proj_mla", 1024,
                         rows=[(proj_a, 512, 2), (proj_a, 512, 4), (pa, 512)], outs=[(D, 512, BF16), (D, 512, BF16)])
    def post1(yv, xv, gate, gpost, gpre, sc, sh):
        x1 = xv + gate * (yv * _rstd(yv) * gpost)
        return (x1, (x1 * _rstd(x1) * gpre) * (1.0 + sc) + sh, yv), ()
    x1, h2, y = _mm_epi(merged, w["w_out"], "nn", D, post1, "out_proj", 512, rows=[(x, D)],
                        vecs=[gate_mix, g_post_mix, g_pre_ffn, scale_ffn, shift_ffn],
                        outs=[(D, D, F32), (D, D, BF16), (D, D, F32)])
    w = {**w, **wts("ffn", h2)}

    def swiglu(r):
        g, u = r[:, :FFN_T], r[:, FFN_T:]
        return (g * _gate(g) * u, r), ()
    act, gu = _mm_epi(h2, w["w_ffn_in"], "nn", 2 * FFN_T, swiglu, "ffn_in", 1024,
                      outs=[(D_FF, FFN_T, BF16), (2 * D_FF, 2 * FFN_T, BF16)])

    def head(y2v, x1v, tv, gate, gpost):
        r = _rstd(y2v)
        yn = y2v * r
        n2 = yn * gpost
        err = (x1v + gate * n2) - tv
        dx2 = err * (1.0 / D)
        dn2 = dx2 * gate
        dy2 = _norm_bwd(dn2 * gpost, yn, r)
        return (dx2, dy2), (_colsum(err * err), _colsum(dx2 * n2), _colsum(dn2 * yn))
    dx2, dy2, err_cols, d_gate_ffn, d_g_post_ffn = _mm_epi(
        act, w["w_ffn_out"], "nn", D, head, "ffn_out", 512, rows=[(x1, D), (target, D)], vecs=[gate_ffn, g_post_ffn],
        outs=[(D, D, F32), (D, D, BF16)], sums=[D, D, D])

    def swiglu_bwd(da, guv):
        g, u = guv[:, :FFN_T].astype(F32), guv[:, FFN_T:].astype(F32)
        sg = _gate(g)
        return (jnp.concatenate([da * u * (sg * (1.0 + g * (1.0 - sg))), da * (g * sg)], axis=1),), ()
    (dgu,) = _mm_epi(dy2, w["w_ffn_out"], "nt", FFN_T, swiglu_bwd, "ffn_out_dx", 1024, rows=[(gu, 2 * FFN_T)],
                     outs=[(2 * D_FF, 2 * FFN_T, BF16)])
    dw = {"w_ffn_out": _mm(act, dy2, "tn", BF16, "ffn_out_dw")}
    dw["w_ffn_in"] = _mm(dgu, h2, "tn", BF16, "ffn_in_dw")
    gate_mix = gate_mix + send({n: dw.pop(n) for n in ("w_ffn_in", "w_ffn_out")})[0, 0]

    def mid(dh, x1v, dx2v, yv, gpre, sc, gate, gpost):
        r2 = _rstd(x1v)
        x1n = x1v * r2
        t = dh * x1n
        dx1 = dx2v + _norm_bwd(dh * (gpre * (1.0 + sc)), x1n, r2)
        ry = _rstd(yv)
        yn = yv * ry
        dn1 = dx1 * gate
        dy = _norm_bwd(dn1 * gpost, yn, ry)
        sums = (_colsum(dh), _colsum(t) * gpre, _colsum(t) * (1.0 + sc), _colsum(dx1 * (yn * gpost)), _colsum(dn1 * yn))
        return (dx1, dy), sums
    dx1, dy, d_shift_ffn, d_scale_ffn, d_g_pre_ffn, d_gate_mix, d_g_post_mix = _mm_epi(
        dgu, w["w_ffn_in"], "nt", D, mid, "ffn_in_dx", 512, rows=[(x1, D), (dx2, D), (y, D)],
        vecs=[g_pre_ffn, scale_ffn, gate_mix, g_post_mix], outs=[(D, D, F32), (D, D, BF16)], sums=[D] * 5)

    dw["w_out"] = _mm(merged, dy, "tn", BF16, "out_proj_dw")

    def merge_bwd(dm, gf, gm, pa_, pb_):
        sf, sm = _gate(gf), _gate(gm)
        return (dm * sf, dm * sm, dm * pa_ * (sf * (1.0 - sf)), dm * pb_ * (sm * (1.0 - sm))), ()
    dpa, dpb, dgf, dgm = _mm_epi(dy, w["w_out"], "nt", 512, merge_bwd, "out_proj_dx", 1024,
                                 rows=[(proj_a, 512, 2), (proj_a, 512, 4), (pa, 512), (pb, 512)],
                                 outs=[(D, 512, BF16)] * 4)
    do_a = _mm(dpa, w["w_pf"], "nt", BF16, "proj_fox_dx")
    do_b = _mm(dpb, w["w_pm"], "nt", BF16, "proj_mla_dx")
    dw["w_pf"] = _mm(o_a, dpa, "tn", BF16, "proj_fox_dw")
    dw["w_pm"] = _mm(o_b, dpb, "tn", BF16, "proj_mla_dw")
    bias = bias + send({n: dw.pop(n) for n in ("w_out", "w_pf", "w_pm")})[0, 0]

    sc_a, sc_b = 1.0 / math.sqrt(HEAD_DIM), 1.0 / math.sqrt(64 + ROPE_DIM)
    dq_a, dk_a, dv_a, dbias = _attn_grad(qkv, 0, qkv, 4, qkv, 8, do_a, lse_a, LANES, sc_a, bias, BF16, "fox_attn_bwd")
    dq_m, dk_m, dv_m = _attn_grad(q_m, 0, k_m, 0, v_m, 0, do_b, lse_b, 2 * LANES, sc_b, None, F32, "mla_attn_bwd")

    def mla_rope_bwd(dq, dk, c_, a_, b_):
        lane = lax.broadcasted_iota(jnp.int32, (1, LANES), 1)
        dqs = [_rope_t(dq[:, hd * LANES:(hd + 1) * LANES], c_, a_, b_) for hd in range(HEADS)]
        dkpe = dk[:, 0:LANES]
        for hd in range(1, HEADS):
            dkpe = dkpe + dk[:, hd * LANES:(hd + 1) * LANES]
        dkpe = jnp.where((lane >= 64) & (lane < 96), dkpe, 0.0)
        dkr = jnp.where((lane >= 64) & (lane < 96), _rope_t(dkpe, c_, a_, b_), 0.0)
        return (jnp.concatenate(dqs, axis=1), dk, dkr), ()
    dqb, dkb, dkr = _rowwise(mla_rope_bwd, [(dq_m, D, 0), (dk_m, D, 0), (ct, LANES, 0), (sa, LANES, 0), (sb, LANES, 0)],
                             [], [(D, BF16), (D, BF16), (LANES, F32)], [], "mla_rope_bwd")
    def lora_q_bwd(dq, cq, gq):
        rq = _rstd(cq)
        cqh = cq * rq
        return (_norm_bwd(dq * gq, cqh, rq),), (_colsum(dq * cqh),)
    dcq, d_g_q = _mm_epi(dqb, w["w_uq"], "nt", Q_LORA, lora_q_bwd, "mla_uq_dx", 512, rows=[(proj_a, Q_LORA, 0)],
                         vecs=[g_q], outs=[(Q_LORA, Q_LORA, BF16)], sums=[Q_LORA])

    def lora_kv_bwd(dv_part, dk_part, ckv, gkv):
        dkv = dv_part + dk_part
        rk = _rstd(ckv)
        ckh = ckv * rk
        return (_norm_bwd(dkv * gkv, ckh, rk),), (_colsum(dkv * ckh),)
    dckv, d_g_kv = _mm_epi(dv_m, w["w_v"], "nt", KV_LORA, lora_kv_bwd, "mla_uv_dx", 1024,
                           rows=[(_mm(dkb, w["w_k"], "nt", F32, "mla_uk_dx"), KV_LORA), (proj_a, KV_LORA, 3)],
                           vecs=[g_kv], outs=[(KV_LORA, KV_LORA, BF16)], sums=[KV_LORA])

    dzt, d_bf = _fox_gates_bwd(dbias.reshape(HEADS, S), zt, bf)
    dmisc = (dkr + jnp.pad(jnp.transpose(dzt), ((0, 0), (0, LANES - HEADS)))).astype(BF16)
    dproj_a = jnp.concatenate([dcq, dckv, dgf, dgm, dmisc], axis=1)
    dqkv = jnp.concatenate([dq_a, dk_a, dv_a], axis=1)
    dw["w_a"] = _mm(h, dproj_a, "tn", BF16, "in_proj_a_dw")
    dw["w_b"] = _mm(h, dqkv, "tn", BF16, "in_proj_b_dw")
    tok = send(dw, True)
    dh_a = _mm(dproj_a, w["w_a"], "nt", F32, "in_proj_a_dx", dep=tok)
    tok = relay(dh_a)
    tok = send({"w_uq": _mm(cqn, dqb, "tn", BF16, "mla_uq_dw", dep=tok),
                "w_k": _mm(ckvn, dkb, "tn", BF16, "mla_uk_dw", dep=tok),
                "w_v": _mm(ckvn, dv_m, "tn", BF16, "mla_uv_dw", dep=tok)}, late=True)
    g_pre_mix = g_pre_mix + tok[0, 0]

    def first(dh_b, dh_a, xv, dx1v, gpre, sc):
        dhv = dh_b + dh_a
        r = _rstd(xv)
        xn = xv * r
        t = dhv * xn
        dx = dx1v + _norm_bwd(dhv * (gpre * (1.0 + sc)), xn, r)
        return (dx,), (_colsum(dhv), _colsum(t) * gpre, _colsum(t) * (1.0 + sc))
    grad_x, d_shift_mix, d_scale_mix, d_g_pre_mix = _mm_epi(
        dqkv, w["w_b"], "nt", D, first, "in_proj_b_dx", 512,
        rows=[(dh_a, D), (x, D), (dx1, D)],
        vecs=[g_pre_mix, scale_mix], outs=[(D, D, F32)], sums=[D] * 3)

    dmod = jnp.concatenate([d_shift_mix, d_scale_mix, d_gate_mix, d_shift_ffn, d_scale_ffn, d_gate_ffn], axis=1)
    small = dict(dmod=dmod, g_pre_mix=d_g_pre_mix, g_post_mix=d_g_post_mix, g_pre_ffn=d_g_pre_ffn,
                 g_post_ffn=d_g_post_ffn, g_q_lora=d_g_q, g_kv_lora=d_g_kv,
                 b_forget=jnp.pad(jnp.transpose(d_bf), ((0, 0), (0, LANES - HEADS))), err=err_cols)
    return grad_x, small


SMALL_ORDER = ("dmod", "g_pre_mix", "g_post_mix", "g_pre_ffn", "g_post_ffn", "g_q_lora", "g_kv_lora", "b_forget", "err")
SMALL_PARAM = {"dmod": "b_ada"}
MATRICES = ("w_in", "w_uq", "w_ukv", "w_proj_fox", "w_proj_mla", "w_out", "w_ffn_in", "w_ffn_out")
WEIGHTS = ("w_ada", "b_ada", "g_pre_mix", "g_post_mix", "g_pre_ffn", "g_post_ffn", "w_in", "b_forget", "g_q_lora",
           "w_uq", "g_kv_lora", "w_ukv", "w_proj_fox", "w_proj_mla", "w_out", "w_ffn_in", "w_ffn_out")


def kernel(x, c, positions, w_ada, b_ada, g_pre_mix, g_post_mix, g_pre_ffn, g_post_ffn, w_in, b_forget, g_q_lora, w_uq, g_kv_lora, w_ukv, w_proj_fox, w_proj_mla, w_out, w_ffn_in, w_ffn_out, loss_target, m_w_ada, m_b_ada, m_g_pre_mix, m_g_post_mix, m_g_pre_ffn, m_g_post_ffn, m_w_in, m_b_forget, m_g_q_lora, m_w_uq, m_g_kv_lora, m_w_ukv, m_w_proj_fox, m_w_proj_mla, m_w_out, m_w_ffn_in, m_w_ffn_out, v_w_ada, v_b_ada, v_g_pre_mix, v_g_post_mix, v_g_pre_ffn, v_g_post_ffn, v_w_in, v_b_forget, v_g_q_lora, v_w_uq, v_g_kv_lora, v_w_ukv, v_w_proj_fox, v_w_proj_mla, v_w_out, v_w_ffn_in, v_w_ffn_out):
    prm = dict(w_ada=w_ada, b_ada=b_ada, g_pre_mix=g_pre_mix, g_post_mix=g_post_mix, g_pre_ffn=g_pre_ffn,
               g_post_ffn=g_post_ffn, w_in=w_in, b_forget=b_forget, g_q_lora=g_q_lora, w_uq=w_uq, g_kv_lora=g_kv_lora,
               w_ukv=w_ukv, w_proj_fox=w_proj_fox, w_proj_mla=w_proj_mla, w_out=w_out, w_ffn_in=w_ffn_in, w_ffn_out=w_ffn_out)
    mom = dict(w_ada=m_w_ada, b_ada=m_b_ada, g_pre_mix=m_g_pre_mix, g_post_mix=m_g_post_mix, g_pre_ffn=m_g_pre_ffn,
               g_post_ffn=m_g_post_ffn, w_in=m_w_in, b_forget=m_b_forget, g_q_lora=m_g_q_lora, w_uq=m_w_uq,
               g_kv_lora=m_g_kv_lora, w_ukv=m_w_ukv, w_proj_fox=m_w_proj_fox, w_proj_mla=m_w_proj_mla, w_out=m_w_out,
               w_ffn_in=m_w_ffn_in, w_ffn_out=m_w_ffn_out)
    var = dict(w_ada=v_w_ada, b_ada=v_b_ada, g_pre_mix=v_g_pre_mix, g_post_mix=v_g_post_mix, g_pre_ffn=v_g_pre_ffn,
               g_post_ffn=v_g_post_ffn, w_in=v_w_in, b_forget=v_b_forget, g_q_lora=v_g_q_lora, w_uq=v_w_uq,
               g_kv_lora=v_g_kv_lora, w_ukv=v_w_ukv, w_proj_fox=v_w_proj_fox, w_proj_mla=v_w_proj_mla, w_out=v_w_out,
               w_ffn_in=v_w_ffn_in, w_ffn_out=v_w_ffn_out)
    me = _flat(*_coords())
    slot = jnp.reshape(me, (1,)).astype(jnp.int32)

    own = {n: prm[n][0].astype(BF16) for n in MATRICES}
    no_dep = jnp.zeros((8, LANES), F32)
    (st_c, st_in), tok = _async_start([[c], [own["w_in"]]], ["gather", "spread"], no_dep, "gather_in_start")
    (c_own,), (c_land,) = _async_wait(st_c, tok, "gather_c_wait")
    c_all = _with_own(c_land, c_own, me).reshape(N_DEV, D)
    ada_cols = w_ada.shape[2]
    b_cols = lax.dynamic_slice(b_ada, (0, me * ada_cols), (1, ada_cols))
    mod_cols, silu_c = _mod_part(c_all, w_ada[0], b_cols)
    (mod_all,) = _all_gather([mod_cols], "gather_mod")

    (w_in_own,), (w_in_land,) = _async_wait(st_in, mod_all, "gather_in_wait")
    (st_in,), tok = _async_start([[w_in_land]], "forward", no_dep, "gather_in_forward")
    _, (w_in_land,) = _async_wait(st_in, tok, "gather_in_forward_wait")
    w = _prepare_weights({"w_in": w_in_land}, {"w_in": w_in_own}, slot)
    later = dict(lora=("w_uq", "w_ukv"), proj=("w_proj_fox", "w_proj_mla", "w_out"), ffn=("w_ffn_in", "w_ffn_out"))
    states, tok = _async_start([[own[n] for n in names] for names in later.values()], ["gather", "spread", "spread"],
                               w["w_b"], "gather_rest_start")
    gather_state = dict(zip(later, states))
    own_thru = {}

    def wts(group, after):
        if group.startswith("relay_"):
            name = group[len("relay_"):]
            own_thru[name], lands = _async_wait(gather_state[name], after, "gather_" + name + "_wait")
            (gather_state[name],), t = _async_start([lands], "forward", no_dep, "gather_" + name + "_forward")
            return {"tok": t}
        srcs, lands = _async_wait(gather_state[group], after, "gather_" + group + "_landed")
        srcs = own_thru.get(group, srcs)
        return _prepare_weights(dict(zip(later[group], lands)), dict(zip(later[group], srcs)), slot)

    sent, late_sent, last = [], [], {}

    def send(grads, final=False, late=False):
        shards = _shard_grads(grads)
        names = list(shards)
        (state,), t = _async_start([[shards[n] for n in names]], "pair" if final else "exchange", no_dep,
                                   "exchange_" + names[0] + "_start")
        if final:
            last.update(names=names, state=state)
        else:
            (late_sent if late else sent).append((names, state))
        return t

    def relay(after):
        srcs, lands = _async_wait(last["state"], after, "exchange_pair_wait")
        core = jnp.reshape(lax.axis_index("c"), (1,)).astype(jnp.int32)
        sums = [_add_sibling(src, land, core) for src, land in zip(srcs, lands)]
        (last["state"],), t = _async_start([sums], "chips", no_dep, "exchange_chips_start")
        return t

    mod = lax.dynamic_index_in_dim(mod_all, me, axis=1, keepdims=False).reshape(1, 6 * D) + tok[0, 0]

    vec = dict(g_pre_mix=g_pre_mix, g_post_mix=g_post_mix, g_pre_ffn=g_pre_ffn, g_post_ffn=g_post_ffn,
               g_q_lora=g_q_lora, g_kv_lora=g_kv_lora, b_forget=b_forget)
    pos = positions.astype(F32).reshape(S, 1)
    grad_x, small = _fwd_bwd(x[0], pos, mod, loss_target[0], w, vec, wts, send, relay)

    bundle = jnp.concatenate([small[n] for n in SMALL_ORDER], axis=1)
    (small_state,), tok = _async_start([[bundle]], "gather", jnp.zeros((8, LANES), F32), "gather_small_start")

    out = {}
    swap = lambda a: jnp.swapaxes(a, -1, -2)

    def update(n, land, src, sl):
        if n != "w_ffn_in":
            out[n] = _adamw(prm[n][0], mom[n][0], var[n][0], land, "adamw_" + n, src, sl)
            return out[n][0]
        res = _adamw(swap(prm[n][0]), swap(mom[n][0]), swap(var[n][0]), land, "adamw_" + n, src, sl)
        out[n] = tuple(swap(t) for t in res)
        return res[0]

    after = tok
    for names, state in sent:
        srcs, lands = _async_wait(state, after, "exchange_" + names[0] + "_wait")
        for n, src, land in zip(names, srcs, lands):
            after = update(n, land, src, slot)
    srcs, lands = _async_wait(last["state"], after, "exchange_chips_wait")
    for n, src, land in zip(last["names"], srcs, lands):
        after = update(n, land, src, slot // 2)
    for names, state in late_sent:
        srcs, lands = _async_wait(state, after, "exchange_" + names[0] + "_wait")
        for n, src, land in zip(names, srcs, lands):
            after = update(n, land, src, slot)

    (own_bundle,), (bundle_all,) = _async_wait(small_state, after, "gather_small_wait")
    bundle_all = _with_own(bundle_all, own_bundle, me)
    dmod_all = bundle_all[:, 0, :6 * D]
    dm_cols = lax.dynamic_slice(dmod_all, (0, me * ada_cols), (N_DEV, ada_cols))
    out["w_ada"] = _adamw_w_ada(w_ada[0], m_w_ada[0], v_w_ada[0], jnp.transpose(silu_c), dm_cols)

    offsets, off = {}, 0
    for n in SMALL_ORDER:
        offsets[n] = off
        off += small[n].shape[1]
    names = [SMALL_PARAM.get(n, n) for n in SMALL_ORDER if n != "err"]
    results, err = _adamw_rows(bundle_all, [offsets[n] for n in SMALL_ORDER if n != "err"],
                               [prm[n] for n in names], [mom[n] for n in names], [var[n] for n in names],
                               offsets["err"], D)
    out.update(zip(names, results))
    loss = 0.5 * jnp.sum(err) / D

    res = [loss, grad_x[None]]
    for kind in range(4):
        for n in WEIGHTS:
            t = out[n][kind]
            res.append(t[None] if prm[n].ndim == 3 else t)
    return tuple(res)
```

```python
import functools
import math

import jax
import jax.numpy as jnp
from jax import lax
from jax.experimental import pallas as pl
from jax.experimental.pallas import tpu as pltpu

F32 = jnp.float32
BF16 = jnp.bfloat16

N_DEV = 8
S = 2048
D = 1024
D_FF = 2816
HEADS = 8
HEAD_DIM = 64
Q_LORA = 768
KV_LORA = 256
ROPE_DIM = 32
ROPE_THETA = 10000.0
NORM_EPS = 1e-6
LANES = 128
VMEM_LIMIT = 56 * 1024 * 1024

ADAM_LR = 0.001
ADAM_B1 = 0.9
ADAM_B2 = 0.999
ADAM_EPS = 1e-08
ADAM_WD = 0.01
ADAM_STEP = 10

ATT_T = 256
LOG2E = 1.4426950408889634
N_ATT = S // ATT_T

NN = (((1,), (0,)), ((), ()))
NT = (((1,), (1,)), ((), ()))
TN = (((0,), (0,)), ((), ()))
MESH = pl.DeviceIdType.MESH


def _params(sem=None):
    return pltpu.CompilerParams(dimension_semantics=sem, vmem_limit_bytes=VMEM_LIMIT)


def _pick(n, cap):
    best = None
    for t in range(LANES, cap + 1, LANES):
        if n % t == 0:
            best = t
    return best if best is not None else n


def _mm(a, b, mode, out_dtype, name, acc=None, dep=None):
    if mode == "nn":
        (m, k), (k2, n), dn = a.shape, b.shape, NN
    elif mode == "nt":
        (m, k), (n, k2), dn = a.shape, b.shape, NT
    else:
        (k, m), (k2, n), dn = a.shape, b.shape, TN
    assert k == k2, (a.shape, b.shape, mode)
    tn = _pick(n, 640)
    tm = _pick(m, 1536)
    osz = jnp.dtype(out_dtype).itemsize

    def need(tm_):
        blk = tm_ * k * 2 + tn * k * 2 + tm_ * tn * osz + (tm_ * tn * 4 if acc is not None else 0)
        return 2 * blk + tm_ * tn * 4
    while need(tm) > 36 * 1024 * 1024 and tm % 256 == 0:
        tm //= 2

    def body(*refs):
        a_ref, b_ref, o_ref = refs[0], refs[1], refs[-1]
        r = lax.dot_general(a_ref[...], b_ref[...], dn, preferred_element_type=F32)
        if acc is not None:
            r = r + refs[2][...]
        o_ref[...] = r.astype(o_ref.dtype)

    if mode == "tn":
        a_spec = pl.BlockSpec((k, tm), lambda i, j: (0, i))
    else:
        a_spec = pl.BlockSpec((tm, k), lambda i, j: (i, 0))
    if mode == "nt":
        b_spec = pl.BlockSpec((tn, k), lambda i, j: (j, 0))
    else:
        b_spec = pl.BlockSpec((k, tn), lambda i, j: (0, j))
    o_spec = pl.BlockSpec((tm, tn), lambda i, j: (i, j))
    in_specs = [a_spec, b_spec] + ([o_spec] if acc is not None else [])
    in_specs += [pl.BlockSpec(memory_space=pl.ANY)] if dep is not None else []
    args = (a, b) + ((acc,) if acc is not None else ()) + ((dep,) if dep is not None else ())
    return pl.pallas_call(
        body, name=name, grid=(m // tm, n // tn),
        in_specs=in_specs, out_specs=o_spec,
        out_shape=jax.ShapeDtypeStruct((m, n), out_dtype),
        compiler_params=_params(("parallel", "parallel")),
    )(*args)


def _mm_epi(a, b, mode, tnb, epi, name, tm, rows=(), vecs=(), outs=(), sums=(), pro=None):
    m = a.shape[0]
    k, nb = (b.shape if mode == "nn" else b.shape[::-1])
    dn = NN if mode == "nn" else NT
    pro_fn, pro_vecs, a_off = pro if pro is not None else (None, (), 0)
    n_in = 2 + len(rows) + len(vecs)
    n_all = n_in + len(pro_vecs)
    sub = min(tm, 256)

    def body(*refs):
        if pro is not None:
            a_out, a_scr = refs[-2:]
            refs = refs[:-2]

            @pl.when(pl.program_id(1) == 0)
            def _():
                a_scr[...] = pro_fn(refs[0][...], *[x[...] for x in refs[n_in:n_all]]).astype(BF16)
                a_out[...] = a_scr[...]
            a_ref = a_scr
        else:
            a_ref = refs[0]
        o_refs = refs[n_all:n_all + len(outs)]
        s_refs = refs[n_all + len(outs):]
        if sums:
            @pl.when((pl.program_id(0) == 0) & (pl.program_id(1) == 0))
            def _():
                for s_ref in s_refs:
                    s_ref[...] = jnp.zeros(s_ref.shape, F32)
        for c in range(tm // sub):
            rs = slice(c * sub, (c + 1) * sub)
            r = lax.dot_general(a_ref[rs, :], refs[1][...], dn, preferred_element_type=F32)
            o_vals, s_vals = epi(r, *[x[rs, :] for x in refs[2:2 + len(rows)]], *[x[...] for x in refs[2 + len(rows):n_in]])
            assert len(o_vals) == len(o_refs) and len(s_vals) == len(s_refs)
            for o_ref, val in zip(o_refs, o_vals):
                o_ref[rs, :] = val.astype(o_ref.dtype)
            for s_ref, val in zip(s_refs, s_vals):
                s_ref[...] += val

    once = dict(pipeline_mode=pl.Buffered(1)) if nb == tnb else {}
    if mode == "nn":
        b_spec = pl.BlockSpec((k, tnb), lambda i, j: (0, j), **once)
    else:
        b_spec = pl.BlockSpec((tnb, k), lambda i, j: (j, 0), **once)
    in_specs = [pl.BlockSpec((tm, k), lambda i, j: (i, a_off)), b_spec]
    rows = [tuple(r) + (0,) * (3 - len(r)) for r in rows]
    in_specs += [pl.BlockSpec((tm, w), functools.partial(lambda i, j, off: (i, j + off), off=off)) for _, w, off in rows]
    in_specs += [pl.BlockSpec(v.shape, lambda i, j: (0, 0)) for v in list(vecs) + list(pro_vecs)]
    out_specs = [pl.BlockSpec((tm, w), lambda i, j: (i, j)) for _, w, _ in outs]
    out_specs += [pl.BlockSpec((1, w), lambda i, j: (0, 0)) for w in sums]
    out_shape = [jax.ShapeDtypeStruct((m, full), dt) for full, _, dt in outs]
    out_shape += [jax.ShapeDtypeStruct((1, w), F32) for w in sums]
    if pro is not None:
        out_specs.append(pl.BlockSpec((tm, k), lambda i, j: (i, 0)))
        out_shape.append(jax.ShapeDtypeStruct((m, k), BF16))
    return pl.pallas_call(
        body, name=name, grid=(m // tm, nb // tnb),
        in_specs=in_specs, out_specs=out_specs, out_shape=out_shape,
        scratch_shapes=[pltpu.VMEM((tm, k), BF16)] if pro is not None else [],
        compiler_params=_params(("arbitrary", "arbitrary") if sums else ("parallel", "arbitrary" if pro is not None else "parallel")),
    )(a, b, *[r[0] for r in rows], *vecs, *pro_vecs)


def _rowwise(fn, row_ins, vec_ins, row_outs, sum_outs, name, tm=256):
    n_in = len(row_ins) + len(vec_ins)
    n_o = len(row_outs)
    rows = row_ins[0][0].shape[0]

    def body(*refs):
        vals = [r[...] for r in refs[:n_in]]
        outs = refs[n_in:]
        ro, so = fn(*vals)
        assert len(ro) == n_o and len(so) == len(sum_outs)
        for r, v in zip(outs[:n_o], ro):
            r[...] = v.astype(r.dtype)
        if sum_outs:
            @pl.when(pl.program_id(0) == 0)
            def _():
                for r in outs[n_o:]:
                    r[...] = jnp.zeros(r.shape, F32)
            for r, v in zip(outs[n_o:], so):
                r[...] += v

    in_specs = [pl.BlockSpec((tm, w), functools.partial(lambda i, b: (i, b), b=b)) for _, w, b in row_ins]
    in_specs += [pl.BlockSpec(v.shape, lambda i: (0, 0)) for v in vec_ins]
    out_specs = [pl.BlockSpec((tm, w), lambda i: (i, 0)) for w, _ in row_outs]
    out_specs += [pl.BlockSpec((1, w), lambda i: (0, 0)) for w in sum_outs]
    out_shape = [jax.ShapeDtypeStruct((rows, w), dt) for w, dt in row_outs]
    out_shape += [jax.ShapeDtypeStruct((1, w), F32) for w in sum_outs]
    return pl.pallas_call(
        body, name=name, grid=(rows // tm,),
        in_specs=in_specs, out_specs=out_specs, out_shape=out_shape,
        compiler_params=_params(("arbitrary",)),
    )(*[a for a, _, _ in row_ins], *vec_ins)


def _sigmoid(x):
    return 1.0 / (1.0 + jnp.exp(-x))


def _rstd(x):
    return lax.rsqrt(jnp.mean(x * x, axis=-1, keepdims=True) + NORM_EPS)


def _norm_bwd(dyn, xn, r):
    return r * (dyn - xn * jnp.mean(dyn * xn, axis=-1, keepdims=True))


def _colsum(x):
    return jnp.sum(x, axis=0, keepdims=True)


def _rope_tables(pos, invf):
    def fn(p, f):
        lane = lax.broadcasted_iota(jnp.int32, (1, LANES), 1)
        ang = p * f
        cs, sn = jnp.cos(ang), jnp.sin(ang)
        rot = (lane >= 64) & (lane < 96)
        ct = jnp.where(lane < 64, 1.0, jnp.where(rot, cs, 0.0))
        sa = jnp.where((lane >= 64) & (lane < 80), -sn, 0.0)
        sb = jnp.where((lane >= 80) & (lane < 96), sn, 0.0)
        return (ct, sa, sb), ()
    return _rowwise(fn, [(pos, 1, 0)], [invf], [(LANES, F32)] * 3, [], "rope_tables")


def _rope(x, ct, sa, sb):
    return x * ct + pltpu.roll(x, LANES - 16, 1) * sa + pltpu.roll(x, 16, 1) * sb


def _rope_t(x, ct, sa, sb):
    return x * ct - pltpu.roll(x, LANES - 16, 1) * sa - pltpu.roll(x, 16, 1) * sb


def _head_mask(width, hh):
    lane = lax.broadcasted_iota(jnp.int32, (1, width), 1)
    half = width // 2
    return (lane >= hh * half) & (lane < (hh + 1) * half)


ATT_PP = 2
ATT_CHAINS = [(a, hh) for a in range(ATT_PP) for hh in range(2)]
ATT_G = HEADS // (2 * ATT_PP)


def _pair(ref_or_val, a, width, rows=slice(None)):
    return ref_or_val[rows, a * width:(a + 1) * width]


def _attn_fwd(q, qo, k, ko, v, vo, dkp, scale, bias, name):
    T = ATT_T
    assert qo % ATT_PP == 0 and ko % ATT_PP == 0 and vo % ATT_PP == 0
    qo, ko, vo = qo // ATT_PP, ko // ATT_PP, vo // ATT_PP

    def body(*refs):
        if bias is not None:
            q_ref, k_ref, v_ref, b_ref, o_ref, lse_ref, s_scr = refs
        else:
            q_ref, k_ref, v_ref, o_ref, lse_ref, s_scr = refs
        i = pl.program_id(1)
        row = lax.broadcasted_iota(jnp.int32, (T, T), 0)
        col = lax.broadcasted_iota(jnp.int32, (T, T), 1)
        qms = []
        for a, hh in ATT_CHAINS:
            qb = _pair(q_ref, a, dkp)
            qms.append(jnp.where(_head_mask(dkp, hh), qb, jnp.zeros_like(qb)))

        def fold(t):
            return [t[:, c * LANES:(c + 1) * LANES] for c in range(T // LANES)]

        def run(nt):
            mls = [jnp.full((T, LANES), -jnp.inf, F32) for _ in ATT_CHAINS]
            for j in range(nt):
                ks = slice(j * T, (j + 1) * T)
                for ci, (a, hh) in enumerate(ATT_CHAINS):
                    s = lax.dot_general(qms[ci], _pair(k_ref, a, dkp, ks), NT, preferred_element_type=F32) * (scale * LOG2E)
                    if bias is not None:
                        s = s + b_ref[2 * a + hh, j] * LOG2E
                    if j == nt - 1:
                        s = jnp.where(row >= col, s, -jnp.inf)
                    s_scr[ci, j] = s
                    for part in fold(s):
                        mls[ci] = jnp.maximum(mls[ci], part)
            ms = [jnp.max(ml, axis=1, keepdims=True) for ml in mls]
            mbs = [jnp.broadcast_to(m, (T, LANES)) for m in ms]
            for a in range(ATT_PP):
                ls = [jnp.zeros((T, LANES), F32) for _ in range(2)]
                ps, vms = [], []
                for j in range(nt):
                    vb = _pair(v_ref, a, LANES, slice(j * T, (j + 1) * T))
                    for hh in range(2):
                        parts = [jnp.exp2(part - mbs[2 * a + hh]) for part in fold(s_scr[2 * a + hh, j])]
                        for part in parts:
                            ls[hh] = ls[hh] + part
                        ps.append(jnp.concatenate(parts, axis=1).astype(BF16))
                        vms.append(jnp.where(_head_mask(LANES, hh), vb, jnp.zeros_like(vb)))
                acc = lax.dot_general(jnp.concatenate(ps, axis=1), jnp.concatenate(vms, axis=0), NN,
                                      preferred_element_type=F32)
                l0, l1 = [jnp.sum(l, axis=1, keepdims=True) for l in ls]
                lse_ref[2 * a] = ms[2 * a] + jnp.log2(l0)
                lse_ref[2 * a + 1] = ms[2 * a + 1] + jnp.log2(l1)
                inv = jnp.where(_head_mask(LANES, 0), 1.0 / l0, 1.0 / l1)
                o_ref[:, a * LANES:(a + 1) * LANES] = (acc * inv).astype(o_ref.dtype)

        for nt in range(1, N_ATT + 1):
            pl.when(i == nt - 1)(functools.partial(run, nt))

    in_specs = [
        pl.BlockSpec((T, ATT_PP * dkp), lambda g, i: (i, qo + g)),
        pl.BlockSpec((S, ATT_PP * dkp), lambda g, i: (0, ko + g)),
        pl.BlockSpec((S, ATT_PP * LANES), lambda g, i: (0, vo + g)),
    ]
    args = [q, k, v]
    if bias is not None:
        in_specs.append(pl.BlockSpec((2 * ATT_PP, N_ATT, 1, T), lambda g, i: (g, 0, 0, 0)))
        args.append(bias)
    return pl.pallas_call(
        body, name=name, grid=(ATT_G, N_ATT),
        in_specs=in_specs,
        out_specs=[pl.BlockSpec((T, ATT_PP * LANES), lambda g, i: (i, g)),
                   pl.BlockSpec((2 * ATT_PP, T, 1), lambda g, i: (g, i, 0))],
        out_shape=[jax.ShapeDtypeStruct((S, HEADS * HEAD_DIM), BF16),
                   jax.ShapeDtypeStruct((HEADS, S, 1), F32)],
        scratch_shapes=[pltpu.VMEM((len(ATT_CHAINS), N_ATT, T, T), F32)],
        compiler_params=_params(("parallel", "arbitrary")),
    )(*args)


def _attn_grad(q, qo, k, ko, v, vo, do, lse, dkp, scale, bias, qk_dtype, name):
    T = ATT_T
    has_b = bias is not None
    qo, ko, vo = qo // ATT_PP, ko // ATT_PP, vo // ATT_PP
    n_ch = len(ATT_CHAINS)

    def body(*refs):
        q_ref, k_ref, v_ref, do_ref, lse_ref = refs[:5]
        refs = refs[5:]
        if has_b:
            b_ref, refs = refs[0], refs[1:]
        dq_ref, dk_ref, dv_ref = refs[:3]
        refs = refs[3:]
        if has_b:
            db_ref, refs = refs[0], refs[1:]
        p_scr, dp_scr, dk_acc, dv_acc = refs[:4]
        db_acc = refs[4] if has_b else None
        i = pl.program_id(1)

        @pl.when(i == 0)
        def _():
            dk_acc[...] = jnp.zeros(dk_acc.shape, F32)
            dv_acc[...] = jnp.zeros(dv_acc.shape, F32)
            if has_b:
                db_acc[...] = jnp.zeros(db_acc.shape, F32)

        row = lax.broadcasted_iota(jnp.int32, (T, T), 0)
        col = lax.broadcasted_iota(jnp.int32, (T, T), 1)

        def fold(t):
            return [t[:, c * LANES:(c + 1) * LANES] for c in range(T // LANES)]

        qms, doms, lses = [], [], []
        for a, hh in ATT_CHAINS:
            qb, dob = _pair(q_ref, a, dkp), _pair(do_ref, a, LANES)
            qms.append(jnp.where(_head_mask(dkp, hh), qb, jnp.zeros_like(qb)))
            doms.append(jnp.where(_head_mask(LANES, hh), dob, jnp.zeros_like(dob)))
            lses.append(lse_ref[2 * a + hh])

        def run(nt):
            dls = [jnp.zeros((T, LANES), F32) for _ in ATT_CHAINS]
            for j in range(nt):
                ks = slice(j * T, (j + 1) * T)
                for ci, (a, hh) in enumerate(ATT_CHAINS):
                    s = lax.dot_general(qms[ci], _pair(k_ref, a, dkp, ks), NT, preferred_element_type=F32) * (scale * LOG2E)
                    if has_b:
                        s = s + b_ref[ci, j] * LOG2E
                    s = s - lses[ci]
                    if j == nt - 1:
                        s = jnp.where(row >= col, s, -jnp.inf)
                    p = jnp.exp2(s)
                    dp = lax.dot_general(doms[ci], _pair(v_ref, a, LANES, ks), NT, preferred_element_type=F32)
                    p_scr[ci, j] = p
                    dp_scr[ci, j] = dp
                    for part in fold(p * dp):
                        dls[ci] = dls[ci] + part
            deltas = [jnp.broadcast_to(jnp.sum(dl, axis=1, keepdims=True), (T, LANES)) for dl in dls]
            for a in range(ATT_PP):
                ds_all, km_all = [], []
                qm2t = jnp.transpose(jnp.concatenate([qms[2 * a], qms[2 * a + 1]], axis=0))
                dom2t = jnp.transpose(jnp.concatenate([doms[2 * a], doms[2 * a + 1]], axis=0))
                for j in range(nt):
                    ks = slice(j * T, (j + 1) * T)
                    kb = _pair(k_ref, a, dkp, ks)
                    p2, ds2 = [], []
                    for hh in range(2):
                        ci = 2 * a + hh
                        p = p_scr[ci, j]
                        ds = jnp.concatenate([pp * (dd - deltas[ci]) for pp, dd in zip(fold(p), fold(dp_scr[ci, j]))], axis=1)
                        if has_b:
                            db_acc[ci, j] += jnp.sum(ds, axis=0, keepdims=True)
                        p2.append(p.astype(BF16))
                        ds2.append((ds * scale).astype(BF16))
                        km_all.append(jnp.where(_head_mask(dkp, hh), kb, jnp.zeros_like(kb)))
                    dv_acc[a * LANES:(a + 1) * LANES, ks] += lax.dot_general(
                        dom2t, jnp.concatenate(p2, axis=0), NN, preferred_element_type=F32)
                    dk_acc[a * dkp:(a + 1) * dkp, ks] += lax.dot_general(
                        qm2t, jnp.concatenate(ds2, axis=0), NN, preferred_element_type=F32)
                    ds_all += ds2
                dq = lax.dot_general(jnp.concatenate(ds_all, axis=1), jnp.concatenate(km_all, axis=0), NN,
                                     preferred_element_type=F32)
                dq_ref[:, a * dkp:(a + 1) * dkp] = dq.astype(dq_ref.dtype)

        for nt in range(1, N_ATT + 1):
            pl.when(i == nt - 1)(functools.partial(run, nt))

        @pl.when(i == N_ATT - 1)
        def _():
            dk_ref[...] = jnp.transpose(dk_acc[...]).astype(dk_ref.dtype)
            dv_ref[...] = jnp.transpose(dv_acc[...]).astype(dv_ref.dtype)
            if has_b:
                db_ref[...] = db_acc[...]

    in_specs = [
        pl.BlockSpec((T, ATT_PP * dkp), lambda g, i: (i, qo + g)),
        pl.BlockSpec((S, ATT_PP * dkp), lambda g, i: (0, ko + g)),
        pl.BlockSpec((S, ATT_PP * LANES), lambda g, i: (0, vo + g)),
        pl.BlockSpec((T, ATT_PP * LANES), lambda g, i: (i, g)),
        pl.BlockSpec((2 * ATT_PP, T, 1), lambda g, i: (g, i, 0)),
    ]
    args = [q, k, v, do, lse]
    out_specs = [
        pl.BlockSpec((T, ATT_PP * dkp), lambda g, i: (i, g)),
        pl.BlockSpec((S, ATT_PP * dkp), lambda g, i: (0, g)),
        pl.BlockSpec((S, ATT_PP * LANES), lambda g, i: (0, g)),
    ]
    width = (HEADS // 2) * dkp
    out_shape = [
        jax.ShapeDtypeStruct((S, width), qk_dtype),
        jax.ShapeDtypeStruct((S, width), qk_dtype),
        jax.ShapeDtypeStruct((S, HEADS * HEAD_DIM), BF16),
    ]
    scratch = [pltpu.VMEM((n_ch, N_ATT, T, T), F32), pltpu.VMEM((n_ch, N_ATT, T, T), F32),
               pltpu.VMEM((ATT_PP * dkp, S), F32), pltpu.VMEM((ATT_PP * LANES, S), F32)]
    if has_b:
        bspec = pl.BlockSpec((2 * ATT_PP, N_ATT, 1, T), lambda g, i: (g, 0, 0, 0))
        in_specs.append(bspec)
        args.append(bias)
        out_specs.append(bspec)
        out_shape.append(jax.ShapeDtypeStruct((HEADS, N_ATT, 1, T), F32))
        scratch.append(pltpu.VMEM((2 * ATT_PP, N_ATT, 1, T), F32))
    return pl.pallas_call(
        body, name=name, grid=(ATT_G, N_ATT),
        in_specs=in_specs, out_specs=out_specs, out_shape=out_shape, scratch_shapes=scratch,
        compiler_params=_params(("parallel", "arbitrary")),
    )(*args)


def _tri(upper):
    a = lax.broadcasted_iota(jnp.int32, (LANES, LANES), 0)
    b = lax.broadcasted_iota(jnp.int32, (LANES, LANES), 1)
    return jnp.where(a <= b if upper else a >= b, 1.0, 0.0).astype(F32)


def _fox_gates(proj, blk, bf):
    def body(m_ref, b_ref, z_out, o_ref):
        tri = _tri(True)
        carry = jnp.zeros((HEADS, 1), F32)
        for t in range(S // LANES):
            sl = slice(t * LANES, (t + 1) * LANES)
            zt = jnp.transpose(m_ref[sl, :])[:HEADS]
            z_out[:, sl] = zt
            z = zt + b_ref[...]
            logf = jnp.minimum(z, 0.0) - jnp.log(1.0 + jnp.exp(-jnp.abs(z)))
            c = lax.dot_general(logf, tri, NN, preferred_element_type=F32,
                                precision=lax.Precision.HIGHEST) + carry
            o_ref[:, sl] = -c
            carry = c[:, LANES - 1:LANES]

    return pl.pallas_call(
        body, name="fox_gates", grid=(1,),
        in_specs=[pl.BlockSpec((S, LANES), lambda i: (0, blk)), pl.BlockSpec(bf.shape, lambda i: (0, 0))],
        out_specs=[pl.BlockSpec((HEADS, S), lambda i: (0, 0))] * 2,
        out_shape=[jax.ShapeDtypeStruct((HEADS, S), F32)] * 2,
        compiler_params=_params(("arbitrary",)),
    )(proj, bf)


def _fox_gates_bwd(dbias, zt, bf):
    def body(d_ref, z_ref, b_ref, dz_ref, dbf_ref):
        tri = _tri(False)
        carry = jnp.zeros((HEADS, 1), F32)
        tot = jnp.zeros((HEADS, 1), F32)
        for t in reversed(range(S // LANES)):
            sl = slice(t * LANES, (t + 1) * LANES)
            df = -d_ref[:, sl]
            c = lax.dot_general(df, tri, NN, preferred_element_type=F32,
                                precision=lax.Precision.HIGHEST) + carry
            carry = c[:, 0:1]
            z = z_ref[:, sl] + b_ref[...]
            dz = c * _sigmoid(-z)
            dz_ref[:, sl] = dz
            tot = tot + jnp.sum(dz, axis=1, keepdims=True)
        dbf_ref[...] = tot

    return pl.pallas_call(
        body, name="fox_gates_bwd",
        out_shape=[jax.ShapeDtypeStruct((HEADS, S), F32), jax.ShapeDtypeStruct((HEADS, 1), F32)],
        compiler_params=_params(),
    )(dbias, zt, bf)


def _mod_part(c_all, w_ada, b_cols):
    def body(c_ref, w_ref, b_ref, o_ref, s_ref):
        c = c_ref[...]
        sc = c * _sigmoid(c)
        s_ref[...] = sc
        o_ref[...] = lax.dot_general(sc, w_ref[...], NN, preferred_element_type=F32,
                                     precision=lax.Precision.HIGHEST) + b_ref[...]

    return pl.pallas_call(
        body, name="mod_part",
        out_shape=[jax.ShapeDtypeStruct((N_DEV, w_ada.shape[1]), F32), jax.ShapeDtypeStruct(c_all.shape, F32)],
        compiler_params=_params(),
    )(c_all, w_ada, b_cols)


def _adamw_w_ada(w, m, v, sc_t, dm):
    rows, cols = w.shape
    tr = 256

    def body(w_ref, m_ref, v_ref, s_ref, d_ref, g_out, d_out, m_out, v_out):
        g = s_ref[:, 0:1] * d_ref[0:1, :]
        for b in range(1, N_DEV):
            g = g + s_ref[:, b:b + 1] * d_ref[b:b + 1, :]
        g_out[...] = g
        d_out[...], m_out[...], v_out[...] = _adamw_math(w_ref[...], g, m_ref[...], v_ref[...])

    spec = pl.BlockSpec((tr, cols), lambda i: (i, 0))
    return pl.pallas_call(
        body, name="adamw_w_ada", grid=(rows // tr,),
        in_specs=[spec, spec, spec, pl.BlockSpec((tr, N_DEV), lambda i: (i, 0)), pl.BlockSpec(dm.shape, lambda i: (0, 0))],
        out_specs=[spec] * 4, out_shape=[jax.ShapeDtypeStruct((rows, cols), F32)] * 4,
        compiler_params=_params(("parallel",)),
    )(w, m, v, sc_t, dm)


def _adamw(w, m, v, parts, name, own=None, slot=None):
    rows, cols = w.shape
    n = parts.shape[0]
    by_cols = rows % 256 != 0 and cols % 256 == 0
    tr, tc = (rows, 256) if by_cols else ((rows if rows <= 512 else 256), cols)
    tile = (lambda i: (0, i)) if by_cols else (lambda i: (i, 0))

    def body(*refs):
        if own is not None:
            s_ref, refs = refs[0], refs[1:]
            w_ref, m_ref, v_ref, p_ref, o_ref, g_out, d_out, m_out, v_out = refs
            terms = [jnp.where(s_ref[0] == kk, o_ref[0], p_ref[kk]) for kk in range(n)]
        else:
            w_ref, m_ref, v_ref, p_ref, g_out, d_out, m_out, v_out = refs
            terms = [p_ref[kk] for kk in range(n)]
        g = terms[0].astype(F32)
        for term in terms[1:]:
            g = g + term.astype(F32)
        g_out[...] = g
        d_out[...], m_out[...], v_out[...] = _adamw_math(w_ref[...], g, m_ref[...], v_ref[...])

    spec = pl.BlockSpec((tr, tc), lambda i, *_: tile(i))
    in_specs = [spec, spec, spec, pl.BlockSpec((n, tr, tc), lambda i, *_: (0,) + tile(i))]
    out_shape = [jax.ShapeDtypeStruct((rows, cols), F32)] * 4
    grid = (rows // tr if not by_cols else cols // tc,)
    if own is None:
        return pl.pallas_call(
            body, name=name, grid=grid, in_specs=in_specs, out_specs=[spec] * 4, out_shape=out_shape,
            compiler_params=_params(("parallel",)),
        )(w, m, v, parts)
    in_specs.append(pl.BlockSpec((1, tr, tc), lambda i, s: (s[0],) + tile(i)))
    return pl.pallas_call(
        body, name=name, out_shape=out_shape, compiler_params=_params(("parallel",)),
        grid_spec=pltpu.PrefetchScalarGridSpec(num_scalar_prefetch=1, grid=grid, in_specs=in_specs,
                                               out_specs=[spec] * 4),
    )(slot, w, m, v, parts, own)


def _adamw_math(w, g, m, v):
    mm = ADAM_B1 * m + (1.0 - ADAM_B1) * g
    vv = ADAM_B2 * v + (1.0 - ADAM_B2) * (g * g)
    m_hat = mm / (1.0 - ADAM_B1 ** ADAM_STEP)
    v_hat = vv / (1.0 - ADAM_B2 ** ADAM_STEP)
    return -ADAM_LR * (m_hat / (jnp.sqrt(v_hat) + ADAM_EPS) + ADAM_WD * w), mm, vv


def _adamw_rows(bundles, offsets, ws, ms, vs, err_off, err_width):
    k = len(ws)

    def body(*refs):
        b_ref = refs[0]
        w_refs, m_refs, v_refs = refs[1:1 + k], refs[1 + k:1 + 2 * k], refs[1 + 2 * k:1 + 3 * k]
        outs = refs[1 + 3 * k:]
        g_all = b_ref[0]
        for kk in range(1, N_DEV):
            g_all = g_all + b_ref[kk]
        for i in range(k):
            width = w_refs[i].shape[1]
            g = g_all[:, offsets[i]:offsets[i] + width]
            outs[4 * i][...] = g
            outs[4 * i + 1][...], outs[4 * i + 2][...], outs[4 * i + 3][...] = _adamw_math(
                w_refs[i][...], g, m_refs[i][...], v_refs[i][...])
        outs[4 * k][...] = g_all[:, err_off:err_off + err_width]

    out_shape = []
    for w_ in ws:
        out_shape += [jax.ShapeDtypeStruct(w_.shape, F32)] * 4
    out_shape.append(jax.ShapeDtypeStruct((1, err_width), F32))
    res = pl.pallas_call(body, name="adamw_rows", out_shape=out_shape, compiler_params=_params())(bundles, *ws, *ms, *vs)
    return [tuple(res[4 * i:4 * i + 4]) for i in range(k)], res[-1]


def _coords():
    return lax.axis_index("x"), lax.axis_index("y"), lax.axis_index("c")


def _flat(px, py, pc):
    return 4 * px + 2 * py + pc


def _all_gather(arrs, name):
    n = len(arrs)

    def body(*refs):
        ins, outs = refs[:n], refs[n:2 * n]
        send, recv, lsem = refs[2 * n:]
        x, y, c = _coords()
        me, sibling = (x, y, c), (x, y, 1 - c)
        chips = [(1 - x, y), (x, 1 - y), (1 - x, 1 - y)]

        def copy(a, kk, block, to, src=None):
            slot = outs[a].at[_flat(*block)]
            return pltpu.make_async_remote_copy(
                src_ref=slot if src is None else src, dst_ref=slot,
                send_sem=send.at[a, kk], recv_sem=recv.at[a, kk],
                device_id=to, device_id_type=MESH)

        mine = [pltpu.make_async_copy(ins[a], outs[a].at[_flat(*me)], lsem.at[a]) for a in range(n)]
        for cp in mine:
            cp.start()
        first = []
        for a in range(n):
            first.append(copy(a, 0, me, sibling, src=ins[a]))
            first += [copy(a, 1 + j, me, (*chip, c), src=ins[a]) for j, chip in enumerate(chips)]
        for cp in first:
            cp.start()
        passed = []
        for j, chip in enumerate(chips):
            for a in range(n):
                copy(a, 1 + j, (*chip, c), me).wait_recv()
                cp = copy(a, 4 + j, (*chip, c), sibling)
                cp.start()
                passed.append(cp)
        for a in range(n):
            copy(a, 0, sibling, me).wait_recv()
        for j, chip in enumerate(chips):
            for a in range(n):
                copy(a, 4 + j, (*chip, 1 - c), me).wait_recv()
        for cp in first + passed:
            cp.wait_send()
        for cp in mine:
            cp.wait()

    any_spec = pl.BlockSpec(memory_space=pl.ANY)
    return pl.pallas_call(
        body, name=name,
        in_specs=[any_spec] * n, out_specs=[any_spec] * n,
        out_shape=[jax.ShapeDtypeStruct((N_DEV,) + a.shape, a.dtype) for a in arrs],
        scratch_shapes=[pltpu.SemaphoreType.DMA((n, 7)), pltpu.SemaphoreType.DMA((n, 7)),
                        pltpu.SemaphoreType.DMA((n,))],
    )(*arrs)


def _peer_list():
    x, y, c = _coords()
    return [((1 - x if r & 4 else x), (1 - y if r & 2 else y), (1 - c if r & 1 else c)) for r in range(1, N_DEV)]


def _copy_plan(mode, src, land):
    x, y, c = _coords()
    me = _flat(x, y, c)
    if mode == "gather":
        return [(src, land.at[me], peer) for peer in _peer_list()]
    if mode == "exchange":
        return [(src.at[_flat(*peer)], land.at[me], peer) for peer in _peer_list()]
    if mode == "pair":
        return [(src.at[_flat(q // 2, q % 2, 1 - c)], land.at[q], (x, y, 1 - c)) for q in range(N_DEV // 2)]
    chips = [((1 - x if r & 2 else x), (1 - y if r & 1 else y)) for r in range(1, N_DEV // 2)]
    if mode == "chips":
        return [(src.at[2 * qx + qy], land.at[2 * x + y], (qx, qy, c)) for qx, qy in chips]
    if mode == "spread":
        return [(src, land.at[me], (x, y, 1 - c))] + [(src, land.at[me], (qx, qy, c)) for qx, qy in chips]
    assert mode == "forward"
    return [(land.at[_flat(qx, qy, c)], land.at[_flat(qx, qy, c)], (x, y, 1 - c)) for qx, qy in chips]


N_COPIES = dict(gather=N_DEV - 1, exchange=N_DEV - 1, pair=N_DEV // 2, chips=N_DEV // 2 - 1, spread=N_DEV // 2,
                forward=N_DEV // 2 - 1)


def _land_shape(mode, shape):
    return {"gather": (N_DEV,) + shape, "spread": (N_DEV,) + shape, "exchange": shape,
            "pair": (N_DEV // 2,) + shape[1:], "chips": shape}[mode]


HBM_SPEC = pl.BlockSpec(memory_space=pltpu.HBM)
SEM_SPEC = pl.BlockSpec(memory_space=pltpu.SEMAPHORE)
ANY_SPEC = pl.BlockSpec(memory_space=pl.ANY)
SIDE_EFFECT = pltpu.SideEffectType.DATAFLOW_SIDE_EFFECTING


def _async_start(groups, modes, after, name):
    modes = [modes] * len(groups) if isinstance(modes, str) else list(modes)
    arrs = [(a, m) for g, m in zip(groups, modes) for a in g]
    n = len(arrs)
    fresh = [i for i, (_, m) in enumerate(arrs) if m != "forward"]

    def body(*refs):
        srcs, new_lands = refs[:n], refs[n:n + len(fresh)]
        outs = refs[n + len(fresh) + 1:]
        lands = list(srcs)
        for k, i in enumerate(fresh):
            lands[i] = new_lands[k]
        for ai, (_, mode) in enumerate(arrs):
            for src_ref, dst_ref, peer in _copy_plan(mode, srcs[ai], lands[ai]):
                pltpu.make_async_remote_copy(src_ref=src_ref, dst_ref=dst_ref, send_sem=outs[2 * ai],
                                             recv_sem=outs[2 * ai + 1], device_id=peer, device_id_type=MESH).start()
        outs[-1][...] = jnp.zeros(outs[-1].shape, F32)

    land_shapes = [(_land_shape(arrs[i][1], arrs[i][0].shape), arrs[i][0].dtype) for i in fresh]
    n_buf = n + len(fresh)
    out_shape = [pltpu.SemaphoreType.DMA(())] * (2 * n)
    out_shape += [pltpu.HBM(a.shape, a.dtype) for a, _ in arrs]
    out_shape += [pltpu.HBM(shape, dt) for shape, dt in land_shapes]
    out_shape.append(jax.ShapeDtypeStruct((8, LANES), F32))
    res = pl.pallas_call(
        body, name=name, out_shape=tuple(out_shape),
        in_specs=[HBM_SPEC] * n_buf + [ANY_SPEC],
        out_specs=tuple([SEM_SPEC] * (2 * n) + [HBM_SPEC] * n_buf + [pl.BlockSpec(memory_space=pltpu.VMEM)]),
        input_output_aliases={i: 2 * n + i for i in range(n_buf)},
        compiler_params=pltpu.CompilerParams(has_side_effects=SIDE_EFFECT),
    )(*[pltpu.with_memory_space_constraint(a, pltpu.HBM) for a, _ in arrs],
      *[pltpu.with_memory_space_constraint(lax.empty(shape, dt), pltpu.HBM) for shape, dt in land_shapes],
      after)
    sems, thru = res[:2 * n], res[2 * n:-1]
    land_of = {i: thru[n + k] for k, i in enumerate(fresh)}
    states, idx = [], 0
    for g, mode in zip(groups, modes):
        ids = range(idx, idx + len(g))
        idx += len(g)
        states.append(([sems[2 * i] for i in ids], [sems[2 * i + 1] for i in ids],
                       None if mode == "forward" else [thru[i] for i in ids],
                       [land_of.get(i, thru[i]) for i in ids], mode))
    return states, res[-1]


def _async_wait(state, after, name):
    sends, recvs, srcs, lands, mode = state
    g = len(lands)
    bufs = (list(srcs) if srcs is not None else []) + list(lands)
    nb = len(bufs)

    def body(*refs):
        l_refs, sems = refs[nb - g:nb], refs[nb:nb + 2 * g]
        for ai in range(g):
            moved = l_refs[ai].at[pl.ds(0, N_COPIES[mode])]
            cp = pltpu.make_async_remote_copy(src_ref=moved, dst_ref=moved, send_sem=sems[ai], recv_sem=sems[g + ai],
                                              device_id=_coords(), device_id_type=MESH)
            cp.wait_send()
            cp.wait_recv()

    res = pl.pallas_call(
        body, name=name,
        out_shape=tuple(pltpu.HBM(a.shape, a.dtype) for a in bufs),
        in_specs=[HBM_SPEC] * nb + [SEM_SPEC] * (2 * g) + [ANY_SPEC],
        out_specs=tuple([HBM_SPEC] * nb),
        input_output_aliases={i: i for i in range(nb)},
        compiler_params=pltpu.CompilerParams(has_side_effects=SIDE_EFFECT),
    )(*bufs, *sends, *recvs, after)
    return (list(res[:nb - g]) if srcs is not None else None), list(res[nb - g:])


def _add_sibling(mine, theirs, core):
    def body(c_ref, a_ref, b_ref, o_ref):
        o_ref[...] = (a_ref[...].astype(F32) + b_ref[...].astype(F32)).astype(o_ref.dtype)

    blk = (1,) + mine.shape[1:]
    return pl.pallas_call(
        body, name="add_sibling", out_shape=jax.ShapeDtypeStruct(theirs.shape, mine.dtype),
        grid_spec=pltpu.PrefetchScalarGridSpec(
            num_scalar_prefetch=1, grid=(theirs.shape[0],),
            in_specs=[pl.BlockSpec(blk, lambda q, c: (2 * q + c[0], 0, 0)), pl.BlockSpec(blk, lambda q, c: (q, 0, 0))],
            out_specs=pl.BlockSpec(blk, lambda q, c: (q, 0, 0))),
        compiler_params=_params(("parallel",)),
    )(core, mine, theirs)


def _with_own(land, own, me):
    return lax.dynamic_update_index_in_dim(land, own, me, 0)


IN_SPLITS = (512, 512, 512, 8, 768, 256, 32, 1024, 1024)


def _from_shards(g, fn, out_widths, name, own=None, slot=None):
    _, k, n = g.shape
    tr = min(k, 256)

    def body(*refs):
        if own is not None:
            s_ref, g_ref, own_ref = refs[:3]
            cols = [jnp.where(s_ref[0] == j, own_ref[...], g_ref[j]) for j in range(N_DEV)]
        else:
            g_ref = refs[0]
            cols = [g_ref[j] for j in range(N_DEV)]
        for o_ref, val in zip(refs[-len(out_widths):], fn(jnp.concatenate(cols, axis=1))):
            o_ref[...] = val

    in_specs = [pl.BlockSpec((N_DEV, tr, n), lambda i, *_: (0, i, 0))]
    out_spec = [pl.BlockSpec((tr, wd), lambda i, *_: (i, 0)) for wd in out_widths]
    out_shape = [jax.ShapeDtypeStruct((k, wd), g.dtype) for wd in out_widths]
    if own is None:
        return pl.pallas_call(body, name=name, grid=(k // tr,), in_specs=in_specs, out_specs=out_spec,
                              out_shape=out_shape, compiler_params=_params(("parallel",)))(g)
    in_specs.append(pl.BlockSpec((tr, n), lambda i, *_: (i, 0)))
    return pl.pallas_call(
        body, name=name, out_shape=out_shape, compiler_params=_params(("parallel",)),
        grid_spec=pltpu.PrefetchScalarGridSpec(num_scalar_prefetch=1, grid=(k // tr,), in_specs=in_specs, out_specs=out_spec),
    )(slot, g, own)


def _unshard_cols(g, own=None, slot=None):
    return _from_shards(g, lambda full: (full,), [N_DEV * g.shape[2]], "unshard_cols_%d" % g.shape[2], own, slot)[0]


FFN_T = D_FF // 2
FFN_SHARD = 2 * D_FF // N_DEV


def _unshard_ffn_in(g, own=None, slot=None):
    def pairs(full):
        parts = []
        for j in range(D_FF // FFN_T):
            parts += [full[:, j * FFN_T:(j + 1) * FFN_T], full[:, D_FF + j * FFN_T:D_FF + (j + 1) * FFN_T]]
        return (jnp.concatenate(parts, axis=1),)

    return _from_shards(g, pairs, [2 * D_FF], "unshard_ffn_in", own, slot)[0]


def _shard_ffn_in_t(wt):
    tc = 256

    def body(w_ref, o_ref):
        x = w_ref[...]
        nb = D_FF // FFN_T
        full = jnp.concatenate([x[(2 * j + half) * FFN_T:(2 * j + half + 1) * FFN_T]
                                for half in range(2) for j in range(nb)], axis=0)
        for j in range(N_DEV):
            o_ref[j] = full[j * FFN_SHARD:(j + 1) * FFN_SHARD]

    return pl.pallas_call(
        body, name="shard_ffn_in_t", grid=(D // tc,),
        in_specs=[pl.BlockSpec((2 * D_FF, tc), lambda i: (0, i))],
        out_specs=pl.BlockSpec((N_DEV, FFN_SHARD, tc), lambda i: (0, 0, i)),
        out_shape=jax.ShapeDtypeStruct((N_DEV, FFN_SHARD, D), wt.dtype),
        compiler_params=_params(("parallel",)),
    )(wt)


def _shard_cols(w):
    k, n = w.shape[0], w.shape[1] // N_DEV
    tr = min(k, 256)

    def body(w_ref, o_ref):
        full = w_ref[...]
        for j in range(N_DEV):
            o_ref[j] = full[:, j * n:(j + 1) * n]

    return pl.pallas_call(
        body, name="shard_cols_%d" % n, grid=(k // tr,),
        in_specs=[pl.BlockSpec((tr, N_DEV * n), lambda i: (i, 0))],
        out_specs=pl.BlockSpec((N_DEV, tr, n), lambda i: (0, i, 0)),
        out_shape=jax.ShapeDtypeStruct((N_DEV, k, n), w.dtype),
        compiler_params=_params(("parallel",)),
    )(w)


IN_OFFS = tuple(sum(IN_SPLITS[:i]) for i in range(len(IN_SPLITS) + 1))
IN_SHARD = IN_OFFS[-1] // N_DEV
REGROUP_ROWS = 128
MISC_AT = Q_LORA + KV_LORA + 2 * D
A_COLS = MISC_AT + LANES
A_TILE = A_COLS
B_COLS = 3 * HEADS * HEAD_DIM
MISC_BLOCK = MISC_AT // LANES
KR_AT = 64


def _w_in_regroup(g, own=None, slot=None):
    def groups(full):
        fq, fk, fv, wf, cq, ckv, kr, gf, gm = [full[:, IN_OFFS[i]:IN_OFFS[i + 1]] for i in range(9)]
        rows = full.shape[0]
        gap = jnp.zeros((rows, KR_AT - HEADS), BF16)
        tail = jnp.zeros((rows, LANES - KR_AT - ROPE_DIM), BF16)
        return jnp.concatenate([cq, ckv, gf, gm, wf, gap, kr, tail], axis=1), jnp.concatenate([fq, fk, fv], axis=1)

    return _from_shards(g, groups, [A_COLS, B_COLS], "w_in_regroup", own, slot)


def _w_in_ungroup(da, db_):
    def body(a_ref, b_ref, o_ref):
        a = a_ref[...]
        lora = Q_LORA + KV_LORA
        full = jnp.concatenate([b_ref[...], a[:, MISC_AT:MISC_AT + HEADS], a[:, :lora],
                                a[:, MISC_AT + KR_AT:MISC_AT + KR_AT + ROPE_DIM], a[:, lora:MISC_AT]], axis=1)
        for j in range(N_DEV):
            o_ref[j] = full[:, j * IN_SHARD:(j + 1) * IN_SHARD]

    tr = REGROUP_ROWS
    return pl.pallas_call(
        body, name="w_in_ungroup", grid=(D // tr,),
        in_specs=[pl.BlockSpec((tr, A_COLS), lambda i: (i, 0)), pl.BlockSpec((tr, B_COLS), lambda i: (i, 0))],
        out_specs=pl.BlockSpec((N_DEV, tr, IN_SHARD), lambda i: (0, i, 0)),
        out_shape=jax.ShapeDtypeStruct((N_DEV, D, IN_SHARD), BF16),
        compiler_params=_params(("parallel",)),
    )(da, db_)


def _prepare_weights(g, own=None, slot=None):
    w = {}
    if own is not None:
        small = ("w_uq", "w_ukv", "w_out", "w_ffn_out")
        g = {n: (_with_own(a, own[n], slot[0]) if n in small else a) for n, a in g.items()}
    pick = (lambda n: (own[n], slot)) if own is not None else (lambda n: (None, None))
    if "w_in" in g:
        w["w_a"], w["w_b"] = _w_in_regroup(g["w_in"], *pick("w_in"))
    if "w_uq" in g:
        w_uq = g["w_uq"].reshape(Q_LORA, HEADS, 96)
        w["w_uq"] = jnp.pad(w_uq, ((0, 0), (0, 0), (0, 32))).reshape(Q_LORA, HEADS * LANES)
        ukv = g["w_ukv"]
        w["w_k"] = jnp.transpose(jnp.pad(ukv[:, :, :64], ((0, 0), (0, 0), (0, 64))), (1, 0, 2)).reshape(KV_LORA, HEADS * LANES)
        w["w_v"] = jnp.transpose(ukv[:, :, 64:], (1, 0, 2)).reshape(KV_LORA, HEADS * HEAD_DIM)
    if "w_out" in g:
        w["w_pf"] = _unshard_cols(g["w_proj_fox"], *pick("w_proj_fox"))
        w["w_pm"] = _unshard_cols(g["w_proj_mla"], *pick("w_proj_mla"))
        w["w_out"] = g["w_out"].reshape(D, D)
    if "w_ffn_in" in g:
        w["w_ffn_in"] = _unshard_ffn_in(g["w_ffn_in"], *pick("w_ffn_in"))
        w["w_ffn_out"] = g["w_ffn_out"].reshape(D_FF, D)
    return w


def _shard_grads(dw):
    out = {}
    if "w_a" in dw:
        out["w_in"] = _w_in_ungroup(dw["w_a"], dw["w_b"])
    if "w_uq" in dw:
        w_uq = dw["w_uq"].reshape(Q_LORA, HEADS, LANES)[:, :, :96].reshape(Q_LORA, Q_LORA)
        out["w_uq"] = w_uq.reshape(N_DEV, Q_LORA // N_DEV, Q_LORA)
        k_part = dw["w_k"].reshape(KV_LORA, HEADS, LANES)[:, :, :64]
        v_part = dw["w_v"].reshape(KV_LORA, HEADS, HEAD_DIM)
        out["w_ukv"] = jnp.transpose(jnp.concatenate([k_part, v_part], axis=2), (1, 0, 2))
    if "w_out" in dw:
        out["w_proj_fox"] = _shard_cols(dw["w_pf"])
        out["w_proj_mla"] = _shard_cols(dw["w_pm"])
        out["w_out"] = dw["w_out"].reshape(N_DEV, D // N_DEV, D)
    if "w_ffn_in" in dw:
        out["w_ffn_in"] = _shard_ffn_in_t(dw["w_ffn_in"])
        out["w_ffn_out"] = dw["w_ffn_out"].reshape(N_DEV, D_FF // N_DEV, D)
    return out


def _fwd_bwd(x, pos, mod, target, w, vec, wts, send, relay):
    shift_mix, scale_mix, gate_mix, shift_ffn, scale_ffn, gate_ffn = [mod[:, i * D:(i + 1) * D] for i in range(6)]
    g_pre_mix, g_post_mix, g_pre_ffn, g_post_ffn = vec["g_pre_mix"], vec["g_post_mix"], vec["g_pre_ffn"], vec["g_post_ffn"]
    g_q, g_kv = vec["g_q_lora"], vec["g_kv_lora"]

    inv_freq = 1.0 / (ROPE_THETA ** (jnp.arange(0, ROPE_DIM, 2, dtype=F32) / ROPE_DIM))
    invf = jnp.concatenate([jnp.zeros((64,), F32), inv_freq, inv_freq, jnp.zeros((32,), F32)]).reshape(1, LANES)
    ct, sa, sb = _rope_tables(pos, invf)

    def pre1(xv, g, sc, sh):
        return (xv * _rstd(xv) * g) * (1.0 + sc) + sh
    proj_a, h = _mm_epi(x, w["w_a"], "nn", A_TILE, lambda r: ((r,), ()), "in_proj_a", 512, outs=[(A_COLS, A_TILE, F32)],
                        pro=(pre1, [g_pre_mix, scale_mix, shift_mix], 0))
    qkv = _mm(h, w["w_b"], "nn", BF16, "in_proj_b")

    def lora_norm(cv, g):
        return cv * _rstd(cv) * g
    w = {**w, **wts("lora", qkv)}
    tables = [(ct, LANES), (sa, LANES), (sb, LANES)]

    def rope_q(qv, c_, a_, b_):
        return (jnp.concatenate([_rope(qv[:, hd * LANES:(hd + 1) * LANES], c_, a_, b_) for hd in range(HEADS)], axis=1),), ()
    q_m, cqn = _mm_epi(proj_a, w["w_uq"], "nn", D, rope_q, "mla_uq", 512, rows=tables, outs=[(D, D, BF16)],
                       pro=(lora_norm, [g_q], 0))

    def rope_k(kv, misc, c_, a_, b_):
        lane = lax.broadcasted_iota(jnp.int32, (1, LANES), 1)
        kpe = jnp.where((lane >= 64) & (lane < 96), _rope(misc, c_, a_, b_), 0.0)
        return (jnp.concatenate([kv[:, hd * LANES:(hd + 1) * LANES] + kpe for hd in range(HEADS)], axis=1),), ()
    k_m, ckvn = _mm_epi(proj_a, w["w_k"], "nn", D, rope_k, "mla_uk", 512, rows=[(proj_a, LANES, MISC_BLOCK)] + tables,
                        outs=[(D, D, BF16)], pro=(lora_norm, [g_kv], Q_LORA // KV_LORA))
    v_m = _mm(ckvn, w["w_v"], "nn", BF16, "mla_uv")

    bf = jnp.transpose(vec["b_forget"])
    zt, neg_f = _fox_gates(proj_a, MISC_BLOCK, bf)
    bias = neg_f.reshape(HEADS, N_ATT, 1, ATT_T)
    o_b, lse_b = _attn_fwd(q_m, 0, k_m, 0, v_m, 0, 2 * LANES, 1.0 / math.sqrt(64 + ROPE_DIM), None, "mla_attn")
    bias = bias + wts("relay_proj", o_b)["tok"][0, 0]
    o_a, lse_a = _attn_fwd(qkv, 0, qkv, 4, qkv, 8, LANES, 1.0 / math.sqrt(HEAD_DIM), bias, "fox_attn")

    w = {**w, **wts("proj", o_a)}
    gate_mix = gate_mix + wts("relay_ffn", o_a)["tok"][0, 0]
    pa = _mm(o_a, w["w_pf"], "nn", BF16, "proj_fox")

    def merge(pb_, gf, gm, pa_):
        return (_sigmoid(gf) * pa_ + _sigmoid(gm) * pb_, pb_), ()
    merged, pb = _mm_epi(o_b, w["w_pm"], "nn", 512, merge, "proj_mla", 1024,
                         rows=[(proj_a, 512, 2), (proj_a, 512, 4), (pa, 512)], outs=[(D, 512, BF16), (D, 512, BF16)])
    def post1(yv, xv, gate, gpost, gpre, sc, sh):
        x1 = xv + gate * (yv * _rstd(yv) * gpost)
        return (x1, (x1 * _rstd(x1) * gpre) * (1.0 + sc) + sh, yv), ()
    x1, h2, y = _mm_epi(merged, w["w_out"], "nn", D, post1, "out_proj", 512, rows=[(x, D)],
                        vecs=[gate_mix, g_post_mix, g_pre_ffn, scale_ffn, shift_ffn],
                        outs=[(D, D, F32), (D, D, BF16), (D, D, F32)])
    w = {**w, **wts("ffn", h2)}

    def swiglu(r):
        g, u = r[:, :FFN_T], r[:, FFN_T:]
        return (g * _sigmoid(g) * u, r), ()
    act, gu = _mm_epi(h2, w["w_ffn_in"], "nn", 2 * FFN_T, swiglu, "ffn_in", 512,
                      outs=[(D_FF, FFN_T, BF16), (2 * D_FF, 2 * FFN_T, BF16)])

    def head(y2v, x1v, tv, gate, gpost):
        r = _rstd(y2v)
        yn = y2v * r
        n2 = yn * gpost
        err = (x1v + gate * n2) - tv
        dx2 = err * (1.0 / D)
        dn2 = dx2 * gate
        dy2 = _norm_bwd(dn2 * gpost, yn, r)
        return (dx2, dy2), (_colsum(err * err), _colsum(dx2 * n2), _colsum(dn2 * yn))
    dx2, dy2, err_cols, d_gate_ffn, d_g_post_ffn = _mm_epi(
        act, w["w_ffn_out"], "nn", D, head, "ffn_out", 512, rows=[(x1, D), (target, D)], vecs=[gate_ffn, g_post_ffn],
        outs=[(D, D, F32), (D, D, BF16)], sums=[D, D, D])

    def swiglu_bwd(da, guv):
        g, u = guv[:, :FFN_T].astype(F32), guv[:, FFN_T:].astype(F32)
        sg = _sigmoid(g)
        return (jnp.concatenate([da * u * (sg * (1.0 + g * (1.0 - sg))), da * (g * sg)], axis=1),), ()
    (dgu,) = _mm_epi(dy2, w["w_ffn_out"], "nt", FFN_T, swiglu_bwd, "ffn_out_dx", 512, rows=[(gu, 2 * FFN_T)],
                     outs=[(2 * D_FF, 2 * FFN_T, BF16)])
    dw = {"w_ffn_out": _mm(act, dy2, "tn", BF16, "ffn_out_dw")}
    dw["w_ffn_in"] = _mm(dgu, h2, "tn", BF16, "ffn_in_dw")
    gate_mix = gate_mix + send({n: dw.pop(n) for n in ("w_ffn_in", "w_ffn_out")})[0, 0]

    def mid(dh, x1v, dx2v, yv, gpre, sc, gate, gpost):
        r2 = _rstd(x1v)
        x1n = x1v * r2
        t = dh * x1n
        dx1 = dx2v + _norm_bwd(dh * (gpre * (1.0 + sc)), x1n, r2)
        ry = _rstd(yv)
        yn = yv * ry
        dn1 = dx1 * gate
        dy = _norm_bwd(dn1 * gpost, yn, ry)
        sums = (_colsum(dh), _colsum(t) * gpre, _colsum(t) * (1.0 + sc), _colsum(dx1 * (yn * gpost)), _colsum(dn1 * yn))
        return (dx1, dy), sums
    dx1, dy, d_shift_ffn, d_scale_ffn, d_g_pre_ffn, d_gate_mix, d_g_post_mix = _mm_epi(
        dgu, w["w_ffn_in"], "nt", D, mid, "ffn_in_dx", 512, rows=[(x1, D), (dx2, D), (y, D)],
        vecs=[g_pre_ffn, scale_ffn, gate_mix, g_post_mix], outs=[(D, D, F32), (D, D, BF16)], sums=[D] * 5)

    dw["w_out"] = _mm(merged, dy, "tn", BF16, "out_proj_dw")

    def merge_bwd(dm, gf, gm, pa_, pb_):
        sf, sm = _sigmoid(gf), _sigmoid(gm)
        return (dm * sf, dm * sm, dm * pa_ * (sf * (1.0 - sf)), dm * pb_ * (sm * (1.0 - sm))), ()
    dpa, dpb, dgf, dgm = _mm_epi(dy, w["w_out"], "nt", 512, merge_bwd, "out_proj_dx", 1024,
                                 rows=[(proj_a, 512, 2), (proj_a, 512, 4), (pa, 512), (pb, 512)],
                                 outs=[(D, 512, BF16)] * 4)
    do_a = _mm(dpa, w["w_pf"], "nt", BF16, "proj_fox_dx")
    do_b = _mm(dpb, w["w_pm"], "nt", BF16, "proj_mla_dx")
    dw["w_pf"] = _mm(o_a, dpa, "tn", BF16, "proj_fox_dw")
    dw["w_pm"] = _mm(o_b, dpb, "tn", BF16, "proj_mla_dw")
    bias = bias + send({n: dw.pop(n) for n in ("w_out", "w_pf", "w_pm")})[0, 0]

    sc_a, sc_b = 1.0 / math.sqrt(HEAD_DIM), 1.0 / math.sqrt(64 + ROPE_DIM)
    dq_a, dk_a, dv_a, dbias = _attn_grad(qkv, 0, qkv, 4, qkv, 8, do_a, lse_a, LANES, sc_a, bias, BF16, "fox_attn_bwd")
    dq_m, dk_m, dv_m = _attn_grad(q_m, 0, k_m, 0, v_m, 0, do_b, lse_b, 2 * LANES, sc_b, None, F32, "mla_attn_bwd")

    def mla_rope_bwd(dq, dk, c_, a_, b_):
        lane = lax.broadcasted_iota(jnp.int32, (1, LANES), 1)
        dqs = [_rope_t(dq[:, hd * LANES:(hd + 1) * LANES], c_, a_, b_) for hd in range(HEADS)]
        dkpe = dk[:, 0:LANES]
        for hd in range(1, HEADS):
            dkpe = dkpe + dk[:, hd * LANES:(hd + 1) * LANES]
        dkpe = jnp.where((lane >= 64) & (lane < 96), dkpe, 0.0)
        dkr = jnp.where((lane >= 64) & (lane < 96), _rope_t(dkpe, c_, a_, b_), 0.0)
        return (jnp.concatenate(dqs, axis=1), dk, dkr), ()
    dqb, dkb, dkr = _rowwise(mla_rope_bwd, [(dq_m, D, 0), (dk_m, D, 0), (ct, LANES, 0), (sa, LANES, 0), (sb, LANES, 0)],
                             [], [(D, BF16), (D, BF16), (LANES, F32)], [], "mla_rope_bwd")
    def lora_q_bwd(dq, cq, gq):
        rq = _rstd(cq)
        cqh = cq * rq
        return (_norm_bwd(dq * gq, cqh, rq),), (_colsum(dq * cqh),)
    dcq, d_g_q = _mm_epi(dqb, w["w_uq"], "nt", Q_LORA, lora_q_bwd, "mla_uq_dx", 512, rows=[(proj_a, Q_LORA, 0)],
                         vecs=[g_q], outs=[(Q_LORA, Q_LORA, BF16)], sums=[Q_LORA])

    def lora_kv_bwd(dv_part, dk_part, ckv, gkv):
        dkv = dv_part + dk_part
        rk = _rstd(ckv)
        ckh = ckv * rk
        return (_norm_bwd(dkv * gkv, ckh, rk),), (_colsum(dkv * ckh),)
    dckv, d_g_kv = _mm_epi(dv_m, w["w_v"], "nt", KV_LORA, lora_kv_bwd, "mla_uv_dx", 1024,
                           rows=[(_mm(dkb, w["w_k"], "nt", F32, "mla_uk_dx"), KV_LORA), (proj_a, KV_LORA, 3)],
                           vecs=[g_kv], outs=[(KV_LORA, KV_LORA, BF16)], sums=[KV_LORA])

    dzt, d_bf = _fox_gates_bwd(dbias.reshape(HEADS, S), zt, bf)
    dmisc = (dkr + jnp.pad(jnp.transpose(dzt), ((0, 0), (0, LANES - HEADS)))).astype(BF16)
    dproj_a = jnp.concatenate([dcq, dckv, dgf, dgm, dmisc], axis=1)
    dqkv = jnp.concatenate([dq_a, dk_a, dv_a], axis=1)
    dw["w_a"] = _mm(h, dproj_a, "tn", BF16, "in_proj_a_dw")
    dw["w_b"] = _mm(h, dqkv, "tn", BF16, "in_proj_b_dw")
    tok = send(dw, True)
    dh_a = _mm(dproj_a, w["w_a"], "nt", F32, "in_proj_a_dx", dep=tok)
    tok = relay(dh_a)
    tok = send({"w_uq": _mm(cqn, dqb, "tn", BF16, "mla_uq_dw", dep=tok),
                "w_k": _mm(ckvn, dkb, "tn", BF16, "mla_uk_dw", dep=tok),
                "w_v": _mm(ckvn, dv_m, "tn", BF16, "mla_uv_dw", dep=tok)}, late=True)
    g_pre_mix = g_pre_mix + tok[0, 0]

    def first(dh_b, dh_a, xv, dx1v, gpre, sc):
        dhv = dh_b + dh_a
        r = _rstd(xv)
        xn = xv * r
        t = dhv * xn
        dx = dx1v + _norm_bwd(dhv * (gpre * (1.0 + sc)), xn, r)
        return (dx,), (_colsum(dhv), _colsum(t) * gpre, _colsum(t) * (1.0 + sc))
    grad_x, d_shift_mix, d_scale_mix, d_g_pre_mix = _mm_epi(
        dqkv, w["w_b"], "nt", D, first, "in_proj_b_dx", 512,
        rows=[(dh_a, D), (x, D), (dx1, D)],
        vecs=[g_pre_mix, scale_mix], outs=[(D, D, F32)], sums=[D] * 3)

    dmod = jnp.concatenate([d_shift_mix, d_scale_mix, d_gate_mix, d_shift_ffn, d_scale_ffn, d_gate_ffn], axis=1)
    small = dict(dmod=dmod, g_pre_mix=d_g_pre_mix, g_post_mix=d_g_post_mix, g_pre_ffn=d_g_pre_ffn,
                 g_post_ffn=d_g_post_ffn, g_q_lora=d_g_q, g_kv_lora=d_g_kv,
                 b_forget=jnp.pad(jnp.transpose(d_bf), ((0, 0), (0, LANES - HEADS))), err=err_cols)
    return grad_x, small


SMALL_ORDER = ("dmod", "g_pre_mix", "g_post_mix", "g_pre_ffn", "g_post_ffn", "g_q_lora", "g_kv_lora", "b_forget", "err")
SMALL_PARAM = {"dmod": "b_ada"}
MATRICES = ("w_in", "w_uq", "w_ukv", "w_proj_fox", "w_proj_mla", "w_out", "w_ffn_in", "w_ffn_out")
WEIGHTS = ("w_ada", "b_ada", "g_pre_mix", "g_post_mix", "g_pre_ffn", "g_post_ffn", "w_in", "b_forget", "g_q_lora",
           "w_uq", "g_kv_lora", "w_ukv", "w_proj_fox", "w_proj_mla", "w_out", "w_ffn_in", "w_ffn_out")


def kernel(x, c, positions, w_ada, b_ada, g_pre_mix, g_post_mix, g_pre_ffn, g_post_ffn, w_in, b_forget, g_q_lora, w_uq, g_kv_lora, w_ukv, w_proj_fox, w_proj_mla, w_out, w_ffn_in, w_ffn_out, loss_target, m_w_ada, m_b_ada, m_g_pre_mix, m_g_post_mix, m_g_pre_ffn, m_g_post_ffn, m_w_in, m_b_forget, m_g_q_lora, m_w_uq, m_g_kv_lora, m_w_ukv, m_w_proj_fox, m_w_proj_mla, m_w_out, m_w_ffn_in, m_w_ffn_out, v_w_ada, v_b_ada, v_g_pre_mix, v_g_post_mix, v_g_pre_ffn, v_g_post_ffn, v_w_in, v_b_forget, v_g_q_lora, v_w_uq, v_g_kv_lora, v_w_ukv, v_w_proj_fox, v_w_proj_mla, v_w_out, v_w_ffn_in, v_w_ffn_out):
    prm = dict(w_ada=w_ada, b_ada=b_ada, g_pre_mix=g_pre_mix, g_post_mix=g_post_mix, g_pre_ffn=g_pre_ffn,
               g_post_ffn=g_post_ffn, w_in=w_in, b_forget=b_forget, g_q_lora=g_q_lora, w_uq=w_uq, g_kv_lora=g_kv_lora,
               w_ukv=w_ukv, w_proj_fox=w_proj_fox, w_proj_mla=w_proj_mla, w_out=w_out, w_ffn_in=w_ffn_in, w_ffn_out=w_ffn_out)
    mom = dict(w_ada=m_w_ada, b_ada=m_b_ada, g_pre_mix=m_g_pre_mix, g_post_mix=m_g_post_mix, g_pre_ffn=m_g_pre_ffn,
               g_post_ffn=m_g_post_ffn, w_in=m_w_in, b_forget=m_b_forget, g_q_lora=m_g_q_lora, w_uq=m_w_uq,
               g_kv_lora=m_g_kv_lora, w_ukv=m_w_ukv, w_proj_fox=m_w_proj_fox, w_proj_mla=m_w_proj_mla, w_out=m_w_out,
               w_ffn_in=m_w_ffn_in, w_ffn_out=m_w_ffn_out)
    var = dict(w_ada=v_w_ada, b_ada=v_b_ada, g_pre_mix=v_g_pre_mix, g_post_mix=v_g_post_mix, g_pre_ffn=v_g_pre_ffn,
               g_post_ffn=v_g_post_ffn, w_in=v_w_in, b_forget=v_b_forget, g_q_lora=v_g_q_lora, w_uq=v_w_uq,
               g_kv_lora=v_g_kv_lora, w_ukv=v_w_ukv, w_proj_fox=v_w_proj_fox, w_proj_mla=v_w_proj_mla, w_out=v_w_out,
               w_ffn_in=v_w_ffn_in, w_ffn_out=v_w_ffn_out)
    me = _flat(*_coords())
    slot = jnp.reshape(me, (1,)).astype(jnp.int32)

    own = {n: prm[n][0].astype(BF16) for n in MATRICES}
    no_dep = jnp.zeros((8, LANES), F32)
    (st_c, st_in), tok = _async_start([[c], [own["w_in"]]], ["gather", "spread"], no_dep, "gather_in_start")
    (c_own,), (c_land,) = _async_wait(st_c, tok, "gather_c_wait")
    c_all = _with_own(c_land, c_own, me).reshape(N_DEV, D)
    ada_cols = w_ada.shape[2]
    b_cols = lax.dynamic_slice(b_ada, (0, me * ada_cols), (1, ada_cols))
    mod_cols, silu_c = _mod_part(c_all, w_ada[0], b_cols)
    (mod_all,) = _all_gather([mod_cols], "gather_mod")

    (w_in_own,), (w_in_land,) = _async_wait(st_in, mod_all, "gather_in_wait")
    (st_in,), tok = _async_start([[w_in_land]], "forward", no_dep, "gather_in_forward")
    _, (w_in_land,) = _async_wait(st_in, tok, "gather_in_forward_wait")
    w = _prepare_weights({"w_in": w_in_land}, {"w_in": w_in_own}, slot)
    later = dict(lora=("w_uq", "w_ukv"), proj=("w_proj_fox", "w_proj_mla", "w_out"), ffn=("w_ffn_in", "w_ffn_out"))
    states, tok = _async_start([[own[n] for n in names] for names in later.values()], ["gather", "spread", "spread"],
                               w["w_b"], "gather_rest_start")
    gather_state = dict(zip(later, states))
    own_thru = {}

    def wts(group, after):
        if group.startswith("relay_"):
            name = group[len("relay_"):]
            own_thru[name], lands = _async_wait(gather_state[name], after, "gather_" + name + "_wait")
            (gather_state[name],), t = _async_start([lands], "forward", no_dep, "gather_" + name + "_forward")
            return {"tok": t}
        srcs, lands = _async_wait(gather_state[group], after, "gather_" + group + "_landed")
        srcs = own_thru.get(group, srcs)
        return _prepare_weights(dict(zip(later[group], lands)), dict(zip(later[group], srcs)), slot)

    sent, late_sent, last = [], [], {}

    def send(grads, final=False, late=False):
        shards = _shard_grads(grads)
        names = list(shards)
        (state,), t = _async_start([[shards[n] for n in names]], "pair" if final else "exchange", no_dep,
                                   "exchange_" + names[0] + "_start")
        if final:
            last.update(names=names, state=state)
        else:
            (late_sent if late else sent).append((names, state))
        return t

    def relay(after):
        srcs, lands = _async_wait(last["state"], after, "exchange_pair_wait")
        core = jnp.reshape(lax.axis_index("c"), (1,)).astype(jnp.int32)
        sums = [_add_sibling(src, land, core) for src, land in zip(srcs, lands)]
        (last["state"],), t = _async_start([sums], "chips", no_dep, "exchange_chips_start")
        return t

    mod = lax.dynamic_index_in_dim(mod_all, me, axis=1, keepdims=False).reshape(1, 6 * D) + tok[0, 0]

    vec = dict(g_pre_mix=g_pre_mix, g_post_mix=g_post_mix, g_pre_ffn=g_pre_ffn, g_post_ffn=g_post_ffn,
               g_q_lora=g_q_lora, g_kv_lora=g_kv_lora, b_forget=b_forget)
    pos = positions.astype(F32).reshape(S, 1)
    grad_x, small = _fwd_bwd(x[0], pos, mod, loss_target[0], w, vec, wts, send, relay)

    bundle = jnp.concatenate([small[n] for n in SMALL_ORDER], axis=1)
    (small_state,), tok = _async_start([[bundle]], "gather", jnp.zeros((8, LANES), F32), "gather_small_start")

    out = {}
    swap = lambda a: jnp.swapaxes(a, -1, -2)

    def update(n, land, src, sl):
        if n != "w_ffn_in":
            out[n] = _adamw(prm[n][0], mom[n][0], var[n][0], land, "adamw_" + n, src, sl)
            return out[n][0]
        res = _adamw(swap(prm[n][0]), swap(mom[n][0]), swap(var[n][0]), land, "adamw_" + n, src, sl)
        out[n] = tuple(swap(t) for t in res)
        return res[0]

    after = tok
    for names, state in sent:
        srcs, lands = _async_wait(state, after, "exchange_" + names[0] + "_wait")
        for n, src, land in zip(names, srcs, lands):
            after = update(n, land, src, slot)
    srcs, lands = _async_wait(last["state"], after, "exchange_chips_wait")
    for n, src, land in zip(last["names"], srcs, lands):
        after = update(n, land, src, slot // 2)
    for names, state in late_sent:
        srcs, lands = _async_wait(state, after, "exchange_" + names[0] + "_wait")
        for n, src, land in zip(names, srcs, lands):
            after = update(n, land, src, slot)

    (own_bundle,), (bundle_all,) = _async_wait(small_state, after, "gather_small_wait")
    bundle_all = _with_own(bundle_all, own_bundle, me)
    dmod_all = bundle_all[:, 0, :6 * D]
    dm_cols = lax.dynamic_slice(dmod_all, (0, me * ada_cols), (N_DEV, ada_cols))
    out["w_ada"] = _adamw_w_ada(w_ada[0], m_w_ada[0], v_w_ada[0], jnp.transpose(silu_c), dm_cols)

    offsets, off = {}, 0
    for n in SMALL_ORDER:
        offsets[n] = off
        off += small[n].shape[1]
    names = [SMALL_PARAM.get(n, n) for n in SMALL_ORDER if n != "err"]
    results, err = _adamw_rows(bundle_all, [offsets[n] for n in SMALL_ORDER if n != "err"],
                               [prm[n] for n in names], [mom[n] for n in names], [var[n] for n in names],
                               offsets["err"], D)
    out.update(zip(names, results))
    loss = 0.5 * jnp.sum(err) / D

    res = [loss, grad_x[None]]
    for kind in range(4):
        for n in WEIGHTS:
            t = out[n][kind]
            res.append(t[None] if prm[n].ndim == 3 else t)
    return tuple(res)
```

```python
import functools
import math

import jax
import jax.numpy as jnp
from jax import lax
from jax.experimental import pallas as pl
from jax.experimental.pallas import tpu as pltpu

F32 = jnp.float32
BF16 = jnp.bfloat16

N_DEV = 8
S = 2048
D = 1024
D_FF = 2816
HEADS = 8
HEAD_DIM = 64
Q_LORA = 768
KV_LORA = 256
ROPE_DIM = 32
ROPE_THETA = 10000.0
NORM_EPS = 1e-6
LANES = 128
VMEM_LIMIT = 56 * 1024 * 1024

ADAM_LR = 0.001
ADAM_B1 = 0.9
ADAM_B2 = 0.999
ADAM_EPS = 1e-08
ADAM_WD = 0.01
ADAM_STEP = 10

ATT_T = 256
LOG2E = 1.4426950408889634
N_ATT = S // ATT_T

NN = (((1,), (0,)), ((), ()))
NT = (((1,), (1,)), ((), ()))
TN = (((0,), (0,)), ((), ()))
MESH = pl.DeviceIdType.MESH


def _params(sem=None):
    return pltpu.CompilerParams(dimension_semantics=sem, vmem_limit_bytes=VMEM_LIMIT)


def _pick(n, cap):
    best = None
    for t in range(LANES, cap + 1, LANES):
        if n % t == 0:
            best = t
    return best if best is not None else n


def _mm(a, b, mode, out_dtype, name, acc=None, dep=None):
    if mode == "nn":
        (m, k), (k2, n), dn = a.shape, b.shape, NN
    elif mode == "nt":
        (m, k), (n, k2), dn = a.shape, b.shape, NT
    else:
        (k, m), (k2, n), dn = a.shape, b.shape, TN
    assert k == k2, (a.shape, b.shape, mode)
    tn = _pick(n, 1024)
    tm = _pick(m, 1536)
    osz = jnp.dtype(out_dtype).itemsize

    def need(tm_):
        blk = tm_ * k * 2 + tn * k * 2 + tm_ * tn * osz + (tm_ * tn * 4 if acc is not None else 0)
        return 2 * blk + tm_ * tn * 4
    while need(tm) > 36 * 1024 * 1024 and tm % 256 == 0:
        tm //= 2

    def body(*refs):
        a_ref, b_ref, o_ref = refs[0], refs[1], refs[-1]
        r = lax.dot_general(a_ref[...], b_ref[...], dn, preferred_element_type=F32)
        if acc is not None:
            r = r + refs[2][...]
        o_ref[...] = r.astype(o_ref.dtype)

    if mode == "tn":
        a_spec = pl.BlockSpec((k, tm), lambda i, j: (0, i))
    else:
        a_spec = pl.BlockSpec((tm, k), lambda i, j: (i, 0))
    if mode == "nt":
        b_spec = pl.BlockSpec((tn, k), lambda i, j: (j, 0))
    else:
        b_spec = pl.BlockSpec((k, tn), lambda i, j: (0, j))
    o_spec = pl.BlockSpec((tm, tn), lambda i, j: (i, j))
    in_specs = [a_spec, b_spec] + ([o_spec] if acc is not None else [])
    in_specs += [pl.BlockSpec(memory_space=pl.ANY)] if dep is not None else []
    args = (a, b) + ((acc,) if acc is not None else ()) + ((dep,) if dep is not None else ())
    return pl.pallas_call(
        body, name=name, grid=(m // tm, n // tn),
        in_specs=in_specs, out_specs=o_spec,
        out_shape=jax.ShapeDtypeStruct((m, n), out_dtype),
        compiler_params=_params(("parallel", "parallel")),
    )(*args)


def _mm_epi(a, b, mode, tnb, epi, name, tm, rows=(), vecs=(), outs=(), sums=(), pro=None):
    m = a.shape[0]
    k, nb = (b.shape if mode == "nn" else b.shape[::-1])
    dn = NN if mode == "nn" else NT
    pro_fn, pro_vecs, a_off = pro if pro is not None else (None, (), 0)
    n_in = 2 + len(rows) + len(vecs)
    n_all = n_in + len(pro_vecs)
    sub = min(tm, 256)

    def body(*refs):
        if pro is not None:
            a_out, a_scr = refs[-2:]
            refs = refs[:-2]

            @pl.when(pl.program_id(1) == 0)
            def _():
                a_scr[...] = pro_fn(refs[0][...], *[x[...] for x in refs[n_in:n_all]]).astype(BF16)
                a_out[...] = a_scr[...]
            a_ref = a_scr
        else:
            a_ref = refs[0]
        o_refs = refs[n_all:n_all + len(outs)]
        s_refs = refs[n_all + len(outs):]
        if sums:
            @pl.when((pl.program_id(0) == 0) & (pl.program_id(1) == 0))
            def _():
                for s_ref in s_refs:
                    s_ref[...] = jnp.zeros(s_ref.shape, F32)
        for c in range(tm // sub):
            rs = slice(c * sub, (c + 1) * sub)
            r = lax.dot_general(a_ref[rs, :], refs[1][...], dn, preferred_element_type=F32)
            o_vals, s_vals = epi(r, *[x[rs, :] for x in refs[2:2 + len(rows)]], *[x[...] for x in refs[2 + len(rows):n_in]])
            assert len(o_vals) == len(o_refs) and len(s_vals) == len(s_refs)
            for o_ref, val in zip(o_refs, o_vals):
                o_ref[rs, :] = val.astype(o_ref.dtype)
            for s_ref, val in zip(s_refs, s_vals):
                s_ref[...] += val

    once = dict(pipeline_mode=pl.Buffered(1)) if nb == tnb else {}
    if mode == "nn":
        b_spec = pl.BlockSpec((k, tnb), lambda i, j: (0, j), **once)
    else:
        b_spec = pl.BlockSpec((tnb, k), lambda i, j: (j, 0), **once)
    in_specs = [pl.BlockSpec((tm, k), lambda i, j: (i, a_off)), b_spec]
    rows = [tuple(r) + (0,) * (3 - len(r)) for r in rows]
    in_specs += [pl.BlockSpec((tm, w), functools.partial(lambda i, j, off: (i, j + off), off=off)) for _, w, off in rows]
    in_specs += [pl.BlockSpec(v.shape, lambda i, j: (0, 0)) for v in list(vecs) + list(pro_vecs)]
    out_specs = [pl.BlockSpec((tm, w), lambda i, j: (i, j)) for _, w, _ in outs]
    out_specs += [pl.BlockSpec((1, w), lambda i, j: (0, 0)) for w in sums]
    out_shape = [jax.ShapeDtypeStruct((m, full), dt) for full, _, dt in outs]
    out_shape += [jax.ShapeDtypeStruct((1, w), F32) for w in sums]
    if pro is not None:
        out_specs.append(pl.BlockSpec((tm, k), lambda i, j: (i, 0)))
        out_shape.append(jax.ShapeDtypeStruct((m, k), BF16))
    return pl.pallas_call(
        body, name=name, grid=(m // tm, nb // tnb),
        in_specs=in_specs, out_specs=out_specs, out_shape=out_shape,
        scratch_shapes=[pltpu.VMEM((tm, k), BF16)] if pro is not None else [],
        compiler_params=_params(("arbitrary", "arbitrary") if sums else ("parallel", "arbitrary" if pro is not None else "parallel")),
    )(a, b, *[r[0] for r in rows], *vecs, *pro_vecs)


def _rowwise(fn, row_ins, vec_ins, row_outs, sum_outs, name, tm=256):
    n_in = len(row_ins) + len(vec_ins)
    n_o = len(row_outs)
    rows = row_ins[0][0].shape[0]

    def body(*refs):
        vals = [r[...] for r in refs[:n_in]]
        outs = refs[n_in:]
        ro, so = fn(*vals)
        assert len(ro) == n_o and len(so) == len(sum_outs)
        for r, v in zip(outs[:n_o], ro):
            r[...] = v.astype(r.dtype)
        if sum_outs:
            @pl.when(pl.program_id(0) == 0)
            def _():
                for r in outs[n_o:]:
                    r[...] = jnp.zeros(r.shape, F32)
            for r, v in zip(outs[n_o:], so):
                r[...] += v

    in_specs = [pl.BlockSpec((tm, w), functools.partial(lambda i, b: (i, b), b=b)) for _, w, b in row_ins]
    in_specs += [pl.BlockSpec(v.shape, lambda i: (0, 0)) for v in vec_ins]
    out_specs = [pl.BlockSpec((tm, w), lambda i: (i, 0)) for w, _ in row_outs]
    out_specs += [pl.BlockSpec((1, w), lambda i: (0, 0)) for w in sum_outs]
    out_shape = [jax.ShapeDtypeStruct((rows, w), dt) for w, dt in row_outs]
    out_shape += [jax.ShapeDtypeStruct((1, w), F32) for w in sum_outs]
    return pl.pallas_call(
        body, name=name, grid=(rows // tm,),
        in_specs=in_specs, out_specs=out_specs, out_shape=out_shape,
        compiler_params=_params(("arbitrary",)),
    )(*[a for a, _, _ in row_ins], *vec_ins)


def _sigmoid(x):
    return 1.0 / (1.0 + jnp.exp(-x))


def _rstd(x):
    return lax.rsqrt(jnp.mean(x * x, axis=-1, keepdims=True) + NORM_EPS)


def _norm_bwd(dyn, xn, r):
    return r * (dyn - xn * jnp.mean(dyn * xn, axis=-1, keepdims=True))


def _colsum(x):
    return jnp.sum(x, axis=0, keepdims=True)


def _rope_tables(pos, invf):
    def fn(p, f):
        lane = lax.broadcasted_iota(jnp.int32, (1, LANES), 1)
        ang = p * f
        cs, sn = jnp.cos(ang), jnp.sin(ang)
        rot = (lane >= 64) & (lane < 96)
        ct = jnp.where(lane < 64, 1.0, jnp.where(rot, cs, 0.0))
        sa = jnp.where((lane >= 64) & (lane < 80), -sn, 0.0)
        sb = jnp.where((lane >= 80) & (lane < 96), sn, 0.0)
        return (ct, sa, sb), ()
    return _rowwise(fn, [(pos, 1, 0)], [invf], [(LANES, F32)] * 3, [], "rope_tables")


def _rope(x, ct, sa, sb):
    return x * ct + pltpu.roll(x, LANES - 16, 1) * sa + pltpu.roll(x, 16, 1) * sb


def _rope_t(x, ct, sa, sb):
    return x * ct - pltpu.roll(x, LANES - 16, 1) * sa - pltpu.roll(x, 16, 1) * sb


def _head_mask(width, hh):
    lane = lax.broadcasted_iota(jnp.int32, (1, width), 1)
    half = width // 2
    return (lane >= hh * half) & (lane < (hh + 1) * half)


ATT_PP = 2
ATT_CHAINS = [(a, hh) for a in range(ATT_PP) for hh in range(2)]
ATT_G = HEADS // (2 * ATT_PP)


def _pair(ref_or_val, a, width, rows=slice(None)):
    return ref_or_val[rows, a * width:(a + 1) * width]


def _attn_fwd(q, qo, k, ko, v, vo, dkp, scale, bias, name):
    T = ATT_T
    assert qo % ATT_PP == 0 and ko % ATT_PP == 0 and vo % ATT_PP == 0
    qo, ko, vo = qo // ATT_PP, ko // ATT_PP, vo // ATT_PP

    def body(*refs):
        if bias is not None:
            q_ref, k_ref, v_ref, b_ref, o_ref, lse_ref, s_scr = refs
        else:
            q_ref, k_ref, v_ref, o_ref, lse_ref, s_scr = refs
        i = pl.program_id(1)
        row = lax.broadcasted_iota(jnp.int32, (T, T), 0)
        col = lax.broadcasted_iota(jnp.int32, (T, T), 1)
        qms = []
        for a, hh in ATT_CHAINS:
            qb = _pair(q_ref, a, dkp)
            qms.append(jnp.where(_head_mask(dkp, hh), qb, jnp.zeros_like(qb)))

        def fold(t):
            return [t[:, c * LANES:(c + 1) * LANES] for c in range(T // LANES)]

        def run(nt):
            mls = [jnp.full((T, LANES), -jnp.inf, F32) for _ in ATT_CHAINS]
            for j in range(nt):
                ks = slice(j * T, (j + 1) * T)
                for ci, (a, hh) in enumerate(ATT_CHAINS):
                    s = lax.dot_general(qms[ci], _pair(k_ref, a, dkp, ks), NT, preferred_element_type=F32) * (scale * LOG2E)
                    if bias is not None:
                        s = s + b_ref[2 * a + hh, j] * LOG2E
                    if j == nt - 1:
                        s = jnp.where(row >= col, s, -jnp.inf)
                    s_scr[ci, j] = s
                    for part in fold(s):
                        mls[ci] = jnp.maximum(mls[ci], part)
            ms = [jnp.max(ml, axis=1, keepdims=True) for ml in mls]
            mbs = [jnp.broadcast_to(m, (T, LANES)) for m in ms]
            for a in range(ATT_PP):
                ls = [jnp.zeros((T, LANES), F32) for _ in range(2)]
                ps, vms = [], []
                for j in range(nt):
                    vb = _pair(v_ref, a, LANES, slice(j * T, (j + 1) * T))
                    for hh in range(2):
                        parts = [jnp.exp2(part - mbs[2 * a + hh]) for part in fold(s_scr[2 * a + hh, j])]
                        for part in parts:
                            ls[hh] = ls[hh] + part
                        ps.append(jnp.concatenate(parts, axis=1).astype(BF16))
                        vms.append(jnp.where(_head_mask(LANES, hh), vb, jnp.zeros_like(vb)))
                acc = lax.dot_general(jnp.concatenate(ps, axis=1), jnp.concatenate(vms, axis=0), NN,
                                      preferred_element_type=F32)
                l0, l1 = [jnp.sum(l, axis=1, keepdims=True) for l in ls]
                lse_ref[2 * a] = ms[2 * a] + jnp.log2(l0)
                lse_ref[2 * a + 1] = ms[2 * a + 1] + jnp.log2(l1)
                inv = jnp.where(_head_mask(LANES, 0), 1.0 / l0, 1.0 / l1)
                o_ref[:, a * LANES:(a + 1) * LANES] = (acc * inv).astype(o_ref.dtype)

        for nt in range(1, N_ATT + 1):
            pl.when(i == nt - 1)(functools.partial(run, nt))

    in_specs = [
        pl.BlockSpec((T, ATT_PP * dkp), lambda g, i: (i, qo + g)),
        pl.BlockSpec((S, ATT_PP * dkp), lambda g, i: (0, ko + g)),
        pl.BlockSpec((S, ATT_PP * LANES), lambda g, i: (0, vo + g)),
    ]
    args = [q, k, v]
    if bias is not None:
        in_specs.append(pl.BlockSpec((2 * ATT_PP, N_ATT, 1, T), lambda g, i: (g, 0, 0, 0)))
        args.append(bias)
    return pl.pallas_call(
        body, name=name, grid=(ATT_G, N_ATT),
        in_specs=in_specs,
        out_specs=[pl.BlockSpec((T, ATT_PP * LANES), lambda g, i: (i, g)),
                   pl.BlockSpec((2 * ATT_PP, T, 1), lambda g, i: (g, i, 0))],
        out_shape=[jax.ShapeDtypeStruct((S, HEADS * HEAD_DIM), BF16),
                   jax.ShapeDtypeStruct((HEADS, S, 1), F32)],
        scratch_shapes=[pltpu.VMEM((len(ATT_CHAINS), N_ATT, T, T), F32)],
        compiler_params=_params(("parallel", "arbitrary")),
    )(*args)


def _attn_grad(q, qo, k, ko, v, vo, do, lse, dkp, scale, bias, qk_dtype, name):
    T = ATT_T
    has_b = bias is not None
    qo, ko, vo = qo // ATT_PP, ko // ATT_PP, vo // ATT_PP
    n_ch = len(ATT_CHAINS)

    def body(*refs):
        q_ref, k_ref, v_ref, do_ref, lse_ref = refs[:5]
        refs = refs[5:]
        if has_b:
            b_ref, refs = refs[0], refs[1:]
        dq_ref, dk_ref, dv_ref = refs[:3]
        refs = refs[3:]
        if has_b:
            db_ref, refs = refs[0], refs[1:]
        p_scr, dp_scr, dk_acc, dv_acc = refs[:4]
        db_acc = refs[4] if has_b else None
        i = pl.program_id(1)

        @pl.when(i == 0)
        def _():
            dk_acc[...] = jnp.zeros(dk_acc.shape, F32)
            dv_acc[...] = jnp.zeros(dv_acc.shape, F32)
            if has_b:
                db_acc[...] = jnp.zeros(db_acc.shape, F32)

        row = lax.broadcasted_iota(jnp.int32, (T, T), 0)
        col = lax.broadcasted_iota(jnp.int32, (T, T), 1)

        def fold(t):
            return [t[:, c * LANES:(c + 1) * LANES] for c in range(T // LANES)]

        qms, doms, lses = [], [], []
        for a, hh in ATT_CHAINS:
            qb, dob = _pair(q_ref, a, dkp), _pair(do_ref, a, LANES)
            qms.append(jnp.where(_head_mask(dkp, hh), qb, jnp.zeros_like(qb)))
            doms.append(jnp.where(_head_mask(LANES, hh), dob, jnp.zeros_like(dob)))
            lses.append(lse_ref[2 * a + hh])

        def run(nt):
            dls = [jnp.zeros((T, LANES), F32) for _ in ATT_CHAINS]
            for j in range(nt):
                ks = slice(j * T, (j + 1) * T)
                for ci, (a, hh) in enumerate(ATT_CHAINS):
                    s = lax.dot_general(qms[ci], _pair(k_ref, a, dkp, ks), NT, preferred_element_type=F32) * (scale * LOG2E)
                    if has_b:
                        s = s + b_ref[ci, j] * LOG2E
                    s = s - lses[ci]
                    if j == nt - 1:
                        s = jnp.where(row >= col, s, -jnp.inf)
                    p = jnp.exp2(s)
                    dp = lax.dot_general(doms[ci], _pair(v_ref, a, LANES, ks), NT, preferred_element_type=F32)
                    p_scr[ci, j] = p
                    dp_scr[ci, j] = dp
                    for part in fold(p * dp):
                        dls[ci] = dls[ci] + part
            deltas = [jnp.broadcast_to(jnp.sum(dl, axis=1, keepdims=True), (T, LANES)) for dl in dls]
            for a in range(ATT_PP):
                ds_all, km_all = [], []
                qm2t = jnp.transpose(jnp.concatenate([qms[2 * a], qms[2 * a + 1]], axis=0))
                dom2t = jnp.transpose(jnp.concatenate([doms[2 * a], doms[2 * a + 1]], axis=0))
                for j in range(nt):
                    ks = slice(j * T, (j + 1) * T)
                    kb = _pair(k_ref, a, dkp, ks)
                    p2, ds2 = [], []
                    for hh in range(2):
                        ci = 2 * a + hh
                        p = p_scr[ci, j]
                        ds = jnp.concatenate([pp * (dd - deltas[ci]) for pp, dd in zip(fold(p), fold(dp_scr[ci, j]))], axis=1)
                        if has_b:
                            db_acc[ci, j] += jnp.sum(ds, axis=0, keepdims=True)
                        p2.append(p.astype(BF16))
                        ds2.append((ds * scale).astype(BF16))
                        km_all.append(jnp.where(_head_mask(dkp, hh), kb, jnp.zeros_like(kb)))
                    dv_acc[a * LANES:(a + 1) * LANES, ks] += lax.dot_general(
                        dom2t, jnp.concatenate(p2, axis=0), NN, preferred_element_type=F32)
                    dk_acc[a * dkp:(a + 1) * dkp, ks] += lax.dot_general(
                        qm2t, jnp.concatenate(ds2, axis=0), NN, preferred_element_type=F32)
                    ds_all += ds2
                dq = lax.dot_general(jnp.concatenate(ds_all, axis=1), jnp.concatenate(km_all, axis=0), NN,
                                     preferred_element_type=F32)
                dq_ref[:, a * dkp:(a + 1) * dkp] = dq.astype(dq_ref.dtype)

        for nt in range(1, N_ATT + 1):
            pl.when(i == nt - 1)(functools.partial(run, nt))

        @pl.when(i == N_ATT - 1)
        def _():
            dk_ref[...] = jnp.transpose(dk_acc[...]).astype(dk_ref.dtype)
            dv_ref[...] = jnp.transpose(dv_acc[...]).astype(dv_ref.dtype)
            if has_b:
                db_ref[...] = db_acc[...]

    in_specs = [
        pl.BlockSpec((T, ATT_PP * dkp), lambda g, i: (i, qo + g)),
        pl.BlockSpec((S, ATT_PP * dkp), lambda g, i: (0, ko + g)),
        pl.BlockSpec((S, ATT_PP * LANES), lambda g, i: (0, vo + g)),
        pl.BlockSpec((T, ATT_PP * LANES), lambda g, i: (i, g)),
        pl.BlockSpec((2 * ATT_PP, T, 1), lambda g, i: (g, i, 0)),
    ]
    args = [q, k, v, do, lse]
    out_specs = [
        pl.BlockSpec((T, ATT_PP * dkp), lambda g, i: (i, g)),
        pl.BlockSpec((S, ATT_PP * dkp), lambda g, i: (0, g)),
        pl.BlockSpec((S, ATT_PP * LANES), lambda g, i: (0, g)),
    ]
    width = (HEADS // 2) * dkp
    out_shape = [
        jax.ShapeDtypeStruct((S, width), qk_dtype),
        jax.ShapeDtypeStruct((S, width), qk_dtype),
        jax.ShapeDtypeStruct((S, HEADS * HEAD_DIM), BF16),
    ]
    scratch = [pltpu.VMEM((n_ch, N_ATT, T, T), F32), pltpu.VMEM((n_ch, N_ATT, T, T), F32),
               pltpu.VMEM((ATT_PP * dkp, S), F32), pltpu.VMEM((ATT_PP * LANES, S), F32)]
    if has_b:
        bspec = pl.BlockSpec((2 * ATT_PP, N_ATT, 1, T), lambda g, i: (g, 0, 0, 0))
        in_specs.append(bspec)
        args.append(bias)
        out_specs.append(bspec)
        out_shape.append(jax.ShapeDtypeStruct((HEADS, N_ATT, 1, T), F32))
        scratch.append(pltpu.VMEM((2 * ATT_PP, N_ATT, 1, T), F32))
    return pl.pallas_call(
        body, name=name, grid=(ATT_G, N_ATT),
        in_specs=in_specs, out_specs=out_specs, out_shape=out_shape, scratch_shapes=scratch,
        compiler_params=_params(("parallel", "arbitrary")),
    )(*args)


def _tri(upper):
    a = lax.broadcasted_iota(jnp.int32, (LANES, LANES), 0)
    b = lax.broadcasted_iota(jnp.int32, (LANES, LANES), 1)
    return jnp.where(a <= b if upper else a >= b, 1.0, 0.0).astype(F32)


def _fox_gates(proj, blk, bf):
    def body(m_ref, b_ref, z_out, o_ref):
        tri = _tri(True)
        carry = jnp.zeros((HEADS, 1), F32)
        for t in range(S // LANES):
            sl = slice(t * LANES, (t + 1) * LANES)
            zt = jnp.transpose(m_ref[sl, :])[:HEADS]
            z_out[:, sl] = zt
            z = zt + b_ref[...]
            logf = jnp.minimum(z, 0.0) - jnp.log(1.0 + jnp.exp(-jnp.abs(z)))
            c = lax.dot_general(logf, tri, NN, preferred_element_type=F32,
                                precision=lax.Precision.HIGHEST) + carry
            o_ref[:, sl] = -c
            carry = c[:, LANES - 1:LANES]

    return pl.pallas_call(
        body, name="fox_gates", grid=(1,),
        in_specs=[pl.BlockSpec((S, LANES), lambda i: (0, blk)), pl.BlockSpec(bf.shape, lambda i: (0, 0))],
        out_specs=[pl.BlockSpec((HEADS, S), lambda i: (0, 0))] * 2,
        out_shape=[jax.ShapeDtypeStruct((HEADS, S), F32)] * 2,
        compiler_params=_params(("arbitrary",)),
    )(proj, bf)


def _fox_gates_bwd(dbias, zt, bf):
    def body(d_ref, z_ref, b_ref, dz_ref, dbf_ref):
        tri = _tri(False)
        carry = jnp.zeros((HEADS, 1), F32)
        tot = jnp.zeros((HEADS, 1), F32)
        for t in reversed(range(S // LANES)):
            sl = slice(t * LANES, (t + 1) * LANES)
            df = -d_ref[:, sl]
            c = lax.dot_general(df, tri, NN, preferred_element_type=F32,
                                precision=lax.Precision.HIGHEST) + carry
            carry = c[:, 0:1]
            z = z_ref[:, sl] + b_ref[...]
            dz = c * _sigmoid(-z)
            dz_ref[:, sl] = dz
            tot = tot + jnp.sum(dz, axis=1, keepdims=True)
        dbf_ref[...] = tot

    return pl.pallas_call(
        body, name="fox_gates_bwd",
        out_shape=[jax.ShapeDtypeStruct((HEADS, S), F32), jax.ShapeDtypeStruct((HEADS, 1), F32)],
        compiler_params=_params(),
    )(dbias, zt, bf)


def _mod_part(c_all, w_ada, b_cols):
    def body(c_ref, w_ref, b_ref, o_ref, s_ref):
        c = c_ref[...]
        sc = c * _sigmoid(c)
        s_ref[...] = sc
        o_ref[...] = lax.dot_general(sc, w_ref[...], NN, preferred_element_type=F32,
                                     precision=lax.Precision.HIGHEST) + b_ref[...]

    return pl.pallas_call(
        body, name="mod_part",
        out_shape=[jax.ShapeDtypeStruct((N_DEV, w_ada.shape[1]), F32), jax.ShapeDtypeStruct(c_all.shape, F32)],
        compiler_params=_params(),
    )(c_all, w_ada, b_cols)


def _adamw_w_ada(w, m, v, sc_t, dm):
    rows, cols = w.shape
    tr = 256

    def body(w_ref, m_ref, v_ref, s_ref, d_ref, g_out, d_out, m_out, v_out):
        g = s_ref[:, 0:1] * d_ref[0:1, :]
        for b in range(1, N_DEV):
            g = g + s_ref[:, b:b + 1] * d_ref[b:b + 1, :]
        g_out[...] = g
        d_out[...], m_out[...], v_out[...] = _adamw_math(w_ref[...], g, m_ref[...], v_ref[...])

    spec = pl.BlockSpec((tr, cols), lambda i: (i, 0))
    return pl.pallas_call(
        body, name="adamw_w_ada", grid=(rows // tr,),
        in_specs=[spec, spec, spec, pl.BlockSpec((tr, N_DEV), lambda i: (i, 0)), pl.BlockSpec(dm.shape, lambda i: (0, 0))],
        out_specs=[spec] * 4, out_shape=[jax.ShapeDtypeStruct((rows, cols), F32)] * 4,
        compiler_params=_params(("parallel",)),
    )(w, m, v, sc_t, dm)


def _adamw(w, m, v, parts, name, own=None, slot=None):
    rows, cols = w.shape
    n = parts.shape[0]
    by_cols = rows % 256 != 0 and cols % 256 == 0
    tr, tc = (rows, 256) if by_cols else ((rows if rows <= 512 else 256), cols)
    tile = (lambda i: (0, i)) if by_cols else (lambda i: (i, 0))

    def body(*refs):
        if own is not None:
            s_ref, refs = refs[0], refs[1:]
            w_ref, m_ref, v_ref, p_ref, o_ref, g_out, d_out, m_out, v_out = refs
            terms = [jnp.where(s_ref[0] == kk, o_ref[0], p_ref[kk]) for kk in range(n)]
        else:
            w_ref, m_ref, v_ref, p_ref, g_out, d_out, m_out, v_out = refs
            terms = [p_ref[kk] for kk in range(n)]
        g = terms[0].astype(F32)
        for term in terms[1:]:
            g = g + term.astype(F32)
        g_out[...] = g
        d_out[...], m_out[...], v_out[...] = _adamw_math(w_ref[...], g, m_ref[...], v_ref[...])

    spec = pl.BlockSpec((tr, tc), lambda i, *_: tile(i))
    in_specs = [spec, spec, spec, pl.BlockSpec((n, tr, tc), lambda i, *_: (0,) + tile(i))]
    out_shape = [jax.ShapeDtypeStruct((rows, cols), F32)] * 4
    grid = (rows // tr if not by_cols else cols // tc,)
    if own is None:
        return pl.pallas_call(
            body, name=name, grid=grid, in_specs=in_specs, out_specs=[spec] * 4, out_shape=out_shape,
            compiler_params=_params(("parallel",)),
        )(w, m, v, parts)
    in_specs.append(pl.BlockSpec((1, tr, tc), lambda i, s: (s[0],) + tile(i)))
    return pl.pallas_call(
        body, name=name, out_shape=out_shape, compiler_params=_params(("parallel",)),
        grid_spec=pltpu.PrefetchScalarGridSpec(num_scalar_prefetch=1, grid=grid, in_specs=in_specs,
                                               out_specs=[spec] * 4),
    )(slot, w, m, v, parts, own)


def _adamw_math(w, g, m, v):
    mm = ADAM_B1 * m + (1.0 - ADAM_B1) * g
    vv = ADAM_B2 * v + (1.0 - ADAM_B2) * (g * g)
    m_hat = mm / (1.0 - ADAM_B1 ** ADAM_STEP)
    v_hat = vv / (1.0 - ADAM_B2 ** ADAM_STEP)
    return -ADAM_LR * (m_hat / (jnp.sqrt(v_hat) + ADAM_EPS) + ADAM_WD * w), mm, vv


def _adamw_rows(bundles, offsets, ws, ms, vs, err_off, err_width):
    k = len(ws)

    def body(*refs):
        b_ref = refs[0]
        w_refs, m_refs, v_refs = refs[1:1 + k], refs[1 + k:1 + 2 * k], refs[1 + 2 * k:1 + 3 * k]
        outs = refs[1 + 3 * k:]
        g_all = b_ref[0]
        for kk in range(1, N_DEV):
            g_all = g_all + b_ref[kk]
        for i in range(k):
            width = w_refs[i].shape[1]
            g = g_all[:, offsets[i]:offsets[i] + width]
            outs[4 * i][...] = g
            outs[4 * i + 1][...], outs[4 * i + 2][...], outs[4 * i + 3][...] = _adamw_math(
                w_refs[i][...], g, m_refs[i][...], v_refs[i][...])
        outs[4 * k][...] = g_all[:, err_off:err_off + err_width]

    out_shape = []
    for w_ in ws:
        out_shape += [jax.ShapeDtypeStruct(w_.shape, F32)] * 4
    out_shape.append(jax.ShapeDtypeStruct((1, err_width), F32))
    res = pl.pallas_call(body, name="adamw_rows", out_shape=out_shape, compiler_params=_params())(bundles, *ws, *ms, *vs)
    return [tuple(res[4 * i:4 * i + 4]) for i in range(k)], res[-1]


def _coords():
    return lax.axis_index("x"), lax.axis_index("y"), lax.axis_index("c")


def _flat(px, py, pc):
    return 4 * px + 2 * py + pc


def _all_gather(arrs, name):
    n = len(arrs)

    def body(*refs):
        ins, outs = refs[:n], refs[n:2 * n]
        send, recv, lsem = refs[2 * n:]
        x, y, c = _coords()
        me, sibling = (x, y, c), (x, y, 1 - c)
        chips = [(1 - x, y), (x, 1 - y), (1 - x, 1 - y)]

        def copy(a, kk, block, to, src=None):
            slot = outs[a].at[_flat(*block)]
            return pltpu.make_async_remote_copy(
                src_ref=slot if src is None else src, dst_ref=slot,
                send_sem=send.at[a, kk], recv_sem=recv.at[a, kk],
                device_id=to, device_id_type=MESH)

        mine = [pltpu.make_async_copy(ins[a], outs[a].at[_flat(*me)], lsem.at[a]) for a in range(n)]
        for cp in mine:
            cp.start()
        first = []
        for a in range(n):
            first.append(copy(a, 0, me, sibling, src=ins[a]))
            first += [copy(a, 1 + j, me, (*chip, c), src=ins[a]) for j, chip in enumerate(chips)]
        for cp in first:
            cp.start()
        passed = []
        for j, chip in enumerate(chips):
            for a in range(n):
                copy(a, 1 + j, (*chip, c), me).wait_recv()
                cp = copy(a, 4 + j, (*chip, c), sibling)
                cp.start()
                passed.append(cp)
        for a in range(n):
            copy(a, 0, sibling, me).wait_recv()
        for j, chip in enumerate(chips):
            for a in range(n):
                copy(a, 4 + j, (*chip, 1 - c), me).wait_recv()
        for cp in first + passed:
            cp.wait_send()
        for cp in mine:
            cp.wait()

    any_spec = pl.BlockSpec(memory_space=pl.ANY)
    return pl.pallas_call(
        body, name=name,
        in_specs=[any_spec] * n, out_specs=[any_spec] * n,
        out_shape=[jax.ShapeDtypeStruct((N_DEV,) + a.shape, a.dtype) for a in arrs],
        scratch_shapes=[pltpu.SemaphoreType.DMA((n, 7)), pltpu.SemaphoreType.DMA((n, 7)),
                        pltpu.SemaphoreType.DMA((n,))],
    )(*arrs)


def _peer_list():
    x, y, c = _coords()
    return [((1 - x if r & 4 else x), (1 - y if r & 2 else y), (1 - c if r & 1 else c)) for r in range(1, N_DEV)]


def _copy_plan(mode, src, land):
    x, y, c = _coords()
    me = _flat(x, y, c)
    if mode == "gather":
        return [(src, land.at[me], peer) for peer in _peer_list()]
    if mode == "exchange":
        return [(src.at[_flat(*peer)], land.at[me], peer) for peer in _peer_list()]
    if mode == "pair":
        return [(src.at[_flat(q // 2, q % 2, 1 - c)], land.at[q], (x, y, 1 - c)) for q in range(N_DEV // 2)]
    chips = [((1 - x if r & 2 else x), (1 - y if r & 1 else y)) for r in range(1, N_DEV // 2)]
    if mode == "chips":
        return [(src.at[2 * qx + qy], land.at[2 * x + y], (qx, qy, c)) for qx, qy in chips]
    if mode == "spread":
        return [(src, land.at[me], (x, y, 1 - c))] + [(src, land.at[me], (qx, qy, c)) for qx, qy in chips]
    assert mode == "forward"
    return [(land.at[_flat(qx, qy, c)], land.at[_flat(qx, qy, c)], (x, y, 1 - c)) for qx, qy in chips]


N_COPIES = dict(gather=N_DEV - 1, exchange=N_DEV - 1, pair=N_DEV // 2, chips=N_DEV // 2 - 1, spread=N_DEV // 2,
                forward=N_DEV // 2 - 1)


def _land_shape(mode, shape):
    return {"gather": (N_DEV,) + shape, "spread": (N_DEV,) + shape, "exchange": shape,
            "pair": (N_DEV // 2,) + shape[1:], "chips": shape}[mode]


HBM_SPEC = pl.BlockSpec(memory_space=pltpu.HBM)
SEM_SPEC = pl.BlockSpec(memory_space=pltpu.SEMAPHORE)
ANY_SPEC = pl.BlockSpec(memory_space=pl.ANY)
SIDE_EFFECT = pltpu.SideEffectType.DATAFLOW_SIDE_EFFECTING


def _async_start(groups, modes, after, name):
    modes = [modes] * len(groups) if isinstance(modes, str) else list(modes)
    arrs = [(a, m) for g, m in zip(groups, modes) for a in g]
    n = len(arrs)
    fresh = [i for i, (_, m) in enumerate(arrs) if m != "forward"]

    def body(*refs):
        srcs, new_lands = refs[:n], refs[n:n + len(fresh)]
        outs = refs[n + len(fresh) + 1:]
        lands = list(srcs)
        for k, i in enumerate(fresh):
            lands[i] = new_lands[k]
        for ai, (_, mode) in enumerate(arrs):
            for src_ref, dst_ref, peer in _copy_plan(mode, srcs[ai], lands[ai]):
                pltpu.make_async_remote_copy(src_ref=src_ref, dst_ref=dst_ref, send_sem=outs[2 * ai],
                                             recv_sem=outs[2 * ai + 1], device_id=peer, device_id_type=MESH).start()
        outs[-1][...] = jnp.zeros(outs[-1].shape, F32)

    land_shapes = [(_land_shape(arrs[i][1], arrs[i][0].shape), arrs[i][0].dtype) for i in fresh]
    n_buf = n + len(fresh)
    out_shape = [pltpu.SemaphoreType.DMA(())] * (2 * n)
    out_shape += [pltpu.HBM(a.shape, a.dtype) for a, _ in arrs]
    out_shape += [pltpu.HBM(shape, dt) for shape, dt in land_shapes]
    out_shape.append(jax.ShapeDtypeStruct((8, LANES), F32))
    res = pl.pallas_call(
        body, name=name, out_shape=tuple(out_shape),
        in_specs=[HBM_SPEC] * n_buf + [ANY_SPEC],
        out_specs=tuple([SEM_SPEC] * (2 * n) + [HBM_SPEC] * n_buf + [pl.BlockSpec(memory_space=pltpu.VMEM)]),
        input_output_aliases={i: 2 * n + i for i in range(n_buf)},
        compiler_params=pltpu.CompilerParams(has_side_effects=SIDE_EFFECT),
    )(*[pltpu.with_memory_space_constraint(a, pltpu.HBM) for a, _ in arrs],
      *[pltpu.with_memory_space_constraint(lax.empty(shape, dt), pltpu.HBM) for shape, dt in land_shapes],
      after)
    sems, thru = res[:2 * n], res[2 * n:-1]
    land_of = {i: thru[n + k] for k, i in enumerate(fresh)}
    states, idx = [], 0
    for g, mode in zip(groups, modes):
        ids = range(idx, idx + len(g))
        idx += len(g)
        states.append(([sems[2 * i] for i in ids], [sems[2 * i + 1] for i in ids],
                       None if mode == "forward" else [thru[i] for i in ids],
                       [land_of.get(i, thru[i]) for i in ids], mode))
    return states, res[-1]


def _async_wait(state, after, name):
    sends, recvs, srcs, lands, mode = state
    g = len(lands)
    bufs = (list(srcs) if srcs is not None else []) + list(lands)
    nb = len(bufs)

    def body(*refs):
        l_refs, sems = refs[nb - g:nb], refs[nb:nb + 2 * g]
        for ai in range(g):
            moved = l_refs[ai].at[pl.ds(0, N_COPIES[mode])]
            cp = pltpu.make_async_remote_copy(src_ref=moved, dst_ref=moved, send_sem=sems[ai], recv_sem=sems[g + ai],
                                              device_id=_coords(), device_id_type=MESH)
            cp.wait_send()
            cp.wait_recv()

    res = pl.pallas_call(
        body, name=name,
        out_shape=tuple(pltpu.HBM(a.shape, a.dtype) for a in bufs),
        in_specs=[HBM_SPEC] * nb + [SEM_SPEC] * (2 * g) + [ANY_SPEC],
        out_specs=tuple([HBM_SPEC] * nb),
        input_output_aliases={i: i for i in range(nb)},
        compiler_params=pltpu.CompilerParams(has_side_effects=SIDE_EFFECT),
    )(*bufs, *sends, *recvs, after)
    return (list(res[:nb - g]) if srcs is not None else None), list(res[nb - g:])


def _add_sibling(mine, theirs, core):
    def body(c_ref, a_ref, b_ref, o_ref):
        o_ref[...] = (a_ref[...].astype(F32) + b_ref[...].astype(F32)).astype(o_ref.dtype)

    blk = (1,) + mine.shape[1:]
    return pl.pallas_call(
        body, name="add_sibling", out_shape=jax.ShapeDtypeStruct(theirs.shape, mine.dtype),
        grid_spec=pltpu.PrefetchScalarGridSpec(
            num_scalar_prefetch=1, grid=(theirs.shape[0],),
            in_specs=[pl.BlockSpec(blk, lambda q, c: (2 * q + c[0], 0, 0)), pl.BlockSpec(blk, lambda q, c: (q, 0, 0))],
            out_specs=pl.BlockSpec(blk, lambda q, c: (q, 0, 0))),
        compiler_params=_params(("parallel",)),
    )(core, mine, theirs)


def _with_own(land, own, me):
    return lax.dynamic_update_index_in_dim(land, own, me, 0)


IN_SPLITS = (512, 512, 512, 8, 768, 256, 32, 1024, 1024)


def _from_shards(g, fn, out_widths, name, own=None, slot=None):
    _, k, n = g.shape
    tr = min(k, 256)

    def body(*refs):
        if own is not None:
            s_ref, g_ref, own_ref = refs[:3]
            cols = [jnp.where(s_ref[0] == j, own_ref[...], g_ref[j]) for j in range(N_DEV)]
        else:
            g_ref = refs[0]
            cols = [g_ref[j] for j in range(N_DEV)]
        for o_ref, val in zip(refs[-len(out_widths):], fn(jnp.concatenate(cols, axis=1))):
            o_ref[...] = val

    in_specs = [pl.BlockSpec((N_DEV, tr, n), lambda i, *_: (0, i, 0))]
    out_spec = [pl.BlockSpec((tr, wd), lambda i, *_: (i, 0)) for wd in out_widths]
    out_shape = [jax.ShapeDtypeStruct((k, wd), g.dtype) for wd in out_widths]
    if own is None:
        return pl.pallas_call(body, name=name, grid=(k // tr,), in_specs=in_specs, out_specs=out_spec,
                              out_shape=out_shape, compiler_params=_params(("parallel",)))(g)
    in_specs.append(pl.BlockSpec((tr, n), lambda i, *_: (i, 0)))
    return pl.pallas_call(
        body, name=name, out_shape=out_shape, compiler_params=_params(("parallel",)),
        grid_spec=pltpu.PrefetchScalarGridSpec(num_scalar_prefetch=1, grid=(k // tr,), in_specs=in_specs, out_specs=out_spec),
    )(slot, g, own)


def _unshard_cols(g, own=None, slot=None):
    return _from_shards(g, lambda full: (full,), [N_DEV * g.shape[2]], "unshard_cols_%d" % g.shape[2], own, slot)[0]


FFN_T = D_FF // 2
FFN_SHARD = 2 * D_FF // N_DEV


def _unshard_ffn_in(g, own=None, slot=None):
    def pairs(full):
        parts = []
        for j in range(D_FF // FFN_T):
            parts += [full[:, j * FFN_T:(j + 1) * FFN_T], full[:, D_FF + j * FFN_T:D_FF + (j + 1) * FFN_T]]
        return (jnp.concatenate(parts, axis=1),)

    return _from_shards(g, pairs, [2 * D_FF], "unshard_ffn_in", own, slot)[0]


def _shard_ffn_in_t(wt):
    tc = 256

    def body(w_ref, o_ref):
        x = w_ref[...]
        nb = D_FF // FFN_T
        full = jnp.concatenate([x[(2 * j + half) * FFN_T:(2 * j + half + 1) * FFN_T]
                                for half in range(2) for j in range(nb)], axis=0)
        for j in range(N_DEV):
            o_ref[j] = full[j * FFN_SHARD:(j + 1) * FFN_SHARD]

    return pl.pallas_call(
        body, name="shard_ffn_in_t", grid=(D // tc,),
        in_specs=[pl.BlockSpec((2 * D_FF, tc), lambda i: (0, i))],
        out_specs=pl.BlockSpec((N_DEV, FFN_SHARD, tc), lambda i: (0, 0, i)),
        out_shape=jax.ShapeDtypeStruct((N_DEV, FFN_SHARD, D), wt.dtype),
        compiler_params=_params(("parallel",)),
    )(wt)


def _shard_cols(w):
    k, n = w.shape[0], w.shape[1] // N_DEV
    tr = min(k, 256)

    def body(w_ref, o_ref):
        full = w_ref[...]
        for j in range(N_DEV):
            o_ref[j] = full[:, j * n:(j + 1) * n]

    return pl.pallas_call(
        body, name="shard_cols_%d" % n, grid=(k // tr,),
        in_specs=[pl.BlockSpec((tr, N_DEV * n), lambda i: (i, 0))],
        out_specs=pl.BlockSpec((N_DEV, tr, n), lambda i: (0, i, 0)),
        out_shape=jax.ShapeDtypeStruct((N_DEV, k, n), w.dtype),
        compiler_params=_params(("parallel",)),
    )(w)


IN_OFFS = tuple(sum(IN_SPLITS[:i]) for i in range(len(IN_SPLITS) + 1))
IN_SHARD = IN_OFFS[-1] // N_DEV
REGROUP_ROWS = 128
MISC_AT = Q_LORA + KV_LORA + 2 * D
A_COLS = MISC_AT + LANES
A_TILE = A_COLS
B_COLS = 3 * HEADS * HEAD_DIM
MISC_BLOCK = MISC_AT // LANES
KR_AT = 64


def _w_in_regroup(g, own=None, slot=None):
    def groups(full):
        fq, fk, fv, wf, cq, ckv, kr, gf, gm = [full[:, IN_OFFS[i]:IN_OFFS[i + 1]] for i in range(9)]
        rows = full.shape[0]
        gap = jnp.zeros((rows, KR_AT - HEADS), BF16)
        tail = jnp.zeros((rows, LANES - KR_AT - ROPE_DIM), BF16)
        return jnp.concatenate([cq, ckv, gf, gm, wf, gap, kr, tail], axis=1), jnp.concatenate([fq, fk, fv], axis=1)

    return _from_shards(g, groups, [A_COLS, B_COLS], "w_in_regroup", own, slot)


def _w_in_ungroup(da, db_):
    def body(a_ref, b_ref, o_ref):
        a = a_ref[...]
        lora = Q_LORA + KV_LORA
        full = jnp.concatenate([b_ref[...], a[:, MISC_AT:MISC_AT + HEADS], a[:, :lora],
                                a[:, MISC_AT + KR_AT:MISC_AT + KR_AT + ROPE_DIM], a[:, lora:MISC_AT]], axis=1)
        for j in range(N_DEV):
            o_ref[j] = full[:, j * IN_SHARD:(j + 1) * IN_SHARD]

    tr = REGROUP_ROWS
    return pl.pallas_call(
        body, name="w_in_ungroup", grid=(D // tr,),
        in_specs=[pl.BlockSpec((tr, A_COLS), lambda i: (i, 0)), pl.BlockSpec((tr, B_COLS), lambda i: (i, 0))],
        out_specs=pl.BlockSpec((N_DEV, tr, IN_SHARD), lambda i: (0, i, 0)),
        out_shape=jax.ShapeDtypeStruct((N_DEV, D, IN_SHARD), BF16),
        compiler_params=_params(("parallel",)),
    )(da, db_)


def _prepare_weights(g, own=None, slot=None):
    w = {}
    if own is not None:
        small = ("w_uq", "w_ukv", "w_out", "w_ffn_out")
        g = {n: (_with_own(a, own[n], slot[0]) if n in small else a) for n, a in g.items()}
    pick = (lambda n: (own[n], slot)) if own is not None else (lambda n: (None, None))
    if "w_in" in g:
        w["w_a"], w["w_b"] = _w_in_regroup(g["w_in"], *pick("w_in"))
    if "w_uq" in g:
        w_uq = g["w_uq"].reshape(Q_LORA, HEADS, 96)
        w["w_uq"] = jnp.pad(w_uq, ((0, 0), (0, 0), (0, 32))).reshape(Q_LORA, HEADS * LANES)
        ukv = g["w_ukv"]
        w["w_k"] = jnp.transpose(jnp.pad(ukv[:, :, :64], ((0, 0), (0, 0), (0, 64))), (1, 0, 2)).reshape(KV_LORA, HEADS * LANES)
        w["w_v"] = jnp.transpose(ukv[:, :, 64:], (1, 0, 2)).reshape(KV_LORA, HEADS * HEAD_DIM)
    if "w_out" in g:
        w["w_pf"] = _unshard_cols(g["w_proj_fox"], *pick("w_proj_fox"))
        w["w_pm"] = _unshard_cols(g["w_proj_mla"], *pick("w_proj_mla"))
        w["w_out"] = g["w_out"].reshape(D, D)
    if "w_ffn_in" in g:
        w["w_ffn_in"] = _unshard_ffn_in(g["w_ffn_in"], *pick("w_ffn_in"))
        w["w_ffn_out"] = g["w_ffn_out"].reshape(D_FF, D)
    return w


def _shard_grads(dw):
    out = {}
    if "w_a" in dw:
        out["w_in"] = _w_in_ungroup(dw["w_a"], dw["w_b"])
    if "w_uq" in dw:
        w_uq = dw["w_uq"].reshape(Q_LORA, HEADS, LANES)[:, :, :96].reshape(Q_LORA, Q_LORA)
        out["w_uq"] = w_uq.reshape(N_DEV, Q_LORA // N_DEV, Q_LORA)
        k_part = dw["w_k"].reshape(KV_LORA, HEADS, LANES)[:, :, :64]
        v_part = dw["w_v"].reshape(KV_LORA, HEADS, HEAD_DIM)
        out["w_ukv"] = jnp.transpose(jnp.concatenate([k_part, v_part], axis=2), (1, 0, 2))
    if "w_out" in dw:
        out["w_proj_fox"] = _shard_cols(dw["w_pf"])
        out["w_proj_mla"] = _shard_cols(dw["w_pm"])
        out["w_out"] = dw["w_out"].reshape(N_DEV, D // N_DEV, D)
    if "w_ffn_in" in dw:
        out["w_ffn_in"] = _shard_ffn_in_t(dw["w_ffn_in"])
        out["w_ffn_out"] = dw["w_ffn_out"].reshape(N_DEV, D_FF // N_DEV, D)
    return out


def _fwd_bwd(x, pos, mod, target, w, vec, wts, send, relay):
    shift_mix, scale_mix, gate_mix, shift_ffn, scale_ffn, gate_ffn = [mod[:, i * D:(i + 1) * D] for i in range(6)]
    g_pre_mix, g_post_mix, g_pre_ffn, g_post_ffn = vec["g_pre_mix"], vec["g_post_mix"], vec["g_pre_ffn"], vec["g_post_ffn"]
    g_q, g_kv = vec["g_q_lora"], vec["g_kv_lora"]

    inv_freq = 1.0 / (ROPE_THETA ** (jnp.arange(0, ROPE_DIM, 2, dtype=F32) / ROPE_DIM))
    invf = jnp.concatenate([jnp.zeros((64,), F32), inv_freq, inv_freq, jnp.zeros((32,), F32)]).reshape(1, LANES)
    ct, sa, sb = _rope_tables(pos, invf)

    def pre1(xv, g, sc, sh):
        return (xv * _rstd(xv) * g) * (1.0 + sc) + sh
    proj_a, h = _mm_epi(x, w["w_a"], "nn", A_TILE, lambda r: ((r,), ()), "in_proj_a", 512, outs=[(A_COLS, A_TILE, F32)],
                        pro=(pre1, [g_pre_mix, scale_mix, shift_mix], 0))
    qkv = _mm(h, w["w_b"], "nn", BF16, "in_proj_b")

    def lora_norm(cv, g):
        return cv * _rstd(cv) * g
    w = {**w, **wts("lora", qkv)}
    tables = [(ct, LANES), (sa, LANES), (sb, LANES)]

    def rope_q(qv, c_, a_, b_):
        return (jnp.concatenate([_rope(qv[:, hd * LANES:(hd + 1) * LANES], c_, a_, b_) for hd in range(HEADS)], axis=1),), ()
    q_m, cqn = _mm_epi(proj_a, w["w_uq"], "nn", D, rope_q, "mla_uq", 512, rows=tables, outs=[(D, D, BF16)],
                       pro=(lora_norm, [g_q], 0))

    def rope_k(kv, misc, c_, a_, b_):
        lane = lax.broadcasted_iota(jnp.int32, (1, LANES), 1)
        kpe = jnp.where((lane >= 64) & (lane < 96), _rope(misc, c_, a_, b_), 0.0)
        return (jnp.concatenate([kv[:, hd * LANES:(hd + 1) * LANES] + kpe for hd in range(HEADS)], axis=1),), ()
    k_m, ckvn = _mm_epi(proj_a, w["w_k"], "nn", D, rope_k, "mla_uk", 512, rows=[(proj_a, LANES, MISC_BLOCK)] + tables,
                        outs=[(D, D, BF16)], pro=(lora_norm, [g_kv], Q_LORA // KV_LORA))
    v_m = _mm(ckvn, w["w_v"], "nn", BF16, "mla_uv")

    bf = jnp.transpose(vec["b_forget"])
    zt, neg_f = _fox_gates(proj_a, MISC_BLOCK, bf)
    bias = neg_f.reshape(HEADS, N_ATT, 1, ATT_T)
    o_b, lse_b = _attn_fwd(q_m, 0, k_m, 0, v_m, 0, 2 * LANES, 1.0 / math.sqrt(64 + ROPE_DIM), None, "mla_attn")
    bias = bias + wts("relay_proj", o_b)["tok"][0, 0]
    o_a, lse_a = _attn_fwd(qkv, 0, qkv, 4, qkv, 8, LANES, 1.0 / math.sqrt(HEAD_DIM), bias, "fox_attn")

    w = {**w, **wts("proj", o_a)}
    gate_mix = gate_mix + wts("relay_ffn", o_a)["tok"][0, 0]
    pa = _mm(o_a, w["w_pf"], "nn", BF16, "proj_fox")

    def merge(pb_, gf, gm, pa_):
        return (_sigmoid(gf) * pa_ + _sigmoid(gm) * pb_, pb_), ()
    merged, pb = _mm_epi(o_b, w["w_pm"], "nn", 512, merge, "proj_mla", 1024,
                         rows=[(proj_a, 512, 2), (proj_a, 512, 4), (pa, 512)], outs=[(D, 512, BF16), (D, 512, BF16)])
    def post1(yv, xv, gate, gpost, gpre, sc, sh):
        x1 = xv + gate * (yv * _rstd(yv) * gpost)
        return (x1, (x1 * _rstd(x1) * gpre) * (1.0 + sc) + sh, yv), ()
    x1, h2, y = _mm_epi(merged, w["w_out"], "nn", D, post1, "out_proj", 512, rows=[(x, D)],
                        vecs=[gate_mix, g_post_mix, g_pre_ffn, scale_ffn, shift_ffn],
                        outs=[(D, D, F32), (D, D, BF16), (D, D, F32)])
    w = {**w, **wts("ffn", h2)}

    def swiglu(r):
        g, u = r[:, :FFN_T], r[:, FFN_T:]
        return (g * _sigmoid(g) * u, r), ()
    act, gu = _mm_epi(h2, w["w_ffn_in"], "nn", 2 * FFN_T, swiglu, "ffn_in", 512,
                      outs=[(D_FF, FFN_T, BF16), (2 * D_FF, 2 * FFN_T, BF16)])

    def head(y2v, x1v, tv, gate, gpost):
        r = _rstd(y2v)
        yn = y2v * r
        n2 = yn * gpost
        err = (x1v + gate * n2) - tv
        dx2 = err * (1.0 / D)
        dn2 = dx2 * gate
        dy2 = _norm_bwd(dn2 * gpost, yn, r)
        return (dx2, dy2), (_colsum(err * err), _colsum(dx2 * n2), _colsum(dn2 * yn))
    dx2, dy2, err_cols, d_gate_ffn, d_g_post_ffn = _mm_epi(
        act, w["w_ffn_out"], "nn", D, head, "ffn_out", 512, rows=[(x1, D), (target, D)], vecs=[gate_ffn, g_post_ffn],
        outs=[(D, D, F32), (D, D, BF16)], sums=[D, D, D])

    def swiglu_bwd(da, guv):
        g, u = guv[:, :FFN_T].astype(F32), guv[:, FFN_T:].astype(F32)
        sg = _sigmoid(g)
        return (jnp.concatenate([da * u * (sg * (1.0 + g * (1.0 - sg))), da * (g * sg)], axis=1),), ()
    (dgu,) = _mm_epi(dy2, w["w_ffn_out"], "nt", FFN_T, swiglu_bwd, "ffn_out_dx", 512, rows=[(gu, 2 * FFN_T)],
                     outs=[(2 * D_FF, 2 * FFN_T, BF16)])
    dw = {"w_ffn_out": _mm(act, dy2, "tn", BF16, "ffn_out_dw")}
    dw["w_ffn_in"] = _mm(dgu, h2, "tn", BF16, "ffn_in_dw")
    gate_mix = gate_mix + send({n: dw.pop(n) for n in ("w_ffn_in", "w_ffn_out")})[0, 0]

    def mid(dh, x1v, dx2v, yv, gpre, sc, gate, gpost):
        r2 = _rstd(x1v)
        x1n = x1v * r2
        t = dh * x1n
        dx1 = dx2v + _norm_bwd(dh * (gpre * (1.0 + sc)), x1n, r2)
        ry = _rstd(yv)
        yn = yv * ry
        dn1 = dx1 * gate
        dy = _norm_bwd(dn1 * gpost, yn, ry)
        sums = (_colsum(dh), _colsum(t) * gpre, _colsum(t) * (1.0 + sc), _colsum(dx1 * (yn * gpost)), _colsum(dn1 * yn))
        return (dx1, dy), sums
    dx1, dy, d_shift_ffn, d_scale_ffn, d_g_pre_ffn, d_gate_mix, d_g_post_mix = _mm_epi(
        dgu, w["w_ffn_in"], "nt", D, mid, "ffn_in_dx", 512, rows=[(x1, D), (dx2, D), (y, D)],
        vecs=[g_pre_ffn, scale_ffn, gate_mix, g_post_mix], outs=[(D, D, F32), (D, D, BF16)], sums=[D] * 5)

    dw["w_out"] = _mm(merged, dy, "tn", BF16, "out_proj_dw")

    def merge_bwd(dm, gf, gm, pa_, pb_):
        sf, sm = _sigmoid(gf), _sigmoid(gm)
        return (dm * sf, dm * sm, dm * pa_ * (sf * (1.0 - sf)), dm * pb_ * (sm * (1.0 - sm))), ()
    dpa, dpb, dgf, dgm = _mm_epi(dy, w["w_out"], "nt", 512, merge_bwd, "out_proj_dx", 1024,
                                 rows=[(proj_a, 512, 2), (proj_a, 512, 4), (pa, 512), (pb, 512)],
                                 outs=[(D, 512, BF16)] * 4)
    do_a = _mm(dpa, w["w_pf"], "nt", BF16, "proj_fox_dx")
    do_b = _mm(dpb, w["w_pm"], "nt", BF16, "proj_mla_dx")
    dw["w_pf"] = _mm(o_a, dpa, "tn", BF16, "proj_fox_dw")
    dw["w_pm"] = _mm(o_b, dpb, "tn", BF16, "proj_mla_dw")
    bias = bias + send({n: dw.pop(n) for n in ("w_out", "w_pf", "w_pm")})[0, 0]

    sc_a, sc_b = 1.0 / math.sqrt(HEAD_DIM), 1.0 / math.sqrt(64 + ROPE_DIM)
    dq_a, dk_a, dv_a, dbias = _attn_grad(qkv, 0, qkv, 4, qkv, 8, do_a, lse_a, LANES, sc_a, bias, BF16, "fox_attn_bwd")
    dq_m, dk_m, dv_m = _attn_grad(q_m, 0, k_m, 0, v_m, 0, do_b, lse_b, 2 * LANES, sc_b, None, F32, "mla_attn_bwd")

    def mla_rope_bwd(dq, dk, c_, a_, b_):
        lane = lax.broadcasted_iota(jnp.int32, (1, LANES), 1)
        dqs = [_rope_t(dq[:, hd * LANES:(hd + 1) * LANES], c_, a_, b_) for hd in range(HEADS)]
        dkpe = dk[:, 0:LANES]
        for hd in range(1, HEADS):
            dkpe = dkpe + dk[:, hd * LANES:(hd + 1) * LANES]
        dkpe = jnp.where((lane >= 64) & (lane < 96), dkpe, 0.0)
        dkr = jnp.where((lane >= 64) & (lane < 96), _rope_t(dkpe, c_, a_, b_), 0.0)
        return (jnp.concatenate(dqs, axis=1), dk, dkr), ()
    dqb, dkb, dkr = _rowwise(mla_rope_bwd, [(dq_m, D, 0), (dk_m, D, 0), (ct, LANES, 0), (sa, LANES, 0), (sb, LANES, 0)],
                             [], [(D, BF16), (D, BF16), (LANES, F32)], [], "mla_rope_bwd")
    def lora_q_bwd(dq, cq, gq):
        rq = _rstd(cq)
        cqh = cq * rq
        return (_norm_bwd(dq * gq, cqh, rq),), (_colsum(dq * cqh),)
    dcq, d_g_q = _mm_epi(dqb, w["w_uq"], "nt", Q_LORA, lora_q_bwd, "mla_uq_dx", 512, rows=[(proj_a, Q_LORA, 0)],
                         vecs=[g_q], outs=[(Q_LORA, Q_LORA, BF16)], sums=[Q_LORA])

    def lora_kv_bwd(dv_part, dk_part, ckv, gkv):
        dkv = dv_part + dk_part
        rk = _rstd(ckv)
        ckh = ckv * rk
        return (_norm_bwd(dkv * gkv, ckh, rk),), (_colsum(dkv * ckh),)
    dckv, d_g_kv = _mm_epi(dv_m, w["w_v"], "nt", KV_LORA, lora_kv_bwd, "mla_uv_dx", 1024,
                           rows=[(_mm(dkb, w["w_k"], "nt", F32, "mla_uk_dx"), KV_LORA), (proj_a, KV_LORA, 3)],
                           vecs=[g_kv], outs=[(KV_LORA, KV_LORA, BF16)], sums=[KV_LORA])

    dzt, d_bf = _fox_gates_bwd(dbias.reshape(HEADS, S), zt, bf)
    dmisc = (dkr + jnp.pad(jnp.transpose(dzt), ((0, 0), (0, LANES - HEADS)))).astype(BF16)
    dproj_a = jnp.concatenate([dcq, dckv, dgf, dgm, dmisc], axis=1)
    dqkv = jnp.concatenate([dq_a, dk_a, dv_a], axis=1)
    dw["w_a"] = _mm(h, dproj_a, "tn", BF16, "in_proj_a_dw")
    dw["w_b"] = _mm(h, dqkv, "tn", BF16, "in_proj_b_dw")
    tok = send(dw, True)
    dh_a = _mm(dproj_a, w["w_a"], "nt", F32, "in_proj_a_dx", dep=tok)
    tok = relay(dh_a)
    tok = send({"w_uq": _mm(cqn, dqb, "tn", BF16, "mla_uq_dw", dep=tok),
                "w_k": _mm(ckvn, dkb, "tn", BF16, "mla_uk_dw", dep=tok),
                "w_v": _mm(ckvn, dv_m, "tn", BF16, "mla_uv_dw", dep=tok)}, late=True)
    g_pre_mix = g_pre_mix + tok[0, 0]

    def first(dh_b, dh_a, xv, dx1v, gpre, sc):
        dhv = dh_b + dh_a
        r = _rstd(xv)
        xn = xv * r
        t = dhv * xn
        dx = dx1v + _norm_bwd(dhv * (gpre * (1.0 + sc)), xn, r)
        return (dx,), (_colsum(dhv), _colsum(t) * gpre, _colsum(t) * (1.0 + sc))
    grad_x, d_shift_mix, d_scale_mix, d_g_pre_mix = _mm_epi(
        dqkv, w["w_b"], "nt", D, first, "in_proj_b_dx", 512,
        rows=[(dh_a, D), (x, D), (dx1, D)],
        vecs=[g_pre_mix, scale_mix], outs=[(D, D, F32)], sums=[D] * 3)

    dmod = jnp.concatenate([d_shift_mix, d_scale_mix, d_gate_mix, d_shift_ffn, d_scale_ffn, d_gate_ffn], axis=1)
    small = dict(dmod=dmod, g_pre_mix=d_g_pre_mix, g_post_mix=d_g_post_mix, g_pre_ffn=d_g_pre_ffn,
                 g_post_ffn=d_g_post_ffn, g_q_lora=d_g_q, g_kv_lora=d_g_kv,
                 b_forget=jnp.pad(jnp.transpose(d_bf), ((0, 0), (0, LANES - HEADS))), err=err_cols)
    return grad_x, small


SMALL_ORDER = ("dmod", "g_pre_mix", "g_post_mix", "g_pre_ffn", "g_post_ffn", "g_q_lora", "g_kv_lora", "b_forget", "err")
SMALL_PARAM = {"dmod": "b_ada"}
MATRICES = ("w_in", "w_uq", "w_ukv", "w_proj_fox", "w_proj_mla", "w_out", "w_ffn_in", "w_ffn_out")
WEIGHTS = ("w_ada", "b_ada", "g_pre_mix", "g_post_mix", "g_pre_ffn", "g_post_ffn", "w_in", "b_forget", "g_q_lora",
           "w_uq", "g_kv_lora", "w_ukv", "w_proj_fox", "w_proj_mla", "w_out", "w_ffn_in", "w_ffn_out")


def kernel(x, c, positions, w_ada, b_ada, g_pre_mix, g_post_mix, g_pre_ffn, g_post_ffn, w_in, b_forget, g_q_lora, w_uq, g_kv_lora, w_ukv, w_proj_fox, w_proj_mla, w_out, w_ffn_in, w_ffn_out, loss_target, m_w_ada, m_b_ada, m_g_pre_mix, m_g_post_mix, m_g_pre_ffn, m_g_post_ffn, m_w_in, m_b_forget, m_g_q_lora, m_w_uq, m_g_kv_lora, m_w_ukv, m_w_proj_fox, m_w_proj_mla, m_w_out, m_w_ffn_in, m_w_ffn_out, v_w_ada, v_b_ada, v_g_pre_mix, v_g_post_mix, v_g_pre_ffn, v_g_post_ffn, v_w_in, v_b_forget, v_g_q_lora, v_w_uq, v_g_kv_lora, v_w_ukv, v_w_proj_fox, v_w_proj_mla, v_w_out, v_w_ffn_in, v_w_ffn_out):
    prm = dict(w_ada=w_ada, b_ada=b_ada, g_pre_mix=g_pre_mix, g_post_mix=g_post_mix, g_pre_ffn=g_pre_ffn,
               g_post_ffn=g_post_ffn, w_in=w_in, b_forget=b_forget, g_q_lora=g_q_lora, w_uq=w_uq, g_kv_lora=g_kv_lora,
               w_ukv=w_ukv, w_proj_fox=w_proj_fox, w_proj_mla=w_proj_mla, w_out=w_out, w_ffn_in=w_ffn_in, w_ffn_out=w_ffn_out)
    mom = dict(w_ada=m_w_ada, b_ada=m_b_ada, g_pre_mix=m_g_pre_mix, g_post_mix=m_g_post_mix, g_pre_ffn=m_g_pre_ffn,
               g_post_ffn=m_g_post_ffn, w_in=m_w_in, b_forget=m_b_forget, g_q_lora=m_g_q_lora, w_uq=m_w_uq,
               g_kv_lora=m_g_kv_lora, w_ukv=m_w_ukv, w_proj_fox=m_w_proj_fox, w_proj_mla=m_w_proj_mla, w_out=m_w_out,
               w_ffn_in=m_w_ffn_in, w_ffn_out=m_w_ffn_out)
    var = dict(w_ada=v_w_ada, b_ada=v_b_ada, g_pre_mix=v_g_pre_mix, g_post_mix=v_g_post_mix, g_pre_ffn=v_g_pre_ffn,
               g_post_ffn=v_g_post_ffn, w_in=v_w_in, b_forget=v_b_forget, g_q_lora=v_g_q_lora, w_uq=v_w_uq,
               g_kv_lora=v_g_kv_lora, w_ukv=v_w_ukv, w_proj_fox=v_w_proj_fox, w_proj_mla=v_w_proj_mla, w_out=v_w_out,
               w_ffn_in=v_w_ffn_in, w_ffn_out=v_w_ffn_out)
    me = _flat(*_coords())
    slot = jnp.reshape(me, (1,)).astype(jnp.int32)

    own = {n: prm[n][0].astype(BF16) for n in MATRICES}
    no_dep = jnp.zeros((8, LANES), F32)
    (st_c, st_in), tok = _async_start([[c], [own["w_in"]]], ["gather", "spread"], no_dep, "gather_in_start")
    (c_own,), (c_land,) = _async_wait(st_c, tok, "gather_c_wait")
    c_all = _with_own(c_land, c_own, me).reshape(N_DEV, D)
    ada_cols = w_ada.shape[2]
    b_cols = lax.dynamic_slice(b_ada, (0, me * ada_cols), (1, ada_cols))
    mod_cols, silu_c = _mod_part(c_all, w_ada[0], b_cols)
    (mod_all,) = _all_gather([mod_cols], "gather_mod")

    (w_in_own,), (w_in_land,) = _async_wait(st_in, mod_all, "gather_in_wait")
    (st_in,), tok = _async_start([[w_in_land]], "forward", no_dep, "gather_in_forward")
    _, (w_in_land,) = _async_wait(st_in, tok, "gather_in_forward_wait")
    w = _prepare_weights({"w_in": w_in_land}, {"w_in": w_in_own}, slot)
    later = dict(lora=("w_uq", "w_ukv"), proj=("w_proj_fox", "w_proj_mla", "w_out"), ffn=("w_ffn_in", "w_ffn_out"))
    states, tok = _async_start([[own[n] for n in names] for names in later.values()], ["gather", "spread", "spread"],
                               w["w_b"], "gather_rest_start")
    gather_state = dict(zip(later, states))
    own_thru = {}

    def wts(group, after):
        if group.startswith("relay_"):
            name = group[len("relay_"):]
            own_thru[name], lands = _async_wait(gather_state[name], after, "gather_" + name + "_wait")
            (gather_state[name],), t = _async_start([lands], "forward", no_dep, "gather_" + name + "_forward")
            return {"tok": t}
        srcs, lands = _async_wait(gather_state[group], after, "gather_" + group + "_landed")
        srcs = own_thru.get(group, srcs)
        return _prepare_weights(dict(zip(later[group], lands)), dict(zip(later[group], srcs)), slot)

    sent, late_sent, last = [], [], {}

    def send(grads, final=False, late=False):
        shards = _shard_grads(grads)
        names = list(shards)
        (state,), t = _async_start([[shards[n] for n in names]], "pair" if final else "exchange", no_dep,
                                   "exchange_" + names[0] + "_start")
        if final:
            last.update(names=names, state=state)
        else:
            (late_sent if late else sent).append((names, state))
        return t

    def relay(after):
        srcs, lands = _async_wait(last["state"], after, "exchange_pair_wait")
        core = jnp.reshape(lax.axis_index("c"), (1,)).astype(jnp.int32)
        sums = [_add_sibling(src, land, core) for src, land in zip(srcs, lands)]
        (last["state"],), t = _async_start([sums], "chips", no_dep, "exchange_chips_start")
        return t

    mod = lax.dynamic_index_in_dim(mod_all, me, axis=1, keepdims=False).reshape(1, 6 * D) + tok[0, 0]

    vec = dict(g_pre_mix=g_pre_mix, g_post_mix=g_post_mix, g_pre_ffn=g_pre_ffn, g_post_ffn=g_post_ffn,
               g_q_lora=g_q_lora, g_kv_lora=g_kv_lora, b_forget=b_forget)
    pos = positions.astype(F32).reshape(S, 1)
    grad_x, small = _fwd_bwd(x[0], pos, mod, loss_target[0], w, vec, wts, send, relay)

    bundle = jnp.concatenate([small[n] for n in SMALL_ORDER], axis=1)
    (small_state,), tok = _async_start([[bundle]], "gather", jnp.zeros((8, LANES), F32), "gather_small_start")

    out = {}
    swap = lambda a: jnp.swapaxes(a, -1, -2)

    def update(n, land, src, sl):
        if n != "w_ffn_in":
            out[n] = _adamw(prm[n][0], mom[n][0], var[n][0], land, "adamw_" + n, src, sl)
            return out[n][0]
        res = _adamw(swap(prm[n][0]), swap(mom[n][0]), swap(var[n][0]), land, "adamw_" + n, src, sl)
        out[n] = tuple(swap(t) for t in res)
        return res[0]

    after = tok
    for names, state in sent:
        srcs, lands = _async_wait(state, after, "exchange_" + names[0] + "_wait")
        for n, src, land in zip(names, srcs, lands):
            after = update(n, land, src, slot)
    srcs, lands = _async_wait(last["state"], after, "exchange_chips_wait")
    for n, src, land in zip(last["names"], srcs, lands):
        after = update(n, land, src, slot // 2)
    for names, state in late_sent:
        srcs, lands = _async_wait(state, after, "exchange_" + names[0] + "_wait")
        for n, src, land in zip(names, srcs, lands):
            after = update(n, land, src, slot)

    (own_bundle,), (bundle_all,) = _async_wait(small_state, after, "gather_small_wait")
    bundle_all = _with_own(bundle_all, own_bundle, me)
    dmod_all = bundle_all[:, 0, :6 * D]
    dm_cols = lax.dynamic_slice(dmod_all, (0, me * ada_cols), (N_DEV, ada_cols))
    out["w_ada"] = _adamw_w_ada(w_ada[0], m_w_ada[0], v_w_ada[0], jnp.transpose(silu_c), dm_cols)

    offsets, off = {}, 0
    for n in SMALL_ORDER:
        offsets[n] = off
        off += small[n].shape[1]
    names = [SMALL_PARAM.get(n, n) for n in SMALL_ORDER if n != "err"]
    results, err = _adamw_rows(bundle_all, [offsets[n] for n in SMALL_ORDER if n != "err"],
                               [prm[n] for n in names], [mom[n] for n in names], [var[n] for n in names],
                               offsets["err"], D)
    out.update(zip(names, results))
    loss = 0.5 * jnp.sum(err) / D

    res = [loss, grad_x[None]]
    for kind in range(4):
        for n in WEIGHTS:
            t = out[n][kind]
            res.append(t[None] if prm[n].ndim == 3 else t)
    return tuple(res)
```

```python
import functools
import math

import jax
import jax.numpy as jnp
from jax import lax
from jax.experimental import pallas as pl
from jax.experimental.pallas import tpu as pltpu

F32 = jnp.float32
BF16 = jnp.bfloat16

N_DEV = 8
S = 2048
D = 1024
D_FF = 2816
HEADS = 8
HEAD_DIM = 64
Q_LORA = 768
KV_LORA = 256
ROPE_DIM = 32
ROPE_THETA = 10000.0
NORM_EPS = 1e-6
LANES = 128
VMEM_LIMIT = 56 * 1024 * 1024

ADAM_LR = 0.001
ADAM_B1 = 0.9
ADAM_B2 = 0.999
ADAM_EPS = 1e-08
ADAM_WD = 0.01
ADAM_STEP = 10

ATT_T = 256
LOG2E = 1.4426950408889634
N_ATT = S // ATT_T

NN = (((1,), (0,)), ((), ()))
NT = (((1,), (1,)), ((), ()))
TN = (((0,), (0,)), ((), ()))
MESH = pl.DeviceIdType.MESH


def _params(sem=None):
    return pltpu.CompilerParams(dimension_semantics=sem, vmem_limit_bytes=VMEM_LIMIT)


def _pick(n, cap):
    best = None
    for t in range(LANES, cap + 1, LANES):
        if n % t == 0:
            best = t
    return best if best is not None else n


def _mm(a, b, mode, out_dtype, name, acc=None, dep=None):
    if mode == "nn":
        (m, k), (k2, n), dn = a.shape, b.shape, NN
    elif mode == "nt":
        (m, k), (n, k2), dn = a.shape, b.shape, NT
    else:
        (k, m), (k2, n), dn = a.shape, b.shape, TN
    assert k == k2, (a.shape, b.shape, mode)
    tn = _pick(n, 1024)
    tm = _pick(m, 1536)
    osz = jnp.dtype(out_dtype).itemsize

    def need(tm_):
        blk = tm_ * k * 2 + tn * k * 2 + tm_ * tn * osz + (tm_ * tn * 4 if acc is not None else 0)
        return 2 * blk + tm_ * tn * 4
    while need(tm) > 36 * 1024 * 1024 and tm % 256 == 0:
        tm //= 2

    def body(*refs):
        a_ref, b_ref, o_ref = refs[0], refs[1], refs[-1]
        r = lax.dot_general(a_ref[...], b_ref[...], dn, preferred_element_type=F32)
        if acc is not None:
            r = r + refs[2][...]
        o_ref[...] = r.astype(o_ref.dtype)

    if mode == "tn":
        a_spec = pl.BlockSpec((k, tm), lambda i, j: (0, i))
    else:
        a_spec = pl.BlockSpec((tm, k), lambda i, j: (i, 0))
    if mode == "nt":
        b_spec = pl.BlockSpec((tn, k), lambda i, j: (j, 0))
    else:
        b_spec = pl.BlockSpec((k, tn), lambda i, j: (0, j))
    o_spec = pl.BlockSpec((tm, tn), lambda i, j: (i, j))
    in_specs = [a_spec, b_spec] + ([o_spec] if acc is not None else [])
    in_specs += [pl.BlockSpec(memory_space=pl.ANY)] if dep is not None else []
    args = (a, b) + ((acc,) if acc is not None else ()) + ((dep,) if dep is not None else ())
    return pl.pallas_call(
        body, name=name, grid=(m // tm, n // tn),
        in_specs=in_specs, out_specs=o_spec,
        out_shape=jax.ShapeDtypeStruct((m, n), out_dtype),
        compiler_params=_params(("parallel", "parallel")),
    )(*args)


def _mm_epi(a, b, mode, tnb, epi, name, tm, rows=(), vecs=(), outs=(), sums=(), pro=None):
    m = a.shape[0]
    k, nb = (b.shape if mode == "nn" else b.shape[::-1])
    dn = NN if mode == "nn" else NT
    pro_fn, pro_vecs, a_off = pro if pro is not None else (None, (), 0)
    n_in = 2 + len(rows) + len(vecs)
    n_all = n_in + len(pro_vecs)
    sub = min(tm, 256)

    def body(*refs):
        if pro is not None:
            a_out, a_scr = refs[-2:]
            refs = refs[:-2]

            @pl.when(pl.program_id(1) == 0)
            def _():
                a_scr[...] = pro_fn(refs[0][...], *[x[...] for x in refs[n_in:n_all]]).astype(BF16)
                a_out[...] = a_scr[...]
            a_ref = a_scr
        else:
            a_ref = refs[0]
        o_refs = refs[n_all:n_all + len(outs)]
        s_refs = refs[n_all + len(outs):]
        if sums:
            @pl.when((pl.program_id(0) == 0) & (pl.program_id(1) == 0))
            def _():
                for s_ref in s_refs:
                    s_ref[...] = jnp.zeros(s_ref.shape, F32)
        for c in range(tm // sub):
            rs = slice(c * sub, (c + 1) * sub)
            r = lax.dot_general(a_ref[rs, :], refs[1][...], dn, preferred_element_type=F32)
            o_vals, s_vals = epi(r, *[x[rs, :] for x in refs[2:2 + len(rows)]], *[x[...] for x in refs[2 + len(rows):n_in]])
            assert len(o_vals) == len(o_refs) and len(s_vals) == len(s_refs)
            for o_ref, val in zip(o_refs, o_vals):
                o_ref[rs, :] = val.astype(o_ref.dtype)
            for s_ref, val in zip(s_refs, s_vals):
                s_ref[...] += val

    once = dict(pipeline_mode=pl.Buffered(1)) if nb == tnb else {}
    if mode == "nn":
        b_spec = pl.BlockSpec((k, tnb), lambda i, j: (0, j), **once)
    else:
        b_spec = pl.BlockSpec((tnb, k), lambda i, j: (j, 0), **once)
    in_specs = [pl.BlockSpec((tm, k), lambda i, j: (i, a_off)), b_spec]
    rows = [tuple(r) + (0,) * (3 - len(r)) for r in rows]
    in_specs += [pl.BlockSpec((tm, w), functools.partial(lambda i, j, off: (i, j + off), off=off)) for _, w, off in rows]
    in_specs += [pl.BlockSpec(v.shape, lambda i, j: (0, 0)) for v in list(vecs) + list(pro_vecs)]
    out_specs = [pl.BlockSpec((tm, w), lambda i, j: (i, j)) for _, w, _ in outs]
    out_specs += [pl.BlockSpec((1, w), lambda i, j: (0, 0)) for w in sums]
    out_shape = [jax.ShapeDtypeStruct((m, full), dt) for full, _, dt in outs]
    out_shape += [jax.ShapeDtypeStruct((1, w), F32) for w in sums]
    if pro is not None:
        out_specs.append(pl.BlockSpec((tm, k), lambda i, j: (i, 0)))
        out_shape.append(jax.ShapeDtypeStruct((m, k), BF16))
    return pl.pallas_call(
        body, name=name, grid=(m // tm, nb // tnb),
        in_specs=in_specs, out_specs=out_specs, out_shape=out_shape,
        scratch_shapes=[pltpu.VMEM((tm, k), BF16)] if pro is not None else [],
        compiler_params=_params(("arbitrary", "arbitrary") if sums else ("parallel", "arbitrary" if pro is not None else "parallel")),
    )(a, b, *[r[0] for r in rows], *vecs, *pro_vecs)


def _rowwise(fn, row_ins, vec_ins, row_outs, sum_outs, name, tm=512):
    n_in = len(row_ins) + len(vec_ins)
    n_o = len(row_outs)
    rows = row_ins[0][0].shape[0]

    def body(*refs):
        vals = [r[...] for r in refs[:n_in]]
        outs = refs[n_in:]
        ro, so = fn(*vals)
        assert len(ro) == n_o and len(so) == len(sum_outs)
        for r, v in zip(outs[:n_o], ro):
            r[...] = v.astype(r.dtype)
        if sum_outs:
            @pl.when(pl.program_id(0) == 0)
            def _():
                for r in outs[n_o:]:
                    r[...] = jnp.zeros(r.shape, F32)
            for r, v in zip(outs[n_o:], so):
                r[...] += v

    in_specs = [pl.BlockSpec((tm, w), functools.partial(lambda i, b: (i, b), b=b)) for _, w, b in row_ins]
    in_specs += [pl.BlockSpec(v.shape, lambda i: (0, 0)) for v in vec_ins]
    out_specs = [pl.BlockSpec((tm, w), lambda i: (i, 0)) for w, _ in row_outs]
    out_specs += [pl.BlockSpec((1, w), lambda i: (0, 0)) for w in sum_outs]
    out_shape = [jax.ShapeDtypeStruct((rows, w), dt) for w, dt in row_outs]
    out_shape += [jax.ShapeDtypeStruct((1, w), F32) for w in sum_outs]
    return pl.pallas_call(
        body, name=name, grid=(rows // tm,),
        in_specs=in_specs, out_specs=out_specs, out_shape=out_shape,
        compiler_params=_params(("arbitrary",)),
    )(*[a for a, _, _ in row_ins], *vec_ins)


def _sigmoid(x):
    return 1.0 / (1.0 + jnp.exp(-x))


def _rstd(x):
    return lax.rsqrt(jnp.mean(x * x, axis=-1, keepdims=True) + NORM_EPS)


def _norm_bwd(dyn, xn, r):
    return r * (dyn - xn * jnp.mean(dyn * xn, axis=-1, keepdims=True))


def _colsum(x):
    return jnp.sum(x, axis=0, keepdims=True)


def _rope_tables(pos, invf):
    def fn(p, f):
        lane = lax.broadcasted_iota(jnp.int32, (1, LANES), 1)
        ang = p * f
        cs, sn = jnp.cos(ang), jnp.sin(ang)
        rot = (lane >= 64) & (lane < 96)
        ct = jnp.where(lane < 64, 1.0, jnp.where(rot, cs, 0.0))
        sa = jnp.where((lane >= 64) & (lane < 80), -sn, 0.0)
        sb = jnp.where((lane >= 80) & (lane < 96), sn, 0.0)
        return (ct, sa, sb), ()
    return _rowwise(fn, [(pos, 1, 0)], [invf], [(LANES, F32)] * 3, [], "rope_tables")


def _rope(x, ct, sa, sb):
    return x * ct + pltpu.roll(x, LANES - 16, 1) * sa + pltpu.roll(x, 16, 1) * sb


def _rope_t(x, ct, sa, sb):
    return x * ct - pltpu.roll(x, LANES - 16, 1) * sa - pltpu.roll(x, 16, 1) * sb


def _head_mask(width, hh):
    lane = lax.broadcasted_iota(jnp.int32, (1, width), 1)
    half = width // 2
    return (lane >= hh * half) & (lane < (hh + 1) * half)


ATT_PP = 2
ATT_CHAINS = [(a, hh) for a in range(ATT_PP) for hh in range(2)]
ATT_G = HEADS // (2 * ATT_PP)


def _pair(ref_or_val, a, width, rows=slice(None)):
    return ref_or_val[rows, a * width:(a + 1) * width]


def _attn_fwd(q, qo, k, ko, v, vo, dkp, scale, bias, name):
    T = ATT_T
    assert qo % ATT_PP == 0 and ko % ATT_PP == 0 and vo % ATT_PP == 0
    qo, ko, vo = qo // ATT_PP, ko // ATT_PP, vo // ATT_PP

    def body(*refs):
        if bias is not None:
            q_ref, k_ref, v_ref, b_ref, o_ref, lse_ref, s_scr = refs
        else:
            q_ref, k_ref, v_ref, o_ref, lse_ref, s_scr = refs
        i = pl.program_id(1)
        row = lax.broadcasted_iota(jnp.int32, (T, T), 0)
        col = lax.broadcasted_iota(jnp.int32, (T, T), 1)
        qms = []
        for a, hh in ATT_CHAINS:
            qb = _pair(q_ref, a, dkp)
            qms.append(jnp.where(_head_mask(dkp, hh), qb, jnp.zeros_like(qb)))

        def fold(t):
            return [t[:, c * LANES:(c + 1) * LANES] for c in range(T // LANES)]

        def run(nt):
            mls = [jnp.full((T, LANES), -jnp.inf, F32) for _ in ATT_CHAINS]
            for j in range(nt):
                ks = slice(j * T, (j + 1) * T)
                for ci, (a, hh) in enumerate(ATT_CHAINS):
                    s = lax.dot_general(qms[ci], _pair(k_ref, a, dkp, ks), NT, preferred_element_type=F32) * (scale * LOG2E)
                    if bias is not None:
                        s = s + b_ref[2 * a + hh, j] * LOG2E
                    if j == nt - 1:
                        s = jnp.where(row >= col, s, -jnp.inf)
                    s_scr[ci, j] = s
                    for part in fold(s):
                        mls[ci] = jnp.maximum(mls[ci], part)
            ms = [jnp.max(ml, axis=1, keepdims=True) for ml in mls]
            mbs = [jnp.broadcast_to(m, (T, LANES)) for m in ms]
            for a in range(ATT_PP):
                ls = [jnp.zeros((T, LANES), F32) for _ in range(2)]
                ps, vms = [], []
                for j in range(nt):
                    vb = _pair(v_ref, a, LANES, slice(j * T, (j + 1) * T))
                    for hh in range(2):
                        parts = [jnp.exp2(part - mbs[2 * a + hh]) for part in fold(s_scr[2 * a + hh, j])]
                        for part in parts:
                            ls[hh] = ls[hh] + part
                        ps.append(jnp.concatenate(parts, axis=1).astype(BF16))
                        vms.append(jnp.where(_head_mask(LANES, hh), vb, jnp.zeros_like(vb)))
                acc = lax.dot_general(jnp.concatenate(ps, axis=1), jnp.concatenate(vms, axis=0), NN,
                                      preferred_element_type=F32)
                l0, l1 = [jnp.sum(l, axis=1, keepdims=True) for l in ls]
                lse_ref[2 * a] = ms[2 * a] + jnp.log2(l0)
                lse_ref[2 * a + 1] = ms[2 * a + 1] + jnp.log2(l1)
                inv = jnp.where(_head_mask(LANES, 0), 1.0 / l0, 1.0 / l1)
                o_ref[:, a * LANES:(a + 1) * LANES] = (acc * inv).astype(o_ref.dtype)

        for nt in range(1, N_ATT + 1):
            pl.when(i == nt - 1)(functools.partial(run, nt))

    in_specs = [
        pl.BlockSpec((T, ATT_PP * dkp), lambda g, i: (i, qo + g)),
        pl.BlockSpec((S, ATT_PP * dkp), lambda g, i: (0, ko + g)),
        pl.BlockSpec((S, ATT_PP * LANES), lambda g, i: (0, vo + g)),
    ]
    args = [q, k, v]
    if bias is not None:
        in_specs.append(pl.BlockSpec((2 * ATT_PP, N_ATT, 1, T), lambda g, i: (g, 0, 0, 0)))
        args.append(bias)
    return pl.pallas_call(
        body, name=name, grid=(ATT_G, N_ATT),
        in_specs=in_specs,
        out_specs=[pl.BlockSpec((T, ATT_PP * LANES), lambda g, i: (i, g)),
                   pl.BlockSpec((2 * ATT_PP, T, 1), lambda g, i: (g, i, 0))],
        out_shape=[jax.ShapeDtypeStruct((S, HEADS * HEAD_DIM), BF16),
                   jax.ShapeDtypeStruct((HEADS, S, 1), F32)],
        scratch_shapes=[pltpu.VMEM((len(ATT_CHAINS), N_ATT, T, T), F32)],
        compiler_params=_params(("parallel", "arbitrary")),
    )(*args)


def _attn_grad(q, qo, k, ko, v, vo, do, lse, dkp, scale, bias, qk_dtype, name):
    T = ATT_T
    has_b = bias is not None
    qo, ko, vo = qo // ATT_PP, ko // ATT_PP, vo // ATT_PP
    n_ch = len(ATT_CHAINS)

    def body(*refs):
        q_ref, k_ref, v_ref, do_ref, lse_ref = refs[:5]
        refs = refs[5:]
        if has_b:
            b_ref, refs = refs[0], refs[1:]
        dq_ref, dk_ref, dv_ref = refs[:3]
        refs = refs[3:]
        if has_b:
            db_ref, refs = refs[0], refs[1:]
        p_scr, dp_scr, dk_acc, dv_acc = refs[:4]
        db_acc = refs[4] if has_b else None
        i = pl.program_id(1)

        @pl.when(i == 0)
        def _():
            dk_acc[...] = jnp.zeros(dk_acc.shape, F32)
            dv_acc[...] = jnp.zeros(dv_acc.shape, F32)
            if has_b:
                db_acc[...] = jnp.zeros(db_acc.shape, F32)

        row = lax.broadcasted_iota(jnp.int32, (T, T), 0)
        col = lax.broadcasted_iota(jnp.int32, (T, T), 1)

        def fold(t):
            return [t[:, c * LANES:(c + 1) * LANES] for c in range(T // LANES)]

        qms, doms, lses = [], [], []
        for a, hh in ATT_CHAINS:
            qb, dob = _pair(q_ref, a, dkp), _pair(do_ref, a, LANES)
            qms.append(jnp.where(_head_mask(dkp, hh), qb, jnp.zeros_like(qb)))
            doms.append(jnp.where(_head_mask(LANES, hh), dob, jnp.zeros_like(dob)))
            lses.append(lse_ref[2 * a + hh])

        def run(nt):
            dls = [jnp.zeros((T, LANES), F32) for _ in ATT_CHAINS]
            for j in range(nt):
                ks = slice(j * T, (j + 1) * T)
                for ci, (a, hh) in enumerate(ATT_CHAINS):
                    s = lax.dot_general(qms[ci], _pair(k_ref, a, dkp, ks), NT, preferred_element_type=F32) * (scale * LOG2E)
                    if has_b:
                        s = s + b_ref[ci, j] * LOG2E
                    s = s - lses[ci]
                    if j == nt - 1:
                        s = jnp.where(row >= col, s, -jnp.inf)
                    p = jnp.exp2(s)
                    dp = lax.dot_general(doms[ci], _pair(v_ref, a, LANES, ks), NT, preferred_element_type=F32)
                    p_scr[ci, j] = p
                    dp_scr[ci, j] = dp
                    for part in fold(p * dp):
                        dls[ci] = dls[ci] + part
            deltas = [jnp.broadcast_to(jnp.sum(dl, axis=1, keepdims=True), (T, LANES)) for dl in dls]
            for a in range(ATT_PP):
                ds_all, km_all = [], []
                qm2t = jnp.transpose(jnp.concatenate([qms[2 * a], qms[2 * a + 1]], axis=0))
                dom2t = jnp.transpose(jnp.concatenate([doms[2 * a], doms[2 * a + 1]], axis=0))
                for j in range(nt):
                    ks = slice(j * T, (j + 1) * T)
                    kb = _pair(k_ref, a, dkp, ks)
                    p2, ds2 = [], []
                    for hh in range(2):
                        ci = 2 * a + hh
                        p = p_scr[ci, j]
                        ds = jnp.concatenate([pp * (dd - deltas[ci]) for pp, dd in zip(fold(p), fold(dp_scr[ci, j]))], axis=1)
                        if has_b:
                            db_acc[ci, j] += jnp.sum(ds, axis=0, keepdims=True)
                        p2.append(p.astype(BF16))
                        ds2.append((ds * scale).astype(BF16))
                        km_all.append(jnp.where(_head_mask(dkp, hh), kb, jnp.zeros_like(kb)))
                    dv_acc[a * LANES:(a + 1) * LANES, ks] += lax.dot_general(
                        dom2t, jnp.concatenate(p2, axis=0), NN, preferred_element_type=F32)
                    dk_acc[a * dkp:(a + 1) * dkp, ks] += lax.dot_general(
                        qm2t, jnp.concatenate(ds2, axis=0), NN, preferred_element_type=F32)
                    ds_all += ds2
                dq = lax.dot_general(jnp.concatenate(ds_all, axis=1), jnp.concatenate(km_all, axis=0), NN,
                                     preferred_element_type=F32)
                dq_ref[:, a * dkp:(a + 1) * dkp] = dq.astype(dq_ref.dtype)

        for nt in range(1, N_ATT + 1):
            pl.when(i == nt - 1)(functools.partial(run, nt))

        @pl.when(i == N_ATT - 1)
        def _():
            dk_ref[...] = jnp.transpose(dk_acc[...]).astype(dk_ref.dtype)
            dv_ref[...] = jnp.transpose(dv_acc[...]).astype(dv_ref.dtype)
            if has_b:
                db_ref[...] = db_acc[...]

    in_specs = [
        pl.BlockSpec((T, ATT_PP * dkp), lambda g, i: (i, qo + g)),
        pl.BlockSpec((S, ATT_PP * dkp), lambda g, i: (0, ko + g)),
        pl.BlockSpec((S, ATT_PP * LANES), lambda g, i: (0, vo + g)),
        pl.BlockSpec((T, ATT_PP * LANES), lambda g, i: (i, g)),
        pl.BlockSpec((2 * ATT_PP, T, 1), lambda g, i: (g, i, 0)),
    ]
    args = [q, k, v, do, lse]
    out_specs = [
        pl.BlockSpec((T, ATT_PP * dkp), lambda g, i: (i, g)),
        pl.BlockSpec((S, ATT_PP * dkp), lambda g, i: (0, g)),
        pl.BlockSpec((S, ATT_PP * LANES), lambda g, i: (0, g)),
    ]
    width = (HEADS // 2) * dkp
    out_shape = [
        jax.ShapeDtypeStruct((S, width), qk_dtype),
        jax.ShapeDtypeStruct((S, width), qk_dtype),
        jax.ShapeDtypeStruct((S, HEADS * HEAD_DIM), BF16),
    ]
    scratch = [pltpu.VMEM((n_ch, N_ATT, T, T), F32), pltpu.VMEM((n_ch, N_ATT, T, T), F32),
               pltpu.VMEM((ATT_PP * dkp, S), F32), pltpu.VMEM((ATT_PP * LANES, S), F32)]
    if has_b:
        bspec = pl.BlockSpec((2 * ATT_PP, N_ATT, 1, T), lambda g, i: (g, 0, 0, 0))
        in_specs.append(bspec)
        args.append(bias)
        out_specs.append(bspec)
        out_shape.append(jax.ShapeDtypeStruct((HEADS, N_ATT, 1, T), F32))
        scratch.append(pltpu.VMEM((2 * ATT_PP, N_ATT, 1, T), F32))
    return pl.pallas_call(
        body, name=name, grid=(ATT_G, N_ATT),
        in_specs=in_specs, out_specs=out_specs, out_shape=out_shape, scratch_shapes=scratch,
        compiler_params=_params(("parallel", "arbitrary")),
    )(*args)


def _tri(upper):
    a = lax.broadcasted_iota(jnp.int32, (LANES, LANES), 0)
    b = lax.broadcasted_iota(jnp.int32, (LANES, LANES), 1)
    return jnp.where(a <= b if upper else a >= b, 1.0, 0.0).astype(F32)


def _fox_gates(proj, blk, bf):
    def body(m_ref, b_ref, z_out, o_ref):
        tri = _tri(True)
        carry = jnp.zeros((HEADS, 1), F32)
        for t in range(S // LANES):
            sl = slice(t * LANES, (t + 1) * LANES)
            zt = jnp.transpose(m_ref[sl, :])[:HEADS]
            z_out[:, sl] = zt
            z = zt + b_ref[...]
            logf = jnp.minimum(z, 0.0) - jnp.log(1.0 + jnp.exp(-jnp.abs(z)))
            c = lax.dot_general(logf, tri, NN, preferred_element_type=F32,
                                precision=lax.Precision.HIGHEST) + carry
            o_ref[:, sl] = -c
            carry = c[:, LANES - 1:LANES]

    return pl.pallas_call(
        body, name="fox_gates", grid=(1,),
        in_specs=[pl.BlockSpec((S, LANES), lambda i: (0, blk)), pl.BlockSpec(bf.shape, lambda i: (0, 0))],
        out_specs=[pl.BlockSpec((HEADS, S), lambda i: (0, 0))] * 2,
        out_shape=[jax.ShapeDtypeStruct((HEADS, S), F32)] * 2,
        compiler_params=_params(("arbitrary",)),
    )(proj, bf)


def _fox_gates_bwd(dbias, zt, bf):
    def body(d_ref, z_ref, b_ref, dz_ref, dbf_ref):
        tri = _tri(False)
        carry = jnp.zeros((HEADS, 1), F32)
        tot = jnp.zeros((HEADS, 1), F32)
        for t in reversed(range(S // LANES)):
            sl = slice(t * LANES, (t + 1) * LANES)
            df = -d_ref[:, sl]
            c = lax.dot_general(df, tri, NN, preferred_element_type=F32,
                                precision=lax.Precision.HIGHEST) + carry
            carry = c[:, 0:1]
            z = z_ref[:, sl] + b_ref[...]
            dz = c * _sigmoid(-z)
            dz_ref[:, sl] = dz
            tot = tot + jnp.sum(dz, axis=1, keepdims=True)
        dbf_ref[...] = tot

    return pl.pallas_call(
        body, name="fox_gates_bwd",
        out_shape=[jax.ShapeDtypeStruct((HEADS, S), F32), jax.ShapeDtypeStruct((HEADS, 1), F32)],
        compiler_params=_params(),
    )(dbias, zt, bf)


def _mod_part(c_all, w_ada, b_cols):
    def body(c_ref, w_ref, b_ref, o_ref, s_ref):
        c = c_ref[...]
        sc = c * _sigmoid(c)
        s_ref[...] = sc
        o_ref[...] = lax.dot_general(sc, w_ref[...], NN, preferred_element_type=F32,
                                     precision=lax.Precision.HIGHEST) + b_ref[...]

    return pl.pallas_call(
        body, name="mod_part",
        out_shape=[jax.ShapeDtypeStruct((N_DEV, w_ada.shape[1]), F32), jax.ShapeDtypeStruct(c_all.shape, F32)],
        compiler_params=_params(),
    )(c_all, w_ada, b_cols)


def _adamw_w_ada(w, m, v, sc_t, dm):
    rows, cols = w.shape
    tr = 256

    def body(w_ref, m_ref, v_ref, s_ref, d_ref, g_out, d_out, m_out, v_out):
        g = s_ref[:, 0:1] * d_ref[0:1, :]
        for b in range(1, N_DEV):
            g = g + s_ref[:, b:b + 1] * d_ref[b:b + 1, :]
        g_out[...] = g
        d_out[...], m_out[...], v_out[...] = _adamw_math(w_ref[...], g, m_ref[...], v_ref[...])

    spec = pl.BlockSpec((tr, cols), lambda i: (i, 0))
    return pl.pallas_call(
        body, name="adamw_w_ada", grid=(rows // tr,),
        in_specs=[spec, spec, spec, pl.BlockSpec((tr, N_DEV), lambda i: (i, 0)), pl.BlockSpec(dm.shape, lambda i: (0, 0))],
        out_specs=[spec] * 4, out_shape=[jax.ShapeDtypeStruct((rows, cols), F32)] * 4,
        compiler_params=_params(("parallel",)),
    )(w, m, v, sc_t, dm)


def _adamw(w, m, v, parts, name, own=None, slot=None):
    rows, cols = w.shape
    n = parts.shape[0]
    by_cols = rows % 256 != 0 and cols % 256 == 0
    tr, tc = (rows, 256) if by_cols else ((rows if rows <= 512 else 256), cols)
    tile = (lambda i: (0, i)) if by_cols else (lambda i: (i, 0))

    def body(*refs):
        if own is not None:
            s_ref, refs = refs[0], refs[1:]
            w_ref, m_ref, v_ref, p_ref, o_ref, g_out, d_out, m_out, v_out = refs
            terms = [jnp.where(s_ref[0] == kk, o_ref[0], p_ref[kk]) for kk in range(n)]
        else:
            w_ref, m_ref, v_ref, p_ref, g_out, d_out, m_out, v_out = refs
            terms = [p_ref[kk] for kk in range(n)]
        g = terms[0].astype(F32)
        for term in terms[1:]:
            g = g + term.astype(F32)
        g_out[...] = g
        d_out[...], m_out[...], v_out[...] = _adamw_math(w_ref[...], g, m_ref[...], v_ref[...])

    spec = pl.BlockSpec((tr, tc), lambda i, *_: tile(i))
    in_specs = [spec, spec, spec, pl.BlockSpec((n, tr, tc), lambda i, *_: (0,) + tile(i))]
    out_shape = [jax.ShapeDtypeStruct((rows, cols), F32)] * 4
    grid = (rows // tr if not by_cols else cols // tc,)
    if own is None:
        return pl.pallas_call(
            body, name=name, grid=grid, in_specs=in_specs, out_specs=[spec] * 4, out_shape=out_shape,
            compiler_params=_params(("parallel",)),
        )(w, m, v, parts)
    in_specs.append(pl.BlockSpec((1, tr, tc), lambda i, s: (s[0],) + tile(i)))
    return pl.pallas_call(
        body, name=name, out_shape=out_shape, compiler_params=_params(("parallel",)),
        grid_spec=pltpu.PrefetchScalarGridSpec(num_scalar_prefetch=1, grid=grid, in_specs=in_specs,
                                               out_specs=[spec] * 4),
    )(slot, w, m, v, parts, own)


def _adamw_math(w, g, m, v):
    mm = ADAM_B1 * m + (1.0 - ADAM_B1) * g
    vv = ADAM_B2 * v + (1.0 - ADAM_B2) * (g * g)
    m_hat = mm / (1.0 - ADAM_B1 ** ADAM_STEP)
    v_hat = vv / (1.0 - ADAM_B2 ** ADAM_STEP)
    return -ADAM_LR * (m_hat / (jnp.sqrt(v_hat) + ADAM_EPS) + ADAM_WD * w), mm, vv


def _adamw_rows(bundles, offsets, ws, ms, vs, err_off, err_width):
    k = len(ws)

    def body(*refs):
        b_ref = refs[0]
        w_refs, m_refs, v_refs = refs[1:1 + k], refs[1 + k:1 + 2 * k], refs[1 + 2 * k:1 + 3 * k]
        outs = refs[1 + 3 * k:]
        g_all = b_ref[0]
        for kk in range(1, N_DEV):
            g_all = g_all + b_ref[kk]
        for i in range(k):
            width = w_refs[i].shape[1]
            g = g_all[:, offsets[i]:offsets[i] + width]
            outs[4 * i][...] = g
            outs[4 * i + 1][...], outs[4 * i + 2][...], outs[4 * i + 3][...] = _adamw_math(
                w_refs[i][...], g, m_refs[i][...], v_refs[i][...])
        outs[4 * k][...] = g_all[:, err_off:err_off + err_width]

    out_shape = []
    for w_ in ws:
        out_shape += [jax.ShapeDtypeStruct(w_.shape, F32)] * 4
    out_shape.append(jax.ShapeDtypeStruct((1, err_width), F32))
    res = pl.pallas_call(body, name="adamw_rows", out_shape=out_shape, compiler_params=_params())(bundles, *ws, *ms, *vs)
    return [tuple(res[4 * i:4 * i + 4]) for i in range(k)], res[-1]


def _coords():
    return lax.axis_index("x"), lax.axis_index("y"), lax.axis_index("c")


def _flat(px, py, pc):
    return 4 * px + 2 * py + pc


def _all_gather(arrs, name):
    n = len(arrs)

    def body(*refs):
        ins, outs = refs[:n], refs[n:2 * n]
        send, recv, lsem = refs[2 * n:]
        x, y, c = _coords()
        me, sibling = (x, y, c), (x, y, 1 - c)
        chips = [(1 - x, y), (x, 1 - y), (1 - x, 1 - y)]

        def copy(a, kk, block, to, src=None):
            slot = outs[a].at[_flat(*block)]
            return pltpu.make_async_remote_copy(
                src_ref=slot if src is None else src, dst_ref=slot,
                send_sem=send.at[a, kk], recv_sem=recv.at[a, kk],
                device_id=to, device_id_type=MESH)

        mine = [pltpu.make_async_copy(ins[a], outs[a].at[_flat(*me)], lsem.at[a]) for a in range(n)]
        for cp in mine:
            cp.start()
        first = []
        for a in range(n):
            first.append(copy(a, 0, me, sibling, src=ins[a]))
            first += [copy(a, 1 + j, me, (*chip, c), src=ins[a]) for j, chip in enumerate(chips)]
        for cp in first:
            cp.start()
        passed = []
        for j, chip in enumerate(chips):
            for a in range(n):
                copy(a, 1 + j, (*chip, c), me).wait_recv()
                cp = copy(a, 4 + j, (*chip, c), sibling)
                cp.start()
                passed.append(cp)
        for a in range(n):
            copy(a, 0, sibling, me).wait_recv()
        for j, chip in enumerate(chips):
            for a in range(n):
                copy(a, 4 + j, (*chip, 1 - c), me).wait_recv()
        for cp in first + passed:
            cp.wait_send()
        for cp in mine:
            cp.wait()

    any_spec = pl.BlockSpec(memory_space=pl.ANY)
    return pl.pallas_call(
        body, name=name,
        in_specs=[any_spec] * n, out_specs=[any_spec] * n,
        out_shape=[jax.ShapeDtypeStruct((N_DEV,) + a.shape, a.dtype) for a in arrs],
        scratch_shapes=[pltpu.SemaphoreType.DMA((n, 7)), pltpu.SemaphoreType.DMA((n, 7)),
                        pltpu.SemaphoreType.DMA((n,))],
    )(*arrs)


def _peer_list():
    x, y, c = _coords()
    return [((1 - x if r & 4 else x), (1 - y if r & 2 else y), (1 - c if r & 1 else c)) for r in range(1, N_DEV)]


def _copy_plan(mode, src, land):
    x, y, c = _coords()
    me = _flat(x, y, c)
    if mode == "gather":
        return [(src, land.at[me], peer) for peer in _peer_list()]
    if mode == "exchange":
        return [(src.at[_flat(*peer)], land.at[me], peer) for peer in _peer_list()]
    if mode == "pair":
        return [(src.at[_flat(q // 2, q % 2, 1 - c)], land.at[q], (x, y, 1 - c)) for q in range(N_DEV // 2)]
    chips = [((1 - x if r & 2 else x), (1 - y if r & 1 else y)) for r in range(1, N_DEV // 2)]
    if mode == "chips":
        return [(src.at[2 * qx + qy], land.at[2 * x + y], (qx, qy, c)) for qx, qy in chips]
    if mode == "spread":
        return [(src, land.at[me], (x, y, 1 - c))] + [(src, land.at[me], (qx, qy, c)) for qx, qy in chips]
    assert mode == "forward"
    return [(land.at[_flat(qx, qy, c)], land.at[_flat(qx, qy, c)], (x, y, 1 - c)) for qx, qy in chips]


N_COPIES = dict(gather=N_DEV - 1, exchange=N_DEV - 1, pair=N_DEV // 2, chips=N_DEV // 2 - 1, spread=N_DEV // 2,
                forward=N_DEV // 2 - 1)


def _land_shape(mode, shape):
    return {"gather": (N_DEV,) + shape, "spread": (N_DEV,) + shape, "exchange": shape,
            "pair": (N_DEV // 2,) + shape[1:], "chips": shape}[mode]


HBM_SPEC = pl.BlockSpec(memory_space=pltpu.HBM)
SEM_SPEC = pl.BlockSpec(memory_space=pltpu.SEMAPHORE)
ANY_SPEC = pl.BlockSpec(memory_space=pl.ANY)
SIDE_EFFECT = pltpu.SideEffectType.DATAFLOW_SIDE_EFFECTING


def _async_start(groups, modes, after, name):
    modes = [modes] * len(groups) if isinstance(modes, str) else list(modes)
    arrs = [(a, m) for g, m in zip(groups, modes) for a in g]
    n = len(arrs)
    fresh = [i for i, (_, m) in enumerate(arrs) if m != "forward"]

    def body(*refs):
        srcs, new_lands = refs[:n], refs[n:n + len(fresh)]
        outs = refs[n + len(fresh) + 1:]
        lands = list(srcs)
        for k, i in enumerate(fresh):
            lands[i] = new_lands[k]
        for ai, (_, mode) in enumerate(arrs):
            for src_ref, dst_ref, peer in _copy_plan(mode, srcs[ai], lands[ai]):
                pltpu.make_async_remote_copy(src_ref=src_ref, dst_ref=dst_ref, send_sem=outs[2 * ai],
                                             recv_sem=outs[2 * ai + 1], device_id=peer, device_id_type=MESH).start()
        outs[-1][...] = jnp.zeros(outs[-1].shape, F32)

    land_shapes = [(_land_shape(arrs[i][1], arrs[i][0].shape), arrs[i][0].dtype) for i in fresh]
    n_buf = n + len(fresh)
    out_shape = [pltpu.SemaphoreType.DMA(())] * (2 * n)
    out_shape += [pltpu.HBM(a.shape, a.dtype) for a, _ in arrs]
    out_shape += [pltpu.HBM(shape, dt) for shape, dt in land_shapes]
    out_shape.append(jax.ShapeDtypeStruct((8, LANES), F32))
    res = pl.pallas_call(
        body, name=name, out_shape=tuple(out_shape),
        in_specs=[HBM_SPEC] * n_buf + [ANY_SPEC],
        out_specs=tuple([SEM_SPEC] * (2 * n) + [HBM_SPEC] * n_buf + [pl.BlockSpec(memory_space=pltpu.VMEM)]),
        input_output_aliases={i: 2 * n + i for i in range(n_buf)},
        compiler_params=pltpu.CompilerParams(has_side_effects=SIDE_EFFECT),
    )(*[pltpu.with_memory_space_constraint(a, pltpu.HBM) for a, _ in arrs],
      *[pltpu.with_memory_space_constraint(lax.empty(shape, dt), pltpu.HBM) for shape, dt in land_shapes],
      after)
    sems, thru = res[:2 * n], res[2 * n:-1]
    land_of = {i: thru[n + k] for k, i in enumerate(fresh)}
    states, idx = [], 0
    for g, mode in zip(groups, modes):
        ids = range(idx, idx + len(g))
        idx += len(g)
        states.append(([sems[2 * i] for i in ids], [sems[2 * i + 1] for i in ids],
                       None if mode == "forward" else [thru[i] for i in ids],
                       [land_of.get(i, thru[i]) for i in ids], mode))
    return states, res[-1]


def _async_wait(state, after, name):
    sends, recvs, srcs, lands, mode = state
    g = len(lands)
    bufs = (list(srcs) if srcs is not None else []) + list(lands)
    nb = len(bufs)

    def body(*refs):
        l_refs, sems = refs[nb - g:nb], refs[nb:nb + 2 * g]
        for ai in range(g):
            moved = l_refs[ai].at[pl.ds(0, N_COPIES[mode])]
            cp = pltpu.make_async_remote_copy(src_ref=moved, dst_ref=moved, send_sem=sems[ai], recv_sem=sems[g + ai],
                                              device_id=_coords(), device_id_type=MESH)
            cp.wait_send()
            cp.wait_recv()

    res = pl.pallas_call(
        body, name=name,
        out_shape=tuple(pltpu.HBM(a.shape, a.dtype) for a in bufs),
        in_specs=[HBM_SPEC] * nb + [SEM_SPEC] * (2 * g) + [ANY_SPEC],
        out_specs=tuple([HBM_SPEC] * nb),
        input_output_aliases={i: i for i in range(nb)},
        compiler_params=pltpu.CompilerParams(has_side_effects=SIDE_EFFECT),
    )(*bufs, *sends, *recvs, after)
    return (list(res[:nb - g]) if srcs is not None else None), list(res[nb - g:])


def _add_sibling(mine, theirs, core):
    def body(c_ref, a_ref, b_ref, o_ref):
        o_ref[...] = (a_ref[...].astype(F32) + b_ref[...].astype(F32)).astype(o_ref.dtype)

    blk = (1,) + mine.shape[1:]
    return pl.pallas_call(
        body, name="add_sibling", out_shape=jax.ShapeDtypeStruct(theirs.shape, mine.dtype),
        grid_spec=pltpu.PrefetchScalarGridSpec(
            num_scalar_prefetch=1, grid=(theirs.shape[0],),
            in_specs=[pl.BlockSpec(blk, lambda q, c: (2 * q + c[0], 0, 0)), pl.BlockSpec(blk, lambda q, c: (q, 0, 0))],
            out_specs=pl.BlockSpec(blk, lambda q, c: (q, 0, 0))),
        compiler_params=_params(("parallel",)),
    )(core, mine, theirs)


def _with_own(land, own, me):
    return lax.dynamic_update_index_in_dim(land, own, me, 0)


IN_SPLITS = (512, 512, 512, 8, 768, 256, 32, 1024, 1024)


def _from_shards(g, fn, out_widths, name, own=None, slot=None):
    _, k, n = g.shape
    tr = min(k, 256)

    def body(*refs):
        if own is not None:
            s_ref, g_ref, own_ref = refs[:3]
            cols = [jnp.where(s_ref[0] == j, own_ref[...], g_ref[j]) for j in range(N_DEV)]
        else:
            g_ref = refs[0]
            cols = [g_ref[j] for j in range(N_DEV)]
        for o_ref, val in zip(refs[-len(out_widths):], fn(jnp.concatenate(cols, axis=1))):
            o_ref[...] = val

    in_specs = [pl.BlockSpec((N_DEV, tr, n), lambda i, *_: (0, i, 0))]
    out_spec = [pl.BlockSpec((tr, wd), lambda i, *_: (i, 0)) for wd in out_widths]
    out_shape = [jax.ShapeDtypeStruct((k, wd), g.dtype) for wd in out_widths]
    if own is None:
        return pl.pallas_call(body, name=name, grid=(k // tr,), in_specs=in_specs, out_specs=out_spec,
                              out_shape=out_shape, compiler_params=_params(("parallel",)))(g)
    in_specs.append(pl.BlockSpec((tr, n), lambda i, *_: (i, 0)))
    return pl.pallas_call(
        body, name=name, out_shape=out_shape, compiler_params=_params(("parallel",)),
        grid_spec=pltpu.PrefetchScalarGridSpec(num_scalar_prefetch=1, grid=(k // tr,), in_specs=in_specs, out_specs=out_spec),
    )(slot, g, own)


def _unshard_cols(g, own=None, slot=None):
    return _from_shards(g, lambda full: (full,), [N_DEV * g.shape[2]], "unshard_cols_%d" % g.shape[2], own, slot)[0]


FFN_T = D_FF // 2
FFN_SHARD = 2 * D_FF // N_DEV


def _unshard_ffn_in(g, own=None, slot=None):
    def pairs(full):
        parts = []
        for j in range(D_FF // FFN_T):
            parts += [full[:, j * FFN_T:(j + 1) * FFN_T], full[:, D_FF + j * FFN_T:D_FF + (j + 1) * FFN_T]]
        return (jnp.concatenate(parts, axis=1),)

    return _from_shards(g, pairs, [2 * D_FF], "unshard_ffn_in", own, slot)[0]


def _shard_ffn_in_t(wt):
    tc = 256

    def body(w_ref, o_ref):
        x = w_ref[...]
        nb = D_FF // FFN_T
        full = jnp.concatenate([x[(2 * j + half) * FFN_T:(2 * j + half + 1) * FFN_T]
                                for half in range(2) for j in range(nb)], axis=0)
        for j in range(N_DEV):
            o_ref[j] = full[j * FFN_SHARD:(j + 1) * FFN_SHARD]

    return pl.pallas_call(
        body, name="shard_ffn_in_t", grid=(D // tc,),
        in_specs=[pl.BlockSpec((2 * D_FF, tc), lambda i: (0, i))],
        out_specs=pl.BlockSpec((N_DEV, FFN_SHARD, tc), lambda i: (0, 0, i)),
        out_shape=jax.ShapeDtypeStruct((N_DEV, FFN_SHARD, D), wt.dtype),
        compiler_params=_params(("parallel",)),
    )(wt)


def _shard_cols(w):
    k, n = w.shape[0], w.shape[1] // N_DEV
    tr = min(k, 256)

    def body(w_ref, o_ref):
        full = w_ref[...]
        for j in range(N_DEV):
            o_ref[j] = full[:, j * n:(j + 1) * n]

    return pl.pallas_call(
        body, name="shard_cols_%d" % n, grid=(k // tr,),
        in_specs=[pl.BlockSpec((tr, N_DEV * n), lambda i: (i, 0))],
        out_specs=pl.BlockSpec((N_DEV, tr, n), lambda i: (0, i, 0)),
        out_shape=jax.ShapeDtypeStruct((N_DEV, k, n), w.dtype),
        compiler_params=_params(("parallel",)),
    )(w)


IN_OFFS = tuple(sum(IN_SPLITS[:i]) for i in range(len(IN_SPLITS) + 1))
IN_SHARD = IN_OFFS[-1] // N_DEV
REGROUP_ROWS = 256
MISC_AT = Q_LORA + KV_LORA + 2 * D
A_COLS = MISC_AT + LANES
A_TILE = A_COLS
B_COLS = 3 * HEADS * HEAD_DIM
MISC_BLOCK = MISC_AT // LANES
KR_AT = 64


def _w_in_regroup(g, own=None, slot=None):
    def groups(full):
        fq, fk, fv, wf, cq, ckv, kr, gf, gm = [full[:, IN_OFFS[i]:IN_OFFS[i + 1]] for i in range(9)]
        rows = full.shape[0]
        gap = jnp.zeros((rows, KR_AT - HEADS), BF16)
        tail = jnp.zeros((rows, LANES - KR_AT - ROPE_DIM), BF16)
        return jnp.concatenate([cq, ckv, gf, gm, wf, gap, kr, tail], axis=1), jnp.concatenate([fq, fk, fv], axis=1)

    return _from_shards(g, groups, [A_COLS, B_COLS], "w_in_regroup", own, slot)


def _w_in_ungroup(da, db_):
    def body(a_ref, b_ref, o_ref):
        a = a_ref[...]
        lora = Q_LORA + KV_LORA
        full = jnp.concatenate([b_ref[...], a[:, MISC_AT:MISC_AT + HEADS], a[:, :lora],
                                a[:, MISC_AT + KR_AT:MISC_AT + KR_AT + ROPE_DIM], a[:, lora:MISC_AT]], axis=1)
        for j in range(N_DEV):
            o_ref[j] = full[:, j * IN_SHARD:(j + 1) * IN_SHARD]

    tr = REGROUP_ROWS
    return pl.pallas_call(
        body, name="w_in_ungroup", grid=(D // tr,),
        in_specs=[pl.BlockSpec((tr, A_COLS), lambda i: (i, 0)), pl.BlockSpec((tr, B_COLS), lambda i: (i, 0))],
        out_specs=pl.BlockSpec((N_DEV, tr, IN_SHARD), lambda i: (0, i, 0)),
        out_shape=jax.ShapeDtypeStruct((N_DEV, D, IN_SHARD), BF16),
        compiler_params=_params(("parallel",)),
    )(da, db_)


def _prepare_weights(g, own=None, slot=None):
    w = {}
    if own is not None:
        small = ("w_uq", "w_ukv", "w_out", "w_ffn_out")
        g = {n: (_with_own(a, own[n], slot[0]) if n in small else a) for n, a in g.items()}
    pick = (lambda n: (own[n], slot)) if own is not None else (lambda n: (None, None))
    if "w_in" in g:
        w["w_a"], w["w_b"] = _w_in_regroup(g["w_in"], *pick("w_in"))
    if "w_uq" in g:
        w_uq = g["w_uq"].reshape(Q_LORA, HEADS, 96)
        w["w_uq"] = jnp.pad(w_uq, ((0, 0), (0, 0), (0, 32))).reshape(Q_LORA, HEADS * LANES)
        ukv = g["w_ukv"]
        w["w_k"] = jnp.transpose(jnp.pad(ukv[:, :, :64], ((0, 0), (0, 0), (0, 64))), (1, 0, 2)).reshape(KV_LORA, HEADS * LANES)
        w["w_v"] = jnp.transpose(ukv[:, :, 64:], (1, 0, 2)).reshape(KV_LORA, HEADS * HEAD_DIM)
    if "w_out" in g:
        w["w_pf"] = _unshard_cols(g["w_proj_fox"], *pick("w_proj_fox"))
        w["w_pm"] = _unshard_cols(g["w_proj_mla"], *pick("w_proj_mla"))
        w["w_out"] = g["w_out"].reshape(D, D)
    if "w_ffn_in" in g:
        w["w_ffn_in"] = _unshard_ffn_in(g["w_ffn_in"], *pick("w_ffn_in"))
        w["w_ffn_out"] = g["w_ffn_out"].reshape(D_FF, D)
    return w


def _shard_grads(dw):
    out = {}
    if "w_a" in dw:
        out["w_in"] = _w_in_ungroup(dw["w_a"], dw["w_b"])
    if "w_uq" in dw:
        w_uq = dw["w_uq"].reshape(Q_LORA, HEADS, LANES)[:, :, :96].reshape(Q_LORA, Q_LORA)
        out["w_uq"] = w_uq.reshape(N_DEV, Q_LORA // N_DEV, Q_LORA)
        k_part = dw["w_k"].reshape(KV_LORA, HEADS, LANES)[:, :, :64]
        v_part = dw["w_v"].reshape(KV_LORA, HEADS, HEAD_DIM)
        out["w_ukv"] = jnp.transpose(jnp.concatenate([k_part, v_part], axis=2), (1, 0, 2))
    if "w_out" in dw:
        out["w_proj_fox"] = _shard_cols(dw["w_pf"])
        out["w_proj_mla"] = _shard_cols(dw["w_pm"])
        out["w_out"] = dw["w_out"].reshape(N_DEV, D // N_DEV, D)
    if "w_ffn_in" in dw:
        out["w_ffn_in"] = _shard_ffn_in_t(dw["w_ffn_in"])
        out["w_ffn_out"] = dw["w_ffn_out"].reshape(N_DEV, D_FF // N_DEV, D)
    return out


def _fwd_bwd(x, pos, mod, target, w, vec, wts, send, relay):
    shift_mix, scale_mix, gate_mix, shift_ffn, scale_ffn, gate_ffn = [mod[:, i * D:(i + 1) * D] for i in range(6)]
    g_pre_mix, g_post_mix, g_pre_ffn, g_post_ffn = vec["g_pre_mix"], vec["g_post_mix"], vec["g_pre_ffn"], vec["g_post_ffn"]
    g_q, g_kv = vec["g_q_lora"], vec["g_kv_lora"]

    inv_freq = 1.0 / (ROPE_THETA ** (jnp.arange(0, ROPE_DIM, 2, dtype=F32) / ROPE_DIM))
    invf = jnp.concatenate([jnp.zeros((64,), F32), inv_freq, inv_freq, jnp.zeros((32,), F32)]).reshape(1, LANES)
    ct, sa, sb = _rope_tables(pos, invf)

    def pre1(xv, g, sc, sh):
        return (xv * _rstd(xv) * g) * (1.0 + sc) + sh
    proj_a, h = _mm_epi(x, w["w_a"], "nn", A_TILE, lambda r: ((r,), ()), "in_proj_a", 512, outs=[(A_COLS, A_TILE, F32)],
                        pro=(pre1, [g_pre_mix, scale_mix, shift_mix], 0))
    qkv = _mm(h, w["w_b"], "nn", BF16, "in_proj_b")

    def lora_norm(cv, g):
        return cv * _rstd(cv) * g
    w = {**w, **wts("lora", qkv)}
    tables = [(ct, LANES), (sa, LANES), (sb, LANES)]

    def rope_q(qv, c_, a_, b_):
        return (jnp.concatenate([_rope(qv[:, hd * LANES:(hd + 1) * LANES], c_, a_, b_) for hd in range(HEADS)], axis=1),), ()
    q_m, cqn = _mm_epi(proj_a, w["w_uq"], "nn", D, rope_q, "mla_uq", 512, rows=tables, outs=[(D, D, BF16)],
                       pro=(lora_norm, [g_q], 0))

    def rope_k(kv, misc, c_, a_, b_):
        lane = lax.broadcasted_iota(jnp.int32, (1, LANES), 1)
        kpe = jnp.where((lane >= 64) & (lane < 96), _rope(misc, c_, a_, b_), 0.0)
        return (jnp.concatenate([kv[:, hd * LANES:(hd + 1) * LANES] + kpe for hd in range(HEADS)], axis=1),), ()
    k_m, ckvn = _mm_epi(proj_a, w["w_k"], "nn", D, rope_k, "mla_uk", 512, rows=[(proj_a, LANES, MISC_BLOCK)] + tables,
                        outs=[(D, D, BF16)], pro=(lora_norm, [g_kv], Q_LORA // KV_LORA))
    v_m = _mm(ckvn, w["w_v"], "nn", BF16, "mla_uv")

    bf = jnp.transpose(vec["b_forget"])
    zt, neg_f = _fox_gates(proj_a, MISC_BLOCK, bf)
    bias = neg_f.reshape(HEADS, N_ATT, 1, ATT_T)
    o_b, lse_b = _attn_fwd(q_m, 0, k_m, 0, v_m, 0, 2 * LANES, 1.0 / math.sqrt(64 + ROPE_DIM), None, "mla_attn")
    bias = bias + wts("relay_proj", o_b)["tok"][0, 0]
    o_a, lse_a = _attn_fwd(qkv, 0, qkv, 4, qkv, 8, LANES, 1.0 / math.sqrt(HEAD_DIM), bias, "fox_attn")

    w = {**w, **wts("proj", o_a)}
    gate_mix = gate_mix + wts("relay_ffn", o_a)["tok"][0, 0]
    pa = _mm(o_a, w["w_pf"], "nn", BF16, "proj_fox")

    def merge(pb_, gf, gm, pa_):
        return (_sigmoid(gf) * pa_ + _sigmoid(gm) * pb_, pb_), ()
    merged, pb = _mm_epi(o_b, w["w_pm"], "nn", 512, merge, "proj_mla", 1024,
                         rows=[(proj_a, 512, 2), (proj_a, 512, 4), (pa, 512)], outs=[(D, 512, BF16), (D, 512, BF16)])
    def post1(yv, xv, gate, gpost, gpre, sc, sh):
        x1 = xv + gate * (yv * _rstd(yv) * gpost)
        return (x1, (x1 * _rstd(x1) * gpre) * (1.0 + sc) + sh, yv), ()
    x1, h2, y = _mm_epi(merged, w["w_out"], "nn", D, post1, "out_proj", 512, rows=[(x, D)],
                        vecs=[gate_mix, g_post_mix, g_pre_ffn, scale_ffn, shift_ffn],
                        outs=[(D, D, F32), (D, D, BF16), (D, D, F32)])
    w = {**w, **wts("ffn", h2)}

    def swiglu(r):
        g, u = r[:, :FFN_T], r[:, FFN_T:]
        return (g * _sigmoid(g) * u, r), ()
    act, gu = _mm_epi(h2, w["w_ffn_in"], "nn", 2 * FFN_T, swiglu, "ffn_in", 512,
                      outs=[(D_FF, FFN_T, BF16), (2 * D_FF, 2 * FFN_T, BF16)])

    def head(y2v, x1v, tv, gate, gpost):
        r = _rstd(y2v)
        yn = y2v * r
        n2 = yn * gpost
        err = (x1v + gate * n2) - tv
        dx2 = err * (1.0 / D)
        dn2 = dx2 * gate
        dy2 = _norm_bwd(dn2 * gpost, yn, r)
        return (dx2, dy2), (_colsum(err * err), _colsum(dx2 * n2), _colsum(dn2 * yn))
    dx2, dy2, err_cols, d_gate_ffn, d_g_post_ffn = _mm_epi(
        act, w["w_ffn_out"], "nn", D, head, "ffn_out", 512, rows=[(x1, D), (target, D)], vecs=[gate_ffn, g_post_ffn],
        outs=[(D, D, F32), (D, D, BF16)], sums=[D, D, D])

    def swiglu_bwd(da, guv):
        g, u = guv[:, :FFN_T].astype(F32), guv[:, FFN_T:].astype(F32)
        sg = _sigmoid(g)
        return (jnp.concatenate([da * u * (sg * (1.0 + g * (1.0 - sg))), da * (g * sg)], axis=1),), ()
    (dgu,) = _mm_epi(dy2, w["w_ffn_out"], "nt", FFN_T, swiglu_bwd, "ffn_out_dx", 512, rows=[(gu, 2 * FFN_T)],
                     outs=[(2 * D_FF, 2 * FFN_T, BF16)])
    dw = {"w_ffn_out": _mm(act, dy2, "tn", BF16, "ffn_out_dw")}
    dw["w_ffn_in"] = _mm(dgu, h2, "tn", BF16, "ffn_in_dw")
    gate_mix = gate_mix + send({n: dw.pop(n) for n in ("w_ffn_in", "w_ffn_out")})[0, 0]

    def mid(dh, x1v, dx2v, yv, gpre, sc, gate, gpost):
        r2 = _rstd(x1v)
        x1n = x1v * r2
        t = dh * x1n
        dx1 = dx2v + _norm_bwd(dh * (gpre * (1.0 + sc)), x1n, r2)
        ry = _rstd(yv)
        yn = yv * ry
        dn1 = dx1 * gate
        dy = _norm_bwd(dn1 * gpost, yn, ry)
        sums = (_colsum(dh), _colsum(t) * gpre, _colsum(t) * (1.0 + sc), _colsum(dx1 * (yn * gpost)), _colsum(dn1 * yn))
        return (dx1, dy), sums
    dx1, dy, d_shift_ffn, d_scale_ffn, d_g_pre_ffn, d_gate_mix, d_g_post_mix = _mm_epi(
        dgu, w["w_ffn_in"], "nt", D, mid, "ffn_in_dx", 512, rows=[(x1, D), (dx2, D), (y, D)],
        vecs=[g_pre_ffn, scale_ffn, gate_mix, g_post_mix], outs=[(D, D, F32), (D, D, BF16)], sums=[D] * 5)

    dw["w_out"] = _mm(merged, dy, "tn", BF16, "out_proj_dw")

    def merge_bwd(dm, gf, gm, pa_, pb_):
        sf, sm = _sigmoid(gf), _sigmoid(gm)
        return (dm * sf, dm * sm, dm * pa_ * (sf * (1.0 - sf)), dm * pb_ * (sm * (1.0 - sm))), ()
    dpa, dpb, dgf, dgm = _mm_epi(dy, w["w_out"], "nt", 512, merge_bwd, "out_proj_dx", 1024,
                                 rows=[(proj_a, 512, 2), (proj_a, 512, 4), (pa, 512), (pb, 512)],
                                 outs=[(D, 512, BF16)] * 4)
    do_a = _mm(dpa, w["w_pf"], "nt", BF16, "proj_fox_dx")
    do_b = _mm(dpb, w["w_pm"], "nt", BF16, "proj_mla_dx")
    dw["w_pf"] = _mm(o_a, dpa, "tn", BF16, "proj_fox_dw")
    dw["w_pm"] = _mm(o_b, dpb, "tn", BF16, "proj_mla_dw")
    bias = bias + send({n: dw.pop(n) for n in ("w_out", "w_pf", "w_pm")})[0, 0]

    sc_a, sc_b = 1.0 / math.sqrt(HEAD_DIM), 1.0 / math.sqrt(64 + ROPE_DIM)
    dq_a, dk_a, dv_a, dbias = _attn_grad(qkv, 0, qkv, 4, qkv, 8, do_a, lse_a, LANES, sc_a, bias, BF16, "fox_attn_bwd")
    dq_m, dk_m, dv_m = _attn_grad(q_m, 0, k_m, 0, v_m, 0, do_b, lse_b, 2 * LANES, sc_b, None, F32, "mla_attn_bwd")

    def mla_rope_bwd(dq, dk, c_, a_, b_):
        lane = lax.broadcasted_iota(jnp.int32, (1, LANES), 1)
        dqs = [_rope_t(dq[:, hd * LANES:(hd + 1) * LANES], c_, a_, b_) for hd in range(HEADS)]
        dkpe = dk[:, 0:LANES]
        for hd in range(1, HEADS):
            dkpe = dkpe + dk[:, hd * LANES:(hd + 1) * LANES]
        dkpe = jnp.where((lane >= 64) & (lane < 96), dkpe, 0.0)
        dkr = jnp.where((lane >= 64) & (lane < 96), _rope_t(dkpe, c_, a_, b_), 0.0)
        return (jnp.concatenate(dqs, axis=1), dk, dkr), ()
    dqb, dkb, dkr = _rowwise(mla_rope_bwd, [(dq_m, D, 0), (dk_m, D, 0), (ct, LANES, 0), (sa, LANES, 0), (sb, LANES, 0)],
                             [], [(D, BF16), (D, BF16), (LANES, F32)], [], "mla_rope_bwd")
    def lora_q_bwd(dq, cq, gq):
        rq = _rstd(cq)
        cqh = cq * rq
        return (_norm_bwd(dq * gq, cqh, rq),), (_colsum(dq * cqh),)
    dcq, d_g_q = _mm_epi(dqb, w["w_uq"], "nt", Q_LORA, lora_q_bwd, "mla_uq_dx", 512, rows=[(proj_a, Q_LORA, 0)],
                         vecs=[g_q], outs=[(Q_LORA, Q_LORA, BF16)], sums=[Q_LORA])

    def lora_kv_bwd(dv_part, dk_part, ckv, gkv):
        dkv = dv_part + dk_part
        rk = _rstd(ckv)
        ckh = ckv * rk
        return (_norm_bwd(dkv * gkv, ckh, rk),), (_colsum(dkv * ckh),)
    dckv, d_g_kv = _mm_epi(dv_m, w["w_v"], "nt", KV_LORA, lora_kv_bwd, "mla_uv_dx", 1024,
                           rows=[(_mm(dkb, w["w_k"], "nt", F32, "mla_uk_dx"), KV_LORA), (proj_a, KV_LORA, 3)],
                           vecs=[g_kv], outs=[(KV_LORA, KV_LORA, BF16)], sums=[KV_LORA])

    dzt, d_bf = _fox_gates_bwd(dbias.reshape(HEADS, S), zt, bf)
    dmisc = (dkr + jnp.pad(jnp.transpose(dzt), ((0, 0), (0, LANES - HEADS)))).astype(BF16)
    dproj_a = jnp.concatenate([dcq, dckv, dgf, dgm, dmisc], axis=1)
    dqkv = jnp.concatenate([dq_a, dk_a, dv_a], axis=1)
    dw["w_a"] = _mm(h, dproj_a, "tn", BF16, "in_proj_a_dw")
    dw["w_b"] = _mm(h, dqkv, "tn", BF16, "in_proj_b_dw")
    tok = send(dw, True)
    dh_a = _mm(dproj_a, w["w_a"], "nt", F32, "in_proj_a_dx", dep=tok)
    tok = relay(dh_a)
    tok = send({"w_uq": _mm(cqn, dqb, "tn", BF16, "mla_uq_dw", dep=tok),
                "w_k": _mm(ckvn, dkb, "tn", BF16, "mla_uk_dw", dep=tok),
                "w_v": _mm(ckvn, dv_m, "tn", BF16, "mla_uv_dw", dep=tok)}, late=True)
    g_pre_mix = g_pre_mix + tok[0, 0]

    def first(dh_b, dh_a, xv, dx1v, gpre, sc):
        dhv = dh_b + dh_a
        r = _rstd(xv)
        xn = xv * r
        t = dhv * xn
        dx = dx1v + _norm_bwd(dhv * (gpre * (1.0 + sc)), xn, r)
        return (dx,), (_colsum(dhv), _colsum(t) * gpre, _colsum(t) * (1.0 + sc))
    grad_x, d_shift_mix, d_scale_mix, d_g_pre_mix = _mm_epi(
        dqkv, w["w_b"], "nt", D, first, "in_proj_b_dx", 512,
        rows=[(dh_a, D), (x, D), (dx1, D)],
        vecs=[g_pre_mix, scale_mix], outs=[(D, D, F32)], sums=[D] * 3)

    dmod = jnp.concatenate([d_shift_mix, d_scale_mix, d_gate_mix, d_shift_ffn, d_scale_ffn, d_gate_ffn], axis=1)
    small = dict(dmod=dmod, g_pre_mix=d_g_pre_mix, g_post_mix=d_g_post_mix, g_pre_ffn=d_g_pre_ffn,
                 g_post_ffn=d_g_post_ffn, g_q_lora=d_g_q, g_kv_lora=d_g_kv,
                 b_forget=jnp.pad(jnp.transpose(d_bf), ((0, 0), (0, LANES - HEADS))), err=err_cols)
    return grad_x, small


SMALL_ORDER = ("dmod", "g_pre_mix", "g_post_mix", "g_pre_ffn", "g_post_ffn", "g_q_lora", "g_kv_lora", "b_forget", "err")
SMALL_PARAM = {"dmod": "b_ada"}
MATRICES = ("w_in", "w_uq", "w_ukv", "w_proj_fox", "w_proj_mla", "w_out", "w_ffn_in", "w_ffn_out")
WEIGHTS = ("w_ada", "b_ada", "g_pre_mix", "g_post_mix", "g_pre_ffn", "g_post_ffn", "w_in", "b_forget", "g_q_lora",
           "w_uq", "g_kv_lora", "w_ukv", "w_proj_fox", "w_proj_mla", "w_out", "w_ffn_in", "w_ffn_out")


def kernel(x, c, positions, w_ada, b_ada, g_pre_mix, g_post_mix, g_pre_ffn, g_post_ffn, w_in, b_forget, g_q_lora, w_uq, g_kv_lora, w_ukv, w_proj_fox, w_proj_mla, w_out, w_ffn_in, w_ffn_out, loss_target, m_w_ada, m_b_ada, m_g_pre_mix, m_g_post_mix, m_g_pre_ffn, m_g_post_ffn, m_w_in, m_b_forget, m_g_q_lora, m_w_uq, m_g_kv_lora, m_w_ukv, m_w_proj_fox, m_w_proj_mla, m_w_out, m_w_ffn_in, m_w_ffn_out, v_w_ada, v_b_ada, v_g_pre_mix, v_g_post_mix, v_g_pre_ffn, v_g_post_ffn, v_w_in, v_b_forget, v_g_q_lora, v_w_uq, v_g_kv_lora, v_w_ukv, v_w_proj_fox, v_w_proj_mla, v_w_out, v_w_ffn_in, v_w_ffn_out):
    prm = dict(w_ada=w_ada, b_ada=b_ada, g_pre_mix=g_pre_mix, g_post_mix=g_post_mix, g_pre_ffn=g_pre_ffn,
               g_post_ffn=g_post_ffn, w_in=w_in, b_forget=b_forget, g_q_lora=g_q_lora, w_uq=w_uq, g_kv_lora=g_kv_lora,
               w_ukv=w_ukv, w_proj_fox=w_proj_fox, w_proj_mla=w_proj_mla, w_out=w_out, w_ffn_in=w_ffn_in, w_ffn_out=w_ffn_out)
    mom = dict(w_ada=m_w_ada, b_ada=m_b_ada, g_pre_mix=m_g_pre_mix, g_post_mix=m_g_post_mix, g_pre_ffn=m_g_pre_ffn,
               g_post_ffn=m_g_post_ffn, w_in=m_w_in, b_forget=m_b_forget, g_q_lora=m_g_q_lora, w_uq=m_w_uq,
               g_kv_lora=m_g_kv_lora, w_ukv=m_w_ukv, w_proj_fox=m_w_proj_fox, w_proj_mla=m_w_proj_mla, w_out=m_w_out,
               w_ffn_in=m_w_ffn_in, w_ffn_out=m_w_ffn_out)
    var = dict(w_ada=v_w_ada, b_ada=v_b_ada, g_pre_mix=v_g_pre_mix, g_post_mix=v_g_post_mix, g_pre_ffn=v_g_pre_ffn,
               g_post_ffn=v_g_post_ffn, w_in=v_w_in, b_forget=v_b_forget, g_q_lora=v_g_q_lora, w_uq=v_w_uq,
               g_kv_lora=v_g_kv_lora, w_ukv=v_w_ukv, w_proj_fox=v_w_proj_fox, w_proj_mla=v_w_proj_mla, w_out=v_w_out,
               w_ffn_in=v_w_ffn_in, w_ffn_out=v_w_ffn_out)
    me = _flat(*_coords())
    slot = jnp.reshape(me, (1,)).astype(jnp.int32)

    own = {n: prm[n][0].astype(BF16) for n in MATRICES}
    no_dep = jnp.zeros((8, LANES), F32)
    (st_c, st_in), tok = _async_start([[c], [own["w_in"]]], ["gather", "spread"], no_dep, "gather_in_start")
    (c_own,), (c_land,) = _async_wait(st_c, tok, "gather_c_wait")
    c_all = _with_own(c_land, c_own, me).reshape(N_DEV, D)
    ada_cols = w_ada.shape[2]
    b_cols = lax.dynamic_slice(b_ada, (0, me * ada_cols), (1, ada_cols))
    mod_cols, silu_c = _mod_part(c_all, w_ada[0], b_cols)
    (mod_all,) = _all_gather([mod_cols], "gather_mod")

    (w_in_own,), (w_in_land,) = _async_wait(st_in, mod_all, "gather_in_wait")
    (st_in,), tok = _async_start([[w_in_land]], "forward", no_dep, "gather_in_forward")
    _, (w_in_land,) = _async_wait(st_in, tok, "gather_in_forward_wait")
    w = _prepare_weights({"w_in": w_in_land}, {"w_in": w_in_own}, slot)
    later = dict(lora=("w_uq", "w_ukv"), proj=("w_proj_fox", "w_proj_mla", "w_out"), ffn=("w_ffn_in", "w_ffn_out"))
    states, tok = _async_start([[own[n] for n in names] for names in later.values()], ["gather", "spread", "spread"],
                               w["w_b"], "gather_rest_start")
    gather_state = dict(zip(later, states))
    own_thru = {}

    def wts(group, after):
        if group.startswith("relay_"):
            name = group[len("relay_"):]
            own_thru[name], lands = _async_wait(gather_state[name], after, "gather_" + name + "_wait")
            (gather_state[name],), t = _async_start([lands], "forward", no_dep, "gather_" + name + "_forward")
            return {"tok": t}
        srcs, lands = _async_wait(gather_state[group], after, "gather_" + group + "_landed")
        srcs = own_thru.get(group, srcs)
        return _prepare_weights(dict(zip(later[group], lands)), dict(zip(later[group], srcs)), slot)

    sent, late_sent, last = [], [], {}

    def send(grads, final=False, late=False):
        shards = _shard_grads(grads)
        names = list(shards)
        (state,), t = _async_start([[shards[n] for n in names]], "pair" if final else "exchange", no_dep,
                                   "exchange_" + names[0] + "_start")
        if final:
            last.update(names=names, state=state)
        else:
            (late_sent if late else sent).append((names, state))
        return t

    def relay(after):
        srcs, lands = _async_wait(last["state"], after, "exchange_pair_wait")
        core = jnp.reshape(lax.axis_index("c"), (1,)).astype(jnp.int32)
        sums = [_add_sibling(src, land, core) for src, land in zip(srcs, lands)]
        (last["state"],), t = _async_start([sums], "chips", no_dep, "exchange_chips_start")
        return t

    mod = lax.dynamic_index_in_dim(mod_all, me, axis=1, keepdims=False).reshape(1, 6 * D) + tok[0, 0]

    vec = dict(g_pre_mix=g_pre_mix, g_post_mix=g_post_mix, g_pre_ffn=g_pre_ffn, g_post_ffn=g_post_ffn,
               g_q_lora=g_q_lora, g_kv_lora=g_kv_lora, b_forget=b_forget)
    pos = positions.astype(F32).reshape(S, 1)
    grad_x, small = _fwd_bwd(x[0], pos, mod, loss_target[0], w, vec, wts, send, relay)

    bundle = jnp.concatenate([small[n] for n in SMALL_ORDER], axis=1)
    (small_state,), tok = _async_start([[bundle]], "gather", jnp.zeros((8, LANES), F32), "gather_small_start")

    out = {}
    swap = lambda a: jnp.swapaxes(a, -1, -2)

    def update(n, land, src, sl):
        if n != "w_ffn_in":
            out[n] = _adamw(prm[n][0], mom[n][0], var[n][0], land, "adamw_" + n, src, sl)
            return out[n][0]
        res = _adamw(swap(prm[n][0]), swap(mom[n][0]), swap(var[n][0]), land, "adamw_" + n, src, sl)
        out[n] = tuple(swap(t) for t in res)
        return res[0]

    after = tok
    for names, state in sent:
        srcs, lands = _async_wait(state, after, "exchange_" + names[0] + "_wait")
        for n, src, land in zip(names, srcs, lands):
            after = update(n, land, src, slot)
    srcs, lands = _async_wait(last["state"], after, "exchange_chips_wait")
    for n, src, land in zip(last["names"], srcs, lands):
        after = update(n, land, src, slot // 2)
    for names, state in late_sent:
        srcs, lands = _async_wait(state, after, "exchange_" + names[0] + "_wait")
        for n, src, land in zip(names, srcs, lands):
            after = update(n, land, src, slot)

    (own_bundle,), (bundle_all,) = _async_wait(small_state, after, "gather_small_wait")
    bundle_all = _with_own(bundle_all, own_bundle, me)
    dmod_all = bundle_all[:, 0, :6 * D]
    dm_cols = lax.dynamic_slice(dmod_all, (0, me * ada_cols), (N_DEV, ada_cols))
    out["w_ada"] = _adamw_w_ada(w_ada[0], m_w_ada[0], v_w_ada[0], jnp.transpose(silu_c), dm_cols)

    offsets, off = {}, 0
    for n in SMALL_ORDER:
        offsets[n] = off
        off += small[n].shape[1]
    names = [SMALL_PARAM.get(n, n) for n in SMALL_ORDER if n != "err"]
    results, err = _adamw_rows(bundle_all, [offsets[n] for n in SMALL_ORDER if n != "err"],
                               [prm[n] for n in names], [mom[n] for n in names], [var[n] for n in names],
                               offsets["err"], D)
    out.update(zip(names, results))
    loss = 0.5 * jnp.sum(err) / D

    res = [loss, grad_x[None]]
    for kind in range(4):
        for n in WEIGHTS:
            t = out[n][kind]
            res.append(t[None] if prm[n].ndim == 3 else t)
    return tuple(res)
```

```python
import functools
import math

import jax
import jax.numpy as jnp
from jax import lax
from jax.experimental import pallas as pl
from jax.experimental.pallas import tpu as pltpu

F32 = jnp.float32
BF16 = jnp.bfloat16

N_DEV = 8
S = 2048
D = 1024
D_FF = 2816
HEADS = 8
HEAD_DIM = 64
Q_LORA = 768
KV_LORA = 256
ROPE_DIM = 32
ROPE_THETA = 10000.0
NORM_EPS = 1e-6
LANES = 128
VMEM_LIMIT = 56 * 1024 * 1024

ADAM_LR = 0.001
ADAM_B1 = 0.9
ADAM_B2 = 0.999
ADAM_EPS = 1e-08
ADAM_WD = 0.01
ADAM_STEP = 10

ATT_T = 256
LOG2E = 1.4426950408889634
N_ATT = S // ATT_T

NN = (((1,), (0,)), ((), ()))
NT = (((1,), (1,)), ((), ()))
TN = (((0,), (0,)), ((), ()))
MESH = pl.DeviceIdType.MESH


def _params(sem=None):
    return pltpu.CompilerParams(dimension_semantics=sem, vmem_limit_bytes=VMEM_LIMIT)


def _pick(n, cap):
    best = None
    for t in range(LANES, cap + 1, LANES):
        if n % t == 0:
            best = t
    return best if best is not None else n


def _mm(a, b, mode, out_dtype, name, acc=None, dep=None):
    if mode == "nn":
        (m, k), (k2, n), dn = a.shape, b.shape, NN
    elif mode == "nt":
        (m, k), (n, k2), dn = a.shape, b.shape, NT
    else:
        (k, m), (k2, n), dn = a.shape, b.shape, TN
    assert k == k2, (a.shape, b.shape, mode)
    tn = _pick(n, 1024)
    tm = _pick(m, 1536)
    osz = jnp.dtype(out_dtype).itemsize

    def need(tm_):
        blk = tm_ * k * 2 + tn * k * 2 + tm_ * tn * osz + (tm_ * tn * 4 if acc is not None else 0)
        return 2 * blk + tm_ * tn * 4
    while need(tm) > 36 * 1024 * 1024 and tm % 256 == 0:
        tm //= 2

    def body(*refs):
        a_ref, b_ref, o_ref = refs[0], refs[1], refs[-1]
        r = lax.dot_general(a_ref[...], b_ref[...], dn, preferred_element_type=F32)
        if acc is not None:
            r = r + refs[2][...]
        o_ref[...] = r.astype(o_ref.dtype)

    if mode == "tn":
        a_spec = pl.BlockSpec((k, tm), lambda i, j: (0, i))
    else:
        a_spec = pl.BlockSpec((tm, k), lambda i, j: (i, 0))
    if mode == "nt":
        b_spec = pl.BlockSpec((tn, k), lambda i, j: (j, 0))
    else:
        b_spec = pl.BlockSpec((k, tn), lambda i, j: (0, j))
    o_spec = pl.BlockSpec((tm, tn), lambda i, j: (i, j))
    in_specs = [a_spec, b_spec] + ([o_spec] if acc is not None else [])
    in_specs += [pl.BlockSpec(memory_space=pl.ANY)] if dep is not None else []
    args = (a, b) + ((acc,) if acc is not None else ()) + ((dep,) if dep is not None else ())
    return pl.pallas_call(
        body, name=name, grid=(m // tm, n // tn),
        in_specs=in_specs, out_specs=o_spec,
        out_shape=jax.ShapeDtypeStruct((m, n), out_dtype),
        compiler_params=_params(("parallel", "parallel")),
    )(*args)


def _mm_epi(a, b, mode, tnb, epi, name, tm, rows=(), vecs=(), outs=(), sums=(), pro=None):
    m = a.shape[0]
    k, nb = (b.shape if mode == "nn" else b.shape[::-1])
    dn = NN if mode == "nn" else NT
    pro_fn, pro_vecs, a_off = pro if pro is not None else (None, (), 0)
    n_in = 2 + len(rows) + len(vecs)
    n_all = n_in + len(pro_vecs)
    sub = min(tm, 256)

    def body(*refs):
        if pro is not None:
            a_out, a_scr = refs[-2:]
            refs = refs[:-2]

            @pl.when(pl.program_id(1) == 0)
            def _():
                a_scr[...] = pro_fn(refs[0][...], *[x[...] for x in refs[n_in:n_all]]).astype(BF16)
                a_out[...] = a_scr[...]
            a_ref = a_scr
        else:
            a_ref = refs[0]
        o_refs = refs[n_all:n_all + len(outs)]
        s_refs = refs[n_all + len(outs):]
        if sums:
            @pl.when((pl.program_id(0) == 0) & (pl.program_id(1) == 0))
            def _():
                for s_ref in s_refs:
                    s_ref[...] = jnp.zeros(s_ref.shape, F32)
        for c in range(tm // sub):
            rs = slice(c * sub, (c + 1) * sub)
            r = lax.dot_general(a_ref[rs, :], refs[1][...], dn, preferred_element_type=F32)
            o_vals, s_vals = epi(r, *[x[rs, :] for x in refs[2:2 + len(rows)]], *[x[...] for x in refs[2 + len(rows):n_in]])
            assert len(o_vals) == len(o_refs) and len(s_vals) == len(s_refs)
            for o_ref, val in zip(o_refs, o_vals):
                o_ref[rs, :] = val.astype(o_ref.dtype)
            for s_ref, val in zip(s_refs, s_vals):
                s_ref[...] += val

    once = dict(pipeline_mode=pl.Buffered(1)) if nb == tnb else {}
    if mode == "nn":
        b_spec = pl.BlockSpec((k, tnb), lambda i, j: (0, j), **once)
    else:
        b_spec = pl.BlockSpec((tnb, k), lambda i, j: (j, 0), **once)
    in_specs = [pl.BlockSpec((tm, k), lambda i, j: (i, a_off)), b_spec]
    rows = [tuple(r) + (0,) * (3 - len(r)) for r in rows]
    in_specs += [pl.BlockSpec((tm, w), functools.partial(lambda i, j, off: (i, j + off), off=off)) for _, w, off in rows]
    in_specs += [pl.BlockSpec(v.shape, lambda i, j: (0, 0)) for v in list(vecs) + list(pro_vecs)]
    out_specs = [pl.BlockSpec((tm, w), lambda i, j: (i, j)) for _, w, _ in outs]
    out_specs += [pl.BlockSpec((1, w), lambda i, j: (0, 0)) for w in sums]
    out_shape = [jax.ShapeDtypeStruct((m, full), dt) for full, _, dt in outs]
    out_shape += [jax.ShapeDtypeStruct((1, w), F32) for w in sums]
    if pro is not None:
        out_specs.append(pl.BlockSpec((tm, k), lambda i, j: (i, 0)))
        out_shape.append(jax.ShapeDtypeStruct((m, k), BF16))
    return pl.pallas_call(
        body, name=name, grid=(m // tm, nb // tnb),
        in_specs=in_specs, out_specs=out_specs, out_shape=out_shape,
        scratch_shapes=[pltpu.VMEM((tm, k), BF16)] if pro is not None else [],
        compiler_params=_params(("arbitrary", "arbitrary") if sums else ("parallel", "arbitrary" if pro is not None else "parallel")),
    )(a, b, *[r[0] for r in rows], *vecs, *pro_vecs)


def _rowwise(fn, row_ins, vec_ins, row_outs, sum_outs, name, tm=512):
    n_in = len(row_ins) + len(vec_ins)
    n_o = len(row_outs)
    rows = row_ins[0][0].shape[0]

    def body(*refs):
        vals = [r[...] for r in refs[:n_in]]
        outs = refs[n_in:]
        ro, so = fn(*vals)
        assert len(ro) == n_o and len(so) == len(sum_outs)
        for r, v in zip(outs[:n_o], ro):
            r[...] = v.astype(r.dtype)
        if sum_outs:
            @pl.when(pl.program_id(0) == 0)
            def _():
                for r in outs[n_o:]:
                    r[...] = jnp.zeros(r.shape, F32)
            for r, v in zip(outs[n_o:], so):
                r[...] += v

    in_specs = [pl.BlockSpec((tm, w), functools.partial(lambda i, b: (i, b), b=b)) for _, w, b in row_ins]
    in_specs += [pl.BlockSpec(v.shape, lambda i: (0, 0)) for v in vec_ins]
    out_specs = [pl.BlockSpec((tm, w), lambda i: (i, 0)) for w, _ in row_outs]
    out_specs += [pl.BlockSpec((1, w), lambda i: (0, 0)) for w in sum_outs]
    out_shape = [jax.ShapeDtypeStruct((rows, w), dt) for w, dt in row_outs]
    out_shape += [jax.ShapeDtypeStruct((1, w), F32) for w in sum_outs]
    return pl.pallas_call(
        body, name=name, grid=(rows // tm,),
        in_specs=in_specs, out_specs=out_specs, out_shape=out_shape,
        compiler_params=_params(("arbitrary",)),
    )(*[a for a, _, _ in row_ins], *vec_ins)


def _sigmoid(x):
    return 1.0 / (1.0 + jnp.exp(-x))


def _rstd(x):
    return lax.rsqrt(jnp.mean(x * x, axis=-1, keepdims=True) + NORM_EPS)


def _norm_bwd(dyn, xn, r):
    return r * (dyn - xn * jnp.mean(dyn * xn, axis=-1, keepdims=True))


def _colsum(x):
    return jnp.sum(x, axis=0, keepdims=True)


def _rope_tables(pos, invf):
    def fn(p, f):
        lane = lax.broadcasted_iota(jnp.int32, (1, LANES), 1)
        ang = p * f
        cs, sn = jnp.cos(ang), jnp.sin(ang)
        rot = (lane >= 64) & (lane < 96)
        ct = jnp.where(lane < 64, 1.0, jnp.where(rot, cs, 0.0))
        sa = jnp.where((lane >= 64) & (lane < 80), -sn, 0.0)
        sb = jnp.where((lane >= 80) & (lane < 96), sn, 0.0)
        return (ct, sa, sb), ()
    return _rowwise(fn, [(pos, 1, 0)], [invf], [(LANES, F32)] * 3, [], "rope_tables")


def _rope(x, ct, sa, sb):
    return x * ct + pltpu.roll(x, LANES - 16, 1) * sa + pltpu.roll(x, 16, 1) * sb


def _rope_t(x, ct, sa, sb):
    return x * ct - pltpu.roll(x, LANES - 16, 1) * sa - pltpu.roll(x, 16, 1) * sb


def _head_mask(width, hh):
    lane = lax.broadcasted_iota(jnp.int32, (1, width), 1)
    half = width // 2
    return (lane >= hh * half) & (lane < (hh + 1) * half)


ATT_PP = 2
ATT_CHAINS = [(a, hh) for a in range(ATT_PP) for hh in range(2)]
ATT_G = HEADS // (2 * ATT_PP)


def _pair(ref_or_val, a, width, rows=slice(None)):
    return ref_or_val[rows, a * width:(a + 1) * width]


def _head_cols(ref, a, hh, dkp, rows=slice(None)):
    if dkp == 2 * LANES:
        return ref[rows, a * dkp + hh * LANES:a * dkp + (hh + 1) * LANES]
    blk = _pair(ref, a, dkp, rows)
    return jnp.where(_head_mask(dkp, hh), blk, jnp.zeros_like(blk))


def _attn_fwd(q, qo, k, ko, v, vo, dkp, scale, bias, name):
    T = ATT_T
    assert qo % ATT_PP == 0 and ko % ATT_PP == 0 and vo % ATT_PP == 0
    qo, ko, vo = qo // ATT_PP, ko // ATT_PP, vo // ATT_PP
    split = dkp == 2 * LANES

    def body(*refs):
        if bias is not None:
            q_ref, k_ref, v_ref, b_ref, o_ref, lse_ref, s_scr = refs
        else:
            q_ref, k_ref, v_ref, o_ref, lse_ref, s_scr = refs
        i = pl.program_id(1)
        row = lax.broadcasted_iota(jnp.int32, (T, T), 0)
        col = lax.broadcasted_iota(jnp.int32, (T, T), 1)
        qms = [_head_cols(q_ref, a, hh, dkp) for a, hh in ATT_CHAINS]

        def k_of(a, hh, ks):
            return _head_cols(k_ref, a, hh, dkp, ks) if split else _pair(k_ref, a, dkp, ks)

        def fold(t):
            return [t[:, c * LANES:(c + 1) * LANES] for c in range(T // LANES)]

        def run(nt):
            mls = [jnp.full((T, LANES), -jnp.inf, F32) for _ in ATT_CHAINS]
            for j in range(nt):
                ks = slice(j * T, (j + 1) * T)
                for ci, (a, hh) in enumerate(ATT_CHAINS):
                    s = lax.dot_general(qms[ci], k_of(a, hh, ks), NT, preferred_element_type=F32) * (scale * LOG2E)
                    if bias is not None:
                        s = s + b_ref[2 * a + hh, j] * LOG2E
                    if j == nt - 1:
                        s = jnp.where(row >= col, s, -jnp.inf)
                    s_scr[ci, j] = s
                    for part in fold(s):
                        mls[ci] = jnp.maximum(mls[ci], part)
            ms = [jnp.max(ml, axis=1, keepdims=True) for ml in mls]
            mbs = [jnp.broadcast_to(m, (T, LANES)) for m in ms]
            for a in range(ATT_PP):
                ls = [jnp.zeros((T, LANES), F32) for _ in range(2)]
                ps, vms = [], []
                for j in range(nt):
                    vb = _pair(v_ref, a, LANES, slice(j * T, (j + 1) * T))
                    for hh in range(2):
                        parts = [jnp.exp2(part - mbs[2 * a + hh]) for part in fold(s_scr[2 * a + hh, j])]
                        for part in parts:
                            ls[hh] = ls[hh] + part
                        ps.append(jnp.concatenate(parts, axis=1).astype(BF16))
                        vms.append(jnp.where(_head_mask(LANES, hh), vb, jnp.zeros_like(vb)))
                acc = lax.dot_general(jnp.concatenate(ps, axis=1), jnp.concatenate(vms, axis=0), NN,
                                      preferred_element_type=F32)
                l0, l1 = [jnp.sum(l, axis=1, keepdims=True) for l in ls]
                lse_ref[2 * a] = ms[2 * a] + jnp.log2(l0)
                lse_ref[2 * a + 1] = ms[2 * a + 1] + jnp.log2(l1)
                inv = jnp.where(_head_mask(LANES, 0), 1.0 / l0, 1.0 / l1)
                o_ref[:, a * LANES:(a + 1) * LANES] = (acc * inv).astype(o_ref.dtype)

        for nt in range(1, N_ATT + 1):
            pl.when(i == nt - 1)(functools.partial(run, nt))

    in_specs = [
        pl.BlockSpec((T, ATT_PP * dkp), lambda g, i: (i, qo + g)),
        pl.BlockSpec((S, ATT_PP * dkp), lambda g, i: (0, ko + g)),
        pl.BlockSpec((S, ATT_PP * LANES), lambda g, i: (0, vo + g)),
    ]
    args = [q, k, v]
    if bias is not None:
        in_specs.append(pl.BlockSpec((2 * ATT_PP, N_ATT, 1, T), lambda g, i: (g, 0, 0, 0)))
        args.append(bias)
    return pl.pallas_call(
        body, name=name, grid=(ATT_G, N_ATT),
        in_specs=in_specs,
        out_specs=[pl.BlockSpec((T, ATT_PP * LANES), lambda g, i: (i, g)),
                   pl.BlockSpec((2 * ATT_PP, T, 1), lambda g, i: (g, i, 0))],
        out_shape=[jax.ShapeDtypeStruct((S, HEADS * HEAD_DIM), BF16),
                   jax.ShapeDtypeStruct((HEADS, S, 1), F32)],
        scratch_shapes=[pltpu.VMEM((len(ATT_CHAINS), N_ATT, T, T), F32)],
        compiler_params=_params(("parallel", "arbitrary")),
    )(*args)


def _attn_grad(q, qo, k, ko, v, vo, do, lse, dkp, scale, bias, qk_dtype, name):
    T = ATT_T
    has_b = bias is not None
    qo, ko, vo = qo // ATT_PP, ko // ATT_PP, vo // ATT_PP
    n_ch = len(ATT_CHAINS)
    split = dkp == 2 * LANES

    def body(*refs):
        q_ref, k_ref, v_ref, do_ref, lse_ref = refs[:5]
        refs = refs[5:]
        if has_b:
            b_ref, refs = refs[0], refs[1:]
        dq_ref, dk_ref, dv_ref = refs[:3]
        refs = refs[3:]
        if has_b:
            db_ref, refs = refs[0], refs[1:]
        p_scr, dp_scr, dk_acc, dv_acc = refs[:4]
        db_acc = refs[4] if has_b else None
        i = pl.program_id(1)

        @pl.when(i == 0)
        def _():
            dk_acc[...] = jnp.zeros(dk_acc.shape, F32)
            dv_acc[...] = jnp.zeros(dv_acc.shape, F32)
            if has_b:
                db_acc[...] = jnp.zeros(db_acc.shape, F32)

        row = lax.broadcasted_iota(jnp.int32, (T, T), 0)
        col = lax.broadcasted_iota(jnp.int32, (T, T), 1)

        def fold(t):
            return [t[:, c * LANES:(c + 1) * LANES] for c in range(T // LANES)]

        qms, doms, lses = [], [], []
        for a, hh in ATT_CHAINS:
            dob = _pair(do_ref, a, LANES)
            qms.append(_head_cols(q_ref, a, hh, dkp))
            doms.append(jnp.where(_head_mask(LANES, hh), dob, jnp.zeros_like(dob)))
            lses.append(lse_ref[2 * a + hh])

        def k_of(a, hh, ks):
            return _head_cols(k_ref, a, hh, dkp, ks) if split else _pair(k_ref, a, dkp, ks)

        def run(nt):
            dls = [jnp.zeros((T, LANES), F32) for _ in ATT_CHAINS]
            for j in range(nt):
                ks = slice(j * T, (j + 1) * T)
                for ci, (a, hh) in enumerate(ATT_CHAINS):
                    s = lax.dot_general(qms[ci], k_of(a, hh, ks), NT, preferred_element_type=F32) * (scale * LOG2E)
                    if has_b:
                        s = s + b_ref[ci, j] * LOG2E
                    s = s - lses[ci]
                    if j == nt - 1:
                        s = jnp.where(row >= col, s, -jnp.inf)
                    p = jnp.exp2(s)
                    dp = lax.dot_general(doms[ci], _pair(v_ref, a, LANES, ks), NT, preferred_element_type=F32)
                    p_scr[ci, j] = p
                    dp_scr[ci, j] = dp
                    for part in fold(p * dp):
                        dls[ci] = dls[ci] + part
            deltas = [jnp.broadcast_to(jnp.sum(dl, axis=1, keepdims=True), (T, LANES)) for dl in dls]
            for a in range(ATT_PP):
                ds_all, km_all = [], []
                if split:
                    qts = [jnp.transpose(qms[2 * a + hh]) for hh in range(2)]
                else:
                    qm2t = jnp.transpose(jnp.concatenate([qms[2 * a], qms[2 * a + 1]], axis=0))
                dom2t = jnp.transpose(jnp.concatenate([doms[2 * a], doms[2 * a + 1]], axis=0))
                for j in range(nt):
                    ks = slice(j * T, (j + 1) * T)
                    p2, ds2 = [], []
                    for hh in range(2):
                        ci = 2 * a + hh
                        p = p_scr[ci, j]
                        ds = jnp.concatenate([pp * (dd - deltas[ci]) for pp, dd in zip(fold(p), fold(dp_scr[ci, j]))], axis=1)
                        if has_b:
                            db_acc[ci, j] += jnp.sum(ds, axis=0, keepdims=True)
                        p2.append(p.astype(BF16))
                        ds2.append((ds * scale).astype(BF16))
                        if not split:
                            km_all.append(_head_cols(k_ref, a, hh, dkp, ks))
                    dv_acc[a * LANES:(a + 1) * LANES, ks] += lax.dot_general(
                        dom2t, jnp.concatenate(p2, axis=0), NN, preferred_element_type=F32)
                    if split:
                        for hh in range(2):
                            dk_acc[a * dkp + hh * LANES:a * dkp + (hh + 1) * LANES, ks] += lax.dot_general(
                                qts[hh], ds2[hh], NN, preferred_element_type=F32)
                    else:
                        dk_acc[a * dkp:(a + 1) * dkp, ks] += lax.dot_general(
                            qm2t, jnp.concatenate(ds2, axis=0), NN, preferred_element_type=F32)
                    ds_all += ds2
                if split:
                    for hh in range(2):
                        dq = lax.dot_general(jnp.concatenate(ds_all[hh::2], axis=1),
                                             _head_cols(k_ref, a, hh, dkp, slice(0, nt * T)), NN,
                                             preferred_element_type=F32)
                        dq_ref[:, a * dkp + hh * LANES:a * dkp + (hh + 1) * LANES] = dq.astype(dq_ref.dtype)
                else:
                    dq = lax.dot_general(jnp.concatenate(ds_all, axis=1), jnp.concatenate(km_all, axis=0), NN,
                                         preferred_element_type=F32)
                    dq_ref[:, a * dkp:(a + 1) * dkp] = dq.astype(dq_ref.dtype)

        for nt in range(1, N_ATT + 1):
            pl.when(i == nt - 1)(functools.partial(run, nt))

        @pl.when(i == N_ATT - 1)
        def _():
            dk_ref[...] = jnp.transpose(dk_acc[...]).astype(dk_ref.dtype)
            dv_ref[...] = jnp.transpose(dv_acc[...]).astype(dv_ref.dtype)
            if has_b:
                db_ref[...] = db_acc[...]

    in_specs = [
        pl.BlockSpec((T, ATT_PP * dkp), lambda g, i: (i, qo + g)),
        pl.BlockSpec((S, ATT_PP * dkp), lambda g, i: (0, ko + g)),
        pl.BlockSpec((S, ATT_PP * LANES), lambda g, i: (0, vo + g)),
        pl.BlockSpec((T, ATT_PP * LANES), lambda g, i: (i, g)),
        pl.BlockSpec((2 * ATT_PP, T, 1), lambda g, i: (g, i, 0)),
    ]
    args = [q, k, v, do, lse]
    out_specs = [
        pl.BlockSpec((T, ATT_PP * dkp), lambda g, i: (i, g)),
        pl.BlockSpec((S, ATT_PP * dkp), lambda g, i: (0, g)),
        pl.BlockSpec((S, ATT_PP * LANES), lambda g, i: (0, g)),
    ]
    width = (HEADS // 2) * dkp
    out_shape = [
        jax.ShapeDtypeStruct((S, width), qk_dtype),
        jax.ShapeDtypeStruct((S, width), qk_dtype),
        jax.ShapeDtypeStruct((S, HEADS * HEAD_DIM), BF16),
    ]
    scratch = [pltpu.VMEM((n_ch, N_ATT, T, T), F32), pltpu.VMEM((n_ch, N_ATT, T, T), F32),
               pltpu.VMEM((ATT_PP * dkp, S), F32), pltpu.VMEM((ATT_PP * LANES, S), F32)]
    if has_b:
        bspec = pl.BlockSpec((2 * ATT_PP, N_ATT, 1, T), lambda g, i: (g, 0, 0, 0))
        in_specs.append(bspec)
        args.append(bias)
        out_specs.append(bspec)
        out_shape.append(jax.ShapeDtypeStruct((HEADS, N_ATT, 1, T), F32))
        scratch.append(pltpu.VMEM((2 * ATT_PP, N_ATT, 1, T), F32))
    return pl.pallas_call(
        body, name=name, grid=(ATT_G, N_ATT),
        in_specs=in_specs, out_specs=out_specs, out_shape=out_shape, scratch_shapes=scratch,
        compiler_params=_params(("parallel", "arbitrary")),
    )(*args)


def _tri(upper):
    a = lax.broadcasted_iota(jnp.int32, (LANES, LANES), 0)
    b = lax.broadcasted_iota(jnp.int32, (LANES, LANES), 1)
    return jnp.where(a <= b if upper else a >= b, 1.0, 0.0).astype(F32)


def _fox_gates(proj, blk, bf):
    def body(m_ref, b_ref, z_out, o_ref):
        tri = _tri(True)
        carry = jnp.zeros((HEADS, 1), F32)
        for t in range(S // LANES):
            sl = slice(t * LANES, (t + 1) * LANES)
            zt = jnp.transpose(m_ref[sl, :])[:HEADS]
            z_out[:, sl] = zt
            z = zt + b_ref[...]
            logf = jnp.minimum(z, 0.0) - jnp.log(1.0 + jnp.exp(-jnp.abs(z)))
            c = lax.dot_general(logf, tri, NN, preferred_element_type=F32,
                                precision=lax.Precision.HIGHEST) + carry
            o_ref[:, sl] = -c
            carry = c[:, LANES - 1:LANES]

    return pl.pallas_call(
        body, name="fox_gates", grid=(1,),
        in_specs=[pl.BlockSpec((S, LANES), lambda i: (0, blk)), pl.BlockSpec(bf.shape, lambda i: (0, 0))],
        out_specs=[pl.BlockSpec((HEADS, S), lambda i: (0, 0))] * 2,
        out_shape=[jax.ShapeDtypeStruct((HEADS, S), F32)] * 2,
        compiler_params=_params(("arbitrary",)),
    )(proj, bf)


def _fox_gates_bwd(dbias, zt, bf):
    def body(d_ref, z_ref, b_ref, dz_ref, dbf_ref):
        tri = _tri(False)
        carry = jnp.zeros((HEADS, 1), F32)
        tot = jnp.zeros((HEADS, 1), F32)
        for t in reversed(range(S // LANES)):
            sl = slice(t * LANES, (t + 1) * LANES)
            df = -d_ref[:, sl]
            c = lax.dot_general(df, tri, NN, preferred_element_type=F32,
                                precision=lax.Precision.HIGHEST) + carry
            carry = c[:, 0:1]
            z = z_ref[:, sl] + b_ref[...]
            dz = c * _sigmoid(-z)
            dz_ref[:, sl] = dz
            tot = tot + jnp.sum(dz, axis=1, keepdims=True)
        dbf_ref[...] = tot

    return pl.pallas_call(
        body, name="fox_gates_bwd",
        out_shape=[jax.ShapeDtypeStruct((HEADS, S), F32), jax.ShapeDtypeStruct((HEADS, 1), F32)],
        compiler_params=_params(),
    )(dbias, zt, bf)


def _mod_part(c_all, w_ada, b_cols):
    def body(c_ref, w_ref, b_ref, o_ref, s_ref):
        c = c_ref[...]
        sc = c * _sigmoid(c)
        s_ref[...] = sc
        o_ref[...] = lax.dot_general(sc, w_ref[...], NN, preferred_element_type=F32,
                                     precision=lax.Precision.HIGHEST) + b_ref[...]

    return pl.pallas_call(
        body, name="mod_part",
        out_shape=[jax.ShapeDtypeStruct((N_DEV, w_ada.shape[1]), F32), jax.ShapeDtypeStruct(c_all.shape, F32)],
        compiler_params=_params(),
    )(c_all, w_ada, b_cols)


def _adamw_w_ada(w, m, v, sc_t, dm):
    rows, cols = w.shape
    tr = 256

    def body(w_ref, m_ref, v_ref, s_ref, d_ref, g_out, d_out, m_out, v_out):
        g = s_ref[:, 0:1] * d_ref[0:1, :]
        for b in range(1, N_DEV):
            g = g + s_ref[:, b:b + 1] * d_ref[b:b + 1, :]
        g_out[...] = g
        d_out[...], m_out[...], v_out[...] = _adamw_math(w_ref[...], g, m_ref[...], v_ref[...])

    spec = pl.BlockSpec((tr, cols), lambda i: (i, 0))
    return pl.pallas_call(
        body, name="adamw_w_ada", grid=(rows // tr,),
        in_specs=[spec, spec, spec, pl.BlockSpec((tr, N_DEV), lambda i: (i, 0)), pl.BlockSpec(dm.shape, lambda i: (0, 0))],
        out_specs=[spec] * 4, out_shape=[jax.ShapeDtypeStruct((rows, cols), F32)] * 4,
        compiler_params=_params(("parallel",)),
    )(w, m, v, sc_t, dm)


def _adamw(w, m, v, parts, name, own=None, slot=None):
    rows, cols = w.shape
    n = parts.shape[0]
    by_cols = rows % 256 != 0 and cols % 256 == 0
    tr, tc = (rows, 256) if by_cols else ((rows if rows <= 512 else 256), cols)
    tile = (lambda i: (0, i)) if by_cols else (lambda i: (i, 0))

    def body(*refs):
        if own is not None:
            s_ref, refs = refs[0], refs[1:]
            w_ref, m_ref, v_ref, p_ref, o_ref, g_out, d_out, m_out, v_out = refs
            terms = [jnp.where(s_ref[0] == kk, o_ref[0], p_ref[kk]) for kk in range(n)]
        else:
            w_ref, m_ref, v_ref, p_ref, g_out, d_out, m_out, v_out = refs
            terms = [p_ref[kk] for kk in range(n)]
        g = terms[0].astype(F32)
        for term in terms[1:]:
            g = g + term.astype(F32)
        g_out[...] = g
        d_out[...], m_out[...], v_out[...] = _adamw_math(w_ref[...], g, m_ref[...], v_ref[...])

    spec = pl.BlockSpec((tr, tc), lambda i, *_: tile(i))
    in_specs = [spec, spec, spec, pl.BlockSpec((n, tr, tc), lambda i, *_: (0,) + tile(i))]
    out_shape = [jax.ShapeDtypeStruct((rows, cols), F32)] * 4
    grid = (rows // tr if not by_cols else cols // tc,)
    if own is None:
        return pl.pallas_call(
            body, name=name, grid=grid, in_specs=in_specs, out_specs=[spec] * 4, out_shape=out_shape,
            compiler_params=_params(("parallel",)),
        )(w, m, v, parts)
    in_specs.append(pl.BlockSpec((1, tr, tc), lambda i, s: (s[0],) + tile(i)))
    return pl.pallas_call(
        body, name=name, out_shape=out_shape, compiler_params=_params(("parallel",)),
        grid_spec=pltpu.PrefetchScalarGridSpec(num_scalar_prefetch=1, grid=grid, in_specs=in_specs,
                                               out_specs=[spec] * 4),
    )(slot, w, m, v, parts, own)


def _adamw_math(w, g, m, v):
    mm = ADAM_B1 * m + (1.0 - ADAM_B1) * g
    vv = ADAM_B2 * v + (1.0 - ADAM_B2) * (g * g)
    m_hat = mm / (1.0 - ADAM_B1 ** ADAM_STEP)
    v_hat = vv / (1.0 - ADAM_B2 ** ADAM_STEP)
    return -ADAM_LR * (m_hat / (jnp.sqrt(v_hat) + ADAM_EPS) + ADAM_WD * w), mm, vv


def _adamw_rows(bundles, offsets, ws, ms, vs, err_off, err_width):
    k = len(ws)

    def body(*refs):
        b_ref = refs[0]
        w_refs, m_refs, v_refs = refs[1:1 + k], refs[1 + k:1 + 2 * k], refs[1 + 2 * k:1 + 3 * k]
        outs = refs[1 + 3 * k:]
        g_all = b_ref[0]
        for kk in range(1, N_DEV):
            g_all = g_all + b_ref[kk]
        for i in range(k):
            width = w_refs[i].shape[1]
            g = g_all[:, offsets[i]:offsets[i] + width]
            outs[4 * i][...] = g
            outs[4 * i + 1][...], outs[4 * i + 2][...], outs[4 * i + 3][...] = _adamw_math(
                w_refs[i][...], g, m_refs[i][...], v_refs[i][...])
        outs[4 * k][...] = g_all[:, err_off:err_off + err_width]

    out_shape = []
    for w_ in ws:
        out_shape += [jax.ShapeDtypeStruct(w_.shape, F32)] * 4
    out_shape.append(jax.ShapeDtypeStruct((1, err_width), F32))
    res = pl.pallas_call(body, name="adamw_rows", out_shape=out_shape, compiler_params=_params())(bundles, *ws, *ms, *vs)
    return [tuple(res[4 * i:4 * i + 4]) for i in range(k)], res[-1]


def _coords():
    return lax.axis_index("x"), lax.axis_index("y"), lax.axis_index("c")


def _flat(px, py, pc):
    return 4 * px + 2 * py + pc


def _all_gather(arrs, name):
    n = len(arrs)

    def body(*refs):
        ins, outs = refs[:n], refs[n:2 * n]
        send, recv, lsem = refs[2 * n:]
        x, y, c = _coords()
        me, sibling = (x, y, c), (x, y, 1 - c)
        chips = [(1 - x, y), (x, 1 - y), (1 - x, 1 - y)]

        def copy(a, kk, block, to, src=None):
            slot = outs[a].at[_flat(*block)]
            return pltpu.make_async_remote_copy(
                src_ref=slot if src is None else src, dst_ref=slot,
                send_sem=send.at[a, kk], recv_sem=recv.at[a, kk],
                device_id=to, device_id_type=MESH)

        mine = [pltpu.make_async_copy(ins[a], outs[a].at[_flat(*me)], lsem.at[a]) for a in range(n)]
        for cp in mine:
            cp.start()
        first = []
        for a in range(n):
            first.append(copy(a, 0, me, sibling, src=ins[a]))
            first += [copy(a, 1 + j, me, (*chip, c), src=ins[a]) for j, chip in enumerate(chips)]
        for cp in first:
            cp.start()
        passed = []
        for j, chip in enumerate(chips):
            for a in range(n):
                copy(a, 1 + j, (*chip, c), me).wait_recv()
                cp = copy(a, 4 + j, (*chip, c), sibling)
                cp.start()
                passed.append(cp)
        for a in range(n):
            copy(a, 0, sibling, me).wait_recv()
        for j, chip in enumerate(chips):
            for a in range(n):
                copy(a, 4 + j, (*chip, 1 - c), me).wait_recv()
        for cp in first + passed:
            cp.wait_send()
        for cp in mine:
            cp.wait()

    any_spec = pl.BlockSpec(memory_space=pl.ANY)
    return pl.pallas_call(
        body, name=name,
        in_specs=[any_spec] * n, out_specs=[any_spec] * n,
        out_shape=[jax.ShapeDtypeStruct((N_DEV,) + a.shape, a.dtype) for a in arrs],
        scratch_shapes=[pltpu.SemaphoreType.DMA((n, 7)), pltpu.SemaphoreType.DMA((n, 7)),
                        pltpu.SemaphoreType.DMA((n,))],
    )(*arrs)


def _peer_list():
    x, y, c = _coords()
    return [((1 - x if r & 4 else x), (1 - y if r & 2 else y), (1 - c if r & 1 else c)) for r in range(1, N_DEV)]


def _copy_plan(mode, src, land):
    x, y, c = _coords()
    me = _flat(x, y, c)
    if mode == "gather":
        return [(src, land.at[me], peer) for peer in _peer_list()]
    if mode == "exchange":
        return [(src.at[_flat(*peer)], land.at[me], peer) for peer in _peer_list()]
    if mode == "pair":
        return [(src.at[_flat(q // 2, q % 2, 1 - c)], land.at[q], (x, y, 1 - c)) for q in range(N_DEV // 2)]
    chips = [((1 - x if r & 2 else x), (1 - y if r & 1 else y)) for r in range(1, N_DEV // 2)]
    if mode == "chips":
        return [(src.at[2 * qx + qy], land.at[2 * x + y], (qx, qy, c)) for qx, qy in chips]
    if mode == "spread":
        return [(src, land.at[me], (x, y, 1 - c))] + [(src, land.at[me], (qx, qy, c)) for qx, qy in chips]
    assert mode == "forward"
    return [(land.at[_flat(qx, qy, c)], land.at[_flat(qx, qy, c)], (x, y, 1 - c)) for qx, qy in chips]


N_COPIES = dict(gather=N_DEV - 1, exchange=N_DEV - 1, pair=N_DEV // 2, chips=N_DEV // 2 - 1, spread=N_DEV // 2,
                forward=N_DEV // 2 - 1)


def _land_shape(mode, shape):
    return {"gather": (N_DEV,) + shape, "spread": (N_DEV,) + shape, "exchange": shape,
            "pair": (N_DEV // 2,) + shape[1:], "chips": shape}[mode]


HBM_SPEC = pl.BlockSpec(memory_space=pltpu.HBM)
SEM_SPEC = pl.BlockSpec(memory_space=pltpu.SEMAPHORE)
ANY_SPEC = pl.BlockSpec(memory_space=pl.ANY)
SIDE_EFFECT = pltpu.SideEffectType.DATAFLOW_SIDE_EFFECTING


def _async_start(groups, modes, after, name):
    modes = [modes] * len(groups) if isinstance(modes, str) else list(modes)
    arrs = [(a, m) for g, m in zip(groups, modes) for a in g]
    n = len(arrs)
    fresh = [i for i, (_, m) in enumerate(arrs) if m != "forward"]

    def body(*refs):
        srcs, new_lands = refs[:n], refs[n:n + len(fresh)]
        outs = refs[n + len(fresh) + 1:]
        lands = list(srcs)
        for k, i in enumerate(fresh):
            lands[i] = new_lands[k]
        for ai, (_, mode) in enumerate(arrs):
            for src_ref, dst_ref, peer in _copy_plan(mode, srcs[ai], lands[ai]):
                pltpu.make_async_remote_copy(src_ref=src_ref, dst_ref=dst_ref, send_sem=outs[2 * ai],
                                             recv_sem=outs[2 * ai + 1], device_id=peer, device_id_type=MESH).start()
        outs[-1][...] = jnp.zeros(outs[-1].shape, F32)

    land_shapes = [(_land_shape(arrs[i][1], arrs[i][0].shape), arrs[i][0].dtype) for i in fresh]
    n_buf = n + len(fresh)
    out_shape = [pltpu.SemaphoreType.DMA(())] * (2 * n)
    out_shape += [pltpu.HBM(a.shape, a.dtype) for a, _ in arrs]
    out_shape += [pltpu.HBM(shape, dt) for shape, dt in land_shapes]
    out_shape.append(jax.ShapeDtypeStruct((8, LANES), F32))
    res = pl.pallas_call(
        body, name=name, out_shape=tuple(out_shape),
        in_specs=[HBM_SPEC] * n_buf + [ANY_SPEC],
        out_specs=tuple([SEM_SPEC] * (2 * n) + [HBM_SPEC] * n_buf + [pl.BlockSpec(memory_space=pltpu.VMEM)]),
        input_output_aliases={i: 2 * n + i for i in range(n_buf)},
        compiler_params=pltpu.CompilerParams(has_side_effects=SIDE_EFFECT),
    )(*[pltpu.with_memory_space_constraint(a, pltpu.HBM) for a, _ in arrs],
      *[pltpu.with_memory_space_constraint(lax.empty(shape, dt), pltpu.HBM) for shape, dt in land_shapes],
      after)
    sems, thru = res[:2 * n], res[2 * n:-1]
    land_of = {i: thru[n + k] for k, i in enumerate(fresh)}
    states, idx = [], 0
    for g, mode in zip(groups, modes):
        ids = range(idx, idx + len(g))
        idx += len(g)
        states.append(([sems[2 * i] for i in ids], [sems[2 * i + 1] for i in ids],
                       None if mode == "forward" else [thru[i] for i in ids],
                       [land_of.get(i, thru[i]) for i in ids], mode))
    return states, res[-1]


def _async_wait(state, after, name):
    sends, recvs, srcs, lands, mode = state
    g = len(lands)
    bufs = (list(srcs) if srcs is not None else []) + list(lands)
    nb = len(bufs)

    def body(*refs):
        l_refs, sems = refs[nb - g:nb], refs[nb:nb + 2 * g]
        for ai in range(g):
            moved = l_refs[ai].at[pl.ds(0, N_COPIES[mode])]
            cp = pltpu.make_async_remote_copy(src_ref=moved, dst_ref=moved, send_sem=sems[ai], recv_sem=sems[g + ai],
                                              device_id=_coords(), device_id_type=MESH)
            cp.wait_send()
            cp.wait_recv()

    res = pl.pallas_call(
        body, name=name,
        out_shape=tuple(pltpu.HBM(a.shape, a.dtype) for a in bufs),
        in_specs=[HBM_SPEC] * nb + [SEM_SPEC] * (2 * g) + [ANY_SPEC],
        out_specs=tuple([HBM_SPEC] * nb),
        input_output_aliases={i: i for i in range(nb)},
        compiler_params=pltpu.CompilerParams(has_side_effects=SIDE_EFFECT),
    )(*bufs, *sends, *recvs, after)
    return (list(res[:nb - g]) if srcs is not None else None), list(res[nb - g:])


def _add_sibling(mine, theirs, core):
    def body(c_ref, a_ref, b_ref, o_ref):
        o_ref[...] = (a_ref[...].astype(F32) + b_ref[...].astype(F32)).astype(o_ref.dtype)

    blk = (1,) + mine.shape[1:]
    return pl.pallas_call(
        body, name="add_sibling", out_shape=jax.ShapeDtypeStruct(theirs.shape, mine.dtype),
        grid_spec=pltpu.PrefetchScalarGridSpec(
            num_scalar_prefetch=1, grid=(theirs.shape[0],),
            in_specs=[pl.BlockSpec(blk, lambda q, c: (2 * q + c[0], 0, 0)), pl.BlockSpec(blk, lambda q, c: (q, 0, 0))],
            out_specs=pl.BlockSpec(blk, lambda q, c: (q, 0, 0))),
        compiler_params=_params(("parallel",)),
    )(core, mine, theirs)


def _with_own(land, own, me):
    return lax.dynamic_update_index_in_dim(land, own, me, 0)


IN_SPLITS = (512, 512, 512, 8, 768, 256, 32, 1024, 1024)


def _from_shards(g, fn, out_widths, name, own=None, slot=None):
    _, k, n = g.shape
    tr = min(k, 256)

    def body(*refs):
        if own is not None:
            s_ref, g_ref, own_ref = refs[:3]
            cols = [jnp.where(s_ref[0] == j, own_ref[...], g_ref[j]) for j in range(N_DEV)]
        else:
            g_ref = refs[0]
            cols = [g_ref[j] for j in range(N_DEV)]
        for o_ref, val in zip(refs[-len(out_widths):], fn(jnp.concatenate(cols, axis=1))):
            o_ref[...] = val

    in_specs = [pl.BlockSpec((N_DEV, tr, n), lambda i, *_: (0, i, 0))]
    out_spec = [pl.BlockSpec((tr, wd), lambda i, *_: (i, 0)) for wd in out_widths]
    out_shape = [jax.ShapeDtypeStruct((k, wd), g.dtype) for wd in out_widths]
    if own is None:
        return pl.pallas_call(body, name=name, grid=(k // tr,), in_specs=in_specs, out_specs=out_spec,
                              out_shape=out_shape, compiler_params=_params(("parallel",)))(g)
    in_specs.append(pl.BlockSpec((tr, n), lambda i, *_: (i, 0)))
    return pl.pallas_call(
        body, name=name, out_shape=out_shape, compiler_params=_params(("parallel",)),
        grid_spec=pltpu.PrefetchScalarGridSpec(num_scalar_prefetch=1, grid=(k // tr,), in_specs=in_specs, out_specs=out_spec),
    )(slot, g, own)


def _unshard_cols(g, own=None, slot=None):
    return _from_shards(g, lambda full: (full,), [N_DEV * g.shape[2]], "unshard_cols_%d" % g.shape[2], own, slot)[0]


FFN_T = D_FF // 2
FFN_SHARD = 2 * D_FF // N_DEV


def _unshard_ffn_in(g, own=None, slot=None):
    def pairs(full):
        parts = []
        for j in range(D_FF // FFN_T):
            parts += [full[:, j * FFN_T:(j + 1) * FFN_T], full[:, D_FF + j * FFN_T:D_FF + (j + 1) * FFN_T]]
        return (jnp.concatenate(parts, axis=1),)

    return _from_shards(g, pairs, [2 * D_FF], "unshard_ffn_in", own, slot)[0]


def _shard_ffn_in_t(wt):
    tc = 256

    def body(w_ref, o_ref):
        x = w_ref[...]
        nb = D_FF // FFN_T
        full = jnp.concatenate([x[(2 * j + half) * FFN_T:(2 * j + half + 1) * FFN_T]
                                for half in range(2) for j in range(nb)], axis=0)
        for j in range(N_DEV):
            o_ref[j] = full[j * FFN_SHARD:(j + 1) * FFN_SHARD]

    return pl.pallas_call(
        body, name="shard_ffn_in_t", grid=(D // tc,),
        in_specs=[pl.BlockSpec((2 * D_FF, tc), lambda i: (0, i))],
        out_specs=pl.BlockSpec((N_DEV, FFN_SHARD, tc), lambda i: (0, 0, i)),
        out_shape=jax.ShapeDtypeStruct((N_DEV, FFN_SHARD, D), wt.dtype),
        compiler_params=_params(("parallel",)),
    )(wt)


def _shard_cols(w):
    k, n = w.shape[0], w.shape[1] // N_DEV
    tr = min(k, 256)

    def body(w_ref, o_ref):
        full = w_ref[...]
        for j in range(N_DEV):
            o_ref[j] = full[:, j * n:(j + 1) * n]

    return pl.pallas_call(
        body, name="shard_cols_%d" % n, grid=(k // tr,),
        in_specs=[pl.BlockSpec((tr, N_DEV * n), lambda i: (i, 0))],
        out_specs=pl.BlockSpec((N_DEV, tr, n), lambda i: (0, i, 0)),
        out_shape=jax.ShapeDtypeStruct((N_DEV, k, n), w.dtype),
        compiler_params=_params(("parallel",)),
    )(w)


IN_OFFS = tuple(sum(IN_SPLITS[:i]) for i in range(len(IN_SPLITS) + 1))
IN_SHARD = IN_OFFS[-1] // N_DEV
REGROUP_ROWS = 256
MISC_AT = Q_LORA + KV_LORA + 2 * D
A_COLS = MISC_AT + LANES
A_TILE = A_COLS
B_COLS = 3 * HEADS * HEAD_DIM
MISC_BLOCK = MISC_AT // LANES
KR_AT = 64


def _w_in_regroup(g, own=None, slot=None):
    def groups(full):
        fq, fk, fv, wf, cq, ckv, kr, gf, gm = [full[:, IN_OFFS[i]:IN_OFFS[i + 1]] for i in range(9)]
        rows = full.shape[0]
        gap = jnp.zeros((rows, KR_AT - HEADS), BF16)
        tail = jnp.zeros((rows, LANES - KR_AT - ROPE_DIM), BF16)
        return jnp.concatenate([cq, ckv, gf, gm, wf, gap, kr, tail], axis=1), jnp.concatenate([fq, fk, fv], axis=1)

    return _from_shards(g, groups, [A_COLS, B_COLS], "w_in_regroup", own, slot)


def _w_in_ungroup(da, db_):
    def body(a_ref, b_ref, o_ref):
        a = a_ref[...]
        lora = Q_LORA + KV_LORA
        full = jnp.concatenate([b_ref[...], a[:, MISC_AT:MISC_AT + HEADS], a[:, :lora],
                                a[:, MISC_AT + KR_AT:MISC_AT + KR_AT + ROPE_DIM], a[:, lora:MISC_AT]], axis=1)
        for j in range(N_DEV):
            o_ref[j] = full[:, j * IN_SHARD:(j + 1) * IN_SHARD]

    tr = REGROUP_ROWS
    return pl.pallas_call(
        body, name="w_in_ungroup", grid=(D // tr,),
        in_specs=[pl.BlockSpec((tr, A_COLS), lambda i: (i, 0)), pl.BlockSpec((tr, B_COLS), lambda i: (i, 0))],
        out_specs=pl.BlockSpec((N_DEV, tr, IN_SHARD), lambda i: (0, i, 0)),
        out_shape=jax.ShapeDtypeStruct((N_DEV, D, IN_SHARD), BF16),
        compiler_params=_params(("parallel",)),
    )(da, db_)


def _prepare_weights(g, own=None, slot=None):
    w = {}
    if own is not None:
        small = ("w_uq", "w_ukv", "w_out", "w_ffn_out")
        g = {n: (_with_own(a, own[n], slot[0]) if n in small else a) for n, a in g.items()}
    pick = (lambda n: (own[n], slot)) if own is not None else (lambda n: (None, None))
    if "w_in" in g:
        w["w_a"], w["w_b"] = _w_in_regroup(g["w_in"], *pick("w_in"))
    if "w_uq" in g:
        w_uq = g["w_uq"].reshape(Q_LORA, HEADS, 96)
        w["w_uq"] = jnp.pad(w_uq, ((0, 0), (0, 0), (0, 32))).reshape(Q_LORA, HEADS * LANES)
        ukv = g["w_ukv"]
        w["w_k"] = jnp.transpose(jnp.pad(ukv[:, :, :64], ((0, 0), (0, 0), (0, 64))), (1, 0, 2)).reshape(KV_LORA, HEADS * LANES)
        w["w_v"] = jnp.transpose(ukv[:, :, 64:], (1, 0, 2)).reshape(KV_LORA, HEADS * HEAD_DIM)
    if "w_out" in g:
        w["w_pf"] = _unshard_cols(g["w_proj_fox"], *pick("w_proj_fox"))
        w["w_pm"] = _unshard_cols(g["w_proj_mla"], *pick("w_proj_mla"))
        w["w_out"] = g["w_out"].reshape(D, D)
    if "w_ffn_in" in g:
        w["w_ffn_in"] = _unshard_ffn_in(g["w_ffn_in"], *pick("w_ffn_in"))
        w["w_ffn_out"] = g["w_ffn_out"].reshape(D_FF, D)
    return w


def _shard_grads(dw):
    out = {}
    if "w_a" in dw:
        out["w_in"] = _w_in_ungroup(dw["w_a"], dw["w_b"])
    if "w_uq" in dw:
        w_uq = dw["w_uq"].reshape(Q_LORA, HEADS, LANES)[:, :, :96].reshape(Q_LORA, Q_LORA)
        out["w_uq"] = w_uq.reshape(N_DEV, Q_LORA // N_DEV, Q_LORA)
        k_part = dw["w_k"].reshape(KV_LORA, HEADS, LANES)[:, :, :64]
        v_part = dw["w_v"].reshape(KV_LORA, HEADS, HEAD_DIM)
        out["w_ukv"] = jnp.transpose(jnp.concatenate([k_part, v_part], axis=2), (1, 0, 2))
    if "w_out" in dw:
        out["w_proj_fox"] = _shard_cols(dw["w_pf"])
        out["w_proj_mla"] = _shard_cols(dw["w_pm"])
        out["w_out"] = dw["w_out"].reshape(N_DEV, D // N_DEV, D)
    if "w_ffn_in" in dw:
        out["w_ffn_in"] = _shard_ffn_in_t(dw["w_ffn_in"])
        out["w_ffn_out"] = dw["w_ffn_out"].reshape(N_DEV, D_FF // N_DEV, D)
    return out


def _fwd_bwd(x, pos, mod, target, w, vec, wts, send, relay):
    shift_mix, scale_mix, gate_mix, shift_ffn, scale_ffn, gate_ffn = [mod[:, i * D:(i + 1) * D] for i in range(6)]
    g_pre_mix, g_post_mix, g_pre_ffn, g_post_ffn = vec["g_pre_mix"], vec["g_post_mix"], vec["g_pre_ffn"], vec["g_post_ffn"]
    g_q, g_kv = vec["g_q_lora"], vec["g_kv_lora"]

    inv_freq = 1.0 / (ROPE_THETA ** (jnp.arange(0, ROPE_DIM, 2, dtype=F32) / ROPE_DIM))
    invf = jnp.concatenate([jnp.zeros((64,), F32), inv_freq, inv_freq, jnp.zeros((32,), F32)]).reshape(1, LANES)
    ct, sa, sb = _rope_tables(pos, invf)

    def pre1(xv, g, sc, sh):
        return (xv * _rstd(xv) * g) * (1.0 + sc) + sh
    proj_a, h = _mm_epi(x, w["w_a"], "nn", A_TILE, lambda r: ((r,), ()), "in_proj_a", 512, outs=[(A_COLS, A_TILE, F32)],
                        pro=(pre1, [g_pre_mix, scale_mix, shift_mix], 0))
    qkv = _mm(h, w["w_b"], "nn", BF16, "in_proj_b")

    def lora_norm(cv, g):
        return cv * _rstd(cv) * g
    w = {**w, **wts("lora", qkv)}
    tables = [(ct, LANES), (sa, LANES), (sb, LANES)]

    def rope_q(qv, c_, a_, b_):
        return (jnp.concatenate([_rope(qv[:, hd * LANES:(hd + 1) * LANES], c_, a_, b_) for hd in range(HEADS)], axis=1),), ()
    q_m, cqn = _mm_epi(proj_a, w["w_uq"], "nn", D, rope_q, "mla_uq", 512, rows=tables, outs=[(D, D, BF16)],
                       pro=(lora_norm, [g_q], 0))

    def rope_k(kv, misc, c_, a_, b_):
        lane = lax.broadcasted_iota(jnp.int32, (1, LANES), 1)
        kpe = jnp.where((lane >= 64) & (lane < 96), _rope(misc, c_, a_, b_), 0.0)
        return (jnp.concatenate([kv[:, hd * LANES:(hd + 1) * LANES] + kpe for hd in range(HEADS)], axis=1),), ()
    k_m, ckvn = _mm_epi(proj_a, w["w_k"], "nn", D, rope_k, "mla_uk", 512, rows=[(proj_a, LANES, MISC_BLOCK)] + tables,
                        outs=[(D, D, BF16)], pro=(lora_norm, [g_kv], Q_LORA // KV_LORA))
    v_m = _mm(ckvn, w["w_v"], "nn", BF16, "mla_uv")

    bf = jnp.transpose(vec["b_forget"])
    zt, neg_f = _fox_gates(proj_a, MISC_BLOCK, bf)
    bias = neg_f.reshape(HEADS, N_ATT, 1, ATT_T)
    o_b, lse_b = _attn_fwd(q_m, 0, k_m, 0, v_m, 0, 2 * LANES, 1.0 / math.sqrt(64 + ROPE_DIM), None, "mla_attn")
    bias = bias + wts("relay_proj", o_b)["tok"][0, 0]
    o_a, lse_a = _attn_fwd(qkv, 0, qkv, 4, qkv, 8, LANES, 1.0 / math.sqrt(HEAD_DIM), bias, "fox_attn")

    w = {**w, **wts("proj", o_a)}
    gate_mix = gate_mix + wts("relay_ffn", o_a)["tok"][0, 0]
    pa = _mm(o_a, w["w_pf"], "nn", BF16, "proj_fox")

    def merge(pb_, gf, gm, pa_):
        return (_sigmoid(gf) * pa_ + _sigmoid(gm) * pb_, pb_), ()
    merged, pb = _mm_epi(o_b, w["w_pm"], "nn", 512, merge, "proj_mla", 1024,
                         rows=[(proj_a, 512, 2), (proj_a, 512, 4), (pa, 512)], outs=[(D, 512, BF16), (D, 512, BF16)])
    def post1(yv, xv, gate, gpost, gpre, sc, sh):
        x1 = xv + gate * (yv * _rstd(yv) * gpost)
        return (x1, (x1 * _rstd(x1) * gpre) * (1.0 + sc) + sh, yv), ()
    x1, h2, y = _mm_epi(merged, w["w_out"], "nn", D, post1, "out_proj", 512, rows=[(x, D)],
                        vecs=[gate_mix, g_post_mix, g_pre_ffn, scale_ffn, shift_ffn],
                        outs=[(D, D, F32), (D, D, BF16), (D, D, F32)])
    w = {**w, **wts("ffn", h2)}

    def swiglu(r):
        g, u = r[:, :FFN_T], r[:, FFN_T:]
        return (g * _sigmoid(g) * u, r), ()
    act, gu = _mm_epi(h2, w["w_ffn_in"], "nn", 2 * FFN_T, swiglu, "ffn_in", 512,
                      outs=[(D_FF, FFN_T, BF16), (2 * D_FF, 2 * FFN_T, BF16)])

    def head(y2v, x1v, tv, gate, gpost):
        r = _rstd(y2v)
        yn = y2v * r
        n2 = yn * gpost
        err = (x1v + gate * n2) - tv
        dx2 = err * (1.0 / D)
        dn2 = dx2 * gate
        dy2 = _norm_bwd(dn2 * gpost, yn, r)
        return (dx2, dy2), (_colsum(err * err), _colsum(dx2 * n2), _colsum(dn2 * yn))
    dx2, dy2, err_cols, d_gate_ffn, d_g_post_ffn = _mm_epi(
        act, w["w_ffn_out"], "nn", D, head, "ffn_out", 512, rows=[(x1, D), (target, D)], vecs=[gate_ffn, g_post_ffn],
        outs=[(D, D, F32), (D, D, BF16)], sums=[D, D, D])

    def swiglu_bwd(da, guv):
        g, u = guv[:, :FFN_T].astype(F32), guv[:, FFN_T:].astype(F32)
        sg = _sigmoid(g)
        return (jnp.concatenate([da * u * (sg * (1.0 + g * (1.0 - sg))), da * (g * sg)], axis=1),), ()
    (dgu,) = _mm_epi(dy2, w["w_ffn_out"], "nt", FFN_T, swiglu_bwd, "ffn_out_dx", 512, rows=[(gu, 2 * FFN_T)],
                     outs=[(2 * D_FF, 2 * FFN_T, BF16)])
    dw = {"w_ffn_out": _mm(act, dy2, "tn", BF16, "ffn_out_dw")}
    dw["w_ffn_in"] = _mm(dgu, h2, "tn", BF16, "ffn_in_dw")
    gate_mix = gate_mix + send({n: dw.pop(n) for n in ("w_ffn_in", "w_ffn_out")})[0, 0]

    def mid(dh, x1v, dx2v, yv, gpre, sc, gate, gpost):
        r2 = _rstd(x1v)
        x1n = x1v * r2
        t = dh * x1n
        dx1 = dx2v + _norm_bwd(dh * (gpre * (1.0 + sc)), x1n, r2)
        ry = _rstd(yv)
        yn = yv * ry
        dn1 = dx1 * gate
        dy = _norm_bwd(dn1 * gpost, yn, ry)
        sums = (_colsum(dh), _colsum(t) * gpre, _colsum(t) * (1.0 + sc), _colsum(dx1 * (yn * gpost)), _colsum(dn1 * yn))
        return (dx1, dy), sums
    dx1, dy, d_shift_ffn, d_scale_ffn, d_g_pre_ffn, d_gate_mix, d_g_post_mix = _mm_epi(
        dgu, w["w_ffn_in"], "nt", D, mid, "ffn_in_dx", 512, rows=[(x1, D), (dx2, D), (y, D)],
        vecs=[g_pre_ffn, scale_ffn, gate_mix, g_post_mix], outs=[(D, D, F32), (D, D, BF16)], sums=[D] * 5)

    dw["w_out"] = _mm(merged, dy, "tn", BF16, "out_proj_dw")

    def merge_bwd(dm, gf, gm, pa_, pb_):
        sf, sm = _sigmoid(gf), _sigmoid(gm)
        return (dm * sf, dm * sm, dm * pa_ * (sf * (1.0 - sf)), dm * pb_ * (sm * (1.0 - sm))), ()
    dpa, dpb, dgf, dgm = _mm_epi(dy, w["w_out"], "nt", 512, merge_bwd, "out_proj_dx", 1024,
                                 rows=[(proj_a, 512, 2), (proj_a, 512, 4), (pa, 512), (pb, 512)],
                                 outs=[(D, 512, BF16)] * 4)
    do_a = _mm(dpa, w["w_pf"], "nt", BF16, "proj_fox_dx")
    do_b = _mm(dpb, w["w_pm"], "nt", BF16, "proj_mla_dx")
    dw["w_pf"] = _mm(o_a, dpa, "tn", BF16, "proj_fox_dw")
    dw["w_pm"] = _mm(o_b, dpb, "tn", BF16, "proj_mla_dw")
    bias = bias + send({n: dw.pop(n) for n in ("w_out", "w_pf", "w_pm")})[0, 0]

    sc_a, sc_b = 1.0 / math.sqrt(HEAD_DIM), 1.0 / math.sqrt(64 + ROPE_DIM)
    dq_a, dk_a, dv_a, dbias = _attn_grad(qkv, 0, qkv, 4, qkv, 8, do_a, lse_a, LANES, sc_a, bias, BF16, "fox_attn_bwd")
    dq_m, dk_m, dv_m = _attn_grad(q_m, 0, k_m, 0, v_m, 0, do_b, lse_b, 2 * LANES, sc_b, None, F32, "mla_attn_bwd")

    def mla_rope_bwd(dq, dk, c_, a_, b_):
        lane = lax.broadcasted_iota(jnp.int32, (1, LANES), 1)
        dqs = [_rope_t(dq[:, hd * LANES:(hd + 1) * LANES], c_, a_, b_) for hd in range(HEADS)]
        dkpe = dk[:, 0:LANES]
        for hd in range(1, HEADS):
            dkpe = dkpe + dk[:, hd * LANES:(hd + 1) * LANES]
        dkpe = jnp.where((lane >= 64) & (lane < 96), dkpe, 0.0)
        dkr = jnp.where((lane >= 64) & (lane < 96), _rope_t(dkpe, c_, a_, b_), 0.0)
        return (jnp.concatenate(dqs, axis=1), dk, dkr), ()
    dqb, dkb, dkr = _rowwise(mla_rope_bwd, [(dq_m, D, 0), (dk_m, D, 0), (ct, LANES, 0), (sa, LANES, 0), (sb, LANES, 0)],
                             [], [(D, BF16), (D, BF16), (LANES, F32)], [], "mla_rope_bwd")
    def lora_q_bwd(dq, cq, gq):
        rq = _rstd(cq)
        cqh = cq * rq
        return (_norm_bwd(dq * gq, cqh, rq),), (_colsum(dq * cqh),)
    dcq, d_g_q = _mm_epi(dqb, w["w_uq"], "nt", Q_LORA, lora_q_bwd, "mla_uq_dx", 512, rows=[(proj_a, Q_LORA, 0)],
                         vecs=[g_q], outs=[(Q_LORA, Q_LORA, BF16)], sums=[Q_LORA])

    def lora_kv_bwd(dv_part, dk_part, ckv, gkv):
        dkv = dv_part + dk_part
        rk = _rstd(ckv)
        ckh = ckv * rk
        return (_norm_bwd(dkv * gkv, ckh, rk),), (_colsum(dkv * ckh),)
    dckv, d_g_kv = _mm_epi(dv_m, w["w_v"], "nt", KV_LORA, lora_kv_bwd, "mla_uv_dx", 1024,
                           rows=[(_mm(dkb, w["w_k"], "nt", F32, "mla_uk_dx"), KV_LORA), (proj_a, KV_LORA, 3)],
                           vecs=[g_kv], outs=[(KV_LORA, KV_LORA, BF16)], sums=[KV_LORA])

    dzt, d_bf = _fox_gates_bwd(dbias.reshape(HEADS, S), zt, bf)
    dmisc = (dkr + jnp.pad(jnp.transpose(dzt), ((0, 0), (0, LANES - HEADS)))).astype(BF16)
    dproj_a = jnp.concatenate([dcq, dckv, dgf, dgm, dmisc], axis=1)
    dqkv = jnp.concatenate([dq_a, dk_a, dv_a], axis=1)
    dw["w_a"] = _mm(h, dproj_a, "tn", BF16, "in_proj_a_dw")
    dw["w_b"] = _mm(h, dqkv, "tn", BF16, "in_proj_b_dw")
    tok = send(dw, True)
    dh_a = _mm(dproj_a, w["w_a"], "nt", F32, "in_proj_a_dx", dep=tok)
    tok = relay(dh_a)
    tok = send({"w_uq": _mm(cqn, dqb, "tn", BF16, "mla_uq_dw", dep=tok),
                "w_k": _mm(ckvn, dkb, "tn", BF16, "mla_uk_dw", dep=tok),
                "w_v": _mm(ckvn, dv_m, "tn", BF16, "mla_uv_dw", dep=tok)}, late=True)
    g_pre_mix = g_pre_mix + tok[0, 0]

    def first(dh_b, dh_a, xv, dx1v, gpre, sc):
        dhv = dh_b + dh_a
        r = _rstd(xv)
        xn = xv * r
        t = dhv * xn
        dx = dx1v + _norm_bwd(dhv * (gpre * (1.0 + sc)), xn, r)
        return (dx,), (_colsum(dhv), _colsum(t) * gpre, _colsum(t) * (1.0 + sc))
    grad_x, d_shift_mix, d_scale_mix, d_g_pre_mix = _mm_epi(
        dqkv, w["w_b"], "nt", D, first, "in_proj_b_dx", 512,
        rows=[(dh_a, D), (x, D), (dx1, D)],
        vecs=[g_pre_mix, scale_mix], outs=[(D, D, F32)], sums=[D] * 3)

    dmod = jnp.concatenate([d_shift_mix, d_scale_mix, d_gate_mix, d_shift_ffn, d_scale_ffn, d_gate_ffn], axis=1)
    small = dict(dmod=dmod, g_pre_mix=d_g_pre_mix, g_post_mix=d_g_post_mix, g_pre_ffn=d_g_pre_ffn,
                 g_post_ffn=d_g_post_ffn, g_q_lora=d_g_q, g_kv_lora=d_g_kv,
                 b_forget=jnp.pad(jnp.transpose(d_bf), ((0, 0), (0, LANES - HEADS))), err=err_cols)
    return grad_x, small


SMALL_ORDER = ("dmod", "g_pre_mix", "g_post_mix", "g_pre_ffn", "g_post_ffn", "g_q_lora", "g_kv_lora", "b_forget", "err")
SMALL_PARAM = {"dmod": "b_ada"}
MATRICES = ("w_in", "w_uq", "w_ukv", "w_proj_fox", "w_proj_mla", "w_out", "w_ffn_in", "w_ffn_out")
WEIGHTS = ("w_ada", "b_ada", "g_pre_mix", "g_post_mix", "g_pre_ffn", "g_post_ffn", "w_in", "b_forget", "g_q_lora",
           "w_uq", "g_kv_lora", "w_ukv", "w_proj_fox", "w_proj_mla", "w_out", "w_ffn_in", "w_ffn_out")


def kernel(x, c, positions, w_ada, b_ada, g_pre_mix, g_post_mix, g_pre_ffn, g_post_ffn, w_in, b_forget, g_q_lora, w_uq, g_kv_lora, w_ukv, w_proj_fox, w_proj_mla, w_out, w_ffn_in, w_ffn_out, loss_target, m_w_ada, m_b_ada, m_g_pre_mix, m_g_post_mix, m_g_pre_ffn, m_g_post_ffn, m_w_in, m_b_forget, m_g_q_lora, m_w_uq, m_g_kv_lora, m_w_ukv, m_w_proj_fox, m_w_proj_mla, m_w_out, m_w_ffn_in, m_w_ffn_out, v_w_ada, v_b_ada, v_g_pre_mix, v_g_post_mix, v_g_pre_ffn, v_g_post_ffn, v_w_in, v_b_forget, v_g_q_lora, v_w_uq, v_g_kv_lora, v_w_ukv, v_w_proj_fox, v_w_proj_mla, v_w_out, v_w_ffn_in, v_w_ffn_out):
    prm = dict(w_ada=w_ada, b_ada=b_ada, g_pre_mix=g_pre_mix, g_post_mix=g_post_mix, g_pre_ffn=g_pre_ffn,
               g_post_ffn=g_post_ffn, w_in=w_in, b_forget=b_forget, g_q_lora=g_q_lora, w_uq=w_uq, g_kv_lora=g_kv_lora,
               w_ukv=w_ukv, w_proj_fox=w_proj_fox, w_proj_mla=w_proj_mla, w_out=w_out, w_ffn_in=w_ffn_in, w_ffn_out=w_ffn_out)
    mom = dict(w_ada=m_w_ada, b_ada=m_b_ada, g_pre_mix=m_g_pre_mix, g_post_mix=m_g_post_mix, g_pre_ffn=m_g_pre_ffn,
               g_post_ffn=m_g_post_ffn, w_in=m_w_in, b_forget=m_b_forget, g_q_lora=m_g_q_lora, w_uq=m_w_uq,
               g_kv_lora=m_g_kv_lora, w_ukv=m_w_ukv, w_proj_fox=m_w_proj_fox, w_proj_mla=m_w_proj_mla, w_out=m_w_out,
               w_ffn_in=m_w_ffn_in, w_ffn_out=m_w_ffn_out)
    var = dict(w_ada=v_w_ada, b_ada=v_b_ada, g_pre_mix=v_g_pre_mix, g_post_mix=v_g_post_mix, g_pre_ffn=v_g_pre_ffn,
               g_post_ffn=v_g_post_ffn, w_in=v_w_in, b_forget=v_b_forget, g_q_lora=v_g_q_lora, w_uq=v_w_uq,
               g_kv_lora=v_g_kv_lora, w_ukv=v_w_ukv, w_proj_fox=v_w_proj_fox, w_proj_mla=v_w_proj_mla, w_out=v_w_out,
               w_ffn_in=v_w_ffn_in, w_ffn_out=v_w_ffn_out)
    me = _flat(*_coords())
    slot = jnp.reshape(me, (1,)).astype(jnp.int32)

    own = {n: prm[n][0].astype(BF16) for n in MATRICES}
    no_dep = jnp.zeros((8, LANES), F32)
    (st_c, st_in), tok = _async_start([[c], [own["w_in"]]], ["gather", "spread"], no_dep, "gather_in_start")
    (c_own,), (c_land,) = _async_wait(st_c, tok, "gather_c_wait")
    c_all = _with_own(c_land, c_own, me).reshape(N_DEV, D)
    ada_cols = w_ada.shape[2]
    b_cols = lax.dynamic_slice(b_ada, (0, me * ada_cols), (1, ada_cols))
    mod_cols, silu_c = _mod_part(c_all, w_ada[0], b_cols)
    (mod_all,) = _all_gather([mod_cols], "gather_mod")

    (w_in_own,), (w_in_land,) = _async_wait(st_in, mod_all, "gather_in_wait")
    (st_in,), tok = _async_start([[w_in_land]], "forward", no_dep, "gather_in_forward")
    _, (w_in_land,) = _async_wait(st_in, tok, "gather_in_forward_wait")
    w = _prepare_weights({"w_in": w_in_land}, {"w_in": w_in_own}, slot)
    later = dict(lora=("w_uq", "w_ukv"), proj=("w_proj_fox", "w_proj_mla", "w_out"), ffn=("w_ffn_in", "w_ffn_out"))
    states, tok = _async_start([[own[n] for n in names] for names in later.values()], ["gather", "spread", "spread"],
                               w["w_b"], "gather_rest_start")
    gather_state = dict(zip(later, states))
    own_thru = {}

    def wts(group, after):
        if group.startswith("relay_"):
            name = group[len("relay_"):]
            own_thru[name], lands = _async_wait(gather_state[name], after, "gather_" + name + "_wait")
            (gather_state[name],), t = _async_start([lands], "forward", no_dep, "gather_" + name + "_forward")
            return {"tok": t}
        srcs, lands = _async_wait(gather_state[group], after, "gather_" + group + "_landed")
        srcs = own_thru.get(group, srcs)
        return _prepare_weights(dict(zip(later[group], lands)), dict(zip(later[group], srcs)), slot)

    sent, late_sent, last = [], [], {}

    def send(grads, final=False, late=False):
        shards = _shard_grads(grads)
        names = list(shards)
        (state,), t = _async_start([[shards[n] for n in names]], "pair" if final else "exchange", no_dep,
                                   "exchange_" + names[0] + "_start")
        if final:
            last.update(names=names, state=state)
        else:
            (late_sent if late else sent).append((names, state))
        return t

    def relay(after):
        srcs, lands = _async_wait(last["state"], after, "exchange_pair_wait")
        core = jnp.reshape(lax.axis_index("c"), (1,)).astype(jnp.int32)
        sums = [_add_sibling(src, land, core) for src, land in zip(srcs, lands)]
        (last["state"],), t = _async_start([sums], "chips", no_dep, "exchange_chips_start")
        return t

    mod = lax.dynamic_index_in_dim(mod_all, me, axis=1, keepdims=False).reshape(1, 6 * D) + tok[0, 0]

    vec = dict(g_pre_mix=g_pre_mix, g_post_mix=g_post_mix, g_pre_ffn=g_pre_ffn, g_post_ffn=g_post_ffn,
               g_q_lora=g_q_lora, g_kv_lora=g_kv_lora, b_forget=b_forget)
    pos = positions.astype(F32).reshape(S, 1)
    grad_x, small = _fwd_bwd(x[0], pos, mod, loss_target[0], w, vec, wts, send, relay)

    bundle = jnp.concatenate([small[n] for n in SMALL_ORDER], axis=1)
    (small_state,), tok = _async_start([[bundle]], "gather", jnp.zeros((8, LANES), F32), "gather_small_start")

    out = {}
    swap = lambda a: jnp.swapaxes(a, -1, -2)

    def update(n, land, src, sl):
        if n != "w_ffn_in":
            out[n] = _adamw(prm[n][0], mom[n][0], var[n][0], land, "adamw_" + n, src, sl)
            return out[n][0]
        res = _adamw(swap(prm[n][0]), swap(mom[n][0]), swap(var[n][0]), land, "adamw_" + n, src, sl)
        out[n] = tuple(swap(t) for t in res)
        return res[0]

    after = tok
    for names, state in sent:
        srcs, lands = _async_wait(state, after, "exchange_" + names[0] + "_wait")
        for n, src, land in zip(names, srcs, lands):
            after = update(n, land, src, slot)
    srcs, lands = _async_wait(last["state"], after, "exchange_chips_wait")
    for n, src, land in zip(last["names"], srcs, lands):
        after = update(n, land, src, slot // 2)
    for names, state in late_sent:
        srcs, lands = _async_wait(state, after, "exchange_" + names[0] + "_wait")
        for n, src, land in zip(names, srcs, lands):
            after = update(n, land, src, slot)

    (own_bundle,), (bundle_all,) = _async_wait(small_state, after, "gather_small_wait")
    bundle_all = _with_own(bundle_all, own_bundle, me)
    dmod_all = bundle_all[:, 0, :6 * D]
    dm_cols = lax.dynamic_slice(dmod_all, (0, me * ada_cols), (N_DEV, ada_cols))
    out["w_ada"] = _adamw_w_ada(w_ada[0], m_w_ada[0], v_w_ada[0], jnp.transpose(silu_c), dm_cols)

    offsets, off = {}, 0
    for n in SMALL_ORDER:
        offsets[n] = off
        off += small[n].shape[1]
    names = [SMALL_PARAM.get(n, n) for n in SMALL_ORDER if n != "err"]
    results, err = _adamw_rows(bundle_all, [offsets[n] for n in SMALL_ORDER if n != "err"],
                               [prm[n] for n in names], [mom[n] for n in names], [var[n] for n in names],
                               offsets["err"], D)
    out.update(zip(names, results))
    loss = 0.5 * jnp.sum(err) / D

    res = [loss, grad_x[None]]
    for kind in range(4):
        for n in WEIGHTS:
            t = out[n][kind]
            res.append(t[None] if prm[n].ndim == 3 else t)
    return tuple(res)
```

```python
import functools
import math

import jax
import jax.numpy as jnp
from jax import lax
from jax.experimental import pallas as pl
from jax.experimental.pallas import tpu as pltpu

F32 = jnp.float32
BF16 = jnp.bfloat16

N_DEV = 8
S = 2048
D = 1024
D_FF = 2816
HEADS = 8
HEAD_DIM = 64
Q_LORA = 768
KV_LORA = 256
ROPE_DIM = 32
ROPE_THETA = 10000.0
NORM_EPS = 1e-6
LANES = 128
VMEM_LIMIT = 56 * 1024 * 1024

ADAM_LR = 0.001
ADAM_B1 = 0.9
ADAM_B2 = 0.999
ADAM_EPS = 1e-08
ADAM_WD = 0.01
ADAM_STEP = 10

ATT_T = 256
LOG2E = 1.4426950408889634
N_ATT = S // ATT_T

NN = (((1,), (0,)), ((), ()))
NT = (((1,), (1,)), ((), ()))
TN = (((0,), (0,)), ((), ()))
MESH = pl.DeviceIdType.MESH


def _params(sem=None):
    return pltpu.CompilerParams(dimension_semantics=sem, vmem_limit_bytes=VMEM_LIMIT)


def _pick(n, cap):
    best = None
    for t in range(LANES, cap + 1, LANES):
        if n % t == 0:
            best = t
    return best if best is not None else n


def _mm(a, b, mode, out_dtype, name, acc=None, dep=None):
    if mode == "nn":
        (m, k), (k2, n), dn = a.shape, b.shape, NN
    elif mode == "nt":
        (m, k), (n, k2), dn = a.shape, b.shape, NT
    else:
        (k, m), (k2, n), dn = a.shape, b.shape, TN
    assert k == k2, (a.shape, b.shape, mode)
    tn = _pick(n, 1024)
    tm = _pick(m, 1536)
    osz = jnp.dtype(out_dtype).itemsize

    def need(tm_):
        blk = tm_ * k * 2 + tn * k * 2 + tm_ * tn * osz + (tm_ * tn * 4 if acc is not None else 0)
        return 2 * blk + tm_ * tn * 4
    while need(tm) > 36 * 1024 * 1024 and tm % 256 == 0:
        tm //= 2

    def body(*refs):
        a_ref, b_ref, o_ref = refs[0], refs[1], refs[-1]
        r = lax.dot_general(a_ref[...], b_ref[...], dn, preferred_element_type=F32)
        if acc is not None:
            r = r + refs[2][...]
        o_ref[...] = r.astype(o_ref.dtype)

    if mode == "tn":
        a_spec = pl.BlockSpec((k, tm), lambda i, j: (0, i))
    else:
        a_spec = pl.BlockSpec((tm, k), lambda i, j: (i, 0))
    if mode == "nt":
        b_spec = pl.BlockSpec((tn, k), lambda i, j: (j, 0))
    else:
        b_spec = pl.BlockSpec((k, tn), lambda i, j: (0, j))
    o_spec = pl.BlockSpec((tm, tn), lambda i, j: (i, j))
    in_specs = [a_spec, b_spec] + ([o_spec] if acc is not None else [])
    in_specs += [pl.BlockSpec(memory_space=pl.ANY)] if dep is not None else []
    args = (a, b) + ((acc,) if acc is not None else ()) + ((dep,) if dep is not None else ())
    return pl.pallas_call(
        body, name=name, grid=(m // tm, n // tn),
        in_specs=in_specs, out_specs=o_spec,
        out_shape=jax.ShapeDtypeStruct((m, n), out_dtype),
        compiler_params=_params(("parallel", "parallel")),
    )(*args)


def _mm_epi(a, b, mode, tnb, epi, name, tm, rows=(), vecs=(), outs=(), sums=(), pro=None):
    m = a.shape[0]
    k, nb = (b.shape if mode == "nn" else b.shape[::-1])
    dn = NN if mode == "nn" else NT
    pro_fn, pro_vecs, a_off = pro if pro is not None else (None, (), 0)
    n_in = 2 + len(rows) + len(vecs)
    n_all = n_in + len(pro_vecs)
    sub = min(tm, 256)

    def body(*refs):
        if pro is not None:
            a_out, a_scr = refs[-2:]
            refs = refs[:-2]

            @pl.when(pl.program_id(1) == 0)
            def _():
                a_scr[...] = pro_fn(refs[0][...], *[x[...] for x in refs[n_in:n_all]]).astype(BF16)
                a_out[...] = a_scr[...]
            a_ref = a_scr
        else:
            a_ref = refs[0]
        o_refs = refs[n_all:n_all + len(outs)]
        s_refs = refs[n_all + len(outs):]
        if sums:
            @pl.when((pl.program_id(0) == 0) & (pl.program_id(1) == 0))
            def _():
                for s_ref in s_refs:
                    s_ref[...] = jnp.zeros(s_ref.shape, F32)
        for c in range(tm // sub):
            rs = slice(c * sub, (c + 1) * sub)
            r = lax.dot_general(a_ref[rs, :], refs[1][...], dn, preferred_element_type=F32)
            o_vals, s_vals = epi(r, *[x[rs, :] for x in refs[2:2 + len(rows)]], *[x[...] for x in refs[2 + len(rows):n_in]])
            assert len(o_vals) == len(o_refs) and len(s_vals) == len(s_refs)
            for o_ref, val in zip(o_refs, o_vals):
                o_ref[rs, :] = val.astype(o_ref.dtype)
            for s_ref, val in zip(s_refs, s_vals):
                s_ref[...] += val

    once = dict(pipeline_mode=pl.Buffered(1)) if nb == tnb else {}
    if mode == "nn":
        b_spec = pl.BlockSpec((k, tnb), lambda i, j: (0, j), **once)
    else:
        b_spec = pl.BlockSpec((tnb, k), lambda i, j: (j, 0), **once)
    in_specs = [pl.BlockSpec((tm, k), lambda i, j: (i, a_off)), b_spec]
    rows = [tuple(r) + (0,) * (3 - len(r)) for r in rows]
    in_specs += [pl.BlockSpec((tm, w), functools.partial(lambda i, j, off: (i, j + off), off=off)) for _, w, off in rows]
    in_specs += [pl.BlockSpec(v.shape, lambda i, j: (0, 0)) for v in list(vecs) + list(pro_vecs)]
    out_specs = [pl.BlockSpec((tm, w), lambda i, j: (i, j)) for _, w, _ in outs]
    out_specs += [pl.BlockSpec((1, w), lambda i, j: (0, 0)) for w in sums]
    out_shape = [jax.ShapeDtypeStruct((m, full), dt) for full, _, dt in outs]
    out_shape += [jax.ShapeDtypeStruct((1, w), F32) for w in sums]
    if pro is not None:
        out_specs.append(pl.BlockSpec((tm, k), lambda i, j: (i, 0)))
        out_shape.append(jax.ShapeDtypeStruct((m, k), BF16))
    return pl.pallas_call(
        body, name=name, grid=(m // tm, nb // tnb),
        in_specs=in_specs, out_specs=out_specs, out_shape=out_shape,
        scratch_shapes=[pltpu.VMEM((tm, k), BF16)] if pro is not None else [],
        compiler_params=_params(("arbitrary", "arbitrary") if sums else ("parallel", "arbitrary" if pro is not None else "parallel")),
    )(a, b, *[r[0] for r in rows], *vecs, *pro_vecs)


def _rowwise(fn, row_ins, vec_ins, row_outs, sum_outs, name, tm=512):
    n_in = len(row_ins) + len(vec_ins)
    n_o = len(row_outs)
    rows = row_ins[0][0].shape[0]

    def body(*refs):
        vals = [r[...] for r in refs[:n_in]]
        outs = refs[n_in:]
        ro, so = fn(*vals)
        assert len(ro) == n_o and len(so) == len(sum_outs)
        for r, v in zip(outs[:n_o], ro):
            r[...] = v.astype(r.dtype)
        if sum_outs:
            @pl.when(pl.program_id(0) == 0)
            def _():
                for r in outs[n_o:]:
                    r[...] = jnp.zeros(r.shape, F32)
            for r, v in zip(outs[n_o:], so):
                r[...] += v

    in_specs = [pl.BlockSpec((tm, w), functools.partial(lambda i, b: (i, b), b=b)) for _, w, b in row_ins]
    in_specs += [pl.BlockSpec(v.shape, lambda i: (0, 0)) for v in vec_ins]
    out_specs = [pl.BlockSpec((tm, w), lambda i: (i, 0)) for w, _ in row_outs]
    out_specs += [pl.BlockSpec((1, w), lambda i: (0, 0)) for w in sum_outs]
    out_shape = [jax.ShapeDtypeStruct((rows, w), dt) for w, dt in row_outs]
    out_shape += [jax.ShapeDtypeStruct((1, w), F32) for w in sum_outs]
    return pl.pallas_call(
        body, name=name, grid=(rows // tm,),
        in_specs=in_specs, out_specs=out_specs, out_shape=out_shape,
        compiler_params=_params(("arbitrary",)),
    )(*[a for a, _, _ in row_ins], *vec_ins)


def _sigmoid(x):
    return 1.0 / (1.0 + jnp.exp(-x))


def _rstd(x):
    return lax.rsqrt(jnp.mean(x * x, axis=-1, keepdims=True) + NORM_EPS)


def _norm_bwd(dyn, xn, r):
    return r * (dyn - xn * jnp.mean(dyn * xn, axis=-1, keepdims=True))


def _colsum(x):
    return jnp.sum(x, axis=0, keepdims=True)


def _rope_tables(pos, invf):
    def fn(p, f):
        lane = lax.broadcasted_iota(jnp.int32, (1, LANES), 1)
        ang = p * f
        cs, sn = jnp.cos(ang), jnp.sin(ang)
        rot = (lane >= 64) & (lane < 96)
        ct = jnp.where(lane < 64, 1.0, jnp.where(rot, cs, 0.0))
        sa = jnp.where((lane >= 64) & (lane < 80), -sn, 0.0)
        sb = jnp.where((lane >= 80) & (lane < 96), sn, 0.0)
        return (ct, sa, sb), ()
    return _rowwise(fn, [(pos, 1, 0)], [invf], [(LANES, F32)] * 3, [], "rope_tables")


def _rope(x, ct, sa, sb):
    return x * ct + pltpu.roll(x, LANES - 16, 1) * sa + pltpu.roll(x, 16, 1) * sb


def _rope_t(x, ct, sa, sb):
    return x * ct - pltpu.roll(x, LANES - 16, 1) * sa - pltpu.roll(x, 16, 1) * sb


def _head_mask(width, hh):
    lane = lax.broadcasted_iota(jnp.int32, (1, width), 1)
    half = width // 2
    return (lane >= hh * half) & (lane < (hh + 1) * half)


ATT_PP = 2
ATT_CHAINS = [(a, hh) for a in range(ATT_PP) for hh in range(2)]
ATT_G = HEADS // (2 * ATT_PP)


def _pair(ref_or_val, a, width, rows=slice(None)):
    return ref_or_val[rows, a * width:(a + 1) * width]


def _head_cols(ref, a, hh, dkp, rows=slice(None)):
    if dkp == 2 * LANES:
        return ref[rows, a * dkp + hh * LANES:a * dkp + (hh + 1) * LANES]
    blk = _pair(ref, a, dkp, rows)
    return jnp.where(_head_mask(dkp, hh), blk, jnp.zeros_like(blk))


def _attn_fwd(q, qo, k, ko, v, vo, dkp, scale, bias, name):
    T = ATT_T
    assert qo % ATT_PP == 0 and ko % ATT_PP == 0 and vo % ATT_PP == 0
    qo, ko, vo = qo // ATT_PP, ko // ATT_PP, vo // ATT_PP
    split = dkp == 2 * LANES

    def body(*refs):
        if bias is not None:
            q_ref, k_ref, v_ref, b_ref, o_ref, lse_ref, s_scr = refs
        else:
            q_ref, k_ref, v_ref, o_ref, lse_ref, s_scr = refs
        i = pl.program_id(1)
        row = lax.broadcasted_iota(jnp.int32, (T, T), 0)
        col = lax.broadcasted_iota(jnp.int32, (T, T), 1)
        qms = [_head_cols(q_ref, a, hh, dkp) for a, hh in ATT_CHAINS]

        def k_of(a, hh, ks):
            return _head_cols(k_ref, a, hh, dkp, ks) if split else _pair(k_ref, a, dkp, ks)

        def fold(t):
            return [t[:, c * LANES:(c + 1) * LANES] for c in range(T // LANES)]

        def run(nt):
            mls = [jnp.full((T, LANES), -jnp.inf, F32) for _ in ATT_CHAINS]
            for j in range(nt):
                ks = slice(j * T, (j + 1) * T)
                for ci, (a, hh) in enumerate(ATT_CHAINS):
                    s = lax.dot_general(qms[ci], k_of(a, hh, ks), NT, preferred_element_type=F32) * (scale * LOG2E)
                    if bias is not None:
                        s = s + b_ref[2 * a + hh, j] * LOG2E
                    if j == nt - 1:
                        s = jnp.where(row >= col, s, -jnp.inf)
                    s_scr[ci, j] = s
                    for part in fold(s):
                        mls[ci] = jnp.maximum(mls[ci], part)
            ms = [jnp.max(ml, axis=1, keepdims=True) for ml in mls]
            mbs = [jnp.broadcast_to(m, (T, LANES)) for m in ms]
            for a in range(ATT_PP):
                ls = [jnp.zeros((T, LANES), F32) for _ in range(2)]
                ps, vms = [], []
                for j in range(nt):
                    vb = _pair(v_ref, a, LANES, slice(j * T, (j + 1) * T))
                    for hh in range(2):
                        parts = [jnp.exp2(part - mbs[2 * a + hh]) for part in fold(s_scr[2 * a + hh, j])]
                        for part in parts:
                            ls[hh] = ls[hh] + part
                        ps.append(jnp.concatenate(parts, axis=1).astype(BF16))
                        vms.append(jnp.where(_head_mask(LANES, hh), vb, jnp.zeros_like(vb)))
                acc = lax.dot_general(jnp.concatenate(ps, axis=1), jnp.concatenate(vms, axis=0), NN,
                                      preferred_element_type=F32)
                l0, l1 = [jnp.sum(l, axis=1, keepdims=True) for l in ls]
                lse_ref[2 * a] = ms[2 * a] + jnp.log2(l0)
                lse_ref[2 * a + 1] = ms[2 * a + 1] + jnp.log2(l1)
                inv = jnp.where(_head_mask(LANES, 0), 1.0 / l0, 1.0 / l1)
                o_ref[:, a * LANES:(a + 1) * LANES] = (acc * inv).astype(o_ref.dtype)

        for nt in range(1, N_ATT + 1):
            pl.when(i == nt - 1)(functools.partial(run, nt))

    in_specs = [
        pl.BlockSpec((T, ATT_PP * dkp), lambda g, i: (i, qo + g)),
        pl.BlockSpec((S, ATT_PP * dkp), lambda g, i: (0, ko + g)),
        pl.BlockSpec((S, ATT_PP * LANES), lambda g, i: (0, vo + g)),
    ]
    args = [q, k, v]
    if bias is not None:
        in_specs.append(pl.BlockSpec((2 * ATT_PP, N_ATT, 1, T), lambda g, i: (g, 0, 0, 0)))
        args.append(bias)
    return pl.pallas_call(
        body, name=name, grid=(ATT_G, N_ATT),
        in_specs=in_specs,
        out_specs=[pl.BlockSpec((T, ATT_PP * LANES), lambda g, i: (i, g)),
                   pl.BlockSpec((2 * ATT_PP, T, 1), lambda g, i: (g, i, 0))],
        out_shape=[jax.ShapeDtypeStruct((S, HEADS * HEAD_DIM), BF16),
                   jax.ShapeDtypeStruct((HEADS, S, 1), F32)],
        scratch_shapes=[pltpu.VMEM((len(ATT_CHAINS), N_ATT, T, T), F32)],
        compiler_params=_params(("parallel", "arbitrary")),
    )(*args)


def _attn_grad(q, qo, k, ko, v, vo, do, lse, dkp, scale, bias, qk_dtype, name):
    T = ATT_T
    has_b = bias is not None
    qo, ko, vo = qo // ATT_PP, ko // ATT_PP, vo // ATT_PP
    n_ch = len(ATT_CHAINS)
    split = dkp == 2 * LANES

    def body(*refs):
        q_ref, k_ref, v_ref, do_ref, lse_ref = refs[:5]
        refs = refs[5:]
        if has_b:
            b_ref, refs = refs[0], refs[1:]
        dq_ref, dk_ref, dv_ref = refs[:3]
        refs = refs[3:]
        if has_b:
            db_ref, refs = refs[0], refs[1:]
        p_scr, dp_scr, dk_acc, dv_acc = refs[:4]
        db_acc = refs[4] if has_b else None
        i = pl.program_id(1)

        @pl.when(i == 0)
        def _():
            dk_acc[...] = jnp.zeros(dk_acc.shape, F32)
            dv_acc[...] = jnp.zeros(dv_acc.shape, F32)
            if has_b:
                db_acc[...] = jnp.zeros(db_acc.shape, F32)

        row = lax.broadcasted_iota(jnp.int32, (T, T), 0)
        col = lax.broadcasted_iota(jnp.int32, (T, T), 1)

        def fold(t):
            return [t[:, c * LANES:(c + 1) * LANES] for c in range(T // LANES)]

        qms, doms, lses = [], [], []
        for a, hh in ATT_CHAINS:
            dob = _pair(do_ref, a, LANES)
            qms.append(_head_cols(q_ref, a, hh, dkp))
            doms.append(jnp.where(_head_mask(LANES, hh), dob, jnp.zeros_like(dob)))
            lses.append(lse_ref[2 * a + hh])

        def k_of(a, hh, ks):
            return _head_cols(k_ref, a, hh, dkp, ks) if split else _pair(k_ref, a, dkp, ks)

        def run(nt):
            dls = [jnp.zeros((T, LANES), F32) for _ in ATT_CHAINS]
            for j in range(nt):
                ks = slice(j * T, (j + 1) * T)
                for ci, (a, hh) in enumerate(ATT_CHAINS):
                    s = lax.dot_general(qms[ci], k_of(a, hh, ks), NT, preferred_element_type=F32) * (scale * LOG2E)
                    if has_b:
                        s = s + b_ref[ci, j] * LOG2E
                    s = s - lses[ci]
                    if j == nt - 1:
                        s = jnp.where(row >= col, s, -jnp.inf)
                    p = jnp.exp2(s)
                    dp = lax.dot_general(doms[ci], _pair(v_ref, a, LANES, ks), NT, preferred_element_type=F32)
                    p_scr[ci, j] = p
                    dp_scr[ci, j] = dp
                    for part in fold(p * dp):
                        dls[ci] = dls[ci] + part
            deltas = [jnp.broadcast_to(jnp.sum(dl, axis=1, keepdims=True), (T, LANES)) for dl in dls]
            for a in range(ATT_PP):
                ds_all, km_all = [], []
                if split:
                    qts = [jnp.transpose(qms[2 * a + hh]) for hh in range(2)]
                else:
                    qm2t = jnp.transpose(jnp.concatenate([qms[2 * a], qms[2 * a + 1]], axis=0))
                dom2t = jnp.transpose(jnp.concatenate([doms[2 * a], doms[2 * a + 1]], axis=0))
                for j in range(nt):
                    ks = slice(j * T, (j + 1) * T)
                    p2, ds2 = [], []
                    for hh in range(2):
                        ci = 2 * a + hh
                        p = p_scr[ci, j]
                        ds = jnp.concatenate([pp * (dd - deltas[ci]) for pp, dd in zip(fold(p), fold(dp_scr[ci, j]))], axis=1)
                        if has_b:
                            db_acc[ci, j] += jnp.sum(ds, axis=0, keepdims=True)
                        p2.append(p.astype(BF16))
                        ds2.append((ds * scale).astype(BF16))
                        if not split:
                            km_all.append(_head_cols(k_ref, a, hh, dkp, ks))
                    dv_acc[a * LANES:(a + 1) * LANES, ks] += lax.dot_general(
                        dom2t, jnp.concatenate(p2, axis=0), NN, preferred_element_type=F32)
                    if split:
                        for hh in range(2):
                            dk_acc[a * dkp + hh * LANES:a * dkp + (hh + 1) * LANES, ks] += lax.dot_general(
                                qts[hh], ds2[hh], NN, preferred_element_type=F32)
                    else:
                        dk_acc[a * dkp:(a + 1) * dkp, ks] += lax.dot_general(
                            qm2t, jnp.concatenate(ds2, axis=0), NN, preferred_element_type=F32)
                    ds_all += ds2
                if split:
                    for hh in range(2):
                        dq = lax.dot_general(jnp.concatenate(ds_all[hh::2], axis=1),
                                             _head_cols(k_ref, a, hh, dkp, slice(0, nt * T)), NN,
                                             preferred_element_type=F32)
                        dq_ref[:, a * dkp + hh * LANES:a * dkp + (hh + 1) * LANES] = dq.astype(dq_ref.dtype)
                else:
                    dq = lax.dot_general(jnp.concatenate(ds_all, axis=1), jnp.concatenate(km_all, axis=0), NN,
                                         preferred_element_type=F32)
                    dq_ref[:, a * dkp:(a + 1) * dkp] = dq.astype(dq_ref.dtype)

        for nt in range(1, N_ATT + 1):
            pl.when(i == nt - 1)(functools.partial(run, nt))

        @pl.when(i == N_ATT - 1)
        def _():
            dk_ref[...] = jnp.transpose(dk_acc[...]).astype(dk_ref.dtype)
            dv_ref[...] = jnp.transpose(dv_acc[...]).astype(dv_ref.dtype)
            if has_b:
                db_ref[...] = db_acc[...]

    in_specs = [
        pl.BlockSpec((T, ATT_PP * dkp), lambda g, i: (i, qo + g)),
        pl.BlockSpec((S, ATT_PP * dkp), lambda g, i: (0, ko + g)),
        pl.BlockSpec((S, ATT_PP * LANES), lambda g, i: (0, vo + g)),
        pl.BlockSpec((T, ATT_PP * LANES), lambda g, i: (i, g)),
        pl.BlockSpec((2 * ATT_PP, T, 1), lambda g, i: (g, i, 0)),
    ]
    args = [q, k, v, do, lse]
    out_specs = [
        pl.BlockSpec((T, ATT_PP * dkp), lambda g, i: (i, g)),
        pl.BlockSpec((S, ATT_PP * dkp), lambda g, i: (0, g)),
        pl.BlockSpec((S, ATT_PP * LANES), lambda g, i: (0, g)),
    ]
    width = (HEADS // 2) * dkp
    out_shape = [
        jax.ShapeDtypeStruct((S, width), qk_dtype),
        jax.ShapeDtypeStruct((S, width), qk_dtype),
        jax.ShapeDtypeStruct((S, HEADS * HEAD_DIM), BF16),
    ]
    scratch = [pltpu.VMEM((n_ch, N_ATT, T, T), F32), pltpu.VMEM((n_ch, N_ATT, T, T), F32),
               pltpu.VMEM((ATT_PP * dkp, S), F32), pltpu.VMEM((ATT_PP * LANES, S), F32)]
    if has_b:
        bspec = pl.BlockSpec((2 * ATT_PP, N_ATT, 1, T), lambda g, i: (g, 0, 0, 0))
        in_specs.append(bspec)
        args.append(bias)
        out_specs.append(bspec)
        out_shape.append(jax.ShapeDtypeStruct((HEADS, N_ATT, 1, T), F32))
        scratch.append(pltpu.VMEM((2 * ATT_PP, N_ATT, 1, T), F32))
    return pl.pallas_call(
        body, name=name, grid=(ATT_G, N_ATT),
        in_specs=in_specs, out_specs=out_specs, out_shape=out_shape, scratch_shapes=scratch,
        compiler_params=_params(("parallel", "arbitrary")),
    )(*args)


def _tri(upper):
    a = lax.broadcasted_iota(jnp.int32, (LANES, LANES), 0)
    b = lax.broadcasted_iota(jnp.int32, (LANES, LANES), 1)
    return jnp.where(a <= b if upper else a >= b, 1.0, 0.0).astype(F32)


def _fox_gates(proj, blk, bf):
    def body(m_ref, b_ref, z_out, o_ref):
        tri = _tri(True)
        carry = jnp.zeros((HEADS, 1), F32)
        for t in range(S // LANES):
            sl = slice(t * LANES, (t + 1) * LANES)
            zt = jnp.transpose(m_ref[sl, :])[:HEADS]
            z_out[:, sl] = zt
            z = zt + b_ref[...]
            logf = jnp.minimum(z, 0.0) - jnp.log(1.0 + jnp.exp(-jnp.abs(z)))
            c = lax.dot_general(logf, tri, NN, preferred_element_type=F32,
                                precision=lax.Precision.HIGHEST) + carry
            o_ref[:, sl] = -c
            carry = c[:, LANES - 1:LANES]

    return pl.pallas_call(
        body, name="fox_gates", grid=(1,),
        in_specs=[pl.BlockSpec((S, LANES), lambda i: (0, blk)), pl.BlockSpec(bf.shape, lambda i: (0, 0))],
        out_specs=[pl.BlockSpec((HEADS, S), lambda i: (0, 0))] * 2,
        out_shape=[jax.ShapeDtypeStruct((HEADS, S), F32)] * 2,
        compiler_params=_params(("arbitrary",)),
    )(proj, bf)


def _fox_gates_bwd(dbias, zt, bf):
    def body(d_ref, z_ref, b_ref, dz_ref, dbf_ref):
        tri = _tri(False)
        carry = jnp.zeros((HEADS, 1), F32)
        tot = jnp.zeros((HEADS, 1), F32)
        for t in reversed(range(S // LANES)):
            sl = slice(t * LANES, (t + 1) * LANES)
            df = -d_ref[:, sl]
            c = lax.dot_general(df, tri, NN, preferred_element_type=F32,
                                precision=lax.Precision.HIGHEST) + carry
            carry = c[:, 0:1]
            z = z_ref[:, sl] + b_ref[...]
            dz = c * _sigmoid(-z)
            dz_ref[:, sl] = dz
            tot = tot + jnp.sum(dz, axis=1, keepdims=True)
        dbf_ref[...] = tot

    return pl.pallas_call(
        body, name="fox_gates_bwd",
        out_shape=[jax.ShapeDtypeStruct((HEADS, S), F32), jax.ShapeDtypeStruct((HEADS, 1), F32)],
        compiler_params=_params(),
    )(dbias, zt, bf)


def _mod_part(c_all, w_ada, b_cols):
    def body(c_ref, w_ref, b_ref, o_ref, s_ref):
        c = c_ref[...]
        sc = c * _sigmoid(c)
        s_ref[...] = sc
        o_ref[...] = lax.dot_general(sc, w_ref[...], NN, preferred_element_type=F32,
                                     precision=lax.Precision.HIGHEST) + b_ref[...]

    return pl.pallas_call(
        body, name="mod_part",
        out_shape=[jax.ShapeDtypeStruct((N_DEV, w_ada.shape[1]), F32), jax.ShapeDtypeStruct(c_all.shape, F32)],
        compiler_params=_params(),
    )(c_all, w_ada, b_cols)


def _adamw_w_ada(w, m, v, sc_t, dm):
    rows, cols = w.shape
    tr = 256

    def body(w_ref, m_ref, v_ref, s_ref, d_ref, g_out, d_out, m_out, v_out):
        g = s_ref[:, 0:1] * d_ref[0:1, :]
        for b in range(1, N_DEV):
            g = g + s_ref[:, b:b + 1] * d_ref[b:b + 1, :]
        g_out[...] = g
        d_out[...], m_out[...], v_out[...] = _adamw_math(w_ref[...], g, m_ref[...], v_ref[...])

    spec = pl.BlockSpec((tr, cols), lambda i: (i, 0))
    return pl.pallas_call(
        body, name="adamw_w_ada", grid=(rows // tr,),
        in_specs=[spec, spec, spec, pl.BlockSpec((tr, N_DEV), lambda i: (i, 0)), pl.BlockSpec(dm.shape, lambda i: (0, 0))],
        out_specs=[spec] * 4, out_shape=[jax.ShapeDtypeStruct((rows, cols), F32)] * 4,
        compiler_params=_params(("parallel",)),
    )(w, m, v, sc_t, dm)


def _adamw(w, m, v, parts, name, own=None, slot=None):
    rows, cols = w.shape
    n = parts.shape[0]
    by_cols = rows % 256 != 0 and cols % 256 == 0
    tr, tc = (rows, 256) if by_cols else ((rows if rows <= 512 else 256), cols)
    tile = (lambda i: (0, i)) if by_cols else (lambda i: (i, 0))

    def body(*refs):
        if own is not None:
            s_ref, refs = refs[0], refs[1:]
            w_ref, m_ref, v_ref, p_ref, o_ref, g_out, d_out, m_out, v_out = refs
            terms = [jnp.where(s_ref[0] == kk, o_ref[0], p_ref[kk]) for kk in range(n)]
        else:
            w_ref, m_ref, v_ref, p_ref, g_out, d_out, m_out, v_out = refs
            terms = [p_ref[kk] for kk in range(n)]
        g = terms[0].astype(F32)
        for term in terms[1:]:
            g = g + term.astype(F32)
        g_out[...] = g
        d_out[...], m_out[...], v_out[...] = _adamw_math(w_ref[...], g, m_ref[...], v_ref[...])

    spec = pl.BlockSpec((tr, tc), lambda i, *_: tile(i))
    in_specs = [spec, spec, spec, pl.BlockSpec((n, tr, tc), lambda i, *_: (0,) + tile(i))]
    out_shape = [jax.ShapeDtypeStruct((rows, cols), F32)] * 4
    grid = (rows // tr if not by_cols else cols // tc,)
    if own is None:
        return pl.pallas_call(
            body, name=name, grid=grid, in_specs=in_specs, out_specs=[spec] * 4, out_shape=out_shape,
            compiler_params=_params(("parallel",)),
        )(w, m, v, parts)
    in_specs.append(pl.BlockSpec((1, tr, tc), lambda i, s: (s[0],) + tile(i)))
    return pl.pallas_call(
        body, name=name, out_shape=out_shape, compiler_params=_params(("parallel",)),
        grid_spec=pltpu.PrefetchScalarGridSpec(num_scalar_prefetch=1, grid=grid, in_specs=in_specs,
                                               out_specs=[spec] * 4),
    )(slot, w, m, v, parts, own)


def _adamw_math(w, g, m, v):
    mm = ADAM_B1 * m + (1.0 - ADAM_B1) * g
    vv = ADAM_B2 * v + (1.0 - ADAM_B2) * (g * g)
    m_hat = mm / (1.0 - ADAM_B1 ** ADAM_STEP)
    v_hat = vv / (1.0 - ADAM_B2 ** ADAM_STEP)
    return -ADAM_LR * (m_hat / (jnp.sqrt(v_hat) + ADAM_EPS) + ADAM_WD * w), mm, vv


def _adamw_rows(bundles, offsets, ws, ms, vs, err_off, err_width):
    k = len(ws)

    def body(*refs):
        b_ref = refs[0]
        w_refs, m_refs, v_refs = refs[1:1 + k], refs[1 + k:1 + 2 * k], refs[1 + 2 * k:1 + 3 * k]
        outs = refs[1 + 3 * k:]
        g_all = b_ref[0]
        for kk in range(1, N_DEV):
            g_all = g_all + b_ref[kk]
        for i in range(k):
            width = w_refs[i].shape[1]
            g = g_all[:, offsets[i]:offsets[i] + width]
            outs[4 * i][...] = g
            outs[4 * i + 1][...], outs[4 * i + 2][...], outs[4 * i + 3][...] = _adamw_math(
                w_refs[i][...], g, m_refs[i][...], v_refs[i][...])
        outs[4 * k][...] = g_all[:, err_off:err_off + err_width]

    out_shape = []
    for w_ in ws:
        out_shape += [jax.ShapeDtypeStruct(w_.shape, F32)] * 4
    out_shape.append(jax.ShapeDtypeStruct((1, err_width), F32))
    res = pl.pallas_call(body, name="adamw_rows", out_shape=out_shape, compiler_params=_params())(bundles, *ws, *ms, *vs)
    return [tuple(res[4 * i:4 * i + 4]) for i in range(k)], res[-1]


def _coords():
    return lax.axis_index("x"), lax.axis_index("y"), lax.axis_index("c")


def _flat(px, py, pc):
    return 4 * px + 2 * py + pc


def _all_gather(arrs, name):
    n = len(arrs)

    def body(*refs):
        ins, outs = refs[:n], refs[n:2 * n]
        send, recv, lsem = refs[2 * n:]
        x, y, c = _coords()
        me, sibling = (x, y, c), (x, y, 1 - c)
        chips = [(1 - x, y), (x, 1 - y), (1 - x, 1 - y)]

        def copy(a, kk, block, to, src=None):
            slot = outs[a].at[_flat(*block)]
            return pltpu.make_async_remote_copy(
                src_ref=slot if src is None else src, dst_ref=slot,
                send_sem=send.at[a, kk], recv_sem=recv.at[a, kk],
                device_id=to, device_id_type=MESH)

        mine = [pltpu.make_async_copy(ins[a], outs[a].at[_flat(*me)], lsem.at[a]) for a in range(n)]
        for cp in mine:
            cp.start()
        first = []
        for a in range(n):
            first.append(copy(a, 0, me, sibling, src=ins[a]))
            first += [copy(a, 1 + j, me, (*chip, c), src=ins[a]) for j, chip in enumerate(chips)]
        for cp in first:
            cp.start()
        passed = []
        for j, chip in enumerate(chips):
            for a in range(n):
                copy(a, 1 + j, (*chip, c), me).wait_recv()
                cp = copy(a, 4 + j, (*chip, c), sibling)
                cp.start()
                passed.append(cp)
        for a in range(n):
            copy(a, 0, sibling, me).wait_recv()
        for j, chip in enumerate(chips):
            for a in range(n):
                copy(a, 4 + j, (*chip, 1 - c), me).wait_recv()
        for cp in first + passed:
            cp.wait_send()
        for cp in mine:
            cp.wait()

    any_spec = pl.BlockSpec(memory_space=pl.ANY)
    return pl.pallas_call(
        body, name=name,
        in_specs=[any_spec] * n, out_specs=[any_spec] * n,
        out_shape=[jax.ShapeDtypeStruct((N_DEV,) + a.shape, a.dtype) for a in arrs],
        scratch_shapes=[pltpu.SemaphoreType.DMA((n, 7)), pltpu.SemaphoreType.DMA((n, 7)),
                        pltpu.SemaphoreType.DMA((n,))],
    )(*arrs)


def _peer_list():
    x, y, c = _coords()
    return [((1 - x if r & 4 else x), (1 - y if r & 2 else y), (1 - c if r & 1 else c)) for r in range(1, N_DEV)]


def _copy_plan(mode, src, land):
    x, y, c = _coords()
    me = _flat(x, y, c)
    if mode == "gather":
        return [(src, land.at[me], peer) for peer in _peer_list()]
    if mode == "exchange":
        return [(src.at[_flat(*peer)], land.at[me], peer) for peer in _peer_list()]
    if mode == "pair":
        return [(src.at[_flat(q // 2, q % 2, 1 - c)], land.at[q], (x, y, 1 - c)) for q in range(N_DEV // 2)]
    chips = [((1 - x if r & 2 else x), (1 - y if r & 1 else y)) for r in range(1, N_DEV // 2)]
    if mode == "chips":
        return [(src.at[2 * qx + qy], land.at[2 * x + y], (qx, qy, c)) for qx, qy in chips]
    if mode == "spread":
        return [(src, land.at[me], (x, y, 1 - c))] + [(src, land.at[me], (qx, qy, c)) for qx, qy in chips]
    assert mode == "forward"
    return [(land.at[_flat(qx, qy, c)], land.at[_flat(qx, qy, c)], (x, y, 1 - c)) for qx, qy in chips]


N_COPIES = dict(gather=N_DEV - 1, exchange=N_DEV - 1, pair=N_DEV // 2, chips=N_DEV // 2 - 1, spread=N_DEV // 2,
                forward=N_DEV // 2 - 1)


def _land_shape(mode, shape):
    return {"gather": (N_DEV,) + shape, "spread": (N_DEV,) + shape, "exchange": shape,
            "pair": (N_DEV // 2,) + shape[1:], "chips": shape}[mode]


HBM_SPEC = pl.BlockSpec(memory_space=pltpu.HBM)
SEM_SPEC = pl.BlockSpec(memory_space=pltpu.SEMAPHORE)
ANY_SPEC = pl.BlockSpec(memory_space=pl.ANY)
SIDE_EFFECT = pltpu.SideEffectType.DATAFLOW_SIDE_EFFECTING


def _async_start(groups, modes, after, name):
    modes = [modes] * len(groups) if isinstance(modes, str) else list(modes)
    arrs = [(a, m) for g, m in zip(groups, modes) for a in g]
    n = len(arrs)
    fresh = [i for i, (_, m) in enumerate(arrs) if m != "forward"]

    def body(*refs):
        srcs, new_lands = refs[:n], refs[n:n + len(fresh)]
        outs = refs[n + len(fresh) + 1:]
        lands = list(srcs)
        for k, i in enumerate(fresh):
            lands[i] = new_lands[k]
        for ai, (_, mode) in enumerate(arrs):
            for src_ref, dst_ref, peer in _copy_plan(mode, srcs[ai], lands[ai]):
                pltpu.make_async_remote_copy(src_ref=src_ref, dst_ref=dst_ref, send_sem=outs[2 * ai],
                                             recv_sem=outs[2 * ai + 1], device_id=peer, device_id_type=MESH).start()
        outs[-1][...] = jnp.zeros(outs[-1].shape, F32)

    land_shapes = [(_land_shape(arrs[i][1], arrs[i][0].shape), arrs[i][0].dtype) for i in fresh]
    n_buf = n + len(fresh)
    out_shape = [pltpu.SemaphoreType.DMA(())] * (2 * n)
    out_shape += [pltpu.HBM(a.shape, a.dtype) for a, _ in arrs]
    out_shape += [pltpu.HBM(shape, dt) for shape, dt in land_shapes]
    out_shape.append(jax.ShapeDtypeStruct((8, LANES), F32))
    res = pl.pallas_call(
        body, name=name, out_shape=tuple(out_shape),
        in_specs=[HBM_SPEC] * n_buf + [ANY_SPEC],
        out_specs=tuple([SEM_SPEC] * (2 * n) + [HBM_SPEC] * n_buf + [pl.BlockSpec(memory_space=pltpu.VMEM)]),
        input_output_aliases={i: 2 * n + i for i in range(n_buf)},
        compiler_params=pltpu.CompilerParams(has_side_effects=SIDE_EFFECT),
    )(*[pltpu.with_memory_space_constraint(a, pltpu.HBM) for a, _ in arrs],
      *[pltpu.with_memory_space_constraint(lax.empty(shape, dt), pltpu.HBM) for shape, dt in land_shapes],
      after)
    sems, thru = res[:2 * n], res[2 * n:-1]
    land_of = {i: thru[n + k] for k, i in enumerate(fresh)}
    states, idx = [], 0
    for g, mode in zip(groups, modes):
        ids = range(idx, idx + len(g))
        idx += len(g)
        states.append(([sems[2 * i] for i in ids], [sems[2 * i + 1] for i in ids],
                       None if mode == "forward" else [thru[i] for i in ids],
                       [land_of.get(i, thru[i]) for i in ids], mode))
    return states, res[-1]


def _async_wait(state, after, name):
    sends, recvs, srcs, lands, mode = state
    g = len(lands)
    bufs = (list(srcs) if srcs is not None else []) + list(lands)
    nb = len(bufs)

    def body(*refs):
        l_refs, sems = refs[nb - g:nb], refs[nb:nb + 2 * g]
        for ai in range(g):
            moved = l_refs[ai].at[pl.ds(0, N_COPIES[mode])]
            cp = pltpu.make_async_remote_copy(src_ref=moved, dst_ref=moved, send_sem=sems[ai], recv_sem=sems[g + ai],
                                              device_id=_coords(), device_id_type=MESH)
            cp.wait_send()
            cp.wait_recv()

    res = pl.pallas_call(
        body, name=name,
        out_shape=tuple(pltpu.HBM(a.shape, a.dtype) for a in bufs),
        in_specs=[HBM_SPEC] * nb + [SEM_SPEC] * (2 * g) + [ANY_SPEC],
        out_specs=tuple([HBM_SPEC] * nb),
        input_output_aliases={i: i for i in range(nb)},
        compiler_params=pltpu.CompilerParams(has_side_effects=SIDE_EFFECT),
    )(*bufs, *sends, *recvs, after)
    return (list(res[:nb - g]) if srcs is not None else None), list(res[nb - g:])


def _add_sibling(mine, theirs, core):
    def body(c_ref, a_ref, b_ref, o_ref):
        o_ref[...] = (a_ref[...].astype(F32) + b_ref[...].astype(F32)).astype(o_ref.dtype)

    blk = (1,) + mine.shape[1:]
    return pl.pallas_call(
        body, name="add_sibling", out_shape=jax.ShapeDtypeStruct(theirs.shape, mine.dtype),
        grid_spec=pltpu.PrefetchScalarGridSpec(
            num_scalar_prefetch=1, grid=(theirs.shape[0],),
            in_specs=[pl.BlockSpec(blk, lambda q, c: (2 * q + c[0], 0, 0)), pl.BlockSpec(blk, lambda q, c: (q, 0, 0))],
            out_specs=pl.BlockSpec(blk, lambda q, c: (q, 0, 0))),
        compiler_params=_params(("parallel",)),
    )(core, mine, theirs)


def _with_own(land, own, me):
    return lax.dynamic_update_index_in_dim(land, own, me, 0)


IN_SPLITS = (512, 512, 512, 8, 768, 256, 32, 1024, 1024)


def _from_shards(g, fn, out_widths, name, own=None, slot=None):
    _, k, n = g.shape
    tr = min(k, 256)

    def body(*refs):
        if own is not None:
            s_ref, g_ref, own_ref = refs[:3]
            cols = [jnp.where(s_ref[0] == j, own_ref[...], g_ref[j]) for j in range(N_DEV)]
        else:
            g_ref = refs[0]
            cols = [g_ref[j] for j in range(N_DEV)]
        for o_ref, val in zip(refs[-len(out_widths):], fn(jnp.concatenate(cols, axis=1))):
            o_ref[...] = val

    in_specs = [pl.BlockSpec((N_DEV, tr, n), lambda i, *_: (0, i, 0))]
    out_spec = [pl.BlockSpec((tr, wd), lambda i, *_: (i, 0)) for wd in out_widths]
    out_shape = [jax.ShapeDtypeStruct((k, wd), g.dtype) for wd in out_widths]
    if own is None:
        return pl.pallas_call(body, name=name, grid=(k // tr,), in_specs=in_specs, out_specs=out_spec,
                              out_shape=out_shape, compiler_params=_params(("parallel",)))(g)
    in_specs.append(pl.BlockSpec((tr, n), lambda i, *_: (i, 0)))
    return pl.pallas_call(
        body, name=name, out_shape=out_shape, compiler_params=_params(("parallel",)),
        grid_spec=pltpu.PrefetchScalarGridSpec(num_scalar_prefetch=1, grid=(k // tr,), in_specs=in_specs, out_specs=out_spec),
    )(slot, g, own)


def _unshard_cols(g, own=None, slot=None):
    return _from_shards(g, lambda full: (full,), [N_DEV * g.shape[2]], "unshard_cols_%d" % g.shape[2], own, slot)[0]


FFN_T = D_FF // 2
FFN_SHARD = 2 * D_FF // N_DEV


def _unshard_ffn_in(g, own=None, slot=None):
    def pairs(full):
        parts = []
        for j in range(D_FF // FFN_T):
            parts += [full[:, j * FFN_T:(j + 1) * FFN_T], full[:, D_FF + j * FFN_T:D_FF + (j + 1) * FFN_T]]
        return (jnp.concatenate(parts, axis=1),)

    return _from_shards(g, pairs, [2 * D_FF], "unshard_ffn_in", own, slot)[0]


def _shard_ffn_in_t(wt):
    tc = 256

    def body(w_ref, o_ref):
        x = w_ref[...]
        nb = D_FF // FFN_T
        full = jnp.concatenate([x[(2 * j + half) * FFN_T:(2 * j + half + 1) * FFN_T]
                                for half in range(2) for j in range(nb)], axis=0)
        for j in range(N_DEV):
            o_ref[j] = full[j * FFN_SHARD:(j + 1) * FFN_SHARD]

    return pl.pallas_call(
        body, name="shard_ffn_in_t", grid=(D // tc,),
        in_specs=[pl.BlockSpec((2 * D_FF, tc), lambda i: (0, i))],
        out_specs=pl.BlockSpec((N_DEV, FFN_SHARD, tc), lambda i: (0, 0, i)),
        out_shape=jax.ShapeDtypeStruct((N_DEV, FFN_SHARD, D), wt.dtype),
        compiler_params=_params(("parallel",)),
    )(wt)


def _shard_cols(w):
    k, n = w.shape[0], w.shape[1] // N_DEV
    tr = min(k, 256)

    def body(w_ref, o_ref):
        full = w_ref[...]
        for j in range(N_DEV):
            o_ref[j] = full[:, j * n:(j + 1) * n]

    return pl.pallas_call(
        body, name="shard_cols_%d" % n, grid=(k // tr,),
        in_specs=[pl.BlockSpec((tr, N_DEV * n), lambda i: (i, 0))],
        out_specs=pl.BlockSpec((N_DEV, tr, n), lambda i: (0, i, 0)),
        out_shape=jax.ShapeDtypeStruct((N_DEV, k, n), w.dtype),
        compiler_params=_params(("parallel",)),
    )(w)


IN_OFFS = tuple(sum(IN_SPLITS[:i]) for i in range(len(IN_SPLITS) + 1))
IN_SHARD = IN_OFFS[-1] // N_DEV
REGROUP_ROWS = 256
MISC_AT = Q_LORA + KV_LORA + 2 * D
A_COLS = MISC_AT + LANES
A_TILE = A_COLS
B_COLS = 3 * HEADS * HEAD_DIM
MISC_BLOCK = MISC_AT // LANES
KR_AT = 64


def _w_in_regroup(g, own=None, slot=None):
    def groups(full):
        fq, fk, fv, wf, cq, ckv, kr, gf, gm = [full[:, IN_OFFS[i]:IN_OFFS[i + 1]] for i in range(9)]
        rows = full.shape[0]
        gap = jnp.zeros((rows, KR_AT - HEADS), BF16)
        tail = jnp.zeros((rows, LANES - KR_AT - ROPE_DIM), BF16)
        return jnp.concatenate([cq, ckv, gf, gm, wf, gap, kr, tail], axis=1), jnp.concatenate([fq, fk, fv], axis=1)

    return _from_shards(g, groups, [A_COLS, B_COLS], "w_in_regroup", own, slot)


def _w_in_ungroup(da, db_):
    def body(a_ref, b_ref, o_ref):
        a = a_ref[...]
        lora = Q_LORA + KV_LORA
        full = jnp.concatenate([b_ref[...], a[:, MISC_AT:MISC_AT + HEADS], a[:, :lora],
                                a[:, MISC_AT + KR_AT:MISC_AT + KR_AT + ROPE_DIM], a[:, lora:MISC_AT]], axis=1)
        for j in range(N_DEV):
            o_ref[j] = full[:, j * IN_SHARD:(j + 1) * IN_SHARD]

    tr = REGROUP_ROWS
    return pl.pallas_call(
        body, name="w_in_ungroup", grid=(D // tr,),
        in_specs=[pl.BlockSpec((tr, A_COLS), lambda i: (i, 0)), pl.BlockSpec((tr, B_COLS), lambda i: (i, 0))],
        out_specs=pl.BlockSpec((N_DEV, tr, IN_SHARD), lambda i: (0, i, 0)),
        out_shape=jax.ShapeDtypeStruct((N_DEV, D, IN_SHARD), BF16),
        compiler_params=_params(("parallel",)),
    )(da, db_)


def _prepare_weights(g, own=None, slot=None):
    w = {}
    if own is not None:
        small = ("w_uq", "w_ukv", "w_out", "w_ffn_out")
        g = {n: (_with_own(a, own[n], slot[0]) if n in small else a) for n, a in g.items()}
    pick = (lambda n: (own[n], slot)) if own is not None else (lambda n: (None, None))
    if "w_in" in g:
        w["w_a"], w["w_b"] = _w_in_regroup(g["w_in"], *pick("w_in"))
    if "w_uq" in g:
        w_uq = g["w_uq"].reshape(Q_LORA, HEADS, 96)
        w["w_uq"] = jnp.pad(w_uq, ((0, 0), (0, 0), (0, 32))).reshape(Q_LORA, HEADS * LANES)
        ukv = g["w_ukv"]
        w["w_k"] = jnp.transpose(jnp.pad(ukv[:, :, :64], ((0, 0), (0, 0), (0, 64))), (1, 0, 2)).reshape(KV_LORA, HEADS * LANES)
        w["w_v"] = jnp.transpose(ukv[:, :, 64:], (1, 0, 2)).reshape(KV_LORA, HEADS * HEAD_DIM)
    if "w_out" in g:
        w["w_pf"] = _unshard_cols(g["w_proj_fox"], *pick("w_proj_fox"))
        w["w_pm"] = _unshard_cols(g["w_proj_mla"], *pick("w_proj_mla"))
        w["w_out"] = g["w_out"].reshape(D, D)
    if "w_ffn_in" in g:
        w["w_ffn_in"] = _unshard_ffn_in(g["w_ffn_in"], *pick("w_ffn_in"))
        w["w_ffn_out"] = g["w_ffn_out"].reshape(D_FF, D)
    return w


def _shard_grads(dw):
    out = {}
    if "w_a" in dw:
        out["w_in"] = _w_in_ungroup(dw["w_a"], dw["w_b"])
    if "w_uq" in dw:
        w_uq = dw["w_uq"].reshape(Q_LORA, HEADS, LANES)[:, :, :96].reshape(Q_LORA, Q_LORA)
        out["w_uq"] = w_uq.reshape(N_DEV, Q_LORA // N_DEV, Q_LORA)
        k_part = dw["w_k"].reshape(KV_LORA, HEADS, LANES)[:, :, :64]
        v_part = dw["w_v"].reshape(KV_LORA, HEADS, HEAD_DIM)
        out["w_ukv"] = jnp.transpose(jnp.concatenate([k_part, v_part], axis=2), (1, 0, 2))
    if "w_out" in dw:
        out["w_proj_fox"] = _shard_cols(dw["w_pf"])
        out["w_proj_mla"] = _shard_cols(dw["w_pm"])
        out["w_out"] = dw["w_out"].reshape(N_DEV, D // N_DEV, D)
    if "w_ffn_in" in dw:
        out["w_ffn_in"] = _shard_ffn_in_t(dw["w_ffn_in"])
        out["w_ffn_out"] = dw["w_ffn_out"].reshape(N_DEV, D_FF // N_DEV, D)
    return out


def _fwd_bwd(x, pos, mod, target, w, vec, wts, send, relay):
    shift_mix, scale_mix, gate_mix, shift_ffn, scale_ffn, gate_ffn = [mod[:, i * D:(i + 1) * D] for i in range(6)]
    g_pre_mix, g_post_mix, g_pre_ffn, g_post_ffn = vec["g_pre_mix"], vec["g_post_mix"], vec["g_pre_ffn"], vec["g_post_ffn"]
    g_q, g_kv = vec["g_q_lora"], vec["g_kv_lora"]

    inv_freq = 1.0 / (ROPE_THETA ** (jnp.arange(0, ROPE_DIM, 2, dtype=F32) / ROPE_DIM))
    invf = jnp.concatenate([jnp.zeros((64,), F32), inv_freq, inv_freq, jnp.zeros((32,), F32)]).reshape(1, LANES)
    ct, sa, sb = _rope_tables(pos, invf)

    def pre1(xv, g, sc, sh):
        return (xv * _rstd(xv) * g) * (1.0 + sc) + sh
    proj_a, h = _mm_epi(x, w["w_a"], "nn", A_TILE, lambda r: ((r,), ()), "in_proj_a", 512, outs=[(A_COLS, A_TILE, F32)],
                        pro=(pre1, [g_pre_mix, scale_mix, shift_mix], 0))
    qkv = _mm(h, w["w_b"], "nn", BF16, "in_proj_b")

    def lora_norm(cv, g):
        return cv * _rstd(cv) * g
    w = {**w, **wts("lora", qkv)}
    tables = [(ct, LANES), (sa, LANES), (sb, LANES)]

    def rope_q(qv, c_, a_, b_):
        return (jnp.concatenate([_rope(qv[:, hd * LANES:(hd + 1) * LANES], c_, a_, b_) for hd in range(HEADS)], axis=1),), ()
    q_m, cqn = _mm_epi(proj_a, w["w_uq"], "nn", D, rope_q, "mla_uq", 512, rows=tables, outs=[(D, D, BF16)],
                       pro=(lora_norm, [g_q], 0))

    def rope_k(kv, misc, c_, a_, b_):
        lane = lax.broadcasted_iota(jnp.int32, (1, LANES), 1)
        kpe = jnp.where((lane >= 64) & (lane < 96), _rope(misc, c_, a_, b_), 0.0)
        return (jnp.concatenate([kv[:, hd * LANES:(hd + 1) * LANES] + kpe for hd in range(HEADS)], axis=1),), ()
    k_m, ckvn = _mm_epi(proj_a, w["w_k"], "nn", D, rope_k, "mla_uk", 512, rows=[(proj_a, LANES, MISC_BLOCK)] + tables,
                        outs=[(D, D, BF16)], pro=(lora_norm, [g_kv], Q_LORA // KV_LORA))
    v_m = _mm(ckvn, w["w_v"], "nn", BF16, "mla_uv")

    bf = jnp.transpose(vec["b_forget"])
    zt, neg_f = _fox_gates(proj_a, MISC_BLOCK, bf)
    bias = neg_f.reshape(HEADS, N_ATT, 1, ATT_T)
    o_b, lse_b = _attn_fwd(q_m, 0, k_m, 0, v_m, 0, 2 * LANES, 1.0 / math.sqrt(64 + ROPE_DIM), None, "mla_attn")
    bias = bias + wts("relay_proj", o_b)["tok"][0, 0]
    o_a, lse_a = _attn_fwd(qkv, 0, qkv, 4, qkv, 8, LANES, 1.0 / math.sqrt(HEAD_DIM), bias, "fox_attn")

    w = {**w, **wts("proj", o_a)}
    gate_mix = gate_mix + wts("relay_ffn", o_a)["tok"][0, 0]
    pa = _mm(o_a, w["w_pf"], "nn", BF16, "proj_fox")

    def merge(pb_, gf, gm, pa_):
        return (_sigmoid(gf) * pa_ + _sigmoid(gm) * pb_, pb_), ()
    merged, pb = _mm_epi(o_b, w["w_pm"], "nn", 512, merge, "proj_mla", 1024,
                         rows=[(proj_a, 512, 2), (proj_a, 512, 4), (pa, 512)], outs=[(D, 512, BF16), (D, 512, BF16)])
    def post1(yv, xv, gate, gpost, gpre, sc, sh):
        x1 = xv + gate * (yv * _rstd(yv) * gpost)
        return (x1, (x1 * _rstd(x1) * gpre) * (1.0 + sc) + sh, yv), ()
    x1, h2, y = _mm_epi(merged, w["w_out"], "nn", D, post1, "out_proj", 512, rows=[(x, D)],
                        vecs=[gate_mix, g_post_mix, g_pre_ffn, scale_ffn, shift_ffn],
                        outs=[(D, D, F32), (D, D, BF16), (D, D, F32)])
    w = {**w, **wts("ffn", h2)}

    def swiglu(r):
        g, u = r[:, :FFN_T], r[:, FFN_T:]
        return (g * _sigmoid(g) * u, r), ()
    act, gu = _mm_epi(h2, w["w_ffn_in"], "nn", 2 * FFN_T, swiglu, "ffn_in", 512,
                      outs=[(D_FF, FFN_T, BF16), (2 * D_FF, 2 * FFN_T, BF16)])

    def head(y2v, x1v, tv, gate, gpost):
        r = _rstd(y2v)
        yn = y2v * r
        n2 = yn * gpost
        err = (x1v + gate * n2) - tv
        dx2 = err * (1.0 / D)
        dn2 = dx2 * gate
        dy2 = _norm_bwd(dn2 * gpost, yn, r)
        return (dx2, dy2), (_colsum(err * err), _colsum(dx2 * n2), _colsum(dn2 * yn))
    dx2, dy2, err_cols, d_gate_ffn, d_g_post_ffn = _mm_epi(
        act, w["w_ffn_out"], "nn", D, head, "ffn_out", 512, rows=[(x1, D), (target, D)], vecs=[gate_ffn, g_post_ffn],
        outs=[(D, D, F32), (D, D, BF16)], sums=[D, D, D])

    def swiglu_bwd(da, guv):
        g, u = guv[:, :FFN_T].astype(F32), guv[:, FFN_T:].astype(F32)
        sg = _sigmoid(g)
        return (jnp.concatenate([da * u * (sg * (1.0 + g * (1.0 - sg))), da * (g * sg)], axis=1),), ()
    (dgu,) = _mm_epi(dy2, w["w_ffn_out"], "nt", FFN_T, swiglu_bwd, "ffn_out_dx", 512, rows=[(gu, 2 * FFN_T)],
                     outs=[(2 * D_FF, 2 * FFN_T, BF16)])
    dw = {"w_ffn_out": _mm(act, dy2, "tn", BF16, "ffn_out_dw")}
    dw["w_ffn_in"] = _mm(dgu, h2, "tn", BF16, "ffn_in_dw")
    gate_mix = gate_mix + send({n: dw.pop(n) for n in ("w_ffn_in", "w_ffn_out")})[0, 0]

    def mid(dh, x1v, dx2v, yv, gpre, sc, gate, gpost):
        r2 = _rstd(x1v)
        x1n = x1v * r2
        t = dh * x1n
        dx1 = dx2v + _norm_bwd(dh * (gpre * (1.0 + sc)), x1n, r2)
        ry = _rstd(yv)
        yn = yv * ry
        dn1 = dx1 * gate
        dy = _norm_bwd(dn1 * gpost, yn, ry)
        sums = (_colsum(dh), _colsum(t) * gpre, _colsum(t) * (1.0 + sc), _colsum(dx1 * (yn * gpost)), _colsum(dn1 * yn))
        return (dx1, dy), sums
    dx1, dy, d_shift_ffn, d_scale_ffn, d_g_pre_ffn, d_gate_mix, d_g_post_mix = _mm_epi(
        dgu, w["w_ffn_in"], "nt", D, mid, "ffn_in_dx", 512, rows=[(x1, D), (dx2, D), (y, D)],
        vecs=[g_pre_ffn, scale_ffn, gate_mix, g_post_mix], outs=[(D, D, F32), (D, D, BF16)], sums=[D] * 5)

    dw["w_out"] = _mm(merged, dy, "tn", BF16, "out_proj_dw")

    def merge_bwd(dm, gf, gm, pa_, pb_):
        sf, sm = _sigmoid(gf), _sigmoid(gm)
        return (dm * sf, dm * sm, dm * pa_ * (sf * (1.0 - sf)), dm * pb_ * (sm * (1.0 - sm))), ()
    dpa, dpb, dgf, dgm = _mm_epi(dy, w["w_out"], "nt", 512, merge_bwd, "out_proj_dx", 1024,
                                 rows=[(proj_a, 512, 2), (proj_a, 512, 4), (pa, 512), (pb, 512)],
                                 outs=[(D, 512, BF16)] * 4)
    do_a = _mm(dpa, w["w_pf"], "nt", BF16, "proj_fox_dx")
    do_b = _mm(dpb, w["w_pm"], "nt", BF16, "proj_mla_dx")
    dw["w_pf"] = _mm(o_a, dpa, "tn", BF16, "proj_fox_dw")
    dw["w_pm"] = _mm(o_b, dpb, "tn", BF16, "proj_mla_dw")
    bias = bias + send({n: dw.pop(n) for n in ("w_out", "w_pf", "w_pm")})[0, 0]

    sc_a, sc_b = 1.0 / math.sqrt(HEAD_DIM), 1.0 / math.sqrt(64 + ROPE_DIM)
    dq_a, dk_a, dv_a, dbias = _attn_grad(qkv, 0, qkv, 4, qkv, 8, do_a, lse_a, LANES, sc_a, bias, BF16, "fox_attn_bwd")
    dq_m, dk_m, dv_m = _attn_grad(q_m, 0, k_m, 0, v_m, 0, do_b, lse_b, 2 * LANES, sc_b, None, BF16, "mla_attn_bwd")

    def mla_rope_bwd(dq, dk, c_, a_, b_):
        dq, dk = dq.astype(F32), dk.astype(F32)
        lane = lax.broadcasted_iota(jnp.int32, (1, LANES), 1)
        dqs = [_rope_t(dq[:, hd * LANES:(hd + 1) * LANES], c_, a_, b_) for hd in range(HEADS)]
        dkpe = dk[:, 0:LANES]
        for hd in range(1, HEADS):
            dkpe = dkpe + dk[:, hd * LANES:(hd + 1) * LANES]
        dkpe = jnp.where((lane >= 64) & (lane < 96), dkpe, 0.0)
        dkr = jnp.where((lane >= 64) & (lane < 96), _rope_t(dkpe, c_, a_, b_), 0.0)
        return (jnp.concatenate(dqs, axis=1), dk, dkr), ()
    dqb, dkb, dkr = _rowwise(mla_rope_bwd, [(dq_m, D, 0), (dk_m, D, 0), (ct, LANES, 0), (sa, LANES, 0), (sb, LANES, 0)],
                             [], [(D, BF16), (D, BF16), (LANES, F32)], [], "mla_rope_bwd")
    def lora_q_bwd(dq, cq, gq):
        rq = _rstd(cq)
        cqh = cq * rq
        return (_norm_bwd(dq * gq, cqh, rq),), (_colsum(dq * cqh),)
    dcq, d_g_q = _mm_epi(dqb, w["w_uq"], "nt", Q_LORA, lora_q_bwd, "mla_uq_dx", 512, rows=[(proj_a, Q_LORA, 0)],
                         vecs=[g_q], outs=[(Q_LORA, Q_LORA, BF16)], sums=[Q_LORA])

    def lora_kv_bwd(dv_part, dk_part, ckv, gkv):
        dkv = dv_part + dk_part
        rk = _rstd(ckv)
        ckh = ckv * rk
        return (_norm_bwd(dkv * gkv, ckh, rk),), (_colsum(dkv * ckh),)
    dckv, d_g_kv = _mm_epi(dv_m, w["w_v"], "nt", KV_LORA, lora_kv_bwd, "mla_uv_dx", 1024,
                           rows=[(_mm(dkb, w["w_k"], "nt", F32, "mla_uk_dx"), KV_LORA), (proj_a, KV_LORA, 3)],
                           vecs=[g_kv], outs=[(KV_LORA, KV_LORA, BF16)], sums=[KV_LORA])

    dzt, d_bf = _fox_gates_bwd(dbias.reshape(HEADS, S), zt, bf)
    dmisc = (dkr + jnp.pad(jnp.transpose(dzt), ((0, 0), (0, LANES - HEADS)))).astype(BF16)
    dproj_a = jnp.concatenate([dcq, dckv, dgf, dgm, dmisc], axis=1)
    dqkv = jnp.concatenate([dq_a, dk_a, dv_a], axis=1)
    dw["w_a"] = _mm(h, dproj_a, "tn", BF16, "in_proj_a_dw")
    dw["w_b"] = _mm(h, dqkv, "tn", BF16, "in_proj_b_dw")
    tok = send(dw, True)
    dh_a = _mm(dproj_a, w["w_a"], "nt", F32, "in_proj_a_dx", dep=tok)
    tok = relay(dh_a)
    tok = send({"w_uq": _mm(cqn, dqb, "tn", BF16, "mla_uq_dw", dep=tok),
                "w_k": _mm(ckvn, dkb, "tn", BF16, "mla_uk_dw", dep=tok),
                "w_v": _mm(ckvn, dv_m, "tn", BF16, "mla_uv_dw", dep=tok)}, late=True)
    g_pre_mix = g_pre_mix + tok[0, 0]

    def first(dh_b, dh_a, xv, dx1v, gpre, sc):
        dhv = dh_b + dh_a
        r = _rstd(xv)
        xn = xv * r
        t = dhv * xn
        dx = dx1v + _norm_bwd(dhv * (gpre * (1.0 + sc)), xn, r)
        return (dx,), (_colsum(dhv), _colsum(t) * gpre, _colsum(t) * (1.0 + sc))
    grad_x, d_shift_mix, d_scale_mix, d_g_pre_mix = _mm_epi(
        dqkv, w["w_b"], "nt", D, first, "in_proj_b_dx", 512,
        rows=[(dh_a, D), (x, D), (dx1, D)],
        vecs=[g_pre_mix, scale_mix], outs=[(D, D, F32)], sums=[D] * 3)

    dmod = jnp.concatenate([d_shift_mix, d_scale_mix, d_gate_mix, d_shift_ffn, d_scale_ffn, d_gate_ffn], axis=1)
    small = dict(dmod=dmod, g_pre_mix=d_g_pre_mix, g_post_mix=d_g_post_mix, g_pre_ffn=d_g_pre_ffn,
                 g_post_ffn=d_g_post_ffn, g_q_lora=d_g_q, g_kv_lora=d_g_kv,
                 b_forget=jnp.pad(jnp.transpose(d_bf), ((0, 0), (0, LANES - HEADS))), err=err_cols)
    return grad_x, small


SMALL_ORDER = ("dmod", "g_pre_mix", "g_post_mix", "g_pre_ffn", "g_post_ffn", "g_q_lora", "g_kv_lora", "b_forget", "err")
SMALL_PARAM = {"dmod": "b_ada"}
MATRICES = ("w_in", "w_uq", "w_ukv", "w_proj_fox", "w_proj_mla", "w_out", "w_ffn_in", "w_ffn_out")
WEIGHTS = ("w_ada", "b_ada", "g_pre_mix", "g_post_mix", "g_pre_ffn", "g_post_ffn", "w_in", "b_forget", "g_q_lora",
           "w_uq", "g_kv_lora", "w_ukv", "w_proj_fox", "w_proj_mla", "w_out", "w_ffn_in", "w_ffn_out")


def kernel(x, c, positions, w_ada, b_ada, g_pre_mix, g_post_mix, g_pre_ffn, g_post_ffn, w_in, b_forget, g_q_lora, w_uq, g_kv_lora, w_ukv, w_proj_fox, w_proj_mla, w_out, w_ffn_in, w_ffn_out, loss_target, m_w_ada, m_b_ada, m_g_pre_mix, m_g_post_mix, m_g_pre_ffn, m_g_post_ffn, m_w_in, m_b_forget, m_g_q_lora, m_w_uq, m_g_kv_lora, m_w_ukv, m_w_proj_fox, m_w_proj_mla, m_w_out, m_w_ffn_in, m_w_ffn_out, v_w_ada, v_b_ada, v_g_pre_mix, v_g_post_mix, v_g_pre_ffn, v_g_post_ffn, v_w_in, v_b_forget, v_g_q_lora, v_w_uq, v_g_kv_lora, v_w_ukv, v_w_proj_fox, v_w_proj_mla, v_w_out, v_w_ffn_in, v_w_ffn_out):
    prm = dict(w_ada=w_ada, b_ada=b_ada, g_pre_mix=g_pre_mix, g_post_mix=g_post_mix, g_pre_ffn=g_pre_ffn,
               g_post_ffn=g_post_ffn, w_in=w_in, b_forget=b_forget, g_q_lora=g_q_lora, w_uq=w_uq, g_kv_lora=g_kv_lora,
               w_ukv=w_ukv, w_proj_fox=w_proj_fox, w_proj_mla=w_proj_mla, w_out=w_out, w_ffn_in=w_ffn_in, w_ffn_out=w_ffn_out)
    mom = dict(w_ada=m_w_ada, b_ada=m_b_ada, g_pre_mix=m_g_pre_mix, g_post_mix=m_g_post_mix, g_pre_ffn=m_g_pre_ffn,
               g_post_ffn=m_g_post_ffn, w_in=m_w_in, b_forget=m_b_forget, g_q_lora=m_g_q_lora, w_uq=m_w_uq,
               g_kv_lora=m_g_kv_lora, w_ukv=m_w_ukv, w_proj_fox=m_w_proj_fox, w_proj_mla=m_w_proj_mla, w_out=m_w_out,
               w_ffn_in=m_w_ffn_in, w_ffn_out=m_w_ffn_out)
    var = dict(w_ada=v_w_ada, b_ada=v_b_ada, g_pre_mix=v_g_pre_mix, g_post_mix=v_g_post_mix, g_pre_ffn=v_g_pre_ffn,
               g_post_ffn=v_g_post_ffn, w_in=v_w_in, b_forget=v_b_forget, g_q_lora=v_g_q_lora, w_uq=v_w_uq,
               g_kv_lora=v_g_kv_lora, w_ukv=v_w_ukv, w_proj_fox=v_w_proj_fox, w_proj_mla=v_w_proj_mla, w_out=v_w_out,
               w_ffn_in=v_w_ffn_in, w_ffn_out=v_w_ffn_out)
    me = _flat(*_coords())
    slot = jnp.reshape(me, (1,)).astype(jnp.int32)

    own = {n: prm[n][0].astype(BF16) for n in MATRICES}
    no_dep = jnp.zeros((8, LANES), F32)
    (st_c, st_in), tok = _async_start([[c], [own["w_in"]]], ["gather", "spread"], no_dep, "gather_in_start")
    (c_own,), (c_land,) = _async_wait(st_c, tok, "gather_c_wait")
    c_all = _with_own(c_land, c_own, me).reshape(N_DEV, D)
    ada_cols = w_ada.shape[2]
    b_cols = lax.dynamic_slice(b_ada, (0, me * ada_cols), (1, ada_cols))
    mod_cols, silu_c = _mod_part(c_all, w_ada[0], b_cols)
    (mod_all,) = _all_gather([mod_cols], "gather_mod")

    (w_in_own,), (w_in_land,) = _async_wait(st_in, mod_all, "gather_in_wait")
    (st_in,), tok = _async_start([[w_in_land]], "forward", no_dep, "gather_in_forward")
    _, (w_in_land,) = _async_wait(st_in, tok, "gather_in_forward_wait")
    w = _prepare_weights({"w_in": w_in_land}, {"w_in": w_in_own}, slot)
    later = dict(lora=("w_uq", "w_ukv"), proj=("w_proj_fox", "w_proj_mla", "w_out"), ffn=("w_ffn_in", "w_ffn_out"))
    states, tok = _async_start([[own[n] for n in names] for names in later.values()], ["gather", "spread", "spread"],
                               w["w_b"], "gather_rest_start")
    gather_state = dict(zip(later, states))
    own_thru = {}

    def wts(group, after):
        if group.startswith("relay_"):
            name = group[len("relay_"):]
            own_thru[name], lands = _async_wait(gather_state[name], after, "gather_" + name + "_wait")
            (gather_state[name],), t = _async_start([lands], "forward", no_dep, "gather_" + name + "_forward")
            return {"tok": t}
        srcs, lands = _async_wait(gather_state[group], after, "gather_" + group + "_landed")
        srcs = own_thru.get(group, srcs)
        return _prepare_weights(dict(zip(later[group], lands)), dict(zip(later[group], srcs)), slot)

    sent, late_sent, last = [], [], {}

    def send(grads, final=False, late=False):
        shards = _shard_grads(grads)
        names = list(shards)
        (state,), t = _async_start([[shards[n] for n in names]], "pair" if final else "exchange", no_dep,
                                   "exchange_" + names[0] + "_start")
        if final:
            last.update(names=names, state=state)
        else:
            (late_sent if late else sent).append((names, state))
        return t

    def relay(after):
        srcs, lands = _async_wait(last["state"], after, "exchange_pair_wait")
        core = jnp.reshape(lax.axis_index("c"), (1,)).astype(jnp.int32)
        sums = [_add_sibling(src, land, core) for src, land in zip(srcs, lands)]
        (last["state"],), t = _async_start([sums], "chips", no_dep, "exchange_chips_start")
        return t

    mod = lax.dynamic_index_in_dim(mod_all, me, axis=1, keepdims=False).reshape(1, 6 * D) + tok[0, 0]

    vec = dict(g_pre_mix=g_pre_mix, g_post_mix=g_post_mix, g_pre_ffn=g_pre_ffn, g_post_ffn=g_post_ffn,
               g_q_lora=g_q_lora, g_kv_lora=g_kv_lora, b_forget=b_forget)
    pos = positions.astype(F32).reshape(S, 1)
    grad_x, small = _fwd_bwd(x[0], pos, mod, loss_target[0], w, vec, wts, send, relay)

    bundle = jnp.concatenate([small[n] for n in SMALL_ORDER], axis=1)
    (small_state,), tok = _async_start([[bundle]], "gather", jnp.zeros((8, LANES), F32), "gather_small_start")

    out = {}
    swap = lambda a: jnp.swapaxes(a, -1, -2)

    def update(n, land, src, sl):
        if n != "w_ffn_in":
            out[n] = _adamw(prm[n][0], mom[n][0], var[n][0], land, "adamw_" + n, src, sl)
            return out[n][0]
        res = _adamw(swap(prm[n][0]), swap(mom[n][0]), swap(var[n][0]), land, "adamw_" + n, src, sl)
        out[n] = tuple(swap(t) for t in res)
        return res[0]

    after = tok
    for names, state in sent:
        srcs, lands = _async_wait(state, after, "exchange_" + names[0] + "_wait")
        for n, src, land in zip(names, srcs, lands):
            after = update(n, land, src, slot)
    srcs, lands = _async_wait(last["state"], after, "exchange_chips_wait")
    for n, src, land in zip(last["names"], srcs, lands):
        after = update(n, land, src, slot // 2)
    for names, state in late_sent:
        srcs, lands = _async_wait(state, after, "exchange_" + names[0] + "_wait")
        for n, src, land in zip(names, srcs, lands):
            after = update(n, land, src, slot)

    (own_bundle,), (bundle_all,) = _async_wait(small_state, after, "gather_small_wait")
    bundle_all = _with_own(bundle_all, own_bundle, me)
    dmod_all = bundle_all[:, 0, :6 * D]
    dm_cols = lax.dynamic_slice(dmod_all, (0, me * ada_cols), (N_DEV, ada_cols))
    out["w_ada"] = _adamw_w_ada(w_ada[0], m_w_ada[0], v_w_ada[0], jnp.transpose(silu_c), dm_cols)

    offsets, off = {}, 0
    for n in SMALL_ORDER:
        offsets[n] = off
        off += small[n].shape[1]
    names = [SMALL_PARAM.get(n, n) for n in SMALL_ORDER if n != "err"]
    results, err = _adamw_rows(bundle_all, [offsets[n] for n in SMALL_ORDER if n != "err"],
                               [prm[n] for n in names], [mom[n] for n in names], [var[n] for n in names],
                               offsets["err"], D)
    out.update(zip(names, results))
    loss = 0.5 * jnp.sum(err) / D

    res = [loss, grad_x[None]]
    for kind in range(4):
        for n in WEIGHTS:
            t = out[n][kind]
            res.append(t[None] if prm[n].ndim == 3 else t)
    return tuple(res)
```

```python
import functools
import math

import jax
import jax.numpy as jnp
from jax import lax
from jax.experimental import pallas as pl
from jax.experimental.pallas import tpu as pltpu

F32 = jnp.float32
BF16 = jnp.bfloat16

N_DEV = 8
S = 2048
D = 1024
D_FF = 2816
HEADS = 8
HEAD_DIM = 64
Q_LORA = 768
KV_LORA = 256
ROPE_DIM = 32
ROPE_THETA = 10000.0
NORM_EPS = 1e-6
LANES = 128
VMEM_LIMIT = 56 * 1024 * 1024

ADAM_LR = 0.001
ADAM_B1 = 0.9
ADAM_B2 = 0.999
ADAM_EPS = 1e-08
ADAM_WD = 0.01
ADAM_STEP = 10

ATT_T = 256
LOG2E = 1.4426950408889634
N_ATT = S // ATT_T

NN = (((1,), (0,)), ((), ()))
NT = (((1,), (1,)), ((), ()))
TN = (((0,), (0,)), ((), ()))
MESH = pl.DeviceIdType.MESH


def _params(sem=None):
    return pltpu.CompilerParams(dimension_semantics=sem, vmem_limit_bytes=VMEM_LIMIT)


def _pick(n, cap):
    best = None
    for t in range(LANES, cap + 1, LANES):
        if n % t == 0:
            best = t
    return best if best is not None else n


def _mm(a, b, mode, out_dtype, name, acc=None, dep=None):
    if mode == "nn":
        (m, k), (k2, n), dn = a.shape, b.shape, NN
    elif mode == "nt":
        (m, k), (n, k2), dn = a.shape, b.shape, NT
    else:
        (k, m), (k2, n), dn = a.shape, b.shape, TN
    assert k == k2, (a.shape, b.shape, mode)
    tn = _pick(n, 1024)
    tm = _pick(m, 1536)
    osz = jnp.dtype(out_dtype).itemsize

    def need(tm_):
        blk = tm_ * k * 2 + tn * k * 2 + tm_ * tn * osz + (tm_ * tn * 4 if acc is not None else 0)
        return 2 * blk + tm_ * tn * 4
    while need(tm) > 36 * 1024 * 1024 and tm % 256 == 0:
        tm //= 2

    def body(*refs):
        a_ref, b_ref, o_ref = refs[0], refs[1], refs[-1]
        r = lax.dot_general(a_ref[...], b_ref[...], dn, preferred_element_type=F32)
        if acc is not None:
            r = r + refs[2][...]
        o_ref[...] = r.astype(o_ref.dtype)

    if mode == "tn":
        a_spec = pl.BlockSpec((k, tm), lambda i, j: (0, i))
    else:
        a_spec = pl.BlockSpec((tm, k), lambda i, j: (i, 0))
    if mode == "nt":
        b_spec = pl.BlockSpec((tn, k), lambda i, j: (j, 0))
    else:
        b_spec = pl.BlockSpec((k, tn), lambda i, j: (0, j))
    o_spec = pl.BlockSpec((tm, tn), lambda i, j: (i, j))
    in_specs = [a_spec, b_spec] + ([o_spec] if acc is not None else [])
    in_specs += [pl.BlockSpec(memory_space=pl.ANY)] if dep is not None else []
    args = (a, b) + ((acc,) if acc is not None else ()) + ((dep,) if dep is not None else ())
    return pl.pallas_call(
        body, name=name, grid=(m // tm, n // tn),
        in_specs=in_specs, out_specs=o_spec,
        out_shape=jax.ShapeDtypeStruct((m, n), out_dtype),
        compiler_params=_params(("parallel", "parallel")),
    )(*args)


def _offsets(widths):
    return [sum(widths[:p]) for p in range(len(widths))]


def _mm_cat(a, b, mode, out_dtype, name, dep=None):
    pieces = a if mode == "nt" else b
    widths = [p.shape[1] for p in pieces]
    offs = _offsets(widths)
    assert all(w_ % LANES == 0 for w_ in widths)
    resident = dict(pipeline_mode=pl.Buffered(1))
    if mode == "nt":
        m, (n, k) = pieces[0].shape[0], b.shape
        assert k == sum(widths)
        tm = _pick(m, 512)

        def body(*refs):
            b_ref, o_ref = refs[len(pieces)], refs[-1]
            r = None
            for p_ref, off, w_ in zip(refs, offs, widths):
                t = lax.dot_general(p_ref[...], b_ref[:, off:off + w_], NT, preferred_element_type=F32)
                r = t if r is None else r + t
            o_ref[...] = r.astype(o_ref.dtype)
        in_specs = [pl.BlockSpec((tm, w_), lambda i: (i, 0)) for w_ in widths]
        in_specs.append(pl.BlockSpec((n, k), lambda i: (0, 0), **resident))
        args = (*pieces, b)
    else:
        assert mode == "tn"
        (k, m), n = a.shape, sum(widths)
        tm = _pick(m, 512)

        def body(*refs):
            a_ref, o_ref = refs[0], refs[-1]
            for p_ref, off, w_ in zip(refs[1:], offs, widths):
                o_ref[:, off:off + w_] = lax.dot_general(
                    a_ref[...], p_ref[...], TN, preferred_element_type=F32).astype(o_ref.dtype)
        in_specs = [pl.BlockSpec((k, tm), lambda i: (0, i))]
        in_specs += [pl.BlockSpec((k, w_), lambda i: (0, 0), **resident) for w_ in widths]
        args = (a, *pieces)
    if dep is not None:
        in_specs.append(pl.BlockSpec(memory_space=pl.ANY))
        args += (dep,)
    return pl.pallas_call(
        body, name=name, grid=(m // tm,),
        in_specs=in_specs, out_specs=pl.BlockSpec((tm, n), lambda i: (i, 0)),
        out_shape=jax.ShapeDtypeStruct((m, n), out_dtype),
        compiler_params=_params(("parallel",)),
    )(*args)


def _mm_epi(a, b, mode, tnb, epi, name, tm, rows=(), vecs=(), outs=(), sums=(), pro=None):
    pieces = list(a) if isinstance(a, (list, tuple)) else [a]
    assert len(pieces) == 1 or pro is None
    widths = [p.shape[1] for p in pieces]
    offs = _offsets(widths)
    m = pieces[0].shape[0]
    k, nb = (b.shape if mode == "nn" else b.shape[::-1])
    dn = NN if mode == "nn" else NT
    pro_fn, pro_vecs, a_off = pro if pro is not None else (None, (), 0)
    n_in = 2 + len(rows) + len(vecs)
    n_all = n_in + len(pro_vecs)
    sub = min(tm, 256)

    def body(*refs):
        a_refs, refs = refs[:len(pieces)], refs[len(pieces) - 1:]
        if pro is not None:
            a_out, a_scr = refs[-2:]
            refs = refs[:-2]

            @pl.when(pl.program_id(1) == 0)
            def _():
                a_scr[...] = pro_fn(refs[0][...], *[x[...] for x in refs[n_in:n_all]]).astype(BF16)
                a_out[...] = a_scr[...]
            a_ref = a_scr
        else:
            a_ref = refs[0]
        o_refs = refs[n_all:n_all + len(outs)]
        s_refs = refs[n_all + len(outs):]
        if sums:
            @pl.when((pl.program_id(0) == 0) & (pl.program_id(1) == 0))
            def _():
                for s_ref in s_refs:
                    s_ref[...] = jnp.zeros(s_ref.shape, F32)
        for c in range(tm // sub):
            rs = slice(c * sub, (c + 1) * sub)
            if len(pieces) == 1:
                r = lax.dot_general(a_ref[rs, :], refs[1][...], dn, preferred_element_type=F32)
            else:
                r = None
                for p_ref, off, w_ in zip(a_refs, offs, widths):
                    b_part = refs[1][off:off + w_, :] if mode == "nn" else refs[1][:, off:off + w_]
                    t = lax.dot_general(p_ref[rs, :], b_part, dn, preferred_element_type=F32)
                    r = t if r is None else r + t
            o_vals, s_vals = epi(r, *[x[rs, :] for x in refs[2:2 + len(rows)]], *[x[...] for x in refs[2 + len(rows):n_in]])
            assert len(o_vals) == len(o_refs) and len(s_vals) == len(s_refs)
            for o_ref, val in zip(o_refs, o_vals):
                o_ref[rs, :] = val.astype(o_ref.dtype)
            for s_ref, val in zip(s_refs, s_vals):
                s_ref[...] += val

    once = dict(pipeline_mode=pl.Buffered(1)) if nb == tnb else {}
    if mode == "nn":
        b_spec = pl.BlockSpec((k, tnb), lambda i, j: (0, j), **once)
    else:
        b_spec = pl.BlockSpec((tnb, k), lambda i, j: (j, 0), **once)
    if len(pieces) == 1:
        in_specs = [pl.BlockSpec((tm, k), lambda i, j: (i, a_off)), b_spec]
    else:
        assert sum(widths) == k and all(w_ % LANES == 0 for w_ in widths)
        in_specs = [pl.BlockSpec((tm, w_), lambda i, j: (i, 0)) for w_ in widths] + [b_spec]
    rows = [tuple(r) + (0,) * (3 - len(r)) for r in rows]
    in_specs += [pl.BlockSpec((tm, w), functools.partial(lambda i, j, off: (i, j + off), off=off)) for _, w, off in rows]
    in_specs += [pl.BlockSpec(v.shape, lambda i, j: (0, 0)) for v in list(vecs) + list(pro_vecs)]
    out_specs = [pl.BlockSpec((tm, w), lambda i, j: (i, j)) for _, w, _ in outs]
    out_specs += [pl.BlockSpec((1, w), lambda i, j: (0, 0)) for w in sums]
    out_shape = [jax.ShapeDtypeStruct((m, full), dt) for full, _, dt in outs]
    out_shape += [jax.ShapeDtypeStruct((1, w), F32) for w in sums]
    if pro is not None:
        out_specs.append(pl.BlockSpec((tm, k), lambda i, j: (i, 0)))
        out_shape.append(jax.ShapeDtypeStruct((m, k), BF16))
    return pl.pallas_call(
        body, name=name, grid=(m // tm, nb // tnb),
        in_specs=in_specs, out_specs=out_specs, out_shape=out_shape,
        scratch_shapes=[pltpu.VMEM((tm, k), BF16)] if pro is not None else [],
        compiler_params=_params(("arbitrary", "arbitrary") if sums else ("parallel", "arbitrary" if pro is not None else "parallel")),
    )(*pieces, b, *[r[0] for r in rows], *vecs, *pro_vecs)


def _rowwise(fn, row_ins, vec_ins, row_outs, sum_outs, name, tm=512):
    n_in = len(row_ins) + len(vec_ins)
    n_o = len(row_outs)
    rows = row_ins[0][0].shape[0]

    def body(*refs):
        vals = [r[...] for r in refs[:n_in]]
        outs = refs[n_in:]
        ro, so = fn(*vals)
        assert len(ro) == n_o and len(so) == len(sum_outs)
        for r, v in zip(outs[:n_o], ro):
            r[...] = v.astype(r.dtype)
        if sum_outs:
            @pl.when(pl.program_id(0) == 0)
            def _():
                for r in outs[n_o:]:
                    r[...] = jnp.zeros(r.shape, F32)
            for r, v in zip(outs[n_o:], so):
                r[...] += v

    in_specs = [pl.BlockSpec((tm, w), functools.partial(lambda i, b: (i, b), b=b)) for _, w, b in row_ins]
    in_specs += [pl.BlockSpec(v.shape, lambda i: (0, 0)) for v in vec_ins]
    out_specs = [pl.BlockSpec((tm, w), lambda i: (i, 0)) for w, _ in row_outs]
    out_specs += [pl.BlockSpec((1, w), lambda i: (0, 0)) for w in sum_outs]
    out_shape = [jax.ShapeDtypeStruct((rows, w), dt) for w, dt in row_outs]
    out_shape += [jax.ShapeDtypeStruct((1, w), F32) for w in sum_outs]
    return pl.pallas_call(
        body, name=name, grid=(rows // tm,),
        in_specs=in_specs, out_specs=out_specs, out_shape=out_shape,
        compiler_params=_params(("arbitrary",)),
    )(*[a for a, _, _ in row_ins], *vec_ins)


def _sigmoid(x):
    return 1.0 / (1.0 + jnp.exp(-x))


def _rstd(x):
    return lax.rsqrt(jnp.mean(x * x, axis=-1, keepdims=True) + NORM_EPS)


def _norm_bwd(dyn, xn, r):
    return r * (dyn - xn * jnp.mean(dyn * xn, axis=-1, keepdims=True))


def _colsum(x):
    return jnp.sum(x, axis=0, keepdims=True)


def _rope_tables(pos, invf):
    def fn(p, f):
        lane = lax.broadcasted_iota(jnp.int32, (1, LANES), 1)
        ang = p * f
        cs, sn = jnp.cos(ang), jnp.sin(ang)
        rot = (lane >= 64) & (lane < 96)
        ct = jnp.where(lane < 64, 1.0, jnp.where(rot, cs, 0.0))
        sa = jnp.where((lane >= 64) & (lane < 80), -sn, 0.0)
        sb = jnp.where((lane >= 80) & (lane < 96), sn, 0.0)
        return (ct, sa, sb), ()
    return _rowwise(fn, [(pos, 1, 0)], [invf], [(LANES, F32)] * 3, [], "rope_tables")


def _rope(x, ct, sa, sb):
    return x * ct + pltpu.roll(x, LANES - 16, 1) * sa + pltpu.roll(x, 16, 1) * sb


def _rope_t(x, ct, sa, sb):
    return x * ct - pltpu.roll(x, LANES - 16, 1) * sa - pltpu.roll(x, 16, 1) * sb


def _head_mask(width, hh):
    lane = lax.broadcasted_iota(jnp.int32, (1, width), 1)
    half = width // 2
    return (lane >= hh * half) & (lane < (hh + 1) * half)


ATT_PP = 2
ATT_CHAINS = [(a, hh) for a in range(ATT_PP) for hh in range(2)]
ATT_G = HEADS // (2 * ATT_PP)


def _pair(ref_or_val, a, width, rows=slice(None)):
    return ref_or_val[rows, a * width:(a + 1) * width]


def _head_cols(ref, a, hh, dkp, rows=slice(None)):
    if dkp == 2 * LANES:
        return ref[rows, a * dkp + hh * LANES:a * dkp + (hh + 1) * LANES]
    blk = _pair(ref, a, dkp, rows)
    return jnp.where(_head_mask(dkp, hh), blk, jnp.zeros_like(blk))


def _attn_fwd(q, qo, k, ko, v, vo, dkp, scale, bias, name):
    T = ATT_T
    assert qo % ATT_PP == 0 and ko % ATT_PP == 0 and vo % ATT_PP == 0
    qo, ko, vo = qo // ATT_PP, ko // ATT_PP, vo // ATT_PP
    split = dkp == 2 * LANES

    def body(*refs):
        if bias is not None:
            q_ref, k_ref, v_ref, b_ref, o_ref, lse_ref, s_scr = refs
        else:
            q_ref, k_ref, v_ref, o_ref, lse_ref, s_scr = refs
        i = pl.program_id(1)
        row = lax.broadcasted_iota(jnp.int32, (T, T), 0)
        col = lax.broadcasted_iota(jnp.int32, (T, T), 1)
        qms = [_head_cols(q_ref, a, hh, dkp) for a, hh in ATT_CHAINS]

        def k_of(a, hh, ks):
            return _head_cols(k_ref, a, hh, dkp, ks) if split else _pair(k_ref, a, dkp, ks)

        def fold(t):
            return [t[:, c * LANES:(c + 1) * LANES] for c in range(T // LANES)]

        def run(nt):
            mls = [jnp.full((T, LANES), -jnp.inf, F32) for _ in ATT_CHAINS]
            for j in range(nt):
                ks = slice(j * T, (j + 1) * T)
                for ci, (a, hh) in enumerate(ATT_CHAINS):
                    s = lax.dot_general(qms[ci], k_of(a, hh, ks), NT, preferred_element_type=F32) * (scale * LOG2E)
                    if bias is not None:
                        s = s + b_ref[2 * a + hh, j] * LOG2E
                    if j == nt - 1:
                        s = jnp.where(row >= col, s, -jnp.inf)
                    s_scr[ci, j] = s
                    for part in fold(s):
                        mls[ci] = jnp.maximum(mls[ci], part)
            ms = [jnp.max(ml, axis=1, keepdims=True) for ml in mls]
            mbs = [jnp.broadcast_to(m, (T, LANES)) for m in ms]
            for a in range(ATT_PP):
                ls = [jnp.zeros((T, LANES), F32) for _ in range(2)]
                ps, vms = [], []
                for j in range(nt):
                    vb = _pair(v_ref, a, LANES, slice(j * T, (j + 1) * T))
                    for hh in range(2):
                        parts = [jnp.exp2(part - mbs[2 * a + hh]) for part in fold(s_scr[2 * a + hh, j])]
                        for part in parts:
                            ls[hh] = ls[hh] + part
                        ps.append(jnp.concatenate(parts, axis=1).astype(BF16))
                        vms.append(jnp.where(_head_mask(LANES, hh), vb, jnp.zeros_like(vb)))
                acc = lax.dot_general(jnp.concatenate(ps, axis=1), jnp.concatenate(vms, axis=0), NN,
                                      preferred_element_type=F32)
                l0, l1 = [jnp.sum(l, axis=1, keepdims=True) for l in ls]
                lse_ref[2 * a] = ms[2 * a] + jnp.log2(l0)
                lse_ref[2 * a + 1] = ms[2 * a + 1] + jnp.log2(l1)
                inv = jnp.where(_head_mask(LANES, 0), 1.0 / l0, 1.0 / l1)
                o_ref[:, a * LANES:(a + 1) * LANES] = (acc * inv).astype(o_ref.dtype)

        for nt in range(1, N_ATT + 1):
            pl.when(i == nt - 1)(functools.partial(run, nt))

    in_specs = [
        pl.BlockSpec((T, ATT_PP * dkp), lambda g, i: (i, qo + g)),
        pl.BlockSpec((S, ATT_PP * dkp), lambda g, i: (0, ko + g)),
        pl.BlockSpec((S, ATT_PP * LANES), lambda g, i: (0, vo + g)),
    ]
    args = [q, k, v]
    if bias is not None:
        in_specs.append(pl.BlockSpec((2 * ATT_PP, N_ATT, 1, T), lambda g, i: (g, 0, 0, 0)))
        args.append(bias)
    return pl.pallas_call(
        body, name=name, grid=(ATT_G, N_ATT),
        in_specs=in_specs,
        out_specs=[pl.BlockSpec((T, ATT_PP * LANES), lambda g, i: (i, g)),
                   pl.BlockSpec((2 * ATT_PP, T, 1), lambda g, i: (g, i, 0))],
        out_shape=[jax.ShapeDtypeStruct((S, HEADS * HEAD_DIM), BF16),
                   jax.ShapeDtypeStruct((HEADS, S, 1), F32)],
        scratch_shapes=[pltpu.VMEM((len(ATT_CHAINS), N_ATT, T, T), F32)],
        compiler_params=_params(("parallel", "arbitrary")),
    )(*args)


def _attn_grad(q, qo, k, ko, v, vo, do, lse, dkp, scale, bias, qk_dtype, name):
    T = ATT_T
    has_b = bias is not None
    qo, ko, vo = qo // ATT_PP, ko // ATT_PP, vo // ATT_PP
    n_ch = len(ATT_CHAINS)
    split = dkp == 2 * LANES

    def body(*refs):
        q_ref, k_ref, v_ref, do_ref, lse_ref = refs[:5]
        refs = refs[5:]
        if has_b:
            b_ref, refs = refs[0], refs[1:]
        dq_ref, dk_ref, dv_ref = refs[:3]
        refs = refs[3:]
        if has_b:
            db_ref, refs = refs[0], refs[1:]
        p_scr, dp_scr, dk_acc, dv_acc = refs[:4]
        db_acc = refs[4] if has_b else None
        i = pl.program_id(1)

        @pl.when(i == 0)
        def _():
            dk_acc[...] = jnp.zeros(dk_acc.shape, F32)
            dv_acc[...] = jnp.zeros(dv_acc.shape, F32)
            if has_b:
                db_acc[...] = jnp.zeros(db_acc.shape, F32)

        row = lax.broadcasted_iota(jnp.int32, (T, T), 0)
        col = lax.broadcasted_iota(jnp.int32, (T, T), 1)

        def fold(t):
            return [t[:, c * LANES:(c + 1) * LANES] for c in range(T // LANES)]

        qms, doms, lses = [], [], []
        for a, hh in ATT_CHAINS:
            dob = _pair(do_ref, a, LANES)
            qms.append(_head_cols(q_ref, a, hh, dkp))
            doms.append(jnp.where(_head_mask(LANES, hh), dob, jnp.zeros_like(dob)))
            lses.append(lse_ref[2 * a + hh])

        def k_of(a, hh, ks):
            return _head_cols(k_ref, a, hh, dkp, ks) if split else _pair(k_ref, a, dkp, ks)

        def run(nt):
            dls = [jnp.zeros((T, LANES), F32) for _ in ATT_CHAINS]
            for j in range(nt):
                ks = slice(j * T, (j + 1) * T)
                for ci, (a, hh) in enumerate(ATT_CHAINS):
                    s = lax.dot_general(qms[ci], k_of(a, hh, ks), NT, preferred_element_type=F32) * (scale * LOG2E)
                    if has_b:
                        s = s + b_ref[ci, j] * LOG2E
                    s = s - lses[ci]
                    if j == nt - 1:
                        s = jnp.where(row >= col, s, -jnp.inf)
                    p = jnp.exp2(s)
                    dp = lax.dot_general(doms[ci], _pair(v_ref, a, LANES, ks), NT, preferred_element_type=F32)
                    p_scr[ci, j] = p
                    dp_scr[ci, j] = dp
                    for part in fold(p * dp):
                        dls[ci] = dls[ci] + part
            deltas = [jnp.broadcast_to(jnp.sum(dl, axis=1, keepdims=True), (T, LANES)) for dl in dls]
            for a in range(ATT_PP):
                ds_all, km_all = [], []
                if split:
                    qts = [jnp.transpose(qms[2 * a + hh]) for hh in range(2)]
                else:
                    qm2t = jnp.transpose(jnp.concatenate([qms[2 * a], qms[2 * a + 1]], axis=0))
                dom2t = jnp.transpose(jnp.concatenate([doms[2 * a], doms[2 * a + 1]], axis=0))
                for j in range(nt):
                    ks = slice(j * T, (j + 1) * T)
                    p2, ds2 = [], []
                    for hh in range(2):
                        ci = 2 * a + hh
                        p = p_scr[ci, j]
                        ds = jnp.concatenate([pp * (dd - deltas[ci]) for pp, dd in zip(fold(p), fold(dp_scr[ci, j]))], axis=1)
                        if has_b:
                            db_acc[ci, j] += jnp.sum(ds, axis=0, keepdims=True)
                        p2.append(p.astype(BF16))
                        ds2.append((ds * scale).astype(BF16))
                        if not split:
                            km_all.append(_head_cols(k_ref, a, hh, dkp, ks))
                    dv_acc[a * LANES:(a + 1) * LANES, ks] += lax.dot_general(
                        dom2t, jnp.concatenate(p2, axis=0), NN, preferred_element_type=F32)
                    if split:
                        for hh in range(2):
                            dk_acc[a * dkp + hh * LANES:a * dkp + (hh + 1) * LANES, ks] += lax.dot_general(
                                qts[hh], ds2[hh], NN, preferred_element_type=F32)
                    else:
                        dk_acc[a * dkp:(a + 1) * dkp, ks] += lax.dot_general(
                            qm2t, jnp.concatenate(ds2, axis=0), NN, preferred_element_type=F32)
                    ds_all += ds2
                if split:
                    for hh in range(2):
                        dq = lax.dot_general(jnp.concatenate(ds_all[hh::2], axis=1),
                                             _head_cols(k_ref, a, hh, dkp, slice(0, nt * T)), NN,
                                             preferred_element_type=F32)
                        dq_ref[:, a * dkp + hh * LANES:a * dkp + (hh + 1) * LANES] = dq.astype(dq_ref.dtype)
                else:
                    dq = lax.dot_general(jnp.concatenate(ds_all, axis=1), jnp.concatenate(km_all, axis=0), NN,
                                         preferred_element_type=F32)
                    dq_ref[:, a * dkp:(a + 1) * dkp] = dq.astype(dq_ref.dtype)

        for nt in range(1, N_ATT + 1):
            pl.when(i == nt - 1)(functools.partial(run, nt))

        @pl.when(i == N_ATT - 1)
        def _():
            dk_ref[...] = jnp.transpose(dk_acc[...]).astype(dk_ref.dtype)
            dv_ref[...] = jnp.transpose(dv_acc[...]).astype(dv_ref.dtype)
            if has_b:
                db_ref[...] = db_acc[...]

    in_specs = [
        pl.BlockSpec((T, ATT_PP * dkp), lambda g, i: (i, qo + g)),
        pl.BlockSpec((S, ATT_PP * dkp), lambda g, i: (0, ko + g)),
        pl.BlockSpec((S, ATT_PP * LANES), lambda g, i: (0, vo + g)),
        pl.BlockSpec((T, ATT_PP * LANES), lambda g, i: (i, g)),
        pl.BlockSpec((2 * ATT_PP, T, 1), lambda g, i: (g, i, 0)),
    ]
    args = [q, k, v, do, lse]
    out_specs = [
        pl.BlockSpec((T, ATT_PP * dkp), lambda g, i: (i, g)),
        pl.BlockSpec((S, ATT_PP * dkp), lambda g, i: (0, g)),
        pl.BlockSpec((S, ATT_PP * LANES), lambda g, i: (0, g)),
    ]
    width = (HEADS // 2) * dkp
    out_shape = [
        jax.ShapeDtypeStruct((S, width), qk_dtype),
        jax.ShapeDtypeStruct((S, width), qk_dtype),
        jax.ShapeDtypeStruct((S, HEADS * HEAD_DIM), BF16),
    ]
    scratch = [pltpu.VMEM((n_ch, N_ATT, T, T), F32), pltpu.VMEM((n_ch, N_ATT, T, T), F32),
               pltpu.VMEM((ATT_PP * dkp, S), F32), pltpu.VMEM((ATT_PP * LANES, S), F32)]
    if has_b:
        bspec = pl.BlockSpec((2 * ATT_PP, N_ATT, 1, T), lambda g, i: (g, 0, 0, 0))
        in_specs.append(bspec)
        args.append(bias)
        out_specs.append(bspec)
        out_shape.append(jax.ShapeDtypeStruct((HEADS, N_ATT, 1, T), F32))
        scratch.append(pltpu.VMEM((2 * ATT_PP, N_ATT, 1, T), F32))
    return pl.pallas_call(
        body, name=name, grid=(ATT_G, N_ATT),
        in_specs=in_specs, out_specs=out_specs, out_shape=out_shape, scratch_shapes=scratch,
        compiler_params=_params(("parallel", "arbitrary")),
    )(*args)


def _tri(upper):
    a = lax.broadcasted_iota(jnp.int32, (LANES, LANES), 0)
    b = lax.broadcasted_iota(jnp.int32, (LANES, LANES), 1)
    return jnp.where(a <= b if upper else a >= b, 1.0, 0.0).astype(F32)


def _fox_gates(proj, blk, bf):
    def body(m_ref, b_ref, z_out, o_ref):
        tri = _tri(True)
        carry = jnp.zeros((HEADS, 1), F32)
        for t in range(S // LANES):
            sl = slice(t * LANES, (t + 1) * LANES)
            zt = jnp.transpose(m_ref[sl, :])[:HEADS]
            z_out[:, sl] = zt
            z = zt + b_ref[...]
            logf = jnp.minimum(z, 0.0) - jnp.log(1.0 + jnp.exp(-jnp.abs(z)))
            c = lax.dot_general(logf, tri, NN, preferred_element_type=F32,
                                precision=lax.Precision.HIGHEST) + carry
            o_ref[:, sl] = -c
            carry = c[:, LANES - 1:LANES]

    return pl.pallas_call(
        body, name="fox_gates", grid=(1,),
        in_specs=[pl.BlockSpec((S, LANES), lambda i: (0, blk)), pl.BlockSpec(bf.shape, lambda i: (0, 0))],
        out_specs=[pl.BlockSpec((HEADS, S), lambda i: (0, 0))] * 2,
        out_shape=[jax.ShapeDtypeStruct((HEADS, S), F32)] * 2,
        compiler_params=_params(("arbitrary",)),
    )(proj, bf)


def _fox_gates_bwd(dbias, zt, bf):
    def body(d_ref, z_ref, b_ref, dz_ref, dbf_ref):
        tri = _tri(False)
        carry = jnp.zeros((HEADS, 1), F32)
        tot = jnp.zeros((HEADS, 1), F32)
        for t in reversed(range(S // LANES)):
            sl = slice(t * LANES, (t + 1) * LANES)
            df = -d_ref[:, sl]
            c = lax.dot_general(df, tri, NN, preferred_element_type=F32,
                                precision=lax.Precision.HIGHEST) + carry
            carry = c[:, 0:1]
            z = z_ref[:, sl] + b_ref[...]
            dz = c * _sigmoid(-z)
            dz_ref[:, sl] = dz
            tot = tot + jnp.sum(dz, axis=1, keepdims=True)
        dbf_ref[...] = tot

    return pl.pallas_call(
        body, name="fox_gates_bwd",
        out_shape=[jax.ShapeDtypeStruct((HEADS, S), F32), jax.ShapeDtypeStruct((HEADS, 1), F32)],
        compiler_params=_params(),
    )(dbias, zt, bf)


def _mod_part(c_all, w_ada, b_cols):
    def body(c_ref, w_ref, b_ref, o_ref, s_ref):
        c = c_ref[...]
        sc = c * _sigmoid(c)
        s_ref[...] = sc
        o_ref[...] = lax.dot_general(sc, w_ref[...], NN, preferred_element_type=F32,
                                     precision=lax.Precision.HIGHEST) + b_ref[...]

    return pl.pallas_call(
        body, name="mod_part",
        out_shape=[jax.ShapeDtypeStruct((N_DEV, w_ada.shape[1]), F32), jax.ShapeDtypeStruct(c_all.shape, F32)],
        compiler_params=_params(),
    )(c_all, w_ada, b_cols)


def _adamw_w_ada(w, m, v, sc_t, dm):
    rows, cols = w.shape
    tr = 256

    def body(w_ref, m_ref, v_ref, s_ref, d_ref, g_out, d_out, m_out, v_out):
        g = s_ref[:, 0:1] * d_ref[0:1, :]
        for b in range(1, N_DEV):
            g = g + s_ref[:, b:b + 1] * d_ref[b:b + 1, :]
        g_out[...] = g
        d_out[...], m_out[...], v_out[...] = _adamw_math(w_ref[...], g, m_ref[...], v_ref[...])

    spec = pl.BlockSpec((tr, cols), lambda i: (i, 0))
    return pl.pallas_call(
        body, name="adamw_w_ada", grid=(rows // tr,),
        in_specs=[spec, spec, spec, pl.BlockSpec((tr, N_DEV), lambda i: (i, 0)), pl.BlockSpec(dm.shape, lambda i: (0, 0))],
        out_specs=[spec] * 4, out_shape=[jax.ShapeDtypeStruct((rows, cols), F32)] * 4,
        compiler_params=_params(("parallel",)),
    )(w, m, v, sc_t, dm)


def _adamw(w, m, v, parts, name, own=None, slot=None):
    rows, cols = w.shape
    n = parts.shape[0]
    by_cols = rows % 256 != 0 and cols % 256 == 0
    tr, tc = (rows, 256) if by_cols else ((rows if rows <= 512 else 256), cols)
    tile = (lambda i: (0, i)) if by_cols else (lambda i: (i, 0))

    def body(*refs):
        if own is not None:
            s_ref, refs = refs[0], refs[1:]
            w_ref, m_ref, v_ref, p_ref, o_ref, g_out, d_out, m_out, v_out = refs
            terms = [jnp.where(s_ref[0] == kk, o_ref[0], p_ref[kk]) for kk in range(n)]
        else:
            w_ref, m_ref, v_ref, p_ref, g_out, d_out, m_out, v_out = refs
            terms = [p_ref[kk] for kk in range(n)]
        g = terms[0].astype(F32)
        for term in terms[1:]:
            g = g + term.astype(F32)
        g_out[...] = g
        d_out[...], m_out[...], v_out[...] = _adamw_math(w_ref[...], g, m_ref[...], v_ref[...])

    spec = pl.BlockSpec((tr, tc), lambda i, *_: tile(i))
    in_specs = [spec, spec, spec, pl.BlockSpec((n, tr, tc), lambda i, *_: (0,) + tile(i))]
    out_shape = [jax.ShapeDtypeStruct((rows, cols), F32)] * 4
    grid = (rows // tr if not by_cols else cols // tc,)
    if own is None:
        return pl.pallas_call(
            body, name=name, grid=grid, in_specs=in_specs, out_specs=[spec] * 4, out_shape=out_shape,
            compiler_params=_params(("parallel",)),
        )(w, m, v, parts)
    in_specs.append(pl.BlockSpec((1, tr, tc), lambda i, s: (s[0],) + tile(i)))
    return pl.pallas_call(
        body, name=name, out_shape=out_shape, compiler_params=_params(("parallel",)),
        grid_spec=pltpu.PrefetchScalarGridSpec(num_scalar_prefetch=1, grid=grid, in_specs=in_specs,
                                               out_specs=[spec] * 4),
    )(slot, w, m, v, parts, own)


def _adamw_math(w, g, m, v):
    mm = ADAM_B1 * m + (1.0 - ADAM_B1) * g
    vv = ADAM_B2 * v + (1.0 - ADAM_B2) * (g * g)
    m_hat = mm / (1.0 - ADAM_B1 ** ADAM_STEP)
    v_hat = vv / (1.0 - ADAM_B2 ** ADAM_STEP)
    return -ADAM_LR * (m_hat / (jnp.sqrt(v_hat) + ADAM_EPS) + ADAM_WD * w), mm, vv


def _adamw_rows(bundles, offsets, ws, ms, vs, err_off, err_width):
    k = len(ws)

    def body(*refs):
        b_ref = refs[0]
        w_refs, m_refs, v_refs = refs[1:1 + k], refs[1 + k:1 + 2 * k], refs[1 + 2 * k:1 + 3 * k]
        outs = refs[1 + 3 * k:]
        g_all = b_ref[0]
        for kk in range(1, N_DEV):
            g_all = g_all + b_ref[kk]
        for i in range(k):
            width = w_refs[i].shape[1]
            g = g_all[:, offsets[i]:offsets[i] + width]
            outs[4 * i][...] = g
            outs[4 * i + 1][...], outs[4 * i + 2][...], outs[4 * i + 3][...] = _adamw_math(
                w_refs[i][...], g, m_refs[i][...], v_refs[i][...])
        outs[4 * k][...] = g_all[:, err_off:err_off + err_width]

    out_shape = []
    for w_ in ws:
        out_shape += [jax.ShapeDtypeStruct(w_.shape, F32)] * 4
    out_shape.append(jax.ShapeDtypeStruct((1, err_width), F32))
    res = pl.pallas_call(body, name="adamw_rows", out_shape=out_shape, compiler_params=_params())(bundles, *ws, *ms, *vs)
    return [tuple(res[4 * i:4 * i + 4]) for i in range(k)], res[-1]


def _coords():
    return lax.axis_index("x"), lax.axis_index("y"), lax.axis_index("c")


def _flat(px, py, pc):
    return 4 * px + 2 * py + pc


def _all_gather(arrs, name):
    n = len(arrs)

    def body(*refs):
        ins, outs = refs[:n], refs[n:2 * n]
        send, recv, lsem = refs[2 * n:]
        x, y, c = _coords()
        me, sibling = (x, y, c), (x, y, 1 - c)
        chips = [(1 - x, y), (x, 1 - y), (1 - x, 1 - y)]

        def copy(a, kk, block, to, src=None):
            slot = outs[a].at[_flat(*block)]
            return pltpu.make_async_remote_copy(
                src_ref=slot if src is None else src, dst_ref=slot,
                send_sem=send.at[a, kk], recv_sem=recv.at[a, kk],
                device_id=to, device_id_type=MESH)

        mine = [pltpu.make_async_copy(ins[a], outs[a].at[_flat(*me)], lsem.at[a]) for a in range(n)]
        for cp in mine:
            cp.start()
        first = []
        for a in range(n):
            first.append(copy(a, 0, me, sibling, src=ins[a]))
            first += [copy(a, 1 + j, me, (*chip, c), src=ins[a]) for j, chip in enumerate(chips)]
        for cp in first:
            cp.start()
        passed = []
        for j, chip in enumerate(chips):
            for a in range(n):
                copy(a, 1 + j, (*chip, c), me).wait_recv()
                cp = copy(a, 4 + j, (*chip, c), sibling)
                cp.start()
                passed.append(cp)
        for a in range(n):
            copy(a, 0, sibling, me).wait_recv()
        for j, chip in enumerate(chips):
            for a in range(n):
                copy(a, 4 + j, (*chip, 1 - c), me).wait_recv()
        for cp in first + passed:
            cp.wait_send()
        for cp in mine:
            cp.wait()

    any_spec = pl.BlockSpec(memory_space=pl.ANY)
    return pl.pallas_call(
        body, name=name,
        in_specs=[any_spec] * n, out_specs=[any_spec] * n,
        out_shape=[jax.ShapeDtypeStruct((N_DEV,) + a.shape, a.dtype) for a in arrs],
        scratch_shapes=[pltpu.SemaphoreType.DMA((n, 7)), pltpu.SemaphoreType.DMA((n, 7)),
                        pltpu.SemaphoreType.DMA((n,))],
    )(*arrs)


def _peer_list():
    x, y, c = _coords()
    return [((1 - x if r & 4 else x), (1 - y if r & 2 else y), (1 - c if r & 1 else c)) for r in range(1, N_DEV)]


def _copy_plan(mode, src, land):
    x, y, c = _coords()
    me = _flat(x, y, c)
    if mode == "gather":
        return [(src, land.at[me], peer) for peer in _peer_list()]
    if mode == "exchange":
        return [(src.at[_flat(*peer)], land.at[me], peer) for peer in _peer_list()]
    if mode == "pair":
        return [(src.at[_flat(q // 2, q % 2, 1 - c)], land.at[q], (x, y, 1 - c)) for q in range(N_DEV // 2)]
    chips = [((1 - x if r & 2 else x), (1 - y if r & 1 else y)) for r in range(1, N_DEV // 2)]
    if mode == "chips":
        return [(src.at[2 * qx + qy], land.at[2 * x + y], (qx, qy, c)) for qx, qy in chips]
    if mode == "spread":
        return [(src, land.at[me], (x, y, 1 - c))] + [(src, land.at[me], (qx, qy, c)) for qx, qy in chips]
    assert mode == "forward"
    return [(land.at[_flat(qx, qy, c)], land.at[_flat(qx, qy, c)], (x, y, 1 - c)) for qx, qy in chips]


N_COPIES = dict(gather=N_DEV - 1, exchange=N_DEV - 1, pair=N_DEV // 2, chips=N_DEV // 2 - 1, spread=N_DEV // 2,
                forward=N_DEV // 2 - 1)


def _land_shape(mode, shape):
    return {"gather": (N_DEV,) + shape, "spread": (N_DEV,) + shape, "exchange": shape,
            "pair": (N_DEV // 2,) + shape[1:], "chips": shape}[mode]


HBM_SPEC = pl.BlockSpec(memory_space=pltpu.HBM)
SEM_SPEC = pl.BlockSpec(memory_space=pltpu.SEMAPHORE)
ANY_SPEC = pl.BlockSpec(memory_space=pl.ANY)
SIDE_EFFECT = pltpu.SideEffectType.DATAFLOW_SIDE_EFFECTING


def _async_start(groups, modes, after, name):
    modes = [modes] * len(groups) if isinstance(modes, str) else list(modes)
    arrs = [(a, m) for g, m in zip(groups, modes) for a in g]
    n = len(arrs)
    fresh = [i for i, (_, m) in enumerate(arrs) if m != "forward"]

    def body(*refs):
        srcs, new_lands = refs[:n], refs[n:n + len(fresh)]
        outs = refs[n + len(fresh) + 1:]
        lands = list(srcs)
        for k, i in enumerate(fresh):
            lands[i] = new_lands[k]
        for ai, (_, mode) in enumerate(arrs):
            for src_ref, dst_ref, peer in _copy_plan(mode, srcs[ai], lands[ai]):
                pltpu.make_async_remote_copy(src_ref=src_ref, dst_ref=dst_ref, send_sem=outs[2 * ai],
                                             recv_sem=outs[2 * ai + 1], device_id=peer, device_id_type=MESH).start()
        outs[-1][...] = jnp.zeros(outs[-1].shape, F32)

    land_shapes = [(_land_shape(arrs[i][1], arrs[i][0].shape), arrs[i][0].dtype) for i in fresh]
    n_buf = n + len(fresh)
    out_shape = [pltpu.SemaphoreType.DMA(())] * (2 * n)
    out_shape += [pltpu.HBM(a.shape, a.dtype) for a, _ in arrs]
    out_shape += [pltpu.HBM(shape, dt) for shape, dt in land_shapes]
    out_shape.append(jax.ShapeDtypeStruct((8, LANES), F32))
    res = pl.pallas_call(
        body, name=name, out_shape=tuple(out_shape),
        in_specs=[HBM_SPEC] * n_buf + [ANY_SPEC],
        out_specs=tuple([SEM_SPEC] * (2 * n) + [HBM_SPEC] * n_buf + [pl.BlockSpec(memory_space=pltpu.VMEM)]),
        input_output_aliases={i: 2 * n + i for i in range(n_buf)},
        compiler_params=pltpu.CompilerParams(has_side_effects=SIDE_EFFECT),
    )(*[pltpu.with_memory_space_constraint(a, pltpu.HBM) for a, _ in arrs],
      *[pltpu.with_memory_space_constraint(lax.empty(shape, dt), pltpu.HBM) for shape, dt in land_shapes],
      after)
    sems, thru = res[:2 * n], res[2 * n:-1]
    land_of = {i: thru[n + k] for k, i in enumerate(fresh)}
    states, idx = [], 0
    for g, mode in zip(groups, modes):
        ids = range(idx, idx + len(g))
        idx += len(g)
        states.append(([sems[2 * i] for i in ids], [sems[2 * i + 1] for i in ids],
                       None if mode == "forward" else [thru[i] for i in ids],
                       [land_of.get(i, thru[i]) for i in ids], mode))
    return states, res[-1]


def _async_wait(state, after, name):
    sends, recvs, srcs, lands, mode = state
    g = len(lands)
    bufs = (list(srcs) if srcs is not None else []) + list(lands)
    nb = len(bufs)

    def body(*refs):
        l_refs, sems = refs[nb - g:nb], refs[nb:nb + 2 * g]
        for ai in range(g):
            moved = l_refs[ai].at[pl.ds(0, N_COPIES[mode])]
            cp = pltpu.make_async_remote_copy(src_ref=moved, dst_ref=moved, send_sem=sems[ai], recv_sem=sems[g + ai],
                                              device_id=_coords(), device_id_type=MESH)
            cp.wait_send()
            cp.wait_recv()

    res = pl.pallas_call(
        body, name=name,
        out_shape=tuple(pltpu.HBM(a.shape, a.dtype) for a in bufs),
        in_specs=[HBM_SPEC] * nb + [SEM_SPEC] * (2 * g) + [ANY_SPEC],
        out_specs=tuple([HBM_SPEC] * nb),
        input_output_aliases={i: i for i in range(nb)},
        compiler_params=pltpu.CompilerParams(has_side_effects=SIDE_EFFECT),
    )(*bufs, *sends, *recvs, after)
    return (list(res[:nb - g]) if srcs is not None else None), list(res[nb - g:])


def _add_sibling(mine, theirs, core):
    def body(c_ref, a_ref, b_ref, o_ref):
        o_ref[...] = (a_ref[...].astype(F32) + b_ref[...].astype(F32)).astype(o_ref.dtype)

    blk = (1,) + mine.shape[1:]
    return pl.pallas_call(
        body, name="add_sibling", out_shape=jax.ShapeDtypeStruct(theirs.shape, mine.dtype),
        grid_spec=pltpu.PrefetchScalarGridSpec(
            num_scalar_prefetch=1, grid=(theirs.shape[0],),
            in_specs=[pl.BlockSpec(blk, lambda q, c: (2 * q + c[0], 0, 0)), pl.BlockSpec(blk, lambda q, c: (q, 0, 0))],
            out_specs=pl.BlockSpec(blk, lambda q, c: (q, 0, 0))),
        compiler_params=_params(("parallel",)),
    )(core, mine, theirs)


def _with_own(land, own, me):
    return lax.dynamic_update_index_in_dim(land, own, me, 0)


IN_SPLITS = (512, 512, 512, 8, 768, 256, 32, 1024, 1024)


def _from_shards(g, fn, out_widths, name, own=None, slot=None):
    _, k, n = g.shape
    tr = min(k, 256)

    def body(*refs):
        if own is not None:
            s_ref, g_ref, own_ref = refs[:3]
            cols = [jnp.where(s_ref[0] == j, own_ref[...], g_ref[j]) for j in range(N_DEV)]
        else:
            g_ref = refs[0]
            cols = [g_ref[j] for j in range(N_DEV)]
        for o_ref, val in zip(refs[-len(out_widths):], fn(jnp.concatenate(cols, axis=1))):
            o_ref[...] = val

    in_specs = [pl.BlockSpec((N_DEV, tr, n), lambda i, *_: (0, i, 0))]
    out_spec = [pl.BlockSpec((tr, wd), lambda i, *_: (i, 0)) for wd in out_widths]
    out_shape = [jax.ShapeDtypeStruct((k, wd), g.dtype) for wd in out_widths]
    if own is None:
        return pl.pallas_call(body, name=name, grid=(k // tr,), in_specs=in_specs, out_specs=out_spec,
                              out_shape=out_shape, compiler_params=_params(("parallel",)))(g)
    in_specs.append(pl.BlockSpec((tr, n), lambda i, *_: (i, 0)))
    return pl.pallas_call(
        body, name=name, out_shape=out_shape, compiler_params=_params(("parallel",)),
        grid_spec=pltpu.PrefetchScalarGridSpec(num_scalar_prefetch=1, grid=(k // tr,), in_specs=in_specs, out_specs=out_spec),
    )(slot, g, own)


def _unshard_cols(g, own=None, slot=None):
    return _from_shards(g, lambda full: (full,), [N_DEV * g.shape[2]], "unshard_cols_%d" % g.shape[2], own, slot)[0]


FFN_T = D_FF // 2
FFN_SHARD = 2 * D_FF // N_DEV


def _unshard_ffn_in(g, own=None, slot=None):
    def pairs(full):
        parts = []
        for j in range(D_FF // FFN_T):
            parts += [full[:, j * FFN_T:(j + 1) * FFN_T], full[:, D_FF + j * FFN_T:D_FF + (j + 1) * FFN_T]]
        return (jnp.concatenate(parts, axis=1),)

    return _from_shards(g, pairs, [2 * D_FF], "unshard_ffn_in", own, slot)[0]


def _shard_ffn_in_t(wt):
    tc = 256

    def body(w_ref, o_ref):
        x = w_ref[...]
        nb = D_FF // FFN_T
        full = jnp.concatenate([x[(2 * j + half) * FFN_T:(2 * j + half + 1) * FFN_T]
                                for half in range(2) for j in range(nb)], axis=0)
        for j in range(N_DEV):
            o_ref[j] = full[j * FFN_SHARD:(j + 1) * FFN_SHARD]

    return pl.pallas_call(
        body, name="shard_ffn_in_t", grid=(D // tc,),
        in_specs=[pl.BlockSpec((2 * D_FF, tc), lambda i: (0, i))],
        out_specs=pl.BlockSpec((N_DEV, FFN_SHARD, tc), lambda i: (0, 0, i)),
        out_shape=jax.ShapeDtypeStruct((N_DEV, FFN_SHARD, D), wt.dtype),
        compiler_params=_params(("parallel",)),
    )(wt)


def _shard_cols(w):
    k, n = w.shape[0], w.shape[1] // N_DEV
    tr = min(k, 256)

    def body(w_ref, o_ref):
        full = w_ref[...]
        for j in range(N_DEV):
            o_ref[j] = full[:, j * n:(j + 1) * n]

    return pl.pallas_call(
        body, name="shard_cols_%d" % n, grid=(k // tr,),
        in_specs=[pl.BlockSpec((tr, N_DEV * n), lambda i: (i, 0))],
        out_specs=pl.BlockSpec((N_DEV, tr, n), lambda i: (0, i, 0)),
        out_shape=jax.ShapeDtypeStruct((N_DEV, k, n), w.dtype),
        compiler_params=_params(("parallel",)),
    )(w)


IN_OFFS = tuple(sum(IN_SPLITS[:i]) for i in range(len(IN_SPLITS) + 1))
IN_SHARD = IN_OFFS[-1] // N_DEV
REGROUP_ROWS = 256
MISC_AT = Q_LORA + KV_LORA + 2 * D
A_COLS = MISC_AT + LANES
A_TILE = A_COLS
B_COLS = 3 * HEADS * HEAD_DIM
MISC_BLOCK = MISC_AT // LANES
KR_AT = 64


def _w_in_regroup(g, own=None, slot=None):
    def groups(full):
        fq, fk, fv, wf, cq, ckv, kr, gf, gm = [full[:, IN_OFFS[i]:IN_OFFS[i + 1]] for i in range(9)]
        rows = full.shape[0]
        gap = jnp.zeros((rows, KR_AT - HEADS), BF16)
        tail = jnp.zeros((rows, LANES - KR_AT - ROPE_DIM), BF16)
        return jnp.concatenate([cq, ckv, gf, gm, wf, gap, kr, tail], axis=1), jnp.concatenate([fq, fk, fv], axis=1)

    return _from_shards(g, groups, [A_COLS, B_COLS], "w_in_regroup", own, slot)


def _w_in_ungroup(da, db_):
    def body(a_ref, b_ref, o_ref):
        a = a_ref[...]
        lora = Q_LORA + KV_LORA
        full = jnp.concatenate([b_ref[...], a[:, MISC_AT:MISC_AT + HEADS], a[:, :lora],
                                a[:, MISC_AT + KR_AT:MISC_AT + KR_AT + ROPE_DIM], a[:, lora:MISC_AT]], axis=1)
        for j in range(N_DEV):
            o_ref[j] = full[:, j * IN_SHARD:(j + 1) * IN_SHARD]

    tr = REGROUP_ROWS
    return pl.pallas_call(
        body, name="w_in_ungroup", grid=(D // tr,),
        in_specs=[pl.BlockSpec((tr, A_COLS), lambda i: (i, 0)), pl.BlockSpec((tr, B_COLS), lambda i: (i, 0))],
        out_specs=pl.BlockSpec((N_DEV, tr, IN_SHARD), lambda i: (0, i, 0)),
        out_shape=jax.ShapeDtypeStruct((N_DEV, D, IN_SHARD), BF16),
        compiler_params=_params(("parallel",)),
    )(da, db_)


def _prepare_weights(g, own=None, slot=None):
    w = {}
    if own is not None:
        small = ("w_uq", "w_ukv", "w_out", "w_ffn_out")
        g = {n: (_with_own(a, own[n], slot[0]) if n in small else a) for n, a in g.items()}
    pick = (lambda n: (own[n], slot)) if own is not None else (lambda n: (None, None))
    if "w_in" in g:
        w["w_a"], w["w_b"] = _w_in_regroup(g["w_in"], *pick("w_in"))
    if "w_uq" in g:
        w_uq = g["w_uq"].reshape(Q_LORA, HEADS, 96)
        w["w_uq"] = jnp.pad(w_uq, ((0, 0), (0, 0), (0, 32))).reshape(Q_LORA, HEADS * LANES)
        ukv = g["w_ukv"]
        w["w_k"] = jnp.transpose(jnp.pad(ukv[:, :, :64], ((0, 0), (0, 0), (0, 64))), (1, 0, 2)).reshape(KV_LORA, HEADS * LANES)
        w["w_v"] = jnp.transpose(ukv[:, :, 64:], (1, 0, 2)).reshape(KV_LORA, HEADS * HEAD_DIM)
    if "w_out" in g:
        w["w_pf"] = _unshard_cols(g["w_proj_fox"], *pick("w_proj_fox"))
        w["w_pm"] = _unshard_cols(g["w_proj_mla"], *pick("w_proj_mla"))
        w["w_out"] = g["w_out"].reshape(D, D)
    if "w_ffn_in" in g:
        w["w_ffn_in"] = _unshard_ffn_in(g["w_ffn_in"], *pick("w_ffn_in"))
        w["w_ffn_out"] = g["w_ffn_out"].reshape(D_FF, D)
    return w


def _shard_grads(dw):
    out = {}
    if "w_a" in dw:
        out["w_in"] = _w_in_ungroup(dw["w_a"], dw["w_b"])
    if "w_uq" in dw:
        w_uq = dw["w_uq"].reshape(Q_LORA, HEADS, LANES)[:, :, :96].reshape(Q_LORA, Q_LORA)
        out["w_uq"] = w_uq.reshape(N_DEV, Q_LORA // N_DEV, Q_LORA)
        k_part = dw["w_k"].reshape(KV_LORA, HEADS, LANES)[:, :, :64]
        v_part = dw["w_v"].reshape(KV_LORA, HEADS, HEAD_DIM)
        out["w_ukv"] = jnp.transpose(jnp.concatenate([k_part, v_part], axis=2), (1, 0, 2))
    if "w_out" in dw:
        out["w_proj_fox"] = _shard_cols(dw["w_pf"])
        out["w_proj_mla"] = _shard_cols(dw["w_pm"])
        out["w_out"] = dw["w_out"].reshape(N_DEV, D // N_DEV, D)
    if "w_ffn_in" in dw:
        out["w_ffn_in"] = _shard_ffn_in_t(dw["w_ffn_in"])
        out["w_ffn_out"] = dw["w_ffn_out"].reshape(N_DEV, D_FF // N_DEV, D)
    return out


def _fwd_bwd(x, pos, mod, target, w, vec, wts, send, relay):
    shift_mix, scale_mix, gate_mix, shift_ffn, scale_ffn, gate_ffn = [mod[:, i * D:(i + 1) * D] for i in range(6)]
    g_pre_mix, g_post_mix, g_pre_ffn, g_post_ffn = vec["g_pre_mix"], vec["g_post_mix"], vec["g_pre_ffn"], vec["g_post_ffn"]
    g_q, g_kv = vec["g_q_lora"], vec["g_kv_lora"]

    inv_freq = 1.0 / (ROPE_THETA ** (jnp.arange(0, ROPE_DIM, 2, dtype=F32) / ROPE_DIM))
    invf = jnp.concatenate([jnp.zeros((64,), F32), inv_freq, inv_freq, jnp.zeros((32,), F32)]).reshape(1, LANES)
    ct, sa, sb = _rope_tables(pos, invf)

    def pre1(xv, g, sc, sh):
        return (xv * _rstd(xv) * g) * (1.0 + sc) + sh
    proj_a, h = _mm_epi(x, w["w_a"], "nn", A_TILE, lambda r: ((r,), ()), "in_proj_a", 512, outs=[(A_COLS, A_TILE, F32)],
                        pro=(pre1, [g_pre_mix, scale_mix, shift_mix], 0))
    qkv = _mm(h, w["w_b"], "nn", BF16, "in_proj_b")

    def lora_norm(cv, g):
        return cv * _rstd(cv) * g
    w = {**w, **wts("lora", qkv)}
    tables = [(ct, LANES), (sa, LANES), (sb, LANES)]

    def rope_q(qv, c_, a_, b_):
        return (jnp.concatenate([_rope(qv[:, hd * LANES:(hd + 1) * LANES], c_, a_, b_) for hd in range(HEADS)], axis=1),), ()
    q_m, cqn = _mm_epi(proj_a, w["w_uq"], "nn", D, rope_q, "mla_uq", 512, rows=tables, outs=[(D, D, BF16)],
                       pro=(lora_norm, [g_q], 0))

    def rope_k(kv, misc, c_, a_, b_):
        lane = lax.broadcasted_iota(jnp.int32, (1, LANES), 1)
        kpe = jnp.where((lane >= 64) & (lane < 96), _rope(misc, c_, a_, b_), 0.0)
        return (jnp.concatenate([kv[:, hd * LANES:(hd + 1) * LANES] + kpe for hd in range(HEADS)], axis=1),), ()
    k_m, ckvn = _mm_epi(proj_a, w["w_k"], "nn", D, rope_k, "mla_uk", 512, rows=[(proj_a, LANES, MISC_BLOCK)] + tables,
                        outs=[(D, D, BF16)], pro=(lora_norm, [g_kv], Q_LORA // KV_LORA))
    v_m = _mm(ckvn, w["w_v"], "nn", BF16, "mla_uv")

    bf = jnp.transpose(vec["b_forget"])
    zt, neg_f = _fox_gates(proj_a, MISC_BLOCK, bf)
    bias = neg_f.reshape(HEADS, N_ATT, 1, ATT_T)
    o_b, lse_b = _attn_fwd(q_m, 0, k_m, 0, v_m, 0, 2 * LANES, 1.0 / math.sqrt(64 + ROPE_DIM), None, "mla_attn")
    bias = bias + wts("relay_proj", o_b)["tok"][0, 0]
    o_a, lse_a = _attn_fwd(qkv, 0, qkv, 4, qkv, 8, LANES, 1.0 / math.sqrt(HEAD_DIM), bias, "fox_attn")

    w = {**w, **wts("proj", o_a)}
    gate_mix = gate_mix + wts("relay_ffn", o_a)["tok"][0, 0]
    pa = _mm(o_a, w["w_pf"], "nn", BF16, "proj_fox")

    def merge(pb_, gf, gm, pa_):
        return (_sigmoid(gf) * pa_ + _sigmoid(gm) * pb_, pb_), ()
    merged, pb = _mm_epi(o_b, w["w_pm"], "nn", 512, merge, "proj_mla", 1024,
                         rows=[(proj_a, 512, 2), (proj_a, 512, 4), (pa, 512)], outs=[(D, 512, BF16), (D, 512, BF16)])
    def post1(yv, xv, gate, gpost, gpre, sc, sh):
        x1 = xv + gate * (yv * _rstd(yv) * gpost)
        return (x1, (x1 * _rstd(x1) * gpre) * (1.0 + sc) + sh, yv), ()
    x1, h2, y = _mm_epi(merged, w["w_out"], "nn", D, post1, "out_proj", 512, rows=[(x, D)],
                        vecs=[gate_mix, g_post_mix, g_pre_ffn, scale_ffn, shift_ffn],
                        outs=[(D, D, F32), (D, D, BF16), (D, D, F32)])
    w = {**w, **wts("ffn", h2)}

    def swiglu(r):
        g, u = r[:, :FFN_T], r[:, FFN_T:]
        return (g * _sigmoid(g) * u, r), ()
    act, gu = _mm_epi(h2, w["w_ffn_in"], "nn", 2 * FFN_T, swiglu, "ffn_in", 512,
                      outs=[(D_FF, FFN_T, BF16), (2 * D_FF, 2 * FFN_T, BF16)])

    def head(y2v, x1v, tv, gate, gpost):
        r = _rstd(y2v)
        yn = y2v * r
        n2 = yn * gpost
        err = (x1v + gate * n2) - tv
        dx2 = err * (1.0 / D)
        dn2 = dx2 * gate
        dy2 = _norm_bwd(dn2 * gpost, yn, r)
        return (dx2, dy2), (_colsum(err * err), _colsum(dx2 * n2), _colsum(dn2 * yn))
    dx2, dy2, err_cols, d_gate_ffn, d_g_post_ffn = _mm_epi(
        act, w["w_ffn_out"], "nn", D, head, "ffn_out", 512, rows=[(x1, D), (target, D)], vecs=[gate_ffn, g_post_ffn],
        outs=[(D, D, F32), (D, D, BF16)], sums=[D, D, D])

    def swiglu_bwd(da, guv):
        g, u = guv[:, :FFN_T].astype(F32), guv[:, FFN_T:].astype(F32)
        sg = _sigmoid(g)
        return (jnp.concatenate([da * u * (sg * (1.0 + g * (1.0 - sg))), da * (g * sg)], axis=1),), ()
    (dgu,) = _mm_epi(dy2, w["w_ffn_out"], "nt", FFN_T, swiglu_bwd, "ffn_out_dx", 512, rows=[(gu, 2 * FFN_T)],
                     outs=[(2 * D_FF, 2 * FFN_T, BF16)])
    dw = {"w_ffn_out": _mm(act, dy2, "tn", BF16, "ffn_out_dw")}
    dw["w_ffn_in"] = _mm(dgu, h2, "tn", BF16, "ffn_in_dw")
    gate_mix = gate_mix + send({n: dw.pop(n) for n in ("w_ffn_in", "w_ffn_out")})[0, 0]

    def mid(dh, x1v, dx2v, yv, gpre, sc, gate, gpost):
        r2 = _rstd(x1v)
        x1n = x1v * r2
        t = dh * x1n
        dx1 = dx2v + _norm_bwd(dh * (gpre * (1.0 + sc)), x1n, r2)
        ry = _rstd(yv)
        yn = yv * ry
        dn1 = dx1 * gate
        dy = _norm_bwd(dn1 * gpost, yn, ry)
        sums = (_colsum(dh), _colsum(t) * gpre, _colsum(t) * (1.0 + sc), _colsum(dx1 * (yn * gpost)), _colsum(dn1 * yn))
        return (dx1, dy), sums
    dx1, dy, d_shift_ffn, d_scale_ffn, d_g_pre_ffn, d_gate_mix, d_g_post_mix = _mm_epi(
        dgu, w["w_ffn_in"], "nt", D, mid, "ffn_in_dx", 512, rows=[(x1, D), (dx2, D), (y, D)],
        vecs=[g_pre_ffn, scale_ffn, gate_mix, g_post_mix], outs=[(D, D, F32), (D, D, BF16)], sums=[D] * 5)

    dw["w_out"] = _mm(merged, dy, "tn", BF16, "out_proj_dw")

    def merge_bwd(dm, gf, gm, pa_, pb_):
        sf, sm = _sigmoid(gf), _sigmoid(gm)
        return (dm * sf, dm * sm, dm * pa_ * (sf * (1.0 - sf)), dm * pb_ * (sm * (1.0 - sm))), ()
    dpa, dpb, dgf, dgm = _mm_epi(dy, w["w_out"], "nt", 512, merge_bwd, "out_proj_dx", 1024,
                                 rows=[(proj_a, 512, 2), (proj_a, 512, 4), (pa, 512), (pb, 512)],
                                 outs=[(D, 512, BF16)] * 4)
    do_a = _mm(dpa, w["w_pf"], "nt", BF16, "proj_fox_dx")
    do_b = _mm(dpb, w["w_pm"], "nt", BF16, "proj_mla_dx")
    dw["w_pf"] = _mm(o_a, dpa, "tn", BF16, "proj_fox_dw")
    dw["w_pm"] = _mm(o_b, dpb, "tn", BF16, "proj_mla_dw")
    bias = bias + send({n: dw.pop(n) for n in ("w_out", "w_pf", "w_pm")})[0, 0]

    sc_a, sc_b = 1.0 / math.sqrt(HEAD_DIM), 1.0 / math.sqrt(64 + ROPE_DIM)
    dq_a, dk_a, dv_a, dbias = _attn_grad(qkv, 0, qkv, 4, qkv, 8, do_a, lse_a, LANES, sc_a, bias, BF16, "fox_attn_bwd")
    dq_m, dk_m, dv_m = _attn_grad(q_m, 0, k_m, 0, v_m, 0, do_b, lse_b, 2 * LANES, sc_b, None, F32, "mla_attn_bwd")

    def mla_rope_bwd(dq, dk, c_, a_, b_):
        lane = lax.broadcasted_iota(jnp.int32, (1, LANES), 1)
        dqs = [_rope_t(dq[:, hd * LANES:(hd + 1) * LANES], c_, a_, b_) for hd in range(HEADS)]
        dkpe = dk[:, 0:LANES]
        for hd in range(1, HEADS):
            dkpe = dkpe + dk[:, hd * LANES:(hd + 1) * LANES]
        dkpe = jnp.where((lane >= 64) & (lane < 96), dkpe, 0.0)
        dkr = jnp.where((lane >= 64) & (lane < 96), _rope_t(dkpe, c_, a_, b_), 0.0)
        return (jnp.concatenate(dqs, axis=1), dk, dkr), ()
    dqb, dkb, dkr = _rowwise(mla_rope_bwd, [(dq_m, D, 0), (dk_m, D, 0), (ct, LANES, 0), (sa, LANES, 0), (sb, LANES, 0)],
                             [], [(D, BF16), (D, BF16), (LANES, F32)], [], "mla_rope_bwd")
    def lora_q_bwd(dq, cq, gq):
        rq = _rstd(cq)
        cqh = cq * rq
        return (_norm_bwd(dq * gq, cqh, rq),), (_colsum(dq * cqh),)
    dcq, d_g_q = _mm_epi(dqb, w["w_uq"], "nt", Q_LORA, lora_q_bwd, "mla_uq_dx", 512, rows=[(proj_a, Q_LORA, 0)],
                         vecs=[g_q], outs=[(Q_LORA, Q_LORA, BF16)], sums=[Q_LORA])

    def lora_kv_bwd(dv_part, dk_part, ckv, gkv):
        dkv = dv_part + dk_part
        rk = _rstd(ckv)
        ckh = ckv * rk
        return (_norm_bwd(dkv * gkv, ckh, rk),), (_colsum(dkv * ckh),)
    dckv, d_g_kv = _mm_epi(dv_m, w["w_v"], "nt", KV_LORA, lora_kv_bwd, "mla_uv_dx", 1024,
                           rows=[(_mm(dkb, w["w_k"], "nt", F32, "mla_uk_dx"), KV_LORA), (proj_a, KV_LORA, 3)],
                           vecs=[g_kv], outs=[(KV_LORA, KV_LORA, BF16)], sums=[KV_LORA])

    dzt, d_bf = _fox_gates_bwd(dbias.reshape(HEADS, S), zt, bf)
    dmisc = (dkr + jnp.pad(jnp.transpose(dzt), ((0, 0), (0, LANES - HEADS)))).astype(BF16)
    dproj_a = [dcq, dckv, dgf, dgm, dmisc]
    dqkv = [dq_a, dk_a, dv_a]
    dw["w_a"] = _mm_cat(h, dproj_a, "tn", BF16, "in_proj_a_dw")
    dw["w_b"] = _mm_cat(h, dqkv, "tn", BF16, "in_proj_b_dw")
    tok = send(dw, True)
    dh_a = _mm_cat(dproj_a, w["w_a"], "nt", F32, "in_proj_a_dx", dep=tok)
    tok = relay(dh_a)
    tok = send({"w_uq": _mm(cqn, dqb, "tn", BF16, "mla_uq_dw", dep=tok),
                "w_k": _mm(ckvn, dkb, "tn", BF16, "mla_uk_dw", dep=tok),
                "w_v": _mm(ckvn, dv_m, "tn", BF16, "mla_uv_dw", dep=tok)}, late=True)
    g_pre_mix = g_pre_mix + tok[0, 0]

    def first(dh_b, dh_a, xv, dx1v, gpre, sc):
        dhv = dh_b + dh_a
        r = _rstd(xv)
        xn = xv * r
        t = dhv * xn
        dx = dx1v + _norm_bwd(dhv * (gpre * (1.0 + sc)), xn, r)
        return (dx,), (_colsum(dhv), _colsum(t) * gpre, _colsum(t) * (1.0 + sc))
    grad_x, d_shift_mix, d_scale_mix, d_g_pre_mix = _mm_epi(
        dqkv, w["w_b"], "nt", D, first, "in_proj_b_dx", 512,
        rows=[(dh_a, D), (x, D), (dx1, D)],
        vecs=[g_pre_mix, scale_mix], outs=[(D, D, F32)], sums=[D] * 3)

    dmod = jnp.concatenate([d_shift_mix, d_scale_mix, d_gate_mix, d_shift_ffn, d_scale_ffn, d_gate_ffn], axis=1)
    small = dict(dmod=dmod, g_pre_mix=d_g_pre_mix, g_post_mix=d_g_post_mix, g_pre_ffn=d_g_pre_ffn,
                 g_post_ffn=d_g_post_ffn, g_q_lora=d_g_q, g_kv_lora=d_g_kv,
                 b_forget=jnp.pad(jnp.transpose(d_bf), ((0, 0), (0, LANES - HEADS))), err=err_cols)
    return grad_x, small


SMALL_ORDER = ("dmod", "g_pre_mix", "g_post_mix", "g_pre_ffn", "g_post_ffn", "g_q_lora", "g_kv_lora", "b_forget", "err")
SMALL_PARAM = {"dmod": "b_ada"}
MATRICES = ("w_in", "w_uq", "w_ukv", "w_proj_fox", "w_proj_mla", "w_out", "w_ffn_in", "w_ffn_out")
WEIGHTS = ("w_ada", "b_ada", "g_pre_mix", "g_post_mix", "g_pre_ffn", "g_post_ffn", "w_in", "b_forget", "g_q_lora",
           "w_uq", "g_kv_lora", "w_ukv", "w_proj_fox", "w_proj_mla", "w_out", "w_ffn_in", "w_ffn_out")


def kernel(x, c, positions, w_ada, b_ada, g_pre_mix, g_post_mix, g_pre_ffn, g_post_ffn, w_in, b_forget, g_q_lora, w_uq, g_kv_lora, w_ukv, w_proj_fox, w_proj_mla, w_out, w_ffn_in, w_ffn_out, loss_target, m_w_ada, m_b_ada, m_g_pre_mix, m_g_post_mix, m_g_pre_ffn, m_g_post_ffn, m_w_in, m_b_forget, m_g_q_lora, m_w_uq, m_g_kv_lora, m_w_ukv, m_w_proj_fox, m_w_proj_mla, m_w_out, m_w_ffn_in, m_w_ffn_out, v_w_ada, v_b_ada, v_g_pre_mix, v_g_post_mix, v_g_pre_ffn, v_g_post_ffn, v_w_in, v_b_forget, v_g_q_lora, v_w_uq, v_g_kv_lora, v_w_ukv, v_w_proj_fox, v_w_proj_mla, v_w_out, v_w_ffn_in, v_w_ffn_out):
    prm = dict(w_ada=w_ada, b_ada=b_ada, g_pre_mix=g_pre_mix, g_post_mix=g_post_mix, g_pre_ffn=g_pre_ffn,
               g_post_ffn=g_post_ffn, w_in=w_in, b_forget=b_forget, g_q_lora=g_q_lora, w_uq=w_uq, g_kv_lora=g_kv_lora,
               w_ukv=w_ukv, w_proj_fox=w_proj_fox, w_proj_mla=w_proj_mla, w_out=w_out, w_ffn_in=w_ffn_in, w_ffn_out=w_ffn_out)
    mom = dict(w_ada=m_w_ada, b_ada=m_b_ada, g_pre_mix=m_g_pre_mix, g_post_mix=m_g_post_mix, g_pre_ffn=m_g_pre_ffn,
               g_post_ffn=m_g_post_ffn, w_in=m_w_in, b_forget=m_b_forget, g_q_lora=m_g_q_lora, w_uq=m_w_uq,
               g_kv_lora=m_g_kv_lora, w_ukv=m_w_ukv, w_proj_fox=m_w_proj_fox, w_proj_mla=m_w_proj_mla, w_out=m_w_out,
               w_ffn_in=m_w_ffn_in, w_ffn_out=m_w_ffn_out)
    var = dict(w_ada=v_w_ada, b_ada=v_b_ada, g_pre_mix=v_g_pre_mix, g_post_mix=v_g_post_mix, g_pre_ffn=v_g_pre_ffn,
               g_post_ffn=v_g_post_ffn, w_in=v_w_in, b_forget=v_b_forget, g_q_lora=v_g_q_lora, w_uq=v_w_uq,
               g_kv_lora=v_g_kv_lora, w_ukv=v_w_ukv, w_proj_fox=v_w_proj_fox, w_proj_mla=v_w_proj_mla, w_out=v_w_out,
               w_ffn_in=v_w_ffn_in, w_ffn_out=v_w_ffn_out)
    me = _flat(*_coords())
    slot = jnp.reshape(me, (1,)).astype(jnp.int32)

    own = {n: prm[n][0].astype(BF16) for n in MATRICES}
    no_dep = jnp.zeros((8, LANES), F32)
    (st_c, st_in), tok = _async_start([[c], [own["w_in"]]], ["gather", "spread"], no_dep, "gather_in_start")
    (c_own,), (c_land,) = _async_wait(st_c, tok, "gather_c_wait")
    c_all = _with_own(c_land, c_own, me).reshape(N_DEV, D)
    ada_cols = w_ada.shape[2]
    b_cols = lax.dynamic_slice(b_ada, (0, me * ada_cols), (1, ada_cols))
    mod_cols, silu_c = _mod_part(c_all, w_ada[0], b_cols)
    (mod_all,) = _all_gather([mod_cols], "gather_mod")

    (w_in_own,), (w_in_land,) = _async_wait(st_in, mod_all, "gather_in_wait")
    (st_in,), tok = _async_start([[w_in_land]], "forward", no_dep, "gather_in_forward")
    _, (w_in_land,) = _async_wait(st_in, tok, "gather_in_forward_wait")
    w = _prepare_weights({"w_in": w_in_land}, {"w_in": w_in_own}, slot)
    later = dict(lora=("w_uq", "w_ukv"), proj=("w_proj_fox", "w_proj_mla", "w_out"), ffn=("w_ffn_in", "w_ffn_out"))
    states, tok = _async_start([[own[n] for n in names] for names in later.values()], ["gather", "spread", "spread"],
                               w["w_b"], "gather_rest_start")
    gather_state = dict(zip(later, states))
    own_thru = {}

    def wts(group, after):
        if group.startswith("relay_"):
            name = group[len("relay_"):]
            own_thru[name], lands = _async_wait(gather_state[name], after, "gather_" + name + "_wait")
            (gather_state[name],), t = _async_start([lands], "forward", no_dep, "gather_" + name + "_forward")
            return {"tok": t}
        srcs, lands = _async_wait(gather_state[group], after, "gather_" + group + "_landed")
        srcs = own_thru.get(group, srcs)
        return _prepare_weights(dict(zip(later[group], lands)), dict(zip(later[group], srcs)), slot)

    sent, late_sent, last = [], [], {}

    def send(grads, final=False, late=False):
        shards = _shard_grads(grads)
        names = list(shards)
        (state,), t = _async_start([[shards[n] for n in names]], "pair" if final else "exchange", no_dep,
                                   "exchange_" + names[0] + "_start")
        if final:
            last.update(names=names, state=state)
        else:
            (late_sent if late else sent).append((names, state))
        return t

    def relay(after):
        srcs, lands = _async_wait(last["state"], after, "exchange_pair_wait")
        core = jnp.reshape(lax.axis_index("c"), (1,)).astype(jnp.int32)
        sums = [_add_sibling(src, land, core) for src, land in zip(srcs, lands)]
        (last["state"],), t = _async_start([sums], "chips", no_dep, "exchange_chips_start")
        return t

    mod = lax.dynamic_index_in_dim(mod_all, me, axis=1, keepdims=False).reshape(1, 6 * D) + tok[0, 0]

    vec = dict(g_pre_mix=g_pre_mix, g_post_mix=g_post_mix, g_pre_ffn=g_pre_ffn, g_post_ffn=g_post_ffn,
               g_q_lora=g_q_lora, g_kv_lora=g_kv_lora, b_forget=b_forget)
    pos = positions.astype(F32).reshape(S, 1)
    grad_x, small = _fwd_bwd(x[0], pos, mod, loss_target[0], w, vec, wts, send, relay)

    bundle = jnp.concatenate([small[n] for n in SMALL_ORDER], axis=1)
    (small_state,), tok = _async_start([[bundle]], "gather", jnp.zeros((8, LANES), F32), "gather_small_start")

    out = {}
    swap = lambda a: jnp.swapaxes(a, -1, -2)

    def update(n, land, src, sl):
        if n != "w_ffn_in":
            out[n] = _adamw(prm[n][0], mom[n][0], var[n][0], land, "adamw_" + n, src, sl)
            return out[n][0]
        res = _adamw(swap(prm[n][0]), swap(mom[n][0]), swap(var[n][0]), land, "adamw_" + n, src, sl)
        out[n] = tuple(swap(t) for t in res)
        return res[0]

    after = tok
    for names, state in sent:
        srcs, lands = _async_wait(state, after, "exchange_" + names[0] + "_wait")
        for n, src, land in zip(names, srcs, lands):
            after = update(n, land, src, slot)
    srcs, lands = _async_wait(last["state"], after, "exchange_chips_wait")
    for n, src, land in zip(last["names"], srcs, lands):
        after = update(n, land, src, slot // 2)
    for names, state in late_sent:
        srcs, lands = _async_wait(state, after, "exchange_" + names[0] + "_wait")
        for n, src, land in zip(names, srcs, lands):
            after = update(n, land, src, slot)

    (own_bundle,), (bundle_all,) = _async_wait(small_state, after, "gather_small_wait")
    bundle_all = _with_own(bundle_all, own_bundle, me)
    dmod_all = bundle_all[:, 0, :6 * D]
    dm_cols = lax.dynamic_slice(dmod_all, (0, me * ada_cols), (N_DEV, ada_cols))
    out["w_ada"] = _adamw_w_ada(w_ada[0], m_w_ada[0], v_w_ada[0], jnp.transpose(silu_c), dm_cols)

    offsets, off = {}, 0
    for n in SMALL_ORDER:
        offsets[n] = off
        off += small[n].shape[1]
    names = [SMALL_PARAM.get(n, n) for n in SMALL_ORDER if n != "err"]
    results, err = _adamw_rows(bundle_all, [offsets[n] for n in SMALL_ORDER if n != "err"],
                               [prm[n] for n in names], [mom[n] for n in names], [var[n] for n in names],
                               offsets["err"], D)
    out.update(zip(names, results))
    loss = 0.5 * jnp.sum(err) / D

    res = [loss, grad_x[None]]
    for kind in range(4):
        for n in WEIGHTS:
            t = out[n][kind]
            res.append(t[None] if prm[n].ndim == 3 else t)
    return tuple(res)
```

```python
import functools
import math

import jax
import jax.numpy as jnp
from jax import lax
from jax.experimental import pallas as pl
from jax.experimental.pallas import tpu as pltpu

F32 = jnp.float32
BF16 = jnp.bfloat16

N_DEV = 8
S = 2048
D = 1024
D_FF = 2816
HEADS = 8
HEAD_DIM = 64
Q_LORA = 768
KV_LORA = 256
ROPE_DIM = 32
ROPE_THETA = 10000.0
NORM_EPS = 1e-6
LANES = 128
VMEM_LIMIT = 56 * 1024 * 1024

ADAM_LR = 0.001
ADAM_B1 = 0.9
ADAM_B2 = 0.999
ADAM_EPS = 1e-08
ADAM_WD = 0.01
ADAM_STEP = 10

ATT_T = 256
LOG2E = 1.4426950408889634
N_ATT = S // ATT_T

NN = (((1,), (0,)), ((), ()))
NT = (((1,), (1,)), ((), ()))
TN = (((0,), (0,)), ((), ()))
MESH = pl.DeviceIdType.MESH


def _params(sem=None):
    return pltpu.CompilerParams(dimension_semantics=sem, vmem_limit_bytes=VMEM_LIMIT)


def _pick(n, cap):
    best = None
    for t in range(LANES, cap + 1, LANES):
        if n % t == 0:
            best = t
    return best if best is not None else n


def _mm(a, b, mode, out_dtype, name, acc=None, dep=None):
    if mode == "nn":
        (m, k), (k2, n), dn = a.shape, b.shape, NN
    elif mode == "nt":
        (m, k), (n, k2), dn = a.shape, b.shape, NT
    else:
        (k, m), (k2, n), dn = a.shape, b.shape, TN
    assert k == k2, (a.shape, b.shape, mode)
    tn = _pick(n, 1024)
    tm = _pick(m, 1536)
    osz = jnp.dtype(out_dtype).itemsize

    def need(tm_):
        blk = tm_ * k * 2 + tn * k * 2 + tm_ * tn * osz + (tm_ * tn * 4 if acc is not None else 0)
        return 2 * blk + tm_ * tn * 4
    while need(tm) > 36 * 1024 * 1024 and tm % 256 == 0:
        tm //= 2

    def body(*refs):
        a_ref, b_ref, o_ref = refs[0], refs[1], refs[-1]
        r = lax.dot_general(a_ref[...], b_ref[...], dn, preferred_element_type=F32)
        if acc is not None:
            r = r + refs[2][...]
        o_ref[...] = r.astype(o_ref.dtype)

    if mode == "tn":
        a_spec = pl.BlockSpec((k, tm), lambda i, j: (0, i))
    else:
        a_spec = pl.BlockSpec((tm, k), lambda i, j: (i, 0))
    if mode == "nt":
        b_spec = pl.BlockSpec((tn, k), lambda i, j: (j, 0))
    else:
        b_spec = pl.BlockSpec((k, tn), lambda i, j: (0, j))
    o_spec = pl.BlockSpec((tm, tn), lambda i, j: (i, j))
    in_specs = [a_spec, b_spec] + ([o_spec] if acc is not None else [])
    in_specs += [pl.BlockSpec(memory_space=pl.ANY)] if dep is not None else []
    args = (a, b) + ((acc,) if acc is not None else ()) + ((dep,) if dep is not None else ())
    return pl.pallas_call(
        body, name=name, grid=(m // tm, n // tn),
        in_specs=in_specs, out_specs=o_spec,
        out_shape=jax.ShapeDtypeStruct((m, n), out_dtype),
        compiler_params=_params(("parallel", "parallel")),
    )(*args)


def _offsets(widths):
    return [sum(widths[:p]) for p in range(len(widths))]


def _mm_cat(a, b, mode, out_dtype, name, dep=None):
    pieces = a if mode == "nt" else b
    widths = [p.shape[1] for p in pieces]
    offs = _offsets(widths)
    assert all(w_ % LANES == 0 for w_ in widths)
    resident = dict(pipeline_mode=pl.Buffered(1))
    if mode == "nt":
        m, (n, k) = pieces[0].shape[0], b.shape
        assert k == sum(widths)
        tm = _pick(m, 512)

        def body(*refs):
            b_ref, o_ref = refs[len(pieces)], refs[-1]
            r = None
            for p_ref, off, w_ in zip(refs, offs, widths):
                t = lax.dot_general(p_ref[...], b_ref[:, off:off + w_], NT, preferred_element_type=F32)
                r = t if r is None else r + t
            o_ref[...] = r.astype(o_ref.dtype)
        in_specs = [pl.BlockSpec((tm, w_), lambda i: (i, 0)) for w_ in widths]
        in_specs.append(pl.BlockSpec((n, k), lambda i: (0, 0), **resident))
        args = (*pieces, b)
    else:
        assert mode == "tn"
        (k, m), n = a.shape, sum(widths)
        tm = _pick(m, 512)

        def body(*refs):
            a_ref, o_ref = refs[0], refs[-1]
            for p_ref, off, w_ in zip(refs[1:], offs, widths):
                o_ref[:, off:off + w_] = lax.dot_general(
                    a_ref[...], p_ref[...], TN, preferred_element_type=F32).astype(o_ref.dtype)
        in_specs = [pl.BlockSpec((k, tm), lambda i: (0, i))]
        in_specs += [pl.BlockSpec((k, w_), lambda i: (0, 0), **resident) for w_ in widths]
        args = (a, *pieces)
    if dep is not None:
        in_specs.append(pl.BlockSpec(memory_space=pl.ANY))
        args += (dep,)
    return pl.pallas_call(
        body, name=name, grid=(m // tm,),
        in_specs=in_specs, out_specs=pl.BlockSpec((tm, n), lambda i: (i, 0)),
        out_shape=jax.ShapeDtypeStruct((m, n), out_dtype),
        compiler_params=_params(("parallel",)),
    )(*args)


def _mm_epi(a, b, mode, tnb, epi, name, tm, rows=(), vecs=(), outs=(), sums=(), pro=None, dep=None):
    pieces = list(a) if isinstance(a, (list, tuple)) else [a]
    assert len(pieces) == 1 or pro is None
    widths = [p.shape[1] for p in pieces]
    offs = _offsets(widths)
    m = pieces[0].shape[0]
    k, nb = (b.shape if mode == "nn" else b.shape[::-1])
    dn = NN if mode == "nn" else NT
    pro_fn, pro_vecs, a_off = pro if pro is not None else (None, (), 0)
    n_in = 2 + len(rows) + len(vecs)
    n_all = n_in + len(pro_vecs)
    sub = min(tm, 256)

    def body(*refs):
        a_refs, refs = refs[:len(pieces)], refs[len(pieces) - 1:]
        if pro is not None:
            a_out, a_scr = refs[-2:]
            refs = refs[:-2]

            @pl.when(pl.program_id(1) == 0)
            def _():
                a_scr[...] = pro_fn(refs[0][...], *[x[...] for x in refs[n_in:n_all]]).astype(BF16)
                a_out[...] = a_scr[...]
            a_ref = a_scr
        else:
            a_ref = refs[0]
        n_skip = n_all + (dep is not None)
        o_refs = refs[n_skip:n_skip + len(outs)]
        s_refs = refs[n_skip + len(outs):]
        if sums:
            @pl.when((pl.program_id(0) == 0) & (pl.program_id(1) == 0))
            def _():
                for s_ref in s_refs:
                    s_ref[...] = jnp.zeros(s_ref.shape, F32)
        for c in range(tm // sub):
            rs = slice(c * sub, (c + 1) * sub)
            if len(pieces) == 1:
                r = lax.dot_general(a_ref[rs, :], refs[1][...], dn, preferred_element_type=F32)
            else:
                r = None
                for p_ref, off, w_ in zip(a_refs, offs, widths):
                    b_part = refs[1][off:off + w_, :] if mode == "nn" else refs[1][:, off:off + w_]
                    t = lax.dot_general(p_ref[rs, :], b_part, dn, preferred_element_type=F32)
                    r = t if r is None else r + t
            o_vals, s_vals = epi(r, *[x[rs, :] for x in refs[2:2 + len(rows)]], *[x[...] for x in refs[2 + len(rows):n_in]])
            assert len(o_vals) == len(o_refs) and len(s_vals) == len(s_refs)
            for o_ref, val in zip(o_refs, o_vals):
                o_ref[rs, :] = val.astype(o_ref.dtype)
            for s_ref, val in zip(s_refs, s_vals):
                s_ref[...] += val

    once = dict(pipeline_mode=pl.Buffered(1)) if nb == tnb else {}
    if mode == "nn":
        b_spec = pl.BlockSpec((k, tnb), lambda i, j: (0, j), **once)
    else:
        b_spec = pl.BlockSpec((tnb, k), lambda i, j: (j, 0), **once)
    if len(pieces) == 1:
        in_specs = [pl.BlockSpec((tm, k), lambda i, j: (i, a_off)), b_spec]
    else:
        assert sum(widths) == k and all(w_ % LANES == 0 for w_ in widths)
        in_specs = [pl.BlockSpec((tm, w_), lambda i, j: (i, 0)) for w_ in widths] + [b_spec]
    rows = [tuple(r) + (0,) * (3 - len(r)) for r in rows]
    in_specs += [pl.BlockSpec((tm, w), functools.partial(lambda i, j, off: (i, j + off), off=off)) for _, w, off in rows]
    in_specs += [pl.BlockSpec(v.shape, lambda i, j: (0, 0)) for v in list(vecs) + list(pro_vecs)]
    in_specs += [pl.BlockSpec(memory_space=pl.ANY)] if dep is not None else []
    out_specs = [pl.BlockSpec((tm, w), lambda i, j: (i, j)) for _, w, _ in outs]
    out_specs += [pl.BlockSpec((1, w), lambda i, j: (0, 0)) for w in sums]
    out_shape = [jax.ShapeDtypeStruct((m, full), dt) for full, _, dt in outs]
    out_shape += [jax.ShapeDtypeStruct((1, w), F32) for w in sums]
    if pro is not None:
        out_specs.append(pl.BlockSpec((tm, k), lambda i, j: (i, 0)))
        out_shape.append(jax.ShapeDtypeStruct((m, k), BF16))
    return pl.pallas_call(
        body, name=name, grid=(m // tm, nb // tnb),
        in_specs=in_specs, out_specs=out_specs, out_shape=out_shape,
        scratch_shapes=[pltpu.VMEM((tm, k), BF16)] if pro is not None else [],
        compiler_params=_params(("arbitrary", "arbitrary") if sums else ("parallel", "arbitrary" if pro is not None else "parallel")),
    )(*pieces, b, *[r[0] for r in rows], *vecs, *pro_vecs, *([dep] if dep is not None else []))


def _rowwise(fn, row_ins, vec_ins, row_outs, sum_outs, name, tm=512):
    n_in = len(row_ins) + len(vec_ins)
    n_o = len(row_outs)
    rows = row_ins[0][0].shape[0]

    def body(*refs):
        vals = [r[...] for r in refs[:n_in]]
        outs = refs[n_in:]
        ro, so = fn(*vals)
        assert len(ro) == n_o and len(so) == len(sum_outs)
        for r, v in zip(outs[:n_o], ro):
            r[...] = v.astype(r.dtype)
        if sum_outs:
            @pl.when(pl.program_id(0) == 0)
            def _():
                for r in outs[n_o:]:
                    r[...] = jnp.zeros(r.shape, F32)
            for r, v in zip(outs[n_o:], so):
                r[...] += v

    in_specs = [pl.BlockSpec((tm, w), functools.partial(lambda i, b: (i, b), b=b)) for _, w, b in row_ins]
    in_specs += [pl.BlockSpec(v.shape, lambda i: (0, 0)) for v in vec_ins]
    out_specs = [pl.BlockSpec((tm, w), lambda i: (i, 0)) for w, _ in row_outs]
    out_specs += [pl.BlockSpec((1, w), lambda i: (0, 0)) for w in sum_outs]
    out_shape = [jax.ShapeDtypeStruct((rows, w), dt) for w, dt in row_outs]
    out_shape += [jax.ShapeDtypeStruct((1, w), F32) for w in sum_outs]
    return pl.pallas_call(
        body, name=name, grid=(rows // tm,),
        in_specs=in_specs, out_specs=out_specs, out_shape=out_shape,
        compiler_params=_params(("arbitrary",)),
    )(*[a for a, _, _ in row_ins], *vec_ins)


def _sigmoid(x):
    return 1.0 / (1.0 + jnp.exp(-x))


def _rstd(x):
    return lax.rsqrt(jnp.mean(x * x, axis=-1, keepdims=True) + NORM_EPS)


def _norm_bwd(dyn, xn, r):
    return r * (dyn - xn * jnp.mean(dyn * xn, axis=-1, keepdims=True))


def _colsum(x):
    return jnp.sum(x, axis=0, keepdims=True)


def _rope_tables(pos, invf):
    def fn(p, f):
        lane = lax.broadcasted_iota(jnp.int32, (1, LANES), 1)
        ang = p * f
        cs, sn = jnp.cos(ang), jnp.sin(ang)
        rot = (lane >= 64) & (lane < 96)
        ct = jnp.where(lane < 64, 1.0, jnp.where(rot, cs, 0.0))
        sa = jnp.where((lane >= 64) & (lane < 80), -sn, 0.0)
        sb = jnp.where((lane >= 80) & (lane < 96), sn, 0.0)
        return (ct, sa, sb), ()
    return _rowwise(fn, [(pos, 1, 0)], [invf], [(LANES, F32)] * 3, [], "rope_tables")


def _rope(x, ct, sa, sb):
    return x * ct + pltpu.roll(x, LANES - 16, 1) * sa + pltpu.roll(x, 16, 1) * sb


def _rope_t(x, ct, sa, sb):
    return x * ct - pltpu.roll(x, LANES - 16, 1) * sa - pltpu.roll(x, 16, 1) * sb


def _head_mask(width, hh):
    lane = lax.broadcasted_iota(jnp.int32, (1, width), 1)
    half = width // 2
    return (lane >= hh * half) & (lane < (hh + 1) * half)


ATT_PP = 2
ATT_CHAINS = [(a, hh) for a in range(ATT_PP) for hh in range(2)]
ATT_G = HEADS // (2 * ATT_PP)


def _pair(ref_or_val, a, width, rows=slice(None)):
    return ref_or_val[rows, a * width:(a + 1) * width]


def _head_cols(ref, a, hh, dkp, rows=slice(None)):
    if dkp == 2 * LANES:
        return ref[rows, a * dkp + hh * LANES:a * dkp + (hh + 1) * LANES]
    blk = _pair(ref, a, dkp, rows)
    return jnp.where(_head_mask(dkp, hh), blk, jnp.zeros_like(blk))


def _attn_fwd(q, qo, k, ko, v, vo, dkp, scale, bias, name, dep=None):
    T = ATT_T
    assert qo % ATT_PP == 0 and ko % ATT_PP == 0 and vo % ATT_PP == 0
    qo, ko, vo = qo // ATT_PP, ko // ATT_PP, vo // ATT_PP
    split = dkp == 2 * LANES

    def body(*refs):
        refs = list(refs)
        if dep is not None:
            del refs[3 + (bias is not None)]
        if bias is not None:
            q_ref, k_ref, v_ref, b_ref, o_ref, lse_ref, s_scr = refs
        else:
            q_ref, k_ref, v_ref, o_ref, lse_ref, s_scr = refs
        i = pl.program_id(1)
        row = lax.broadcasted_iota(jnp.int32, (T, T), 0)
        col = lax.broadcasted_iota(jnp.int32, (T, T), 1)
        qms = [_head_cols(q_ref, a, hh, dkp) for a, hh in ATT_CHAINS]

        def k_of(a, hh, ks):
            return _head_cols(k_ref, a, hh, dkp, ks) if split else _pair(k_ref, a, dkp, ks)

        def fold(t):
            return [t[:, c * LANES:(c + 1) * LANES] for c in range(T // LANES)]

        def run(nt):
            mls = [jnp.full((T, LANES), -jnp.inf, F32) for _ in ATT_CHAINS]
            for j in range(nt):
                ks = slice(j * T, (j + 1) * T)
                for ci, (a, hh) in enumerate(ATT_CHAINS):
                    s = lax.dot_general(qms[ci], k_of(a, hh, ks), NT, preferred_element_type=F32) * (scale * LOG2E)
                    if bias is not None:
                        s = s + b_ref[2 * a + hh, j] * LOG2E
                    if j == nt - 1:
                        s = jnp.where(row >= col, s, -jnp.inf)
                    s_scr[ci, j] = s
                    for part in fold(s):
                        mls[ci] = jnp.maximum(mls[ci], part)
            ms = [jnp.max(ml, axis=1, keepdims=True) for ml in mls]
            mbs = [jnp.broadcast_to(m, (T, LANES)) for m in ms]
            for a in range(ATT_PP):
                ls = [jnp.zeros((T, LANES), F32) for _ in range(2)]
                ps, vms = [], []
                for j in range(nt):
                    vb = _pair(v_ref, a, LANES, slice(j * T, (j + 1) * T))
                    for hh in range(2):
                        parts = [jnp.exp2(part - mbs[2 * a + hh]) for part in fold(s_scr[2 * a + hh, j])]
                        for part in parts:
                            ls[hh] = ls[hh] + part
                        ps.append(jnp.concatenate(parts, axis=1).astype(BF16))
                        vms.append(jnp.where(_head_mask(LANES, hh), vb, jnp.zeros_like(vb)))
                acc = lax.dot_general(jnp.concatenate(ps, axis=1), jnp.concatenate(vms, axis=0), NN,
                                      preferred_element_type=F32)
                l0, l1 = [jnp.sum(l, axis=1, keepdims=True) for l in ls]
                lse_ref[2 * a] = ms[2 * a] + jnp.log2(l0)
                lse_ref[2 * a + 1] = ms[2 * a + 1] + jnp.log2(l1)
                inv = jnp.where(_head_mask(LANES, 0), 1.0 / l0, 1.0 / l1)
                o_ref[:, a * LANES:(a + 1) * LANES] = (acc * inv).astype(o_ref.dtype)

        for nt in range(1, N_ATT + 1):
            pl.when(i == nt - 1)(functools.partial(run, nt))

    in_specs = [
        pl.BlockSpec((T, ATT_PP * dkp), lambda g, i: (i, qo + g)),
        pl.BlockSpec((S, ATT_PP * dkp), lambda g, i: (0, ko + g)),
        pl.BlockSpec((S, ATT_PP * LANES), lambda g, i: (0, vo + g)),
    ]
    args = [q, k, v]
    if bias is not None:
        in_specs.append(pl.BlockSpec((2 * ATT_PP, N_ATT, 1, T), lambda g, i: (g, 0, 0, 0)))
        args.append(bias)
    if dep is not None:
        in_specs.append(pl.BlockSpec(memory_space=pl.ANY))
        args.append(dep)
    return pl.pallas_call(
        body, name=name, grid=(ATT_G, N_ATT),
        in_specs=in_specs,
        out_specs=[pl.BlockSpec((T, ATT_PP * LANES), lambda g, i: (i, g)),
                   pl.BlockSpec((2 * ATT_PP, T, 1), lambda g, i: (g, i, 0))],
        out_shape=[jax.ShapeDtypeStruct((S, HEADS * HEAD_DIM), BF16),
                   jax.ShapeDtypeStruct((HEADS, S, 1), F32)],
        scratch_shapes=[pltpu.VMEM((len(ATT_CHAINS), N_ATT, T, T), F32)],
        compiler_params=_params(("parallel", "arbitrary")),
    )(*args)


def _attn_grad(q, qo, k, ko, v, vo, do, lse, dkp, scale, bias, qk_dtype, name, dep=None):
    T = ATT_T
    has_b = bias is not None
    qo, ko, vo = qo // ATT_PP, ko // ATT_PP, vo // ATT_PP
    n_ch = len(ATT_CHAINS)
    split = dkp == 2 * LANES

    def body(*refs):
        refs = list(refs)
        if dep is not None:
            del refs[5 + has_b]
        q_ref, k_ref, v_ref, do_ref, lse_ref = refs[:5]
        refs = refs[5:]
        if has_b:
            b_ref, refs = refs[0], refs[1:]
        dq_ref, dk_ref, dv_ref = refs[:3]
        refs = refs[3:]
        if has_b:
            db_ref, refs = refs[0], refs[1:]
        p_scr, dp_scr, dk_acc, dv_acc = refs[:4]
        db_acc = refs[4] if has_b else None
        i = pl.program_id(1)

        @pl.when(i == 0)
        def _():
            dk_acc[...] = jnp.zeros(dk_acc.shape, F32)
            dv_acc[...] = jnp.zeros(dv_acc.shape, F32)
            if has_b:
                db_acc[...] = jnp.zeros(db_acc.shape, F32)

        row = lax.broadcasted_iota(jnp.int32, (T, T), 0)
        col = lax.broadcasted_iota(jnp.int32, (T, T), 1)

        def fold(t):
            return [t[:, c * LANES:(c + 1) * LANES] for c in range(T // LANES)]

        qms, doms, lses = [], [], []
        for a, hh in ATT_CHAINS:
            dob = _pair(do_ref, a, LANES)
            qms.append(_head_cols(q_ref, a, hh, dkp))
            doms.append(jnp.where(_head_mask(LANES, hh), dob, jnp.zeros_like(dob)))
            lses.append(lse_ref[2 * a + hh])

        def k_of(a, hh, ks):
            return _head_cols(k_ref, a, hh, dkp, ks) if split else _pair(k_ref, a, dkp, ks)

        def run(nt):
            dls = [jnp.zeros((T, LANES), F32) for _ in ATT_CHAINS]
            for j in range(nt):
                ks = slice(j * T, (j + 1) * T)
                for ci, (a, hh) in enumerate(ATT_CHAINS):
                    s = lax.dot_general(qms[ci], k_of(a, hh, ks), NT, preferred_element_type=F32) * (scale * LOG2E)
                    if has_b:
                        s = s + b_ref[ci, j] * LOG2E
                    s = s - lses[ci]
                    if j == nt - 1:
                        s = jnp.where(row >= col, s, -jnp.inf)
                    p = jnp.exp2(s)
                    dp = lax.dot_general(doms[ci], _pair(v_ref, a, LANES, ks), NT, preferred_element_type=F32)
                    p_scr[ci, j] = p
                    dp_scr[ci, j] = dp
                    for part in fold(p * dp):
                        dls[ci] = dls[ci] + part
            deltas = [jnp.broadcast_to(jnp.sum(dl, axis=1, keepdims=True), (T, LANES)) for dl in dls]
            for a in range(ATT_PP):
                ds_all, km_all = [], []
                if split:
                    qts = [jnp.transpose(qms[2 * a + hh]) for hh in range(2)]
                else:
                    qm2t = jnp.transpose(jnp.concatenate([qms[2 * a], qms[2 * a + 1]], axis=0))
                dom2t = jnp.transpose(jnp.concatenate([doms[2 * a], doms[2 * a + 1]], axis=0))
                for j in range(nt):
                    ks = slice(j * T, (j + 1) * T)
                    p2, ds2 = [], []
                    for hh in range(2):
                        ci = 2 * a + hh
                        p = p_scr[ci, j]
                        ds = jnp.concatenate([pp * (dd - deltas[ci]) for pp, dd in zip(fold(p), fold(dp_scr[ci, j]))], axis=1)
                        if has_b:
                            db_acc[ci, j] += jnp.sum(ds, axis=0, keepdims=True)
                        p2.append(p.astype(BF16))
                        ds2.append((ds * scale).astype(BF16))
                        if not split:
                            km_all.append(_head_cols(k_ref, a, hh, dkp, ks))
                    dv_acc[a * LANES:(a + 1) * LANES, ks] += lax.dot_general(
                        dom2t, jnp.concatenate(p2, axis=0), NN, preferred_element_type=F32)
                    if split:
                        for hh in range(2):
                            dk_acc[a * dkp + hh * LANES:a * dkp + (hh + 1) * LANES, ks] += lax.dot_general(
                                qts[hh], ds2[hh], NN, preferred_element_type=F32)
                    else:
                        dk_acc[a * dkp:(a + 1) * dkp, ks] += lax.dot_general(
                            qm2t, jnp.concatenate(ds2, axis=0), NN, preferred_element_type=F32)
                    ds_all += ds2
                if split:
                    for hh in range(2):
                        dq = lax.dot_general(jnp.concatenate(ds_all[hh::2], axis=1),
                                             _head_cols(k_ref, a, hh, dkp, slice(0, nt * T)), NN,
                                             preferred_element_type=F32)
                        dq_ref[:, a * dkp + hh * LANES:a * dkp + (hh + 1) * LANES] = dq.astype(dq_ref.dtype)
                else:
                    dq = lax.dot_general(jnp.concatenate(ds_all, axis=1), jnp.concatenate(km_all, axis=0), NN,
                                         preferred_element_type=F32)
                    dq_ref[:, a * dkp:(a + 1) * dkp] = dq.astype(dq_ref.dtype)

        for nt in range(1, N_ATT + 1):
            pl.when(i == nt - 1)(functools.partial(run, nt))

        @pl.when(i == N_ATT - 1)
        def _():
            dk_ref[...] = jnp.transpose(dk_acc[...]).astype(dk_ref.dtype)
            dv_ref[...] = jnp.transpose(dv_acc[...]).astype(dv_ref.dtype)
            if has_b:
                db_ref[...] = db_acc[...]

    in_specs = [
        pl.BlockSpec((T, ATT_PP * dkp), lambda g, i: (i, qo + g)),
        pl.BlockSpec((S, ATT_PP * dkp), lambda g, i: (0, ko + g)),
        pl.BlockSpec((S, ATT_PP * LANES), lambda g, i: (0, vo + g)),
        pl.BlockSpec((T, ATT_PP * LANES), lambda g, i: (i, g)),
        pl.BlockSpec((2 * ATT_PP, T, 1), lambda g, i: (g, i, 0)),
    ]
    args = [q, k, v, do, lse]
    out_specs = [
        pl.BlockSpec((T, ATT_PP * dkp), lambda g, i: (i, g)),
        pl.BlockSpec((S, ATT_PP * dkp), lambda g, i: (0, g)),
        pl.BlockSpec((S, ATT_PP * LANES), lambda g, i: (0, g)),
    ]
    width = (HEADS // 2) * dkp
    out_shape = [
        jax.ShapeDtypeStruct((S, width), qk_dtype),
        jax.ShapeDtypeStruct((S, width), qk_dtype),
        jax.ShapeDtypeStruct((S, HEADS * HEAD_DIM), BF16),
    ]
    scratch = [pltpu.VMEM((n_ch, N_ATT, T, T), F32), pltpu.VMEM((n_ch, N_ATT, T, T), F32),
               pltpu.VMEM((ATT_PP * dkp, S), F32), pltpu.VMEM((ATT_PP * LANES, S), F32)]
    if has_b:
        bspec = pl.BlockSpec((2 * ATT_PP, N_ATT, 1, T), lambda g, i: (g, 0, 0, 0))
        in_specs.append(bspec)
        args.append(bias)
        out_specs.append(bspec)
        out_shape.append(jax.ShapeDtypeStruct((HEADS, N_ATT, 1, T), F32))
        scratch.append(pltpu.VMEM((2 * ATT_PP, N_ATT, 1, T), F32))
    if dep is not None:
        in_specs.append(pl.BlockSpec(memory_space=pl.ANY))
        args.append(dep)
    return pl.pallas_call(
        body, name=name, grid=(ATT_G, N_ATT),
        in_specs=in_specs, out_specs=out_specs, out_shape=out_shape, scratch_shapes=scratch,
        compiler_params=_params(("parallel", "arbitrary")),
    )(*args)


def _tri(upper):
    a = lax.broadcasted_iota(jnp.int32, (LANES, LANES), 0)
    b = lax.broadcasted_iota(jnp.int32, (LANES, LANES), 1)
    return jnp.where(a <= b if upper else a >= b, 1.0, 0.0).astype(F32)


def _fox_gates(proj, blk, bf):
    def body(m_ref, b_ref, z_out, o_ref):
        tri = _tri(True)
        carry = jnp.zeros((HEADS, 1), F32)
        for t in range(S // LANES):
            sl = slice(t * LANES, (t + 1) * LANES)
            zt = jnp.transpose(m_ref[sl, :])[:HEADS]
            z_out[:, sl] = zt
            z = zt + b_ref[...]
            logf = jnp.minimum(z, 0.0) - jnp.log(1.0 + jnp.exp(-jnp.abs(z)))
            c = lax.dot_general(logf, tri, NN, preferred_element_type=F32,
                                precision=lax.Precision.HIGHEST) + carry
            o_ref[:, sl] = -c
            carry = c[:, LANES - 1:LANES]

    return pl.pallas_call(
        body, name="fox_gates", grid=(1,),
        in_specs=[pl.BlockSpec((S, LANES), lambda i: (0, blk)), pl.BlockSpec(bf.shape, lambda i: (0, 0))],
        out_specs=[pl.BlockSpec((HEADS, S), lambda i: (0, 0))] * 2,
        out_shape=[jax.ShapeDtypeStruct((HEADS, S), F32)] * 2,
        compiler_params=_params(("arbitrary",)),
    )(proj, bf)


def _fox_gates_bwd(dbias, zt, bf):
    def body(d_ref, z_ref, b_ref, dz_ref, dbf_ref):
        tri = _tri(False)
        carry = jnp.zeros((HEADS, 1), F32)
        tot = jnp.zeros((HEADS, 1), F32)
        for t in reversed(range(S // LANES)):
            sl = slice(t * LANES, (t + 1) * LANES)
            df = -d_ref[:, sl]
            c = lax.dot_general(df, tri, NN, preferred_element_type=F32,
                                precision=lax.Precision.HIGHEST) + carry
            carry = c[:, 0:1]
            z = z_ref[:, sl] + b_ref[...]
            dz = c * _sigmoid(-z)
            dz_ref[:, sl] = dz
            tot = tot + jnp.sum(dz, axis=1, keepdims=True)
        dbf_ref[...] = tot

    return pl.pallas_call(
        body, name="fox_gates_bwd",
        out_shape=[jax.ShapeDtypeStruct((HEADS, S), F32), jax.ShapeDtypeStruct((HEADS, 1), F32)],
        compiler_params=_params(),
    )(dbias, zt, bf)


def _mod_part(c_all, w_ada, b_cols):
    def body(c_ref, w_ref, b_ref, o_ref, s_ref):
        c = c_ref[...]
        sc = c * _sigmoid(c)
        s_ref[...] = sc
        o_ref[...] = lax.dot_general(sc, w_ref[...], NN, preferred_element_type=F32,
                                     precision=lax.Precision.HIGHEST) + b_ref[...]

    return pl.pallas_call(
        body, name="mod_part",
        out_shape=[jax.ShapeDtypeStruct((N_DEV, w_ada.shape[1]), F32), jax.ShapeDtypeStruct(c_all.shape, F32)],
        compiler_params=_params(),
    )(c_all, w_ada, b_cols)


def _adamw_w_ada(w, m, v, sc_t, dm):
    rows, cols = w.shape
    tr = 256

    def body(w_ref, m_ref, v_ref, s_ref, d_ref, g_out, d_out, m_out, v_out):
        g = s_ref[:, 0:1] * d_ref[0:1, :]
        for b in range(1, N_DEV):
            g = g + s_ref[:, b:b + 1] * d_ref[b:b + 1, :]
        g_out[...] = g
        d_out[...], m_out[...], v_out[...] = _adamw_math(w_ref[...], g, m_ref[...], v_ref[...])

    spec = pl.BlockSpec((tr, cols), lambda i: (i, 0))
    return pl.pallas_call(
        body, name="adamw_w_ada", grid=(rows // tr,),
        in_specs=[spec, spec, spec, pl.BlockSpec((tr, N_DEV), lambda i: (i, 0)), pl.BlockSpec(dm.shape, lambda i: (0, 0))],
        out_specs=[spec] * 4, out_shape=[jax.ShapeDtypeStruct((rows, cols), F32)] * 4,
        compiler_params=_params(("parallel",)),
    )(w, m, v, sc_t, dm)


def _adamw(w, m, v, parts, name, own=None, slot=None):
    rows, cols = w.shape
    n = parts.shape[0]
    by_cols = rows % 256 != 0 and cols % 256 == 0
    tr, tc = (rows, 256) if by_cols else ((rows if rows <= 512 else 256), cols)
    tile = (lambda i: (0, i)) if by_cols else (lambda i: (i, 0))

    def body(*refs):
        if own is not None:
            s_ref, refs = refs[0], refs[1:]
            w_ref, m_ref, v_ref, p_ref, o_ref, g_out, d_out, m_out, v_out = refs
            terms = [jnp.where(s_ref[0] == kk, o_ref[0], p_ref[kk]) for kk in range(n)]
        else:
            w_ref, m_ref, v_ref, p_ref, g_out, d_out, m_out, v_out = refs
            terms = [p_ref[kk] for kk in range(n)]
        g = terms[0].astype(F32)
        for term in terms[1:]:
            g = g + term.astype(F32)
        g_out[...] = g
        d_out[...], m_out[...], v_out[...] = _adamw_math(w_ref[...], g, m_ref[...], v_ref[...])

    spec = pl.BlockSpec((tr, tc), lambda i, *_: tile(i))
    in_specs = [spec, spec, spec, pl.BlockSpec((n, tr, tc), lambda i, *_: (0,) + tile(i))]
    out_shape = [jax.ShapeDtypeStruct((rows, cols), F32)] * 4
    grid = (rows // tr if not by_cols else cols // tc,)
    if own is None:
        return pl.pallas_call(
            body, name=name, grid=grid, in_specs=in_specs, out_specs=[spec] * 4, out_shape=out_shape,
            compiler_params=_params(("parallel",)),
        )(w, m, v, parts)
    in_specs.append(pl.BlockSpec((1, tr, tc), lambda i, s: (s[0],) + tile(i)))
    return pl.pallas_call(
        body, name=name, out_shape=out_shape, compiler_params=_params(("parallel",)),
        grid_spec=pltpu.PrefetchScalarGridSpec(num_scalar_prefetch=1, grid=grid, in_specs=in_specs,
                                               out_specs=[spec] * 4),
    )(slot, w, m, v, parts, own)


def _adamw_math(w, g, m, v):
    mm = ADAM_B1 * m + (1.0 - ADAM_B1) * g
    vv = ADAM_B2 * v + (1.0 - ADAM_B2) * (g * g)
    m_hat = mm / (1.0 - ADAM_B1 ** ADAM_STEP)
    v_hat = vv / (1.0 - ADAM_B2 ** ADAM_STEP)
    return -ADAM_LR * (m_hat / (jnp.sqrt(v_hat) + ADAM_EPS) + ADAM_WD * w), mm, vv


def _adamw_rows(bundles, offsets, ws, ms, vs, err_off, err_width):
    k = len(ws)

    def body(*refs):
        b_ref = refs[0]
        w_refs, m_refs, v_refs = refs[1:1 + k], refs[1 + k:1 + 2 * k], refs[1 + 2 * k:1 + 3 * k]
        outs = refs[1 + 3 * k:]
        g_all = b_ref[0]
        for kk in range(1, N_DEV):
            g_all = g_all + b_ref[kk]
        for i in range(k):
            width = w_refs[i].shape[1]
            g = g_all[:, offsets[i]:offsets[i] + width]
            outs[4 * i][...] = g
            outs[4 * i + 1][...], outs[4 * i + 2][...], outs[4 * i + 3][...] = _adamw_math(
                w_refs[i][...], g, m_refs[i][...], v_refs[i][...])
        outs[4 * k][...] = g_all[:, err_off:err_off + err_width]

    out_shape = []
    for w_ in ws:
        out_shape += [jax.ShapeDtypeStruct(w_.shape, F32)] * 4
    out_shape.append(jax.ShapeDtypeStruct((1, err_width), F32))
    res = pl.pallas_call(body, name="adamw_rows", out_shape=out_shape, compiler_params=_params())(bundles, *ws, *ms, *vs)
    return [tuple(res[4 * i:4 * i + 4]) for i in range(k)], res[-1]


def _coords():
    return lax.axis_index("x"), lax.axis_index("y"), lax.axis_index("c")


def _flat(px, py, pc):
    return 4 * px + 2 * py + pc


def _all_gather(arrs, name):
    n = len(arrs)

    def body(*refs):
        ins, outs = refs[:n], refs[n:2 * n]
        send, recv, lsem = refs[2 * n:]
        x, y, c = _coords()
        me, sibling = (x, y, c), (x, y, 1 - c)
        chips = [(1 - x, y), (x, 1 - y), (1 - x, 1 - y)]

        def copy(a, kk, block, to, src=None):
            slot = outs[a].at[_flat(*block)]
            return pltpu.make_async_remote_copy(
                src_ref=slot if src is None else src, dst_ref=slot,
                send_sem=send.at[a, kk], recv_sem=recv.at[a, kk],
                device_id=to, device_id_type=MESH)

        mine = [pltpu.make_async_copy(ins[a], outs[a].at[_flat(*me)], lsem.at[a]) for a in range(n)]
        for cp in mine:
            cp.start()
        first = []
        for a in range(n):
            first.append(copy(a, 0, me, sibling, src=ins[a]))
            first += [copy(a, 1 + j, me, (*chip, c), src=ins[a]) for j, chip in enumerate(chips)]
        for cp in first:
            cp.start()
        passed = []
        for j, chip in enumerate(chips):
            for a in range(n):
                copy(a, 1 + j, (*chip, c), me).wait_recv()
                cp = copy(a, 4 + j, (*chip, c), sibling)
                cp.start()
                passed.append(cp)
        for a in range(n):
            copy(a, 0, sibling, me).wait_recv()
        for j, chip in enumerate(chips):
            for a in range(n):
                copy(a, 4 + j, (*chip, 1 - c), me).wait_recv()
        for cp in first + passed:
            cp.wait_send()
        for cp in mine:
            cp.wait()

    any_spec = pl.BlockSpec(memory_space=pl.ANY)
    return pl.pallas_call(
        body, name=name,
        in_specs=[any_spec] * n, out_specs=[any_spec] * n,
        out_shape=[jax.ShapeDtypeStruct((N_DEV,) + a.shape, a.dtype) for a in arrs],
        scratch_shapes=[pltpu.SemaphoreType.DMA((n, 7)), pltpu.SemaphoreType.DMA((n, 7)),
                        pltpu.SemaphoreType.DMA((n,))],
    )(*arrs)


def _peer_list():
    x, y, c = _coords()
    return [((1 - x if r & 4 else x), (1 - y if r & 2 else y), (1 - c if r & 1 else c)) for r in range(1, N_DEV)]


def _copy_plan(mode, src, land):
    x, y, c = _coords()
    me = _flat(x, y, c)
    if mode == "gather":
        return [(src, land.at[me], peer) for peer in _peer_list()]
    if mode == "exchange":
        return [(src.at[_flat(*peer)], land.at[me], peer) for peer in _peer_list()]
    if mode == "pair":
        return [(src.at[_flat(q // 2, q % 2, 1 - c)], land.at[q], (x, y, 1 - c)) for q in range(N_DEV // 2)]
    chips = [((1 - x if r & 2 else x), (1 - y if r & 1 else y)) for r in range(1, N_DEV // 2)]
    if mode == "chips":
        return [(src.at[2 * qx + qy], land.at[2 * x + y], (qx, qy, c)) for qx, qy in chips]
    if mode == "spread":
        return [(src, land.at[me], (x, y, 1 - c))] + [(src, land.at[me], (qx, qy, c)) for qx, qy in chips]
    assert mode == "forward"
    return [(land.at[_flat(qx, qy, c)], land.at[_flat(qx, qy, c)], (x, y, 1 - c)) for qx, qy in chips]


N_COPIES = dict(gather=N_DEV - 1, exchange=N_DEV - 1, pair=N_DEV // 2, chips=N_DEV // 2 - 1, spread=N_DEV // 2,
                forward=N_DEV // 2 - 1)


def _land_shape(mode, shape):
    return {"gather": (N_DEV,) + shape, "spread": (N_DEV,) + shape, "exchange": shape,
            "pair": (N_DEV // 2,) + shape[1:], "chips": shape}[mode]


HBM_SPEC = pl.BlockSpec(memory_space=pltpu.HBM)
SEM_SPEC = pl.BlockSpec(memory_space=pltpu.SEMAPHORE)
ANY_SPEC = pl.BlockSpec(memory_space=pl.ANY)
SIDE_EFFECT = pltpu.SideEffectType.DATAFLOW_SIDE_EFFECTING


def _async_start(groups, modes, after, name):
    modes = [modes] * len(groups) if isinstance(modes, str) else list(modes)
    arrs = [(a, m) for g, m in zip(groups, modes) for a in g]
    n = len(arrs)
    fresh = [i for i, (_, m) in enumerate(arrs) if m != "forward"]

    def body(*refs):
        srcs, new_lands = refs[:n], refs[n:n + len(fresh)]
        outs = refs[n + len(fresh) + 1:]
        lands = list(srcs)
        for k, i in enumerate(fresh):
            lands[i] = new_lands[k]
        for ai, (_, mode) in enumerate(arrs):
            for src_ref, dst_ref, peer in _copy_plan(mode, srcs[ai], lands[ai]):
                pltpu.make_async_remote_copy(src_ref=src_ref, dst_ref=dst_ref, send_sem=outs[2 * ai],
                                             recv_sem=outs[2 * ai + 1], device_id=peer, device_id_type=MESH).start()
        outs[-1][...] = jnp.zeros(outs[-1].shape, F32)

    land_shapes = [(_land_shape(arrs[i][1], arrs[i][0].shape), arrs[i][0].dtype) for i in fresh]
    n_buf = n + len(fresh)
    out_shape = [pltpu.SemaphoreType.DMA(())] * (2 * n)
    out_shape += [pltpu.HBM(a.shape, a.dtype) for a, _ in arrs]
    out_shape += [pltpu.HBM(shape, dt) for shape, dt in land_shapes]
    out_shape.append(jax.ShapeDtypeStruct((8, LANES), F32))
    res = pl.pallas_call(
        body, name=name, out_shape=tuple(out_shape),
        in_specs=[HBM_SPEC] * n_buf + [ANY_SPEC],
        out_specs=tuple([SEM_SPEC] * (2 * n) + [HBM_SPEC] * n_buf + [pl.BlockSpec(memory_space=pltpu.VMEM)]),
        input_output_aliases={i: 2 * n + i for i in range(n_buf)},
        compiler_params=pltpu.CompilerParams(has_side_effects=SIDE_EFFECT),
    )(*[pltpu.with_memory_space_constraint(a, pltpu.HBM) for a, _ in arrs],
      *[pltpu.with_memory_space_constraint(lax.empty(shape, dt), pltpu.HBM) for shape, dt in land_shapes],
      after)
    sems, thru = res[:2 * n], res[2 * n:-1]
    land_of = {i: thru[n + k] for k, i in enumerate(fresh)}
    states, idx = [], 0
    for g, mode in zip(groups, modes):
        ids = range(idx, idx + len(g))
        idx += len(g)
        states.append(([sems[2 * i] for i in ids], [sems[2 * i + 1] for i in ids],
                       None if mode == "forward" else [thru[i] for i in ids],
                       [land_of.get(i, thru[i]) for i in ids], mode))
    return states, res[-1]


def _async_wait(state, after, name):
    sends, recvs, srcs, lands, mode = state
    g = len(lands)
    bufs = (list(srcs) if srcs is not None else []) + list(lands)
    nb = len(bufs)

    def body(*refs):
        l_refs, sems = refs[nb - g:nb], refs[nb:nb + 2 * g]
        for ai in range(g):
            moved = l_refs[ai].at[pl.ds(0, N_COPIES[mode])]
            cp = pltpu.make_async_remote_copy(src_ref=moved, dst_ref=moved, send_sem=sems[ai], recv_sem=sems[g + ai],
                                              device_id=_coords(), device_id_type=MESH)
            cp.wait_send()
            cp.wait_recv()

    res = pl.pallas_call(
        body, name=name,
        out_shape=tuple(pltpu.HBM(a.shape, a.dtype) for a in bufs),
        in_specs=[HBM_SPEC] * nb + [SEM_SPEC] * (2 * g) + [ANY_SPEC],
        out_specs=tuple([HBM_SPEC] * nb),
        input_output_aliases={i: i for i in range(nb)},
        compiler_params=pltpu.CompilerParams(has_side_effects=SIDE_EFFECT),
    )(*bufs, *sends, *recvs, after)
    return (list(res[:nb - g]) if srcs is not None else None), list(res[nb - g:])


def _add_sibling(mine, theirs, core):
    def body(c_ref, a_ref, b_ref, o_ref):
        o_ref[...] = (a_ref[...].astype(F32) + b_ref[...].astype(F32)).astype(o_ref.dtype)

    blk = (1,) + mine.shape[1:]
    return pl.pallas_call(
        body, name="add_sibling", out_shape=jax.ShapeDtypeStruct(theirs.shape, mine.dtype),
        grid_spec=pltpu.PrefetchScalarGridSpec(
            num_scalar_prefetch=1, grid=(theirs.shape[0],),
            in_specs=[pl.BlockSpec(blk, lambda q, c: (2 * q + c[0], 0, 0)), pl.BlockSpec(blk, lambda q, c: (q, 0, 0))],
            out_specs=pl.BlockSpec(blk, lambda q, c: (q, 0, 0))),
        compiler_params=_params(("parallel",)),
    )(core, mine, theirs)


def _with_own(land, own, me):
    return lax.dynamic_update_index_in_dim(land, own, me, 0)


IN_SPLITS = (512, 512, 512, 8, 768, 256, 32, 1024, 1024)


def _from_shards(g, fn, out_widths, name, own=None, slot=None):
    _, k, n = g.shape
    tr = min(k, 256)

    def body(*refs):
        if own is not None:
            s_ref, g_ref, own_ref = refs[:3]
            cols = [jnp.where(s_ref[0] == j, own_ref[...], g_ref[j]) for j in range(N_DEV)]
        else:
            g_ref = refs[0]
            cols = [g_ref[j] for j in range(N_DEV)]
        for o_ref, val in zip(refs[-len(out_widths):], fn(jnp.concatenate(cols, axis=1))):
            o_ref[...] = val

    in_specs = [pl.BlockSpec((N_DEV, tr, n), lambda i, *_: (0, i, 0))]
    out_spec = [pl.BlockSpec((tr, wd), lambda i, *_: (i, 0)) for wd in out_widths]
    out_shape = [jax.ShapeDtypeStruct((k, wd), g.dtype) for wd in out_widths]
    if own is None:
        return pl.pallas_call(body, name=name, grid=(k // tr,), in_specs=in_specs, out_specs=out_spec,
                              out_shape=out_shape, compiler_params=_params(("parallel",)))(g)
    in_specs.append(pl.BlockSpec((tr, n), lambda i, *_: (i, 0)))
    return pl.pallas_call(
        body, name=name, out_shape=out_shape, compiler_params=_params(("parallel",)),
        grid_spec=pltpu.PrefetchScalarGridSpec(num_scalar_prefetch=1, grid=(k // tr,), in_specs=in_specs, out_specs=out_spec),
    )(slot, g, own)


def _unshard_cols(g, own=None, slot=None):
    return _from_shards(g, lambda full: (full,), [N_DEV * g.shape[2]], "unshard_cols_%d" % g.shape[2], own, slot)[0]


FFN_T = D_FF // 2
FFN_SHARD = 2 * D_FF // N_DEV


def _unshard_ffn_in(g, own=None, slot=None):
    def pairs(full):
        parts = []
        for j in range(D_FF // FFN_T):
            parts += [full[:, j * FFN_T:(j + 1) * FFN_T], full[:, D_FF + j * FFN_T:D_FF + (j + 1) * FFN_T]]
        return (jnp.concatenate(parts, axis=1),)

    return _from_shards(g, pairs, [2 * D_FF], "unshard_ffn_in", own, slot)[0]


def _shard_ffn_in_t(wt):
    tc = 256

    def body(w_ref, o_ref):
        x = w_ref[...]
        nb = D_FF // FFN_T
        full = jnp.concatenate([x[(2 * j + half) * FFN_T:(2 * j + half + 1) * FFN_T]
                                for half in range(2) for j in range(nb)], axis=0)
        for j in range(N_DEV):
            o_ref[j] = full[j * FFN_SHARD:(j + 1) * FFN_SHARD]

    return pl.pallas_call(
        body, name="shard_ffn_in_t", grid=(D // tc,),
        in_specs=[pl.BlockSpec((2 * D_FF, tc), lambda i: (0, i))],
        out_specs=pl.BlockSpec((N_DEV, FFN_SHARD, tc), lambda i: (0, 0, i)),
        out_shape=jax.ShapeDtypeStruct((N_DEV, FFN_SHARD, D), wt.dtype),
        compiler_params=_params(("parallel",)),
    )(wt)


def _shard_cols(w):
    k, n = w.shape[0], w.shape[1] // N_DEV
    tr = min(k, 256)

    def body(w_ref, o_ref):
        full = w_ref[...]
        for j in range(N_DEV):
            o_ref[j] = full[:, j * n:(j + 1) * n]

    return pl.pallas_call(
        body, name="shard_cols_%d" % n, grid=(k // tr,),
        in_specs=[pl.BlockSpec((tr, N_DEV * n), lambda i: (i, 0))],
        out_specs=pl.BlockSpec((N_DEV, tr, n), lambda i: (0, i, 0)),
        out_shape=jax.ShapeDtypeStruct((N_DEV, k, n), w.dtype),
        compiler_params=_params(("parallel",)),
    )(w)


IN_OFFS = tuple(sum(IN_SPLITS[:i]) for i in range(len(IN_SPLITS) + 1))
IN_SHARD = IN_OFFS[-1] // N_DEV
REGROUP_ROWS = 256
MISC_AT = Q_LORA + KV_LORA + 2 * D
A_COLS = MISC_AT + LANES
A_TILE = A_COLS
B_COLS = 3 * HEADS * HEAD_DIM
MISC_BLOCK = MISC_AT // LANES
KR_AT = 64


def _w_in_regroup(g, own=None, slot=None):
    def groups(full):
        fq, fk, fv, wf, cq, ckv, kr, gf, gm = [full[:, IN_OFFS[i]:IN_OFFS[i + 1]] for i in range(9)]
        rows = full.shape[0]
        gap = jnp.zeros((rows, KR_AT - HEADS), BF16)
        tail = jnp.zeros((rows, LANES - KR_AT - ROPE_DIM), BF16)
        return jnp.concatenate([cq, ckv, gf, gm, wf, gap, kr, tail], axis=1), jnp.concatenate([fq, fk, fv], axis=1)

    return _from_shards(g, groups, [A_COLS, B_COLS], "w_in_regroup", own, slot)


def _w_in_ungroup(da, db_):
    def body(a_ref, b_ref, o_ref):
        a = a_ref[...]
        lora = Q_LORA + KV_LORA
        full = jnp.concatenate([b_ref[...], a[:, MISC_AT:MISC_AT + HEADS], a[:, :lora],
                                a[:, MISC_AT + KR_AT:MISC_AT + KR_AT + ROPE_DIM], a[:, lora:MISC_AT]], axis=1)
        for j in range(N_DEV):
            o_ref[j] = full[:, j * IN_SHARD:(j + 1) * IN_SHARD]

    tr = REGROUP_ROWS
    return pl.pallas_call(
        body, name="w_in_ungroup", grid=(D // tr,),
        in_specs=[pl.BlockSpec((tr, A_COLS), lambda i: (i, 0)), pl.BlockSpec((tr, B_COLS), lambda i: (i, 0))],
        out_specs=pl.BlockSpec((N_DEV, tr, IN_SHARD), lambda i: (0, i, 0)),
        out_shape=jax.ShapeDtypeStruct((N_DEV, D, IN_SHARD), BF16),
        compiler_params=_params(("parallel",)),
    )(da, db_)


def _prepare_weights(g, own=None, slot=None):
    w = {}
    if own is not None:
        small = ("w_uq", "w_ukv", "w_out", "w_ffn_out")
        g = {n: (_with_own(a, own[n], slot[0]) if n in small else a) for n, a in g.items()}
    pick = (lambda n: (own[n], slot)) if own is not None else (lambda n: (None, None))
    if "w_in" in g:
        w["w_a"], w["w_b"] = _w_in_regroup(g["w_in"], *pick("w_in"))
    if "w_uq" in g:
        w_uq = g["w_uq"].reshape(Q_LORA, HEADS, 96)
        w["w_uq"] = jnp.pad(w_uq, ((0, 0), (0, 0), (0, 32))).reshape(Q_LORA, HEADS * LANES)
        ukv = g["w_ukv"]
        w["w_k"] = jnp.transpose(jnp.pad(ukv[:, :, :64], ((0, 0), (0, 0), (0, 64))), (1, 0, 2)).reshape(KV_LORA, HEADS * LANES)
        w["w_v"] = jnp.transpose(ukv[:, :, 64:], (1, 0, 2)).reshape(KV_LORA, HEADS * HEAD_DIM)
    if "w_out" in g:
        w["w_pf"] = _unshard_cols(g["w_proj_fox"], *pick("w_proj_fox"))
        w["w_pm"] = _unshard_cols(g["w_proj_mla"], *pick("w_proj_mla"))
        w["w_out"] = g["w_out"].reshape(D, D)
    if "w_ffn_in" in g:
        w["w_ffn_in"] = _unshard_ffn_in(g["w_ffn_in"], *pick("w_ffn_in"))
        w["w_ffn_out"] = g["w_ffn_out"].reshape(D_FF, D)
    return w


def _shard_grads(dw):
    out = {}
    if "w_a" in dw:
        out["w_in"] = _w_in_ungroup(dw["w_a"], dw["w_b"])
    if "w_uq" in dw:
        w_uq = dw["w_uq"].reshape(Q_LORA, HEADS, LANES)[:, :, :96].reshape(Q_LORA, Q_LORA)
        out["w_uq"] = w_uq.reshape(N_DEV, Q_LORA // N_DEV, Q_LORA)
        k_part = dw["w_k"].reshape(KV_LORA, HEADS, LANES)[:, :, :64]
        v_part = dw["w_v"].reshape(KV_LORA, HEADS, HEAD_DIM)
        out["w_ukv"] = jnp.transpose(jnp.concatenate([k_part, v_part], axis=2), (1, 0, 2))
    if "w_out" in dw:
        out["w_proj_fox"] = _shard_cols(dw["w_pf"])
        out["w_proj_mla"] = _shard_cols(dw["w_pm"])
        out["w_out"] = dw["w_out"].reshape(N_DEV, D // N_DEV, D)
    if "w_ffn_in" in dw:
        out["w_ffn_in"] = _shard_ffn_in_t(dw["w_ffn_in"])
        out["w_ffn_out"] = dw["w_ffn_out"].reshape(N_DEV, D_FF // N_DEV, D)
    return out


def _fwd_bwd(x, pos, mod, target, w, vec, wts, send, relay):
    shift_mix, scale_mix, gate_mix, shift_ffn, scale_ffn, gate_ffn = [mod[:, i * D:(i + 1) * D] for i in range(6)]
    g_pre_mix, g_post_mix, g_pre_ffn, g_post_ffn = vec["g_pre_mix"], vec["g_post_mix"], vec["g_pre_ffn"], vec["g_post_ffn"]
    g_q, g_kv = vec["g_q_lora"], vec["g_kv_lora"]

    inv_freq = 1.0 / (ROPE_THETA ** (jnp.arange(0, ROPE_DIM, 2, dtype=F32) / ROPE_DIM))
    invf = jnp.concatenate([jnp.zeros((64,), F32), inv_freq, inv_freq, jnp.zeros((32,), F32)]).reshape(1, LANES)
    ct, sa, sb = _rope_tables(pos, invf)

    def pre1(xv, g, sc, sh):
        return (xv * _rstd(xv) * g) * (1.0 + sc) + sh
    proj_a, h = _mm_epi(x, w["w_a"], "nn", A_TILE, lambda r: ((r,), ()), "in_proj_a", 512, outs=[(A_COLS, A_TILE, F32)],
                        pro=(pre1, [g_pre_mix, scale_mix, shift_mix], 0))
    qkv = _mm(h, w["w_b"], "nn", BF16, "in_proj_b")

    def lora_norm(cv, g):
        return cv * _rstd(cv) * g
    w = {**w, **wts("lora", qkv)}
    tables = [(ct, LANES), (sa, LANES), (sb, LANES)]

    def rope_q(qv, c_, a_, b_):
        return (jnp.concatenate([_rope(qv[:, hd * LANES:(hd + 1) * LANES], c_, a_, b_) for hd in range(HEADS)], axis=1),), ()
    q_m, cqn = _mm_epi(proj_a, w["w_uq"], "nn", D, rope_q, "mla_uq", 512, rows=tables, outs=[(D, D, BF16)],
                       pro=(lora_norm, [g_q], 0))

    def rope_k(kv, misc, c_, a_, b_):
        lane = lax.broadcasted_iota(jnp.int32, (1, LANES), 1)
        kpe = jnp.where((lane >= 64) & (lane < 96), _rope(misc, c_, a_, b_), 0.0)
        return (jnp.concatenate([kv[:, hd * LANES:(hd + 1) * LANES] + kpe for hd in range(HEADS)], axis=1),), ()
    k_m, ckvn = _mm_epi(proj_a, w["w_k"], "nn", D, rope_k, "mla_uk", 512, rows=[(proj_a, LANES, MISC_BLOCK)] + tables,
                        outs=[(D, D, BF16)], pro=(lora_norm, [g_kv], Q_LORA // KV_LORA))
    v_m = _mm(ckvn, w["w_v"], "nn", BF16, "mla_uv")

    bf = jnp.transpose(vec["b_forget"])
    zt, neg_f = _fox_gates(proj_a, MISC_BLOCK, bf)
    bias = neg_f.reshape(HEADS, N_ATT, 1, ATT_T)
    o_b, lse_b = _attn_fwd(q_m, 0, k_m, 0, v_m, 0, 2 * LANES, 1.0 / math.sqrt(64 + ROPE_DIM), None, "mla_attn")
    o_a, lse_a = _attn_fwd(qkv, 0, qkv, 4, qkv, 8, LANES, 1.0 / math.sqrt(HEAD_DIM), bias, "fox_attn",
                           dep=wts("relay_proj", o_b)["tok"])

    w = {**w, **wts("proj", o_a)}
    pa = _mm(o_a, w["w_pf"], "nn", BF16, "proj_fox", dep=wts("relay_ffn", o_a)["tok"])

    def merge(pb_, gf, gm, pa_):
        return (_sigmoid(gf) * pa_ + _sigmoid(gm) * pb_, pb_), ()
    merged, pb = _mm_epi(o_b, w["w_pm"], "nn", 512, merge, "proj_mla", 1024,
                         rows=[(proj_a, 512, 2), (proj_a, 512, 4), (pa, 512)], outs=[(D, 512, BF16), (D, 512, BF16)])
    def post1(yv, xv, gate, gpost, gpre, sc, sh):
        x1 = xv + gate * (yv * _rstd(yv) * gpost)
        return (x1, (x1 * _rstd(x1) * gpre) * (1.0 + sc) + sh, yv), ()
    x1, h2, y = _mm_epi(merged, w["w_out"], "nn", D, post1, "out_proj", 512, rows=[(x, D)],
                        vecs=[gate_mix, g_post_mix, g_pre_ffn, scale_ffn, shift_ffn],
                        outs=[(D, D, F32), (D, D, BF16), (D, D, F32)])
    w = {**w, **wts("ffn", h2)}

    def swiglu(r):
        g, u = r[:, :FFN_T], r[:, FFN_T:]
        return (g * _sigmoid(g) * u, r), ()
    act, gu = _mm_epi(h2, w["w_ffn_in"], "nn", 2 * FFN_T, swiglu, "ffn_in", 512,
                      outs=[(D_FF, FFN_T, BF16), (2 * D_FF, 2 * FFN_T, BF16)])

    def head(y2v, x1v, tv, gate, gpost):
        r = _rstd(y2v)
        yn = y2v * r
        n2 = yn * gpost
        err = (x1v + gate * n2) - tv
        dx2 = err * (1.0 / D)
        dn2 = dx2 * gate
        dy2 = _norm_bwd(dn2 * gpost, yn, r)
        return (dx2, dy2), (_colsum(err * err), _colsum(dx2 * n2), _colsum(dn2 * yn))
    dx2, dy2, err_cols, d_gate_ffn, d_g_post_ffn = _mm_epi(
        act, w["w_ffn_out"], "nn", D, head, "ffn_out", 512, rows=[(x1, D), (target, D)], vecs=[gate_ffn, g_post_ffn],
        outs=[(D, D, F32), (D, D, BF16)], sums=[D, D, D])

    def swiglu_bwd(da, guv):
        g, u = guv[:, :FFN_T].astype(F32), guv[:, FFN_T:].astype(F32)
        sg = _sigmoid(g)
        return (jnp.concatenate([da * u * (sg * (1.0 + g * (1.0 - sg))), da * (g * sg)], axis=1),), ()
    (dgu,) = _mm_epi(dy2, w["w_ffn_out"], "nt", FFN_T, swiglu_bwd, "ffn_out_dx", 512, rows=[(gu, 2 * FFN_T)],
                     outs=[(2 * D_FF, 2 * FFN_T, BF16)])
    dw = {"w_ffn_out": _mm(act, dy2, "tn", BF16, "ffn_out_dw")}
    dw["w_ffn_in"] = _mm(dgu, h2, "tn", BF16, "ffn_in_dw")
    tok = send({n: dw.pop(n) for n in ("w_ffn_in", "w_ffn_out")})

    def mid(dh, x1v, dx2v, yv, gpre, sc, gate, gpost):
        r2 = _rstd(x1v)
        x1n = x1v * r2
        t = dh * x1n
        dx1 = dx2v + _norm_bwd(dh * (gpre * (1.0 + sc)), x1n, r2)
        ry = _rstd(yv)
        yn = yv * ry
        dn1 = dx1 * gate
        dy = _norm_bwd(dn1 * gpost, yn, ry)
        sums = (_colsum(dh), _colsum(t) * gpre, _colsum(t) * (1.0 + sc), _colsum(dx1 * (yn * gpost)), _colsum(dn1 * yn))
        return (dx1, dy), sums
    dx1, dy, d_shift_ffn, d_scale_ffn, d_g_pre_ffn, d_gate_mix, d_g_post_mix = _mm_epi(
        dgu, w["w_ffn_in"], "nt", D, mid, "ffn_in_dx", 512, rows=[(x1, D), (dx2, D), (y, D)],
        vecs=[g_pre_ffn, scale_ffn, gate_mix, g_post_mix], outs=[(D, D, F32), (D, D, BF16)], sums=[D] * 5, dep=tok)

    dw["w_out"] = _mm(merged, dy, "tn", BF16, "out_proj_dw")

    def merge_bwd(dm, gf, gm, pa_, pb_):
        sf, sm = _sigmoid(gf), _sigmoid(gm)
        return (dm * sf, dm * sm, dm * pa_ * (sf * (1.0 - sf)), dm * pb_ * (sm * (1.0 - sm))), ()
    dpa, dpb, dgf, dgm = _mm_epi(dy, w["w_out"], "nt", 512, merge_bwd, "out_proj_dx", 1024,
                                 rows=[(proj_a, 512, 2), (proj_a, 512, 4), (pa, 512), (pb, 512)],
                                 outs=[(D, 512, BF16)] * 4)
    do_a = _mm(dpa, w["w_pf"], "nt", BF16, "proj_fox_dx")
    do_b = _mm(dpb, w["w_pm"], "nt", BF16, "proj_mla_dx")
    dw["w_pf"] = _mm(o_a, dpa, "tn", BF16, "proj_fox_dw")
    dw["w_pm"] = _mm(o_b, dpb, "tn", BF16, "proj_mla_dw")
    tok = send({n: dw.pop(n) for n in ("w_out", "w_pf", "w_pm")})

    sc_a, sc_b = 1.0 / math.sqrt(HEAD_DIM), 1.0 / math.sqrt(64 + ROPE_DIM)
    dq_a, dk_a, dv_a, dbias = _attn_grad(qkv, 0, qkv, 4, qkv, 8, do_a, lse_a, LANES, sc_a, bias, BF16, "fox_attn_bwd",
                                         dep=tok)
    dq_m, dk_m, dv_m = _attn_grad(q_m, 0, k_m, 0, v_m, 0, do_b, lse_b, 2 * LANES, sc_b, None, F32, "mla_attn_bwd")

    def mla_rope_bwd(dq, dk, c_, a_, b_):
        lane = lax.broadcasted_iota(jnp.int32, (1, LANES), 1)
        dqs = [_rope_t(dq[:, hd * LANES:(hd + 1) * LANES], c_, a_, b_) for hd in range(HEADS)]
        dkpe = dk[:, 0:LANES]
        for hd in range(1, HEADS):
            dkpe = dkpe + dk[:, hd * LANES:(hd + 1) * LANES]
        dkpe = jnp.where((lane >= 64) & (lane < 96), dkpe, 0.0)
        dkr = jnp.where((lane >= 64) & (lane < 96), _rope_t(dkpe, c_, a_, b_), 0.0)
        return (jnp.concatenate(dqs, axis=1), dk, dkr), ()
    dqb, dkb, dkr = _rowwise(mla_rope_bwd, [(dq_m, D, 0), (dk_m, D, 0), (ct, LANES, 0), (sa, LANES, 0), (sb, LANES, 0)],
                             [], [(D, BF16), (D, BF16), (LANES, F32)], [], "mla_rope_bwd")
    def lora_q_bwd(dq, cq, gq):
        rq = _rstd(cq)
        cqh = cq * rq
        return (_norm_bwd(dq * gq, cqh, rq),), (_colsum(dq * cqh),)
    dcq, d_g_q = _mm_epi(dqb, w["w_uq"], "nt", Q_LORA, lora_q_bwd, "mla_uq_dx", 512, rows=[(proj_a, Q_LORA, 0)],
                         vecs=[g_q], outs=[(Q_LORA, Q_LORA, BF16)], sums=[Q_LORA])

    def lora_kv_bwd(dv_part, dk_part, ckv, gkv):
        dkv = dv_part + dk_part
        rk = _rstd(ckv)
        ckh = ckv * rk
        return (_norm_bwd(dkv * gkv, ckh, rk),), (_colsum(dkv * ckh),)
    dckv, d_g_kv = _mm_epi(dv_m, w["w_v"], "nt", KV_LORA, lora_kv_bwd, "mla_uv_dx", 1024,
                           rows=[(_mm(dkb, w["w_k"], "nt", F32, "mla_uk_dx"), KV_LORA), (proj_a, KV_LORA, 3)],
                           vecs=[g_kv], outs=[(KV_LORA, KV_LORA, BF16)], sums=[KV_LORA])

    dzt, d_bf = _fox_gates_bwd(dbias.reshape(HEADS, S), zt, bf)
    dmisc = (dkr + jnp.pad(jnp.transpose(dzt), ((0, 0), (0, LANES - HEADS)))).astype(BF16)
    dproj_a = [dcq, dckv, dgf, dgm, dmisc]
    dqkv = [dq_a, dk_a, dv_a]
    dw["w_a"] = _mm_cat(h, dproj_a, "tn", BF16, "in_proj_a_dw")
    dw["w_b"] = _mm_cat(h, dqkv, "tn", BF16, "in_proj_b_dw")
    tok = send(dw, True)
    dh_a = _mm_cat(dproj_a, w["w_a"], "nt", F32, "in_proj_a_dx", dep=tok)
    tok = relay(dh_a)
    tok = send({"w_uq": _mm(cqn, dqb, "tn", BF16, "mla_uq_dw", dep=tok),
                "w_k": _mm(ckvn, dkb, "tn", BF16, "mla_uk_dw", dep=tok),
                "w_v": _mm(ckvn, dv_m, "tn", BF16, "mla_uv_dw", dep=tok)}, late=True)

    def first(dh_b, dh_a, xv, dx1v, gpre, sc):
        dhv = dh_b + dh_a
        r = _rstd(xv)
        xn = xv * r
        t = dhv * xn
        dx = dx1v + _norm_bwd(dhv * (gpre * (1.0 + sc)), xn, r)
        return (dx,), (_colsum(dhv), _colsum(t) * gpre, _colsum(t) * (1.0 + sc))
    grad_x, d_shift_mix, d_scale_mix, d_g_pre_mix = _mm_epi(
        dqkv, w["w_b"], "nt", D, first, "in_proj_b_dx", 512,
        rows=[(dh_a, D), (x, D), (dx1, D)],
        vecs=[g_pre_mix, scale_mix], outs=[(D, D, F32)], sums=[D] * 3, dep=tok)

    dmod = jnp.concatenate([d_shift_mix, d_scale_mix, d_gate_mix, d_shift_ffn, d_scale_ffn, d_gate_ffn], axis=1)
    small = dict(dmod=dmod, g_pre_mix=d_g_pre_mix, g_post_mix=d_g_post_mix, g_pre_ffn=d_g_pre_ffn,
                 g_post_ffn=d_g_post_ffn, g_q_lora=d_g_q, g_kv_lora=d_g_kv,
                 b_forget=jnp.pad(jnp.transpose(d_bf), ((0, 0), (0, LANES - HEADS))), err=err_cols)
    return grad_x, small


SMALL_ORDER = ("dmod", "g_pre_mix", "g_post_mix", "g_pre_ffn", "g_post_ffn", "g_q_lora", "g_kv_lora", "b_forget", "err")
SMALL_PARAM = {"dmod": "b_ada"}
MATRICES = ("w_in", "w_uq", "w_ukv", "w_proj_fox", "w_proj_mla", "w_out", "w_ffn_in", "w_ffn_out")
WEIGHTS = ("w_ada", "b_ada", "g_pre_mix", "g_post_mix", "g_pre_ffn", "g_post_ffn", "w_in", "b_forget", "g_q_lora",
           "w_uq", "g_kv_lora", "w_ukv", "w_proj_fox", "w_proj_mla", "w_out", "w_ffn_in", "w_ffn_out")


def kernel(x, c, positions, w_ada, b_ada, g_pre_mix, g_post_mix, g_pre_ffn, g_post_ffn, w_in, b_forget, g_q_lora, w_uq, g_kv_lora, w_ukv, w_proj_fox, w_proj_mla, w_out, w_ffn_in, w_ffn_out, loss_target, m_w_ada, m_b_ada, m_g_pre_mix, m_g_post_mix, m_g_pre_ffn, m_g_post_ffn, m_w_in, m_b_forget, m_g_q_lora, m_w_uq, m_g_kv_lora, m_w_ukv, m_w_proj_fox, m_w_proj_mla, m_w_out, m_w_ffn_in, m_w_ffn_out, v_w_ada, v_b_ada, v_g_pre_mix, v_g_post_mix, v_g_pre_ffn, v_g_post_ffn, v_w_in, v_b_forget, v_g_q_lora, v_w_uq, v_g_kv_lora, v_w_ukv, v_w_proj_fox, v_w_proj_mla, v_w_out, v_w_ffn_in, v_w_ffn_out):
    prm = dict(w_ada=w_ada, b_ada=b_ada, g_pre_mix=g_pre_mix, g_post_mix=g_post_mix, g_pre_ffn=g_pre_ffn,
               g_post_ffn=g_post_ffn, w_in=w_in, b_forget=b_forget, g_q_lora=g_q_lora, w_uq=w_uq, g_kv_lora=g_kv_lora,
               w_ukv=w_ukv, w_proj_fox=w_proj_fox, w_proj_mla=w_proj_mla, w_out=w_out, w_ffn_in=w_ffn_in, w_ffn_out=w_ffn_out)
    mom = dict(w_ada=m_w_ada, b_ada=m_b_ada, g_pre_mix=m_g_pre_mix, g_post_mix=m_g_post_mix, g_pre_ffn=m_g_pre_ffn,
               g_post_ffn=m_g_post_ffn, w_in=m_w_in, b_forget=m_b_forget, g_q_lora=m_g_q_lora, w_uq=m_w_uq,
               g_kv_lora=m_g_kv_lora, w_ukv=m_w_ukv, w_proj_fox=m_w_proj_fox, w_proj_mla=m_w_proj_mla, w_out=m_w_out,
               w_ffn_in=m_w_ffn_in, w_ffn_out=m_w_ffn_out)
    var = dict(w_ada=v_w_ada, b_ada=v_b_ada, g_pre_mix=v_g_pre_mix, g_post_mix=v_g_post_mix, g_pre_ffn=v_g_pre_ffn,
               g_post_ffn=v_g_post_ffn, w_in=v_w_in, b_forget=v_b_forget, g_q_lora=v_g_q_lora, w_uq=v_w_uq,
               g_kv_lora=v_g_kv_lora, w_ukv=v_w_ukv, w_proj_fox=v_w_proj_fox, w_proj_mla=v_w_proj_mla, w_out=v_w_out,
               w_ffn_in=v_w_ffn_in, w_ffn_out=v_w_ffn_out)
    me = _flat(*_coords())
    slot = jnp.reshape(me, (1,)).astype(jnp.int32)

    own = {n: prm[n][0].astype(BF16) for n in MATRICES}
    no_dep = jnp.zeros((8, LANES), F32)
    (st_c, st_in), tok = _async_start([[c], [own["w_in"]]], ["gather", "spread"], no_dep, "gather_in_start")
    (c_own,), (c_land,) = _async_wait(st_c, tok, "gather_c_wait")
    c_all = _with_own(c_land, c_own, me).reshape(N_DEV, D)
    ada_cols = w_ada.shape[2]
    b_cols = lax.dynamic_slice(b_ada, (0, me * ada_cols), (1, ada_cols))
    mod_cols, silu_c = _mod_part(c_all, w_ada[0], b_cols)
    (mod_all,) = _all_gather([mod_cols], "gather_mod")

    (w_in_own,), (w_in_land,) = _async_wait(st_in, mod_all, "gather_in_wait")
    (st_in,), tok = _async_start([[w_in_land]], "forward", no_dep, "gather_in_forward")
    _, (w_in_land,) = _async_wait(st_in, tok, "gather_in_forward_wait")
    w = _prepare_weights({"w_in": w_in_land}, {"w_in": w_in_own}, slot)
    later = dict(lora=("w_uq", "w_ukv"), proj=("w_proj_fox", "w_proj_mla", "w_out"), ffn=("w_ffn_in", "w_ffn_out"))
    states, tok = _async_start([[own[n] for n in names] for names in later.values()], ["gather", "spread", "spread"],
                               w["w_b"], "gather_rest_start")
    gather_state = dict(zip(later, states))
    own_thru = {}

    def wts(group, after):
        if group.startswith("relay_"):
            name = group[len("relay_"):]
            own_thru[name], lands = _async_wait(gather_state[name], after, "gather_" + name + "_wait")
            (gather_state[name],), t = _async_start([lands], "forward", no_dep, "gather_" + name + "_forward")
            return {"tok": t}
        srcs, lands = _async_wait(gather_state[group], after, "gather_" + group + "_landed")
        srcs = own_thru.get(group, srcs)
        return _prepare_weights(dict(zip(later[group], lands)), dict(zip(later[group], srcs)), slot)

    sent, late_sent, last = [], [], {}

    def send(grads, final=False, late=False):
        shards = _shard_grads(grads)
        names = list(shards)
        (state,), t = _async_start([[shards[n] for n in names]], "pair" if final else "exchange", no_dep,
                                   "exchange_" + names[0] + "_start")
        if final:
            last.update(names=names, state=state)
        else:
            (late_sent if late else sent).append((names, state))
        return t

    def relay(after):
        srcs, lands = _async_wait(last["state"], after, "exchange_pair_wait")
        core = jnp.reshape(lax.axis_index("c"), (1,)).astype(jnp.int32)
        sums = [_add_sibling(src, land, core) for src, land in zip(srcs, lands)]
        (last["state"],), t = _async_start([sums], "chips", no_dep, "exchange_chips_start")
        return t

    mod = lax.dynamic_index_in_dim(mod_all, me, axis=1, keepdims=False).reshape(1, 6 * D) + tok[0, 0]

    vec = dict(g_pre_mix=g_pre_mix, g_post_mix=g_post_mix, g_pre_ffn=g_pre_ffn, g_post_ffn=g_post_ffn,
               g_q_lora=g_q_lora, g_kv_lora=g_kv_lora, b_forget=b_forget)
    pos = positions.astype(F32).reshape(S, 1)
    grad_x, small = _fwd_bwd(x[0], pos, mod, loss_target[0], w, vec, wts, send, relay)

    bundle = jnp.concatenate([small[n] for n in SMALL_ORDER], axis=1)
    (small_state,), tok = _async_start([[bundle]], "gather", jnp.zeros((8, LANES), F32), "gather_small_start")

    out = {}
    swap = lambda a: jnp.swapaxes(a, -1, -2)

    def update(n, land, src, sl):
        if n != "w_ffn_in":
            out[n] = _adamw(prm[n][0], mom[n][0], var[n][0], land, "adamw_" + n, src, sl)
            return out[n][0]
        res = _adamw(swap(prm[n][0]), swap(mom[n][0]), swap(var[n][0]), land, "adamw_" + n, src, sl)
        out[n] = tuple(swap(t) for t in res)
        return res[0]

    after = tok
    for names, state in sent:
        srcs, lands = _async_wait(state, after, "exchange_" + names[0] + "_wait")
        for n, src, land in zip(names, srcs, lands):
            after = update(n, land, src, slot)
    srcs, lands = _async_wait(last["state"], after, "exchange_chips_wait")
    for n, src, land in zip(last["names"], srcs, lands):
        after = update(n, land, src, slot // 2)
    for names, state in late_sent:
        srcs, lands = _async_wait(state, after, "exchange_" + names[0] + "_wait")
        for n, src, land in zip(names, srcs, lands):
            after = update(n, land, src, slot)

    (own_bundle,), (bundle_all,) = _async_wait(small_state, after, "gather_small_wait")
    bundle_all = _with_own(bundle_all, own_bundle, me)
    dmod_all = bundle_all[:, 0, :6 * D]
    dm_cols = lax.dynamic_slice(dmod_all, (0, me * ada_cols), (N_DEV, ada_cols))
    out["w_ada"] = _adamw_w_ada(w_ada[0], m_w_ada[0], v_w_ada[0], jnp.transpose(silu_c), dm_cols)

    offsets, off = {}, 0
    for n in SMALL_ORDER:
        offsets[n] = off
        off += small[n].shape[1]
    names = [SMALL_PARAM.get(n, n) for n in SMALL_ORDER if n != "err"]
    results, err = _adamw_rows(bundle_all, [offsets[n] for n in SMALL_ORDER if n != "err"],
                               [prm[n] for n in names], [mom[n] for n in names], [var[n] for n in names],
                               offsets["err"], D)
    out.update(zip(names, results))
    loss = 0.5 * jnp.sum(err) / D

    res = [loss, grad_x[None]]
    for kind in range(4):
        for n in WEIGHTS:
            t = out[n][kind]
            res.append(t[None] if prm[n].ndim == 3 else t)
    return tuple(res)
```

```python
import functools
import math

import jax
import jax.numpy as jnp
from jax import lax
from jax.experimental import pallas as pl
from jax.experimental.pallas import tpu as pltpu

F32 = jnp.float32
BF16 = jnp.bfloat16

N_DEV = 8
S = 2048
D = 1024
D_FF = 2816
HEADS = 8
HEAD_DIM = 64
Q_LORA = 768
KV_LORA = 256
ROPE_DIM = 32
ROPE_THETA = 10000.0
NORM_EPS = 1e-6
LANES = 128
VMEM_LIMIT = 56 * 1024 * 1024

ADAM_LR = 0.001
ADAM_B1 = 0.9
ADAM_B2 = 0.999
ADAM_EPS = 1e-08
ADAM_WD = 0.01
ADAM_STEP = 10

ATT_T = 256
LOG2E = 1.4426950408889634
N_ATT = S // ATT_T

NN = (((1,), (0,)), ((), ()))
NT = (((1,), (1,)), ((), ()))
TN = (((0,), (0,)), ((), ()))
MESH = pl.DeviceIdType.MESH


def _params(sem=None):
    return pltpu.CompilerParams(dimension_semantics=sem, vmem_limit_bytes=VMEM_LIMIT)


def _pick(n, cap):
    best = None
    for t in range(LANES, cap + 1, LANES):
        if n % t == 0:
            best = t
    return best if best is not None else n


def _mm(a, b, mode, out_dtype, name, acc=None, dep=None):
    if mode == "nn":
        (m, k), (k2, n), dn = a.shape, b.shape, NN
    elif mode == "nt":
        (m, k), (n, k2), dn = a.shape, b.shape, NT
    else:
        (k, m), (k2, n), dn = a.shape, b.shape, TN
    assert k == k2, (a.shape, b.shape, mode)
    tn = _pick(n, 1024)
    tm = _pick(m, 1536)
    osz = jnp.dtype(out_dtype).itemsize

    def need(tm_):
        blk = tm_ * k * 2 + tn * k * 2 + tm_ * tn * osz + (tm_ * tn * 4 if acc is not None else 0)
        return 2 * blk + tm_ * tn * 4
    while need(tm) > 36 * 1024 * 1024 and tm % 256 == 0:
        tm //= 2

    def body(*refs):
        a_ref, b_ref, o_ref = refs[0], refs[1], refs[-1]
        r = lax.dot_general(a_ref[...], b_ref[...], dn, preferred_element_type=F32)
        if acc is not None:
            r = r + refs[2][...]
        o_ref[...] = r.astype(o_ref.dtype)

    if mode == "tn":
        a_spec = pl.BlockSpec((k, tm), lambda i, j: (0, i))
    else:
        a_spec = pl.BlockSpec((tm, k), lambda i, j: (i, 0))
    if mode == "nt":
        b_spec = pl.BlockSpec((tn, k), lambda i, j: (j, 0))
    else:
        b_spec = pl.BlockSpec((k, tn), lambda i, j: (0, j))
    o_spec = pl.BlockSpec((tm, tn), lambda i, j: (i, j))
    in_specs = [a_spec, b_spec] + ([o_spec] if acc is not None else [])
    in_specs += [pl.BlockSpec(memory_space=pl.ANY)] if dep is not None else []
    args = (a, b) + ((acc,) if acc is not None else ()) + ((dep,) if dep is not None else ())
    return pl.pallas_call(
        body, name=name, grid=(m // tm, n // tn),
        in_specs=in_specs, out_specs=o_spec,
        out_shape=jax.ShapeDtypeStruct((m, n), out_dtype),
        compiler_params=_params(("parallel", "parallel")),
    )(*args)


def _offsets(widths):
    return [sum(widths[:p]) for p in range(len(widths))]


def _mm_cat(a, b, mode, out_dtype, name, dep=None):
    pieces = a if mode == "nt" else b
    widths = [p.shape[1] for p in pieces]
    offs = _offsets(widths)
    assert all(w_ % LANES == 0 for w_ in widths)
    resident = dict(pipeline_mode=pl.Buffered(1))
    if mode == "nt":
        m, (n, k) = pieces[0].shape[0], b.shape
        assert k == sum(widths)
        tm = _pick(m, 512)

        def body(*refs):
            b_ref, o_ref = refs[len(pieces)], refs[-1]
            r = None
            for p_ref, off, w_ in zip(refs, offs, widths):
                t = lax.dot_general(p_ref[...], b_ref[:, off:off + w_], NT, preferred_element_type=F32)
                r = t if r is None else r + t
            o_ref[...] = r.astype(o_ref.dtype)
        in_specs = [pl.BlockSpec((tm, w_), lambda i: (i, 0)) for w_ in widths]
        in_specs.append(pl.BlockSpec((n, k), lambda i: (0, 0), **resident))
        args = (*pieces, b)
    else:
        assert mode == "tn"
        (k, m), n = a.shape, sum(widths)
        tm = _pick(m, 512)

        def body(*refs):
            a_ref, o_ref = refs[0], refs[-1]
            for p_ref, off, w_ in zip(refs[1:], offs, widths):
                o_ref[:, off:off + w_] = lax.dot_general(
                    a_ref[...], p_ref[...], TN, preferred_element_type=F32).astype(o_ref.dtype)
        in_specs = [pl.BlockSpec((k, tm), lambda i: (0, i))]
        in_specs += [pl.BlockSpec((k, w_), lambda i: (0, 0), **resident) for w_ in widths]
        args = (a, *pieces)
    if dep is not None:
        in_specs.append(pl.BlockSpec(memory_space=pl.ANY))
        args += (dep,)
    return pl.pallas_call(
        body, name=name, grid=(m // tm,),
        in_specs=in_specs, out_specs=pl.BlockSpec((tm, n), lambda i: (i, 0)),
        out_shape=jax.ShapeDtypeStruct((m, n), out_dtype),
        compiler_params=_params(("parallel",)),
    )(*args)


def _mm_epi(a, b, mode, tnb, epi, name, tm, rows=(), vecs=(), outs=(), sums=(), pro=None, dep=None):
    pieces = list(a) if isinstance(a, (list, tuple)) else [a]
    assert len(pieces) == 1 or pro is None
    widths = [p.shape[1] for p in pieces]
    offs = _offsets(widths)
    m = pieces[0].shape[0]
    k, nb = (b.shape if mode == "nn" else b.shape[::-1])
    dn = NN if mode == "nn" else NT
    pro_fn, pro_vecs, a_off = pro if pro is not None else (None, (), 0)
    n_in = 2 + len(rows) + len(vecs)
    n_all = n_in + len(pro_vecs)
    sub = min(tm, 256)

    def body(*refs):
        a_refs, refs = refs[:len(pieces)], refs[len(pieces) - 1:]
        if pro is not None:
            a_out, a_scr = refs[-2:]
            refs = refs[:-2]

            @pl.when(pl.program_id(1) == 0)
            def _():
                a_scr[...] = pro_fn(refs[0][...], *[x[...] for x in refs[n_in:n_all]]).astype(BF16)
                a_out[...] = a_scr[...]
            a_ref = a_scr
        else:
            a_ref = refs[0]
        n_skip = n_all + (dep is not None)
        o_refs = refs[n_skip:n_skip + len(outs)]
        s_refs = refs[n_skip + len(outs):]
        if sums:
            @pl.when((pl.program_id(0) == 0) & (pl.program_id(1) == 0))
            def _():
                for s_ref in s_refs:
                    s_ref[...] = jnp.zeros(s_ref.shape, F32)
        for c in range(tm // sub):
            rs = slice(c * sub, (c + 1) * sub)
            if len(pieces) == 1:
                r = lax.dot_general(a_ref[rs, :], refs[1][...], dn, preferred_element_type=F32)
            else:
                r = None
                for p_ref, off, w_ in zip(a_refs, offs, widths):
                    b_part = refs[1][off:off + w_, :] if mode == "nn" else refs[1][:, off:off + w_]
                    t = lax.dot_general(p_ref[rs, :], b_part, dn, preferred_element_type=F32)
                    r = t if r is None else r + t
            o_vals, s_vals = epi(r, *[x[rs, :] for x in refs[2:2 + len(rows)]], *[x[...] for x in refs[2 + len(rows):n_in]])
            assert len(o_vals) == len(o_refs) and len(s_vals) == len(s_refs)
            for o_ref, val in zip(o_refs, o_vals):
                o_ref[rs, :] = val.astype(o_ref.dtype)
            for s_ref, val in zip(s_refs, s_vals):
                s_ref[...] += val

    once = dict(pipeline_mode=pl.Buffered(1)) if nb == tnb else {}
    if mode == "nn":
        b_spec = pl.BlockSpec((k, tnb), lambda i, j: (0, j), **once)
    else:
        b_spec = pl.BlockSpec((tnb, k), lambda i, j: (j, 0), **once)
    if len(pieces) == 1:
        in_specs = [pl.BlockSpec((tm, k), lambda i, j: (i, a_off)), b_spec]
    else:
        assert sum(widths) == k and all(w_ % LANES == 0 for w_ in widths)
        in_specs = [pl.BlockSpec((tm, w_), lambda i, j: (i, 0)) for w_ in widths] + [b_spec]
    rows = [tuple(r) + (0,) * (3 - len(r)) for r in rows]
    in_specs += [pl.BlockSpec((tm, w), functools.partial(lambda i, j, off: (i, j + off), off=off)) for _, w, off in rows]
    in_specs += [pl.BlockSpec(v.shape, lambda i, j: (0, 0)) for v in list(vecs) + list(pro_vecs)]
    in_specs += [pl.BlockSpec(memory_space=pl.ANY)] if dep is not None else []
    out_specs = [pl.BlockSpec((tm, w), lambda i, j: (i, j)) for _, w, _ in outs]
    out_specs += [pl.BlockSpec((1, w), lambda i, j: (0, 0)) for w in sums]
    out_shape = [jax.ShapeDtypeStruct((m, full), dt) for full, _, dt in outs]
    out_shape += [jax.ShapeDtypeStruct((1, w), F32) for w in sums]
    if pro is not None:
        out_specs.append(pl.BlockSpec((tm, k), lambda i, j: (i, 0)))
        out_shape.append(jax.ShapeDtypeStruct((m, k), BF16))
    return pl.pallas_call(
        body, name=name, grid=(m // tm, nb // tnb),
        in_specs=in_specs, out_specs=out_specs, out_shape=out_shape,
        scratch_shapes=[pltpu.VMEM((tm, k), BF16)] if pro is not None else [],
        compiler_params=_params(("arbitrary", "arbitrary") if sums else ("parallel", "arbitrary" if pro is not None else "parallel")),
    )(*pieces, b, *[r[0] for r in rows], *vecs, *pro_vecs, *([dep] if dep is not None else []))


def _rowwise(fn, row_ins, vec_ins, row_outs, sum_outs, name, tm=512):
    n_in = len(row_ins) + len(vec_ins)
    n_o = len(row_outs)
    rows = row_ins[0][0].shape[0]

    def body(*refs):
        vals = [r[...] for r in refs[:n_in]]
        outs = refs[n_in:]
        ro, so = fn(*vals)
        assert len(ro) == n_o and len(so) == len(sum_outs)
        for r, v in zip(outs[:n_o], ro):
            r[...] = v.astype(r.dtype)
        if sum_outs:
            @pl.when(pl.program_id(0) == 0)
            def _():
                for r in outs[n_o:]:
                    r[...] = jnp.zeros(r.shape, F32)
            for r, v in zip(outs[n_o:], so):
                r[...] += v

    in_specs = [pl.BlockSpec((tm, w), functools.partial(lambda i, b: (i, b), b=b)) for _, w, b in row_ins]
    in_specs += [pl.BlockSpec(v.shape, lambda i: (0, 0)) for v in vec_ins]
    out_specs = [pl.BlockSpec((tm, w), lambda i: (i, 0)) for w, _ in row_outs]
    out_specs += [pl.BlockSpec((1, w), lambda i: (0, 0)) for w in sum_outs]
    out_shape = [jax.ShapeDtypeStruct((rows, w), dt) for w, dt in row_outs]
    out_shape += [jax.ShapeDtypeStruct((1, w), F32) for w in sum_outs]
    return pl.pallas_call(
        body, name=name, grid=(rows // tm,),
        in_specs=in_specs, out_specs=out_specs, out_shape=out_shape,
        compiler_params=_params(("arbitrary",)),
    )(*[a for a, _, _ in row_ins], *vec_ins)


def _sigmoid(x):
    return 1.0 / (1.0 + jnp.exp(-x))


def _rstd(x):
    return lax.rsqrt(jnp.mean(x * x, axis=-1, keepdims=True) + NORM_EPS)


def _norm_bwd(dyn, xn, r):
    return r * (dyn - xn * jnp.mean(dyn * xn, axis=-1, keepdims=True))


def _colsum(x):
    return jnp.sum(x, axis=0, keepdims=True)


def _rope_tables(pos, invf):
    def fn(p, f):
        lane = lax.broadcasted_iota(jnp.int32, (1, LANES), 1)
        ang = p * f
        cs, sn = jnp.cos(ang), jnp.sin(ang)
        rot = (lane >= 64) & (lane < 96)
        ct = jnp.where(lane < 64, 1.0, jnp.where(rot, cs, 0.0))
        sa = jnp.where((lane >= 64) & (lane < 80), -sn, 0.0)
        sb = jnp.where((lane >= 80) & (lane < 96), sn, 0.0)
        return (ct, sa, sb), ()
    return _rowwise(fn, [(pos, 1, 0)], [invf], [(LANES, F32)] * 3, [], "rope_tables")


def _rope(x, ct, sa, sb):
    return x * ct + pltpu.roll(x, LANES - 16, 1) * sa + pltpu.roll(x, 16, 1) * sb


def _rope_t(x, ct, sa, sb):
    return x * ct - pltpu.roll(x, LANES - 16, 1) * sa - pltpu.roll(x, 16, 1) * sb


def _head_mask(width, hh):
    lane = lax.broadcasted_iota(jnp.int32, (1, width), 1)
    half = width // 2
    return (lane >= hh * half) & (lane < (hh + 1) * half)


ATT_PP = 2
ATT_CHAINS = [(a, hh) for a in range(ATT_PP) for hh in range(2)]
ATT_G = HEADS // (2 * ATT_PP)


def _pair(ref_or_val, a, width, rows=slice(None)):
    return ref_or_val[rows, a * width:(a + 1) * width]


def _head_cols(ref, a, hh, dkp, rows=slice(None)):
    if dkp == 2 * LANES:
        return ref[rows, a * dkp + hh * LANES:a * dkp + (hh + 1) * LANES]
    blk = _pair(ref, a, dkp, rows)
    return jnp.where(_head_mask(dkp, hh), blk, jnp.zeros_like(blk))


def _attn_fwd(q, qo, k, ko, v, vo, dkp, scale, bias, name, dep=None):
    T = ATT_T
    assert qo % ATT_PP == 0 and ko % ATT_PP == 0 and vo % ATT_PP == 0
    qo, ko, vo = qo // ATT_PP, ko // ATT_PP, vo // ATT_PP
    split = dkp == 2 * LANES

    def body(*refs):
        refs = list(refs)
        if dep is not None:
            del refs[3 + (bias is not None)]
        if bias is not None:
            q_ref, k_ref, v_ref, b_ref, o_ref, lse_ref, s_scr = refs
        else:
            q_ref, k_ref, v_ref, o_ref, lse_ref, s_scr = refs
        i = pl.program_id(1)
        row = lax.broadcasted_iota(jnp.int32, (T, T), 0)
        col = lax.broadcasted_iota(jnp.int32, (T, T), 1)
        qms = [_head_cols(q_ref, a, hh, dkp) for a, hh in ATT_CHAINS]

        def k_of(a, hh, ks):
            return _head_cols(k_ref, a, hh, dkp, ks) if split else _pair(k_ref, a, dkp, ks)

        def fold(t):
            return [t[:, c * LANES:(c + 1) * LANES] for c in range(T // LANES)]

        def run(nt):
            mls = [jnp.full((T, LANES), -jnp.inf, F32) for _ in ATT_CHAINS]
            for j in range(nt):
                ks = slice(j * T, (j + 1) * T)
                for ci, (a, hh) in enumerate(ATT_CHAINS):
                    s = lax.dot_general(qms[ci], k_of(a, hh, ks), NT, preferred_element_type=F32) * (scale * LOG2E)
                    if bias is not None:
                        s = s + b_ref[2 * a + hh, j] * LOG2E
                    if j == nt - 1:
                        s = jnp.where(row >= col, s, -jnp.inf)
                    s_scr[ci, j] = s
                    for part in fold(s):
                        mls[ci] = jnp.maximum(mls[ci], part)
            ms = [jnp.max(ml, axis=1, keepdims=True) for ml in mls]
            mbs = [jnp.broadcast_to(m, (T, LANES)) for m in ms]
            for a in range(ATT_PP):
                ls = [jnp.zeros((T, LANES), F32) for _ in range(2)]
                ps, vms = [], []
                for j in range(nt):
                    vb = _pair(v_ref, a, LANES, slice(j * T, (j + 1) * T))
                    for hh in range(2):
                        parts = [jnp.exp2(part - mbs[2 * a + hh]) for part in fold(s_scr[2 * a + hh, j])]
                        for part in parts:
                            ls[hh] = ls[hh] + part
                        ps.append(jnp.concatenate(parts, axis=1).astype(BF16))
                        vms.append(jnp.where(_head_mask(LANES, hh), vb, jnp.zeros_like(vb)))
                acc = lax.dot_general(jnp.concatenate(ps, axis=1), jnp.concatenate(vms, axis=0), NN,
                                      preferred_element_type=F32)
                l0, l1 = [jnp.sum(l, axis=1, keepdims=True) for l in ls]
                lse_ref[2 * a] = ms[2 * a] + jnp.log2(l0)
                lse_ref[2 * a + 1] = ms[2 * a + 1] + jnp.log2(l1)
                inv = jnp.where(_head_mask(LANES, 0), 1.0 / l0, 1.0 / l1)
                o_ref[:, a * LANES:(a + 1) * LANES] = (acc * inv).astype(o_ref.dtype)

        for nt in range(1, N_ATT + 1):
            pl.when(i == nt - 1)(functools.partial(run, nt))

    in_specs = [
        pl.BlockSpec((T, ATT_PP * dkp), lambda g, i: (i, qo + g)),
        pl.BlockSpec((S, ATT_PP * dkp), lambda g, i: (0, ko + g)),
        pl.BlockSpec((S, ATT_PP * LANES), lambda g, i: (0, vo + g)),
    ]
    args = [q, k, v]
    if bias is not None:
        in_specs.append(pl.BlockSpec((2 * ATT_PP, N_ATT, 1, T), lambda g, i: (g, 0, 0, 0)))
        args.append(bias)
    if dep is not None:
        in_specs.append(pl.BlockSpec(memory_space=pl.ANY))
        args.append(dep)
    return pl.pallas_call(
        body, name=name, grid=(ATT_G, N_ATT),
        in_specs=in_specs,
        out_specs=[pl.BlockSpec((T, ATT_PP * LANES), lambda g, i: (i, g)),
                   pl.BlockSpec((2 * ATT_PP, T, 1), lambda g, i: (g, i, 0))],
        out_shape=[jax.ShapeDtypeStruct((S, HEADS * HEAD_DIM), BF16),
                   jax.ShapeDtypeStruct((HEADS, S, 1), F32)],
        scratch_shapes=[pltpu.VMEM((len(ATT_CHAINS), N_ATT, T, T), F32)],
        compiler_params=_params(("parallel", "arbitrary")),
    )(*args)


def _attn_grad(q, qo, k, ko, v, vo, do, lse, dkp, scale, bias, qk_dtype, name, dep=None):
    T = ATT_T
    has_b = bias is not None
    qo, ko, vo = qo // ATT_PP, ko // ATT_PP, vo // ATT_PP
    n_ch = len(ATT_CHAINS)
    split = dkp == 2 * LANES

    def body(*refs):
        refs = list(refs)
        if dep is not None:
            del refs[5 + has_b]
        q_ref, k_ref, v_ref, do_ref, lse_ref = refs[:5]
        refs = refs[5:]
        if has_b:
            b_ref, refs = refs[0], refs[1:]
        dq_ref, dk_ref, dv_ref = refs[:3]
        refs = refs[3:]
        if has_b:
            db_ref, refs = refs[0], refs[1:]
        p_scr, dp_scr, dk_acc, dv_acc = refs[:4]
        db_acc = refs[4] if has_b else None
        i = pl.program_id(1)

        @pl.when(i == 0)
        def _():
            dk_acc[...] = jnp.zeros(dk_acc.shape, F32)
            dv_acc[...] = jnp.zeros(dv_acc.shape, F32)
            if has_b:
                db_acc[...] = jnp.zeros(db_acc.shape, F32)

        row = lax.broadcasted_iota(jnp.int32, (T, T), 0)
        col = lax.broadcasted_iota(jnp.int32, (T, T), 1)

        def fold(t):
            return [t[:, c * LANES:(c + 1) * LANES] for c in range(T // LANES)]

        qms, doms, lses = [], [], []
        for a, hh in ATT_CHAINS:
            dob = _pair(do_ref, a, LANES)
            qms.append(_head_cols(q_ref, a, hh, dkp))
            doms.append(jnp.where(_head_mask(LANES, hh), dob, jnp.zeros_like(dob)))
            lses.append(lse_ref[2 * a + hh])

        def k_of(a, hh, ks):
            return _head_cols(k_ref, a, hh, dkp, ks) if split else _pair(k_ref, a, dkp, ks)

        def run(nt):
            dls = [jnp.zeros((T, LANES), F32) for _ in ATT_CHAINS]
            for j in range(nt):
                ks = slice(j * T, (j + 1) * T)
                for ci, (a, hh) in enumerate(ATT_CHAINS):
                    s = lax.dot_general(qms[ci], k_of(a, hh, ks), NT, preferred_element_type=F32) * (scale * LOG2E)
                    if has_b:
                        s = s + b_ref[ci, j] * LOG2E
                    s = s - lses[ci]
                    if j == nt - 1:
                        s = jnp.where(row >= col, s, -jnp.inf)
                    p = jnp.exp2(s)
                    dp = lax.dot_general(doms[ci], _pair(v_ref, a, LANES, ks), NT, preferred_element_type=F32)
                    p_scr[ci, j] = p
                    dp_scr[ci, j] = dp
                    for part in fold(p * dp):
                        dls[ci] = dls[ci] + part
            deltas = [jnp.broadcast_to(jnp.sum(dl, axis=1, keepdims=True), (T, LANES)) for dl in dls]
            for a in range(ATT_PP):
                ds_all, km_all = [], []
                if split:
                    qts = [jnp.transpose(qms[2 * a + hh]) for hh in range(2)]
                else:
                    qm2t = jnp.transpose(jnp.concatenate([qms[2 * a], qms[2 * a + 1]], axis=0))
                dom2t = jnp.transpose(jnp.concatenate([doms[2 * a], doms[2 * a + 1]], axis=0))
                for j in range(nt):
                    ks = slice(j * T, (j + 1) * T)
                    p2, ds2 = [], []
                    for hh in range(2):
                        ci = 2 * a + hh
                        p = p_scr[ci, j]
                        ds = jnp.concatenate([pp * (dd - deltas[ci]) for pp, dd in zip(fold(p), fold(dp_scr[ci, j]))], axis=1)
                        if has_b:
                            db_acc[ci, j] += jnp.sum(ds, axis=0, keepdims=True)
                        p2.append(p.astype(BF16))
                        ds2.append((ds * scale).astype(BF16))
                        if not split:
                            km_all.append(_head_cols(k_ref, a, hh, dkp, ks))
                    dv_acc[a * LANES:(a + 1) * LANES, ks] += lax.dot_general(
                        dom2t, jnp.concatenate(p2, axis=0), NN, preferred_element_type=F32)
                    if split:
                        for hh in range(2):
                            dk_acc[a * dkp + hh * LANES:a * dkp + (hh + 1) * LANES, ks] += lax.dot_general(
                                qts[hh], ds2[hh], NN, preferred_element_type=F32)
                    else:
                        dk_acc[a * dkp:(a + 1) * dkp, ks] += lax.dot_general(
                            qm2t, jnp.concatenate(ds2, axis=0), NN, preferred_element_type=F32)
                    ds_all += ds2
                if split:
                    for hh in range(2):
                        dq = lax.dot_general(jnp.concatenate(ds_all[hh::2], axis=1),
                                             _head_cols(k_ref, a, hh, dkp, slice(0, nt * T)), NN,
                                             preferred_element_type=F32)
                        dq_ref[:, a * dkp + hh * LANES:a * dkp + (hh + 1) * LANES] = dq.astype(dq_ref.dtype)
                else:
                    dq = lax.dot_general(jnp.concatenate(ds_all, axis=1), jnp.concatenate(km_all, axis=0), NN,
                                         preferred_element_type=F32)
                    dq_ref[:, a * dkp:(a + 1) * dkp] = dq.astype(dq_ref.dtype)

        for nt in range(1, N_ATT + 1):
            pl.when(i == nt - 1)(functools.partial(run, nt))

        @pl.when(i == N_ATT - 1)
        def _():
            dk_ref[...] = jnp.transpose(dk_acc[...]).astype(dk_ref.dtype)
            dv_ref[...] = jnp.transpose(dv_acc[...]).astype(dv_ref.dtype)
            if has_b:
                db_ref[...] = db_acc[...]

    in_specs = [
        pl.BlockSpec((T, ATT_PP * dkp), lambda g, i: (i, qo + g)),
        pl.BlockSpec((S, ATT_PP * dkp), lambda g, i: (0, ko + g)),
        pl.BlockSpec((S, ATT_PP * LANES), lambda g, i: (0, vo + g)),
        pl.BlockSpec((T, ATT_PP * LANES), lambda g, i: (i, g)),
        pl.BlockSpec((2 * ATT_PP, T, 1), lambda g, i: (g, i, 0)),
    ]
    args = [q, k, v, do, lse]
    out_specs = [
        pl.BlockSpec((T, ATT_PP * dkp), lambda g, i: (i, g)),
        pl.BlockSpec((S, ATT_PP * dkp), lambda g, i: (0, g)),
        pl.BlockSpec((S, ATT_PP * LANES), lambda g, i: (0, g)),
    ]
    width = (HEADS // 2) * dkp
    out_shape = [
        jax.ShapeDtypeStruct((S, width), qk_dtype),
        jax.ShapeDtypeStruct((S, width), qk_dtype),
        jax.ShapeDtypeStruct((S, HEADS * HEAD_DIM), BF16),
    ]
    scratch = [pltpu.VMEM((n_ch, N_ATT, T, T), F32), pltpu.VMEM((n_ch, N_ATT, T, T), F32),
               pltpu.VMEM((ATT_PP * dkp, S), F32), pltpu.VMEM((ATT_PP * LANES, S), F32)]
    if has_b:
        bspec = pl.BlockSpec((2 * ATT_PP, N_ATT, 1, T), lambda g, i: (g, 0, 0, 0))
        in_specs.append(bspec)
        args.append(bias)
        out_specs.append(bspec)
        out_shape.append(jax.ShapeDtypeStruct((HEADS, N_ATT, 1, T), F32))
        scratch.append(pltpu.VMEM((2 * ATT_PP, N_ATT, 1, T), F32))
    if dep is not None:
        in_specs.append(pl.BlockSpec(memory_space=pl.ANY))
        args.append(dep)
    return pl.pallas_call(
        body, name=name, grid=(ATT_G, N_ATT),
        in_specs=in_specs, out_specs=out_specs, out_shape=out_shape, scratch_shapes=scratch,
        compiler_params=_params(("parallel", "arbitrary")),
    )(*args)


def _tri(upper):
    a = lax.broadcasted_iota(jnp.int32, (LANES, LANES), 0)
    b = lax.broadcasted_iota(jnp.int32, (LANES, LANES), 1)
    return jnp.where(a <= b if upper else a >= b, 1.0, 0.0).astype(F32)


def _fox_gates(proj, blk, bf):
    def body(m_ref, b_ref, z_out, o_ref):
        tri = _tri(True)
        carry = jnp.zeros((HEADS, 1), F32)
        for t in range(S // LANES):
            sl = slice(t * LANES, (t + 1) * LANES)
            zt = jnp.transpose(m_ref[sl, :])[:HEADS]
            z_out[:, sl] = zt
            z = zt + b_ref[...]
            logf = jnp.minimum(z, 0.0) - jnp.log(1.0 + jnp.exp(-jnp.abs(z)))
            c = lax.dot_general(logf, tri, NN, preferred_element_type=F32,
                                precision=lax.Precision.HIGHEST) + carry
            o_ref[:, sl] = -c
            carry = c[:, LANES - 1:LANES]

    return pl.pallas_call(
        body, name="fox_gates", grid=(1,),
        in_specs=[pl.BlockSpec((S, LANES), lambda i: (0, blk)), pl.BlockSpec(bf.shape, lambda i: (0, 0))],
        out_specs=[pl.BlockSpec((HEADS, S), lambda i: (0, 0))] * 2,
        out_shape=[jax.ShapeDtypeStruct((HEADS, S), F32)] * 2,
        compiler_params=_params(("arbitrary",)),
    )(proj, bf)


def _fox_gates_bwd(dbias, zt, bf):
    def body(d_ref, z_ref, b_ref, dz_ref, dbf_ref):
        tri = _tri(False)
        carry = jnp.zeros((HEADS, 1), F32)
        tot = jnp.zeros((HEADS, 1), F32)
        for t in reversed(range(S // LANES)):
            sl = slice(t * LANES, (t + 1) * LANES)
            df = -d_ref[:, sl]
            c = lax.dot_general(df, tri, NN, preferred_element_type=F32,
                                precision=lax.Precision.HIGHEST) + carry
            carry = c[:, 0:1]
            z = z_ref[:, sl] + b_ref[...]
            dz = c * _sigmoid(-z)
            dz_ref[:, sl] = dz
            tot = tot + jnp.sum(dz, axis=1, keepdims=True)
        dbf_ref[...] = tot

    return pl.pallas_call(
        body, name="fox_gates_bwd",
        out_shape=[jax.ShapeDtypeStruct((HEADS, S), F32), jax.ShapeDtypeStruct((HEADS, 1), F32)],
        compiler_params=_params(),
    )(dbias, zt, bf)


def _mod_part(c_all, w_ada, b_cols, before=()):
    def body(c_ref, w_ref, b_ref, *rest):
        o_ref, s_ref = rest[-2:]
        c = c_ref[...]
        sc = c * _sigmoid(c)
        s_ref[...] = sc
        o_ref[...] = lax.dot_general(sc, w_ref[...], NN, preferred_element_type=F32,
                                     precision=lax.Precision.HIGHEST) + b_ref[...]

    return pl.pallas_call(
        body, name="mod_part",
        in_specs=[pl.BlockSpec(memory_space=pltpu.VMEM)] * 3 + [pl.BlockSpec(memory_space=pl.ANY)] * len(before),
        out_shape=[jax.ShapeDtypeStruct((N_DEV, w_ada.shape[1]), F32), jax.ShapeDtypeStruct(c_all.shape, F32)],
        compiler_params=_params(),
    )(c_all, w_ada, b_cols, *before)


def _adamw_w_ada(w, m, v, sc_t, dm):
    rows, cols = w.shape
    tr = 256

    def body(w_ref, m_ref, v_ref, s_ref, d_ref, g_out, d_out, m_out, v_out):
        g = s_ref[:, 0:1] * d_ref[0:1, :]
        for b in range(1, N_DEV):
            g = g + s_ref[:, b:b + 1] * d_ref[b:b + 1, :]
        g_out[...] = g
        d_out[...], m_out[...], v_out[...] = _adamw_math(w_ref[...], g, m_ref[...], v_ref[...])

    spec = pl.BlockSpec((tr, cols), lambda i: (i, 0))
    return pl.pallas_call(
        body, name="adamw_w_ada", grid=(rows // tr,),
        in_specs=[spec, spec, spec, pl.BlockSpec((tr, N_DEV), lambda i: (i, 0)), pl.BlockSpec(dm.shape, lambda i: (0, 0))],
        out_specs=[spec] * 4, out_shape=[jax.ShapeDtypeStruct((rows, cols), F32)] * 4,
        compiler_params=_params(("parallel",)),
    )(w, m, v, sc_t, dm)


def _adamw(w, m, v, parts, name, own=None, slot=None):
    rows, cols = w.shape
    n = parts.shape[0]
    by_cols = rows % 256 != 0 and cols % 256 == 0
    tr, tc = (rows, 256) if by_cols else ((rows if rows <= 512 else 256), cols)
    tile = (lambda i: (0, i)) if by_cols else (lambda i: (i, 0))

    def body(*refs):
        if own is not None:
            s_ref, refs = refs[0], refs[1:]
            w_ref, m_ref, v_ref, p_ref, o_ref, g_out, d_out, m_out, v_out = refs
            terms = [jnp.where(s_ref[0] == kk, o_ref[0], p_ref[kk]) for kk in range(n)]
        else:
            w_ref, m_ref, v_ref, p_ref, g_out, d_out, m_out, v_out = refs
            terms = [p_ref[kk] for kk in range(n)]
        g = terms[0].astype(F32)
        for term in terms[1:]:
            g = g + term.astype(F32)
        g_out[...] = g
        d_out[...], m_out[...], v_out[...] = _adamw_math(w_ref[...], g, m_ref[...], v_ref[...])

    spec = pl.BlockSpec((tr, tc), lambda i, *_: tile(i))
    in_specs = [spec, spec, spec, pl.BlockSpec((n, tr, tc), lambda i, *_: (0,) + tile(i))]
    out_shape = [jax.ShapeDtypeStruct((rows, cols), F32)] * 4
    grid = (rows // tr if not by_cols else cols // tc,)
    if own is None:
        return pl.pallas_call(
            body, name=name, grid=grid, in_specs=in_specs, out_specs=[spec] * 4, out_shape=out_shape,
            compiler_params=_params(("parallel",)),
        )(w, m, v, parts)
    in_specs.append(pl.BlockSpec((1, tr, tc), lambda i, s: (s[0],) + tile(i)))
    return pl.pallas_call(
        body, name=name, out_shape=out_shape, compiler_params=_params(("parallel",)),
        grid_spec=pltpu.PrefetchScalarGridSpec(num_scalar_prefetch=1, grid=grid, in_specs=in_specs,
                                               out_specs=[spec] * 4),
    )(slot, w, m, v, parts, own)


def _adamw_math(w, g, m, v):
    mm = ADAM_B1 * m + (1.0 - ADAM_B1) * g
    vv = ADAM_B2 * v + (1.0 - ADAM_B2) * (g * g)
    m_hat = mm / (1.0 - ADAM_B1 ** ADAM_STEP)
    v_hat = vv / (1.0 - ADAM_B2 ** ADAM_STEP)
    return -ADAM_LR * (m_hat / (jnp.sqrt(v_hat) + ADAM_EPS) + ADAM_WD * w), mm, vv


def _adamw_rows(bundles, offsets, ws, ms, vs, err_off, err_width):
    k = len(ws)

    def body(*refs):
        b_ref = refs[0]
        w_refs, m_refs, v_refs = refs[1:1 + k], refs[1 + k:1 + 2 * k], refs[1 + 2 * k:1 + 3 * k]
        outs = refs[1 + 3 * k:]
        g_all = b_ref[0]
        for kk in range(1, N_DEV):
            g_all = g_all + b_ref[kk]
        for i in range(k):
            width = w_refs[i].shape[1]
            g = g_all[:, offsets[i]:offsets[i] + width]
            outs[4 * i][...] = g
            outs[4 * i + 1][...], outs[4 * i + 2][...], outs[4 * i + 3][...] = _adamw_math(
                w_refs[i][...], g, m_refs[i][...], v_refs[i][...])
        outs[4 * k][...] = g_all[:, err_off:err_off + err_width]

    out_shape = []
    for w_ in ws:
        out_shape += [jax.ShapeDtypeStruct(w_.shape, F32)] * 4
    out_shape.append(jax.ShapeDtypeStruct((1, err_width), F32))
    res = pl.pallas_call(body, name="adamw_rows", out_shape=out_shape, compiler_params=_params())(bundles, *ws, *ms, *vs)
    return [tuple(res[4 * i:4 * i + 4]) for i in range(k)], res[-1]


def _coords():
    return lax.axis_index("x"), lax.axis_index("y"), lax.axis_index("c")


def _flat(px, py, pc):
    return 4 * px + 2 * py + pc


def _all_gather(arrs, name):
    n = len(arrs)

    def body(*refs):
        ins, outs = refs[:n], refs[n:2 * n]
        send, recv, lsem = refs[2 * n:]
        x, y, c = _coords()
        me, sibling = (x, y, c), (x, y, 1 - c)
        chips = [(1 - x, y), (x, 1 - y), (1 - x, 1 - y)]

        def copy(a, kk, block, to, src=None):
            slot = outs[a].at[_flat(*block)]
            return pltpu.make_async_remote_copy(
                src_ref=slot if src is None else src, dst_ref=slot,
                send_sem=send.at[a, kk], recv_sem=recv.at[a, kk],
                device_id=to, device_id_type=MESH)

        mine = [pltpu.make_async_copy(ins[a], outs[a].at[_flat(*me)], lsem.at[a]) for a in range(n)]
        for cp in mine:
            cp.start()
        first = []
        for a in range(n):
            first.append(copy(a, 0, me, sibling, src=ins[a]))
            first += [copy(a, 1 + j, me, (*chip, c), src=ins[a]) for j, chip in enumerate(chips)]
        for cp in first:
            cp.start()
        passed = []
        for j, chip in enumerate(chips):
            for a in range(n):
                copy(a, 1 + j, (*chip, c), me).wait_recv()
                cp = copy(a, 4 + j, (*chip, c), sibling)
                cp.start()
                passed.append(cp)
        for a in range(n):
            copy(a, 0, sibling, me).wait_recv()
        for j, chip in enumerate(chips):
            for a in range(n):
                copy(a, 4 + j, (*chip, 1 - c), me).wait_recv()
        for cp in first + passed:
            cp.wait_send()
        for cp in mine:
            cp.wait()

    any_spec = pl.BlockSpec(memory_space=pl.ANY)
    return pl.pallas_call(
        body, name=name,
        in_specs=[any_spec] * n, out_specs=[any_spec] * n,
        out_shape=[jax.ShapeDtypeStruct((N_DEV,) + a.shape, a.dtype) for a in arrs],
        scratch_shapes=[pltpu.SemaphoreType.DMA((n, 7)), pltpu.SemaphoreType.DMA((n, 7)),
                        pltpu.SemaphoreType.DMA((n,))],
    )(*arrs)


def _peer_list():
    x, y, c = _coords()
    return [((1 - x if r & 4 else x), (1 - y if r & 2 else y), (1 - c if r & 1 else c)) for r in range(1, N_DEV)]


def _copy_plan(mode, src, land):
    x, y, c = _coords()
    me = _flat(x, y, c)
    if mode == "gather":
        return [(src, land.at[me], peer) for peer in _peer_list()]
    if mode == "exchange":
        return [(src.at[_flat(*peer)], land.at[me], peer) for peer in _peer_list()]
    if mode == "pair":
        return [(src.at[_flat(q // 2, q % 2, 1 - c)], land.at[q], (x, y, 1 - c)) for q in range(N_DEV // 2)]
    chips = [((1 - x if r & 2 else x), (1 - y if r & 1 else y)) for r in range(1, N_DEV // 2)]
    if mode == "chips":
        return [(src.at[2 * qx + qy], land.at[2 * x + y], (qx, qy, c)) for qx, qy in chips]
    if mode == "spread":
        return [(src, land.at[me], (x, y, 1 - c))] + [(src, land.at[me], (qx, qy, c)) for qx, qy in chips]
    assert mode == "forward"
    return [(land.at[_flat(qx, qy, c)], land.at[_flat(qx, qy, c)], (x, y, 1 - c)) for qx, qy in chips]


N_COPIES = dict(gather=N_DEV - 1, exchange=N_DEV - 1, pair=N_DEV // 2, chips=N_DEV // 2 - 1, spread=N_DEV // 2,
                forward=N_DEV // 2 - 1)


def _land_shape(mode, shape):
    return {"gather": (N_DEV,) + shape, "spread": (N_DEV,) + shape, "exchange": shape,
            "pair": (N_DEV // 2,) + shape[1:], "chips": shape}[mode]


HBM_SPEC = pl.BlockSpec(memory_space=pltpu.HBM)
SEM_SPEC = pl.BlockSpec(memory_space=pltpu.SEMAPHORE)
ANY_SPEC = pl.BlockSpec(memory_space=pl.ANY)
SIDE_EFFECT = pltpu.SideEffectType.DATAFLOW_SIDE_EFFECTING


def _async_start(groups, modes, after, name):
    modes = [modes] * len(groups) if isinstance(modes, str) else list(modes)
    arrs = [(a, m) for g, m in zip(groups, modes) for a in g]
    n = len(arrs)
    fresh = [i for i, (_, m) in enumerate(arrs) if m != "forward"]

    def body(*refs):
        srcs, new_lands = refs[:n], refs[n:n + len(fresh)]
        outs = refs[n + len(fresh) + 1:]
        lands = list(srcs)
        for k, i in enumerate(fresh):
            lands[i] = new_lands[k]
        for ai, (_, mode) in enumerate(arrs):
            for src_ref, dst_ref, peer in _copy_plan(mode, srcs[ai], lands[ai]):
                pltpu.make_async_remote_copy(src_ref=src_ref, dst_ref=dst_ref, send_sem=outs[2 * ai],
                                             recv_sem=outs[2 * ai + 1], device_id=peer, device_id_type=MESH).start()
        outs[-1][...] = jnp.zeros(outs[-1].shape, F32)

    land_shapes = [(_land_shape(arrs[i][1], arrs[i][0].shape), arrs[i][0].dtype) for i in fresh]
    n_buf = n + len(fresh)
    out_shape = [pltpu.SemaphoreType.DMA(())] * (2 * n)
    out_shape += [pltpu.HBM(a.shape, a.dtype) for a, _ in arrs]
    out_shape += [pltpu.HBM(shape, dt) for shape, dt in land_shapes]
    out_shape.append(jax.ShapeDtypeStruct((8, LANES), F32))
    res = pl.pallas_call(
        body, name=name, out_shape=tuple(out_shape),
        in_specs=[HBM_SPEC] * n_buf + [ANY_SPEC],
        out_specs=tuple([SEM_SPEC] * (2 * n) + [HBM_SPEC] * n_buf + [pl.BlockSpec(memory_space=pltpu.VMEM)]),
        input_output_aliases={i: 2 * n + i for i in range(n_buf)},
        compiler_params=pltpu.CompilerParams(has_side_effects=SIDE_EFFECT),
    )(*[pltpu.with_memory_space_constraint(a, pltpu.HBM) for a, _ in arrs],
      *[pltpu.with_memory_space_constraint(lax.empty(shape, dt), pltpu.HBM) for shape, dt in land_shapes],
      after)
    sems, thru = res[:2 * n], res[2 * n:-1]
    land_of = {i: thru[n + k] for k, i in enumerate(fresh)}
    states, idx = [], 0
    for g, mode in zip(groups, modes):
        ids = range(idx, idx + len(g))
        idx += len(g)
        states.append(([sems[2 * i] for i in ids], [sems[2 * i + 1] for i in ids],
                       None if mode == "forward" else [thru[i] for i in ids],
                       [land_of.get(i, thru[i]) for i in ids], mode))
    return states, res[-1]


def _async_wait(state, after, name):
    sends, recvs, srcs, lands, mode = state
    g = len(lands)
    bufs = (list(srcs) if srcs is not None else []) + list(lands)
    nb = len(bufs)

    def body(*refs):
        l_refs, sems = refs[nb - g:nb], refs[nb:nb + 2 * g]
        for ai in range(g):
            moved = l_refs[ai].at[pl.ds(0, N_COPIES[mode])]
            cp = pltpu.make_async_remote_copy(src_ref=moved, dst_ref=moved, send_sem=sems[ai], recv_sem=sems[g + ai],
                                              device_id=_coords(), device_id_type=MESH)
            cp.wait_send()
            cp.wait_recv()

    res = pl.pallas_call(
        body, name=name,
        out_shape=tuple(pltpu.HBM(a.shape, a.dtype) for a in bufs),
        in_specs=[HBM_SPEC] * nb + [SEM_SPEC] * (2 * g) + [ANY_SPEC],
        out_specs=tuple([HBM_SPEC] * nb),
        input_output_aliases={i: i for i in range(nb)},
        compiler_params=pltpu.CompilerParams(has_side_effects=SIDE_EFFECT),
    )(*bufs, *sends, *recvs, after)
    return (list(res[:nb - g]) if srcs is not None else None), list(res[nb - g:])


def _add_sibling(mine, theirs, core):
    def body(c_ref, a_ref, b_ref, o_ref):
        o_ref[...] = (a_ref[...].astype(F32) + b_ref[...].astype(F32)).astype(o_ref.dtype)

    blk = (1,) + mine.shape[1:]
    return pl.pallas_call(
        body, name="add_sibling", out_shape=jax.ShapeDtypeStruct(theirs.shape, mine.dtype),
        grid_spec=pltpu.PrefetchScalarGridSpec(
            num_scalar_prefetch=1, grid=(theirs.shape[0],),
            in_specs=[pl.BlockSpec(blk, lambda q, c: (2 * q + c[0], 0, 0)), pl.BlockSpec(blk, lambda q, c: (q, 0, 0))],
            out_specs=pl.BlockSpec(blk, lambda q, c: (q, 0, 0))),
        compiler_params=_params(("parallel",)),
    )(core, mine, theirs)


def _with_own(land, own, me):
    return lax.dynamic_update_index_in_dim(land, own, me, 0)


IN_SPLITS = (512, 512, 512, 8, 768, 256, 32, 1024, 1024)


def _from_shards(g, fn, out_widths, name, own=None, slot=None):
    _, k, n = g.shape
    tr = min(k, 256)

    def body(*refs):
        if own is not None:
            s_ref, g_ref, own_ref = refs[:3]
            cols = [jnp.where(s_ref[0] == j, own_ref[...], g_ref[j]) for j in range(N_DEV)]
        else:
            g_ref = refs[0]
            cols = [g_ref[j] for j in range(N_DEV)]
        for o_ref, val in zip(refs[-len(out_widths):], fn(jnp.concatenate(cols, axis=1))):
            o_ref[...] = val

    in_specs = [pl.BlockSpec((N_DEV, tr, n), lambda i, *_: (0, i, 0))]
    out_spec = [pl.BlockSpec((tr, wd), lambda i, *_: (i, 0)) for wd in out_widths]
    out_shape = [jax.ShapeDtypeStruct((k, wd), g.dtype) for wd in out_widths]
    if own is None:
        return pl.pallas_call(body, name=name, grid=(k // tr,), in_specs=in_specs, out_specs=out_spec,
                              out_shape=out_shape, compiler_params=_params(("parallel",)))(g)
    in_specs.append(pl.BlockSpec((tr, n), lambda i, *_: (i, 0)))
    return pl.pallas_call(
        body, name=name, out_shape=out_shape, compiler_params=_params(("parallel",)),
        grid_spec=pltpu.PrefetchScalarGridSpec(num_scalar_prefetch=1, grid=(k // tr,), in_specs=in_specs, out_specs=out_spec),
    )(slot, g, own)


def _unshard_cols(g, own=None, slot=None):
    return _from_shards(g, lambda full: (full,), [N_DEV * g.shape[2]], "unshard_cols_%d" % g.shape[2], own, slot)[0]


FFN_T = D_FF // 2
FFN_SHARD = 2 * D_FF // N_DEV


def _unshard_ffn_in(g, own=None, slot=None):
    def pairs(full):
        parts = []
        for j in range(D_FF // FFN_T):
            parts += [full[:, j * FFN_T:(j + 1) * FFN_T], full[:, D_FF + j * FFN_T:D_FF + (j + 1) * FFN_T]]
        return (jnp.concatenate(parts, axis=1),)

    return _from_shards(g, pairs, [2 * D_FF], "unshard_ffn_in", own, slot)[0]


def _shard_ffn_in_t(wt):
    tc = 256

    def body(w_ref, o_ref):
        x = w_ref[...]
        nb = D_FF // FFN_T
        full = jnp.concatenate([x[(2 * j + half) * FFN_T:(2 * j + half + 1) * FFN_T]
                                for half in range(2) for j in range(nb)], axis=0)
        for j in range(N_DEV):
            o_ref[j] = full[j * FFN_SHARD:(j + 1) * FFN_SHARD]

    return pl.pallas_call(
        body, name="shard_ffn_in_t", grid=(D // tc,),
        in_specs=[pl.BlockSpec((2 * D_FF, tc), lambda i: (0, i))],
        out_specs=pl.BlockSpec((N_DEV, FFN_SHARD, tc), lambda i: (0, 0, i)),
        out_shape=jax.ShapeDtypeStruct((N_DEV, FFN_SHARD, D), wt.dtype),
        compiler_params=_params(("parallel",)),
    )(wt)


def _shard_cols(w):
    k, n = w.shape[0], w.shape[1] // N_DEV
    tr = min(k, 256)

    def body(w_ref, o_ref):
        full = w_ref[...]
        for j in range(N_DEV):
            o_ref[j] = full[:, j * n:(j + 1) * n]

    return pl.pallas_call(
        body, name="shard_cols_%d" % n, grid=(k // tr,),
        in_specs=[pl.BlockSpec((tr, N_DEV * n), lambda i: (i, 0))],
        out_specs=pl.BlockSpec((N_DEV, tr, n), lambda i: (0, i, 0)),
        out_shape=jax.ShapeDtypeStruct((N_DEV, k, n), w.dtype),
        compiler_params=_params(("parallel",)),
    )(w)


IN_OFFS = tuple(sum(IN_SPLITS[:i]) for i in range(len(IN_SPLITS) + 1))
IN_SHARD = IN_OFFS[-1] // N_DEV
REGROUP_ROWS = 256
MISC_AT = Q_LORA + KV_LORA + 2 * D
A_COLS = MISC_AT + LANES
A_TILE = A_COLS
B_COLS = 3 * HEADS * HEAD_DIM
MISC_BLOCK = MISC_AT // LANES
KR_AT = 64


def _w_in_regroup(g, own=None, slot=None):
    def groups(full):
        fq, fk, fv, wf, cq, ckv, kr, gf, gm = [full[:, IN_OFFS[i]:IN_OFFS[i + 1]] for i in range(9)]
        rows = full.shape[0]
        gap = jnp.zeros((rows, KR_AT - HEADS), BF16)
        tail = jnp.zeros((rows, LANES - KR_AT - ROPE_DIM), BF16)
        return jnp.concatenate([cq, ckv, gf, gm, wf, gap, kr, tail], axis=1), jnp.concatenate([fq, fk, fv], axis=1)

    return _from_shards(g, groups, [A_COLS, B_COLS], "w_in_regroup", own, slot)


def _w_in_ungroup(da, db_):
    def body(a_ref, b_ref, o_ref):
        a = a_ref[...]
        lora = Q_LORA + KV_LORA
        full = jnp.concatenate([b_ref[...], a[:, MISC_AT:MISC_AT + HEADS], a[:, :lora],
                                a[:, MISC_AT + KR_AT:MISC_AT + KR_AT + ROPE_DIM], a[:, lora:MISC_AT]], axis=1)
        for j in range(N_DEV):
            o_ref[j] = full[:, j * IN_SHARD:(j + 1) * IN_SHARD]

    tr = REGROUP_ROWS
    return pl.pallas_call(
        body, name="w_in_ungroup", grid=(D // tr,),
        in_specs=[pl.BlockSpec((tr, A_COLS), lambda i: (i, 0)), pl.BlockSpec((tr, B_COLS), lambda i: (i, 0))],
        out_specs=pl.BlockSpec((N_DEV, tr, IN_SHARD), lambda i: (0, i, 0)),
        out_shape=jax.ShapeDtypeStruct((N_DEV, D, IN_SHARD), BF16),
        compiler_params=_params(("parallel",)),
    )(da, db_)


def _prepare_weights(g, own=None, slot=None):
    w = {}
    if own is not None:
        small = ("w_uq", "w_ukv", "w_out", "w_ffn_out")
        g = {n: (_with_own(a, own[n], slot[0]) if n in small else a) for n, a in g.items()}
    pick = (lambda n: (own[n], slot)) if own is not None else (lambda n: (None, None))
    if "w_in" in g:
        w["w_a"], w["w_b"] = _w_in_regroup(g["w_in"], *pick("w_in"))
    if "w_uq" in g:
        w_uq = g["w_uq"].reshape(Q_LORA, HEADS, 96)
        w["w_uq"] = jnp.pad(w_uq, ((0, 0), (0, 0), (0, 32))).reshape(Q_LORA, HEADS * LANES)
        ukv = g["w_ukv"]
        w["w_k"] = jnp.transpose(jnp.pad(ukv[:, :, :64], ((0, 0), (0, 0), (0, 64))), (1, 0, 2)).reshape(KV_LORA, HEADS * LANES)
        w["w_v"] = jnp.transpose(ukv[:, :, 64:], (1, 0, 2)).reshape(KV_LORA, HEADS * HEAD_DIM)
    if "w_out" in g:
        w["w_pf"] = _unshard_cols(g["w_proj_fox"], *pick("w_proj_fox"))
        w["w_pm"] = _unshard_cols(g["w_proj_mla"], *pick("w_proj_mla"))
        w["w_out"] = g["w_out"].reshape(D, D)
    if "w_ffn_in" in g:
        w["w_ffn_in"] = _unshard_ffn_in(g["w_ffn_in"], *pick("w_ffn_in"))
        w["w_ffn_out"] = g["w_ffn_out"].reshape(D_FF, D)
    return w


def _shard_grads(dw):
    out = {}
    if "w_a" in dw:
        out["w_in"] = _w_in_ungroup(dw["w_a"], dw["w_b"])
    if "w_uq" in dw:
        w_uq = dw["w_uq"].reshape(Q_LORA, HEADS, LANES)[:, :, :96].reshape(Q_LORA, Q_LORA)
        out["w_uq"] = w_uq.reshape(N_DEV, Q_LORA // N_DEV, Q_LORA)
        k_part = dw["w_k"].reshape(KV_LORA, HEADS, LANES)[:, :, :64]
        v_part = dw["w_v"].reshape(KV_LORA, HEADS, HEAD_DIM)
        out["w_ukv"] = jnp.transpose(jnp.concatenate([k_part, v_part], axis=2), (1, 0, 2))
    if "w_out" in dw:
        out["w_proj_fox"] = _shard_cols(dw["w_pf"])
        out["w_proj_mla"] = _shard_cols(dw["w_pm"])
        out["w_out"] = dw["w_out"].reshape(N_DEV, D // N_DEV, D)
    if "w_ffn_in" in dw:
        out["w_ffn_in"] = _shard_ffn_in_t(dw["w_ffn_in"])
        out["w_ffn_out"] = dw["w_ffn_out"].reshape(N_DEV, D_FF // N_DEV, D)
    return out


def _fwd_bwd(x, pos, mod, target, w, vec, wts, send, relay):
    shift_mix, scale_mix, gate_mix, shift_ffn, scale_ffn, gate_ffn = [mod[:, i * D:(i + 1) * D] for i in range(6)]
    g_pre_mix, g_post_mix, g_pre_ffn, g_post_ffn = vec["g_pre_mix"], vec["g_post_mix"], vec["g_pre_ffn"], vec["g_post_ffn"]
    g_q, g_kv = vec["g_q_lora"], vec["g_kv_lora"]

    inv_freq = 1.0 / (ROPE_THETA ** (jnp.arange(0, ROPE_DIM, 2, dtype=F32) / ROPE_DIM))
    invf = jnp.concatenate([jnp.zeros((64,), F32), inv_freq, inv_freq, jnp.zeros((32,), F32)]).reshape(1, LANES)
    ct, sa, sb = _rope_tables(pos, invf)

    def pre1(xv, g, sc, sh):
        return (xv * _rstd(xv) * g) * (1.0 + sc) + sh
    proj_a, h = _mm_epi(x, w["w_a"], "nn", A_TILE, lambda r: ((r,), ()), "in_proj_a", 512, outs=[(A_COLS, A_TILE, F32)],
                        pro=(pre1, [g_pre_mix, scale_mix, shift_mix], 0))
    qkv = _mm(h, w["w_b"], "nn", BF16, "in_proj_b")

    def lora_norm(cv, g):
        return cv * _rstd(cv) * g
    w = {**w, **wts("lora", qkv)}
    tables = [(ct, LANES), (sa, LANES), (sb, LANES)]

    def rope_q(qv, c_, a_, b_):
        return (jnp.concatenate([_rope(qv[:, hd * LANES:(hd + 1) * LANES], c_, a_, b_) for hd in range(HEADS)], axis=1),), ()
    q_m, cqn = _mm_epi(proj_a, w["w_uq"], "nn", D, rope_q, "mla_uq", 512, rows=tables, outs=[(D, D, BF16)],
                       pro=(lora_norm, [g_q], 0))

    def rope_k(kv, misc, c_, a_, b_):
        lane = lax.broadcasted_iota(jnp.int32, (1, LANES), 1)
        kpe = jnp.where((lane >= 64) & (lane < 96), _rope(misc, c_, a_, b_), 0.0)
        return (jnp.concatenate([kv[:, hd * LANES:(hd + 1) * LANES] + kpe for hd in range(HEADS)], axis=1),), ()
    k_m, ckvn = _mm_epi(proj_a, w["w_k"], "nn", D, rope_k, "mla_uk", 512, rows=[(proj_a, LANES, MISC_BLOCK)] + tables,
                        outs=[(D, D, BF16)], pro=(lora_norm, [g_kv], Q_LORA // KV_LORA))
    v_m = _mm(ckvn, w["w_v"], "nn", BF16, "mla_uv")

    bf = jnp.transpose(vec["b_forget"])
    zt, neg_f = _fox_gates(proj_a, MISC_BLOCK, bf)
    bias = neg_f.reshape(HEADS, N_ATT, 1, ATT_T)
    o_b, lse_b = _attn_fwd(q_m, 0, k_m, 0, v_m, 0, 2 * LANES, 1.0 / math.sqrt(64 + ROPE_DIM), None, "mla_attn")
    o_a, lse_a = _attn_fwd(qkv, 0, qkv, 4, qkv, 8, LANES, 1.0 / math.sqrt(HEAD_DIM), bias, "fox_attn",
                           dep=wts("relay_proj", o_b)["tok"])

    w = {**w, **wts("proj", o_a)}
    pa = _mm(o_a, w["w_pf"], "nn", BF16, "proj_fox", dep=wts("relay_ffn", o_a)["tok"])

    def merge(pb_, gf, gm, pa_):
        return (_sigmoid(gf) * pa_ + _sigmoid(gm) * pb_, pb_), ()
    merged, pb = _mm_epi(o_b, w["w_pm"], "nn", 512, merge, "proj_mla", 1024,
                         rows=[(proj_a, 512, 2), (proj_a, 512, 4), (pa, 512)], outs=[(D, 512, BF16), (D, 512, BF16)])
    def post1(yv, xv, gate, gpost, gpre, sc, sh):
        x1 = xv + gate * (yv * _rstd(yv) * gpost)
        return (x1, (x1 * _rstd(x1) * gpre) * (1.0 + sc) + sh, yv), ()
    x1, h2, y = _mm_epi(merged, w["w_out"], "nn", D, post1, "out_proj", 512, rows=[(x, D)],
                        vecs=[gate_mix, g_post_mix, g_pre_ffn, scale_ffn, shift_ffn],
                        outs=[(D, D, F32), (D, D, BF16), (D, D, F32)])
    w = {**w, **wts("ffn", h2)}

    def swiglu(r):
        g, u = r[:, :FFN_T], r[:, FFN_T:]
        return (g * _sigmoid(g) * u, r), ()
    act, gu = _mm_epi(h2, w["w_ffn_in"], "nn", 2 * FFN_T, swiglu, "ffn_in", 512,
                      outs=[(D_FF, FFN_T, BF16), (2 * D_FF, 2 * FFN_T, BF16)])

    def head(y2v, x1v, tv, gate, gpost):
        r = _rstd(y2v)
        yn = y2v * r
        n2 = yn * gpost
        err = (x1v + gate * n2) - tv
        dx2 = err * (1.0 / D)
        dn2 = dx2 * gate
        dy2 = _norm_bwd(dn2 * gpost, yn, r)
        return (dx2, dy2), (_colsum(err * err), _colsum(dx2 * n2), _colsum(dn2 * yn))
    dx2, dy2, err_cols, d_gate_ffn, d_g_post_ffn = _mm_epi(
        act, w["w_ffn_out"], "nn", D, head, "ffn_out", 512, rows=[(x1, D), (target, D)], vecs=[gate_ffn, g_post_ffn],
        outs=[(D, D, F32), (D, D, BF16)], sums=[D, D, D])

    def swiglu_bwd(da, guv):
        g, u = guv[:, :FFN_T].astype(F32), guv[:, FFN_T:].astype(F32)
        sg = _sigmoid(g)
        return (jnp.concatenate([da * u * (sg * (1.0 + g * (1.0 - sg))), da * (g * sg)], axis=1),), ()
    (dgu,) = _mm_epi(dy2, w["w_ffn_out"], "nt", FFN_T, swiglu_bwd, "ffn_out_dx", 512, rows=[(gu, 2 * FFN_T)],
                     outs=[(2 * D_FF, 2 * FFN_T, BF16)])
    dw = {"w_ffn_out": _mm(act, dy2, "tn", BF16, "ffn_out_dw")}
    dw["w_ffn_in"] = _mm(dgu, h2, "tn", BF16, "ffn_in_dw")
    tok = send({n: dw.pop(n) for n in ("w_ffn_in", "w_ffn_out")})

    def mid(dh, x1v, dx2v, yv, gpre, sc, gate, gpost):
        r2 = _rstd(x1v)
        x1n = x1v * r2
        t = dh * x1n
        dx1 = dx2v + _norm_bwd(dh * (gpre * (1.0 + sc)), x1n, r2)
        ry = _rstd(yv)
        yn = yv * ry
        dn1 = dx1 * gate
        dy = _norm_bwd(dn1 * gpost, yn, ry)
        sums = (_colsum(dh), _colsum(t) * gpre, _colsum(t) * (1.0 + sc), _colsum(dx1 * (yn * gpost)), _colsum(dn1 * yn))
        return (dx1, dy), sums
    dx1, dy, d_shift_ffn, d_scale_ffn, d_g_pre_ffn, d_gate_mix, d_g_post_mix = _mm_epi(
        dgu, w["w_ffn_in"], "nt", D, mid, "ffn_in_dx", 512, rows=[(x1, D), (dx2, D), (y, D)],
        vecs=[g_pre_ffn, scale_ffn, gate_mix, g_post_mix], outs=[(D, D, F32), (D, D, BF16)], sums=[D] * 5, dep=tok)

    dw["w_out"] = _mm(merged, dy, "tn", BF16, "out_proj_dw")

    def merge_bwd(dm, gf, gm, pa_, pb_):
        sf, sm = _sigmoid(gf), _sigmoid(gm)
        return (dm * sf, dm * sm, dm * pa_ * (sf * (1.0 - sf)), dm * pb_ * (sm * (1.0 - sm))), ()
    dpa, dpb, dgf, dgm = _mm_epi(dy, w["w_out"], "nt", 512, merge_bwd, "out_proj_dx", 1024,
                                 rows=[(proj_a, 512, 2), (proj_a, 512, 4), (pa, 512), (pb, 512)],
                                 outs=[(D, 512, BF16)] * 4)
    do_a = _mm(dpa, w["w_pf"], "nt", BF16, "proj_fox_dx")
    do_b = _mm(dpb, w["w_pm"], "nt", BF16, "proj_mla_dx")
    dw["w_pf"] = _mm(o_a, dpa, "tn", BF16, "proj_fox_dw")
    dw["w_pm"] = _mm(o_b, dpb, "tn", BF16, "proj_mla_dw")
    tok = send({n: dw.pop(n) for n in ("w_out", "w_pf", "w_pm")})

    sc_a, sc_b = 1.0 / math.sqrt(HEAD_DIM), 1.0 / math.sqrt(64 + ROPE_DIM)
    dq_a, dk_a, dv_a, dbias = _attn_grad(qkv, 0, qkv, 4, qkv, 8, do_a, lse_a, LANES, sc_a, bias, BF16, "fox_attn_bwd",
                                         dep=tok)
    dq_m, dk_m, dv_m = _attn_grad(q_m, 0, k_m, 0, v_m, 0, do_b, lse_b, 2 * LANES, sc_b, None, F32, "mla_attn_bwd")

    def mla_rope_bwd(dq, dk, c_, a_, b_):
        lane = lax.broadcasted_iota(jnp.int32, (1, LANES), 1)
        dqs = [_rope_t(dq[:, hd * LANES:(hd + 1) * LANES], c_, a_, b_) for hd in range(HEADS)]
        dkpe = dk[:, 0:LANES]
        for hd in range(1, HEADS):
            dkpe = dkpe + dk[:, hd * LANES:(hd + 1) * LANES]
        dkpe = jnp.where((lane >= 64) & (lane < 96), dkpe, 0.0)
        dkr = jnp.where((lane >= 64) & (lane < 96), _rope_t(dkpe, c_, a_, b_), 0.0)
        return (jnp.concatenate(dqs, axis=1), dk, dkr), ()
    dqb, dkb, dkr = _rowwise(mla_rope_bwd, [(dq_m, D, 0), (dk_m, D, 0), (ct, LANES, 0), (sa, LANES, 0), (sb, LANES, 0)],
                             [], [(D, BF16), (D, BF16), (LANES, F32)], [], "mla_rope_bwd")
    def lora_q_bwd(dq, cq, gq):
        rq = _rstd(cq)
        cqh = cq * rq
        return (_norm_bwd(dq * gq, cqh, rq),), (_colsum(dq * cqh),)
    dcq, d_g_q = _mm_epi(dqb, w["w_uq"], "nt", Q_LORA, lora_q_bwd, "mla_uq_dx", 512, rows=[(proj_a, Q_LORA, 0)],
                         vecs=[g_q], outs=[(Q_LORA, Q_LORA, BF16)], sums=[Q_LORA])

    def lora_kv_bwd(dv_part, dk_part, ckv, gkv):
        dkv = dv_part + dk_part
        rk = _rstd(ckv)
        ckh = ckv * rk
        return (_norm_bwd(dkv * gkv, ckh, rk),), (_colsum(dkv * ckh),)
    dckv, d_g_kv = _mm_epi(dv_m, w["w_v"], "nt", KV_LORA, lora_kv_bwd, "mla_uv_dx", 1024,
                           rows=[(_mm(dkb, w["w_k"], "nt", F32, "mla_uk_dx"), KV_LORA), (proj_a, KV_LORA, 3)],
                           vecs=[g_kv], outs=[(KV_LORA, KV_LORA, BF16)], sums=[KV_LORA])

    dzt, d_bf = _fox_gates_bwd(dbias.reshape(HEADS, S), zt, bf)
    dmisc = (dkr + jnp.pad(jnp.transpose(dzt), ((0, 0), (0, LANES - HEADS)))).astype(BF16)
    dproj_a = [dcq, dckv, dgf, dgm, dmisc]
    dqkv = [dq_a, dk_a, dv_a]
    dw["w_a"] = _mm_cat(h, dproj_a, "tn", BF16, "in_proj_a_dw")
    dw["w_b"] = _mm_cat(h, dqkv, "tn", BF16, "in_proj_b_dw")
    tok = send(dw, True)
    dh_a = _mm_cat(dproj_a, w["w_a"], "nt", F32, "in_proj_a_dx", dep=tok)
    tok = relay(dh_a)
    tok = send({"w_uq": _mm(cqn, dqb, "tn", BF16, "mla_uq_dw", dep=tok),
                "w_k": _mm(ckvn, dkb, "tn", BF16, "mla_uk_dw", dep=tok),
                "w_v": _mm(ckvn, dv_m, "tn", BF16, "mla_uv_dw", dep=tok)}, late=True)

    def first(dh_b, dh_a, xv, dx1v, gpre, sc):
        dhv = dh_b + dh_a
        r = _rstd(xv)
        xn = xv * r
        t = dhv * xn
        dx = dx1v + _norm_bwd(dhv * (gpre * (1.0 + sc)), xn, r)
        return (dx,), (_colsum(dhv), _colsum(t) * gpre, _colsum(t) * (1.0 + sc))
    grad_x, d_shift_mix, d_scale_mix, d_g_pre_mix = _mm_epi(
        dqkv, w["w_b"], "nt", D, first, "in_proj_b_dx", 512,
        rows=[(dh_a, D), (x, D), (dx1, D)],
        vecs=[g_pre_mix, scale_mix], outs=[(D, D, F32)], sums=[D] * 3, dep=tok)

    dmod = jnp.concatenate([d_shift_mix, d_scale_mix, d_gate_mix, d_shift_ffn, d_scale_ffn, d_gate_ffn], axis=1)
    small = dict(dmod=dmod, g_pre_mix=d_g_pre_mix, g_post_mix=d_g_post_mix, g_pre_ffn=d_g_pre_ffn,
                 g_post_ffn=d_g_post_ffn, g_q_lora=d_g_q, g_kv_lora=d_g_kv,
                 b_forget=jnp.pad(jnp.transpose(d_bf), ((0, 0), (0, LANES - HEADS))), err=err_cols)
    return grad_x, small


SMALL_ORDER = ("dmod", "g_pre_mix", "g_post_mix", "g_pre_ffn", "g_post_ffn", "g_q_lora", "g_kv_lora", "b_forget", "err")
SMALL_PARAM = {"dmod": "b_ada"}
MATRICES = ("w_in", "w_uq", "w_ukv", "w_proj_fox", "w_proj_mla", "w_out", "w_ffn_in", "w_ffn_out")
WEIGHTS = ("w_ada", "b_ada", "g_pre_mix", "g_post_mix", "g_pre_ffn", "g_post_ffn", "w_in", "b_forget", "g_q_lora",
           "w_uq", "g_kv_lora", "w_ukv", "w_proj_fox", "w_proj_mla", "w_out", "w_ffn_in", "w_ffn_out")


def kernel(x, c, positions, w_ada, b_ada, g_pre_mix, g_post_mix, g_pre_ffn, g_post_ffn, w_in, b_forget, g_q_lora, w_uq, g_kv_lora, w_ukv, w_proj_fox, w_proj_mla, w_out, w_ffn_in, w_ffn_out, loss_target, m_w_ada, m_b_ada, m_g_pre_mix, m_g_post_mix, m_g_pre_ffn, m_g_post_ffn, m_w_in, m_b_forget, m_g_q_lora, m_w_uq, m_g_kv_lora, m_w_ukv, m_w_proj_fox, m_w_proj_mla, m_w_out, m_w_ffn_in, m_w_ffn_out, v_w_ada, v_b_ada, v_g_pre_mix, v_g_post_mix, v_g_pre_ffn, v_g_post_ffn, v_w_in, v_b_forget, v_g_q_lora, v_w_uq, v_g_kv_lora, v_w_ukv, v_w_proj_fox, v_w_proj_mla, v_w_out, v_w_ffn_in, v_w_ffn_out):
    prm = dict(w_ada=w_ada, b_ada=b_ada, g_pre_mix=g_pre_mix, g_post_mix=g_post_mix, g_pre_ffn=g_pre_ffn,
               g_post_ffn=g_post_ffn, w_in=w_in, b_forget=b_forget, g_q_lora=g_q_lora, w_uq=w_uq, g_kv_lora=g_kv_lora,
               w_ukv=w_ukv, w_proj_fox=w_proj_fox, w_proj_mla=w_proj_mla, w_out=w_out, w_ffn_in=w_ffn_in, w_ffn_out=w_ffn_out)
    mom = dict(w_ada=m_w_ada, b_ada=m_b_ada, g_pre_mix=m_g_pre_mix, g_post_mix=m_g_post_mix, g_pre_ffn=m_g_pre_ffn,
               g_post_ffn=m_g_post_ffn, w_in=m_w_in, b_forget=m_b_forget, g_q_lora=m_g_q_lora, w_uq=m_w_uq,
               g_kv_lora=m_g_kv_lora, w_ukv=m_w_ukv, w_proj_fox=m_w_proj_fox, w_proj_mla=m_w_proj_mla, w_out=m_w_out,
               w_ffn_in=m_w_ffn_in, w_ffn_out=m_w_ffn_out)
    var = dict(w_ada=v_w_ada, b_ada=v_b_ada, g_pre_mix=v_g_pre_mix, g_post_mix=v_g_post_mix, g_pre_ffn=v_g_pre_ffn,
               g_post_ffn=v_g_post_ffn, w_in=v_w_in, b_forget=v_b_forget, g_q_lora=v_g_q_lora, w_uq=v_w_uq,
               g_kv_lora=v_g_kv_lora, w_ukv=v_w_ukv, w_proj_fox=v_w_proj_fox, w_proj_mla=v_w_proj_mla, w_out=v_w_out,
               w_ffn_in=v_w_ffn_in, w_ffn_out=v_w_ffn_out)
    me = _flat(*_coords())
    slot = jnp.reshape(me, (1,)).astype(jnp.int32)

    own = {n: prm[n][0].astype(BF16) for n in MATRICES}
    no_dep = jnp.zeros((8, LANES), F32)
    (st_c, st_in), tok = _async_start([[c], [own["w_in"]]], ["gather", "spread"], no_dep, "gather_in_start")
    (c_own,), (c_land,) = _async_wait(st_c, tok, "gather_c_wait")
    c_all = _with_own(c_land, c_own, me).reshape(N_DEV, D)
    ada_cols = w_ada.shape[2]
    b_cols = lax.dynamic_slice(b_ada, (0, me * ada_cols), (1, ada_cols))
    mod_cols, silu_c = _mod_part(c_all, w_ada[0], b_cols, before=[own[n] for n in MATRICES if n != "w_in"])
    (mod_all,) = _all_gather([mod_cols], "gather_mod")

    (w_in_own,), (w_in_land,) = _async_wait(st_in, mod_all, "gather_in_wait")
    (st_in,), tok = _async_start([[w_in_land]], "forward", no_dep, "gather_in_forward")
    _, (w_in_land,) = _async_wait(st_in, tok, "gather_in_forward_wait")
    w = _prepare_weights({"w_in": w_in_land}, {"w_in": w_in_own}, slot)
    later = dict(lora=("w_uq", "w_ukv"), proj=("w_proj_fox", "w_proj_mla", "w_out"), ffn=("w_ffn_in", "w_ffn_out"))
    states, tok = _async_start([[own[n] for n in names] for names in later.values()], ["gather", "spread", "spread"],
                               w["w_b"], "gather_rest_start")
    gather_state = dict(zip(later, states))
    own_thru = {}

    def wts(group, after):
        if group.startswith("relay_"):
            name = group[len("relay_"):]
            own_thru[name], lands = _async_wait(gather_state[name], after, "gather_" + name + "_wait")
            (gather_state[name],), t = _async_start([lands], "forward", no_dep, "gather_" + name + "_forward")
            return {"tok": t}
        srcs, lands = _async_wait(gather_state[group], after, "gather_" + group + "_landed")
        srcs = own_thru.get(group, srcs)
        return _prepare_weights(dict(zip(later[group], lands)), dict(zip(later[group], srcs)), slot)

    sent, late_sent, last = [], [], {}

    def send(grads, final=False, late=False):
        shards = _shard_grads(grads)
        names = list(shards)
        (state,), t = _async_start([[shards[n] for n in names]], "pair" if final else "exchange", no_dep,
                                   "exchange_" + names[0] + "_start")
        if final:
            last.update(names=names, state=state)
        else:
            (late_sent if late else sent).append((names, state))
        return t

    def relay(after):
        srcs, lands = _async_wait(last["state"], after, "exchange_pair_wait")
        core = jnp.reshape(lax.axis_index("c"), (1,)).astype(jnp.int32)
        sums = [_add_sibling(src, land, core) for src, land in zip(srcs, lands)]
        (last["state"],), t = _async_start([sums], "chips", no_dep, "exchange_chips_start")
        return t

    mod = lax.dynamic_index_in_dim(mod_all, me, axis=1, keepdims=False).reshape(1, 6 * D) + tok[0, 0]

    vec = dict(g_pre_mix=g_pre_mix, g_post_mix=g_post_mix, g_pre_ffn=g_pre_ffn, g_post_ffn=g_post_ffn,
               g_q_lora=g_q_lora, g_kv_lora=g_kv_lora, b_forget=b_forget)
    pos = positions.astype(F32).reshape(S, 1)
    grad_x, small = _fwd_bwd(x[0], pos, mod, loss_target[0], w, vec, wts, send, relay)

    bundle = jnp.concatenate([small[n] for n in SMALL_ORDER], axis=1)
    (small_state,), tok = _async_start([[bundle]], "gather", jnp.zeros((8, LANES), F32), "gather_small_start")

    out = {}
    swap = lambda a: jnp.swapaxes(a, -1, -2)

    def update(n, land, src, sl):
        if n != "w_ffn_in":
            out[n] = _adamw(prm[n][0], mom[n][0], var[n][0], land, "adamw_" + n, src, sl)
            return out[n][0]
        res = _adamw(swap(prm[n][0]), swap(mom[n][0]), swap(var[n][0]), land, "adamw_" + n, src, sl)
        out[n] = tuple(swap(t) for t in res)
        return res[0]

    after = tok
    for names, state in sent:
        srcs, lands = _async_wait(state, after, "exchange_" + names[0] + "_wait")
        for n, src, land in zip(names, srcs, lands):
            after = update(n, land, src, slot)
    srcs, lands = _async_wait(last["state"], after, "exchange_chips_wait")
    for n, src, land in zip(last["names"], srcs, lands):
        after = update(n, land, src, slot // 2)
    for names, state in late_sent:
        srcs, lands = _async_wait(state, after, "exchange_" + names[0] + "_wait")
        for n, src, land in zip(names, srcs, lands):
            after = update(n, land, src, slot)

    (own_bundle,), (bundle_all,) = _async_wait(small_state, after, "gather_small_wait")
    bundle_all = _with_own(bundle_all, own_bundle, me)
    dmod_all = bundle_all[:, 0, :6 * D]
    dm_cols = lax.dynamic_slice(dmod_all, (0, me * ada_cols), (N_DEV, ada_cols))
    out["w_ada"] = _adamw_w_ada(w_ada[0], m_w_ada[0], v_w_ada[0], jnp.transpose(silu_c), dm_cols)

    offsets, off = {}, 0
    for n in SMALL_ORDER:
        offsets[n] = off
        off += small[n].shape[1]
    names = [SMALL_PARAM.get(n, n) for n in SMALL_ORDER if n != "err"]
    results, err = _adamw_rows(bundle_all, [offsets[n] for n in SMALL_ORDER if n != "err"],
                               [prm[n] for n in names], [mom[n] for n in names], [var[n] for n in names],
                               offsets["err"], D)
    out.update(zip(names, results))
    loss = 0.5 * jnp.sum(err) / D

    res = [loss, grad_x[None]]
    for kind in range(4):
        for n in WEIGHTS:
            t = out[n][kind]
            res.append(t[None] if prm[n].ndim == 3 else t)
    return tuple(res)
```

```python
import functools
import math

import jax
import jax.numpy as jnp
from jax import lax
from jax.experimental import pallas as pl
from jax.experimental.pallas import tpu as pltpu

F32 = jnp.float32
BF16 = jnp.bfloat16

N_DEV = 8
S = 2048
D = 1024
D_FF = 2816
HEADS = 8
HEAD_DIM = 64
Q_LORA = 768
KV_LORA = 256
ROPE_DIM = 32
ROPE_THETA = 10000.0
NORM_EPS = 1e-6
LANES = 128
VMEM_LIMIT = 56 * 1024 * 1024

ADAM_LR = 0.001
ADAM_B1 = 0.9
ADAM_B2 = 0.999
ADAM_EPS = 1e-08
ADAM_WD = 0.01
ADAM_STEP = 10

ATT_T = 256
LOG2E = 1.4426950408889634
N_ATT = S // ATT_T

NN = (((1,), (0,)), ((), ()))
NT = (((1,), (1,)), ((), ()))
TN = (((0,), (0,)), ((), ()))
MESH = pl.DeviceIdType.MESH


def _params(sem=None):
    return pltpu.CompilerParams(dimension_semantics=sem, vmem_limit_bytes=VMEM_LIMIT)


def _pick(n, cap):
    best = None
    for t in range(LANES, cap + 1, LANES):
        if n % t == 0:
            best = t
    return best if best is not None else n


def _mm(a, b, mode, out_dtype, name, acc=None, dep=None):
    if mode == "nn":
        (m, k), (k2, n), dn = a.shape, b.shape, NN
    elif mode == "nt":
        (m, k), (n, k2), dn = a.shape, b.shape, NT
    else:
        (k, m), (k2, n), dn = a.shape, b.shape, TN
    assert k == k2, (a.shape, b.shape, mode)
    tn = _pick(n, 1024)
    tm = _pick(m, 1536)
    osz = jnp.dtype(out_dtype).itemsize

    def need(tm_):
        blk = tm_ * k * 2 + tn * k * 2 + tm_ * tn * osz + (tm_ * tn * 4 if acc is not None else 0)
        return 2 * blk + tm_ * tn * 4
    while need(tm) > 36 * 1024 * 1024 and tm % 256 == 0:
        tm //= 2

    def body(*refs):
        a_ref, b_ref, o_ref = refs[0], refs[1], refs[-1]
        r = lax.dot_general(a_ref[...], b_ref[...], dn, preferred_element_type=F32)
        if acc is not None:
            r = r + refs[2][...]
        o_ref[...] = r.astype(o_ref.dtype)

    if mode == "tn":
        a_spec = pl.BlockSpec((k, tm), lambda i, j: (0, i))
    else:
        a_spec = pl.BlockSpec((tm, k), lambda i, j: (i, 0))
    if mode == "nt":
        b_spec = pl.BlockSpec((tn, k), lambda i, j: (j, 0))
    else:
        b_spec = pl.BlockSpec((k, tn), lambda i, j: (0, j))
    o_spec = pl.BlockSpec((tm, tn), lambda i, j: (i, j))
    in_specs = [a_spec, b_spec] + ([o_spec] if acc is not None else [])
    in_specs += [pl.BlockSpec(memory_space=pl.ANY)] if dep is not None else []
    args = (a, b) + ((acc,) if acc is not None else ()) + ((dep,) if dep is not None else ())
    return pl.pallas_call(
        body, name=name, grid=(m // tm, n // tn),
        in_specs=in_specs, out_specs=o_spec,
        out_shape=jax.ShapeDtypeStruct((m, n), out_dtype),
        compiler_params=_params(("parallel", "parallel")),
    )(*args)


def _offsets(widths):
    return [sum(widths[:p]) for p in range(len(widths))]


def _mm_cat(a, b, mode, out_dtype, name, dep=None):
    pieces = a if mode == "nt" else b
    widths = [p.shape[1] for p in pieces]
    offs = _offsets(widths)
    assert all(w_ % LANES == 0 for w_ in widths)
    resident = dict(pipeline_mode=pl.Buffered(1))
    if mode == "nt":
        m, (n, k) = pieces[0].shape[0], b.shape
        assert k == sum(widths)
        tm = _pick(m, 512)

        def body(*refs):
            b_ref, o_ref = refs[len(pieces)], refs[-1]
            r = None
            for p_ref, off, w_ in zip(refs, offs, widths):
                t = lax.dot_general(p_ref[...], b_ref[:, off:off + w_], NT, preferred_element_type=F32)
                r = t if r is None else r + t
            o_ref[...] = r.astype(o_ref.dtype)
        in_specs = [pl.BlockSpec((tm, w_), lambda i: (i, 0)) for w_ in widths]
        in_specs.append(pl.BlockSpec((n, k), lambda i: (0, 0), **resident))
        args = (*pieces, b)
    else:
        assert mode == "tn"
        (k, m), n = a.shape, sum(widths)
        tm = _pick(m, 512)

        def body(*refs):
            a_ref, o_ref = refs[0], refs[-1]
            for p_ref, off, w_ in zip(refs[1:], offs, widths):
                o_ref[:, off:off + w_] = lax.dot_general(
                    a_ref[...], p_ref[...], TN, preferred_element_type=F32).astype(o_ref.dtype)
        in_specs = [pl.BlockSpec((k, tm), lambda i: (0, i))]
        in_specs += [pl.BlockSpec((k, w_), lambda i: (0, 0), **resident) for w_ in widths]
        args = (a, *pieces)
    if dep is not None:
        in_specs.append(pl.BlockSpec(memory_space=pl.ANY))
        args += (dep,)
    return pl.pallas_call(
        body, name=name, grid=(m // tm,),
        in_specs=in_specs, out_specs=pl.BlockSpec((tm, n), lambda i: (i, 0)),
        out_shape=jax.ShapeDtypeStruct((m, n), out_dtype),
        compiler_params=_params(("parallel",)),
    )(*args)


def _mm_epi(a, b, mode, tnb, epi, name, tm, rows=(), vecs=(), outs=(), sums=(), pro=None, dep=None):
    pieces = list(a) if isinstance(a, (list, tuple)) else [a]
    assert len(pieces) == 1 or pro is None
    widths = [p.shape[1] for p in pieces]
    offs = _offsets(widths)
    m = pieces[0].shape[0]
    k, nb = (b.shape if mode == "nn" else b.shape[::-1])
    dn = NN if mode == "nn" else NT
    pro_fn, pro_vecs, a_off = pro if pro is not None else (None, (), 0)
    n_in = 2 + len(rows) + len(vecs)
    n_all = n_in + len(pro_vecs)
    sub = min(tm, 256)

    def body(*refs):
        a_refs, refs = refs[:len(pieces)], refs[len(pieces) - 1:]
        if pro is not None:
            a_out, a_scr = refs[-2:]
            refs = refs[:-2]

            @pl.when(pl.program_id(1) == 0)
            def _():
                a_scr[...] = pro_fn(refs[0][...], *[x[...] for x in refs[n_in:n_all]]).astype(BF16)
                a_out[...] = a_scr[...]
            a_ref = a_scr
        else:
            a_ref = refs[0]
        n_skip = n_all + (dep is not None)
        o_refs = refs[n_skip:n_skip + len(outs)]
        s_refs = refs[n_skip + len(outs):]
        if sums:
            @pl.when((pl.program_id(0) == 0) & (pl.program_id(1) == 0))
            def _():
                for s_ref in s_refs:
                    s_ref[...] = jnp.zeros(s_ref.shape, F32)
        for c in range(tm // sub):
            rs = slice(c * sub, (c + 1) * sub)
            if len(pieces) == 1:
                r = lax.dot_general(a_ref[rs, :], refs[1][...], dn, preferred_element_type=F32)
            else:
                r = None
                for p_ref, off, w_ in zip(a_refs, offs, widths):
                    b_part = refs[1][off:off + w_, :] if mode == "nn" else refs[1][:, off:off + w_]
                    t = lax.dot_general(p_ref[rs, :], b_part, dn, preferred_element_type=F32)
                    r = t if r is None else r + t
            o_vals, s_vals = epi(r, *[x[rs, :] for x in refs[2:2 + len(rows)]], *[x[...] for x in refs[2 + len(rows):n_in]])
            assert len(o_vals) == len(o_refs) and len(s_vals) == len(s_refs)
            for o_ref, val in zip(o_refs, o_vals):
                o_ref[rs, :] = val.astype(o_ref.dtype)
            for s_ref, val in zip(s_refs, s_vals):
                s_ref[...] += val

    once = dict(pipeline_mode=pl.Buffered(1)) if nb == tnb else {}
    if mode == "nn":
        b_spec = pl.BlockSpec((k, tnb), lambda i, j: (0, j), **once)
    else:
        b_spec = pl.BlockSpec((tnb, k), lambda i, j: (j, 0), **once)
    if len(pieces) == 1:
        in_specs = [pl.BlockSpec((tm, k), lambda i, j: (i, a_off)), b_spec]
    else:
        assert sum(widths) == k and all(w_ % LANES == 0 for w_ in widths)
        in_specs = [pl.BlockSpec((tm, w_), lambda i, j: (i, 0)) for w_ in widths] + [b_spec]
    rows = [tuple(r) + (0,) * (3 - len(r)) for r in rows]
    in_specs += [pl.BlockSpec((tm, w), functools.partial(lambda i, j, off: (i, j + off), off=off)) for _, w, off in rows]
    in_specs += [pl.BlockSpec(v.shape, lambda i, j: (0, 0)) for v in list(vecs) + list(pro_vecs)]
    in_specs += [pl.BlockSpec(memory_space=pl.ANY)] if dep is not None else []
    out_specs = [pl.BlockSpec((tm, w), lambda i, j: (i, j)) for _, w, _ in outs]
    out_specs += [pl.BlockSpec((1, w), lambda i, j: (0, 0)) for w in sums]
    out_shape = [jax.ShapeDtypeStruct((m, full), dt) for full, _, dt in outs]
    out_shape += [jax.ShapeDtypeStruct((1, w), F32) for w in sums]
    if pro is not None:
        out_specs.append(pl.BlockSpec((tm, k), lambda i, j: (i, 0)))
        out_shape.append(jax.ShapeDtypeStruct((m, k), BF16))
    return pl.pallas_call(
        body, name=name, grid=(m // tm, nb // tnb),
        in_specs=in_specs, out_specs=out_specs, out_shape=out_shape,
        scratch_shapes=[pltpu.VMEM((tm, k), BF16)] if pro is not None else [],
        compiler_params=_params(("arbitrary", "arbitrary") if sums else ("parallel", "arbitrary" if pro is not None else "parallel")),
    )(*pieces, b, *[r[0] for r in rows], *vecs, *pro_vecs, *([dep] if dep is not None else []))


def _rowwise(fn, row_ins, vec_ins, row_outs, sum_outs, name, tm=512):
    n_in = len(row_ins) + len(vec_ins)
    n_o = len(row_outs)
    rows = row_ins[0][0].shape[0]

    def body(*refs):
        vals = [r[...] for r in refs[:n_in]]
        outs = refs[n_in:]
        ro, so = fn(*vals)
        assert len(ro) == n_o and len(so) == len(sum_outs)
        for r, v in zip(outs[:n_o], ro):
            r[...] = v.astype(r.dtype)
        if sum_outs:
            @pl.when(pl.program_id(0) == 0)
            def _():
                for r in outs[n_o:]:
                    r[...] = jnp.zeros(r.shape, F32)
            for r, v in zip(outs[n_o:], so):
                r[...] += v

    in_specs = [pl.BlockSpec((tm, w), functools.partial(lambda i, b: (i, b), b=b)) for _, w, b in row_ins]
    in_specs += [pl.BlockSpec(v.shape, lambda i: (0, 0)) for v in vec_ins]
    out_specs = [pl.BlockSpec((tm, w), lambda i: (i, 0)) for w, _ in row_outs]
    out_specs += [pl.BlockSpec((1, w), lambda i: (0, 0)) for w in sum_outs]
    out_shape = [jax.ShapeDtypeStruct((rows, w), dt) for w, dt in row_outs]
    out_shape += [jax.ShapeDtypeStruct((1, w), F32) for w in sum_outs]
    return pl.pallas_call(
        body, name=name, grid=(rows // tm,),
        in_specs=in_specs, out_specs=out_specs, out_shape=out_shape,
        compiler_params=_params(("arbitrary",)),
    )(*[a for a, _, _ in row_ins], *vec_ins)


def _sigmoid(x):
    return 1.0 / (1.0 + jnp.exp(-x))


def _rstd(x):
    return lax.rsqrt(jnp.mean(x * x, axis=-1, keepdims=True) + NORM_EPS)


def _norm_bwd(dyn, xn, r):
    return r * (dyn - xn * jnp.mean(dyn * xn, axis=-1, keepdims=True))


def _colsum(x):
    return jnp.sum(x, axis=0, keepdims=True)


def _rope_tables(pos, invf):
    def fn(p, f):
        lane = lax.broadcasted_iota(jnp.int32, (1, LANES), 1)
        ang = p * f
        cs, sn = jnp.cos(ang), jnp.sin(ang)
        rot = (lane >= 64) & (lane < 96)
        ct = jnp.where(lane < 64, 1.0, jnp.where(rot, cs, 0.0))
        sa = jnp.where((lane >= 64) & (lane < 80), -sn, 0.0)
        sb = jnp.where((lane >= 80) & (lane < 96), sn, 0.0)
        return (ct, sa, sb), ()
    return _rowwise(fn, [(pos, 1, 0)], [invf], [(LANES, F32)] * 3, [], "rope_tables")


def _rope(x, ct, sa, sb):
    return x * ct + pltpu.roll(x, LANES - 16, 1) * sa + pltpu.roll(x, 16, 1) * sb


def _rope_t(x, ct, sa, sb):
    return x * ct - pltpu.roll(x, LANES - 16, 1) * sa - pltpu.roll(x, 16, 1) * sb


def _head_mask(width, hh):
    lane = lax.broadcasted_iota(jnp.int32, (1, width), 1)
    half = width // 2
    return (lane >= hh * half) & (lane < (hh + 1) * half)


ATT_PP = 2
ATT_CHAINS = [(a, hh) for a in range(ATT_PP) for hh in range(2)]
ATT_G = HEADS // (2 * ATT_PP)


def _pair(ref_or_val, a, width, rows=slice(None)):
    return ref_or_val[rows, a * width:(a + 1) * width]


def _head_cols(ref, a, hh, dkp, rows=slice(None)):
    if dkp == 2 * LANES:
        return ref[rows, a * dkp + hh * LANES:a * dkp + (hh + 1) * LANES]
    blk = _pair(ref, a, dkp, rows)
    return jnp.where(_head_mask(dkp, hh), blk, jnp.zeros_like(blk))


def _attn_fwd(q, qo, k, ko, v, vo, dkp, scale, bias, name, dep=None):
    T = ATT_T
    assert qo % ATT_PP == 0 and ko % ATT_PP == 0 and vo % ATT_PP == 0
    qo, ko, vo = qo // ATT_PP, ko // ATT_PP, vo // ATT_PP
    split = dkp == 2 * LANES

    def body(*refs):
        refs = list(refs)
        if dep is not None:
            del refs[3 + (bias is not None)]
        if bias is not None:
            q_ref, k_ref, v_ref, b_ref, o_ref, lse_ref, s_scr = refs
        else:
            q_ref, k_ref, v_ref, o_ref, lse_ref, s_scr = refs
        i = pl.program_id(1)
        row = lax.broadcasted_iota(jnp.int32, (T, T), 0)
        col = lax.broadcasted_iota(jnp.int32, (T, T), 1)
        qms = [_head_cols(q_ref, a, hh, dkp) for a, hh in ATT_CHAINS]

        def k_of(a, hh, ks):
            return _head_cols(k_ref, a, hh, dkp, ks) if split else _pair(k_ref, a, dkp, ks)

        def fold(t):
            return [t[:, c * LANES:(c + 1) * LANES] for c in range(T // LANES)]

        def run(nt):
            mls = [jnp.full((T, LANES), -jnp.inf, F32) for _ in ATT_CHAINS]
            for j in range(nt):
                ks = slice(j * T, (j + 1) * T)
                for ci, (a, hh) in enumerate(ATT_CHAINS):
                    s = lax.dot_general(qms[ci], k_of(a, hh, ks), NT, preferred_element_type=F32) * (scale * LOG2E)
                    if bias is not None:
                        s = s + b_ref[2 * a + hh, j] * LOG2E
                    if j == nt - 1:
                        s = jnp.where(row >= col, s, -jnp.inf)
                    s_scr[ci, j] = s
                    for part in fold(s):
                        mls[ci] = jnp.maximum(mls[ci], part)
            ms = [jnp.max(ml, axis=1, keepdims=True) for ml in mls]
            mbs = [jnp.broadcast_to(m, (T, LANES)) for m in ms]
            for a in range(ATT_PP):
                ls = [jnp.zeros((T, LANES), F32) for _ in range(2)]
                ps, vms = [], []
                for j in range(nt):
                    vb = _pair(v_ref, a, LANES, slice(j * T, (j + 1) * T))
                    for hh in range(2):
                        parts = [jnp.exp2(part - mbs[2 * a + hh]) for part in fold(s_scr[2 * a + hh, j])]
                        for part in parts:
                            ls[hh] = ls[hh] + part
                        ps.append(jnp.concatenate(parts, axis=1).astype(BF16))
                        vms.append(jnp.where(_head_mask(LANES, hh), vb, jnp.zeros_like(vb)))
                acc = lax.dot_general(jnp.concatenate(ps, axis=1), jnp.concatenate(vms, axis=0), NN,
                                      preferred_element_type=F32)
                l0, l1 = [jnp.sum(l, axis=1, keepdims=True) for l in ls]
                lse_ref[2 * a] = ms[2 * a] + jnp.log2(l0)
                lse_ref[2 * a + 1] = ms[2 * a + 1] + jnp.log2(l1)
                inv = jnp.where(_head_mask(LANES, 0), 1.0 / l0, 1.0 / l1)
                o_ref[:, a * LANES:(a + 1) * LANES] = (acc * inv).astype(o_ref.dtype)

        for nt in range(1, N_ATT + 1):
            pl.when(i == nt - 1)(functools.partial(run, nt))

    in_specs = [
        pl.BlockSpec((T, ATT_PP * dkp), lambda g, i: (i, qo + g)),
        pl.BlockSpec((S, ATT_PP * dkp), lambda g, i: (0, ko + g)),
        pl.BlockSpec((S, ATT_PP * LANES), lambda g, i: (0, vo + g)),
    ]
    args = [q, k, v]
    if bias is not None:
        in_specs.append(pl.BlockSpec((2 * ATT_PP, N_ATT, 1, T), lambda g, i: (g, 0, 0, 0)))
        args.append(bias)
    if dep is not None:
        in_specs.append(pl.BlockSpec(memory_space=pl.ANY))
        args.append(dep)
    return pl.pallas_call(
        body, name=name, grid=(ATT_G, N_ATT),
        in_specs=in_specs,
        out_specs=[pl.BlockSpec((T, ATT_PP * LANES), lambda g, i: (i, g)),
                   pl.BlockSpec((2 * ATT_PP, T, 1), lambda g, i: (g, i, 0))],
        out_shape=[jax.ShapeDtypeStruct((S, HEADS * HEAD_DIM), BF16),
                   jax.ShapeDtypeStruct((HEADS, S, 1), F32)],
        scratch_shapes=[pltpu.VMEM((len(ATT_CHAINS), N_ATT, T, T), F32)],
        compiler_params=_params(("parallel", "arbitrary")),
    )(*args)


def _attn_grad(q, qo, k, ko, v, vo, do, lse, dkp, scale, bias, qk_dtype, name, dep=None):
    T = ATT_T
    has_b = bias is not None
    qo, ko, vo = qo // ATT_PP, ko // ATT_PP, vo // ATT_PP
    n_ch = len(ATT_CHAINS)
    split = dkp == 2 * LANES

    def body(*refs):
        refs = list(refs)
        if dep is not None:
            del refs[5 + has_b]
        q_ref, k_ref, v_ref, do_ref, lse_ref = refs[:5]
        refs = refs[5:]
        if has_b:
            b_ref, refs = refs[0], refs[1:]
        dq_ref, dk_ref, dv_ref = refs[:3]
        refs = refs[3:]
        if has_b:
            db_ref, refs = refs[0], refs[1:]
        p_scr, dp_scr, dk_acc, dv_acc = refs[:4]
        db_acc = refs[4] if has_b else None
        i = pl.program_id(1)

        @pl.when(i == 0)
        def _():
            dk_acc[...] = jnp.zeros(dk_acc.shape, F32)
            dv_acc[...] = jnp.zeros(dv_acc.shape, F32)
            if has_b:
                db_acc[...] = jnp.zeros(db_acc.shape, F32)

        row = lax.broadcasted_iota(jnp.int32, (T, T), 0)
        col = lax.broadcasted_iota(jnp.int32, (T, T), 1)

        def fold(t):
            return [t[:, c * LANES:(c + 1) * LANES] for c in range(T // LANES)]

        qms, doms, lses = [], [], []
        for a, hh in ATT_CHAINS:
            dob = _pair(do_ref, a, LANES)
            qms.append(_head_cols(q_ref, a, hh, dkp))
            doms.append(jnp.where(_head_mask(LANES, hh), dob, jnp.zeros_like(dob)))
            lses.append(lse_ref[2 * a + hh])

        def k_of(a, hh, ks):
            return _head_cols(k_ref, a, hh, dkp, ks) if split else _pair(k_ref, a, dkp, ks)

        def run(nt):
            dls = [jnp.zeros((T, LANES), F32) for _ in ATT_CHAINS]
            for j in range(nt):
                ks = slice(j * T, (j + 1) * T)
                for ci, (a, hh) in enumerate(ATT_CHAINS):
                    s = lax.dot_general(qms[ci], k_of(a, hh, ks), NT, preferred_element_type=F32) * (scale * LOG2E)
                    if has_b:
                        s = s + b_ref[ci, j] * LOG2E
                    s = s - lses[ci]
                    if j == nt - 1:
                        s = jnp.where(row >= col, s, -jnp.inf)
                    p = jnp.exp2(s)
                    dp = lax.dot_general(doms[ci], _pair(v_ref, a, LANES, ks), NT, preferred_element_type=F32)
                    p_scr[ci, j] = p
                    dp_scr[ci, j] = dp
                    for part in fold(p * dp):
                        dls[ci] = dls[ci] + part
            deltas = [jnp.broadcast_to(jnp.sum(dl, axis=1, keepdims=True), (T, LANES)) for dl in dls]
            for a in range(ATT_PP):
                ds_all, km_all = [], []
                if split:
                    qts = [jnp.transpose(qms[2 * a + hh]) for hh in range(2)]
                else:
                    qm2t = jnp.transpose(jnp.concatenate([qms[2 * a], qms[2 * a + 1]], axis=0))
                dom2t = jnp.transpose(jnp.concatenate([doms[2 * a], doms[2 * a + 1]], axis=0))
                for j in range(nt):
                    ks = slice(j * T, (j + 1) * T)
                    p2, ds2 = [], []
                    for hh in range(2):
                        ci = 2 * a + hh
                        p = p_scr[ci, j]
                        ds = jnp.concatenate([pp * (dd - deltas[ci]) for pp, dd in zip(fold(p), fold(dp_scr[ci, j]))], axis=1)
                        if has_b:
                            db_acc[ci, j] += jnp.sum(ds, axis=0, keepdims=True)
                        p2.append(p.astype(BF16))
                        ds2.append((ds * scale).astype(BF16))
                        if not split:
                            km_all.append(_head_cols(k_ref, a, hh, dkp, ks))
                    dv_acc[a * LANES:(a + 1) * LANES, ks] += lax.dot_general(
                        dom2t, jnp.concatenate(p2, axis=0), NN, preferred_element_type=F32)
                    if split:
                        for hh in range(2):
                            dk_acc[a * dkp + hh * LANES:a * dkp + (hh + 1) * LANES, ks] += lax.dot_general(
                                qts[hh], ds2[hh], NN, preferred_element_type=F32)
                    else:
                        dk_acc[a * dkp:(a + 1) * dkp, ks] += lax.dot_general(
                            qm2t, jnp.concatenate(ds2, axis=0), NN, preferred_element_type=F32)
                    ds_all += ds2
                if split:
                    for hh in range(2):
                        dq = lax.dot_general(jnp.concatenate(ds_all[hh::2], axis=1),
                                             _head_cols(k_ref, a, hh, dkp, slice(0, nt * T)), NN,
                                             preferred_element_type=F32)
                        dq_ref[:, a * dkp + hh * LANES:a * dkp + (hh + 1) * LANES] = dq.astype(dq_ref.dtype)
                else:
                    dq = lax.dot_general(jnp.concatenate(ds_all, axis=1), jnp.concatenate(km_all, axis=0), NN,
                                         preferred_element_type=F32)
                    dq_ref[:, a * dkp:(a + 1) * dkp] = dq.astype(dq_ref.dtype)

        for nt in range(1, N_ATT + 1):
            pl.when(i == nt - 1)(functools.partial(run, nt))

        @pl.when(i == N_ATT - 1)
        def _():
            dk_ref[...] = jnp.transpose(dk_acc[...]).astype(dk_ref.dtype)
            dv_ref[...] = jnp.transpose(dv_acc[...]).astype(dv_ref.dtype)
            if has_b:
                db_ref[...] = db_acc[...]

    in_specs = [
        pl.BlockSpec((T, ATT_PP * dkp), lambda g, i: (i, qo + g)),
        pl.BlockSpec((S, ATT_PP * dkp), lambda g, i: (0, ko + g)),
        pl.BlockSpec((S, ATT_PP * LANES), lambda g, i: (0, vo + g)),
        pl.BlockSpec((T, ATT_PP * LANES), lambda g, i: (i, g)),
        pl.BlockSpec((2 * ATT_PP, T, 1), lambda g, i: (g, i, 0)),
    ]
    args = [q, k, v, do, lse]
    out_specs = [
        pl.BlockSpec((T, ATT_PP * dkp), lambda g, i: (i, g)),
        pl.BlockSpec((S, ATT_PP * dkp), lambda g, i: (0, g)),
        pl.BlockSpec((S, ATT_PP * LANES), lambda g, i: (0, g)),
    ]
    width = (HEADS // 2) * dkp
    out_shape = [
        jax.ShapeDtypeStruct((S, width), qk_dtype),
        jax.ShapeDtypeStruct((S, width), qk_dtype),
        jax.ShapeDtypeStruct((S, HEADS * HEAD_DIM), BF16),
    ]
    scratch = [pltpu.VMEM((n_ch, N_ATT, T, T), F32), pltpu.VMEM((n_ch, N_ATT, T, T), F32),
               pltpu.VMEM((ATT_PP * dkp, S), F32), pltpu.VMEM((ATT_PP * LANES, S), F32)]
    if has_b:
        bspec = pl.BlockSpec((2 * ATT_PP, N_ATT, 1, T), lambda g, i: (g, 0, 0, 0))
        in_specs.append(bspec)
        args.append(bias)
        out_specs.append(bspec)
        out_shape.append(jax.ShapeDtypeStruct((HEADS, N_ATT, 1, T), F32))
        scratch.append(pltpu.VMEM((2 * ATT_PP, N_ATT, 1, T), F32))
    if dep is not None:
        in_specs.append(pl.BlockSpec(memory_space=pl.ANY))
        args.append(dep)
    return pl.pallas_call(
        body, name=name, grid=(ATT_G, N_ATT),
        in_specs=in_specs, out_specs=out_specs, out_shape=out_shape, scratch_shapes=scratch,
        compiler_params=_params(("parallel", "arbitrary")),
    )(*args)


def _tri(upper):
    a = lax.broadcasted_iota(jnp.int32, (LANES, LANES), 0)
    b = lax.broadcasted_iota(jnp.int32, (LANES, LANES), 1)
    return jnp.where(a <= b if upper else a >= b, 1.0, 0.0).astype(F32)


def _fox_gates(proj, blk, bf):
    def body(m_ref, b_ref, z_out, o_ref):
        tri = _tri(True)
        carry = jnp.zeros((HEADS, 1), F32)
        for t in range(S // LANES):
            sl = slice(t * LANES, (t + 1) * LANES)
            zt = jnp.transpose(m_ref[sl, :])[:HEADS]
            z_out[:, sl] = zt
            z = zt + b_ref[...]
            logf = jnp.minimum(z, 0.0) - jnp.log(1.0 + jnp.exp(-jnp.abs(z)))
            c = lax.dot_general(logf, tri, NN, preferred_element_type=F32,
                                precision=lax.Precision.HIGHEST) + carry
            o_ref[:, sl] = -c
            carry = c[:, LANES - 1:LANES]

    return pl.pallas_call(
        body, name="fox_gates", grid=(1,),
        in_specs=[pl.BlockSpec((S, LANES), lambda i: (0, blk)), pl.BlockSpec(bf.shape, lambda i: (0, 0))],
        out_specs=[pl.BlockSpec((HEADS, S), lambda i: (0, 0))] * 2,
        out_shape=[jax.ShapeDtypeStruct((HEADS, S), F32)] * 2,
        compiler_params=_params(("arbitrary",)),
    )(proj, bf)


def _fox_gates_bwd(dbias, zt, bf):
    def body(d_ref, z_ref, b_ref, dz_ref, dbf_ref):
        tri = _tri(False)
        carry = jnp.zeros((HEADS, 1), F32)
        tot = jnp.zeros((HEADS, 1), F32)
        for t in reversed(range(S // LANES)):
            sl = slice(t * LANES, (t + 1) * LANES)
            df = -d_ref[:, sl]
            c = lax.dot_general(df, tri, NN, preferred_element_type=F32,
                                precision=lax.Precision.HIGHEST) + carry
            carry = c[:, 0:1]
            z = z_ref[:, sl] + b_ref[...]
            dz = c * _sigmoid(-z)
            dz_ref[:, sl] = dz
            tot = tot + jnp.sum(dz, axis=1, keepdims=True)
        dbf_ref[...] = tot

    return pl.pallas_call(
        body, name="fox_gates_bwd",
        out_shape=[jax.ShapeDtypeStruct((HEADS, S), F32), jax.ShapeDtypeStruct((HEADS, 1), F32)],
        compiler_params=_params(),
    )(dbias, zt, bf)


def _mod_part(c_all, w_ada, b_cols, before=()):
    def body(c_ref, w_ref, b_ref, *rest):
        o_ref, s_ref = rest[-2:]
        c = c_ref[...]
        sc = c * _sigmoid(c)
        s_ref[...] = sc
        o_ref[...] = lax.dot_general(sc, w_ref[...], NN, preferred_element_type=F32,
                                     precision=lax.Precision.HIGHEST) + b_ref[...]

    return pl.pallas_call(
        body, name="mod_part",
        in_specs=[pl.BlockSpec(memory_space=pltpu.VMEM)] * 3 + [pl.BlockSpec(memory_space=pl.ANY)] * len(before),
        out_shape=[jax.ShapeDtypeStruct((N_DEV, w_ada.shape[1]), F32), jax.ShapeDtypeStruct(c_all.shape, F32)],
        compiler_params=_params(),
    )(c_all, w_ada, b_cols, *before)


def _adamw_w_ada(w, m, v, sc_t, dm):
    rows, cols = w.shape
    tr = 256

    def body(w_ref, m_ref, v_ref, s_ref, d_ref, g_out, d_out, m_out, v_out):
        g = s_ref[:, 0:1] * d_ref[0:1, :]
        for b in range(1, N_DEV):
            g = g + s_ref[:, b:b + 1] * d_ref[b:b + 1, :]
        g_out[...] = g
        d_out[...], m_out[...], v_out[...] = _adamw_math(w_ref[...], g, m_ref[...], v_ref[...])

    spec = pl.BlockSpec((tr, cols), lambda i: (i, 0))
    return pl.pallas_call(
        body, name="adamw_w_ada", grid=(rows // tr,),
        in_specs=[spec, spec, spec, pl.BlockSpec((tr, N_DEV), lambda i: (i, 0)), pl.BlockSpec(dm.shape, lambda i: (0, 0))],
        out_specs=[spec] * 4, out_shape=[jax.ShapeDtypeStruct((rows, cols), F32)] * 4,
        compiler_params=_params(("parallel",)),
    )(w, m, v, sc_t, dm)


def _adamw(w, m, v, parts, name, own=None, slot=None):
    rows, cols = w.shape
    n = parts.shape[0]
    by_cols = rows % 256 != 0 and cols % 256 == 0
    tr, tc = (rows, 256) if by_cols else ((rows if rows <= 512 else 256), cols)
    tile = (lambda i: (0, i)) if by_cols else (lambda i: (i, 0))

    def body(*refs):
        if own is not None:
            s_ref, refs = refs[0], refs[1:]
            w_ref, m_ref, v_ref, p_ref, o_ref, g_out, d_out, m_out, v_out = refs
            terms = [jnp.where(s_ref[0] == kk, o_ref[0], p_ref[kk]) for kk in range(n)]
        else:
            w_ref, m_ref, v_ref, p_ref, g_out, d_out, m_out, v_out = refs
            terms = [p_ref[kk] for kk in range(n)]
        g = terms[0].astype(F32)
        for term in terms[1:]:
            g = g + term.astype(F32)
        g_out[...] = g
        d_out[...], m_out[...], v_out[...] = _adamw_math(w_ref[...], g, m_ref[...], v_ref[...])

    spec = pl.BlockSpec((tr, tc), lambda i, *_: tile(i))
    in_specs = [spec, spec, spec, pl.BlockSpec((n, tr, tc), lambda i, *_: (0,) + tile(i))]
    out_shape = [jax.ShapeDtypeStruct((rows, cols), F32)] * 4
    grid = (rows // tr if not by_cols else cols // tc,)
    if own is None:
        return pl.pallas_call(
            body, name=name, grid=grid, in_specs=in_specs, out_specs=[spec] * 4, out_shape=out_shape,
            compiler_params=_params(("parallel",)),
        )(w, m, v, parts)
    in_specs.append(pl.BlockSpec((1, tr, tc), lambda i, s: (s[0],) + tile(i)))
    return pl.pallas_call(
        body, name=name, out_shape=out_shape, compiler_params=_params(("parallel",)),
        grid_spec=pltpu.PrefetchScalarGridSpec(num_scalar_prefetch=1, grid=grid, in_specs=in_specs,
                                               out_specs=[spec] * 4),
    )(slot, w, m, v, parts, own)


def _adamw_math(w, g, m, v):
    mm = ADAM_B1 * m + (1.0 - ADAM_B1) * g
    vv = ADAM_B2 * v + (1.0 - ADAM_B2) * (g * g)
    m_hat = mm / (1.0 - ADAM_B1 ** ADAM_STEP)
    v_hat = vv / (1.0 - ADAM_B2 ** ADAM_STEP)
    return -ADAM_LR * (m_hat / (jnp.sqrt(v_hat) + ADAM_EPS) + ADAM_WD * w), mm, vv


def _adamw_rows(bundles, offsets, ws, ms, vs, err_off, err_width):
    k = len(ws)

    def body(*refs):
        b_ref = refs[0]
        w_refs, m_refs, v_refs = refs[1:1 + k], refs[1 + k:1 + 2 * k], refs[1 + 2 * k:1 + 3 * k]
        outs = refs[1 + 3 * k:]
        g_all = b_ref[0]
        for kk in range(1, N_DEV):
            g_all = g_all + b_ref[kk]
        for i in range(k):
            width = w_refs[i].shape[1]
            g = g_all[:, offsets[i]:offsets[i] + width]
            outs[4 * i][...] = g
            outs[4 * i + 1][...], outs[4 * i + 2][...], outs[4 * i + 3][...] = _adamw_math(
                w_refs[i][...], g, m_refs[i][...], v_refs[i][...])
        outs[4 * k][...] = g_all[:, err_off:err_off + err_width]

    out_shape = []
    for w_ in ws:
        out_shape += [jax.ShapeDtypeStruct(w_.shape, F32)] * 4
    out_shape.append(jax.ShapeDtypeStruct((1, err_width), F32))
    res = pl.pallas_call(body, name="adamw_rows", out_shape=out_shape, compiler_params=_params())(bundles, *ws, *ms, *vs)
    return [tuple(res[4 * i:4 * i + 4]) for i in range(k)], res[-1]


def _coords():
    return lax.axis_index("x"), lax.axis_index("y"), lax.axis_index("c")


def _flat(px, py, pc):
    return 4 * px + 2 * py + pc


def _all_gather(arrs, name):
    n = len(arrs)

    def body(*refs):
        ins, outs = refs[:n], refs[n:2 * n]
        send, recv, lsem = refs[2 * n:]
        x, y, c = _coords()
        me, sibling = (x, y, c), (x, y, 1 - c)
        chips = [(1 - x, y), (x, 1 - y), (1 - x, 1 - y)]

        def copy(a, kk, block, to, src=None):
            slot = outs[a].at[_flat(*block)]
            return pltpu.make_async_remote_copy(
                src_ref=slot if src is None else src, dst_ref=slot,
                send_sem=send.at[a, kk], recv_sem=recv.at[a, kk],
                device_id=to, device_id_type=MESH)

        mine = [pltpu.make_async_copy(ins[a], outs[a].at[_flat(*me)], lsem.at[a]) for a in range(n)]
        for cp in mine:
            cp.start()
        first = []
        for a in range(n):
            first.append(copy(a, 0, me, sibling, src=ins[a]))
            first += [copy(a, 1 + j, me, (*chip, c), src=ins[a]) for j, chip in enumerate(chips)]
        for cp in first:
            cp.start()
        passed = []
        for j, chip in enumerate(chips):
            for a in range(n):
                copy(a, 1 + j, (*chip, c), me).wait_recv()
                cp = copy(a, 4 + j, (*chip, c), sibling)
                cp.start()
                passed.append(cp)
        for a in range(n):
            copy(a, 0, sibling, me).wait_recv()
        for j, chip in enumerate(chips):
            for a in range(n):
                copy(a, 4 + j, (*chip, 1 - c), me).wait_recv()
        for cp in first + passed:
            cp.wait_send()
        for cp in mine:
            cp.wait()

    any_spec = pl.BlockSpec(memory_space=pl.ANY)
    return pl.pallas_call(
        body, name=name,
        in_specs=[any_spec] * n, out_specs=[any_spec] * n,
        out_shape=[jax.ShapeDtypeStruct((N_DEV,) + a.shape, a.dtype) for a in arrs],
        scratch_shapes=[pltpu.SemaphoreType.DMA((n, 7)), pltpu.SemaphoreType.DMA((n, 7)),
                        pltpu.SemaphoreType.DMA((n,))],
    )(*arrs)


def _peer_list():
    x, y, c = _coords()
    return [((1 - x if r & 4 else x), (1 - y if r & 2 else y), (1 - c if r & 1 else c)) for r in range(1, N_DEV)]


def _copy_plan(mode, src, land):
    x, y, c = _coords()
    me = _flat(x, y, c)
    if mode == "gather":
        return [(src, land.at[me], peer) for peer in _peer_list()]
    if mode == "exchange":
        return [(src.at[_flat(*peer)], land.at[me], peer) for peer in _peer_list()]
    if mode == "pair":
        return [(src.at[_flat(q // 2, q % 2, 1 - c)], land.at[q], (x, y, 1 - c)) for q in range(N_DEV // 2)]
    chips = [((1 - x if r & 2 else x), (1 - y if r & 1 else y)) for r in range(1, N_DEV // 2)]
    if mode == "chips":
        return [(src.at[2 * qx + qy], land.at[2 * x + y], (qx, qy, c)) for qx, qy in chips]
    if mode == "spread":
        return [(src, land.at[me], (x, y, 1 - c))] + [(src, land.at[me], (qx, qy, c)) for qx, qy in chips]
    assert mode == "forward"
    return [(land.at[_flat(qx, qy, c)], land.at[_flat(qx, qy, c)], (x, y, 1 - c)) for qx, qy in chips]


N_COPIES = dict(gather=N_DEV - 1, exchange=N_DEV - 1, pair=N_DEV // 2, chips=N_DEV // 2 - 1, spread=N_DEV // 2,
                forward=N_DEV // 2 - 1)


def _land_shape(mode, shape):
    return {"gather": (N_DEV,) + shape, "spread": (N_DEV,) + shape, "exchange": shape,
            "pair": (N_DEV // 2,) + shape[1:], "chips": shape}[mode]


HBM_SPEC = pl.BlockSpec(memory_space=pltpu.HBM)
SEM_SPEC = pl.BlockSpec(memory_space=pltpu.SEMAPHORE)
ANY_SPEC = pl.BlockSpec(memory_space=pl.ANY)
SIDE_EFFECT = pltpu.SideEffectType.DATAFLOW_SIDE_EFFECTING


def _async_start(groups, modes, after, name):
    modes = [modes] * len(groups) if isinstance(modes, str) else list(modes)
    arrs = [(a, m) for g, m in zip(groups, modes) for a in g]
    n = len(arrs)
    fresh = [i for i, (_, m) in enumerate(arrs) if m != "forward"]

    def body(*refs):
        srcs, new_lands = refs[:n], refs[n:n + len(fresh)]
        outs = refs[n + len(fresh) + 1:]
        lands = list(srcs)
        for k, i in enumerate(fresh):
            lands[i] = new_lands[k]
        for ai, (_, mode) in enumerate(arrs):
            for src_ref, dst_ref, peer in _copy_plan(mode, srcs[ai], lands[ai]):
                pltpu.make_async_remote_copy(src_ref=src_ref, dst_ref=dst_ref, send_sem=outs[2 * ai],
                                             recv_sem=outs[2 * ai + 1], device_id=peer, device_id_type=MESH).start()
        outs[-1][...] = jnp.zeros(outs[-1].shape, F32)

    land_shapes = [(_land_shape(arrs[i][1], arrs[i][0].shape), arrs[i][0].dtype) for i in fresh]
    n_buf = n + len(fresh)
    out_shape = [pltpu.SemaphoreType.DMA(())] * (2 * n)
    out_shape += [pltpu.HBM(a.shape, a.dtype) for a, _ in arrs]
    out_shape += [pltpu.HBM(shape, dt) for shape, dt in land_shapes]
    out_shape.append(jax.ShapeDtypeStruct((8, LANES), F32))
    res = pl.pallas_call(
        body, name=name, out_shape=tuple(out_shape),
        in_specs=[HBM_SPEC] * n_buf + [ANY_SPEC],
        out_specs=tuple([SEM_SPEC] * (2 * n) + [HBM_SPEC] * n_buf + [pl.BlockSpec(memory_space=pltpu.VMEM)]),
        input_output_aliases={i: 2 * n + i for i in range(n_buf)},
        compiler_params=pltpu.CompilerParams(has_side_effects=SIDE_EFFECT),
    )(*[pltpu.with_memory_space_constraint(a, pltpu.HBM) for a, _ in arrs],
      *[pltpu.with_memory_space_constraint(lax.empty(shape, dt), pltpu.HBM) for shape, dt in land_shapes],
      after)
    sems, thru = res[:2 * n], res[2 * n:-1]
    land_of = {i: thru[n + k] for k, i in enumerate(fresh)}
    states, idx = [], 0
    for g, mode in zip(groups, modes):
        ids = range(idx, idx + len(g))
        idx += len(g)
        states.append(([sems[2 * i] for i in ids], [sems[2 * i + 1] for i in ids],
                       None if mode == "forward" else [thru[i] for i in ids],
                       [land_of.get(i, thru[i]) for i in ids], mode))
    return states, res[-1]


def _async_wait(state, after, name):
    sends, recvs, srcs, lands, mode = state
    g = len(lands)
    bufs = (list(srcs) if srcs is not None else []) + list(lands)
    nb = len(bufs)

    def body(*refs):
        l_refs, sems = refs[nb - g:nb], refs[nb:nb + 2 * g]
        for ai in range(g):
            moved = l_refs[ai].at[pl.ds(0, N_COPIES[mode])]
            cp = pltpu.make_async_remote_copy(src_ref=moved, dst_ref=moved, send_sem=sems[ai], recv_sem=sems[g + ai],
                                              device_id=_coords(), device_id_type=MESH)
            cp.wait_send()
            cp.wait_recv()

    res = pl.pallas_call(
        body, name=name,
        out_shape=tuple(pltpu.HBM(a.shape, a.dtype) for a in bufs),
        in_specs=[HBM_SPEC] * nb + [SEM_SPEC] * (2 * g) + [ANY_SPEC],
        out_specs=tuple([HBM_SPEC] * nb),
        input_output_aliases={i: i for i in range(nb)},
        compiler_params=pltpu.CompilerParams(has_side_effects=SIDE_EFFECT),
    )(*bufs, *sends, *recvs, after)
    return (list(res[:nb - g]) if srcs is not None else None), list(res[nb - g:])


def _add_sibling(mine, theirs, core):
    def body(c_ref, a_ref, b_ref, o_ref):
        o_ref[...] = (a_ref[...].astype(F32) + b_ref[...].astype(F32)).astype(o_ref.dtype)

    blk = (1,) + mine.shape[1:]
    return pl.pallas_call(
        body, name="add_sibling", out_shape=jax.ShapeDtypeStruct(theirs.shape, mine.dtype),
        grid_spec=pltpu.PrefetchScalarGridSpec(
            num_scalar_prefetch=1, grid=(theirs.shape[0],),
            in_specs=[pl.BlockSpec(blk, lambda q, c: (2 * q + c[0], 0, 0)), pl.BlockSpec(blk, lambda q, c: (q, 0, 0))],
            out_specs=pl.BlockSpec(blk, lambda q, c: (q, 0, 0))),
        compiler_params=_params(("parallel",)),
    )(core, mine, theirs)


def _with_own(land, own, me):
    return lax.dynamic_update_index_in_dim(land, own, me, 0)


IN_SPLITS = (512, 512, 512, 8, 768, 256, 32, 1024, 1024)


def _from_shards(g, fn, out_widths, name, own=None, slot=None):
    _, k, n = g.shape
    tr = min(k, 256)

    def body(*refs):
        if own is not None:
            s_ref, g_ref, own_ref = refs[:3]
            cols = [jnp.where(s_ref[0] == j, own_ref[...], g_ref[j]) for j in range(N_DEV)]
        else:
            g_ref = refs[0]
            cols = [g_ref[j] for j in range(N_DEV)]
        for o_ref, val in zip(refs[-len(out_widths):], fn(jnp.concatenate(cols, axis=1))):
            o_ref[...] = val

    in_specs = [pl.BlockSpec((N_DEV, tr, n), lambda i, *_: (0, i, 0))]
    out_spec = [pl.BlockSpec((tr, wd), lambda i, *_: (i, 0)) for wd in out_widths]
    out_shape = [jax.ShapeDtypeStruct((k, wd), g.dtype) for wd in out_widths]
    if own is None:
        return pl.pallas_call(body, name=name, grid=(k // tr,), in_specs=in_specs, out_specs=out_spec,
                              out_shape=out_shape, compiler_params=_params(("parallel",)))(g)
    in_specs.append(pl.BlockSpec((tr, n), lambda i, *_: (i, 0)))
    return pl.pallas_call(
        body, name=name, out_shape=out_shape, compiler_params=_params(("parallel",)),
        grid_spec=pltpu.PrefetchScalarGridSpec(num_scalar_prefetch=1, grid=(k // tr,), in_specs=in_specs, out_specs=out_spec),
    )(slot, g, own)


def _unshard_cols(g, own=None, slot=None):
    return _from_shards(g, lambda full: (full,), [N_DEV * g.shape[2]], "unshard_cols_%d" % g.shape[2], own, slot)[0]


FFN_T = D_FF // 2
FFN_SHARD = 2 * D_FF // N_DEV


def _unshard_ffn_in(g, own=None, slot=None):
    def pairs(full):
        parts = []
        for j in range(D_FF // FFN_T):
            parts += [full[:, j * FFN_T:(j + 1) * FFN_T], full[:, D_FF + j * FFN_T:D_FF + (j + 1) * FFN_T]]
        return (jnp.concatenate(parts, axis=1),)

    return _from_shards(g, pairs, [2 * D_FF], "unshard_ffn_in", own, slot)[0]


def _shard_ffn_in_t(wt):
    tc = 256

    def body(w_ref, o_ref):
        x = w_ref[...]
        nb = D_FF // FFN_T
        full = jnp.concatenate([x[(2 * j + half) * FFN_T:(2 * j + half + 1) * FFN_T]
                                for half in range(2) for j in range(nb)], axis=0)
        for j in range(N_DEV):
            o_ref[j] = full[j * FFN_SHARD:(j + 1) * FFN_SHARD]

    return pl.pallas_call(
        body, name="shard_ffn_in_t", grid=(D // tc,),
        in_specs=[pl.BlockSpec((2 * D_FF, tc), lambda i: (0, i))],
        out_specs=pl.BlockSpec((N_DEV, FFN_SHARD, tc), lambda i: (0, 0, i)),
        out_shape=jax.ShapeDtypeStruct((N_DEV, FFN_SHARD, D), wt.dtype),
        compiler_params=_params(("parallel",)),
    )(wt)


def _shard_cols(w):
    k, n = w.shape[0], w.shape[1] // N_DEV
    tr = min(k, 256)

    def body(w_ref, o_ref):
        full = w_ref[...]
        for j in range(N_DEV):
            o_ref[j] = full[:, j * n:(j + 1) * n]

    return pl.pallas_call(
        body, name="shard_cols_%d" % n, grid=(k // tr,),
        in_specs=[pl.BlockSpec((tr, N_DEV * n), lambda i: (i, 0))],
        out_specs=pl.BlockSpec((N_DEV, tr, n), lambda i: (0, i, 0)),
        out_shape=jax.ShapeDtypeStruct((N_DEV, k, n), w.dtype),
        compiler_params=_params(("parallel",)),
    )(w)


IN_OFFS = tuple(sum(IN_SPLITS[:i]) for i in range(len(IN_SPLITS) + 1))
IN_SHARD = IN_OFFS[-1] // N_DEV
REGROUP_ROWS = 256
MISC_AT = Q_LORA + KV_LORA + 2 * D
A_COLS = MISC_AT + LANES
A_TILE = A_COLS
B_COLS = 3 * HEADS * HEAD_DIM
MISC_BLOCK = MISC_AT // LANES
KR_AT = 64


def _w_in_regroup(g, own=None, slot=None):
    def groups(full):
        fq, fk, fv, wf, cq, ckv, kr, gf, gm = [full[:, IN_OFFS[i]:IN_OFFS[i + 1]] for i in range(9)]
        rows = full.shape[0]
        gap = jnp.zeros((rows, KR_AT - HEADS), BF16)
        tail = jnp.zeros((rows, LANES - KR_AT - ROPE_DIM), BF16)
        return jnp.concatenate([cq, ckv, gf, gm, wf, gap, kr, tail], axis=1), jnp.concatenate([fq, fk, fv], axis=1)

    return _from_shards(g, groups, [A_COLS, B_COLS], "w_in_regroup", own, slot)


def _w_in_ungroup(da, db_):
    def body(a_ref, b_ref, o_ref):
        a = a_ref[...]
        lora = Q_LORA + KV_LORA
        full = jnp.concatenate([b_ref[...], a[:, MISC_AT:MISC_AT + HEADS], a[:, :lora],
                                a[:, MISC_AT + KR_AT:MISC_AT + KR_AT + ROPE_DIM], a[:, lora:MISC_AT]], axis=1)
        for j in range(N_DEV):
            o_ref[j] = full[:, j * IN_SHARD:(j + 1) * IN_SHARD]

    tr = REGROUP_ROWS
    return pl.pallas_call(
        body, name="w_in_ungroup", grid=(D // tr,),
        in_specs=[pl.BlockSpec((tr, A_COLS), lambda i: (i, 0)), pl.BlockSpec((tr, B_COLS), lambda i: (i, 0))],
        out_specs=pl.BlockSpec((N_DEV, tr, IN_SHARD), lambda i: (0, i, 0)),
        out_shape=jax.ShapeDtypeStruct((N_DEV, D, IN_SHARD), BF16),
        compiler_params=_params(("parallel",)),
    )(da, db_)


def _prepare_weights(g, own=None, slot=None):
    w = {}
    if own is not None:
        small = ("w_uq", "w_ukv", "w_out", "w_ffn_out")
        g = {n: (_with_own(a, own[n], slot[0]) if n in small else a) for n, a in g.items()}
    pick = (lambda n: (own[n], slot)) if own is not None else (lambda n: (None, None))
    if "w_in" in g:
        w["w_a"], w["w_b"] = _w_in_regroup(g["w_in"], *pick("w_in"))
    if "w_uq" in g:
        w_uq = g["w_uq"].reshape(Q_LORA, HEADS, 96)
        w["w_uq"] = jnp.pad(w_uq, ((0, 0), (0, 0), (0, 32))).reshape(Q_LORA, HEADS * LANES)
        ukv = g["w_ukv"]
        w["w_k"] = jnp.transpose(jnp.pad(ukv[:, :, :64], ((0, 0), (0, 0), (0, 64))), (1, 0, 2)).reshape(KV_LORA, HEADS * LANES)
        w["w_v"] = jnp.transpose(ukv[:, :, 64:], (1, 0, 2)).reshape(KV_LORA, HEADS * HEAD_DIM)
    if "w_out" in g:
        w["w_pf"] = _unshard_cols(g["w_proj_fox"], *pick("w_proj_fox"))
        w["w_pm"] = _unshard_cols(g["w_proj_mla"], *pick("w_proj_mla"))
        w["w_out"] = g["w_out"].reshape(D, D)
    if "w_ffn_in" in g:
        w["w_ffn_in"] = _unshard_ffn_in(g["w_ffn_in"], *pick("w_ffn_in"))
        w["w_ffn_out"] = g["w_ffn_out"].reshape(D_FF, D)
    return w


def _shard_grads(dw):
    out = {}
    if "w_a" in dw:
        out["w_in"] = _w_in_ungroup(dw["w_a"], dw["w_b"])
    if "w_uq" in dw:
        w_uq = dw["w_uq"].reshape(Q_LORA, HEADS, LANES)[:, :, :96].reshape(Q_LORA, Q_LORA)
        out["w_uq"] = w_uq.reshape(N_DEV, Q_LORA // N_DEV, Q_LORA)
        k_part = dw["w_k"].reshape(KV_LORA, HEADS, LANES)[:, :, :64]
        v_part = dw["w_v"].reshape(KV_LORA, HEADS, HEAD_DIM)
        out["w_ukv"] = jnp.transpose(jnp.concatenate([k_part, v_part], axis=2), (1, 0, 2))
    if "w_out" in dw:
        out["w_proj_fox"] = _shard_cols(dw["w_pf"])
        out["w_proj_mla"] = _shard_cols(dw["w_pm"])
        out["w_out"] = dw["w_out"].reshape(N_DEV, D // N_DEV, D)
    if "w_ffn_in" in dw:
        out["w_ffn_in"] = _shard_ffn_in_t(dw["w_ffn_in"])
        out["w_ffn_out"] = dw["w_ffn_out"].reshape(N_DEV, D_FF // N_DEV, D)
    return out


def _rope_table_set(pos):
    inv_freq = 1.0 / (ROPE_THETA ** (jnp.arange(0, ROPE_DIM, 2, dtype=F32) / ROPE_DIM))
    invf = jnp.concatenate([jnp.zeros((64,), F32), inv_freq, inv_freq, jnp.zeros((32,), F32)]).reshape(1, LANES)
    return _rope_tables(pos, invf)


def _fwd_bwd(x, pos, mod, target, w, vec, wts, send, relay, tables=None):
    shift_mix, scale_mix, gate_mix, shift_ffn, scale_ffn, gate_ffn = [mod[:, i * D:(i + 1) * D] for i in range(6)]
    g_pre_mix, g_post_mix, g_pre_ffn, g_post_ffn = vec["g_pre_mix"], vec["g_post_mix"], vec["g_pre_ffn"], vec["g_post_ffn"]
    g_q, g_kv = vec["g_q_lora"], vec["g_kv_lora"]

    ct, sa, sb = tables if tables is not None else _rope_table_set(pos)

    def pre1(xv, g, sc, sh):
        return (xv * _rstd(xv) * g) * (1.0 + sc) + sh
    proj_a, h = _mm_epi(x, w["w_a"], "nn", A_TILE, lambda r: ((r,), ()), "in_proj_a", 512, outs=[(A_COLS, A_TILE, F32)],
                        pro=(pre1, [g_pre_mix, scale_mix, shift_mix], 0))
    qkv = _mm(h, w["w_b"], "nn", BF16, "in_proj_b")

    def lora_norm(cv, g):
        return cv * _rstd(cv) * g
    w = {**w, **wts("lora", qkv)}
    tables = [(ct, LANES), (sa, LANES), (sb, LANES)]

    def rope_q(qv, c_, a_, b_):
        return (jnp.concatenate([_rope(qv[:, hd * LANES:(hd + 1) * LANES], c_, a_, b_) for hd in range(HEADS)], axis=1),), ()
    q_m, cqn = _mm_epi(proj_a, w["w_uq"], "nn", D, rope_q, "mla_uq", 512, rows=tables, outs=[(D, D, BF16)],
                       pro=(lora_norm, [g_q], 0))

    def rope_k(kv, misc, c_, a_, b_):
        lane = lax.broadcasted_iota(jnp.int32, (1, LANES), 1)
        kpe = jnp.where((lane >= 64) & (lane < 96), _rope(misc, c_, a_, b_), 0.0)
        return (jnp.concatenate([kv[:, hd * LANES:(hd + 1) * LANES] + kpe for hd in range(HEADS)], axis=1),), ()
    k_m, ckvn = _mm_epi(proj_a, w["w_k"], "nn", D, rope_k, "mla_uk", 512, rows=[(proj_a, LANES, MISC_BLOCK)] + tables,
                        outs=[(D, D, BF16)], pro=(lora_norm, [g_kv], Q_LORA // KV_LORA))
    v_m = _mm(ckvn, w["w_v"], "nn", BF16, "mla_uv")

    bf = jnp.transpose(vec["b_forget"])
    zt, neg_f = _fox_gates(proj_a, MISC_BLOCK, bf)
    bias = neg_f.reshape(HEADS, N_ATT, 1, ATT_T)
    o_b, lse_b = _attn_fwd(q_m, 0, k_m, 0, v_m, 0, 2 * LANES, 1.0 / math.sqrt(64 + ROPE_DIM), None, "mla_attn")
    o_a, lse_a = _attn_fwd(qkv, 0, qkv, 4, qkv, 8, LANES, 1.0 / math.sqrt(HEAD_DIM), bias, "fox_attn",
                           dep=wts("relay_proj", o_b)["tok"])

    w = {**w, **wts("proj", o_a)}
    pa = _mm(o_a, w["w_pf"], "nn", BF16, "proj_fox", dep=wts("relay_ffn", o_a)["tok"])

    def merge(pb_, gf, gm, pa_):
        return (_sigmoid(gf) * pa_ + _sigmoid(gm) * pb_, pb_), ()
    merged, pb = _mm_epi(o_b, w["w_pm"], "nn", 512, merge, "proj_mla", 1024,
                         rows=[(proj_a, 512, 2), (proj_a, 512, 4), (pa, 512)], outs=[(D, 512, BF16), (D, 512, BF16)])
    def post1(yv, xv, gate, gpost, gpre, sc, sh):
        x1 = xv + gate * (yv * _rstd(yv) * gpost)
        return (x1, (x1 * _rstd(x1) * gpre) * (1.0 + sc) + sh, yv), ()
    x1, h2, y = _mm_epi(merged, w["w_out"], "nn", D, post1, "out_proj", 512, rows=[(x, D)],
                        vecs=[gate_mix, g_post_mix, g_pre_ffn, scale_ffn, shift_ffn],
                        outs=[(D, D, F32), (D, D, BF16), (D, D, F32)])
    w = {**w, **wts("ffn", h2)}

    def swiglu(r):
        g, u = r[:, :FFN_T], r[:, FFN_T:]
        return (g * _sigmoid(g) * u, r), ()
    act, gu = _mm_epi(h2, w["w_ffn_in"], "nn", 2 * FFN_T, swiglu, "ffn_in", 512,
                      outs=[(D_FF, FFN_T, BF16), (2 * D_FF, 2 * FFN_T, BF16)])

    def head(y2v, x1v, tv, gate, gpost):
        r = _rstd(y2v)
        yn = y2v * r
        n2 = yn * gpost
        err = (x1v + gate * n2) - tv
        dx2 = err * (1.0 / D)
        dn2 = dx2 * gate
        dy2 = _norm_bwd(dn2 * gpost, yn, r)
        return (dx2, dy2), (_colsum(err * err), _colsum(dx2 * n2), _colsum(dn2 * yn))
    dx2, dy2, err_cols, d_gate_ffn, d_g_post_ffn = _mm_epi(
        act, w["w_ffn_out"], "nn", D, head, "ffn_out", 512, rows=[(x1, D), (target, D)], vecs=[gate_ffn, g_post_ffn],
        outs=[(D, D, F32), (D, D, BF16)], sums=[D, D, D])

    def swiglu_bwd(da, guv):
        g, u = guv[:, :FFN_T].astype(F32), guv[:, FFN_T:].astype(F32)
        sg = _sigmoid(g)
        return (jnp.concatenate([da * u * (sg * (1.0 + g * (1.0 - sg))), da * (g * sg)], axis=1),), ()
    (dgu,) = _mm_epi(dy2, w["w_ffn_out"], "nt", FFN_T, swiglu_bwd, "ffn_out_dx", 512, rows=[(gu, 2 * FFN_T)],
                     outs=[(2 * D_FF, 2 * FFN_T, BF16)])
    dw = {"w_ffn_out": _mm(act, dy2, "tn", BF16, "ffn_out_dw")}
    dw["w_ffn_in"] = _mm(dgu, h2, "tn", BF16, "ffn_in_dw")
    tok = send({n: dw.pop(n) for n in ("w_ffn_in", "w_ffn_out")})

    def mid(dh, x1v, dx2v, yv, gpre, sc, gate, gpost):
        r2 = _rstd(x1v)
        x1n = x1v * r2
        t = dh * x1n
        dx1 = dx2v + _norm_bwd(dh * (gpre * (1.0 + sc)), x1n, r2)
        ry = _rstd(yv)
        yn = yv * ry
        dn1 = dx1 * gate
        dy = _norm_bwd(dn1 * gpost, yn, ry)
        sums = (_colsum(dh), _colsum(t) * gpre, _colsum(t) * (1.0 + sc), _colsum(dx1 * (yn * gpost)), _colsum(dn1 * yn))
        return (dx1, dy), sums
    dx1, dy, d_shift_ffn, d_scale_ffn, d_g_pre_ffn, d_gate_mix, d_g_post_mix = _mm_epi(
        dgu, w["w_ffn_in"], "nt", D, mid, "ffn_in_dx", 512, rows=[(x1, D), (dx2, D), (y, D)],
        vecs=[g_pre_ffn, scale_ffn, gate_mix, g_post_mix], outs=[(D, D, F32), (D, D, BF16)], sums=[D] * 5, dep=tok)

    dw["w_out"] = _mm(merged, dy, "tn", BF16, "out_proj_dw")

    def merge_bwd(dm, gf, gm, pa_, pb_):
        sf, sm = _sigmoid(gf), _sigmoid(gm)
        return (dm * sf, dm * sm, dm * pa_ * (sf * (1.0 - sf)), dm * pb_ * (sm * (1.0 - sm))), ()
    dpa, dpb, dgf, dgm = _mm_epi(dy, w["w_out"], "nt", 512, merge_bwd, "out_proj_dx", 1024,
                                 rows=[(proj_a, 512, 2), (proj_a, 512, 4), (pa, 512), (pb, 512)],
                                 outs=[(D, 512, BF16)] * 4)
    do_a = _mm(dpa, w["w_pf"], "nt", BF16, "proj_fox_dx")
    do_b = _mm(dpb, w["w_pm"], "nt", BF16, "proj_mla_dx")
    dw["w_pf"] = _mm(o_a, dpa, "tn", BF16, "proj_fox_dw")
    dw["w_pm"] = _mm(o_b, dpb, "tn", BF16, "proj_mla_dw")
    tok = send({n: dw.pop(n) for n in ("w_out", "w_pf", "w_pm")})

    sc_a, sc_b = 1.0 / math.sqrt(HEAD_DIM), 1.0 / math.sqrt(64 + ROPE_DIM)
    dq_a, dk_a, dv_a, dbias = _attn_grad(qkv, 0, qkv, 4, qkv, 8, do_a, lse_a, LANES, sc_a, bias, BF16, "fox_attn_bwd",
                                         dep=tok)
    dq_m, dk_m, dv_m = _attn_grad(q_m, 0, k_m, 0, v_m, 0, do_b, lse_b, 2 * LANES, sc_b, None, F32, "mla_attn_bwd")

    def mla_rope_bwd(dq, dk, c_, a_, b_):
        lane = lax.broadcasted_iota(jnp.int32, (1, LANES), 1)
        dqs = [_rope_t(dq[:, hd * LANES:(hd + 1) * LANES], c_, a_, b_) for hd in range(HEADS)]
        dkpe = dk[:, 0:LANES]
        for hd in range(1, HEADS):
            dkpe = dkpe + dk[:, hd * LANES:(hd + 1) * LANES]
        dkpe = jnp.where((lane >= 64) & (lane < 96), dkpe, 0.0)
        dkr = jnp.where((lane >= 64) & (lane < 96), _rope_t(dkpe, c_, a_, b_), 0.0)
        return (jnp.concatenate(dqs, axis=1), dk, dkr), ()
    dqb, dkb, dkr = _rowwise(mla_rope_bwd, [(dq_m, D, 0), (dk_m, D, 0), (ct, LANES, 0), (sa, LANES, 0), (sb, LANES, 0)],
                             [], [(D, BF16), (D, BF16), (LANES, F32)], [], "mla_rope_bwd")
    def lora_q_bwd(dq, cq, gq):
        rq = _rstd(cq)
        cqh = cq * rq
        return (_norm_bwd(dq * gq, cqh, rq),), (_colsum(dq * cqh),)
    dcq, d_g_q = _mm_epi(dqb, w["w_uq"], "nt", Q_LORA, lora_q_bwd, "mla_uq_dx", 512, rows=[(proj_a, Q_LORA, 0)],
                         vecs=[g_q], outs=[(Q_LORA, Q_LORA, BF16)], sums=[Q_LORA])

    def lora_kv_bwd(dv_part, dk_part, ckv, gkv):
        dkv = dv_part + dk_part
        rk = _rstd(ckv)
        ckh = ckv * rk
        return (_norm_bwd(dkv * gkv, ckh, rk),), (_colsum(dkv * ckh),)
    dckv, d_g_kv = _mm_epi(dv_m, w["w_v"], "nt", KV_LORA, lora_kv_bwd, "mla_uv_dx", 1024,
                           rows=[(_mm(dkb, w["w_k"], "nt", F32, "mla_uk_dx"), KV_LORA), (proj_a, KV_LORA, 3)],
                           vecs=[g_kv], outs=[(KV_LORA, KV_LORA, BF16)], sums=[KV_LORA])

    dzt, d_bf = _fox_gates_bwd(dbias.reshape(HEADS, S), zt, bf)
    dmisc = (dkr + jnp.pad(jnp.transpose(dzt), ((0, 0), (0, LANES - HEADS)))).astype(BF16)
    dproj_a = [dcq, dckv, dgf, dgm, dmisc]
    dqkv = [dq_a, dk_a, dv_a]
    dw["w_a"] = _mm_cat(h, dproj_a, "tn", BF16, "in_proj_a_dw")
    dw["w_b"] = _mm_cat(h, dqkv, "tn", BF16, "in_proj_b_dw")
    tok = send(dw, True)
    dh_a = _mm_cat(dproj_a, w["w_a"], "nt", F32, "in_proj_a_dx", dep=tok)
    tok = relay(dh_a)
    tok = send({"w_uq": _mm(cqn, dqb, "tn", BF16, "mla_uq_dw", dep=tok),
                "w_k": _mm(ckvn, dkb, "tn", BF16, "mla_uk_dw", dep=tok),
                "w_v": _mm(ckvn, dv_m, "tn", BF16, "mla_uv_dw", dep=tok)}, late=True)

    def first(dh_b, dh_a, xv, dx1v, gpre, sc):
        dhv = dh_b + dh_a
        r = _rstd(xv)
        xn = xv * r
        t = dhv * xn
        dx = dx1v + _norm_bwd(dhv * (gpre * (1.0 + sc)), xn, r)
        return (dx,), (_colsum(dhv), _colsum(t) * gpre, _colsum(t) * (1.0 + sc))
    grad_x, d_shift_mix, d_scale_mix, d_g_pre_mix = _mm_epi(
        dqkv, w["w_b"], "nt", D, first, "in_proj_b_dx", 512,
        rows=[(dh_a, D), (x, D), (dx1, D)],
        vecs=[g_pre_mix, scale_mix], outs=[(D, D, F32)], sums=[D] * 3, dep=tok)

    dmod = jnp.concatenate([d_shift_mix, d_scale_mix, d_gate_mix, d_shift_ffn, d_scale_ffn, d_gate_ffn], axis=1)
    small = dict(dmod=dmod, g_pre_mix=d_g_pre_mix, g_post_mix=d_g_post_mix, g_pre_ffn=d_g_pre_ffn,
                 g_post_ffn=d_g_post_ffn, g_q_lora=d_g_q, g_kv_lora=d_g_kv,
                 b_forget=jnp.pad(jnp.transpose(d_bf), ((0, 0), (0, LANES - HEADS))), err=err_cols)
    return grad_x, small


SMALL_ORDER = ("dmod", "g_pre_mix", "g_post_mix", "g_pre_ffn", "g_post_ffn", "g_q_lora", "g_kv_lora", "b_forget", "err")
SMALL_PARAM = {"dmod": "b_ada"}
MATRICES = ("w_in", "w_uq", "w_ukv", "w_proj_fox", "w_proj_mla", "w_out", "w_ffn_in", "w_ffn_out")
WEIGHTS = ("w_ada", "b_ada", "g_pre_mix", "g_post_mix", "g_pre_ffn", "g_post_ffn", "w_in", "b_forget", "g_q_lora",
           "w_uq", "g_kv_lora", "w_ukv", "w_proj_fox", "w_proj_mla", "w_out", "w_ffn_in", "w_ffn_out")


def kernel(x, c, positions, w_ada, b_ada, g_pre_mix, g_post_mix, g_pre_ffn, g_post_ffn, w_in, b_forget, g_q_lora, w_uq, g_kv_lora, w_ukv, w_proj_fox, w_proj_mla, w_out, w_ffn_in, w_ffn_out, loss_target, m_w_ada, m_b_ada, m_g_pre_mix, m_g_post_mix, m_g_pre_ffn, m_g_post_ffn, m_w_in, m_b_forget, m_g_q_lora, m_w_uq, m_g_kv_lora, m_w_ukv, m_w_proj_fox, m_w_proj_mla, m_w_out, m_w_ffn_in, m_w_ffn_out, v_w_ada, v_b_ada, v_g_pre_mix, v_g_post_mix, v_g_pre_ffn, v_g_post_ffn, v_w_in, v_b_forget, v_g_q_lora, v_w_uq, v_g_kv_lora, v_w_ukv, v_w_proj_fox, v_w_proj_mla, v_w_out, v_w_ffn_in, v_w_ffn_out):
    prm = dict(w_ada=w_ada, b_ada=b_ada, g_pre_mix=g_pre_mix, g_post_mix=g_post_mix, g_pre_ffn=g_pre_ffn,
               g_post_ffn=g_post_ffn, w_in=w_in, b_forget=b_forget, g_q_lora=g_q_lora, w_uq=w_uq, g_kv_lora=g_kv_lora,
               w_ukv=w_ukv, w_proj_fox=w_proj_fox, w_proj_mla=w_proj_mla, w_out=w_out, w_ffn_in=w_ffn_in, w_ffn_out=w_ffn_out)
    mom = dict(w_ada=m_w_ada, b_ada=m_b_ada, g_pre_mix=m_g_pre_mix, g_post_mix=m_g_post_mix, g_pre_ffn=m_g_pre_ffn,
               g_post_ffn=m_g_post_ffn, w_in=m_w_in, b_forget=m_b_forget, g_q_lora=m_g_q_lora, w_uq=m_w_uq,
               g_kv_lora=m_g_kv_lora, w_ukv=m_w_ukv, w_proj_fox=m_w_proj_fox, w_proj_mla=m_w_proj_mla, w_out=m_w_out,
               w_ffn_in=m_w_ffn_in, w_ffn_out=m_w_ffn_out)
    var = dict(w_ada=v_w_ada, b_ada=v_b_ada, g_pre_mix=v_g_pre_mix, g_post_mix=v_g_post_mix, g_pre_ffn=v_g_pre_ffn,
               g_post_ffn=v_g_post_ffn, w_in=v_w_in, b_forget=v_b_forget, g_q_lora=v_g_q_lora, w_uq=v_w_uq,
               g_kv_lora=v_g_kv_lora, w_ukv=v_w_ukv, w_proj_fox=v_w_proj_fox, w_proj_mla=v_w_proj_mla, w_out=v_w_out,
               w_ffn_in=v_w_ffn_in, w_ffn_out=v_w_ffn_out)
    me = _flat(*_coords())
    slot = jnp.reshape(me, (1,)).astype(jnp.int32)

    own = {n: prm[n][0].astype(BF16) for n in MATRICES}
    no_dep = jnp.zeros((8, LANES), F32)
    (st_c, st_in), tok = _async_start([[c], [own["w_in"]]], ["gather", "spread"], no_dep, "gather_in_start")
    (c_own,), (c_land,) = _async_wait(st_c, tok, "gather_c_wait")
    c_all = _with_own(c_land, c_own, me).reshape(N_DEV, D)
    ada_cols = w_ada.shape[2]
    b_cols = lax.dynamic_slice(b_ada, (0, me * ada_cols), (1, ada_cols))
    pos = positions.astype(F32).reshape(S, 1)
    tables = _rope_table_set(pos)
    mod_cols, silu_c = _mod_part(c_all, w_ada[0], b_cols,
                                 before=[own[n] for n in MATRICES if n != "w_in"] + list(tables))
    (mod_all,) = _all_gather([mod_cols], "gather_mod")

    (w_in_own,), (w_in_land,) = _async_wait(st_in, mod_all, "gather_in_wait")
    (st_in,), tok = _async_start([[w_in_land]], "forward", no_dep, "gather_in_forward")
    _, (w_in_land,) = _async_wait(st_in, tok, "gather_in_forward_wait")
    w = _prepare_weights({"w_in": w_in_land}, {"w_in": w_in_own}, slot)
    later = dict(lora=("w_uq", "w_ukv"), proj=("w_proj_fox", "w_proj_mla", "w_out"), ffn=("w_ffn_in", "w_ffn_out"))
    states, tok = _async_start([[own[n] for n in names] for names in later.values()], ["gather", "spread", "spread"],
                               w["w_b"], "gather_rest_start")
    gather_state = dict(zip(later, states))
    own_thru = {}

    def wts(group, after):
        if group.startswith("relay_"):
            name = group[len("relay_"):]
            own_thru[name], lands = _async_wait(gather_state[name], after, "gather_" + name + "_wait")
            (gather_state[name],), t = _async_start([lands], "forward", no_dep, "gather_" + name + "_forward")
            return {"tok": t}
        srcs, lands = _async_wait(gather_state[group], after, "gather_" + group + "_landed")
        srcs = own_thru.get(group, srcs)
        return _prepare_weights(dict(zip(later[group], lands)), dict(zip(later[group], srcs)), slot)

    sent, late_sent, last = [], [], {}

    def send(grads, final=False, late=False):
        shards = _shard_grads(grads)
        names = list(shards)
        (state,), t = _async_start([[shards[n] for n in names]], "pair" if final else "exchange", no_dep,
                                   "exchange_" + names[0] + "_start")
        if final:
            last.update(names=names, state=state)
        else:
            (late_sent if late else sent).append((names, state))
        return t

    def relay(after):
        srcs, lands = _async_wait(last["state"], after, "exchange_pair_wait")
        core = jnp.reshape(lax.axis_index("c"), (1,)).astype(jnp.int32)
        sums = [_add_sibling(src, land, core) for src, land in zip(srcs, lands)]
        (last["state"],), t = _async_start([sums], "chips", no_dep, "exchange_chips_start")
        return t

    mod = lax.dynamic_index_in_dim(mod_all, me, axis=1, keepdims=False).reshape(1, 6 * D) + tok[0, 0]

    vec = dict(g_pre_mix=g_pre_mix, g_post_mix=g_post_mix, g_pre_ffn=g_pre_ffn, g_post_ffn=g_post_ffn,
               g_q_lora=g_q_lora, g_kv_lora=g_kv_lora, b_forget=b_forget)
    grad_x, small = _fwd_bwd(x[0], pos, mod, loss_target[0], w, vec, wts, send, relay, tables)

    bundle = jnp.concatenate([small[n] for n in SMALL_ORDER], axis=1)
    (small_state,), tok = _async_start([[bundle]], "gather", jnp.zeros((8, LANES), F32), "gather_small_start")

    out = {}
    swap = lambda a: jnp.swapaxes(a, -1, -2)

    def update(n, land, src, sl):
        if n != "w_ffn_in":
            out[n] = _adamw(prm[n][0], mom[n][0], var[n][0], land, "adamw_" + n, src, sl)
            return out[n][0]
        res = _adamw(swap(prm[n][0]), swap(mom[n][0]), swap(var[n][0]), land, "adamw_" + n, src, sl)
        out[n] = tuple(swap(t) for t in res)
        return res[0]

    after = tok
    for names, state in sent:
        srcs, lands = _async_wait(state, after, "exchange_" + names[0] + "_wait")
        for n, src, land in zip(names, srcs, lands):
            after = update(n, land, src, slot)
    srcs, lands = _async_wait(last["state"], after, "exchange_chips_wait")
    for n, src, land in zip(last["names"], srcs, lands):
        after = update(n, land, src, slot // 2)
    for names, state in late_sent:
        srcs, lands = _async_wait(state, after, "exchange_" + names[0] + "_wait")
        for n, src, land in zip(names, srcs, lands):
            after = update(n, land, src, slot)

    (own_bundle,), (bundle_all,) = _async_wait(small_state, after, "gather_small_wait")
    bundle_all = _with_own(bundle_all, own_bundle, me)
    dmod_all = bundle_all[:, 0, :6 * D]
    dm_cols = lax.dynamic_slice(dmod_all, (0, me * ada_cols), (N_DEV, ada_cols))
    out["w_ada"] = _adamw_w_ada(w_ada[0], m_w_ada[0], v_w_ada[0], jnp.transpose(silu_c), dm_cols)

    offsets, off = {}, 0
    for n in SMALL_ORDER:
        offsets[n] = off
        off += small[n].shape[1]
    names = [SMALL_PARAM.get(n, n) for n in SMALL_ORDER if n != "err"]
    results, err = _adamw_rows(bundle_all, [offsets[n] for n in SMALL_ORDER if n != "err"],
                               [prm[n] for n in names], [mom[n] for n in names], [var[n] for n in names],
                               offsets["err"], D)
    out.update(zip(names, results))
    loss = 0.5 * jnp.sum(err) / D

    res = [loss, grad_x[None]]
    for kind in range(4):
        for n in WEIGHTS:
            t = out[n][kind]
            res.append(t[None] if prm[n].ndim == 3 else t)
    return tuple(res)
```

```python
import functools
import math

import jax
import jax.numpy as jnp
from jax import lax
from jax.experimental import pallas as pl
from jax.experimental.pallas import tpu as pltpu

F32 = jnp.float32
BF16 = jnp.bfloat16

N_DEV = 8
S = 2048
D = 1024
D_FF = 2816
HEADS = 8
HEAD_DIM = 64
Q_LORA = 768
KV_LORA = 256
ROPE_DIM = 32
ROPE_THETA = 10000.0
NORM_EPS = 1e-6
LANES = 128
VMEM_LIMIT = 56 * 1024 * 1024

ADAM_LR = 0.001
ADAM_B1 = 0.9
ADAM_B2 = 0.999
ADAM_EPS = 1e-08
ADAM_WD = 0.01
ADAM_STEP = 10

ATT_T = 256
LOG2E = 1.4426950408889634
N_ATT = S // ATT_T

NN = (((1,), (0,)), ((), ()))
NT = (((1,), (1,)), ((), ()))
TN = (((0,), (0,)), ((), ()))
MESH = pl.DeviceIdType.MESH


def _params(sem=None):
    return pltpu.CompilerParams(dimension_semantics=sem, vmem_limit_bytes=VMEM_LIMIT)


def _pick(n, cap):
    best = None
    for t in range(LANES, cap + 1, LANES):
        if n % t == 0:
            best = t
    return best if best is not None else n


def _mm(a, b, mode, out_dtype, name, acc=None, dep=None):
    if mode == "nn":
        (m, k), (k2, n), dn = a.shape, b.shape, NN
    elif mode == "nt":
        (m, k), (n, k2), dn = a.shape, b.shape, NT
    else:
        (k, m), (k2, n), dn = a.shape, b.shape, TN
    assert k == k2, (a.shape, b.shape, mode)
    tn = _pick(n, 1024)
    tm = _pick(m, 1536)
    osz = jnp.dtype(out_dtype).itemsize

    def need(tm_):
        blk = tm_ * k * 2 + tn * k * 2 + tm_ * tn * osz + (tm_ * tn * 4 if acc is not None else 0)
        return 2 * blk + tm_ * tn * 4
    while need(tm) > 36 * 1024 * 1024 and tm % 256 == 0:
        tm //= 2

    def body(*refs):
        a_ref, b_ref, o_ref = refs[0], refs[1], refs[-1]
        r = lax.dot_general(a_ref[...], b_ref[...], dn, preferred_element_type=F32)
        if acc is not None:
            r = r + refs[2][...]
        o_ref[...] = r.astype(o_ref.dtype)

    if mode == "tn":
        a_spec = pl.BlockSpec((k, tm), lambda i, j: (0, i))
    else:
        a_spec = pl.BlockSpec((tm, k), lambda i, j: (i, 0))
    if mode == "nt":
        b_spec = pl.BlockSpec((tn, k), lambda i, j: (j, 0))
    else:
        b_spec = pl.BlockSpec((k, tn), lambda i, j: (0, j))
    o_spec = pl.BlockSpec((tm, tn), lambda i, j: (i, j))
    in_specs = [a_spec, b_spec] + ([o_spec] if acc is not None else [])
    in_specs += [pl.BlockSpec(memory_space=pl.ANY)] if dep is not None else []
    args = (a, b) + ((acc,) if acc is not None else ()) + ((dep,) if dep is not None else ())
    return pl.pallas_call(
        body, name=name, grid=(m // tm, n // tn),
        in_specs=in_specs, out_specs=o_spec,
        out_shape=jax.ShapeDtypeStruct((m, n), out_dtype),
        compiler_params=_params(("parallel", "parallel")),
    )(*args)


def _offsets(widths):
    return [sum(widths[:p]) for p in range(len(widths))]


def _mm_cat(a, b, mode, out_dtype, name, dep=None):
    pieces = a if mode == "nt" else b
    widths = [p.shape[1] for p in pieces]
    offs = _offsets(widths)
    assert all(w_ % LANES == 0 for w_ in widths)
    resident = dict(pipeline_mode=pl.Buffered(1))
    if mode == "nt":
        m, (n, k) = pieces[0].shape[0], b.shape
        assert k == sum(widths)
        tm = _pick(m, 512)

        def body(*refs):
            b_ref, o_ref = refs[len(pieces)], refs[-1]
            r = None
            for p_ref, off, w_ in zip(refs, offs, widths):
                t = lax.dot_general(p_ref[...], b_ref[:, off:off + w_], NT, preferred_element_type=F32)
                r = t if r is None else r + t
            o_ref[...] = r.astype(o_ref.dtype)
        in_specs = [pl.BlockSpec((tm, w_), lambda i: (i, 0)) for w_ in widths]
        in_specs.append(pl.BlockSpec((n, k), lambda i: (0, 0), **resident))
        args = (*pieces, b)
    else:
        assert mode == "tn"
        (k, m), n = a.shape, sum(widths)
        tm = _pick(m, 512)

        def body(*refs):
            a_ref, o_ref = refs[0], refs[-1]
            for p_ref, off, w_ in zip(refs[1:], offs, widths):
                o_ref[:, off:off + w_] = lax.dot_general(
                    a_ref[...], p_ref[...], TN, preferred_element_type=F32).astype(o_ref.dtype)
        in_specs = [pl.BlockSpec((k, tm), lambda i: (0, i))]
        in_specs += [pl.BlockSpec((k, w_), lambda i: (0, 0), **resident) for w_ in widths]
        args = (a, *pieces)
    if dep is not None:
        in_specs.append(pl.BlockSpec(memory_space=pl.ANY))
        args += (dep,)
    return pl.pallas_call(
        body, name=name, grid=(m // tm,),
        in_specs=in_specs, out_specs=pl.BlockSpec((tm, n), lambda i: (i, 0)),
        out_shape=jax.ShapeDtypeStruct((m, n), out_dtype),
        compiler_params=_params(("parallel",)),
    )(*args)


def _mm_epi(a, b, mode, tnb, epi, name, tm, rows=(), vecs=(), outs=(), sums=(), pro=None, dep=None):
    pieces = list(a) if isinstance(a, (list, tuple)) else [a]
    assert len(pieces) == 1 or pro is None
    widths = [p.shape[1] for p in pieces]
    offs = _offsets(widths)
    m = pieces[0].shape[0]
    k, nb = (b.shape if mode == "nn" else b.shape[::-1])
    dn = NN if mode == "nn" else NT
    pro_fn, pro_vecs, a_off = pro if pro is not None else (None, (), 0)
    n_in = 2 + len(rows) + len(vecs)
    n_all = n_in + len(pro_vecs)
    sub = min(tm, 256)

    def body(*refs):
        a_refs, refs = refs[:len(pieces)], refs[len(pieces) - 1:]
        if pro is not None:
            a_out, a_scr = refs[-2:]
            refs = refs[:-2]

            @pl.when(pl.program_id(1) == 0)
            def _():
                a_scr[...] = pro_fn(refs[0][...], *[x[...] for x in refs[n_in:n_all]]).astype(BF16)
                a_out[...] = a_scr[...]
            a_ref = a_scr
        else:
            a_ref = refs[0]
        n_skip = n_all + (dep is not None)
        o_refs = refs[n_skip:n_skip + len(outs)]
        s_refs = refs[n_skip + len(outs):]
        if sums:
            @pl.when((pl.program_id(0) == 0) & (pl.program_id(1) == 0))
            def _():
                for s_ref in s_refs:
                    s_ref[...] = jnp.zeros(s_ref.shape, F32)
        for c in range(tm // sub):
            rs = slice(c * sub, (c + 1) * sub)
            if len(pieces) == 1:
                r = lax.dot_general(a_ref[rs, :], refs[1][...], dn, preferred_element_type=F32)
            else:
                r = None
                for p_ref, off, w_ in zip(a_refs, offs, widths):
                    b_part = refs[1][off:off + w_, :] if mode == "nn" else refs[1][:, off:off + w_]
                    t = lax.dot_general(p_ref[rs, :], b_part, dn, preferred_element_type=F32)
                    r = t if r is None else r + t
            o_vals, s_vals = epi(r, *[x[rs, :] for x in refs[2:2 + len(rows)]], *[x[...] for x in refs[2 + len(rows):n_in]])
            assert len(o_vals) == len(o_refs) and len(s_vals) == len(s_refs)
            for o_ref, val in zip(o_refs, o_vals):
                o_ref[rs, :] = val.astype(o_ref.dtype)
            for s_ref, val in zip(s_refs, s_vals):
                s_ref[...] += val

    once = dict(pipeline_mode=pl.Buffered(1)) if nb == tnb else {}
    if mode == "nn":
        b_spec = pl.BlockSpec((k, tnb), lambda i, j: (0, j), **once)
    else:
        b_spec = pl.BlockSpec((tnb, k), lambda i, j: (j, 0), **once)
    if len(pieces) == 1:
        in_specs = [pl.BlockSpec((tm, k), lambda i, j: (i, a_off)), b_spec]
    else:
        assert sum(widths) == k and all(w_ % LANES == 0 for w_ in widths)
        in_specs = [pl.BlockSpec((tm, w_), lambda i, j: (i, 0)) for w_ in widths] + [b_spec]
    rows = [tuple(r) + (0,) * (3 - len(r)) for r in rows]
    in_specs += [pl.BlockSpec((tm, w), functools.partial(lambda i, j, off: (i, j + off), off=off)) for _, w, off in rows]
    in_specs += [pl.BlockSpec(v.shape, lambda i, j: (0, 0)) for v in list(vecs) + list(pro_vecs)]
    in_specs += [pl.BlockSpec(memory_space=pl.ANY)] if dep is not None else []
    out_specs = [pl.BlockSpec((tm, w), lambda i, j: (i, j)) for _, w, _ in outs]
    out_specs += [pl.BlockSpec((1, w), lambda i, j: (0, 0)) for w in sums]
    out_shape = [jax.ShapeDtypeStruct((m, full), dt) for full, _, dt in outs]
    out_shape += [jax.ShapeDtypeStruct((1, w), F32) for w in sums]
    if pro is not None:
        out_specs.append(pl.BlockSpec((tm, k), lambda i, j: (i, 0)))
        out_shape.append(jax.ShapeDtypeStruct((m, k), BF16))
    return pl.pallas_call(
        body, name=name, grid=(m // tm, nb // tnb),
        in_specs=in_specs, out_specs=out_specs, out_shape=out_shape,
        scratch_shapes=[pltpu.VMEM((tm, k), BF16)] if pro is not None else [],
        compiler_params=_params(("arbitrary", "arbitrary") if sums else ("parallel", "arbitrary" if pro is not None else "parallel")),
    )(*pieces, b, *[r[0] for r in rows], *vecs, *pro_vecs, *([dep] if dep is not None else []))


def _rowwise(fn, row_ins, vec_ins, row_outs, sum_outs, name, tm=512):
    n_in = len(row_ins) + len(vec_ins)
    n_o = len(row_outs)
    rows = row_ins[0][0].shape[0]

    def body(*refs):
        vals = [r[...] for r in refs[:n_in]]
        outs = refs[n_in:]
        ro, so = fn(*vals)
        assert len(ro) == n_o and len(so) == len(sum_outs)
        for r, v in zip(outs[:n_o], ro):
            r[...] = v.astype(r.dtype)
        if sum_outs:
            @pl.when(pl.program_id(0) == 0)
            def _():
                for r in outs[n_o:]:
                    r[...] = jnp.zeros(r.shape, F32)
            for r, v in zip(outs[n_o:], so):
                r[...] += v

    in_specs = [pl.BlockSpec((tm, w), functools.partial(lambda i, b: (i, b), b=b)) for _, w, b in row_ins]
    in_specs += [pl.BlockSpec(v.shape, lambda i: (0, 0)) for v in vec_ins]
    out_specs = [pl.BlockSpec((tm, w), lambda i: (i, 0)) for w, _ in row_outs]
    out_specs += [pl.BlockSpec((1, w), lambda i: (0, 0)) for w in sum_outs]
    out_shape = [jax.ShapeDtypeStruct((rows, w), dt) for w, dt in row_outs]
    out_shape += [jax.ShapeDtypeStruct((1, w), F32) for w in sum_outs]
    return pl.pallas_call(
        body, name=name, grid=(rows // tm,),
        in_specs=in_specs, out_specs=out_specs, out_shape=out_shape,
        compiler_params=_params(("arbitrary",)),
    )(*[a for a, _, _ in row_ins], *vec_ins)


def _sigmoid(x):
    return 1.0 / (1.0 + jnp.exp(-x))


def _rstd(x):
    return lax.rsqrt(jnp.mean(x * x, axis=-1, keepdims=True) + NORM_EPS)


def _norm_bwd(dyn, xn, r):
    return r * (dyn - xn * jnp.mean(dyn * xn, axis=-1, keepdims=True))


def _colsum(x):
    return jnp.sum(x, axis=0, keepdims=True)


def _rope_tables(pos, invf):
    def fn(p, f):
        lane = lax.broadcasted_iota(jnp.int32, (1, LANES), 1)
        ang = p * f
        cs, sn = jnp.cos(ang), jnp.sin(ang)
        rot = (lane >= 64) & (lane < 96)
        ct = jnp.where(lane < 64, 1.0, jnp.where(rot, cs, 0.0))
        sa = jnp.where((lane >= 64) & (lane < 80), -sn, 0.0)
        sb = jnp.where((lane >= 80) & (lane < 96), sn, 0.0)
        return (ct, sa, sb), ()
    return _rowwise(fn, [(pos, 1, 0)], [invf], [(LANES, F32)] * 3, [], "rope_tables")


def _rope(x, ct, sa, sb):
    return x * ct + pltpu.roll(x, LANES - 16, 1) * sa + pltpu.roll(x, 16, 1) * sb


def _rope_t(x, ct, sa, sb):
    return x * ct - pltpu.roll(x, LANES - 16, 1) * sa - pltpu.roll(x, 16, 1) * sb


def _head_mask(width, hh):
    lane = lax.broadcasted_iota(jnp.int32, (1, width), 1)
    half = width // 2
    return (lane >= hh * half) & (lane < (hh + 1) * half)


ATT_PP = 2
ATT_CHAINS = [(a, hh) for a in range(ATT_PP) for hh in range(2)]
ATT_G = HEADS // (2 * ATT_PP)


def _pair(ref_or_val, a, width, rows=slice(None)):
    return ref_or_val[rows, a * width:(a + 1) * width]


def _head_cols(ref, a, hh, dkp, rows=slice(None)):
    if dkp == 2 * LANES:
        return ref[rows, a * dkp + hh * LANES:a * dkp + (hh + 1) * LANES]
    blk = _pair(ref, a, dkp, rows)
    return jnp.where(_head_mask(dkp, hh), blk, jnp.zeros_like(blk))


def _attn_fwd(q, qo, k, ko, v, vo, dkp, scale, bias, name, dep=None):
    T = ATT_T
    assert qo % ATT_PP == 0 and ko % ATT_PP == 0 and vo % ATT_PP == 0
    qo, ko, vo = qo // ATT_PP, ko // ATT_PP, vo // ATT_PP
    split = dkp == 2 * LANES

    def body(*refs):
        refs = list(refs)
        if dep is not None:
            del refs[3 + (bias is not None)]
        if bias is not None:
            q_ref, k_ref, v_ref, b_ref, o_ref, lse_ref, s_scr = refs
        else:
            q_ref, k_ref, v_ref, o_ref, lse_ref, s_scr = refs
        i = pl.program_id(1)
        row = lax.broadcasted_iota(jnp.int32, (T, T), 0)
        col = lax.broadcasted_iota(jnp.int32, (T, T), 1)
        qms = [_head_cols(q_ref, a, hh, dkp) for a, hh in ATT_CHAINS]

        def k_of(a, hh, ks):
            return _head_cols(k_ref, a, hh, dkp, ks) if split else _pair(k_ref, a, dkp, ks)

        def fold(t):
            return [t[:, c * LANES:(c + 1) * LANES] for c in range(T // LANES)]

        def run(nt):
            mls = [jnp.full((T, LANES), -jnp.inf, F32) for _ in ATT_CHAINS]
            for j in range(nt):
                ks = slice(j * T, (j + 1) * T)
                for ci, (a, hh) in enumerate(ATT_CHAINS):
                    s = lax.dot_general(qms[ci], k_of(a, hh, ks), NT, preferred_element_type=F32) * (scale * LOG2E)
                    if bias is not None:
                        s = s + b_ref[2 * a + hh, j] * LOG2E
                    if j == nt - 1:
                        s = jnp.where(row >= col, s, -jnp.inf)
                    s_scr[ci, j] = s
                    for part in fold(s):
                        mls[ci] = jnp.maximum(mls[ci], part)
            ms = [jnp.max(ml, axis=1, keepdims=True) for ml in mls]
            mbs = [jnp.broadcast_to(m, (T, LANES)) for m in ms]
            for a in range(ATT_PP):
                ls = [jnp.zeros((T, LANES), F32) for _ in range(2)]
                ps, vms = [], []
                for j in range(nt):
                    vb = _pair(v_ref, a, LANES, slice(j * T, (j + 1) * T))
                    for hh in range(2):
                        parts = [jnp.exp2(part - mbs[2 * a + hh]) for part in fold(s_scr[2 * a + hh, j])]
                        for part in parts:
                            ls[hh] = ls[hh] + part
                        ps.append(jnp.concatenate(parts, axis=1).astype(BF16))
                        vms.append(jnp.where(_head_mask(LANES, hh), vb, jnp.zeros_like(vb)))
                acc = lax.dot_general(jnp.concatenate(ps, axis=1), jnp.concatenate(vms, axis=0), NN,
                                      preferred_element_type=F32)
                l0, l1 = [jnp.sum(l, axis=1, keepdims=True) for l in ls]
                lse_ref[2 * a] = ms[2 * a] + jnp.log2(l0)
                lse_ref[2 * a + 1] = ms[2 * a + 1] + jnp.log2(l1)
                inv = jnp.where(_head_mask(LANES, 0), 1.0 / l0, 1.0 / l1)
                o_ref[:, a * LANES:(a + 1) * LANES] = (acc * inv).astype(o_ref.dtype)

        for nt in range(1, N_ATT + 1):
            pl.when(i == nt - 1)(functools.partial(run, nt))

    in_specs = [
        pl.BlockSpec((T, ATT_PP * dkp), lambda g, i: (i, qo + g)),
        pl.BlockSpec((S, ATT_PP * dkp), lambda g, i: (0, ko + g)),
        pl.BlockSpec((S, ATT_PP * LANES), lambda g, i: (0, vo + g)),
    ]
    args = [q, k, v]
    if bias is not None:
        in_specs.append(pl.BlockSpec((2 * ATT_PP, N_ATT, 1, T), lambda g, i: (g, 0, 0, 0)))
        args.append(bias)
    if dep is not None:
        in_specs.append(pl.BlockSpec(memory_space=pl.ANY))
        args.append(dep)
    return pl.pallas_call(
        body, name=name, grid=(ATT_G, N_ATT),
        in_specs=in_specs,
        out_specs=[pl.BlockSpec((T, ATT_PP * LANES), lambda g, i: (i, g)),
                   pl.BlockSpec((2 * ATT_PP, T, 1), lambda g, i: (g, i, 0))],
        out_shape=[jax.ShapeDtypeStruct((S, HEADS * HEAD_DIM), BF16),
                   jax.ShapeDtypeStruct((HEADS, S, 1), F32)],
        scratch_shapes=[pltpu.VMEM((len(ATT_CHAINS), N_ATT, T, T), F32)],
        compiler_params=_params(("parallel", "arbitrary")),
    )(*args)


def _attn_grad(q, qo, k, ko, v, vo, do, lse, dkp, scale, bias, qk_dtype, name, dep=None):
    T = ATT_T
    has_b = bias is not None
    qo, ko, vo = qo // ATT_PP, ko // ATT_PP, vo // ATT_PP
    n_ch = len(ATT_CHAINS)
    split = dkp == 2 * LANES

    def body(*refs):
        refs = list(refs)
        if dep is not None:
            del refs[5 + has_b]
        q_ref, k_ref, v_ref, do_ref, lse_ref = refs[:5]
        refs = refs[5:]
        if has_b:
            b_ref, refs = refs[0], refs[1:]
        dq_ref, dk_ref, dv_ref = refs[:3]
        refs = refs[3:]
        if has_b:
            db_ref, refs = refs[0], refs[1:]
        p_scr, dp_scr, dk_acc, dv_acc = refs[:4]
        db_acc = refs[4] if has_b else None
        i = pl.program_id(1)

        @pl.when(i == 0)
        def _():
            dk_acc[...] = jnp.zeros(dk_acc.shape, F32)
            dv_acc[...] = jnp.zeros(dv_acc.shape, F32)
            if has_b:
                db_acc[...] = jnp.zeros(db_acc.shape, F32)

        row = lax.broadcasted_iota(jnp.int32, (T, T), 0)
        col = lax.broadcasted_iota(jnp.int32, (T, T), 1)

        def fold(t):
            return [t[:, c * LANES:(c + 1) * LANES] for c in range(T // LANES)]

        qrows = pl.ds(pl.multiple_of(i * T, T), T)
        qms, doms, lses = [], [], []
        for a, hh in ATT_CHAINS:
            dob = _pair(do_ref, a, LANES, qrows)
            qms.append(_head_cols(q_ref, a, hh, dkp, qrows))
            doms.append(jnp.where(_head_mask(LANES, hh), dob, jnp.zeros_like(dob)))
            lses.append(lse_ref[2 * a + hh])

        def k_of(a, hh, ks):
            return _head_cols(k_ref, a, hh, dkp, ks) if split else _pair(k_ref, a, dkp, ks)

        def run(nt):
            dls = [jnp.zeros((T, LANES), F32) for _ in ATT_CHAINS]
            for j in range(nt):
                ks = slice(j * T, (j + 1) * T)
                for ci, (a, hh) in enumerate(ATT_CHAINS):
                    s = lax.dot_general(qms[ci], k_of(a, hh, ks), NT, preferred_element_type=F32) * (scale * LOG2E)
                    if has_b:
                        s = s + b_ref[ci, j] * LOG2E
                    s = s - lses[ci]
                    if j == nt - 1:
                        s = jnp.where(row >= col, s, -jnp.inf)
                    p = jnp.exp2(s)
                    dp = lax.dot_general(doms[ci], _pair(v_ref, a, LANES, ks), NT, preferred_element_type=F32)
                    p_scr[ci, j] = p
                    dp_scr[ci, j] = dp
                    for part in fold(p * dp):
                        dls[ci] = dls[ci] + part
            deltas = [jnp.broadcast_to(jnp.sum(dl, axis=1, keepdims=True), (T, LANES)) for dl in dls]
            for a in range(ATT_PP):
                ds_all, km_all = [], []
                if split:
                    qts = [jnp.transpose(qms[2 * a + hh]) for hh in range(2)]
                else:
                    qm2t = jnp.transpose(jnp.concatenate([qms[2 * a], qms[2 * a + 1]], axis=0))
                dom2t = jnp.transpose(jnp.concatenate([doms[2 * a], doms[2 * a + 1]], axis=0))
                for j in range(nt):
                    ks = slice(j * T, (j + 1) * T)
                    p2, ds2 = [], []
                    for hh in range(2):
                        ci = 2 * a + hh
                        p = p_scr[ci, j]
                        ds = jnp.concatenate([pp * (dd - deltas[ci]) for pp, dd in zip(fold(p), fold(dp_scr[ci, j]))], axis=1)
                        if has_b:
                            db_acc[ci, j] += jnp.sum(ds, axis=0, keepdims=True)
                        p2.append(p.astype(BF16))
                        ds2.append((ds * scale).astype(BF16))
                        if not split:
                            km_all.append(_head_cols(k_ref, a, hh, dkp, ks))
                    dv_acc[a * LANES:(a + 1) * LANES, ks] += lax.dot_general(
                        dom2t, jnp.concatenate(p2, axis=0), NN, preferred_element_type=F32)
                    if split:
                        for hh in range(2):
                            dk_acc[a * dkp + hh * LANES:a * dkp + (hh + 1) * LANES, ks] += lax.dot_general(
                                qts[hh], ds2[hh], NN, preferred_element_type=F32)
                    else:
                        dk_acc[a * dkp:(a + 1) * dkp, ks] += lax.dot_general(
                            qm2t, jnp.concatenate(ds2, axis=0), NN, preferred_element_type=F32)
                    ds_all += ds2
                if split:
                    for hh in range(2):
                        dq = lax.dot_general(jnp.concatenate(ds_all[hh::2], axis=1),
                                             _head_cols(k_ref, a, hh, dkp, slice(0, nt * T)), NN,
                                             preferred_element_type=F32)
                        dq_ref[:, a * dkp + hh * LANES:a * dkp + (hh + 1) * LANES] = dq.astype(dq_ref.dtype)
                else:
                    dq = lax.dot_general(jnp.concatenate(ds_all, axis=1), jnp.concatenate(km_all, axis=0), NN,
                                         preferred_element_type=F32)
                    dq_ref[:, a * dkp:(a + 1) * dkp] = dq.astype(dq_ref.dtype)

        for nt in range(1, N_ATT + 1):
            pl.when(i == nt - 1)(functools.partial(run, nt))

        @pl.when(i == N_ATT - 1)
        def _():
            dk_ref[...] = jnp.transpose(dk_acc[...]).astype(dk_ref.dtype)
            dv_ref[...] = jnp.transpose(dv_acc[...]).astype(dv_ref.dtype)
            if has_b:
                db_ref[...] = db_acc[...]

    in_specs = [
        pl.BlockSpec((S, ATT_PP * dkp), lambda g, i: (0, qo + g)),
        pl.BlockSpec((S, ATT_PP * dkp), lambda g, i: (0, ko + g)),
        pl.BlockSpec((S, ATT_PP * LANES), lambda g, i: (0, vo + g)),
        pl.BlockSpec((S, ATT_PP * LANES), lambda g, i: (0, g)),
        pl.BlockSpec((2 * ATT_PP, T, 1), lambda g, i: (g, i, 0)),
    ]
    args = [q, k, v, do, lse]
    out_specs = [
        pl.BlockSpec((T, ATT_PP * dkp), lambda g, i: (i, g)),
        pl.BlockSpec((S, ATT_PP * dkp), lambda g, i: (0, g)),
        pl.BlockSpec((S, ATT_PP * LANES), lambda g, i: (0, g)),
    ]
    width = (HEADS // 2) * dkp
    out_shape = [
        jax.ShapeDtypeStruct((S, width), qk_dtype),
        jax.ShapeDtypeStruct((S, width), qk_dtype),
        jax.ShapeDtypeStruct((S, HEADS * HEAD_DIM), BF16),
    ]
    scratch = [pltpu.VMEM((n_ch, N_ATT, T, T), F32), pltpu.VMEM((n_ch, N_ATT, T, T), F32),
               pltpu.VMEM((ATT_PP * dkp, S), F32), pltpu.VMEM((ATT_PP * LANES, S), F32)]
    if has_b:
        bspec = pl.BlockSpec((2 * ATT_PP, N_ATT, 1, T), lambda g, i: (g, 0, 0, 0))
        in_specs.append(bspec)
        args.append(bias)
        out_specs.append(bspec)
        out_shape.append(jax.ShapeDtypeStruct((HEADS, N_ATT, 1, T), F32))
        scratch.append(pltpu.VMEM((2 * ATT_PP, N_ATT, 1, T), F32))
    if dep is not None:
        in_specs.append(pl.BlockSpec(memory_space=pl.ANY))
        args.append(dep)
    return pl.pallas_call(
        body, name=name, grid=(ATT_G, N_ATT),
        in_specs=in_specs, out_specs=out_specs, out_shape=out_shape, scratch_shapes=scratch,
        compiler_params=_params(("parallel", "arbitrary")),
    )(*args)


def _tri(upper):
    a = lax.broadcasted_iota(jnp.int32, (LANES, LANES), 0)
    b = lax.broadcasted_iota(jnp.int32, (LANES, LANES), 1)
    return jnp.where(a <= b if upper else a >= b, 1.0, 0.0).astype(F32)


def _fox_gates(proj, blk, bf):
    def body(m_ref, b_ref, z_out, o_ref):
        tri = _tri(True)
        carry = jnp.zeros((HEADS, 1), F32)
        for t in range(S // LANES):
            sl = slice(t * LANES, (t + 1) * LANES)
            zt = jnp.transpose(m_ref[sl, :])[:HEADS]
            z_out[:, sl] = zt
            z = zt + b_ref[...]
            logf = jnp.minimum(z, 0.0) - jnp.log(1.0 + jnp.exp(-jnp.abs(z)))
            c = lax.dot_general(logf, tri, NN, preferred_element_type=F32,
                                precision=lax.Precision.HIGHEST) + carry
            o_ref[:, sl] = -c
            carry = c[:, LANES - 1:LANES]

    return pl.pallas_call(
        body, name="fox_gates", grid=(1,),
        in_specs=[pl.BlockSpec((S, LANES), lambda i: (0, blk)), pl.BlockSpec(bf.shape, lambda i: (0, 0))],
        out_specs=[pl.BlockSpec((HEADS, S), lambda i: (0, 0))] * 2,
        out_shape=[jax.ShapeDtypeStruct((HEADS, S), F32)] * 2,
        compiler_params=_params(("arbitrary",)),
    )(proj, bf)


def _fox_gates_bwd(dbias, zt, bf):
    def body(d_ref, z_ref, b_ref, dz_ref, dbf_ref):
        tri = _tri(False)
        carry = jnp.zeros((HEADS, 1), F32)
        tot = jnp.zeros((HEADS, 1), F32)
        for t in reversed(range(S // LANES)):
            sl = slice(t * LANES, (t + 1) * LANES)
            df = -d_ref[:, sl]
            c = lax.dot_general(df, tri, NN, preferred_element_type=F32,
                                precision=lax.Precision.HIGHEST) + carry
            carry = c[:, 0:1]
            z = z_ref[:, sl] + b_ref[...]
            dz = c * _sigmoid(-z)
            dz_ref[:, sl] = dz
            tot = tot + jnp.sum(dz, axis=1, keepdims=True)
        dbf_ref[...] = tot

    return pl.pallas_call(
        body, name="fox_gates_bwd",
        out_shape=[jax.ShapeDtypeStruct((HEADS, S), F32), jax.ShapeDtypeStruct((HEADS, 1), F32)],
        compiler_params=_params(),
    )(dbias, zt, bf)


def _mod_part(c_all, w_ada, b_cols, before=()):
    def body(c_ref, w_ref, b_ref, *rest):
        o_ref, s_ref = rest[-2:]
        c = c_ref[...]
        sc = c * _sigmoid(c)
        s_ref[...] = sc
        o_ref[...] = lax.dot_general(sc, w_ref[...], NN, preferred_element_type=F32,
                                     precision=lax.Precision.HIGHEST) + b_ref[...]

    return pl.pallas_call(
        body, name="mod_part",
        in_specs=[pl.BlockSpec(memory_space=pltpu.VMEM)] * 3 + [pl.BlockSpec(memory_space=pl.ANY)] * len(before),
        out_shape=[jax.ShapeDtypeStruct((N_DEV, w_ada.shape[1]), F32), jax.ShapeDtypeStruct(c_all.shape, F32)],
        compiler_params=_params(),
    )(c_all, w_ada, b_cols, *before)


def _adamw_w_ada(w, m, v, sc_t, dm):
    rows, cols = w.shape
    tr = 256

    def body(w_ref, m_ref, v_ref, s_ref, d_ref, g_out, d_out, m_out, v_out):
        g = s_ref[:, 0:1] * d_ref[0:1, :]
        for b in range(1, N_DEV):
            g = g + s_ref[:, b:b + 1] * d_ref[b:b + 1, :]
        g_out[...] = g
        d_out[...], m_out[...], v_out[...] = _adamw_math(w_ref[...], g, m_ref[...], v_ref[...])

    spec = pl.BlockSpec((tr, cols), lambda i: (i, 0))
    return pl.pallas_call(
        body, name="adamw_w_ada", grid=(rows // tr,),
        in_specs=[spec, spec, spec, pl.BlockSpec((tr, N_DEV), lambda i: (i, 0)), pl.BlockSpec(dm.shape, lambda i: (0, 0))],
        out_specs=[spec] * 4, out_shape=[jax.ShapeDtypeStruct((rows, cols), F32)] * 4,
        compiler_params=_params(("parallel",)),
    )(w, m, v, sc_t, dm)


def _adamw(w, m, v, parts, name, own=None, slot=None):
    rows, cols = w.shape
    n = parts.shape[0]
    by_cols = rows % 256 != 0 and cols % 256 == 0
    tr, tc = (rows, 256) if by_cols else ((rows if rows <= 512 else 256), cols)
    tile = (lambda i: (0, i)) if by_cols else (lambda i: (i, 0))

    def body(*refs):
        if own is not None:
            s_ref, refs = refs[0], refs[1:]
            w_ref, m_ref, v_ref, p_ref, o_ref, g_out, d_out, m_out, v_out = refs
            terms = [jnp.where(s_ref[0] == kk, o_ref[0], p_ref[kk]) for kk in range(n)]
        else:
            w_ref, m_ref, v_ref, p_ref, g_out, d_out, m_out, v_out = refs
            terms = [p_ref[kk] for kk in range(n)]
        g = terms[0].astype(F32)
        for term in terms[1:]:
            g = g + term.astype(F32)
        g_out[...] = g
        d_out[...], m_out[...], v_out[...] = _adamw_math(w_ref[...], g, m_ref[...], v_ref[...])

    spec = pl.BlockSpec((tr, tc), lambda i, *_: tile(i))
    in_specs = [spec, spec, spec, pl.BlockSpec((n, tr, tc), lambda i, *_: (0,) + tile(i))]
    out_shape = [jax.ShapeDtypeStruct((rows, cols), F32)] * 4
    grid = (rows // tr if not by_cols else cols // tc,)
    if own is None:
        return pl.pallas_call(
            body, name=name, grid=grid, in_specs=in_specs, out_specs=[spec] * 4, out_shape=out_shape,
            compiler_params=_params(("parallel",)),
        )(w, m, v, parts)
    in_specs.append(pl.BlockSpec((1, tr, tc), lambda i, s: (s[0],) + tile(i)))
    return pl.pallas_call(
        body, name=name, out_shape=out_shape, compiler_params=_params(("parallel",)),
        grid_spec=pltpu.PrefetchScalarGridSpec(num_scalar_prefetch=1, grid=grid, in_specs=in_specs,
                                               out_specs=[spec] * 4),
    )(slot, w, m, v, parts, own)


def _adamw_math(w, g, m, v):
    mm = ADAM_B1 * m + (1.0 - ADAM_B1) * g
    vv = ADAM_B2 * v + (1.0 - ADAM_B2) * (g * g)
    m_hat = mm / (1.0 - ADAM_B1 ** ADAM_STEP)
    v_hat = vv / (1.0 - ADAM_B2 ** ADAM_STEP)
    return -ADAM_LR * (m_hat / (jnp.sqrt(v_hat) + ADAM_EPS) + ADAM_WD * w), mm, vv


def _adamw_rows(bundles, offsets, ws, ms, vs, err_off, err_width):
    k = len(ws)

    def body(*refs):
        b_ref = refs[0]
        w_refs, m_refs, v_refs = refs[1:1 + k], refs[1 + k:1 + 2 * k], refs[1 + 2 * k:1 + 3 * k]
        outs = refs[1 + 3 * k:]
        g_all = b_ref[0]
        for kk in range(1, N_DEV):
            g_all = g_all + b_ref[kk]
        for i in range(k):
            width = w_refs[i].shape[1]
            g = g_all[:, offsets[i]:offsets[i] + width]
            outs[4 * i][...] = g
            outs[4 * i + 1][...], outs[4 * i + 2][...], outs[4 * i + 3][...] = _adamw_math(
                w_refs[i][...], g, m_refs[i][...], v_refs[i][...])
        outs[4 * k][...] = g_all[:, err_off:err_off + err_width]

    out_shape = []
    for w_ in ws:
        out_shape += [jax.ShapeDtypeStruct(w_.shape, F32)] * 4
    out_shape.append(jax.ShapeDtypeStruct((1, err_width), F32))
    res = pl.pallas_call(body, name="adamw_rows", out_shape=out_shape, compiler_params=_params())(bundles, *ws, *ms, *vs)
    return [tuple(res[4 * i:4 * i + 4]) for i in range(k)], res[-1]


def _coords():
    return lax.axis_index("x"), lax.axis_index("y"), lax.axis_index("c")


def _flat(px, py, pc):
    return 4 * px + 2 * py + pc


def _all_gather(arrs, name):
    n = len(arrs)

    def body(*refs):
        ins, outs = refs[:n], refs[n:2 * n]
        send, recv, lsem = refs[2 * n:]
        x, y, c = _coords()
        me, sibling = (x, y, c), (x, y, 1 - c)
        chips = [(1 - x, y), (x, 1 - y), (1 - x, 1 - y)]

        def copy(a, kk, block, to, src=None):
            slot = outs[a].at[_flat(*block)]
            return pltpu.make_async_remote_copy(
                src_ref=slot if src is None else src, dst_ref=slot,
                send_sem=send.at[a, kk], recv_sem=recv.at[a, kk],
                device_id=to, device_id_type=MESH)

        mine = [pltpu.make_async_copy(ins[a], outs[a].at[_flat(*me)], lsem.at[a]) for a in range(n)]
        for cp in mine:
            cp.start()
        first = []
        for a in range(n):
            first.append(copy(a, 0, me, sibling, src=ins[a]))
            first += [copy(a, 1 + j, me, (*chip, c), src=ins[a]) for j, chip in enumerate(chips)]
        for cp in first:
            cp.start()
        passed = []
        for j, chip in enumerate(chips):
            for a in range(n):
                copy(a, 1 + j, (*chip, c), me).wait_recv()
                cp = copy(a, 4 + j, (*chip, c), sibling)
                cp.start()
                passed.append(cp)
        for a in range(n):
            copy(a, 0, sibling, me).wait_recv()
        for j, chip in enumerate(chips):
            for a in range(n):
                copy(a, 4 + j, (*chip, 1 - c), me).wait_recv()
        for cp in first + passed:
            cp.wait_send()
        for cp in mine:
            cp.wait()

    any_spec = pl.BlockSpec(memory_space=pl.ANY)
    return pl.pallas_call(
        body, name=name,
        in_specs=[any_spec] * n, out_specs=[any_spec] * n,
        out_shape=[jax.ShapeDtypeStruct((N_DEV,) + a.shape, a.dtype) for a in arrs],
        scratch_shapes=[pltpu.SemaphoreType.DMA((n, 7)), pltpu.SemaphoreType.DMA((n, 7)),
                        pltpu.SemaphoreType.DMA((n,))],
    )(*arrs)


def _peer_list():
    x, y, c = _coords()
    return [((1 - x if r & 4 else x), (1 - y if r & 2 else y), (1 - c if r & 1 else c)) for r in range(1, N_DEV)]


def _copy_plan(mode, src, land):
    x, y, c = _coords()
    me = _flat(x, y, c)
    if mode == "gather":
        return [(src, land.at[me], peer) for peer in _peer_list()]
    if mode == "exchange":
        return [(src.at[_flat(*peer)], land.at[me], peer) for peer in _peer_list()]
    if mode == "pair":
        return [(src.at[_flat(q // 2, q % 2, 1 - c)], land.at[q], (x, y, 1 - c)) for q in range(N_DEV // 2)]
    chips = [((1 - x if r & 2 else x), (1 - y if r & 1 else y)) for r in range(1, N_DEV // 2)]
    if mode == "chips":
        return [(src.at[2 * qx + qy], land.at[2 * x + y], (qx, qy, c)) for qx, qy in chips]
    if mode == "spread":
        return [(src, land.at[me], (x, y, 1 - c))] + [(src, land.at[me], (qx, qy, c)) for qx, qy in chips]
    assert mode == "forward"
    return [(land.at[_flat(qx, qy, c)], land.at[_flat(qx, qy, c)], (x, y, 1 - c)) for qx, qy in chips]


N_COPIES = dict(gather=N_DEV - 1, exchange=N_DEV - 1, pair=N_DEV // 2, chips=N_DEV // 2 - 1, spread=N_DEV // 2,
                forward=N_DEV // 2 - 1)


def _land_shape(mode, shape):
    return {"gather": (N_DEV,) + shape, "spread": (N_DEV,) + shape, "exchange": shape,
            "pair": (N_DEV // 2,) + shape[1:], "chips": shape}[mode]


HBM_SPEC = pl.BlockSpec(memory_space=pltpu.HBM)
SEM_SPEC = pl.BlockSpec(memory_space=pltpu.SEMAPHORE)
ANY_SPEC = pl.BlockSpec(memory_space=pl.ANY)
SIDE_EFFECT = pltpu.SideEffectType.DATAFLOW_SIDE_EFFECTING


def _async_start(groups, modes, after, name):
    modes = [modes] * len(groups) if isinstance(modes, str) else list(modes)
    arrs = [(a, m) for g, m in zip(groups, modes) for a in g]
    n = len(arrs)
    fresh = [i for i, (_, m) in enumerate(arrs) if m != "forward"]

    def body(*refs):
        srcs, new_lands = refs[:n], refs[n:n + len(fresh)]
        outs = refs[n + len(fresh) + 1:]
        lands = list(srcs)
        for k, i in enumerate(fresh):
            lands[i] = new_lands[k]
        for ai, (_, mode) in enumerate(arrs):
            for src_ref, dst_ref, peer in _copy_plan(mode, srcs[ai], lands[ai]):
                pltpu.make_async_remote_copy(src_ref=src_ref, dst_ref=dst_ref, send_sem=outs[2 * ai],
                                             recv_sem=outs[2 * ai + 1], device_id=peer, device_id_type=MESH).start()
        outs[-1][...] = jnp.zeros(outs[-1].shape, F32)

    land_shapes = [(_land_shape(arrs[i][1], arrs[i][0].shape), arrs[i][0].dtype) for i in fresh]
    n_buf = n + len(fresh)
    out_shape = [pltpu.SemaphoreType.DMA(())] * (2 * n)
    out_shape += [pltpu.HBM(a.shape, a.dtype) for a, _ in arrs]
    out_shape += [pltpu.HBM(shape, dt) for shape, dt in land_shapes]
    out_shape.append(jax.ShapeDtypeStruct((8, LANES), F32))
    res = pl.pallas_call(
        body, name=name, out_shape=tuple(out_shape),
        in_specs=[HBM_SPEC] * n_buf + [ANY_SPEC],
        out_specs=tuple([SEM_SPEC] * (2 * n) + [HBM_SPEC] * n_buf + [pl.BlockSpec(memory_space=pltpu.VMEM)]),
        input_output_aliases={i: 2 * n + i for i in range(n_buf)},
        compiler_params=pltpu.CompilerParams(has_side_effects=SIDE_EFFECT),
    )(*[pltpu.with_memory_space_constraint(a, pltpu.HBM) for a, _ in arrs],
      *[pltpu.with_memory_space_constraint(lax.empty(shape, dt), pltpu.HBM) for shape, dt in land_shapes],
      after)
    sems, thru = res[:2 * n], res[2 * n:-1]
    land_of = {i: thru[n + k] for k, i in enumerate(fresh)}
    states, idx = [], 0
    for g, mode in zip(groups, modes):
        ids = range(idx, idx + len(g))
        idx += len(g)
        states.append(([sems[2 * i] for i in ids], [sems[2 * i + 1] for i in ids],
                       None if mode == "forward" else [thru[i] for i in ids],
                       [land_of.get(i, thru[i]) for i in ids], mode))
    return states, res[-1]


def _async_wait(state, after, name):
    sends, recvs, srcs, lands, mode = state
    g = len(lands)
    bufs = (list(srcs) if srcs is not None else []) + list(lands)
    nb = len(bufs)

    def body(*refs):
        l_refs, sems = refs[nb - g:nb], refs[nb:nb + 2 * g]
        for ai in range(g):
            moved = l_refs[ai].at[pl.ds(0, N_COPIES[mode])]
            cp = pltpu.make_async_remote_copy(src_ref=moved, dst_ref=moved, send_sem=sems[ai], recv_sem=sems[g + ai],
                                              device_id=_coords(), device_id_type=MESH)
            cp.wait_send()
            cp.wait_recv()

    res = pl.pallas_call(
        body, name=name,
        out_shape=tuple(pltpu.HBM(a.shape, a.dtype) for a in bufs),
        in_specs=[HBM_SPEC] * nb + [SEM_SPEC] * (2 * g) + [ANY_SPEC],
        out_specs=tuple([HBM_SPEC] * nb),
        input_output_aliases={i: i for i in range(nb)},
        compiler_params=pltpu.CompilerParams(has_side_effects=SIDE_EFFECT),
    )(*bufs, *sends, *recvs, after)
    return (list(res[:nb - g]) if srcs is not None else None), list(res[nb - g:])


def _add_sibling(mine, theirs, core):
    def body(c_ref, a_ref, b_ref, o_ref):
        o_ref[...] = (a_ref[...].astype(F32) + b_ref[...].astype(F32)).astype(o_ref.dtype)

    blk = (1,) + mine.shape[1:]
    return pl.pallas_call(
        body, name="add_sibling", out_shape=jax.ShapeDtypeStruct(theirs.shape, mine.dtype),
        grid_spec=pltpu.PrefetchScalarGridSpec(
            num_scalar_prefetch=1, grid=(theirs.shape[0],),
            in_specs=[pl.BlockSpec(blk, lambda q, c: (2 * q + c[0], 0, 0)), pl.BlockSpec(blk, lambda q, c: (q, 0, 0))],
            out_specs=pl.BlockSpec(blk, lambda q, c: (q, 0, 0))),
        compiler_params=_params(("parallel",)),
    )(core, mine, theirs)


def _with_own(land, own, me):
    return lax.dynamic_update_index_in_dim(land, own, me, 0)


IN_SPLITS = (512, 512, 512, 8, 768, 256, 32, 1024, 1024)


def _from_shards(g, fn, out_widths, name, own=None, slot=None):
    _, k, n = g.shape
    tr = min(k, 256)

    def body(*refs):
        if own is not None:
            s_ref, g_ref, own_ref = refs[:3]
            cols = [jnp.where(s_ref[0] == j, own_ref[...], g_ref[j]) for j in range(N_DEV)]
        else:
            g_ref = refs[0]
            cols = [g_ref[j] for j in range(N_DEV)]
        for o_ref, val in zip(refs[-len(out_widths):], fn(jnp.concatenate(cols, axis=1))):
            o_ref[...] = val

    in_specs = [pl.BlockSpec((N_DEV, tr, n), lambda i, *_: (0, i, 0))]
    out_spec = [pl.BlockSpec((tr, wd), lambda i, *_: (i, 0)) for wd in out_widths]
    out_shape = [jax.ShapeDtypeStruct((k, wd), g.dtype) for wd in out_widths]
    if own is None:
        return pl.pallas_call(body, name=name, grid=(k // tr,), in_specs=in_specs, out_specs=out_spec,
                              out_shape=out_shape, compiler_params=_params(("parallel",)))(g)
    in_specs.append(pl.BlockSpec((tr, n), lambda i, *_: (i, 0)))
    return pl.pallas_call(
        body, name=name, out_shape=out_shape, compiler_params=_params(("parallel",)),
        grid_spec=pltpu.PrefetchScalarGridSpec(num_scalar_prefetch=1, grid=(k // tr,), in_specs=in_specs, out_specs=out_spec),
    )(slot, g, own)


def _unshard_cols(g, own=None, slot=None):
    return _from_shards(g, lambda full: (full,), [N_DEV * g.shape[2]], "unshard_cols_%d" % g.shape[2], own, slot)[0]


FFN_T = D_FF // 2
FFN_SHARD = 2 * D_FF // N_DEV


def _unshard_ffn_in(g, own=None, slot=None):
    def pairs(full):
        parts = []
        for j in range(D_FF // FFN_T):
            parts += [full[:, j * FFN_T:(j + 1) * FFN_T], full[:, D_FF + j * FFN_T:D_FF + (j + 1) * FFN_T]]
        return (jnp.concatenate(parts, axis=1),)

    return _from_shards(g, pairs, [2 * D_FF], "unshard_ffn_in", own, slot)[0]


def _shard_ffn_in_t(wt):
    tc = 256

    def body(w_ref, o_ref):
        x = w_ref[...]
        nb = D_FF // FFN_T
        full = jnp.concatenate([x[(2 * j + half) * FFN_T:(2 * j + half + 1) * FFN_T]
                                for half in range(2) for j in range(nb)], axis=0)
        for j in range(N_DEV):
            o_ref[j] = full[j * FFN_SHARD:(j + 1) * FFN_SHARD]

    return pl.pallas_call(
        body, name="shard_ffn_in_t", grid=(D // tc,),
        in_specs=[pl.BlockSpec((2 * D_FF, tc), lambda i: (0, i))],
        out_specs=pl.BlockSpec((N_DEV, FFN_SHARD, tc), lambda i: (0, 0, i)),
        out_shape=jax.ShapeDtypeStruct((N_DEV, FFN_SHARD, D), wt.dtype),
        compiler_params=_params(("parallel",)),
    )(wt)


def _shard_cols(w):
    k, n = w.shape[0], w.shape[1] // N_DEV
    tr = min(k, 256)

    def body(w_ref, o_ref):
        full = w_ref[...]
        for j in range(N_DEV):
            o_ref[j] = full[:, j * n:(j + 1) * n]

    return pl.pallas_call(
        body, name="shard_cols_%d" % n, grid=(k // tr,),
        in_specs=[pl.BlockSpec((tr, N_DEV * n), lambda i: (i, 0))],
        out_specs=pl.BlockSpec((N_DEV, tr, n), lambda i: (0, i, 0)),
        out_shape=jax.ShapeDtypeStruct((N_DEV, k, n), w.dtype),
        compiler_params=_params(("parallel",)),
    )(w)


IN_OFFS = tuple(sum(IN_SPLITS[:i]) for i in range(len(IN_SPLITS) + 1))
IN_SHARD = IN_OFFS[-1] // N_DEV
REGROUP_ROWS = 256
MISC_AT = Q_LORA + KV_LORA + 2 * D
A_COLS = MISC_AT + LANES
A_TILE = A_COLS
B_COLS = 3 * HEADS * HEAD_DIM
MISC_BLOCK = MISC_AT // LANES
KR_AT = 64


def _w_in_regroup(g, own=None, slot=None):
    def groups(full):
        fq, fk, fv, wf, cq, ckv, kr, gf, gm = [full[:, IN_OFFS[i]:IN_OFFS[i + 1]] for i in range(9)]
        rows = full.shape[0]
        gap = jnp.zeros((rows, KR_AT - HEADS), BF16)
        tail = jnp.zeros((rows, LANES - KR_AT - ROPE_DIM), BF16)
        return jnp.concatenate([cq, ckv, gf, gm, wf, gap, kr, tail], axis=1), jnp.concatenate([fq, fk, fv], axis=1)

    return _from_shards(g, groups, [A_COLS, B_COLS], "w_in_regroup", own, slot)


def _w_in_ungroup(da, db_):
    def body(a_ref, b_ref, o_ref):
        a = a_ref[...]
        lora = Q_LORA + KV_LORA
        full = jnp.concatenate([b_ref[...], a[:, MISC_AT:MISC_AT + HEADS], a[:, :lora],
                                a[:, MISC_AT + KR_AT:MISC_AT + KR_AT + ROPE_DIM], a[:, lora:MISC_AT]], axis=1)
        for j in range(N_DEV):
            o_ref[j] = full[:, j * IN_SHARD:(j + 1) * IN_SHARD]

    tr = REGROUP_ROWS
    return pl.pallas_call(
        body, name="w_in_ungroup", grid=(D // tr,),
        in_specs=[pl.BlockSpec((tr, A_COLS), lambda i: (i, 0)), pl.BlockSpec((tr, B_COLS), lambda i: (i, 0))],
        out_specs=pl.BlockSpec((N_DEV, tr, IN_SHARD), lambda i: (0, i, 0)),
        out_shape=jax.ShapeDtypeStruct((N_DEV, D, IN_SHARD), BF16),
        compiler_params=_params(("parallel",)),
    )(da, db_)


def _prepare_weights(g, own=None, slot=None):
    w = {}
    if own is not None:
        small = ("w_uq", "w_ukv", "w_out", "w_ffn_out")
        g = {n: (_with_own(a, own[n], slot[0]) if n in small else a) for n, a in g.items()}
    pick = (lambda n: (own[n], slot)) if own is not None else (lambda n: (None, None))
    if "w_in" in g:
        w["w_a"], w["w_b"] = _w_in_regroup(g["w_in"], *pick("w_in"))
    if "w_uq" in g:
        w_uq = g["w_uq"].reshape(Q_LORA, HEADS, 96)
        w["w_uq"] = jnp.pad(w_uq, ((0, 0), (0, 0), (0, 32))).reshape(Q_LORA, HEADS * LANES)
        ukv = g["w_ukv"]
        w["w_k"] = jnp.transpose(jnp.pad(ukv[:, :, :64], ((0, 0), (0, 0), (0, 64))), (1, 0, 2)).reshape(KV_LORA, HEADS * LANES)
        w["w_v"] = jnp.transpose(ukv[:, :, 64:], (1, 0, 2)).reshape(KV_LORA, HEADS * HEAD_DIM)
    if "w_out" in g:
        w["w_pf"] = _unshard_cols(g["w_proj_fox"], *pick("w_proj_fox"))
        w["w_pm"] = _unshard_cols(g["w_proj_mla"], *pick("w_proj_mla"))
        w["w_out"] = g["w_out"].reshape(D, D)
    if "w_ffn_in" in g:
        w["w_ffn_in"] = _unshard_ffn_in(g["w_ffn_in"], *pick("w_ffn_in"))
        w["w_ffn_out"] = g["w_ffn_out"].reshape(D_FF, D)
    return w


def _shard_grads(dw):
    out = {}
    if "w_a" in dw:
        out["w_in"] = _w_in_ungroup(dw["w_a"], dw["w_b"])
    if "w_uq" in dw:
        w_uq = dw["w_uq"].reshape(Q_LORA, HEADS, LANES)[:, :, :96].reshape(Q_LORA, Q_LORA)
        out["w_uq"] = w_uq.reshape(N_DEV, Q_LORA // N_DEV, Q_LORA)
        k_part = dw["w_k"].reshape(KV_LORA, HEADS, LANES)[:, :, :64]
        v_part = dw["w_v"].reshape(KV_LORA, HEADS, HEAD_DIM)
        out["w_ukv"] = jnp.transpose(jnp.concatenate([k_part, v_part], axis=2), (1, 0, 2))
    if "w_out" in dw:
        out["w_proj_fox"] = _shard_cols(dw["w_pf"])
        out["w_proj_mla"] = _shard_cols(dw["w_pm"])
        out["w_out"] = dw["w_out"].reshape(N_DEV, D // N_DEV, D)
    if "w_ffn_in" in dw:
        out["w_ffn_in"] = _shard_ffn_in_t(dw["w_ffn_in"])
        out["w_ffn_out"] = dw["w_ffn_out"].reshape(N_DEV, D_FF // N_DEV, D)
    return out


def _rope_table_set(pos):
    inv_freq = 1.0 / (ROPE_THETA ** (jnp.arange(0, ROPE_DIM, 2, dtype=F32) / ROPE_DIM))
    invf = jnp.concatenate([jnp.zeros((64,), F32), inv_freq, inv_freq, jnp.zeros((32,), F32)]).reshape(1, LANES)
    return _rope_tables(pos, invf)


def _fwd_bwd(x, pos, mod, target, w, vec, wts, send, relay, tables=None):
    shift_mix, scale_mix, gate_mix, shift_ffn, scale_ffn, gate_ffn = [mod[:, i * D:(i + 1) * D] for i in range(6)]
    g_pre_mix, g_post_mix, g_pre_ffn, g_post_ffn = vec["g_pre_mix"], vec["g_post_mix"], vec["g_pre_ffn"], vec["g_post_ffn"]
    g_q, g_kv = vec["g_q_lora"], vec["g_kv_lora"]

    ct, sa, sb = tables if tables is not None else _rope_table_set(pos)

    def pre1(xv, g, sc, sh):
        return (xv * _rstd(xv) * g) * (1.0 + sc) + sh
    proj_a, h = _mm_epi(x, w["w_a"], "nn", A_TILE, lambda r: ((r,), ()), "in_proj_a", 512, outs=[(A_COLS, A_TILE, F32)],
                        pro=(pre1, [g_pre_mix, scale_mix, shift_mix], 0))
    qkv = _mm(h, w["w_b"], "nn", BF16, "in_proj_b")

    def lora_norm(cv, g):
        return cv * _rstd(cv) * g
    w = {**w, **wts("lora", qkv)}
    tables = [(ct, LANES), (sa, LANES), (sb, LANES)]

    def rope_q(qv, c_, a_, b_):
        return (jnp.concatenate([_rope(qv[:, hd * LANES:(hd + 1) * LANES], c_, a_, b_) for hd in range(HEADS)], axis=1),), ()
    q_m, cqn = _mm_epi(proj_a, w["w_uq"], "nn", D, rope_q, "mla_uq", 512, rows=tables, outs=[(D, D, BF16)],
                       pro=(lora_norm, [g_q], 0))

    def rope_k(kv, misc, c_, a_, b_):
        lane = lax.broadcasted_iota(jnp.int32, (1, LANES), 1)
        kpe = jnp.where((lane >= 64) & (lane < 96), _rope(misc, c_, a_, b_), 0.0)
        return (jnp.concatenate([kv[:, hd * LANES:(hd + 1) * LANES] + kpe for hd in range(HEADS)], axis=1),), ()
    k_m, ckvn = _mm_epi(proj_a, w["w_k"], "nn", D, rope_k, "mla_uk", 512, rows=[(proj_a, LANES, MISC_BLOCK)] + tables,
                        outs=[(D, D, BF16)], pro=(lora_norm, [g_kv], Q_LORA // KV_LORA))
    v_m = _mm(ckvn, w["w_v"], "nn", BF16, "mla_uv")

    bf = jnp.transpose(vec["b_forget"])
    zt, neg_f = _fox_gates(proj_a, MISC_BLOCK, bf)
    bias = neg_f.reshape(HEADS, N_ATT, 1, ATT_T)
    o_b, lse_b = _attn_fwd(q_m, 0, k_m, 0, v_m, 0, 2 * LANES, 1.0 / math.sqrt(64 + ROPE_DIM), None, "mla_attn")
    o_a, lse_a = _attn_fwd(qkv, 0, qkv, 4, qkv, 8, LANES, 1.0 / math.sqrt(HEAD_DIM), bias, "fox_attn",
                           dep=wts("relay_proj", o_b)["tok"])

    w = {**w, **wts("proj", o_a)}
    pa = _mm(o_a, w["w_pf"], "nn", BF16, "proj_fox", dep=wts("relay_ffn", o_a)["tok"])

    def merge(pb_, gf, gm, pa_):
        return (_sigmoid(gf) * pa_ + _sigmoid(gm) * pb_, pb_), ()
    merged, pb = _mm_epi(o_b, w["w_pm"], "nn", 512, merge, "proj_mla", 1024,
                         rows=[(proj_a, 512, 2), (proj_a, 512, 4), (pa, 512)], outs=[(D, 512, BF16), (D, 512, BF16)])
    def post1(yv, xv, gate, gpost, gpre, sc, sh):
        x1 = xv + gate * (yv * _rstd(yv) * gpost)
        return (x1, (x1 * _rstd(x1) * gpre) * (1.0 + sc) + sh, yv), ()
    x1, h2, y = _mm_epi(merged, w["w_out"], "nn", D, post1, "out_proj", 512, rows=[(x, D)],
                        vecs=[gate_mix, g_post_mix, g_pre_ffn, scale_ffn, shift_ffn],
                        outs=[(D, D, F32), (D, D, BF16), (D, D, F32)])
    w = {**w, **wts("ffn", h2)}

    def swiglu(r):
        g, u = r[:, :FFN_T], r[:, FFN_T:]
        return (g * _sigmoid(g) * u, r), ()
    act, gu = _mm_epi(h2, w["w_ffn_in"], "nn", 2 * FFN_T, swiglu, "ffn_in", 512,
                      outs=[(D_FF, FFN_T, BF16), (2 * D_FF, 2 * FFN_T, BF16)])

    def head(y2v, x1v, tv, gate, gpost):
        r = _rstd(y2v)
        yn = y2v * r
        n2 = yn * gpost
        err = (x1v + gate * n2) - tv
        dx2 = err * (1.0 / D)
        dn2 = dx2 * gate
        dy2 = _norm_bwd(dn2 * gpost, yn, r)
        return (dx2, dy2), (_colsum(err * err), _colsum(dx2 * n2), _colsum(dn2 * yn))
    dx2, dy2, err_cols, d_gate_ffn, d_g_post_ffn = _mm_epi(
        act, w["w_ffn_out"], "nn", D, head, "ffn_out", 512, rows=[(x1, D), (target, D)], vecs=[gate_ffn, g_post_ffn],
        outs=[(D, D, F32), (D, D, BF16)], sums=[D, D, D])

    def swiglu_bwd(da, guv):
        g, u = guv[:, :FFN_T].astype(F32), guv[:, FFN_T:].astype(F32)
        sg = _sigmoid(g)
        return (jnp.concatenate([da * u * (sg * (1.0 + g * (1.0 - sg))), da * (g * sg)], axis=1),), ()
    (dgu,) = _mm_epi(dy2, w["w_ffn_out"], "nt", FFN_T, swiglu_bwd, "ffn_out_dx", 512, rows=[(gu, 2 * FFN_T)],
                     outs=[(2 * D_FF, 2 * FFN_T, BF16)])
    dw = {"w_ffn_out": _mm(act, dy2, "tn", BF16, "ffn_out_dw")}
    dw["w_ffn_in"] = _mm(dgu, h2, "tn", BF16, "ffn_in_dw")
    tok = send({n: dw.pop(n) for n in ("w_ffn_in", "w_ffn_out")})

    def mid(dh, x1v, dx2v, yv, gpre, sc, gate, gpost):
        r2 = _rstd(x1v)
        x1n = x1v * r2
        t = dh * x1n
        dx1 = dx2v + _norm_bwd(dh * (gpre * (1.0 + sc)), x1n, r2)
        ry = _rstd(yv)
        yn = yv * ry
        dn1 = dx1 * gate
        dy = _norm_bwd(dn1 * gpost, yn, ry)
        sums = (_colsum(dh), _colsum(t) * gpre, _colsum(t) * (1.0 + sc), _colsum(dx1 * (yn * gpost)), _colsum(dn1 * yn))
        return (dx1, dy), sums
    dx1, dy, d_shift_ffn, d_scale_ffn, d_g_pre_ffn, d_gate_mix, d_g_post_mix = _mm_epi(
        dgu, w["w_ffn_in"], "nt", D, mid, "ffn_in_dx", 512, rows=[(x1, D), (dx2, D), (y, D)],
        vecs=[g_pre_ffn, scale_ffn, gate_mix, g_post_mix], outs=[(D, D, F32), (D, D, BF16)], sums=[D] * 5, dep=tok)

    dw["w_out"] = _mm(merged, dy, "tn", BF16, "out_proj_dw")

    def merge_bwd(dm, gf, gm, pa_, pb_):
        sf, sm = _sigmoid(gf), _sigmoid(gm)
        return (dm * sf, dm * sm, dm * pa_ * (sf * (1.0 - sf)), dm * pb_ * (sm * (1.0 - sm))), ()
    dpa, dpb, dgf, dgm = _mm_epi(dy, w["w_out"], "nt", 512, merge_bwd, "out_proj_dx", 1024,
                                 rows=[(proj_a, 512, 2), (proj_a, 512, 4), (pa, 512), (pb, 512)],
                                 outs=[(D, 512, BF16)] * 4)
    do_a = _mm(dpa, w["w_pf"], "nt", BF16, "proj_fox_dx")
    do_b = _mm(dpb, w["w_pm"], "nt", BF16, "proj_mla_dx")
    dw["w_pf"] = _mm(o_a, dpa, "tn", BF16, "proj_fox_dw")
    dw["w_pm"] = _mm(o_b, dpb, "tn", BF16, "proj_mla_dw")
    tok = send({n: dw.pop(n) for n in ("w_out", "w_pf", "w_pm")})

    sc_a, sc_b = 1.0 / math.sqrt(HEAD_DIM), 1.0 / math.sqrt(64 + ROPE_DIM)
    dq_a, dk_a, dv_a, dbias = _attn_grad(qkv, 0, qkv, 4, qkv, 8, do_a, lse_a, LANES, sc_a, bias, BF16, "fox_attn_bwd",
                                         dep=tok)
    dq_m, dk_m, dv_m = _attn_grad(q_m, 0, k_m, 0, v_m, 0, do_b, lse_b, 2 * LANES, sc_b, None, F32, "mla_attn_bwd")

    def mla_rope_bwd(dq, dk, c_, a_, b_):
        lane = lax.broadcasted_iota(jnp.int32, (1, LANES), 1)
        dqs = [_rope_t(dq[:, hd * LANES:(hd + 1) * LANES], c_, a_, b_) for hd in range(HEADS)]
        dkpe = dk[:, 0:LANES]
        for hd in range(1, HEADS):
            dkpe = dkpe + dk[:, hd * LANES:(hd + 1) * LANES]
        dkpe = jnp.where((lane >= 64) & (lane < 96), dkpe, 0.0)
        dkr = jnp.where((lane >= 64) & (lane < 96), _rope_t(dkpe, c_, a_, b_), 0.0)
        return (jnp.concatenate(dqs, axis=1), dk, dkr), ()
    dqb, dkb, dkr = _rowwise(mla_rope_bwd, [(dq_m, D, 0), (dk_m, D, 0), (ct, LANES, 0), (sa, LANES, 0), (sb, LANES, 0)],
                             [], [(D, BF16), (D, BF16), (LANES, F32)], [], "mla_rope_bwd")
    def lora_q_bwd(dq, cq, gq):
        rq = _rstd(cq)
        cqh = cq * rq
        return (_norm_bwd(dq * gq, cqh, rq),), (_colsum(dq * cqh),)
    dcq, d_g_q = _mm_epi(dqb, w["w_uq"], "nt", Q_LORA, lora_q_bwd, "mla_uq_dx", 512, rows=[(proj_a, Q_LORA, 0)],
                         vecs=[g_q], outs=[(Q_LORA, Q_LORA, BF16)], sums=[Q_LORA])

    def lora_kv_bwd(dv_part, dk_part, ckv, gkv):
        dkv = dv_part + dk_part
        rk = _rstd(ckv)
        ckh = ckv * rk
        return (_norm_bwd(dkv * gkv, ckh, rk),), (_colsum(dkv * ckh),)
    dckv, d_g_kv = _mm_epi(dv_m, w["w_v"], "nt", KV_LORA, lora_kv_bwd, "mla_uv_dx", 1024,
                           rows=[(_mm(dkb, w["w_k"], "nt", F32, "mla_uk_dx"), KV_LORA), (proj_a, KV_LORA, 3)],
                           vecs=[g_kv], outs=[(KV_LORA, KV_LORA, BF16)], sums=[KV_LORA])

    dzt, d_bf = _fox_gates_bwd(dbias.reshape(HEADS, S), zt, bf)
    dmisc = (dkr + jnp.pad(jnp.transpose(dzt), ((0, 0), (0, LANES - HEADS)))).astype(BF16)
    dproj_a = [dcq, dckv, dgf, dgm, dmisc]
    dqkv = [dq_a, dk_a, dv_a]
    dw["w_a"] = _mm_cat(h, dproj_a, "tn", BF16, "in_proj_a_dw")
    dw["w_b"] = _mm_cat(h, dqkv, "tn", BF16, "in_proj_b_dw")
    tok = send(dw, True)
    dh_a = _mm_cat(dproj_a, w["w_a"], "nt", F32, "in_proj_a_dx", dep=tok)
    tok = relay(dh_a)
    tok = send({"w_uq": _mm(cqn, dqb, "tn", BF16, "mla_uq_dw", dep=tok),
                "w_k": _mm(ckvn, dkb, "tn", BF16, "mla_uk_dw", dep=tok),
                "w_v": _mm(ckvn, dv_m, "tn", BF16, "mla_uv_dw", dep=tok)}, late=True)

    def first(dh_b, dh_a, xv, dx1v, gpre, sc):
        dhv = dh_b + dh_a
        r = _rstd(xv)
        xn = xv * r
        t = dhv * xn
        dx = dx1v + _norm_bwd(dhv * (gpre * (1.0 + sc)), xn, r)
        return (dx,), (_colsum(dhv), _colsum(t) * gpre, _colsum(t) * (1.0 + sc))
    grad_x, d_shift_mix, d_scale_mix, d_g_pre_mix = _mm_epi(
        dqkv, w["w_b"], "nt", D, first, "in_proj_b_dx", 512,
        rows=[(dh_a, D), (x, D), (dx1, D)],
        vecs=[g_pre_mix, scale_mix], outs=[(D, D, F32)], sums=[D] * 3, dep=tok)

    dmod = jnp.concatenate([d_shift_mix, d_scale_mix, d_gate_mix, d_shift_ffn, d_scale_ffn, d_gate_ffn], axis=1)
    small = dict(dmod=dmod, g_pre_mix=d_g_pre_mix, g_post_mix=d_g_post_mix, g_pre_ffn=d_g_pre_ffn,
                 g_post_ffn=d_g_post_ffn, g_q_lora=d_g_q, g_kv_lora=d_g_kv,
                 b_forget=jnp.pad(jnp.transpose(d_bf), ((0, 0), (0, LANES - HEADS))), err=err_cols)
    return grad_x, small


SMALL_ORDER = ("dmod", "g_pre_mix", "g_post_mix", "g_pre_ffn", "g_post_ffn", "g_q_lora", "g_kv_lora", "b_forget", "err")
SMALL_PARAM = {"dmod": "b_ada"}
MATRICES = ("w_in", "w_uq", "w_ukv", "w_proj_fox", "w_proj_mla", "w_out", "w_ffn_in", "w_ffn_out")
WEIGHTS = ("w_ada", "b_ada", "g_pre_mix", "g_post_mix", "g_pre_ffn", "g_post_ffn", "w_in", "b_forget", "g_q_lora",
           "w_uq", "g_kv_lora", "w_ukv", "w_proj_fox", "w_proj_mla", "w_out", "w_ffn_in", "w_ffn_out")


def kernel(x, c, positions, w_ada, b_ada, g_pre_mix, g_post_mix, g_pre_ffn, g_post_ffn, w_in, b_forget, g_q_lora, w_uq, g_kv_lora, w_ukv, w_proj_fox, w_proj_mla, w_out, w_ffn_in, w_ffn_out, loss_target, m_w_ada, m_b_ada, m_g_pre_mix, m_g_post_mix, m_g_pre_ffn, m_g_post_ffn, m_w_in, m_b_forget, m_g_q_lora, m_w_uq, m_g_kv_lora, m_w_ukv, m_w_proj_fox, m_w_proj_mla, m_w_out, m_w_ffn_in, m_w_ffn_out, v_w_ada, v_b_ada, v_g_pre_mix, v_g_post_mix, v_g_pre_ffn, v_g_post_ffn, v_w_in, v_b_forget, v_g_q_lora, v_w_uq, v_g_kv_lora, v_w_ukv, v_w_proj_fox, v_w_proj_mla, v_w_out, v_w_ffn_in, v_w_ffn_out):
    prm = dict(w_ada=w_ada, b_ada=b_ada, g_pre_mix=g_pre_mix, g_post_mix=g_post_mix, g_pre_ffn=g_pre_ffn,
               g_post_ffn=g_post_ffn, w_in=w_in, b_forget=b_forget, g_q_lora=g_q_lora, w_uq=w_uq, g_kv_lora=g_kv_lora,
               w_ukv=w_ukv, w_proj_fox=w_proj_fox, w_proj_mla=w_proj_mla, w_out=w_out, w_ffn_in=w_ffn_in, w_ffn_out=w_ffn_out)
    mom = dict(w_ada=m_w_ada, b_ada=m_b_ada, g_pre_mix=m_g_pre_mix, g_post_mix=m_g_post_mix, g_pre_ffn=m_g_pre_ffn,
               g_post_ffn=m_g_post_ffn, w_in=m_w_in, b_forget=m_b_forget, g_q_lora=m_g_q_lora, w_uq=m_w_uq,
               g_kv_lora=m_g_kv_lora, w_ukv=m_w_ukv, w_proj_fox=m_w_proj_fox, w_proj_mla=m_w_proj_mla, w_out=m_w_out,
               w_ffn_in=m_w_ffn_in, w_ffn_out=m_w_ffn_out)
    var = dict(w_ada=v_w_ada, b_ada=v_b_ada, g_pre_mix=v_g_pre_mix, g_post_mix=v_g_post_mix, g_pre_ffn=v_g_pre_ffn,
               g_post_ffn=v_g_post_ffn, w_in=v_w_in, b_forget=v_b_forget, g_q_lora=v_g_q_lora, w_uq=v_w_uq,
               g_kv_lora=v_g_kv_lora, w_ukv=v_w_ukv, w_proj_fox=v_w_proj_fox, w_proj_mla=v_w_proj_mla, w_out=v_w_out,
               w_ffn_in=v_w_ffn_in, w_ffn_out=v_w_ffn_out)
    me = _flat(*_coords())
    slot = jnp.reshape(me, (1,)).astype(jnp.int32)

    own = {n: prm[n][0].astype(BF16) for n in MATRICES}
    no_dep = jnp.zeros((8, LANES), F32)
    (st_c, st_in), tok = _async_start([[c], [own["w_in"]]], ["gather", "spread"], no_dep, "gather_in_start")
    (c_own,), (c_land,) = _async_wait(st_c, tok, "gather_c_wait")
    c_all = _with_own(c_land, c_own, me).reshape(N_DEV, D)
    ada_cols = w_ada.shape[2]
    b_cols = lax.dynamic_slice(b_ada, (0, me * ada_cols), (1, ada_cols))
    pos = positions.astype(F32).reshape(S, 1)
    tables = _rope_table_set(pos)
    mod_cols, silu_c = _mod_part(c_all, w_ada[0], b_cols,
                                 before=[own[n] for n in MATRICES if n != "w_in"] + list(tables))
    (mod_all,) = _all_gather([mod_cols], "gather_mod")

    (w_in_own,), (w_in_land,) = _async_wait(st_in, mod_all, "gather_in_wait")
    (st_in,), tok = _async_start([[w_in_land]], "forward", no_dep, "gather_in_forward")
    _, (w_in_land,) = _async_wait(st_in, tok, "gather_in_forward_wait")
    w = _prepare_weights({"w_in": w_in_land}, {"w_in": w_in_own}, slot)
    later = dict(lora=("w_uq", "w_ukv"), proj=("w_proj_fox", "w_proj_mla", "w_out"), ffn=("w_ffn_in", "w_ffn_out"))
    states, tok = _async_start([[own[n] for n in names] for names in later.values()], ["gather", "spread", "spread"],
                               w["w_b"], "gather_rest_start")
    gather_state = dict(zip(later, states))
    own_thru = {}

    def wts(group, after):
        if group.startswith("relay_"):
            name = group[len("relay_"):]
            own_thru[name], lands = _async_wait(gather_state[name], after, "gather_" + name + "_wait")
            (gather_state[name],), t = _async_start([lands], "forward", no_dep, "gather_" + name + "_forward")
            return {"tok": t}
        srcs, lands = _async_wait(gather_state[group], after, "gather_" + group + "_landed")
        srcs = own_thru.get(group, srcs)
        return _prepare_weights(dict(zip(later[group], lands)), dict(zip(later[group], srcs)), slot)

    sent, late_sent, last = [], [], {}

    def send(grads, final=False, late=False):
        shards = _shard_grads(grads)
        names = list(shards)
        (state,), t = _async_start([[shards[n] for n in names]], "pair" if final else "exchange", no_dep,
                                   "exchange_" + names[0] + "_start")
        if final:
            last.update(names=names, state=state)
        else:
            (late_sent if late else sent).append((names, state))
        return t

    def relay(after):
        srcs, lands = _async_wait(last["state"], after, "exchange_pair_wait")
        core = jnp.reshape(lax.axis_index("c"), (1,)).astype(jnp.int32)
        sums = [_add_sibling(src, land, core) for src, land in zip(srcs, lands)]
        (last["state"],), t = _async_start([sums], "chips", no_dep, "exchange_chips_start")
        return t

    mod = lax.dynamic_index_in_dim(mod_all, me, axis=1, keepdims=False).reshape(1, 6 * D) + tok[0, 0]

    vec = dict(g_pre_mix=g_pre_mix, g_post_mix=g_post_mix, g_pre_ffn=g_pre_ffn, g_post_ffn=g_post_ffn,
               g_q_lora=g_q_lora, g_kv_lora=g_kv_lora, b_forget=b_forget)
    grad_x, small = _fwd_bwd(x[0], pos, mod, loss_target[0], w, vec, wts, send, relay, tables)

    bundle = jnp.concatenate([small[n] for n in SMALL_ORDER], axis=1)
    (small_state,), tok = _async_start([[bundle]], "gather", jnp.zeros((8, LANES), F32), "gather_small_start")

    out = {}
    swap = lambda a: jnp.swapaxes(a, -1, -2)

    def update(n, land, src, sl):
        if n != "w_ffn_in":
            out[n] = _adamw(prm[n][0], mom[n][0], var[n][0], land, "adamw_" + n, src, sl)
            return out[n][0]
        res = _adamw(swap(prm[n][0]), swap(mom[n][0]), swap(var[n][0]), land, "adamw_" + n, src, sl)
        out[n] = tuple(swap(t) for t in res)
        return res[0]

    after = tok
    for names, state in sent:
        srcs, lands = _async_wait(state, after, "exchange_" + names[0] + "_wait")
        for n, src, land in zip(names, srcs, lands):
            after = update(n, land, src, slot)
    srcs, lands = _async_wait(last["state"], after, "exchange_chips_wait")
    for n, src, land in zip(last["names"], srcs, lands):
        after = update(n, land, src, slot // 2)
    for names, state in late_sent:
        srcs, lands = _async_wait(state, after, "exchange_" + names[0] + "_wait")
        for n, src, land in zip(names, srcs, lands):
            after = update(n, land, src, slot)

    (own_bundle,), (bundle_all,) = _async_wait(small_state, after, "gather_small_wait")
    bundle_all = _with_own(bundle_all, own_bundle, me)
    dmod_all = bundle_all[:, 0, :6 * D]
    dm_cols = lax.dynamic_slice(dmod_all, (0, me * ada_cols), (N_DEV, ada_cols))
    out["w_ada"] = _adamw_w_ada(w_ada[0], m_w_ada[0], v_w_ada[0], jnp.transpose(silu_c), dm_cols)

    offsets, off = {}, 0
    for n in SMALL_ORDER:
        offsets[n] = off
        off += small[n].shape[1]
    names = [SMALL_PARAM.get(n, n) for n in SMALL_ORDER if n != "err"]
    results, err = _adamw_rows(bundle_all, [offsets[n] for n in SMALL_ORDER if n != "err"],
                               [prm[n] for n in names], [mom[n] for n in names], [var[n] for n in names],
                               offsets["err"], D)
    out.update(zip(names, results))
    loss = 0.5 * jnp.sum(err) / D

    res = [loss, grad_x[None]]
    for kind in range(4):
        for n in WEIGHTS:
            t = out[n][kind]
            res.append(t[None] if prm[n].ndim == 3 else t)
    return tuple(res)
```
